```python
import math
import jax, jax.numpy as jnp
from jax import lax
import numpy as np

D_MODEL = 1024
BATCH = 8
SEQ = 8192
DEPTH = 4

D_MIX = D_MODEL
DN_HEADS = 4
DN_HEAD_DIM = 128
DN_WIDTH = DN_HEADS * DN_HEAD_DIM
DN_CONV = 4
CHUNK = 64
SC_WIDTH = D_MIX - DN_WIDTH
SC_GROUPS = 4
SC_GROUP_DIM = SC_WIDTH // SC_GROUPS
SC_CONV = 3
D_FF = ((8 * D_MODEL + 3 * 256 - 1) // (3 * 256)) * 256
W_IN_COLS = 4 * DN_WIDTH + 2 * DN_HEADS + 3 * SC_WIDTH
EPS = 1e-6

kernel_name = 'hybrid_gdn_shortconv_swiglu'


def rms_norm(x, gain):
    xf = x.astype(jnp.float32)
    y = xf * lax.rsqrt(jnp.mean(xf * xf, axis=-1, keepdims=True) + EPS)
    return (y * gain.astype(jnp.float32)).astype(x.dtype)


def l2_normalize(x):
    return x * lax.rsqrt(jnp.sum(x * x, axis=-1, keepdims=True) + EPS)


def causal_depthwise_conv(x, w):
    K = w.shape[0]
    T = x.shape[1]
    xp = jnp.pad(x, ((0, 0), (K - 1, 0), (0, 0)))
    y = xp[:, 0:T, :] * w[0]
    for j in range(1, K):
        y = y + xp[:, j:j + T, :] * w[j]
    return y


def chunk_gated_delta_rule(q, k, v, g, beta):
    Bsz, T, H, DK = q.shape
    DV = v.shape[-1]
    C = CHUNK
    N = T // C
    q = q * (DK ** -0.5)

    def to_chunks(t):
        return t.reshape(Bsz, N, C, H, t.shape[-1]).transpose(0, 3, 1, 2, 4)

    q, k, v = to_chunks(q), to_chunks(k), to_chunks(v)
    g = jnp.cumsum(g.reshape(Bsz, N, C, H).transpose(0, 3, 1, 2), axis=-1)
    beta = beta.reshape(Bsz, N, C, H).transpose(0, 3, 1, 2)

    causal = jnp.tril(jnp.ones((C, C), dtype=bool))
    strict = jnp.tril(jnp.ones((C, C), dtype=bool), -1)
    decay = jnp.exp(jnp.where(causal, g[..., :, None] - g[..., None, :], -jnp.inf))

    k_beta = k * beta[..., None]
    v_beta = v * beta[..., None]
    lower = jnp.where(strict, jnp.einsum('bhncd,bhnmd->bhncm', k_beta, k) * decay, 0.0)
    a_mat = lower + jnp.eye(C, dtype=jnp.float32)
    rhs = jnp.concatenate([v_beta, k_beta * jnp.exp(g)[..., None]], axis=-1)
    sol = lax.linalg.triangular_solve(a_mat, rhs, left_side=True, lower=True, unit_diagonal=True)
    u, w = sol[..., :DV], sol[..., DV:]

    qk = jnp.einsum('bhncd,bhnmd->bhncm', q, k) * decay
    q_dec = q * jnp.exp(g)[..., None]
    k_dec = k * jnp.exp(g[..., -1:] - g)[..., None]
    g_last = jnp.exp(g[..., -1])

    def step(S, xs):
        qk_i, q_dec_i, k_dec_i, u_i, w_i, gl_i = xs
        v_new = u_i - jnp.einsum('bhck,bhkv->bhcv', w_i, S)
        o_i = jnp.einsum('bhck,bhkv->bhcv', q_dec_i, S) + jnp.einsum('bhcm,bhmv->bhcv', qk_i, v_new)
        S = S * gl_i[..., None, None] + jnp.einsum('bhck,bhcv->bhkv', k_dec_i, v_new)
        return S, o_i

    xs = (jnp.moveaxis(qk, 2, 0), jnp.moveaxis(q_dec, 2, 0), jnp.moveaxis(k_dec, 2, 0),
          jnp.moveaxis(u, 2, 0), jnp.moveaxis(w, 2, 0), jnp.moveaxis(g_last, 2, 0))
    S0 = jnp.zeros((Bsz, H, DK, DV), dtype=jnp.float32)
    _, o = lax.scan(step, S0, xs)
    return o.transpose(1, 0, 3, 2, 4).reshape(Bsz, T, H, DV)


def hybrid_layer(x, norm1_g, w_in, dn_conv_w, dn_a_log, dn_dt_bias, dn_norm_g,
                 sc_conv_w, sc_norm_g, w_out, norm2_g, ffn_w_gate, ffn_w_up, ffn_w_down):
    Bsz, T, _ = x.shape
    h = rms_norm(x, norm1_g)
    proj = h @ w_in
    o1 = 3 * DN_WIDTH
    o2 = o1 + DN_WIDTH
    o3 = o2 + DN_HEADS
    o4 = o3 + DN_HEADS
    qkv, z, b_in, a_in, sc_in = proj[..., :o1], proj[..., o1:o2], proj[..., o2:o3], proj[..., o3:o4], proj[..., o4:]

    qkv = jax.nn.silu(causal_depthwise_conv(qkv, dn_conv_w)).astype(jnp.float32)
    q = l2_normalize(qkv[..., :DN_WIDTH].reshape(Bsz, T, DN_HEADS, DN_HEAD_DIM))
    k = l2_normalize(qkv[..., DN_WIDTH:2 * DN_WIDTH].reshape(Bsz, T, DN_HEADS, DN_HEAD_DIM))
    v = qkv[..., 2 * DN_WIDTH:].reshape(Bsz, T, DN_HEADS, DN_HEAD_DIM)
    beta = jax.nn.sigmoid(b_in.astype(jnp.float32))
    g = -jnp.exp(dn_a_log.astype(jnp.float32)) * jax.nn.softplus(
        a_in.astype(jnp.float32) + dn_dt_bias.astype(jnp.float32))
    o_dn = chunk_gated_delta_rule(q, k, v, g, beta)
    zf = z.astype(jnp.float32).reshape(Bsz, T, DN_HEADS, DN_HEAD_DIM)
    o_dn = (o_dn * lax.rsqrt(jnp.mean(o_dn * o_dn, axis=-1, keepdims=True) + EPS)
            * dn_norm_g.astype(jnp.float32) * jax.nn.silu(zf))
    o_dn = o_dn.reshape(Bsz, T, DN_WIDTH).astype(x.dtype)

    gate_b, gate_c, hv = sc_in[..., :SC_WIDTH], sc_in[..., SC_WIDTH:2 * SC_WIDTH], sc_in[..., 2 * SC_WIDTH:]
    y = gate_b * causal_depthwise_conv(gate_c * hv, sc_conv_w)
    yf = y.astype(jnp.float32).reshape(Bsz, T, SC_GROUPS, SC_GROUP_DIM)
    yf = yf * lax.rsqrt(jnp.mean(yf * yf, axis=-1, keepdims=True) + EPS)
    o_sc = (yf * sc_norm_g.astype(jnp.float32).reshape(SC_GROUPS, SC_GROUP_DIM)).reshape(Bsz, T, SC_WIDTH).astype(x.dtype)

    x = x + jnp.concatenate([o_dn, o_sc], axis=-1) @ w_out

    h2 = rms_norm(x, norm2_g)
    x = x + (jax.nn.silu(h2 @ ffn_w_gate) * (h2 @ ffn_w_up)) @ ffn_w_down
    return x


def _fwd_setup_inputs(seed: int = 0) -> dict:
    key = jax.random.key(seed)
    ks = jax.random.split(key, 16)
    f32 = jnp.float32

    def nrm(k, shape, scale):
        return jax.random.normal(k, shape, f32) * scale

    def gain(k, shape):
        return 1.0 + 0.02 * jax.random.normal(k, shape, f32)

    x = nrm(ks[0], (BATCH, SEQ, D_MODEL), 1.0)
    norm1_g = gain(ks[1], (DEPTH, D_MODEL))
    w_in = nrm(ks[2], (DEPTH, D_MODEL, W_IN_COLS), D_MODEL ** -0.5)
    dn_conv_w = nrm(ks[3], (DEPTH, DN_CONV, 3 * DN_WIDTH), DN_CONV ** -0.5)
    dn_a_log = jnp.log(jax.random.uniform(ks[4], (DEPTH, DN_HEADS), f32, 1.0, 16.0))
    dt = jnp.exp(jax.random.uniform(ks[5], (DEPTH, DN_HEADS), f32, math.log(1e-3), math.log(1e-1)))
    dn_dt_bias = dt + jnp.log(-jnp.expm1(-dt))
    dn_norm_g = gain(ks[6], (DEPTH, DN_HEAD_DIM))
    sc_conv_w = nrm(ks[7], (DEPTH, SC_CONV, SC_WIDTH), SC_CONV ** -0.5)
    sc_norm_g = gain(ks[8], (DEPTH, SC_WIDTH))
    w_out = nrm(ks[9], (DEPTH, D_MIX, D_MODEL), D_MIX ** -0.5)
    norm2_g = gain(ks[10], (DEPTH, D_MODEL))
    ffn_w_gate = nrm(ks[11], (DEPTH, D_MODEL, D_FF), D_MODEL ** -0.5)
    ffn_w_up = nrm(ks[12], (DEPTH, D_MODEL, D_FF), D_MODEL ** -0.5)
    ffn_w_down = nrm(ks[13], (DEPTH, D_FF, D_MODEL), D_FF ** -0.5)
    final_norm_g = gain(ks[14], (D_MODEL,))
    return {'x': x, 'norm1_g': norm1_g, 'w_in': w_in, 'dn_conv_w': dn_conv_w,
            'dn_a_log': dn_a_log, 'dn_dt_bias': dn_dt_bias, 'dn_norm_g': dn_norm_g,
            'sc_conv_w': sc_conv_w, 'sc_norm_g': sc_norm_g, 'w_out': w_out,
            'norm2_g': norm2_g, 'ffn_w_gate': ffn_w_gate, 'ffn_w_up': ffn_w_up,
            'ffn_w_down': ffn_w_down, 'final_norm_g': final_norm_g}


def _fwd_reference(x, norm1_g, w_in, dn_conv_w, dn_a_log, dn_dt_bias, dn_norm_g, sc_conv_w,
              sc_norm_g, w_out, norm2_g, ffn_w_gate, ffn_w_up, ffn_w_down, final_norm_g):
    for l in range(DEPTH):
        x = hybrid_layer(x, norm1_g[l], w_in[l], dn_conv_w[l], dn_a_log[l], dn_dt_bias[l],
                         dn_norm_g[l], sc_conv_w[l], sc_norm_g[l], w_out[l], norm2_g[l],
                         ffn_w_gate[l], ffn_w_up[l], ffn_w_down[l])
    return rms_norm(x, final_norm_g)


import jax as _jax
import jax.numpy as _jnp

TWIN_FORMAT = 'train_step'
FWD_PARAMS = ['x', 'norm1_g', 'w_in', 'dn_conv_w', 'dn_a_log', 'dn_dt_bias', 'dn_norm_g', 'sc_conv_w', 'sc_norm_g', 'w_out', 'norm2_g', 'ffn_w_gate', 'ffn_w_up', 'ffn_w_down', 'final_norm_g']
TWIN_WEIGHTS = ['norm1_g', 'w_in', 'dn_conv_w', 'dn_a_log', 'dn_dt_bias', 'dn_norm_g', 'sc_conv_w', 'sc_norm_g', 'w_out', 'norm2_g', 'ffn_w_gate', 'ffn_w_up', 'ffn_w_down', 'final_norm_g']
TWIN_DIFF_INPUT = 'x'
TWIN_INPUTS = ['x', 'norm1_g', 'w_in', 'dn_conv_w', 'dn_a_log', 'dn_dt_bias', 'dn_norm_g', 'sc_conv_w', 'sc_norm_g', 'w_out', 'norm2_g', 'ffn_w_gate', 'ffn_w_up', 'ffn_w_down', 'final_norm_g', 'loss_target', 'm_norm1_g', 'm_w_in', 'm_dn_conv_w', 'm_dn_a_log', 'm_dn_dt_bias', 'm_dn_norm_g', 'm_sc_conv_w', 'm_sc_norm_g', 'm_w_out', 'm_norm2_g', 'm_ffn_w_gate', 'm_ffn_w_up', 'm_ffn_w_down', 'm_final_norm_g', 'v_norm1_g', 'v_w_in', 'v_dn_conv_w', 'v_dn_a_log', 'v_dn_dt_bias', 'v_dn_norm_g', 'v_sc_conv_w', 'v_sc_norm_g', 'v_w_out', 'v_norm2_g', 'v_ffn_w_gate', 'v_ffn_w_up', 'v_ffn_w_down', 'v_final_norm_g']
TWIN_OUTPUTS = ['loss', 'grad_x', 'grad_norm1_g', 'grad_w_in', 'grad_dn_conv_w', 'grad_dn_a_log', 'grad_dn_dt_bias', 'grad_dn_norm_g', 'grad_sc_conv_w', 'grad_sc_norm_g', 'grad_w_out', 'grad_norm2_g', 'grad_ffn_w_gate', 'grad_ffn_w_up', 'grad_ffn_w_down', 'grad_final_norm_g', 'delta_norm1_g', 'delta_w_in', 'delta_dn_conv_w', 'delta_dn_a_log', 'delta_dn_dt_bias', 'delta_dn_norm_g', 'delta_sc_conv_w', 'delta_sc_norm_g', 'delta_w_out', 'delta_norm2_g', 'delta_ffn_w_gate', 'delta_ffn_w_up', 'delta_ffn_w_down', 'delta_final_norm_g', 'new_m_norm1_g', 'new_m_w_in', 'new_m_dn_conv_w', 'new_m_dn_a_log', 'new_m_dn_dt_bias', 'new_m_dn_norm_g', 'new_m_sc_conv_w', 'new_m_sc_norm_g', 'new_m_w_out', 'new_m_norm2_g', 'new_m_ffn_w_gate', 'new_m_ffn_w_up', 'new_m_ffn_w_down', 'new_m_final_norm_g', 'new_v_norm1_g', 'new_v_w_in', 'new_v_dn_conv_w', 'new_v_dn_a_log', 'new_v_dn_dt_bias', 'new_v_dn_norm_g', 'new_v_sc_conv_w', 'new_v_sc_norm_g', 'new_v_w_out', 'new_v_norm2_g', 'new_v_ffn_w_gate', 'new_v_ffn_w_up', 'new_v_ffn_w_down', 'new_v_final_norm_g']
TWIN_LEAF_KINDS = {'loss': 'loss', 'grad_x': 'grad_x', 'grad_norm1_g': 'grad_w', 'grad_w_in': 'grad_w', 'grad_dn_conv_w': 'grad_w', 'grad_dn_a_log': 'grad_w', 'grad_dn_dt_bias': 'grad_w', 'grad_dn_norm_g': 'grad_w', 'grad_sc_conv_w': 'grad_w', 'grad_sc_norm_g': 'grad_w', 'grad_w_out': 'grad_w', 'grad_norm2_g': 'grad_w', 'grad_ffn_w_gate': 'grad_w', 'grad_ffn_w_up': 'grad_w', 'grad_ffn_w_down': 'grad_w', 'grad_final_norm_g': 'grad_w', 'delta_norm1_g': 'delta_w', 'delta_w_in': 'delta_w', 'delta_dn_conv_w': 'delta_w', 'delta_dn_a_log': 'delta_w', 'delta_dn_dt_bias': 'delta_w', 'delta_dn_norm_g': 'delta_w', 'delta_sc_conv_w': 'delta_w', 'delta_sc_norm_g': 'delta_w', 'delta_w_out': 'delta_w', 'delta_norm2_g': 'delta_w', 'delta_ffn_w_gate': 'delta_w', 'delta_ffn_w_up': 'delta_w', 'delta_ffn_w_down': 'delta_w', 'delta_final_norm_g': 'delta_w', 'new_m_norm1_g': 'new_m', 'new_m_w_in': 'new_m', 'new_m_dn_conv_w': 'new_m', 'new_m_dn_a_log': 'new_m', 'new_m_dn_dt_bias': 'new_m', 'new_m_dn_norm_g': 'new_m', 'new_m_sc_conv_w': 'new_m', 'new_m_sc_norm_g': 'new_m', 'new_m_w_out': 'new_m', 'new_m_norm2_g': 'new_m', 'new_m_ffn_w_gate': 'new_m', 'new_m_ffn_w_up': 'new_m', 'new_m_ffn_w_down': 'new_m', 'new_m_final_norm_g': 'new_m', 'new_v_norm1_g': 'new_v', 'new_v_w_in': 'new_v', 'new_v_dn_conv_w': 'new_v', 'new_v_dn_a_log': 'new_v', 'new_v_dn_dt_bias': 'new_v', 'new_v_dn_norm_g': 'new_v', 'new_v_sc_conv_w': 'new_v', 'new_v_sc_norm_g': 'new_v', 'new_v_w_out': 'new_v', 'new_v_norm2_g': 'new_v', 'new_v_ffn_w_gate': 'new_v', 'new_v_ffn_w_up': 'new_v', 'new_v_ffn_w_down': 'new_v', 'new_v_final_norm_g': 'new_v'}


def _forward(args):
    return _fwd_reference(*[args[k] for k in FWD_PARAMS])


def _output_shape():
    out = _jax.eval_shape(lambda: _forward(_fwd_setup_inputs(0)))
    return out.shape, out.dtype

N_MICROBATCH = 1
ADAM_LR = 0.001
ADAM_B1 = 0.9
ADAM_B2 = 0.999
ADAM_EPS = 1e-08
ADAM_WD = 0.01
ADAM_STEP = 10
PER_EXAMPLE_BATCH_AXIS = {'x': 0, 'loss_target': 0}
SHARED_INPUTS = []
_WEIGHT_DTYPES = {'norm1_g': _jnp.float32, 'w_in': _jnp.float32, 'dn_conv_w': _jnp.float32, 'dn_a_log': _jnp.float32, 'dn_dt_bias': _jnp.float32, 'dn_norm_g': _jnp.float32, 'sc_conv_w': _jnp.float32, 'sc_norm_g': _jnp.float32, 'w_out': _jnp.float32, 'norm2_g': _jnp.float32, 'ffn_w_gate': _jnp.float32, 'ffn_w_up': _jnp.float32, 'ffn_w_down': _jnp.float32, 'final_norm_g': _jnp.float32}
MOMENT_SCALE = {'norm1_g': 3.646719e-01, 'w_in': 1.959251e-01, 'dn_conv_w': 1.113151e-01, 'dn_a_log': 7.628188e-01, 'dn_dt_bias': 7.326219e-01, 'dn_norm_g': 2.850396e-01, 'sc_conv_w': 2.655728e-01, 'sc_norm_g': 2.616827e-01, 'w_out': 2.078938e-01, 'norm2_g': 1.860829e-01, 'ffn_w_gate': 8.121041e-02, 'ffn_w_up': 7.866183e-02, 'ffn_w_down': 1.302540e-01, 'final_norm_g': 6.401418e+01}


def _to_microbatches(a, axis):
    t = _jnp.moveaxis(a, axis, 0)
    t = t.reshape((N_MICROBATCH, t.shape[0] // N_MICROBATCH) + t.shape[1:])
    return _jnp.moveaxis(t, 1, axis + 1)


def setup_inputs(seed: int = 0) -> dict:
    inp = _fwd_setup_inputs(seed)
    key = _jax.random.fold_in(_jax.random.key(seed), 7919)
    shape, _ = _output_shape()
    out = dict(inp)
    out["loss_target"] = _jax.random.normal(_jax.random.fold_in(key, 0), shape, _jnp.float32)
    for i, name in enumerate(TWIN_WEIGHTS):
        w = inp[name].astype(_jnp.float32)
        if MOMENT_SCALE is None:
            s = _jnp.sqrt(_jnp.mean(_jnp.square(w)) + 1e-30)
        else:
            s = MOMENT_SCALE[name]
        km, kv = _jax.random.split(_jax.random.fold_in(key, i + 1))
        out[name] = w
        out["m_" + name] = s * _jax.random.normal(km, w.shape, _jnp.float32)
        out["v_" + name] = (s * s) * _jax.random.uniform(kv, w.shape, _jnp.float32, 0.5, 1.5)
    if N_MICROBATCH > 1:
        for name, axis in PER_EXAMPLE_BATCH_AXIS.items():
            out[name] = _to_microbatches(out[name], axis)
    return {'x': out['x'], 'norm1_g': out['norm1_g'], 'w_in': out['w_in'], 'dn_conv_w': out['dn_conv_w'], 'dn_a_log': out['dn_a_log'], 'dn_dt_bias': out['dn_dt_bias'], 'dn_norm_g': out['dn_norm_g'], 'sc_conv_w': out['sc_conv_w'], 'sc_norm_g': out['sc_norm_g'], 'w_out': out['w_out'], 'norm2_g': out['norm2_g'], 'ffn_w_gate': out['ffn_w_gate'], 'ffn_w_up': out['ffn_w_up'], 'ffn_w_down': out['ffn_w_down'], 'final_norm_g': out['final_norm_g'], 'loss_target': out['loss_target'], 'm_norm1_g': out['m_norm1_g'], 'm_w_in': out['m_w_in'], 'm_dn_conv_w': out['m_dn_conv_w'], 'm_dn_a_log': out['m_dn_a_log'], 'm_dn_dt_bias': out['m_dn_dt_bias'], 'm_dn_norm_g': out['m_dn_norm_g'], 'm_sc_conv_w': out['m_sc_conv_w'], 'm_sc_norm_g': out['m_sc_norm_g'], 'm_w_out': out['m_w_out'], 'm_norm2_g': out['m_norm2_g'], 'm_ffn_w_gate': out['m_ffn_w_gate'], 'm_ffn_w_up': out['m_ffn_w_up'], 'm_ffn_w_down': out['m_ffn_w_down'], 'm_final_norm_g': out['m_final_norm_g'], 'v_norm1_g': out['v_norm1_g'], 'v_w_in': out['v_w_in'], 'v_dn_conv_w': out['v_dn_conv_w'], 'v_dn_a_log': out['v_dn_a_log'], 'v_dn_dt_bias': out['v_dn_dt_bias'], 'v_dn_norm_g': out['v_dn_norm_g'], 'v_sc_conv_w': out['v_sc_conv_w'], 'v_sc_norm_g': out['v_sc_norm_g'], 'v_w_out': out['v_w_out'], 'v_norm2_g': out['v_norm2_g'], 'v_ffn_w_gate': out['v_ffn_w_gate'], 'v_ffn_w_up': out['v_ffn_w_up'], 'v_ffn_w_down': out['v_ffn_w_down'], 'v_final_norm_g': out['v_final_norm_g']}


def _loss(weights, diff, rest, loss_target):
    with _jax.named_scope("forward"):
        args = {**rest, TWIN_DIFF_INPUT: diff, **{k: w.astype(_WEIGHT_DTYPES[k]) for k, w in weights.items()}}
        y = _forward(args)
    with _jax.named_scope("loss_head"):
        err = _jnp.square(y.astype(_jnp.float32) - loss_target)
        return 0.5 * _jnp.sum(_jnp.mean(err, axis=-1)) if err.ndim else 0.5 * err


def _adamw(w, g, m, v):
    m = ADAM_B1 * m + (1.0 - ADAM_B1) * g
    v = ADAM_B2 * v + (1.0 - ADAM_B2) * _jnp.square(g)
    m_hat = m / (1.0 - ADAM_B1 ** ADAM_STEP)
    v_hat = v / (1.0 - ADAM_B2 ** ADAM_STEP)
    delta = -ADAM_LR * (m_hat / (_jnp.sqrt(v_hat) + ADAM_EPS) + ADAM_WD * w)
    return delta, m, v


def reference(x, norm1_g, w_in, dn_conv_w, dn_a_log, dn_dt_bias, dn_norm_g, sc_conv_w, sc_norm_g, w_out, norm2_g, ffn_w_gate, ffn_w_up, ffn_w_down, final_norm_g, loss_target, m_norm1_g, m_w_in, m_dn_conv_w, m_dn_a_log, m_dn_dt_bias, m_dn_norm_g, m_sc_conv_w, m_sc_norm_g, m_w_out, m_norm2_g, m_ffn_w_gate, m_ffn_w_up, m_ffn_w_down, m_final_norm_g, v_norm1_g, v_w_in, v_dn_conv_w, v_dn_a_log, v_dn_dt_bias, v_dn_norm_g, v_sc_conv_w, v_sc_norm_g, v_w_out, v_norm2_g, v_ffn_w_gate, v_ffn_w_up, v_ffn_w_down, v_final_norm_g):
    given = dict(x=x, norm1_g=norm1_g, w_in=w_in, dn_conv_w=dn_conv_w, dn_a_log=dn_a_log, dn_dt_bias=dn_dt_bias, dn_norm_g=dn_norm_g, sc_conv_w=sc_conv_w, sc_norm_g=sc_norm_g, w_out=w_out, norm2_g=norm2_g, ffn_w_gate=ffn_w_gate, ffn_w_up=ffn_w_up, ffn_w_down=ffn_w_down, final_norm_g=final_norm_g, loss_target=loss_target, m_norm1_g=m_norm1_g, m_w_in=m_w_in, m_dn_conv_w=m_dn_conv_w, m_dn_a_log=m_dn_a_log, m_dn_dt_bias=m_dn_dt_bias, m_dn_norm_g=m_dn_norm_g, m_sc_conv_w=m_sc_conv_w, m_sc_norm_g=m_sc_norm_g, m_w_out=m_w_out, m_norm2_g=m_norm2_g, m_ffn_w_gate=m_ffn_w_gate, m_ffn_w_up=m_ffn_w_up, m_ffn_w_down=m_ffn_w_down, m_final_norm_g=m_final_norm_g, v_norm1_g=v_norm1_g, v_w_in=v_w_in, v_dn_conv_w=v_dn_conv_w, v_dn_a_log=v_dn_a_log, v_dn_dt_bias=v_dn_dt_bias, v_dn_norm_g=v_dn_norm_g, v_sc_conv_w=v_sc_conv_w, v_sc_norm_g=v_sc_norm_g, v_w_out=v_w_out, v_norm2_g=v_norm2_g, v_ffn_w_gate=v_ffn_w_gate, v_ffn_w_up=v_ffn_w_up, v_ffn_w_down=v_ffn_w_down, v_final_norm_g=v_final_norm_g)
    weights = {n: given[n] for n in TWIN_WEIGHTS}
    shared = {n: given[n] for n in SHARED_INPUTS}
    per_example = {n: given[n] for n in ['x']}
    grad_fn = _jax.value_and_grad(_loss, argnums=(0, 1))

    def one_microbatch(ex, loss_target):
        ex = dict(ex)
        diff = ex.pop(TWIN_DIFF_INPUT)
        return grad_fn(weights, diff, {**shared, **ex}, loss_target)

    if N_MICROBATCH == 1:
        loss, (grad_w, grad_x) = one_microbatch(per_example, given["loss_target"])
    else:
        def body(carry, xs):
            loss_sum, grad_sum = carry
            l_k, (gw_k, gx_k) = one_microbatch(xs[0], xs[1])
            with _jax.named_scope("update"):
                return (loss_sum + l_k, _jax.tree.map(_jnp.add, grad_sum, gw_k)), gx_k

        init = (_jnp.zeros((), _jnp.float32), _jax.tree.map(_jnp.zeros_like, weights))
        (loss, grad_w), grad_x = _jax.lax.scan(body, init, (per_example, given["loss_target"]))
    with _jax.named_scope("update"):
        delta_w, new_m, new_v = {}, {}, {}
        for n in TWIN_WEIGHTS:
            delta_w[n], new_m[n], new_v[n] = _adamw(weights[n], grad_w[n], given["m_" + n], given["v_" + n])
    return (loss, grad_x, *[grad_w[n] for n in TWIN_WEIGHTS], *[delta_w[n] for n in TWIN_WEIGHTS],
            *[new_m[n] for n in TWIN_WEIGHTS], *[new_v[n] for n in TWIN_WEIGHTS])
```

```python
import jax
import jax.numpy as jnp
from jax import lax
from jax.experimental import pallas as pl
from jax.experimental.pallas import tpu as pltpu

F32 = jnp.float32
BF16 = jnp.bfloat16
MESH = pl.DeviceIdType.MESH

D_MODEL = 1024
DEPTH = 4
HEADS = 4
HEAD_DIM = 128
DN_WIDTH = HEADS * HEAD_DIM
SC_WIDTH = 512
SC_GROUPS = 4
D_FF = 2816
CHUNK = 64
QKV = 3 * DN_WIDTH
W_IN_COLS = 4 * DN_WIDTH + 2 * HEADS + 3 * SC_WIDTH
WA_COLS = QKV + DN_WIDTH + 3 * SC_WIDTH
LANES = 128
EPS = 1e-6
Q_SCALE = HEAD_DIM ** -0.5
N_CHIPS = 4
N_DEV = 8

ADAM_LR = 0.001
ADAM_B1 = 0.9
ADAM_B2 = 0.999
ADAM_EPS = 1e-08
ADAM_WD = 0.01
ADAM_STEP = 10

VMEM_LIMIT = 56 * 1024 * 1024

NN = (((1,), (0,)), ((), ()))
NT = (((1,), (1,)), ((), ()))
TN = (((0,), (0,)), ((), ()))


def _mm(a, b, dims=NN):
    return lax.dot_general(a.astype(BF16), b.astype(BF16), dims, preferred_element_type=F32)


def _mm32(a, b, dims=NN):
    return lax.dot_general(a, b, dims, preferred_element_type=F32, precision=lax.Precision.HIGHEST)


def _params(sem, vmem=VMEM_LIMIT):
    return pltpu.CompilerParams(dimension_semantics=sem, vmem_limit_bytes=vmem)


def _sigmoid(x):
    return 1.0 / (1.0 + jnp.exp(-x))


def _softplus(x):
    return jnp.maximum(x, 0.0) + jnp.log1p(jnp.exp(-jnp.abs(x)))


def _row_acc(acc_ref, val):
    acc_ref[0:1, :] += jnp.sum(val, axis=0, keepdims=True)


def _rms_bwd(dh, xh, r, gain):
    dxh = dh * gain
    return r * (dxh - xh * jnp.mean(dxh * xh, axis=-1, keepdims=True))


def _before_halo(tb):
    return lambda i: (jnp.maximum(i * (tb // 8) - 1, 0), 0)


def _after_halo(tb, n_rows):
    last = n_rows // 8 - 1
    return lambda i: (jnp.minimum((i + 1) * (tb // 8), last), 0)


def _taps(xc, w, n_taps, tb, first):
    out = w[0:1, :] * xc[first:first + tb, :]
    for j in range(1, n_taps):
        out = out + w[j:j + 1, :] * xc[first + j:first + j + tb, :]
    return out


def _in_proj(x, g1, wa, wbd):
    T = x.shape[0]
    tb = 256

    def body(x_ref, g_ref, wa_ref, wbd_ref, qkv_ref, z_ref, sc_ref, bd_ref, ht_ref):
        xv = x_ref[...]
        r = lax.rsqrt(jnp.mean(xv * xv, axis=-1, keepdims=True) + EPS)
        h = (xv * r * g_ref[...]).astype(BF16)
        p = jnp.dot(h, wa_ref[...], preferred_element_type=F32)
        qkv_ref[...] = p[:, :QKV]
        z_ref[...] = p[:, QKV:QKV + DN_WIDTH]
        sc_ref[...] = p[:, QKV + DN_WIDTH:]
        bd_ref[...] = jnp.dot(h, wbd_ref[...], preferred_element_type=F32)
        ht_ref[...] = h.T

    tok = lambda w: pl.BlockSpec((tb, w), lambda i: (i, 0))
    full = lambda a: pl.BlockSpec(a.shape, lambda i: (0, 0))
    return pl.pallas_call(
        body, name="in_proj", grid=(T // tb,),
        in_specs=[tok(D_MODEL), full(g1), full(wa), full(wbd)],
        out_specs=[tok(QKV), tok(DN_WIDTH), tok(3 * SC_WIDTH), tok(LANES),
                   pl.BlockSpec((D_MODEL, tb), lambda i: (0, i))],
        out_shape=[jax.ShapeDtypeStruct((T, QKV), F32), jax.ShapeDtypeStruct((T, DN_WIDTH), F32),
                   jax.ShapeDtypeStruct((T, 3 * SC_WIDTH), F32), jax.ShapeDtypeStruct((T, LANES), F32),
                   jax.ShapeDtypeStruct((D_MODEL, T), BF16)],
        compiler_params=_params(("parallel",)),
    )(x, g1, wa, wbd)


def _dn_act(pre, halo, cw, tb):
    xc = jnp.concatenate([halo, pre], axis=0)
    c = _taps(xc, cw, 4, tb, 5)
    sg = _sigmoid(c)
    return xc, c, sg, c * sg


def _gates(bd, al_row, dt_row):
    lane = lax.broadcasted_iota(jnp.int32, bd.shape, 1)
    beta = _sigmoid(bd)
    g = -jnp.exp(al_row) * _softplus(bd + dt_row)
    return jnp.where(lane < HEADS, beta, jnp.where(lane < 2 * HEADS, g, 0.0))


def _dn_prep(qkv, cw, bd, al_row, dt_row):
    T = qkv.shape[0]
    tb = 512

    def body(pre_ref, halo_ref, cw_ref, bd_ref, al_ref, dt_ref, q_ref, k_ref, v_ref, bg_ref):
        halo = jnp.where(pl.program_id(0) > 0, halo_ref[...], 0.0)
        _, _, _, a = _dn_act(pre_ref[...], halo, cw_ref[...], tb)
        for hh in range(HEADS):
            sl = slice(HEAD_DIM * hh, HEAD_DIM * (hh + 1))
            qs = a[:, sl]
            q_ref[:, sl] = qs * (lax.rsqrt(jnp.sum(qs * qs, axis=-1, keepdims=True) + EPS) * Q_SCALE)
            ks = a[:, DN_WIDTH + HEAD_DIM * hh:DN_WIDTH + HEAD_DIM * (hh + 1)]
            k_ref[:, sl] = ks * lax.rsqrt(jnp.sum(ks * ks, axis=-1, keepdims=True) + EPS)
        v_ref[...] = a[:, 2 * DN_WIDTH:]
        bg_ref[...] = _gates(bd_ref[...], al_ref[...], dt_ref[...])

    tok = lambda w: pl.BlockSpec((tb, w), lambda i: (i, 0))
    full = lambda a: pl.BlockSpec(a.shape, lambda i: (0, 0))
    return pl.pallas_call(
        body, name="dn_prep", grid=(T // tb,),
        in_specs=[tok(QKV), pl.BlockSpec((8, QKV), _before_halo(tb)), full(cw), tok(LANES), full(al_row), full(dt_row)],
        out_specs=[tok(DN_WIDTH), tok(DN_WIDTH), tok(DN_WIDTH), tok(LANES)],
        out_shape=[jax.ShapeDtypeStruct((T, DN_WIDTH), F32)] * 3 + [jax.ShapeDtypeStruct((T, LANES), F32)],
        compiler_params=_params(("parallel",)),
    )(qkv, qkv, cw, bd, al_row, dt_row)


def _inv_unit_lower(low, eye):
    acc = eye - low
    power = low
    for _ in range(5):
        power = _mm32(power, power)
        acc = _mm32(acc, eye + power)
    return acc


def _chunk_consts():
    row = lax.broadcasted_iota(jnp.int32, (CHUNK, CHUNK), 0)
    col = lax.broadcasted_iota(jnp.int32, (CHUNK, CHUNK), 1)
    return row >= col, row > col, (row == col).astype(F32)


def _chunk_gates(bgc, gcum, h):
    lane = lax.broadcasted_iota(jnp.int32, bgc.shape, 1)
    beta = jnp.sum(jnp.where(lane == h, bgc, 0.0), axis=1, keepdims=True)
    gsel = jnp.where(lane == h + HEADS, gcum, 0.0)
    gc = jnp.sum(gsel, axis=1, keepdims=True)
    gr = jnp.sum(gsel.T, axis=0, keepdims=True)
    return beta, gc, gr


def _chunk_local(q, k, v, beta, gc, gr, consts):
    causal, strict, eye = consts
    diff = gc - gr
    decay = jnp.exp(jnp.where(causal, diff, -1e30))
    kb = k * beta
    vb = v * beta
    low = jnp.where(strict, _mm(kb, k, NT) * decay, 0.0)
    tm = _inv_unit_lower(low, eye)
    eg = jnp.exp(gc)
    kbg = kb * eg
    u = _mm(tm, vb)
    w = _mm(tm, kbg)
    qk = jnp.where(causal, _mm(q, k, NT) * decay, 0.0)
    g_last = gc[CHUNK - 1:CHUNK, :]
    ek = jnp.exp(g_last - gc)
    return dict(decay=decay, kb=kb, vb=vb, low=low, tm=tm, eg=eg, kbg=kbg, u=u, w=w, qk=qk,
                gl=jnp.exp(g_last), ek=ek, q_dec=q * eg, k_dec=k * ek)


def _delta_fwd(q, k, v, bg):
    T = q.shape[0]
    tb = 512
    n_chunk = tb // CHUNK

    def body(q_ref, k_ref, v_ref, bg_ref, o_ref, st_ref, s_ref):
        h = pl.program_id(0)

        @pl.when(pl.program_id(1) == 0)
        def _():
            s_ref[...] = jnp.zeros_like(s_ref)

        consts = _chunk_consts()
        tril = consts[0].astype(F32)

        def chunk(ci, carry):
            rows = pl.ds(pl.multiple_of(ci * CHUNK, CHUNK), CHUNK)
            bgc = bg_ref[rows, :]
            beta, gc, gr = _chunk_gates(bgc, _mm32(tril, bgc), h)
            loc = _chunk_local(q_ref[rows, :], k_ref[rows, :], v_ref[rows, :], beta, gc, gr, consts)
            state = s_ref[...]
            st_ref[0, ci] = state
            v_new = loc["u"] - _mm(loc["w"], state)
            o_ref[rows, :] = _mm(loc["q_dec"], state) + _mm(loc["qk"], v_new)
            s_ref[...] = loc["gl"] * state + _mm(loc["k_dec"], v_new, TN)
            return carry

        lax.fori_loop(0, n_chunk, chunk, 0)

    head = pl.BlockSpec((tb, HEAD_DIM), lambda h, i: (i, h))
    return pl.pallas_call(
        body, name="delta_fwd", grid=(HEADS, T // tb),
        in_specs=[head, head, head, pl.BlockSpec((tb, LANES), lambda h, i: (i, 0))],
        out_specs=[head, pl.BlockSpec((1, n_chunk, HEAD_DIM, HEAD_DIM), lambda h, i: (h, i, 0, 0))],
        out_shape=[jax.ShapeDtypeStruct((T, DN_WIDTH), F32),
                   jax.ShapeDtypeStruct((HEADS, T // CHUNK, HEAD_DIM, HEAD_DIM), F32)],
        scratch_shapes=[pltpu.VMEM((HEAD_DIM, HEAD_DIM), F32)],
        compiler_params=_params(("parallel", "arbitrary")),
    )(q, k, v, bg)


def _dn_out(o, z, gn):
    outs, ohs, rs = [], [], []
    for hh in range(HEADS):
        oh = o[:, HEAD_DIM * hh:HEAD_DIM * (hh + 1)]
        r = lax.rsqrt(jnp.mean(oh * oh, axis=-1, keepdims=True) + EPS)
        ohs.append(oh * r)
        rs.append(r)
    sz = _sigmoid(z)
    oh = jnp.concatenate(ohs, axis=1)
    gn4 = jnp.concatenate([gn] * HEADS, axis=1)
    return oh * gn4 * (z * sz), oh, rs, sz, gn4


def _sc_fwd(sc_in, halo, cw, tb):
    xc = jnp.concatenate([halo, sc_in], axis=0)
    u = xc[:, SC_WIDTH:2 * SC_WIDTH] * xc[:, 2 * SC_WIDTH:]
    cv = _taps(u, cw, 3, tb, 6)
    gate_b = sc_in[:, :SC_WIDTH]
    y = gate_b * cv
    gw = SC_WIDTH // SC_GROUPS
    yhs, rs = [], []
    for gi in range(SC_GROUPS):
        yg = y[:, gw * gi:gw * (gi + 1)]
        r = lax.rsqrt(jnp.mean(yg * yg, axis=-1, keepdims=True) + EPS)
        yhs.append(yg * r)
        rs.append(r)
    return u, cv, gate_b, jnp.concatenate(yhs, axis=1), rs


def _mix_out(o, z, sc_in, x, w_out, gn, scw, gs):
    T = x.shape[0]
    tb = 256

    def body(o_ref, z_ref, sc_ref, halo_ref, x_ref, w_ref, gn_ref, scw_ref, gs_ref, x1_ref, mt_ref):
        o_n = _dn_out(o_ref[...], z_ref[...], gn_ref[...])[0]
        halo = jnp.where(pl.program_id(0) > 0, halo_ref[...], 0.0)
        yh = _sc_fwd(sc_ref[...], halo, scw_ref[...], tb)[3]
        mix = jnp.concatenate([o_n, yh * gs_ref[...]], axis=1).astype(BF16)
        x1_ref[...] = x_ref[...] + jnp.dot(mix, w_ref[...], preferred_element_type=F32)
        mt_ref[...] = mix.T

    tok = lambda w: pl.BlockSpec((tb, w), lambda i: (i, 0))
    full = lambda a: pl.BlockSpec(a.shape, lambda i: (0, 0))
    return pl.pallas_call(
        body, name="mix_out", grid=(T // tb,),
        in_specs=[tok(DN_WIDTH), tok(DN_WIDTH), tok(3 * SC_WIDTH), pl.BlockSpec((8, 3 * SC_WIDTH), _before_halo(tb)),
                  tok(D_MODEL), full(w_out), full(gn), full(scw), full(gs)],
        out_specs=[tok(D_MODEL), pl.BlockSpec((D_MODEL, tb), lambda i: (0, i))],
        out_shape=[jax.ShapeDtypeStruct((T, D_MODEL), F32), jax.ShapeDtypeStruct((D_MODEL, T), BF16)],
        compiler_params=_params(("parallel",)),
    )(o, z, sc_in, sc_in, x, w_out, gn, scw, gs)


def _ffn(x1, g2, wg, wu, wd):
    T = x1.shape[0]
    tb = 256

    def body(x_ref, g_ref, wg_ref, wu_ref, wd_ref, x2_ref, a_ref, b_ref, ht_ref):
        xv = x_ref[...]
        r = lax.rsqrt(jnp.mean(xv * xv, axis=-1, keepdims=True) + EPS)
        h = (xv * r * g_ref[...]).astype(BF16)
        a = jnp.dot(h, wg_ref[...], preferred_element_type=F32)
        b = jnp.dot(h, wu_ref[...], preferred_element_type=F32)
        act = (a * _sigmoid(a) * b).astype(BF16)
        x2_ref[...] = xv + jnp.dot(act, wd_ref[...], preferred_element_type=F32)
        a_ref[...] = a.astype(BF16)
        b_ref[...] = b.astype(BF16)
        ht_ref[...] = h.T

    tok = lambda w: pl.BlockSpec((tb, w), lambda i: (i, 0))
    full = lambda a: pl.BlockSpec(a.shape, lambda i: (0, 0))
    return pl.pallas_call(
        body, name="ffn", grid=(T // tb,),
        in_specs=[tok(D_MODEL), full(g2), full(wg), full(wu), full(wd)],
        out_specs=[tok(D_MODEL), tok(D_FF), tok(D_FF), pl.BlockSpec((D_MODEL, tb), lambda i: (0, i))],
        out_shape=[jax.ShapeDtypeStruct((T, D_MODEL), F32), jax.ShapeDtypeStruct((T, D_FF), BF16),
                   jax.ShapeDtypeStruct((T, D_FF), BF16), jax.ShapeDtypeStruct((D_MODEL, T), BF16)],
        compiler_params=_params(("parallel",)),
    )(x1, g2, wg, wu, wd)


def _loss_head(x, gf, target):
    T = x.shape[0]
    tb = 512

    def body(x_ref, g_ref, t_ref, dx_ref, loss_ref, dg_ref):
        @pl.when(pl.program_id(0) == 0)
        def _():
            loss_ref[...] = jnp.zeros_like(loss_ref)
            dg_ref[...] = jnp.zeros_like(dg_ref)

        xv = x_ref[...]
        r = lax.rsqrt(jnp.mean(xv * xv, axis=-1, keepdims=True) + EPS)
        xh = xv * r
        err = xh * g_ref[...] - t_ref[...]
        per_tok = jnp.mean(err * err, axis=-1, keepdims=True)
        loss_ref[...] += 0.5 * jnp.sum(per_tok, axis=0, keepdims=True)
        dy = err * (1.0 / D_MODEL)
        _row_acc(dg_ref, dy * xh)
        dx_ref[...] = _rms_bwd(dy, xh, r, g_ref[...])

    tok = pl.BlockSpec((tb, D_MODEL), lambda i: (i, 0))
    return pl.pallas_call(
        body, name="loss_head", grid=(T // tb,),
        in_specs=[tok, pl.BlockSpec(gf.shape, lambda i: (0, 0)), tok],
        out_specs=[tok, pl.BlockSpec((8, LANES), lambda i: (0, 0)), pl.BlockSpec((8, D_MODEL), lambda i: (0, 0))],
        out_shape=[jax.ShapeDtypeStruct((T, D_MODEL), F32), jax.ShapeDtypeStruct((8, LANES), F32),
                   jax.ShapeDtypeStruct((8, D_MODEL), F32)],
        compiler_params=_params(("arbitrary",)),
    )(x, gf, target)


def _ffn_bwd(dx2, x1, a, b, g2, wg, wu, wd):
    T = x1.shape[0]
    tb = 256

    def body(dx2_ref, x_ref, a_ref, b_ref, g_ref, wg_ref, wu_ref, wd_ref, dx1_ref, da_ref, db_ref, at_ref, dg_ref):
        @pl.when(pl.program_id(0) == 0)
        def _():
            dg_ref[...] = jnp.zeros_like(dg_ref)

        dx2v = dx2_ref[...]
        av = a_ref[...].astype(F32)
        bv = b_ref[...].astype(F32)
        dact = _mm(dx2v, wd_ref[...], NT)
        sa = _sigmoid(av)
        silu = av * sa
        da = (dact * bv * (sa * (1.0 + av * (1.0 - sa)))).astype(BF16)
        db = (dact * silu).astype(BF16)
        dh = _mm(da, wg_ref[...], NT) + _mm(db, wu_ref[...], NT)
        xv = x_ref[...]
        r = lax.rsqrt(jnp.mean(xv * xv, axis=-1, keepdims=True) + EPS)
        xh = xv * r
        _row_acc(dg_ref, dh * xh)
        dx1_ref[...] = dx2v + _rms_bwd(dh, xh, r, g_ref[...])
        da_ref[...] = da
        db_ref[...] = db
        at_ref[...] = (silu * bv).astype(BF16).T

    tok = lambda w: pl.BlockSpec((tb, w), lambda i: (i, 0))
    full = lambda t: pl.BlockSpec(t.shape, lambda i: (0, 0))
    return pl.pallas_call(
        body, name="ffn_bwd", grid=(T // tb,),
        in_specs=[tok(D_MODEL), tok(D_MODEL), tok(D_FF), tok(D_FF), full(g2), full(wg), full(wu), full(wd)],
        out_specs=[tok(D_MODEL), tok(D_FF), tok(D_FF), pl.BlockSpec((D_FF, tb), lambda i: (0, i)),
                   pl.BlockSpec((8, D_MODEL), lambda i: (0, 0))],
        out_shape=[jax.ShapeDtypeStruct((T, D_MODEL), F32), jax.ShapeDtypeStruct((T, D_FF), BF16),
                   jax.ShapeDtypeStruct((T, D_FF), BF16), jax.ShapeDtypeStruct((D_FF, T), BF16),
                   jax.ShapeDtypeStruct((8, D_MODEL), F32)],
        compiler_params=_params(("arbitrary",)),
    )(dx2, x1, a, b, g2, wg, wu, wd)


def _wgrad(at, b, bm, bn, name):
    M, T = at.shape
    N = b.shape[1]
    bk = min(T, 1024)

    def body(a_ref, b_ref, o_ref):
        @pl.when(pl.program_id(2) == 0)
        def _():
            o_ref[...] = jnp.zeros_like(o_ref)

        o_ref[...] += jnp.dot(a_ref[...], b_ref[...], preferred_element_type=F32)

    return pl.pallas_call(
        body, name=name, grid=(M // bm, N // bn, T // bk),
        in_specs=[pl.BlockSpec((bm, bk), lambda i, j, kk: (i, kk)), pl.BlockSpec((bk, bn), lambda i, j, kk: (kk, j))],
        out_specs=pl.BlockSpec((bm, bn), lambda i, j, kk: (i, j)),
        out_shape=jax.ShapeDtypeStruct((M, N), F32),
        compiler_params=_params(("parallel", "parallel", "arbitrary")),
    )(at, b)


def _mix_out_bwd(dx1, o, z, sc_in, w_out, gn, scw, gs):
    T = dx1.shape[0]
    tb = 256

    def body(dx_ref, o_ref, z_ref, sc_ref, halo_ref, w_ref, gn_ref, scw_ref, gs_ref,
             do_ref, dz_ref, dgb_ref, dcv_ref, dgn_ref, dgs_ref, dscw_ref):
        @pl.when(pl.program_id(0) == 0)
        def _():
            dgn_ref[...] = jnp.zeros_like(dgn_ref)
            dgs_ref[...] = jnp.zeros_like(dgs_ref)
            dscw_ref[...] = jnp.zeros_like(dscw_ref)

        dmix = _mm(dx_ref[...], w_ref[...], NT)
        don = dmix[:, :DN_WIDTH]
        dosc = dmix[:, DN_WIDTH:]
        zv = z_ref[...]
        _, oh, rs, sz, gn4 = _dn_out(o_ref[...], zv, gn_ref[...])
        silu_z = zv * sz
        dgn_full = don * oh * silu_z
        dgn_ref[0:1, :] += jnp.sum(sum(dgn_full[:, HEAD_DIM * hh:HEAD_DIM * (hh + 1)] for hh in range(HEADS)),
                                   axis=0, keepdims=True)
        dz_ref[...] = (don * oh * gn4 * (sz * (1.0 + zv * (1.0 - sz)))).astype(BF16)
        t = don * gn4 * silu_z
        for hh in range(HEADS):
            sl = slice(HEAD_DIM * hh, HEAD_DIM * (hh + 1))
            th, ohh = t[:, sl], oh[:, sl]
            do_ref[:, sl] = rs[hh] * (th - ohh * jnp.mean(th * ohh, axis=-1, keepdims=True))
        halo = jnp.where(pl.program_id(0) > 0, halo_ref[...], 0.0)
        u, cv, gate_b, yh, rys = _sc_fwd(sc_ref[...], halo, scw_ref[...], tb)
        _row_acc(dgs_ref, dosc * yh)
        ty = dosc * gs_ref[...]
        gw = SC_WIDTH // SC_GROUPS
        dys = []
        for gi in range(SC_GROUPS):
            sl = slice(gw * gi, gw * (gi + 1))
            tg, yg = ty[:, sl], yh[:, sl]
            dys.append(rys[gi] * (tg - yg * jnp.mean(tg * yg, axis=-1, keepdims=True)))
        dy = jnp.concatenate(dys, axis=1)
        dgb_ref[...] = dy * cv
        dcv = dy * gate_b
        dcv_ref[...] = dcv
        for j in range(3):
            dscw_ref[j:j + 1, :] += jnp.sum(dcv * u[6 + j:6 + j + tb, :], axis=0, keepdims=True)

    tok = lambda w: pl.BlockSpec((tb, w), lambda i: (i, 0))
    full = lambda t: pl.BlockSpec(t.shape, lambda i: (0, 0))
    acc = lambda w: pl.BlockSpec((8, w), lambda i: (0, 0))
    return pl.pallas_call(
        body, name="mix_out_bwd", grid=(T // tb,),
        in_specs=[tok(D_MODEL), tok(DN_WIDTH), tok(DN_WIDTH), tok(3 * SC_WIDTH),
                  pl.BlockSpec((8, 3 * SC_WIDTH), _before_halo(tb)), full(w_out), full(gn), full(scw), full(gs)],
        out_specs=[tok(DN_WIDTH), tok(DN_WIDTH), tok(SC_WIDTH), tok(SC_WIDTH), acc(HEAD_DIM), acc(SC_WIDTH), acc(SC_WIDTH)],
        out_shape=[jax.ShapeDtypeStruct((T, DN_WIDTH), F32), jax.ShapeDtypeStruct((T, DN_WIDTH), BF16),
                   jax.ShapeDtypeStruct((T, SC_WIDTH), F32), jax.ShapeDtypeStruct((T, SC_WIDTH), F32),
                   jax.ShapeDtypeStruct((8, HEAD_DIM), F32), jax.ShapeDtypeStruct((8, SC_WIDTH), F32),
                   jax.ShapeDtypeStruct((8, SC_WIDTH), F32)],
        compiler_params=_params(("arbitrary",)),
    )(dx1, o, z, sc_in, sc_in, w_out, gn, scw, gs)


def _sc_conv_bwd(dcv, dgb, sc_in, scw):
    T = dcv.shape[0]
    tb = 512

    def body(dcv_ref, halo_ref, dgb_ref, sc_ref, w_ref, out_ref):
        last = pl.program_id(0) == pl.num_programs(0) - 1
        halo = jnp.where(last, 0.0, halo_ref[...])
        xc = jnp.concatenate([dcv_ref[...], halo], axis=0)
        w = w_ref[...]
        du = w[2:3, :] * xc[0:tb, :] + w[1:2, :] * xc[1:tb + 1, :] + w[0:1, :] * xc[2:tb + 2, :]
        sc = sc_ref[...]
        out_ref[:, :SC_WIDTH] = dgb_ref[...].astype(BF16)
        out_ref[:, SC_WIDTH:2 * SC_WIDTH] = (du * sc[:, 2 * SC_WIDTH:]).astype(BF16)
        out_ref[:, 2 * SC_WIDTH:] = (du * sc[:, SC_WIDTH:2 * SC_WIDTH]).astype(BF16)

    tok = lambda w: pl.BlockSpec((tb, w), lambda i: (i, 0))
    return pl.pallas_call(
        body, name="sc_conv_bwd", grid=(T // tb,),
        in_specs=[tok(SC_WIDTH), pl.BlockSpec((8, SC_WIDTH), _after_halo(tb, T)), tok(SC_WIDTH), tok(3 * SC_WIDTH),
                  pl.BlockSpec(scw.shape, lambda i: (0, 0))],
        out_specs=tok(3 * SC_WIDTH),
        out_shape=jax.ShapeDtypeStruct((T, 3 * SC_WIDTH), BF16),
        compiler_params=_params(("parallel",)),
    )(dcv, dcv, dgb, sc_in, scw)


def _delta_bwd(q, k, v, bg, states, do):
    T = q.shape[0]
    tb = 512
    n_chunk = tb // CHUNK
    nb = T // tb

    def body(q_ref, k_ref, v_ref, bg_ref, st_ref, do_ref, dq_ref, dk_ref, dv_ref, dbg_ref, ds_ref):
        h = pl.program_id(0)

        @pl.when(pl.program_id(1) == 0)
        def _():
            ds_ref[...] = jnp.zeros_like(ds_ref)

        consts = _chunk_consts()
        causal, strict, _ = consts
        tril = causal.astype(F32)
        lane = lax.broadcasted_iota(jnp.int32, (CHUNK, LANES), 1)

        def chunk(cj, carry):
            ci = n_chunk - 1 - cj
            rows = pl.ds(pl.multiple_of(ci * CHUNK, CHUNK), CHUNK)
            bgc = bg_ref[rows, :]
            beta, gc, gr = _chunk_gates(bgc, _mm32(tril, bgc), h)
            qv, kv, vv = q_ref[rows, :], k_ref[rows, :], v_ref[rows, :]
            loc = _chunk_local(qv, kv, vv, beta, gc, gr, consts)
            state = st_ref[0, ci]
            ds_next = ds_ref[...]
            dov = do_ref[rows, :]
            tm, qk, w, decay = loc["tm"], loc["qk"], loc["w"], loc["decay"]
            v_new = loc["u"] - _mm(w, state)
            dv_new = _mm(qk, dov, TN) + _mm(loc["k_dec"], ds_next)
            dq_dec = _mm(dov, state, NT)
            dqk = jnp.where(causal, _mm(dov, v_new, NT), 0.0)
            dk_dec = _mm(v_new, ds_next, NT)
            dgl = jnp.sum(jnp.sum(ds_next * state, axis=1, keepdims=True), axis=0, keepdims=True)
            ds_ref[...] = loc["gl"] * ds_next + _mm(loc["q_dec"], dov, TN) - _mm(w, dv_new, TN)
            dw = -_mm(dv_new, state, NT)
            dtm = _mm(dv_new, loc["vb"], NT) + _mm(dw, loc["kbg"], NT)
            dvb = _mm(tm, dv_new, TN)
            dkbg = _mm(tm, dw, TN)
            dlow = jnp.where(strict, -_mm(_mm(tm, dtm, TN), tm, NT), 0.0)
            dmm = dlow * decay
            dkb = _mm(dmm, kv) + dkbg * loc["eg"]
            dnn = dqk * decay
            dq_ref[rows, :] = _mm(dnn, kv) + dq_dec * loc["eg"]
            dk_ref[rows, :] = (_mm(dmm, loc["kb"], TN) + _mm(dnn, qv, TN) + dk_dec * loc["ek"] + dkb * beta)
            dv_ref[rows, :] = dvb * beta
            dbeta = jnp.sum(dkb * kv + dvb * vv, axis=1, keepdims=True)
            e = dlow * loc["low"] + dqk * qk
            kd = jnp.sum(dk_dec * loc["k_dec"], axis=1, keepdims=True)
            dgc = (jnp.sum(e, axis=1, keepdims=True) - jnp.sum(e.T, axis=1, keepdims=True)
                   + jnp.sum(dq_dec * loc["q_dec"], axis=1, keepdims=True) - kd
                   + jnp.sum(dkbg * loc["kbg"], axis=1, keepdims=True))
            d_last = jnp.sum(kd, axis=0, keepdims=True) + dgl * loc["gl"]
            row = lax.broadcasted_iota(jnp.int32, (CHUNK, 1), 0)
            dgc = dgc + jnp.where(row == CHUNK - 1, d_last, 0.0)
            dgc_full = jnp.where(lane == h + HEADS, dgc, 0.0)
            dg_full = _mm32(tril, dgc_full, TN)
            dbg_ref[0, rows, :] = jnp.where(lane == h, dbeta, dg_full)
            return carry

        lax.fori_loop(0, n_chunk, chunk, 0)

    head = pl.BlockSpec((tb, HEAD_DIM), lambda h, i: (nb - 1 - i, h))
    return pl.pallas_call(
        body, name="delta_bwd", grid=(HEADS, nb),
        in_specs=[head, head, head, pl.BlockSpec((tb, LANES), lambda h, i: (nb - 1 - i, 0)),
                  pl.BlockSpec((1, n_chunk, HEAD_DIM, HEAD_DIM), lambda h, i: (h, nb - 1 - i, 0, 0)), head],
        out_specs=[head, head, head, pl.BlockSpec((1, tb, LANES), lambda h, i: (h, nb - 1 - i, 0))],
        out_shape=[jax.ShapeDtypeStruct((T, DN_WIDTH), F32)] * 3 + [jax.ShapeDtypeStruct((HEADS, T, LANES), F32)],
        scratch_shapes=[pltpu.VMEM((HEAD_DIM, HEAD_DIM), F32)],
        compiler_params=_params(("parallel", "arbitrary")),
    )(q, k, v, bg, states, do)


def _dn_prep_bwd(dq, dk, dv, dbg, qkv, cw, bd, al_row, dt_row):
    T = qkv.shape[0]
    tb = 256

    def body(dq_ref, dk_ref, dv_ref, dbg_ref, pre_ref, halo_ref, cw_ref, bd_ref, al_ref, dt_ref,
             dc_ref, dbd_ref, dcw_ref, dal_ref, ddt_ref):
        @pl.when(pl.program_id(0) == 0)
        def _():
            dcw_ref[...] = jnp.zeros_like(dcw_ref)
            dal_ref[...] = jnp.zeros_like(dal_ref)
            ddt_ref[...] = jnp.zeros_like(ddt_ref)

        halo = jnp.where(pl.program_id(0) > 0, halo_ref[...], 0.0)
        xc, c, sg, a = _dn_act(pre_ref[...], halo, cw_ref[...], tb)
        dsilu = sg * (1.0 + c * (1.0 - sg))
        for hh in range(HEADS):
            sl = slice(HEAD_DIM * hh, HEAD_DIM * (hh + 1))
            for base, g_ref, scale in ((0, dq_ref, Q_SCALE), (DN_WIDTH, dk_ref, 1.0)):
                sa = slice(base + HEAD_DIM * hh, base + HEAD_DIM * (hh + 1))
                raw = a[:, sa]
                r = lax.rsqrt(jnp.sum(raw * raw, axis=-1, keepdims=True) + EPS)
                nrm = raw * r
                gn_ = g_ref[:, sl] * scale
                dc_ref[:, sa] = r * (gn_ - nrm * jnp.sum(gn_ * nrm, axis=-1, keepdims=True)) * dsilu[:, sa]
        dc_ref[:, 2 * DN_WIDTH:] = dv_ref[...] * dsilu[:, 2 * DN_WIDTH:]
        dc = dc_ref[...]
        for j in range(4):
            dcw_ref[j:j + 1, :] += jnp.sum(dc * xc[5 + j:5 + j + tb, :], axis=0, keepdims=True)
        dbgv = dbg_ref[0] + dbg_ref[1] + dbg_ref[2] + dbg_ref[3]
        bdv = bd_ref[...]
        lane = lax.broadcasted_iota(jnp.int32, bdv.shape, 1)
        is_b = lane < HEADS
        is_g = jnp.logical_and(lane >= HEADS, lane < 2 * HEADS)
        beta = _sigmoid(bdv)
        neg_a = -jnp.exp(al_ref[...])
        pre_sp = bdv + dt_ref[...]
        g = neg_a * _softplus(pre_sp)
        da_in = dbgv * neg_a * _sigmoid(pre_sp)
        dbd_ref[...] = jnp.where(is_b, dbgv * beta * (1.0 - beta), jnp.where(is_g, da_in, 0.0)).astype(BF16)
        _row_acc(dal_ref, jnp.where(is_g, dbgv * g, 0.0))
        _row_acc(ddt_ref, jnp.where(is_g, da_in, 0.0))

    tok = lambda w: pl.BlockSpec((tb, w), lambda i: (i, 0))
    full = lambda t: pl.BlockSpec(t.shape, lambda i: (0, 0))
    acc = lambda w: pl.BlockSpec((8, w), lambda i: (0, 0))
    return pl.pallas_call(
        body, name="dn_prep_bwd", grid=(T // tb,),
        in_specs=[tok(DN_WIDTH), tok(DN_WIDTH), tok(DN_WIDTH), pl.BlockSpec((HEADS, tb, LANES), lambda i: (0, i, 0)),
                  tok(QKV), pl.BlockSpec((8, QKV), _before_halo(tb)), full(cw), tok(LANES), full(al_row), full(dt_row)],
        out_specs=[tok(QKV), tok(LANES), acc(QKV), acc(LANES), acc(LANES)],
        out_shape=[jax.ShapeDtypeStruct((T, QKV), F32), jax.ShapeDtypeStruct((T, LANES), BF16),
                   jax.ShapeDtypeStruct((8, QKV), F32), jax.ShapeDtypeStruct((8, LANES), F32),
                   jax.ShapeDtypeStruct((8, LANES), F32)],
        compiler_params=_params(("arbitrary",)),
    )(dq, dk, dv, dbg, qkv, qkv, cw, bd, al_row, dt_row)


def _dn_conv_bwd(dc, cw):
    T = dc.shape[0]
    tb = 512

    def body(dc_ref, halo_ref, w_ref, out_ref):
        last = pl.program_id(0) == pl.num_programs(0) - 1
        halo = jnp.where(last, 0.0, halo_ref[...])
        xc = jnp.concatenate([dc_ref[...], halo], axis=0)
        w = w_ref[...]
        acc = w[3:4, :] * xc[0:tb, :]
        for j in range(3):
            acc = acc + w[j:j + 1, :] * xc[3 - j:3 - j + tb, :]
        out_ref[...] = acc.astype(BF16)

    tok = pl.BlockSpec((tb, QKV), lambda i: (i, 0))
    return pl.pallas_call(
        body, name="dn_conv_bwd", grid=(T // tb,),
        in_specs=[tok, pl.BlockSpec((8, QKV), _after_halo(tb, T)), pl.BlockSpec(cw.shape, lambda i: (0, 0))],
        out_specs=tok,
        out_shape=jax.ShapeDtypeStruct((T, QKV), BF16),
        compiler_params=_params(("parallel",)),
    )(dc, dc, cw)


def _in_proj_bwd(dqkv, dz, dsc, dbd, dx1, x, g1, wa, wbd):
    T = x.shape[0]
    tb = 256

    def body(dqkv_ref, dz_ref, dsc_ref, dbd_ref, dx1_ref, x_ref, g_ref, wa_ref, wbd_ref, dx_ref, dg_ref):
        @pl.when(pl.program_id(0) == 0)
        def _():
            dg_ref[...] = jnp.zeros_like(dg_ref)

        dh = (_mm(dqkv_ref[...], wa_ref[:, :QKV], NT) + _mm(dz_ref[...], wa_ref[:, QKV:QKV + DN_WIDTH], NT)
              + _mm(dsc_ref[...], wa_ref[:, QKV + DN_WIDTH:], NT) + _mm(dbd_ref[...], wbd_ref[...], NT))
        xv = x_ref[...]
        r = lax.rsqrt(jnp.mean(xv * xv, axis=-1, keepdims=True) + EPS)
        xh = xv * r
        _row_acc(dg_ref, dh * xh)
        dx_ref[...] = dx1_ref[...] + _rms_bwd(dh, xh, r, g_ref[...])

    tok = lambda w: pl.BlockSpec((tb, w), lambda i: (i, 0))
    full = lambda t: pl.BlockSpec(t.shape, lambda i: (0, 0))
    return pl.pallas_call(
        body, name="in_proj_bwd", grid=(T // tb,),
        in_specs=[tok(QKV), tok(DN_WIDTH), tok(3 * SC_WIDTH), tok(LANES), tok(D_MODEL), tok(D_MODEL),
                  full(g1), full(wa), full(wbd)],
        out_specs=[tok(D_MODEL), pl.BlockSpec((8, D_MODEL), lambda i: (0, 0))],
        out_shape=[jax.ShapeDtypeStruct((T, D_MODEL), F32), jax.ShapeDtypeStruct((8, D_MODEL), F32)],
        compiler_params=_params(("arbitrary",)),
    )(dqkv, dz, dsc, dbd, dx1, x, g1, wa, wbd)


def _pad_rows(a, rows=8):
    return jnp.pad(a, ((0, rows - a.shape[0]), (0, 0)))


def _gate_rows(a_log, dt_bias):
    put = lambda t: jnp.pad(t.reshape(1, HEADS), ((0, 0), (HEADS, LANES - 2 * HEADS)))
    return put(a_log), put(dt_bias)


def _split_w_in(w_in):
    o = QKV + DN_WIDTH
    wa = jnp.concatenate([w_in[:, :o], w_in[:, o + 2 * HEADS:]], axis=1)
    wbd = jnp.pad(w_in[:, o:o + 2 * HEADS], ((0, 0), (0, LANES - 2 * HEADS)))
    return wa, wbd


def _layer_fwd(x, p):
    qkv, z, sc_in, bd, ht = _in_proj(x, p["g1"], p["wa"], p["wbd"])
    q, k, v, bg = _dn_prep(qkv, p["cw"], bd, p["al"], p["dt"])
    o, states = _delta_fwd(q, k, v, bg)
    x1, mt = _mix_out(o, z, sc_in, x, p["w_out"], p["gn"], p["scw"], p["gs"])
    x2, a, b, h2t = _ffn(x1, p["g2"], p["wg"], p["wu"], p["wd"])
    saved = dict(x=x, qkv=qkv, z=z, sc_in=sc_in, bd=bd, ht=ht, q=q, k=k, v=v, bg=bg, o=o, states=states,
                 x1=x1, mt=mt, a=a, b=b, h2t=h2t)
    return x2, saved


def _layer_bwd(dx2, s, p):
    dx1, da, db, act_t, dg2 = _ffn_bwd(dx2, s["x1"], s["a"], s["b"], p["g2"], p["wg"], p["wu"], p["wd"])
    g = {}
    g["wd"] = _wgrad(act_t, dx2.astype(BF16), 704, 1024, "wgrad_down")
    g["wg"] = _wgrad(s["h2t"], da, 512, 1408, "wgrad_gate")
    g["wu"] = _wgrad(s["h2t"], db, 512, 1408, "wgrad_up")
    do, dz, dgb, dcv, dgn, dgs, dscw = _mix_out_bwd(dx1, s["o"], s["z"], s["sc_in"], p["w_out"], p["gn"], p["scw"], p["gs"])
    g["w_out"] = _wgrad(s["mt"], dx1.astype(BF16), 512, 1024, "wgrad_out")
    dsc = _sc_conv_bwd(dcv, dgb, s["sc_in"], p["scw"])
    dq, dk, dv, dbg = _delta_bwd(s["q"], s["k"], s["v"], s["bg"], s["states"], do)
    dc, dbd, dcw, dal, ddt = _dn_prep_bwd(dq, dk, dv, dbg, s["qkv"], p["cw"], s["bd"], p["al"], p["dt"])
    dqkv = _dn_conv_bwd(dc, p["cw"])
    dx, dg1 = _in_proj_bwd(dqkv, dz, dsc, dbd, dx1, s["x"], p["g1"], p["wa"], p["wbd"])
    o = QKV + DN_WIDTH
    g["w_in"] = jnp.concatenate([
        _wgrad(s["ht"], dqkv, 512, 768, "wgrad_qkv"), _wgrad(s["ht"], dz, 512, 512, "wgrad_z"),
        _wgrad(s["ht"], dbd, 512, LANES, "wgrad_bd")[:, :2 * HEADS], _wgrad(s["ht"], dsc, 512, 768, "wgrad_sc")], axis=1)
    g.update(g1=dg1[0], g2=dg2[0], gn=dgn[0], gs=dgs[0], scw=dscw[:3], cw=dcw[:4],
             al=dal[0, HEADS:2 * HEADS], dt=ddt[0, HEADS:2 * HEADS])
    return dx, g


def _place():
    return lax.axis_index("x"), lax.axis_index("y"), lax.axis_index("c")


def _other_chips(x, y):
    return [(1 - x, y), (x, 1 - y), (1 - x, 1 - y)]


_HBM = pl.BlockSpec(memory_space=pltpu.HBM)


def _chip_exchange(arrs, name, gather):
    n = len(arrs)

    def body(*refs):
        ins, outs = refs[:n], refs[n:2 * n]
        send_sems, recv_sems, local_sems = refs[2 * n:]
        x, y, c = _place()
        me = 2 * x + y
        others = _other_chips(x, y)

        def remote(k, j, landing):
            px, py = others[j]
            src = ins[k] if gather else ins[k].at[2 * px + py]
            return pltpu.make_async_remote_copy(src_ref=src, dst_ref=outs[k].at[landing], send_sem=send_sems.at[k, j],
                                                recv_sem=recv_sems.at[k, j], device_id=(px, py, c), device_id_type=MESH)

        local = [pltpu.make_async_copy(ins[k] if gather else ins[k].at[me], outs[k].at[me], local_sems.at[k])
                 for k in range(n)]
        sends = [remote(k, j, me) for k in range(n) for j in range(3)]
        for cp in local + sends:
            cp.start()
        for k in range(n):
            for j, (px, py) in enumerate(others):
                remote(k, j, 2 * px + py).wait_recv()
        for cp in sends:
            cp.wait_send()
        for cp in local:
            cp.wait()

    shapes = [jax.ShapeDtypeStruct(((N_CHIPS,) + a.shape) if gather else a.shape, a.dtype) for a in arrs]
    return pl.pallas_call(
        body, name=name, in_specs=[_HBM] * n, out_specs=[_HBM] * n, out_shape=shapes,
        scratch_shapes=[pltpu.SemaphoreType.DMA((n, 3)), pltpu.SemaphoreType.DMA((n, 3)), pltpu.SemaphoreType.DMA((n,))],
    )(*arrs)


def _swap_sibling(arrs):
    n = len(arrs)

    def body(*refs):
        ins, outs = refs[:n], refs[n:2 * n]
        send_sems, recv_sems = refs[2 * n:]
        x, y, c = _place()
        copies = [pltpu.make_async_remote_copy(src_ref=ins[k], dst_ref=outs[k], send_sem=send_sems.at[k],
                                               recv_sem=recv_sems.at[k], device_id=(x, y, 1 - c), device_id_type=MESH)
                  for k in range(n)]
        for cp in copies:
            cp.start()
        for cp in copies:
            cp.wait()

    return pl.pallas_call(
        body, name="swap_sibling", in_specs=[_HBM] * n, out_specs=[_HBM] * n,
        out_shape=[jax.ShapeDtypeStruct(a.shape, a.dtype) for a in arrs],
        scratch_shapes=[pltpu.SemaphoreType.DMA((n,)), pltpu.SemaphoreType.DMA((n,))],
    )(*arrs)


def _all_reduce_small(v):
    rows = v.shape[0]
    flips = [(a, b, cc) for a in (0, 1) for b in (0, 1) for cc in (0, 1)][1:]

    def body(v_ref, out_ref, buf_ref, send_sems, recv_sems):
        x, y, c = _place()
        me = 4 * x + 2 * y + c
        peers = [((1 - x) if a else x, (1 - y) if b else y, (1 - c) if cc else c) for a, b, cc in flips]

        def copy(j, landing):
            return pltpu.make_async_remote_copy(src_ref=v_ref, dst_ref=buf_ref.at[landing], send_sem=send_sems.at[j],
                                                recv_sem=recv_sems.at[j], device_id=peers[j], device_id_type=MESH)

        sends = [copy(j, me) for j in range(N_DEV - 1)]
        for cp in sends:
            cp.start()
        buf_ref[me] = v_ref[...]
        for j, (px, py, pc) in enumerate(peers):
            copy(j, 4 * px + 2 * py + pc).wait_recv()
        for cp in sends:
            cp.wait_send()
        acc = buf_ref[0]
        for d in range(1, N_DEV):
            acc = acc + buf_ref[d]
        out_ref[...] = acc

    vmem = pl.BlockSpec(memory_space=pltpu.VMEM)
    return pl.pallas_call(
        body, name="all_reduce_small", in_specs=[vmem], out_specs=vmem,
        out_shape=jax.ShapeDtypeStruct(v.shape, F32),
        scratch_shapes=[pltpu.VMEM((N_DEV, rows, LANES), F32), pltpu.SemaphoreType.DMA((N_DEV - 1,)),
                        pltpu.SemaphoreType.DMA((N_DEV - 1,))],
    )(v)


def _sum_chips(parts):
    _, rows, cols = parts.shape
    tr = 256

    def body(p_ref, o_ref):
        acc = p_ref[0].astype(F32)
        for s in range(1, N_CHIPS):
            acc = acc + p_ref[s].astype(F32)
        o_ref[...] = acc

    return pl.pallas_call(
        body, name="sum_chips", grid=(rows // tr,),
        in_specs=[pl.BlockSpec((N_CHIPS, tr, cols), lambda i: (0, i, 0))],
        out_specs=pl.BlockSpec((tr, cols), lambda i: (i, 0)),
        out_shape=jax.ShapeDtypeStruct((rows, cols), F32),
        compiler_params=_params(("parallel",)),
    )(parts)


def _adamw(w, m, v, g_parts, name):
    rows, cols = w.shape
    tr = min(rows, 256)
    n = len(g_parts)
    c1 = 1.0 - ADAM_B1 ** ADAM_STEP
    c2 = 1.0 - ADAM_B2 ** ADAM_STEP

    def body(*refs):
        w_ref, m_ref, v_ref = refs[:3]
        g_refs = refs[3:3 + n]
        g_out, d_out, m_out, v_out = refs[3 + n:]
        g = g_refs[0][...]
        for r in g_refs[1:]:
            g = g + r[...]
        m_new = ADAM_B1 * m_ref[...] + (1.0 - ADAM_B1) * g
        v_new = ADAM_B2 * v_ref[...] + (1.0 - ADAM_B2) * (g * g)
        g_out[...] = g
        m_out[...] = m_new
        v_out[...] = v_new
        d_out[...] = -ADAM_LR * ((m_new / c1) / (jnp.sqrt(v_new / c2) + ADAM_EPS) + ADAM_WD * w_ref[...])

    blk = pl.BlockSpec((tr, cols), lambda i: (i, 0))
    return pl.pallas_call(
        body, name=name, grid=(rows // tr,),
        in_specs=[blk] * (3 + n), out_specs=[blk] * 4,
        out_shape=[jax.ShapeDtypeStruct((rows, cols), F32)] * 4,
        compiler_params=_params(("parallel",)),
    )(w, m, v, *g_parts)


def _pack(parts, rows, fill=0.0):
    flat = jnp.concatenate([p.reshape(-1) for p in parts])
    return jnp.pad(flat, (0, rows * LANES - flat.shape[0]), constant_values=fill).reshape(rows, LANES)


def _unpack(packed, shapes):
    flat = packed.reshape(-1)
    out, at = [], 0
    for shp in shapes:
        size = 1
        for s in shp:
            size *= s
        out.append(flat[at:at + size].reshape(shp))
        at += size
    return out


def _packed_rows(shapes):
    total = 0
    for shp in shapes:
        size = 1
        for s in shp:
            size *= s
        total += size
    return -(-total // (8 * LANES)) * 8


def _cols_full(g, l):
    t = g[:, l]
    return jnp.moveaxis(t, 0, 1).reshape(t.shape[1], N_CHIPS * t.shape[2])


def _rows_full(g, l):
    t = g[:, l]
    return t.reshape(N_CHIPS * t.shape[1], t.shape[2])


def _cols_to_shards(t):
    L, rows, cols = t.shape
    return jnp.moveaxis(t.reshape(L, rows, N_CHIPS, cols // N_CHIPS), 2, 0)


def _rows_to_shards(t):
    L, rows, cols = t.shape
    return jnp.moveaxis(t.reshape(L, N_CHIPS, rows // N_CHIPS, cols), 1, 0)


def kernel(x, norm1_g, w_in, dn_conv_w, dn_a_log, dn_dt_bias, dn_norm_g, sc_conv_w, sc_norm_g, w_out, norm2_g, ffn_w_gate, ffn_w_up, ffn_w_down, final_norm_g, loss_target, m_norm1_g, m_w_in, m_dn_conv_w, m_dn_a_log, m_dn_dt_bias, m_dn_norm_g, m_sc_conv_w, m_sc_norm_g, m_w_out, m_norm2_g, m_ffn_w_gate, m_ffn_w_up, m_ffn_w_down, m_final_norm_g, v_norm1_g, v_w_in, v_dn_conv_w, v_dn_a_log, v_dn_dt_bias, v_dn_norm_g, v_sc_conv_w, v_sc_norm_g, v_w_out, v_norm2_g, v_ffn_w_gate, v_ffn_w_up, v_ffn_w_down, v_final_norm_g):
    chip = 2 * lax.axis_index("x") + lax.axis_index("y")

    g_in, g_out, g_gate, g_up, g_down, g_cw, g_scw = _chip_exchange(
        [w_in.astype(BF16), w_out.astype(BF16), ffn_w_gate.astype(BF16), ffn_w_up.astype(BF16),
         ffn_w_down.astype(BF16), dn_conv_w, sc_conv_w], "gather_weights", gather=True)

    layers = []
    for l in range(DEPTH):
        wa, wbd = _split_w_in(_cols_full(g_in, l))
        al, dt = _gate_rows(dn_a_log[l], dn_dt_bias[l])
        layers.append(dict(
            g1=norm1_g[l][None], wa=wa, wbd=wbd, cw=_pad_rows(_cols_full(g_cw, l)), al=al, dt=dt,
            gn=dn_norm_g[l][None], scw=_pad_rows(_cols_full(g_scw, l)), gs=sc_norm_g[l][None],
            w_out=_rows_full(g_out, l), g2=norm2_g[l][None], wg=_cols_full(g_gate, l), wu=_cols_full(g_up, l),
            wd=_rows_full(g_down, l)))

    act = x[0]
    saved = []
    for l in range(DEPTH):
        act, s = _layer_fwd(act, layers[l])
        saved.append(s)
    dact, loss_part, d_final = _loss_head(act, final_norm_g[None], loss_target[0])
    grads = [None] * DEPTH
    for l in reversed(range(DEPTH)):
        dact, grads[l] = _layer_bwd(dact, saved[l], layers[l])
    loss = lax.psum(loss_part[0, 0], ("x", "y", "c"))
    stack = lambda key: jnp.stack([grads[l][key] for l in range(DEPTH)])

    parts = [_cols_to_shards(stack("w_in")), _rows_to_shards(stack("w_out")), _cols_to_shards(stack("wg")),
             _cols_to_shards(stack("wu")), _rows_to_shards(stack("wd"))]
    got = _chip_exchange([p.astype(BF16) for p in parts], "reduce_grads", gather=False)
    mine = [_sum_chips(t.reshape(N_CHIPS, -1, t.shape[-1])) for t in got]
    theirs = _swap_sibling(mine)
    big = {}
    for key, w, m, v, a, b in zip(("w_in", "w_out", "ffn_w_gate", "ffn_w_up", "ffn_w_down"),
                                   (w_in, w_out, ffn_w_gate, ffn_w_up, ffn_w_down),
                                   (m_w_in, m_w_out, m_ffn_w_gate, m_ffn_w_up, m_ffn_w_down),
                                   (v_w_in, v_w_out, v_ffn_w_gate, v_ffn_w_up, v_ffn_w_down), mine, theirs):
        flat = lambda t: t.reshape(-1, t.shape[-1])
        big[key] = [o.reshape(w.shape) for o in _adamw(flat(w), flat(m), flat(v), [a, b], "adamw_" + key)]

    full_shapes = [(DEPTH, D_MODEL), (DEPTH, D_MODEL), (DEPTH, HEAD_DIM), (DEPTH, SC_WIDTH), (DEPTH, HEADS),
                   (DEPTH, HEADS), (D_MODEL,), (DEPTH, 4, QKV), (DEPTH, 3, SC_WIDTH)]
    small_keys = ("g1", "g2", "gn", "gs", "al", "dt")
    packed = _pack([stack(k) for k in small_keys] + [d_final[0], stack("cw"), stack("scw")], _packed_rows(full_shapes))
    sg = _unpack(_all_reduce_small(packed), full_shapes)
    sg[7] = lax.dynamic_slice_in_dim(sg[7], chip * (QKV // N_CHIPS), QKV // N_CHIPS, axis=2)
    sg[8] = lax.dynamic_slice_in_dim(sg[8], chip * (SC_WIDTH // N_CHIPS), SC_WIDTH // N_CHIPS, axis=2)
    small_names = ("norm1_g", "norm2_g", "dn_norm_g", "sc_norm_g", "dn_a_log", "dn_dt_bias", "final_norm_g",
                   "dn_conv_w", "sc_conv_w")
    sw = (norm1_g, norm2_g, dn_norm_g, sc_norm_g, dn_a_log, dn_dt_bias, final_norm_g, dn_conv_w, sc_conv_w)
    sm = (m_norm1_g, m_norm2_g, m_dn_norm_g, m_sc_norm_g, m_dn_a_log, m_dn_dt_bias, m_final_norm_g, m_dn_conv_w, m_sc_conv_w)
    sv = (v_norm1_g, v_norm2_g, v_dn_norm_g, v_sc_norm_g, v_dn_a_log, v_dn_dt_bias, v_final_norm_g, v_dn_conv_w, v_sc_conv_w)
    shard_shapes = [t.shape for t in sw]
    rows = _packed_rows(shard_shapes)
    outs = _adamw(_pack(sw, rows), _pack(sm, rows), _pack(sv, rows, fill=1.0), [_pack(sg, rows)], "adamw_small")
    small = {name: [] for name in small_names}
    for o in outs:
        for name, t in zip(small_names, _unpack(o, shard_shapes)):
            small[name].append(t)

    order = ("norm1_g", "w_in", "dn_conv_w", "dn_a_log", "dn_dt_bias", "dn_norm_g", "sc_conv_w", "sc_norm_g", "w_out",
             "norm2_g", "ffn_w_gate", "ffn_w_up", "ffn_w_down", "final_norm_g")
    result = {**big, **small}
    return (loss, dact[None], *[result[n][0] for n in order], *[result[n][1] for n in order],
            *[result[n][2] for n in order], *[result[n][3] for n in order])
```

```python
import jax
import jax.numpy as jnp
from jax import lax
from jax.experimental import pallas as pl
from jax.experimental.pallas import tpu as pltpu

F32 = jnp.float32
BF16 = jnp.bfloat16
MESH = pl.DeviceIdType.MESH

D_MODEL = 1024
DEPTH = 4
HEADS = 4
HEAD_DIM = 128
DN_WIDTH = HEADS * HEAD_DIM
SC_WIDTH = 512
SC_GROUPS = 4
D_FF = 2816
CHUNK = 64
QKV = 3 * DN_WIDTH
W_IN_COLS = 4 * DN_WIDTH + 2 * HEADS + 3 * SC_WIDTH
WA_COLS = QKV + DN_WIDTH + 3 * SC_WIDTH
LANES = 128
EPS = 1e-6
Q_SCALE = HEAD_DIM ** -0.5
N_CHIPS = 4
N_DEV = 8

ADAM_LR = 0.001
ADAM_B1 = 0.9
ADAM_B2 = 0.999
ADAM_EPS = 1e-08
ADAM_WD = 0.01
ADAM_STEP = 10

VMEM_LIMIT = 56 * 1024 * 1024

NN = (((1,), (0,)), ((), ()))
NT = (((1,), (1,)), ((), ()))
TN = (((0,), (0,)), ((), ()))


def _mm(a, b, dims=NN):
    return lax.dot_general(a.astype(BF16), b.astype(BF16), dims, preferred_element_type=F32)


def _mm32(a, b, dims=NN):
    return lax.dot_general(a, b, dims, preferred_element_type=F32, precision=lax.Precision.HIGHEST)


def _params(sem, vmem=VMEM_LIMIT):
    return pltpu.CompilerParams(dimension_semantics=sem, vmem_limit_bytes=vmem)


def _sigmoid(x):
    return 1.0 / (1.0 + jnp.exp(-x))


def _softplus(x):
    return jnp.maximum(x, 0.0) + jnp.log1p(jnp.exp(-jnp.abs(x)))


def _row_acc(acc_ref, val):
    acc_ref[0:1, :] += jnp.sum(val, axis=0, keepdims=True)


def _rms_bwd(dh, xh, r, gain):
    dxh = dh * gain
    return r * (dxh - xh * jnp.mean(dxh * xh, axis=-1, keepdims=True))


def _before_halo(tb):
    return lambda i: (jnp.maximum(i * (tb // 8) - 1, 0), 0)


def _after_halo(tb, n_rows):
    last = n_rows // 8 - 1
    return lambda i: (jnp.minimum((i + 1) * (tb // 8), last), 0)


def _taps(xc, w, n_taps, tb, first):
    out = w[0:1, :] * xc[first:first + tb, :]
    for j in range(1, n_taps):
        out = out + w[j:j + 1, :] * xc[first + j:first + j + tb, :]
    return out


def _in_proj(x, g1, wa, wbd):
    T = x.shape[0]
    tb = 256

    def body(x_ref, g_ref, wa_ref, wbd_ref, qkv_ref, z_ref, sc_ref, bd_ref, ht_ref):
        xv = x_ref[...]
        r = lax.rsqrt(jnp.mean(xv * xv, axis=-1, keepdims=True) + EPS)
        h = (xv * r * g_ref[...]).astype(BF16)
        p = jnp.dot(h, wa_ref[...], preferred_element_type=F32)
        qkv_ref[...] = p[:, :QKV]
        z_ref[...] = p[:, QKV:QKV + DN_WIDTH]
        sc_ref[...] = p[:, QKV + DN_WIDTH:]
        bd_ref[...] = jnp.dot(h, wbd_ref[...], preferred_element_type=F32)
        ht_ref[...] = h.T

    tok = lambda w: pl.BlockSpec((tb, w), lambda i: (i, 0))
    full = lambda a: pl.BlockSpec(a.shape, lambda i: (0, 0))
    return pl.pallas_call(
        body, name="in_proj", grid=(T // tb,),
        in_specs=[tok(D_MODEL), full(g1), full(wa), full(wbd)],
        out_specs=[tok(QKV), tok(DN_WIDTH), tok(3 * SC_WIDTH), tok(LANES),
                   pl.BlockSpec((D_MODEL, tb), lambda i: (0, i))],
        out_shape=[jax.ShapeDtypeStruct((T, QKV), F32), jax.ShapeDtypeStruct((T, DN_WIDTH), F32),
                   jax.ShapeDtypeStruct((T, 3 * SC_WIDTH), F32), jax.ShapeDtypeStruct((T, LANES), F32),
                   jax.ShapeDtypeStruct((D_MODEL, T), BF16)],
        compiler_params=_params(("parallel",)),
    )(x, g1, wa, wbd)


def _dn_act(pre, halo, cw, tb):
    xc = jnp.concatenate([halo, pre], axis=0)
    c = _taps(xc, cw, 4, tb, 5)
    sg = _sigmoid(c)
    return xc, c, sg, c * sg


def _gates(bd, al_row, dt_row):
    lane = lax.broadcasted_iota(jnp.int32, bd.shape, 1)
    beta = _sigmoid(bd)
    g = -jnp.exp(al_row) * _softplus(bd + dt_row)
    return jnp.where(lane < HEADS, beta, jnp.where(lane < 2 * HEADS, g, 0.0))


def _dn_prep(qkv, cw, bd, al_row, dt_row):
    T = qkv.shape[0]
    tb = 512

    def body(pre_ref, halo_ref, cw_ref, bd_ref, al_ref, dt_ref, q_ref, k_ref, v_ref, bg_ref):
        halo = jnp.where(pl.program_id(0) > 0, halo_ref[...], 0.0)
        _, _, _, a = _dn_act(pre_ref[...], halo, cw_ref[...], tb)
        for hh in range(HEADS):
            sl = slice(HEAD_DIM * hh, HEAD_DIM * (hh + 1))
            qs = a[:, sl]
            q_ref[:, sl] = qs * (lax.rsqrt(jnp.sum(qs * qs, axis=-1, keepdims=True) + EPS) * Q_SCALE)
            ks = a[:, DN_WIDTH + HEAD_DIM * hh:DN_WIDTH + HEAD_DIM * (hh + 1)]
            k_ref[:, sl] = ks * lax.rsqrt(jnp.sum(ks * ks, axis=-1, keepdims=True) + EPS)
        v_ref[...] = a[:, 2 * DN_WIDTH:]
        gates = _gates(bd_ref[...], al_ref[...], dt_ref[...])
        lane = lax.broadcasted_iota(jnp.int32, gates.shape, 1)
        bg_ref[...] = jnp.where(lane < HEADS, gates, _mm32(_chunk_cumsum_matrix(tb), gates))

    tok = lambda w: pl.BlockSpec((tb, w), lambda i: (i, 0))
    full = lambda a: pl.BlockSpec(a.shape, lambda i: (0, 0))
    return pl.pallas_call(
        body, name="dn_prep", grid=(T // tb,),
        in_specs=[tok(QKV), pl.BlockSpec((8, QKV), _before_halo(tb)), full(cw), tok(LANES), full(al_row), full(dt_row)],
        out_specs=[tok(DN_WIDTH), tok(DN_WIDTH), tok(DN_WIDTH), tok(LANES)],
        out_shape=[jax.ShapeDtypeStruct((T, DN_WIDTH), F32)] * 3 + [jax.ShapeDtypeStruct((T, LANES), F32)],
        compiler_params=_params(("parallel",)),
    )(qkv, qkv, cw, bd, al_row, dt_row)


def _chunk_masks():
    row = lax.broadcasted_iota(jnp.int32, (CHUNK, CHUNK), 0)
    col = lax.broadcasted_iota(jnp.int32, (CHUNK, CHUNK), 1)
    return row >= col, row > col


def _chunk_cumsum_matrix(n):
    row = lax.broadcasted_iota(jnp.int32, (n, n), 0)
    col = lax.broadcasted_iota(jnp.int32, (n, n), 1)
    return jnp.logical_and(row >= col, row // CHUNK == col // CHUNK).astype(F32)


def _chunk_units(q_ref, k_ref, v_ref, bg_ref, rows):
    bgc = bg_ref[rows, :]
    bg_t = bgc.T
    qv, kv, vv = q_ref[rows, :], k_ref[rows, :], v_ref[rows, :]
    units = []
    for h in range(HEADS):
        sl = slice(HEAD_DIM * h, HEAD_DIM * (h + 1))
        units.append((qv[:, sl], kv[:, sl], vv[:, sl], bgc[:, h:h + 1], bgc[:, HEADS + h:HEADS + h + 1],
                      bg_t[HEADS + h:HEADS + h + 1, :]))
    return units


def _units_local(units, masks):
    causal, strict = masks
    pre = []
    for q, k, v, beta, gc, gr in units:
        kb = k * beta
        eg = jnp.exp(gc)
        g_last = gc[CHUNK - 1:CHUNK, :]
        ek = jnp.exp(g_last - gc)
        pre.append(dict(q=q, k=k, v=v, beta=beta, decay=jnp.exp(jnp.where(causal, gc - gr, -1e30)), kb=kb, vb=v * beta,
                        eg=eg, kbg=kb * eg, ek=ek, gl=jnp.exp(g_last), q_dec=q * eg, k_dec=k * ek))
    both = [_mm(jnp.concatenate([p["kb"], p["q"]], axis=0), p["k"], NT) for p in pre]
    for p, b in zip(pre, both):
        p["low"] = jnp.where(strict, b[:CHUNK] * p["decay"], 0.0)
        p["qk"] = jnp.where(causal, b[CHUNK:] * p["decay"], 0.0)
    xs = [-p["low"] for p in pre]
    pw = [_mm(p["low"], p["low"]) for p in pre]
    for _ in range(4):
        both = [_mm(jnp.concatenate([pp, x], axis=0), pp) for pp, x in zip(pw, xs)]
        xs = [x + pp + b[CHUNK:] for x, pp, b in zip(xs, pw, both)]
        pw = [b[:CHUNK] for b in both]
    last = [_mm(x, pp) for x, pp in zip(xs, pw)]
    xs = [x + pp + b for x, pp, b in zip(xs, pw, last)]
    uw = [_mm(x, jnp.concatenate([p["vb"], p["kbg"]], axis=1)) for x, p in zip(xs, pre)]
    for p, x, b in zip(pre, xs, uw):
        p["xm"] = x
        p["u"] = p["vb"] + b[:, :HEAD_DIM]
        p["w"] = p["kbg"] + b[:, HEAD_DIM:]
    return pre


def _delta_fwd(q, k, v, bg):
    T = q.shape[0]
    tb = 512
    n_chunk = tb // CHUNK

    def body(q_ref, k_ref, v_ref, bg_ref, o_ref, st_ref, s_ref):
        @pl.when(pl.program_id(0) == 0)
        def _():
            s_ref[...] = jnp.zeros_like(s_ref)

        masks = _chunk_masks()

        def pair(pi, carry):
            rows = [pl.ds(pl.multiple_of((2 * pi + j) * CHUNK, CHUNK), CHUNK) for j in range(2)]
            loc = _units_local(_chunk_units(q_ref, k_ref, v_ref, bg_ref, rows[0])
                               + _chunk_units(q_ref, k_ref, v_ref, bg_ref, rows[1]), masks)
            states = [s_ref[h] for h in range(HEADS)]
            for j in range(2):
                lj = loc[HEADS * j:HEADS * (j + 1)]
                ws = [_mm(jnp.concatenate([p["w"], p["q_dec"]], axis=0), s) for p, s in zip(lj, states)]
                v_new = [p["u"] - b[:CHUNK] for p, b in zip(lj, ws)]
                intra = [_mm(p["qk"], vn) for p, vn in zip(lj, v_new)]
                upd = [_mm(p["k_dec"], vn, TN) for p, vn in zip(lj, v_new)]
                o_ref[rows[j], :] = jnp.concatenate([b[CHUNK:] + a for b, a in zip(ws, intra)], axis=1)
                for h in range(HEADS):
                    st_ref[2 * pi + j, h] = states[h]
                states = [p["gl"] * s + d for p, s, d in zip(lj, states, upd)]
            for h in range(HEADS):
                s_ref[h] = states[h]
            return carry

        lax.fori_loop(0, n_chunk // 2, pair, 0)

    tok = lambda w: pl.BlockSpec((tb, w), lambda i: (i, 0))
    return pl.pallas_call(
        body, name="delta_fwd", grid=(T // tb,),
        in_specs=[tok(DN_WIDTH), tok(DN_WIDTH), tok(DN_WIDTH), tok(LANES)],
        out_specs=[tok(DN_WIDTH), pl.BlockSpec((n_chunk, HEADS, HEAD_DIM, HEAD_DIM), lambda i: (i, 0, 0, 0))],
        out_shape=[jax.ShapeDtypeStruct((T, DN_WIDTH), F32),
                   jax.ShapeDtypeStruct((T // CHUNK, HEADS, HEAD_DIM, HEAD_DIM), F32)],
        scratch_shapes=[pltpu.VMEM((HEADS, HEAD_DIM, HEAD_DIM), F32)],
        compiler_params=_params(("arbitrary",)),
    )(q, k, v, bg)


def _dn_out(o, z, gn):
    outs, ohs, rs = [], [], []
    for hh in range(HEADS):
        oh = o[:, HEAD_DIM * hh:HEAD_DIM * (hh + 1)]
        r = lax.rsqrt(jnp.mean(oh * oh, axis=-1, keepdims=True) + EPS)
        ohs.append(oh * r)
        rs.append(r)
    sz = _sigmoid(z)
    oh = jnp.concatenate(ohs, axis=1)
    gn4 = jnp.concatenate([gn] * HEADS, axis=1)
    return oh * gn4 * (z * sz), oh, rs, sz, gn4


def _sc_fwd(sc_in, halo, cw, tb):
    xc = jnp.concatenate([halo, sc_in], axis=0)
    u = xc[:, SC_WIDTH:2 * SC_WIDTH] * xc[:, 2 * SC_WIDTH:]
    cv = _taps(u, cw, 3, tb, 6)
    gate_b = sc_in[:, :SC_WIDTH]
    y = gate_b * cv
    gw = SC_WIDTH // SC_GROUPS
    yhs, rs = [], []
    for gi in range(SC_GROUPS):
        yg = y[:, gw * gi:gw * (gi + 1)]
        r = lax.rsqrt(jnp.mean(yg * yg, axis=-1, keepdims=True) + EPS)
        yhs.append(yg * r)
        rs.append(r)
    return u, cv, gate_b, jnp.concatenate(yhs, axis=1), rs


def _mix_out(o, z, sc_in, x, w_out, gn, scw, gs):
    T = x.shape[0]
    tb = 256

    def body(o_ref, z_ref, sc_ref, halo_ref, x_ref, w_ref, gn_ref, scw_ref, gs_ref, x1_ref, mt_ref):
        o_n = _dn_out(o_ref[...], z_ref[...], gn_ref[...])[0]
        halo = jnp.where(pl.program_id(0) > 0, halo_ref[...], 0.0)
        yh = _sc_fwd(sc_ref[...], halo, scw_ref[...], tb)[3]
        mix = jnp.concatenate([o_n, yh * gs_ref[...]], axis=1).astype(BF16)
        x1_ref[...] = x_ref[...] + jnp.dot(mix, w_ref[...], preferred_element_type=F32)
        mt_ref[...] = mix.T

    tok = lambda w: pl.BlockSpec((tb, w), lambda i: (i, 0))
    full = lambda a: pl.BlockSpec(a.shape, lambda i: (0, 0))
    return pl.pallas_call(
        body, name="mix_out", grid=(T // tb,),
        in_specs=[tok(DN_WIDTH), tok(DN_WIDTH), tok(3 * SC_WIDTH), pl.BlockSpec((8, 3 * SC_WIDTH), _before_halo(tb)),
                  tok(D_MODEL), full(w_out), full(gn), full(scw), full(gs)],
        out_specs=[tok(D_MODEL), pl.BlockSpec((D_MODEL, tb), lambda i: (0, i))],
        out_shape=[jax.ShapeDtypeStruct((T, D_MODEL), F32), jax.ShapeDtypeStruct((D_MODEL, T), BF16)],
        compiler_params=_params(("parallel",)),
    )(o, z, sc_in, sc_in, x, w_out, gn, scw, gs)


def _ffn(x1, g2, wg, wu, wd):
    T = x1.shape[0]
    tb = 256

    def body(x_ref, g_ref, wg_ref, wu_ref, wd_ref, x2_ref, a_ref, b_ref, ht_ref):
        xv = x_ref[...]
        r = lax.rsqrt(jnp.mean(xv * xv, axis=-1, keepdims=True) + EPS)
        h = (xv * r * g_ref[...]).astype(BF16)
        a = jnp.dot(h, wg_ref[...], preferred_element_type=F32)
        b = jnp.dot(h, wu_ref[...], preferred_element_type=F32)
        act = (a * _sigmoid(a) * b).astype(BF16)
        x2_ref[...] = xv + jnp.dot(act, wd_ref[...], preferred_element_type=F32)
        a_ref[...] = a.astype(BF16)
        b_ref[...] = b.astype(BF16)
        ht_ref[...] = h.T

    tok = lambda w: pl.BlockSpec((tb, w), lambda i: (i, 0))
    full = lambda a: pl.BlockSpec(a.shape, lambda i: (0, 0))
    return pl.pallas_call(
        body, name="ffn", grid=(T // tb,),
        in_specs=[tok(D_MODEL), full(g2), full(wg), full(wu), full(wd)],
        out_specs=[tok(D_MODEL), tok(D_FF), tok(D_FF), pl.BlockSpec((D_MODEL, tb), lambda i: (0, i))],
        out_shape=[jax.ShapeDtypeStruct((T, D_MODEL), F32), jax.ShapeDtypeStruct((T, D_FF), BF16),
                   jax.ShapeDtypeStruct((T, D_FF), BF16), jax.ShapeDtypeStruct((D_MODEL, T), BF16)],
        compiler_params=_params(("parallel",)),
    )(x1, g2, wg, wu, wd)


def _loss_head(x, gf, target):
    T = x.shape[0]
    tb = 512

    def body(x_ref, g_ref, t_ref, dx_ref, loss_ref, dg_ref):
        @pl.when(pl.program_id(0) == 0)
        def _():
            loss_ref[...] = jnp.zeros_like(loss_ref)
            dg_ref[...] = jnp.zeros_like(dg_ref)

        xv = x_ref[...]
        r = lax.rsqrt(jnp.mean(xv * xv, axis=-1, keepdims=True) + EPS)
        xh = xv * r
        err = xh * g_ref[...] - t_ref[...]
        per_tok = jnp.mean(err * err, axis=-1, keepdims=True)
        loss_ref[...] += 0.5 * jnp.sum(per_tok, axis=0, keepdims=True)
        dy = err * (1.0 / D_MODEL)
        _row_acc(dg_ref, dy * xh)
        dx_ref[...] = _rms_bwd(dy, xh, r, g_ref[...])

    tok = pl.BlockSpec((tb, D_MODEL), lambda i: (i, 0))
    return pl.pallas_call(
        body, name="loss_head", grid=(T // tb,),
        in_specs=[tok, pl.BlockSpec(gf.shape, lambda i: (0, 0)), tok],
        out_specs=[tok, pl.BlockSpec((8, LANES), lambda i: (0, 0)), pl.BlockSpec((8, D_MODEL), lambda i: (0, 0))],
        out_shape=[jax.ShapeDtypeStruct((T, D_MODEL), F32), jax.ShapeDtypeStruct((8, LANES), F32),
                   jax.ShapeDtypeStruct((8, D_MODEL), F32)],
        compiler_params=_params(("arbitrary",)),
    )(x, gf, target)


def _ffn_bwd(dx2, x1, a, b, g2, wg, wu, wd):
    T = x1.shape[0]
    tb = 256

    def body(dx2_ref, x_ref, a_ref, b_ref, g_ref, wg_ref, wu_ref, wd_ref, dx1_ref, da_ref, db_ref, at_ref, dg_ref):
        @pl.when(pl.program_id(0) == 0)
        def _():
            dg_ref[...] = jnp.zeros_like(dg_ref)

        dx2v = dx2_ref[...]
        av = a_ref[...].astype(F32)
        bv = b_ref[...].astype(F32)
        dact = _mm(dx2v, wd_ref[...], NT)
        sa = _sigmoid(av)
        silu = av * sa
        da = (dact * bv * (sa * (1.0 + av * (1.0 - sa)))).astype(BF16)
        db = (dact * silu).astype(BF16)
        dh = _mm(da, wg_ref[...], NT) + _mm(db, wu_ref[...], NT)
        xv = x_ref[...]
        r = lax.rsqrt(jnp.mean(xv * xv, axis=-1, keepdims=True) + EPS)
        xh = xv * r
        _row_acc(dg_ref, dh * xh)
        dx1_ref[...] = dx2v + _rms_bwd(dh, xh, r, g_ref[...])
        da_ref[...] = da
        db_ref[...] = db
        at_ref[...] = (silu * bv).astype(BF16).T

    tok = lambda w: pl.BlockSpec((tb, w), lambda i: (i, 0))
    full = lambda t: pl.BlockSpec(t.shape, lambda i: (0, 0))
    return pl.pallas_call(
        body, name="ffn_bwd", grid=(T // tb,),
        in_specs=[tok(D_MODEL), tok(D_MODEL), tok(D_FF), tok(D_FF), full(g2), full(wg), full(wu), full(wd)],
        out_specs=[tok(D_MODEL), tok(D_FF), tok(D_FF), pl.BlockSpec((D_FF, tb), lambda i: (0, i)),
                   pl.BlockSpec((8, D_MODEL), lambda i: (0, 0))],
        out_shape=[jax.ShapeDtypeStruct((T, D_MODEL), F32), jax.ShapeDtypeStruct((T, D_FF), BF16),
                   jax.ShapeDtypeStruct((T, D_FF), BF16), jax.ShapeDtypeStruct((D_FF, T), BF16),
                   jax.ShapeDtypeStruct((8, D_MODEL), F32)],
        compiler_params=_params(("arbitrary",)),
    )(dx2, x1, a, b, g2, wg, wu, wd)


def _wgrad(at, b, bm, bn, name):
    M, T = at.shape
    N = b.shape[1]
    bk = min(T, 1024)

    def body(a_ref, b_ref, o_ref):
        @pl.when(pl.program_id(2) == 0)
        def _():
            o_ref[...] = jnp.zeros_like(o_ref)

        o_ref[...] += jnp.dot(a_ref[...], b_ref[...], preferred_element_type=F32)

    return pl.pallas_call(
        body, name=name, grid=(M // bm, N // bn, T // bk),
        in_specs=[pl.BlockSpec((bm, bk), lambda i, j, kk: (i, kk)), pl.BlockSpec((bk, bn), lambda i, j, kk: (kk, j))],
        out_specs=pl.BlockSpec((bm, bn), lambda i, j, kk: (i, j)),
        out_shape=jax.ShapeDtypeStruct((M, N), F32),
        compiler_params=_params(("parallel", "parallel", "arbitrary")),
    )(at, b)


def _mix_out_bwd(dx1, o, z, sc_in, w_out, gn, scw, gs):
    T = dx1.shape[0]
    tb = 256

    def body(dx_ref, o_ref, z_ref, sc_ref, halo_ref, w_ref, gn_ref, scw_ref, gs_ref,
             do_ref, dz_ref, dgb_ref, dcv_ref, dgn_ref, dgs_ref, dscw_ref):
        @pl.when(pl.program_id(0) == 0)
        def _():
            dgn_ref[...] = jnp.zeros_like(dgn_ref)
            dgs_ref[...] = jnp.zeros_like(dgs_ref)
            dscw_ref[...] = jnp.zeros_like(dscw_ref)

        dmix = _mm(dx_ref[...], w_ref[...], NT)
        don = dmix[:, :DN_WIDTH]
        dosc = dmix[:, DN_WIDTH:]
        zv = z_ref[...]
        _, oh, rs, sz, gn4 = _dn_out(o_ref[...], zv, gn_ref[...])
        silu_z = zv * sz
        dgn_full = don * oh * silu_z
        dgn_ref[0:1, :] += jnp.sum(sum(dgn_full[:, HEAD_DIM * hh:HEAD_DIM * (hh + 1)] for hh in range(HEADS)),
                                   axis=0, keepdims=True)
        dz_ref[...] = (don * oh * gn4 * (sz * (1.0 + zv * (1.0 - sz)))).astype(BF16)
        t = don * gn4 * silu_z
        for hh in range(HEADS):
            sl = slice(HEAD_DIM * hh, HEAD_DIM * (hh + 1))
            th, ohh = t[:, sl], oh[:, sl]
            do_ref[:, sl] = rs[hh] * (th - ohh * jnp.mean(th * ohh, axis=-1, keepdims=True))
        halo = jnp.where(pl.program_id(0) > 0, halo_ref[...], 0.0)
        u, cv, gate_b, yh, rys = _sc_fwd(sc_ref[...], halo, scw_ref[...], tb)
        _row_acc(dgs_ref, dosc * yh)
        ty = dosc * gs_ref[...]
        gw = SC_WIDTH // SC_GROUPS
        dys = []
        for gi in range(SC_GROUPS):
            sl = slice(gw * gi, gw * (gi + 1))
            tg, yg = ty[:, sl], yh[:, sl]
            dys.append(rys[gi] * (tg - yg * jnp.mean(tg * yg, axis=-1, keepdims=True)))
        dy = jnp.concatenate(dys, axis=1)
        dgb_ref[...] = dy * cv
        dcv = dy * gate_b
        dcv_ref[...] = dcv
        for j in range(3):
            dscw_ref[j:j + 1, :] += jnp.sum(dcv * u[6 + j:6 + j + tb, :], axis=0, keepdims=True)

    tok = lambda w: pl.BlockSpec((tb, w), lambda i: (i, 0))
    full = lambda t: pl.BlockSpec(t.shape, lambda i: (0, 0))
    acc = lambda w: pl.BlockSpec((8, w), lambda i: (0, 0))
    return pl.pallas_call(
        body, name="mix_out_bwd", grid=(T // tb,),
        in_specs=[tok(D_MODEL), tok(DN_WIDTH), tok(DN_WIDTH), tok(3 * SC_WIDTH),
                  pl.BlockSpec((8, 3 * SC_WIDTH), _before_halo(tb)), full(w_out), full(gn), full(scw), full(gs)],
        out_specs=[tok(DN_WIDTH), tok(DN_WIDTH), tok(SC_WIDTH), tok(SC_WIDTH), acc(HEAD_DIM), acc(SC_WIDTH), acc(SC_WIDTH)],
        out_shape=[jax.ShapeDtypeStruct((T, DN_WIDTH), F32), jax.ShapeDtypeStruct((T, DN_WIDTH), BF16),
                   jax.ShapeDtypeStruct((T, SC_WIDTH), F32), jax.ShapeDtypeStruct((T, SC_WIDTH), F32),
                   jax.ShapeDtypeStruct((8, HEAD_DIM), F32), jax.ShapeDtypeStruct((8, SC_WIDTH), F32),
                   jax.ShapeDtypeStruct((8, SC_WIDTH), F32)],
        compiler_params=_params(("arbitrary",)),
    )(dx1, o, z, sc_in, sc_in, w_out, gn, scw, gs)


def _sc_conv_bwd(dcv, dgb, sc_in, scw):
    T = dcv.shape[0]
    tb = 512

    def body(dcv_ref, halo_ref, dgb_ref, sc_ref, w_ref, out_ref):
        last = pl.program_id(0) == pl.num_programs(0) - 1
        halo = jnp.where(last, 0.0, halo_ref[...])
        xc = jnp.concatenate([dcv_ref[...], halo], axis=0)
        w = w_ref[...]
        du = w[2:3, :] * xc[0:tb, :] + w[1:2, :] * xc[1:tb + 1, :] + w[0:1, :] * xc[2:tb + 2, :]
        sc = sc_ref[...]
        out_ref[:, :SC_WIDTH] = dgb_ref[...].astype(BF16)
        out_ref[:, SC_WIDTH:2 * SC_WIDTH] = (du * sc[:, 2 * SC_WIDTH:]).astype(BF16)
        out_ref[:, 2 * SC_WIDTH:] = (du * sc[:, SC_WIDTH:2 * SC_WIDTH]).astype(BF16)

    tok = lambda w: pl.BlockSpec((tb, w), lambda i: (i, 0))
    return pl.pallas_call(
        body, name="sc_conv_bwd", grid=(T // tb,),
        in_specs=[tok(SC_WIDTH), pl.BlockSpec((8, SC_WIDTH), _after_halo(tb, T)), tok(SC_WIDTH), tok(3 * SC_WIDTH),
                  pl.BlockSpec(scw.shape, lambda i: (0, 0))],
        out_specs=tok(3 * SC_WIDTH),
        out_shape=jax.ShapeDtypeStruct((T, 3 * SC_WIDTH), BF16),
        compiler_params=_params(("parallel",)),
    )(dcv, dcv, dgb, sc_in, scw)


def _delta_bwd(q, k, v, bg, states, do):
    T = q.shape[0]
    tb = 512
    n_chunk = tb // CHUNK
    nb = T // tb

    def body(q_ref, k_ref, v_ref, bg_ref, st_ref, do_ref, dq_ref, dk_ref, dv_ref, dbg_ref, ds_ref):
        @pl.when(pl.program_id(0) == 0)
        def _():
            ds_ref[...] = jnp.zeros_like(ds_ref)

        masks = _chunk_masks()
        causal, strict = masks
        lane = lax.broadcasted_iota(jnp.int32, (CHUNK, LANES), 1)
        last_row = lax.broadcasted_iota(jnp.int32, (CHUNK, 1), 0) == CHUNK - 1
        cat = jnp.concatenate
        heads = range(HEADS)

        def chunk(cj, carry):
            ci = n_chunk - 1 - cj
            rows = pl.ds(pl.multiple_of(ci * CHUNK, CHUNK), CHUNK)
            loc = _units_local(_chunk_units(q_ref, k_ref, v_ref, bg_ref, rows), masks)
            dov = do_ref[rows, :]
            do = [dov[:, HEAD_DIM * h:HEAD_DIM * (h + 1)] for h in heads]
            state = [st_ref[ci, h] for h in heads]
            ds_next = [ds_ref[h] for h in heads]
            w_s = [_mm(p["w"], s) for p, s in zip(loc, state)]
            dq_dec = [_mm(d, s, NT) for d, s in zip(do, state)]
            qk_do = [_mm(p["qk"], d, TN) for p, d in zip(loc, do)]
            kd_ds = [_mm(p["k_dec"], d) for p, d in zip(loc, ds_next)]
            qd_do = [_mm(p["q_dec"], d, TN) for p, d in zip(loc, do)]
            v_new = [p["u"] - t for p, t in zip(loc, w_s)]
            dv_new = [a + b for a, b in zip(qk_do, kd_ds)]
            dqk = [jnp.where(causal, _mm(d, vn, NT), 0.0) for d, vn in zip(do, v_new)]
            dk_dec = [_mm(vn, d, NT) for vn, d in zip(v_new, ds_next)]
            w_dv = [_mm(p["w"], dvn, TN) for p, dvn in zip(loc, dv_new)]
            dw = [-_mm(dvn, s, NT) for dvn, s in zip(dv_new, state)]
            for h in heads:
                ds_ref[h] = loc[h]["gl"] * ds_next[h] + qd_do[h] - w_dv[h]
            dtm = [_mm(cat([dvn, d], axis=1), cat([p["vb"], p["kbg"]], axis=1), NT) for dvn, d, p in zip(dv_new, dw, loc)]
            x_t = [_mm(p["xm"], cat([dvn, d], axis=1), TN) for p, dvn, d in zip(loc, dv_new, dw)]
            dvb = [dvn + t[:, :HEAD_DIM] for dvn, t in zip(dv_new, x_t)]
            dkbg = [d + t[:, HEAD_DIM:] for d, t in zip(dw, x_t)]
            y = [t + _mm(p["xm"], t, TN) for p, t in zip(loc, dtm)]
            dlow = [jnp.where(strict, -(t + _mm(t, p["xm"], NT)), 0.0) for p, t in zip(loc, y)]
            dmm = [d * p["decay"] for d, p in zip(dlow, loc)]
            dnn = [d * p["decay"] for d, p in zip(dqk, loc)]
            by_k = [_mm(cat([a, b], axis=0), p["k"]) for a, b, p in zip(dmm, dnn, loc)]
            dk_mm = [_mm(cat([a, b], axis=0), cat([p["kb"], p["q"]], axis=0), TN) for a, b, p in zip(dmm, dnn, loc)]
            dq_out, dk_out, dv_out = [], [], []
            dbeta_all = jnp.zeros((CHUNK, LANES), F32)
            dgc_all = jnp.zeros((CHUNK, LANES), F32)
            for h in heads:
                p = loc[h]
                dkb = by_k[h][:CHUNK] + dkbg[h] * p["eg"]
                dq_out.append(by_k[h][CHUNK:] + dq_dec[h] * p["eg"])
                dk_out.append(dk_mm[h] + dk_dec[h] * p["ek"] + dkb * p["beta"])
                dv_out.append(dvb[h] * p["beta"])
                dbeta = jnp.sum(dkb * p["k"] + dvb[h] * p["v"], axis=1, keepdims=True)
                e = dlow[h] * p["low"] + dqk[h] * p["qk"]
                kd = jnp.sum(dk_dec[h] * p["k_dec"], axis=1, keepdims=True)
                dgc = (jnp.sum(e, axis=1, keepdims=True) - jnp.sum(e.T, axis=1, keepdims=True)
                       + jnp.sum(dq_dec[h] * p["q_dec"], axis=1, keepdims=True) - kd
                       + jnp.sum(dkbg[h] * p["kbg"], axis=1, keepdims=True))
                dgl = jnp.sum(jnp.sum(ds_next[h] * state[h], axis=1, keepdims=True), axis=0, keepdims=True)
                d_last = jnp.sum(kd, axis=0, keepdims=True) + dgl * p["gl"]
                dgc = dgc + jnp.where(last_row, d_last, 0.0)
                dbeta_all = jnp.where(lane == h, dbeta, dbeta_all)
                dgc_all = jnp.where(lane == h + HEADS, dgc, dgc_all)
            dq_ref[rows, :] = cat(dq_out, axis=1)
            dk_ref[rows, :] = cat(dk_out, axis=1)
            dv_ref[rows, :] = cat(dv_out, axis=1)
            dbg_ref[rows, :] = dbeta_all + dgc_all
            return carry

        lax.fori_loop(0, n_chunk, chunk, 0)

    tok = lambda w: pl.BlockSpec((tb, w), lambda i: (nb - 1 - i, 0))
    return pl.pallas_call(
        body, name="delta_bwd", grid=(nb,),
        in_specs=[tok(DN_WIDTH), tok(DN_WIDTH), tok(DN_WIDTH), tok(LANES),
                  pl.BlockSpec((n_chunk, HEADS, HEAD_DIM, HEAD_DIM), lambda i: (nb - 1 - i, 0, 0, 0)), tok(DN_WIDTH)],
        out_specs=[tok(DN_WIDTH), tok(DN_WIDTH), tok(DN_WIDTH), tok(LANES)],
        out_shape=[jax.ShapeDtypeStruct((T, DN_WIDTH), F32)] * 3 + [jax.ShapeDtypeStruct((T, LANES), F32)],
        scratch_shapes=[pltpu.VMEM((HEADS, HEAD_DIM, HEAD_DIM), F32)],
        compiler_params=_params(("arbitrary",)),
    )(q, k, v, bg, states, do)


def _dn_prep_bwd(dq, dk, dv, dbg, qkv, cw, bd, al_row, dt_row):
    T = qkv.shape[0]
    tb = 256

    def body(dq_ref, dk_ref, dv_ref, dbg_ref, pre_ref, halo_ref, cw_ref, bd_ref, al_ref, dt_ref,
             dc_ref, dbd_ref, dcw_ref, dal_ref, ddt_ref):
        @pl.when(pl.program_id(0) == 0)
        def _():
            dcw_ref[...] = jnp.zeros_like(dcw_ref)
            dal_ref[...] = jnp.zeros_like(dal_ref)
            ddt_ref[...] = jnp.zeros_like(ddt_ref)

        halo = jnp.where(pl.program_id(0) > 0, halo_ref[...], 0.0)
        xc, c, sg, a = _dn_act(pre_ref[...], halo, cw_ref[...], tb)
        dsilu = sg * (1.0 + c * (1.0 - sg))
        for hh in range(HEADS):
            sl = slice(HEAD_DIM * hh, HEAD_DIM * (hh + 1))
            for base, g_ref, scale in ((0, dq_ref, Q_SCALE), (DN_WIDTH, dk_ref, 1.0)):
                sa = slice(base + HEAD_DIM * hh, base + HEAD_DIM * (hh + 1))
                raw = a[:, sa]
                r = lax.rsqrt(jnp.sum(raw * raw, axis=-1, keepdims=True) + EPS)
                nrm = raw * r
                gn_ = g_ref[:, sl] * scale
                dc_ref[:, sa] = r * (gn_ - nrm * jnp.sum(gn_ * nrm, axis=-1, keepdims=True)) * dsilu[:, sa]
        dc_ref[:, 2 * DN_WIDTH:] = dv_ref[...] * dsilu[:, 2 * DN_WIDTH:]
        dc = dc_ref[...]
        for j in range(4):
            dcw_ref[j:j + 1, :] += jnp.sum(dc * xc[5 + j:5 + j + tb, :], axis=0, keepdims=True)
        bdv = bd_ref[...]
        lane = lax.broadcasted_iota(jnp.int32, bdv.shape, 1)
        is_b = lane < HEADS
        dbg_in = dbg_ref[...]
        dbgv = jnp.where(is_b, dbg_in, _mm32(_chunk_cumsum_matrix(tb), dbg_in, TN))
        is_g = jnp.logical_and(lane >= HEADS, lane < 2 * HEADS)
        beta = _sigmoid(bdv)
        neg_a = -jnp.exp(al_ref[...])
        pre_sp = bdv + dt_ref[...]
        g = neg_a * _softplus(pre_sp)
        da_in = dbgv * neg_a * _sigmoid(pre_sp)
        dbd_ref[...] = jnp.where(is_b, dbgv * beta * (1.0 - beta), jnp.where(is_g, da_in, 0.0)).astype(BF16)
        _row_acc(dal_ref, jnp.where(is_g, dbgv * g, 0.0))
        _row_acc(ddt_ref, jnp.where(is_g, da_in, 0.0))

    tok = lambda w: pl.BlockSpec((tb, w), lambda i: (i, 0))
    full = lambda t: pl.BlockSpec(t.shape, lambda i: (0, 0))
    acc = lambda w: pl.BlockSpec((8, w), lambda i: (0, 0))
    return pl.pallas_call(
        body, name="dn_prep_bwd", grid=(T // tb,),
        in_specs=[tok(DN_WIDTH), tok(DN_WIDTH), tok(DN_WIDTH), tok(LANES),
                  tok(QKV), pl.BlockSpec((8, QKV), _before_halo(tb)), full(cw), tok(LANES), full(al_row), full(dt_row)],
        out_specs=[tok(QKV), tok(LANES), acc(QKV), acc(LANES), acc(LANES)],
        out_shape=[jax.ShapeDtypeStruct((T, QKV), F32), jax.ShapeDtypeStruct((T, LANES), BF16),
                   jax.ShapeDtypeStruct((8, QKV), F32), jax.ShapeDtypeStruct((8, LANES), F32),
                   jax.ShapeDtypeStruct((8, LANES), F32)],
        compiler_params=_params(("arbitrary",)),
    )(dq, dk, dv, dbg, qkv, qkv, cw, bd, al_row, dt_row)


def _dn_conv_bwd(dc, cw):
    T = dc.shape[0]
    tb = 512

    def body(dc_ref, halo_ref, w_ref, out_ref):
        last = pl.program_id(0) == pl.num_programs(0) - 1
        halo = jnp.where(last, 0.0, halo_ref[...])
        xc = jnp.concatenate([dc_ref[...], halo], axis=0)
        w = w_ref[...]
        acc = w[3:4, :] * xc[0:tb, :]
        for j in range(3):
            acc = acc + w[j:j + 1, :] * xc[3 - j:3 - j + tb, :]
        out_ref[...] = acc.astype(BF16)

    tok = pl.BlockSpec((tb, QKV), lambda i: (i, 0))
    return pl.pallas_call(
        body, name="dn_conv_bwd", grid=(T // tb,),
        in_specs=[tok, pl.BlockSpec((8, QKV), _after_halo(tb, T)), pl.BlockSpec(cw.shape, lambda i: (0, 0))],
        out_specs=tok,
        out_shape=jax.ShapeDtypeStruct((T, QKV), BF16),
        compiler_params=_params(("parallel",)),
    )(dc, dc, cw)


def _in_proj_bwd(dqkv, dz, dsc, dbd, dx1, x, g1, wa, wbd):
    T = x.shape[0]
    tb = 256

    def body(dqkv_ref, dz_ref, dsc_ref, dbd_ref, dx1_ref, x_ref, g_ref, wa_ref, wbd_ref, dx_ref, dg_ref):
        @pl.when(pl.program_id(0) == 0)
        def _():
            dg_ref[...] = jnp.zeros_like(dg_ref)

        dh = (_mm(dqkv_ref[...], wa_ref[:, :QKV], NT) + _mm(dz_ref[...], wa_ref[:, QKV:QKV + DN_WIDTH], NT)
              + _mm(dsc_ref[...], wa_ref[:, QKV + DN_WIDTH:], NT) + _mm(dbd_ref[...], wbd_ref[...], NT))
        xv = x_ref[...]
        r = lax.rsqrt(jnp.mean(xv * xv, axis=-1, keepdims=True) + EPS)
        xh = xv * r
        _row_acc(dg_ref, dh * xh)
        dx_ref[...] = dx1_ref[...] + _rms_bwd(dh, xh, r, g_ref[...])

    tok = lambda w: pl.BlockSpec((tb, w), lambda i: (i, 0))
    full = lambda t: pl.BlockSpec(t.shape, lambda i: (0, 0))
    return pl.pallas_call(
        body, name="in_proj_bwd", grid=(T // tb,),
        in_specs=[tok(QKV), tok(DN_WIDTH), tok(3 * SC_WIDTH), tok(LANES), tok(D_MODEL), tok(D_MODEL),
                  full(g1), full(wa), full(wbd)],
        out_specs=[tok(D_MODEL), pl.BlockSpec((8, D_MODEL), lambda i: (0, 0))],
        out_shape=[jax.ShapeDtypeStruct((T, D_MODEL), F32), jax.ShapeDtypeStruct((8, D_MODEL), F32)],
        compiler_params=_params(("arbitrary",)),
    )(dqkv, dz, dsc, dbd, dx1, x, g1, wa, wbd)


def _pad_rows(a, rows=8):
    return jnp.pad(a, ((0, rows - a.shape[0]), (0, 0)))


def _gate_rows(a_log, dt_bias):
    put = lambda t: jnp.pad(t.reshape(1, HEADS), ((0, 0), (HEADS, LANES - 2 * HEADS)))
    return put(a_log), put(dt_bias)


def _split_w_in(w_in):
    o = QKV + DN_WIDTH
    wa = jnp.concatenate([w_in[:, :o], w_in[:, o + 2 * HEADS:]], axis=1)
    wbd = jnp.pad(w_in[:, o:o + 2 * HEADS], ((0, 0), (0, LANES - 2 * HEADS)))
    return wa, wbd


def _layer_fwd(x, p):
    qkv, z, sc_in, bd, ht = _in_proj(x, p["g1"], p["wa"], p["wbd"])
    q, k, v, bg = _dn_prep(qkv, p["cw"], bd, p["al"], p["dt"])
    o, states = _delta_fwd(q, k, v, bg)
    x1, mt = _mix_out(o, z, sc_in, x, p["w_out"], p["gn"], p["scw"], p["gs"])
    x2, a, b, h2t = _ffn(x1, p["g2"], p["wg"], p["wu"], p["wd"])
    saved = dict(x=x, qkv=qkv, z=z, sc_in=sc_in, bd=bd, ht=ht, q=q, k=k, v=v, bg=bg, o=o, states=states,
                 x1=x1, mt=mt, a=a, b=b, h2t=h2t)
    return x2, saved


def _layer_bwd(dx2, s, p):
    dx1, da, db, act_t, dg2 = _ffn_bwd(dx2, s["x1"], s["a"], s["b"], p["g2"], p["wg"], p["wu"], p["wd"])
    g = {}
    g["wd"] = _wgrad(act_t, dx2.astype(BF16), 704, 1024, "wgrad_down")
    g["wg"] = _wgrad(s["h2t"], da, 512, 1408, "wgrad_gate")
    g["wu"] = _wgrad(s["h2t"], db, 512, 1408, "wgrad_up")
    do, dz, dgb, dcv, dgn, dgs, dscw = _mix_out_bwd(dx1, s["o"], s["z"], s["sc_in"], p["w_out"], p["gn"], p["scw"], p["gs"])
    g["w_out"] = _wgrad(s["mt"], dx1.astype(BF16), 512, 1024, "wgrad_out")
    dsc = _sc_conv_bwd(dcv, dgb, s["sc_in"], p["scw"])
    dq, dk, dv, dbg = _delta_bwd(s["q"], s["k"], s["v"], s["bg"], s["states"], do)
    dc, dbd, dcw, dal, ddt = _dn_prep_bwd(dq, dk, dv, dbg, s["qkv"], p["cw"], s["bd"], p["al"], p["dt"])
    dqkv = _dn_conv_bwd(dc, p["cw"])
    dx, dg1 = _in_proj_bwd(dqkv, dz, dsc, dbd, dx1, s["x"], p["g1"], p["wa"], p["wbd"])
    o = QKV + DN_WIDTH
    g["w_in"] = jnp.concatenate([
        _wgrad(s["ht"], dqkv, 512, 768, "wgrad_qkv"), _wgrad(s["ht"], dz, 512, 512, "wgrad_z"),
        _wgrad(s["ht"], dbd, 512, LANES, "wgrad_bd")[:, :2 * HEADS], _wgrad(s["ht"], dsc, 512, 768, "wgrad_sc")], axis=1)
    g.update(g1=dg1[0], g2=dg2[0], gn=dgn[0], gs=dgs[0], scw=dscw[:3], cw=dcw[:4],
             al=dal[0, HEADS:2 * HEADS], dt=ddt[0, HEADS:2 * HEADS])
    return dx, g


def _place():
    return lax.axis_index("x"), lax.axis_index("y"), lax.axis_index("c")


def _other_chips(x, y):
    return [(1 - x, y), (x, 1 - y), (1 - x, 1 - y)]


_HBM = pl.BlockSpec(memory_space=pltpu.HBM)


def _chip_exchange(arrs, name, gather):
    n = len(arrs)

    def body(*refs):
        ins, outs = refs[:n], refs[n:2 * n]
        send_sems, recv_sems, local_sems = refs[2 * n:]
        x, y, c = _place()
        me = 2 * x + y
        others = _other_chips(x, y)

        def remote(k, j, landing):
            px, py = others[j]
            src = ins[k] if gather else ins[k].at[2 * px + py]
            return pltpu.make_async_remote_copy(src_ref=src, dst_ref=outs[k].at[landing], send_sem=send_sems.at[k, j],
                                                recv_sem=recv_sems.at[k, j], device_id=(px, py, c), device_id_type=MESH)

        local = [pltpu.make_async_copy(ins[k] if gather else ins[k].at[me], outs[k].at[me], local_sems.at[k])
                 for k in range(n)]
        sends = [remote(k, j, me) for k in range(n) for j in range(3)]
        for cp in local + sends:
            cp.start()
        for k in range(n):
            for j, (px, py) in enumerate(others):
                remote(k, j, 2 * px + py).wait_recv()
        for cp in sends:
            cp.wait_send()
        for cp in local:
            cp.wait()

    shapes = [jax.ShapeDtypeStruct(((N_CHIPS,) + a.shape) if gather else a.shape, a.dtype) for a in arrs]
    return pl.pallas_call(
        body, name=name, in_specs=[_HBM] * n, out_specs=[_HBM] * n, out_shape=shapes,
        scratch_shapes=[pltpu.SemaphoreType.DMA((n, 3)), pltpu.SemaphoreType.DMA((n, 3)), pltpu.SemaphoreType.DMA((n,))],
    )(*arrs)


def _swap_sibling(arrs):
    n = len(arrs)

    def body(*refs):
        ins, outs = refs[:n], refs[n:2 * n]
        send_sems, recv_sems = refs[2 * n:]
        x, y, c = _place()
        copies = [pltpu.make_async_remote_copy(src_ref=ins[k], dst_ref=outs[k], send_sem=send_sems.at[k],
                                               recv_sem=recv_sems.at[k], device_id=(x, y, 1 - c), device_id_type=MESH)
                  for k in range(n)]
        for cp in copies:
            cp.start()
        for cp in copies:
            cp.wait()

    return pl.pallas_call(
        body, name="swap_sibling", in_specs=[_HBM] * n, out_specs=[_HBM] * n,
        out_shape=[jax.ShapeDtypeStruct(a.shape, a.dtype) for a in arrs],
        scratch_shapes=[pltpu.SemaphoreType.DMA((n,)), pltpu.SemaphoreType.DMA((n,))],
    )(*arrs)


def _all_reduce_small(v):
    rows = v.shape[0]
    flips = [(a, b, cc) for a in (0, 1) for b in (0, 1) for cc in (0, 1)][1:]

    def body(v_ref, out_ref, buf_ref, send_sems, recv_sems):
        x, y, c = _place()
        me = 4 * x + 2 * y + c
        peers = [((1 - x) if a else x, (1 - y) if b else y, (1 - c) if cc else c) for a, b, cc in flips]

        def copy(j, landing):
            return pltpu.make_async_remote_copy(src_ref=v_ref, dst_ref=buf_ref.at[landing], send_sem=send_sems.at[j],
                                                recv_sem=recv_sems.at[j], device_id=peers[j], device_id_type=MESH)

        sends = [copy(j, me) for j in range(N_DEV - 1)]
        for cp in sends:
            cp.start()
        buf_ref[me] = v_ref[...]
        for j, (px, py, pc) in enumerate(peers):
            copy(j, 4 * px + 2 * py + pc).wait_recv()
        for cp in sends:
            cp.wait_send()
        acc = buf_ref[0]
        for d in range(1, N_DEV):
            acc = acc + buf_ref[d]
        out_ref[...] = acc

    vmem = pl.BlockSpec(memory_space=pltpu.VMEM)
    return pl.pallas_call(
        body, name="all_reduce_small", in_specs=[vmem], out_specs=vmem,
        out_shape=jax.ShapeDtypeStruct(v.shape, F32),
        scratch_shapes=[pltpu.VMEM((N_DEV, rows, LANES), F32), pltpu.SemaphoreType.DMA((N_DEV - 1,)),
                        pltpu.SemaphoreType.DMA((N_DEV - 1,))],
    )(v)


def _sum_chips(parts):
    _, rows, cols = parts.shape
    tr = 256

    def body(p_ref, o_ref):
        acc = p_ref[0].astype(F32)
        for s in range(1, N_CHIPS):
            acc = acc + p_ref[s].astype(F32)
        o_ref[...] = acc

    return pl.pallas_call(
        body, name="sum_chips", grid=(rows // tr,),
        in_specs=[pl.BlockSpec((N_CHIPS, tr, cols), lambda i: (0, i, 0))],
        out_specs=pl.BlockSpec((tr, cols), lambda i: (i, 0)),
        out_shape=jax.ShapeDtypeStruct((rows, cols), F32),
        compiler_params=_params(("parallel",)),
    )(parts)


def _adamw(w, m, v, g_parts, name):
    rows, cols = w.shape
    tr = min(rows, 256)
    n = len(g_parts)
    c1 = 1.0 - ADAM_B1 ** ADAM_STEP
    c2 = 1.0 - ADAM_B2 ** ADAM_STEP

    def body(*refs):
        w_ref, m_ref, v_ref = refs[:3]
        g_refs = refs[3:3 + n]
        g_out, d_out, m_out, v_out = refs[3 + n:]
        g = g_refs[0][...]
        for r in g_refs[1:]:
            g = g + r[...]
        m_new = ADAM_B1 * m_ref[...] + (1.0 - ADAM_B1) * g
        v_new = ADAM_B2 * v_ref[...] + (1.0 - ADAM_B2) * (g * g)
        g_out[...] = g
        m_out[...] = m_new
        v_out[...] = v_new
        d_out[...] = -ADAM_LR * ((m_new / c1) / (jnp.sqrt(v_new / c2) + ADAM_EPS) + ADAM_WD * w_ref[...])

    blk = pl.BlockSpec((tr, cols), lambda i: (i, 0))
    return pl.pallas_call(
        body, name=name, grid=(rows // tr,),
        in_specs=[blk] * (3 + n), out_specs=[blk] * 4,
        out_shape=[jax.ShapeDtypeStruct((rows, cols), F32)] * 4,
        compiler_params=_params(("parallel",)),
    )(w, m, v, *g_parts)


def _pack(parts, rows, fill=0.0):
    flat = jnp.concatenate([p.reshape(-1) for p in parts])
    return jnp.pad(flat, (0, rows * LANES - flat.shape[0]), constant_values=fill).reshape(rows, LANES)


def _unpack(packed, shapes):
    flat = packed.reshape(-1)
    out, at = [], 0
    for shp in shapes:
        size = 1
        for s in shp:
            size *= s
        out.append(flat[at:at + size].reshape(shp))
        at += size
    return out


def _packed_rows(shapes):
    total = 0
    for shp in shapes:
        size = 1
        for s in shp:
            size *= s
        total += size
    return -(-total // (8 * LANES)) * 8


def _cols_full(g, l):
    t = g[:, l]
    return jnp.moveaxis(t, 0, 1).reshape(t.shape[1], N_CHIPS * t.shape[2])


def _rows_full(g, l):
    t = g[:, l]
    return t.reshape(N_CHIPS * t.shape[1], t.shape[2])


def _cols_to_shards(t):
    L, rows, cols = t.shape
    return jnp.moveaxis(t.reshape(L, rows, N_CHIPS, cols // N_CHIPS), 2, 0)


def _rows_to_shards(t):
    L, rows, cols = t.shape
    return jnp.moveaxis(t.reshape(L, N_CHIPS, rows // N_CHIPS, cols), 1, 0)


def kernel(x, norm1_g, w_in, dn_conv_w, dn_a_log, dn_dt_bias, dn_norm_g, sc_conv_w, sc_norm_g, w_out, norm2_g, ffn_w_gate, ffn_w_up, ffn_w_down, final_norm_g, loss_target, m_norm1_g, m_w_in, m_dn_conv_w, m_dn_a_log, m_dn_dt_bias, m_dn_norm_g, m_sc_conv_w, m_sc_norm_g, m_w_out, m_norm2_g, m_ffn_w_gate, m_ffn_w_up, m_ffn_w_down, m_final_norm_g, v_norm1_g, v_w_in, v_dn_conv_w, v_dn_a_log, v_dn_dt_bias, v_dn_norm_g, v_sc_conv_w, v_sc_norm_g, v_w_out, v_norm2_g, v_ffn_w_gate, v_ffn_w_up, v_ffn_w_down, v_final_norm_g):
    chip = 2 * lax.axis_index("x") + lax.axis_index("y")

    g_in, g_out, g_gate, g_up, g_down, g_cw, g_scw = _chip_exchange(
        [w_in.astype(BF16), w_out.astype(BF16), ffn_w_gate.astype(BF16), ffn_w_up.astype(BF16),
         ffn_w_down.astype(BF16), dn_conv_w, sc_conv_w], "gather_weights", gather=True)

    layers = []
    for l in range(DEPTH):
        wa, wbd = _split_w_in(_cols_full(g_in, l))
        al, dt = _gate_rows(dn_a_log[l], dn_dt_bias[l])
        layers.append(dict(
            g1=norm1_g[l][None], wa=wa, wbd=wbd, cw=_pad_rows(_cols_full(g_cw, l)), al=al, dt=dt,
            gn=dn_norm_g[l][None], scw=_pad_rows(_cols_full(g_scw, l)), gs=sc_norm_g[l][None],
            w_out=_rows_full(g_out, l), g2=norm2_g[l][None], wg=_cols_full(g_gate, l), wu=_cols_full(g_up, l),
            wd=_rows_full(g_down, l)))

    act = x[0]
    saved = []
    for l in range(DEPTH):
        act, s = _layer_fwd(act, layers[l])
        saved.append(s)
    dact, loss_part, d_final = _loss_head(act, final_norm_g[None], loss_target[0])
    grads = [None] * DEPTH
    for l in reversed(range(DEPTH)):
        dact, grads[l] = _layer_bwd(dact, saved[l], layers[l])
    loss = lax.psum(loss_part[0, 0], ("x", "y", "c"))
    stack = lambda key: jnp.stack([grads[l][key] for l in range(DEPTH)])

    parts = [_cols_to_shards(stack("w_in")), _rows_to_shards(stack("w_out")), _cols_to_shards(stack("wg")),
             _cols_to_shards(stack("wu")), _rows_to_shards(stack("wd"))]
    got = _chip_exchange([p.astype(BF16) for p in parts], "reduce_grads", gather=False)
    mine = [_sum_chips(t.reshape(N_CHIPS, -1, t.shape[-1])) for t in got]
    theirs = _swap_sibling(mine)
    big = {}
    for key, w, m, v, a, b in zip(("w_in", "w_out", "ffn_w_gate", "ffn_w_up", "ffn_w_down"),
                                   (w_in, w_out, ffn_w_gate, ffn_w_up, ffn_w_down),
                                   (m_w_in, m_w_out, m_ffn_w_gate, m_ffn_w_up, m_ffn_w_down),
                                   (v_w_in, v_w_out, v_ffn_w_gate, v_ffn_w_up, v_ffn_w_down), mine, theirs):
        flat = lambda t: t.reshape(-1, t.shape[-1])
        big[key] = [o.reshape(w.shape) for o in _adamw(flat(w), flat(m), flat(v), [a, b], "adamw_" + key)]

    full_shapes = [(DEPTH, D_MODEL), (DEPTH, D_MODEL), (DEPTH, HEAD_DIM), (DEPTH, SC_WIDTH), (DEPTH, HEADS),
                   (DEPTH, HEADS), (D_MODEL,), (DEPTH, 4, QKV), (DEPTH, 3, SC_WIDTH)]
    small_keys = ("g1", "g2", "gn", "gs", "al", "dt")
    packed = _pack([stack(k) for k in small_keys] + [d_final[0], stack("cw"), stack("scw")], _packed_rows(full_shapes))
    sg = _unpack(_all_reduce_small(packed), full_shapes)
    sg[7] = lax.dynamic_slice_in_dim(sg[7], chip * (QKV // N_CHIPS), QKV // N_CHIPS, axis=2)
    sg[8] = lax.dynamic_slice_in_dim(sg[8], chip * (SC_WIDTH // N_CHIPS), SC_WIDTH // N_CHIPS, axis=2)
    small_names = ("norm1_g", "norm2_g", "dn_norm_g", "sc_norm_g", "dn_a_log", "dn_dt_bias", "final_norm_g",
                   "dn_conv_w", "sc_conv_w")
    sw = (norm1_g, norm2_g, dn_norm_g, sc_norm_g, dn_a_log, dn_dt_bias, final_norm_g, dn_conv_w, sc_conv_w)
    sm = (m_norm1_g, m_norm2_g, m_dn_norm_g, m_sc_norm_g, m_dn_a_log, m_dn_dt_bias, m_final_norm_g, m_dn_conv_w, m_sc_conv_w)
    sv = (v_norm1_g, v_norm2_g, v_dn_norm_g, v_sc_norm_g, v_dn_a_log, v_dn_dt_bias, v_final_norm_g, v_dn_conv_w, v_sc_conv_w)
    shard_shapes = [t.shape for t in sw]
    rows = _packed_rows(shard_shapes)
    outs = _adamw(_pack(sw, rows), _pack(sm, rows), _pack(sv, rows, fill=1.0), [_pack(sg, rows)], "adamw_small")
    small = {name: [] for name in small_names}
    for o in outs:
        for name, t in zip(small_names, _unpack(o, shard_shapes)):
            small[name].append(t)

    order = ("norm1_g", "w_in", "dn_conv_w", "dn_a_log", "dn_dt_bias", "dn_norm_g", "sc_conv_w", "sc_norm_g", "w_out",
             "norm2_g", "ffn_w_gate", "ffn_w_up", "ffn_w_down", "final_norm_g")
    result = {**big, **small}
    return (loss, dact[None], *[result[n][0] for n in order], *[result[n][1] for n in order],
            *[result[n][2] for n in order], *[result[n][3] for n in order])
```

```python
import jax
import jax.numpy as jnp
from jax import lax
from jax.experimental import pallas as pl
from jax.experimental.pallas import tpu as pltpu

F32 = jnp.float32
BF16 = jnp.bfloat16
MESH = pl.DeviceIdType.MESH

D_MODEL = 1024
DEPTH = 4
HEADS = 4
HEAD_DIM = 128
DN_WIDTH = HEADS * HEAD_DIM
SC_WIDTH = 512
SC_GROUPS = 4
D_FF = 2816
CHUNK = 64
QKV = 3 * DN_WIDTH
W_IN_COLS = 4 * DN_WIDTH + 2 * HEADS + 3 * SC_WIDTH
WA_COLS = QKV + DN_WIDTH + 3 * SC_WIDTH
LANES = 128
EPS = 1e-6
Q_SCALE = HEAD_DIM ** -0.5
N_CHIPS = 4
N_DEV = 8

ADAM_LR = 0.001
ADAM_B1 = 0.9
ADAM_B2 = 0.999
ADAM_EPS = 1e-08
ADAM_WD = 0.01
ADAM_STEP = 10

VMEM_LIMIT = 56 * 1024 * 1024

NN = (((1,), (0,)), ((), ()))
NT = (((1,), (1,)), ((), ()))
TN = (((0,), (0,)), ((), ()))


def _mm(a, b, dims=NN):
    return lax.dot_general(a.astype(BF16), b.astype(BF16), dims, preferred_element_type=F32)


def _mm32(a, b, dims=NN):
    return lax.dot_general(a, b, dims, preferred_element_type=F32, precision=lax.Precision.HIGHEST)


def _params(sem, vmem=VMEM_LIMIT):
    return pltpu.CompilerParams(dimension_semantics=sem, vmem_limit_bytes=vmem)


def _sigmoid(x):
    return 1.0 / (1.0 + jnp.exp(-x))


def _softplus(x):
    return jnp.maximum(x, 0.0) + jnp.log1p(jnp.exp(-jnp.abs(x)))


def _row_acc(acc_ref, val):
    acc_ref[0:1, :] += jnp.sum(val, axis=0, keepdims=True)


def _rms_bwd(dh, xh, r, gain):
    dxh = dh * gain
    return r * (dxh - xh * jnp.mean(dxh * xh, axis=-1, keepdims=True))


def _before_halo(tb):
    return lambda i: (jnp.maximum(i * (tb // 8) - 1, 0), 0)


def _after_halo(tb, n_rows):
    last = n_rows // 8 - 1
    return lambda i: (jnp.minimum((i + 1) * (tb // 8), last), 0)


def _taps(xc, w, n_taps, tb, first):
    out = w[0:1, :] * xc[first:first + tb, :]
    for j in range(1, n_taps):
        out = out + w[j:j + 1, :] * xc[first + j:first + j + tb, :]
    return out


def _in_proj(x, g1, wa, wbd):
    T = x.shape[0]
    tb = 256

    def body(x_ref, g_ref, wa_ref, wbd_ref, qkv_ref, z_ref, sc_ref, bd_ref, ht_ref):
        xv = x_ref[...]
        r = lax.rsqrt(jnp.mean(xv * xv, axis=-1, keepdims=True) + EPS)
        h = (xv * r * g_ref[...]).astype(BF16)
        p = jnp.dot(h, wa_ref[...], preferred_element_type=F32)
        qkv_ref[...] = p[:, :QKV]
        z_ref[...] = p[:, QKV:QKV + DN_WIDTH]
        sc_ref[...] = p[:, QKV + DN_WIDTH:]
        bd_ref[...] = jnp.dot(h, wbd_ref[...], preferred_element_type=F32)
        ht_ref[...] = h.T

    tok = lambda w: pl.BlockSpec((tb, w), lambda i: (i, 0))
    full = lambda a: pl.BlockSpec(a.shape, lambda i: (0, 0))
    return pl.pallas_call(
        body, name="in_proj", grid=(T // tb,),
        in_specs=[tok(D_MODEL), full(g1), full(wa), full(wbd)],
        out_specs=[tok(QKV), tok(DN_WIDTH), tok(3 * SC_WIDTH), tok(LANES),
                   pl.BlockSpec((D_MODEL, tb), lambda i: (0, i))],
        out_shape=[jax.ShapeDtypeStruct((T, QKV), F32), jax.ShapeDtypeStruct((T, DN_WIDTH), F32),
                   jax.ShapeDtypeStruct((T, 3 * SC_WIDTH), F32), jax.ShapeDtypeStruct((T, LANES), F32),
                   jax.ShapeDtypeStruct((D_MODEL, T), BF16)],
        compiler_params=_params(("parallel",)),
    )(x, g1, wa, wbd)


def _dn_act(pre, halo, cw, tb):
    xc = jnp.concatenate([halo, pre], axis=0)
    c = _taps(xc, cw, 4, tb, 5)
    sg = _sigmoid(c)
    return xc, c, sg, c * sg


def _gates(bd, al_row, dt_row):
    lane = lax.broadcasted_iota(jnp.int32, bd.shape, 1)
    beta = _sigmoid(bd)
    g = -jnp.exp(al_row) * _softplus(bd + dt_row)
    return jnp.where(lane < HEADS, beta, jnp.where(lane < 2 * HEADS, g, 0.0))


def _dn_prep(qkv, cw, bd, al_row, dt_row):
    T = qkv.shape[0]
    tb = 512

    def body(pre_ref, halo_ref, cw_ref, bd_ref, al_ref, dt_ref, q_ref, k_ref, v_ref, bg_ref):
        halo = jnp.where(pl.program_id(0) > 0, halo_ref[...], 0.0)
        _, _, _, a = _dn_act(pre_ref[...], halo, cw_ref[...], tb)
        for hh in range(HEADS):
            sl = slice(HEAD_DIM * hh, HEAD_DIM * (hh + 1))
            qs = a[:, sl]
            q_ref[:, sl] = qs * (lax.rsqrt(jnp.sum(qs * qs, axis=-1, keepdims=True) + EPS) * Q_SCALE)
            ks = a[:, DN_WIDTH + HEAD_DIM * hh:DN_WIDTH + HEAD_DIM * (hh + 1)]
            k_ref[:, sl] = ks * lax.rsqrt(jnp.sum(ks * ks, axis=-1, keepdims=True) + EPS)
        v_ref[...] = a[:, 2 * DN_WIDTH:]
        gates = _gates(bd_ref[...], al_ref[...], dt_ref[...])
        lane = lax.broadcasted_iota(jnp.int32, gates.shape, 1)
        bg_ref[...] = jnp.where(lane < HEADS, gates, _mm32(_chunk_cumsum_matrix(tb), gates))

    tok = lambda w: pl.BlockSpec((tb, w), lambda i: (i, 0))
    full = lambda a: pl.BlockSpec(a.shape, lambda i: (0, 0))
    return pl.pallas_call(
        body, name="dn_prep", grid=(T // tb,),
        in_specs=[tok(QKV), pl.BlockSpec((8, QKV), _before_halo(tb)), full(cw), tok(LANES), full(al_row), full(dt_row)],
        out_specs=[tok(DN_WIDTH), tok(DN_WIDTH), tok(DN_WIDTH), tok(LANES)],
        out_shape=[jax.ShapeDtypeStruct((T, DN_WIDTH), F32)] * 3 + [jax.ShapeDtypeStruct((T, LANES), F32)],
        compiler_params=_params(("parallel",)),
    )(qkv, qkv, cw, bd, al_row, dt_row)


def _chunk_masks():
    row = lax.broadcasted_iota(jnp.int32, (CHUNK, CHUNK), 0)
    col = lax.broadcasted_iota(jnp.int32, (CHUNK, CHUNK), 1)
    return row >= col, row > col


def _chunk_cumsum_matrix(n):
    row = lax.broadcasted_iota(jnp.int32, (n, n), 0)
    col = lax.broadcasted_iota(jnp.int32, (n, n), 1)
    return jnp.logical_and(row >= col, row // CHUNK == col // CHUNK).astype(F32)


def _chunk_units(q_ref, k_ref, v_ref, bg_ref, rows):
    bgc = bg_ref[rows, :]
    bg_t = bgc.T
    qv, kv, vv = q_ref[rows, :], k_ref[rows, :], v_ref[rows, :]
    units = []
    for h in range(HEADS):
        sl = slice(HEAD_DIM * h, HEAD_DIM * (h + 1))
        units.append((qv[:, sl], kv[:, sl], vv[:, sl], bgc[:, h:h + 1], bgc[:, HEADS + h:HEADS + h + 1],
                      bg_t[HEADS + h:HEADS + h + 1, :]))
    return units


def _units_local(units, masks):
    causal, strict = masks
    pre = []
    for q, k, v, beta, gc, gr in units:
        kb = k * beta
        eg = jnp.exp(gc)
        g_last = gc[CHUNK - 1:CHUNK, :]
        ek = jnp.exp(g_last - gc)
        pre.append(dict(q=q, k=k, v=v, beta=beta, decay=jnp.exp(jnp.where(causal, gc - gr, -1e30)), kb=kb, vb=v * beta,
                        eg=eg, kbg=kb * eg, ek=ek, gl=jnp.exp(g_last), q_dec=q * eg, k_dec=k * ek))
    both = [_mm(jnp.concatenate([p["kb"], p["q"]], axis=0), p["k"], NT) for p in pre]
    for p, b in zip(pre, both):
        p["low"] = jnp.where(strict, b[:CHUNK] * p["decay"], 0.0)
        p["qk"] = jnp.where(causal, b[CHUNK:] * p["decay"], 0.0)
    xs = [-p["low"] for p in pre]
    pw = [_mm(p["low"], p["low"]) for p in pre]
    for _ in range(4):
        both = [_mm(jnp.concatenate([pp, x], axis=0), pp) for pp, x in zip(pw, xs)]
        xs = [x + pp + b[CHUNK:] for x, pp, b in zip(xs, pw, both)]
        pw = [b[:CHUNK] for b in both]
    last = [_mm(x, pp) for x, pp in zip(xs, pw)]
    xs = [x + pp + b for x, pp, b in zip(xs, pw, last)]
    uw = [_mm(x, jnp.concatenate([p["vb"], p["kbg"]], axis=1)) for x, p in zip(xs, pre)]
    for p, x, b in zip(pre, xs, uw):
        p["xm"] = x
        p["u"] = p["vb"] + b[:, :HEAD_DIM]
        p["w"] = p["kbg"] + b[:, HEAD_DIM:]
    return pre


def _delta_fwd(q, k, v, bg):
    T = q.shape[0]
    tb = 512
    n_chunk = tb // CHUNK

    def body(q_ref, k_ref, v_ref, bg_ref, o_ref, st_ref, s_ref):
        @pl.when(pl.program_id(0) == 0)
        def _():
            s_ref[...] = jnp.zeros_like(s_ref)

        masks = _chunk_masks()

        def pair(pi, carry):
            rows = [pl.ds(pl.multiple_of((2 * pi + j) * CHUNK, CHUNK), CHUNK) for j in range(2)]
            loc = _units_local(_chunk_units(q_ref, k_ref, v_ref, bg_ref, rows[0])
                               + _chunk_units(q_ref, k_ref, v_ref, bg_ref, rows[1]), masks)
            states = [s_ref[h] for h in range(HEADS)]
            for j in range(2):
                lj = loc[HEADS * j:HEADS * (j + 1)]
                ws = [_mm(jnp.concatenate([p["w"], p["q_dec"]], axis=0), s) for p, s in zip(lj, states)]
                v_new = [p["u"] - b[:CHUNK] for p, b in zip(lj, ws)]
                intra = [_mm(p["qk"], vn) for p, vn in zip(lj, v_new)]
                upd = [_mm(p["k_dec"], vn, TN) for p, vn in zip(lj, v_new)]
                o_ref[rows[j], :] = jnp.concatenate([b[CHUNK:] + a for b, a in zip(ws, intra)], axis=1)
                for h in range(HEADS):
                    st_ref[2 * pi + j, h] = states[h]
                states = [p["gl"] * s + d for p, s, d in zip(lj, states, upd)]
            for h in range(HEADS):
                s_ref[h] = states[h]
            return carry

        lax.fori_loop(0, n_chunk // 2, pair, 0)

    tok = lambda w: pl.BlockSpec((tb, w), lambda i: (i, 0))
    return pl.pallas_call(
        body, name="delta_fwd", grid=(T // tb,),
        in_specs=[tok(DN_WIDTH), tok(DN_WIDTH), tok(DN_WIDTH), tok(LANES)],
        out_specs=[tok(DN_WIDTH), pl.BlockSpec((n_chunk, HEADS, HEAD_DIM, HEAD_DIM), lambda i: (i, 0, 0, 0))],
        out_shape=[jax.ShapeDtypeStruct((T, DN_WIDTH), F32),
                   jax.ShapeDtypeStruct((T // CHUNK, HEADS, HEAD_DIM, HEAD_DIM), F32)],
        scratch_shapes=[pltpu.VMEM((HEADS, HEAD_DIM, HEAD_DIM), F32)],
        compiler_params=_params(("arbitrary",)),
    )(q, k, v, bg)


def _dn_out(o, z, gn):
    outs, ohs, rs = [], [], []
    for hh in range(HEADS):
        oh = o[:, HEAD_DIM * hh:HEAD_DIM * (hh + 1)]
        r = lax.rsqrt(jnp.mean(oh * oh, axis=-1, keepdims=True) + EPS)
        ohs.append(oh * r)
        rs.append(r)
    sz = _sigmoid(z)
    oh = jnp.concatenate(ohs, axis=1)
    gn4 = jnp.concatenate([gn] * HEADS, axis=1)
    return oh * gn4 * (z * sz), oh, rs, sz, gn4


def _sc_fwd(sc_in, halo, cw, tb):
    xc = jnp.concatenate([halo, sc_in], axis=0)
    u = xc[:, SC_WIDTH:2 * SC_WIDTH] * xc[:, 2 * SC_WIDTH:]
    cv = _taps(u, cw, 3, tb, 6)
    gate_b = sc_in[:, :SC_WIDTH]
    y = gate_b * cv
    gw = SC_WIDTH // SC_GROUPS
    yhs, rs = [], []
    for gi in range(SC_GROUPS):
        yg = y[:, gw * gi:gw * (gi + 1)]
        r = lax.rsqrt(jnp.mean(yg * yg, axis=-1, keepdims=True) + EPS)
        yhs.append(yg * r)
        rs.append(r)
    return u, cv, gate_b, jnp.concatenate(yhs, axis=1), rs


def _mix_out(o, z, sc_in, x, w_out, gn, scw, gs):
    T = x.shape[0]
    tb = 256

    def body(o_ref, z_ref, sc_ref, halo_ref, x_ref, w_ref, gn_ref, scw_ref, gs_ref, x1_ref, mt_ref):
        o_n = _dn_out(o_ref[...], z_ref[...], gn_ref[...])[0]
        halo = jnp.where(pl.program_id(0) > 0, halo_ref[...], 0.0)
        yh = _sc_fwd(sc_ref[...], halo, scw_ref[...], tb)[3]
        mix = jnp.concatenate([o_n, yh * gs_ref[...]], axis=1).astype(BF16)
        x1_ref[...] = x_ref[...] + jnp.dot(mix, w_ref[...], preferred_element_type=F32)
        mt_ref[...] = mix.T

    tok = lambda w: pl.BlockSpec((tb, w), lambda i: (i, 0))
    full = lambda a: pl.BlockSpec(a.shape, lambda i: (0, 0))
    return pl.pallas_call(
        body, name="mix_out", grid=(T // tb,),
        in_specs=[tok(DN_WIDTH), tok(DN_WIDTH), tok(3 * SC_WIDTH), pl.BlockSpec((8, 3 * SC_WIDTH), _before_halo(tb)),
                  tok(D_MODEL), full(w_out), full(gn), full(scw), full(gs)],
        out_specs=[tok(D_MODEL), pl.BlockSpec((D_MODEL, tb), lambda i: (0, i))],
        out_shape=[jax.ShapeDtypeStruct((T, D_MODEL), F32), jax.ShapeDtypeStruct((D_MODEL, T), BF16)],
        compiler_params=_params(("parallel",)),
    )(o, z, sc_in, sc_in, x, w_out, gn, scw, gs)


def _ffn(x1, g2, wg, wu, wd):
    T = x1.shape[0]
    tb = 256

    def body(x_ref, g_ref, wg_ref, wu_ref, wd_ref, x2_ref, a_ref, b_ref, ht_ref):
        xv = x_ref[...]
        r = lax.rsqrt(jnp.mean(xv * xv, axis=-1, keepdims=True) + EPS)
        h = (xv * r * g_ref[...]).astype(BF16)
        a = jnp.dot(h, wg_ref[...], preferred_element_type=F32)
        b = jnp.dot(h, wu_ref[...], preferred_element_type=F32)
        act = (a * _sigmoid(a) * b).astype(BF16)
        x2_ref[...] = xv + jnp.dot(act, wd_ref[...], preferred_element_type=F32)
        a_ref[...] = a.astype(BF16)
        b_ref[...] = b.astype(BF16)
        ht_ref[...] = h.T

    tok = lambda w: pl.BlockSpec((tb, w), lambda i: (i, 0))
    full = lambda a: pl.BlockSpec(a.shape, lambda i: (0, 0))
    return pl.pallas_call(
        body, name="ffn", grid=(T // tb,),
        in_specs=[tok(D_MODEL), full(g2), full(wg), full(wu), full(wd)],
        out_specs=[tok(D_MODEL), tok(D_FF), tok(D_FF), pl.BlockSpec((D_MODEL, tb), lambda i: (0, i))],
        out_shape=[jax.ShapeDtypeStruct((T, D_MODEL), F32), jax.ShapeDtypeStruct((T, D_FF), BF16),
                   jax.ShapeDtypeStruct((T, D_FF), BF16), jax.ShapeDtypeStruct((D_MODEL, T), BF16)],
        compiler_params=_params(("parallel",)),
    )(x1, g2, wg, wu, wd)


def _loss_head(x, gf, target):
    T = x.shape[0]
    tb = 512

    def body(x_ref, g_ref, t_ref, dx_ref, loss_ref, dg_ref):
        @pl.when(pl.program_id(0) == 0)
        def _():
            loss_ref[...] = jnp.zeros_like(loss_ref)
            dg_ref[...] = jnp.zeros_like(dg_ref)

        xv = x_ref[...]
        r = lax.rsqrt(jnp.mean(xv * xv, axis=-1, keepdims=True) + EPS)
        xh = xv * r
        err = xh * g_ref[...] - t_ref[...]
        per_tok = jnp.mean(err * err, axis=-1, keepdims=True)
        loss_ref[...] += 0.5 * jnp.sum(per_tok, axis=0, keepdims=True)
        dy = err * (1.0 / D_MODEL)
        _row_acc(dg_ref, dy * xh)
        dx_ref[...] = _rms_bwd(dy, xh, r, g_ref[...])

    tok = pl.BlockSpec((tb, D_MODEL), lambda i: (i, 0))
    return pl.pallas_call(
        body, name="loss_head", grid=(T // tb,),
        in_specs=[tok, pl.BlockSpec(gf.shape, lambda i: (0, 0)), tok],
        out_specs=[tok, pl.BlockSpec((8, LANES), lambda i: (0, 0)), pl.BlockSpec((8, D_MODEL), lambda i: (0, 0))],
        out_shape=[jax.ShapeDtypeStruct((T, D_MODEL), F32), jax.ShapeDtypeStruct((8, LANES), F32),
                   jax.ShapeDtypeStruct((8, D_MODEL), F32)],
        compiler_params=_params(("arbitrary",)),
    )(x, gf, target)


def _ffn_bwd(dx2, x1, a, b, g2, wg, wu, wd):
    T = x1.shape[0]
    tb = 256

    def body(dx2_ref, x_ref, a_ref, b_ref, g_ref, wg_ref, wu_ref, wd_ref, dx1_ref, da_ref, db_ref, at_ref, dg_ref):
        @pl.when(pl.program_id(0) == 0)
        def _():
            dg_ref[...] = jnp.zeros_like(dg_ref)

        dx2v = dx2_ref[...]
        av = a_ref[...].astype(F32)
        bv = b_ref[...].astype(F32)
        dact = _mm(dx2v, wd_ref[...], NT)
        sa = _sigmoid(av)
        silu = av * sa
        da = (dact * bv * (sa * (1.0 + av * (1.0 - sa)))).astype(BF16)
        db = (dact * silu).astype(BF16)
        dh = _mm(da, wg_ref[...], NT) + _mm(db, wu_ref[...], NT)
        xv = x_ref[...]
        r = lax.rsqrt(jnp.mean(xv * xv, axis=-1, keepdims=True) + EPS)
        xh = xv * r
        _row_acc(dg_ref, dh * xh)
        dx1_ref[...] = dx2v + _rms_bwd(dh, xh, r, g_ref[...])
        da_ref[...] = da
        db_ref[...] = db
        at_ref[...] = (silu * bv).astype(BF16).T

    tok = lambda w: pl.BlockSpec((tb, w), lambda i: (i, 0))
    full = lambda t: pl.BlockSpec(t.shape, lambda i: (0, 0))
    return pl.pallas_call(
        body, name="ffn_bwd", grid=(T // tb,),
        in_specs=[tok(D_MODEL), tok(D_MODEL), tok(D_FF), tok(D_FF), full(g2), full(wg), full(wu), full(wd)],
        out_specs=[tok(D_MODEL), tok(D_FF), tok(D_FF), pl.BlockSpec((D_FF, tb), lambda i: (0, i)),
                   pl.BlockSpec((8, D_MODEL), lambda i: (0, 0))],
        out_shape=[jax.ShapeDtypeStruct((T, D_MODEL), F32), jax.ShapeDtypeStruct((T, D_FF), BF16),
                   jax.ShapeDtypeStruct((T, D_FF), BF16), jax.ShapeDtypeStruct((D_FF, T), BF16),
                   jax.ShapeDtypeStruct((8, D_MODEL), F32)],
        compiler_params=_params(("arbitrary",)),
    )(dx2, x1, a, b, g2, wg, wu, wd)


def _wgrad(at, b, bm, bn, name):
    M, T = at.shape
    N = b.shape[1]
    bk = min(T, 1024)

    def body(a_ref, b_ref, o_ref):
        @pl.when(pl.program_id(2) == 0)
        def _():
            o_ref[...] = jnp.zeros_like(o_ref)

        o_ref[...] += jnp.dot(a_ref[...], b_ref[...], preferred_element_type=F32)

    return pl.pallas_call(
        body, name=name, grid=(M // bm, N // bn, T // bk),
        in_specs=[pl.BlockSpec((bm, bk), lambda i, j, kk: (i, kk)), pl.BlockSpec((bk, bn), lambda i, j, kk: (kk, j))],
        out_specs=pl.BlockSpec((bm, bn), lambda i, j, kk: (i, j)),
        out_shape=jax.ShapeDtypeStruct((M, N), F32),
        compiler_params=_params(("parallel", "parallel", "arbitrary")),
    )(at, b)


def _mix_out_bwd(dx1, o, z, sc_in, w_out, gn, scw, gs):
    T = dx1.shape[0]
    tb = 256

    def body(dx_ref, o_ref, z_ref, sc_ref, halo_ref, w_ref, gn_ref, scw_ref, gs_ref,
             do_ref, dz_ref, dgb_ref, dcv_ref, dgn_ref, dgs_ref, dscw_ref):
        @pl.when(pl.program_id(0) == 0)
        def _():
            dgn_ref[...] = jnp.zeros_like(dgn_ref)
            dgs_ref[...] = jnp.zeros_like(dgs_ref)
            dscw_ref[...] = jnp.zeros_like(dscw_ref)

        dmix = _mm(dx_ref[...], w_ref[...], NT)
        don = dmix[:, :DN_WIDTH]
        dosc = dmix[:, DN_WIDTH:]
        zv = z_ref[...]
        _, oh, rs, sz, gn4 = _dn_out(o_ref[...], zv, gn_ref[...])
        silu_z = zv * sz
        dgn_full = don * oh * silu_z
        dgn_ref[0:1, :] += jnp.sum(sum(dgn_full[:, HEAD_DIM * hh:HEAD_DIM * (hh + 1)] for hh in range(HEADS)),
                                   axis=0, keepdims=True)
        dz_ref[...] = (don * oh * gn4 * (sz * (1.0 + zv * (1.0 - sz)))).astype(BF16)
        t = don * gn4 * silu_z
        for hh in range(HEADS):
            sl = slice(HEAD_DIM * hh, HEAD_DIM * (hh + 1))
            th, ohh = t[:, sl], oh[:, sl]
            do_ref[:, sl] = rs[hh] * (th - ohh * jnp.mean(th * ohh, axis=-1, keepdims=True))
        halo = jnp.where(pl.program_id(0) > 0, halo_ref[...], 0.0)
        u, cv, gate_b, yh, rys = _sc_fwd(sc_ref[...], halo, scw_ref[...], tb)
        _row_acc(dgs_ref, dosc * yh)
        ty = dosc * gs_ref[...]
        gw = SC_WIDTH // SC_GROUPS
        dys = []
        for gi in range(SC_GROUPS):
            sl = slice(gw * gi, gw * (gi + 1))
            tg, yg = ty[:, sl], yh[:, sl]
            dys.append(rys[gi] * (tg - yg * jnp.mean(tg * yg, axis=-1, keepdims=True)))
        dy = jnp.concatenate(dys, axis=1)
        dgb_ref[...] = dy * cv
        dcv = dy * gate_b
        dcv_ref[...] = dcv
        for j in range(3):
            dscw_ref[j:j + 1, :] += jnp.sum(dcv * u[6 + j:6 + j + tb, :], axis=0, keepdims=True)

    tok = lambda w: pl.BlockSpec((tb, w), lambda i: (i, 0))
    full = lambda t: pl.BlockSpec(t.shape, lambda i: (0, 0))
    acc = lambda w: pl.BlockSpec((8, w), lambda i: (0, 0))
    return pl.pallas_call(
        body, name="mix_out_bwd", grid=(T // tb,),
        in_specs=[tok(D_MODEL), tok(DN_WIDTH), tok(DN_WIDTH), tok(3 * SC_WIDTH),
                  pl.BlockSpec((8, 3 * SC_WIDTH), _before_halo(tb)), full(w_out), full(gn), full(scw), full(gs)],
        out_specs=[tok(DN_WIDTH), tok(DN_WIDTH), tok(SC_WIDTH), tok(SC_WIDTH), acc(HEAD_DIM), acc(SC_WIDTH), acc(SC_WIDTH)],
        out_shape=[jax.ShapeDtypeStruct((T, DN_WIDTH), F32), jax.ShapeDtypeStruct((T, DN_WIDTH), BF16),
                   jax.ShapeDtypeStruct((T, SC_WIDTH), F32), jax.ShapeDtypeStruct((T, SC_WIDTH), F32),
                   jax.ShapeDtypeStruct((8, HEAD_DIM), F32), jax.ShapeDtypeStruct((8, SC_WIDTH), F32),
                   jax.ShapeDtypeStruct((8, SC_WIDTH), F32)],
        compiler_params=_params(("arbitrary",)),
    )(dx1, o, z, sc_in, sc_in, w_out, gn, scw, gs)


def _sc_conv_bwd(dcv, dgb, sc_in, scw):
    T = dcv.shape[0]
    tb = 512

    def body(dcv_ref, halo_ref, dgb_ref, sc_ref, w_ref, out_ref):
        last = pl.program_id(0) == pl.num_programs(0) - 1
        halo = jnp.where(last, 0.0, halo_ref[...])
        xc = jnp.concatenate([dcv_ref[...], halo], axis=0)
        w = w_ref[...]
        du = w[2:3, :] * xc[0:tb, :] + w[1:2, :] * xc[1:tb + 1, :] + w[0:1, :] * xc[2:tb + 2, :]
        sc = sc_ref[...]
        out_ref[:, :SC_WIDTH] = dgb_ref[...].astype(BF16)
        out_ref[:, SC_WIDTH:2 * SC_WIDTH] = (du * sc[:, 2 * SC_WIDTH:]).astype(BF16)
        out_ref[:, 2 * SC_WIDTH:] = (du * sc[:, SC_WIDTH:2 * SC_WIDTH]).astype(BF16)

    tok = lambda w: pl.BlockSpec((tb, w), lambda i: (i, 0))
    return pl.pallas_call(
        body, name="sc_conv_bwd", grid=(T // tb,),
        in_specs=[tok(SC_WIDTH), pl.BlockSpec((8, SC_WIDTH), _after_halo(tb, T)), tok(SC_WIDTH), tok(3 * SC_WIDTH),
                  pl.BlockSpec(scw.shape, lambda i: (0, 0))],
        out_specs=tok(3 * SC_WIDTH),
        out_shape=jax.ShapeDtypeStruct((T, 3 * SC_WIDTH), BF16),
        compiler_params=_params(("parallel",)),
    )(dcv, dcv, dgb, sc_in, scw)


def _delta_bwd(q, k, v, bg, states, do):
    T = q.shape[0]
    tb = 512
    n_chunk = tb // CHUNK
    nb = T // tb

    def body(q_ref, k_ref, v_ref, bg_ref, st_ref, do_ref, dq_ref, dk_ref, dv_ref, dbg_ref, ds_ref):
        @pl.when(pl.program_id(0) == 0)
        def _():
            ds_ref[...] = jnp.zeros_like(ds_ref)

        masks = _chunk_masks()
        causal, strict = masks
        lane = lax.broadcasted_iota(jnp.int32, (CHUNK, LANES), 1)
        last_row = lax.broadcasted_iota(jnp.int32, (CHUNK, 1), 0) == CHUNK - 1
        cat = jnp.concatenate
        heads = range(HEADS)

        def chunk(cj, carry):
            ci = n_chunk - 1 - cj
            rows = pl.ds(pl.multiple_of(ci * CHUNK, CHUNK), CHUNK)
            loc = _units_local(_chunk_units(q_ref, k_ref, v_ref, bg_ref, rows), masks)
            dov = do_ref[rows, :]
            do = [dov[:, HEAD_DIM * h:HEAD_DIM * (h + 1)] for h in heads]
            state = [st_ref[ci, h] for h in heads]
            ds_next = [ds_ref[h] for h in heads]
            w_s = [_mm(p["w"], s) for p, s in zip(loc, state)]
            dq_dec = [_mm(d, s, NT) for d, s in zip(do, state)]
            qk_do = [_mm(p["qk"], d, TN) for p, d in zip(loc, do)]
            kd_ds = [_mm(p["k_dec"], d) for p, d in zip(loc, ds_next)]
            qd_do = [_mm(p["q_dec"], d, TN) for p, d in zip(loc, do)]
            v_new = [p["u"] - t for p, t in zip(loc, w_s)]
            dv_new = [a + b for a, b in zip(qk_do, kd_ds)]
            dqk = [jnp.where(causal, _mm(d, vn, NT), 0.0) for d, vn in zip(do, v_new)]
            dk_dec = [_mm(vn, d, NT) for vn, d in zip(v_new, ds_next)]
            w_dv = [_mm(p["w"], dvn, TN) for p, dvn in zip(loc, dv_new)]
            dw = [-_mm(dvn, s, NT) for dvn, s in zip(dv_new, state)]
            for h in heads:
                ds_ref[h] = loc[h]["gl"] * ds_next[h] + qd_do[h] - w_dv[h]
            dtm = [_mm(cat([dvn, d], axis=1), cat([p["vb"], p["kbg"]], axis=1), NT) for dvn, d, p in zip(dv_new, dw, loc)]
            x_t = [_mm(p["xm"], cat([dvn, d], axis=1), TN) for p, dvn, d in zip(loc, dv_new, dw)]
            dvb = [dvn + t[:, :HEAD_DIM] for dvn, t in zip(dv_new, x_t)]
            dkbg = [d + t[:, HEAD_DIM:] for d, t in zip(dw, x_t)]
            y = [t + _mm(p["xm"], t, TN) for p, t in zip(loc, dtm)]
            dlow = [jnp.where(strict, -(t + _mm(t, p["xm"], NT)), 0.0) for p, t in zip(loc, y)]
            dmm = [d * p["decay"] for d, p in zip(dlow, loc)]
            dnn = [d * p["decay"] for d, p in zip(dqk, loc)]
            by_k = [_mm(cat([a, b], axis=0), p["k"]) for a, b, p in zip(dmm, dnn, loc)]
            dk_mm = [_mm(cat([a, b], axis=0), cat([p["kb"], p["q"]], axis=0), TN) for a, b, p in zip(dmm, dnn, loc)]
            dq_out, dk_out, dv_out = [], [], []
            dbeta_all = jnp.zeros((CHUNK, LANES), F32)
            dgc_all = jnp.zeros((CHUNK, LANES), F32)
            for h in heads:
                p = loc[h]
                dkb = by_k[h][:CHUNK] + dkbg[h] * p["eg"]
                dq_out.append(by_k[h][CHUNK:] + dq_dec[h] * p["eg"])
                dk_out.append(dk_mm[h] + dk_dec[h] * p["ek"] + dkb * p["beta"])
                dv_out.append(dvb[h] * p["beta"])
                dbeta = jnp.sum(dkb * p["k"] + dvb[h] * p["v"], axis=1, keepdims=True)
                e = dlow[h] * p["low"] + dqk[h] * p["qk"]
                kd = jnp.sum(dk_dec[h] * p["k_dec"], axis=1, keepdims=True)
                dgc = (jnp.sum(e, axis=1, keepdims=True) - jnp.sum(e.T, axis=1, keepdims=True)
                       + jnp.sum(dq_dec[h] * p["q_dec"], axis=1, keepdims=True) - kd
                       + jnp.sum(dkbg[h] * p["kbg"], axis=1, keepdims=True))
                dgl = jnp.sum(jnp.sum(ds_next[h] * state[h], axis=1, keepdims=True), axis=0, keepdims=True)
                d_last = jnp.sum(kd, axis=0, keepdims=True) + dgl * p["gl"]
                dgc = dgc + jnp.where(last_row, d_last, 0.0)
                dbeta_all = jnp.where(lane == h, dbeta, dbeta_all)
                dgc_all = jnp.where(lane == h + HEADS, dgc, dgc_all)
            dq_ref[rows, :] = cat(dq_out, axis=1)
            dk_ref[rows, :] = cat(dk_out, axis=1)
            dv_ref[rows, :] = cat(dv_out, axis=1)
            dbg_ref[rows, :] = dbeta_all + dgc_all
            return carry

        lax.fori_loop(0, n_chunk, chunk, 0)

    tok = lambda w: pl.BlockSpec((tb, w), lambda i: (nb - 1 - i, 0))
    return pl.pallas_call(
        body, name="delta_bwd", grid=(nb,),
        in_specs=[tok(DN_WIDTH), tok(DN_WIDTH), tok(DN_WIDTH), tok(LANES),
                  pl.BlockSpec((n_chunk, HEADS, HEAD_DIM, HEAD_DIM), lambda i: (nb - 1 - i, 0, 0, 0)), tok(DN_WIDTH)],
        out_specs=[tok(DN_WIDTH), tok(DN_WIDTH), tok(DN_WIDTH), tok(LANES)],
        out_shape=[jax.ShapeDtypeStruct((T, DN_WIDTH), F32)] * 3 + [jax.ShapeDtypeStruct((T, LANES), F32)],
        scratch_shapes=[pltpu.VMEM((HEADS, HEAD_DIM, HEAD_DIM), F32)],
        compiler_params=_params(("arbitrary",)),
    )(q, k, v, bg, states, do)


def _dn_prep_bwd(dq, dk, dv, dbg, qkv, cw, bd, al_row, dt_row):
    T = qkv.shape[0]
    tb = 256

    def body(dq_ref, dk_ref, dv_ref, dbg_ref, pre_ref, halo_ref, cw_ref, bd_ref, al_ref, dt_ref,
             dc_ref, dbd_ref, dcw_ref, dal_ref, ddt_ref):
        @pl.when(pl.program_id(0) == 0)
        def _():
            dcw_ref[...] = jnp.zeros_like(dcw_ref)
            dal_ref[...] = jnp.zeros_like(dal_ref)
            ddt_ref[...] = jnp.zeros_like(ddt_ref)

        halo = jnp.where(pl.program_id(0) > 0, halo_ref[...], 0.0)
        xc, c, sg, a = _dn_act(pre_ref[...], halo, cw_ref[...], tb)
        dsilu = sg * (1.0 + c * (1.0 - sg))
        for hh in range(HEADS):
            sl = slice(HEAD_DIM * hh, HEAD_DIM * (hh + 1))
            for base, g_ref, scale in ((0, dq_ref, Q_SCALE), (DN_WIDTH, dk_ref, 1.0)):
                sa = slice(base + HEAD_DIM * hh, base + HEAD_DIM * (hh + 1))
                raw = a[:, sa]
                r = lax.rsqrt(jnp.sum(raw * raw, axis=-1, keepdims=True) + EPS)
                nrm = raw * r
                gn_ = g_ref[:, sl] * scale
                dc_ref[:, sa] = r * (gn_ - nrm * jnp.sum(gn_ * nrm, axis=-1, keepdims=True)) * dsilu[:, sa]
        dc_ref[:, 2 * DN_WIDTH:] = dv_ref[...] * dsilu[:, 2 * DN_WIDTH:]
        dc = dc_ref[...]
        for j in range(4):
            dcw_ref[j:j + 1, :] += jnp.sum(dc * xc[5 + j:5 + j + tb, :], axis=0, keepdims=True)
        bdv = bd_ref[...]
        lane = lax.broadcasted_iota(jnp.int32, bdv.shape, 1)
        is_b = lane < HEADS
        dbg_in = dbg_ref[...]
        dbgv = jnp.where(is_b, dbg_in, _mm32(_chunk_cumsum_matrix(tb), dbg_in, TN))
        is_g = jnp.logical_and(lane >= HEADS, lane < 2 * HEADS)
        beta = _sigmoid(bdv)
        neg_a = -jnp.exp(al_ref[...])
        pre_sp = bdv + dt_ref[...]
        g = neg_a * _softplus(pre_sp)
        da_in = dbgv * neg_a * _sigmoid(pre_sp)
        dbd_ref[...] = jnp.where(is_b, dbgv * beta * (1.0 - beta), jnp.where(is_g, da_in, 0.0)).astype(BF16)
        _row_acc(dal_ref, jnp.where(is_g, dbgv * g, 0.0))
        _row_acc(ddt_ref, jnp.where(is_g, da_in, 0.0))

    tok = lambda w: pl.BlockSpec((tb, w), lambda i: (i, 0))
    full = lambda t: pl.BlockSpec(t.shape, lambda i: (0, 0))
    acc = lambda w: pl.BlockSpec((8, w), lambda i: (0, 0))
    return pl.pallas_call(
        body, name="dn_prep_bwd", grid=(T // tb,),
        in_specs=[tok(DN_WIDTH), tok(DN_WIDTH), tok(DN_WIDTH), tok(LANES),
                  tok(QKV), pl.BlockSpec((8, QKV), _before_halo(tb)), full(cw), tok(LANES), full(al_row), full(dt_row)],
        out_specs=[tok(QKV), tok(LANES), acc(QKV), acc(LANES), acc(LANES)],
        out_shape=[jax.ShapeDtypeStruct((T, QKV), F32), jax.ShapeDtypeStruct((T, LANES), BF16),
                   jax.ShapeDtypeStruct((8, QKV), F32), jax.ShapeDtypeStruct((8, LANES), F32),
                   jax.ShapeDtypeStruct((8, LANES), F32)],
        compiler_params=_params(("arbitrary",)),
    )(dq, dk, dv, dbg, qkv, qkv, cw, bd, al_row, dt_row)


def _dn_conv_bwd(dc, cw):
    T = dc.shape[0]
    tb = 512

    def body(dc_ref, halo_ref, w_ref, out_ref):
        last = pl.program_id(0) == pl.num_programs(0) - 1
        halo = jnp.where(last, 0.0, halo_ref[...])
        xc = jnp.concatenate([dc_ref[...], halo], axis=0)
        w = w_ref[...]
        acc = w[3:4, :] * xc[0:tb, :]
        for j in range(3):
            acc = acc + w[j:j + 1, :] * xc[3 - j:3 - j + tb, :]
        out_ref[...] = acc.astype(BF16)

    tok = pl.BlockSpec((tb, QKV), lambda i: (i, 0))
    return pl.pallas_call(
        body, name="dn_conv_bwd", grid=(T // tb,),
        in_specs=[tok, pl.BlockSpec((8, QKV), _after_halo(tb, T)), pl.BlockSpec(cw.shape, lambda i: (0, 0))],
        out_specs=tok,
        out_shape=jax.ShapeDtypeStruct((T, QKV), BF16),
        compiler_params=_params(("parallel",)),
    )(dc, dc, cw)


def _in_proj_bwd(dqkv, dz, dsc, dbd, dx1, x, g1, wa, wbd):
    T = x.shape[0]
    tb = 256

    def body(dqkv_ref, dz_ref, dsc_ref, dbd_ref, dx1_ref, x_ref, g_ref, wa_ref, wbd_ref, dx_ref, dg_ref):
        @pl.when(pl.program_id(0) == 0)
        def _():
            dg_ref[...] = jnp.zeros_like(dg_ref)

        dh = (_mm(dqkv_ref[...], wa_ref[:, :QKV], NT) + _mm(dz_ref[...], wa_ref[:, QKV:QKV + DN_WIDTH], NT)
              + _mm(dsc_ref[...], wa_ref[:, QKV + DN_WIDTH:], NT) + _mm(dbd_ref[...], wbd_ref[...], NT))
        xv = x_ref[...]
        r = lax.rsqrt(jnp.mean(xv * xv, axis=-1, keepdims=True) + EPS)
        xh = xv * r
        _row_acc(dg_ref, dh * xh)
        dx_ref[...] = dx1_ref[...] + _rms_bwd(dh, xh, r, g_ref[...])

    tok = lambda w: pl.BlockSpec((tb, w), lambda i: (i, 0))
    full = lambda t: pl.BlockSpec(t.shape, lambda i: (0, 0))
    return pl.pallas_call(
        body, name="in_proj_bwd", grid=(T // tb,),
        in_specs=[tok(QKV), tok(DN_WIDTH), tok(3 * SC_WIDTH), tok(LANES), tok(D_MODEL), tok(D_MODEL),
                  full(g1), full(wa), full(wbd)],
        out_specs=[tok(D_MODEL), pl.BlockSpec((8, D_MODEL), lambda i: (0, 0))],
        out_shape=[jax.ShapeDtypeStruct((T, D_MODEL), F32), jax.ShapeDtypeStruct((8, D_MODEL), F32)],
        compiler_params=_params(("arbitrary",)),
    )(dqkv, dz, dsc, dbd, dx1, x, g1, wa, wbd)


def _pad_rows(a, rows=8):
    return jnp.pad(a, ((0, rows - a.shape[0]), (0, 0)))


def _gate_rows(a_log, dt_bias):
    put = lambda t: jnp.pad(t.reshape(1, HEADS), ((0, 0), (HEADS, LANES - 2 * HEADS)))
    return put(a_log), put(dt_bias)


def _split_w_in(w_in):
    o = QKV + DN_WIDTH
    wa = jnp.concatenate([w_in[:, :o], w_in[:, o + 2 * HEADS:]], axis=1)
    wbd = jnp.pad(w_in[:, o:o + 2 * HEADS], ((0, 0), (0, LANES - 2 * HEADS)))
    return wa, wbd


def _layer_fwd(x, p):
    qkv, z, sc_in, bd, ht = _in_proj(x, p["g1"], p["wa"], p["wbd"])
    q, k, v, bg = _dn_prep(qkv, p["cw"], bd, p["al"], p["dt"])
    o, states = _delta_fwd(q, k, v, bg)
    x1, mt = _mix_out(o, z, sc_in, x, p["w_out"], p["gn"], p["scw"], p["gs"])
    x2, a, b, h2t = _ffn(x1, p["g2"], p["wg"], p["wu"], p["wd"])
    saved = dict(x=x, qkv=qkv, z=z, sc_in=sc_in, bd=bd, ht=ht, q=q, k=k, v=v, bg=bg, o=o, states=states,
                 x1=x1, mt=mt, a=a, b=b, h2t=h2t)
    return x2, saved


def _layer_bwd(dx2, s, p):
    dx1, da, db, act_t, dg2 = _ffn_bwd(dx2, s["x1"], s["a"], s["b"], p["g2"], p["wg"], p["wu"], p["wd"])
    g = {}
    g["wd"] = _wgrad(act_t, dx2.astype(BF16), 704, 1024, "wgrad_down")
    g["wg"] = _wgrad(s["h2t"], da, 512, 1408, "wgrad_gate")
    g["wu"] = _wgrad(s["h2t"], db, 512, 1408, "wgrad_up")
    do, dz, dgb, dcv, dgn, dgs, dscw = _mix_out_bwd(dx1, s["o"], s["z"], s["sc_in"], p["w_out"], p["gn"], p["scw"], p["gs"])
    g["w_out"] = _wgrad(s["mt"], dx1.astype(BF16), 512, 1024, "wgrad_out")
    dsc = _sc_conv_bwd(dcv, dgb, s["sc_in"], p["scw"])
    dq, dk, dv, dbg = _delta_bwd(s["q"], s["k"], s["v"], s["bg"], s["states"], do)
    dc, dbd, dcw, dal, ddt = _dn_prep_bwd(dq, dk, dv, dbg, s["qkv"], p["cw"], s["bd"], p["al"], p["dt"])
    dqkv = _dn_conv_bwd(dc, p["cw"])
    dx, dg1 = _in_proj_bwd(dqkv, dz, dsc, dbd, dx1, s["x"], p["g1"], p["wa"], p["wbd"])
    o = QKV + DN_WIDTH
    g["w_in"] = jnp.concatenate([
        _wgrad(s["ht"], dqkv, 512, 768, "wgrad_qkv"), _wgrad(s["ht"], dz, 512, 512, "wgrad_z"),
        _wgrad(s["ht"], dbd, 512, LANES, "wgrad_bd")[:, :2 * HEADS], _wgrad(s["ht"], dsc, 512, 768, "wgrad_sc")], axis=1)
    g.update(g1=dg1[0], g2=dg2[0], gn=dgn[0], gs=dgs[0], scw=dscw[:3], cw=dcw[:4],
             al=dal[0, HEADS:2 * HEADS], dt=ddt[0, HEADS:2 * HEADS])
    return dx, g


def _place():
    return lax.axis_index("x"), lax.axis_index("y"), lax.axis_index("c")


def _other_chips(x, y):
    return [(1 - x, y), (x, 1 - y), (1 - x, 1 - y)]


_HBM = pl.BlockSpec(memory_space=pltpu.HBM)


def _chip_exchange(arrs, name, gather):
    n = len(arrs)

    def body(*refs):
        ins, outs = refs[:n], refs[n:2 * n]
        send_sems, recv_sems, local_sems = refs[2 * n:]
        x, y, c = _place()
        me = 2 * x + y
        others = _other_chips(x, y)

        def remote(k, j, landing):
            px, py = others[j]
            src = ins[k] if gather else ins[k].at[2 * px + py]
            return pltpu.make_async_remote_copy(src_ref=src, dst_ref=outs[k].at[landing], send_sem=send_sems.at[k, j],
                                                recv_sem=recv_sems.at[k, j], device_id=(px, py, c), device_id_type=MESH)

        local = [pltpu.make_async_copy(ins[k] if gather else ins[k].at[me], outs[k].at[me], local_sems.at[k])
                 for k in range(n)]
        sends = [remote(k, j, me) for k in range(n) for j in range(3)]
        for cp in local + sends:
            cp.start()
        for k in range(n):
            for j, (px, py) in enumerate(others):
                remote(k, j, 2 * px + py).wait_recv()
        for cp in sends:
            cp.wait_send()
        for cp in local:
            cp.wait()

    shapes = [jax.ShapeDtypeStruct(((N_CHIPS,) + a.shape) if gather else a.shape, a.dtype) for a in arrs]
    return pl.pallas_call(
        body, name=name, in_specs=[_HBM] * n, out_specs=[_HBM] * n, out_shape=shapes,
        scratch_shapes=[pltpu.SemaphoreType.DMA((n, 3)), pltpu.SemaphoreType.DMA((n, 3)), pltpu.SemaphoreType.DMA((n,))],
    )(*arrs)


_SEM = pl.BlockSpec(memory_space=pltpu.SEMAPHORE)
_ANY = pl.BlockSpec(memory_space=pl.ANY)
_EFFECT = pltpu.SideEffectType.DATAFLOW_SIDE_EFFECTING


def _split_copies(src_ref, land_ref, send_sems, recv_sems, gather, sending):
    x, y, c = _place()
    me = 2 * x + y
    copies = []
    for j, (px, py) in enumerate(_other_chips(x, y)):
        peer = 2 * px + py
        copies.append(pltpu.make_async_remote_copy(
            src_ref=src_ref if gather else src_ref.at[peer], dst_ref=land_ref.at[me if sending else peer],
            send_sem=send_sems.at[j], recv_sem=recv_sems.at[j], device_id=(px, py, c), device_id_type=MESH))
    return copies


def _exchange_start(src, land, after, name, gather):
    def body(src_ref, land_ref, after_ref, send_sems, recv_sems, src_thru, land_thru, token):
        for cp in _split_copies(src_ref, land_ref, send_sems, recv_sems, gather, sending=True):
            cp.start()
        token[...] = jnp.zeros_like(token)

    hbm = lambda t: pltpu.with_memory_space_constraint(t, pltpu.HBM)
    return pl.pallas_call(
        body, name=name,
        out_shape=(pltpu.SemaphoreType.DMA((3,)), pltpu.SemaphoreType.DMA((3,)), pltpu.HBM(src.shape, src.dtype),
                   pltpu.HBM(land.shape, land.dtype), jax.ShapeDtypeStruct((8, LANES), F32)),
        in_specs=(_HBM, _HBM, _ANY), out_specs=(_SEM, _SEM, _HBM, _HBM, pl.BlockSpec(memory_space=pltpu.VMEM)),
        input_output_aliases={0: 2, 1: 3},
        compiler_params=pltpu.CompilerParams(has_side_effects=_EFFECT),
    )(hbm(src), hbm(land), after)


def _exchange_wait(started, after, name, gather):
    send_sems, recv_sems, src_thru, land_thru, _ = started

    def body(src_ref, land_ref, send_sems, recv_sems, after_ref, src_dead, got_ref):
        for cp in _split_copies(src_ref, land_ref, send_sems, recv_sems, gather, sending=False):
            cp.wait_send()
            cp.wait_recv()

    return pl.pallas_call(
        body, name=name,
        out_shape=(pltpu.HBM(src_thru.shape, src_thru.dtype), pltpu.HBM(land_thru.shape, land_thru.dtype)),
        in_specs=(_HBM, _HBM, _SEM, _SEM, _ANY), out_specs=(_HBM, _HBM), input_output_aliases={0: 0, 1: 1},
        compiler_params=pltpu.CompilerParams(has_side_effects=_EFFECT),
    )(src_thru, land_thru, send_sems, recv_sems, after)[1]


def _swap_sibling(arrs):
    n = len(arrs)

    def body(*refs):
        ins, outs = refs[:n], refs[n:2 * n]
        send_sems, recv_sems = refs[2 * n:]
        x, y, c = _place()
        copies = [pltpu.make_async_remote_copy(src_ref=ins[k], dst_ref=outs[k], send_sem=send_sems.at[k],
                                               recv_sem=recv_sems.at[k], device_id=(x, y, 1 - c), device_id_type=MESH)
                  for k in range(n)]
        for cp in copies:
            cp.start()
        for cp in copies:
            cp.wait()

    return pl.pallas_call(
        body, name="swap_sibling", in_specs=[_HBM] * n, out_specs=[_HBM] * n,
        out_shape=[jax.ShapeDtypeStruct(a.shape, a.dtype) for a in arrs],
        scratch_shapes=[pltpu.SemaphoreType.DMA((n,)), pltpu.SemaphoreType.DMA((n,))],
    )(*arrs)


def _all_reduce_small(v):
    rows = v.shape[0]
    flips = [(a, b, cc) for a in (0, 1) for b in (0, 1) for cc in (0, 1)][1:]

    def body(v_ref, out_ref, buf_ref, send_sems, recv_sems):
        x, y, c = _place()
        me = 4 * x + 2 * y + c
        peers = [((1 - x) if a else x, (1 - y) if b else y, (1 - c) if cc else c) for a, b, cc in flips]

        def copy(j, landing):
            return pltpu.make_async_remote_copy(src_ref=v_ref, dst_ref=buf_ref.at[landing], send_sem=send_sems.at[j],
                                                recv_sem=recv_sems.at[j], device_id=peers[j], device_id_type=MESH)

        sends = [copy(j, me) for j in range(N_DEV - 1)]
        for cp in sends:
            cp.start()
        buf_ref[me] = v_ref[...]
        for j, (px, py, pc) in enumerate(peers):
            copy(j, 4 * px + 2 * py + pc).wait_recv()
        for cp in sends:
            cp.wait_send()
        acc = buf_ref[0]
        for d in range(1, N_DEV):
            acc = acc + buf_ref[d]
        out_ref[...] = acc

    vmem = pl.BlockSpec(memory_space=pltpu.VMEM)
    return pl.pallas_call(
        body, name="all_reduce_small", in_specs=[vmem], out_specs=vmem,
        out_shape=jax.ShapeDtypeStruct(v.shape, F32),
        scratch_shapes=[pltpu.VMEM((N_DEV, rows, LANES), F32), pltpu.SemaphoreType.DMA((N_DEV - 1,)),
                        pltpu.SemaphoreType.DMA((N_DEV - 1,))],
    )(v)


def _sum_chips(parts):
    _, rows, cols = parts.shape
    tr = 64

    def body(p_ref, o_ref):
        acc = p_ref[0].astype(F32)
        for s in range(1, N_CHIPS):
            acc = acc + p_ref[s].astype(F32)
        o_ref[...] = acc

    return pl.pallas_call(
        body, name="sum_chips", grid=(rows // tr,),
        in_specs=[pl.BlockSpec((N_CHIPS, tr, cols), lambda i: (0, i, 0))],
        out_specs=pl.BlockSpec((tr, cols), lambda i: (i, 0)),
        out_shape=jax.ShapeDtypeStruct((rows, cols), F32),
        compiler_params=_params(("parallel",)),
    )(parts)


def _adamw(w, m, v, g_parts, name):
    rows, cols = w.shape
    tr = min(rows, 256)
    n = len(g_parts)
    c1 = 1.0 - ADAM_B1 ** ADAM_STEP
    c2 = 1.0 - ADAM_B2 ** ADAM_STEP

    def body(*refs):
        w_ref, m_ref, v_ref = refs[:3]
        g_refs = refs[3:3 + n]
        g_out, d_out, m_out, v_out = refs[3 + n:]
        g = g_refs[0][...]
        for r in g_refs[1:]:
            g = g + r[...]
        m_new = ADAM_B1 * m_ref[...] + (1.0 - ADAM_B1) * g
        v_new = ADAM_B2 * v_ref[...] + (1.0 - ADAM_B2) * (g * g)
        g_out[...] = g
        m_out[...] = m_new
        v_out[...] = v_new
        d_out[...] = -ADAM_LR * ((m_new / c1) / (jnp.sqrt(v_new / c2) + ADAM_EPS) + ADAM_WD * w_ref[...])

    blk = pl.BlockSpec((tr, cols), lambda i: (i, 0))
    return pl.pallas_call(
        body, name=name, grid=(rows // tr,),
        in_specs=[blk] * (3 + n), out_specs=[blk] * 4,
        out_shape=[jax.ShapeDtypeStruct((rows, cols), F32)] * 4,
        compiler_params=_params(("parallel",)),
    )(w, m, v, *g_parts)


def _pack(parts, rows, fill=0.0):
    flat = jnp.concatenate([p.reshape(-1) for p in parts])
    return jnp.pad(flat, (0, rows * LANES - flat.shape[0]), constant_values=fill).reshape(rows, LANES)


def _unpack(packed, shapes):
    flat = packed.reshape(-1)
    out, at = [], 0
    for shp in shapes:
        size = 1
        for s in shp:
            size *= s
        out.append(flat[at:at + size].reshape(shp))
        at += size
    return out


def _packed_rows(shapes):
    total = 0
    for shp in shapes:
        size = 1
        for s in shp:
            size *= s
        total += size
    return -(-total // (8 * LANES)) * 8


def _cols_full(g, l):
    t = g[:, l]
    return jnp.moveaxis(t, 0, 1).reshape(t.shape[1], N_CHIPS * t.shape[2])


IN_SHARD = W_IN_COLS // N_CHIPS
OUT_SHARD = D_MODEL // N_CHIPS
FF_SHARD = D_FF // N_CHIPS
PACK_AT = (0, D_MODEL, D_MODEL + OUT_SHARD, D_MODEL + OUT_SHARD + FF_SHARD, D_MODEL + OUT_SHARD + 2 * FF_SHARD,
           D_MODEL + OUT_SHARD + 3 * FF_SHARD)


def _pack_shard(w_in_s, w_out_s, wg_s, wu_s, wd_s):
    return jnp.concatenate([jnp.pad(w_in_s, ((0, 0), (0, D_MODEL - IN_SHARD))), w_out_s, wg_s.T, wu_s.T, wd_s],
                           axis=0).astype(BF16)


def _unpack_shard(t):
    part = lambda i: t[PACK_AT[i]:PACK_AT[i + 1]]
    return part(0)[:, :IN_SHARD], part(1), part(2).T, part(3).T, part(4)


def _unpack_full(land):
    part = lambda i: land[:, PACK_AT[i]:PACK_AT[i + 1]]
    cols = lambda t: jnp.transpose(t, (2, 0, 1)).reshape(t.shape[2], N_CHIPS * t.shape[1])
    w_in = jnp.moveaxis(part(0)[:, :, :IN_SHARD], 0, 1).reshape(D_MODEL, W_IN_COLS)
    return w_in, part(1).reshape(D_MODEL, D_MODEL), cols(part(2)), cols(part(3)), part(4).reshape(D_FF, D_MODEL)


def kernel(x, norm1_g, w_in, dn_conv_w, dn_a_log, dn_dt_bias, dn_norm_g, sc_conv_w, sc_norm_g, w_out, norm2_g, ffn_w_gate, ffn_w_up, ffn_w_down, final_norm_g, loss_target, m_norm1_g, m_w_in, m_dn_conv_w, m_dn_a_log, m_dn_dt_bias, m_dn_norm_g, m_sc_conv_w, m_sc_norm_g, m_w_out, m_norm2_g, m_ffn_w_gate, m_ffn_w_up, m_ffn_w_down, m_final_norm_g, v_norm1_g, v_w_in, v_dn_conv_w, v_dn_a_log, v_dn_dt_bias, v_dn_norm_g, v_sc_conv_w, v_sc_norm_g, v_w_out, v_norm2_g, v_ffn_w_gate, v_ffn_w_up, v_ffn_w_down, v_final_norm_g):
    chip = 2 * lax.axis_index("x") + lax.axis_index("y")

    g_cw, g_scw = _chip_exchange([dn_conv_w, sc_conv_w], "gather_conv", gather=True)

    packed = [_pack_shard(w_in[l], w_out[l], ffn_w_gate[l], ffn_w_up[l], ffn_w_down[l]) for l in range(DEPTH)]
    everywhere = lambda t: jnp.broadcast_to(t[None], (N_CHIPS,) + t.shape)
    zero_token = jnp.zeros((8, LANES), F32)
    started = _exchange_start(packed[0], everywhere(packed[0]), zero_token, "gather_start_0", gather=True)
    land = _exchange_wait(started, zero_token, "gather_wait_0", gather=True)

    act = x[0]
    layers, saved = [], []
    for l in range(DEPTH):
        hold = 0.0
        if l + 1 < DEPTH:
            started = _exchange_start(packed[l + 1], everywhere(packed[l + 1]), land, "gather_start_%d" % (l + 1), gather=True)
            hold = started[4][0:1, 0:1]
        w_in_l, w_out_l, wg_l, wu_l, wd_l = _unpack_full(land)
        wa, wbd = _split_w_in(w_in_l)
        al, dt = _gate_rows(dn_a_log[l], dn_dt_bias[l])
        layers.append(dict(
            g1=norm1_g[l][None] + hold, wa=wa, wbd=wbd, cw=_pad_rows(_cols_full(g_cw, l)), al=al, dt=dt,
            gn=dn_norm_g[l][None], scw=_pad_rows(_cols_full(g_scw, l)), gs=sc_norm_g[l][None],
            w_out=w_out_l, g2=norm2_g[l][None], wg=wg_l, wu=wu_l, wd=wd_l))
        act, s = _layer_fwd(act, layers[l])
        saved.append(s)
        if l + 1 < DEPTH:
            land = _exchange_wait(started, act, "gather_wait_%d" % (l + 1), gather=True)

    dact, loss_part, d_final = _loss_head(act, final_norm_g[None], loss_target[0])
    grads, reducing = [None] * DEPTH, [None] * DEPTH
    hold = 0.0
    for l in reversed(range(DEPTH)):
        dact, grads[l] = _layer_bwd(dact, saved[l], dict(layers[l], g2=layers[l]["g2"] + hold))
        g = grads[l]
        parts = jnp.stack([_pack_shard(
            g["w_in"][:, IN_SHARD * s:IN_SHARD * (s + 1)], g["w_out"][OUT_SHARD * s:OUT_SHARD * (s + 1)],
            g["wg"][:, FF_SHARD * s:FF_SHARD * (s + 1)], g["wu"][:, FF_SHARD * s:FF_SHARD * (s + 1)],
            g["wd"][FF_SHARD * s:FF_SHARD * (s + 1)]) for s in range(N_CHIPS)])
        reducing[l] = _exchange_start(parts, parts, zero_token, "reduce_start_%d" % l, gather=False)
        hold = reducing[l][4][0:1, 0:1]
    loss = lax.psum(loss_part[0, 0], ("x", "y", "c"))
    stack = lambda key: jnp.stack([grads[l][key] for l in range(DEPTH)])

    got = [_exchange_wait(reducing[l], dact, "reduce_wait_%d" % l, gather=False) for l in range(DEPTH)]
    mine = [_sum_chips(t) for t in got]
    theirs = _swap_sibling(mine)
    mine = [jnp.stack(ts) for ts in zip(*[_unpack_shard(t) for t in mine])]
    theirs = [jnp.stack(ts) for ts in zip(*[_unpack_shard(t) for t in theirs])]
    big = {}
    for key, w, m, v, a, b in zip(("w_in", "w_out", "ffn_w_gate", "ffn_w_up", "ffn_w_down"),
                                   (w_in, w_out, ffn_w_gate, ffn_w_up, ffn_w_down),
                                   (m_w_in, m_w_out, m_ffn_w_gate, m_ffn_w_up, m_ffn_w_down),
                                   (v_w_in, v_w_out, v_ffn_w_gate, v_ffn_w_up, v_ffn_w_down), mine, theirs):
        flat = lambda t: t.reshape(-1, t.shape[-1])
        big[key] = [o.reshape(w.shape) for o in _adamw(flat(w), flat(m), flat(v), [flat(a), flat(b)], "adamw_" + key)]

    full_shapes = [(DEPTH, D_MODEL), (DEPTH, D_MODEL), (DEPTH, HEAD_DIM), (DEPTH, SC_WIDTH), (DEPTH, HEADS),
                   (DEPTH, HEADS), (D_MODEL,), (DEPTH, 4, QKV), (DEPTH, 3, SC_WIDTH)]
    small_keys = ("g1", "g2", "gn", "gs", "al", "dt")
    packed = _pack([stack(k) for k in small_keys] + [d_final[0], stack("cw"), stack("scw")], _packed_rows(full_shapes))
    sg = _unpack(_all_reduce_small(packed), full_shapes)
    sg[7] = lax.dynamic_slice_in_dim(sg[7], chip * (QKV // N_CHIPS), QKV // N_CHIPS, axis=2)
    sg[8] = lax.dynamic_slice_in_dim(sg[8], chip * (SC_WIDTH // N_CHIPS), SC_WIDTH // N_CHIPS, axis=2)
    small_names = ("norm1_g", "norm2_g", "dn_norm_g", "sc_norm_g", "dn_a_log", "dn_dt_bias", "final_norm_g",
                   "dn_conv_w", "sc_conv_w")
    sw = (norm1_g, norm2_g, dn_norm_g, sc_norm_g, dn_a_log, dn_dt_bias, final_norm_g, dn_conv_w, sc_conv_w)
    sm = (m_norm1_g, m_norm2_g, m_dn_norm_g, m_sc_norm_g, m_dn_a_log, m_dn_dt_bias, m_final_norm_g, m_dn_conv_w, m_sc_conv_w)
    sv = (v_norm1_g, v_norm2_g, v_dn_norm_g, v_sc_norm_g, v_dn_a_log, v_dn_dt_bias, v_final_norm_g, v_dn_conv_w, v_sc_conv_w)
    shard_shapes = [t.shape for t in sw]
    rows = _packed_rows(shard_shapes)
    outs = _adamw(_pack(sw, rows), _pack(sm, rows), _pack(sv, rows, fill=1.0), [_pack(sg, rows)], "adamw_small")
    small = {name: [] for name in small_names}
    for o in outs:
        for name, t in zip(small_names, _unpack(o, shard_shapes)):
            small[name].append(t)

    order = ("norm1_g", "w_in", "dn_conv_w", "dn_a_log", "dn_dt_bias", "dn_norm_g", "sc_conv_w", "sc_norm_g", "w_out",
             "norm2_g", "ffn_w_gate", "ffn_w_up", "ffn_w_down", "final_norm_g")
    result = {**big, **small}
    return (loss, dact[None], *[result[n][0] for n in order], *[result[n][1] for n in order],
            *[result[n][2] for n in order], *[result[n][3] for n in order])
```

```python
import jax
import jax.numpy as jnp
from jax import lax
from jax.experimental import pallas as pl
from jax.experimental.pallas import tpu as pltpu

F32 = jnp.float32
BF16 = jnp.bfloat16
MESH = pl.DeviceIdType.MESH

D_MODEL = 1024
DEPTH = 4
HEADS = 4
HEAD_DIM = 128
DN_WIDTH = HEADS * HEAD_DIM
SC_WIDTH = 512
SC_GROUPS = 4
D_FF = 2816
CHUNK = 64
QKV = 3 * DN_WIDTH
W_IN_COLS = 4 * DN_WIDTH + 2 * HEADS + 3 * SC_WIDTH
WA_COLS = QKV + DN_WIDTH + 3 * SC_WIDTH
LANES = 128
EPS = 1e-6
Q_SCALE = HEAD_DIM ** -0.5
N_CHIPS = 4
N_DEV = 8
IN_SHARD = W_IN_COLS // N_CHIPS
OUT_SHARD = D_MODEL // N_CHIPS
FF_SHARD = D_FF // N_CHIPS
A_OUT_AT = D_MODEL
A_ROWS = D_MODEL + OUT_SHARD
B_ROWS = 3 * FF_SHARD

ADAM_LR = 0.001
ADAM_B1 = 0.9
ADAM_B2 = 0.999
ADAM_EPS = 1e-08
ADAM_WD = 0.01
ADAM_STEP = 10

VMEM_LIMIT = 56 * 1024 * 1024

NN = (((1,), (0,)), ((), ()))
NT = (((1,), (1,)), ((), ()))
TN = (((0,), (0,)), ((), ()))


def _mm(a, b, dims=NN):
    return lax.dot_general(a.astype(BF16), b.astype(BF16), dims, preferred_element_type=F32)


def _mm32(a, b, dims=NN):
    return lax.dot_general(a, b, dims, preferred_element_type=F32, precision=lax.Precision.HIGHEST)


def _params(sem, vmem=VMEM_LIMIT):
    return pltpu.CompilerParams(dimension_semantics=sem, vmem_limit_bytes=vmem)


def _sigmoid(x):
    return 1.0 / (1.0 + jnp.exp(-x))


def _softplus(x):
    return jnp.maximum(x, 0.0) + jnp.log1p(jnp.exp(-jnp.abs(x)))


def _row_acc(acc_ref, val):
    acc_ref[0:1, :] += jnp.sum(val, axis=0, keepdims=True)


def _rms_bwd(dh, xh, r, gain):
    dxh = dh * gain
    return r * (dxh - xh * jnp.mean(dxh * xh, axis=-1, keepdims=True))


def _before_halo(tb):
    return lambda i: (jnp.maximum(i * (tb // 8) - 1, 0), 0)


def _after_halo(tb, n_rows):
    last = n_rows // 8 - 1
    return lambda i: (jnp.minimum((i + 1) * (tb // 8), last), 0)


def _taps(xc, w, n_taps, tb, first):
    out = w[0:1, :] * xc[first:first + tb, :]
    for j in range(1, n_taps):
        out = out + w[j:j + 1, :] * xc[first + j:first + j + tb, :]
    return out


def _in_proj(x, g1, wa, wbd):
    T = x.shape[0]
    tb = 256

    def body(x_ref, g_ref, wa_ref, wbd_ref, qkv_ref, z_ref, sc_ref, bd_ref, ht_ref):
        xv = x_ref[...]
        r = lax.rsqrt(jnp.mean(xv * xv, axis=-1, keepdims=True) + EPS)
        h = (xv * r * g_ref[...]).astype(BF16)
        p = jnp.dot(h, wa_ref[...], preferred_element_type=F32)
        qkv_ref[...] = p[:, :QKV]
        z_ref[...] = p[:, QKV:QKV + DN_WIDTH]
        sc_ref[...] = p[:, QKV + DN_WIDTH:]
        bd_ref[...] = jnp.dot(h, wbd_ref[...], preferred_element_type=F32)
        ht_ref[...] = h.T

    tok = lambda w: pl.BlockSpec((tb, w), lambda i: (i, 0))
    full = lambda a: pl.BlockSpec(a.shape, lambda i: (0, 0))
    return pl.pallas_call(
        body, name="in_proj", grid=(T // tb,),
        in_specs=[tok(D_MODEL), full(g1), full(wa), full(wbd)],
        out_specs=[tok(QKV), tok(DN_WIDTH), tok(3 * SC_WIDTH), tok(LANES),
                   pl.BlockSpec((D_MODEL, tb), lambda i: (0, i))],
        out_shape=[jax.ShapeDtypeStruct((T, QKV), F32), jax.ShapeDtypeStruct((T, DN_WIDTH), F32),
                   jax.ShapeDtypeStruct((T, 3 * SC_WIDTH), F32), jax.ShapeDtypeStruct((T, LANES), F32),
                   jax.ShapeDtypeStruct((D_MODEL, T), BF16)],
        compiler_params=_params(("parallel",)),
    )(x, g1, wa, wbd)


def _dn_act(pre, halo, cw, tb):
    xc = jnp.concatenate([halo, pre], axis=0)
    c = _taps(xc, cw, 4, tb, 5)
    sg = _sigmoid(c)
    return xc, c, sg, c * sg


def _gates(bd, al_row, dt_row):
    lane = lax.broadcasted_iota(jnp.int32, bd.shape, 1)
    beta = _sigmoid(bd)
    g = -jnp.exp(al_row) * _softplus(bd + dt_row)
    return jnp.where(lane < HEADS, beta, jnp.where(lane < 2 * HEADS, g, 0.0))


def _dn_prep(qkv, cw, bd, al_row, dt_row):
    T = qkv.shape[0]
    tb = 512

    def body(pre_ref, halo_ref, cw_ref, bd_ref, al_ref, dt_ref, q_ref, k_ref, v_ref, bg_ref):
        halo = jnp.where(pl.program_id(0) > 0, halo_ref[...], 0.0)
        _, _, _, a = _dn_act(pre_ref[...], halo, cw_ref[...], tb)
        for hh in range(HEADS):
            sl = slice(HEAD_DIM * hh, HEAD_DIM * (hh + 1))
            qs = a[:, sl]
            q_ref[:, sl] = qs * (lax.rsqrt(jnp.sum(qs * qs, axis=-1, keepdims=True) + EPS) * Q_SCALE)
            ks = a[:, DN_WIDTH + HEAD_DIM * hh:DN_WIDTH + HEAD_DIM * (hh + 1)]
            k_ref[:, sl] = ks * lax.rsqrt(jnp.sum(ks * ks, axis=-1, keepdims=True) + EPS)
        v_ref[...] = a[:, 2 * DN_WIDTH:]
        gates = _gates(bd_ref[...], al_ref[...], dt_ref[...])
        lane = lax.broadcasted_iota(jnp.int32, gates.shape, 1)
        bg_ref[...] = jnp.where(lane < HEADS, gates, _mm32(_chunk_cumsum_matrix(tb), gates))

    tok = lambda w: pl.BlockSpec((tb, w), lambda i: (i, 0))
    full = lambda a: pl.BlockSpec(a.shape, lambda i: (0, 0))
    return pl.pallas_call(
        body, name="dn_prep", grid=(T // tb,),
        in_specs=[tok(QKV), pl.BlockSpec((8, QKV), _before_halo(tb)), full(cw), tok(LANES), full(al_row), full(dt_row)],
        out_specs=[tok(DN_WIDTH), tok(DN_WIDTH), tok(DN_WIDTH), tok(LANES)],
        out_shape=[jax.ShapeDtypeStruct((T, DN_WIDTH), F32)] * 3 + [jax.ShapeDtypeStruct((T, LANES), F32)],
        compiler_params=_params(("parallel",)),
    )(qkv, qkv, cw, bd, al_row, dt_row)


def _chunk_masks():
    row = lax.broadcasted_iota(jnp.int32, (CHUNK, CHUNK), 0)
    col = lax.broadcasted_iota(jnp.int32, (CHUNK, CHUNK), 1)
    return row >= col, row > col


def _chunk_cumsum_matrix(n):
    row = lax.broadcasted_iota(jnp.int32, (n, n), 0)
    col = lax.broadcasted_iota(jnp.int32, (n, n), 1)
    return jnp.logical_and(row >= col, row // CHUNK == col // CHUNK).astype(F32)


def _chunk_units(q_ref, k_ref, v_ref, bg_ref, rows):
    bgc = bg_ref[rows, :]
    bg_t = bgc.T
    qv, kv, vv = q_ref[rows, :], k_ref[rows, :], v_ref[rows, :]
    units = []
    for h in range(HEADS):
        sl = slice(HEAD_DIM * h, HEAD_DIM * (h + 1))
        units.append((qv[:, sl], kv[:, sl], vv[:, sl], bgc[:, h:h + 1], bgc[:, HEADS + h:HEADS + h + 1],
                      bg_t[HEADS + h:HEADS + h + 1, :]))
    return units


def _units_local(units, masks):
    causal, strict = masks
    pre = []
    for q, k, v, beta, gc, gr in units:
        kb = k * beta
        eg = jnp.exp(gc)
        g_last = gc[CHUNK - 1:CHUNK, :]
        ek = jnp.exp(g_last - gc)
        pre.append(dict(q=q, k=k, v=v, beta=beta, decay=jnp.exp(jnp.where(causal, gc - gr, -1e30)), kb=kb, vb=v * beta,
                        eg=eg, kbg=kb * eg, ek=ek, gl=jnp.exp(g_last), q_dec=q * eg, k_dec=k * ek))
    both = [_mm(jnp.concatenate([p["kb"], p["q"]], axis=0), p["k"], NT) for p in pre]
    for p, b in zip(pre, both):
        p["low"] = jnp.where(strict, b[:CHUNK] * p["decay"], 0.0)
        p["qk"] = jnp.where(causal, b[CHUNK:] * p["decay"], 0.0)
    xs = [-p["low"] for p in pre]
    pw = [_mm(p["low"], p["low"]) for p in pre]
    for _ in range(4):
        both = [_mm(jnp.concatenate([pp, x], axis=0), pp) for pp, x in zip(pw, xs)]
        xs = [x + pp + b[CHUNK:] for x, pp, b in zip(xs, pw, both)]
        pw = [b[:CHUNK] for b in both]
    last = [_mm(x, pp) for x, pp in zip(xs, pw)]
    xs = [x + pp + b for x, pp, b in zip(xs, pw, last)]
    uw = [_mm(x, jnp.concatenate([p["vb"], p["kbg"]], axis=1)) for x, p in zip(xs, pre)]
    for p, x, b in zip(pre, xs, uw):
        p["xm"] = x
        p["u"] = p["vb"] + b[:, :HEAD_DIM]
        p["w"] = p["kbg"] + b[:, HEAD_DIM:]
    return pre


def _delta_fwd(q, k, v, bg):
    T = q.shape[0]
    tb = 512
    n_chunk = tb // CHUNK

    def body(q_ref, k_ref, v_ref, bg_ref, o_ref, st_ref, s_ref):
        @pl.when(pl.program_id(0) == 0)
        def _():
            s_ref[...] = jnp.zeros_like(s_ref)

        masks = _chunk_masks()

        def pair(pi, carry):
            rows = [pl.ds(pl.multiple_of((2 * pi + j) * CHUNK, CHUNK), CHUNK) for j in range(2)]
            loc = _units_local(_chunk_units(q_ref, k_ref, v_ref, bg_ref, rows[0])
                               + _chunk_units(q_ref, k_ref, v_ref, bg_ref, rows[1]), masks)
            states = [s_ref[h] for h in range(HEADS)]
            for j in range(2):
                lj = loc[HEADS * j:HEADS * (j + 1)]
                ws = [_mm(jnp.concatenate([p["w"], p["q_dec"]], axis=0), s) for p, s in zip(lj, states)]
                v_new = [p["u"] - b[:CHUNK] for p, b in zip(lj, ws)]
                intra = [_mm(p["qk"], vn) for p, vn in zip(lj, v_new)]
                upd = [_mm(p["k_dec"], vn, TN) for p, vn in zip(lj, v_new)]
                o_ref[rows[j], :] = jnp.concatenate([b[CHUNK:] + a for b, a in zip(ws, intra)], axis=1)
                for h in range(HEADS):
                    st_ref[2 * pi + j, h] = states[h]
                states = [p["gl"] * s + d for p, s, d in zip(lj, states, upd)]
            for h in range(HEADS):
                s_ref[h] = states[h]
            return carry

        lax.fori_loop(0, n_chunk // 2, pair, 0)

    tok = lambda w: pl.BlockSpec((tb, w), lambda i: (i, 0))
    return pl.pallas_call(
        body, name="delta_fwd", grid=(T // tb,),
        in_specs=[tok(DN_WIDTH), tok(DN_WIDTH), tok(DN_WIDTH), tok(LANES)],
        out_specs=[tok(DN_WIDTH), pl.BlockSpec((n_chunk, HEADS, HEAD_DIM, HEAD_DIM), lambda i: (i, 0, 0, 0))],
        out_shape=[jax.ShapeDtypeStruct((T, DN_WIDTH), F32),
                   jax.ShapeDtypeStruct((T // CHUNK, HEADS, HEAD_DIM, HEAD_DIM), F32)],
        scratch_shapes=[pltpu.VMEM((HEADS, HEAD_DIM, HEAD_DIM), F32)],
        compiler_params=_params(("arbitrary",)),
    )(q, k, v, bg)


def _dn_out(o, z, gn):
    outs, ohs, rs = [], [], []
    for hh in range(HEADS):
        oh = o[:, HEAD_DIM * hh:HEAD_DIM * (hh + 1)]
        r = lax.rsqrt(jnp.mean(oh * oh, axis=-1, keepdims=True) + EPS)
        ohs.append(oh * r)
        rs.append(r)
    sz = _sigmoid(z)
    oh = jnp.concatenate(ohs, axis=1)
    gn4 = jnp.concatenate([gn] * HEADS, axis=1)
    return oh * gn4 * (z * sz), oh, rs, sz, gn4


def _sc_fwd(sc_in, halo, cw, tb):
    xc = jnp.concatenate([halo, sc_in], axis=0)
    u = xc[:, SC_WIDTH:2 * SC_WIDTH] * xc[:, 2 * SC_WIDTH:]
    cv = _taps(u, cw, 3, tb, 6)
    gate_b = sc_in[:, :SC_WIDTH]
    y = gate_b * cv
    gw = SC_WIDTH // SC_GROUPS
    yhs, rs = [], []
    for gi in range(SC_GROUPS):
        yg = y[:, gw * gi:gw * (gi + 1)]
        r = lax.rsqrt(jnp.mean(yg * yg, axis=-1, keepdims=True) + EPS)
        yhs.append(yg * r)
        rs.append(r)
    return u, cv, gate_b, jnp.concatenate(yhs, axis=1), rs


def _shard_rows(land, first, rows):
    assert first % rows == 0 and land.shape[0] == N_CHIPS
    return pl.BlockSpec((N_CHIPS, rows, land.shape[2]), lambda i: (0, first // rows, 0))


def _whole(w_ref):
    n, rows, cols = w_ref.shape
    return w_ref[...].reshape(n * rows, cols)


def _mix_out(o, z, sc_in, x, land_a, gn, scw, gs):
    T = x.shape[0]
    tb = 256

    def body(o_ref, z_ref, sc_ref, halo_ref, x_ref, w_ref, gn_ref, scw_ref, gs_ref, x1_ref, mt_ref):
        o_n = _dn_out(o_ref[...], z_ref[...], gn_ref[...])[0]
        halo = jnp.where(pl.program_id(0) > 0, halo_ref[...], 0.0)
        yh = _sc_fwd(sc_ref[...], halo, scw_ref[...], tb)[3]
        mix = jnp.concatenate([o_n, yh * gs_ref[...]], axis=1).astype(BF16)
        x1_ref[...] = x_ref[...] + jnp.dot(mix, _whole(w_ref), preferred_element_type=F32)
        mt_ref[...] = mix.T

    tok = lambda w: pl.BlockSpec((tb, w), lambda i: (i, 0))
    full = lambda a: pl.BlockSpec(a.shape, lambda i: (0, 0))
    return pl.pallas_call(
        body, name="mix_out", grid=(T // tb,),
        in_specs=[tok(DN_WIDTH), tok(DN_WIDTH), tok(3 * SC_WIDTH), pl.BlockSpec((8, 3 * SC_WIDTH), _before_halo(tb)),
                  tok(D_MODEL), _shard_rows(land_a, A_OUT_AT, OUT_SHARD), full(gn), full(scw), full(gs)],
        out_specs=[tok(D_MODEL), pl.BlockSpec((D_MODEL, tb), lambda i: (0, i))],
        out_shape=[jax.ShapeDtypeStruct((T, D_MODEL), F32), jax.ShapeDtypeStruct((D_MODEL, T), BF16)],
        compiler_params=_params(("parallel",)),
    )(o, z, sc_in, sc_in, x, land_a, gn, scw, gs)


def _ffn(x1, g2, land_b):
    T = x1.shape[0]
    tb = 256

    def body(x_ref, g_ref, wgt_ref, wut_ref, wd_ref, x2_ref, a_ref, b_ref, h_ref):
        xv = x_ref[...]
        r = lax.rsqrt(jnp.mean(xv * xv, axis=-1, keepdims=True) + EPS)
        h = (xv * r * g_ref[...]).astype(BF16)
        a = lax.dot_general(h, _whole(wgt_ref), NT, preferred_element_type=F32)
        b = lax.dot_general(h, _whole(wut_ref), NT, preferred_element_type=F32)
        act = (a * _sigmoid(a) * b).astype(BF16)
        x2_ref[...] = xv + jnp.dot(act, _whole(wd_ref), preferred_element_type=F32)
        a_ref[...] = a.astype(BF16)
        b_ref[...] = b.astype(BF16)
        h_ref[...] = h

    tok = lambda w: pl.BlockSpec((tb, w), lambda i: (i, 0))
    return pl.pallas_call(
        body, name="ffn", grid=(T // tb,),
        in_specs=[tok(D_MODEL), pl.BlockSpec(g2.shape, lambda i: (0, 0)), _shard_rows(land_b, 0, FF_SHARD),
                  _shard_rows(land_b, FF_SHARD, FF_SHARD), _shard_rows(land_b, 2 * FF_SHARD, FF_SHARD)],
        out_specs=[tok(D_MODEL), tok(D_FF), tok(D_FF), tok(D_MODEL)],
        out_shape=[jax.ShapeDtypeStruct((T, D_MODEL), F32), jax.ShapeDtypeStruct((T, D_FF), BF16),
                   jax.ShapeDtypeStruct((T, D_FF), BF16), jax.ShapeDtypeStruct((T, D_MODEL), BF16)],
        compiler_params=_params(("parallel",)),
    )(x1, g2, land_b, land_b, land_b)


def _loss_head(x, gf, target):
    T = x.shape[0]
    tb = 512

    def body(x_ref, g_ref, t_ref, dx_ref, dxb_ref, loss_ref, dg_ref):
        @pl.when(pl.program_id(0) == 0)
        def _():
            loss_ref[...] = jnp.zeros_like(loss_ref)
            dg_ref[...] = jnp.zeros_like(dg_ref)

        xv = x_ref[...]
        r = lax.rsqrt(jnp.mean(xv * xv, axis=-1, keepdims=True) + EPS)
        xh = xv * r
        err = xh * g_ref[...] - t_ref[...]
        per_tok = jnp.mean(err * err, axis=-1, keepdims=True)
        loss_ref[...] += 0.5 * jnp.sum(per_tok, axis=0, keepdims=True)
        dy = err * (1.0 / D_MODEL)
        _row_acc(dg_ref, dy * xh)
        dx = _rms_bwd(dy, xh, r, g_ref[...])
        dx_ref[...] = dx
        dxb_ref[...] = dx.astype(BF16)

    tok = pl.BlockSpec((tb, D_MODEL), lambda i: (i, 0))
    return pl.pallas_call(
        body, name="loss_head", grid=(T // tb,),
        in_specs=[tok, pl.BlockSpec(gf.shape, lambda i: (0, 0)), tok],
        out_specs=[tok, tok, pl.BlockSpec((8, LANES), lambda i: (0, 0)), pl.BlockSpec((8, D_MODEL), lambda i: (0, 0))],
        out_shape=[jax.ShapeDtypeStruct((T, D_MODEL), F32), jax.ShapeDtypeStruct((T, D_MODEL), BF16),
                   jax.ShapeDtypeStruct((8, LANES), F32), jax.ShapeDtypeStruct((8, D_MODEL), F32)],
        compiler_params=_params(("arbitrary",)),
    )(x, gf, target)


def _ffn_bwd(dx2, x1, a, b, g2, land_b):
    T = x1.shape[0]
    tb = 256

    def body(dx2_ref, x_ref, a_ref, b_ref, g_ref, wgt_ref, wut_ref, wd_ref,
             dx1_ref, dx1b_ref, dat_ref, dbt_ref, at_ref, dg_ref):
        @pl.when(pl.program_id(0) == 0)
        def _():
            dg_ref[...] = jnp.zeros_like(dg_ref)

        dx2v = dx2_ref[...]
        av = a_ref[...].astype(F32)
        bv = b_ref[...].astype(F32)
        dact = _mm(dx2v, _whole(wd_ref), NT)
        sa = _sigmoid(av)
        silu = av * sa
        da = (dact * bv * (sa * (1.0 + av * (1.0 - sa)))).astype(BF16)
        db = (dact * silu).astype(BF16)
        dh = _mm(da, _whole(wgt_ref)) + _mm(db, _whole(wut_ref))
        xv = x_ref[...]
        r = lax.rsqrt(jnp.mean(xv * xv, axis=-1, keepdims=True) + EPS)
        xh = xv * r
        _row_acc(dg_ref, dh * xh)
        dx1 = dx2v + _rms_bwd(dh, xh, r, g_ref[...])
        dx1_ref[...] = dx1
        dx1b_ref[...] = dx1.astype(BF16)
        dat_ref[...] = da.T
        dbt_ref[...] = db.T
        at_ref[...] = (silu * bv).astype(BF16).T

    tok = lambda w: pl.BlockSpec((tb, w), lambda i: (i, 0))
    tr = pl.BlockSpec((D_FF, tb), lambda i: (0, i))
    return pl.pallas_call(
        body, name="ffn_bwd", grid=(T // tb,),
        in_specs=[tok(D_MODEL), tok(D_MODEL), tok(D_FF), tok(D_FF), pl.BlockSpec(g2.shape, lambda i: (0, 0)),
                  _shard_rows(land_b, 0, FF_SHARD), _shard_rows(land_b, FF_SHARD, FF_SHARD),
                  _shard_rows(land_b, 2 * FF_SHARD, FF_SHARD)],
        out_specs=[tok(D_MODEL), tok(D_MODEL), tr, tr, tr, pl.BlockSpec((8, D_MODEL), lambda i: (0, 0))],
        out_shape=[jax.ShapeDtypeStruct((T, D_MODEL), F32), jax.ShapeDtypeStruct((T, D_MODEL), BF16)]
        + [jax.ShapeDtypeStruct((D_FF, T), BF16)] * 3 + [jax.ShapeDtypeStruct((8, D_MODEL), F32)],
        compiler_params=_params(("arbitrary",)),
    )(dx2, x1, a, b, g2, land_b, land_b, land_b)


def _wgrad(at, b, bm, bn, name):
    M, T = at.shape
    N = b.shape[1]
    bk = min(T, 1024)

    def body(a_ref, b_ref, o_ref):
        @pl.when(pl.program_id(2) == 0)
        def _():
            o_ref[...] = jnp.zeros_like(o_ref)

        o_ref[...] += jnp.dot(a_ref[...], b_ref[...], preferred_element_type=F32)

    return pl.pallas_call(
        body, name=name, grid=(M // bm, N // bn, T // bk),
        in_specs=[pl.BlockSpec((bm, bk), lambda i, j, kk: (i, kk)), pl.BlockSpec((bk, bn), lambda i, j, kk: (kk, j))],
        out_specs=pl.BlockSpec((bm, bn), lambda i, j, kk: (i, j)),
        out_shape=jax.ShapeDtypeStruct((M, N), F32),
        compiler_params=_params(("parallel", "parallel", "arbitrary")),
    )(at, b)


def _mix_out_bwd(dx1, o, z, sc_in, land_a, gn, scw, gs):
    T = dx1.shape[0]
    tb = 256

    def body(dx_ref, o_ref, z_ref, sc_ref, halo_ref, w_ref, gn_ref, scw_ref, gs_ref,
             do_ref, dz_ref, dgb_ref, dcv_ref, dgn_ref, dgs_ref, dscw_ref):
        @pl.when(pl.program_id(0) == 0)
        def _():
            dgn_ref[...] = jnp.zeros_like(dgn_ref)
            dgs_ref[...] = jnp.zeros_like(dgs_ref)
            dscw_ref[...] = jnp.zeros_like(dscw_ref)

        dmix = _mm(dx_ref[...], _whole(w_ref), NT)
        don = dmix[:, :DN_WIDTH]
        dosc = dmix[:, DN_WIDTH:]
        zv = z_ref[...]
        _, oh, rs, sz, gn4 = _dn_out(o_ref[...], zv, gn_ref[...])
        silu_z = zv * sz
        dgn_full = don * oh * silu_z
        dgn_ref[0:1, :] += jnp.sum(sum(dgn_full[:, HEAD_DIM * hh:HEAD_DIM * (hh + 1)] for hh in range(HEADS)),
                                   axis=0, keepdims=True)
        dz_ref[...] = (don * oh * gn4 * (sz * (1.0 + zv * (1.0 - sz)))).astype(BF16)
        t = don * gn4 * silu_z
        for hh in range(HEADS):
            sl = slice(HEAD_DIM * hh, HEAD_DIM * (hh + 1))
            th, ohh = t[:, sl], oh[:, sl]
            do_ref[:, sl] = rs[hh] * (th - ohh * jnp.mean(th * ohh, axis=-1, keepdims=True))
        halo = jnp.where(pl.program_id(0) > 0, halo_ref[...], 0.0)
        u, cv, gate_b, yh, rys = _sc_fwd(sc_ref[...], halo, scw_ref[...], tb)
        _row_acc(dgs_ref, dosc * yh)
        ty = dosc * gs_ref[...]
        gw = SC_WIDTH // SC_GROUPS
        dys = []
        for gi in range(SC_GROUPS):
            sl = slice(gw * gi, gw * (gi + 1))
            tg, yg = ty[:, sl], yh[:, sl]
            dys.append(rys[gi] * (tg - yg * jnp.mean(tg * yg, axis=-1, keepdims=True)))
        dy = jnp.concatenate(dys, axis=1)
        dgb_ref[...] = dy * cv
        dcv = dy * gate_b
        dcv_ref[...] = dcv
        for j in range(3):
            dscw_ref[j:j + 1, :] += jnp.sum(dcv * u[6 + j:6 + j + tb, :], axis=0, keepdims=True)

    tok = lambda w: pl.BlockSpec((tb, w), lambda i: (i, 0))
    full = lambda t: pl.BlockSpec(t.shape, lambda i: (0, 0))
    acc = lambda w: pl.BlockSpec((8, w), lambda i: (0, 0))
    return pl.pallas_call(
        body, name="mix_out_bwd", grid=(T // tb,),
        in_specs=[tok(D_MODEL), tok(DN_WIDTH), tok(DN_WIDTH), tok(3 * SC_WIDTH),
                  pl.BlockSpec((8, 3 * SC_WIDTH), _before_halo(tb)), _shard_rows(land_a, A_OUT_AT, OUT_SHARD),
                  full(gn), full(scw), full(gs)],
        out_specs=[tok(DN_WIDTH), tok(DN_WIDTH), tok(SC_WIDTH), tok(SC_WIDTH), acc(HEAD_DIM), acc(SC_WIDTH), acc(SC_WIDTH)],
        out_shape=[jax.ShapeDtypeStruct((T, DN_WIDTH), F32), jax.ShapeDtypeStruct((T, DN_WIDTH), BF16),
                   jax.ShapeDtypeStruct((T, SC_WIDTH), F32), jax.ShapeDtypeStruct((T, SC_WIDTH), F32),
                   jax.ShapeDtypeStruct((8, HEAD_DIM), F32), jax.ShapeDtypeStruct((8, SC_WIDTH), F32),
                   jax.ShapeDtypeStruct((8, SC_WIDTH), F32)],
        compiler_params=_params(("arbitrary",)),
    )(dx1, o, z, sc_in, sc_in, land_a, gn, scw, gs)


def _sc_conv_bwd(dcv, dgb, sc_in, scw):
    T = dcv.shape[0]
    tb = 512

    def body(dcv_ref, halo_ref, dgb_ref, sc_ref, w_ref, out_ref):
        last = pl.program_id(0) == pl.num_programs(0) - 1
        halo = jnp.where(last, 0.0, halo_ref[...])
        xc = jnp.concatenate([dcv_ref[...], halo], axis=0)
        w = w_ref[...]
        du = w[2:3, :] * xc[0:tb, :] + w[1:2, :] * xc[1:tb + 1, :] + w[0:1, :] * xc[2:tb + 2, :]
        sc = sc_ref[...]
        out_ref[:, :SC_WIDTH] = dgb_ref[...].astype(BF16)
        out_ref[:, SC_WIDTH:2 * SC_WIDTH] = (du * sc[:, 2 * SC_WIDTH:]).astype(BF16)
        out_ref[:, 2 * SC_WIDTH:] = (du * sc[:, SC_WIDTH:2 * SC_WIDTH]).astype(BF16)

    tok = lambda w: pl.BlockSpec((tb, w), lambda i: (i, 0))
    return pl.pallas_call(
        body, name="sc_conv_bwd", grid=(T // tb,),
        in_specs=[tok(SC_WIDTH), pl.BlockSpec((8, SC_WIDTH), _after_halo(tb, T)), tok(SC_WIDTH), tok(3 * SC_WIDTH),
                  pl.BlockSpec(scw.shape, lambda i: (0, 0))],
        out_specs=tok(3 * SC_WIDTH),
        out_shape=jax.ShapeDtypeStruct((T, 3 * SC_WIDTH), BF16),
        compiler_params=_params(("parallel",)),
    )(dcv, dcv, dgb, sc_in, scw)


def _delta_bwd(q, k, v, bg, states, do):
    T = q.shape[0]
    tb = 512
    n_chunk = tb // CHUNK
    nb = T // tb

    def body(q_ref, k_ref, v_ref, bg_ref, st_ref, do_ref, dq_ref, dk_ref, dv_ref, dbg_ref, ds_ref):
        @pl.when(pl.program_id(0) == 0)
        def _():
            ds_ref[...] = jnp.zeros_like(ds_ref)

        masks = _chunk_masks()
        causal, strict = masks
        lane = lax.broadcasted_iota(jnp.int32, (CHUNK, LANES), 1)
        last_row = lax.broadcasted_iota(jnp.int32, (CHUNK, 1), 0) == CHUNK - 1
        cat = jnp.concatenate
        heads = range(HEADS)

        def chunk(cj, carry):
            ci = n_chunk - 1 - cj
            rows = pl.ds(pl.multiple_of(ci * CHUNK, CHUNK), CHUNK)
            loc = _units_local(_chunk_units(q_ref, k_ref, v_ref, bg_ref, rows), masks)
            dov = do_ref[rows, :]
            do = [dov[:, HEAD_DIM * h:HEAD_DIM * (h + 1)] for h in heads]
            state = [st_ref[ci, h] for h in heads]
            ds_next = [ds_ref[h] for h in heads]
            w_s = [_mm(p["w"], s) for p, s in zip(loc, state)]
            dq_dec = [_mm(d, s, NT) for d, s in zip(do, state)]
            qk_do = [_mm(p["qk"], d, TN) for p, d in zip(loc, do)]
            kd_ds = [_mm(p["k_dec"], d) for p, d in zip(loc, ds_next)]
            qd_do = [_mm(p["q_dec"], d, TN) for p, d in zip(loc, do)]
            v_new = [p["u"] - t for p, t in zip(loc, w_s)]
            dv_new = [a + b for a, b in zip(qk_do, kd_ds)]
            dqk = [jnp.where(causal, _mm(d, vn, NT), 0.0) for d, vn in zip(do, v_new)]
            dk_dec = [_mm(vn, d, NT) for vn, d in zip(v_new, ds_next)]
            w_dv = [_mm(p["w"], dvn, TN) for p, dvn in zip(loc, dv_new)]
            dw = [-_mm(dvn, s, NT) for dvn, s in zip(dv_new, state)]
            for h in heads:
                ds_ref[h] = loc[h]["gl"] * ds_next[h] + qd_do[h] - w_dv[h]
            dtm = [_mm(cat([dvn, d], axis=1), cat([p["vb"], p["kbg"]], axis=1), NT) for dvn, d, p in zip(dv_new, dw, loc)]
            x_t = [_mm(p["xm"], cat([dvn, d], axis=1), TN) for p, dvn, d in zip(loc, dv_new, dw)]
            dvb = [dvn + t[:, :HEAD_DIM] for dvn, t in zip(dv_new, x_t)]
            dkbg = [d + t[:, HEAD_DIM:] for d, t in zip(dw, x_t)]
            y = [t + _mm(p["xm"], t, TN) for p, t in zip(loc, dtm)]
            dlow = [jnp.where(strict, -(t + _mm(t, p["xm"], NT)), 0.0) for p, t in zip(loc, y)]
            dmm = [d * p["decay"] for d, p in zip(dlow, loc)]
            dnn = [d * p["decay"] for d, p in zip(dqk, loc)]
            by_k = [_mm(cat([a, b], axis=0), p["k"]) for a, b, p in zip(dmm, dnn, loc)]
            dk_mm = [_mm(cat([a, b], axis=0), cat([p["kb"], p["q"]], axis=0), TN) for a, b, p in zip(dmm, dnn, loc)]
            dq_out, dk_out, dv_out = [], [], []
            dbeta_all = jnp.zeros((CHUNK, LANES), F32)
            dgc_all = jnp.zeros((CHUNK, LANES), F32)
            for h in heads:
                p = loc[h]
                dkb = by_k[h][:CHUNK] + dkbg[h] * p["eg"]
                dq_out.append(by_k[h][CHUNK:] + dq_dec[h] * p["eg"])
                dk_out.append(dk_mm[h] + dk_dec[h] * p["ek"] + dkb * p["beta"])
                dv_out.append(dvb[h] * p["beta"])
                dbeta = jnp.sum(dkb * p["k"] + dvb[h] * p["v"], axis=1, keepdims=True)
                e = dlow[h] * p["low"] + dqk[h] * p["qk"]
                kd = jnp.sum(dk_dec[h] * p["k_dec"], axis=1, keepdims=True)
                dgc = (jnp.sum(e, axis=1, keepdims=True) - jnp.sum(e.T, axis=1, keepdims=True)
                       + jnp.sum(dq_dec[h] * p["q_dec"], axis=1, keepdims=True) - kd
                       + jnp.sum(dkbg[h] * p["kbg"], axis=1, keepdims=True))
                dgl = jnp.sum(jnp.sum(ds_next[h] * state[h], axis=1, keepdims=True), axis=0, keepdims=True)
                d_last = jnp.sum(kd, axis=0, keepdims=True) + dgl * p["gl"]
                dgc = dgc + jnp.where(last_row, d_last, 0.0)
                dbeta_all = jnp.where(lane == h, dbeta, dbeta_all)
                dgc_all = jnp.where(lane == h + HEADS, dgc, dgc_all)
            dq_ref[rows, :] = cat(dq_out, axis=1)
            dk_ref[rows, :] = cat(dk_out, axis=1)
            dv_ref[rows, :] = cat(dv_out, axis=1)
            dbg_ref[rows, :] = dbeta_all + dgc_all
            return carry

        lax.fori_loop(0, n_chunk, chunk, 0)

    tok = lambda w: pl.BlockSpec((tb, w), lambda i: (nb - 1 - i, 0))
    return pl.pallas_call(
        body, name="delta_bwd", grid=(nb,),
        in_specs=[tok(DN_WIDTH), tok(DN_WIDTH), tok(DN_WIDTH), tok(LANES),
                  pl.BlockSpec((n_chunk, HEADS, HEAD_DIM, HEAD_DIM), lambda i: (nb - 1 - i, 0, 0, 0)), tok(DN_WIDTH)],
        out_specs=[tok(DN_WIDTH), tok(DN_WIDTH), tok(DN_WIDTH), tok(LANES)],
        out_shape=[jax.ShapeDtypeStruct((T, DN_WIDTH), F32)] * 3 + [jax.ShapeDtypeStruct((T, LANES), F32)],
        scratch_shapes=[pltpu.VMEM((HEADS, HEAD_DIM, HEAD_DIM), F32)],
        compiler_params=_params(("arbitrary",)),
    )(q, k, v, bg, states, do)


def _dn_prep_bwd(dq, dk, dv, dbg, qkv, cw, bd, al_row, dt_row):
    T = qkv.shape[0]
    tb = 256

    def body(dq_ref, dk_ref, dv_ref, dbg_ref, pre_ref, halo_ref, cw_ref, bd_ref, al_ref, dt_ref,
             dc_ref, dbd_ref, dcw_ref, dal_ref, ddt_ref):
        @pl.when(pl.program_id(0) == 0)
        def _():
            dcw_ref[...] = jnp.zeros_like(dcw_ref)
            dal_ref[...] = jnp.zeros_like(dal_ref)
            ddt_ref[...] = jnp.zeros_like(ddt_ref)

        halo = jnp.where(pl.program_id(0) > 0, halo_ref[...], 0.0)
        xc, c, sg, a = _dn_act(pre_ref[...], halo, cw_ref[...], tb)
        dsilu = sg * (1.0 + c * (1.0 - sg))
        for hh in range(HEADS):
            sl = slice(HEAD_DIM * hh, HEAD_DIM * (hh + 1))
            for base, g_ref, scale in ((0, dq_ref, Q_SCALE), (DN_WIDTH, dk_ref, 1.0)):
                sa = slice(base + HEAD_DIM * hh, base + HEAD_DIM * (hh + 1))
                raw = a[:, sa]
                r = lax.rsqrt(jnp.sum(raw * raw, axis=-1, keepdims=True) + EPS)
                nrm = raw * r
                gn_ = g_ref[:, sl] * scale
                dc_ref[:, sa] = r * (gn_ - nrm * jnp.sum(gn_ * nrm, axis=-1, keepdims=True)) * dsilu[:, sa]
        dc_ref[:, 2 * DN_WIDTH:] = dv_ref[...] * dsilu[:, 2 * DN_WIDTH:]
        dc = dc_ref[...]
        for j in range(4):
            dcw_ref[j:j + 1, :] += jnp.sum(dc * xc[5 + j:5 + j + tb, :], axis=0, keepdims=True)
        bdv = bd_ref[...]
        lane = lax.broadcasted_iota(jnp.int32, bdv.shape, 1)
        is_b = lane < HEADS
        dbg_in = dbg_ref[...]
        dbgv = jnp.where(is_b, dbg_in, _mm32(_chunk_cumsum_matrix(tb), dbg_in, TN))
        is_g = jnp.logical_and(lane >= HEADS, lane < 2 * HEADS)
        beta = _sigmoid(bdv)
        neg_a = -jnp.exp(al_ref[...])
        pre_sp = bdv + dt_ref[...]
        g = neg_a * _softplus(pre_sp)
        da_in = dbgv * neg_a * _sigmoid(pre_sp)
        dbd_ref[...] = jnp.where(is_b, dbgv * beta * (1.0 - beta), jnp.where(is_g, da_in, 0.0)).astype(BF16)
        _row_acc(dal_ref, jnp.where(is_g, dbgv * g, 0.0))
        _row_acc(ddt_ref, jnp.where(is_g, da_in, 0.0))

    tok = lambda w: pl.BlockSpec((tb, w), lambda i: (i, 0))
    full = lambda t: pl.BlockSpec(t.shape, lambda i: (0, 0))
    acc = lambda w: pl.BlockSpec((8, w), lambda i: (0, 0))
    return pl.pallas_call(
        body, name="dn_prep_bwd", grid=(T // tb,),
        in_specs=[tok(DN_WIDTH), tok(DN_WIDTH), tok(DN_WIDTH), tok(LANES),
                  tok(QKV), pl.BlockSpec((8, QKV), _before_halo(tb)), full(cw), tok(LANES), full(al_row), full(dt_row)],
        out_specs=[tok(QKV), tok(LANES), acc(QKV), acc(LANES), acc(LANES)],
        out_shape=[jax.ShapeDtypeStruct((T, QKV), F32), jax.ShapeDtypeStruct((T, LANES), BF16),
                   jax.ShapeDtypeStruct((8, QKV), F32), jax.ShapeDtypeStruct((8, LANES), F32),
                   jax.ShapeDtypeStruct((8, LANES), F32)],
        compiler_params=_params(("arbitrary",)),
    )(dq, dk, dv, dbg, qkv, qkv, cw, bd, al_row, dt_row)


def _dn_conv_bwd(dc, cw):
    T = dc.shape[0]
    tb = 512

    def body(dc_ref, halo_ref, w_ref, out_ref):
        last = pl.program_id(0) == pl.num_programs(0) - 1
        halo = jnp.where(last, 0.0, halo_ref[...])
        xc = jnp.concatenate([dc_ref[...], halo], axis=0)
        w = w_ref[...]
        acc = w[3:4, :] * xc[0:tb, :]
        for j in range(3):
            acc = acc + w[j:j + 1, :] * xc[3 - j:3 - j + tb, :]
        out_ref[...] = acc.astype(BF16)

    tok = pl.BlockSpec((tb, QKV), lambda i: (i, 0))
    return pl.pallas_call(
        body, name="dn_conv_bwd", grid=(T // tb,),
        in_specs=[tok, pl.BlockSpec((8, QKV), _after_halo(tb, T)), pl.BlockSpec(cw.shape, lambda i: (0, 0))],
        out_specs=tok,
        out_shape=jax.ShapeDtypeStruct((T, QKV), BF16),
        compiler_params=_params(("parallel",)),
    )(dc, dc, cw)


def _in_proj_bwd(dqkv, dz, dsc, dbd, dx1, x, g1, wa, wbd):
    T = x.shape[0]
    tb = 256

    def body(dqkv_ref, dz_ref, dsc_ref, dbd_ref, dx1_ref, x_ref, g_ref, wa_ref, wbd_ref, dx_ref, dxb_ref, dg_ref):
        @pl.when(pl.program_id(0) == 0)
        def _():
            dg_ref[...] = jnp.zeros_like(dg_ref)

        dh = (_mm(dqkv_ref[...], wa_ref[:, :QKV], NT) + _mm(dz_ref[...], wa_ref[:, QKV:QKV + DN_WIDTH], NT)
              + _mm(dsc_ref[...], wa_ref[:, QKV + DN_WIDTH:], NT) + _mm(dbd_ref[...], wbd_ref[...], NT))
        xv = x_ref[...]
        r = lax.rsqrt(jnp.mean(xv * xv, axis=-1, keepdims=True) + EPS)
        xh = xv * r
        _row_acc(dg_ref, dh * xh)
        dx = dx1_ref[...] + _rms_bwd(dh, xh, r, g_ref[...])
        dx_ref[...] = dx
        dxb_ref[...] = dx.astype(BF16)

    tok = lambda w: pl.BlockSpec((tb, w), lambda i: (i, 0))
    full = lambda t: pl.BlockSpec(t.shape, lambda i: (0, 0))
    return pl.pallas_call(
        body, name="in_proj_bwd", grid=(T // tb,),
        in_specs=[tok(QKV), tok(DN_WIDTH), tok(3 * SC_WIDTH), tok(LANES), tok(D_MODEL), tok(D_MODEL),
                  full(g1), full(wa), full(wbd)],
        out_specs=[tok(D_MODEL), tok(D_MODEL), pl.BlockSpec((8, D_MODEL), lambda i: (0, 0))],
        out_shape=[jax.ShapeDtypeStruct((T, D_MODEL), F32), jax.ShapeDtypeStruct((T, D_MODEL), BF16),
                   jax.ShapeDtypeStruct((8, D_MODEL), F32)],
        compiler_params=_params(("arbitrary",)),
    )(dqkv, dz, dsc, dbd, dx1, x, g1, wa, wbd)


def _pad_rows(a, rows=8):
    return jnp.pad(a, ((0, rows - a.shape[0]), (0, 0)))


def _gate_rows(a_log, dt_bias):
    put = lambda t: jnp.pad(t.reshape(1, HEADS), ((0, 0), (HEADS, LANES - 2 * HEADS)))
    return put(a_log), put(dt_bias)


def _split_w_in(w_in):
    o = QKV + DN_WIDTH
    wa = jnp.concatenate([w_in[:, :o], w_in[:, o + 2 * HEADS:]], axis=1)
    wbd = jnp.pad(w_in[:, o:o + 2 * HEADS], ((0, 0), (0, LANES - 2 * HEADS)))
    return wa, wbd


def _mixer_fwd(x, p):
    qkv, z, sc_in, bd, ht = _in_proj(x, p["g1"], p["wa"], p["wbd"])
    q, k, v, bg = _dn_prep(qkv, p["cw"], bd, p["al"], p["dt"])
    o, states = _delta_fwd(q, k, v, bg)
    x1, mt = _mix_out(o, z, sc_in, x, p["land_a"], p["gn"], p["scw"], p["gs"])
    return x1, dict(x=x, qkv=qkv, z=z, sc_in=sc_in, bd=bd, ht=ht, q=q, k=k, v=v, bg=bg, o=o, states=states, mt=mt)


def _ffn_fwd(x1, p, land_b):
    x2, a, b, h2 = _ffn(x1, p["g2"], land_b)
    return x2, dict(x1=x1, a=a, b=b, h2=h2)


def _ffn_back(dx2, dx2_bf16, s, p, land_b):
    dx1, dx1_bf16, da_t, db_t, act_t, dg2 = _ffn_bwd(dx2, s["x1"], s["a"], s["b"], p["g2"], land_b)
    g = dict(wd=_wgrad(act_t, dx2_bf16, FF_SHARD, 1024, "wgrad_down"), wgt=_wgrad(da_t, s["h2"], FF_SHARD, 1024, "wgrad_gate"),
             wut=_wgrad(db_t, s["h2"], FF_SHARD, 1024, "wgrad_up"), g2=dg2[0])
    return dx1, dx1_bf16, g


def _mixer_bwd(dx1, dx1_bf16, s, p):
    do, dz, dgb, dcv, dgn, dgs, dscw = _mix_out_bwd(dx1, s["o"], s["z"], s["sc_in"], p["land_a"], p["gn"], p["scw"], p["gs"])
    g = dict(w_out=_wgrad(s["mt"], dx1_bf16, 512, 1024, "wgrad_out"))
    dsc = _sc_conv_bwd(dcv, dgb, s["sc_in"], p["scw"])
    dq, dk, dv, dbg = _delta_bwd(s["q"], s["k"], s["v"], s["bg"], s["states"], do)
    dc, dbd, dcw, dal, ddt = _dn_prep_bwd(dq, dk, dv, dbg, s["qkv"], p["cw"], s["bd"], p["al"], p["dt"])
    dqkv = _dn_conv_bwd(dc, p["cw"])
    dx, dx_bf16, dg1 = _in_proj_bwd(dqkv, dz, dsc, dbd, dx1, s["x"], p["g1"], p["wa"], p["wbd"])
    g["w_in"] = jnp.concatenate([
        _wgrad(s["ht"], dqkv, 512, 768, "wgrad_qkv"), _wgrad(s["ht"], dz, 512, 512, "wgrad_z"),
        _wgrad(s["ht"], dbd, 512, LANES, "wgrad_bd")[:, :2 * HEADS], _wgrad(s["ht"], dsc, 512, 768, "wgrad_sc")], axis=1)
    g.update(g1=dg1[0], gn=dgn[0], gs=dgs[0], scw=dscw[:3], cw=dcw[:4],
             al=dal[0, HEADS:2 * HEADS], dt=ddt[0, HEADS:2 * HEADS])
    return dx, dx_bf16, g


def _place():
    return lax.axis_index("x"), lax.axis_index("y"), lax.axis_index("c")


def _other_chips(x, y):
    return [(1 - x, y), (x, 1 - y), (1 - x, 1 - y)]


_HBM = pl.BlockSpec(memory_space=pltpu.HBM)


def _chip_exchange(arrs, name, gather):
    n = len(arrs)

    def body(*refs):
        ins, outs = refs[:n], refs[n:2 * n]
        send_sems, recv_sems, local_sems = refs[2 * n:]
        x, y, c = _place()
        me = 2 * x + y
        others = _other_chips(x, y)

        def remote(k, j, landing):
            px, py = others[j]
            src = ins[k] if gather else ins[k].at[2 * px + py]
            return pltpu.make_async_remote_copy(src_ref=src, dst_ref=outs[k].at[landing], send_sem=send_sems.at[k, j],
                                                recv_sem=recv_sems.at[k, j], device_id=(px, py, c), device_id_type=MESH)

        local = [pltpu.make_async_copy(ins[k] if gather else ins[k].at[me], outs[k].at[me], local_sems.at[k])
                 for k in range(n)]
        sends = [remote(k, j, me) for k in range(n) for j in range(3)]
        for cp in local + sends:
            cp.start()
        for k in range(n):
            for j, (px, py) in enumerate(others):
                remote(k, j, 2 * px + py).wait_recv()
        for cp in sends:
            cp.wait_send()
        for cp in local:
            cp.wait()

    shapes = [jax.ShapeDtypeStruct(((N_CHIPS,) + a.shape) if gather else a.shape, a.dtype) for a in arrs]
    return pl.pallas_call(
        body, name=name, in_specs=[_HBM] * n, out_specs=[_HBM] * n, out_shape=shapes,
        scratch_shapes=[pltpu.SemaphoreType.DMA((n, 3)), pltpu.SemaphoreType.DMA((n, 3)), pltpu.SemaphoreType.DMA((n,))],
    )(*arrs)


_SEM = pl.BlockSpec(memory_space=pltpu.SEMAPHORE)
_ANY = pl.BlockSpec(memory_space=pl.ANY)
_EFFECT = pltpu.SideEffectType.DATAFLOW_SIDE_EFFECTING


def _split_copies(src_ref, land_ref, send_sems, recv_sems, gather, sending):
    x, y, c = _place()
    me = 2 * x + y
    copies = []
    for j, (px, py) in enumerate(_other_chips(x, y)):
        peer = 2 * px + py
        copies.append(pltpu.make_async_remote_copy(
            src_ref=src_ref if gather else src_ref.at[peer], dst_ref=land_ref.at[me if sending else peer],
            send_sem=send_sems.at[j], recv_sem=recv_sems.at[j], device_id=(px, py, c), device_id_type=MESH))
    return copies


def _own_slot(src, gather):
    shape = ((N_CHIPS,) + src.shape) if gather else src.shape

    def body(src_ref, land_ref, sem):
        x, y, _ = _place()
        me = 2 * x + y
        cp = pltpu.make_async_copy(src_ref if gather else src_ref.at[me], land_ref.at[me], sem)
        cp.start()
        cp.wait()

    return pl.pallas_call(
        body, name="own_slot", in_specs=[_HBM], out_specs=_HBM, out_shape=jax.ShapeDtypeStruct(shape, src.dtype),
        scratch_shapes=[pltpu.SemaphoreType.DMA],
    )(src)


def _exchange_start(src, land, after, name, gather):
    def body(src_ref, land_ref, after_ref, send_sems, recv_sems, src_thru, land_thru, token):
        for cp in _split_copies(src_ref, land_ref, send_sems, recv_sems, gather, sending=True):
            cp.start()
        token[...] = jnp.zeros_like(token)

    hbm = lambda t: pltpu.with_memory_space_constraint(t, pltpu.HBM)
    return pl.pallas_call(
        body, name=name,
        out_shape=(pltpu.SemaphoreType.DMA((3,)), pltpu.SemaphoreType.DMA((3,)), pltpu.HBM(src.shape, src.dtype),
                   pltpu.HBM(land.shape, land.dtype), jax.ShapeDtypeStruct((8, LANES), F32)),
        in_specs=(_HBM, _HBM, _ANY), out_specs=(_SEM, _SEM, _HBM, _HBM, pl.BlockSpec(memory_space=pltpu.VMEM)),
        input_output_aliases={0: 2, 1: 3},
        compiler_params=pltpu.CompilerParams(has_side_effects=_EFFECT),
    )(hbm(src), hbm(land), after)


def _exchange_wait(started, after, name, gather):
    send_sems, recv_sems, src_thru, land_thru, _ = started

    def body(src_ref, land_ref, send_sems, recv_sems, after_ref, src_dead, got_ref):
        for cp in _split_copies(src_ref, land_ref, send_sems, recv_sems, gather, sending=False):
            cp.wait_send()
            cp.wait_recv()

    return pl.pallas_call(
        body, name=name,
        out_shape=(pltpu.HBM(src_thru.shape, src_thru.dtype), pltpu.HBM(land_thru.shape, land_thru.dtype)),
        in_specs=(_HBM, _HBM, _SEM, _SEM, _ANY), out_specs=(_HBM, _HBM), input_output_aliases={0: 0, 1: 1},
        compiler_params=pltpu.CompilerParams(has_side_effects=_EFFECT),
    )(src_thru, land_thru, send_sems, recv_sems, after)[1]


def _swap_sibling(arrs):
    n = len(arrs)

    def body(*refs):
        ins, outs = refs[:n], refs[n:2 * n]
        send_sems, recv_sems = refs[2 * n:]
        x, y, c = _place()
        copies = [pltpu.make_async_remote_copy(src_ref=ins[k], dst_ref=outs[k], send_sem=send_sems.at[k],
                                               recv_sem=recv_sems.at[k], device_id=(x, y, 1 - c), device_id_type=MESH)
                  for k in range(n)]
        for cp in copies:
            cp.start()
        for cp in copies:
            cp.wait()

    return pl.pallas_call(
        body, name="swap_sibling", in_specs=[_HBM] * n, out_specs=[_HBM] * n,
        out_shape=[jax.ShapeDtypeStruct(a.shape, a.dtype) for a in arrs],
        scratch_shapes=[pltpu.SemaphoreType.DMA((n,)), pltpu.SemaphoreType.DMA((n,))],
    )(*arrs)


def _all_reduce_small(v):
    rows = v.shape[0]
    flips = [(a, b, cc) for a in (0, 1) for b in (0, 1) for cc in (0, 1)][1:]

    def body(v_ref, out_ref, buf_ref, send_sems, recv_sems):
        x, y, c = _place()
        me = 4 * x + 2 * y + c
        peers = [((1 - x) if a else x, (1 - y) if b else y, (1 - c) if cc else c) for a, b, cc in flips]

        def copy(j, landing):
            return pltpu.make_async_remote_copy(src_ref=v_ref, dst_ref=buf_ref.at[landing], send_sem=send_sems.at[j],
                                                recv_sem=recv_sems.at[j], device_id=peers[j], device_id_type=MESH)

        sends = [copy(j, me) for j in range(N_DEV - 1)]
        for cp in sends:
            cp.start()
        buf_ref[me] = v_ref[...]
        for j, (px, py, pc) in enumerate(peers):
            copy(j, 4 * px + 2 * py + pc).wait_recv()
        for cp in sends:
            cp.wait_send()
        acc = buf_ref[0]
        for d in range(1, N_DEV):
            acc = acc + buf_ref[d]
        out_ref[...] = acc

    vmem = pl.BlockSpec(memory_space=pltpu.VMEM)
    return pl.pallas_call(
        body, name="all_reduce_small", in_specs=[vmem], out_specs=vmem,
        out_shape=jax.ShapeDtypeStruct(v.shape, F32),
        scratch_shapes=[pltpu.VMEM((N_DEV, rows, LANES), F32), pltpu.SemaphoreType.DMA((N_DEV - 1,)),
                        pltpu.SemaphoreType.DMA((N_DEV - 1,))],
    )(v)


def _row_block(*sizes):
    return next(t for t in (256, 192, 128, 64) if all(s % t == 0 for s in sizes))


def _sum_chips(parts, name):
    _, rows, cols = parts[0].shape
    n = len(parts)
    tr = _row_block(rows)

    def body(*refs):
        o_ref = refs[n]
        for l in range(n):
            @pl.when(pl.program_id(0) == l)
            def _(p_ref=refs[l]):
                acc = p_ref[0].astype(F32)
                for s in range(1, N_CHIPS):
                    acc = acc + p_ref[s].astype(F32)
                o_ref[0] = acc

    return pl.pallas_call(
        body, name=name, grid=(n, rows // tr),
        in_specs=[pl.BlockSpec((N_CHIPS, tr, cols), lambda l, i, k=k: (0, jnp.where(l == k, i, 0), 0)) for k in range(n)],
        out_specs=pl.BlockSpec((1, tr, cols), lambda l, i: (l, i, 0)),
        out_shape=jax.ShapeDtypeStruct((n, rows, cols), F32),
        compiler_params=_params(("arbitrary", "arbitrary")),
    )(*parts)


def _adam_update(w, m, v, g):
    c1 = 1.0 - ADAM_B1 ** ADAM_STEP
    c2 = 1.0 - ADAM_B2 ** ADAM_STEP
    m_new = ADAM_B1 * m + (1.0 - ADAM_B1) * g
    v_new = ADAM_B2 * v + (1.0 - ADAM_B2) * (g * g)
    return -ADAM_LR * ((m_new / c1) / (jnp.sqrt(v_new / c2) + ADAM_EPS) + ADAM_WD * w), m_new, v_new


def _adamw_rows(w, m, v, g_parts, first, name):
    n_layers, rows, cols = w.shape
    tr = _row_block(rows, first)
    n = len(g_parts)

    def body(*refs):
        w_ref, m_ref, v_ref = refs[:3]
        g_out, d_out, m_out, v_out = refs[3 + n:]
        g = refs[3][...]
        for r in refs[4:3 + n]:
            g = g + r[...]
        g = g[:, :, :cols]
        d_out[...], m_out[...], v_out[...] = _adam_update(w_ref[...], m_ref[...], v_ref[...], g)
        g_out[...] = g

    blk = pl.BlockSpec((1, tr, cols), lambda l, i: (l, i, 0))
    g_blk = pl.BlockSpec((1, tr, g_parts[0].shape[2]), lambda l, i: (l, first // tr + i, 0))
    return pl.pallas_call(
        body, name=name, grid=(n_layers, rows // tr),
        in_specs=[blk] * 3 + [g_blk] * n, out_specs=[blk] * 4,
        out_shape=[jax.ShapeDtypeStruct(w.shape, F32)] * 4,
        compiler_params=_params(("parallel", "parallel")),
    )(w, m, v, *g_parts)


def _adamw(w, m, v, g_parts, name):
    rows, cols = w.shape
    tr = min(rows, 256)
    n = len(g_parts)

    def body(*refs):
        w_ref, m_ref, v_ref = refs[:3]
        g_refs = refs[3:3 + n]
        g_out, d_out, m_out, v_out = refs[3 + n:]
        g = g_refs[0][...]
        for r in g_refs[1:]:
            g = g + r[...]
        d_out[...], m_out[...], v_out[...] = _adam_update(w_ref[...], m_ref[...], v_ref[...], g)
        g_out[...] = g

    blk = pl.BlockSpec((tr, cols), lambda i: (i, 0))
    return pl.pallas_call(
        body, name=name, grid=(rows // tr,),
        in_specs=[blk] * (3 + n), out_specs=[blk] * 4,
        out_shape=[jax.ShapeDtypeStruct((rows, cols), F32)] * 4,
        compiler_params=_params(("parallel",)),
    )(w, m, v, *g_parts)


def _pack(parts, rows, fill=0.0):
    flat = jnp.concatenate([p.reshape(-1) for p in parts])
    return jnp.pad(flat, (0, rows * LANES - flat.shape[0]), constant_values=fill).reshape(rows, LANES)


def _unpack(packed, shapes):
    flat = packed.reshape(-1)
    out, at = [], 0
    for shp in shapes:
        size = 1
        for s in shp:
            size *= s
        out.append(flat[at:at + size].reshape(shp))
        at += size
    return out


def _packed_rows(shapes):
    total = 0
    for shp in shapes:
        size = 1
        for s in shp:
            size *= s
        total += size
    return -(-total // (8 * LANES)) * 8


def _cols_full(g, l):
    t = g[:, l]
    return jnp.moveaxis(t, 0, 1).reshape(t.shape[1], N_CHIPS * t.shape[2])


def _pad_cols(t):
    return jnp.pad(t, ((0, 0),) * (t.ndim - 1) + ((0, D_MODEL - t.shape[-1]),))


def _w_in_of(land_a):
    return jnp.moveaxis(land_a[:, :D_MODEL, :IN_SHARD], 0, 1).reshape(D_MODEL, W_IN_COLS)


def _parts_a(g_w_in, g_w_out):
    cols = jnp.moveaxis(g_w_in.reshape(D_MODEL, N_CHIPS, IN_SHARD), 1, 0)
    return jnp.concatenate([_pad_cols(cols), g_w_out.reshape(N_CHIPS, OUT_SHARD, D_MODEL)], axis=1).astype(BF16)


def _parts_b(g_wgt, g_wut, g_wd):
    return jnp.concatenate([t.reshape(N_CHIPS, FF_SHARD, D_MODEL) for t in (g_wgt, g_wut, g_wd)], axis=1).astype(BF16)


def kernel(x, norm1_g, w_in, dn_conv_w, dn_a_log, dn_dt_bias, dn_norm_g, sc_conv_w, sc_norm_g, w_out, norm2_g, ffn_w_gate, ffn_w_up, ffn_w_down, final_norm_g, loss_target, m_norm1_g, m_w_in, m_dn_conv_w, m_dn_a_log, m_dn_dt_bias, m_dn_norm_g, m_sc_conv_w, m_sc_norm_g, m_w_out, m_norm2_g, m_ffn_w_gate, m_ffn_w_up, m_ffn_w_down, m_final_norm_g, v_norm1_g, v_w_in, v_dn_conv_w, v_dn_a_log, v_dn_dt_bias, v_dn_norm_g, v_sc_conv_w, v_sc_norm_g, v_w_out, v_norm2_g, v_ffn_w_gate, v_ffn_w_up, v_ffn_w_down, v_final_norm_g):
    chip = 2 * lax.axis_index("x") + lax.axis_index("y")

    g_cw, g_scw = _chip_exchange([dn_conv_w, sc_conv_w], "gather_conv", gather=True)

    t_last = lambda t: jnp.swapaxes(t, -1, -2)
    gate_t, up_t = t_last(ffn_w_gate), t_last(ffn_w_up)
    share_a = [jnp.concatenate([_pad_cols(w_in[l]), w_out[l]], axis=0).astype(BF16) for l in range(DEPTH)]
    share_b = [jnp.concatenate([gate_t[l], up_t[l], ffn_w_down[l]], axis=0).astype(BF16) for l in range(DEPTH)]
    zero_token = jnp.zeros((8, LANES), F32)

    def gather_start(l, after):
        a = _exchange_start(share_a[l], _own_slot(share_a[l], True), after, "gather_a_start_%d" % l, gather=True)
        b = _exchange_start(share_b[l], _own_slot(share_b[l], True), a[4], "gather_b_start_%d" % l, gather=True)
        return a, b

    ga, gb = gather_start(0, zero_token)
    land_a = _exchange_wait(ga, gb[4], "gather_a_wait_0", gather=True)
    act = x[0]
    layers, saved_m, saved_f, lands_b = [], [], [], []
    for l in range(DEPTH):
        hold = 0.0
        if l + 1 < DEPTH:
            ga, gb_next = gather_start(l + 1, land_a)
            hold = gb_next[4][0:1, 0:1]
        wa, wbd = _split_w_in(_w_in_of(land_a))
        al, dt = _gate_rows(dn_a_log[l], dn_dt_bias[l])
        layers.append(dict(
            g1=norm1_g[l][None] + hold, wa=wa, wbd=wbd, cw=_pad_rows(_cols_full(g_cw, l)), al=al, dt=dt,
            gn=dn_norm_g[l][None], scw=_pad_rows(_cols_full(g_scw, l)), gs=sc_norm_g[l][None],
            land_a=land_a, g2=norm2_g[l][None]))
        x1, s = _mixer_fwd(act, layers[l])
        saved_m.append(s)
        lands_b.append(_exchange_wait(gb, x1, "gather_b_wait_%d" % l, gather=True))
        act, s = _ffn_fwd(x1, layers[l], lands_b[l])
        saved_f.append(s)
        if l + 1 < DEPTH:
            land_a = _exchange_wait(ga, act, "gather_a_wait_%d" % (l + 1), gather=True)
            gb = gb_next

    dact, dact_bf16, loss_part, d_final = _loss_head(act, final_norm_g[None], loss_target[0])
    grads, reduce_a, reduce_b = [None] * DEPTH, [None] * DEPTH, [None] * DEPTH
    hold = 0.0
    for l in reversed(range(DEPTH)):
        p = layers[l]
        dx1, dx1_bf16, g = _ffn_back(dact, dact_bf16, saved_f[l], dict(p, g2=p["g2"] + hold), lands_b[l])
        parts = _parts_b(g["wgt"], g["wut"], g["wd"])
        reduce_b[l] = _exchange_start(parts, _own_slot(parts, False), zero_token, "reduce_b_start_%d" % l, gather=False)
        dact, dact_bf16, gm = _mixer_bwd(dx1, dx1_bf16, saved_m[l], dict(p, gn=p["gn"] + reduce_b[l][4][0:1, 0:1]))
        parts = _parts_a(gm["w_in"], gm["w_out"])
        reduce_a[l] = _exchange_start(parts, _own_slot(parts, False), zero_token, "reduce_a_start_%d" % l, gather=False)
        hold = reduce_a[l][4][0:1, 0:1]
        grads[l] = dict(gm, g2=g["g2"])
    loss = lax.psum(loss_part[0, 0], ("x", "y", "c"))
    stack = lambda key: jnp.stack([grads[l][key] for l in range(DEPTH)])

    got_b = [_exchange_wait(reduce_b[l], dact, "reduce_b_wait_%d" % l, gather=False) for l in reversed(range(DEPTH))][::-1]
    got_a = [_exchange_wait(reduce_a[l], dact, "reduce_a_wait_%d" % l, gather=False) for l in reversed(range(DEPTH))][::-1]
    sum_a, sum_b = _sum_chips(got_a, "sum_chips_a"), _sum_chips(got_b, "sum_chips_b")
    other_a, other_b = _swap_sibling([sum_a, sum_b])
    big = dict(
        w_in=_adamw_rows(w_in, m_w_in, v_w_in, [sum_a, other_a], 0, "adamw_w_in"),
        w_out=_adamw_rows(w_out, m_w_out, v_w_out, [sum_a, other_a], A_OUT_AT, "adamw_w_out"),
        ffn_w_gate=[t_last(o) for o in _adamw_rows(gate_t, t_last(m_ffn_w_gate), t_last(v_ffn_w_gate),
                                                   [sum_b, other_b], 0, "adamw_gate")],
        ffn_w_up=[t_last(o) for o in _adamw_rows(up_t, t_last(m_ffn_w_up), t_last(v_ffn_w_up),
                                                 [sum_b, other_b], FF_SHARD, "adamw_up")],
        ffn_w_down=_adamw_rows(ffn_w_down, m_ffn_w_down, v_ffn_w_down, [sum_b, other_b], 2 * FF_SHARD, "adamw_down"))

    full_shapes = [(DEPTH, D_MODEL), (DEPTH, D_MODEL), (DEPTH, HEAD_DIM), (DEPTH, SC_WIDTH), (DEPTH, HEADS),
                   (DEPTH, HEADS), (D_MODEL,), (DEPTH, 4, QKV), (DEPTH, 3, SC_WIDTH)]
    small_keys = ("g1", "g2", "gn", "gs", "al", "dt")
    packed = _pack([stack(k) for k in small_keys] + [d_final[0], stack("cw"), stack("scw")], _packed_rows(full_shapes))
    sg = _unpack(_all_reduce_small(packed), full_shapes)
    sg[7] = lax.dynamic_slice_in_dim(sg[7], chip * (QKV // N_CHIPS), QKV // N_CHIPS, axis=2)
    sg[8] = lax.dynamic_slice_in_dim(sg[8], chip * (SC_WIDTH // N_CHIPS), SC_WIDTH // N_CHIPS, axis=2)
    small_names = ("norm1_g", "norm2_g", "dn_norm_g", "sc_norm_g", "dn_a_log", "dn_dt_bias", "final_norm_g",
                   "dn_conv_w", "sc_conv_w")
    sw = (norm1_g, norm2_g, dn_norm_g, sc_norm_g, dn_a_log, dn_dt_bias, final_norm_g, dn_conv_w, sc_conv_w)
    sm = (m_norm1_g, m_norm2_g, m_dn_norm_g, m_sc_norm_g, m_dn_a_log, m_dn_dt_bias, m_final_norm_g, m_dn_conv_w, m_sc_conv_w)
    sv = (v_norm1_g, v_norm2_g, v_dn_norm_g, v_sc_norm_g, v_dn_a_log, v_dn_dt_bias, v_final_norm_g, v_dn_conv_w, v_sc_conv_w)
    shard_shapes = [t.shape for t in sw]
    rows = _packed_rows(shard_shapes)
    outs = _adamw(_pack(sw, rows), _pack(sm, rows), _pack(sv, rows, fill=1.0), [_pack(sg, rows)], "adamw_small")
    small = {name: [] for name in small_names}
    for o in outs:
        for name, t in zip(small_names, _unpack(o, shard_shapes)):
            small[name].append(t)

    order = ("norm1_g", "w_in", "dn_conv_w", "dn_a_log", "dn_dt_bias", "dn_norm_g", "sc_conv_w", "sc_norm_g", "w_out",
             "norm2_g", "ffn_w_gate", "ffn_w_up", "ffn_w_down", "final_norm_g")
    result = {**big, **small}
    return (loss, dact[None], *[result[n][0] for n in order], *[result[n][1] for n in order],
            *[result[n][2] for n in order], *[result[n][3] for n in order])
```

```python
import jax
import jax.numpy as jnp
from jax import lax
from jax.experimental import pallas as pl
from jax.experimental.pallas import tpu as pltpu

F32 = jnp.float32
BF16 = jnp.bfloat16
MESH = pl.DeviceIdType.MESH

D_MODEL = 1024
DEPTH = 4
HEADS = 4
HEAD_DIM = 128
DN_WIDTH = HEADS * HEAD_DIM
SC_WIDTH = 512
SC_GROUPS = 4
D_FF = 2816
CHUNK = 64
QKV = 3 * DN_WIDTH
W_IN_COLS = 4 * DN_WIDTH + 2 * HEADS + 3 * SC_WIDTH
WA_COLS = QKV + DN_WIDTH + 3 * SC_WIDTH
LANES = 128
EPS = 1e-6
Q_SCALE = HEAD_DIM ** -0.5
N_CHIPS = 4
N_DEV = 8
IN_SHARD = W_IN_COLS // N_CHIPS
OUT_SHARD = D_MODEL // N_CHIPS
FF_SHARD = D_FF // N_CHIPS
A_OUT_AT = D_MODEL
A_ROWS = D_MODEL + OUT_SHARD
B_ROWS = 3 * FF_SHARD

ADAM_LR = 0.001
ADAM_B1 = 0.9
ADAM_B2 = 0.999
ADAM_EPS = 1e-08
ADAM_WD = 0.01
ADAM_STEP = 10

VMEM_LIMIT = 56 * 1024 * 1024

NN = (((1,), (0,)), ((), ()))
NT = (((1,), (1,)), ((), ()))
TN = (((0,), (0,)), ((), ()))


def _mm(a, b, dims=NN):
    return lax.dot_general(a.astype(BF16), b.astype(BF16), dims, preferred_element_type=F32)


def _mm32(a, b, dims=NN):
    return lax.dot_general(a, b, dims, preferred_element_type=F32, precision=lax.Precision.HIGHEST)


def _params(sem, vmem=VMEM_LIMIT):
    return pltpu.CompilerParams(dimension_semantics=sem, vmem_limit_bytes=vmem)


def _sigmoid(x):
    return 1.0 / (1.0 + jnp.exp(-x))


def _softplus(x):
    return jnp.maximum(x, 0.0) + jnp.log1p(jnp.exp(-jnp.abs(x)))


def _row_acc(acc_ref, val):
    acc_ref[0:1, :] += jnp.sum(val, axis=0, keepdims=True)


def _rms_bwd(dh, xh, r, gain):
    dxh = dh * gain
    return r * (dxh - xh * jnp.mean(dxh * xh, axis=-1, keepdims=True))


def _before_halo(tb):
    return lambda i: (jnp.maximum(i * (tb // 8) - 1, 0), 0)


def _after_halo(tb, n_rows):
    last = n_rows // 8 - 1
    return lambda i: (jnp.minimum((i + 1) * (tb // 8), last), 0)


def _taps(xc, w, n_taps, tb, first):
    out = w[0:1, :] * xc[first:first + tb, :]
    for j in range(1, n_taps):
        out = out + w[j:j + 1, :] * xc[first + j:first + j + tb, :]
    return out


def _in_proj(x, g1, wa, wbd):
    T = x.shape[0]
    tb = 256

    def body(x_ref, g_ref, wa_ref, wbd_ref, qkv_ref, z_ref, sc_ref, bd_ref, ht_ref):
        xv = x_ref[...]
        r = lax.rsqrt(jnp.mean(xv * xv, axis=-1, keepdims=True) + EPS)
        h = (xv * r * g_ref[...]).astype(BF16)
        p = jnp.dot(h, wa_ref[...], preferred_element_type=F32)
        qkv_ref[...] = p[:, :QKV]
        z_ref[...] = p[:, QKV:QKV + DN_WIDTH]
        sc_ref[...] = p[:, QKV + DN_WIDTH:]
        bd_ref[...] = jnp.dot(h, wbd_ref[...], preferred_element_type=F32)
        ht_ref[...] = h.T

    tok = lambda w: pl.BlockSpec((tb, w), lambda i: (i, 0))
    full = lambda a: pl.BlockSpec(a.shape, lambda i: (0, 0))
    return pl.pallas_call(
        body, name="in_proj", grid=(T // tb,),
        in_specs=[tok(D_MODEL), full(g1), full(wa), full(wbd)],
        out_specs=[tok(QKV), tok(DN_WIDTH), tok(3 * SC_WIDTH), tok(LANES),
                   pl.BlockSpec((D_MODEL, tb), lambda i: (0, i))],
        out_shape=[jax.ShapeDtypeStruct((T, QKV), F32), jax.ShapeDtypeStruct((T, DN_WIDTH), F32),
                   jax.ShapeDtypeStruct((T, 3 * SC_WIDTH), F32), jax.ShapeDtypeStruct((T, LANES), F32),
                   jax.ShapeDtypeStruct((D_MODEL, T), BF16)],
        compiler_params=_params(("parallel",)),
    )(x, g1, wa, wbd)


def _dn_act(pre, halo, cw, tb):
    xc = jnp.concatenate([halo, pre], axis=0)
    c = _taps(xc, cw, 4, tb, 5)
    sg = _sigmoid(c)
    return xc, c, sg, c * sg


def _gates(bd, al_row, dt_row):
    lane = lax.broadcasted_iota(jnp.int32, bd.shape, 1)
    beta = _sigmoid(bd)
    g = -jnp.exp(al_row) * _softplus(bd + dt_row)
    return jnp.where(lane < HEADS, beta, jnp.where(lane < 2 * HEADS, g, 0.0))


def _dn_prep(qkv, cw, bd, al_row, dt_row):
    T = qkv.shape[0]
    tb = 512

    def body(pre_ref, halo_ref, cw_ref, bd_ref, al_ref, dt_ref, q_ref, k_ref, v_ref, bg_ref):
        halo = jnp.where(pl.program_id(0) > 0, halo_ref[...], 0.0)
        _, _, _, a = _dn_act(pre_ref[...], halo, cw_ref[...], tb)
        for hh in range(HEADS):
            sl = slice(HEAD_DIM * hh, HEAD_DIM * (hh + 1))
            qs = a[:, sl]
            q_ref[:, sl] = qs * (lax.rsqrt(jnp.sum(qs * qs, axis=-1, keepdims=True) + EPS) * Q_SCALE)
            ks = a[:, DN_WIDTH + HEAD_DIM * hh:DN_WIDTH + HEAD_DIM * (hh + 1)]
            k_ref[:, sl] = ks * lax.rsqrt(jnp.sum(ks * ks, axis=-1, keepdims=True) + EPS)
        v_ref[...] = a[:, 2 * DN_WIDTH:]
        gates = _gates(bd_ref[...], al_ref[...], dt_ref[...])
        lane = lax.broadcasted_iota(jnp.int32, gates.shape, 1)
        bg_ref[...] = jnp.where(lane < HEADS, gates, _mm32(_chunk_cumsum_matrix(tb), gates))

    tok = lambda w: pl.BlockSpec((tb, w), lambda i: (i, 0))
    full = lambda a: pl.BlockSpec(a.shape, lambda i: (0, 0))
    return pl.pallas_call(
        body, name="dn_prep", grid=(T // tb,),
        in_specs=[tok(QKV), pl.BlockSpec((8, QKV), _before_halo(tb)), full(cw), tok(LANES), full(al_row), full(dt_row)],
        out_specs=[tok(DN_WIDTH), tok(DN_WIDTH), tok(DN_WIDTH), tok(LANES)],
        out_shape=[jax.ShapeDtypeStruct((T, DN_WIDTH), F32)] * 3 + [jax.ShapeDtypeStruct((T, LANES), F32)],
        compiler_params=_params(("parallel",)),
    )(qkv, qkv, cw, bd, al_row, dt_row)


def _chunk_masks():
    row = lax.broadcasted_iota(jnp.int32, (CHUNK, CHUNK), 0)
    col = lax.broadcasted_iota(jnp.int32, (CHUNK, CHUNK), 1)
    return row >= col, row > col


def _chunk_cumsum_matrix(n):
    row = lax.broadcasted_iota(jnp.int32, (n, n), 0)
    col = lax.broadcasted_iota(jnp.int32, (n, n), 1)
    return jnp.logical_and(row >= col, row // CHUNK == col // CHUNK).astype(F32)


def _chunk_units(q_ref, k_ref, v_ref, bg_ref, rows):
    bgc = bg_ref[rows, :]
    bg_t = bgc.T
    qv, kv, vv = q_ref[rows, :], k_ref[rows, :], v_ref[rows, :]
    units = []
    for h in range(HEADS):
        sl = slice(HEAD_DIM * h, HEAD_DIM * (h + 1))
        units.append((qv[:, sl], kv[:, sl], vv[:, sl], bgc[:, h:h + 1], bgc[:, HEADS + h:HEADS + h + 1],
                      bg_t[HEADS + h:HEADS + h + 1, :]))
    return units


def _units_local(units, masks):
    causal, strict = masks
    pre = []
    for q, k, v, beta, gc, gr in units:
        kb = k * beta
        eg = jnp.exp(gc)
        g_last = gc[CHUNK - 1:CHUNK, :]
        ek = jnp.exp(g_last - gc)
        pre.append(dict(q=q, k=k, v=v, beta=beta, decay=jnp.exp(jnp.where(causal, gc - gr, -1e30)), kb=kb, vb=v * beta,
                        eg=eg, kbg=kb * eg, ek=ek, gl=jnp.exp(g_last), q_dec=q * eg, k_dec=k * ek))
    both = [_mm(jnp.concatenate([p["kb"], p["q"]], axis=0), p["k"], NT) for p in pre]
    for p, b in zip(pre, both):
        p["low"] = jnp.where(strict, b[:CHUNK] * p["decay"], 0.0)
        p["qk"] = jnp.where(causal, b[CHUNK:] * p["decay"], 0.0)
    xs = [-p["low"] for p in pre]
    pw = [_mm(p["low"], p["low"]) for p in pre]
    for _ in range(4):
        both = [_mm(jnp.concatenate([pp, x], axis=0), pp) for pp, x in zip(pw, xs)]
        xs = [x + pp + b[CHUNK:] for x, pp, b in zip(xs, pw, both)]
        pw = [b[:CHUNK] for b in both]
    last = [_mm(x, pp) for x, pp in zip(xs, pw)]
    xs = [x + pp + b for x, pp, b in zip(xs, pw, last)]
    uw = [_mm(x, jnp.concatenate([p["vb"], p["kbg"]], axis=1)) for x, p in zip(xs, pre)]
    for p, x, b in zip(pre, xs, uw):
        p["xm"] = x
        p["u"] = p["vb"] + b[:, :HEAD_DIM]
        p["w"] = p["kbg"] + b[:, HEAD_DIM:]
    return pre


def _delta_fwd(q, k, v, bg):
    T = q.shape[0]
    tb = 512
    n_chunk = tb // CHUNK

    def body(q_ref, k_ref, v_ref, bg_ref, o_ref, st_ref, s_ref):
        @pl.when(pl.program_id(0) == 0)
        def _():
            s_ref[...] = jnp.zeros_like(s_ref)

        masks = _chunk_masks()

        def pair(pi, carry):
            rows = [pl.ds(pl.multiple_of((2 * pi + j) * CHUNK, CHUNK), CHUNK) for j in range(2)]
            loc = _units_local(_chunk_units(q_ref, k_ref, v_ref, bg_ref, rows[0])
                               + _chunk_units(q_ref, k_ref, v_ref, bg_ref, rows[1]), masks)
            states = [s_ref[h] for h in range(HEADS)]
            for j in range(2):
                lj = loc[HEADS * j:HEADS * (j + 1)]
                ws = [_mm(jnp.concatenate([p["w"], p["q_dec"]], axis=0), s) for p, s in zip(lj, states)]
                v_new = [p["u"] - b[:CHUNK] for p, b in zip(lj, ws)]
                intra = [_mm(p["qk"], vn) for p, vn in zip(lj, v_new)]
                upd = [_mm(p["k_dec"], vn, TN) for p, vn in zip(lj, v_new)]
                o_ref[rows[j], :] = jnp.concatenate([b[CHUNK:] + a for b, a in zip(ws, intra)], axis=1)
                for h in range(HEADS):
                    st_ref[2 * pi + j, h] = states[h]
                states = [p["gl"] * s + d for p, s, d in zip(lj, states, upd)]
            for h in range(HEADS):
                s_ref[h] = states[h]
            return carry

        lax.fori_loop(0, n_chunk // 2, pair, 0)

    tok = lambda w: pl.BlockSpec((tb, w), lambda i: (i, 0))
    return pl.pallas_call(
        body, name="delta_fwd", grid=(T // tb,),
        in_specs=[tok(DN_WIDTH), tok(DN_WIDTH), tok(DN_WIDTH), tok(LANES)],
        out_specs=[tok(DN_WIDTH), pl.BlockSpec((n_chunk, HEADS, HEAD_DIM, HEAD_DIM), lambda i: (i, 0, 0, 0))],
        out_shape=[jax.ShapeDtypeStruct((T, DN_WIDTH), F32),
                   jax.ShapeDtypeStruct((T // CHUNK, HEADS, HEAD_DIM, HEAD_DIM), F32)],
        scratch_shapes=[pltpu.VMEM((HEADS, HEAD_DIM, HEAD_DIM), F32)],
        compiler_params=_params(("arbitrary",)),
    )(q, k, v, bg)


def _dn_out(o, z, gn):
    outs, ohs, rs = [], [], []
    for hh in range(HEADS):
        oh = o[:, HEAD_DIM * hh:HEAD_DIM * (hh + 1)]
        r = lax.rsqrt(jnp.mean(oh * oh, axis=-1, keepdims=True) + EPS)
        ohs.append(oh * r)
        rs.append(r)
    sz = _sigmoid(z)
    oh = jnp.concatenate(ohs, axis=1)
    gn4 = jnp.concatenate([gn] * HEADS, axis=1)
    return oh * gn4 * (z * sz), oh, rs, sz, gn4


def _sc_fwd(sc_in, halo, cw, tb):
    xc = jnp.concatenate([halo, sc_in], axis=0)
    u = xc[:, SC_WIDTH:2 * SC_WIDTH] * xc[:, 2 * SC_WIDTH:]
    cv = _taps(u, cw, 3, tb, 6)
    gate_b = sc_in[:, :SC_WIDTH]
    y = gate_b * cv
    gw = SC_WIDTH // SC_GROUPS
    yhs, rs = [], []
    for gi in range(SC_GROUPS):
        yg = y[:, gw * gi:gw * (gi + 1)]
        r = lax.rsqrt(jnp.mean(yg * yg, axis=-1, keepdims=True) + EPS)
        yhs.append(yg * r)
        rs.append(r)
    return u, cv, gate_b, jnp.concatenate(yhs, axis=1), rs


def _shard_rows(land, first, rows):
    assert first % rows == 0 and land.shape[0] == N_CHIPS
    return pl.BlockSpec((N_CHIPS, rows, land.shape[2]), lambda i: (0, first // rows, 0))


def _whole(w_ref):
    n, rows, cols = w_ref.shape
    return w_ref[...].reshape(n * rows, cols)


def _mix_out(o, z, sc_in, x, land_a, gn, scw, gs):
    T = x.shape[0]
    tb = 256

    def body(o_ref, z_ref, sc_ref, halo_ref, x_ref, w_ref, gn_ref, scw_ref, gs_ref, x1_ref, mt_ref):
        o_n = _dn_out(o_ref[...], z_ref[...], gn_ref[...])[0]
        halo = jnp.where(pl.program_id(0) > 0, halo_ref[...], 0.0)
        yh = _sc_fwd(sc_ref[...], halo, scw_ref[...], tb)[3]
        mix = jnp.concatenate([o_n, yh * gs_ref[...]], axis=1).astype(BF16)
        x1_ref[...] = x_ref[...] + jnp.dot(mix, _whole(w_ref), preferred_element_type=F32)
        mt_ref[...] = mix.T

    tok = lambda w: pl.BlockSpec((tb, w), lambda i: (i, 0))
    full = lambda a: pl.BlockSpec(a.shape, lambda i: (0, 0))
    return pl.pallas_call(
        body, name="mix_out", grid=(T // tb,),
        in_specs=[tok(DN_WIDTH), tok(DN_WIDTH), tok(3 * SC_WIDTH), pl.BlockSpec((8, 3 * SC_WIDTH), _before_halo(tb)),
                  tok(D_MODEL), _shard_rows(land_a, A_OUT_AT, OUT_SHARD), full(gn), full(scw), full(gs)],
        out_specs=[tok(D_MODEL), pl.BlockSpec((D_MODEL, tb), lambda i: (0, i))],
        out_shape=[jax.ShapeDtypeStruct((T, D_MODEL), F32), jax.ShapeDtypeStruct((D_MODEL, T), BF16)],
        compiler_params=_params(("parallel",)),
    )(o, z, sc_in, sc_in, x, land_a, gn, scw, gs)


def _ffn(x1, g2, land_b):
    T = x1.shape[0]
    tb = 256

    def body(x_ref, g_ref, wgt_ref, wut_ref, wd_ref, x2_ref, a_ref, b_ref, h_ref):
        xv = x_ref[...]
        r = lax.rsqrt(jnp.mean(xv * xv, axis=-1, keepdims=True) + EPS)
        h = (xv * r * g_ref[...]).astype(BF16)
        a = lax.dot_general(h, _whole(wgt_ref), NT, preferred_element_type=F32)
        b = lax.dot_general(h, _whole(wut_ref), NT, preferred_element_type=F32)
        act = (a * _sigmoid(a) * b).astype(BF16)
        x2_ref[...] = xv + jnp.dot(act, _whole(wd_ref), preferred_element_type=F32)
        a_ref[...] = a.astype(BF16)
        b_ref[...] = b.astype(BF16)
        h_ref[...] = h

    tok = lambda w: pl.BlockSpec((tb, w), lambda i: (i, 0))
    return pl.pallas_call(
        body, name="ffn", grid=(T // tb,),
        in_specs=[tok(D_MODEL), pl.BlockSpec(g2.shape, lambda i: (0, 0)), _shard_rows(land_b, 0, FF_SHARD),
                  _shard_rows(land_b, FF_SHARD, FF_SHARD), _shard_rows(land_b, 2 * FF_SHARD, FF_SHARD)],
        out_specs=[tok(D_MODEL), tok(D_FF), tok(D_FF), tok(D_MODEL)],
        out_shape=[jax.ShapeDtypeStruct((T, D_MODEL), F32), jax.ShapeDtypeStruct((T, D_FF), BF16),
                   jax.ShapeDtypeStruct((T, D_FF), BF16), jax.ShapeDtypeStruct((T, D_MODEL), BF16)],
        compiler_params=_params(("parallel",)),
    )(x1, g2, land_b, land_b, land_b)


def _loss_head(x, gf, target):
    T = x.shape[0]
    tb = 512

    def body(x_ref, g_ref, t_ref, dx_ref, dxb_ref, loss_ref, dg_ref):
        @pl.when(pl.program_id(0) == 0)
        def _():
            loss_ref[...] = jnp.zeros_like(loss_ref)
            dg_ref[...] = jnp.zeros_like(dg_ref)

        xv = x_ref[...]
        r = lax.rsqrt(jnp.mean(xv * xv, axis=-1, keepdims=True) + EPS)
        xh = xv * r
        err = xh * g_ref[...] - t_ref[...]
        per_tok = jnp.mean(err * err, axis=-1, keepdims=True)
        loss_ref[...] += 0.5 * jnp.sum(per_tok, axis=0, keepdims=True)
        dy = err * (1.0 / D_MODEL)
        _row_acc(dg_ref, dy * xh)
        dx = _rms_bwd(dy, xh, r, g_ref[...])
        dx_ref[...] = dx
        dxb_ref[...] = dx.astype(BF16)

    tok = pl.BlockSpec((tb, D_MODEL), lambda i: (i, 0))
    return pl.pallas_call(
        body, name="loss_head", grid=(T // tb,),
        in_specs=[tok, pl.BlockSpec(gf.shape, lambda i: (0, 0)), tok],
        out_specs=[tok, tok, pl.BlockSpec((8, LANES), lambda i: (0, 0)), pl.BlockSpec((8, D_MODEL), lambda i: (0, 0))],
        out_shape=[jax.ShapeDtypeStruct((T, D_MODEL), F32), jax.ShapeDtypeStruct((T, D_MODEL), BF16),
                   jax.ShapeDtypeStruct((8, LANES), F32), jax.ShapeDtypeStruct((8, D_MODEL), F32)],
        compiler_params=_params(("arbitrary",)),
    )(x, gf, target)


def _ffn_bwd(dx2, x1, a, b, g2, land_b):
    T = x1.shape[0]
    tb = 256

    def body(dx2_ref, x_ref, a_ref, b_ref, g_ref, wgt_ref, wut_ref, wd_ref,
             dx1_ref, dx1b_ref, dat_ref, dbt_ref, at_ref, dg_ref):
        @pl.when(pl.program_id(0) == 0)
        def _():
            dg_ref[...] = jnp.zeros_like(dg_ref)

        dx2v = dx2_ref[...]
        av = a_ref[...].astype(F32)
        bv = b_ref[...].astype(F32)
        dact = _mm(dx2v, _whole(wd_ref), NT)
        sa = _sigmoid(av)
        silu = av * sa
        da = (dact * bv * (sa * (1.0 + av * (1.0 - sa)))).astype(BF16)
        db = (dact * silu).astype(BF16)
        dh = _mm(da, _whole(wgt_ref)) + _mm(db, _whole(wut_ref))
        xv = x_ref[...]
        r = lax.rsqrt(jnp.mean(xv * xv, axis=-1, keepdims=True) + EPS)
        xh = xv * r
        _row_acc(dg_ref, dh * xh)
        dx1 = dx2v + _rms_bwd(dh, xh, r, g_ref[...])
        dx1_ref[...] = dx1
        dx1b_ref[...] = dx1.astype(BF16)
        dat_ref[...] = da.T
        dbt_ref[...] = db.T
        at_ref[...] = (silu * bv).astype(BF16).T

    tok = lambda w: pl.BlockSpec((tb, w), lambda i: (i, 0))
    tr = pl.BlockSpec((D_FF, tb), lambda i: (0, i))
    return pl.pallas_call(
        body, name="ffn_bwd", grid=(T // tb,),
        in_specs=[tok(D_MODEL), tok(D_MODEL), tok(D_FF), tok(D_FF), pl.BlockSpec(g2.shape, lambda i: (0, 0)),
                  _shard_rows(land_b, 0, FF_SHARD), _shard_rows(land_b, FF_SHARD, FF_SHARD),
                  _shard_rows(land_b, 2 * FF_SHARD, FF_SHARD)],
        out_specs=[tok(D_MODEL), tok(D_MODEL), tr, tr, tr, pl.BlockSpec((8, D_MODEL), lambda i: (0, 0))],
        out_shape=[jax.ShapeDtypeStruct((T, D_MODEL), F32), jax.ShapeDtypeStruct((T, D_MODEL), BF16)]
        + [jax.ShapeDtypeStruct((D_FF, T), BF16)] * 3 + [jax.ShapeDtypeStruct((8, D_MODEL), F32)],
        compiler_params=_params(("arbitrary",)),
    )(dx2, x1, a, b, g2, land_b, land_b, land_b)


def _wgrad(at, b, bm, bn, name):
    M, T = at.shape
    N = b.shape[1]
    bk = min(T, 1024)

    def body(a_ref, b_ref, o_ref):
        @pl.when(pl.program_id(2) == 0)
        def _():
            o_ref[...] = jnp.zeros_like(o_ref)

        o_ref[...] += jnp.dot(a_ref[...], b_ref[...], preferred_element_type=F32)

    return pl.pallas_call(
        body, name=name, grid=(M // bm, N // bn, T // bk),
        in_specs=[pl.BlockSpec((bm, bk), lambda i, j, kk: (i, kk)), pl.BlockSpec((bk, bn), lambda i, j, kk: (kk, j))],
        out_specs=pl.BlockSpec((bm, bn), lambda i, j, kk: (i, j)),
        out_shape=jax.ShapeDtypeStruct((M, N), F32),
        compiler_params=_params(("parallel", "parallel", "arbitrary")),
    )(at, b)


def _mix_out_bwd(dx1, o, z, sc_in, land_a, gn, scw, gs):
    T = dx1.shape[0]
    tb = 256

    def body(dx_ref, o_ref, z_ref, sc_ref, halo_ref, w_ref, gn_ref, scw_ref, gs_ref,
             do_ref, dz_ref, dgb_ref, dcv_ref, dgn_ref, dgs_ref, dscw_ref):
        @pl.when(pl.program_id(0) == 0)
        def _():
            dgn_ref[...] = jnp.zeros_like(dgn_ref)
            dgs_ref[...] = jnp.zeros_like(dgs_ref)
            dscw_ref[...] = jnp.zeros_like(dscw_ref)

        dmix = _mm(dx_ref[...], _whole(w_ref), NT)
        don = dmix[:, :DN_WIDTH]
        dosc = dmix[:, DN_WIDTH:]
        zv = z_ref[...]
        _, oh, rs, sz, gn4 = _dn_out(o_ref[...], zv, gn_ref[...])
        silu_z = zv * sz
        dgn_full = don * oh * silu_z
        dgn_ref[0:1, :] += jnp.sum(sum(dgn_full[:, HEAD_DIM * hh:HEAD_DIM * (hh + 1)] for hh in range(HEADS)),
                                   axis=0, keepdims=True)
        dz_ref[...] = (don * oh * gn4 * (sz * (1.0 + zv * (1.0 - sz)))).astype(BF16)
        t = don * gn4 * silu_z
        for hh in range(HEADS):
            sl = slice(HEAD_DIM * hh, HEAD_DIM * (hh + 1))
            th, ohh = t[:, sl], oh[:, sl]
            do_ref[:, sl] = rs[hh] * (th - ohh * jnp.mean(th * ohh, axis=-1, keepdims=True))
        halo = jnp.where(pl.program_id(0) > 0, halo_ref[...], 0.0)
        u, cv, gate_b, yh, rys = _sc_fwd(sc_ref[...], halo, scw_ref[...], tb)
        _row_acc(dgs_ref, dosc * yh)
        ty = dosc * gs_ref[...]
        gw = SC_WIDTH // SC_GROUPS
        dys = []
        for gi in range(SC_GROUPS):
            sl = slice(gw * gi, gw * (gi + 1))
            tg, yg = ty[:, sl], yh[:, sl]
            dys.append(rys[gi] * (tg - yg * jnp.mean(tg * yg, axis=-1, keepdims=True)))
        dy = jnp.concatenate(dys, axis=1)
        dgb_ref[...] = dy * cv
        dcv = dy * gate_b
        dcv_ref[...] = dcv
        for j in range(3):
            dscw_ref[j:j + 1, :] += jnp.sum(dcv * u[6 + j:6 + j + tb, :], axis=0, keepdims=True)

    tok = lambda w: pl.BlockSpec((tb, w), lambda i: (i, 0))
    full = lambda t: pl.BlockSpec(t.shape, lambda i: (0, 0))
    acc = lambda w: pl.BlockSpec((8, w), lambda i: (0, 0))
    return pl.pallas_call(
        body, name="mix_out_bwd", grid=(T // tb,),
        in_specs=[tok(D_MODEL), tok(DN_WIDTH), tok(DN_WIDTH), tok(3 * SC_WIDTH),
                  pl.BlockSpec((8, 3 * SC_WIDTH), _before_halo(tb)), _shard_rows(land_a, A_OUT_AT, OUT_SHARD),
                  full(gn), full(scw), full(gs)],
        out_specs=[tok(DN_WIDTH), tok(DN_WIDTH), tok(SC_WIDTH), tok(SC_WIDTH), acc(HEAD_DIM), acc(SC_WIDTH), acc(SC_WIDTH)],
        out_shape=[jax.ShapeDtypeStruct((T, DN_WIDTH), F32), jax.ShapeDtypeStruct((T, DN_WIDTH), BF16),
                   jax.ShapeDtypeStruct((T, SC_WIDTH), F32), jax.ShapeDtypeStruct((T, SC_WIDTH), F32),
                   jax.ShapeDtypeStruct((8, HEAD_DIM), F32), jax.ShapeDtypeStruct((8, SC_WIDTH), F32),
                   jax.ShapeDtypeStruct((8, SC_WIDTH), F32)],
        compiler_params=_params(("arbitrary",)),
    )(dx1, o, z, sc_in, sc_in, land_a, gn, scw, gs)


def _sc_conv_bwd(dcv, dgb, sc_in, scw):
    T = dcv.shape[0]
    tb = 512

    def body(dcv_ref, halo_ref, dgb_ref, sc_ref, w_ref, out_ref):
        last = pl.program_id(0) == pl.num_programs(0) - 1
        halo = jnp.where(last, 0.0, halo_ref[...])
        xc = jnp.concatenate([dcv_ref[...], halo], axis=0)
        w = w_ref[...]
        du = w[2:3, :] * xc[0:tb, :] + w[1:2, :] * xc[1:tb + 1, :] + w[0:1, :] * xc[2:tb + 2, :]
        sc = sc_ref[...]
        out_ref[:, :SC_WIDTH] = dgb_ref[...].astype(BF16)
        out_ref[:, SC_WIDTH:2 * SC_WIDTH] = (du * sc[:, 2 * SC_WIDTH:]).astype(BF16)
        out_ref[:, 2 * SC_WIDTH:] = (du * sc[:, SC_WIDTH:2 * SC_WIDTH]).astype(BF16)

    tok = lambda w: pl.BlockSpec((tb, w), lambda i: (i, 0))
    return pl.pallas_call(
        body, name="sc_conv_bwd", grid=(T // tb,),
        in_specs=[tok(SC_WIDTH), pl.BlockSpec((8, SC_WIDTH), _after_halo(tb, T)), tok(SC_WIDTH), tok(3 * SC_WIDTH),
                  pl.BlockSpec(scw.shape, lambda i: (0, 0))],
        out_specs=tok(3 * SC_WIDTH),
        out_shape=jax.ShapeDtypeStruct((T, 3 * SC_WIDTH), BF16),
        compiler_params=_params(("parallel",)),
    )(dcv, dcv, dgb, sc_in, scw)


def _delta_bwd(q, k, v, bg, states, do):
    T = q.shape[0]
    tb = 512
    n_chunk = tb // CHUNK
    nb = T // tb

    def body(q_ref, k_ref, v_ref, bg_ref, st_ref, do_ref, dq_ref, dk_ref, dv_ref, dbg_ref, ds_ref):
        @pl.when(pl.program_id(0) == 0)
        def _():
            ds_ref[...] = jnp.zeros_like(ds_ref)

        masks = _chunk_masks()
        causal, strict = masks
        lane = lax.broadcasted_iota(jnp.int32, (CHUNK, LANES), 1)
        last_row = lax.broadcasted_iota(jnp.int32, (CHUNK, 1), 0) == CHUNK - 1
        cat = jnp.concatenate
        heads = range(HEADS)

        def chunk(cj, carry):
            ci = n_chunk - 1 - cj
            rows = pl.ds(pl.multiple_of(ci * CHUNK, CHUNK), CHUNK)
            loc = _units_local(_chunk_units(q_ref, k_ref, v_ref, bg_ref, rows), masks)
            dov = do_ref[rows, :]
            do = [dov[:, HEAD_DIM * h:HEAD_DIM * (h + 1)] for h in heads]
            state = [st_ref[ci, h] for h in heads]
            ds_next = [ds_ref[h] for h in heads]
            w_s = [_mm(p["w"], s) for p, s in zip(loc, state)]
            dq_dec = [_mm(d, s, NT) for d, s in zip(do, state)]
            qk_do = [_mm(p["qk"], d, TN) for p, d in zip(loc, do)]
            kd_ds = [_mm(p["k_dec"], d) for p, d in zip(loc, ds_next)]
            qd_do = [_mm(p["q_dec"], d, TN) for p, d in zip(loc, do)]
            v_new = [p["u"] - t for p, t in zip(loc, w_s)]
            dv_new = [a + b for a, b in zip(qk_do, kd_ds)]
            dqk = [jnp.where(causal, _mm(d, vn, NT), 0.0) for d, vn in zip(do, v_new)]
            dk_dec = [_mm(vn, d, NT) for vn, d in zip(v_new, ds_next)]
            w_dv = [_mm(p["w"], dvn, TN) for p, dvn in zip(loc, dv_new)]
            dw = [-_mm(dvn, s, NT) for dvn, s in zip(dv_new, state)]
            for h in heads:
                ds_ref[h] = loc[h]["gl"] * ds_next[h] + qd_do[h] - w_dv[h]
            dtm = [_mm(cat([dvn, d], axis=1), cat([p["vb"], p["kbg"]], axis=1), NT) for dvn, d, p in zip(dv_new, dw, loc)]
            x_t = [_mm(p["xm"], cat([dvn, d], axis=1), TN) for p, dvn, d in zip(loc, dv_new, dw)]
            dvb = [dvn + t[:, :HEAD_DIM] for dvn, t in zip(dv_new, x_t)]
            dkbg = [d + t[:, HEAD_DIM:] for d, t in zip(dw, x_t)]
            y = [t + _mm(p["xm"], t, TN) for p, t in zip(loc, dtm)]
            dlow = [jnp.where(strict, -(t + _mm(t, p["xm"], NT)), 0.0) for p, t in zip(loc, y)]
            dmm = [d * p["decay"] for d, p in zip(dlow, loc)]
            dnn = [d * p["decay"] for d, p in zip(dqk, loc)]
            by_k = [_mm(cat([a, b], axis=0), p["k"]) for a, b, p in zip(dmm, dnn, loc)]
            dk_mm = [_mm(cat([a, b], axis=0), cat([p["kb"], p["q"]], axis=0), TN) for a, b, p in zip(dmm, dnn, loc)]
            dq_out, dk_out, dv_out = [], [], []
            dbeta_all = jnp.zeros((CHUNK, LANES), F32)
            dgc_all = jnp.zeros((CHUNK, LANES), F32)
            for h in heads:
                p = loc[h]
                dkb = by_k[h][:CHUNK] + dkbg[h] * p["eg"]
                dq_out.append(by_k[h][CHUNK:] + dq_dec[h] * p["eg"])
                dk_out.append(dk_mm[h] + dk_dec[h] * p["ek"] + dkb * p["beta"])
                dv_out.append(dvb[h] * p["beta"])
                dbeta = jnp.sum(dkb * p["k"] + dvb[h] * p["v"], axis=1, keepdims=True)
                e = dlow[h] * p["low"] + dqk[h] * p["qk"]
                kd = jnp.sum(dk_dec[h] * p["k_dec"], axis=1, keepdims=True)
                dgc = (jnp.sum(e, axis=1, keepdims=True) - jnp.sum(e.T, axis=1, keepdims=True)
                       + jnp.sum(dq_dec[h] * p["q_dec"], axis=1, keepdims=True) - kd
                       + jnp.sum(dkbg[h] * p["kbg"], axis=1, keepdims=True))
                dgl = jnp.sum(jnp.sum(ds_next[h] * state[h], axis=1, keepdims=True), axis=0, keepdims=True)
                d_last = jnp.sum(kd, axis=0, keepdims=True) + dgl * p["gl"]
                dgc = dgc + jnp.where(last_row, d_last, 0.0)
                dbeta_all = jnp.where(lane == h, dbeta, dbeta_all)
                dgc_all = jnp.where(lane == h + HEADS, dgc, dgc_all)
            dq_ref[rows, :] = cat(dq_out, axis=1)
            dk_ref[rows, :] = cat(dk_out, axis=1)
            dv_ref[rows, :] = cat(dv_out, axis=1)
            dbg_ref[rows, :] = dbeta_all + dgc_all
            return carry

        lax.fori_loop(0, n_chunk, chunk, 0)

    tok = lambda w: pl.BlockSpec((tb, w), lambda i: (nb - 1 - i, 0))
    return pl.pallas_call(
        body, name="delta_bwd", grid=(nb,),
        in_specs=[tok(DN_WIDTH), tok(DN_WIDTH), tok(DN_WIDTH), tok(LANES),
                  pl.BlockSpec((n_chunk, HEADS, HEAD_DIM, HEAD_DIM), lambda i: (nb - 1 - i, 0, 0, 0)), tok(DN_WIDTH)],
        out_specs=[tok(DN_WIDTH), tok(DN_WIDTH), tok(DN_WIDTH), tok(LANES)],
        out_shape=[jax.ShapeDtypeStruct((T, DN_WIDTH), F32)] * 3 + [jax.ShapeDtypeStruct((T, LANES), F32)],
        scratch_shapes=[pltpu.VMEM((HEADS, HEAD_DIM, HEAD_DIM), F32)],
        compiler_params=_params(("arbitrary",)),
    )(q, k, v, bg, states, do)


def _dn_prep_bwd(dq, dk, dv, dbg, qkv, cw, bd, al_row, dt_row):
    T = qkv.shape[0]
    tb = 256

    def body(dq_ref, dk_ref, dv_ref, dbg_ref, pre_ref, halo_ref, cw_ref, bd_ref, al_ref, dt_ref,
             dc_ref, dbd_ref, dcw_ref, dal_ref, ddt_ref):
        @pl.when(pl.program_id(0) == 0)
        def _():
            dcw_ref[...] = jnp.zeros_like(dcw_ref)
            dal_ref[...] = jnp.zeros_like(dal_ref)
            ddt_ref[...] = jnp.zeros_like(ddt_ref)

        halo = jnp.where(pl.program_id(0) > 0, halo_ref[...], 0.0)
        xc, c, sg, a = _dn_act(pre_ref[...], halo, cw_ref[...], tb)
        dsilu = sg * (1.0 + c * (1.0 - sg))
        for hh in range(HEADS):
            sl = slice(HEAD_DIM * hh, HEAD_DIM * (hh + 1))
            for base, g_ref, scale in ((0, dq_ref, Q_SCALE), (DN_WIDTH, dk_ref, 1.0)):
                sa = slice(base + HEAD_DIM * hh, base + HEAD_DIM * (hh + 1))
                raw = a[:, sa]
                r = lax.rsqrt(jnp.sum(raw * raw, axis=-1, keepdims=True) + EPS)
                nrm = raw * r
                gn_ = g_ref[:, sl] * scale
                dc_ref[:, sa] = r * (gn_ - nrm * jnp.sum(gn_ * nrm, axis=-1, keepdims=True)) * dsilu[:, sa]
        dc_ref[:, 2 * DN_WIDTH:] = dv_ref[...] * dsilu[:, 2 * DN_WIDTH:]
        dc = dc_ref[...]
        for j in range(4):
            dcw_ref[j:j + 1, :] += jnp.sum(dc * xc[5 + j:5 + j + tb, :], axis=0, keepdims=True)
        bdv = bd_ref[...]
        lane = lax.broadcasted_iota(jnp.int32, bdv.shape, 1)
        is_b = lane < HEADS
        dbg_in = dbg_ref[...]
        dbgv = jnp.where(is_b, dbg_in, _mm32(_chunk_cumsum_matrix(tb), dbg_in, TN))
        is_g = jnp.logical_and(lane >= HEADS, lane < 2 * HEADS)
        beta = _sigmoid(bdv)
        neg_a = -jnp.exp(al_ref[...])
        pre_sp = bdv + dt_ref[...]
        g = neg_a * _softplus(pre_sp)
        da_in = dbgv * neg_a * _sigmoid(pre_sp)
        dbd_ref[...] = jnp.where(is_b, dbgv * beta * (1.0 - beta), jnp.where(is_g, da_in, 0.0)).astype(BF16)
        _row_acc(dal_ref, jnp.where(is_g, dbgv * g, 0.0))
        _row_acc(ddt_ref, jnp.where(is_g, da_in, 0.0))

    tok = lambda w: pl.BlockSpec((tb, w), lambda i: (i, 0))
    full = lambda t: pl.BlockSpec(t.shape, lambda i: (0, 0))
    acc = lambda w: pl.BlockSpec((8, w), lambda i: (0, 0))
    return pl.pallas_call(
        body, name="dn_prep_bwd", grid=(T // tb,),
        in_specs=[tok(DN_WIDTH), tok(DN_WIDTH), tok(DN_WIDTH), tok(LANES),
                  tok(QKV), pl.BlockSpec((8, QKV), _before_halo(tb)), full(cw), tok(LANES), full(al_row), full(dt_row)],
        out_specs=[tok(QKV), tok(LANES), acc(QKV), acc(LANES), acc(LANES)],
        out_shape=[jax.ShapeDtypeStruct((T, QKV), F32), jax.ShapeDtypeStruct((T, LANES), BF16),
                   jax.ShapeDtypeStruct((8, QKV), F32), jax.ShapeDtypeStruct((8, LANES), F32),
                   jax.ShapeDtypeStruct((8, LANES), F32)],
        compiler_params=_params(("arbitrary",)),
    )(dq, dk, dv, dbg, qkv, qkv, cw, bd, al_row, dt_row)


def _dn_conv_bwd(dc, cw):
    T = dc.shape[0]
    tb = 512

    def body(dc_ref, halo_ref, w_ref, out_ref):
        last = pl.program_id(0) == pl.num_programs(0) - 1
        halo = jnp.where(last, 0.0, halo_ref[...])
        xc = jnp.concatenate([dc_ref[...], halo], axis=0)
        w = w_ref[...]
        acc = w[3:4, :] * xc[0:tb, :]
        for j in range(3):
            acc = acc + w[j:j + 1, :] * xc[3 - j:3 - j + tb, :]
        out_ref[...] = acc.astype(BF16)

    tok = pl.BlockSpec((tb, QKV), lambda i: (i, 0))
    return pl.pallas_call(
        body, name="dn_conv_bwd", grid=(T // tb,),
        in_specs=[tok, pl.BlockSpec((8, QKV), _after_halo(tb, T)), pl.BlockSpec(cw.shape, lambda i: (0, 0))],
        out_specs=tok,
        out_shape=jax.ShapeDtypeStruct((T, QKV), BF16),
        compiler_params=_params(("parallel",)),
    )(dc, dc, cw)


def _in_proj_bwd(dqkv, dz, dsc, dbd, dx1, x, g1, wa, wbd):
    T = x.shape[0]
    tb = 256

    def body(dqkv_ref, dz_ref, dsc_ref, dbd_ref, dx1_ref, x_ref, g_ref, wa_ref, wbd_ref, dx_ref, dxb_ref, dg_ref):
        @pl.when(pl.program_id(0) == 0)
        def _():
            dg_ref[...] = jnp.zeros_like(dg_ref)

        dh = (_mm(dqkv_ref[...], wa_ref[:, :QKV], NT) + _mm(dz_ref[...], wa_ref[:, QKV:QKV + DN_WIDTH], NT)
              + _mm(dsc_ref[...], wa_ref[:, QKV + DN_WIDTH:], NT) + _mm(dbd_ref[...], wbd_ref[...], NT))
        xv = x_ref[...]
        r = lax.rsqrt(jnp.mean(xv * xv, axis=-1, keepdims=True) + EPS)
        xh = xv * r
        _row_acc(dg_ref, dh * xh)
        dx = dx1_ref[...] + _rms_bwd(dh, xh, r, g_ref[...])
        dx_ref[...] = dx
        dxb_ref[...] = dx.astype(BF16)

    tok = lambda w: pl.BlockSpec((tb, w), lambda i: (i, 0))
    full = lambda t: pl.BlockSpec(t.shape, lambda i: (0, 0))
    return pl.pallas_call(
        body, name="in_proj_bwd", grid=(T // tb,),
        in_specs=[tok(QKV), tok(DN_WIDTH), tok(3 * SC_WIDTH), tok(LANES), tok(D_MODEL), tok(D_MODEL),
                  full(g1), full(wa), full(wbd)],
        out_specs=[tok(D_MODEL), tok(D_MODEL), pl.BlockSpec((8, D_MODEL), lambda i: (0, 0))],
        out_shape=[jax.ShapeDtypeStruct((T, D_MODEL), F32), jax.ShapeDtypeStruct((T, D_MODEL), BF16),
                   jax.ShapeDtypeStruct((8, D_MODEL), F32)],
        compiler_params=_params(("arbitrary",)),
    )(dqkv, dz, dsc, dbd, dx1, x, g1, wa, wbd)


def _pad_rows(a, rows=8):
    return jnp.pad(a, ((0, rows - a.shape[0]), (0, 0)))


def _gate_rows(a_log, dt_bias):
    put = lambda t: jnp.pad(t.reshape(1, HEADS), ((0, 0), (HEADS, LANES - 2 * HEADS)))
    return put(a_log), put(dt_bias)


def _split_w_in(w_in):
    o = QKV + DN_WIDTH
    wa = jnp.concatenate([w_in[:, :o], w_in[:, o + 2 * HEADS:]], axis=1)
    wbd = jnp.pad(w_in[:, o:o + 2 * HEADS], ((0, 0), (0, LANES - 2 * HEADS)))
    return wa, wbd


def _mixer_fwd(x, p):
    qkv, z, sc_in, bd, ht = _in_proj(x, p["g1"], p["wa"], p["wbd"])
    q, k, v, bg = _dn_prep(qkv, p["cw"], bd, p["al"], p["dt"])
    o, states = _delta_fwd(q, k, v, bg)
    x1, mt = _mix_out(o, z, sc_in, x, p["land_a"], p["gn"], p["scw"], p["gs"])
    return x1, dict(x=x, qkv=qkv, z=z, sc_in=sc_in, bd=bd, ht=ht, q=q, k=k, v=v, bg=bg, o=o, states=states, mt=mt)


def _ffn_fwd(x1, p, land_b):
    x2, a, b, h2 = _ffn(x1, p["g2"], land_b)
    return x2, dict(x1=x1, a=a, b=b, h2=h2)


def _ffn_back(dx2, dx2_bf16, s, p, land_b):
    dx1, dx1_bf16, da_t, db_t, act_t, dg2 = _ffn_bwd(dx2, s["x1"], s["a"], s["b"], p["g2"], land_b)
    g = dict(wd=_wgrad(act_t, dx2_bf16, FF_SHARD, 1024, "wgrad_down"), wgt=_wgrad(da_t, s["h2"], FF_SHARD, 1024, "wgrad_gate"),
             wut=_wgrad(db_t, s["h2"], FF_SHARD, 1024, "wgrad_up"), g2=dg2[0])
    return dx1, dx1_bf16, g


def _mixer_bwd(dx1, dx1_bf16, s, p):
    do, dz, dgb, dcv, dgn, dgs, dscw = _mix_out_bwd(dx1, s["o"], s["z"], s["sc_in"], p["land_a"], p["gn"], p["scw"], p["gs"])
    g = dict(w_out=_wgrad(s["mt"], dx1_bf16, 512, 1024, "wgrad_out"))
    dsc = _sc_conv_bwd(dcv, dgb, s["sc_in"], p["scw"])
    dq, dk, dv, dbg = _delta_bwd(s["q"], s["k"], s["v"], s["bg"], s["states"], do)
    dc, dbd, dcw, dal, ddt = _dn_prep_bwd(dq, dk, dv, dbg, s["qkv"], p["cw"], s["bd"], p["al"], p["dt"])
    dqkv = _dn_conv_bwd(dc, p["cw"])
    dx, dx_bf16, dg1 = _in_proj_bwd(dqkv, dz, dsc, dbd, dx1, s["x"], p["g1"], p["wa"], p["wbd"])
    g["w_in"] = jnp.concatenate([
        _wgrad(s["ht"], dqkv, 512, 768, "wgrad_qkv"), _wgrad(s["ht"], dz, 512, 512, "wgrad_z"),
        _wgrad(s["ht"], dbd, 512, LANES, "wgrad_bd")[:, :2 * HEADS], _wgrad(s["ht"], dsc, 512, 768, "wgrad_sc")], axis=1)
    g.update(g1=dg1[0], gn=dgn[0], gs=dgs[0], scw=dscw[:3], cw=dcw[:4],
             al=dal[0, HEADS:2 * HEADS], dt=ddt[0, HEADS:2 * HEADS])
    return dx, dx_bf16, g


def _place():
    return lax.axis_index("x"), lax.axis_index("y"), lax.axis_index("c")


def _other_chips(x, y):
    return [(1 - x, y), (x, 1 - y), (1 - x, 1 - y)]


_HBM = pl.BlockSpec(memory_space=pltpu.HBM)


def _chip_exchange(arrs, name, gather):
    n = len(arrs)

    def body(*refs):
        ins, outs = refs[:n], refs[n:2 * n]
        send_sems, recv_sems, local_sems = refs[2 * n:]
        x, y, c = _place()
        me = 2 * x + y
        others = _other_chips(x, y)

        def remote(k, j, landing):
            px, py = others[j]
            src = ins[k] if gather else ins[k].at[2 * px + py]
            return pltpu.make_async_remote_copy(src_ref=src, dst_ref=outs[k].at[landing], send_sem=send_sems.at[k, j],
                                                recv_sem=recv_sems.at[k, j], device_id=(px, py, c), device_id_type=MESH)

        local = [pltpu.make_async_copy(ins[k] if gather else ins[k].at[me], outs[k].at[me], local_sems.at[k])
                 for k in range(n)]
        sends = [remote(k, j, me) for k in range(n) for j in range(3)]
        for cp in local + sends:
            cp.start()
        for k in range(n):
            for j, (px, py) in enumerate(others):
                remote(k, j, 2 * px + py).wait_recv()
        for cp in sends:
            cp.wait_send()
        for cp in local:
            cp.wait()

    shapes = [jax.ShapeDtypeStruct(((N_CHIPS,) + a.shape) if gather else a.shape, a.dtype) for a in arrs]
    return pl.pallas_call(
        body, name=name, in_specs=[_HBM] * n, out_specs=[_HBM] * n, out_shape=shapes,
        scratch_shapes=[pltpu.SemaphoreType.DMA((n, 3)), pltpu.SemaphoreType.DMA((n, 3)), pltpu.SemaphoreType.DMA((n,))],
    )(*arrs)


_SEM = pl.BlockSpec(memory_space=pltpu.SEMAPHORE)
_ANY = pl.BlockSpec(memory_space=pl.ANY)
_EFFECT = pltpu.SideEffectType.DATAFLOW_SIDE_EFFECTING


def _split_copies(src_ref, land_ref, send_sems, recv_sems, gather, sending):
    x, y, c = _place()
    me = 2 * x + y
    copies = []
    for j, (px, py) in enumerate(_other_chips(x, y)):
        peer = 2 * px + py
        copies.append(pltpu.make_async_remote_copy(
            src_ref=src_ref if gather else src_ref.at[peer], dst_ref=land_ref.at[me if sending else peer],
            send_sem=send_sems.at[j], recv_sem=recv_sems.at[j], device_id=(px, py, c), device_id_type=MESH))
    return copies


def _own_slot(share):
    chip = 2 * lax.axis_index("x") + lax.axis_index("y")
    return lax.dynamic_update_slice(lax.empty((N_CHIPS,) + share.shape, share.dtype), share[None], (chip, 0, 0))


def _exchange_start(src, land, after, name, gather):
    def body(src_ref, land_ref, after_ref, send_sems, recv_sems, src_thru, land_thru, token):
        for cp in _split_copies(src_ref, land_ref, send_sems, recv_sems, gather, sending=True):
            cp.start()
        token[...] = jnp.zeros_like(token)

    hbm = lambda t: pltpu.with_memory_space_constraint(t, pltpu.HBM)
    return pl.pallas_call(
        body, name=name,
        out_shape=(pltpu.SemaphoreType.DMA((3,)), pltpu.SemaphoreType.DMA((3,)), pltpu.HBM(src.shape, src.dtype),
                   pltpu.HBM(land.shape, land.dtype), jax.ShapeDtypeStruct((8, LANES), F32)),
        in_specs=(_HBM, _HBM, _ANY), out_specs=(_SEM, _SEM, _HBM, _HBM, pl.BlockSpec(memory_space=pltpu.VMEM)),
        input_output_aliases={0: 2, 1: 3},
        compiler_params=pltpu.CompilerParams(has_side_effects=_EFFECT),
    )(hbm(src), hbm(land), after)


def _exchange_wait(started, after, name, gather):
    send_sems, recv_sems, src_thru, land_thru, _ = started

    def body(src_ref, land_ref, send_sems, recv_sems, after_ref, src_dead, got_ref):
        for cp in _split_copies(src_ref, land_ref, send_sems, recv_sems, gather, sending=False):
            cp.wait_send()
            cp.wait_recv()

    return pl.pallas_call(
        body, name=name,
        out_shape=(pltpu.HBM(src_thru.shape, src_thru.dtype), pltpu.HBM(land_thru.shape, land_thru.dtype)),
        in_specs=(_HBM, _HBM, _SEM, _SEM, _ANY), out_specs=(_HBM, _HBM), input_output_aliases={0: 0, 1: 1},
        compiler_params=pltpu.CompilerParams(has_side_effects=_EFFECT),
    )(src_thru, land_thru, send_sems, recv_sems, after)[1]


def _swap_sibling(arrs):
    n = len(arrs)

    def body(*refs):
        ins, outs = refs[:n], refs[n:2 * n]
        send_sems, recv_sems = refs[2 * n:]
        x, y, c = _place()
        copies = [pltpu.make_async_remote_copy(src_ref=ins[k], dst_ref=outs[k], send_sem=send_sems.at[k],
                                               recv_sem=recv_sems.at[k], device_id=(x, y, 1 - c), device_id_type=MESH)
                  for k in range(n)]
        for cp in copies:
            cp.start()
        for cp in copies:
            cp.wait()

    return pl.pallas_call(
        body, name="swap_sibling", in_specs=[_HBM] * n, out_specs=[_HBM] * n,
        out_shape=[jax.ShapeDtypeStruct(a.shape, a.dtype) for a in arrs],
        scratch_shapes=[pltpu.SemaphoreType.DMA((n,)), pltpu.SemaphoreType.DMA((n,))],
    )(*arrs)


def _all_reduce_small(v):
    rows = v.shape[0]
    flips = [(a, b, cc) for a in (0, 1) for b in (0, 1) for cc in (0, 1)][1:]

    def body(v_ref, out_ref, buf_ref, send_sems, recv_sems):
        x, y, c = _place()
        me = 4 * x + 2 * y + c
        peers = [((1 - x) if a else x, (1 - y) if b else y, (1 - c) if cc else c) for a, b, cc in flips]

        def copy(j, landing):
            return pltpu.make_async_remote_copy(src_ref=v_ref, dst_ref=buf_ref.at[landing], send_sem=send_sems.at[j],
                                                recv_sem=recv_sems.at[j], device_id=peers[j], device_id_type=MESH)

        sends = [copy(j, me) for j in range(N_DEV - 1)]
        for cp in sends:
            cp.start()
        buf_ref[me] = v_ref[...]
        for j, (px, py, pc) in enumerate(peers):
            copy(j, 4 * px + 2 * py + pc).wait_recv()
        for cp in sends:
            cp.wait_send()
        acc = buf_ref[0]
        for d in range(1, N_DEV):
            acc = acc + buf_ref[d]
        out_ref[...] = acc

    vmem = pl.BlockSpec(memory_space=pltpu.VMEM)
    return pl.pallas_call(
        body, name="all_reduce_small", in_specs=[vmem], out_specs=vmem,
        out_shape=jax.ShapeDtypeStruct(v.shape, F32),
        scratch_shapes=[pltpu.VMEM((N_DEV, rows, LANES), F32), pltpu.SemaphoreType.DMA((N_DEV - 1,)),
                        pltpu.SemaphoreType.DMA((N_DEV - 1,))],
    )(v)


def _row_block(*sizes):
    return next(t for t in (256, 192, 128, 64) if all(s % t == 0 for s in sizes))


def _sum_chips(parts, name):
    _, rows, cols = parts[0].shape
    n = len(parts)
    tr = _row_block(rows)

    def body(*refs):
        o_ref = refs[n]
        for l in range(n):
            @pl.when(pl.program_id(0) == l)
            def _(p_ref=refs[l]):
                acc = p_ref[0].astype(F32)
                for s in range(1, N_CHIPS):
                    acc = acc + p_ref[s].astype(F32)
                o_ref[0] = acc

    return pl.pallas_call(
        body, name=name, grid=(n, rows // tr),
        in_specs=[pl.BlockSpec((N_CHIPS, tr, cols), lambda l, i, k=k: (0, jnp.where(l == k, i, 0), 0)) for k in range(n)],
        out_specs=pl.BlockSpec((1, tr, cols), lambda l, i: (l, i, 0)),
        out_shape=jax.ShapeDtypeStruct((n, rows, cols), F32),
        compiler_params=_params(("arbitrary", "arbitrary")),
    )(*parts)


def _adam_update(w, m, v, g):
    c1 = 1.0 - ADAM_B1 ** ADAM_STEP
    c2 = 1.0 - ADAM_B2 ** ADAM_STEP
    m_new = ADAM_B1 * m + (1.0 - ADAM_B1) * g
    v_new = ADAM_B2 * v + (1.0 - ADAM_B2) * (g * g)
    return -ADAM_LR * ((m_new / c1) / (jnp.sqrt(v_new / c2) + ADAM_EPS) + ADAM_WD * w), m_new, v_new


def _adamw_rows(w, m, v, g_parts, first, name):
    n_layers, rows, cols = w.shape
    tr = _row_block(rows, first)
    n = len(g_parts)

    def body(*refs):
        w_ref, m_ref, v_ref = refs[:3]
        g_out, d_out, m_out, v_out = refs[3 + n:]
        g = refs[3][...]
        for r in refs[4:3 + n]:
            g = g + r[...]
        g = g[:, :, :cols]
        d_out[...], m_out[...], v_out[...] = _adam_update(w_ref[...], m_ref[...], v_ref[...], g)
        g_out[...] = g

    blk = pl.BlockSpec((1, tr, cols), lambda l, i: (l, i, 0))
    g_blk = pl.BlockSpec((1, tr, g_parts[0].shape[2]), lambda l, i: (l, first // tr + i, 0))
    return pl.pallas_call(
        body, name=name, grid=(n_layers, rows // tr),
        in_specs=[blk] * 3 + [g_blk] * n, out_specs=[blk] * 4,
        out_shape=[jax.ShapeDtypeStruct(w.shape, F32)] * 4,
        compiler_params=_params(("parallel", "parallel")),
    )(w, m, v, *g_parts)


def _adamw(w, m, v, g_parts, name):
    rows, cols = w.shape
    tr = min(rows, 256)
    n = len(g_parts)

    def body(*refs):
        w_ref, m_ref, v_ref = refs[:3]
        g_refs = refs[3:3 + n]
        g_out, d_out, m_out, v_out = refs[3 + n:]
        g = g_refs[0][...]
        for r in g_refs[1:]:
            g = g + r[...]
        d_out[...], m_out[...], v_out[...] = _adam_update(w_ref[...], m_ref[...], v_ref[...], g)
        g_out[...] = g

    blk = pl.BlockSpec((tr, cols), lambda i: (i, 0))
    return pl.pallas_call(
        body, name=name, grid=(rows // tr,),
        in_specs=[blk] * (3 + n), out_specs=[blk] * 4,
        out_shape=[jax.ShapeDtypeStruct((rows, cols), F32)] * 4,
        compiler_params=_params(("parallel",)),
    )(w, m, v, *g_parts)


def _pack(parts, rows, fill=0.0):
    flat = jnp.concatenate([p.reshape(-1) for p in parts])
    return jnp.pad(flat, (0, rows * LANES - flat.shape[0]), constant_values=fill).reshape(rows, LANES)


def _unpack(packed, shapes):
    flat = packed.reshape(-1)
    out, at = [], 0
    for shp in shapes:
        size = 1
        for s in shp:
            size *= s
        out.append(flat[at:at + size].reshape(shp))
        at += size
    return out


def _packed_rows(shapes):
    total = 0
    for shp in shapes:
        size = 1
        for s in shp:
            size *= s
        total += size
    return -(-total // (8 * LANES)) * 8


def _cols_full(g, l):
    t = g[:, l]
    return jnp.moveaxis(t, 0, 1).reshape(t.shape[1], N_CHIPS * t.shape[2])


def _pad_cols(t):
    return jnp.pad(t, ((0, 0),) * (t.ndim - 1) + ((0, D_MODEL - t.shape[-1]),))


def _w_in_of(land_a):
    return jnp.moveaxis(land_a[:, :D_MODEL, :IN_SHARD], 0, 1).reshape(D_MODEL, W_IN_COLS)


def _parts_a(g_w_in, g_w_out):
    cols = jnp.moveaxis(g_w_in.reshape(D_MODEL, N_CHIPS, IN_SHARD), 1, 0)
    return jnp.concatenate([_pad_cols(cols), g_w_out.reshape(N_CHIPS, OUT_SHARD, D_MODEL)], axis=1).astype(BF16)


def _parts_b(g_wgt, g_wut, g_wd):
    return jnp.concatenate([t.reshape(N_CHIPS, FF_SHARD, D_MODEL) for t in (g_wgt, g_wut, g_wd)], axis=1).astype(BF16)


def kernel(x, norm1_g, w_in, dn_conv_w, dn_a_log, dn_dt_bias, dn_norm_g, sc_conv_w, sc_norm_g, w_out, norm2_g, ffn_w_gate, ffn_w_up, ffn_w_down, final_norm_g, loss_target, m_norm1_g, m_w_in, m_dn_conv_w, m_dn_a_log, m_dn_dt_bias, m_dn_norm_g, m_sc_conv_w, m_sc_norm_g, m_w_out, m_norm2_g, m_ffn_w_gate, m_ffn_w_up, m_ffn_w_down, m_final_norm_g, v_norm1_g, v_w_in, v_dn_conv_w, v_dn_a_log, v_dn_dt_bias, v_dn_norm_g, v_sc_conv_w, v_sc_norm_g, v_w_out, v_norm2_g, v_ffn_w_gate, v_ffn_w_up, v_ffn_w_down, v_final_norm_g):
    chip = 2 * lax.axis_index("x") + lax.axis_index("y")

    g_cw, g_scw = _chip_exchange([dn_conv_w, sc_conv_w], "gather_conv", gather=True)

    t_last = lambda t: jnp.swapaxes(t, -1, -2)
    gate_t, up_t = t_last(ffn_w_gate), t_last(ffn_w_up)
    share_a = [jnp.concatenate([_pad_cols(w_in[l]), w_out[l]], axis=0).astype(BF16) for l in range(DEPTH)]
    share_b = [jnp.concatenate([gate_t[l], up_t[l], ffn_w_down[l]], axis=0).astype(BF16) for l in range(DEPTH)]
    zero_token = jnp.zeros((8, LANES), F32)

    def gather_start(l, after):
        a = _exchange_start(share_a[l], _own_slot(share_a[l]), after, "gather_a_start_%d" % l, gather=True)
        b = _exchange_start(share_b[l], _own_slot(share_b[l]), a[4], "gather_b_start_%d" % l, gather=True)
        return a, b

    ga, gb = gather_start(0, zero_token)
    land_a = _exchange_wait(ga, gb[4], "gather_a_wait_0", gather=True)
    act = x[0]
    layers, saved_m, saved_f, lands_b = [], [], [], []
    for l in range(DEPTH):
        hold = 0.0
        if l + 1 < DEPTH:
            ga, gb_next = gather_start(l + 1, land_a)
            hold = gb_next[4][0:1, 0:1]
        wa, wbd = _split_w_in(_w_in_of(land_a))
        al, dt = _gate_rows(dn_a_log[l], dn_dt_bias[l])
        layers.append(dict(
            g1=norm1_g[l][None] + hold, wa=wa, wbd=wbd, cw=_pad_rows(_cols_full(g_cw, l)), al=al, dt=dt,
            gn=dn_norm_g[l][None], scw=_pad_rows(_cols_full(g_scw, l)), gs=sc_norm_g[l][None],
            land_a=land_a, g2=norm2_g[l][None]))
        x1, s = _mixer_fwd(act, layers[l])
        saved_m.append(s)
        lands_b.append(_exchange_wait(gb, x1, "gather_b_wait_%d" % l, gather=True))
        act, s = _ffn_fwd(x1, layers[l], lands_b[l])
        saved_f.append(s)
        if l + 1 < DEPTH:
            land_a = _exchange_wait(ga, act, "gather_a_wait_%d" % (l + 1), gather=True)
            gb = gb_next

    dact, dact_bf16, loss_part, d_final = _loss_head(act, final_norm_g[None], loss_target[0])
    grads, reduce_a, reduce_b = [None] * DEPTH, [None] * DEPTH, [None] * DEPTH
    hold = 0.0
    for l in reversed(range(DEPTH)):
        p = layers[l]
        dx1, dx1_bf16, g = _ffn_back(dact, dact_bf16, saved_f[l], dict(p, g2=p["g2"] + hold), lands_b[l])
        parts = _parts_b(g["wgt"], g["wut"], g["wd"])
        reduce_b[l] = _exchange_start(parts, parts, zero_token, "reduce_b_start_%d" % l, gather=False)
        dact, dact_bf16, gm = _mixer_bwd(dx1, dx1_bf16, saved_m[l], dict(p, gn=p["gn"] + reduce_b[l][4][0:1, 0:1]))
        parts = _parts_a(gm["w_in"], gm["w_out"])
        reduce_a[l] = _exchange_start(parts, parts, zero_token, "reduce_a_start_%d" % l, gather=False)
        hold = reduce_a[l][4][0:1, 0:1]
        grads[l] = dict(gm, g2=g["g2"])
    loss = lax.psum(loss_part[0, 0], ("x", "y", "c"))
    stack = lambda key: jnp.stack([grads[l][key] for l in range(DEPTH)])

    got_b = [_exchange_wait(reduce_b[l], dact, "reduce_b_wait_%d" % l, gather=False) for l in reversed(range(DEPTH))][::-1]
    got_a = [_exchange_wait(reduce_a[l], dact, "reduce_a_wait_%d" % l, gather=False) for l in reversed(range(DEPTH))][::-1]
    sum_a, sum_b = _sum_chips(got_a, "sum_chips_a"), _sum_chips(got_b, "sum_chips_b")
    other_a, other_b = _swap_sibling([sum_a, sum_b])
    big = dict(
        w_in=_adamw_rows(w_in, m_w_in, v_w_in, [sum_a, other_a], 0, "adamw_w_in"),
        w_out=_adamw_rows(w_out, m_w_out, v_w_out, [sum_a, other_a], A_OUT_AT, "adamw_w_out"),
        ffn_w_gate=[t_last(o) for o in _adamw_rows(gate_t, t_last(m_ffn_w_gate), t_last(v_ffn_w_gate),
                                                   [sum_b, other_b], 0, "adamw_gate")],
        ffn_w_up=[t_last(o) for o in _adamw_rows(up_t, t_last(m_ffn_w_up), t_last(v_ffn_w_up),
                                                 [sum_b, other_b], FF_SHARD, "adamw_up")],
        ffn_w_down=_adamw_rows(ffn_w_down, m_ffn_w_down, v_ffn_w_down, [sum_b, other_b], 2 * FF_SHARD, "adamw_down"))

    full_shapes = [(DEPTH, D_MODEL), (DEPTH, D_MODEL), (DEPTH, HEAD_DIM), (DEPTH, SC_WIDTH), (DEPTH, HEADS),
                   (DEPTH, HEADS), (D_MODEL,), (DEPTH, 4, QKV), (DEPTH, 3, SC_WIDTH)]
    small_keys = ("g1", "g2", "gn", "gs", "al", "dt")
    packed = _pack([stack(k) for k in small_keys] + [d_final[0], stack("cw"), stack("scw")], _packed_rows(full_shapes))
    sg = _unpack(_all_reduce_small(packed), full_shapes)
    sg[7] = lax.dynamic_slice_in_dim(sg[7], chip * (QKV // N_CHIPS), QKV // N_CHIPS, axis=2)
    sg[8] = lax.dynamic_slice_in_dim(sg[8], chip * (SC_WIDTH // N_CHIPS), SC_WIDTH // N_CHIPS, axis=2)
    small_names = ("norm1_g", "norm2_g", "dn_norm_g", "sc_norm_g", "dn_a_log", "dn_dt_bias", "final_norm_g",
                   "dn_conv_w", "sc_conv_w")
    sw = (norm1_g, norm2_g, dn_norm_g, sc_norm_g, dn_a_log, dn_dt_bias, final_norm_g, dn_conv_w, sc_conv_w)
    sm = (m_norm1_g, m_norm2_g, m_dn_norm_g, m_sc_norm_g, m_dn_a_log, m_dn_dt_bias, m_final_norm_g, m_dn_conv_w, m_sc_conv_w)
    sv = (v_norm1_g, v_norm2_g, v_dn_norm_g, v_sc_norm_g, v_dn_a_log, v_dn_dt_bias, v_final_norm_g, v_dn_conv_w, v_sc_conv_w)
    shard_shapes = [t.shape for t in sw]
    rows = _packed_rows(shard_shapes)
    outs = _adamw(_pack(sw, rows), _pack(sm, rows), _pack(sv, rows, fill=1.0), [_pack(sg, rows)], "adamw_small")
    small = {name: [] for name in small_names}
    for o in outs:
        for name, t in zip(small_names, _unpack(o, shard_shapes)):
            small[name].append(t)

    order = ("norm1_g", "w_in", "dn_conv_w", "dn_a_log", "dn_dt_bias", "dn_norm_g", "sc_conv_w", "sc_norm_g", "w_out",
             "norm2_g", "ffn_w_gate", "ffn_w_up", "ffn_w_down", "final_norm_g")
    result = {**big, **small}
    return (loss, dact[None], *[result[n][0] for n in order], *[result[n][1] for n in order],
            *[result[n][2] for n in order], *[result[n][3] for n in order])
```

```python
import jax
import jax.numpy as jnp
from jax import lax
from jax.experimental import pallas as pl
from jax.experimental.pallas import tpu as pltpu

F32 = jnp.float32
BF16 = jnp.bfloat16
MESH = pl.DeviceIdType.MESH

D_MODEL = 1024
DEPTH = 4
HEADS = 4
HEAD_DIM = 128
DN_WIDTH = HEADS * HEAD_DIM
SC_WIDTH = 512
SC_GROUPS = 4
D_FF = 2816
CHUNK = 64
QKV = 3 * DN_WIDTH
W_IN_COLS = 4 * DN_WIDTH + 2 * HEADS + 3 * SC_WIDTH
WA_COLS = QKV + DN_WIDTH + 3 * SC_WIDTH
LANES = 128
EPS = 1e-6
Q_SCALE = HEAD_DIM ** -0.5
N_CHIPS = 4
N_DEV = 8
IN_SHARD = W_IN_COLS // N_CHIPS
OUT_SHARD = D_MODEL // N_CHIPS
FF_SHARD = D_FF // N_CHIPS
A_OUT_AT = D_MODEL
A_ROWS = D_MODEL + OUT_SHARD
B_ROWS = 3 * FF_SHARD

ADAM_LR = 0.001
ADAM_B1 = 0.9
ADAM_B2 = 0.999
ADAM_EPS = 1e-08
ADAM_WD = 0.01
ADAM_STEP = 10

VMEM_LIMIT = 56 * 1024 * 1024

NN = (((1,), (0,)), ((), ()))
NT = (((1,), (1,)), ((), ()))
TN = (((0,), (0,)), ((), ()))


def _mm(a, b, dims=NN):
    return lax.dot_general(a.astype(BF16), b.astype(BF16), dims, preferred_element_type=F32)


def _mm32(a, b, dims=NN):
    return lax.dot_general(a, b, dims, preferred_element_type=F32, precision=lax.Precision.HIGHEST)


def _params(sem, vmem=VMEM_LIMIT):
    return pltpu.CompilerParams(dimension_semantics=sem, vmem_limit_bytes=vmem)


def _sigmoid(x):
    return 1.0 / (1.0 + jnp.exp(-x))


def _softplus(x):
    return jnp.maximum(x, 0.0) + jnp.log1p(jnp.exp(-jnp.abs(x)))


def _row_acc(acc_ref, val):
    acc_ref[0:1, :] += jnp.sum(val, axis=0, keepdims=True)


def _rms_bwd(dh, xh, r, gain):
    dxh = dh * gain
    return r * (dxh - xh * jnp.mean(dxh * xh, axis=-1, keepdims=True))


def _before_halo(tb):
    return lambda i: (jnp.maximum(i * (tb // 8) - 1, 0), 0)


def _after_halo(tb, n_rows):
    last = n_rows // 8 - 1
    return lambda i: (jnp.minimum((i + 1) * (tb // 8), last), 0)


def _taps(xc, w, n_taps, tb, first):
    out = w[0:1, :] * xc[first:first + tb, :]
    for j in range(1, n_taps):
        out = out + w[j:j + 1, :] * xc[first + j:first + j + tb, :]
    return out


def _in_proj(x, g1, wa, wbd):
    T = x.shape[0]
    tb = 256

    def body(x_ref, g_ref, wa_ref, wbd_ref, qkv_ref, z_ref, sc_ref, bd_ref, ht_ref):
        xv = x_ref[...]
        r = lax.rsqrt(jnp.mean(xv * xv, axis=-1, keepdims=True) + EPS)
        h = (xv * r * g_ref[...]).astype(BF16)
        p = jnp.dot(h, wa_ref[...], preferred_element_type=F32)
        qkv_ref[...] = p[:, :QKV]
        z_ref[...] = p[:, QKV:QKV + DN_WIDTH]
        sc_ref[...] = p[:, QKV + DN_WIDTH:]
        bd_ref[...] = jnp.dot(h, wbd_ref[...], preferred_element_type=F32)
        ht_ref[...] = h.T

    tok = lambda w: pl.BlockSpec((tb, w), lambda i: (i, 0))
    full = lambda a: pl.BlockSpec(a.shape, lambda i: (0, 0))
    return pl.pallas_call(
        body, name="in_proj", grid=(T // tb,),
        in_specs=[tok(D_MODEL), full(g1), full(wa), full(wbd)],
        out_specs=[tok(QKV), tok(DN_WIDTH), tok(3 * SC_WIDTH), tok(LANES),
                   pl.BlockSpec((D_MODEL, tb), lambda i: (0, i))],
        out_shape=[jax.ShapeDtypeStruct((T, QKV), F32), jax.ShapeDtypeStruct((T, DN_WIDTH), F32),
                   jax.ShapeDtypeStruct((T, 3 * SC_WIDTH), F32), jax.ShapeDtypeStruct((T, LANES), F32),
                   jax.ShapeDtypeStruct((D_MODEL, T), BF16)],
        compiler_params=_params(("parallel",)),
    )(x, g1, wa, wbd)


def _dn_act(pre, halo, cw, tb):
    xc = jnp.concatenate([halo, pre], axis=0)
    c = _taps(xc, cw, 4, tb, 5)
    sg = _sigmoid(c)
    return xc, c, sg, c * sg


def _gates(bd, al_row, dt_row):
    lane = lax.broadcasted_iota(jnp.int32, bd.shape, 1)
    beta = _sigmoid(bd)
    g = -jnp.exp(al_row) * _softplus(bd + dt_row)
    return jnp.where(lane < HEADS, beta, jnp.where(lane < 2 * HEADS, g, 0.0))


def _dn_prep(qkv, cw, bd, al_row, dt_row):
    T = qkv.shape[0]
    tb = 512

    def body(pre_ref, halo_ref, cw_ref, bd_ref, al_ref, dt_ref, q_ref, k_ref, v_ref, bg_ref):
        halo = jnp.where(pl.program_id(0) > 0, halo_ref[...], 0.0)
        _, _, _, a = _dn_act(pre_ref[...], halo, cw_ref[...], tb)
        for hh in range(HEADS):
            sl = slice(HEAD_DIM * hh, HEAD_DIM * (hh + 1))
            qs = a[:, sl]
            q_ref[:, sl] = qs * (lax.rsqrt(jnp.sum(qs * qs, axis=-1, keepdims=True) + EPS) * Q_SCALE)
            ks = a[:, DN_WIDTH + HEAD_DIM * hh:DN_WIDTH + HEAD_DIM * (hh + 1)]
            k_ref[:, sl] = ks * lax.rsqrt(jnp.sum(ks * ks, axis=-1, keepdims=True) + EPS)
        v_ref[...] = a[:, 2 * DN_WIDTH:]
        gates = _gates(bd_ref[...], al_ref[...], dt_ref[...])
        lane = lax.broadcasted_iota(jnp.int32, gates.shape, 1)
        bg_ref[...] = jnp.where(lane < HEADS, gates, _mm32(_chunk_cumsum_matrix(tb), gates))

    tok = lambda w: pl.BlockSpec((tb, w), lambda i: (i, 0))
    full = lambda a: pl.BlockSpec(a.shape, lambda i: (0, 0))
    return pl.pallas_call(
        body, name="dn_prep", grid=(T // tb,),
        in_specs=[tok(QKV), pl.BlockSpec((8, QKV), _before_halo(tb)), full(cw), tok(LANES), full(al_row), full(dt_row)],
        out_specs=[tok(DN_WIDTH), tok(DN_WIDTH), tok(DN_WIDTH), tok(LANES)],
        out_shape=[jax.ShapeDtypeStruct((T, DN_WIDTH), F32)] * 3 + [jax.ShapeDtypeStruct((T, LANES), F32)],
        compiler_params=_params(("parallel",)),
    )(qkv, qkv, cw, bd, al_row, dt_row)


def _chunk_masks():
    row = lax.broadcasted_iota(jnp.int32, (CHUNK, CHUNK), 0)
    col = lax.broadcasted_iota(jnp.int32, (CHUNK, CHUNK), 1)
    return row >= col, row > col


def _chunk_cumsum_matrix(n):
    row = lax.broadcasted_iota(jnp.int32, (n, n), 0)
    col = lax.broadcasted_iota(jnp.int32, (n, n), 1)
    return jnp.logical_and(row >= col, row // CHUNK == col // CHUNK).astype(F32)


def _chunk_units(q_ref, k_ref, v_ref, bg_ref, rows):
    bgc = bg_ref[rows, :]
    bg_t = bgc.T
    qv, kv, vv = q_ref[rows, :], k_ref[rows, :], v_ref[rows, :]
    units = []
    for h in range(HEADS):
        sl = slice(HEAD_DIM * h, HEAD_DIM * (h + 1))
        units.append((qv[:, sl], kv[:, sl], vv[:, sl], bgc[:, h:h + 1], bgc[:, HEADS + h:HEADS + h + 1],
                      bg_t[HEADS + h:HEADS + h + 1, :]))
    return units


def _units_local(units, masks):
    causal, strict = masks
    pre = []
    for q, k, v, beta, gc, gr in units:
        kb = k * beta
        eg = jnp.exp(gc)
        g_last = gc[CHUNK - 1:CHUNK, :]
        ek = jnp.exp(g_last - gc)
        pre.append(dict(q=q, k=k, v=v, beta=beta, decay=jnp.exp(jnp.where(causal, gc - gr, -1e30)), kb=kb, vb=v * beta,
                        eg=eg, kbg=kb * eg, ek=ek, gl=jnp.exp(g_last), q_dec=q * eg, k_dec=k * ek))
    both = [_mm(jnp.concatenate([p["kb"], p["q"]], axis=0), p["k"], NT) for p in pre]
    for p, b in zip(pre, both):
        p["low"] = jnp.where(strict, b[:CHUNK] * p["decay"], 0.0)
        p["qk"] = jnp.where(causal, b[CHUNK:] * p["decay"], 0.0)
    xs = [-p["low"] for p in pre]
    pw = [_mm(p["low"], p["low"]) for p in pre]
    for _ in range(4):
        both = [_mm(jnp.concatenate([pp, x], axis=0), pp) for pp, x in zip(pw, xs)]
        xs = [x + pp + b[CHUNK:] for x, pp, b in zip(xs, pw, both)]
        pw = [b[:CHUNK] for b in both]
    last = [_mm(x, pp) for x, pp in zip(xs, pw)]
    xs = [x + pp + b for x, pp, b in zip(xs, pw, last)]
    uw = [_mm(x, jnp.concatenate([p["vb"], p["kbg"]], axis=1)) for x, p in zip(xs, pre)]
    for p, x, b in zip(pre, xs, uw):
        p["xm"] = x
        p["u"] = p["vb"] + b[:, :HEAD_DIM]
        p["w"] = p["kbg"] + b[:, HEAD_DIM:]
    return pre


def _delta_fwd(q, k, v, bg):
    T = q.shape[0]
    tb = 512
    n_chunk = tb // CHUNK

    def body(q_ref, k_ref, v_ref, bg_ref, o_ref, st_ref, s_ref):
        @pl.when(pl.program_id(0) == 0)
        def _():
            s_ref[...] = jnp.zeros_like(s_ref)

        masks = _chunk_masks()

        def pair(pi, carry):
            rows = [pl.ds(pl.multiple_of((2 * pi + j) * CHUNK, CHUNK), CHUNK) for j in range(2)]
            loc = _units_local(_chunk_units(q_ref, k_ref, v_ref, bg_ref, rows[0])
                               + _chunk_units(q_ref, k_ref, v_ref, bg_ref, rows[1]), masks)
            states = [s_ref[h] for h in range(HEADS)]
            for j in range(2):
                lj = loc[HEADS * j:HEADS * (j + 1)]
                ws = [_mm(jnp.concatenate([p["w"], p["q_dec"]], axis=0), s) for p, s in zip(lj, states)]
                v_new = [p["u"] - b[:CHUNK] for p, b in zip(lj, ws)]
                intra = [_mm(p["qk"], vn) for p, vn in zip(lj, v_new)]
                upd = [_mm(p["k_dec"], vn, TN) for p, vn in zip(lj, v_new)]
                o_ref[rows[j], :] = jnp.concatenate([b[CHUNK:] + a for b, a in zip(ws, intra)], axis=1)
                for h in range(HEADS):
                    st_ref[2 * pi + j, h] = states[h]
                states = [p["gl"] * s + d for p, s, d in zip(lj, states, upd)]
            for h in range(HEADS):
                s_ref[h] = states[h]
            return carry

        lax.fori_loop(0, n_chunk // 2, pair, 0)

    tok = lambda w: pl.BlockSpec((tb, w), lambda i: (i, 0))
    return pl.pallas_call(
        body, name="delta_fwd", grid=(T // tb,),
        in_specs=[tok(DN_WIDTH), tok(DN_WIDTH), tok(DN_WIDTH), tok(LANES)],
        out_specs=[tok(DN_WIDTH), pl.BlockSpec((n_chunk, HEADS, HEAD_DIM, HEAD_DIM), lambda i: (i, 0, 0, 0))],
        out_shape=[jax.ShapeDtypeStruct((T, DN_WIDTH), F32),
                   jax.ShapeDtypeStruct((T // CHUNK, HEADS, HEAD_DIM, HEAD_DIM), F32)],
        scratch_shapes=[pltpu.VMEM((HEADS, HEAD_DIM, HEAD_DIM), F32)],
        compiler_params=_params(("arbitrary",)),
    )(q, k, v, bg)


def _dn_out(o, z, gn):
    outs, ohs, rs = [], [], []
    for hh in range(HEADS):
        oh = o[:, HEAD_DIM * hh:HEAD_DIM * (hh + 1)]
        r = lax.rsqrt(jnp.mean(oh * oh, axis=-1, keepdims=True) + EPS)
        ohs.append(oh * r)
        rs.append(r)
    sz = _sigmoid(z)
    oh = jnp.concatenate(ohs, axis=1)
    gn4 = jnp.concatenate([gn] * HEADS, axis=1)
    return oh * gn4 * (z * sz), oh, rs, sz, gn4


def _sc_fwd(sc_in, halo, cw, tb):
    xc = jnp.concatenate([halo, sc_in], axis=0)
    u = xc[:, SC_WIDTH:2 * SC_WIDTH] * xc[:, 2 * SC_WIDTH:]
    cv = _taps(u, cw, 3, tb, 6)
    gate_b = sc_in[:, :SC_WIDTH]
    y = gate_b * cv
    gw = SC_WIDTH // SC_GROUPS
    yhs, rs = [], []
    for gi in range(SC_GROUPS):
        yg = y[:, gw * gi:gw * (gi + 1)]
        r = lax.rsqrt(jnp.mean(yg * yg, axis=-1, keepdims=True) + EPS)
        yhs.append(yg * r)
        rs.append(r)
    return u, cv, gate_b, jnp.concatenate(yhs, axis=1), rs


def _shard_rows(land, first, rows):
    assert first % rows == 0 and land.shape[0] == N_CHIPS
    return pl.BlockSpec((N_CHIPS, rows, land.shape[2]), lambda i: (0, first // rows, 0))


def _whole(w_ref):
    n, rows, cols = w_ref.shape
    return w_ref[...].reshape(n * rows, cols)


def _mix_out(o, z, sc_in, x, land_a, gn, scw, gs):
    T = x.shape[0]
    tb = 256

    def body(o_ref, z_ref, sc_ref, halo_ref, x_ref, w_ref, gn_ref, scw_ref, gs_ref, x1_ref, mt_ref):
        o_n = _dn_out(o_ref[...], z_ref[...], gn_ref[...])[0]
        halo = jnp.where(pl.program_id(0) > 0, halo_ref[...], 0.0)
        yh = _sc_fwd(sc_ref[...], halo, scw_ref[...], tb)[3]
        mix = jnp.concatenate([o_n, yh * gs_ref[...]], axis=1).astype(BF16)
        x1_ref[...] = x_ref[...] + jnp.dot(mix, _whole(w_ref), preferred_element_type=F32)
        mt_ref[...] = mix.T

    tok = lambda w: pl.BlockSpec((tb, w), lambda i: (i, 0))
    full = lambda a: pl.BlockSpec(a.shape, lambda i: (0, 0))
    return pl.pallas_call(
        body, name="mix_out", grid=(T // tb,),
        in_specs=[tok(DN_WIDTH), tok(DN_WIDTH), tok(3 * SC_WIDTH), pl.BlockSpec((8, 3 * SC_WIDTH), _before_halo(tb)),
                  tok(D_MODEL), _shard_rows(land_a, A_OUT_AT, OUT_SHARD), full(gn), full(scw), full(gs)],
        out_specs=[tok(D_MODEL), pl.BlockSpec((D_MODEL, tb), lambda i: (0, i))],
        out_shape=[jax.ShapeDtypeStruct((T, D_MODEL), F32), jax.ShapeDtypeStruct((D_MODEL, T), BF16)],
        compiler_params=_params(("parallel",)),
    )(o, z, sc_in, sc_in, x, land_a, gn, scw, gs)


def _ffn(x1, g2, land_b):
    T = x1.shape[0]
    tb = 256

    def body(x_ref, g_ref, wgt_ref, wut_ref, wd_ref, x2_ref, a_ref, b_ref, h_ref):
        xv = x_ref[...]
        r = lax.rsqrt(jnp.mean(xv * xv, axis=-1, keepdims=True) + EPS)
        h = (xv * r * g_ref[...]).astype(BF16)
        a = lax.dot_general(h, _whole(wgt_ref), NT, preferred_element_type=F32)
        b = lax.dot_general(h, _whole(wut_ref), NT, preferred_element_type=F32)
        act = (a * _sigmoid(a) * b).astype(BF16)
        x2_ref[...] = xv + jnp.dot(act, _whole(wd_ref), preferred_element_type=F32)
        a_ref[...] = a.astype(BF16)
        b_ref[...] = b.astype(BF16)
        h_ref[...] = h

    tok = lambda w: pl.BlockSpec((tb, w), lambda i: (i, 0))
    return pl.pallas_call(
        body, name="ffn", grid=(T // tb,),
        in_specs=[tok(D_MODEL), pl.BlockSpec(g2.shape, lambda i: (0, 0)), _shard_rows(land_b, 0, FF_SHARD),
                  _shard_rows(land_b, FF_SHARD, FF_SHARD), _shard_rows(land_b, 2 * FF_SHARD, FF_SHARD)],
        out_specs=[tok(D_MODEL), tok(D_FF), tok(D_FF), tok(D_MODEL)],
        out_shape=[jax.ShapeDtypeStruct((T, D_MODEL), F32), jax.ShapeDtypeStruct((T, D_FF), BF16),
                   jax.ShapeDtypeStruct((T, D_FF), BF16), jax.ShapeDtypeStruct((T, D_MODEL), BF16)],
        compiler_params=_params(("parallel",)),
    )(x1, g2, land_b, land_b, land_b)


def _loss_head(x, gf, target):
    T = x.shape[0]
    tb = 512

    def body(x_ref, g_ref, t_ref, dx_ref, dxb_ref, loss_ref, dg_ref):
        @pl.when(pl.program_id(0) == 0)
        def _():
            loss_ref[...] = jnp.zeros_like(loss_ref)
            dg_ref[...] = jnp.zeros_like(dg_ref)

        xv = x_ref[...]
        r = lax.rsqrt(jnp.mean(xv * xv, axis=-1, keepdims=True) + EPS)
        xh = xv * r
        err = xh * g_ref[...] - t_ref[...]
        per_tok = jnp.mean(err * err, axis=-1, keepdims=True)
        loss_ref[...] += 0.5 * jnp.sum(per_tok, axis=0, keepdims=True)
        dy = err * (1.0 / D_MODEL)
        _row_acc(dg_ref, dy * xh)
        dx = _rms_bwd(dy, xh, r, g_ref[...])
        dx_ref[...] = dx
        dxb_ref[...] = dx.astype(BF16)

    tok = pl.BlockSpec((tb, D_MODEL), lambda i: (i, 0))
    return pl.pallas_call(
        body, name="loss_head", grid=(T // tb,),
        in_specs=[tok, pl.BlockSpec(gf.shape, lambda i: (0, 0)), tok],
        out_specs=[tok, tok, pl.BlockSpec((8, LANES), lambda i: (0, 0)), pl.BlockSpec((8, D_MODEL), lambda i: (0, 0))],
        out_shape=[jax.ShapeDtypeStruct((T, D_MODEL), F32), jax.ShapeDtypeStruct((T, D_MODEL), BF16),
                   jax.ShapeDtypeStruct((8, LANES), F32), jax.ShapeDtypeStruct((8, D_MODEL), F32)],
        compiler_params=_params(("arbitrary",)),
    )(x, gf, target)


def _ffn_bwd(dx2, x1, a, b, g2, land_b):
    T = x1.shape[0]
    tb = 256

    def body(dx2_ref, x_ref, a_ref, b_ref, g_ref, wgt_ref, wut_ref, wd_ref,
             dx1_ref, dx1b_ref, dat_ref, dbt_ref, at_ref, dg_ref):
        @pl.when(pl.program_id(0) == 0)
        def _():
            dg_ref[...] = jnp.zeros_like(dg_ref)

        dx2v = dx2_ref[...]
        av = a_ref[...].astype(F32)
        bv = b_ref[...].astype(F32)
        dact = _mm(dx2v, _whole(wd_ref), NT)
        sa = _sigmoid(av)
        silu = av * sa
        da = (dact * bv * (sa * (1.0 + av * (1.0 - sa)))).astype(BF16)
        db = (dact * silu).astype(BF16)
        dh = _mm(da, _whole(wgt_ref)) + _mm(db, _whole(wut_ref))
        xv = x_ref[...]
        r = lax.rsqrt(jnp.mean(xv * xv, axis=-1, keepdims=True) + EPS)
        xh = xv * r
        _row_acc(dg_ref, dh * xh)
        dx1 = dx2v + _rms_bwd(dh, xh, r, g_ref[...])
        dx1_ref[...] = dx1
        dx1b_ref[...] = dx1.astype(BF16)
        dat_ref[...] = da.T
        dbt_ref[...] = db.T
        at_ref[...] = (silu * bv).astype(BF16).T

    tok = lambda w: pl.BlockSpec((tb, w), lambda i: (i, 0))
    tr = pl.BlockSpec((D_FF, tb), lambda i: (0, i))
    return pl.pallas_call(
        body, name="ffn_bwd", grid=(T // tb,),
        in_specs=[tok(D_MODEL), tok(D_MODEL), tok(D_FF), tok(D_FF), pl.BlockSpec(g2.shape, lambda i: (0, 0)),
                  _shard_rows(land_b, 0, FF_SHARD), _shard_rows(land_b, FF_SHARD, FF_SHARD),
                  _shard_rows(land_b, 2 * FF_SHARD, FF_SHARD)],
        out_specs=[tok(D_MODEL), tok(D_MODEL), tr, tr, tr, pl.BlockSpec((8, D_MODEL), lambda i: (0, 0))],
        out_shape=[jax.ShapeDtypeStruct((T, D_MODEL), F32), jax.ShapeDtypeStruct((T, D_MODEL), BF16)]
        + [jax.ShapeDtypeStruct((D_FF, T), BF16)] * 3 + [jax.ShapeDtypeStruct((8, D_MODEL), F32)],
        compiler_params=_params(("arbitrary",)),
    )(dx2, x1, a, b, g2, land_b, land_b, land_b)


def _wgrad(at, b, bm, bn, name):
    M, T = at.shape
    N = b.shape[1]
    bk = min(T, 1024)

    def body(a_ref, b_ref, o_ref):
        @pl.when(pl.program_id(2) == 0)
        def _():
            o_ref[...] = jnp.zeros_like(o_ref)

        o_ref[...] += jnp.dot(a_ref[...], b_ref[...], preferred_element_type=F32)

    return pl.pallas_call(
        body, name=name, grid=(M // bm, N // bn, T // bk),
        in_specs=[pl.BlockSpec((bm, bk), lambda i, j, kk: (i, kk)), pl.BlockSpec((bk, bn), lambda i, j, kk: (kk, j))],
        out_specs=pl.BlockSpec((bm, bn), lambda i, j, kk: (i, j)),
        out_shape=jax.ShapeDtypeStruct((M, N), F32),
        compiler_params=_params(("parallel", "parallel", "arbitrary")),
    )(at, b)


def _mix_out_bwd(dx1, o, z, sc_in, land_a, gn, scw, gs):
    T = dx1.shape[0]
    tb = 256

    def body(dx_ref, o_ref, z_ref, sc_ref, halo_ref, w_ref, gn_ref, scw_ref, gs_ref,
             do_ref, dz_ref, dgb_ref, dcv_ref, dgn_ref, dgs_ref, dscw_ref):
        @pl.when(pl.program_id(0) == 0)
        def _():
            dgn_ref[...] = jnp.zeros_like(dgn_ref)
            dgs_ref[...] = jnp.zeros_like(dgs_ref)
            dscw_ref[...] = jnp.zeros_like(dscw_ref)

        dmix = _mm(dx_ref[...], _whole(w_ref), NT)
        don = dmix[:, :DN_WIDTH]
        dosc = dmix[:, DN_WIDTH:]
        zv = z_ref[...]
        _, oh, rs, sz, gn4 = _dn_out(o_ref[...], zv, gn_ref[...])
        silu_z = zv * sz
        dgn_full = don * oh * silu_z
        dgn_ref[0:1, :] += jnp.sum(sum(dgn_full[:, HEAD_DIM * hh:HEAD_DIM * (hh + 1)] for hh in range(HEADS)),
                                   axis=0, keepdims=True)
        dz_ref[...] = (don * oh * gn4 * (sz * (1.0 + zv * (1.0 - sz)))).astype(BF16)
        t = don * gn4 * silu_z
        for hh in range(HEADS):
            sl = slice(HEAD_DIM * hh, HEAD_DIM * (hh + 1))
            th, ohh = t[:, sl], oh[:, sl]
            do_ref[:, sl] = rs[hh] * (th - ohh * jnp.mean(th * ohh, axis=-1, keepdims=True))
        halo = jnp.where(pl.program_id(0) > 0, halo_ref[...], 0.0)
        u, cv, gate_b, yh, rys = _sc_fwd(sc_ref[...], halo, scw_ref[...], tb)
        _row_acc(dgs_ref, dosc * yh)
        ty = dosc * gs_ref[...]
        gw = SC_WIDTH // SC_GROUPS
        dys = []
        for gi in range(SC_GROUPS):
            sl = slice(gw * gi, gw * (gi + 1))
            tg, yg = ty[:, sl], yh[:, sl]
            dys.append(rys[gi] * (tg - yg * jnp.mean(tg * yg, axis=-1, keepdims=True)))
        dy = jnp.concatenate(dys, axis=1)
        dgb_ref[...] = dy * cv
        dcv = dy * gate_b
        dcv_ref[...] = dcv
        for j in range(3):
            dscw_ref[j:j + 1, :] += jnp.sum(dcv * u[6 + j:6 + j + tb, :], axis=0, keepdims=True)

    tok = lambda w: pl.BlockSpec((tb, w), lambda i: (i, 0))
    full = lambda t: pl.BlockSpec(t.shape, lambda i: (0, 0))
    acc = lambda w: pl.BlockSpec((8, w), lambda i: (0, 0))
    return pl.pallas_call(
        body, name="mix_out_bwd", grid=(T // tb,),
        in_specs=[tok(D_MODEL), tok(DN_WIDTH), tok(DN_WIDTH), tok(3 * SC_WIDTH),
                  pl.BlockSpec((8, 3 * SC_WIDTH), _before_halo(tb)), _shard_rows(land_a, A_OUT_AT, OUT_SHARD),
                  full(gn), full(scw), full(gs)],
        out_specs=[tok(DN_WIDTH), tok(DN_WIDTH), tok(SC_WIDTH), tok(SC_WIDTH), acc(HEAD_DIM), acc(SC_WIDTH), acc(SC_WIDTH)],
        out_shape=[jax.ShapeDtypeStruct((T, DN_WIDTH), F32), jax.ShapeDtypeStruct((T, DN_WIDTH), BF16),
                   jax.ShapeDtypeStruct((T, SC_WIDTH), F32), jax.ShapeDtypeStruct((T, SC_WIDTH), F32),
                   jax.ShapeDtypeStruct((8, HEAD_DIM), F32), jax.ShapeDtypeStruct((8, SC_WIDTH), F32),
                   jax.ShapeDtypeStruct((8, SC_WIDTH), F32)],
        compiler_params=_params(("arbitrary",)),
    )(dx1, o, z, sc_in, sc_in, land_a, gn, scw, gs)


def _sc_conv_bwd(dcv, dgb, sc_in, scw):
    T = dcv.shape[0]
    tb = 512

    def body(dcv_ref, halo_ref, dgb_ref, sc_ref, w_ref, out_ref):
        last = pl.program_id(0) == pl.num_programs(0) - 1
        halo = jnp.where(last, 0.0, halo_ref[...])
        xc = jnp.concatenate([dcv_ref[...], halo], axis=0)
        w = w_ref[...]
        du = w[2:3, :] * xc[0:tb, :] + w[1:2, :] * xc[1:tb + 1, :] + w[0:1, :] * xc[2:tb + 2, :]
        sc = sc_ref[...]
        out_ref[:, :SC_WIDTH] = dgb_ref[...].astype(BF16)
        out_ref[:, SC_WIDTH:2 * SC_WIDTH] = (du * sc[:, 2 * SC_WIDTH:]).astype(BF16)
        out_ref[:, 2 * SC_WIDTH:] = (du * sc[:, SC_WIDTH:2 * SC_WIDTH]).astype(BF16)

    tok = lambda w: pl.BlockSpec((tb, w), lambda i: (i, 0))
    return pl.pallas_call(
        body, name="sc_conv_bwd", grid=(T // tb,),
        in_specs=[tok(SC_WIDTH), pl.BlockSpec((8, SC_WIDTH), _after_halo(tb, T)), tok(SC_WIDTH), tok(3 * SC_WIDTH),
                  pl.BlockSpec(scw.shape, lambda i: (0, 0))],
        out_specs=tok(3 * SC_WIDTH),
        out_shape=jax.ShapeDtypeStruct((T, 3 * SC_WIDTH), BF16),
        compiler_params=_params(("parallel",)),
    )(dcv, dcv, dgb, sc_in, scw)


def _delta_bwd(q, k, v, bg, states, do):
    T = q.shape[0]
    tb = 512
    n_chunk = tb // CHUNK
    nb = T // tb

    def body(q_ref, k_ref, v_ref, bg_ref, st_ref, do_ref, dq_ref, dk_ref, dv_ref, dbg_ref, ds_ref):
        @pl.when(pl.program_id(0) == 0)
        def _():
            ds_ref[...] = jnp.zeros_like(ds_ref)

        masks = _chunk_masks()
        causal, strict = masks
        lane = lax.broadcasted_iota(jnp.int32, (CHUNK, LANES), 1)
        last_row = lax.broadcasted_iota(jnp.int32, (CHUNK, 1), 0) == CHUNK - 1
        cat = jnp.concatenate
        heads = range(HEADS)

        def open_chunk(ci, loc):
            rows = pl.ds(pl.multiple_of(ci * CHUNK, CHUNK), CHUNK)
            dov = do_ref[rows, :]
            return dict(rows=rows, loc=loc, do=[dov[:, HEAD_DIM * h:HEAD_DIM * (h + 1)] for h in heads],
                        state=[st_ref[ci, h] for h in heads])

        def a_free(c):
            loc, do, state = c["loc"], c["do"], c["state"]
            w_s = [_mm(p["w"], s) for p, s in zip(loc, state)]
            c["dq_dec"] = [_mm(d, s, NT) for d, s in zip(do, state)]
            c["qk_do"] = [_mm(p["qk"], d, TN) for p, d in zip(loc, do)]
            c["qd_do"] = [_mm(p["q_dec"], d, TN) for p, d in zip(loc, do)]
            c["v_new"] = [p["u"] - t for p, t in zip(loc, w_s)]
            c["dqk"] = [jnp.where(causal, _mm(d, vn, NT), 0.0) for d, vn in zip(do, c["v_new"])]

        def a_state(c, ds_next):
            c["ds_next"] = ds_next
            kd_ds = [_mm(p["k_dec"], d) for p, d in zip(c["loc"], ds_next)]
            c["dk_dec"] = [_mm(vn, d, NT) for vn, d in zip(c["v_new"], ds_next)]
            c["dv_new"] = [a + b for a, b in zip(c["qk_do"], kd_ds)]

        def b_state(c):
            loc = c["loc"]
            w_dv = [_mm(p["w"], dvn, TN) for p, dvn in zip(loc, c["dv_new"])]
            c["dw"] = [-_mm(dvn, s, NT) for dvn, s in zip(c["dv_new"], c["state"])]
            return [loc[h]["gl"] * c["ds_next"][h] + c["qd_do"][h] - w_dv[h] for h in heads]

        def c_solve(c):
            loc, dv_new, dw = c["loc"], c["dv_new"], c["dw"]
            c["dtm"] = [_mm(cat([dvn, d], axis=1), cat([p["vb"], p["kbg"]], axis=1), NT) for dvn, d, p in zip(dv_new, dw, loc)]
            x_t = [_mm(p["xm"], cat([dvn, d], axis=1), TN) for p, dvn, d in zip(loc, dv_new, dw)]
            c["dvb"] = [dvn + t[:, :HEAD_DIM] for dvn, t in zip(dv_new, x_t)]
            c["dkbg"] = [d + t[:, HEAD_DIM:] for d, t in zip(dw, x_t)]

        def d_solve(c):
            c["y"] = [t + _mm(p["xm"], t, TN) for p, t in zip(c["loc"], c["dtm"])]

        def e_solve(c):
            c["dlow"] = [jnp.where(strict, -(t + _mm(t, p["xm"], NT)), 0.0) for p, t in zip(c["loc"], c["y"])]

        def f_close(c):
            loc, rows = c["loc"], c["rows"]
            dmm = [d * p["decay"] for d, p in zip(c["dlow"], loc)]
            dnn = [d * p["decay"] for d, p in zip(c["dqk"], loc)]
            by_k = [_mm(cat([a, b], axis=0), p["k"]) for a, b, p in zip(dmm, dnn, loc)]
            dk_mm = [_mm(cat([a, b], axis=0), cat([p["kb"], p["q"]], axis=0), TN) for a, b, p in zip(dmm, dnn, loc)]
            dq_out, dk_out, dv_out = [], [], []
            dbeta_all = jnp.zeros((CHUNK, LANES), F32)
            dgc_all = jnp.zeros((CHUNK, LANES), F32)
            for h in heads:
                p = loc[h]
                dkb = by_k[h][:CHUNK] + c["dkbg"][h] * p["eg"]
                dq_out.append(by_k[h][CHUNK:] + c["dq_dec"][h] * p["eg"])
                dk_out.append(dk_mm[h] + c["dk_dec"][h] * p["ek"] + dkb * p["beta"])
                dv_out.append(c["dvb"][h] * p["beta"])
                dbeta = jnp.sum(dkb * p["k"] + c["dvb"][h] * p["v"], axis=1, keepdims=True)
                e = c["dlow"][h] * p["low"] + c["dqk"][h] * p["qk"]
                kd = jnp.sum(c["dk_dec"][h] * p["k_dec"], axis=1, keepdims=True)
                dgc = (jnp.sum(e, axis=1, keepdims=True) - jnp.sum(e.T, axis=1, keepdims=True)
                       + jnp.sum(c["dq_dec"][h] * p["q_dec"], axis=1, keepdims=True) - kd
                       + jnp.sum(c["dkbg"][h] * p["kbg"], axis=1, keepdims=True))
                dgl = jnp.sum(jnp.sum(c["ds_next"][h] * c["state"][h], axis=1, keepdims=True), axis=0, keepdims=True)
                d_last = jnp.sum(kd, axis=0, keepdims=True) + dgl * p["gl"]
                dgc = dgc + jnp.where(last_row, d_last, 0.0)
                dbeta_all = jnp.where(lane == h, dbeta, dbeta_all)
                dgc_all = jnp.where(lane == h + HEADS, dgc, dgc_all)
            dq_ref[rows, :] = cat(dq_out, axis=1)
            dk_ref[rows, :] = cat(dk_out, axis=1)
            dv_ref[rows, :] = cat(dv_out, axis=1)
            dbg_ref[rows, :] = dbeta_all + dgc_all

        def pair(pj, carry):
            hi = n_chunk - 1 - 2 * pj
            lo = hi - 1
            rows = [pl.ds(pl.multiple_of(ci * CHUNK, CHUNK), CHUNK) for ci in (hi, lo)]
            loc = _units_local(_chunk_units(q_ref, k_ref, v_ref, bg_ref, rows[0])
                               + _chunk_units(q_ref, k_ref, v_ref, bg_ref, rows[1]), masks)
            c_hi, c_lo = open_chunk(hi, loc[:HEADS]), open_chunk(lo, loc[HEADS:])
            a_free(c_hi)
            a_free(c_lo)
            a_state(c_hi, [ds_ref[h] for h in heads])
            ds_mid = b_state(c_hi)
            a_state(c_lo, ds_mid)
            c_solve(c_hi)
            ds_out = b_state(c_lo)
            for h in heads:
                ds_ref[h] = ds_out[h]
            d_solve(c_hi)
            c_solve(c_lo)
            e_solve(c_hi)
            d_solve(c_lo)
            f_close(c_hi)
            e_solve(c_lo)
            f_close(c_lo)
            return carry

        lax.fori_loop(0, n_chunk // 2, pair, 0)

    tok = lambda w: pl.BlockSpec((tb, w), lambda i: (nb - 1 - i, 0))
    return pl.pallas_call(
        body, name="delta_bwd", grid=(nb,),
        in_specs=[tok(DN_WIDTH), tok(DN_WIDTH), tok(DN_WIDTH), tok(LANES),
                  pl.BlockSpec((n_chunk, HEADS, HEAD_DIM, HEAD_DIM), lambda i: (nb - 1 - i, 0, 0, 0)), tok(DN_WIDTH)],
        out_specs=[tok(DN_WIDTH), tok(DN_WIDTH), tok(DN_WIDTH), tok(LANES)],
        out_shape=[jax.ShapeDtypeStruct((T, DN_WIDTH), F32)] * 3 + [jax.ShapeDtypeStruct((T, LANES), F32)],
        scratch_shapes=[pltpu.VMEM((HEADS, HEAD_DIM, HEAD_DIM), F32)],
        compiler_params=_params(("arbitrary",)),
    )(q, k, v, bg, states, do)


def _dn_prep_bwd(dq, dk, dv, dbg, qkv, cw, bd, al_row, dt_row):
    T = qkv.shape[0]
    tb = 256

    def body(dq_ref, dk_ref, dv_ref, dbg_ref, pre_ref, halo_ref, cw_ref, bd_ref, al_ref, dt_ref,
             dc_ref, dbd_ref, dcw_ref, dal_ref, ddt_ref):
        @pl.when(pl.program_id(0) == 0)
        def _():
            dcw_ref[...] = jnp.zeros_like(dcw_ref)
            dal_ref[...] = jnp.zeros_like(dal_ref)
            ddt_ref[...] = jnp.zeros_like(ddt_ref)

        halo = jnp.where(pl.program_id(0) > 0, halo_ref[...], 0.0)
        xc, c, sg, a = _dn_act(pre_ref[...], halo, cw_ref[...], tb)
        dsilu = sg * (1.0 + c * (1.0 - sg))
        for hh in range(HEADS):
            sl = slice(HEAD_DIM * hh, HEAD_DIM * (hh + 1))
            for base, g_ref, scale in ((0, dq_ref, Q_SCALE), (DN_WIDTH, dk_ref, 1.0)):
                sa = slice(base + HEAD_DIM * hh, base + HEAD_DIM * (hh + 1))
                raw = a[:, sa]
                r = lax.rsqrt(jnp.sum(raw * raw, axis=-1, keepdims=True) + EPS)
                nrm = raw * r
                gn_ = g_ref[:, sl] * scale
                dc_ref[:, sa] = r * (gn_ - nrm * jnp.sum(gn_ * nrm, axis=-1, keepdims=True)) * dsilu[:, sa]
        dc_ref[:, 2 * DN_WIDTH:] = dv_ref[...] * dsilu[:, 2 * DN_WIDTH:]
        dc = dc_ref[...]
        for j in range(4):
            dcw_ref[j:j + 1, :] += jnp.sum(dc * xc[5 + j:5 + j + tb, :], axis=0, keepdims=True)
        bdv = bd_ref[...]
        lane = lax.broadcasted_iota(jnp.int32, bdv.shape, 1)
        is_b = lane < HEADS
        dbg_in = dbg_ref[...]
        dbgv = jnp.where(is_b, dbg_in, _mm32(_chunk_cumsum_matrix(tb), dbg_in, TN))
        is_g = jnp.logical_and(lane >= HEADS, lane < 2 * HEADS)
        beta = _sigmoid(bdv)
        neg_a = -jnp.exp(al_ref[...])
        pre_sp = bdv + dt_ref[...]
        g = neg_a * _softplus(pre_sp)
        da_in = dbgv * neg_a * _sigmoid(pre_sp)
        dbd_ref[...] = jnp.where(is_b, dbgv * beta * (1.0 - beta), jnp.where(is_g, da_in, 0.0)).astype(BF16)
        _row_acc(dal_ref, jnp.where(is_g, dbgv * g, 0.0))
        _row_acc(ddt_ref, jnp.where(is_g, da_in, 0.0))

    tok = lambda w: pl.BlockSpec((tb, w), lambda i: (i, 0))
    full = lambda t: pl.BlockSpec(t.shape, lambda i: (0, 0))
    acc = lambda w: pl.BlockSpec((8, w), lambda i: (0, 0))
    return pl.pallas_call(
        body, name="dn_prep_bwd", grid=(T // tb,),
        in_specs=[tok(DN_WIDTH), tok(DN_WIDTH), tok(DN_WIDTH), tok(LANES),
                  tok(QKV), pl.BlockSpec((8, QKV), _before_halo(tb)), full(cw), tok(LANES), full(al_row), full(dt_row)],
        out_specs=[tok(QKV), tok(LANES), acc(QKV), acc(LANES), acc(LANES)],
        out_shape=[jax.ShapeDtypeStruct((T, QKV), F32), jax.ShapeDtypeStruct((T, LANES), BF16),
                   jax.ShapeDtypeStruct((8, QKV), F32), jax.ShapeDtypeStruct((8, LANES), F32),
                   jax.ShapeDtypeStruct((8, LANES), F32)],
        compiler_params=_params(("arbitrary",)),
    )(dq, dk, dv, dbg, qkv, qkv, cw, bd, al_row, dt_row)


def _dn_conv_bwd(dc, cw):
    T = dc.shape[0]
    tb = 512

    def body(dc_ref, halo_ref, w_ref, out_ref):
        last = pl.program_id(0) == pl.num_programs(0) - 1
        halo = jnp.where(last, 0.0, halo_ref[...])
        xc = jnp.concatenate([dc_ref[...], halo], axis=0)
        w = w_ref[...]
        acc = w[3:4, :] * xc[0:tb, :]
        for j in range(3):
            acc = acc + w[j:j + 1, :] * xc[3 - j:3 - j + tb, :]
        out_ref[...] = acc.astype(BF16)

    tok = pl.BlockSpec((tb, QKV), lambda i: (i, 0))
    return pl.pallas_call(
        body, name="dn_conv_bwd", grid=(T // tb,),
        in_specs=[tok, pl.BlockSpec((8, QKV), _after_halo(tb, T)), pl.BlockSpec(cw.shape, lambda i: (0, 0))],
        out_specs=tok,
        out_shape=jax.ShapeDtypeStruct((T, QKV), BF16),
        compiler_params=_params(("parallel",)),
    )(dc, dc, cw)


def _in_proj_bwd(dqkv, dz, dsc, dbd, dx1, x, g1, wa, wbd):
    T = x.shape[0]
    tb = 256

    def body(dqkv_ref, dz_ref, dsc_ref, dbd_ref, dx1_ref, x_ref, g_ref, wa_ref, wbd_ref, dx_ref, dxb_ref, dg_ref):
        @pl.when(pl.program_id(0) == 0)
        def _():
            dg_ref[...] = jnp.zeros_like(dg_ref)

        dh = (_mm(dqkv_ref[...], wa_ref[:, :QKV], NT) + _mm(dz_ref[...], wa_ref[:, QKV:QKV + DN_WIDTH], NT)
              + _mm(dsc_ref[...], wa_ref[:, QKV + DN_WIDTH:], NT) + _mm(dbd_ref[...], wbd_ref[...], NT))
        xv = x_ref[...]
        r = lax.rsqrt(jnp.mean(xv * xv, axis=-1, keepdims=True) + EPS)
        xh = xv * r
        _row_acc(dg_ref, dh * xh)
        dx = dx1_ref[...] + _rms_bwd(dh, xh, r, g_ref[...])
        dx_ref[...] = dx
        dxb_ref[...] = dx.astype(BF16)

    tok = lambda w: pl.BlockSpec((tb, w), lambda i: (i, 0))
    full = lambda t: pl.BlockSpec(t.shape, lambda i: (0, 0))
    return pl.pallas_call(
        body, name="in_proj_bwd", grid=(T // tb,),
        in_specs=[tok(QKV), tok(DN_WIDTH), tok(3 * SC_WIDTH), tok(LANES), tok(D_MODEL), tok(D_MODEL),
                  full(g1), full(wa), full(wbd)],
        out_specs=[tok(D_MODEL), tok(D_MODEL), pl.BlockSpec((8, D_MODEL), lambda i: (0, 0))],
        out_shape=[jax.ShapeDtypeStruct((T, D_MODEL), F32), jax.ShapeDtypeStruct((T, D_MODEL), BF16),
                   jax.ShapeDtypeStruct((8, D_MODEL), F32)],
        compiler_params=_params(("arbitrary",)),
    )(dqkv, dz, dsc, dbd, dx1, x, g1, wa, wbd)


def _pad_rows(a, rows=8):
    return jnp.pad(a, ((0, rows - a.shape[0]), (0, 0)))


def _gate_rows(a_log, dt_bias):
    put = lambda t: jnp.pad(t.reshape(1, HEADS), ((0, 0), (HEADS, LANES - 2 * HEADS)))
    return put(a_log), put(dt_bias)


def _split_w_in(w_in):
    o = QKV + DN_WIDTH
    wa = jnp.concatenate([w_in[:, :o], w_in[:, o + 2 * HEADS:]], axis=1)
    wbd = jnp.pad(w_in[:, o:o + 2 * HEADS], ((0, 0), (0, LANES - 2 * HEADS)))
    return wa, wbd


def _mixer_fwd(x, p):
    qkv, z, sc_in, bd, ht = _in_proj(x, p["g1"], p["wa"], p["wbd"])
    q, k, v, bg = _dn_prep(qkv, p["cw"], bd, p["al"], p["dt"])
    o, states = _delta_fwd(q, k, v, bg)
    x1, mt = _mix_out(o, z, sc_in, x, p["land_a"], p["gn"], p["scw"], p["gs"])
    return x1, dict(x=x, qkv=qkv, z=z, sc_in=sc_in, bd=bd, ht=ht, q=q, k=k, v=v, bg=bg, o=o, states=states, mt=mt)


def _ffn_fwd(x1, p, land_b):
    x2, a, b, h2 = _ffn(x1, p["g2"], land_b)
    return x2, dict(x1=x1, a=a, b=b, h2=h2)


def _ffn_back(dx2, dx2_bf16, s, p, land_b):
    dx1, dx1_bf16, da_t, db_t, act_t, dg2 = _ffn_bwd(dx2, s["x1"], s["a"], s["b"], p["g2"], land_b)
    g = dict(wd=_wgrad(act_t, dx2_bf16, FF_SHARD, 1024, "wgrad_down"), wgt=_wgrad(da_t, s["h2"], FF_SHARD, 1024, "wgrad_gate"),
             wut=_wgrad(db_t, s["h2"], FF_SHARD, 1024, "wgrad_up"), g2=dg2[0])
    return dx1, dx1_bf16, g


def _mixer_bwd(dx1, dx1_bf16, s, p):
    do, dz, dgb, dcv, dgn, dgs, dscw = _mix_out_bwd(dx1, s["o"], s["z"], s["sc_in"], p["land_a"], p["gn"], p["scw"], p["gs"])
    g = dict(w_out=_wgrad(s["mt"], dx1_bf16, 512, 1024, "wgrad_out"))
    dsc = _sc_conv_bwd(dcv, dgb, s["sc_in"], p["scw"])
    dq, dk, dv, dbg = _delta_bwd(s["q"], s["k"], s["v"], s["bg"], s["states"], do)
    dc, dbd, dcw, dal, ddt = _dn_prep_bwd(dq, dk, dv, dbg, s["qkv"], p["cw"], s["bd"], p["al"], p["dt"])
    dqkv = _dn_conv_bwd(dc, p["cw"])
    dx, dx_bf16, dg1 = _in_proj_bwd(dqkv, dz, dsc, dbd, dx1, s["x"], p["g1"], p["wa"], p["wbd"])
    g["w_in"] = jnp.concatenate([
        _wgrad(s["ht"], dqkv, 512, 768, "wgrad_qkv"), _wgrad(s["ht"], dz, 512, 512, "wgrad_z"),
        _wgrad(s["ht"], dbd, 512, LANES, "wgrad_bd")[:, :2 * HEADS], _wgrad(s["ht"], dsc, 512, 768, "wgrad_sc")], axis=1)
    g.update(g1=dg1[0], gn=dgn[0], gs=dgs[0], scw=dscw[:3], cw=dcw[:4],
             al=dal[0, HEADS:2 * HEADS], dt=ddt[0, HEADS:2 * HEADS])
    return dx, dx_bf16, g


def _place():
    return lax.axis_index("x"), lax.axis_index("y"), lax.axis_index("c")


def _other_chips(x, y):
    return [(1 - x, y), (x, 1 - y), (1 - x, 1 - y)]


_HBM = pl.BlockSpec(memory_space=pltpu.HBM)


def _chip_exchange(arrs, name, gather):
    n = len(arrs)

    def body(*refs):
        ins, outs = refs[:n], refs[n:2 * n]
        send_sems, recv_sems, local_sems = refs[2 * n:]
        x, y, c = _place()
        me = 2 * x + y
        others = _other_chips(x, y)

        def remote(k, j, landing):
            px, py = others[j]
            src = ins[k] if gather else ins[k].at[2 * px + py]
            return pltpu.make_async_remote_copy(src_ref=src, dst_ref=outs[k].at[landing], send_sem=send_sems.at[k, j],
                                                recv_sem=recv_sems.at[k, j], device_id=(px, py, c), device_id_type=MESH)

        local = [pltpu.make_async_copy(ins[k] if gather else ins[k].at[me], outs[k].at[me], local_sems.at[k])
                 for k in range(n)]
        sends = [remote(k, j, me) for k in range(n) for j in range(3)]
        for cp in local + sends:
            cp.start()
        for k in range(n):
            for j, (px, py) in enumerate(others):
                remote(k, j, 2 * px + py).wait_recv()
        for cp in sends:
            cp.wait_send()
        for cp in local:
            cp.wait()

    shapes = [jax.ShapeDtypeStruct(((N_CHIPS,) + a.shape) if gather else a.shape, a.dtype) for a in arrs]
    return pl.pallas_call(
        body, name=name, in_specs=[_HBM] * n, out_specs=[_HBM] * n, out_shape=shapes,
        scratch_shapes=[pltpu.SemaphoreType.DMA((n, 3)), pltpu.SemaphoreType.DMA((n, 3)), pltpu.SemaphoreType.DMA((n,))],
    )(*arrs)


_SEM = pl.BlockSpec(memory_space=pltpu.SEMAPHORE)
_ANY = pl.BlockSpec(memory_space=pl.ANY)
_EFFECT = pltpu.SideEffectType.DATAFLOW_SIDE_EFFECTING


def _split_copies(src_ref, land_ref, send_sems, recv_sems, gather, sending):
    x, y, c = _place()
    me = 2 * x + y
    copies = []
    for j, (px, py) in enumerate(_other_chips(x, y)):
        peer = 2 * px + py
        copies.append(pltpu.make_async_remote_copy(
            src_ref=src_ref if gather else src_ref.at[peer], dst_ref=land_ref.at[me if sending else peer],
            send_sem=send_sems.at[j], recv_sem=recv_sems.at[j], device_id=(px, py, c), device_id_type=MESH))
    return copies


def _own_slot(share):
    chip = 2 * lax.axis_index("x") + lax.axis_index("y")
    return lax.dynamic_update_slice(lax.empty((N_CHIPS,) + share.shape, share.dtype), share[None], (chip, 0, 0))


def _exchange_start(src, land, after, name, gather):
    def body(src_ref, land_ref, after_ref, send_sems, recv_sems, src_thru, land_thru, token):
        for cp in _split_copies(src_ref, land_ref, send_sems, recv_sems, gather, sending=True):
            cp.start()
        token[...] = jnp.zeros_like(token)

    hbm = lambda t: pltpu.with_memory_space_constraint(t, pltpu.HBM)
    return pl.pallas_call(
        body, name=name,
        out_shape=(pltpu.SemaphoreType.DMA((3,)), pltpu.SemaphoreType.DMA((3,)), pltpu.HBM(src.shape, src.dtype),
                   pltpu.HBM(land.shape, land.dtype), jax.ShapeDtypeStruct((8, LANES), F32)),
        in_specs=(_HBM, _HBM, _ANY), out_specs=(_SEM, _SEM, _HBM, _HBM, pl.BlockSpec(memory_space=pltpu.VMEM)),
        input_output_aliases={0: 2, 1: 3},
        compiler_params=pltpu.CompilerParams(has_side_effects=_EFFECT),
    )(hbm(src), hbm(land), after)


def _exchange_wait(started, after, name, gather):
    send_sems, recv_sems, src_thru, land_thru, _ = started

    def body(src_ref, land_ref, send_sems, recv_sems, after_ref, src_dead, got_ref):
        for cp in _split_copies(src_ref, land_ref, send_sems, recv_sems, gather, sending=False):
            cp.wait_send()
            cp.wait_recv()

    return pl.pallas_call(
        body, name=name,
        out_shape=(pltpu.HBM(src_thru.shape, src_thru.dtype), pltpu.HBM(land_thru.shape, land_thru.dtype)),
        in_specs=(_HBM, _HBM, _SEM, _SEM, _ANY), out_specs=(_HBM, _HBM), input_output_aliases={0: 0, 1: 1},
        compiler_params=pltpu.CompilerParams(has_side_effects=_EFFECT),
    )(src_thru, land_thru, send_sems, recv_sems, after)[1]


def _swap_sibling(arrs):
    n = len(arrs)

    def body(*refs):
        ins, outs = refs[:n], refs[n:2 * n]
        send_sems, recv_sems = refs[2 * n:]
        x, y, c = _place()
        copies = [pltpu.make_async_remote_copy(src_ref=ins[k], dst_ref=outs[k], send_sem=send_sems.at[k],
                                               recv_sem=recv_sems.at[k], device_id=(x, y, 1 - c), device_id_type=MESH)
                  for k in range(n)]
        for cp in copies:
            cp.start()
        for cp in copies:
            cp.wait()

    return pl.pallas_call(
        body, name="swap_sibling", in_specs=[_HBM] * n, out_specs=[_HBM] * n,
        out_shape=[jax.ShapeDtypeStruct(a.shape, a.dtype) for a in arrs],
        scratch_shapes=[pltpu.SemaphoreType.DMA((n,)), pltpu.SemaphoreType.DMA((n,))],
    )(*arrs)


def _all_reduce_small(v):
    rows = v.shape[0]
    flips = [(a, b, cc) for a in (0, 1) for b in (0, 1) for cc in (0, 1)][1:]

    def body(v_ref, out_ref, buf_ref, send_sems, recv_sems):
        x, y, c = _place()
        me = 4 * x + 2 * y + c
        peers = [((1 - x) if a else x, (1 - y) if b else y, (1 - c) if cc else c) for a, b, cc in flips]

        def copy(j, landing):
            return pltpu.make_async_remote_copy(src_ref=v_ref, dst_ref=buf_ref.at[landing], send_sem=send_sems.at[j],
                                                recv_sem=recv_sems.at[j], device_id=peers[j], device_id_type=MESH)

        sends = [copy(j, me) for j in range(N_DEV - 1)]
        for cp in sends:
            cp.start()
        buf_ref[me] = v_ref[...]
        for j, (px, py, pc) in enumerate(peers):
            copy(j, 4 * px + 2 * py + pc).wait_recv()
        for cp in sends:
            cp.wait_send()
        acc = buf_ref[0]
        for d in range(1, N_DEV):
            acc = acc + buf_ref[d]
        out_ref[...] = acc

    vmem = pl.BlockSpec(memory_space=pltpu.VMEM)
    return pl.pallas_call(
        body, name="all_reduce_small", in_specs=[vmem], out_specs=vmem,
        out_shape=jax.ShapeDtypeStruct(v.shape, F32),
        scratch_shapes=[pltpu.VMEM((N_DEV, rows, LANES), F32), pltpu.SemaphoreType.DMA((N_DEV - 1,)),
                        pltpu.SemaphoreType.DMA((N_DEV - 1,))],
    )(v)


def _row_block(*sizes):
    return next(t for t in (256, 192, 128, 64) if all(s % t == 0 for s in sizes))


def _sum_chips(parts, name):
    _, rows, cols = parts[0].shape
    n = len(parts)
    tr = _row_block(rows)

    def body(*refs):
        o_ref = refs[n]
        for l in range(n):
            @pl.when(pl.program_id(0) == l)
            def _(p_ref=refs[l]):
                acc = p_ref[0].astype(F32)
                for s in range(1, N_CHIPS):
                    acc = acc + p_ref[s].astype(F32)
                o_ref[0] = acc

    return pl.pallas_call(
        body, name=name, grid=(n, rows // tr),
        in_specs=[pl.BlockSpec((N_CHIPS, tr, cols), lambda l, i, k=k: (0, jnp.where(l == k, i, 0), 0)) for k in range(n)],
        out_specs=pl.BlockSpec((1, tr, cols), lambda l, i: (l, i, 0)),
        out_shape=jax.ShapeDtypeStruct((n, rows, cols), F32),
        compiler_params=_params(("arbitrary", "arbitrary")),
    )(*parts)


def _adam_update(w, m, v, g):
    c1 = 1.0 - ADAM_B1 ** ADAM_STEP
    c2 = 1.0 - ADAM_B2 ** ADAM_STEP
    m_new = ADAM_B1 * m + (1.0 - ADAM_B1) * g
    v_new = ADAM_B2 * v + (1.0 - ADAM_B2) * (g * g)
    return -ADAM_LR * ((m_new / c1) / (jnp.sqrt(v_new / c2) + ADAM_EPS) + ADAM_WD * w), m_new, v_new


def _adamw_rows(w, m, v, g_parts, first, name):
    n_layers, rows, cols = w.shape
    tr = _row_block(rows, first)
    n = len(g_parts)

    def body(*refs):
        w_ref, m_ref, v_ref = refs[:3]
        g_out, d_out, m_out, v_out = refs[3 + n:]
        g = refs[3][...]
        for r in refs[4:3 + n]:
            g = g + r[...]
        g = g[:, :, :cols]
        d_out[...], m_out[...], v_out[...] = _adam_update(w_ref[...], m_ref[...], v_ref[...], g)
        g_out[...] = g

    blk = pl.BlockSpec((1, tr, cols), lambda l, i: (l, i, 0))
    g_blk = pl.BlockSpec((1, tr, g_parts[0].shape[2]), lambda l, i: (l, first // tr + i, 0))
    return pl.pallas_call(
        body, name=name, grid=(n_layers, rows // tr),
        in_specs=[blk] * 3 + [g_blk] * n, out_specs=[blk] * 4,
        out_shape=[jax.ShapeDtypeStruct(w.shape, F32)] * 4,
        compiler_params=_params(("parallel", "parallel")),
    )(w, m, v, *g_parts)


def _adamw(w, m, v, g_parts, name):
    rows, cols = w.shape
    tr = min(rows, 256)
    n = len(g_parts)

    def body(*refs):
        w_ref, m_ref, v_ref = refs[:3]
        g_refs = refs[3:3 + n]
        g_out, d_out, m_out, v_out = refs[3 + n:]
        g = g_refs[0][...]
        for r in g_refs[1:]:
            g = g + r[...]
        d_out[...], m_out[...], v_out[...] = _adam_update(w_ref[...], m_ref[...], v_ref[...], g)
        g_out[...] = g

    blk = pl.BlockSpec((tr, cols), lambda i: (i, 0))
    return pl.pallas_call(
        body, name=name, grid=(rows // tr,),
        in_specs=[blk] * (3 + n), out_specs=[blk] * 4,
        out_shape=[jax.ShapeDtypeStruct((rows, cols), F32)] * 4,
        compiler_params=_params(("parallel",)),
    )(w, m, v, *g_parts)


def _pack(parts, rows, fill=0.0):
    flat = jnp.concatenate([p.reshape(-1) for p in parts])
    return jnp.pad(flat, (0, rows * LANES - flat.shape[0]), constant_values=fill).reshape(rows, LANES)


def _unpack(packed, shapes):
    flat = packed.reshape(-1)
    out, at = [], 0
    for shp in shapes:
        size = 1
        for s in shp:
            size *= s
        out.append(flat[at:at + size].reshape(shp))
        at += size
    return out


def _packed_rows(shapes):
    total = 0
    for shp in shapes:
        size = 1
        for s in shp:
            size *= s
        total += size
    return -(-total // (8 * LANES)) * 8


def _cols_full(g, l):
    t = g[:, l]
    return jnp.moveaxis(t, 0, 1).reshape(t.shape[1], N_CHIPS * t.shape[2])


def _pad_cols(t):
    return jnp.pad(t, ((0, 0),) * (t.ndim - 1) + ((0, D_MODEL - t.shape[-1]),))


def _w_in_of(land_a):
    return jnp.moveaxis(land_a[:, :D_MODEL, :IN_SHARD], 0, 1).reshape(D_MODEL, W_IN_COLS)


def _parts_a(g_w_in, g_w_out):
    cols = jnp.moveaxis(g_w_in.reshape(D_MODEL, N_CHIPS, IN_SHARD), 1, 0)
    return jnp.concatenate([_pad_cols(cols), g_w_out.reshape(N_CHIPS, OUT_SHARD, D_MODEL)], axis=1).astype(BF16)


def _parts_b(g_wgt, g_wut, g_wd):
    return jnp.concatenate([t.reshape(N_CHIPS, FF_SHARD, D_MODEL) for t in (g_wgt, g_wut, g_wd)], axis=1).astype(BF16)


def kernel(x, norm1_g, w_in, dn_conv_w, dn_a_log, dn_dt_bias, dn_norm_g, sc_conv_w, sc_norm_g, w_out, norm2_g, ffn_w_gate, ffn_w_up, ffn_w_down, final_norm_g, loss_target, m_norm1_g, m_w_in, m_dn_conv_w, m_dn_a_log, m_dn_dt_bias, m_dn_norm_g, m_sc_conv_w, m_sc_norm_g, m_w_out, m_norm2_g, m_ffn_w_gate, m_ffn_w_up, m_ffn_w_down, m_final_norm_g, v_norm1_g, v_w_in, v_dn_conv_w, v_dn_a_log, v_dn_dt_bias, v_dn_norm_g, v_sc_conv_w, v_sc_norm_g, v_w_out, v_norm2_g, v_ffn_w_gate, v_ffn_w_up, v_ffn_w_down, v_final_norm_g):
    chip = 2 * lax.axis_index("x") + lax.axis_index("y")

    g_cw, g_scw = _chip_exchange([dn_conv_w, sc_conv_w], "gather_conv", gather=True)

    t_last = lambda t: jnp.swapaxes(t, -1, -2)
    gate_t, up_t = t_last(ffn_w_gate), t_last(ffn_w_up)
    share_a = [jnp.concatenate([_pad_cols(w_in[l]), w_out[l]], axis=0).astype(BF16) for l in range(DEPTH)]
    share_b = [jnp.concatenate([gate_t[l], up_t[l], ffn_w_down[l]], axis=0).astype(BF16) for l in range(DEPTH)]
    zero_token = jnp.zeros((8, LANES), F32)

    def gather_start(l, after):
        a = _exchange_start(share_a[l], _own_slot(share_a[l]), after, "gather_a_start_%d" % l, gather=True)
        b = _exchange_start(share_b[l], _own_slot(share_b[l]), a[4], "gather_b_start_%d" % l, gather=True)
        return a, b

    ga, gb = gather_start(0, g_cw)
    land_a = _exchange_wait(ga, gb[4], "gather_a_wait_0", gather=True)
    act = x[0]
    layers, saved_m, saved_f, lands_b = [], [], [], []
    for l in range(DEPTH):
        hold = 0.0
        if l + 1 < DEPTH:
            ga, gb_next = gather_start(l + 1, land_a)
            hold = gb_next[4][0:1, 0:1]
        wa, wbd = _split_w_in(_w_in_of(land_a))
        al, dt = _gate_rows(dn_a_log[l], dn_dt_bias[l])
        layers.append(dict(
            g1=norm1_g[l][None] + hold, wa=wa, wbd=wbd, cw=_pad_rows(_cols_full(g_cw, l)), al=al, dt=dt,
            gn=dn_norm_g[l][None], scw=_pad_rows(_cols_full(g_scw, l)), gs=sc_norm_g[l][None],
            land_a=land_a, g2=norm2_g[l][None]))
        x1, s = _mixer_fwd(act, layers[l])
        saved_m.append(s)
        lands_b.append(_exchange_wait(gb, x1, "gather_b_wait_%d" % l, gather=True))
        act, s = _ffn_fwd(x1, layers[l], lands_b[l])
        saved_f.append(s)
        if l + 1 < DEPTH:
            land_a = _exchange_wait(ga, act, "gather_a_wait_%d" % (l + 1), gather=True)
            gb = gb_next

    dact, dact_bf16, loss_part, d_final = _loss_head(act, final_norm_g[None], loss_target[0])
    grads, reduce_a, reduce_b = [None] * DEPTH, [None] * DEPTH, [None] * DEPTH
    hold = 0.0
    for l in reversed(range(DEPTH)):
        p = layers[l]
        dx1, dx1_bf16, g = _ffn_back(dact, dact_bf16, saved_f[l], dict(p, g2=p["g2"] + hold), lands_b[l])
        parts = _parts_b(g["wgt"], g["wut"], g["wd"])
        reduce_b[l] = _exchange_start(parts, parts, zero_token, "reduce_b_start_%d" % l, gather=False)
        dact, dact_bf16, gm = _mixer_bwd(dx1, dx1_bf16, saved_m[l], dict(p, gn=p["gn"] + reduce_b[l][4][0:1, 0:1]))
        parts = _parts_a(gm["w_in"], gm["w_out"])
        reduce_a[l] = _exchange_start(parts, parts, zero_token, "reduce_a_start_%d" % l, gather=False)
        hold = reduce_a[l][4][0:1, 0:1]
        grads[l] = dict(gm, g2=g["g2"])
    loss = lax.psum(loss_part[0, 0], ("x", "y", "c"))
    stack = lambda key: jnp.stack([grads[l][key] for l in range(DEPTH)])

    got_b = [_exchange_wait(reduce_b[l], dact, "reduce_b_wait_%d" % l, gather=False) for l in reversed(range(DEPTH))][::-1]
    got_a = [_exchange_wait(reduce_a[l], dact, "reduce_a_wait_%d" % l, gather=False) for l in reversed(range(DEPTH))][::-1]
    sum_a, sum_b = _sum_chips(got_a, "sum_chips_a"), _sum_chips(got_b, "sum_chips_b")
    other_a, other_b = _swap_sibling([sum_a, sum_b])
    big = dict(
        w_in=_adamw_rows(w_in, m_w_in, v_w_in, [sum_a, other_a], 0, "adamw_w_in"),
        w_out=_adamw_rows(w_out, m_w_out, v_w_out, [sum_a, other_a], A_OUT_AT, "adamw_w_out"),
        ffn_w_gate=[t_last(o) for o in _adamw_rows(gate_t, t_last(m_ffn_w_gate), t_last(v_ffn_w_gate),
                                                   [sum_b, other_b], 0, "adamw_gate")],
        ffn_w_up=[t_last(o) for o in _adamw_rows(up_t, t_last(m_ffn_w_up), t_last(v_ffn_w_up),
                                                 [sum_b, other_b], FF_SHARD, "adamw_up")],
        ffn_w_down=_adamw_rows(ffn_w_down, m_ffn_w_down, v_ffn_w_down, [sum_b, other_b], 2 * FF_SHARD, "adamw_down"))

    full_shapes = [(DEPTH, D_MODEL), (DEPTH, D_MODEL), (DEPTH, HEAD_DIM), (DEPTH, SC_WIDTH), (DEPTH, HEADS),
                   (DEPTH, HEADS), (D_MODEL,), (DEPTH, 4, QKV), (DEPTH, 3, SC_WIDTH)]
    small_keys = ("g1", "g2", "gn", "gs", "al", "dt")
    packed = _pack([stack(k) for k in small_keys] + [d_final[0], stack("cw"), stack("scw")], _packed_rows(full_shapes))
    sg = _unpack(_all_reduce_small(packed), full_shapes)
    sg[7] = lax.dynamic_slice_in_dim(sg[7], chip * (QKV // N_CHIPS), QKV // N_CHIPS, axis=2)
    sg[8] = lax.dynamic_slice_in_dim(sg[8], chip * (SC_WIDTH // N_CHIPS), SC_WIDTH // N_CHIPS, axis=2)
    small_names = ("norm1_g", "norm2_g", "dn_norm_g", "sc_norm_g", "dn_a_log", "dn_dt_bias", "final_norm_g",
                   "dn_conv_w", "sc_conv_w")
    sw = (norm1_g, norm2_g, dn_norm_g, sc_norm_g, dn_a_log, dn_dt_bias, final_norm_g, dn_conv_w, sc_conv_w)
    sm = (m_norm1_g, m_norm2_g, m_dn_norm_g, m_sc_norm_g, m_dn_a_log, m_dn_dt_bias, m_final_norm_g, m_dn_conv_w, m_sc_conv_w)
    sv = (v_norm1_g, v_norm2_g, v_dn_norm_g, v_sc_norm_g, v_dn_a_log, v_dn_dt_bias, v_final_norm_g, v_dn_conv_w, v_sc_conv_w)
    shard_shapes = [t.shape for t in sw]
    rows = _packed_rows(shard_shapes)
    outs = _adamw(_pack(sw, rows), _pack(sm, rows), _pack(sv, rows, fill=1.0), [_pack(sg, rows)], "adamw_small")
    small = {name: [] for name in small_names}
    for o in outs:
        for name, t in zip(small_names, _unpack(o, shard_shapes)):
            small[name].append(t)

    order = ("norm1_g", "w_in", "dn_conv_w", "dn_a_log", "dn_dt_bias", "dn_norm_g", "sc_conv_w", "sc_norm_g", "w_out",
             "norm2_g", "ffn_w_gate", "ffn_w_up", "ffn_w_down", "final_norm_g")
    result = {**big, **small}
    return (loss, dact[None], *[result[n][0] for n in order], *[result[n][1] for n in order],
            *[result[n][2] for n in order], *[result[n][3] for n in order])
```

```python
import jax
import jax.numpy as jnp
from jax import lax
from jax.experimental import pallas as pl
from jax.experimental.pallas import tpu as pltpu

F32 = jnp.float32
BF16 = jnp.bfloat16
MESH = pl.DeviceIdType.MESH

D_MODEL = 1024
DEPTH = 4
HEADS = 4
HEAD_DIM = 128
DN_WIDTH = HEADS * HEAD_DIM
SC_WIDTH = 512
SC_GROUPS = 4
D_FF = 2816
CHUNK = 64
QKV = 3 * DN_WIDTH
W_IN_COLS = 4 * DN_WIDTH + 2 * HEADS + 3 * SC_WIDTH
WA_COLS = QKV + DN_WIDTH + 3 * SC_WIDTH
LANES = 128
EPS = 1e-6
Q_SCALE = HEAD_DIM ** -0.5
N_CHIPS = 4
N_DEV = 8
IN_SHARD = W_IN_COLS // N_CHIPS
OUT_SHARD = D_MODEL // N_CHIPS
FF_SHARD = D_FF // N_CHIPS
A_OUT_AT = D_MODEL
A_ROWS = D_MODEL + OUT_SHARD
B_ROWS = 3 * FF_SHARD

ADAM_LR = 0.001
ADAM_B1 = 0.9
ADAM_B2 = 0.999
ADAM_EPS = 1e-08
ADAM_WD = 0.01
ADAM_STEP = 10

VMEM_LIMIT = 56 * 1024 * 1024

NN = (((1,), (0,)), ((), ()))
NT = (((1,), (1,)), ((), ()))
TN = (((0,), (0,)), ((), ()))


def _mm(a, b, dims=NN):
    return lax.dot_general(a.astype(BF16), b.astype(BF16), dims, preferred_element_type=F32)


def _mm32(a, b, dims=NN):
    return lax.dot_general(a, b, dims, preferred_element_type=F32, precision=lax.Precision.HIGHEST)


def _params(sem, vmem=VMEM_LIMIT):
    return pltpu.CompilerParams(dimension_semantics=sem, vmem_limit_bytes=vmem)


def _sigmoid(x):
    return 0.5 * jnp.tanh(0.5 * x) + 0.5


def _softplus(x):
    return jnp.maximum(x, 0.0) + jnp.log1p(jnp.exp(-jnp.abs(x)))


def _row_acc(acc_ref, val):
    acc_ref[0:1, :] += jnp.sum(val, axis=0, keepdims=True)


def _rms_bwd(dh, xh, r, gain):
    dxh = dh * gain
    return r * (dxh - xh * jnp.mean(dxh * xh, axis=-1, keepdims=True))


def _before_halo(tb):
    return lambda i: (jnp.maximum(i * (tb // 8) - 1, 0), 0)


def _after_halo(tb, n_rows):
    last = n_rows // 8 - 1
    return lambda i: (jnp.minimum((i + 1) * (tb // 8), last), 0)


def _rows_from(xc, offset, tb):
    part = offset % 8
    if part:
        xc = pltpu.roll(xc, xc.shape[0] - part, 0)
    return xc[offset - part:offset - part + tb, :]


def _taps(xc, w, n_taps, tb, first):
    out = w[0:1, :] * _rows_from(xc, first, tb)
    for j in range(1, n_taps):
        out = out + w[j:j + 1, :] * _rows_from(xc, first + j, tb)
    return out


def _in_proj(x, g1, wa, wbd):
    T = x.shape[0]
    tb = 256

    def body(x_ref, g_ref, wa_ref, wbd_ref, qkv_ref, z_ref, sc_ref, bd_ref, ht_ref):
        xv = x_ref[...]
        r = lax.rsqrt(jnp.mean(xv * xv, axis=-1, keepdims=True) + EPS)
        h = (xv * r * g_ref[...]).astype(BF16)
        p = jnp.dot(h, wa_ref[...], preferred_element_type=F32)
        qkv_ref[...] = p[:, :QKV]
        z_ref[...] = p[:, QKV:QKV + DN_WIDTH]
        sc_ref[...] = p[:, QKV + DN_WIDTH:]
        bd_ref[...] = jnp.dot(h, wbd_ref[...], preferred_element_type=F32)
        ht_ref[...] = h.T

    tok = lambda w: pl.BlockSpec((tb, w), lambda i: (i, 0))
    full = lambda a: pl.BlockSpec(a.shape, lambda i: (0, 0))
    return pl.pallas_call(
        body, name="in_proj", grid=(T // tb,),
        in_specs=[tok(D_MODEL), full(g1), full(wa), full(wbd)],
        out_specs=[tok(QKV), tok(DN_WIDTH), tok(3 * SC_WIDTH), tok(LANES),
                   pl.BlockSpec((D_MODEL, tb), lambda i: (0, i))],
        out_shape=[jax.ShapeDtypeStruct((T, QKV), F32), jax.ShapeDtypeStruct((T, DN_WIDTH), F32),
                   jax.ShapeDtypeStruct((T, 3 * SC_WIDTH), F32), jax.ShapeDtypeStruct((T, LANES), F32),
                   jax.ShapeDtypeStruct((D_MODEL, T), BF16)],
        compiler_params=_params(("parallel",)),
    )(x, g1, wa, wbd)


def _dn_act(pre, halo, cw, tb):
    xc = jnp.concatenate([halo, pre], axis=0)
    c = _taps(xc, cw, 4, tb, 5)
    sg = _sigmoid(c)
    return xc, c, sg, c * sg


def _gates(bd, al_row, dt_row):
    lane = lax.broadcasted_iota(jnp.int32, bd.shape, 1)
    beta = _sigmoid(bd)
    g = -jnp.exp(al_row) * _softplus(bd + dt_row)
    return jnp.where(lane < HEADS, beta, jnp.where(lane < 2 * HEADS, g, 0.0))


def _dn_prep(qkv, cw, bd, al_row, dt_row):
    T = qkv.shape[0]
    tb = 512

    def body(pre_ref, halo_ref, cw_ref, bd_ref, al_ref, dt_ref, q_ref, k_ref, v_ref, bg_ref):
        halo = jnp.where(pl.program_id(0) > 0, halo_ref[...], 0.0)
        _, _, _, a = _dn_act(pre_ref[...], halo, cw_ref[...], tb)
        for hh in range(HEADS):
            sl = slice(HEAD_DIM * hh, HEAD_DIM * (hh + 1))
            qs = a[:, sl]
            q_ref[:, sl] = qs * (lax.rsqrt(jnp.sum(qs * qs, axis=-1, keepdims=True) + EPS) * Q_SCALE)
            ks = a[:, DN_WIDTH + HEAD_DIM * hh:DN_WIDTH + HEAD_DIM * (hh + 1)]
            k_ref[:, sl] = ks * lax.rsqrt(jnp.sum(ks * ks, axis=-1, keepdims=True) + EPS)
        v_ref[...] = a[:, 2 * DN_WIDTH:]
        gates = _gates(bd_ref[...], al_ref[...], dt_ref[...])
        lane = lax.broadcasted_iota(jnp.int32, gates.shape, 1)
        bg_ref[...] = jnp.where(lane < HEADS, gates, _mm32(_chunk_cumsum_matrix(tb), gates))

    tok = lambda w: pl.BlockSpec((tb, w), lambda i: (i, 0))
    full = lambda a: pl.BlockSpec(a.shape, lambda i: (0, 0))
    return pl.pallas_call(
        body, name="dn_prep", grid=(T // tb,),
        in_specs=[tok(QKV), pl.BlockSpec((8, QKV), _before_halo(tb)), full(cw), tok(LANES), full(al_row), full(dt_row)],
        out_specs=[tok(DN_WIDTH), tok(DN_WIDTH), tok(DN_WIDTH), tok(LANES)],
        out_shape=[jax.ShapeDtypeStruct((T, DN_WIDTH), F32)] * 3 + [jax.ShapeDtypeStruct((T, LANES), F32)],
        compiler_params=_params(("parallel",)),
    )(qkv, qkv, cw, bd, al_row, dt_row)


def _chunk_masks():
    row = lax.broadcasted_iota(jnp.int32, (CHUNK, CHUNK), 0)
    col = lax.broadcasted_iota(jnp.int32, (CHUNK, CHUNK), 1)
    return row >= col, row > col


def _chunk_cumsum_matrix(n):
    row = lax.broadcasted_iota(jnp.int32, (n, n), 0)
    col = lax.broadcasted_iota(jnp.int32, (n, n), 1)
    return jnp.logical_and(row >= col, row // CHUNK == col // CHUNK).astype(F32)


def _chunk_units(q_ref, k_ref, v_ref, bg_ref, rows):
    bgc = bg_ref[rows, :]
    bg_t = bgc.T
    qv, kv, vv = q_ref[rows, :], k_ref[rows, :], v_ref[rows, :]
    units = []
    for h in range(HEADS):
        sl = slice(HEAD_DIM * h, HEAD_DIM * (h + 1))
        units.append((qv[:, sl], kv[:, sl], vv[:, sl], bgc[:, h:h + 1], bgc[:, HEADS + h:HEADS + h + 1],
                      bg_t[HEADS + h:HEADS + h + 1, :]))
    return units


def _units_local(units, masks):
    causal, strict = masks
    pre = []
    for q, k, v, beta, gc, gr in units:
        kb = k * beta
        eg = jnp.exp(gc)
        g_last = gc[CHUNK - 1:CHUNK, :]
        ek = jnp.exp(g_last - gc)
        pre.append(dict(q=q, k=k, v=v, beta=beta, decay=jnp.exp(jnp.where(causal, gc - gr, -1e30)), kb=kb, vb=v * beta,
                        eg=eg, kbg=kb * eg, ek=ek, gl=jnp.exp(g_last), q_dec=q * eg, k_dec=k * ek))
    both = [_mm(jnp.concatenate([p["kb"], p["q"]], axis=0), p["k"], NT) for p in pre]
    for p, b in zip(pre, both):
        p["low"] = jnp.where(strict, b[:CHUNK] * p["decay"], 0.0)
        p["qk"] = jnp.where(causal, b[CHUNK:] * p["decay"], 0.0)
    xs = [-p["low"] for p in pre]
    pw = [_mm(p["low"], p["low"]) for p in pre]
    for _ in range(4):
        both = [_mm(jnp.concatenate([pp, x], axis=0), pp) for pp, x in zip(pw, xs)]
        xs = [x + pp + b[CHUNK:] for x, pp, b in zip(xs, pw, both)]
        pw = [b[:CHUNK] for b in both]
    last = [_mm(x, pp) for x, pp in zip(xs, pw)]
    xs = [x + pp + b for x, pp, b in zip(xs, pw, last)]
    uw = [_mm(x, jnp.concatenate([p["vb"], p["kbg"]], axis=1)) for x, p in zip(xs, pre)]
    for p, x, b in zip(pre, xs, uw):
        p["xm"] = x
        p["u"] = p["vb"] + b[:, :HEAD_DIM]
        p["w"] = p["kbg"] + b[:, HEAD_DIM:]
    return pre


def _delta_fwd(q, k, v, bg):
    T = q.shape[0]
    tb = 512
    n_chunk = tb // CHUNK

    def body(q_ref, k_ref, v_ref, bg_ref, o_ref, st_ref, s_ref):
        @pl.when(pl.program_id(0) == 0)
        def _():
            s_ref[...] = jnp.zeros_like(s_ref)

        masks = _chunk_masks()

        def pair(pi, carry):
            rows = [pl.ds(pl.multiple_of((2 * pi + j) * CHUNK, CHUNK), CHUNK) for j in range(2)]
            loc = _units_local(_chunk_units(q_ref, k_ref, v_ref, bg_ref, rows[0])
                               + _chunk_units(q_ref, k_ref, v_ref, bg_ref, rows[1]), masks)
            states = [s_ref[h] for h in range(HEADS)]
            for j in range(2):
                lj = loc[HEADS * j:HEADS * (j + 1)]
                ws = [_mm(jnp.concatenate([p["w"], p["q_dec"]], axis=0), s) for p, s in zip(lj, states)]
                v_new = [p["u"] - b[:CHUNK] for p, b in zip(lj, ws)]
                intra = [_mm(p["qk"], vn) for p, vn in zip(lj, v_new)]
                upd = [_mm(p["k_dec"], vn, TN) for p, vn in zip(lj, v_new)]
                o_ref[rows[j], :] = jnp.concatenate([b[CHUNK:] + a for b, a in zip(ws, intra)], axis=1)
                for h in range(HEADS):
                    st_ref[2 * pi + j, h] = states[h]
                states = [p["gl"] * s + d for p, s, d in zip(lj, states, upd)]
            for h in range(HEADS):
                s_ref[h] = states[h]
            return carry

        lax.fori_loop(0, n_chunk // 2, pair, 0)

    tok = lambda w: pl.BlockSpec((tb, w), lambda i: (i, 0))
    return pl.pallas_call(
        body, name="delta_fwd", grid=(T // tb,),
        in_specs=[tok(DN_WIDTH), tok(DN_WIDTH), tok(DN_WIDTH), tok(LANES)],
        out_specs=[tok(DN_WIDTH), pl.BlockSpec((n_chunk, HEADS, HEAD_DIM, HEAD_DIM), lambda i: (i, 0, 0, 0))],
        out_shape=[jax.ShapeDtypeStruct((T, DN_WIDTH), F32),
                   jax.ShapeDtypeStruct((T // CHUNK, HEADS, HEAD_DIM, HEAD_DIM), F32)],
        scratch_shapes=[pltpu.VMEM((HEADS, HEAD_DIM, HEAD_DIM), F32)],
        compiler_params=_params(("arbitrary",)),
    )(q, k, v, bg)


def _dn_out(o, z, gn):
    outs, ohs, rs = [], [], []
    for hh in range(HEADS):
        oh = o[:, HEAD_DIM * hh:HEAD_DIM * (hh + 1)]
        r = lax.rsqrt(jnp.mean(oh * oh, axis=-1, keepdims=True) + EPS)
        ohs.append(oh * r)
        rs.append(r)
    sz = _sigmoid(z)
    oh = jnp.concatenate(ohs, axis=1)
    gn4 = jnp.concatenate([gn] * HEADS, axis=1)
    return oh * gn4 * (z * sz), oh, rs, sz, gn4


def _sc_fwd(sc_in, halo, cw, tb):
    xc = jnp.concatenate([halo, sc_in], axis=0)
    u = xc[:, SC_WIDTH:2 * SC_WIDTH] * xc[:, 2 * SC_WIDTH:]
    cv = _taps(u, cw, 3, tb, 6)
    gate_b = sc_in[:, :SC_WIDTH]
    y = gate_b * cv
    gw = SC_WIDTH // SC_GROUPS
    yhs, rs = [], []
    for gi in range(SC_GROUPS):
        yg = y[:, gw * gi:gw * (gi + 1)]
        r = lax.rsqrt(jnp.mean(yg * yg, axis=-1, keepdims=True) + EPS)
        yhs.append(yg * r)
        rs.append(r)
    return u, cv, gate_b, jnp.concatenate(yhs, axis=1), rs


def _shard_rows(land, first, rows):
    assert first % rows == 0 and land.shape[0] == N_CHIPS
    return pl.BlockSpec((N_CHIPS, rows, land.shape[2]), lambda i: (0, first // rows, 0))


def _whole(w_ref):
    n, rows, cols = w_ref.shape
    return w_ref[...].reshape(n * rows, cols)


def _mix_out(o, z, sc_in, x, land_a, gn, scw, gs):
    T = x.shape[0]
    tb = 256

    def body(o_ref, z_ref, sc_ref, halo_ref, x_ref, w_ref, gn_ref, scw_ref, gs_ref, x1_ref, mt_ref):
        o_n = _dn_out(o_ref[...], z_ref[...], gn_ref[...])[0]
        halo = jnp.where(pl.program_id(0) > 0, halo_ref[...], 0.0)
        yh = _sc_fwd(sc_ref[...], halo, scw_ref[...], tb)[3]
        mix = jnp.concatenate([o_n, yh * gs_ref[...]], axis=1).astype(BF16)
        x1_ref[...] = x_ref[...] + jnp.dot(mix, _whole(w_ref), preferred_element_type=F32)
        mt_ref[...] = mix.T

    tok = lambda w: pl.BlockSpec((tb, w), lambda i: (i, 0))
    full = lambda a: pl.BlockSpec(a.shape, lambda i: (0, 0))
    return pl.pallas_call(
        body, name="mix_out", grid=(T // tb,),
        in_specs=[tok(DN_WIDTH), tok(DN_WIDTH), tok(3 * SC_WIDTH), pl.BlockSpec((8, 3 * SC_WIDTH), _before_halo(tb)),
                  tok(D_MODEL), _shard_rows(land_a, A_OUT_AT, OUT_SHARD), full(gn), full(scw), full(gs)],
        out_specs=[tok(D_MODEL), pl.BlockSpec((D_MODEL, tb), lambda i: (0, i))],
        out_shape=[jax.ShapeDtypeStruct((T, D_MODEL), F32), jax.ShapeDtypeStruct((D_MODEL, T), BF16)],
        compiler_params=_params(("parallel",)),
    )(o, z, sc_in, sc_in, x, land_a, gn, scw, gs)


def _ffn(x1, g2, land_b):
    T = x1.shape[0]
    tb = 256

    def body(x_ref, g_ref, wgt_ref, wut_ref, wd_ref, x2_ref, a_ref, b_ref, h_ref):
        xv = x_ref[...]
        r = lax.rsqrt(jnp.mean(xv * xv, axis=-1, keepdims=True) + EPS)
        h = (xv * r * g_ref[...]).astype(BF16)
        a = lax.dot_general(h, _whole(wgt_ref), NT, preferred_element_type=F32)
        b = lax.dot_general(h, _whole(wut_ref), NT, preferred_element_type=F32)
        act = (a * _sigmoid(a) * b).astype(BF16)
        x2_ref[...] = xv + jnp.dot(act, _whole(wd_ref), preferred_element_type=F32)
        a_ref[...] = a.astype(BF16)
        b_ref[...] = b.astype(BF16)
        h_ref[...] = h

    tok = lambda w: pl.BlockSpec((tb, w), lambda i: (i, 0))
    return pl.pallas_call(
        body, name="ffn", grid=(T // tb,),
        in_specs=[tok(D_MODEL), pl.BlockSpec(g2.shape, lambda i: (0, 0)), _shard_rows(land_b, 0, FF_SHARD),
                  _shard_rows(land_b, FF_SHARD, FF_SHARD), _shard_rows(land_b, 2 * FF_SHARD, FF_SHARD)],
        out_specs=[tok(D_MODEL), tok(D_FF), tok(D_FF), tok(D_MODEL)],
        out_shape=[jax.ShapeDtypeStruct((T, D_MODEL), F32), jax.ShapeDtypeStruct((T, D_FF), BF16),
                   jax.ShapeDtypeStruct((T, D_FF), BF16), jax.ShapeDtypeStruct((T, D_MODEL), BF16)],
        compiler_params=_params(("parallel",)),
    )(x1, g2, land_b, land_b, land_b)


def _loss_head(x, gf, target):
    T = x.shape[0]
    tb = 512

    def body(x_ref, g_ref, t_ref, dx_ref, dxb_ref, loss_ref, dg_ref):
        @pl.when(pl.program_id(0) == 0)
        def _():
            loss_ref[...] = jnp.zeros_like(loss_ref)
            dg_ref[...] = jnp.zeros_like(dg_ref)

        xv = x_ref[...]
        r = lax.rsqrt(jnp.mean(xv * xv, axis=-1, keepdims=True) + EPS)
        xh = xv * r
        err = xh * g_ref[...] - t_ref[...]
        per_tok = jnp.mean(err * err, axis=-1, keepdims=True)
        loss_ref[...] += 0.5 * jnp.sum(per_tok, axis=0, keepdims=True)
        dy = err * (1.0 / D_MODEL)
        _row_acc(dg_ref, dy * xh)
        dx = _rms_bwd(dy, xh, r, g_ref[...])
        dx_ref[...] = dx
        dxb_ref[...] = dx.astype(BF16)

    tok = pl.BlockSpec((tb, D_MODEL), lambda i: (i, 0))
    return pl.pallas_call(
        body, name="loss_head", grid=(T // tb,),
        in_specs=[tok, pl.BlockSpec(gf.shape, lambda i: (0, 0)), tok],
        out_specs=[tok, tok, pl.BlockSpec((8, LANES), lambda i: (0, 0)), pl.BlockSpec((8, D_MODEL), lambda i: (0, 0))],
        out_shape=[jax.ShapeDtypeStruct((T, D_MODEL), F32), jax.ShapeDtypeStruct((T, D_MODEL), BF16),
                   jax.ShapeDtypeStruct((8, LANES), F32), jax.ShapeDtypeStruct((8, D_MODEL), F32)],
        compiler_params=_params(("arbitrary",)),
    )(x, gf, target)


def _ffn_bwd(dx2, x1, a, b, g2, land_b):
    T = x1.shape[0]
    tb = 256

    def body(dx2_ref, x_ref, a_ref, b_ref, g_ref, wgt_ref, wut_ref, wd_ref,
             dx1_ref, dx1b_ref, dat_ref, dbt_ref, at_ref, dg_ref):
        @pl.when(pl.program_id(0) == 0)
        def _():
            dg_ref[...] = jnp.zeros_like(dg_ref)

        dx2v = dx2_ref[...]
        av = a_ref[...].astype(F32)
        bv = b_ref[...].astype(F32)
        dact = _mm(dx2v, _whole(wd_ref), NT)
        sa = _sigmoid(av)
        silu = av * sa
        da = (dact * bv * (sa * (1.0 + av * (1.0 - sa)))).astype(BF16)
        db = (dact * silu).astype(BF16)
        dh = _mm(da, _whole(wgt_ref)) + _mm(db, _whole(wut_ref))
        xv = x_ref[...]
        r = lax.rsqrt(jnp.mean(xv * xv, axis=-1, keepdims=True) + EPS)
        xh = xv * r
        _row_acc(dg_ref, dh * xh)
        dx1 = dx2v + _rms_bwd(dh, xh, r, g_ref[...])
        dx1_ref[...] = dx1
        dx1b_ref[...] = dx1.astype(BF16)
        dat_ref[...] = da.T
        dbt_ref[...] = db.T
        at_ref[...] = (silu * bv).astype(BF16).T

    tok = lambda w: pl.BlockSpec((tb, w), lambda i: (i, 0))
    tr = pl.BlockSpec((D_FF, tb), lambda i: (0, i))
    return pl.pallas_call(
        body, name="ffn_bwd", grid=(T // tb,),
        in_specs=[tok(D_MODEL), tok(D_MODEL), tok(D_FF), tok(D_FF), pl.BlockSpec(g2.shape, lambda i: (0, 0)),
                  _shard_rows(land_b, 0, FF_SHARD), _shard_rows(land_b, FF_SHARD, FF_SHARD),
                  _shard_rows(land_b, 2 * FF_SHARD, FF_SHARD)],
        out_specs=[tok(D_MODEL), tok(D_MODEL), tr, tr, tr, pl.BlockSpec((8, D_MODEL), lambda i: (0, 0))],
        out_shape=[jax.ShapeDtypeStruct((T, D_MODEL), F32), jax.ShapeDtypeStruct((T, D_MODEL), BF16)]
        + [jax.ShapeDtypeStruct((D_FF, T), BF16)] * 3 + [jax.ShapeDtypeStruct((8, D_MODEL), F32)],
        compiler_params=_params(("arbitrary",)),
    )(dx2, x1, a, b, g2, land_b, land_b, land_b)


def _wgrad(at, b, bm, bn, name):
    M, T = at.shape
    N = b.shape[1]
    bk = min(T, 1024)

    def body(a_ref, b_ref, o_ref):
        @pl.when(pl.program_id(2) == 0)
        def _():
            o_ref[...] = jnp.zeros_like(o_ref)

        o_ref[...] += jnp.dot(a_ref[...], b_ref[...], preferred_element_type=F32)

    return pl.pallas_call(
        body, name=name, grid=(M // bm, N // bn, T // bk),
        in_specs=[pl.BlockSpec((bm, bk), lambda i, j, kk: (i, kk)), pl.BlockSpec((bk, bn), lambda i, j, kk: (kk, j))],
        out_specs=pl.BlockSpec((bm, bn), lambda i, j, kk: (i, j)),
        out_shape=jax.ShapeDtypeStruct((M, N), F32),
        compiler_params=_params(("parallel", "parallel", "arbitrary")),
    )(at, b)


def _wgrad_share(at, b, parts, first, name):
    M, T = at.shape
    rows = M // N_CHIPS
    assert first % rows == 0 and b.shape[1] == parts.shape[2]
    bk = min(T, 1024)
    n_k = T // bk

    def body(a_ref, b_ref, parts_ref, o_ref, acc_ref):
        kk = pl.program_id(1)

        @pl.when(kk == 0)
        def _():
            acc_ref[...] = jnp.zeros_like(acc_ref)

        acc_ref[...] += jnp.dot(a_ref[...], b_ref[...], preferred_element_type=F32)

        @pl.when(kk == n_k - 1)
        def _():
            o_ref[0] = acc_ref[...].astype(BF16)

    return pl.pallas_call(
        body, name=name, grid=(N_CHIPS, n_k),
        in_specs=[pl.BlockSpec((rows, bk), lambda i, kk: (i, kk)), pl.BlockSpec((bk, b.shape[1]), lambda i, kk: (kk, 0)), _ANY],
        out_specs=pl.BlockSpec((1, rows, b.shape[1]), lambda i, kk: (i, first // rows, 0)),
        out_shape=jax.ShapeDtypeStruct(parts.shape, BF16),
        scratch_shapes=[pltpu.VMEM((rows, b.shape[1]), F32)],
        input_output_aliases={2: 0},
        compiler_params=_params(("parallel", "arbitrary")),
    )(at, b, parts)


def _mix_out_bwd(dx1, o, z, sc_in, land_a, gn, scw, gs):
    T = dx1.shape[0]
    tb = 256

    def body(dx_ref, o_ref, z_ref, sc_ref, halo_ref, w_ref, gn_ref, scw_ref, gs_ref,
             do_ref, dz_ref, dgb_ref, dcv_ref, dgn_ref, dgs_ref, dscw_ref):
        @pl.when(pl.program_id(0) == 0)
        def _():
            dgn_ref[...] = jnp.zeros_like(dgn_ref)
            dgs_ref[...] = jnp.zeros_like(dgs_ref)
            dscw_ref[...] = jnp.zeros_like(dscw_ref)

        dmix = _mm(dx_ref[...], _whole(w_ref), NT)
        don = dmix[:, :DN_WIDTH]
        dosc = dmix[:, DN_WIDTH:]
        zv = z_ref[...]
        _, oh, rs, sz, gn4 = _dn_out(o_ref[...], zv, gn_ref[...])
        silu_z = zv * sz
        dgn_full = don * oh * silu_z
        dgn_ref[0:1, :] += jnp.sum(sum(dgn_full[:, HEAD_DIM * hh:HEAD_DIM * (hh + 1)] for hh in range(HEADS)),
                                   axis=0, keepdims=True)
        dz_ref[...] = (don * oh * gn4 * (sz * (1.0 + zv * (1.0 - sz)))).astype(BF16)
        t = don * gn4 * silu_z
        for hh in range(HEADS):
            sl = slice(HEAD_DIM * hh, HEAD_DIM * (hh + 1))
            th, ohh = t[:, sl], oh[:, sl]
            do_ref[:, sl] = rs[hh] * (th - ohh * jnp.mean(th * ohh, axis=-1, keepdims=True))
        halo = jnp.where(pl.program_id(0) > 0, halo_ref[...], 0.0)
        u, cv, gate_b, yh, rys = _sc_fwd(sc_ref[...], halo, scw_ref[...], tb)
        _row_acc(dgs_ref, dosc * yh)
        ty = dosc * gs_ref[...]
        gw = SC_WIDTH // SC_GROUPS
        dys = []
        for gi in range(SC_GROUPS):
            sl = slice(gw * gi, gw * (gi + 1))
            tg, yg = ty[:, sl], yh[:, sl]
            dys.append(rys[gi] * (tg - yg * jnp.mean(tg * yg, axis=-1, keepdims=True)))
        dy = jnp.concatenate(dys, axis=1)
        dgb_ref[...] = dy * cv
        dcv = dy * gate_b
        dcv_ref[...] = dcv
        for j in range(3):
            dscw_ref[j:j + 1, :] += jnp.sum(dcv * _rows_from(u, 6 + j, tb), axis=0, keepdims=True)

    tok = lambda w: pl.BlockSpec((tb, w), lambda i: (i, 0))
    full = lambda t: pl.BlockSpec(t.shape, lambda i: (0, 0))
    acc = lambda w: pl.BlockSpec((8, w), lambda i: (0, 0))
    return pl.pallas_call(
        body, name="mix_out_bwd", grid=(T // tb,),
        in_specs=[tok(D_MODEL), tok(DN_WIDTH), tok(DN_WIDTH), tok(3 * SC_WIDTH),
                  pl.BlockSpec((8, 3 * SC_WIDTH), _before_halo(tb)), _shard_rows(land_a, A_OUT_AT, OUT_SHARD),
                  full(gn), full(scw), full(gs)],
        out_specs=[tok(DN_WIDTH), tok(DN_WIDTH), tok(SC_WIDTH), tok(SC_WIDTH), acc(HEAD_DIM), acc(SC_WIDTH), acc(SC_WIDTH)],
        out_shape=[jax.ShapeDtypeStruct((T, DN_WIDTH), F32), jax.ShapeDtypeStruct((T, DN_WIDTH), BF16),
                   jax.ShapeDtypeStruct((T, SC_WIDTH), F32), jax.ShapeDtypeStruct((T, SC_WIDTH), F32),
                   jax.ShapeDtypeStruct((8, HEAD_DIM), F32), jax.ShapeDtypeStruct((8, SC_WIDTH), F32),
                   jax.ShapeDtypeStruct((8, SC_WIDTH), F32)],
        compiler_params=_params(("arbitrary",)),
    )(dx1, o, z, sc_in, sc_in, land_a, gn, scw, gs)


def _sc_conv_bwd(dcv, dgb, sc_in, scw):
    T = dcv.shape[0]
    tb = 512

    def body(dcv_ref, halo_ref, dgb_ref, sc_ref, w_ref, out_ref):
        last = pl.program_id(0) == pl.num_programs(0) - 1
        halo = jnp.where(last, 0.0, halo_ref[...])
        xc = jnp.concatenate([dcv_ref[...], halo], axis=0)
        w = w_ref[...]
        du = w[2:3, :] * xc[0:tb, :] + w[1:2, :] * _rows_from(xc, 1, tb) + w[0:1, :] * _rows_from(xc, 2, tb)
        sc = sc_ref[...]
        out_ref[:, :SC_WIDTH] = dgb_ref[...].astype(BF16)
        out_ref[:, SC_WIDTH:2 * SC_WIDTH] = (du * sc[:, 2 * SC_WIDTH:]).astype(BF16)
        out_ref[:, 2 * SC_WIDTH:] = (du * sc[:, SC_WIDTH:2 * SC_WIDTH]).astype(BF16)

    tok = lambda w: pl.BlockSpec((tb, w), lambda i: (i, 0))
    return pl.pallas_call(
        body, name="sc_conv_bwd", grid=(T // tb,),
        in_specs=[tok(SC_WIDTH), pl.BlockSpec((8, SC_WIDTH), _after_halo(tb, T)), tok(SC_WIDTH), tok(3 * SC_WIDTH),
                  pl.BlockSpec(scw.shape, lambda i: (0, 0))],
        out_specs=tok(3 * SC_WIDTH),
        out_shape=jax.ShapeDtypeStruct((T, 3 * SC_WIDTH), BF16),
        compiler_params=_params(("parallel",)),
    )(dcv, dcv, dgb, sc_in, scw)


def _delta_bwd(q, k, v, bg, states, do):
    T = q.shape[0]
    tb = 512
    n_chunk = tb // CHUNK
    nb = T // tb

    def body(q_ref, k_ref, v_ref, bg_ref, st_ref, do_ref, dq_ref, dk_ref, dv_ref, dbg_ref, ds_ref):
        @pl.when(pl.program_id(0) == 0)
        def _():
            ds_ref[...] = jnp.zeros_like(ds_ref)

        masks = _chunk_masks()
        causal, strict = masks
        lane = lax.broadcasted_iota(jnp.int32, (CHUNK, LANES), 1)
        last_row = lax.broadcasted_iota(jnp.int32, (CHUNK, 1), 0) == CHUNK - 1
        cat = jnp.concatenate
        heads = range(HEADS)

        def open_chunk(ci, loc):
            rows = pl.ds(pl.multiple_of(ci * CHUNK, CHUNK), CHUNK)
            dov = do_ref[rows, :]
            return dict(rows=rows, loc=loc, do=[dov[:, HEAD_DIM * h:HEAD_DIM * (h + 1)] for h in heads],
                        state=[st_ref[ci, h] for h in heads])

        def a_free(c):
            loc, do, state = c["loc"], c["do"], c["state"]
            w_s = [_mm(p["w"], s) for p, s in zip(loc, state)]
            c["dq_dec"] = [_mm(d, s, NT) for d, s in zip(do, state)]
            c["qk_do"] = [_mm(p["qk"], d, TN) for p, d in zip(loc, do)]
            c["qd_do"] = [_mm(p["q_dec"], d, TN) for p, d in zip(loc, do)]
            c["v_new"] = [p["u"] - t for p, t in zip(loc, w_s)]
            c["dqk"] = [jnp.where(causal, _mm(d, vn, NT), 0.0) for d, vn in zip(do, c["v_new"])]

        def a_state(c, ds_next):
            c["ds_next"] = ds_next
            kd_ds = [_mm(p["k_dec"], d) for p, d in zip(c["loc"], ds_next)]
            c["dk_dec"] = [_mm(vn, d, NT) for vn, d in zip(c["v_new"], ds_next)]
            c["dv_new"] = [a + b for a, b in zip(c["qk_do"], kd_ds)]

        def b_state(c):
            loc = c["loc"]
            w_dv = [_mm(p["w"], dvn, TN) for p, dvn in zip(loc, c["dv_new"])]
            c["dw"] = [-_mm(dvn, s, NT) for dvn, s in zip(c["dv_new"], c["state"])]
            return [loc[h]["gl"] * c["ds_next"][h] + c["qd_do"][h] - w_dv[h] for h in heads]

        def c_solve(c):
            loc, dv_new, dw = c["loc"], c["dv_new"], c["dw"]
            c["dtm"] = [_mm(cat([dvn, d], axis=1), cat([p["vb"], p["kbg"]], axis=1), NT) for dvn, d, p in zip(dv_new, dw, loc)]
            x_t = [_mm(p["xm"], cat([dvn, d], axis=1), TN) for p, dvn, d in zip(loc, dv_new, dw)]
            c["dvb"] = [dvn + t[:, :HEAD_DIM] for dvn, t in zip(dv_new, x_t)]
            c["dkbg"] = [d + t[:, HEAD_DIM:] for d, t in zip(dw, x_t)]

        def d_solve(c):
            c["y"] = [t + _mm(p["xm"], t, TN) for p, t in zip(c["loc"], c["dtm"])]

        def e_solve(c):
            c["dlow"] = [jnp.where(strict, -(t + _mm(t, p["xm"], NT)), 0.0) for p, t in zip(c["loc"], c["y"])]

        def f_close(c):
            loc, rows = c["loc"], c["rows"]
            dmm = [d * p["decay"] for d, p in zip(c["dlow"], loc)]
            dnn = [d * p["decay"] for d, p in zip(c["dqk"], loc)]
            by_k = [_mm(cat([a, b], axis=0), p["k"]) for a, b, p in zip(dmm, dnn, loc)]
            dk_mm = [_mm(cat([a, b], axis=0), cat([p["kb"], p["q"]], axis=0), TN) for a, b, p in zip(dmm, dnn, loc)]
            dq_out, dk_out, dv_out = [], [], []
            dbeta_all = jnp.zeros((CHUNK, LANES), F32)
            dgc_all = jnp.zeros((CHUNK, LANES), F32)
            for h in heads:
                p = loc[h]
                dkb = by_k[h][:CHUNK] + c["dkbg"][h] * p["eg"]
                dq_out.append(by_k[h][CHUNK:] + c["dq_dec"][h] * p["eg"])
                dk_out.append(dk_mm[h] + c["dk_dec"][h] * p["ek"] + dkb * p["beta"])
                dv_out.append(c["dvb"][h] * p["beta"])
                dbeta = jnp.sum(dkb * p["k"] + c["dvb"][h] * p["v"], axis=1, keepdims=True)
                e = c["dlow"][h] * p["low"] + c["dqk"][h] * p["qk"]
                kd = jnp.sum(c["dk_dec"][h] * p["k_dec"], axis=1, keepdims=True)
                dgc = (jnp.sum(e, axis=1, keepdims=True) - jnp.sum(e.T, axis=1, keepdims=True)
                       + jnp.sum(c["dq_dec"][h] * p["q_dec"], axis=1, keepdims=True) - kd
                       + jnp.sum(c["dkbg"][h] * p["kbg"], axis=1, keepdims=True))
                dgl = jnp.sum(jnp.sum(c["ds_next"][h] * c["state"][h], axis=1, keepdims=True), axis=0, keepdims=True)
                d_last = jnp.sum(kd, axis=0, keepdims=True) + dgl * p["gl"]
                dgc = dgc + jnp.where(last_row, d_last, 0.0)
                dbeta_all = jnp.where(lane == h, dbeta, dbeta_all)
                dgc_all = jnp.where(lane == h + HEADS, dgc, dgc_all)
            dq_ref[rows, :] = cat(dq_out, axis=1)
            dk_ref[rows, :] = cat(dk_out, axis=1)
            dv_ref[rows, :] = cat(dv_out, axis=1)
            dbg_ref[rows, :] = dbeta_all + dgc_all

        def pair(pj, carry):
            hi = n_chunk - 1 - 2 * pj
            lo = hi - 1
            rows = [pl.ds(pl.multiple_of(ci * CHUNK, CHUNK), CHUNK) for ci in (hi, lo)]
            loc = _units_local(_chunk_units(q_ref, k_ref, v_ref, bg_ref, rows[0])
                               + _chunk_units(q_ref, k_ref, v_ref, bg_ref, rows[1]), masks)
            c_hi, c_lo = open_chunk(hi, loc[:HEADS]), open_chunk(lo, loc[HEADS:])
            a_free(c_hi)
            a_free(c_lo)
            a_state(c_hi, [ds_ref[h] for h in heads])
            ds_mid = b_state(c_hi)
            a_state(c_lo, ds_mid)
            c_solve(c_hi)
            ds_out = b_state(c_lo)
            for h in heads:
                ds_ref[h] = ds_out[h]
            d_solve(c_hi)
            c_solve(c_lo)
            e_solve(c_hi)
            d_solve(c_lo)
            f_close(c_hi)
            e_solve(c_lo)
            f_close(c_lo)
            return carry

        lax.fori_loop(0, n_chunk // 2, pair, 0)

    tok = lambda w: pl.BlockSpec((tb, w), lambda i: (nb - 1 - i, 0))
    return pl.pallas_call(
        body, name="delta_bwd", grid=(nb,),
        in_specs=[tok(DN_WIDTH), tok(DN_WIDTH), tok(DN_WIDTH), tok(LANES),
                  pl.BlockSpec((n_chunk, HEADS, HEAD_DIM, HEAD_DIM), lambda i: (nb - 1 - i, 0, 0, 0)), tok(DN_WIDTH)],
        out_specs=[tok(DN_WIDTH), tok(DN_WIDTH), tok(DN_WIDTH), tok(LANES)],
        out_shape=[jax.ShapeDtypeStruct((T, DN_WIDTH), F32)] * 3 + [jax.ShapeDtypeStruct((T, LANES), F32)],
        scratch_shapes=[pltpu.VMEM((HEADS, HEAD_DIM, HEAD_DIM), F32)],
        compiler_params=_params(("arbitrary",)),
    )(q, k, v, bg, states, do)


def _dn_prep_bwd(dq, dk, dv, dbg, qkv, cw, bd, al_row, dt_row):
    T = qkv.shape[0]
    tb = 256

    def body(dq_ref, dk_ref, dv_ref, dbg_ref, pre_ref, halo_ref, cw_ref, bd_ref, al_ref, dt_ref,
             dc_ref, dbd_ref, dcw_ref, dal_ref, ddt_ref):
        @pl.when(pl.program_id(0) == 0)
        def _():
            dcw_ref[...] = jnp.zeros_like(dcw_ref)
            dal_ref[...] = jnp.zeros_like(dal_ref)
            ddt_ref[...] = jnp.zeros_like(ddt_ref)

        halo = jnp.where(pl.program_id(0) > 0, halo_ref[...], 0.0)
        xc, c, sg, a = _dn_act(pre_ref[...], halo, cw_ref[...], tb)
        dsilu = sg * (1.0 + c * (1.0 - sg))
        for hh in range(HEADS):
            sl = slice(HEAD_DIM * hh, HEAD_DIM * (hh + 1))
            for base, g_ref, scale in ((0, dq_ref, Q_SCALE), (DN_WIDTH, dk_ref, 1.0)):
                sa = slice(base + HEAD_DIM * hh, base + HEAD_DIM * (hh + 1))
                raw = a[:, sa]
                r = lax.rsqrt(jnp.sum(raw * raw, axis=-1, keepdims=True) + EPS)
                nrm = raw * r
                gn_ = g_ref[:, sl] * scale
                dc_ref[:, sa] = r * (gn_ - nrm * jnp.sum(gn_ * nrm, axis=-1, keepdims=True)) * dsilu[:, sa]
        dc_ref[:, 2 * DN_WIDTH:] = dv_ref[...] * dsilu[:, 2 * DN_WIDTH:]
        dc = dc_ref[...]
        for j in range(4):
            dcw_ref[j:j + 1, :] += jnp.sum(dc * _rows_from(xc, 5 + j, tb), axis=0, keepdims=True)
        bdv = bd_ref[...]
        lane = lax.broadcasted_iota(jnp.int32, bdv.shape, 1)
        is_b = lane < HEADS
        dbg_in = dbg_ref[...]
        dbgv = jnp.where(is_b, dbg_in, _mm32(_chunk_cumsum_matrix(tb), dbg_in, TN))
        is_g = jnp.logical_and(lane >= HEADS, lane < 2 * HEADS)
        beta = _sigmoid(bdv)
        neg_a = -jnp.exp(al_ref[...])
        pre_sp = bdv + dt_ref[...]
        g = neg_a * _softplus(pre_sp)
        da_in = dbgv * neg_a * _sigmoid(pre_sp)
        dbd_ref[...] = jnp.where(is_b, dbgv * beta * (1.0 - beta), jnp.where(is_g, da_in, 0.0)).astype(BF16)
        _row_acc(dal_ref, jnp.where(is_g, dbgv * g, 0.0))
        _row_acc(ddt_ref, jnp.where(is_g, da_in, 0.0))

    tok = lambda w: pl.BlockSpec((tb, w), lambda i: (i, 0))
    full = lambda t: pl.BlockSpec(t.shape, lambda i: (0, 0))
    acc = lambda w: pl.BlockSpec((8, w), lambda i: (0, 0))
    return pl.pallas_call(
        body, name="dn_prep_bwd", grid=(T // tb,),
        in_specs=[tok(DN_WIDTH), tok(DN_WIDTH), tok(DN_WIDTH), tok(LANES),
                  tok(QKV), pl.BlockSpec((8, QKV), _before_halo(tb)), full(cw), tok(LANES), full(al_row), full(dt_row)],
        out_specs=[tok(QKV), tok(LANES), acc(QKV), acc(LANES), acc(LANES)],
        out_shape=[jax.ShapeDtypeStruct((T, QKV), F32), jax.ShapeDtypeStruct((T, LANES), BF16),
                   jax.ShapeDtypeStruct((8, QKV), F32), jax.ShapeDtypeStruct((8, LANES), F32),
                   jax.ShapeDtypeStruct((8, LANES), F32)],
        compiler_params=_params(("arbitrary",)),
    )(dq, dk, dv, dbg, qkv, qkv, cw, bd, al_row, dt_row)


def _dn_conv_bwd(dc, cw):
    T = dc.shape[0]
    tb = 512

    def body(dc_ref, halo_ref, w_ref, out_ref):
        last = pl.program_id(0) == pl.num_programs(0) - 1
        halo = jnp.where(last, 0.0, halo_ref[...])
        xc = jnp.concatenate([dc_ref[...], halo], axis=0)
        w = w_ref[...]
        acc = w[3:4, :] * xc[0:tb, :]
        for j in range(3):
            acc = acc + w[j:j + 1, :] * _rows_from(xc, 3 - j, tb)
        out_ref[...] = acc.astype(BF16)

    tok = pl.BlockSpec((tb, QKV), lambda i: (i, 0))
    return pl.pallas_call(
        body, name="dn_conv_bwd", grid=(T // tb,),
        in_specs=[tok, pl.BlockSpec((8, QKV), _after_halo(tb, T)), pl.BlockSpec(cw.shape, lambda i: (0, 0))],
        out_specs=tok,
        out_shape=jax.ShapeDtypeStruct((T, QKV), BF16),
        compiler_params=_params(("parallel",)),
    )(dc, dc, cw)


def _in_proj_bwd(dqkv, dz, dsc, dbd, dx1, x, g1, wa, wbd):
    T = x.shape[0]
    tb = 256

    def body(dqkv_ref, dz_ref, dsc_ref, dbd_ref, dx1_ref, x_ref, g_ref, wa_ref, wbd_ref, dx_ref, dxb_ref, dg_ref):
        @pl.when(pl.program_id(0) == 0)
        def _():
            dg_ref[...] = jnp.zeros_like(dg_ref)

        dh = (_mm(dqkv_ref[...], wa_ref[:, :QKV], NT) + _mm(dz_ref[...], wa_ref[:, QKV:QKV + DN_WIDTH], NT)
              + _mm(dsc_ref[...], wa_ref[:, QKV + DN_WIDTH:], NT) + _mm(dbd_ref[...], wbd_ref[...], NT))
        xv = x_ref[...]
        r = lax.rsqrt(jnp.mean(xv * xv, axis=-1, keepdims=True) + EPS)
        xh = xv * r
        _row_acc(dg_ref, dh * xh)
        dx = dx1_ref[...] + _rms_bwd(dh, xh, r, g_ref[...])
        dx_ref[...] = dx
        dxb_ref[...] = dx.astype(BF16)

    tok = lambda w: pl.BlockSpec((tb, w), lambda i: (i, 0))
    full = lambda t: pl.BlockSpec(t.shape, lambda i: (0, 0))
    return pl.pallas_call(
        body, name="in_proj_bwd", grid=(T // tb,),
        in_specs=[tok(QKV), tok(DN_WIDTH), tok(3 * SC_WIDTH), tok(LANES), tok(D_MODEL), tok(D_MODEL),
                  full(g1), full(wa), full(wbd)],
        out_specs=[tok(D_MODEL), tok(D_MODEL), pl.BlockSpec((8, D_MODEL), lambda i: (0, 0))],
        out_shape=[jax.ShapeDtypeStruct((T, D_MODEL), F32), jax.ShapeDtypeStruct((T, D_MODEL), BF16),
                   jax.ShapeDtypeStruct((8, D_MODEL), F32)],
        compiler_params=_params(("arbitrary",)),
    )(dqkv, dz, dsc, dbd, dx1, x, g1, wa, wbd)


def _pad_rows(a, rows=8):
    return jnp.pad(a, ((0, rows - a.shape[0]), (0, 0)))


def _gate_rows(a_log, dt_bias):
    put = lambda t: jnp.pad(t.reshape(1, HEADS), ((0, 0), (HEADS, LANES - 2 * HEADS)))
    return put(a_log), put(dt_bias)


def _split_w_in(w_in):
    o = QKV + DN_WIDTH
    wa = jnp.concatenate([w_in[:, :o], w_in[:, o + 2 * HEADS:]], axis=1)
    wbd = jnp.pad(w_in[:, o:o + 2 * HEADS], ((0, 0), (0, LANES - 2 * HEADS)))
    return wa, wbd


def _mixer_fwd(x, p):
    qkv, z, sc_in, bd, ht = _in_proj(x, p["g1"], p["wa"], p["wbd"])
    q, k, v, bg = _dn_prep(qkv, p["cw"], bd, p["al"], p["dt"])
    o, states = _delta_fwd(q, k, v, bg)
    x1, mt = _mix_out(o, z, sc_in, x, p["land_a"], p["gn"], p["scw"], p["gs"])
    return x1, dict(x=x, qkv=qkv, z=z, sc_in=sc_in, bd=bd, ht=ht, q=q, k=k, v=v, bg=bg, o=o, states=states, mt=mt)


def _ffn_fwd(x1, p, land_b):
    x2, a, b, h2 = _ffn(x1, p["g2"], land_b)
    return x2, dict(x1=x1, a=a, b=b, h2=h2)


def _ffn_back(dx2, dx2_bf16, s, p, land_b):
    dx1, dx1_bf16, da_t, db_t, act_t, dg2 = _ffn_bwd(dx2, s["x1"], s["a"], s["b"], p["g2"], land_b)
    parts = lax.empty((N_CHIPS, B_ROWS, D_MODEL), BF16)
    parts = _wgrad_share(act_t, dx2_bf16, parts, 2 * FF_SHARD, "wgrad_down")
    parts = _wgrad_share(da_t, s["h2"], parts, 0, "wgrad_gate")
    parts = _wgrad_share(db_t, s["h2"], parts, FF_SHARD, "wgrad_up")
    return dx1, dx1_bf16, parts, dg2[0]


def _mixer_bwd(dx1, dx1_bf16, s, p):
    do, dz, dgb, dcv, dgn, dgs, dscw = _mix_out_bwd(dx1, s["o"], s["z"], s["sc_in"], p["land_a"], p["gn"], p["scw"], p["gs"])
    dsc = _sc_conv_bwd(dcv, dgb, s["sc_in"], p["scw"])
    dq, dk, dv, dbg = _delta_bwd(s["q"], s["k"], s["v"], s["bg"], s["states"], do)
    dc, dbd, dcw, dal, ddt = _dn_prep_bwd(dq, dk, dv, dbg, s["qkv"], p["cw"], s["bd"], p["al"], p["dt"])
    dqkv = _dn_conv_bwd(dc, p["cw"])
    dx, dx_bf16, dg1 = _in_proj_bwd(dqkv, dz, dsc, dbd, dx1, s["x"], p["g1"], p["wa"], p["wbd"])
    g_w_in = jnp.concatenate([
        _wgrad(s["ht"], dqkv, 512, 768, "wgrad_qkv"), _wgrad(s["ht"], dz, 512, 512, "wgrad_z"),
        _wgrad(s["ht"], dbd, 512, LANES, "wgrad_bd")[:, :2 * HEADS], _wgrad(s["ht"], dsc, 512, 768, "wgrad_sc")], axis=1)
    cols = jnp.moveaxis(g_w_in.reshape(D_MODEL, N_CHIPS, IN_SHARD), 1, 0)
    parts = jnp.pad(cols, ((0, 0), (0, OUT_SHARD), (0, D_MODEL - IN_SHARD))).astype(BF16)
    parts = _wgrad_share(s["mt"], dx1_bf16, parts, A_OUT_AT, "wgrad_out")
    g = dict(g1=dg1[0], gn=dgn[0], gs=dgs[0], scw=dscw[:3], cw=dcw[:4], al=dal[0, HEADS:2 * HEADS], dt=ddt[0, HEADS:2 * HEADS])
    return dx, dx_bf16, parts, g


def _place():
    return lax.axis_index("x"), lax.axis_index("y"), lax.axis_index("c")


def _other_chips(x, y):
    return [(1 - x, y), (x, 1 - y), (1 - x, 1 - y)]


_HBM = pl.BlockSpec(memory_space=pltpu.HBM)


def _chip_exchange(arrs, name, gather):
    n = len(arrs)

    def body(*refs):
        ins, outs = refs[:n], refs[n:2 * n]
        send_sems, recv_sems, local_sems = refs[2 * n:]
        x, y, c = _place()
        me = 2 * x + y
        others = _other_chips(x, y)

        def remote(k, j, landing):
            px, py = others[j]
            src = ins[k] if gather else ins[k].at[2 * px + py]
            return pltpu.make_async_remote_copy(src_ref=src, dst_ref=outs[k].at[landing], send_sem=send_sems.at[k, j],
                                                recv_sem=recv_sems.at[k, j], device_id=(px, py, c), device_id_type=MESH)

        local = [pltpu.make_async_copy(ins[k] if gather else ins[k].at[me], outs[k].at[me], local_sems.at[k])
                 for k in range(n)]
        sends = [remote(k, j, me) for k in range(n) for j in range(3)]
        for cp in local + sends:
            cp.start()
        for k in range(n):
            for j, (px, py) in enumerate(others):
                remote(k, j, 2 * px + py).wait_recv()
        for cp in sends:
            cp.wait_send()
        for cp in local:
            cp.wait()

    shapes = [jax.ShapeDtypeStruct(((N_CHIPS,) + a.shape) if gather else a.shape, a.dtype) for a in arrs]
    return pl.pallas_call(
        body, name=name, in_specs=[_HBM] * n, out_specs=[_HBM] * n, out_shape=shapes,
        scratch_shapes=[pltpu.SemaphoreType.DMA((n, 3)), pltpu.SemaphoreType.DMA((n, 3)), pltpu.SemaphoreType.DMA((n,))],
    )(*arrs)


_SEM = pl.BlockSpec(memory_space=pltpu.SEMAPHORE)
_ANY = pl.BlockSpec(memory_space=pl.ANY)
_EFFECT = pltpu.SideEffectType.DATAFLOW_SIDE_EFFECTING


def _split_copies(src_ref, land_ref, send_sems, recv_sems, gather, sending):
    x, y, c = _place()
    me = 2 * x + y
    copies = []
    for j, (px, py) in enumerate(_other_chips(x, y)):
        peer = 2 * px + py
        copies.append(pltpu.make_async_remote_copy(
            src_ref=src_ref if gather else src_ref.at[peer], dst_ref=land_ref.at[me if sending else peer],
            send_sem=send_sems.at[j], recv_sem=recv_sems.at[j], device_id=(px, py, c), device_id_type=MESH))
    return copies


def _own_slot(share):
    chip = 2 * lax.axis_index("x") + lax.axis_index("y")
    return lax.dynamic_update_slice(lax.empty((N_CHIPS,) + share.shape, share.dtype), share[None], (chip, 0, 0))


def _exchange_start(src, land, after, name, gather):
    def body(src_ref, land_ref, after_ref, send_sems, recv_sems, src_thru, land_thru, token):
        for cp in _split_copies(src_ref, land_ref, send_sems, recv_sems, gather, sending=True):
            cp.start()
        token[...] = jnp.zeros_like(token)

    hbm = lambda t: pltpu.with_memory_space_constraint(t, pltpu.HBM)
    return pl.pallas_call(
        body, name=name,
        out_shape=(pltpu.SemaphoreType.DMA((3,)), pltpu.SemaphoreType.DMA((3,)), pltpu.HBM(src.shape, src.dtype),
                   pltpu.HBM(land.shape, land.dtype), jax.ShapeDtypeStruct((8, LANES), F32)),
        in_specs=(_HBM, _HBM, _ANY), out_specs=(_SEM, _SEM, _HBM, _HBM, pl.BlockSpec(memory_space=pltpu.VMEM)),
        input_output_aliases={0: 2, 1: 3},
        compiler_params=pltpu.CompilerParams(has_side_effects=_EFFECT),
    )(hbm(src), hbm(land), after)


def _exchange_wait(started, after, name, gather):
    send_sems, recv_sems, src_thru, land_thru, _ = started

    def body(src_ref, land_ref, send_sems, recv_sems, after_ref, src_dead, got_ref):
        for cp in _split_copies(src_ref, land_ref, send_sems, recv_sems, gather, sending=False):
            cp.wait_send()
            cp.wait_recv()

    return pl.pallas_call(
        body, name=name,
        out_shape=(pltpu.HBM(src_thru.shape, src_thru.dtype), pltpu.HBM(land_thru.shape, land_thru.dtype)),
        in_specs=(_HBM, _HBM, _SEM, _SEM, _ANY), out_specs=(_HBM, _HBM), input_output_aliases={0: 0, 1: 1},
        compiler_params=pltpu.CompilerParams(has_side_effects=_EFFECT),
    )(src_thru, land_thru, send_sems, recv_sems, after)[1]


def _swap_sibling(arrs):
    n = len(arrs)

    def body(*refs):
        ins, outs = refs[:n], refs[n:2 * n]
        send_sems, recv_sems = refs[2 * n:]
        x, y, c = _place()
        copies = [pltpu.make_async_remote_copy(src_ref=ins[k], dst_ref=outs[k], send_sem=send_sems.at[k],
                                               recv_sem=recv_sems.at[k], device_id=(x, y, 1 - c), device_id_type=MESH)
                  for k in range(n)]
        for cp in copies:
            cp.start()
        for cp in copies:
            cp.wait()

    return pl.pallas_call(
        body, name="swap_sibling", in_specs=[_HBM] * n, out_specs=[_HBM] * n,
        out_shape=[jax.ShapeDtypeStruct(a.shape, a.dtype) for a in arrs],
        scratch_shapes=[pltpu.SemaphoreType.DMA((n,)), pltpu.SemaphoreType.DMA((n,))],
    )(*arrs)


def _all_reduce_small(v):
    rows = v.shape[0]
    flips = [(a, b, cc) for a in (0, 1) for b in (0, 1) for cc in (0, 1)][1:]

    def body(v_ref, out_ref, buf_ref, send_sems, recv_sems):
        x, y, c = _place()
        me = 4 * x + 2 * y + c
        peers = [((1 - x) if a else x, (1 - y) if b else y, (1 - c) if cc else c) for a, b, cc in flips]

        def copy(j, landing):
            return pltpu.make_async_remote_copy(src_ref=v_ref, dst_ref=buf_ref.at[landing], send_sem=send_sems.at[j],
                                                recv_sem=recv_sems.at[j], device_id=peers[j], device_id_type=MESH)

        sends = [copy(j, me) for j in range(N_DEV - 1)]
        for cp in sends:
            cp.start()
        buf_ref[me] = v_ref[...]
        for j, (px, py, pc) in enumerate(peers):
            copy(j, 4 * px + 2 * py + pc).wait_recv()
        for cp in sends:
            cp.wait_send()
        acc = buf_ref[0]
        for d in range(1, N_DEV):
            acc = acc + buf_ref[d]
        out_ref[...] = acc

    vmem = pl.BlockSpec(memory_space=pltpu.VMEM)
    return pl.pallas_call(
        body, name="all_reduce_small", in_specs=[vmem], out_specs=vmem,
        out_shape=jax.ShapeDtypeStruct(v.shape, F32),
        scratch_shapes=[pltpu.VMEM((N_DEV, rows, LANES), F32), pltpu.SemaphoreType.DMA((N_DEV - 1,)),
                        pltpu.SemaphoreType.DMA((N_DEV - 1,))],
    )(v)


def _row_block(*sizes):
    return next(t for t in (256, 192, 128, 64) if all(s % t == 0 for s in sizes))


def _sum_chips(parts, name):
    _, rows, cols = parts[0].shape
    n = len(parts)
    tr = _row_block(rows)

    def body(*refs):
        o_ref = refs[n]
        for l in range(n):
            @pl.when(pl.program_id(0) == l)
            def _(p_ref=refs[l]):
                acc = p_ref[0].astype(F32)
                for s in range(1, N_CHIPS):
                    acc = acc + p_ref[s].astype(F32)
                o_ref[0] = acc

    return pl.pallas_call(
        body, name=name, grid=(n, rows // tr),
        in_specs=[pl.BlockSpec((N_CHIPS, tr, cols), lambda l, i, k=k: (0, jnp.where(l == k, i, 0), 0)) for k in range(n)],
        out_specs=pl.BlockSpec((1, tr, cols), lambda l, i: (l, i, 0)),
        out_shape=jax.ShapeDtypeStruct((n, rows, cols), F32),
        compiler_params=_params(("arbitrary", "arbitrary")),
    )(*parts)


def _adam_update(w, m, v, g):
    c1 = 1.0 - ADAM_B1 ** ADAM_STEP
    c2 = 1.0 - ADAM_B2 ** ADAM_STEP
    m_new = ADAM_B1 * m + (1.0 - ADAM_B1) * g
    v_new = ADAM_B2 * v + (1.0 - ADAM_B2) * (g * g)
    return -ADAM_LR * ((m_new / c1) / (jnp.sqrt(v_new / c2) + ADAM_EPS) + ADAM_WD * w), m_new, v_new


def _adamw_rows(w, m, v, g_parts, first, name):
    n_layers, rows, cols = w.shape
    tr = _row_block(rows, first)
    n = len(g_parts)

    def body(*refs):
        w_ref, m_ref, v_ref = refs[:3]
        g_out, d_out, m_out, v_out = refs[3 + n:]
        g = refs[3][...]
        for r in refs[4:3 + n]:
            g = g + r[...]
        g = g[:, :, :cols]
        d_out[...], m_out[...], v_out[...] = _adam_update(w_ref[...], m_ref[...], v_ref[...], g)
        g_out[...] = g

    blk = pl.BlockSpec((1, tr, cols), lambda l, i: (l, i, 0))
    g_blk = pl.BlockSpec((1, tr, g_parts[0].shape[2]), lambda l, i: (l, first // tr + i, 0))
    return pl.pallas_call(
        body, name=name, grid=(n_layers, rows // tr),
        in_specs=[blk] * 3 + [g_blk] * n, out_specs=[blk] * 4,
        out_shape=[jax.ShapeDtypeStruct(w.shape, F32)] * 4,
        compiler_params=_params(("parallel", "parallel")),
    )(w, m, v, *g_parts)


def _adamw(w, m, v, g_parts, name):
    rows, cols = w.shape
    tr = min(rows, 256)
    n = len(g_parts)

    def body(*refs):
        w_ref, m_ref, v_ref = refs[:3]
        g_refs = refs[3:3 + n]
        g_out, d_out, m_out, v_out = refs[3 + n:]
        g = g_refs[0][...]
        for r in g_refs[1:]:
            g = g + r[...]
        d_out[...], m_out[...], v_out[...] = _adam_update(w_ref[...], m_ref[...], v_ref[...], g)
        g_out[...] = g

    blk = pl.BlockSpec((tr, cols), lambda i: (i, 0))
    return pl.pallas_call(
        body, name=name, grid=(rows // tr,),
        in_specs=[blk] * (3 + n), out_specs=[blk] * 4,
        out_shape=[jax.ShapeDtypeStruct((rows, cols), F32)] * 4,
        compiler_params=_params(("parallel",)),
    )(w, m, v, *g_parts)


def _pack(parts, rows, fill=0.0):
    flat = jnp.concatenate([p.reshape(-1) for p in parts])
    return jnp.pad(flat, (0, rows * LANES - flat.shape[0]), constant_values=fill).reshape(rows, LANES)


def _unpack(packed, shapes):
    flat = packed.reshape(-1)
    out, at = [], 0
    for shp in shapes:
        size = 1
        for s in shp:
            size *= s
        out.append(flat[at:at + size].reshape(shp))
        at += size
    return out


def _packed_rows(shapes):
    total = 0
    for shp in shapes:
        size = 1
        for s in shp:
            size *= s
        total += size
    return -(-total // (8 * LANES)) * 8


def _cols_full(g, l):
    t = g[:, l]
    return jnp.moveaxis(t, 0, 1).reshape(t.shape[1], N_CHIPS * t.shape[2])


def _pad_cols(t):
    return jnp.pad(t, ((0, 0),) * (t.ndim - 1) + ((0, D_MODEL - t.shape[-1]),))


def _w_in_of(land_a):
    return jnp.moveaxis(land_a[:, :D_MODEL, :IN_SHARD], 0, 1).reshape(D_MODEL, W_IN_COLS)


def kernel(x, norm1_g, w_in, dn_conv_w, dn_a_log, dn_dt_bias, dn_norm_g, sc_conv_w, sc_norm_g, w_out, norm2_g, ffn_w_gate, ffn_w_up, ffn_w_down, final_norm_g, loss_target, m_norm1_g, m_w_in, m_dn_conv_w, m_dn_a_log, m_dn_dt_bias, m_dn_norm_g, m_sc_conv_w, m_sc_norm_g, m_w_out, m_norm2_g, m_ffn_w_gate, m_ffn_w_up, m_ffn_w_down, m_final_norm_g, v_norm1_g, v_w_in, v_dn_conv_w, v_dn_a_log, v_dn_dt_bias, v_dn_norm_g, v_sc_conv_w, v_sc_norm_g, v_w_out, v_norm2_g, v_ffn_w_gate, v_ffn_w_up, v_ffn_w_down, v_final_norm_g):
    chip = 2 * lax.axis_index("x") + lax.axis_index("y")

    g_cw, g_scw = _chip_exchange([dn_conv_w, sc_conv_w], "gather_conv", gather=True)

    t_last = lambda t: jnp.swapaxes(t, -1, -2)
    gate_t, up_t = t_last(ffn_w_gate), t_last(ffn_w_up)
    share_a = [jnp.concatenate([_pad_cols(w_in[l]), w_out[l]], axis=0).astype(BF16) for l in range(DEPTH)]
    share_b = [jnp.concatenate([gate_t[l], up_t[l], ffn_w_down[l]], axis=0).astype(BF16) for l in range(DEPTH)]
    zero_token = jnp.zeros((8, LANES), F32)

    def gather_start(l, after):
        a = _exchange_start(share_a[l], _own_slot(share_a[l]), after, "gather_a_start_%d" % l, gather=True)
        b = _exchange_start(share_b[l], _own_slot(share_b[l]), a[4], "gather_b_start_%d" % l, gather=True)
        return a, b

    ga, gb = gather_start(0, g_cw)
    land_a = _exchange_wait(ga, gb[4], "gather_a_wait_0", gather=True)
    act = x[0]
    layers, saved_m, saved_f, lands_b = [], [], [], []
    for l in range(DEPTH):
        hold = 0.0
        if l + 1 < DEPTH:
            ga, gb_next = gather_start(l + 1, land_a)
            hold = gb_next[4][0:1, 0:1]
        wa, wbd = _split_w_in(_w_in_of(land_a))
        al, dt = _gate_rows(dn_a_log[l], dn_dt_bias[l])
        layers.append(dict(
            g1=norm1_g[l][None] + hold, wa=wa, wbd=wbd, cw=_pad_rows(_cols_full(g_cw, l)), al=al, dt=dt,
            gn=dn_norm_g[l][None], scw=_pad_rows(_cols_full(g_scw, l)), gs=sc_norm_g[l][None],
            land_a=land_a, g2=norm2_g[l][None]))
        x1, s = _mixer_fwd(act, layers[l])
        saved_m.append(s)
        lands_b.append(_exchange_wait(gb, x1, "gather_b_wait_%d" % l, gather=True))
        act, s = _ffn_fwd(x1, layers[l], lands_b[l])
        saved_f.append(s)
        if l + 1 < DEPTH:
            land_a = _exchange_wait(ga, act, "gather_a_wait_%d" % (l + 1), gather=True)
            gb = gb_next

    dact, dact_bf16, loss_part, d_final = _loss_head(act, final_norm_g[None], loss_target[0])
    grads, reduce_a, reduce_b = [None] * DEPTH, [None] * DEPTH, [None] * DEPTH
    hold = 0.0
    for l in reversed(range(DEPTH)):
        p = layers[l]
        dx1, dx1_bf16, parts, dg2 = _ffn_back(dact, dact_bf16, saved_f[l], dict(p, g2=p["g2"] + hold), lands_b[l])
        reduce_b[l] = _exchange_start(parts, parts, zero_token, "reduce_b_start_%d" % l, gather=False)
        dact, dact_bf16, parts, gm = _mixer_bwd(dx1, dx1_bf16, saved_m[l], dict(p, gn=p["gn"] + reduce_b[l][4][0:1, 0:1]))
        reduce_a[l] = _exchange_start(parts, parts, zero_token, "reduce_a_start_%d" % l, gather=False)
        hold = reduce_a[l][4][0:1, 0:1]
        grads[l] = dict(gm, g2=dg2)
    loss = lax.psum(loss_part[0, 0], ("x", "y", "c"))
    stack = lambda key: jnp.stack([grads[l][key] for l in range(DEPTH)])

    got_b = [_exchange_wait(reduce_b[l], dact, "reduce_b_wait_%d" % l, gather=False) for l in reversed(range(DEPTH))][::-1]
    got_a = [_exchange_wait(reduce_a[l], dact, "reduce_a_wait_%d" % l, gather=False) for l in reversed(range(DEPTH))][::-1]
    sum_a, sum_b = _sum_chips(got_a, "sum_chips_a"), _sum_chips(got_b, "sum_chips_b")
    other_a, other_b = _swap_sibling([sum_a, sum_b])
    big = dict(
        w_in=_adamw_rows(w_in, m_w_in, v_w_in, [sum_a, other_a], 0, "adamw_w_in"),
        w_out=_adamw_rows(w_out, m_w_out, v_w_out, [sum_a, other_a], A_OUT_AT, "adamw_w_out"),
        ffn_w_gate=[t_last(o) for o in _adamw_rows(gate_t, t_last(m_ffn_w_gate), t_last(v_ffn_w_gate),
                                                   [sum_b, other_b], 0, "adamw_gate")],
        ffn_w_up=[t_last(o) for o in _adamw_rows(up_t, t_last(m_ffn_w_up), t_last(v_ffn_w_up),
                                                 [sum_b, other_b], FF_SHARD, "adamw_up")],
        ffn_w_down=_adamw_rows(ffn_w_down, m_ffn_w_down, v_ffn_w_down, [sum_b, other_b], 2 * FF_SHARD, "adamw_down"))

    full_shapes = [(DEPTH, D_MODEL), (DEPTH, D_MODEL), (DEPTH, HEAD_DIM), (DEPTH, SC_WIDTH), (DEPTH, HEADS),
                   (DEPTH, HEADS), (D_MODEL,), (DEPTH, 4, QKV), (DEPTH, 3, SC_WIDTH)]
    small_keys = ("g1", "g2", "gn", "gs", "al", "dt")
    packed = _pack([stack(k) for k in small_keys] + [d_final[0], stack("cw"), stack("scw")], _packed_rows(full_shapes))
    sg = _unpack(_all_reduce_small(packed), full_shapes)
    sg[7] = lax.dynamic_slice_in_dim(sg[7], chip * (QKV // N_CHIPS), QKV // N_CHIPS, axis=2)
    sg[8] = lax.dynamic_slice_in_dim(sg[8], chip * (SC_WIDTH // N_CHIPS), SC_WIDTH // N_CHIPS, axis=2)
    small_names = ("norm1_g", "norm2_g", "dn_norm_g", "sc_norm_g", "dn_a_log", "dn_dt_bias", "final_norm_g",
                   "dn_conv_w", "sc_conv_w")
    sw = (norm1_g, norm2_g, dn_norm_g, sc_norm_g, dn_a_log, dn_dt_bias, final_norm_g, dn_conv_w, sc_conv_w)
    sm = (m_norm1_g, m_norm2_g, m_dn_norm_g, m_sc_norm_g, m_dn_a_log, m_dn_dt_bias, m_final_norm_g, m_dn_conv_w, m_sc_conv_w)
    sv = (v_norm1_g, v_norm2_g, v_dn_norm_g, v_sc_norm_g, v_dn_a_log, v_dn_dt_bias, v_final_norm_g, v_dn_conv_w, v_sc_conv_w)
    shard_shapes = [t.shape for t in sw]
    rows = _packed_rows(shard_shapes)
    outs = _adamw(_pack(sw, rows), _pack(sm, rows), _pack(sv, rows, fill=1.0), [_pack(sg, rows)], "adamw_small")
    small = {name: [] for name in small_names}
    for o in outs:
        for name, t in zip(small_names, _unpack(o, shard_shapes)):
            small[name].append(t)

    order = ("norm1_g", "w_in", "dn_conv_w", "dn_a_log", "dn_dt_bias", "dn_norm_g", "sc_conv_w", "sc_norm_g", "w_out",
             "norm2_g", "ffn_w_gate", "ffn_w_up", "ffn_w_down", "final_norm_g")
    result = {**big, **small}
    return (loss, dact[None], *[result[n][0] for n in order], *[result[n][1] for n in order],
            *[result[n][2] for n in order], *[result[n][3] for n in order])
```

```python
import jax
import jax.numpy as jnp
from jax import lax
from jax.experimental import pallas as pl
from jax.experimental.pallas import tpu as pltpu

F32 = jnp.float32
BF16 = jnp.bfloat16
MESH = pl.DeviceIdType.MESH

D_MODEL = 1024
DEPTH = 4
HEADS = 4
HEAD_DIM = 128
DN_WIDTH = HEADS * HEAD_DIM
SC_WIDTH = 512
SC_GROUPS = 4
D_FF = 2816
CHUNK = 64
QKV = 3 * DN_WIDTH
W_IN_COLS = 4 * DN_WIDTH + 2 * HEADS + 3 * SC_WIDTH
WA_COLS = QKV + DN_WIDTH + 3 * SC_WIDTH
LANES = 128
EPS = 1e-6
Q_SCALE = HEAD_DIM ** -0.5
N_CHIPS = 4
N_DEV = 8
IN_SHARD = W_IN_COLS // N_CHIPS
OUT_SHARD = D_MODEL // N_CHIPS
FF_SHARD = D_FF // N_CHIPS
A_OUT_AT = D_MODEL
A_ROWS = D_MODEL + OUT_SHARD
B_ROWS = 3 * FF_SHARD

ADAM_LR = 0.001
ADAM_B1 = 0.9
ADAM_B2 = 0.999
ADAM_EPS = 1e-08
ADAM_WD = 0.01
ADAM_STEP = 10

VMEM_LIMIT = 56 * 1024 * 1024

NN = (((1,), (0,)), ((), ()))
NT = (((1,), (1,)), ((), ()))
TN = (((0,), (0,)), ((), ()))


def _mm(a, b, dims=NN):
    return lax.dot_general(a.astype(BF16), b.astype(BF16), dims, preferred_element_type=F32)


def _mm32(a, b, dims=NN):
    return lax.dot_general(a, b, dims, preferred_element_type=F32, precision=lax.Precision.HIGHEST)


def _params(sem, vmem=VMEM_LIMIT):
    return pltpu.CompilerParams(dimension_semantics=sem, vmem_limit_bytes=vmem)


def _sigmoid(x):
    return 0.5 * jnp.tanh(0.5 * x) + 0.5


def _softplus(x):
    return jnp.maximum(x, 0.0) + jnp.log1p(jnp.exp(-jnp.abs(x)))


def _row_acc(acc_ref, val):
    acc_ref[0:1, :] += jnp.sum(val, axis=0, keepdims=True)


def _rms_bwd(dh, xh, r, gain):
    dxh = dh * gain
    return r * (dxh - xh * jnp.mean(dxh * xh, axis=-1, keepdims=True))


def _before_halo(tb):
    return lambda i: (jnp.maximum(i * (tb // 8) - 1, 0), 0)


def _after_halo(tb, n_rows):
    last = n_rows // 8 - 1
    return lambda i: (jnp.minimum((i + 1) * (tb // 8), last), 0)


def _rows_from(xc, offset, tb):
    part = offset % 8
    if part:
        xc = pltpu.roll(xc, xc.shape[0] - part, 0)
    return xc[offset - part:offset - part + tb, :]


def _taps(xc, w, n_taps, tb, first):
    out = w[0:1, :] * _rows_from(xc, first, tb)
    for j in range(1, n_taps):
        out = out + w[j:j + 1, :] * _rows_from(xc, first + j, tb)
    return out


P_SC = QKV
P_Z = P_SC + 3 * SC_WIDTH
P_BD = P_Z + DN_WIDTH
P_COLS = P_BD + LANES


def _in_proj(x, g1, wp):
    T = x.shape[0]
    tb = 256

    def body(x_ref, g_ref, wp_ref, qkv_ref, z_ref, sc_ref, bd_ref, ht_ref):
        xv = x_ref[...]
        r = lax.rsqrt(jnp.mean(xv * xv, axis=-1, keepdims=True) + EPS)
        h = (xv * r * g_ref[...]).astype(BF16)
        p = jnp.dot(h, wp_ref[...], preferred_element_type=F32)
        qkv_ref[...] = p[:, :P_SC]
        sc_ref[...] = p[:, P_SC:P_Z]
        z_ref[...] = p[:, P_Z:P_BD]
        bd_ref[...] = p[:, P_BD:]
        ht_ref[...] = h.T

    tok = lambda w: pl.BlockSpec((tb, w), lambda i: (i, 0))
    full = lambda a: pl.BlockSpec(a.shape, lambda i: (0, 0))
    return pl.pallas_call(
        body, name="in_proj", grid=(T // tb,),
        in_specs=[tok(D_MODEL), full(g1), full(wp)],
        out_specs=[tok(QKV), tok(DN_WIDTH), tok(3 * SC_WIDTH), tok(LANES),
                   pl.BlockSpec((D_MODEL, tb), lambda i: (0, i))],
        out_shape=[jax.ShapeDtypeStruct((T, QKV), F32), jax.ShapeDtypeStruct((T, DN_WIDTH), F32),
                   jax.ShapeDtypeStruct((T, 3 * SC_WIDTH), F32), jax.ShapeDtypeStruct((T, LANES), F32),
                   jax.ShapeDtypeStruct((D_MODEL, T), BF16)],
        compiler_params=_params(("parallel",)),
    )(x, g1, wp)


def _dp_block(tb, first, width, index=lambda i: i):
    assert first % width == 0
    return pl.BlockSpec((tb, width), lambda i: (index(i), first // width))


def _dn_act(pre, halo, cw, tb):
    xc = jnp.concatenate([halo, pre], axis=0)
    c = _taps(xc, cw, 4, tb, 5)
    sg = _sigmoid(c)
    return xc, c, sg, c * sg


def _gates(bd, al_row, dt_row):
    lane = lax.broadcasted_iota(jnp.int32, bd.shape, 1)
    beta = _sigmoid(bd)
    g = -jnp.exp(al_row) * _softplus(bd + dt_row)
    return jnp.where(lane < HEADS, beta, jnp.where(lane < 2 * HEADS, g, 0.0))


def _dn_prep(qkv, cw, bd, al_row, dt_row):
    T = qkv.shape[0]
    tb = 512

    def body(pre_ref, halo_ref, cw_ref, bd_ref, al_ref, dt_ref, q_ref, k_ref, v_ref, bg_ref):
        halo = jnp.where(pl.program_id(0) > 0, halo_ref[...], 0.0)
        _, _, _, a = _dn_act(pre_ref[...], halo, cw_ref[...], tb)
        for hh in range(HEADS):
            sl = slice(HEAD_DIM * hh, HEAD_DIM * (hh + 1))
            qs = a[:, sl]
            q_ref[:, sl] = qs * (lax.rsqrt(jnp.sum(qs * qs, axis=-1, keepdims=True) + EPS) * Q_SCALE)
            ks = a[:, DN_WIDTH + HEAD_DIM * hh:DN_WIDTH + HEAD_DIM * (hh + 1)]
            k_ref[:, sl] = ks * lax.rsqrt(jnp.sum(ks * ks, axis=-1, keepdims=True) + EPS)
        v_ref[...] = a[:, 2 * DN_WIDTH:]
        gates = _gates(bd_ref[...], al_ref[...], dt_ref[...])
        lane = lax.broadcasted_iota(jnp.int32, gates.shape, 1)
        bg_ref[...] = jnp.where(lane < HEADS, gates, _mm32(_chunk_cumsum_matrix(tb), gates))

    tok = lambda w: pl.BlockSpec((tb, w), lambda i: (i, 0))
    full = lambda a: pl.BlockSpec(a.shape, lambda i: (0, 0))
    return pl.pallas_call(
        body, name="dn_prep", grid=(T // tb,),
        in_specs=[tok(QKV), pl.BlockSpec((8, QKV), _before_halo(tb)), full(cw), tok(LANES), full(al_row), full(dt_row)],
        out_specs=[tok(DN_WIDTH), tok(DN_WIDTH), tok(DN_WIDTH), tok(LANES)],
        out_shape=[jax.ShapeDtypeStruct((T, DN_WIDTH), F32)] * 3 + [jax.ShapeDtypeStruct((T, LANES), F32)],
        compiler_params=_params(("parallel",)),
    )(qkv, qkv, cw, bd, al_row, dt_row)


def _chunk_masks():
    row = lax.broadcasted_iota(jnp.int32, (CHUNK, CHUNK), 0)
    col = lax.broadcasted_iota(jnp.int32, (CHUNK, CHUNK), 1)
    return row >= col, row > col


def _chunk_cumsum_matrix(n):
    row = lax.broadcasted_iota(jnp.int32, (n, n), 0)
    col = lax.broadcasted_iota(jnp.int32, (n, n), 1)
    return jnp.logical_and(row >= col, row // CHUNK == col // CHUNK).astype(F32)


def _chunk_units(q_ref, k_ref, v_ref, bg_ref, rows):
    bgc = bg_ref[rows, :]
    bg_t = bgc.T
    qv, kv, vv = q_ref[rows, :], k_ref[rows, :], v_ref[rows, :]
    units = []
    for h in range(HEADS):
        sl = slice(HEAD_DIM * h, HEAD_DIM * (h + 1))
        units.append((qv[:, sl], kv[:, sl], vv[:, sl], bgc[:, h:h + 1], bgc[:, HEADS + h:HEADS + h + 1],
                      bg_t[HEADS + h:HEADS + h + 1, :]))
    return units


def _units_local(units, masks):
    causal, strict = masks
    pre = []
    for q, k, v, beta, gc, gr in units:
        kb = k * beta
        eg = jnp.exp(gc)
        g_last = gc[CHUNK - 1:CHUNK, :]
        ek = jnp.exp(g_last - gc)
        pre.append(dict(q=q, k=k, v=v, beta=beta, decay=jnp.exp(jnp.where(causal, gc - gr, -1e30)), kb=kb, vb=v * beta,
                        eg=eg, kbg=kb * eg, ek=ek, gl=jnp.exp(g_last), q_dec=q * eg, k_dec=k * ek))
    both = [_mm(jnp.concatenate([p["kb"], p["q"]], axis=0), p["k"], NT) for p in pre]
    for p, b in zip(pre, both):
        p["low"] = jnp.where(strict, b[:CHUNK] * p["decay"], 0.0)
        p["qk"] = jnp.where(causal, b[CHUNK:] * p["decay"], 0.0)
    xs = [-p["low"] for p in pre]
    pw = [_mm(p["low"], p["low"]) for p in pre]
    for _ in range(4):
        both = [_mm(jnp.concatenate([pp, x], axis=0), pp) for pp, x in zip(pw, xs)]
        xs = [x + pp + b[CHUNK:] for x, pp, b in zip(xs, pw, both)]
        pw = [b[:CHUNK] for b in both]
    last = [_mm(x, pp) for x, pp in zip(xs, pw)]
    xs = [x + pp + b for x, pp, b in zip(xs, pw, last)]
    uw = [_mm(x, jnp.concatenate([p["vb"], p["kbg"]], axis=1)) for x, p in zip(xs, pre)]
    for p, x, b in zip(pre, xs, uw):
        p["xm"] = x
        p["u"] = p["vb"] + b[:, :HEAD_DIM]
        p["w"] = p["kbg"] + b[:, HEAD_DIM:]
    return pre


def _delta_fwd(q, k, v, bg):
    T = q.shape[0]
    tb = 512
    n_chunk = tb // CHUNK

    def body(q_ref, k_ref, v_ref, bg_ref, o_ref, st_ref, s_ref):
        @pl.when(pl.program_id(0) == 0)
        def _():
            s_ref[...] = jnp.zeros_like(s_ref)

        masks = _chunk_masks()

        def pair(pi, carry):
            rows = [pl.ds(pl.multiple_of((2 * pi + j) * CHUNK, CHUNK), CHUNK) for j in range(2)]
            loc = _units_local(_chunk_units(q_ref, k_ref, v_ref, bg_ref, rows[0])
                               + _chunk_units(q_ref, k_ref, v_ref, bg_ref, rows[1]), masks)
            states = [s_ref[h] for h in range(HEADS)]
            for j in range(2):
                lj = loc[HEADS * j:HEADS * (j + 1)]
                ws = [_mm(jnp.concatenate([p["w"], p["q_dec"]], axis=0), s) for p, s in zip(lj, states)]
                v_new = [p["u"] - b[:CHUNK] for p, b in zip(lj, ws)]
                intra = [_mm(p["qk"], vn) for p, vn in zip(lj, v_new)]
                upd = [_mm(p["k_dec"], vn, TN) for p, vn in zip(lj, v_new)]
                o_ref[rows[j], :] = jnp.concatenate([b[CHUNK:] + a for b, a in zip(ws, intra)], axis=1)
                for h in range(HEADS):
                    st_ref[2 * pi + j, h] = states[h]
                states = [p["gl"] * s + d for p, s, d in zip(lj, states, upd)]
            for h in range(HEADS):
                s_ref[h] = states[h]
            return carry

        lax.fori_loop(0, n_chunk // 2, pair, 0)

    tok = lambda w: pl.BlockSpec((tb, w), lambda i: (i, 0))
    return pl.pallas_call(
        body, name="delta_fwd", grid=(T // tb,),
        in_specs=[tok(DN_WIDTH), tok(DN_WIDTH), tok(DN_WIDTH), tok(LANES)],
        out_specs=[tok(DN_WIDTH), pl.BlockSpec((n_chunk, HEADS, HEAD_DIM, HEAD_DIM), lambda i: (i, 0, 0, 0))],
        out_shape=[jax.ShapeDtypeStruct((T, DN_WIDTH), F32),
                   jax.ShapeDtypeStruct((T // CHUNK, HEADS, HEAD_DIM, HEAD_DIM), F32)],
        scratch_shapes=[pltpu.VMEM((HEADS, HEAD_DIM, HEAD_DIM), F32)],
        compiler_params=_params(("arbitrary",)),
    )(q, k, v, bg)


def _dn_out(o, z, gn):
    outs, ohs, rs = [], [], []
    for hh in range(HEADS):
        oh = o[:, HEAD_DIM * hh:HEAD_DIM * (hh + 1)]
        r = lax.rsqrt(jnp.mean(oh * oh, axis=-1, keepdims=True) + EPS)
        ohs.append(oh * r)
        rs.append(r)
    sz = _sigmoid(z)
    oh = jnp.concatenate(ohs, axis=1)
    gn4 = jnp.concatenate([gn] * HEADS, axis=1)
    return oh * gn4 * (z * sz), oh, rs, sz, gn4


def _sc_fwd(sc_in, halo, cw, tb):
    xc = jnp.concatenate([halo, sc_in], axis=0)
    u = xc[:, SC_WIDTH:2 * SC_WIDTH] * xc[:, 2 * SC_WIDTH:]
    cv = _taps(u, cw, 3, tb, 6)
    gate_b = sc_in[:, :SC_WIDTH]
    y = gate_b * cv
    gw = SC_WIDTH // SC_GROUPS
    yhs, rs = [], []
    for gi in range(SC_GROUPS):
        yg = y[:, gw * gi:gw * (gi + 1)]
        r = lax.rsqrt(jnp.mean(yg * yg, axis=-1, keepdims=True) + EPS)
        yhs.append(yg * r)
        rs.append(r)
    return u, cv, gate_b, jnp.concatenate(yhs, axis=1), rs


def _shard_rows(land, first, rows):
    assert first % rows == 0 and land.shape[0] == N_CHIPS
    return pl.BlockSpec((N_CHIPS, rows, land.shape[2]), lambda i: (0, first // rows, 0))


def _whole(w_ref):
    n, rows, cols = w_ref.shape
    return w_ref[...].reshape(n * rows, cols)


def _mix_out(o, z, sc_in, x, land_a, gn, scw, gs):
    T = x.shape[0]
    tb = 256

    def body(o_ref, z_ref, sc_ref, halo_ref, x_ref, w_ref, gn_ref, scw_ref, gs_ref, x1_ref, mt_ref):
        o_n = _dn_out(o_ref[...], z_ref[...], gn_ref[...])[0]
        halo = jnp.where(pl.program_id(0) > 0, halo_ref[...], 0.0)
        yh = _sc_fwd(sc_ref[...], halo, scw_ref[...], tb)[3]
        mix = jnp.concatenate([o_n, yh * gs_ref[...]], axis=1).astype(BF16)
        x1_ref[...] = x_ref[...] + jnp.dot(mix, _whole(w_ref), preferred_element_type=F32)
        mt_ref[...] = mix.T

    tok = lambda w: pl.BlockSpec((tb, w), lambda i: (i, 0))
    full = lambda a: pl.BlockSpec(a.shape, lambda i: (0, 0))
    return pl.pallas_call(
        body, name="mix_out", grid=(T // tb,),
        in_specs=[tok(DN_WIDTH), tok(DN_WIDTH), tok(3 * SC_WIDTH), pl.BlockSpec((8, 3 * SC_WIDTH), _before_halo(tb)),
                  tok(D_MODEL), _shard_rows(land_a, A_OUT_AT, OUT_SHARD), full(gn), full(scw), full(gs)],
        out_specs=[tok(D_MODEL), pl.BlockSpec((D_MODEL, tb), lambda i: (0, i))],
        out_shape=[jax.ShapeDtypeStruct((T, D_MODEL), F32), jax.ShapeDtypeStruct((D_MODEL, T), BF16)],
        compiler_params=_params(("parallel",)),
    )(o, z, sc_in, sc_in, x, land_a, gn, scw, gs)


def _ffn(x1, g2, land_b):
    T = x1.shape[0]
    tb = 256

    def body(x_ref, g_ref, wgt_ref, wut_ref, wd_ref, x2_ref, a_ref, b_ref, h_ref):
        xv = x_ref[...]
        r = lax.rsqrt(jnp.mean(xv * xv, axis=-1, keepdims=True) + EPS)
        h = (xv * r * g_ref[...]).astype(BF16)
        a = lax.dot_general(h, _whole(wgt_ref), NT, preferred_element_type=F32)
        b = lax.dot_general(h, _whole(wut_ref), NT, preferred_element_type=F32)
        act = (a * _sigmoid(a) * b).astype(BF16)
        x2_ref[...] = xv + jnp.dot(act, _whole(wd_ref), preferred_element_type=F32)
        a_ref[...] = a.astype(BF16)
        b_ref[...] = b.astype(BF16)
        h_ref[...] = h

    tok = lambda w: pl.BlockSpec((tb, w), lambda i: (i, 0))
    return pl.pallas_call(
        body, name="ffn", grid=(T // tb,),
        in_specs=[tok(D_MODEL), pl.BlockSpec(g2.shape, lambda i: (0, 0)), _shard_rows(land_b, 0, FF_SHARD),
                  _shard_rows(land_b, FF_SHARD, FF_SHARD), _shard_rows(land_b, 2 * FF_SHARD, FF_SHARD)],
        out_specs=[tok(D_MODEL), tok(D_FF), tok(D_FF), tok(D_MODEL)],
        out_shape=[jax.ShapeDtypeStruct((T, D_MODEL), F32), jax.ShapeDtypeStruct((T, D_FF), BF16),
                   jax.ShapeDtypeStruct((T, D_FF), BF16), jax.ShapeDtypeStruct((T, D_MODEL), BF16)],
        compiler_params=_params(("parallel",)),
    )(x1, g2, land_b, land_b, land_b)


def _loss_head(x, gf, target):
    T = x.shape[0]
    tb = 512

    def body(x_ref, g_ref, t_ref, dx_ref, dxb_ref, loss_ref, dg_ref):
        @pl.when(pl.program_id(0) == 0)
        def _():
            loss_ref[...] = jnp.zeros_like(loss_ref)
            dg_ref[...] = jnp.zeros_like(dg_ref)

        xv = x_ref[...]
        r = lax.rsqrt(jnp.mean(xv * xv, axis=-1, keepdims=True) + EPS)
        xh = xv * r
        err = xh * g_ref[...] - t_ref[...]
        per_tok = jnp.mean(err * err, axis=-1, keepdims=True)
        loss_ref[...] += 0.5 * jnp.sum(per_tok, axis=0, keepdims=True)
        dy = err * (1.0 / D_MODEL)
        _row_acc(dg_ref, dy * xh)
        dx = _rms_bwd(dy, xh, r, g_ref[...])
        dx_ref[...] = dx
        dxb_ref[...] = dx.astype(BF16)

    tok = pl.BlockSpec((tb, D_MODEL), lambda i: (i, 0))
    return pl.pallas_call(
        body, name="loss_head", grid=(T // tb,),
        in_specs=[tok, pl.BlockSpec(gf.shape, lambda i: (0, 0)), tok],
        out_specs=[tok, tok, pl.BlockSpec((8, LANES), lambda i: (0, 0)), pl.BlockSpec((8, D_MODEL), lambda i: (0, 0))],
        out_shape=[jax.ShapeDtypeStruct((T, D_MODEL), F32), jax.ShapeDtypeStruct((T, D_MODEL), BF16),
                   jax.ShapeDtypeStruct((8, LANES), F32), jax.ShapeDtypeStruct((8, D_MODEL), F32)],
        compiler_params=_params(("arbitrary",)),
    )(x, gf, target)


def _ffn_bwd(dx2, x1, a, b, g2, land_b):
    T = x1.shape[0]
    tb = 256

    def body(dx2_ref, x_ref, a_ref, b_ref, g_ref, wgt_ref, wut_ref, wd_ref,
             dx1_ref, dx1b_ref, dat_ref, dbt_ref, at_ref, dg_ref):
        @pl.when(pl.program_id(0) == 0)
        def _():
            dg_ref[...] = jnp.zeros_like(dg_ref)

        dx2v = dx2_ref[...]
        av = a_ref[...].astype(F32)
        bv = b_ref[...].astype(F32)
        dact = _mm(dx2v, _whole(wd_ref), NT)
        sa = _sigmoid(av)
        silu = av * sa
        da = (dact * bv * (sa * (1.0 + av * (1.0 - sa)))).astype(BF16)
        db = (dact * silu).astype(BF16)
        dh = _mm(da, _whole(wgt_ref)) + _mm(db, _whole(wut_ref))
        xv = x_ref[...]
        r = lax.rsqrt(jnp.mean(xv * xv, axis=-1, keepdims=True) + EPS)
        xh = xv * r
        _row_acc(dg_ref, dh * xh)
        dx1 = dx2v + _rms_bwd(dh, xh, r, g_ref[...])
        dx1_ref[...] = dx1
        dx1b_ref[...] = dx1.astype(BF16)
        dat_ref[...] = da.T
        dbt_ref[...] = db.T
        at_ref[...] = (silu * bv).astype(BF16).T

    tok = lambda w: pl.BlockSpec((tb, w), lambda i: (i, 0))
    tr = pl.BlockSpec((D_FF, tb), lambda i: (0, i))
    return pl.pallas_call(
        body, name="ffn_bwd", grid=(T // tb,),
        in_specs=[tok(D_MODEL), tok(D_MODEL), tok(D_FF), tok(D_FF), pl.BlockSpec(g2.shape, lambda i: (0, 0)),
                  _shard_rows(land_b, 0, FF_SHARD), _shard_rows(land_b, FF_SHARD, FF_SHARD),
                  _shard_rows(land_b, 2 * FF_SHARD, FF_SHARD)],
        out_specs=[tok(D_MODEL), tok(D_MODEL), tr, tr, tr, pl.BlockSpec((8, D_MODEL), lambda i: (0, 0))],
        out_shape=[jax.ShapeDtypeStruct((T, D_MODEL), F32), jax.ShapeDtypeStruct((T, D_MODEL), BF16)]
        + [jax.ShapeDtypeStruct((D_FF, T), BF16)] * 3 + [jax.ShapeDtypeStruct((8, D_MODEL), F32)],
        compiler_params=_params(("arbitrary",)),
    )(dx2, x1, a, b, g2, land_b, land_b, land_b)


def _wgrad(at, b, bm, bn, name):
    M, T = at.shape
    N = b.shape[1]
    bk = min(T, 1024)

    def body(a_ref, b_ref, o_ref):
        @pl.when(pl.program_id(2) == 0)
        def _():
            o_ref[...] = jnp.zeros_like(o_ref)

        o_ref[...] += jnp.dot(a_ref[...], b_ref[...], preferred_element_type=F32)

    return pl.pallas_call(
        body, name=name, grid=(M // bm, N // bn, T // bk),
        in_specs=[pl.BlockSpec((bm, bk), lambda i, j, kk: (i, kk)), pl.BlockSpec((bk, bn), lambda i, j, kk: (kk, j))],
        out_specs=pl.BlockSpec((bm, bn), lambda i, j, kk: (i, j)),
        out_shape=jax.ShapeDtypeStruct((M, N), F32),
        compiler_params=_params(("parallel", "parallel", "arbitrary")),
    )(at, b)


def _wgrad_share(at, b, parts, first, name):
    M, T = at.shape
    rows = M // N_CHIPS
    assert first % rows == 0 and b.shape[1] == parts.shape[2]
    bk = min(T, 1024)
    n_k = T // bk

    def body(a_ref, b_ref, parts_ref, o_ref, acc_ref):
        kk = pl.program_id(1)

        @pl.when(kk == 0)
        def _():
            acc_ref[...] = jnp.zeros_like(acc_ref)

        acc_ref[...] += jnp.dot(a_ref[...], b_ref[...], preferred_element_type=F32)

        @pl.when(kk == n_k - 1)
        def _():
            o_ref[0] = acc_ref[...].astype(BF16)

    return pl.pallas_call(
        body, name=name, grid=(N_CHIPS, n_k),
        in_specs=[pl.BlockSpec((rows, bk), lambda i, kk: (i, kk)), pl.BlockSpec((bk, b.shape[1]), lambda i, kk: (kk, 0)), _ANY],
        out_specs=pl.BlockSpec((1, rows, b.shape[1]), lambda i, kk: (i, first // rows, 0)),
        out_shape=jax.ShapeDtypeStruct(parts.shape, BF16),
        scratch_shapes=[pltpu.VMEM((rows, b.shape[1]), F32)],
        input_output_aliases={2: 0},
        compiler_params=_params(("parallel", "arbitrary")),
    )(at, b, parts)


def _mix_out_bwd(dx1, o, z, sc_in, land_a, gn, scw, gs, dp):
    T = dx1.shape[0]
    tb = 256

    def body(dx_ref, o_ref, z_ref, sc_ref, halo_ref, w_ref, gn_ref, scw_ref, gs_ref, dp_ref,
             do_ref, dz_ref, dgb_ref, dcv_ref, dgn_ref, dgs_ref, dscw_ref):
        @pl.when(pl.program_id(0) == 0)
        def _():
            dgn_ref[...] = jnp.zeros_like(dgn_ref)
            dgs_ref[...] = jnp.zeros_like(dgs_ref)
            dscw_ref[...] = jnp.zeros_like(dscw_ref)

        dmix = _mm(dx_ref[...], _whole(w_ref), NT)
        don = dmix[:, :DN_WIDTH]
        dosc = dmix[:, DN_WIDTH:]
        zv = z_ref[...]
        _, oh, rs, sz, gn4 = _dn_out(o_ref[...], zv, gn_ref[...])
        silu_z = zv * sz
        dgn_full = don * oh * silu_z
        dgn_ref[0:1, :] += jnp.sum(sum(dgn_full[:, HEAD_DIM * hh:HEAD_DIM * (hh + 1)] for hh in range(HEADS)),
                                   axis=0, keepdims=True)
        dz_ref[...] = (don * oh * gn4 * (sz * (1.0 + zv * (1.0 - sz)))).astype(BF16)
        t = don * gn4 * silu_z
        for hh in range(HEADS):
            sl = slice(HEAD_DIM * hh, HEAD_DIM * (hh + 1))
            th, ohh = t[:, sl], oh[:, sl]
            do_ref[:, sl] = rs[hh] * (th - ohh * jnp.mean(th * ohh, axis=-1, keepdims=True))
        halo = jnp.where(pl.program_id(0) > 0, halo_ref[...], 0.0)
        u, cv, gate_b, yh, rys = _sc_fwd(sc_ref[...], halo, scw_ref[...], tb)
        _row_acc(dgs_ref, dosc * yh)
        ty = dosc * gs_ref[...]
        gw = SC_WIDTH // SC_GROUPS
        dys = []
        for gi in range(SC_GROUPS):
            sl = slice(gw * gi, gw * (gi + 1))
            tg, yg = ty[:, sl], yh[:, sl]
            dys.append(rys[gi] * (tg - yg * jnp.mean(tg * yg, axis=-1, keepdims=True)))
        dy = jnp.concatenate(dys, axis=1)
        dgb_ref[...] = dy * cv
        dcv = dy * gate_b
        dcv_ref[...] = dcv
        for j in range(3):
            dscw_ref[j:j + 1, :] += jnp.sum(dcv * _rows_from(u, 6 + j, tb), axis=0, keepdims=True)

    tok = lambda w: pl.BlockSpec((tb, w), lambda i: (i, 0))
    full = lambda t: pl.BlockSpec(t.shape, lambda i: (0, 0))
    acc = lambda w: pl.BlockSpec((8, w), lambda i: (0, 0))
    return pl.pallas_call(
        body, name="mix_out_bwd", grid=(T // tb,),
        in_specs=[tok(D_MODEL), tok(DN_WIDTH), tok(DN_WIDTH), tok(3 * SC_WIDTH),
                  pl.BlockSpec((8, 3 * SC_WIDTH), _before_halo(tb)), _shard_rows(land_a, A_OUT_AT, OUT_SHARD),
                  full(gn), full(scw), full(gs), _ANY],
        out_specs=[tok(DN_WIDTH), _dp_block(tb, P_Z, DN_WIDTH), tok(SC_WIDTH), tok(SC_WIDTH),
                   acc(HEAD_DIM), acc(SC_WIDTH), acc(SC_WIDTH)],
        out_shape=[jax.ShapeDtypeStruct((T, DN_WIDTH), F32), jax.ShapeDtypeStruct(dp.shape, BF16),
                   jax.ShapeDtypeStruct((T, SC_WIDTH), F32), jax.ShapeDtypeStruct((T, SC_WIDTH), F32),
                   jax.ShapeDtypeStruct((8, HEAD_DIM), F32), jax.ShapeDtypeStruct((8, SC_WIDTH), F32),
                   jax.ShapeDtypeStruct((8, SC_WIDTH), F32)],
        input_output_aliases={9: 1},
        compiler_params=_params(("arbitrary",)),
    )(dx1, o, z, sc_in, sc_in, land_a, gn, scw, gs, dp)


def _sc_conv_bwd(dcv, dgb, sc_in, scw, dp):
    T = dcv.shape[0]
    tb = 512

    def body(dcv_ref, halo_ref, dgb_ref, sc_ref, w_ref, dp_ref, out_ref):
        last = pl.program_id(0) == pl.num_programs(0) - 1
        halo = jnp.where(last, 0.0, halo_ref[...])
        xc = jnp.concatenate([dcv_ref[...], halo], axis=0)
        w = w_ref[...]
        du = w[2:3, :] * xc[0:tb, :] + w[1:2, :] * _rows_from(xc, 1, tb) + w[0:1, :] * _rows_from(xc, 2, tb)
        sc = sc_ref[...]
        out_ref[:, :SC_WIDTH] = dgb_ref[...].astype(BF16)
        out_ref[:, SC_WIDTH:2 * SC_WIDTH] = (du * sc[:, 2 * SC_WIDTH:]).astype(BF16)
        out_ref[:, 2 * SC_WIDTH:] = (du * sc[:, SC_WIDTH:2 * SC_WIDTH]).astype(BF16)

    tok = lambda w: pl.BlockSpec((tb, w), lambda i: (i, 0))
    return pl.pallas_call(
        body, name="sc_conv_bwd", grid=(T // tb,),
        in_specs=[tok(SC_WIDTH), pl.BlockSpec((8, SC_WIDTH), _after_halo(tb, T)), tok(SC_WIDTH), tok(3 * SC_WIDTH),
                  pl.BlockSpec(scw.shape, lambda i: (0, 0)), _ANY],
        out_specs=_dp_block(tb, P_SC, 3 * SC_WIDTH),
        out_shape=jax.ShapeDtypeStruct(dp.shape, BF16),
        input_output_aliases={5: 0},
        compiler_params=_params(("parallel",)),
    )(dcv, dcv, dgb, sc_in, scw, dp)


def _delta_bwd(q, k, v, bg, states, do):
    T = q.shape[0]
    tb = 512
    n_chunk = tb // CHUNK
    nb = T // tb

    def body(q_ref, k_ref, v_ref, bg_ref, st_ref, do_ref, dq_ref, dk_ref, dv_ref, dbg_ref, ds_ref):
        @pl.when(pl.program_id(0) == 0)
        def _():
            ds_ref[...] = jnp.zeros_like(ds_ref)

        masks = _chunk_masks()
        causal, strict = masks
        lane = lax.broadcasted_iota(jnp.int32, (CHUNK, LANES), 1)
        last_row = lax.broadcasted_iota(jnp.int32, (CHUNK, 1), 0) == CHUNK - 1
        cat = jnp.concatenate
        heads = range(HEADS)

        def open_chunk(ci, loc):
            rows = pl.ds(pl.multiple_of(ci * CHUNK, CHUNK), CHUNK)
            dov = do_ref[rows, :]
            return dict(rows=rows, loc=loc, do=[dov[:, HEAD_DIM * h:HEAD_DIM * (h + 1)] for h in heads],
                        state=[st_ref[ci, h] for h in heads])

        def a_free(c):
            loc, do, state = c["loc"], c["do"], c["state"]
            w_s = [_mm(p["w"], s) for p, s in zip(loc, state)]
            c["dq_dec"] = [_mm(d, s, NT) for d, s in zip(do, state)]
            c["qk_do"] = [_mm(p["qk"], d, TN) for p, d in zip(loc, do)]
            c["qd_do"] = [_mm(p["q_dec"], d, TN) for p, d in zip(loc, do)]
            c["v_new"] = [p["u"] - t for p, t in zip(loc, w_s)]
            c["dqk"] = [jnp.where(causal, _mm(d, vn, NT), 0.0) for d, vn in zip(do, c["v_new"])]

        def a_state(c, ds_next):
            c["ds_next"] = ds_next
            kd_ds = [_mm(p["k_dec"], d) for p, d in zip(c["loc"], ds_next)]
            c["dk_dec"] = [_mm(vn, d, NT) for vn, d in zip(c["v_new"], ds_next)]
            c["dv_new"] = [a + b for a, b in zip(c["qk_do"], kd_ds)]

        def b_state(c):
            loc = c["loc"]
            w_dv = [_mm(p["w"], dvn, TN) for p, dvn in zip(loc, c["dv_new"])]
            c["dw"] = [-_mm(dvn, s, NT) for dvn, s in zip(c["dv_new"], c["state"])]
            return [loc[h]["gl"] * c["ds_next"][h] + c["qd_do"][h] - w_dv[h] for h in heads]

        def c_solve(c):
            loc, dv_new, dw = c["loc"], c["dv_new"], c["dw"]
            c["dtm"] = [_mm(cat([dvn, d], axis=1), cat([p["vb"], p["kbg"]], axis=1), NT) for dvn, d, p in zip(dv_new, dw, loc)]
            x_t = [_mm(p["xm"], cat([dvn, d], axis=1), TN) for p, dvn, d in zip(loc, dv_new, dw)]
            c["dvb"] = [dvn + t[:, :HEAD_DIM] for dvn, t in zip(dv_new, x_t)]
            c["dkbg"] = [d + t[:, HEAD_DIM:] for d, t in zip(dw, x_t)]

        def d_solve(c):
            c["y"] = [t + _mm(p["xm"], t, TN) for p, t in zip(c["loc"], c["dtm"])]

        def e_solve(c):
            c["dlow"] = [jnp.where(strict, -(t + _mm(t, p["xm"], NT)), 0.0) for p, t in zip(c["loc"], c["y"])]

        def f_close(c):
            loc, rows = c["loc"], c["rows"]
            dmm = [d * p["decay"] for d, p in zip(c["dlow"], loc)]
            dnn = [d * p["decay"] for d, p in zip(c["dqk"], loc)]
            by_k = [_mm(cat([a, b], axis=0), p["k"]) for a, b, p in zip(dmm, dnn, loc)]
            dk_mm = [_mm(cat([a, b], axis=0), cat([p["kb"], p["q"]], axis=0), TN) for a, b, p in zip(dmm, dnn, loc)]
            dq_out, dk_out, dv_out = [], [], []
            dbeta_all = jnp.zeros((CHUNK, LANES), F32)
            dgc_all = jnp.zeros((CHUNK, LANES), F32)
            for h in heads:
                p = loc[h]
                dkb = by_k[h][:CHUNK] + c["dkbg"][h] * p["eg"]
                dq_out.append(by_k[h][CHUNK:] + c["dq_dec"][h] * p["eg"])
                dk_out.append(dk_mm[h] + c["dk_dec"][h] * p["ek"] + dkb * p["beta"])
                dv_out.append(c["dvb"][h] * p["beta"])
                dbeta = jnp.sum(dkb * p["k"] + c["dvb"][h] * p["v"], axis=1, keepdims=True)
                e = c["dlow"][h] * p["low"] + c["dqk"][h] * p["qk"]
                kd = jnp.sum(c["dk_dec"][h] * p["k_dec"], axis=1, keepdims=True)
                dgc = (jnp.sum(e, axis=1, keepdims=True) - jnp.sum(e.T, axis=1, keepdims=True)
                       + jnp.sum(c["dq_dec"][h] * p["q_dec"], axis=1, keepdims=True) - kd
                       + jnp.sum(c["dkbg"][h] * p["kbg"], axis=1, keepdims=True))
                dgl = jnp.sum(jnp.sum(c["ds_next"][h] * c["state"][h], axis=1, keepdims=True), axis=0, keepdims=True)
                d_last = jnp.sum(kd, axis=0, keepdims=True) + dgl * p["gl"]
                dgc = dgc + jnp.where(last_row, d_last, 0.0)
                dbeta_all = jnp.where(lane == h, dbeta, dbeta_all)
                dgc_all = jnp.where(lane == h + HEADS, dgc, dgc_all)
            dq_ref[rows, :] = cat(dq_out, axis=1)
            dk_ref[rows, :] = cat(dk_out, axis=1)
            dv_ref[rows, :] = cat(dv_out, axis=1)
            dbg_ref[rows, :] = dbeta_all + dgc_all

        def pair(pj, carry):
            hi = n_chunk - 1 - 2 * pj
            lo = hi - 1
            rows = [pl.ds(pl.multiple_of(ci * CHUNK, CHUNK), CHUNK) for ci in (hi, lo)]
            loc = _units_local(_chunk_units(q_ref, k_ref, v_ref, bg_ref, rows[0])
                               + _chunk_units(q_ref, k_ref, v_ref, bg_ref, rows[1]), masks)
            c_hi, c_lo = open_chunk(hi, loc[:HEADS]), open_chunk(lo, loc[HEADS:])
            a_free(c_hi)
            a_free(c_lo)
            a_state(c_hi, [ds_ref[h] for h in heads])
            ds_mid = b_state(c_hi)
            a_state(c_lo, ds_mid)
            c_solve(c_hi)
            ds_out = b_state(c_lo)
            for h in heads:
                ds_ref[h] = ds_out[h]
            d_solve(c_hi)
            c_solve(c_lo)
            e_solve(c_hi)
            d_solve(c_lo)
            f_close(c_hi)
            e_solve(c_lo)
            f_close(c_lo)
            return carry

        lax.fori_loop(0, n_chunk // 2, pair, 0)

    tok = lambda w: pl.BlockSpec((tb, w), lambda i: (nb - 1 - i, 0))
    return pl.pallas_call(
        body, name="delta_bwd", grid=(nb,),
        in_specs=[tok(DN_WIDTH), tok(DN_WIDTH), tok(DN_WIDTH), tok(LANES),
                  pl.BlockSpec((n_chunk, HEADS, HEAD_DIM, HEAD_DIM), lambda i: (nb - 1 - i, 0, 0, 0)), tok(DN_WIDTH)],
        out_specs=[tok(DN_WIDTH), tok(DN_WIDTH), tok(DN_WIDTH), tok(LANES)],
        out_shape=[jax.ShapeDtypeStruct((T, DN_WIDTH), F32)] * 3 + [jax.ShapeDtypeStruct((T, LANES), F32)],
        scratch_shapes=[pltpu.VMEM((HEADS, HEAD_DIM, HEAD_DIM), F32)],
        compiler_params=_params(("arbitrary",)),
    )(q, k, v, bg, states, do)


def _dn_prep_bwd(dq, dk, dv, dbg, qkv, cw, bd, al_row, dt_row, dp):
    T = qkv.shape[0]
    tb = 256

    def body(dq_ref, dk_ref, dv_ref, dbg_ref, pre_ref, halo_ref, cw_ref, bd_ref, al_ref, dt_ref, dp_ref,
             dc_ref, dbd_ref, dcw_ref, dal_ref, ddt_ref):
        @pl.when(pl.program_id(0) == 0)
        def _():
            dcw_ref[...] = jnp.zeros_like(dcw_ref)
            dal_ref[...] = jnp.zeros_like(dal_ref)
            ddt_ref[...] = jnp.zeros_like(ddt_ref)

        halo = jnp.where(pl.program_id(0) > 0, halo_ref[...], 0.0)
        xc, c, sg, a = _dn_act(pre_ref[...], halo, cw_ref[...], tb)
        dsilu = sg * (1.0 + c * (1.0 - sg))
        for hh in range(HEADS):
            sl = slice(HEAD_DIM * hh, HEAD_DIM * (hh + 1))
            for base, g_ref, scale in ((0, dq_ref, Q_SCALE), (DN_WIDTH, dk_ref, 1.0)):
                sa = slice(base + HEAD_DIM * hh, base + HEAD_DIM * (hh + 1))
                raw = a[:, sa]
                r = lax.rsqrt(jnp.sum(raw * raw, axis=-1, keepdims=True) + EPS)
                nrm = raw * r
                gn_ = g_ref[:, sl] * scale
                dc_ref[:, sa] = r * (gn_ - nrm * jnp.sum(gn_ * nrm, axis=-1, keepdims=True)) * dsilu[:, sa]
        dc_ref[:, 2 * DN_WIDTH:] = dv_ref[...] * dsilu[:, 2 * DN_WIDTH:]
        dc = dc_ref[...]
        for j in range(4):
            dcw_ref[j:j + 1, :] += jnp.sum(dc * _rows_from(xc, 5 + j, tb), axis=0, keepdims=True)
        bdv = bd_ref[...]
        lane = lax.broadcasted_iota(jnp.int32, bdv.shape, 1)
        is_b = lane < HEADS
        dbg_in = dbg_ref[...]
        dbgv = jnp.where(is_b, dbg_in, _mm32(_chunk_cumsum_matrix(tb), dbg_in, TN))
        is_g = jnp.logical_and(lane >= HEADS, lane < 2 * HEADS)
        beta = _sigmoid(bdv)
        neg_a = -jnp.exp(al_ref[...])
        pre_sp = bdv + dt_ref[...]
        g = neg_a * _softplus(pre_sp)
        da_in = dbgv * neg_a * _sigmoid(pre_sp)
        dbd_ref[...] = jnp.where(is_b, dbgv * beta * (1.0 - beta), jnp.where(is_g, da_in, 0.0)).astype(BF16)
        _row_acc(dal_ref, jnp.where(is_g, dbgv * g, 0.0))
        _row_acc(ddt_ref, jnp.where(is_g, da_in, 0.0))

    tok = lambda w: pl.BlockSpec((tb, w), lambda i: (i, 0))
    full = lambda t: pl.BlockSpec(t.shape, lambda i: (0, 0))
    acc = lambda w: pl.BlockSpec((8, w), lambda i: (0, 0))
    return pl.pallas_call(
        body, name="dn_prep_bwd", grid=(T // tb,),
        in_specs=[tok(DN_WIDTH), tok(DN_WIDTH), tok(DN_WIDTH), tok(LANES),
                  tok(QKV), pl.BlockSpec((8, QKV), _before_halo(tb)), full(cw), tok(LANES), full(al_row), full(dt_row), _ANY],
        out_specs=[tok(QKV), _dp_block(tb, P_BD, LANES), acc(QKV), acc(LANES), acc(LANES)],
        out_shape=[jax.ShapeDtypeStruct((T, QKV), F32), jax.ShapeDtypeStruct(dp.shape, BF16),
                   jax.ShapeDtypeStruct((8, QKV), F32), jax.ShapeDtypeStruct((8, LANES), F32),
                   jax.ShapeDtypeStruct((8, LANES), F32)],
        input_output_aliases={10: 1},
        compiler_params=_params(("arbitrary",)),
    )(dq, dk, dv, dbg, qkv, qkv, cw, bd, al_row, dt_row, dp)


def _dn_conv_bwd(dc, cw, dp):
    T = dc.shape[0]
    tb = 512

    def body(dc_ref, halo_ref, w_ref, dp_ref, out_ref):
        last = pl.program_id(0) == pl.num_programs(0) - 1
        halo = jnp.where(last, 0.0, halo_ref[...])
        xc = jnp.concatenate([dc_ref[...], halo], axis=0)
        w = w_ref[...]
        acc = w[3:4, :] * xc[0:tb, :]
        for j in range(3):
            acc = acc + w[j:j + 1, :] * _rows_from(xc, 3 - j, tb)
        out_ref[...] = acc.astype(BF16)

    tok = pl.BlockSpec((tb, QKV), lambda i: (i, 0))
    return pl.pallas_call(
        body, name="dn_conv_bwd", grid=(T // tb,),
        in_specs=[tok, pl.BlockSpec((8, QKV), _after_halo(tb, T)), pl.BlockSpec(cw.shape, lambda i: (0, 0)), _ANY],
        out_specs=_dp_block(tb, 0, QKV),
        out_shape=jax.ShapeDtypeStruct(dp.shape, BF16),
        input_output_aliases={3: 0},
        compiler_params=_params(("parallel",)),
    )(dc, dc, cw, dp)


def _in_proj_bwd(dp, dx1, x, g1, wp):
    T = x.shape[0]
    tb = 256

    def body(dp_ref, dx1_ref, x_ref, g_ref, wp_ref, dx_ref, dxb_ref, dg_ref):
        @pl.when(pl.program_id(0) == 0)
        def _():
            dg_ref[...] = jnp.zeros_like(dg_ref)

        dh = lax.dot_general(dp_ref[...], wp_ref[...], NT, preferred_element_type=F32)
        xv = x_ref[...]
        r = lax.rsqrt(jnp.mean(xv * xv, axis=-1, keepdims=True) + EPS)
        xh = xv * r
        _row_acc(dg_ref, dh * xh)
        dx = dx1_ref[...] + _rms_bwd(dh, xh, r, g_ref[...])
        dx_ref[...] = dx
        dxb_ref[...] = dx.astype(BF16)

    tok = lambda w: pl.BlockSpec((tb, w), lambda i: (i, 0))
    full = lambda t: pl.BlockSpec(t.shape, lambda i: (0, 0))
    return pl.pallas_call(
        body, name="in_proj_bwd", grid=(T // tb,),
        in_specs=[tok(P_COLS), tok(D_MODEL), tok(D_MODEL), full(g1), full(wp)],
        out_specs=[tok(D_MODEL), tok(D_MODEL), pl.BlockSpec((8, D_MODEL), lambda i: (0, 0))],
        out_shape=[jax.ShapeDtypeStruct((T, D_MODEL), F32), jax.ShapeDtypeStruct((T, D_MODEL), BF16),
                   jax.ShapeDtypeStruct((8, D_MODEL), F32)],
        compiler_params=_params(("arbitrary",)),
    )(dp, dx1, x, g1, wp)


def _pad_rows(a, rows=8):
    return jnp.pad(a, ((0, rows - a.shape[0]), (0, 0)))


def _gate_rows(a_log, dt_bias):
    put = lambda t: jnp.pad(t.reshape(1, HEADS), ((0, 0), (HEADS, LANES - 2 * HEADS)))
    return put(a_log), put(dt_bias)


W_Z = QKV
W_BD = W_Z + DN_WIDTH
W_SC = W_BD + 2 * HEADS


def _projection_of(w_in):
    return jnp.concatenate([w_in[:, :W_Z], w_in[:, W_SC:], w_in[:, W_Z:W_BD],
                            jnp.pad(w_in[:, W_BD:W_SC], ((0, 0), (0, LANES - 2 * HEADS)))], axis=1)


def _w_in_grad_of(g_wp):
    return jnp.concatenate([g_wp[:, :P_SC], g_wp[:, P_Z:P_BD], g_wp[:, P_BD:P_BD + 2 * HEADS], g_wp[:, P_SC:P_Z]], axis=1)


def _mixer_fwd(x, p):
    qkv, z, sc_in, bd, ht = _in_proj(x, p["g1"], p["wp"])
    q, k, v, bg = _dn_prep(qkv, p["cw"], bd, p["al"], p["dt"])
    o, states = _delta_fwd(q, k, v, bg)
    x1, mt = _mix_out(o, z, sc_in, x, p["land_a"], p["gn"], p["scw"], p["gs"])
    return x1, dict(x=x, qkv=qkv, z=z, sc_in=sc_in, bd=bd, ht=ht, q=q, k=k, v=v, bg=bg, o=o, states=states, mt=mt)


def _ffn_fwd(x1, p, land_b):
    x2, a, b, h2 = _ffn(x1, p["g2"], land_b)
    return x2, dict(x1=x1, a=a, b=b, h2=h2)


def _ffn_back(dx2, dx2_bf16, s, p, land_b):
    dx1, dx1_bf16, da_t, db_t, act_t, dg2 = _ffn_bwd(dx2, s["x1"], s["a"], s["b"], p["g2"], land_b)
    parts = lax.empty((N_CHIPS, B_ROWS, D_MODEL), BF16)
    parts = _wgrad_share(act_t, dx2_bf16, parts, 2 * FF_SHARD, "wgrad_down")
    parts = _wgrad_share(da_t, s["h2"], parts, 0, "wgrad_gate")
    parts = _wgrad_share(db_t, s["h2"], parts, FF_SHARD, "wgrad_up")
    return dx1, dx1_bf16, parts, dg2[0]


def _mixer_bwd(dx1, dx1_bf16, s, p):
    dp = lax.empty((dx1.shape[0], P_COLS), BF16)
    do, dp, dgb, dcv, dgn, dgs, dscw = _mix_out_bwd(dx1, s["o"], s["z"], s["sc_in"], p["land_a"], p["gn"], p["scw"], p["gs"], dp)
    dp = _sc_conv_bwd(dcv, dgb, s["sc_in"], p["scw"], dp)
    dq, dk, dv, dbg = _delta_bwd(s["q"], s["k"], s["v"], s["bg"], s["states"], do)
    dc, dp, dcw, dal, ddt = _dn_prep_bwd(dq, dk, dv, dbg, s["qkv"], p["cw"], s["bd"], p["al"], p["dt"], dp)
    dp = _dn_conv_bwd(dc, p["cw"], dp)
    dx, dx_bf16, dg1 = _in_proj_bwd(dp, dx1, s["x"], p["g1"], p["wp"])
    g_w_in = _w_in_grad_of(_wgrad(s["ht"], dp, 512, P_COLS, "wgrad_in"))
    cols = jnp.moveaxis(g_w_in.reshape(D_MODEL, N_CHIPS, IN_SHARD), 1, 0)
    parts = jnp.pad(cols, ((0, 0), (0, OUT_SHARD), (0, D_MODEL - IN_SHARD))).astype(BF16)
    parts = _wgrad_share(s["mt"], dx1_bf16, parts, A_OUT_AT, "wgrad_out")
    g = dict(g1=dg1[0], gn=dgn[0], gs=dgs[0], scw=dscw[:3], cw=dcw[:4], al=dal[0, HEADS:2 * HEADS], dt=ddt[0, HEADS:2 * HEADS])
    return dx, dx_bf16, parts, g


def _place():
    return lax.axis_index("x"), lax.axis_index("y"), lax.axis_index("c")


def _other_chips(x, y):
    return [(1 - x, y), (x, 1 - y), (1 - x, 1 - y)]


_HBM = pl.BlockSpec(memory_space=pltpu.HBM)


def _chip_exchange(arrs, name, gather):
    n = len(arrs)

    def body(*refs):
        ins, outs = refs[:n], refs[n:2 * n]
        send_sems, recv_sems, local_sems = refs[2 * n:]
        x, y, c = _place()
        me = 2 * x + y
        others = _other_chips(x, y)

        def remote(k, j, landing):
            px, py = others[j]
            src = ins[k] if gather else ins[k].at[2 * px + py]
            return pltpu.make_async_remote_copy(src_ref=src, dst_ref=outs[k].at[landing], send_sem=send_sems.at[k, j],
                                                recv_sem=recv_sems.at[k, j], device_id=(px, py, c), device_id_type=MESH)

        local = [pltpu.make_async_copy(ins[k] if gather else ins[k].at[me], outs[k].at[me], local_sems.at[k])
                 for k in range(n)]
        sends = [remote(k, j, me) for k in range(n) for j in range(3)]
        for cp in local + sends:
            cp.start()
        for k in range(n):
            for j, (px, py) in enumerate(others):
                remote(k, j, 2 * px + py).wait_recv()
        for cp in sends:
            cp.wait_send()
        for cp in local:
            cp.wait()

    shapes = [jax.ShapeDtypeStruct(((N_CHIPS,) + a.shape) if gather else a.shape, a.dtype) for a in arrs]
    return pl.pallas_call(
        body, name=name, in_specs=[_HBM] * n, out_specs=[_HBM] * n, out_shape=shapes,
        scratch_shapes=[pltpu.SemaphoreType.DMA((n, 3)), pltpu.SemaphoreType.DMA((n, 3)), pltpu.SemaphoreType.DMA((n,))],
    )(*arrs)


_SEM = pl.BlockSpec(memory_space=pltpu.SEMAPHORE)
_ANY = pl.BlockSpec(memory_space=pl.ANY)
_EFFECT = pltpu.SideEffectType.DATAFLOW_SIDE_EFFECTING


def _split_copies(src_ref, land_ref, send_sems, recv_sems, gather, sending):
    x, y, c = _place()
    me = 2 * x + y
    copies = []
    for j, (px, py) in enumerate(_other_chips(x, y)):
        peer = 2 * px + py
        copies.append(pltpu.make_async_remote_copy(
            src_ref=src_ref if gather else src_ref.at[peer], dst_ref=land_ref.at[me if sending else peer],
            send_sem=send_sems.at[j], recv_sem=recv_sems.at[j], device_id=(px, py, c), device_id_type=MESH))
    return copies


def _own_slot(share):
    chip = 2 * lax.axis_index("x") + lax.axis_index("y")
    return lax.dynamic_update_slice(lax.empty((N_CHIPS,) + share.shape, share.dtype), share[None], (chip, 0, 0))


def _exchange_start(src, land, after, name, gather):
    def body(src_ref, land_ref, after_ref, send_sems, recv_sems, src_thru, land_thru, token):
        for cp in _split_copies(src_ref, land_ref, send_sems, recv_sems, gather, sending=True):
            cp.start()
        token[...] = jnp.zeros_like(token)

    hbm = lambda t: pltpu.with_memory_space_constraint(t, pltpu.HBM)
    return pl.pallas_call(
        body, name=name,
        out_shape=(pltpu.SemaphoreType.DMA((3,)), pltpu.SemaphoreType.DMA((3,)), pltpu.HBM(src.shape, src.dtype),
                   pltpu.HBM(land.shape, land.dtype), jax.ShapeDtypeStruct((8, LANES), F32)),
        in_specs=(_HBM, _HBM, _ANY), out_specs=(_SEM, _SEM, _HBM, _HBM, pl.BlockSpec(memory_space=pltpu.VMEM)),
        input_output_aliases={0: 2, 1: 3},
        compiler_params=pltpu.CompilerParams(has_side_effects=_EFFECT),
    )(hbm(src), hbm(land), after)


def _exchange_wait(started, after, name, gather):
    send_sems, recv_sems, src_thru, land_thru, _ = started

    def body(src_ref, land_ref, send_sems, recv_sems, after_ref, src_dead, got_ref):
        for cp in _split_copies(src_ref, land_ref, send_sems, recv_sems, gather, sending=False):
            cp.wait_send()
            cp.wait_recv()

    return pl.pallas_call(
        body, name=name,
        out_shape=(pltpu.HBM(src_thru.shape, src_thru.dtype), pltpu.HBM(land_thru.shape, land_thru.dtype)),
        in_specs=(_HBM, _HBM, _SEM, _SEM, _ANY), out_specs=(_HBM, _HBM), input_output_aliases={0: 0, 1: 1},
        compiler_params=pltpu.CompilerParams(has_side_effects=_EFFECT),
    )(src_thru, land_thru, send_sems, recv_sems, after)[1]


def _swap_sibling(arrs, name):
    n = len(arrs)

    def body(*refs):
        ins, outs = refs[:n], refs[n:2 * n]
        send_sems, recv_sems = refs[2 * n:]
        x, y, c = _place()
        copies = [pltpu.make_async_remote_copy(src_ref=ins[k], dst_ref=outs[k], send_sem=send_sems.at[k],
                                               recv_sem=recv_sems.at[k], device_id=(x, y, 1 - c), device_id_type=MESH)
                  for k in range(n)]
        for cp in copies:
            cp.start()
        for cp in copies:
            cp.wait()

    return pl.pallas_call(
        body, name=name, in_specs=[_HBM] * n, out_specs=[_HBM] * n,
        out_shape=[jax.ShapeDtypeStruct(a.shape, a.dtype) for a in arrs],
        scratch_shapes=[pltpu.SemaphoreType.DMA((n,)), pltpu.SemaphoreType.DMA((n,))],
    )(*arrs)


def _all_reduce_small(v):
    rows = v.shape[0]
    flips = [(a, b, cc) for a in (0, 1) for b in (0, 1) for cc in (0, 1)][1:]

    def body(v_ref, out_ref, buf_ref, send_sems, recv_sems):
        x, y, c = _place()
        me = 4 * x + 2 * y + c
        peers = [((1 - x) if a else x, (1 - y) if b else y, (1 - c) if cc else c) for a, b, cc in flips]

        def copy(j, landing):
            return pltpu.make_async_remote_copy(src_ref=v_ref, dst_ref=buf_ref.at[landing], send_sem=send_sems.at[j],
                                                recv_sem=recv_sems.at[j], device_id=peers[j], device_id_type=MESH)

        sends = [copy(j, me) for j in range(N_DEV - 1)]
        for cp in sends:
            cp.start()
        buf_ref[me] = v_ref[...]
        for j, (px, py, pc) in enumerate(peers):
            copy(j, 4 * px + 2 * py + pc).wait_recv()
        for cp in sends:
            cp.wait_send()
        acc = buf_ref[0]
        for d in range(1, N_DEV):
            acc = acc + buf_ref[d]
        out_ref[...] = acc

    vmem = pl.BlockSpec(memory_space=pltpu.VMEM)
    return pl.pallas_call(
        body, name="all_reduce_small", in_specs=[vmem], out_specs=vmem,
        out_shape=jax.ShapeDtypeStruct(v.shape, F32),
        scratch_shapes=[pltpu.VMEM((N_DEV, rows, LANES), F32), pltpu.SemaphoreType.DMA((N_DEV - 1,)),
                        pltpu.SemaphoreType.DMA((N_DEV - 1,))],
    )(v)


def _row_block(*sizes):
    return next(t for t in (256, 192, 128, 64) if all(s % t == 0 for s in sizes))


def _sum_chips(parts, name):
    _, rows, cols = parts[0].shape
    n = len(parts)
    tr = _row_block(rows)

    def body(*refs):
        o_ref = refs[n]
        for l in range(n):
            @pl.when(pl.program_id(0) == l)
            def _(p_ref=refs[l]):
                acc = p_ref[0].astype(F32)
                for s in range(1, N_CHIPS):
                    acc = acc + p_ref[s].astype(F32)
                o_ref[0] = acc

    return pl.pallas_call(
        body, name=name, grid=(n, rows // tr),
        in_specs=[pl.BlockSpec((N_CHIPS, tr, cols), lambda l, i, k=k: (0, jnp.where(l == k, i, 0), 0)) for k in range(n)],
        out_specs=pl.BlockSpec((1, tr, cols), lambda l, i: (l, i, 0)),
        out_shape=jax.ShapeDtypeStruct((n, rows, cols), F32),
        compiler_params=_params(("arbitrary", "arbitrary")),
    )(*parts)


def _adam_update(w, m, v, g):
    c1 = 1.0 - ADAM_B1 ** ADAM_STEP
    c2 = 1.0 - ADAM_B2 ** ADAM_STEP
    m_new = ADAM_B1 * m + (1.0 - ADAM_B1) * g
    v_new = ADAM_B2 * v + (1.0 - ADAM_B2) * (g * g)
    return -ADAM_LR * ((m_new / c1) / (jnp.sqrt(v_new / c2) + ADAM_EPS) + ADAM_WD * w), m_new, v_new


def _adamw_rows(w, m, v, g_parts, first, name):
    n_layers, rows, cols = w.shape
    tr = _row_block(rows, first)
    n = len(g_parts)

    def body(*refs):
        w_ref, m_ref, v_ref = refs[:3]
        g_out, d_out, m_out, v_out = refs[3 + n:]
        g = refs[3][...]
        for r in refs[4:3 + n]:
            g = g + r[...]
        g = g[:, :, :cols]
        d_out[...], m_out[...], v_out[...] = _adam_update(w_ref[...], m_ref[...], v_ref[...], g)
        g_out[...] = g

    blk = pl.BlockSpec((1, tr, cols), lambda l, i: (l, i, 0))
    g_blk = pl.BlockSpec((1, tr, g_parts[0].shape[2]), lambda l, i: (l, first // tr + i, 0))
    return pl.pallas_call(
        body, name=name, grid=(n_layers, rows // tr),
        in_specs=[blk] * 3 + [g_blk] * n, out_specs=[blk] * 4,
        out_shape=[jax.ShapeDtypeStruct(w.shape, F32)] * 4,
        compiler_params=_params(("parallel", "parallel")),
    )(w, m, v, *g_parts)


def _adamw(w, m, v, g_parts, name):
    rows, cols = w.shape
    tr = min(rows, 256)
    n = len(g_parts)

    def body(*refs):
        w_ref, m_ref, v_ref = refs[:3]
        g_refs = refs[3:3 + n]
        g_out, d_out, m_out, v_out = refs[3 + n:]
        g = g_refs[0][...]
        for r in g_refs[1:]:
            g = g + r[...]
        d_out[...], m_out[...], v_out[...] = _adam_update(w_ref[...], m_ref[...], v_ref[...], g)
        g_out[...] = g

    blk = pl.BlockSpec((tr, cols), lambda i: (i, 0))
    return pl.pallas_call(
        body, name=name, grid=(rows // tr,),
        in_specs=[blk] * (3 + n), out_specs=[blk] * 4,
        out_shape=[jax.ShapeDtypeStruct((rows, cols), F32)] * 4,
        compiler_params=_params(("parallel",)),
    )(w, m, v, *g_parts)


def _pack(parts, rows, fill=0.0):
    flat = jnp.concatenate([p.reshape(-1) for p in parts])
    return jnp.pad(flat, (0, rows * LANES - flat.shape[0]), constant_values=fill).reshape(rows, LANES)


def _unpack(packed, shapes):
    flat = packed.reshape(-1)
    out, at = [], 0
    for shp in shapes:
        size = 1
        for s in shp:
            size *= s
        out.append(flat[at:at + size].reshape(shp))
        at += size
    return out


def _packed_rows(shapes):
    total = 0
    for shp in shapes:
        size = 1
        for s in shp:
            size *= s
        total += size
    return -(-total // (8 * LANES)) * 8


def _cols_full(g, l):
    t = g[:, l]
    return jnp.moveaxis(t, 0, 1).reshape(t.shape[1], N_CHIPS * t.shape[2])


def _pad_cols(t):
    return jnp.pad(t, ((0, 0),) * (t.ndim - 1) + ((0, D_MODEL - t.shape[-1]),))


def _w_in_of(land_a):
    return jnp.moveaxis(land_a[:, :D_MODEL, :IN_SHARD], 0, 1).reshape(D_MODEL, W_IN_COLS)


def kernel(x, norm1_g, w_in, dn_conv_w, dn_a_log, dn_dt_bias, dn_norm_g, sc_conv_w, sc_norm_g, w_out, norm2_g, ffn_w_gate, ffn_w_up, ffn_w_down, final_norm_g, loss_target, m_norm1_g, m_w_in, m_dn_conv_w, m_dn_a_log, m_dn_dt_bias, m_dn_norm_g, m_sc_conv_w, m_sc_norm_g, m_w_out, m_norm2_g, m_ffn_w_gate, m_ffn_w_up, m_ffn_w_down, m_final_norm_g, v_norm1_g, v_w_in, v_dn_conv_w, v_dn_a_log, v_dn_dt_bias, v_dn_norm_g, v_sc_conv_w, v_sc_norm_g, v_w_out, v_norm2_g, v_ffn_w_gate, v_ffn_w_up, v_ffn_w_down, v_final_norm_g):
    chip = 2 * lax.axis_index("x") + lax.axis_index("y")

    g_cw, g_scw = _chip_exchange([dn_conv_w, sc_conv_w], "gather_conv", gather=True)

    t_last = lambda t: jnp.swapaxes(t, -1, -2)
    gate_t, up_t = t_last(ffn_w_gate), t_last(ffn_w_up)
    zero_token = jnp.zeros((8, LANES), F32)

    def gather_start(l, after, tie):
        share_a = jnp.concatenate([_pad_cols(w_in[l] + tie), w_out[l]], axis=0).astype(BF16)
        a = _exchange_start(share_a, _own_slot(share_a), after, "gather_a_start_%d" % l, gather=True)
        share_b = jnp.concatenate([gate_t[l] + tie, up_t[l], ffn_w_down[l]], axis=0).astype(BF16)
        b = _exchange_start(share_b, _own_slot(share_b), a[4], "gather_b_start_%d" % l, gather=True)
        return a, b

    ga, gb = gather_start(0, g_cw, 0.0)
    first_sent = gb[4][0, 0]
    land_a = _exchange_wait(ga, gb[4], "gather_a_wait_0", gather=True)
    act = x[0]
    layers, saved_m, saved_f, lands_b = [], [], [], []
    for l in range(DEPTH):
        hold = 0.0
        if l + 1 < DEPTH:
            ga, gb_next = gather_start(l + 1, land_a, first_sent)
            hold = gb_next[4][0:1, 0:1]
        al, dt = _gate_rows(dn_a_log[l], dn_dt_bias[l])
        layers.append(dict(
            g1=norm1_g[l][None] + hold, wp=_projection_of(_w_in_of(land_a)), cw=_pad_rows(_cols_full(g_cw, l)), al=al, dt=dt,
            gn=dn_norm_g[l][None], scw=_pad_rows(_cols_full(g_scw, l)), gs=sc_norm_g[l][None],
            land_a=land_a, g2=norm2_g[l][None]))
        x1, s = _mixer_fwd(act, layers[l])
        saved_m.append(s)
        lands_b.append(_exchange_wait(gb, x1, "gather_b_wait_%d" % l, gather=True))
        act, s = _ffn_fwd(x1, layers[l], lands_b[l])
        saved_f.append(s)
        if l + 1 < DEPTH:
            land_a = _exchange_wait(ga, act, "gather_a_wait_%d" % (l + 1), gather=True)
            gb = gb_next

    dact, dact_bf16, loss_part, d_final = _loss_head(act, final_norm_g[None], loss_target[0])
    grads, reduce_a, reduce_b = [None] * DEPTH, [None] * DEPTH, [None] * DEPTH
    hold = 0.0
    for l in reversed(range(DEPTH)):
        p = layers[l]
        dx1, dx1_bf16, parts, dg2 = _ffn_back(dact, dact_bf16, saved_f[l], dict(p, g2=p["g2"] + hold), lands_b[l])
        reduce_b[l] = _exchange_start(parts, parts, zero_token, "reduce_b_start_%d" % l, gather=False)
        dact, dact_bf16, parts, gm = _mixer_bwd(dx1, dx1_bf16, saved_m[l], dict(p, gn=p["gn"] + reduce_b[l][4][0:1, 0:1]))
        reduce_a[l] = _exchange_start(parts, parts, zero_token, "reduce_a_start_%d" % l, gather=False)
        hold = reduce_a[l][4][0:1, 0:1]
        grads[l] = dict(gm, g2=dg2)
    loss = lax.psum(loss_part[0, 0], ("x", "y", "c"))
    stack = lambda key: jnp.stack([grads[l][key] for l in range(DEPTH)])

    got_b = [_exchange_wait(reduce_b[l], dact, "reduce_b_wait_%d" % l, gather=False) for l in reversed(range(DEPTH))][::-1]
    sum_b = _sum_chips(got_b, "sum_chips_b")
    other_b, = _swap_sibling([sum_b], "swap_sibling_b")
    big = dict(
        ffn_w_gate=[t_last(o) for o in _adamw_rows(gate_t, t_last(m_ffn_w_gate), t_last(v_ffn_w_gate),
                                                   [sum_b, other_b], 0, "adamw_gate")],
        ffn_w_up=[t_last(o) for o in _adamw_rows(up_t, t_last(m_ffn_w_up), t_last(v_ffn_w_up),
                                                 [sum_b, other_b], FF_SHARD, "adamw_up")],
        ffn_w_down=_adamw_rows(ffn_w_down, m_ffn_w_down, v_ffn_w_down, [sum_b, other_b], 2 * FF_SHARD, "adamw_down"))
    after_b = big["ffn_w_down"][1]
    got_a = [_exchange_wait(reduce_a[l], after_b, "reduce_a_wait_%d" % l, gather=False) for l in reversed(range(DEPTH))][::-1]
    sum_a = _sum_chips(got_a, "sum_chips_a")
    other_a, = _swap_sibling([sum_a], "swap_sibling_a")
    big.update(
        w_in=_adamw_rows(w_in, m_w_in, v_w_in, [sum_a, other_a], 0, "adamw_w_in"),
        w_out=_adamw_rows(w_out, m_w_out, v_w_out, [sum_a, other_a], A_OUT_AT, "adamw_w_out"))

    full_shapes = [(DEPTH, D_MODEL), (DEPTH, D_MODEL), (DEPTH, HEAD_DIM), (DEPTH, SC_WIDTH), (DEPTH, HEADS),
                   (DEPTH, HEADS), (D_MODEL,), (DEPTH, 4, QKV), (DEPTH, 3, SC_WIDTH)]
    small_keys = ("g1", "g2", "gn", "gs", "al", "dt")
    packed = _pack([stack(k) for k in small_keys] + [d_final[0], stack("cw"), stack("scw")], _packed_rows(full_shapes))
    sg = _unpack(_all_reduce_small(packed), full_shapes)
    sg[7] = lax.dynamic_slice_in_dim(sg[7], chip * (QKV // N_CHIPS), QKV // N_CHIPS, axis=2)
    sg[8] = lax.dynamic_slice_in_dim(sg[8], chip * (SC_WIDTH // N_CHIPS), SC_WIDTH // N_CHIPS, axis=2)
    small_names = ("norm1_g", "norm2_g", "dn_norm_g", "sc_norm_g", "dn_a_log", "dn_dt_bias", "final_norm_g",
                   "dn_conv_w", "sc_conv_w")
    sw = (norm1_g, norm2_g, dn_norm_g, sc_norm_g, dn_a_log, dn_dt_bias, final_norm_g, dn_conv_w, sc_conv_w)
    sm = (m_norm1_g, m_norm2_g, m_dn_norm_g, m_sc_norm_g, m_dn_a_log, m_dn_dt_bias, m_final_norm_g, m_dn_conv_w, m_sc_conv_w)
    sv = (v_norm1_g, v_norm2_g, v_dn_norm_g, v_sc_norm_g, v_dn_a_log, v_dn_dt_bias, v_final_norm_g, v_dn_conv_w, v_sc_conv_w)
    shard_shapes = [t.shape for t in sw]
    rows = _packed_rows(shard_shapes)
    outs = _adamw(_pack(sw, rows), _pack(sm, rows), _pack(sv, rows, fill=1.0), [_pack(sg, rows)], "adamw_small")
    small = {name: [] for name in small_names}
    for o in outs:
        for name, t in zip(small_names, _unpack(o, shard_shapes)):
            small[name].append(t)

    order = ("norm1_g", "w_in", "dn_conv_w", "dn_a_log", "dn_dt_bias", "dn_norm_g", "sc_conv_w", "sc_norm_g", "w_out",
             "norm2_g", "ffn_w_gate", "ffn_w_up", "ffn_w_down", "final_norm_g")
    result = {**big, **small}
    return (loss, dact[None], *[result[n][0] for n in order], *[result[n][1] for n in order],
            *[result[n][2] for n in order], *[result[n][3] for n in order])
```

```python
import jax
import jax.numpy as jnp
from jax import lax
from jax.experimental import pallas as pl
from jax.experimental.pallas import tpu as pltpu

F32 = jnp.float32
BF16 = jnp.bfloat16
MESH = pl.DeviceIdType.MESH

D_MODEL = 1024
DEPTH = 4
HEADS = 4
HEAD_DIM = 128
DN_WIDTH = HEADS * HEAD_DIM
SC_WIDTH = 512
SC_GROUPS = 4
D_FF = 2816
CHUNK = 64
QKV = 3 * DN_WIDTH
W_IN_COLS = 4 * DN_WIDTH + 2 * HEADS + 3 * SC_WIDTH
WA_COLS = QKV + DN_WIDTH + 3 * SC_WIDTH
LANES = 128
EPS = 1e-6
Q_SCALE = HEAD_DIM ** -0.5
N_CHIPS = 4
N_DEV = 8
IN_SHARD = W_IN_COLS // N_CHIPS
OUT_SHARD = D_MODEL // N_CHIPS
FF_SHARD = D_FF // N_CHIPS
A_OUT_AT = D_MODEL
A_ROWS = D_MODEL + OUT_SHARD
B_ROWS = 3 * FF_SHARD

ADAM_LR = 0.001
ADAM_B1 = 0.9
ADAM_B2 = 0.999
ADAM_EPS = 1e-08
ADAM_WD = 0.01
ADAM_STEP = 10

VMEM_LIMIT = 56 * 1024 * 1024

NN = (((1,), (0,)), ((), ()))
NT = (((1,), (1,)), ((), ()))
TN = (((0,), (0,)), ((), ()))


def _mm(a, b, dims=NN):
    return lax.dot_general(a.astype(BF16), b.astype(BF16), dims, preferred_element_type=F32)


def _mm32(a, b, dims=NN):
    return lax.dot_general(a, b, dims, preferred_element_type=F32, precision=lax.Precision.HIGHEST)


def _params(sem, vmem=VMEM_LIMIT):
    return pltpu.CompilerParams(dimension_semantics=sem, vmem_limit_bytes=vmem)


def _sigmoid(x):
    return 0.5 * jnp.tanh(0.5 * x) + 0.5


def _softplus(x):
    return jnp.maximum(x, 0.0) + jnp.log1p(jnp.exp(-jnp.abs(x)))


def _row_acc(acc_ref, val):
    acc_ref[0:1, :] += jnp.sum(val, axis=0, keepdims=True)


def _rms_bwd(dh, xh, r, gain):
    dxh = dh * gain
    return r * (dxh - xh * jnp.mean(dxh * xh, axis=-1, keepdims=True))


def _before_halo(tb):
    return lambda i: (jnp.maximum(i * (tb // 8) - 1, 0), 0)


def _after_halo(tb, n_rows):
    last = n_rows // 8 - 1
    return lambda i: (jnp.minimum((i + 1) * (tb // 8), last), 0)


def _rows_from(xc, offset, tb):
    part = offset % 8
    if part:
        xc = pltpu.roll(xc, xc.shape[0] - part, 0)
    return xc[offset - part:offset - part + tb, :]


def _taps(xc, w, n_taps, tb, first):
    out = w[0:1, :] * _rows_from(xc, first, tb)
    for j in range(1, n_taps):
        out = out + w[j:j + 1, :] * _rows_from(xc, first + j, tb)
    return out


P_SC = QKV
P_Z = P_SC + 3 * SC_WIDTH
P_BD = P_Z + DN_WIDTH
P_COLS = P_BD + LANES


def _in_proj(x, g1, wp):
    T = x.shape[0]
    tb = 256

    def body(x_ref, g_ref, wp_ref, qkv_ref, z_ref, sc_ref, bd_ref, ht_ref):
        xv = x_ref[...]
        r = lax.rsqrt(jnp.mean(xv * xv, axis=-1, keepdims=True) + EPS)
        h = (xv * r * g_ref[...]).astype(BF16)
        p = jnp.dot(h, wp_ref[...], preferred_element_type=F32)
        qkv_ref[...] = p[:, :P_SC]
        sc_ref[...] = p[:, P_SC:P_Z]
        z_ref[...] = p[:, P_Z:P_BD]
        bd_ref[...] = p[:, P_BD:]
        ht_ref[...] = h.T

    tok = lambda w: pl.BlockSpec((tb, w), lambda i: (i, 0))
    full = lambda a: pl.BlockSpec(a.shape, lambda i: (0, 0))
    return pl.pallas_call(
        body, name="in_proj", grid=(T // tb,),
        in_specs=[tok(D_MODEL), full(g1), full(wp)],
        out_specs=[tok(QKV), tok(DN_WIDTH), tok(3 * SC_WIDTH), tok(LANES),
                   pl.BlockSpec((D_MODEL, tb), lambda i: (0, i))],
        out_shape=[jax.ShapeDtypeStruct((T, QKV), F32), jax.ShapeDtypeStruct((T, DN_WIDTH), F32),
                   jax.ShapeDtypeStruct((T, 3 * SC_WIDTH), F32), jax.ShapeDtypeStruct((T, LANES), F32),
                   jax.ShapeDtypeStruct((D_MODEL, T), BF16)],
        compiler_params=_params(("parallel",)),
    )(x, g1, wp)


def _dp_block(tb, first, width, index=lambda i: i):
    assert first % width == 0
    return pl.BlockSpec((tb, width), lambda i: (index(i), first // width))


def _dn_act(pre, halo, cw, tb):
    xc = jnp.concatenate([halo, pre], axis=0)
    c = _taps(xc, cw, 4, tb, 5)
    sg = _sigmoid(c)
    return xc, c, sg, c * sg


def _gates(bd, al_row, dt_row):
    lane = lax.broadcasted_iota(jnp.int32, bd.shape, 1)
    beta = _sigmoid(bd)
    g = -jnp.exp(al_row) * _softplus(bd + dt_row)
    return jnp.where(lane < HEADS, beta, jnp.where(lane < 2 * HEADS, g, 0.0))


def _dn_prep(qkv, cw, bd, al_row, dt_row):
    T = qkv.shape[0]
    tb = 512

    def body(pre_ref, halo_ref, cw_ref, bd_ref, al_ref, dt_ref, q_ref, k_ref, v_ref, bg_ref):
        halo = jnp.where(pl.program_id(0) > 0, halo_ref[...], 0.0)
        _, _, _, a = _dn_act(pre_ref[...], halo, cw_ref[...], tb)
        for hh in range(HEADS):
            sl = slice(HEAD_DIM * hh, HEAD_DIM * (hh + 1))
            qs = a[:, sl]
            q_ref[:, sl] = qs * (lax.rsqrt(jnp.sum(qs * qs, axis=-1, keepdims=True) + EPS) * Q_SCALE)
            ks = a[:, DN_WIDTH + HEAD_DIM * hh:DN_WIDTH + HEAD_DIM * (hh + 1)]
            k_ref[:, sl] = ks * lax.rsqrt(jnp.sum(ks * ks, axis=-1, keepdims=True) + EPS)
        v_ref[...] = a[:, 2 * DN_WIDTH:]
        gates = _gates(bd_ref[...], al_ref[...], dt_ref[...])
        lane = lax.broadcasted_iota(jnp.int32, gates.shape, 1)
        bg_ref[...] = jnp.where(lane < HEADS, gates, _mm32(_chunk_cumsum_matrix(tb), gates))

    tok = lambda w: pl.BlockSpec((tb, w), lambda i: (i, 0))
    full = lambda a: pl.BlockSpec(a.shape, lambda i: (0, 0))
    return pl.pallas_call(
        body, name="dn_prep", grid=(T // tb,),
        in_specs=[tok(QKV), pl.BlockSpec((8, QKV), _before_halo(tb)), full(cw), tok(LANES), full(al_row), full(dt_row)],
        out_specs=[tok(DN_WIDTH), tok(DN_WIDTH), tok(DN_WIDTH), tok(LANES)],
        out_shape=[jax.ShapeDtypeStruct((T, DN_WIDTH), F32)] * 3 + [jax.ShapeDtypeStruct((T, LANES), F32)],
        compiler_params=_params(("parallel",)),
    )(qkv, qkv, cw, bd, al_row, dt_row)


def _chunk_masks():
    row = lax.broadcasted_iota(jnp.int32, (CHUNK, CHUNK), 0)
    col = lax.broadcasted_iota(jnp.int32, (CHUNK, CHUNK), 1)
    return row >= col, row > col


def _chunk_cumsum_matrix(n):
    row = lax.broadcasted_iota(jnp.int32, (n, n), 0)
    col = lax.broadcasted_iota(jnp.int32, (n, n), 1)
    return jnp.logical_and(row >= col, row // CHUNK == col // CHUNK).astype(F32)


def _chunk_units(q_ref, k_ref, v_ref, bg_ref, rows):
    bgc = bg_ref[rows, :]
    bg_t = bgc.T
    qv, kv, vv = q_ref[rows, :], k_ref[rows, :], v_ref[rows, :]
    units = []
    for h in range(HEADS):
        sl = slice(HEAD_DIM * h, HEAD_DIM * (h + 1))
        units.append((qv[:, sl], kv[:, sl], vv[:, sl], bgc[:, h:h + 1], bgc[:, HEADS + h:HEADS + h + 1],
                      bg_t[HEADS + h:HEADS + h + 1, :]))
    return units


def _units_local(units, masks):
    causal, strict = masks
    pre = []
    for q, k, v, beta, gc, gr in units:
        kb = k * beta
        eg = jnp.exp(gc)
        g_last = gc[CHUNK - 1:CHUNK, :]
        ek = jnp.exp(g_last - gc)
        pre.append(dict(q=q, k=k, v=v, beta=beta, decay=jnp.exp(jnp.where(causal, gc - gr, -1e30)), kb=kb, vb=v * beta,
                        eg=eg, kbg=kb * eg, ek=ek, gl=jnp.exp(g_last), q_dec=q * eg, k_dec=k * ek))
    both = [_mm(jnp.concatenate([p["kb"], p["q"]], axis=0), p["k"], NT) for p in pre]
    for p, b in zip(pre, both):
        p["low"] = jnp.where(strict, b[:CHUNK] * p["decay"], 0.0)
        p["qk"] = jnp.where(causal, b[CHUNK:] * p["decay"], 0.0)
    xs = [-p["low"] for p in pre]
    pw = [_mm(p["low"], p["low"]) for p in pre]
    for _ in range(4):
        both = [_mm(jnp.concatenate([pp, x], axis=0), pp) for pp, x in zip(pw, xs)]
        xs = [x + pp + b[CHUNK:] for x, pp, b in zip(xs, pw, both)]
        pw = [b[:CHUNK] for b in both]
    last = [_mm(x, pp) for x, pp in zip(xs, pw)]
    xs = [x + pp + b for x, pp, b in zip(xs, pw, last)]
    uw = [_mm(x, jnp.concatenate([p["vb"], p["kbg"]], axis=1)) for x, p in zip(xs, pre)]
    for p, x, b in zip(pre, xs, uw):
        p["xm"] = x
        p["u"] = p["vb"] + b[:, :HEAD_DIM]
        p["w"] = p["kbg"] + b[:, HEAD_DIM:]
    return pre


def _delta_fwd(q, k, v, bg):
    T = q.shape[0]
    tb = 512
    n_chunk = tb // CHUNK

    def body(q_ref, k_ref, v_ref, bg_ref, o_ref, st_ref, s_ref):
        @pl.when(pl.program_id(0) == 0)
        def _():
            s_ref[...] = jnp.zeros_like(s_ref)

        masks = _chunk_masks()

        def pair(pi, carry):
            rows = [pl.ds(pl.multiple_of((2 * pi + j) * CHUNK, CHUNK), CHUNK) for j in range(2)]
            loc = _units_local(_chunk_units(q_ref, k_ref, v_ref, bg_ref, rows[0])
                               + _chunk_units(q_ref, k_ref, v_ref, bg_ref, rows[1]), masks)
            states = [s_ref[h] for h in range(HEADS)]
            for j in range(2):
                lj = loc[HEADS * j:HEADS * (j + 1)]
                ws = [_mm(jnp.concatenate([p["w"], p["q_dec"]], axis=0), s) for p, s in zip(lj, states)]
                v_new = [p["u"] - b[:CHUNK] for p, b in zip(lj, ws)]
                intra = [_mm(p["qk"], vn) for p, vn in zip(lj, v_new)]
                upd = [_mm(p["k_dec"], vn, TN) for p, vn in zip(lj, v_new)]
                o_ref[rows[j], :] = jnp.concatenate([b[CHUNK:] + a for b, a in zip(ws, intra)], axis=1)
                for h in range(HEADS):
                    st_ref[2 * pi + j, h] = states[h]
                states = [p["gl"] * s + d for p, s, d in zip(lj, states, upd)]
            for h in range(HEADS):
                s_ref[h] = states[h]
            return carry

        lax.fori_loop(0, n_chunk // 2, pair, 0)

    tok = lambda w: pl.BlockSpec((tb, w), lambda i: (i, 0))
    return pl.pallas_call(
        body, name="delta_fwd", grid=(T // tb,),
        in_specs=[tok(DN_WIDTH), tok(DN_WIDTH), tok(DN_WIDTH), tok(LANES)],
        out_specs=[tok(DN_WIDTH), pl.BlockSpec((n_chunk, HEADS, HEAD_DIM, HEAD_DIM), lambda i: (i, 0, 0, 0))],
        out_shape=[jax.ShapeDtypeStruct((T, DN_WIDTH), F32),
                   jax.ShapeDtypeStruct((T // CHUNK, HEADS, HEAD_DIM, HEAD_DIM), F32)],
        scratch_shapes=[pltpu.VMEM((HEADS, HEAD_DIM, HEAD_DIM), F32)],
        compiler_params=_params(("arbitrary",)),
    )(q, k, v, bg)


def _dn_out(o, z, gn):
    outs, ohs, rs = [], [], []
    for hh in range(HEADS):
        oh = o[:, HEAD_DIM * hh:HEAD_DIM * (hh + 1)]
        r = lax.rsqrt(jnp.mean(oh * oh, axis=-1, keepdims=True) + EPS)
        ohs.append(oh * r)
        rs.append(r)
    sz = _sigmoid(z)
    oh = jnp.concatenate(ohs, axis=1)
    gn4 = jnp.concatenate([gn] * HEADS, axis=1)
    return oh * gn4 * (z * sz), oh, rs, sz, gn4


def _sc_fwd(sc_in, halo, cw, tb):
    xc = jnp.concatenate([halo, sc_in], axis=0)
    u = xc[:, SC_WIDTH:2 * SC_WIDTH] * xc[:, 2 * SC_WIDTH:]
    cv = _taps(u, cw, 3, tb, 6)
    gate_b = sc_in[:, :SC_WIDTH]
    y = gate_b * cv
    gw = SC_WIDTH // SC_GROUPS
    yhs, rs = [], []
    for gi in range(SC_GROUPS):
        yg = y[:, gw * gi:gw * (gi + 1)]
        r = lax.rsqrt(jnp.mean(yg * yg, axis=-1, keepdims=True) + EPS)
        yhs.append(yg * r)
        rs.append(r)
    return u, cv, gate_b, jnp.concatenate(yhs, axis=1), rs


def _shard_rows(land, first, rows):
    assert first % rows == 0 and land.shape[0] == N_CHIPS
    return pl.BlockSpec((N_CHIPS, rows, land.shape[2]), lambda i: (0, first // rows, 0))


def _whole(w_ref):
    n, rows, cols = w_ref.shape
    return w_ref[...].reshape(n * rows, cols)


def _mix_out(o, z, sc_in, x, land_a, gn, scw, gs):
    T = x.shape[0]
    tb = 256

    def body(o_ref, z_ref, sc_ref, halo_ref, x_ref, w_ref, gn_ref, scw_ref, gs_ref, x1_ref, mt_ref):
        o_n = _dn_out(o_ref[...], z_ref[...], gn_ref[...])[0]
        halo = jnp.where(pl.program_id(0) > 0, halo_ref[...], 0.0)
        yh = _sc_fwd(sc_ref[...], halo, scw_ref[...], tb)[3]
        mix = jnp.concatenate([o_n, yh * gs_ref[...]], axis=1).astype(BF16)
        x1_ref[...] = x_ref[...] + jnp.dot(mix, _whole(w_ref), preferred_element_type=F32)
        mt_ref[...] = mix.T

    tok = lambda w: pl.BlockSpec((tb, w), lambda i: (i, 0))
    full = lambda a: pl.BlockSpec(a.shape, lambda i: (0, 0))
    return pl.pallas_call(
        body, name="mix_out", grid=(T // tb,),
        in_specs=[tok(DN_WIDTH), tok(DN_WIDTH), tok(3 * SC_WIDTH), pl.BlockSpec((8, 3 * SC_WIDTH), _before_halo(tb)),
                  tok(D_MODEL), _shard_rows(land_a, A_OUT_AT, OUT_SHARD), full(gn), full(scw), full(gs)],
        out_specs=[tok(D_MODEL), pl.BlockSpec((D_MODEL, tb), lambda i: (0, i))],
        out_shape=[jax.ShapeDtypeStruct((T, D_MODEL), F32), jax.ShapeDtypeStruct((D_MODEL, T), BF16)],
        compiler_params=_params(("parallel",)),
    )(o, z, sc_in, sc_in, x, land_a, gn, scw, gs)


def _ffn(x1, g2, land_b):
    T = x1.shape[0]
    tb = 256

    def body(x_ref, g_ref, wgt_ref, wut_ref, wd_ref, x2_ref, a_ref, b_ref, h_ref):
        xv = x_ref[...]
        r = lax.rsqrt(jnp.mean(xv * xv, axis=-1, keepdims=True) + EPS)
        h = (xv * r * g_ref[...]).astype(BF16)
        a = lax.dot_general(h, _whole(wgt_ref), NT, preferred_element_type=F32)
        b = lax.dot_general(h, _whole(wut_ref), NT, preferred_element_type=F32)
        act = (a * _sigmoid(a) * b).astype(BF16)
        x2_ref[...] = xv + jnp.dot(act, _whole(wd_ref), preferred_element_type=F32)
        a_ref[...] = a.astype(BF16)
        b_ref[...] = b.astype(BF16)
        h_ref[...] = h

    tok = lambda w: pl.BlockSpec((tb, w), lambda i: (i, 0))
    return pl.pallas_call(
        body, name="ffn", grid=(T // tb,),
        in_specs=[tok(D_MODEL), pl.BlockSpec(g2.shape, lambda i: (0, 0)), _shard_rows(land_b, 0, FF_SHARD),
                  _shard_rows(land_b, FF_SHARD, FF_SHARD), _shard_rows(land_b, 2 * FF_SHARD, FF_SHARD)],
        out_specs=[tok(D_MODEL), tok(D_FF), tok(D_FF), tok(D_MODEL)],
        out_shape=[jax.ShapeDtypeStruct((T, D_MODEL), F32), jax.ShapeDtypeStruct((T, D_FF), BF16),
                   jax.ShapeDtypeStruct((T, D_FF), BF16), jax.ShapeDtypeStruct((T, D_MODEL), BF16)],
        compiler_params=_params(("parallel",)),
    )(x1, g2, land_b, land_b, land_b)


def _loss_head(x, gf, target):
    T = x.shape[0]
    tb = 512

    def body(x_ref, g_ref, t_ref, dx_ref, dxb_ref, loss_ref, dg_ref):
        @pl.when(pl.program_id(0) == 0)
        def _():
            loss_ref[...] = jnp.zeros_like(loss_ref)
            dg_ref[...] = jnp.zeros_like(dg_ref)

        xv = x_ref[...]
        r = lax.rsqrt(jnp.mean(xv * xv, axis=-1, keepdims=True) + EPS)
        xh = xv * r
        err = xh * g_ref[...] - t_ref[...]
        per_tok = jnp.mean(err * err, axis=-1, keepdims=True)
        loss_ref[...] += 0.5 * jnp.sum(per_tok, axis=0, keepdims=True)
        dy = err * (1.0 / D_MODEL)
        _row_acc(dg_ref, dy * xh)
        dx = _rms_bwd(dy, xh, r, g_ref[...])
        dx_ref[...] = dx
        dxb_ref[...] = dx.astype(BF16)

    tok = pl.BlockSpec((tb, D_MODEL), lambda i: (i, 0))
    return pl.pallas_call(
        body, name="loss_head", grid=(T // tb,),
        in_specs=[tok, pl.BlockSpec(gf.shape, lambda i: (0, 0)), tok],
        out_specs=[tok, tok, pl.BlockSpec((8, LANES), lambda i: (0, 0)), pl.BlockSpec((8, D_MODEL), lambda i: (0, 0))],
        out_shape=[jax.ShapeDtypeStruct((T, D_MODEL), F32), jax.ShapeDtypeStruct((T, D_MODEL), BF16),
                   jax.ShapeDtypeStruct((8, LANES), F32), jax.ShapeDtypeStruct((8, D_MODEL), F32)],
        compiler_params=_params(("arbitrary",)),
    )(x, gf, target)


def _ffn_bwd(dx2, x1, a, b, g2, land_b):
    T = x1.shape[0]
    tb = 256

    def body(dx2_ref, x_ref, a_ref, b_ref, g_ref, wgt_ref, wut_ref, wd_ref,
             dx1_ref, dx1b_ref, dat_ref, dbt_ref, at_ref, dg_ref):
        @pl.when(pl.program_id(0) == 0)
        def _():
            dg_ref[...] = jnp.zeros_like(dg_ref)

        dx2v = dx2_ref[...]
        av = a_ref[...].astype(F32)
        bv = b_ref[...].astype(F32)
        dact = _mm(dx2v, _whole(wd_ref), NT)
        sa = _sigmoid(av)
        silu = av * sa
        da = (dact * bv * (sa * (1.0 + av * (1.0 - sa)))).astype(BF16)
        db = (dact * silu).astype(BF16)
        dh = _mm(da, _whole(wgt_ref)) + _mm(db, _whole(wut_ref))
        xv = x_ref[...]
        r = lax.rsqrt(jnp.mean(xv * xv, axis=-1, keepdims=True) + EPS)
        xh = xv * r
        _row_acc(dg_ref, dh * xh)
        dx1 = dx2v + _rms_bwd(dh, xh, r, g_ref[...])
        dx1_ref[...] = dx1
        dx1b_ref[...] = dx1.astype(BF16)
        dat_ref[...] = da.T
        dbt_ref[...] = db.T
        at_ref[...] = (silu * bv).astype(BF16).T

    tok = lambda w: pl.BlockSpec((tb, w), lambda i: (i, 0))
    tr = pl.BlockSpec((D_FF, tb), lambda i: (0, i))
    return pl.pallas_call(
        body, name="ffn_bwd", grid=(T // tb,),
        in_specs=[tok(D_MODEL), tok(D_MODEL), tok(D_FF), tok(D_FF), pl.BlockSpec(g2.shape, lambda i: (0, 0)),
                  _shard_rows(land_b, 0, FF_SHARD), _shard_rows(land_b, FF_SHARD, FF_SHARD),
                  _shard_rows(land_b, 2 * FF_SHARD, FF_SHARD)],
        out_specs=[tok(D_MODEL), tok(D_MODEL), tr, tr, tr, pl.BlockSpec((8, D_MODEL), lambda i: (0, 0))],
        out_shape=[jax.ShapeDtypeStruct((T, D_MODEL), F32), jax.ShapeDtypeStruct((T, D_MODEL), BF16)]
        + [jax.ShapeDtypeStruct((D_FF, T), BF16)] * 3 + [jax.ShapeDtypeStruct((8, D_MODEL), F32)],
        compiler_params=_params(("arbitrary",)),
    )(dx2, x1, a, b, g2, land_b, land_b, land_b)


def _wgrad(at, b, bm, bn, name):
    M, T = at.shape
    N = b.shape[1]
    bk = min(T, 1024)

    def body(a_ref, b_ref, o_ref):
        @pl.when(pl.program_id(2) == 0)
        def _():
            o_ref[...] = jnp.zeros_like(o_ref)

        o_ref[...] += jnp.dot(a_ref[...], b_ref[...], preferred_element_type=F32)

    return pl.pallas_call(
        body, name=name, grid=(M // bm, N // bn, T // bk),
        in_specs=[pl.BlockSpec((bm, bk), lambda i, j, kk: (i, kk)), pl.BlockSpec((bk, bn), lambda i, j, kk: (kk, j))],
        out_specs=pl.BlockSpec((bm, bn), lambda i, j, kk: (i, j)),
        out_shape=jax.ShapeDtypeStruct((M, N), F32),
        compiler_params=_params(("parallel", "parallel", "arbitrary")),
    )(at, b)


def _wgrad_share(at, b, parts, first, name):
    M, T = at.shape
    rows = M // N_CHIPS
    assert first % rows == 0 and b.shape[1] == parts.shape[2]
    bk = min(T, 1024)
    n_k = T // bk

    def body(a_ref, b_ref, parts_ref, o_ref, acc_ref):
        kk = pl.program_id(1)

        @pl.when(kk == 0)
        def _():
            acc_ref[...] = jnp.zeros_like(acc_ref)

        acc_ref[...] += jnp.dot(a_ref[...], b_ref[...], preferred_element_type=F32)

        @pl.when(kk == n_k - 1)
        def _():
            o_ref[0] = acc_ref[...].astype(BF16)

    return pl.pallas_call(
        body, name=name, grid=(N_CHIPS, n_k),
        in_specs=[pl.BlockSpec((rows, bk), lambda i, kk: (i, kk)), pl.BlockSpec((bk, b.shape[1]), lambda i, kk: (kk, 0)), _ANY],
        out_specs=pl.BlockSpec((1, rows, b.shape[1]), lambda i, kk: (i, first // rows, 0)),
        out_shape=jax.ShapeDtypeStruct(parts.shape, BF16),
        scratch_shapes=[pltpu.VMEM((rows, b.shape[1]), F32)],
        input_output_aliases={2: 0},
        compiler_params=_params(("parallel", "arbitrary")),
    )(at, b, parts)


def _mix_out_bwd(dx1, o, z, sc_in, land_a, gn, scw, gs, dp):
    T = dx1.shape[0]
    tb = 256

    def body(dx_ref, o_ref, z_ref, sc_ref, halo_ref, w_ref, gn_ref, scw_ref, gs_ref, dp_ref,
             do_ref, dz_ref, dgb_ref, dcv_ref, dgn_ref, dgs_ref, dscw_ref):
        @pl.when(pl.program_id(0) == 0)
        def _():
            dgn_ref[...] = jnp.zeros_like(dgn_ref)
            dgs_ref[...] = jnp.zeros_like(dgs_ref)
            dscw_ref[...] = jnp.zeros_like(dscw_ref)

        dmix = _mm(dx_ref[...], _whole(w_ref), NT)
        don = dmix[:, :DN_WIDTH]
        dosc = dmix[:, DN_WIDTH:]
        zv = z_ref[...]
        _, oh, rs, sz, gn4 = _dn_out(o_ref[...], zv, gn_ref[...])
        silu_z = zv * sz
        dgn_full = don * oh * silu_z
        dgn_ref[0:1, :] += jnp.sum(sum(dgn_full[:, HEAD_DIM * hh:HEAD_DIM * (hh + 1)] for hh in range(HEADS)),
                                   axis=0, keepdims=True)
        dz_ref[...] = (don * oh * gn4 * (sz * (1.0 + zv * (1.0 - sz)))).astype(BF16)
        t = don * gn4 * silu_z
        for hh in range(HEADS):
            sl = slice(HEAD_DIM * hh, HEAD_DIM * (hh + 1))
            th, ohh = t[:, sl], oh[:, sl]
            do_ref[:, sl] = rs[hh] * (th - ohh * jnp.mean(th * ohh, axis=-1, keepdims=True))
        halo = jnp.where(pl.program_id(0) > 0, halo_ref[...], 0.0)
        u, cv, gate_b, yh, rys = _sc_fwd(sc_ref[...], halo, scw_ref[...], tb)
        _row_acc(dgs_ref, dosc * yh)
        ty = dosc * gs_ref[...]
        gw = SC_WIDTH // SC_GROUPS
        dys = []
        for gi in range(SC_GROUPS):
            sl = slice(gw * gi, gw * (gi + 1))
            tg, yg = ty[:, sl], yh[:, sl]
            dys.append(rys[gi] * (tg - yg * jnp.mean(tg * yg, axis=-1, keepdims=True)))
        dy = jnp.concatenate(dys, axis=1)
        dgb_ref[...] = dy * cv
        dcv = dy * gate_b
        dcv_ref[...] = dcv
        for j in range(3):
            dscw_ref[j:j + 1, :] += jnp.sum(dcv * _rows_from(u, 6 + j, tb), axis=0, keepdims=True)

    tok = lambda w: pl.BlockSpec((tb, w), lambda i: (i, 0))
    full = lambda t: pl.BlockSpec(t.shape, lambda i: (0, 0))
    acc = lambda w: pl.BlockSpec((8, w), lambda i: (0, 0))
    return pl.pallas_call(
        body, name="mix_out_bwd", grid=(T // tb,),
        in_specs=[tok(D_MODEL), tok(DN_WIDTH), tok(DN_WIDTH), tok(3 * SC_WIDTH),
                  pl.BlockSpec((8, 3 * SC_WIDTH), _before_halo(tb)), _shard_rows(land_a, A_OUT_AT, OUT_SHARD),
                  full(gn), full(scw), full(gs), _ANY],
        out_specs=[tok(DN_WIDTH), _dp_block(tb, P_Z, DN_WIDTH), tok(SC_WIDTH), tok(SC_WIDTH),
                   acc(HEAD_DIM), acc(SC_WIDTH), acc(SC_WIDTH)],
        out_shape=[jax.ShapeDtypeStruct((T, DN_WIDTH), F32), jax.ShapeDtypeStruct(dp.shape, BF16),
                   jax.ShapeDtypeStruct((T, SC_WIDTH), F32), jax.ShapeDtypeStruct((T, SC_WIDTH), F32),
                   jax.ShapeDtypeStruct((8, HEAD_DIM), F32), jax.ShapeDtypeStruct((8, SC_WIDTH), F32),
                   jax.ShapeDtypeStruct((8, SC_WIDTH), F32)],
        input_output_aliases={9: 1},
        compiler_params=_params(("arbitrary",)),
    )(dx1, o, z, sc_in, sc_in, land_a, gn, scw, gs, dp)


def _sc_conv_bwd(dcv, dgb, sc_in, scw, dp):
    T = dcv.shape[0]
    tb = 512

    def body(dcv_ref, halo_ref, dgb_ref, sc_ref, w_ref, dp_ref, out_ref):
        last = pl.program_id(0) == pl.num_programs(0) - 1
        halo = jnp.where(last, 0.0, halo_ref[...])
        xc = jnp.concatenate([dcv_ref[...], halo], axis=0)
        w = w_ref[...]
        du = w[2:3, :] * xc[0:tb, :] + w[1:2, :] * _rows_from(xc, 1, tb) + w[0:1, :] * _rows_from(xc, 2, tb)
        sc = sc_ref[...]
        out_ref[:, :SC_WIDTH] = dgb_ref[...].astype(BF16)
        out_ref[:, SC_WIDTH:2 * SC_WIDTH] = (du * sc[:, 2 * SC_WIDTH:]).astype(BF16)
        out_ref[:, 2 * SC_WIDTH:] = (du * sc[:, SC_WIDTH:2 * SC_WIDTH]).astype(BF16)

    tok = lambda w: pl.BlockSpec((tb, w), lambda i: (i, 0))
    return pl.pallas_call(
        body, name="sc_conv_bwd", grid=(T // tb,),
        in_specs=[tok(SC_WIDTH), pl.BlockSpec((8, SC_WIDTH), _after_halo(tb, T)), tok(SC_WIDTH), tok(3 * SC_WIDTH),
                  pl.BlockSpec(scw.shape, lambda i: (0, 0)), _ANY],
        out_specs=_dp_block(tb, P_SC, 3 * SC_WIDTH),
        out_shape=jax.ShapeDtypeStruct(dp.shape, BF16),
        input_output_aliases={5: 0},
        compiler_params=_params(("parallel",)),
    )(dcv, dcv, dgb, sc_in, scw, dp)


def _delta_bwd(q, k, v, bg, states, do):
    T = q.shape[0]
    tb = 512
    n_chunk = tb // CHUNK
    nb = T // tb

    def body(q_ref, k_ref, v_ref, bg_ref, st_ref, do_ref, dq_ref, dk_ref, dv_ref, dbg_ref, ds_ref):
        @pl.when(pl.program_id(0) == 0)
        def _():
            ds_ref[...] = jnp.zeros_like(ds_ref)

        masks = _chunk_masks()
        causal, strict = masks
        lane = lax.broadcasted_iota(jnp.int32, (CHUNK, LANES), 1)
        last_row = lax.broadcasted_iota(jnp.int32, (CHUNK, 1), 0) == CHUNK - 1
        cat = jnp.concatenate
        heads = range(HEADS)

        def open_chunk(ci, loc):
            rows = pl.ds(pl.multiple_of(ci * CHUNK, CHUNK), CHUNK)
            dov = do_ref[rows, :]
            return dict(rows=rows, loc=loc, do=[dov[:, HEAD_DIM * h:HEAD_DIM * (h + 1)] for h in heads],
                        state=[st_ref[ci, h] for h in heads])

        def a_free(c):
            loc, do, state = c["loc"], c["do"], c["state"]
            w_s = [_mm(p["w"], s) for p, s in zip(loc, state)]
            c["dq_dec"] = [_mm(d, s, NT) for d, s in zip(do, state)]
            c["qk_do"] = [_mm(p["qk"], d, TN) for p, d in zip(loc, do)]
            c["qd_do"] = [_mm(p["q_dec"], d, TN) for p, d in zip(loc, do)]
            c["v_new"] = [p["u"] - t for p, t in zip(loc, w_s)]
            c["dqk"] = [jnp.where(causal, _mm(d, vn, NT), 0.0) for d, vn in zip(do, c["v_new"])]

        def a_state(c, ds_next):
            c["ds_next"] = ds_next
            kd_ds = [_mm(p["k_dec"], d) for p, d in zip(c["loc"], ds_next)]
            c["dk_dec"] = [_mm(vn, d, NT) for vn, d in zip(c["v_new"], ds_next)]
            c["dv_new"] = [a + b for a, b in zip(c["qk_do"], kd_ds)]

        def b_state(c):
            loc = c["loc"]
            w_dv = [_mm(p["w"], dvn, TN) for p, dvn in zip(loc, c["dv_new"])]
            c["dw"] = [-_mm(dvn, s, NT) for dvn, s in zip(c["dv_new"], c["state"])]
            return [loc[h]["gl"] * c["ds_next"][h] + c["qd_do"][h] - w_dv[h] for h in heads]

        def c_solve(c):
            loc, dv_new, dw = c["loc"], c["dv_new"], c["dw"]
            c["dtm"] = [_mm(cat([dvn, d], axis=1), cat([p["vb"], p["kbg"]], axis=1), NT) for dvn, d, p in zip(dv_new, dw, loc)]
            x_t = [_mm(p["xm"], cat([dvn, d], axis=1), TN) for p, dvn, d in zip(loc, dv_new, dw)]
            c["dvb"] = [dvn + t[:, :HEAD_DIM] for dvn, t in zip(dv_new, x_t)]
            c["dkbg"] = [d + t[:, HEAD_DIM:] for d, t in zip(dw, x_t)]

        def d_solve(c):
            c["y"] = [t + _mm(p["xm"], t, TN) for p, t in zip(c["loc"], c["dtm"])]

        def e_solve(c):
            c["dlow"] = [jnp.where(strict, -(t + _mm(t, p["xm"], NT)), 0.0) for p, t in zip(c["loc"], c["y"])]

        def f_close(c):
            loc, rows = c["loc"], c["rows"]
            dmm = [d * p["decay"] for d, p in zip(c["dlow"], loc)]
            dnn = [d * p["decay"] for d, p in zip(c["dqk"], loc)]
            by_k = [_mm(cat([a, b], axis=0), p["k"]) for a, b, p in zip(dmm, dnn, loc)]
            dk_mm = [_mm(cat([a, b], axis=0), cat([p["kb"], p["q"]], axis=0), TN) for a, b, p in zip(dmm, dnn, loc)]
            dq_out, dk_out, dv_out = [], [], []
            dbeta_all = jnp.zeros((CHUNK, LANES), F32)
            dgc_all = jnp.zeros((CHUNK, LANES), F32)
            for h in heads:
                p = loc[h]
                dkb = by_k[h][:CHUNK] + c["dkbg"][h] * p["eg"]
                dq_out.append(by_k[h][CHUNK:] + c["dq_dec"][h] * p["eg"])
                dk_out.append(dk_mm[h] + c["dk_dec"][h] * p["ek"] + dkb * p["beta"])
                dv_out.append(c["dvb"][h] * p["beta"])
                dbeta = jnp.sum(dkb * p["k"] + c["dvb"][h] * p["v"], axis=1, keepdims=True)
                e = c["dlow"][h] * p["low"] + c["dqk"][h] * p["qk"]
                kd = jnp.sum(c["dk_dec"][h] * p["k_dec"], axis=1, keepdims=True)
                dgc = (jnp.sum(e, axis=1, keepdims=True) - jnp.sum(e.T, axis=1, keepdims=True)
                       + jnp.sum(c["dq_dec"][h] * p["q_dec"], axis=1, keepdims=True) - kd
                       + jnp.sum(c["dkbg"][h] * p["kbg"], axis=1, keepdims=True))
                dgl = jnp.sum(jnp.sum(c["ds_next"][h] * c["state"][h], axis=1, keepdims=True), axis=0, keepdims=True)
                d_last = jnp.sum(kd, axis=0, keepdims=True) + dgl * p["gl"]
                dgc = dgc + jnp.where(last_row, d_last, 0.0)
                dbeta_all = jnp.where(lane == h, dbeta, dbeta_all)
                dgc_all = jnp.where(lane == h + HEADS, dgc, dgc_all)
            dq_ref[rows, :] = cat(dq_out, axis=1)
            dk_ref[rows, :] = cat(dk_out, axis=1)
            dv_ref[rows, :] = cat(dv_out, axis=1)
            dbg_ref[rows, :] = dbeta_all + dgc_all

        def pair(pj, carry):
            hi = n_chunk - 1 - 2 * pj
            lo = hi - 1
            rows = [pl.ds(pl.multiple_of(ci * CHUNK, CHUNK), CHUNK) for ci in (hi, lo)]
            loc = _units_local(_chunk_units(q_ref, k_ref, v_ref, bg_ref, rows[0])
                               + _chunk_units(q_ref, k_ref, v_ref, bg_ref, rows[1]), masks)
            c_hi, c_lo = open_chunk(hi, loc[:HEADS]), open_chunk(lo, loc[HEADS:])
            a_free(c_hi)
            a_free(c_lo)
            a_state(c_hi, [ds_ref[h] for h in heads])
            ds_mid = b_state(c_hi)
            a_state(c_lo, ds_mid)
            c_solve(c_hi)
            ds_out = b_state(c_lo)
            for h in heads:
                ds_ref[h] = ds_out[h]
            d_solve(c_hi)
            c_solve(c_lo)
            e_solve(c_hi)
            d_solve(c_lo)
            f_close(c_hi)
            e_solve(c_lo)
            f_close(c_lo)
            return carry

        lax.fori_loop(0, n_chunk // 2, pair, 0)

    tok = lambda w: pl.BlockSpec((tb, w), lambda i: (nb - 1 - i, 0))
    return pl.pallas_call(
        body, name="delta_bwd", grid=(nb,),
        in_specs=[tok(DN_WIDTH), tok(DN_WIDTH), tok(DN_WIDTH), tok(LANES),
                  pl.BlockSpec((n_chunk, HEADS, HEAD_DIM, HEAD_DIM), lambda i: (nb - 1 - i, 0, 0, 0)), tok(DN_WIDTH)],
        out_specs=[tok(DN_WIDTH), tok(DN_WIDTH), tok(DN_WIDTH), tok(LANES)],
        out_shape=[jax.ShapeDtypeStruct((T, DN_WIDTH), F32)] * 3 + [jax.ShapeDtypeStruct((T, LANES), F32)],
        scratch_shapes=[pltpu.VMEM((HEADS, HEAD_DIM, HEAD_DIM), F32)],
        compiler_params=_params(("arbitrary",)),
    )(q, k, v, bg, states, do)


def _dn_prep_bwd(dq, dk, dv, dbg, qkv, cw, bd, al_row, dt_row, dp):
    T = qkv.shape[0]
    tb = 256

    def body(dq_ref, dk_ref, dv_ref, dbg_ref, pre_ref, halo_ref, cw_ref, bd_ref, al_ref, dt_ref, dp_ref,
             dc_ref, dbd_ref, dcw_ref, dal_ref, ddt_ref):
        @pl.when(pl.program_id(0) == 0)
        def _():
            dcw_ref[...] = jnp.zeros_like(dcw_ref)
            dal_ref[...] = jnp.zeros_like(dal_ref)
            ddt_ref[...] = jnp.zeros_like(ddt_ref)

        halo = jnp.where(pl.program_id(0) > 0, halo_ref[...], 0.0)
        xc, c, sg, a = _dn_act(pre_ref[...], halo, cw_ref[...], tb)
        dsilu = sg * (1.0 + c * (1.0 - sg))
        for hh in range(HEADS):
            sl = slice(HEAD_DIM * hh, HEAD_DIM * (hh + 1))
            for base, g_ref, scale in ((0, dq_ref, Q_SCALE), (DN_WIDTH, dk_ref, 1.0)):
                sa = slice(base + HEAD_DIM * hh, base + HEAD_DIM * (hh + 1))
                raw = a[:, sa]
                r = lax.rsqrt(jnp.sum(raw * raw, axis=-1, keepdims=True) + EPS)
                nrm = raw * r
                gn_ = g_ref[:, sl] * scale
                dc_ref[:, sa] = r * (gn_ - nrm * jnp.sum(gn_ * nrm, axis=-1, keepdims=True)) * dsilu[:, sa]
        dc_ref[:, 2 * DN_WIDTH:] = dv_ref[...] * dsilu[:, 2 * DN_WIDTH:]
        dc = dc_ref[...]
        for j in range(4):
            dcw_ref[j:j + 1, :] += jnp.sum(dc * _rows_from(xc, 5 + j, tb), axis=0, keepdims=True)
        bdv = bd_ref[...]
        lane = lax.broadcasted_iota(jnp.int32, bdv.shape, 1)
        is_b = lane < HEADS
        dbg_in = dbg_ref[...]
        dbgv = jnp.where(is_b, dbg_in, _mm32(_chunk_cumsum_matrix(tb), dbg_in, TN))
        is_g = jnp.logical_and(lane >= HEADS, lane < 2 * HEADS)
        beta = _sigmoid(bdv)
        neg_a = -jnp.exp(al_ref[...])
        pre_sp = bdv + dt_ref[...]
        g = neg_a * _softplus(pre_sp)
        da_in = dbgv * neg_a * _sigmoid(pre_sp)
        dbd_ref[...] = jnp.where(is_b, dbgv * beta * (1.0 - beta), jnp.where(is_g, da_in, 0.0)).astype(BF16)
        _row_acc(dal_ref, jnp.where(is_g, dbgv * g, 0.0))
        _row_acc(ddt_ref, jnp.where(is_g, da_in, 0.0))

    tok = lambda w: pl.BlockSpec((tb, w), lambda i: (i, 0))
    full = lambda t: pl.BlockSpec(t.shape, lambda i: (0, 0))
    acc = lambda w: pl.BlockSpec((8, w), lambda i: (0, 0))
    return pl.pallas_call(
        body, name="dn_prep_bwd", grid=(T // tb,),
        in_specs=[tok(DN_WIDTH), tok(DN_WIDTH), tok(DN_WIDTH), tok(LANES),
                  tok(QKV), pl.BlockSpec((8, QKV), _before_halo(tb)), full(cw), tok(LANES), full(al_row), full(dt_row), _ANY],
        out_specs=[tok(QKV), _dp_block(tb, P_BD, LANES), acc(QKV), acc(LANES), acc(LANES)],
        out_shape=[jax.ShapeDtypeStruct((T, QKV), F32), jax.ShapeDtypeStruct(dp.shape, BF16),
                   jax.ShapeDtypeStruct((8, QKV), F32), jax.ShapeDtypeStruct((8, LANES), F32),
                   jax.ShapeDtypeStruct((8, LANES), F32)],
        input_output_aliases={10: 1},
        compiler_params=_params(("arbitrary",)),
    )(dq, dk, dv, dbg, qkv, qkv, cw, bd, al_row, dt_row, dp)


def _dn_conv_bwd(dc, cw, dp):
    T = dc.shape[0]
    tb = 512

    def body(dc_ref, halo_ref, w_ref, dp_ref, out_ref):
        last = pl.program_id(0) == pl.num_programs(0) - 1
        halo = jnp.where(last, 0.0, halo_ref[...])
        xc = jnp.concatenate([dc_ref[...], halo], axis=0)
        w = w_ref[...]
        acc = w[3:4, :] * xc[0:tb, :]
        for j in range(3):
            acc = acc + w[j:j + 1, :] * _rows_from(xc, 3 - j, tb)
        out_ref[...] = acc.astype(BF16)

    tok = pl.BlockSpec((tb, QKV), lambda i: (i, 0))
    return pl.pallas_call(
        body, name="dn_conv_bwd", grid=(T // tb,),
        in_specs=[tok, pl.BlockSpec((8, QKV), _after_halo(tb, T)), pl.BlockSpec(cw.shape, lambda i: (0, 0)), _ANY],
        out_specs=_dp_block(tb, 0, QKV),
        out_shape=jax.ShapeDtypeStruct(dp.shape, BF16),
        input_output_aliases={3: 0},
        compiler_params=_params(("parallel",)),
    )(dc, dc, cw, dp)


def _in_proj_bwd(dp, dx1, x, g1, wp):
    T = x.shape[0]
    tb = 256

    def body(dp_ref, dx1_ref, x_ref, g_ref, wp_ref, dx_ref, dxb_ref, dg_ref):
        @pl.when(pl.program_id(0) == 0)
        def _():
            dg_ref[...] = jnp.zeros_like(dg_ref)

        dh = lax.dot_general(dp_ref[...], wp_ref[...], NT, preferred_element_type=F32)
        xv = x_ref[...]
        r = lax.rsqrt(jnp.mean(xv * xv, axis=-1, keepdims=True) + EPS)
        xh = xv * r
        _row_acc(dg_ref, dh * xh)
        dx = dx1_ref[...] + _rms_bwd(dh, xh, r, g_ref[...])
        dx_ref[...] = dx
        dxb_ref[...] = dx.astype(BF16)

    tok = lambda w: pl.BlockSpec((tb, w), lambda i: (i, 0))
    full = lambda t: pl.BlockSpec(t.shape, lambda i: (0, 0))
    return pl.pallas_call(
        body, name="in_proj_bwd", grid=(T // tb,),
        in_specs=[tok(P_COLS), tok(D_MODEL), tok(D_MODEL), full(g1), full(wp)],
        out_specs=[tok(D_MODEL), tok(D_MODEL), pl.BlockSpec((8, D_MODEL), lambda i: (0, 0))],
        out_shape=[jax.ShapeDtypeStruct((T, D_MODEL), F32), jax.ShapeDtypeStruct((T, D_MODEL), BF16),
                   jax.ShapeDtypeStruct((8, D_MODEL), F32)],
        compiler_params=_params(("arbitrary",)),
    )(dp, dx1, x, g1, wp)


def _pad_rows(a, rows=8):
    return jnp.pad(a, ((0, rows - a.shape[0]), (0, 0)))


def _gate_rows(a_log, dt_bias):
    put = lambda t: jnp.pad(t.reshape(1, HEADS), ((0, 0), (HEADS, LANES - 2 * HEADS)))
    return put(a_log), put(dt_bias)


W_Z = QKV
W_BD = W_Z + DN_WIDTH
W_SC = W_BD + 2 * HEADS


def _projection_of(w_in):
    return jnp.concatenate([w_in[:, :W_Z], w_in[:, W_SC:], w_in[:, W_Z:W_BD],
                            jnp.pad(w_in[:, W_BD:W_SC], ((0, 0), (0, LANES - 2 * HEADS)))], axis=1)


def _w_in_grad_of(g_wp):
    return jnp.concatenate([g_wp[:, :P_SC], g_wp[:, P_Z:P_BD], g_wp[:, P_BD:P_BD + 2 * HEADS], g_wp[:, P_SC:P_Z]], axis=1)


def _mixer_fwd(x, p):
    qkv, z, sc_in, bd, ht = _in_proj(x, p["g1"], p["wp"])
    q, k, v, bg = _dn_prep(qkv, p["cw"], bd, p["al"], p["dt"])
    o, states = _delta_fwd(q, k, v, bg)
    x1, mt = _mix_out(o, z, sc_in, x, p["land_a"], p["gn"], p["scw"], p["gs"])
    return x1, dict(x=x, qkv=qkv, z=z, sc_in=sc_in, bd=bd, ht=ht, q=q, k=k, v=v, bg=bg, o=o, states=states, mt=mt)


def _ffn_fwd(x1, p, land_b):
    x2, a, b, h2 = _ffn(x1, p["g2"], land_b)
    return x2, dict(x1=x1, a=a, b=b, h2=h2)


def _ffn_back(dx2, dx2_bf16, s, p, land_b):
    dx1, dx1_bf16, da_t, db_t, act_t, dg2 = _ffn_bwd(dx2, s["x1"], s["a"], s["b"], p["g2"], land_b)
    parts = lax.empty((N_CHIPS, B_ROWS, D_MODEL), BF16)
    parts = _wgrad_share(act_t, dx2_bf16, parts, 2 * FF_SHARD, "wgrad_down")
    parts = _wgrad_share(da_t, s["h2"], parts, 0, "wgrad_gate")
    parts = _wgrad_share(db_t, s["h2"], parts, FF_SHARD, "wgrad_up")
    return dx1, dx1_bf16, parts, dg2[0]


def _mixer_bwd(dx1, dx1_bf16, s, p):
    dp = lax.empty((dx1.shape[0], P_COLS), BF16)
    do, dp, dgb, dcv, dgn, dgs, dscw = _mix_out_bwd(dx1, s["o"], s["z"], s["sc_in"], p["land_a"], p["gn"], p["scw"], p["gs"], dp)
    dp = _sc_conv_bwd(dcv, dgb, s["sc_in"], p["scw"], dp)
    dq, dk, dv, dbg = _delta_bwd(s["q"], s["k"], s["v"], s["bg"], s["states"], do)
    dc, dp, dcw, dal, ddt = _dn_prep_bwd(dq, dk, dv, dbg, s["qkv"], p["cw"], s["bd"], p["al"], p["dt"], dp)
    dp = _dn_conv_bwd(dc, p["cw"], dp)
    dx, dx_bf16, dg1 = _in_proj_bwd(dp, dx1, s["x"], p["g1"], p["wp"])
    g_w_in = _w_in_grad_of(_wgrad(s["ht"], dp, 512, P_COLS, "wgrad_in"))
    cols = jnp.moveaxis(g_w_in.reshape(D_MODEL, N_CHIPS, IN_SHARD), 1, 0)
    parts = jnp.pad(cols, ((0, 0), (0, OUT_SHARD), (0, D_MODEL - IN_SHARD))).astype(BF16)
    parts = _wgrad_share(s["mt"], dx1_bf16, parts, A_OUT_AT, "wgrad_out")
    g = dict(g1=dg1[0], gn=dgn[0], gs=dgs[0], scw=dscw[:3], cw=dcw[:4], al=dal[0, HEADS:2 * HEADS], dt=ddt[0, HEADS:2 * HEADS])
    return dx, dx_bf16, parts, g


def _place():
    return lax.axis_index("x"), lax.axis_index("y"), lax.axis_index("c")


def _other_chips(x, y):
    return [(1 - x, y), (x, 1 - y), (1 - x, 1 - y)]


_HBM = pl.BlockSpec(memory_space=pltpu.HBM)


def _chip_exchange(arrs, name, gather):
    n = len(arrs)

    def body(*refs):
        ins, outs = refs[:n], refs[n:2 * n]
        send_sems, recv_sems, local_sems = refs[2 * n:]
        x, y, c = _place()
        me = 2 * x + y
        others = _other_chips(x, y)

        def remote(k, j, landing):
            px, py = others[j]
            src = ins[k] if gather else ins[k].at[2 * px + py]
            return pltpu.make_async_remote_copy(src_ref=src, dst_ref=outs[k].at[landing], send_sem=send_sems.at[k, j],
                                                recv_sem=recv_sems.at[k, j], device_id=(px, py, c), device_id_type=MESH)

        local = [pltpu.make_async_copy(ins[k] if gather else ins[k].at[me], outs[k].at[me], local_sems.at[k])
                 for k in range(n)]
        sends = [remote(k, j, me) for k in range(n) for j in range(3)]
        for cp in local + sends:
            cp.start()
        for k in range(n):
            for j, (px, py) in enumerate(others):
                remote(k, j, 2 * px + py).wait_recv()
        for cp in sends:
            cp.wait_send()
        for cp in local:
            cp.wait()

    shapes = [jax.ShapeDtypeStruct(((N_CHIPS,) + a.shape) if gather else a.shape, a.dtype) for a in arrs]
    return pl.pallas_call(
        body, name=name, in_specs=[_HBM] * n, out_specs=[_HBM] * n, out_shape=shapes,
        scratch_shapes=[pltpu.SemaphoreType.DMA((n, 3)), pltpu.SemaphoreType.DMA((n, 3)), pltpu.SemaphoreType.DMA((n,))],
    )(*arrs)


_SEM = pl.BlockSpec(memory_space=pltpu.SEMAPHORE)
_ANY = pl.BlockSpec(memory_space=pl.ANY)
_EFFECT = pltpu.SideEffectType.DATAFLOW_SIDE_EFFECTING


def _split_copies(src_ref, land_ref, send_sems, recv_sems, gather, sending):
    x, y, c = _place()
    me = 2 * x + y
    copies = []
    for j, (px, py) in enumerate(_other_chips(x, y)):
        peer = 2 * px + py
        copies.append(pltpu.make_async_remote_copy(
            src_ref=src_ref if gather else src_ref.at[peer], dst_ref=land_ref.at[me if sending else peer],
            send_sem=send_sems.at[j], recv_sem=recv_sems.at[j], device_id=(px, py, c), device_id_type=MESH))
    return copies


def _own_slot(share):
    chip = 2 * lax.axis_index("x") + lax.axis_index("y")
    return lax.dynamic_update_slice(lax.empty((N_CHIPS,) + share.shape, share.dtype), share[None], (chip, 0, 0))


def _exchange_start(src, land, after, name, gather):
    def body(src_ref, land_ref, after_ref, send_sems, recv_sems, src_thru, land_thru, token):
        for cp in _split_copies(src_ref, land_ref, send_sems, recv_sems, gather, sending=True):
            cp.start()
        token[...] = jnp.zeros_like(token)

    hbm = lambda t: pltpu.with_memory_space_constraint(t, pltpu.HBM)
    return pl.pallas_call(
        body, name=name,
        out_shape=(pltpu.SemaphoreType.DMA((3,)), pltpu.SemaphoreType.DMA((3,)), pltpu.HBM(src.shape, src.dtype),
                   pltpu.HBM(land.shape, land.dtype), jax.ShapeDtypeStruct((8, LANES), F32)),
        in_specs=(_HBM, _HBM, _ANY), out_specs=(_SEM, _SEM, _HBM, _HBM, pl.BlockSpec(memory_space=pltpu.VMEM)),
        input_output_aliases={0: 2, 1: 3},
        compiler_params=pltpu.CompilerParams(has_side_effects=_EFFECT),
    )(hbm(src), hbm(land), after)


def _exchange_wait(started, after, name, gather):
    send_sems, recv_sems, src_thru, land_thru, _ = started

    def body(src_ref, land_ref, send_sems, recv_sems, after_ref, src_dead, got_ref):
        for cp in _split_copies(src_ref, land_ref, send_sems, recv_sems, gather, sending=False):
            cp.wait_send()
            cp.wait_recv()

    return pl.pallas_call(
        body, name=name,
        out_shape=(pltpu.HBM(src_thru.shape, src_thru.dtype), pltpu.HBM(land_thru.shape, land_thru.dtype)),
        in_specs=(_HBM, _HBM, _SEM, _SEM, _ANY), out_specs=(_HBM, _HBM), input_output_aliases={0: 0, 1: 1},
        compiler_params=pltpu.CompilerParams(has_side_effects=_EFFECT),
    )(src_thru, land_thru, send_sems, recv_sems, after)[1]


def _swap_sibling(arrs, name):
    n = len(arrs)

    def body(*refs):
        ins, outs = refs[:n], refs[n:2 * n]
        send_sems, recv_sems = refs[2 * n:]
        x, y, c = _place()
        copies = [pltpu.make_async_remote_copy(src_ref=ins[k], dst_ref=outs[k], send_sem=send_sems.at[k],
                                               recv_sem=recv_sems.at[k], device_id=(x, y, 1 - c), device_id_type=MESH)
                  for k in range(n)]
        for cp in copies:
            cp.start()
        for cp in copies:
            cp.wait()

    return pl.pallas_call(
        body, name=name, in_specs=[_HBM] * n, out_specs=[_HBM] * n,
        out_shape=[jax.ShapeDtypeStruct(a.shape, a.dtype) for a in arrs],
        scratch_shapes=[pltpu.SemaphoreType.DMA((n,)), pltpu.SemaphoreType.DMA((n,))],
    )(*arrs)


def _all_reduce_small(v):
    rows = v.shape[0]
    flips = [(a, b, cc) for a in (0, 1) for b in (0, 1) for cc in (0, 1)][1:]

    def body(v_ref, out_ref, buf_ref, send_sems, recv_sems):
        x, y, c = _place()
        me = 4 * x + 2 * y + c
        peers = [((1 - x) if a else x, (1 - y) if b else y, (1 - c) if cc else c) for a, b, cc in flips]

        def copy(j, landing):
            return pltpu.make_async_remote_copy(src_ref=v_ref, dst_ref=buf_ref.at[landing], send_sem=send_sems.at[j],
                                                recv_sem=recv_sems.at[j], device_id=peers[j], device_id_type=MESH)

        sends = [copy(j, me) for j in range(N_DEV - 1)]
        for cp in sends:
            cp.start()
        buf_ref[me] = v_ref[...]
        for j, (px, py, pc) in enumerate(peers):
            copy(j, 4 * px + 2 * py + pc).wait_recv()
        for cp in sends:
            cp.wait_send()
        acc = buf_ref[0]
        for d in range(1, N_DEV):
            acc = acc + buf_ref[d]
        out_ref[...] = acc

    vmem = pl.BlockSpec(memory_space=pltpu.VMEM)
    return pl.pallas_call(
        body, name="all_reduce_small", in_specs=[vmem], out_specs=vmem,
        out_shape=jax.ShapeDtypeStruct(v.shape, F32),
        scratch_shapes=[pltpu.VMEM((N_DEV, rows, LANES), F32), pltpu.SemaphoreType.DMA((N_DEV - 1,)),
                        pltpu.SemaphoreType.DMA((N_DEV - 1,))],
    )(v)


def _row_block(*sizes):
    return next(t for t in (256, 192, 128, 64) if all(s % t == 0 for s in sizes))


def _sum_chips(parts, name):
    _, rows, cols = parts[0].shape
    n = len(parts)
    tr = _row_block(rows)

    def body(*refs):
        o_ref = refs[n]
        for l in range(n):
            @pl.when(pl.program_id(0) == l)
            def _(p_ref=refs[l]):
                acc = p_ref[0].astype(F32)
                for s in range(1, N_CHIPS):
                    acc = acc + p_ref[s].astype(F32)
                o_ref[0] = acc

    return pl.pallas_call(
        body, name=name, grid=(n, rows // tr),
        in_specs=[pl.BlockSpec((N_CHIPS, tr, cols), lambda l, i, k=k: (0, jnp.where(l == k, i, 0), 0)) for k in range(n)],
        out_specs=pl.BlockSpec((1, tr, cols), lambda l, i: (l, i, 0)),
        out_shape=jax.ShapeDtypeStruct((n, rows, cols), F32),
        compiler_params=_params(("arbitrary", "arbitrary")),
    )(*parts)


def _adam_update(w, m, v, g):
    c1 = 1.0 - ADAM_B1 ** ADAM_STEP
    c2 = 1.0 - ADAM_B2 ** ADAM_STEP
    m_new = ADAM_B1 * m + (1.0 - ADAM_B1) * g
    v_new = ADAM_B2 * v + (1.0 - ADAM_B2) * (g * g)
    return -ADAM_LR * ((m_new / c1) / (jnp.sqrt(v_new / c2) + ADAM_EPS) + ADAM_WD * w), m_new, v_new


def _adamw_rows(w, m, v, g_parts, first, name):
    n_layers, rows, cols = w.shape
    tr = _row_block(rows, first)
    n = len(g_parts)

    def body(*refs):
        w_ref, m_ref, v_ref = refs[:3]
        g_out, d_out, m_out, v_out = refs[3 + n:]
        g = refs[3][...]
        for r in refs[4:3 + n]:
            g = g + r[...]
        g = g[:, :, :cols]
        d_out[...], m_out[...], v_out[...] = _adam_update(w_ref[...], m_ref[...], v_ref[...], g)
        g_out[...] = g

    blk = pl.BlockSpec((1, tr, cols), lambda l, i: (l, i, 0))
    g_blk = pl.BlockSpec((1, tr, g_parts[0].shape[2]), lambda l, i: (l, first // tr + i, 0))
    return pl.pallas_call(
        body, name=name, grid=(n_layers, rows // tr),
        in_specs=[blk] * 3 + [g_blk] * n, out_specs=[blk] * 4,
        out_shape=[jax.ShapeDtypeStruct(w.shape, F32)] * 4,
        compiler_params=_params(("parallel", "parallel")),
    )(w, m, v, *g_parts)


def _adamw(w, m, v, g_parts, name):
    rows, cols = w.shape
    tr = min(rows, 256)
    n = len(g_parts)

    def body(*refs):
        w_ref, m_ref, v_ref = refs[:3]
        g_refs = refs[3:3 + n]
        g_out, d_out, m_out, v_out = refs[3 + n:]
        g = g_refs[0][...]
        for r in g_refs[1:]:
            g = g + r[...]
        d_out[...], m_out[...], v_out[...] = _adam_update(w_ref[...], m_ref[...], v_ref[...], g)
        g_out[...] = g

    blk = pl.BlockSpec((tr, cols), lambda i: (i, 0))
    return pl.pallas_call(
        body, name=name, grid=(rows // tr,),
        in_specs=[blk] * (3 + n), out_specs=[blk] * 4,
        out_shape=[jax.ShapeDtypeStruct((rows, cols), F32)] * 4,
        compiler_params=_params(("parallel",)),
    )(w, m, v, *g_parts)


def _pack(parts, rows, fill=0.0):
    flat = jnp.concatenate([p.reshape(-1) for p in parts])
    return jnp.pad(flat, (0, rows * LANES - flat.shape[0]), constant_values=fill).reshape(rows, LANES)


def _unpack(packed, shapes):
    flat = packed.reshape(-1)
    out, at = [], 0
    for shp in shapes:
        size = 1
        for s in shp:
            size *= s
        out.append(flat[at:at + size].reshape(shp))
        at += size
    return out


def _packed_rows(shapes):
    total = 0
    for shp in shapes:
        size = 1
        for s in shp:
            size *= s
        total += size
    return -(-total // (8 * LANES)) * 8


def _cols_full(g, l):
    t = g[:, l]
    return jnp.moveaxis(t, 0, 1).reshape(t.shape[1], N_CHIPS * t.shape[2])


def _pad_cols(t):
    return jnp.pad(t, ((0, 0),) * (t.ndim - 1) + ((0, D_MODEL - t.shape[-1]),))


def _w_in_of(land_a):
    return jnp.moveaxis(land_a[:, :D_MODEL, :IN_SHARD], 0, 1).reshape(D_MODEL, W_IN_COLS)


def kernel(x, norm1_g, w_in, dn_conv_w, dn_a_log, dn_dt_bias, dn_norm_g, sc_conv_w, sc_norm_g, w_out, norm2_g, ffn_w_gate, ffn_w_up, ffn_w_down, final_norm_g, loss_target, m_norm1_g, m_w_in, m_dn_conv_w, m_dn_a_log, m_dn_dt_bias, m_dn_norm_g, m_sc_conv_w, m_sc_norm_g, m_w_out, m_norm2_g, m_ffn_w_gate, m_ffn_w_up, m_ffn_w_down, m_final_norm_g, v_norm1_g, v_w_in, v_dn_conv_w, v_dn_a_log, v_dn_dt_bias, v_dn_norm_g, v_sc_conv_w, v_sc_norm_g, v_w_out, v_norm2_g, v_ffn_w_gate, v_ffn_w_up, v_ffn_w_down, v_final_norm_g):
    chip = 2 * lax.axis_index("x") + lax.axis_index("y")

    g_cw, g_scw = _chip_exchange([dn_conv_w, sc_conv_w], "gather_conv", gather=True)

    t_last = lambda t: jnp.swapaxes(t, -1, -2)
    gate_t, up_t = t_last(ffn_w_gate), t_last(ffn_w_up)
    zero_token = jnp.zeros((8, LANES), F32)

    def shares(l, tie):
        share_a = jnp.concatenate([_pad_cols(w_in[l] + tie), w_out[l]], axis=0).astype(BF16)
        share_b = jnp.concatenate([gate_t[l] + tie, up_t[l], ffn_w_down[l]], axis=0).astype(BF16)
        return share_a, _own_slot(share_a), share_b, _own_slot(share_b)

    def gather_start(l, packed, after):
        a = _exchange_start(packed[0], packed[1], after, "gather_a_start_%d" % l, gather=True)
        b = _exchange_start(packed[2], packed[3], a[4], "gather_b_start_%d" % l, gather=True)
        return a, b

    ga, gb = gather_start(0, shares(0, 0.0), g_cw)
    packed = [None] + [shares(l, gb[4][0, 0]) for l in range(1, DEPTH)]
    packed_all = sum(t[0, 0].astype(F32) for p in packed[1:] for t in (p[0], p[2]))
    land_a = _exchange_wait(ga, zero_token + packed_all, "gather_a_wait_0", gather=True)
    act = x[0]
    layers, saved_m, saved_f, lands_b = [], [], [], []
    for l in range(DEPTH):
        hold = 0.0
        if l + 1 < DEPTH:
            ga, gb_next = gather_start(l + 1, packed[l + 1], land_a)
            hold = gb_next[4][0:1, 0:1]
        al, dt = _gate_rows(dn_a_log[l], dn_dt_bias[l])
        layers.append(dict(
            g1=norm1_g[l][None] + hold, wp=_projection_of(_w_in_of(land_a)), cw=_pad_rows(_cols_full(g_cw, l)), al=al, dt=dt,
            gn=dn_norm_g[l][None], scw=_pad_rows(_cols_full(g_scw, l)), gs=sc_norm_g[l][None],
            land_a=land_a, g2=norm2_g[l][None]))
        x1, s = _mixer_fwd(act, layers[l])
        saved_m.append(s)
        lands_b.append(_exchange_wait(gb, x1, "gather_b_wait_%d" % l, gather=True))
        act, s = _ffn_fwd(x1, layers[l], lands_b[l])
        saved_f.append(s)
        if l + 1 < DEPTH:
            land_a = _exchange_wait(ga, act, "gather_a_wait_%d" % (l + 1), gather=True)
            gb = gb_next

    dact, dact_bf16, loss_part, d_final = _loss_head(act, final_norm_g[None], loss_target[0])
    grads, reduce_a, reduce_b = [None] * DEPTH, [None] * DEPTH, [None] * DEPTH
    hold = 0.0
    for l in reversed(range(DEPTH)):
        p = layers[l]
        dx1, dx1_bf16, parts, dg2 = _ffn_back(dact, dact_bf16, saved_f[l], dict(p, g2=p["g2"] + hold), lands_b[l])
        reduce_b[l] = _exchange_start(parts, parts, zero_token, "reduce_b_start_%d" % l, gather=False)
        dact, dact_bf16, parts, gm = _mixer_bwd(dx1, dx1_bf16, saved_m[l], dict(p, gn=p["gn"] + reduce_b[l][4][0:1, 0:1]))
        reduce_a[l] = _exchange_start(parts, parts, zero_token, "reduce_a_start_%d" % l, gather=False)
        hold = reduce_a[l][4][0:1, 0:1]
        grads[l] = dict(gm, g2=dg2)
    loss = lax.psum(loss_part[0, 0], ("x", "y", "c"))
    stack = lambda key: jnp.stack([grads[l][key] for l in range(DEPTH)])

    got_b = [_exchange_wait(reduce_b[l], reduce_a[0][4], "reduce_b_wait_%d" % l, gather=False)
             for l in reversed(range(DEPTH))][::-1]
    sum_b = _sum_chips(got_b, "sum_chips_b")
    other_b, = _swap_sibling([sum_b], "swap_sibling_b")
    big = dict(
        ffn_w_gate=[t_last(o) for o in _adamw_rows(gate_t, t_last(m_ffn_w_gate), t_last(v_ffn_w_gate),
                                                   [sum_b, other_b], 0, "adamw_gate")],
        ffn_w_up=[t_last(o) for o in _adamw_rows(up_t, t_last(m_ffn_w_up), t_last(v_ffn_w_up),
                                                 [sum_b, other_b], FF_SHARD, "adamw_up")],
        ffn_w_down=_adamw_rows(ffn_w_down, m_ffn_w_down, v_ffn_w_down, [sum_b, other_b], 2 * FF_SHARD, "adamw_down"))
    after_b = big["ffn_w_down"][1]
    got_a = [_exchange_wait(reduce_a[l], after_b, "reduce_a_wait_%d" % l, gather=False) for l in reversed(range(DEPTH))][::-1]
    sum_a = _sum_chips(got_a, "sum_chips_a")
    other_a, = _swap_sibling([sum_a], "swap_sibling_a")
    big.update(
        w_in=_adamw_rows(w_in, m_w_in, v_w_in, [sum_a, other_a], 0, "adamw_w_in"),
        w_out=_adamw_rows(w_out, m_w_out, v_w_out, [sum_a, other_a], A_OUT_AT, "adamw_w_out"))

    full_shapes = [(DEPTH, D_MODEL), (DEPTH, D_MODEL), (DEPTH, HEAD_DIM), (DEPTH, SC_WIDTH), (DEPTH, HEADS),
                   (DEPTH, HEADS), (D_MODEL,), (DEPTH, 4, QKV), (DEPTH, 3, SC_WIDTH)]
    small_keys = ("g1", "g2", "gn", "gs", "al", "dt")
    packed = _pack([stack(k) for k in small_keys] + [d_final[0], stack("cw"), stack("scw")], _packed_rows(full_shapes))
    sg = _unpack(_all_reduce_small(packed), full_shapes)
    sg[7] = lax.dynamic_slice_in_dim(sg[7], chip * (QKV // N_CHIPS), QKV // N_CHIPS, axis=2)
    sg[8] = lax.dynamic_slice_in_dim(sg[8], chip * (SC_WIDTH // N_CHIPS), SC_WIDTH // N_CHIPS, axis=2)
    small_names = ("norm1_g", "norm2_g", "dn_norm_g", "sc_norm_g", "dn_a_log", "dn_dt_bias", "final_norm_g",
                   "dn_conv_w", "sc_conv_w")
    sw = (norm1_g, norm2_g, dn_norm_g, sc_norm_g, dn_a_log, dn_dt_bias, final_norm_g, dn_conv_w, sc_conv_w)
    sm = (m_norm1_g, m_norm2_g, m_dn_norm_g, m_sc_norm_g, m_dn_a_log, m_dn_dt_bias, m_final_norm_g, m_dn_conv_w, m_sc_conv_w)
    sv = (v_norm1_g, v_norm2_g, v_dn_norm_g, v_sc_norm_g, v_dn_a_log, v_dn_dt_bias, v_final_norm_g, v_dn_conv_w, v_sc_conv_w)
    shard_shapes = [t.shape for t in sw]
    rows = _packed_rows(shard_shapes)
    outs = _adamw(_pack(sw, rows), _pack(sm, rows), _pack(sv, rows, fill=1.0), [_pack(sg, rows)], "adamw_small")
    small = {name: [] for name in small_names}
    for o in outs:
        for name, t in zip(small_names, _unpack(o, shard_shapes)):
            small[name].append(t)

    order = ("norm1_g", "w_in", "dn_conv_w", "dn_a_log", "dn_dt_bias", "dn_norm_g", "sc_conv_w", "sc_norm_g", "w_out",
             "norm2_g", "ffn_w_gate", "ffn_w_up", "ffn_w_down", "final_norm_g")
    result = {**big, **small}
    return (loss, dact[None], *[result[n][0] for n in order], *[result[n][1] for n in order],
            *[result[n][2] for n in order], *[result[n][3] for n in order])
```

```python
import jax
import jax.numpy as jnp
from jax import lax
from jax.experimental import pallas as pl
from jax.experimental.pallas import tpu as pltpu

F32 = jnp.float32
BF16 = jnp.bfloat16
MESH = pl.DeviceIdType.MESH

D_MODEL = 1024
DEPTH = 4
HEADS = 4
HEAD_DIM = 128
DN_WIDTH = HEADS * HEAD_DIM
SC_WIDTH = 512
SC_GROUPS = 4
D_FF = 2816
CHUNK = 64
QKV = 3 * DN_WIDTH
W_IN_COLS = 4 * DN_WIDTH + 2 * HEADS + 3 * SC_WIDTH
WA_COLS = QKV + DN_WIDTH + 3 * SC_WIDTH
LANES = 128
EPS = 1e-6
Q_SCALE = HEAD_DIM ** -0.5
N_CHIPS = 4
N_DEV = 8
IN_SHARD = W_IN_COLS // N_CHIPS
OUT_SHARD = D_MODEL // N_CHIPS
FF_SHARD = D_FF // N_CHIPS
A_OUT_AT = D_MODEL
A_ROWS = D_MODEL + OUT_SHARD
B_ROWS = 3 * FF_SHARD

ADAM_LR = 0.001
ADAM_B1 = 0.9
ADAM_B2 = 0.999
ADAM_EPS = 1e-08
ADAM_WD = 0.01
ADAM_STEP = 10

VMEM_LIMIT = 56 * 1024 * 1024

NN = (((1,), (0,)), ((), ()))
NT = (((1,), (1,)), ((), ()))
TN = (((0,), (0,)), ((), ()))


def _mm(a, b, dims=NN):
    return lax.dot_general(a.astype(BF16), b.astype(BF16), dims, preferred_element_type=F32)


def _mm32(a, b, dims=NN):
    return lax.dot_general(a, b, dims, preferred_element_type=F32, precision=lax.Precision.HIGHEST)


def _params(sem, vmem=VMEM_LIMIT):
    return pltpu.CompilerParams(dimension_semantics=sem, vmem_limit_bytes=vmem)


def _sigmoid(x):
    return 0.5 * jnp.tanh(0.5 * x) + 0.5


def _softplus(x):
    return jnp.maximum(x, 0.0) + jnp.log1p(jnp.exp(-jnp.abs(x)))


def _row_acc(acc_ref, val):
    acc_ref[0:1, :] += jnp.sum(val, axis=0, keepdims=True)


def _rms_bwd(dh, xh, r, gain):
    dxh = dh * gain
    return r * (dxh - xh * jnp.mean(dxh * xh, axis=-1, keepdims=True))


def _before_halo(tb):
    return lambda i: (jnp.maximum(i * (tb // 8) - 1, 0), 0)


def _after_halo(tb, n_rows):
    last = n_rows // 8 - 1
    return lambda i: (jnp.minimum((i + 1) * (tb // 8), last), 0)


def _rows_from(xc, offset, tb):
    part = offset % 8
    if part:
        xc = pltpu.roll(xc, xc.shape[0] - part, 0)
    return xc[offset - part:offset - part + tb, :]


def _taps(xc, w, n_taps, tb, first):
    out = w[0:1, :] * _rows_from(xc, first, tb)
    for j in range(1, n_taps):
        out = out + w[j:j + 1, :] * _rows_from(xc, first + j, tb)
    return out


P_SC = QKV
P_Z = P_SC + 3 * SC_WIDTH
P_BD = P_Z + DN_WIDTH
P_COLS = P_BD + LANES


def _in_proj(x, g1, wp):
    T = x.shape[0]
    tb = 256

    def body(x_ref, g_ref, wp_ref, qkv_ref, z_ref, sc_ref, bd_ref, h_ref):
        xv = x_ref[...]
        r = lax.rsqrt(jnp.mean(xv * xv, axis=-1, keepdims=True) + EPS)
        h = (xv * r * g_ref[...]).astype(BF16)
        p = jnp.dot(h, wp_ref[...], preferred_element_type=F32)
        qkv_ref[...] = p[:, :P_SC]
        sc_ref[...] = p[:, P_SC:P_Z]
        z_ref[...] = p[:, P_Z:P_BD]
        bd_ref[...] = p[:, P_BD:]
        h_ref[...] = h

    tok = lambda w: pl.BlockSpec((tb, w), lambda i: (i, 0))
    full = lambda a: pl.BlockSpec(a.shape, lambda i: (0, 0))
    return pl.pallas_call(
        body, name="in_proj", grid=(T // tb,),
        in_specs=[tok(D_MODEL), full(g1), full(wp)],
        out_specs=[tok(QKV), tok(DN_WIDTH), tok(3 * SC_WIDTH), tok(LANES), tok(D_MODEL)],
        out_shape=[jax.ShapeDtypeStruct((T, QKV), F32), jax.ShapeDtypeStruct((T, DN_WIDTH), F32),
                   jax.ShapeDtypeStruct((T, 3 * SC_WIDTH), F32), jax.ShapeDtypeStruct((T, LANES), F32),
                   jax.ShapeDtypeStruct((T, D_MODEL), BF16)],
        compiler_params=_params(("parallel",)),
    )(x, g1, wp)


def _dp_block(tb, first, width, index=lambda i: i):
    assert first % width == 0
    return pl.BlockSpec((tb, width), lambda i: (index(i), first // width))


def _dn_act(pre, halo, cw, tb):
    xc = jnp.concatenate([halo, pre], axis=0)
    c = _taps(xc, cw, 4, tb, 5)
    sg = _sigmoid(c)
    return xc, c, sg, c * sg


def _gates(bd, al_row, dt_row):
    lane = lax.broadcasted_iota(jnp.int32, bd.shape, 1)
    beta = _sigmoid(bd)
    g = -jnp.exp(al_row) * _softplus(bd + dt_row)
    return jnp.where(lane < HEADS, beta, jnp.where(lane < 2 * HEADS, g, 0.0))


def _dn_prep(qkv, cw, bd, al_row, dt_row):
    T = qkv.shape[0]
    tb = 512

    def body(pre_ref, halo_ref, cw_ref, bd_ref, al_ref, dt_ref, q_ref, k_ref, v_ref, bg_ref):
        halo = jnp.where(pl.program_id(0) > 0, halo_ref[...], 0.0)
        _, _, _, a = _dn_act(pre_ref[...], halo, cw_ref[...], tb)
        for hh in range(HEADS):
            sl = slice(HEAD_DIM * hh, HEAD_DIM * (hh + 1))
            qs = a[:, sl]
            q_ref[:, sl] = qs * (lax.rsqrt(jnp.sum(qs * qs, axis=-1, keepdims=True) + EPS) * Q_SCALE)
            ks = a[:, DN_WIDTH + HEAD_DIM * hh:DN_WIDTH + HEAD_DIM * (hh + 1)]
            k_ref[:, sl] = ks * lax.rsqrt(jnp.sum(ks * ks, axis=-1, keepdims=True) + EPS)
        v_ref[...] = a[:, 2 * DN_WIDTH:]
        gates = _gates(bd_ref[...], al_ref[...], dt_ref[...])
        lane = lax.broadcasted_iota(jnp.int32, gates.shape, 1)
        bg_ref[...] = jnp.where(lane < HEADS, gates, _mm32(_chunk_cumsum_matrix(tb), gates))

    tok = lambda w: pl.BlockSpec((tb, w), lambda i: (i, 0))
    full = lambda a: pl.BlockSpec(a.shape, lambda i: (0, 0))
    return pl.pallas_call(
        body, name="dn_prep", grid=(T // tb,),
        in_specs=[tok(QKV), pl.BlockSpec((8, QKV), _before_halo(tb)), full(cw), tok(LANES), full(al_row), full(dt_row)],
        out_specs=[tok(DN_WIDTH), tok(DN_WIDTH), tok(DN_WIDTH), tok(LANES)],
        out_shape=[jax.ShapeDtypeStruct((T, DN_WIDTH), F32)] * 3 + [jax.ShapeDtypeStruct((T, LANES), F32)],
        compiler_params=_params(("parallel",)),
    )(qkv, qkv, cw, bd, al_row, dt_row)


def _chunk_masks():
    row = lax.broadcasted_iota(jnp.int32, (CHUNK, CHUNK), 0)
    col = lax.broadcasted_iota(jnp.int32, (CHUNK, CHUNK), 1)
    return row >= col, row > col


def _chunk_cumsum_matrix(n):
    row = lax.broadcasted_iota(jnp.int32, (n, n), 0)
    col = lax.broadcasted_iota(jnp.int32, (n, n), 1)
    return jnp.logical_and(row >= col, row // CHUNK == col // CHUNK).astype(F32)


def _chunk_units(q_ref, k_ref, v_ref, bg_ref, rows):
    bgc = bg_ref[rows, :]
    bg_t = bgc.T
    qv, kv, vv = q_ref[rows, :], k_ref[rows, :], v_ref[rows, :]
    units = []
    for h in range(HEADS):
        sl = slice(HEAD_DIM * h, HEAD_DIM * (h + 1))
        units.append((qv[:, sl], kv[:, sl], vv[:, sl], bgc[:, h:h + 1], bgc[:, HEADS + h:HEADS + h + 1],
                      bg_t[HEADS + h:HEADS + h + 1, :]))
    return units


def _units_local(units, masks):
    causal, strict = masks
    pre = []
    for q, k, v, beta, gc, gr in units:
        kb = k * beta
        eg = jnp.exp(gc)
        g_last = gc[CHUNK - 1:CHUNK, :]
        ek = jnp.exp(g_last - gc)
        pre.append(dict(q=q, k=k, v=v, beta=beta, decay=jnp.exp(jnp.where(causal, gc - gr, -1e30)), kb=kb, vb=v * beta,
                        eg=eg, kbg=kb * eg, ek=ek, gl=jnp.exp(g_last), q_dec=q * eg, k_dec=k * ek))
    both = [_mm(jnp.concatenate([p["kb"], p["q"]], axis=0), p["k"], NT) for p in pre]
    for p, b in zip(pre, both):
        p["low"] = jnp.where(strict, b[:CHUNK] * p["decay"], 0.0)
        p["qk"] = jnp.where(causal, b[CHUNK:] * p["decay"], 0.0)
    xs = [-p["low"] for p in pre]
    pw = [_mm(p["low"], p["low"]) for p in pre]
    for _ in range(4):
        both = [_mm(jnp.concatenate([pp, x], axis=0), pp) for pp, x in zip(pw, xs)]
        xs = [x + pp + b[CHUNK:] for x, pp, b in zip(xs, pw, both)]
        pw = [b[:CHUNK] for b in both]
    last = [_mm(x, pp) for x, pp in zip(xs, pw)]
    xs = [x + pp + b for x, pp, b in zip(xs, pw, last)]
    uw = [_mm(x, jnp.concatenate([p["vb"], p["kbg"]], axis=1)) for x, p in zip(xs, pre)]
    for p, x, b in zip(pre, xs, uw):
        p["xm"] = x
        p["u"] = p["vb"] + b[:, :HEAD_DIM]
        p["w"] = p["kbg"] + b[:, HEAD_DIM:]
    return pre


def _delta_fwd(q, k, v, bg):
    T = q.shape[0]
    tb = 512
    n_chunk = tb // CHUNK

    def body(q_ref, k_ref, v_ref, bg_ref, o_ref, st_ref, s_ref):
        @pl.when(pl.program_id(0) == 0)
        def _():
            s_ref[...] = jnp.zeros_like(s_ref)

        masks = _chunk_masks()

        def pair(pi, carry):
            rows = [pl.ds(pl.multiple_of((2 * pi + j) * CHUNK, CHUNK), CHUNK) for j in range(2)]
            loc = _units_local(_chunk_units(q_ref, k_ref, v_ref, bg_ref, rows[0])
                               + _chunk_units(q_ref, k_ref, v_ref, bg_ref, rows[1]), masks)
            states = [s_ref[h] for h in range(HEADS)]
            for j in range(2):
                lj = loc[HEADS * j:HEADS * (j + 1)]
                ws = [_mm(jnp.concatenate([p["w"], p["q_dec"]], axis=0), s) for p, s in zip(lj, states)]
                v_new = [p["u"] - b[:CHUNK] for p, b in zip(lj, ws)]
                intra = [_mm(p["qk"], vn) for p, vn in zip(lj, v_new)]
                upd = [_mm(p["k_dec"], vn, TN) for p, vn in zip(lj, v_new)]
                o_ref[rows[j], :] = jnp.concatenate([b[CHUNK:] + a for b, a in zip(ws, intra)], axis=1)
                for h in range(HEADS):
                    st_ref[2 * pi + j, h] = states[h]
                states = [p["gl"] * s + d for p, s, d in zip(lj, states, upd)]
            for h in range(HEADS):
                s_ref[h] = states[h]
            return carry

        lax.fori_loop(0, n_chunk // 2, pair, 0)

    tok = lambda w: pl.BlockSpec((tb, w), lambda i: (i, 0))
    return pl.pallas_call(
        body, name="delta_fwd", grid=(T // tb,),
        in_specs=[tok(DN_WIDTH), tok(DN_WIDTH), tok(DN_WIDTH), tok(LANES)],
        out_specs=[tok(DN_WIDTH), pl.BlockSpec((n_chunk, HEADS, HEAD_DIM, HEAD_DIM), lambda i: (i, 0, 0, 0))],
        out_shape=[jax.ShapeDtypeStruct((T, DN_WIDTH), F32),
                   jax.ShapeDtypeStruct((T // CHUNK, HEADS, HEAD_DIM, HEAD_DIM), F32)],
        scratch_shapes=[pltpu.VMEM((HEADS, HEAD_DIM, HEAD_DIM), F32)],
        compiler_params=_params(("arbitrary",)),
    )(q, k, v, bg)


def _dn_out(o, z, gn):
    outs, ohs, rs = [], [], []
    for hh in range(HEADS):
        oh = o[:, HEAD_DIM * hh:HEAD_DIM * (hh + 1)]
        r = lax.rsqrt(jnp.mean(oh * oh, axis=-1, keepdims=True) + EPS)
        ohs.append(oh * r)
        rs.append(r)
    sz = _sigmoid(z)
    oh = jnp.concatenate(ohs, axis=1)
    gn4 = jnp.concatenate([gn] * HEADS, axis=1)
    return oh * gn4 * (z * sz), oh, rs, sz, gn4


def _sc_fwd(sc_in, halo, cw, tb):
    xc = jnp.concatenate([halo, sc_in], axis=0)
    u = xc[:, SC_WIDTH:2 * SC_WIDTH] * xc[:, 2 * SC_WIDTH:]
    cv = _taps(u, cw, 3, tb, 6)
    gate_b = sc_in[:, :SC_WIDTH]
    y = gate_b * cv
    gw = SC_WIDTH // SC_GROUPS
    yhs, rs = [], []
    for gi in range(SC_GROUPS):
        yg = y[:, gw * gi:gw * (gi + 1)]
        r = lax.rsqrt(jnp.mean(yg * yg, axis=-1, keepdims=True) + EPS)
        yhs.append(yg * r)
        rs.append(r)
    return u, cv, gate_b, jnp.concatenate(yhs, axis=1), rs


def _shard_rows(land, first, rows):
    assert first % rows == 0 and land.shape[0] == N_CHIPS
    return pl.BlockSpec((N_CHIPS, rows, land.shape[2]), lambda i: (0, first // rows, 0))


def _whole(w_ref):
    n, rows, cols = w_ref.shape
    return w_ref[...].reshape(n * rows, cols)


def _mix_out(o, z, sc_in, x, land_a, gn, scw, gs):
    T = x.shape[0]
    tb = 256

    def body(o_ref, z_ref, sc_ref, halo_ref, x_ref, w_ref, gn_ref, scw_ref, gs_ref, x1_ref, mix_ref):
        o_n = _dn_out(o_ref[...], z_ref[...], gn_ref[...])[0]
        halo = jnp.where(pl.program_id(0) > 0, halo_ref[...], 0.0)
        yh = _sc_fwd(sc_ref[...], halo, scw_ref[...], tb)[3]
        mix = jnp.concatenate([o_n, yh * gs_ref[...]], axis=1).astype(BF16)
        x1_ref[...] = x_ref[...] + jnp.dot(mix, _whole(w_ref), preferred_element_type=F32)
        mix_ref[...] = mix

    tok = lambda w: pl.BlockSpec((tb, w), lambda i: (i, 0))
    full = lambda a: pl.BlockSpec(a.shape, lambda i: (0, 0))
    return pl.pallas_call(
        body, name="mix_out", grid=(T // tb,),
        in_specs=[tok(DN_WIDTH), tok(DN_WIDTH), tok(3 * SC_WIDTH), pl.BlockSpec((8, 3 * SC_WIDTH), _before_halo(tb)),
                  tok(D_MODEL), _shard_rows(land_a, A_OUT_AT, OUT_SHARD), full(gn), full(scw), full(gs)],
        out_specs=[tok(D_MODEL), tok(D_MODEL)],
        out_shape=[jax.ShapeDtypeStruct((T, D_MODEL), F32), jax.ShapeDtypeStruct((T, D_MODEL), BF16)],
        compiler_params=_params(("parallel",)),
    )(o, z, sc_in, sc_in, x, land_a, gn, scw, gs)


def _ffn(x1, g2, land_b):
    T = x1.shape[0]
    tb = 256

    def body(x_ref, g_ref, wgt_ref, wut_ref, wd_ref, x2_ref, a_ref, b_ref, h_ref):
        xv = x_ref[...]
        r = lax.rsqrt(jnp.mean(xv * xv, axis=-1, keepdims=True) + EPS)
        h = (xv * r * g_ref[...]).astype(BF16)
        a = lax.dot_general(h, _whole(wgt_ref), NT, preferred_element_type=F32)
        b = lax.dot_general(h, _whole(wut_ref), NT, preferred_element_type=F32)
        act = (a * _sigmoid(a) * b).astype(BF16)
        x2_ref[...] = xv + jnp.dot(act, _whole(wd_ref), preferred_element_type=F32)
        a_ref[...] = a.astype(BF16)
        b_ref[...] = b.astype(BF16)
        h_ref[...] = h

    tok = lambda w: pl.BlockSpec((tb, w), lambda i: (i, 0))
    return pl.pallas_call(
        body, name="ffn", grid=(T // tb,),
        in_specs=[tok(D_MODEL), pl.BlockSpec(g2.shape, lambda i: (0, 0)), _shard_rows(land_b, 0, FF_SHARD),
                  _shard_rows(land_b, FF_SHARD, FF_SHARD), _shard_rows(land_b, 2 * FF_SHARD, FF_SHARD)],
        out_specs=[tok(D_MODEL), tok(D_FF), tok(D_FF), tok(D_MODEL)],
        out_shape=[jax.ShapeDtypeStruct((T, D_MODEL), F32), jax.ShapeDtypeStruct((T, D_FF), BF16),
                   jax.ShapeDtypeStruct((T, D_FF), BF16), jax.ShapeDtypeStruct((T, D_MODEL), BF16)],
        compiler_params=_params(("parallel",)),
    )(x1, g2, land_b, land_b, land_b)


def _loss_head(x, gf, target):
    T = x.shape[0]
    tb = 512

    def body(x_ref, g_ref, t_ref, dx_ref, dxb_ref, loss_ref, dg_ref):
        @pl.when(pl.program_id(0) == 0)
        def _():
            loss_ref[...] = jnp.zeros_like(loss_ref)
            dg_ref[...] = jnp.zeros_like(dg_ref)

        xv = x_ref[...]
        r = lax.rsqrt(jnp.mean(xv * xv, axis=-1, keepdims=True) + EPS)
        xh = xv * r
        err = xh * g_ref[...] - t_ref[...]
        per_tok = jnp.mean(err * err, axis=-1, keepdims=True)
        loss_ref[...] += 0.5 * jnp.sum(per_tok, axis=0, keepdims=True)
        dy = err * (1.0 / D_MODEL)
        _row_acc(dg_ref, dy * xh)
        dx = _rms_bwd(dy, xh, r, g_ref[...])
        dx_ref[...] = dx
        dxb_ref[...] = dx.astype(BF16)

    tok = pl.BlockSpec((tb, D_MODEL), lambda i: (i, 0))
    return pl.pallas_call(
        body, name="loss_head", grid=(T // tb,),
        in_specs=[tok, pl.BlockSpec(gf.shape, lambda i: (0, 0)), tok],
        out_specs=[tok, tok, pl.BlockSpec((8, LANES), lambda i: (0, 0)), pl.BlockSpec((8, D_MODEL), lambda i: (0, 0))],
        out_shape=[jax.ShapeDtypeStruct((T, D_MODEL), F32), jax.ShapeDtypeStruct((T, D_MODEL), BF16),
                   jax.ShapeDtypeStruct((8, LANES), F32), jax.ShapeDtypeStruct((8, D_MODEL), F32)],
        compiler_params=_params(("arbitrary",)),
    )(x, gf, target)


def _ffn_bwd(dx2, x1, a, b, g2, land_b):
    T = x1.shape[0]
    tb = 256

    def body(dx2_ref, x_ref, a_ref, b_ref, g_ref, wgt_ref, wut_ref, wd_ref,
             dx1_ref, dx1b_ref, da_ref, db_ref, act_ref, dg_ref):
        @pl.when(pl.program_id(0) == 0)
        def _():
            dg_ref[...] = jnp.zeros_like(dg_ref)

        dx2v = dx2_ref[...]
        av = a_ref[...].astype(F32)
        bv = b_ref[...].astype(F32)
        dact = _mm(dx2v, _whole(wd_ref), NT)
        sa = _sigmoid(av)
        silu = av * sa
        da = (dact * bv * (sa * (1.0 + av * (1.0 - sa)))).astype(BF16)
        db = (dact * silu).astype(BF16)
        dh = _mm(da, _whole(wgt_ref)) + _mm(db, _whole(wut_ref))
        xv = x_ref[...]
        r = lax.rsqrt(jnp.mean(xv * xv, axis=-1, keepdims=True) + EPS)
        xh = xv * r
        _row_acc(dg_ref, dh * xh)
        dx1 = dx2v + _rms_bwd(dh, xh, r, g_ref[...])
        dx1_ref[...] = dx1
        dx1b_ref[...] = dx1.astype(BF16)
        da_ref[...] = da
        db_ref[...] = db
        act_ref[...] = (silu * bv).astype(BF16)

    tok = lambda w: pl.BlockSpec((tb, w), lambda i: (i, 0))
    return pl.pallas_call(
        body, name="ffn_bwd", grid=(T // tb,),
        in_specs=[tok(D_MODEL), tok(D_MODEL), tok(D_FF), tok(D_FF), pl.BlockSpec(g2.shape, lambda i: (0, 0)),
                  _shard_rows(land_b, 0, FF_SHARD), _shard_rows(land_b, FF_SHARD, FF_SHARD),
                  _shard_rows(land_b, 2 * FF_SHARD, FF_SHARD)],
        out_specs=[tok(D_MODEL), tok(D_MODEL), tok(D_FF), tok(D_FF), tok(D_FF), pl.BlockSpec((8, D_MODEL), lambda i: (0, 0))],
        out_shape=[jax.ShapeDtypeStruct((T, D_MODEL), F32), jax.ShapeDtypeStruct((T, D_MODEL), BF16)]
        + [jax.ShapeDtypeStruct((T, D_FF), BF16)] * 3 + [jax.ShapeDtypeStruct((8, D_MODEL), F32)],
        compiler_params=_params(("arbitrary",)),
    )(dx2, x1, a, b, g2, land_b, land_b, land_b)


def _wgrad(a, b, bm, bn, name):
    T, M = a.shape
    N = b.shape[1]
    bk = min(T, 1024)

    def body(a_ref, b_ref, o_ref):
        @pl.when(pl.program_id(2) == 0)
        def _():
            o_ref[...] = jnp.zeros_like(o_ref)

        o_ref[...] += lax.dot_general(a_ref[...], b_ref[...], TN, preferred_element_type=F32)

    return pl.pallas_call(
        body, name=name, grid=(M // bm, N // bn, T // bk),
        in_specs=[pl.BlockSpec((bk, bm), lambda i, j, kk: (kk, i)), pl.BlockSpec((bk, bn), lambda i, j, kk: (kk, j))],
        out_specs=pl.BlockSpec((bm, bn), lambda i, j, kk: (i, j)),
        out_shape=jax.ShapeDtypeStruct((M, N), F32),
        compiler_params=_params(("parallel", "parallel", "arbitrary")),
    )(a, b)


def _wgrad_share(a, b, parts, first, name):
    T = b.shape[0]
    rows = a.shape[1] // N_CHIPS
    assert first % rows == 0 and b.shape[1] == parts.shape[2]
    bk = min(T, 1024)
    n_k = T // bk
    group = 2
    assert (group * rows) % LANES == 0

    def body(a_ref, b_ref, parts_ref, o_ref, acc_ref):
        kk = pl.program_id(1)

        @pl.when(kk == 0)
        def _():
            acc_ref[...] = jnp.zeros_like(acc_ref)

        acc_ref[...] += lax.dot_general(a_ref[...], b_ref[...], TN, preferred_element_type=F32)

        @pl.when(kk == n_k - 1)
        def _():
            for s in range(group):
                o_ref[s] = acc_ref[rows * s:rows * (s + 1), :].astype(BF16)

    return pl.pallas_call(
        body, name=name, grid=(N_CHIPS // group, n_k),
        in_specs=[pl.BlockSpec((bk, group * rows), lambda i, kk: (kk, i)),
                  pl.BlockSpec((bk, b.shape[1]), lambda i, kk: (kk, 0)), _ANY],
        out_specs=pl.BlockSpec((group, rows, b.shape[1]), lambda i, kk: (i, first // rows, 0)),
        out_shape=jax.ShapeDtypeStruct(parts.shape, BF16),
        scratch_shapes=[pltpu.VMEM((group * rows, b.shape[1]), F32)],
        input_output_aliases={2: 0},
        compiler_params=_params(("parallel", "arbitrary")),
    )(a, b, parts)


def _mix_out_bwd(dx1, o, z, sc_in, land_a, gn, scw, gs, dp):
    T = dx1.shape[0]
    tb = 256

    def body(dx_ref, o_ref, z_ref, sc_ref, halo_ref, w_ref, gn_ref, scw_ref, gs_ref, dp_ref,
             do_ref, dz_ref, dgb_ref, dcv_ref, dgn_ref, dgs_ref, dscw_ref):
        @pl.when(pl.program_id(0) == 0)
        def _():
            dgn_ref[...] = jnp.zeros_like(dgn_ref)
            dgs_ref[...] = jnp.zeros_like(dgs_ref)
            dscw_ref[...] = jnp.zeros_like(dscw_ref)

        dmix = _mm(dx_ref[...], _whole(w_ref), NT)
        don = dmix[:, :DN_WIDTH]
        dosc = dmix[:, DN_WIDTH:]
        zv = z_ref[...]
        _, oh, rs, sz, gn4 = _dn_out(o_ref[...], zv, gn_ref[...])
        silu_z = zv * sz
        dgn_full = don * oh * silu_z
        dgn_ref[0:1, :] += jnp.sum(sum(dgn_full[:, HEAD_DIM * hh:HEAD_DIM * (hh + 1)] for hh in range(HEADS)),
                                   axis=0, keepdims=True)
        dz_ref[...] = (don * oh * gn4 * (sz * (1.0 + zv * (1.0 - sz)))).astype(BF16)
        t = don * gn4 * silu_z
        for hh in range(HEADS):
            sl = slice(HEAD_DIM * hh, HEAD_DIM * (hh + 1))
            th, ohh = t[:, sl], oh[:, sl]
            do_ref[:, sl] = rs[hh] * (th - ohh * jnp.mean(th * ohh, axis=-1, keepdims=True))
        halo = jnp.where(pl.program_id(0) > 0, halo_ref[...], 0.0)
        u, cv, gate_b, yh, rys = _sc_fwd(sc_ref[...], halo, scw_ref[...], tb)
        _row_acc(dgs_ref, dosc * yh)
        ty = dosc * gs_ref[...]
        gw = SC_WIDTH // SC_GROUPS
        dys = []
        for gi in range(SC_GROUPS):
            sl = slice(gw * gi, gw * (gi + 1))
            tg, yg = ty[:, sl], yh[:, sl]
            dys.append(rys[gi] * (tg - yg * jnp.mean(tg * yg, axis=-1, keepdims=True)))
        dy = jnp.concatenate(dys, axis=1)
        dgb_ref[...] = dy * cv
        dcv = dy * gate_b
        dcv_ref[...] = dcv
        for j in range(3):
            dscw_ref[j:j + 1, :] += jnp.sum(dcv * _rows_from(u, 6 + j, tb), axis=0, keepdims=True)

    tok = lambda w: pl.BlockSpec((tb, w), lambda i: (i, 0))
    full = lambda t: pl.BlockSpec(t.shape, lambda i: (0, 0))
    acc = lambda w: pl.BlockSpec((8, w), lambda i: (0, 0))
    return pl.pallas_call(
        body, name="mix_out_bwd", grid=(T // tb,),
        in_specs=[tok(D_MODEL), tok(DN_WIDTH), tok(DN_WIDTH), tok(3 * SC_WIDTH),
                  pl.BlockSpec((8, 3 * SC_WIDTH), _before_halo(tb)), _shard_rows(land_a, A_OUT_AT, OUT_SHARD),
                  full(gn), full(scw), full(gs), _ANY],
        out_specs=[tok(DN_WIDTH), _dp_block(tb, P_Z, DN_WIDTH), tok(SC_WIDTH), tok(SC_WIDTH),
                   acc(HEAD_DIM), acc(SC_WIDTH), acc(SC_WIDTH)],
        out_shape=[jax.ShapeDtypeStruct((T, DN_WIDTH), F32), jax.ShapeDtypeStruct(dp.shape, BF16),
                   jax.ShapeDtypeStruct((T, SC_WIDTH), F32), jax.ShapeDtypeStruct((T, SC_WIDTH), F32),
                   jax.ShapeDtypeStruct((8, HEAD_DIM), F32), jax.ShapeDtypeStruct((8, SC_WIDTH), F32),
                   jax.ShapeDtypeStruct((8, SC_WIDTH), F32)],
        input_output_aliases={9: 1},
        compiler_params=_params(("arbitrary",)),
    )(dx1, o, z, sc_in, sc_in, land_a, gn, scw, gs, dp)


def _sc_conv_bwd(dcv, dgb, sc_in, scw, dp):
    T = dcv.shape[0]
    tb = 512

    def body(dcv_ref, halo_ref, dgb_ref, sc_ref, w_ref, dp_ref, out_ref):
        last = pl.program_id(0) == pl.num_programs(0) - 1
        halo = jnp.where(last, 0.0, halo_ref[...])
        xc = jnp.concatenate([dcv_ref[...], halo], axis=0)
        w = w_ref[...]
        du = w[2:3, :] * xc[0:tb, :] + w[1:2, :] * _rows_from(xc, 1, tb) + w[0:1, :] * _rows_from(xc, 2, tb)
        sc = sc_ref[...]
        out_ref[:, :SC_WIDTH] = dgb_ref[...].astype(BF16)
        out_ref[:, SC_WIDTH:2 * SC_WIDTH] = (du * sc[:, 2 * SC_WIDTH:]).astype(BF16)
        out_ref[:, 2 * SC_WIDTH:] = (du * sc[:, SC_WIDTH:2 * SC_WIDTH]).astype(BF16)

    tok = lambda w: pl.BlockSpec((tb, w), lambda i: (i, 0))
    return pl.pallas_call(
        body, name="sc_conv_bwd", grid=(T // tb,),
        in_specs=[tok(SC_WIDTH), pl.BlockSpec((8, SC_WIDTH), _after_halo(tb, T)), tok(SC_WIDTH), tok(3 * SC_WIDTH),
                  pl.BlockSpec(scw.shape, lambda i: (0, 0)), _ANY],
        out_specs=_dp_block(tb, P_SC, 3 * SC_WIDTH),
        out_shape=jax.ShapeDtypeStruct(dp.shape, BF16),
        input_output_aliases={5: 0},
        compiler_params=_params(("parallel",)),
    )(dcv, dcv, dgb, sc_in, scw, dp)


def _delta_bwd(q, k, v, bg, states, do):
    T = q.shape[0]
    tb = 512
    n_chunk = tb // CHUNK
    nb = T // tb

    def body(q_ref, k_ref, v_ref, bg_ref, st_ref, do_ref, dq_ref, dk_ref, dv_ref, dbg_ref, ds_ref):
        @pl.when(pl.program_id(0) == 0)
        def _():
            ds_ref[...] = jnp.zeros_like(ds_ref)

        masks = _chunk_masks()
        causal, strict = masks
        lane = lax.broadcasted_iota(jnp.int32, (CHUNK, LANES), 1)
        last_row = lax.broadcasted_iota(jnp.int32, (CHUNK, 1), 0) == CHUNK - 1
        cat = jnp.concatenate
        heads = range(HEADS)

        def open_chunk(ci, loc):
            rows = pl.ds(pl.multiple_of(ci * CHUNK, CHUNK), CHUNK)
            dov = do_ref[rows, :]
            return dict(rows=rows, loc=loc, do=[dov[:, HEAD_DIM * h:HEAD_DIM * (h + 1)] for h in heads],
                        state=[st_ref[ci, h] for h in heads])

        def a_free(c):
            loc, do, state = c["loc"], c["do"], c["state"]
            w_s = [_mm(p["w"], s) for p, s in zip(loc, state)]
            c["dq_dec"] = [_mm(d, s, NT) for d, s in zip(do, state)]
            c["qk_do"] = [_mm(p["qk"], d, TN) for p, d in zip(loc, do)]
            c["qd_do"] = [_mm(p["q_dec"], d, TN) for p, d in zip(loc, do)]
            c["v_new"] = [p["u"] - t for p, t in zip(loc, w_s)]
            c["dqk"] = [jnp.where(causal, _mm(d, vn, NT), 0.0) for d, vn in zip(do, c["v_new"])]

        def a_state(c, ds_next):
            c["ds_next"] = ds_next
            kd_ds = [_mm(p["k_dec"], d) for p, d in zip(c["loc"], ds_next)]
            c["dk_dec"] = [_mm(vn, d, NT) for vn, d in zip(c["v_new"], ds_next)]
            c["dv_new"] = [a + b for a, b in zip(c["qk_do"], kd_ds)]

        def b_state(c):
            loc = c["loc"]
            w_dv = [_mm(p["w"], dvn, TN) for p, dvn in zip(loc, c["dv_new"])]
            c["dw"] = [-_mm(dvn, s, NT) for dvn, s in zip(c["dv_new"], c["state"])]
            return [loc[h]["gl"] * c["ds_next"][h] + c["qd_do"][h] - w_dv[h] for h in heads]

        def c_solve(c):
            loc, dv_new, dw = c["loc"], c["dv_new"], c["dw"]
            c["dtm"] = [_mm(cat([dvn, d], axis=1), cat([p["vb"], p["kbg"]], axis=1), NT) for dvn, d, p in zip(dv_new, dw, loc)]
            x_t = [_mm(p["xm"], cat([dvn, d], axis=1), TN) for p, dvn, d in zip(loc, dv_new, dw)]
            c["dvb"] = [dvn + t[:, :HEAD_DIM] for dvn, t in zip(dv_new, x_t)]
            c["dkbg"] = [d + t[:, HEAD_DIM:] for d, t in zip(dw, x_t)]

        def d_solve(c):
            c["y"] = [t + _mm(p["xm"], t, TN) for p, t in zip(c["loc"], c["dtm"])]

        def e_solve(c):
            c["dlow"] = [jnp.where(strict, -(t + _mm(t, p["xm"], NT)), 0.0) for p, t in zip(c["loc"], c["y"])]

        def f_close(c):
            loc, rows = c["loc"], c["rows"]
            dmm = [d * p["decay"] for d, p in zip(c["dlow"], loc)]
            dnn = [d * p["decay"] for d, p in zip(c["dqk"], loc)]
            by_k = [_mm(cat([a, b], axis=0), p["k"]) for a, b, p in zip(dmm, dnn, loc)]
            dk_mm = [_mm(cat([a, b], axis=0), cat([p["kb"], p["q"]], axis=0), TN) for a, b, p in zip(dmm, dnn, loc)]
            dq_out, dk_out, dv_out = [], [], []
            dbeta_all = jnp.zeros((CHUNK, LANES), F32)
            dgc_all = jnp.zeros((CHUNK, LANES), F32)
            for h in heads:
                p = loc[h]
                dkb = by_k[h][:CHUNK] + c["dkbg"][h] * p["eg"]
                dq_out.append(by_k[h][CHUNK:] + c["dq_dec"][h] * p["eg"])
                dk_out.append(dk_mm[h] + c["dk_dec"][h] * p["ek"] + dkb * p["beta"])
                dv_out.append(c["dvb"][h] * p["beta"])
                dbeta = jnp.sum(dkb * p["k"] + c["dvb"][h] * p["v"], axis=1, keepdims=True)
                e = c["dlow"][h] * p["low"] + c["dqk"][h] * p["qk"]
                kd = jnp.sum(c["dk_dec"][h] * p["k_dec"], axis=1, keepdims=True)
                dgc = (jnp.sum(e, axis=1, keepdims=True) - jnp.sum(e.T, axis=1, keepdims=True)
                       + jnp.sum(c["dq_dec"][h] * p["q_dec"], axis=1, keepdims=True) - kd
                       + jnp.sum(c["dkbg"][h] * p["kbg"], axis=1, keepdims=True))
                dgl = jnp.sum(jnp.sum(c["ds_next"][h] * c["state"][h], axis=1, keepdims=True), axis=0, keepdims=True)
                d_last = jnp.sum(kd, axis=0, keepdims=True) + dgl * p["gl"]
                dgc = dgc + jnp.where(last_row, d_last, 0.0)
                dbeta_all = jnp.where(lane == h, dbeta, dbeta_all)
                dgc_all = jnp.where(lane == h + HEADS, dgc, dgc_all)
            dq_ref[rows, :] = cat(dq_out, axis=1)
            dk_ref[rows, :] = cat(dk_out, axis=1)
            dv_ref[rows, :] = cat(dv_out, axis=1)
            dbg_ref[rows, :] = dbeta_all + dgc_all

        def pair(pj, carry):
            hi = n_chunk - 1 - 2 * pj
            lo = hi - 1
            rows = [pl.ds(pl.multiple_of(ci * CHUNK, CHUNK), CHUNK) for ci in (hi, lo)]
            loc = _units_local(_chunk_units(q_ref, k_ref, v_ref, bg_ref, rows[0])
                               + _chunk_units(q_ref, k_ref, v_ref, bg_ref, rows[1]), masks)
            c_hi, c_lo = open_chunk(hi, loc[:HEADS]), open_chunk(lo, loc[HEADS:])
            a_free(c_hi)
            a_free(c_lo)
            a_state(c_hi, [ds_ref[h] for h in heads])
            ds_mid = b_state(c_hi)
            a_state(c_lo, ds_mid)
            c_solve(c_hi)
            ds_out = b_state(c_lo)
            for h in heads:
                ds_ref[h] = ds_out[h]
            d_solve(c_hi)
            c_solve(c_lo)
            e_solve(c_hi)
            d_solve(c_lo)
            f_close(c_hi)
            e_solve(c_lo)
            f_close(c_lo)
            return carry

        lax.fori_loop(0, n_chunk // 2, pair, 0)

    tok = lambda w: pl.BlockSpec((tb, w), lambda i: (nb - 1 - i, 0))
    return pl.pallas_call(
        body, name="delta_bwd", grid=(nb,),
        in_specs=[tok(DN_WIDTH), tok(DN_WIDTH), tok(DN_WIDTH), tok(LANES),
                  pl.BlockSpec((n_chunk, HEADS, HEAD_DIM, HEAD_DIM), lambda i: (nb - 1 - i, 0, 0, 0)), tok(DN_WIDTH)],
        out_specs=[tok(DN_WIDTH), tok(DN_WIDTH), tok(DN_WIDTH), tok(LANES)],
        out_shape=[jax.ShapeDtypeStruct((T, DN_WIDTH), F32)] * 3 + [jax.ShapeDtypeStruct((T, LANES), F32)],
        scratch_shapes=[pltpu.VMEM((HEADS, HEAD_DIM, HEAD_DIM), F32)],
        compiler_params=_params(("arbitrary",)),
    )(q, k, v, bg, states, do)


def _dn_prep_bwd(dq, dk, dv, dbg, qkv, cw, bd, al_row, dt_row, dp):
    T = qkv.shape[0]
    tb = 256

    def body(dq_ref, dk_ref, dv_ref, dbg_ref, pre_ref, halo_ref, cw_ref, bd_ref, al_ref, dt_ref, dp_ref,
             dc_ref, dbd_ref, dcw_ref, dal_ref, ddt_ref):
        @pl.when(pl.program_id(0) == 0)
        def _():
            dcw_ref[...] = jnp.zeros_like(dcw_ref)
            dal_ref[...] = jnp.zeros_like(dal_ref)
            ddt_ref[...] = jnp.zeros_like(ddt_ref)

        halo = jnp.where(pl.program_id(0) > 0, halo_ref[...], 0.0)
        xc, c, sg, a = _dn_act(pre_ref[...], halo, cw_ref[...], tb)
        dsilu = sg * (1.0 + c * (1.0 - sg))
        for hh in range(HEADS):
            sl = slice(HEAD_DIM * hh, HEAD_DIM * (hh + 1))
            for base, g_ref, scale in ((0, dq_ref, Q_SCALE), (DN_WIDTH, dk_ref, 1.0)):
                sa = slice(base + HEAD_DIM * hh, base + HEAD_DIM * (hh + 1))
                raw = a[:, sa]
                r = lax.rsqrt(jnp.sum(raw * raw, axis=-1, keepdims=True) + EPS)
                nrm = raw * r
                gn_ = g_ref[:, sl] * scale
                dc_ref[:, sa] = r * (gn_ - nrm * jnp.sum(gn_ * nrm, axis=-1, keepdims=True)) * dsilu[:, sa]
        dc_ref[:, 2 * DN_WIDTH:] = dv_ref[...] * dsilu[:, 2 * DN_WIDTH:]
        dc = dc_ref[...]
        for j in range(4):
            dcw_ref[j:j + 1, :] += jnp.sum(dc * _rows_from(xc, 5 + j, tb), axis=0, keepdims=True)
        bdv = bd_ref[...]
        lane = lax.broadcasted_iota(jnp.int32, bdv.shape, 1)
        is_b = lane < HEADS
        dbg_in = dbg_ref[...]
        dbgv = jnp.where(is_b, dbg_in, _mm32(_chunk_cumsum_matrix(tb), dbg_in, TN))
        is_g = jnp.logical_and(lane >= HEADS, lane < 2 * HEADS)
        beta = _sigmoid(bdv)
        neg_a = -jnp.exp(al_ref[...])
        pre_sp = bdv + dt_ref[...]
        g = neg_a * _softplus(pre_sp)
        da_in = dbgv * neg_a * _sigmoid(pre_sp)
        dbd_ref[...] = jnp.where(is_b, dbgv * beta * (1.0 - beta), jnp.where(is_g, da_in, 0.0)).astype(BF16)
        _row_acc(dal_ref, jnp.where(is_g, dbgv * g, 0.0))
        _row_acc(ddt_ref, jnp.where(is_g, da_in, 0.0))

    tok = lambda w: pl.BlockSpec((tb, w), lambda i: (i, 0))
    full = lambda t: pl.BlockSpec(t.shape, lambda i: (0, 0))
    acc = lambda w: pl.BlockSpec((8, w), lambda i: (0, 0))
    return pl.pallas_call(
        body, name="dn_prep_bwd", grid=(T // tb,),
        in_specs=[tok(DN_WIDTH), tok(DN_WIDTH), tok(DN_WIDTH), tok(LANES),
                  tok(QKV), pl.BlockSpec((8, QKV), _before_halo(tb)), full(cw), tok(LANES), full(al_row), full(dt_row), _ANY],
        out_specs=[tok(QKV), _dp_block(tb, P_BD, LANES), acc(QKV), acc(LANES), acc(LANES)],
        out_shape=[jax.ShapeDtypeStruct((T, QKV), F32), jax.ShapeDtypeStruct(dp.shape, BF16),
                   jax.ShapeDtypeStruct((8, QKV), F32), jax.ShapeDtypeStruct((8, LANES), F32),
                   jax.ShapeDtypeStruct((8, LANES), F32)],
        input_output_aliases={10: 1},
        compiler_params=_params(("arbitrary",)),
    )(dq, dk, dv, dbg, qkv, qkv, cw, bd, al_row, dt_row, dp)


def _dn_conv_bwd(dc, cw, dp):
    T = dc.shape[0]
    tb = 512

    def body(dc_ref, halo_ref, w_ref, dp_ref, out_ref):
        last = pl.program_id(0) == pl.num_programs(0) - 1
        halo = jnp.where(last, 0.0, halo_ref[...])
        xc = jnp.concatenate([dc_ref[...], halo], axis=0)
        w = w_ref[...]
        acc = w[3:4, :] * xc[0:tb, :]
        for j in range(3):
            acc = acc + w[j:j + 1, :] * _rows_from(xc, 3 - j, tb)
        out_ref[...] = acc.astype(BF16)

    tok = pl.BlockSpec((tb, QKV), lambda i: (i, 0))
    return pl.pallas_call(
        body, name="dn_conv_bwd", grid=(T // tb,),
        in_specs=[tok, pl.BlockSpec((8, QKV), _after_halo(tb, T)), pl.BlockSpec(cw.shape, lambda i: (0, 0)), _ANY],
        out_specs=_dp_block(tb, 0, QKV),
        out_shape=jax.ShapeDtypeStruct(dp.shape, BF16),
        input_output_aliases={3: 0},
        compiler_params=_params(("parallel",)),
    )(dc, dc, cw, dp)


def _in_proj_bwd(dp, dx1, x, g1, wp):
    T = x.shape[0]
    tb = 256

    def body(dp_ref, dx1_ref, x_ref, g_ref, wp_ref, dx_ref, dxb_ref, dg_ref):
        @pl.when(pl.program_id(0) == 0)
        def _():
            dg_ref[...] = jnp.zeros_like(dg_ref)

        dh = lax.dot_general(dp_ref[...], wp_ref[...], NT, preferred_element_type=F32)
        xv = x_ref[...]
        r = lax.rsqrt(jnp.mean(xv * xv, axis=-1, keepdims=True) + EPS)
        xh = xv * r
        _row_acc(dg_ref, dh * xh)
        dx = dx1_ref[...] + _rms_bwd(dh, xh, r, g_ref[...])
        dx_ref[...] = dx
        dxb_ref[...] = dx.astype(BF16)

    tok = lambda w: pl.BlockSpec((tb, w), lambda i: (i, 0))
    full = lambda t: pl.BlockSpec(t.shape, lambda i: (0, 0))
    return pl.pallas_call(
        body, name="in_proj_bwd", grid=(T // tb,),
        in_specs=[tok(P_COLS), tok(D_MODEL), tok(D_MODEL), full(g1), full(wp)],
        out_specs=[tok(D_MODEL), tok(D_MODEL), pl.BlockSpec((8, D_MODEL), lambda i: (0, 0))],
        out_shape=[jax.ShapeDtypeStruct((T, D_MODEL), F32), jax.ShapeDtypeStruct((T, D_MODEL), BF16),
                   jax.ShapeDtypeStruct((8, D_MODEL), F32)],
        compiler_params=_params(("arbitrary",)),
    )(dp, dx1, x, g1, wp)


def _pad_rows(a, rows=8):
    return jnp.pad(a, ((0, rows - a.shape[0]), (0, 0)))


def _gate_rows(a_log, dt_bias):
    put = lambda t: jnp.pad(t.reshape(1, HEADS), ((0, 0), (HEADS, LANES - 2 * HEADS)))
    return put(a_log), put(dt_bias)


W_Z = QKV
W_BD = W_Z + DN_WIDTH
W_SC = W_BD + 2 * HEADS


def _projection_of(w_in):
    return jnp.concatenate([w_in[:, :W_Z], w_in[:, W_SC:], w_in[:, W_Z:W_BD],
                            jnp.pad(w_in[:, W_BD:W_SC], ((0, 0), (0, LANES - 2 * HEADS)))], axis=1)


def _w_in_grad_of(g_wp):
    return jnp.concatenate([g_wp[:, :P_SC], g_wp[:, P_Z:P_BD], g_wp[:, P_BD:P_BD + 2 * HEADS], g_wp[:, P_SC:P_Z]], axis=1)


def _mixer_fwd(x, p):
    qkv, z, sc_in, bd, h = _in_proj(x, p["g1"], p["wp"])
    q, k, v, bg = _dn_prep(qkv, p["cw"], bd, p["al"], p["dt"])
    o, states = _delta_fwd(q, k, v, bg)
    x1, mix = _mix_out(o, z, sc_in, x, p["land_a"], p["gn"], p["scw"], p["gs"])
    return x1, dict(x=x, qkv=qkv, z=z, sc_in=sc_in, bd=bd, h=h, q=q, k=k, v=v, bg=bg, o=o, states=states, mix=mix)


def _ffn_fwd(x1, p, land_b):
    x2, a, b, h2 = _ffn(x1, p["g2"], land_b)
    return x2, dict(x1=x1, a=a, b=b, h2=h2)


def _ffn_back(dx2, dx2_bf16, s, p, land_b):
    dx1, dx1_bf16, da, db, act, dg2 = _ffn_bwd(dx2, s["x1"], s["a"], s["b"], p["g2"], land_b)
    parts = lax.empty((N_CHIPS, B_ROWS, D_MODEL), BF16)
    parts = _wgrad_share(act, dx2_bf16, parts, 2 * FF_SHARD, "wgrad_down")
    parts = _wgrad_share(da, s["h2"], parts, 0, "wgrad_gate")
    parts = _wgrad_share(db, s["h2"], parts, FF_SHARD, "wgrad_up")
    return dx1, dx1_bf16, parts, dg2[0]


def _mixer_bwd(dx1, dx1_bf16, s, p):
    dp = lax.empty((dx1.shape[0], P_COLS), BF16)
    do, dp, dgb, dcv, dgn, dgs, dscw = _mix_out_bwd(dx1, s["o"], s["z"], s["sc_in"], p["land_a"], p["gn"], p["scw"], p["gs"], dp)
    dp = _sc_conv_bwd(dcv, dgb, s["sc_in"], p["scw"], dp)
    dq, dk, dv, dbg = _delta_bwd(s["q"], s["k"], s["v"], s["bg"], s["states"], do)
    dc, dp, dcw, dal, ddt = _dn_prep_bwd(dq, dk, dv, dbg, s["qkv"], p["cw"], s["bd"], p["al"], p["dt"], dp)
    dp = _dn_conv_bwd(dc, p["cw"], dp)
    dx, dx_bf16, dg1 = _in_proj_bwd(dp, dx1, s["x"], p["g1"], p["wp"])
    g_w_in = _w_in_grad_of(_wgrad(s["h"], dp, 512, P_COLS, "wgrad_in"))
    cols = jnp.moveaxis(g_w_in.reshape(D_MODEL, N_CHIPS, IN_SHARD), 1, 0)
    parts = jnp.pad(cols, ((0, 0), (0, OUT_SHARD), (0, D_MODEL - IN_SHARD))).astype(BF16)
    parts = _wgrad_share(s["mix"], dx1_bf16, parts, A_OUT_AT, "wgrad_out")
    g = dict(g1=dg1[0], gn=dgn[0], gs=dgs[0], scw=dscw[:3], cw=dcw[:4], al=dal[0, HEADS:2 * HEADS], dt=ddt[0, HEADS:2 * HEADS])
    return dx, dx_bf16, parts, g


def _place():
    return lax.axis_index("x"), lax.axis_index("y"), lax.axis_index("c")


def _other_chips(x, y):
    return [(1 - x, y), (x, 1 - y), (1 - x, 1 - y)]


_HBM = pl.BlockSpec(memory_space=pltpu.HBM)


def _chip_exchange(arrs, name, gather):
    n = len(arrs)

    def body(*refs):
        ins, outs = refs[:n], refs[n:2 * n]
        send_sems, recv_sems, local_sems = refs[2 * n:]
        x, y, c = _place()
        me = 2 * x + y
        others = _other_chips(x, y)

        def remote(k, j, landing):
            px, py = others[j]
            src = ins[k] if gather else ins[k].at[2 * px + py]
            return pltpu.make_async_remote_copy(src_ref=src, dst_ref=outs[k].at[landing], send_sem=send_sems.at[k, j],
                                                recv_sem=recv_sems.at[k, j], device_id=(px, py, c), device_id_type=MESH)

        local = [pltpu.make_async_copy(ins[k] if gather else ins[k].at[me], outs[k].at[me], local_sems.at[k])
                 for k in range(n)]
        sends = [remote(k, j, me) for k in range(n) for j in range(3)]
        for cp in local + sends:
            cp.start()
        for k in range(n):
            for j, (px, py) in enumerate(others):
                remote(k, j, 2 * px + py).wait_recv()
        for cp in sends:
            cp.wait_send()
        for cp in local:
            cp.wait()

    shapes = [jax.ShapeDtypeStruct(((N_CHIPS,) + a.shape) if gather else a.shape, a.dtype) for a in arrs]
    return pl.pallas_call(
        body, name=name, in_specs=[_HBM] * n, out_specs=[_HBM] * n, out_shape=shapes,
        scratch_shapes=[pltpu.SemaphoreType.DMA((n, 3)), pltpu.SemaphoreType.DMA((n, 3)), pltpu.SemaphoreType.DMA((n,))],
    )(*arrs)


_SEM = pl.BlockSpec(memory_space=pltpu.SEMAPHORE)
_ANY = pl.BlockSpec(memory_space=pl.ANY)
_EFFECT = pltpu.SideEffectType.DATAFLOW_SIDE_EFFECTING


def _split_copies(src_ref, land_ref, send_sems, recv_sems, gather, sending):
    x, y, c = _place()
    me = 2 * x + y
    copies = []
    for j, (px, py) in enumerate(_other_chips(x, y)):
        peer = 2 * px + py
        copies.append(pltpu.make_async_remote_copy(
            src_ref=src_ref if gather else src_ref.at[peer], dst_ref=land_ref.at[me if sending else peer],
            send_sem=send_sems.at[j], recv_sem=recv_sems.at[j], device_id=(px, py, c), device_id_type=MESH))
    return copies


def _own_slot(share):
    chip = 2 * lax.axis_index("x") + lax.axis_index("y")
    return lax.dynamic_update_slice(lax.empty((N_CHIPS,) + share.shape, share.dtype), share[None], (chip, 0, 0))


def _exchange_start(src, land, after, name, gather):
    def body(src_ref, land_ref, after_ref, send_sems, recv_sems, src_thru, land_thru, token):
        for cp in _split_copies(src_ref, land_ref, send_sems, recv_sems, gather, sending=True):
            cp.start()
        token[...] = jnp.zeros_like(token)

    hbm = lambda t: pltpu.with_memory_space_constraint(t, pltpu.HBM)
    return pl.pallas_call(
        body, name=name,
        out_shape=(pltpu.SemaphoreType.DMA((3,)), pltpu.SemaphoreType.DMA((3,)), pltpu.HBM(src.shape, src.dtype),
                   pltpu.HBM(land.shape, land.dtype), jax.ShapeDtypeStruct((8, LANES), F32)),
        in_specs=(_HBM, _HBM, _ANY), out_specs=(_SEM, _SEM, _HBM, _HBM, pl.BlockSpec(memory_space=pltpu.VMEM)),
        input_output_aliases={0: 2, 1: 3},
        compiler_params=pltpu.CompilerParams(has_side_effects=_EFFECT),
    )(hbm(src), hbm(land), after)


def _exchange_wait(started, after, name, gather):
    send_sems, recv_sems, src_thru, land_thru, _ = started

    def body(src_ref, land_ref, send_sems, recv_sems, after_ref, src_dead, got_ref):
        for cp in _split_copies(src_ref, land_ref, send_sems, recv_sems, gather, sending=False):
            cp.wait_send()
            cp.wait_recv()

    return pl.pallas_call(
        body, name=name,
        out_shape=(pltpu.HBM(src_thru.shape, src_thru.dtype), pltpu.HBM(land_thru.shape, land_thru.dtype)),
        in_specs=(_HBM, _HBM, _SEM, _SEM, _ANY), out_specs=(_HBM, _HBM), input_output_aliases={0: 0, 1: 1},
        compiler_params=pltpu.CompilerParams(has_side_effects=_EFFECT),
    )(src_thru, land_thru, send_sems, recv_sems, after)[1]


def _swap_sibling(arrs, name):
    n = len(arrs)

    def body(*refs):
        ins, outs = refs[:n], refs[n:2 * n]
        send_sems, recv_sems = refs[2 * n:]
        x, y, c = _place()
        copies = [pltpu.make_async_remote_copy(src_ref=ins[k], dst_ref=outs[k], send_sem=send_sems.at[k],
                                               recv_sem=recv_sems.at[k], device_id=(x, y, 1 - c), device_id_type=MESH)
                  for k in range(n)]
        for cp in copies:
            cp.start()
        for cp in copies:
            cp.wait()

    return pl.pallas_call(
        body, name=name, in_specs=[_HBM] * n, out_specs=[_HBM] * n,
        out_shape=[jax.ShapeDtypeStruct(a.shape, a.dtype) for a in arrs],
        scratch_shapes=[pltpu.SemaphoreType.DMA((n,)), pltpu.SemaphoreType.DMA((n,))],
    )(*arrs)


def _all_reduce_small(v):
    rows = v.shape[0]
    flips = [(a, b, cc) for a in (0, 1) for b in (0, 1) for cc in (0, 1)][1:]

    def body(v_ref, out_ref, buf_ref, send_sems, recv_sems):
        x, y, c = _place()
        me = 4 * x + 2 * y + c
        peers = [((1 - x) if a else x, (1 - y) if b else y, (1 - c) if cc else c) for a, b, cc in flips]

        def copy(j, landing):
            return pltpu.make_async_remote_copy(src_ref=v_ref, dst_ref=buf_ref.at[landing], send_sem=send_sems.at[j],
                                                recv_sem=recv_sems.at[j], device_id=peers[j], device_id_type=MESH)

        sends = [copy(j, me) for j in range(N_DEV - 1)]
        for cp in sends:
            cp.start()
        buf_ref[me] = v_ref[...]
        for j, (px, py, pc) in enumerate(peers):
            copy(j, 4 * px + 2 * py + pc).wait_recv()
        for cp in sends:
            cp.wait_send()
        acc = buf_ref[0]
        for d in range(1, N_DEV):
            acc = acc + buf_ref[d]
        out_ref[...] = acc

    vmem = pl.BlockSpec(memory_space=pltpu.VMEM)
    return pl.pallas_call(
        body, name="all_reduce_small", in_specs=[vmem], out_specs=vmem,
        out_shape=jax.ShapeDtypeStruct(v.shape, F32),
        scratch_shapes=[pltpu.VMEM((N_DEV, rows, LANES), F32), pltpu.SemaphoreType.DMA((N_DEV - 1,)),
                        pltpu.SemaphoreType.DMA((N_DEV - 1,))],
    )(v)


def _row_block(*sizes):
    return next(t for t in (256, 192, 128, 64) if all(s % t == 0 for s in sizes))


def _sum_chips(parts, name):
    _, rows, cols = parts[0].shape
    n = len(parts)
    tr = _row_block(rows)

    def body(*refs):
        o_ref = refs[n]
        for l in range(n):
            @pl.when(pl.program_id(0) == l)
            def _(p_ref=refs[l]):
                acc = p_ref[0].astype(F32)
                for s in range(1, N_CHIPS):
                    acc = acc + p_ref[s].astype(F32)
                o_ref[0] = acc

    return pl.pallas_call(
        body, name=name, grid=(n, rows // tr),
        in_specs=[pl.BlockSpec((N_CHIPS, tr, cols), lambda l, i, k=k: (0, jnp.where(l == k, i, 0), 0)) for k in range(n)],
        out_specs=pl.BlockSpec((1, tr, cols), lambda l, i: (l, i, 0)),
        out_shape=jax.ShapeDtypeStruct((n, rows, cols), F32),
        compiler_params=_params(("arbitrary", "arbitrary")),
    )(*parts)


def _adam_update(w, m, v, g):
    c1 = 1.0 - ADAM_B1 ** ADAM_STEP
    c2 = 1.0 - ADAM_B2 ** ADAM_STEP
    m_new = ADAM_B1 * m + (1.0 - ADAM_B1) * g
    v_new = ADAM_B2 * v + (1.0 - ADAM_B2) * (g * g)
    return -ADAM_LR * ((m_new / c1) / (jnp.sqrt(v_new / c2) + ADAM_EPS) + ADAM_WD * w), m_new, v_new


def _adamw_rows(w, m, v, g_parts, first, name):
    n_layers, rows, cols = w.shape
    tr = _row_block(rows, first)
    n = len(g_parts)

    def body(*refs):
        w_ref, m_ref, v_ref = refs[:3]
        g_out, d_out, m_out, v_out = refs[3 + n:]
        g = refs[3][...]
        for r in refs[4:3 + n]:
            g = g + r[...]
        g = g[:, :, :cols]
        d_out[...], m_out[...], v_out[...] = _adam_update(w_ref[...], m_ref[...], v_ref[...], g)
        g_out[...] = g

    blk = pl.BlockSpec((1, tr, cols), lambda l, i: (l, i, 0))
    g_blk = pl.BlockSpec((1, tr, g_parts[0].shape[2]), lambda l, i: (l, first // tr + i, 0))
    return pl.pallas_call(
        body, name=name, grid=(n_layers, rows // tr),
        in_specs=[blk] * 3 + [g_blk] * n, out_specs=[blk] * 4,
        out_shape=[jax.ShapeDtypeStruct(w.shape, F32)] * 4,
        compiler_params=_params(("parallel", "parallel")),
    )(w, m, v, *g_parts)


def _adamw(w, m, v, g_parts, name):
    rows, cols = w.shape
    tr = min(rows, 256)
    n = len(g_parts)

    def body(*refs):
        w_ref, m_ref, v_ref = refs[:3]
        g_refs = refs[3:3 + n]
        g_out, d_out, m_out, v_out = refs[3 + n:]
        g = g_refs[0][...]
        for r in g_refs[1:]:
            g = g + r[...]
        d_out[...], m_out[...], v_out[...] = _adam_update(w_ref[...], m_ref[...], v_ref[...], g)
        g_out[...] = g

    blk = pl.BlockSpec((tr, cols), lambda i: (i, 0))
    return pl.pallas_call(
        body, name=name, grid=(rows // tr,),
        in_specs=[blk] * (3 + n), out_specs=[blk] * 4,
        out_shape=[jax.ShapeDtypeStruct((rows, cols), F32)] * 4,
        compiler_params=_params(("parallel",)),
    )(w, m, v, *g_parts)


def _pack(parts, rows, fill=0.0):
    flat = jnp.concatenate([p.reshape(-1) for p in parts])
    return jnp.pad(flat, (0, rows * LANES - flat.shape[0]), constant_values=fill).reshape(rows, LANES)


def _unpack(packed, shapes):
    flat = packed.reshape(-1)
    out, at = [], 0
    for shp in shapes:
        size = 1
        for s in shp:
            size *= s
        out.append(flat[at:at + size].reshape(shp))
        at += size
    return out


def _packed_rows(shapes):
    total = 0
    for shp in shapes:
        size = 1
        for s in shp:
            size *= s
        total += size
    return -(-total // (8 * LANES)) * 8


def _cols_full(g, l):
    t = g[:, l]
    return jnp.moveaxis(t, 0, 1).reshape(t.shape[1], N_CHIPS * t.shape[2])


def _pad_cols(t):
    return jnp.pad(t, ((0, 0),) * (t.ndim - 1) + ((0, D_MODEL - t.shape[-1]),))


def _w_in_of(land_a):
    return jnp.moveaxis(land_a[:, :D_MODEL, :IN_SHARD], 0, 1).reshape(D_MODEL, W_IN_COLS)


def kernel(x, norm1_g, w_in, dn_conv_w, dn_a_log, dn_dt_bias, dn_norm_g, sc_conv_w, sc_norm_g, w_out, norm2_g, ffn_w_gate, ffn_w_up, ffn_w_down, final_norm_g, loss_target, m_norm1_g, m_w_in, m_dn_conv_w, m_dn_a_log, m_dn_dt_bias, m_dn_norm_g, m_sc_conv_w, m_sc_norm_g, m_w_out, m_norm2_g, m_ffn_w_gate, m_ffn_w_up, m_ffn_w_down, m_final_norm_g, v_norm1_g, v_w_in, v_dn_conv_w, v_dn_a_log, v_dn_dt_bias, v_dn_norm_g, v_sc_conv_w, v_sc_norm_g, v_w_out, v_norm2_g, v_ffn_w_gate, v_ffn_w_up, v_ffn_w_down, v_final_norm_g):
    chip = 2 * lax.axis_index("x") + lax.axis_index("y")

    g_cw, g_scw = _chip_exchange([dn_conv_w, sc_conv_w], "gather_conv", gather=True)

    t_last = lambda t: jnp.swapaxes(t, -1, -2)
    gate_t, up_t = t_last(ffn_w_gate), t_last(ffn_w_up)
    zero_token = jnp.zeros((8, LANES), F32)

    def shares(l, tie):
        share_a = jnp.concatenate([_pad_cols(w_in[l] + tie), w_out[l]], axis=0).astype(BF16)
        share_b = jnp.concatenate([gate_t[l] + tie, up_t[l], ffn_w_down[l]], axis=0).astype(BF16)
        return share_a, _own_slot(share_a), share_b, _own_slot(share_b)

    def gather_start(l, packed, after):
        a = _exchange_start(packed[0], packed[1], after, "gather_a_start_%d" % l, gather=True)
        b = _exchange_start(packed[2], packed[3], a[4], "gather_b_start_%d" % l, gather=True)
        return a, b

    ga, gb = gather_start(0, shares(0, 0.0), g_cw)
    packed = [None] + [shares(l, gb[4][0, 0]) for l in range(1, DEPTH)]
    packed_all = sum(t[0, 0].astype(F32) for p in packed[1:] for t in (p[0], p[2]))
    land_a = _exchange_wait(ga, zero_token + packed_all, "gather_a_wait_0", gather=True)
    act = x[0]
    layers, saved_m, saved_f, lands_b = [], [], [], []
    for l in range(DEPTH):
        hold = 0.0
        if l + 1 < DEPTH:
            ga, gb_next = gather_start(l + 1, packed[l + 1], land_a)
            hold = gb_next[4][0:1, 0:1]
        al, dt = _gate_rows(dn_a_log[l], dn_dt_bias[l])
        layers.append(dict(
            g1=norm1_g[l][None] + hold, wp=_projection_of(_w_in_of(land_a)), cw=_pad_rows(_cols_full(g_cw, l)), al=al, dt=dt,
            gn=dn_norm_g[l][None], scw=_pad_rows(_cols_full(g_scw, l)), gs=sc_norm_g[l][None],
            land_a=land_a, g2=norm2_g[l][None]))
        x1, s = _mixer_fwd(act, layers[l])
        saved_m.append(s)
        lands_b.append(_exchange_wait(gb, x1, "gather_b_wait_%d" % l, gather=True))
        act, s = _ffn_fwd(x1, layers[l], lands_b[l])
        saved_f.append(s)
        if l + 1 < DEPTH:
            land_a = _exchange_wait(ga, act, "gather_a_wait_%d" % (l + 1), gather=True)
            gb = gb_next

    dact, dact_bf16, loss_part, d_final = _loss_head(act, final_norm_g[None], loss_target[0])
    grads, reduce_a, reduce_b = [None] * DEPTH, [None] * DEPTH, [None] * DEPTH
    hold = 0.0
    for l in reversed(range(DEPTH)):
        p = layers[l]
        dx1, dx1_bf16, parts, dg2 = _ffn_back(dact, dact_bf16, saved_f[l], dict(p, g2=p["g2"] + hold), lands_b[l])
        reduce_b[l] = _exchange_start(parts, parts, zero_token, "reduce_b_start_%d" % l, gather=False)
        dact, dact_bf16, parts, gm = _mixer_bwd(dx1, dx1_bf16, saved_m[l], dict(p, gn=p["gn"] + reduce_b[l][4][0:1, 0:1]))
        reduce_a[l] = _exchange_start(parts, parts, zero_token, "reduce_a_start_%d" % l, gather=False)
        hold = reduce_a[l][4][0:1, 0:1]
        grads[l] = dict(gm, g2=dg2)
    loss = lax.psum(loss_part[0, 0], ("x", "y", "c"))
    stack = lambda key: jnp.stack([grads[l][key] for l in range(DEPTH)])

    got_b = [_exchange_wait(reduce_b[l], reduce_a[0][4], "reduce_b_wait_%d" % l, gather=False)
             for l in reversed(range(DEPTH))][::-1]
    sum_b = _sum_chips(got_b, "sum_chips_b")
    other_b, = _swap_sibling([sum_b], "swap_sibling_b")
    big = dict(
        ffn_w_gate=[t_last(o) for o in _adamw_rows(gate_t, t_last(m_ffn_w_gate), t_last(v_ffn_w_gate),
                                                   [sum_b, other_b], 0, "adamw_gate")],
        ffn_w_up=[t_last(o) for o in _adamw_rows(up_t, t_last(m_ffn_w_up), t_last(v_ffn_w_up),
                                                 [sum_b, other_b], FF_SHARD, "adamw_up")],
        ffn_w_down=_adamw_rows(ffn_w_down, m_ffn_w_down, v_ffn_w_down, [sum_b, other_b], 2 * FF_SHARD, "adamw_down"))
    after_b = big["ffn_w_down"][1]
    got_a = [_exchange_wait(reduce_a[l], after_b, "reduce_a_wait_%d" % l, gather=False) for l in reversed(range(DEPTH))][::-1]
    sum_a = _sum_chips(got_a, "sum_chips_a")
    other_a, = _swap_sibling([sum_a], "swap_sibling_a")
    big.update(
        w_in=_adamw_rows(w_in, m_w_in, v_w_in, [sum_a, other_a], 0, "adamw_w_in"),
        w_out=_adamw_rows(w_out, m_w_out, v_w_out, [sum_a, other_a], A_OUT_AT, "adamw_w_out"))

    full_shapes = [(DEPTH, D_MODEL), (DEPTH, D_MODEL), (DEPTH, HEAD_DIM), (DEPTH, SC_WIDTH), (DEPTH, HEADS),
                   (DEPTH, HEADS), (D_MODEL,), (DEPTH, 4, QKV), (DEPTH, 3, SC_WIDTH)]
    small_keys = ("g1", "g2", "gn", "gs", "al", "dt")
    packed = _pack([stack(k) for k in small_keys] + [d_final[0], stack("cw"), stack("scw")], _packed_rows(full_shapes))
    sg = _unpack(_all_reduce_small(packed), full_shapes)
    sg[7] = lax.dynamic_slice_in_dim(sg[7], chip * (QKV // N_CHIPS), QKV // N_CHIPS, axis=2)
    sg[8] = lax.dynamic_slice_in_dim(sg[8], chip * (SC_WIDTH // N_CHIPS), SC_WIDTH // N_CHIPS, axis=2)
    small_names = ("norm1_g", "norm2_g", "dn_norm_g", "sc_norm_g", "dn_a_log", "dn_dt_bias", "final_norm_g",
                   "dn_conv_w", "sc_conv_w")
    sw = (norm1_g, norm2_g, dn_norm_g, sc_norm_g, dn_a_log, dn_dt_bias, final_norm_g, dn_conv_w, sc_conv_w)
    sm = (m_norm1_g, m_norm2_g, m_dn_norm_g, m_sc_norm_g, m_dn_a_log, m_dn_dt_bias, m_final_norm_g, m_dn_conv_w, m_sc_conv_w)
    sv = (v_norm1_g, v_norm2_g, v_dn_norm_g, v_sc_norm_g, v_dn_a_log, v_dn_dt_bias, v_final_norm_g, v_dn_conv_w, v_sc_conv_w)
    shard_shapes = [t.shape for t in sw]
    rows = _packed_rows(shard_shapes)
    outs = _adamw(_pack(sw, rows), _pack(sm, rows), _pack(sv, rows, fill=1.0), [_pack(sg, rows)], "adamw_small")
    small = {name: [] for name in small_names}
    for o in outs:
        for name, t in zip(small_names, _unpack(o, shard_shapes)):
            small[name].append(t)

    order = ("norm1_g", "w_in", "dn_conv_w", "dn_a_log", "dn_dt_bias", "dn_norm_g", "sc_conv_w", "sc_norm_g", "w_out",
             "norm2_g", "ffn_w_gate", "ffn_w_up", "ffn_w_down", "final_norm_g")
    result = {**big, **small}
    return (loss, dact[None], *[result[n][0] for n in order], *[result[n][1] for n in order],
            *[result[n][2] for n in order], *[result[n][3] for n in order])
```

```python
import jax
import jax.numpy as jnp
from jax import lax
from jax.experimental import pallas as pl
from jax.experimental.pallas import tpu as pltpu

F32 = jnp.float32
BF16 = jnp.bfloat16
MESH = pl.DeviceIdType.MESH

D_MODEL = 1024
DEPTH = 4
HEADS = 4
HEAD_DIM = 128
DN_WIDTH = HEADS * HEAD_DIM
SC_WIDTH = 512
SC_GROUPS = 4
D_FF = 2816
CHUNK = 64
QKV = 3 * DN_WIDTH
W_IN_COLS = 4 * DN_WIDTH + 2 * HEADS + 3 * SC_WIDTH
WA_COLS = QKV + DN_WIDTH + 3 * SC_WIDTH
LANES = 128
EPS = 1e-6
Q_SCALE = HEAD_DIM ** -0.5
N_CHIPS = 4
N_DEV = 8
IN_SHARD = W_IN_COLS // N_CHIPS
OUT_SHARD = D_MODEL // N_CHIPS
FF_SHARD = D_FF // N_CHIPS
A_OUT_AT = D_MODEL
A_ROWS = D_MODEL + OUT_SHARD
B_ROWS = 3 * FF_SHARD

ADAM_LR = 0.001
ADAM_B1 = 0.9
ADAM_B2 = 0.999
ADAM_EPS = 1e-08
ADAM_WD = 0.01
ADAM_STEP = 10

VMEM_LIMIT = 56 * 1024 * 1024

NN = (((1,), (0,)), ((), ()))
NT = (((1,), (1,)), ((), ()))
TN = (((0,), (0,)), ((), ()))


def _mm(a, b, dims=NN):
    return lax.dot_general(a.astype(BF16), b.astype(BF16), dims, preferred_element_type=F32)


def _mm32(a, b, dims=NN):
    return lax.dot_general(a, b, dims, preferred_element_type=F32, precision=lax.Precision.HIGHEST)


def _params(sem, vmem=VMEM_LIMIT):
    return pltpu.CompilerParams(dimension_semantics=sem, vmem_limit_bytes=vmem)


def _sigmoid(x):
    return 0.5 * jnp.tanh(0.5 * x) + 0.5


def _softplus(x):
    return jnp.maximum(x, 0.0) + jnp.log1p(jnp.exp(-jnp.abs(x)))


def _row_acc(acc_ref, val):
    acc_ref[0:1, :] += jnp.sum(val, axis=0, keepdims=True)


def _rms_bwd(dh, xh, r, gain):
    dxh = dh * gain
    return r * (dxh - xh * jnp.mean(dxh * xh, axis=-1, keepdims=True))


def _before_halo(tb):
    return lambda i: (jnp.maximum(i * (tb // 8) - 1, 0), 0)


def _after_halo(tb, n_rows):
    last = n_rows // 8 - 1
    return lambda i: (jnp.minimum((i + 1) * (tb // 8), last), 0)


def _rows_from(xc, offset, tb):
    part = offset % 8
    if part:
        xc = pltpu.roll(xc, xc.shape[0] - part, 0)
    return xc[offset - part:offset - part + tb, :]


def _taps(xc, w, n_taps, tb, first):
    out = w[0:1, :] * _rows_from(xc, first, tb)
    for j in range(1, n_taps):
        out = out + w[j:j + 1, :] * _rows_from(xc, first + j, tb)
    return out


W_Z = QKV
W_BD = W_Z + DN_WIDTH
W_SC = W_BD + 2 * HEADS
P_SC = QKV
P_Z = P_SC + 3 * SC_WIDTH
P_BD = P_Z + DN_WIDTH
P_COLS = P_BD + LANES


def _w_in_cols(shards, lo, hi):
    pieces = []
    for s in range(N_CHIPS):
        a, b = max(lo, IN_SHARD * s), min(hi, IN_SHARD * (s + 1))
        if a < b:
            pieces.append(shards[s][:, a - IN_SHARD * s:b - IN_SHARD * s])
    return pieces[0] if len(pieces) == 1 else jnp.concatenate(pieces, axis=1)


def _in_proj(x, g1, land_a):
    T = x.shape[0]
    tb = 256

    def body(x_ref, g_ref, w_ref, qkv_ref, z_ref, sc_ref, bd_ref, h_ref):
        xv = x_ref[...]
        r = lax.rsqrt(jnp.mean(xv * xv, axis=-1, keepdims=True) + EPS)
        h = (xv * r * g_ref[...]).astype(BF16)
        shards = [jnp.dot(h, w_ref[s], preferred_element_type=F32) for s in range(N_CHIPS)]
        qkv_ref[...] = _w_in_cols(shards, 0, W_Z)
        z_ref[...] = _w_in_cols(shards, W_Z, W_BD)
        bd_ref[...] = jnp.concatenate([_w_in_cols(shards, W_BD, W_SC), jnp.zeros((tb, LANES - 2 * HEADS), F32)], axis=1)
        sc_ref[...] = _w_in_cols(shards, W_SC, W_IN_COLS)
        h_ref[...] = h

    tok = lambda w: pl.BlockSpec((tb, w), lambda i: (i, 0))
    return pl.pallas_call(
        body, name="in_proj", grid=(T // tb,),
        in_specs=[tok(D_MODEL), pl.BlockSpec(g1.shape, lambda i: (0, 0)), _shard_rows(land_a, 0, D_MODEL)],
        out_specs=[tok(QKV), tok(DN_WIDTH), tok(3 * SC_WIDTH), tok(LANES), tok(D_MODEL)],
        out_shape=[jax.ShapeDtypeStruct((T, QKV), F32), jax.ShapeDtypeStruct((T, DN_WIDTH), F32),
                   jax.ShapeDtypeStruct((T, 3 * SC_WIDTH), F32), jax.ShapeDtypeStruct((T, LANES), F32),
                   jax.ShapeDtypeStruct((T, D_MODEL), BF16)],
        compiler_params=_params(("parallel",)),
    )(x, g1, land_a)


def _dp_block(tb, first, width, index=lambda i: i):
    assert first % width == 0
    return pl.BlockSpec((tb, width), lambda i: (index(i), first // width))


def _dn_act(pre, halo, cw, tb):
    xc = jnp.concatenate([halo, pre], axis=0)
    c = _taps(xc, cw, 4, tb, 5)
    sg = _sigmoid(c)
    return xc, c, sg, c * sg


def _gates(bd, al_row, dt_row):
    lane = lax.broadcasted_iota(jnp.int32, bd.shape, 1)
    beta = _sigmoid(bd)
    g = -jnp.exp(al_row) * _softplus(bd + dt_row)
    return jnp.where(lane < HEADS, beta, jnp.where(lane < 2 * HEADS, g, 0.0))


def _dn_prep(qkv, cw, bd, al_row, dt_row):
    T = qkv.shape[0]
    tb = 512

    def body(pre_ref, halo_ref, cw_ref, bd_ref, al_ref, dt_ref, q_ref, k_ref, v_ref, bg_ref):
        halo = jnp.where(pl.program_id(0) > 0, halo_ref[...], 0.0)
        _, _, _, a = _dn_act(pre_ref[...], halo, cw_ref[...], tb)
        for hh in range(HEADS):
            sl = slice(HEAD_DIM * hh, HEAD_DIM * (hh + 1))
            qs = a[:, sl]
            q_ref[:, sl] = qs * (lax.rsqrt(jnp.sum(qs * qs, axis=-1, keepdims=True) + EPS) * Q_SCALE)
            ks = a[:, DN_WIDTH + HEAD_DIM * hh:DN_WIDTH + HEAD_DIM * (hh + 1)]
            k_ref[:, sl] = ks * lax.rsqrt(jnp.sum(ks * ks, axis=-1, keepdims=True) + EPS)
        v_ref[...] = a[:, 2 * DN_WIDTH:]
        gates = _gates(bd_ref[...], al_ref[...], dt_ref[...])
        lane = lax.broadcasted_iota(jnp.int32, gates.shape, 1)
        bg_ref[...] = jnp.where(lane < HEADS, gates, _mm32(_chunk_cumsum_matrix(tb), gates))

    tok = lambda w: pl.BlockSpec((tb, w), lambda i: (i, 0))
    full = lambda a: pl.BlockSpec(a.shape, lambda i: (0, 0))
    return pl.pallas_call(
        body, name="dn_prep", grid=(T // tb,),
        in_specs=[tok(QKV), pl.BlockSpec((8, QKV), _before_halo(tb)), full(cw), tok(LANES), full(al_row), full(dt_row)],
        out_specs=[tok(DN_WIDTH), tok(DN_WIDTH), tok(DN_WIDTH), tok(LANES)],
        out_shape=[jax.ShapeDtypeStruct((T, DN_WIDTH), F32)] * 3 + [jax.ShapeDtypeStruct((T, LANES), F32)],
        compiler_params=_params(("parallel",)),
    )(qkv, qkv, cw, bd, al_row, dt_row)


def _chunk_masks():
    row = lax.broadcasted_iota(jnp.int32, (CHUNK, CHUNK), 0)
    col = lax.broadcasted_iota(jnp.int32, (CHUNK, CHUNK), 1)
    return row >= col, row > col


def _chunk_cumsum_matrix(n):
    row = lax.broadcasted_iota(jnp.int32, (n, n), 0)
    col = lax.broadcasted_iota(jnp.int32, (n, n), 1)
    return jnp.logical_and(row >= col, row // CHUNK == col // CHUNK).astype(F32)


def _chunk_units(q_ref, k_ref, v_ref, bg_ref, rows):
    bgc = bg_ref[rows, :]
    bg_t = bgc.T
    qv, kv, vv = q_ref[rows, :], k_ref[rows, :], v_ref[rows, :]
    units = []
    for h in range(HEADS):
        sl = slice(HEAD_DIM * h, HEAD_DIM * (h + 1))
        units.append((qv[:, sl], kv[:, sl], vv[:, sl], bgc[:, h:h + 1], bgc[:, HEADS + h:HEADS + h + 1],
                      bg_t[HEADS + h:HEADS + h + 1, :]))
    return units


def _units_local(units, masks):
    causal, strict = masks
    pre = []
    for q, k, v, beta, gc, gr in units:
        kb = k * beta
        eg = jnp.exp(gc)
        g_last = gc[CHUNK - 1:CHUNK, :]
        ek = jnp.exp(g_last - gc)
        pre.append(dict(q=q, k=k, v=v, beta=beta, decay=jnp.exp(jnp.where(causal, gc - gr, -1e30)), kb=kb, vb=v * beta,
                        eg=eg, kbg=kb * eg, ek=ek, gl=jnp.exp(g_last), q_dec=q * eg, k_dec=k * ek))
    both = [_mm(jnp.concatenate([p["kb"], p["q"]], axis=0), p["k"], NT) for p in pre]
    for p, b in zip(pre, both):
        p["low"] = jnp.where(strict, b[:CHUNK] * p["decay"], 0.0)
        p["qk"] = jnp.where(causal, b[CHUNK:] * p["decay"], 0.0)
    xs = [-p["low"] for p in pre]
    pw = [_mm(p["low"], p["low"]) for p in pre]
    for _ in range(4):
        both = [_mm(jnp.concatenate([pp, x], axis=0), pp) for pp, x in zip(pw, xs)]
        xs = [x + pp + b[CHUNK:] for x, pp, b in zip(xs, pw, both)]
        pw = [b[:CHUNK] for b in both]
    last = [_mm(x, pp) for x, pp in zip(xs, pw)]
    xs = [x + pp + b for x, pp, b in zip(xs, pw, last)]
    uw = [_mm(x, jnp.concatenate([p["vb"], p["kbg"]], axis=1)) for x, p in zip(xs, pre)]
    for p, x, b in zip(pre, xs, uw):
        p["xm"] = x
        p["u"] = p["vb"] + b[:, :HEAD_DIM]
        p["w"] = p["kbg"] + b[:, HEAD_DIM:]
    return pre


def _delta_fwd(q, k, v, bg):
    T = q.shape[0]
    tb = 512
    n_chunk = tb // CHUNK

    def body(q_ref, k_ref, v_ref, bg_ref, o_ref, st_ref, s_ref):
        @pl.when(pl.program_id(0) == 0)
        def _():
            s_ref[...] = jnp.zeros_like(s_ref)

        masks = _chunk_masks()

        def pair(pi, carry):
            rows = [pl.ds(pl.multiple_of((2 * pi + j) * CHUNK, CHUNK), CHUNK) for j in range(2)]
            loc = _units_local(_chunk_units(q_ref, k_ref, v_ref, bg_ref, rows[0])
                               + _chunk_units(q_ref, k_ref, v_ref, bg_ref, rows[1]), masks)
            states = [s_ref[h] for h in range(HEADS)]
            for j in range(2):
                lj = loc[HEADS * j:HEADS * (j + 1)]
                ws = [_mm(jnp.concatenate([p["w"], p["q_dec"]], axis=0), s) for p, s in zip(lj, states)]
                v_new = [p["u"] - b[:CHUNK] for p, b in zip(lj, ws)]
                intra = [_mm(p["qk"], vn) for p, vn in zip(lj, v_new)]
                upd = [_mm(p["k_dec"], vn, TN) for p, vn in zip(lj, v_new)]
                o_ref[rows[j], :] = jnp.concatenate([b[CHUNK:] + a for b, a in zip(ws, intra)], axis=1)
                for h in range(HEADS):
                    st_ref[2 * pi + j, h] = states[h]
                states = [p["gl"] * s + d for p, s, d in zip(lj, states, upd)]
            for h in range(HEADS):
                s_ref[h] = states[h]
            return carry

        lax.fori_loop(0, n_chunk // 2, pair, 0)

    tok = lambda w: pl.BlockSpec((tb, w), lambda i: (i, 0))
    return pl.pallas_call(
        body, name="delta_fwd", grid=(T // tb,),
        in_specs=[tok(DN_WIDTH), tok(DN_WIDTH), tok(DN_WIDTH), tok(LANES)],
        out_specs=[tok(DN_WIDTH), pl.BlockSpec((n_chunk, HEADS, HEAD_DIM, HEAD_DIM), lambda i: (i, 0, 0, 0))],
        out_shape=[jax.ShapeDtypeStruct((T, DN_WIDTH), F32),
                   jax.ShapeDtypeStruct((T // CHUNK, HEADS, HEAD_DIM, HEAD_DIM), F32)],
        scratch_shapes=[pltpu.VMEM((HEADS, HEAD_DIM, HEAD_DIM), F32)],
        compiler_params=_params(("arbitrary",)),
    )(q, k, v, bg)


def _dn_out(o, z, gn):
    outs, ohs, rs = [], [], []
    for hh in range(HEADS):
        oh = o[:, HEAD_DIM * hh:HEAD_DIM * (hh + 1)]
        r = lax.rsqrt(jnp.mean(oh * oh, axis=-1, keepdims=True) + EPS)
        ohs.append(oh * r)
        rs.append(r)
    sz = _sigmoid(z)
    oh = jnp.concatenate(ohs, axis=1)
    gn4 = jnp.concatenate([gn] * HEADS, axis=1)
    return oh * gn4 * (z * sz), oh, rs, sz, gn4


def _sc_fwd(sc_in, halo, cw, tb):
    xc = jnp.concatenate([halo, sc_in], axis=0)
    u = xc[:, SC_WIDTH:2 * SC_WIDTH] * xc[:, 2 * SC_WIDTH:]
    cv = _taps(u, cw, 3, tb, 6)
    gate_b = sc_in[:, :SC_WIDTH]
    y = gate_b * cv
    gw = SC_WIDTH // SC_GROUPS
    yhs, rs = [], []
    for gi in range(SC_GROUPS):
        yg = y[:, gw * gi:gw * (gi + 1)]
        r = lax.rsqrt(jnp.mean(yg * yg, axis=-1, keepdims=True) + EPS)
        yhs.append(yg * r)
        rs.append(r)
    return u, cv, gate_b, jnp.concatenate(yhs, axis=1), rs


def _shard_rows(land, first, rows):
    assert first % rows == 0 and land.shape[0] == N_CHIPS
    return pl.BlockSpec((N_CHIPS, rows, land.shape[2]), lambda i: (0, first // rows, 0))


def _whole(w_ref):
    n, rows, cols = w_ref.shape
    return w_ref[...].reshape(n * rows, cols)


def _mix_out(o, z, sc_in, x, land_a, gn, scw, gs):
    T = x.shape[0]
    tb = 256

    def body(o_ref, z_ref, sc_ref, halo_ref, x_ref, w_ref, gn_ref, scw_ref, gs_ref, x1_ref, mix_ref):
        o_n = _dn_out(o_ref[...], z_ref[...], gn_ref[...])[0]
        halo = jnp.where(pl.program_id(0) > 0, halo_ref[...], 0.0)
        yh = _sc_fwd(sc_ref[...], halo, scw_ref[...], tb)[3]
        mix = jnp.concatenate([o_n, yh * gs_ref[...]], axis=1).astype(BF16)
        x1_ref[...] = x_ref[...] + jnp.dot(mix, _whole(w_ref), preferred_element_type=F32)
        mix_ref[...] = mix

    tok = lambda w: pl.BlockSpec((tb, w), lambda i: (i, 0))
    full = lambda a: pl.BlockSpec(a.shape, lambda i: (0, 0))
    return pl.pallas_call(
        body, name="mix_out", grid=(T // tb,),
        in_specs=[tok(DN_WIDTH), tok(DN_WIDTH), tok(3 * SC_WIDTH), pl.BlockSpec((8, 3 * SC_WIDTH), _before_halo(tb)),
                  tok(D_MODEL), _shard_rows(land_a, A_OUT_AT, OUT_SHARD), full(gn), full(scw), full(gs)],
        out_specs=[tok(D_MODEL), tok(D_MODEL)],
        out_shape=[jax.ShapeDtypeStruct((T, D_MODEL), F32), jax.ShapeDtypeStruct((T, D_MODEL), BF16)],
        compiler_params=_params(("parallel",)),
    )(o, z, sc_in, sc_in, x, land_a, gn, scw, gs)


def _ffn(x1, g2, land_b):
    T = x1.shape[0]
    tb = 256

    def body(x_ref, g_ref, wgt_ref, wut_ref, wd_ref, x2_ref, a_ref, b_ref, h_ref):
        xv = x_ref[...]
        r = lax.rsqrt(jnp.mean(xv * xv, axis=-1, keepdims=True) + EPS)
        h = (xv * r * g_ref[...]).astype(BF16)
        a = lax.dot_general(h, _whole(wgt_ref), NT, preferred_element_type=F32)
        b = lax.dot_general(h, _whole(wut_ref), NT, preferred_element_type=F32)
        act = (a * _sigmoid(a) * b).astype(BF16)
        x2_ref[...] = xv + jnp.dot(act, _whole(wd_ref), preferred_element_type=F32)
        a_ref[...] = a.astype(BF16)
        b_ref[...] = b.astype(BF16)
        h_ref[...] = h

    tok = lambda w: pl.BlockSpec((tb, w), lambda i: (i, 0))
    return pl.pallas_call(
        body, name="ffn", grid=(T // tb,),
        in_specs=[tok(D_MODEL), pl.BlockSpec(g2.shape, lambda i: (0, 0)), _shard_rows(land_b, 0, FF_SHARD),
                  _shard_rows(land_b, FF_SHARD, FF_SHARD), _shard_rows(land_b, 2 * FF_SHARD, FF_SHARD)],
        out_specs=[tok(D_MODEL), tok(D_FF), tok(D_FF), tok(D_MODEL)],
        out_shape=[jax.ShapeDtypeStruct((T, D_MODEL), F32), jax.ShapeDtypeStruct((T, D_FF), BF16),
                   jax.ShapeDtypeStruct((T, D_FF), BF16), jax.ShapeDtypeStruct((T, D_MODEL), BF16)],
        compiler_params=_params(("parallel",)),
    )(x1, g2, land_b, land_b, land_b)


def _loss_head(x, gf, target):
    T = x.shape[0]
    tb = 512

    def body(x_ref, g_ref, t_ref, dx_ref, dxb_ref, loss_ref, dg_ref):
        @pl.when(pl.program_id(0) == 0)
        def _():
            loss_ref[...] = jnp.zeros_like(loss_ref)
            dg_ref[...] = jnp.zeros_like(dg_ref)

        xv = x_ref[...]
        r = lax.rsqrt(jnp.mean(xv * xv, axis=-1, keepdims=True) + EPS)
        xh = xv * r
        err = xh * g_ref[...] - t_ref[...]
        per_tok = jnp.mean(err * err, axis=-1, keepdims=True)
        loss_ref[...] += 0.5 * jnp.sum(per_tok, axis=0, keepdims=True)
        dy = err * (1.0 / D_MODEL)
        _row_acc(dg_ref, dy * xh)
        dx = _rms_bwd(dy, xh, r, g_ref[...])
        dx_ref[...] = dx
        dxb_ref[...] = dx.astype(BF16)

    tok = pl.BlockSpec((tb, D_MODEL), lambda i: (i, 0))
    return pl.pallas_call(
        body, name="loss_head", grid=(T // tb,),
        in_specs=[tok, pl.BlockSpec(gf.shape, lambda i: (0, 0)), tok],
        out_specs=[tok, tok, pl.BlockSpec((8, LANES), lambda i: (0, 0)), pl.BlockSpec((8, D_MODEL), lambda i: (0, 0))],
        out_shape=[jax.ShapeDtypeStruct((T, D_MODEL), F32), jax.ShapeDtypeStruct((T, D_MODEL), BF16),
                   jax.ShapeDtypeStruct((8, LANES), F32), jax.ShapeDtypeStruct((8, D_MODEL), F32)],
        compiler_params=_params(("arbitrary",)),
    )(x, gf, target)


def _ffn_bwd(dx2, x1, a, b, g2, land_b):
    T = x1.shape[0]
    tb = 256

    def body(dx2_ref, x_ref, a_ref, b_ref, g_ref, wgt_ref, wut_ref, wd_ref,
             dx1_ref, dx1b_ref, da_ref, db_ref, act_ref, dg_ref):
        @pl.when(pl.program_id(0) == 0)
        def _():
            dg_ref[...] = jnp.zeros_like(dg_ref)

        dx2v = dx2_ref[...]
        av = a_ref[...].astype(F32)
        bv = b_ref[...].astype(F32)
        dact = _mm(dx2v, _whole(wd_ref), NT)
        sa = _sigmoid(av)
        silu = av * sa
        da = (dact * bv * (sa * (1.0 + av * (1.0 - sa)))).astype(BF16)
        db = (dact * silu).astype(BF16)
        dh = _mm(da, _whole(wgt_ref)) + _mm(db, _whole(wut_ref))
        xv = x_ref[...]
        r = lax.rsqrt(jnp.mean(xv * xv, axis=-1, keepdims=True) + EPS)
        xh = xv * r
        _row_acc(dg_ref, dh * xh)
        dx1 = dx2v + _rms_bwd(dh, xh, r, g_ref[...])
        dx1_ref[...] = dx1
        dx1b_ref[...] = dx1.astype(BF16)
        da_ref[...] = da
        db_ref[...] = db
        act_ref[...] = (silu * bv).astype(BF16)

    tok = lambda w: pl.BlockSpec((tb, w), lambda i: (i, 0))
    return pl.pallas_call(
        body, name="ffn_bwd", grid=(T // tb,),
        in_specs=[tok(D_MODEL), tok(D_MODEL), tok(D_FF), tok(D_FF), pl.BlockSpec(g2.shape, lambda i: (0, 0)),
                  _shard_rows(land_b, 0, FF_SHARD), _shard_rows(land_b, FF_SHARD, FF_SHARD),
                  _shard_rows(land_b, 2 * FF_SHARD, FF_SHARD)],
        out_specs=[tok(D_MODEL), tok(D_MODEL), tok(D_FF), tok(D_FF), tok(D_FF), pl.BlockSpec((8, D_MODEL), lambda i: (0, 0))],
        out_shape=[jax.ShapeDtypeStruct((T, D_MODEL), F32), jax.ShapeDtypeStruct((T, D_MODEL), BF16)]
        + [jax.ShapeDtypeStruct((T, D_FF), BF16)] * 3 + [jax.ShapeDtypeStruct((8, D_MODEL), F32)],
        compiler_params=_params(("arbitrary",)),
    )(dx2, x1, a, b, g2, land_b, land_b, land_b)


def _wgrad_share(a, b, parts, first, name):
    T = b.shape[0]
    rows = a.shape[1] // N_CHIPS
    assert first % rows == 0 and b.shape[1] == parts.shape[2]
    bk = min(T, 1024)
    n_k = T // bk
    group = 2
    assert (group * rows) % LANES == 0

    def body(a_ref, b_ref, parts_ref, o_ref, acc_ref):
        kk = pl.program_id(1)

        @pl.when(kk == 0)
        def _():
            acc_ref[...] = jnp.zeros_like(acc_ref)

        acc_ref[...] += lax.dot_general(a_ref[...], b_ref[...], TN, preferred_element_type=F32)

        @pl.when(kk == n_k - 1)
        def _():
            for s in range(group):
                o_ref[s] = acc_ref[rows * s:rows * (s + 1), :].astype(BF16)

    return pl.pallas_call(
        body, name=name, grid=(N_CHIPS // group, n_k),
        in_specs=[pl.BlockSpec((bk, group * rows), lambda i, kk: (kk, i)),
                  pl.BlockSpec((bk, b.shape[1]), lambda i, kk: (kk, 0)), _ANY],
        out_specs=pl.BlockSpec((group, rows, b.shape[1]), lambda i, kk: (i, first // rows, 0)),
        out_shape=jax.ShapeDtypeStruct(parts.shape, BF16),
        scratch_shapes=[pltpu.VMEM((group * rows, b.shape[1]), F32)],
        input_output_aliases={2: 0},
        compiler_params=_params(("parallel", "arbitrary")),
    )(a, b, parts)


def _mix_out_bwd(dx1, o, z, sc_in, land_a, gn, scw, gs, dp):
    T = dx1.shape[0]
    tb = 256

    def body(dx_ref, o_ref, z_ref, sc_ref, halo_ref, w_ref, gn_ref, scw_ref, gs_ref, dp_ref,
             do_ref, dz_ref, dgb_ref, dcv_ref, dgn_ref, dgs_ref, dscw_ref):
        @pl.when(pl.program_id(0) == 0)
        def _():
            dgn_ref[...] = jnp.zeros_like(dgn_ref)
            dgs_ref[...] = jnp.zeros_like(dgs_ref)
            dscw_ref[...] = jnp.zeros_like(dscw_ref)

        dmix = _mm(dx_ref[...], _whole(w_ref), NT)
        don = dmix[:, :DN_WIDTH]
        dosc = dmix[:, DN_WIDTH:]
        zv = z_ref[...]
        _, oh, rs, sz, gn4 = _dn_out(o_ref[...], zv, gn_ref[...])
        silu_z = zv * sz
        dgn_full = don * oh * silu_z
        dgn_ref[0:1, :] += jnp.sum(sum(dgn_full[:, HEAD_DIM * hh:HEAD_DIM * (hh + 1)] for hh in range(HEADS)),
                                   axis=0, keepdims=True)
        dz_ref[...] = (don * oh * gn4 * (sz * (1.0 + zv * (1.0 - sz)))).astype(BF16)
        t = don * gn4 * silu_z
        for hh in range(HEADS):
            sl = slice(HEAD_DIM * hh, HEAD_DIM * (hh + 1))
            th, ohh = t[:, sl], oh[:, sl]
            do_ref[:, sl] = rs[hh] * (th - ohh * jnp.mean(th * ohh, axis=-1, keepdims=True))
        halo = jnp.where(pl.program_id(0) > 0, halo_ref[...], 0.0)
        u, cv, gate_b, yh, rys = _sc_fwd(sc_ref[...], halo, scw_ref[...], tb)
        _row_acc(dgs_ref, dosc * yh)
        ty = dosc * gs_ref[...]
        gw = SC_WIDTH // SC_GROUPS
        dys = []
        for gi in range(SC_GROUPS):
            sl = slice(gw * gi, gw * (gi + 1))
            tg, yg = ty[:, sl], yh[:, sl]
            dys.append(rys[gi] * (tg - yg * jnp.mean(tg * yg, axis=-1, keepdims=True)))
        dy = jnp.concatenate(dys, axis=1)
        dgb_ref[...] = dy * cv
        dcv = dy * gate_b
        dcv_ref[...] = dcv
        for j in range(3):
            dscw_ref[j:j + 1, :] += jnp.sum(dcv * _rows_from(u, 6 + j, tb), axis=0, keepdims=True)

    tok = lambda w: pl.BlockSpec((tb, w), lambda i: (i, 0))
    full = lambda t: pl.BlockSpec(t.shape, lambda i: (0, 0))
    acc = lambda w: pl.BlockSpec((8, w), lambda i: (0, 0))
    return pl.pallas_call(
        body, name="mix_out_bwd", grid=(T // tb,),
        in_specs=[tok(D_MODEL), tok(DN_WIDTH), tok(DN_WIDTH), tok(3 * SC_WIDTH),
                  pl.BlockSpec((8, 3 * SC_WIDTH), _before_halo(tb)), _shard_rows(land_a, A_OUT_AT, OUT_SHARD),
                  full(gn), full(scw), full(gs), _ANY],
        out_specs=[tok(DN_WIDTH), _dp_block(tb, P_Z, DN_WIDTH), tok(SC_WIDTH), tok(SC_WIDTH),
                   acc(HEAD_DIM), acc(SC_WIDTH), acc(SC_WIDTH)],
        out_shape=[jax.ShapeDtypeStruct((T, DN_WIDTH), F32), jax.ShapeDtypeStruct(dp.shape, BF16),
                   jax.ShapeDtypeStruct((T, SC_WIDTH), F32), jax.ShapeDtypeStruct((T, SC_WIDTH), F32),
                   jax.ShapeDtypeStruct((8, HEAD_DIM), F32), jax.ShapeDtypeStruct((8, SC_WIDTH), F32),
                   jax.ShapeDtypeStruct((8, SC_WIDTH), F32)],
        input_output_aliases={9: 1},
        compiler_params=_params(("arbitrary",)),
    )(dx1, o, z, sc_in, sc_in, land_a, gn, scw, gs, dp)


def _sc_conv_bwd(dcv, dgb, sc_in, scw, dp):
    T = dcv.shape[0]
    tb = 512

    def body(dcv_ref, halo_ref, dgb_ref, sc_ref, w_ref, dp_ref, out_ref):
        last = pl.program_id(0) == pl.num_programs(0) - 1
        halo = jnp.where(last, 0.0, halo_ref[...])
        xc = jnp.concatenate([dcv_ref[...], halo], axis=0)
        w = w_ref[...]
        du = w[2:3, :] * xc[0:tb, :] + w[1:2, :] * _rows_from(xc, 1, tb) + w[0:1, :] * _rows_from(xc, 2, tb)
        sc = sc_ref[...]
        out_ref[:, :SC_WIDTH] = dgb_ref[...].astype(BF16)
        out_ref[:, SC_WIDTH:2 * SC_WIDTH] = (du * sc[:, 2 * SC_WIDTH:]).astype(BF16)
        out_ref[:, 2 * SC_WIDTH:] = (du * sc[:, SC_WIDTH:2 * SC_WIDTH]).astype(BF16)

    tok = lambda w: pl.BlockSpec((tb, w), lambda i: (i, 0))
    return pl.pallas_call(
        body, name="sc_conv_bwd", grid=(T // tb,),
        in_specs=[tok(SC_WIDTH), pl.BlockSpec((8, SC_WIDTH), _after_halo(tb, T)), tok(SC_WIDTH), tok(3 * SC_WIDTH),
                  pl.BlockSpec(scw.shape, lambda i: (0, 0)), _ANY],
        out_specs=_dp_block(tb, P_SC, 3 * SC_WIDTH),
        out_shape=jax.ShapeDtypeStruct(dp.shape, BF16),
        input_output_aliases={5: 0},
        compiler_params=_params(("parallel",)),
    )(dcv, dcv, dgb, sc_in, scw, dp)


def _delta_bwd(q, k, v, bg, states, do):
    T = q.shape[0]
    tb = 512
    n_chunk = tb // CHUNK
    nb = T // tb

    def body(q_ref, k_ref, v_ref, bg_ref, st_ref, do_ref, dq_ref, dk_ref, dv_ref, dbg_ref, ds_ref):
        @pl.when(pl.program_id(0) == 0)
        def _():
            ds_ref[...] = jnp.zeros_like(ds_ref)

        masks = _chunk_masks()
        causal, strict = masks
        lane = lax.broadcasted_iota(jnp.int32, (CHUNK, LANES), 1)
        last_row = lax.broadcasted_iota(jnp.int32, (CHUNK, 1), 0) == CHUNK - 1
        cat = jnp.concatenate
        heads = range(HEADS)

        def open_chunk(ci, loc):
            rows = pl.ds(pl.multiple_of(ci * CHUNK, CHUNK), CHUNK)
            dov = do_ref[rows, :]
            return dict(rows=rows, loc=loc, do=[dov[:, HEAD_DIM * h:HEAD_DIM * (h + 1)] for h in heads],
                        state=[st_ref[ci, h] for h in heads])

        def a_free(c):
            loc, do, state = c["loc"], c["do"], c["state"]
            w_s = [_mm(p["w"], s) for p, s in zip(loc, state)]
            c["dq_dec"] = [_mm(d, s, NT) for d, s in zip(do, state)]
            c["qk_do"] = [_mm(p["qk"], d, TN) for p, d in zip(loc, do)]
            c["qd_do"] = [_mm(p["q_dec"], d, TN) for p, d in zip(loc, do)]
            c["v_new"] = [p["u"] - t for p, t in zip(loc, w_s)]
            c["dqk"] = [jnp.where(causal, _mm(d, vn, NT), 0.0) for d, vn in zip(do, c["v_new"])]

        def a_state(c, ds_next):
            c["ds_next"] = ds_next
            kd_ds = [_mm(p["k_dec"], d) for p, d in zip(c["loc"], ds_next)]
            c["dk_dec"] = [_mm(vn, d, NT) for vn, d in zip(c["v_new"], ds_next)]
            c["dv_new"] = [a + b for a, b in zip(c["qk_do"], kd_ds)]

        def b_state(c):
            loc = c["loc"]
            w_dv = [_mm(p["w"], dvn, TN) for p, dvn in zip(loc, c["dv_new"])]
            c["dw"] = [-_mm(dvn, s, NT) for dvn, s in zip(c["dv_new"], c["state"])]
            return [loc[h]["gl"] * c["ds_next"][h] + c["qd_do"][h] - w_dv[h] for h in heads]

        def c_solve(c):
            loc, dv_new, dw = c["loc"], c["dv_new"], c["dw"]
            c["dtm"] = [_mm(cat([dvn, d], axis=1), cat([p["vb"], p["kbg"]], axis=1), NT) for dvn, d, p in zip(dv_new, dw, loc)]
            x_t = [_mm(p["xm"], cat([dvn, d], axis=1), TN) for p, dvn, d in zip(loc, dv_new, dw)]
            c["dvb"] = [dvn + t[:, :HEAD_DIM] for dvn, t in zip(dv_new, x_t)]
            c["dkbg"] = [d + t[:, HEAD_DIM:] for d, t in zip(dw, x_t)]

        def d_solve(c):
            c["y"] = [t + _mm(p["xm"], t, TN) for p, t in zip(c["loc"], c["dtm"])]

        def e_solve(c):
            c["dlow"] = [jnp.where(strict, -(t + _mm(t, p["xm"], NT)), 0.0) for p, t in zip(c["loc"], c["y"])]

        def f_close(c):
            loc, rows = c["loc"], c["rows"]
            dmm = [d * p["decay"] for d, p in zip(c["dlow"], loc)]
            dnn = [d * p["decay"] for d, p in zip(c["dqk"], loc)]
            by_k = [_mm(cat([a, b], axis=0), p["k"]) for a, b, p in zip(dmm, dnn, loc)]
            dk_mm = [_mm(cat([a, b], axis=0), cat([p["kb"], p["q"]], axis=0), TN) for a, b, p in zip(dmm, dnn, loc)]
            dq_out, dk_out, dv_out = [], [], []
            dbeta_all = jnp.zeros((CHUNK, LANES), F32)
            dgc_all = jnp.zeros((CHUNK, LANES), F32)
            for h in heads:
                p = loc[h]
                dkb = by_k[h][:CHUNK] + c["dkbg"][h] * p["eg"]
                dq_out.append(by_k[h][CHUNK:] + c["dq_dec"][h] * p["eg"])
                dk_out.append(dk_mm[h] + c["dk_dec"][h] * p["ek"] + dkb * p["beta"])
                dv_out.append(c["dvb"][h] * p["beta"])
                dbeta = jnp.sum(dkb * p["k"] + c["dvb"][h] * p["v"], axis=1, keepdims=True)
                e = c["dlow"][h] * p["low"] + c["dqk"][h] * p["qk"]
                kd = jnp.sum(c["dk_dec"][h] * p["k_dec"], axis=1, keepdims=True)
                dgc = (jnp.sum(e, axis=1, keepdims=True) - jnp.sum(e.T, axis=1, keepdims=True)
                       + jnp.sum(c["dq_dec"][h] * p["q_dec"], axis=1, keepdims=True) - kd
                       + jnp.sum(c["dkbg"][h] * p["kbg"], axis=1, keepdims=True))
                dgl = jnp.sum(jnp.sum(c["ds_next"][h] * c["state"][h], axis=1, keepdims=True), axis=0, keepdims=True)
                d_last = jnp.sum(kd, axis=0, keepdims=True) + dgl * p["gl"]
                dgc = dgc + jnp.where(last_row, d_last, 0.0)
                dbeta_all = jnp.where(lane == h, dbeta, dbeta_all)
                dgc_all = jnp.where(lane == h + HEADS, dgc, dgc_all)
            dq_ref[rows, :] = cat(dq_out, axis=1)
            dk_ref[rows, :] = cat(dk_out, axis=1)
            dv_ref[rows, :] = cat(dv_out, axis=1)
            dbg_ref[rows, :] = dbeta_all + dgc_all

        def pair(pj, carry):
            hi = n_chunk - 1 - 2 * pj
            lo = hi - 1
            rows = [pl.ds(pl.multiple_of(ci * CHUNK, CHUNK), CHUNK) for ci in (hi, lo)]
            loc = _units_local(_chunk_units(q_ref, k_ref, v_ref, bg_ref, rows[0])
                               + _chunk_units(q_ref, k_ref, v_ref, bg_ref, rows[1]), masks)
            c_hi, c_lo = open_chunk(hi, loc[:HEADS]), open_chunk(lo, loc[HEADS:])
            a_free(c_hi)
            a_free(c_lo)
            a_state(c_hi, [ds_ref[h] for h in heads])
            ds_mid = b_state(c_hi)
            a_state(c_lo, ds_mid)
            c_solve(c_hi)
            ds_out = b_state(c_lo)
            for h in heads:
                ds_ref[h] = ds_out[h]
            d_solve(c_hi)
            c_solve(c_lo)
            e_solve(c_hi)
            d_solve(c_lo)
            f_close(c_hi)
            e_solve(c_lo)
            f_close(c_lo)
            return carry

        lax.fori_loop(0, n_chunk // 2, pair, 0)

    tok = lambda w: pl.BlockSpec((tb, w), lambda i: (nb - 1 - i, 0))
    return pl.pallas_call(
        body, name="delta_bwd", grid=(nb,),
        in_specs=[tok(DN_WIDTH), tok(DN_WIDTH), tok(DN_WIDTH), tok(LANES),
                  pl.BlockSpec((n_chunk, HEADS, HEAD_DIM, HEAD_DIM), lambda i: (nb - 1 - i, 0, 0, 0)), tok(DN_WIDTH)],
        out_specs=[tok(DN_WIDTH), tok(DN_WIDTH), tok(DN_WIDTH), tok(LANES)],
        out_shape=[jax.ShapeDtypeStruct((T, DN_WIDTH), F32)] * 3 + [jax.ShapeDtypeStruct((T, LANES), F32)],
        scratch_shapes=[pltpu.VMEM((HEADS, HEAD_DIM, HEAD_DIM), F32)],
        compiler_params=_params(("arbitrary",)),
    )(q, k, v, bg, states, do)


def _dn_prep_bwd(dq, dk, dv, dbg, qkv, cw, bd, al_row, dt_row, dp):
    T = qkv.shape[0]
    tb = 256

    def body(dq_ref, dk_ref, dv_ref, dbg_ref, pre_ref, halo_ref, cw_ref, bd_ref, al_ref, dt_ref, dp_ref,
             dc_ref, dbd_ref, dcw_ref, dal_ref, ddt_ref):
        @pl.when(pl.program_id(0) == 0)
        def _():
            dcw_ref[...] = jnp.zeros_like(dcw_ref)
            dal_ref[...] = jnp.zeros_like(dal_ref)
            ddt_ref[...] = jnp.zeros_like(ddt_ref)

        halo = jnp.where(pl.program_id(0) > 0, halo_ref[...], 0.0)
        xc, c, sg, a = _dn_act(pre_ref[...], halo, cw_ref[...], tb)
        dsilu = sg * (1.0 + c * (1.0 - sg))
        for hh in range(HEADS):
            sl = slice(HEAD_DIM * hh, HEAD_DIM * (hh + 1))
            for base, g_ref, scale in ((0, dq_ref, Q_SCALE), (DN_WIDTH, dk_ref, 1.0)):
                sa = slice(base + HEAD_DIM * hh, base + HEAD_DIM * (hh + 1))
                raw = a[:, sa]
                r = lax.rsqrt(jnp.sum(raw * raw, axis=-1, keepdims=True) + EPS)
                nrm = raw * r
                gn_ = g_ref[:, sl] * scale
                dc_ref[:, sa] = r * (gn_ - nrm * jnp.sum(gn_ * nrm, axis=-1, keepdims=True)) * dsilu[:, sa]
        dc_ref[:, 2 * DN_WIDTH:] = dv_ref[...] * dsilu[:, 2 * DN_WIDTH:]
        dc = dc_ref[...]
        for j in range(4):
            dcw_ref[j:j + 1, :] += jnp.sum(dc * _rows_from(xc, 5 + j, tb), axis=0, keepdims=True)
        bdv = bd_ref[...]
        lane = lax.broadcasted_iota(jnp.int32, bdv.shape, 1)
        is_b = lane < HEADS
        dbg_in = dbg_ref[...]
        dbgv = jnp.where(is_b, dbg_in, _mm32(_chunk_cumsum_matrix(tb), dbg_in, TN))
        is_g = jnp.logical_and(lane >= HEADS, lane < 2 * HEADS)
        beta = _sigmoid(bdv)
        neg_a = -jnp.exp(al_ref[...])
        pre_sp = bdv + dt_ref[...]
        g = neg_a * _softplus(pre_sp)
        da_in = dbgv * neg_a * _sigmoid(pre_sp)
        dbd_ref[...] = jnp.where(is_b, dbgv * beta * (1.0 - beta), jnp.where(is_g, da_in, 0.0)).astype(BF16)
        _row_acc(dal_ref, jnp.where(is_g, dbgv * g, 0.0))
        _row_acc(ddt_ref, jnp.where(is_g, da_in, 0.0))

    tok = lambda w: pl.BlockSpec((tb, w), lambda i: (i, 0))
    full = lambda t: pl.BlockSpec(t.shape, lambda i: (0, 0))
    acc = lambda w: pl.BlockSpec((8, w), lambda i: (0, 0))
    return pl.pallas_call(
        body, name="dn_prep_bwd", grid=(T // tb,),
        in_specs=[tok(DN_WIDTH), tok(DN_WIDTH), tok(DN_WIDTH), tok(LANES),
                  tok(QKV), pl.BlockSpec((8, QKV), _before_halo(tb)), full(cw), tok(LANES), full(al_row), full(dt_row), _ANY],
        out_specs=[tok(QKV), _dp_block(tb, P_BD, LANES), acc(QKV), acc(LANES), acc(LANES)],
        out_shape=[jax.ShapeDtypeStruct((T, QKV), F32), jax.ShapeDtypeStruct(dp.shape, BF16),
                   jax.ShapeDtypeStruct((8, QKV), F32), jax.ShapeDtypeStruct((8, LANES), F32),
                   jax.ShapeDtypeStruct((8, LANES), F32)],
        input_output_aliases={10: 1},
        compiler_params=_params(("arbitrary",)),
    )(dq, dk, dv, dbg, qkv, qkv, cw, bd, al_row, dt_row, dp)


def _dn_conv_bwd(dc, cw, dp):
    T = dc.shape[0]
    tb = 512

    def body(dc_ref, halo_ref, w_ref, dp_ref, out_ref):
        last = pl.program_id(0) == pl.num_programs(0) - 1
        halo = jnp.where(last, 0.0, halo_ref[...])
        xc = jnp.concatenate([dc_ref[...], halo], axis=0)
        w = w_ref[...]
        acc = w[3:4, :] * xc[0:tb, :]
        for j in range(3):
            acc = acc + w[j:j + 1, :] * _rows_from(xc, 3 - j, tb)
        out_ref[...] = acc.astype(BF16)

    tok = pl.BlockSpec((tb, QKV), lambda i: (i, 0))
    return pl.pallas_call(
        body, name="dn_conv_bwd", grid=(T // tb,),
        in_specs=[tok, pl.BlockSpec((8, QKV), _after_halo(tb, T)), pl.BlockSpec(cw.shape, lambda i: (0, 0)), _ANY],
        out_specs=_dp_block(tb, 0, QKV),
        out_shape=jax.ShapeDtypeStruct(dp.shape, BF16),
        input_output_aliases={3: 0},
        compiler_params=_params(("parallel",)),
    )(dc, dc, cw, dp)


def _dp_of_chip(dp, s):
    lo, hi = IN_SHARD * s, IN_SHARD * (s + 1)
    pieces = []
    for w_at, w_end, p_at in ((0, W_Z, 0), (W_Z, W_BD, P_Z), (W_BD, W_SC, P_BD), (W_SC, W_IN_COLS, P_SC)):
        a, b = max(lo, w_at), min(hi, w_end)
        if a < b:
            pieces.append(dp[:, p_at + a - w_at:p_at + b - w_at])
    pieces.append(jnp.zeros((dp.shape[0], D_MODEL - IN_SHARD), dp.dtype))
    return jnp.concatenate(pieces, axis=1)


def _in_proj_bwd(dp, dx1, x, g1, land_a):
    T = x.shape[0]
    tb = 256

    def body(dp_ref, dx1_ref, x_ref, g_ref, w_ref, dx_ref, dxb_ref, dps_ref, dg_ref):
        @pl.when(pl.program_id(0) == 0)
        def _():
            dg_ref[...] = jnp.zeros_like(dg_ref)

        dpv = dp_ref[...]
        dh = jnp.zeros((tb, D_MODEL), F32)
        for s in range(N_CHIPS):
            dps = _dp_of_chip(dpv, s)
            dps_ref[:, D_MODEL * s:D_MODEL * (s + 1)] = dps
            dh = dh + lax.dot_general(dps, w_ref[s], NT, preferred_element_type=F32)
        xv = x_ref[...]
        r = lax.rsqrt(jnp.mean(xv * xv, axis=-1, keepdims=True) + EPS)
        xh = xv * r
        _row_acc(dg_ref, dh * xh)
        dx = dx1_ref[...] + _rms_bwd(dh, xh, r, g_ref[...])
        dx_ref[...] = dx
        dxb_ref[...] = dx.astype(BF16)

    tok = lambda w: pl.BlockSpec((tb, w), lambda i: (i, 0))
    return pl.pallas_call(
        body, name="in_proj_bwd", grid=(T // tb,),
        in_specs=[tok(P_COLS), tok(D_MODEL), tok(D_MODEL), pl.BlockSpec(g1.shape, lambda i: (0, 0)),
                  _shard_rows(land_a, 0, D_MODEL)],
        out_specs=[tok(D_MODEL), tok(D_MODEL), tok(N_CHIPS * D_MODEL), pl.BlockSpec((8, D_MODEL), lambda i: (0, 0))],
        out_shape=[jax.ShapeDtypeStruct((T, D_MODEL), F32), jax.ShapeDtypeStruct((T, D_MODEL), BF16),
                   jax.ShapeDtypeStruct((T, N_CHIPS * D_MODEL), BF16), jax.ShapeDtypeStruct((8, D_MODEL), F32)],
        compiler_params=_params(("arbitrary",)),
    )(dp, dx1, x, g1, land_a)


def _wgrad_in_share(h, dps, parts, name):
    T = h.shape[0]
    bk = min(T, 1024)
    n_k = T // bk

    def body(a_ref, b_ref, parts_ref, o_ref, acc_ref):
        kk = pl.program_id(1)

        @pl.when(kk == 0)
        def _():
            acc_ref[...] = jnp.zeros_like(acc_ref)

        acc_ref[...] += lax.dot_general(a_ref[...], b_ref[...], TN, preferred_element_type=F32)

        @pl.when(kk == n_k - 1)
        def _():
            o_ref[0] = acc_ref[...].astype(BF16)

    return pl.pallas_call(
        body, name=name, grid=(N_CHIPS, n_k),
        in_specs=[pl.BlockSpec((bk, D_MODEL), lambda j, kk: (kk, 0)), pl.BlockSpec((bk, D_MODEL), lambda j, kk: (kk, j)), _ANY],
        out_specs=pl.BlockSpec((1, D_MODEL, D_MODEL), lambda j, kk: (j, 0, 0)),
        out_shape=jax.ShapeDtypeStruct(parts.shape, BF16),
        scratch_shapes=[pltpu.VMEM((D_MODEL, D_MODEL), F32)],
        input_output_aliases={2: 0},
        compiler_params=_params(("parallel", "arbitrary")),
    )(h, dps, parts)


def _pad_rows(a, rows=8):
    return jnp.pad(a, ((0, rows - a.shape[0]), (0, 0)))


def _gate_rows(a_log, dt_bias):
    put = lambda t: jnp.pad(t.reshape(1, HEADS), ((0, 0), (HEADS, LANES - 2 * HEADS)))
    return put(a_log), put(dt_bias)


def _mixer_fwd(x, p):
    qkv, z, sc_in, bd, h = _in_proj(x, p["g1"], p["land_a"])
    q, k, v, bg = _dn_prep(qkv, p["cw"], bd, p["al"], p["dt"])
    o, states = _delta_fwd(q, k, v, bg)
    x1, mix = _mix_out(o, z, sc_in, x, p["land_a"], p["gn"], p["scw"], p["gs"])
    return x1, dict(x=x, qkv=qkv, z=z, sc_in=sc_in, bd=bd, h=h, q=q, k=k, v=v, bg=bg, o=o, states=states, mix=mix)


def _ffn_fwd(x1, p, land_b):
    x2, a, b, h2 = _ffn(x1, p["g2"], land_b)
    return x2, dict(x1=x1, a=a, b=b, h2=h2)


def _ffn_back(dx2, dx2_bf16, s, p, land_b):
    dx1, dx1_bf16, da, db, act, dg2 = _ffn_bwd(dx2, s["x1"], s["a"], s["b"], p["g2"], land_b)
    parts = lax.empty((N_CHIPS, B_ROWS, D_MODEL), BF16)
    parts = _wgrad_share(act, dx2_bf16, parts, 2 * FF_SHARD, "wgrad_down")
    parts = _wgrad_share(da, s["h2"], parts, 0, "wgrad_gate")
    parts = _wgrad_share(db, s["h2"], parts, FF_SHARD, "wgrad_up")
    return dx1, dx1_bf16, parts, dg2[0]


def _mixer_bwd(dx1, dx1_bf16, s, p):
    dp = lax.empty((dx1.shape[0], P_COLS), BF16)
    do, dp, dgb, dcv, dgn, dgs, dscw = _mix_out_bwd(dx1, s["o"], s["z"], s["sc_in"], p["land_a"], p["gn"], p["scw"], p["gs"], dp)
    dp = _sc_conv_bwd(dcv, dgb, s["sc_in"], p["scw"], dp)
    dq, dk, dv, dbg = _delta_bwd(s["q"], s["k"], s["v"], s["bg"], s["states"], do)
    dc, dp, dcw, dal, ddt = _dn_prep_bwd(dq, dk, dv, dbg, s["qkv"], p["cw"], s["bd"], p["al"], p["dt"], dp)
    dp = _dn_conv_bwd(dc, p["cw"], dp)
    dx, dx_bf16, dps, dg1 = _in_proj_bwd(dp, dx1, s["x"], p["g1"], p["land_a"])
    parts = lax.empty((N_CHIPS, A_ROWS, D_MODEL), BF16)
    parts = _wgrad_in_share(s["h"], dps, parts, "wgrad_in")
    parts = _wgrad_share(s["mix"], dx1_bf16, parts, A_OUT_AT, "wgrad_out")
    g = dict(g1=dg1[0], gn=dgn[0], gs=dgs[0], scw=dscw[:3], cw=dcw[:4], al=dal[0, HEADS:2 * HEADS], dt=ddt[0, HEADS:2 * HEADS])
    return dx, dx_bf16, parts, g


def _place():
    return lax.axis_index("x"), lax.axis_index("y"), lax.axis_index("c")


def _other_chips(x, y):
    return [(1 - x, y), (x, 1 - y), (1 - x, 1 - y)]


_HBM = pl.BlockSpec(memory_space=pltpu.HBM)


def _chip_exchange(arrs, name, gather):
    n = len(arrs)

    def body(*refs):
        ins, outs = refs[:n], refs[n:2 * n]
        send_sems, recv_sems, local_sems = refs[2 * n:]
        x, y, c = _place()
        me = 2 * x + y
        others = _other_chips(x, y)

        def remote(k, j, landing):
            px, py = others[j]
            src = ins[k] if gather else ins[k].at[2 * px + py]
            return pltpu.make_async_remote_copy(src_ref=src, dst_ref=outs[k].at[landing], send_sem=send_sems.at[k, j],
                                                recv_sem=recv_sems.at[k, j], device_id=(px, py, c), device_id_type=MESH)

        local = [pltpu.make_async_copy(ins[k] if gather else ins[k].at[me], outs[k].at[me], local_sems.at[k])
                 for k in range(n)]
        sends = [remote(k, j, me) for k in range(n) for j in range(3)]
        for cp in local + sends:
            cp.start()
        for k in range(n):
            for j, (px, py) in enumerate(others):
                remote(k, j, 2 * px + py).wait_recv()
        for cp in sends:
            cp.wait_send()
        for cp in local:
            cp.wait()

    shapes = [jax.ShapeDtypeStruct(((N_CHIPS,) + a.shape) if gather else a.shape, a.dtype) for a in arrs]
    return pl.pallas_call(
        body, name=name, in_specs=[_HBM] * n, out_specs=[_HBM] * n, out_shape=shapes,
        scratch_shapes=[pltpu.SemaphoreType.DMA((n, 3)), pltpu.SemaphoreType.DMA((n, 3)), pltpu.SemaphoreType.DMA((n,))],
    )(*arrs)


_SEM = pl.BlockSpec(memory_space=pltpu.SEMAPHORE)
_ANY = pl.BlockSpec(memory_space=pl.ANY)
_EFFECT = pltpu.SideEffectType.DATAFLOW_SIDE_EFFECTING


def _split_copies(src_ref, land_ref, send_sems, recv_sems, gather, sending):
    x, y, c = _place()
    me = 2 * x + y
    copies = []
    for j, (px, py) in enumerate(_other_chips(x, y)):
        peer = 2 * px + py
        copies.append(pltpu.make_async_remote_copy(
            src_ref=src_ref if gather else src_ref.at[peer], dst_ref=land_ref.at[me if sending else peer],
            send_sem=send_sems.at[j], recv_sem=recv_sems.at[j], device_id=(px, py, c), device_id_type=MESH))
    return copies


def _own_slot(share):
    chip = 2 * lax.axis_index("x") + lax.axis_index("y")
    return lax.dynamic_update_slice(lax.empty((N_CHIPS,) + share.shape, share.dtype), share[None], (chip, 0, 0))


def _exchange_start(src, land, after, name, gather):
    def body(src_ref, land_ref, after_ref, send_sems, recv_sems, src_thru, land_thru, token):
        for cp in _split_copies(src_ref, land_ref, send_sems, recv_sems, gather, sending=True):
            cp.start()
        token[...] = jnp.zeros_like(token)

    hbm = lambda t: pltpu.with_memory_space_constraint(t, pltpu.HBM)
    return pl.pallas_call(
        body, name=name,
        out_shape=(pltpu.SemaphoreType.DMA((3,)), pltpu.SemaphoreType.DMA((3,)), pltpu.HBM(src.shape, src.dtype),
                   pltpu.HBM(land.shape, land.dtype), jax.ShapeDtypeStruct((8, LANES), F32)),
        in_specs=(_HBM, _HBM, _ANY), out_specs=(_SEM, _SEM, _HBM, _HBM, pl.BlockSpec(memory_space=pltpu.VMEM)),
        input_output_aliases={0: 2, 1: 3},
        compiler_params=pltpu.CompilerParams(has_side_effects=_EFFECT),
    )(hbm(src), hbm(land), after)


def _exchange_wait(started, after, name, gather):
    send_sems, recv_sems, src_thru, land_thru, _ = started

    def body(src_ref, land_ref, send_sems, recv_sems, after_ref, src_dead, got_ref):
        for cp in _split_copies(src_ref, land_ref, send_sems, recv_sems, gather, sending=False):
            cp.wait_send()
            cp.wait_recv()

    return pl.pallas_call(
        body, name=name,
        out_shape=(pltpu.HBM(src_thru.shape, src_thru.dtype), pltpu.HBM(land_thru.shape, land_thru.dtype)),
        in_specs=(_HBM, _HBM, _SEM, _SEM, _ANY), out_specs=(_HBM, _HBM), input_output_aliases={0: 0, 1: 1},
        compiler_params=pltpu.CompilerParams(has_side_effects=_EFFECT),
    )(src_thru, land_thru, send_sems, recv_sems, after)[1]


def _swap_sibling(arrs, name):
    n = len(arrs)

    def body(*refs):
        ins, outs = refs[:n], refs[n:2 * n]
        send_sems, recv_sems = refs[2 * n:]
        x, y, c = _place()
        copies = [pltpu.make_async_remote_copy(src_ref=ins[k], dst_ref=outs[k], send_sem=send_sems.at[k],
                                               recv_sem=recv_sems.at[k], device_id=(x, y, 1 - c), device_id_type=MESH)
                  for k in range(n)]
        for cp in copies:
            cp.start()
        for cp in copies:
            cp.wait()

    return pl.pallas_call(
        body, name=name, in_specs=[_HBM] * n, out_specs=[_HBM] * n,
        out_shape=[jax.ShapeDtypeStruct(a.shape, a.dtype) for a in arrs],
        scratch_shapes=[pltpu.SemaphoreType.DMA((n,)), pltpu.SemaphoreType.DMA((n,))],
    )(*arrs)


def _all_reduce_small(v):
    rows = v.shape[0]
    flips = [(a, b, cc) for a in (0, 1) for b in (0, 1) for cc in (0, 1)][1:]

    def body(v_ref, out_ref, buf_ref, send_sems, recv_sems):
        x, y, c = _place()
        me = 4 * x + 2 * y + c
        peers = [((1 - x) if a else x, (1 - y) if b else y, (1 - c) if cc else c) for a, b, cc in flips]

        def copy(j, landing):
            return pltpu.make_async_remote_copy(src_ref=v_ref, dst_ref=buf_ref.at[landing], send_sem=send_sems.at[j],
                                                recv_sem=recv_sems.at[j], device_id=peers[j], device_id_type=MESH)

        sends = [copy(j, me) for j in range(N_DEV - 1)]
        for cp in sends:
            cp.start()
        buf_ref[me] = v_ref[...]
        for j, (px, py, pc) in enumerate(peers):
            copy(j, 4 * px + 2 * py + pc).wait_recv()
        for cp in sends:
            cp.wait_send()
        acc = buf_ref[0]
        for d in range(1, N_DEV):
            acc = acc + buf_ref[d]
        out_ref[...] = acc

    vmem = pl.BlockSpec(memory_space=pltpu.VMEM)
    return pl.pallas_call(
        body, name="all_reduce_small", in_specs=[vmem], out_specs=vmem,
        out_shape=jax.ShapeDtypeStruct(v.shape, F32),
        scratch_shapes=[pltpu.VMEM((N_DEV, rows, LANES), F32), pltpu.SemaphoreType.DMA((N_DEV - 1,)),
                        pltpu.SemaphoreType.DMA((N_DEV - 1,))],
    )(v)


def _row_block(*sizes):
    return next(t for t in (256, 192, 128, 64) if all(s % t == 0 for s in sizes))


def _sum_chips(parts, name):
    _, rows, cols = parts[0].shape
    n = len(parts)
    tr = _row_block(rows)

    def body(*refs):
        o_ref = refs[n]
        for l in range(n):
            @pl.when(pl.program_id(0) == l)
            def _(p_ref=refs[l]):
                acc = p_ref[0].astype(F32)
                for s in range(1, N_CHIPS):
                    acc = acc + p_ref[s].astype(F32)
                o_ref[0] = acc

    return pl.pallas_call(
        body, name=name, grid=(n, rows // tr),
        in_specs=[pl.BlockSpec((N_CHIPS, tr, cols), lambda l, i, k=k: (0, jnp.where(l == k, i, 0), 0)) for k in range(n)],
        out_specs=pl.BlockSpec((1, tr, cols), lambda l, i: (l, i, 0)),
        out_shape=jax.ShapeDtypeStruct((n, rows, cols), F32),
        compiler_params=_params(("arbitrary", "arbitrary")),
    )(*parts)


def _adam_update(w, m, v, g):
    c1 = 1.0 - ADAM_B1 ** ADAM_STEP
    c2 = 1.0 - ADAM_B2 ** ADAM_STEP
    m_new = ADAM_B1 * m + (1.0 - ADAM_B1) * g
    v_new = ADAM_B2 * v + (1.0 - ADAM_B2) * (g * g)
    return -ADAM_LR * ((m_new / c1) / (jnp.sqrt(v_new / c2) + ADAM_EPS) + ADAM_WD * w), m_new, v_new


def _adamw_rows(w, m, v, g_parts, first, name):
    n_layers, rows, cols = w.shape
    tr = _row_block(rows, first)
    n = len(g_parts)

    def body(*refs):
        w_ref, m_ref, v_ref = refs[:3]
        g_out, d_out, m_out, v_out = refs[3 + n:]
        g = refs[3][...]
        for r in refs[4:3 + n]:
            g = g + r[...]
        g = g[:, :, :cols]
        d_out[...], m_out[...], v_out[...] = _adam_update(w_ref[...], m_ref[...], v_ref[...], g)
        g_out[...] = g

    blk = pl.BlockSpec((1, tr, cols), lambda l, i: (l, i, 0))
    g_blk = pl.BlockSpec((1, tr, g_parts[0].shape[2]), lambda l, i: (l, first // tr + i, 0))
    return pl.pallas_call(
        body, name=name, grid=(n_layers, rows // tr),
        in_specs=[blk] * 3 + [g_blk] * n, out_specs=[blk] * 4,
        out_shape=[jax.ShapeDtypeStruct(w.shape, F32)] * 4,
        compiler_params=_params(("parallel", "parallel")),
    )(w, m, v, *g_parts)


def _adamw(w, m, v, g_parts, name):
    rows, cols = w.shape
    tr = min(rows, 256)
    n = len(g_parts)

    def body(*refs):
        w_ref, m_ref, v_ref = refs[:3]
        g_refs = refs[3:3 + n]
        g_out, d_out, m_out, v_out = refs[3 + n:]
        g = g_refs[0][...]
        for r in g_refs[1:]:
            g = g + r[...]
        d_out[...], m_out[...], v_out[...] = _adam_update(w_ref[...], m_ref[...], v_ref[...], g)
        g_out[...] = g

    blk = pl.BlockSpec((tr, cols), lambda i: (i, 0))
    return pl.pallas_call(
        body, name=name, grid=(rows // tr,),
        in_specs=[blk] * (3 + n), out_specs=[blk] * 4,
        out_shape=[jax.ShapeDtypeStruct((rows, cols), F32)] * 4,
        compiler_params=_params(("parallel",)),
    )(w, m, v, *g_parts)


def _pack(parts, rows, fill=0.0):
    flat = jnp.concatenate([p.reshape(-1) for p in parts])
    return jnp.pad(flat, (0, rows * LANES - flat.shape[0]), constant_values=fill).reshape(rows, LANES)


def _unpack(packed, shapes):
    flat = packed.reshape(-1)
    out, at = [], 0
    for shp in shapes:
        size = 1
        for s in shp:
            size *= s
        out.append(flat[at:at + size].reshape(shp))
        at += size
    return out


def _packed_rows(shapes):
    total = 0
    for shp in shapes:
        size = 1
        for s in shp:
            size *= s
        total += size
    return -(-total // (8 * LANES)) * 8


def _cols_full(g, l):
    t = g[:, l]
    return jnp.moveaxis(t, 0, 1).reshape(t.shape[1], N_CHIPS * t.shape[2])


def _pad_cols(t):
    return jnp.pad(t, ((0, 0),) * (t.ndim - 1) + ((0, D_MODEL - t.shape[-1]),))


def kernel(x, norm1_g, w_in, dn_conv_w, dn_a_log, dn_dt_bias, dn_norm_g, sc_conv_w, sc_norm_g, w_out, norm2_g, ffn_w_gate, ffn_w_up, ffn_w_down, final_norm_g, loss_target, m_norm1_g, m_w_in, m_dn_conv_w, m_dn_a_log, m_dn_dt_bias, m_dn_norm_g, m_sc_conv_w, m_sc_norm_g, m_w_out, m_norm2_g, m_ffn_w_gate, m_ffn_w_up, m_ffn_w_down, m_final_norm_g, v_norm1_g, v_w_in, v_dn_conv_w, v_dn_a_log, v_dn_dt_bias, v_dn_norm_g, v_sc_conv_w, v_sc_norm_g, v_w_out, v_norm2_g, v_ffn_w_gate, v_ffn_w_up, v_ffn_w_down, v_final_norm_g):
    chip = 2 * lax.axis_index("x") + lax.axis_index("y")

    g_cw, g_scw = _chip_exchange([dn_conv_w, sc_conv_w], "gather_conv", gather=True)

    t_last = lambda t: jnp.swapaxes(t, -1, -2)
    gate_t, up_t = t_last(ffn_w_gate), t_last(ffn_w_up)
    zero_token = jnp.zeros((8, LANES), F32)

    def shares(l, tie):
        share_a = jnp.concatenate([_pad_cols(w_in[l] + tie), w_out[l]], axis=0).astype(BF16)
        share_b = jnp.concatenate([gate_t[l] + tie, up_t[l], ffn_w_down[l]], axis=0).astype(BF16)
        return share_a, _own_slot(share_a), share_b, _own_slot(share_b)

    def gather_start(l, packed, after):
        a = _exchange_start(packed[0], packed[1], after, "gather_a_start_%d" % l, gather=True)
        b = _exchange_start(packed[2], packed[3], a[4], "gather_b_start_%d" % l, gather=True)
        return a, b

    ga, gb = gather_start(0, shares(0, 0.0), g_cw)
    packed = [None] + [shares(l, gb[4][0, 0]) for l in range(1, DEPTH)]
    packed_all = sum(t[0, 0].astype(F32) for p in packed[1:] for t in (p[0], p[2]))
    land_a = _exchange_wait(ga, zero_token + packed_all, "gather_a_wait_0", gather=True)
    act = x[0]
    layers, saved_m, saved_f, lands_b = [], [], [], []
    for l in range(DEPTH):
        hold = 0.0
        if l + 1 < DEPTH:
            ga, gb_next = gather_start(l + 1, packed[l + 1], land_a)
            hold = gb_next[4][0:1, 0:1]
        al, dt = _gate_rows(dn_a_log[l], dn_dt_bias[l])
        layers.append(dict(
            g1=norm1_g[l][None] + hold, cw=_pad_rows(_cols_full(g_cw, l)), al=al, dt=dt,
            gn=dn_norm_g[l][None], scw=_pad_rows(_cols_full(g_scw, l)), gs=sc_norm_g[l][None],
            land_a=land_a, g2=norm2_g[l][None]))
        x1, s = _mixer_fwd(act, layers[l])
        saved_m.append(s)
        lands_b.append(_exchange_wait(gb, x1, "gather_b_wait_%d" % l, gather=True))
        act, s = _ffn_fwd(x1, layers[l], lands_b[l])
        saved_f.append(s)
        if l + 1 < DEPTH:
            land_a = _exchange_wait(ga, act, "gather_a_wait_%d" % (l + 1), gather=True)
            gb = gb_next

    dact, dact_bf16, loss_part, d_final = _loss_head(act, final_norm_g[None], loss_target[0])
    grads, reduce_a, reduce_b = [None] * DEPTH, [None] * DEPTH, [None] * DEPTH
    hold = 0.0
    for l in reversed(range(DEPTH)):
        p = layers[l]
        dx1, dx1_bf16, parts, dg2 = _ffn_back(dact, dact_bf16, saved_f[l], dict(p, g2=p["g2"] + hold), lands_b[l])
        reduce_b[l] = _exchange_start(parts, parts, zero_token, "reduce_b_start_%d" % l, gather=False)
        dact, dact_bf16, parts, gm = _mixer_bwd(dx1, dx1_bf16, saved_m[l], dict(p, gn=p["gn"] + reduce_b[l][4][0:1, 0:1]))
        reduce_a[l] = _exchange_start(parts, parts, zero_token, "reduce_a_start_%d" % l, gather=False)
        hold = reduce_a[l][4][0:1, 0:1]
        grads[l] = dict(gm, g2=dg2)
    loss = lax.psum(loss_part[0, 0], ("x", "y", "c"))
    stack = lambda key: jnp.stack([grads[l][key] for l in range(DEPTH)])

    got_b = [_exchange_wait(reduce_b[l], reduce_a[0][4], "reduce_b_wait_%d" % l, gather=False)
             for l in reversed(range(DEPTH))][::-1]
    sum_b = _sum_chips(got_b, "sum_chips_b")
    other_b, = _swap_sibling([sum_b], "swap_sibling_b")
    big = dict(
        ffn_w_gate=[t_last(o) for o in _adamw_rows(gate_t, t_last(m_ffn_w_gate), t_last(v_ffn_w_gate),
                                                   [sum_b, other_b], 0, "adamw_gate")],
        ffn_w_up=[t_last(o) for o in _adamw_rows(up_t, t_last(m_ffn_w_up), t_last(v_ffn_w_up),
                                                 [sum_b, other_b], FF_SHARD, "adamw_up")],
        ffn_w_down=_adamw_rows(ffn_w_down, m_ffn_w_down, v_ffn_w_down, [sum_b, other_b], 2 * FF_SHARD, "adamw_down"))
    after_b = big["ffn_w_down"][1]
    got_a = [_exchange_wait(reduce_a[l], after_b, "reduce_a_wait_%d" % l, gather=False) for l in reversed(range(DEPTH))][::-1]
    sum_a = _sum_chips(got_a, "sum_chips_a")
    other_a, = _swap_sibling([sum_a], "swap_sibling_a")
    big.update(
        w_in=_adamw_rows(w_in, m_w_in, v_w_in, [sum_a, other_a], 0, "adamw_w_in"),
        w_out=_adamw_rows(w_out, m_w_out, v_w_out, [sum_a, other_a], A_OUT_AT, "adamw_w_out"))

    full_shapes = [(DEPTH, D_MODEL), (DEPTH, D_MODEL), (DEPTH, HEAD_DIM), (DEPTH, SC_WIDTH), (DEPTH, HEADS),
                   (DEPTH, HEADS), (D_MODEL,), (DEPTH, 4, QKV), (DEPTH, 3, SC_WIDTH)]
    small_keys = ("g1", "g2", "gn", "gs", "al", "dt")
    packed = _pack([stack(k) for k in small_keys] + [d_final[0], stack("cw"), stack("scw")], _packed_rows(full_shapes))
    sg = _unpack(_all_reduce_small(packed), full_shapes)
    sg[7] = lax.dynamic_slice_in_dim(sg[7], chip * (QKV // N_CHIPS), QKV // N_CHIPS, axis=2)
    sg[8] = lax.dynamic_slice_in_dim(sg[8], chip * (SC_WIDTH // N_CHIPS), SC_WIDTH // N_CHIPS, axis=2)
    small_names = ("norm1_g", "norm2_g", "dn_norm_g", "sc_norm_g", "dn_a_log", "dn_dt_bias", "final_norm_g",
                   "dn_conv_w", "sc_conv_w")
    sw = (norm1_g, norm2_g, dn_norm_g, sc_norm_g, dn_a_log, dn_dt_bias, final_norm_g, dn_conv_w, sc_conv_w)
    sm = (m_norm1_g, m_norm2_g, m_dn_norm_g, m_sc_norm_g, m_dn_a_log, m_dn_dt_bias, m_final_norm_g, m_dn_conv_w, m_sc_conv_w)
    sv = (v_norm1_g, v_norm2_g, v_dn_norm_g, v_sc_norm_g, v_dn_a_log, v_dn_dt_bias, v_final_norm_g, v_dn_conv_w, v_sc_conv_w)
    shard_shapes = [t.shape for t in sw]
    rows = _packed_rows(shard_shapes)
    outs = _adamw(_pack(sw, rows), _pack(sm, rows), _pack(sv, rows, fill=1.0), [_pack(sg, rows)], "adamw_small")
    small = {name: [] for name in small_names}
    for o in outs:
        for name, t in zip(small_names, _unpack(o, shard_shapes)):
            small[name].append(t)

    order = ("norm1_g", "w_in", "dn_conv_w", "dn_a_log", "dn_dt_bias", "dn_norm_g", "sc_conv_w", "sc_norm_g", "w_out",
             "norm2_g", "ffn_w_gate", "ffn_w_up", "ffn_w_down", "final_norm_g")
    result = {**big, **small}
    return (loss, dact[None], *[result[n][0] for n in order], *[result[n][1] for n in order],
            *[result[n][2] for n in order], *[result[n][3] for n in order])
```

```python
import jax
import jax.numpy as jnp
from jax import lax
from jax.experimental import pallas as pl
from jax.experimental.pallas import tpu as pltpu

F32 = jnp.float32
BF16 = jnp.bfloat16
MESH = pl.DeviceIdType.MESH

D_MODEL = 1024
DEPTH = 4
HEADS = 4
HEAD_DIM = 128
DN_WIDTH = HEADS * HEAD_DIM
SC_WIDTH = 512
SC_GROUPS = 4
D_FF = 2816
CHUNK = 64
QKV = 3 * DN_WIDTH
W_IN_COLS = 4 * DN_WIDTH + 2 * HEADS + 3 * SC_WIDTH
WA_COLS = QKV + DN_WIDTH + 3 * SC_WIDTH
LANES = 128
EPS = 1e-6
Q_SCALE = HEAD_DIM ** -0.5
N_CHIPS = 4
N_DEV = 8
IN_SHARD = W_IN_COLS // N_CHIPS
OUT_SHARD = D_MODEL // N_CHIPS
FF_SHARD = D_FF // N_CHIPS
A_OUT_AT = D_MODEL
A_ROWS = D_MODEL + OUT_SHARD
B_ROWS = 3 * FF_SHARD

ADAM_LR = 0.001
ADAM_B1 = 0.9
ADAM_B2 = 0.999
ADAM_EPS = 1e-08
ADAM_WD = 0.01
ADAM_STEP = 10

VMEM_LIMIT = 56 * 1024 * 1024

NN = (((1,), (0,)), ((), ()))
NT = (((1,), (1,)), ((), ()))
TN = (((0,), (0,)), ((), ()))


def _mm(a, b, dims=NN):
    return lax.dot_general(a.astype(BF16), b.astype(BF16), dims, preferred_element_type=F32)


def _mm32(a, b, dims=NN):
    return lax.dot_general(a, b, dims, preferred_element_type=F32, precision=lax.Precision.HIGHEST)


def _params(sem, vmem=VMEM_LIMIT):
    return pltpu.CompilerParams(dimension_semantics=sem, vmem_limit_bytes=vmem)


def _sigmoid(x):
    return 0.5 * jnp.tanh(0.5 * x) + 0.5


def _softplus(x):
    return jnp.maximum(x, 0.0) + jnp.log1p(jnp.exp(-jnp.abs(x)))


def _row_acc(acc_ref, val):
    acc_ref[0:1, :] += jnp.sum(val, axis=0, keepdims=True)


def _rms_bwd(dh, xh, r, gain):
    dxh = dh * gain
    return r * (dxh - xh * jnp.mean(dxh * xh, axis=-1, keepdims=True))


def _before_halo(tb):
    return lambda i: (jnp.maximum(i * (tb // 8) - 1, 0), 0)


def _after_halo(tb, n_rows):
    last = n_rows // 8 - 1
    return lambda i: (jnp.minimum((i + 1) * (tb // 8), last), 0)


def _rows_from(xc, offset, tb):
    part = offset % 8
    if part:
        xc = pltpu.roll(xc, xc.shape[0] - part, 0)
    return xc[offset - part:offset - part + tb, :]


def _taps(xc, w, n_taps, tb, first):
    out = w[0:1, :] * _rows_from(xc, first, tb)
    for j in range(1, n_taps):
        out = out + w[j:j + 1, :] * _rows_from(xc, first + j, tb)
    return out


W_Z = QKV
W_BD = W_Z + DN_WIDTH
W_SC = W_BD + 2 * HEADS
P_SC = QKV
P_Z = P_SC + 3 * SC_WIDTH
P_BD = P_Z + DN_WIDTH
P_COLS = P_BD + LANES


def _w_in_cols(shards, lo, hi):
    pieces = []
    for s in range(N_CHIPS):
        a, b = max(lo, IN_SHARD * s), min(hi, IN_SHARD * (s + 1))
        if a < b:
            pieces.append(shards[s][:, a - IN_SHARD * s:b - IN_SHARD * s])
    return pieces[0] if len(pieces) == 1 else jnp.concatenate(pieces, axis=1)


def _in_proj(x, g1, land_a):
    T = x.shape[0]
    tb = 256

    def body(x_ref, g_ref, w_ref, qkv_ref, z_ref, sc_ref, bd_ref, h_ref):
        xv = x_ref[...]
        r = lax.rsqrt(jnp.mean(xv * xv, axis=-1, keepdims=True) + EPS)
        h = (xv * r * g_ref[...]).astype(BF16)
        shards = [jnp.dot(h, w_ref[s], preferred_element_type=F32) for s in range(N_CHIPS)]
        qkv_ref[...] = _w_in_cols(shards, 0, W_Z)
        z_ref[...] = _w_in_cols(shards, W_Z, W_BD)
        bd_ref[...] = jnp.concatenate([_w_in_cols(shards, W_BD, W_SC), jnp.zeros((tb, LANES - 2 * HEADS), F32)], axis=1)
        sc_ref[...] = _w_in_cols(shards, W_SC, W_IN_COLS)
        h_ref[...] = h

    tok = lambda w: pl.BlockSpec((tb, w), lambda i: (i, 0))
    return pl.pallas_call(
        body, name="in_proj", grid=(T // tb,),
        in_specs=[tok(D_MODEL), pl.BlockSpec(g1.shape, lambda i: (0, 0)), _shard_rows(land_a, 0, D_MODEL)],
        out_specs=[tok(QKV), tok(DN_WIDTH), tok(3 * SC_WIDTH), tok(LANES), tok(D_MODEL)],
        out_shape=[jax.ShapeDtypeStruct((T, QKV), F32), jax.ShapeDtypeStruct((T, DN_WIDTH), F32),
                   jax.ShapeDtypeStruct((T, 3 * SC_WIDTH), F32), jax.ShapeDtypeStruct((T, LANES), F32),
                   jax.ShapeDtypeStruct((T, D_MODEL), BF16)],
        compiler_params=_params(("parallel",)),
    )(x, g1, land_a)


def _dp_block(tb, first, width, index=lambda i: i):
    assert first % width == 0
    return pl.BlockSpec((tb, width), lambda i: (index(i), first // width))


def _dn_act(pre, halo, cw, tb):
    xc = jnp.concatenate([halo, pre], axis=0)
    c = _taps(xc, cw, 4, tb, 5)
    sg = _sigmoid(c)
    return xc, c, sg, c * sg


def _gates(bd, al_row, dt_row):
    lane = lax.broadcasted_iota(jnp.int32, bd.shape, 1)
    beta = _sigmoid(bd)
    g = -jnp.exp(al_row) * _softplus(bd + dt_row)
    return jnp.where(lane < HEADS, beta, jnp.where(lane < 2 * HEADS, g, 0.0))


def _dn_prep(qkv, cw, bd, al_row, dt_row):
    T = qkv.shape[0]
    tb = 512

    def body(pre_ref, halo_ref, cw_ref, bd_ref, al_ref, dt_ref, q_ref, k_ref, v_ref, bg_ref):
        halo = jnp.where(pl.program_id(0) > 0, halo_ref[...], 0.0)
        _, _, _, a = _dn_act(pre_ref[...], halo, cw_ref[...], tb)
        for hh in range(HEADS):
            sl = slice(HEAD_DIM * hh, HEAD_DIM * (hh + 1))
            qs = a[:, sl]
            q_ref[:, sl] = qs * (lax.rsqrt(jnp.sum(qs * qs, axis=-1, keepdims=True) + EPS) * Q_SCALE)
            ks = a[:, DN_WIDTH + HEAD_DIM * hh:DN_WIDTH + HEAD_DIM * (hh + 1)]
            k_ref[:, sl] = ks * lax.rsqrt(jnp.sum(ks * ks, axis=-1, keepdims=True) + EPS)
        v_ref[...] = a[:, 2 * DN_WIDTH:]
        gates = _gates(bd_ref[...], al_ref[...], dt_ref[...])
        lane = lax.broadcasted_iota(jnp.int32, gates.shape, 1)
        bg_ref[...] = jnp.where(lane < HEADS, gates, _mm32(_chunk_cumsum_matrix(tb), gates))

    tok = lambda w: pl.BlockSpec((tb, w), lambda i: (i, 0))
    full = lambda a: pl.BlockSpec(a.shape, lambda i: (0, 0))
    return pl.pallas_call(
        body, name="dn_prep", grid=(T // tb,),
        in_specs=[tok(QKV), pl.BlockSpec((8, QKV), _before_halo(tb)), full(cw), tok(LANES), full(al_row), full(dt_row)],
        out_specs=[tok(DN_WIDTH), tok(DN_WIDTH), tok(DN_WIDTH), tok(LANES)],
        out_shape=[jax.ShapeDtypeStruct((T, DN_WIDTH), F32)] * 3 + [jax.ShapeDtypeStruct((T, LANES), F32)],
        compiler_params=_params(("parallel",)),
    )(qkv, qkv, cw, bd, al_row, dt_row)


def _chunk_masks():
    row = lax.broadcasted_iota(jnp.int32, (CHUNK, CHUNK), 0)
    col = lax.broadcasted_iota(jnp.int32, (CHUNK, CHUNK), 1)
    return row >= col, row > col


def _chunk_cumsum_matrix(n):
    row = lax.broadcasted_iota(jnp.int32, (n, n), 0)
    col = lax.broadcasted_iota(jnp.int32, (n, n), 1)
    return jnp.logical_and(row >= col, row // CHUNK == col // CHUNK).astype(F32)


def _chunk_units(q_ref, k_ref, v_ref, bg_ref, rows):
    bgc = bg_ref[rows, :]
    bg_t = bgc.T
    qv, kv, vv = q_ref[rows, :], k_ref[rows, :], v_ref[rows, :]
    units = []
    for h in range(HEADS):
        sl = slice(HEAD_DIM * h, HEAD_DIM * (h + 1))
        units.append((qv[:, sl], kv[:, sl], vv[:, sl], bgc[:, h:h + 1], bgc[:, HEADS + h:HEADS + h + 1],
                      bg_t[HEADS + h:HEADS + h + 1, :]))
    return units


def _units_local(units, masks):
    causal, strict = masks
    pre = []
    for q, k, v, beta, gc, gr in units:
        kb = k * beta
        eg = jnp.exp(gc)
        g_last = gc[CHUNK - 1:CHUNK, :]
        ek = jnp.exp(g_last - gc)
        pre.append(dict(q=q, k=k, v=v, beta=beta, decay=jnp.exp(jnp.where(causal, gc - gr, -1e30)), kb=kb, vb=v * beta,
                        eg=eg, kbg=kb * eg, ek=ek, gl=jnp.exp(g_last), q_dec=q * eg, k_dec=k * ek))
    both = [_mm(jnp.concatenate([p["kb"], p["q"]], axis=0), p["k"], NT) for p in pre]
    for p, b in zip(pre, both):
        p["low"] = jnp.where(strict, b[:CHUNK] * p["decay"], 0.0)
        p["qk"] = jnp.where(causal, b[CHUNK:] * p["decay"], 0.0)
    xs = [-p["low"] for p in pre]
    pw = [_mm(p["low"], p["low"]) for p in pre]
    for _ in range(4):
        both = [_mm(jnp.concatenate([pp, x], axis=0), pp) for pp, x in zip(pw, xs)]
        xs = [x + pp + b[CHUNK:] for x, pp, b in zip(xs, pw, both)]
        pw = [b[:CHUNK] for b in both]
    last = [_mm(x, pp) for x, pp in zip(xs, pw)]
    xs = [x + pp + b for x, pp, b in zip(xs, pw, last)]
    uw = [_mm(x, jnp.concatenate([p["vb"], p["kbg"]], axis=1)) for x, p in zip(xs, pre)]
    for p, x, b in zip(pre, xs, uw):
        p["xm"] = x
        p["u"] = p["vb"] + b[:, :HEAD_DIM]
        p["w"] = p["kbg"] + b[:, HEAD_DIM:]
    return pre


def _delta_fwd(q, k, v, bg):
    T = q.shape[0]
    tb = 512
    n_chunk = tb // CHUNK

    def body(q_ref, k_ref, v_ref, bg_ref, o_ref, st_ref, s_ref):
        @pl.when(pl.program_id(0) == 0)
        def _():
            s_ref[...] = jnp.zeros_like(s_ref)

        masks = _chunk_masks()

        def pair(pi, carry):
            rows = [pl.ds(pl.multiple_of((2 * pi + j) * CHUNK, CHUNK), CHUNK) for j in range(2)]
            loc = _units_local(_chunk_units(q_ref, k_ref, v_ref, bg_ref, rows[0])
                               + _chunk_units(q_ref, k_ref, v_ref, bg_ref, rows[1]), masks)
            states = [s_ref[h] for h in range(HEADS)]
            for j in range(2):
                lj = loc[HEADS * j:HEADS * (j + 1)]
                ws = [_mm(jnp.concatenate([p["w"], p["q_dec"]], axis=0), s) for p, s in zip(lj, states)]
                v_new = [p["u"] - b[:CHUNK] for p, b in zip(lj, ws)]
                intra = [_mm(p["qk"], vn) for p, vn in zip(lj, v_new)]
                upd = [_mm(p["k_dec"], vn, TN) for p, vn in zip(lj, v_new)]
                o_ref[rows[j], :] = jnp.concatenate([b[CHUNK:] + a for b, a in zip(ws, intra)], axis=1)
                for h in range(HEADS):
                    st_ref[2 * pi + j, h] = states[h]
                states = [p["gl"] * s + d for p, s, d in zip(lj, states, upd)]
            for h in range(HEADS):
                s_ref[h] = states[h]
            return carry

        lax.fori_loop(0, n_chunk // 2, pair, 0)

    tok = lambda w: pl.BlockSpec((tb, w), lambda i: (i, 0))
    return pl.pallas_call(
        body, name="delta_fwd", grid=(T // tb,),
        in_specs=[tok(DN_WIDTH), tok(DN_WIDTH), tok(DN_WIDTH), tok(LANES)],
        out_specs=[tok(DN_WIDTH), pl.BlockSpec((n_chunk, HEADS, HEAD_DIM, HEAD_DIM), lambda i: (i, 0, 0, 0))],
        out_shape=[jax.ShapeDtypeStruct((T, DN_WIDTH), F32),
                   jax.ShapeDtypeStruct((T // CHUNK, HEADS, HEAD_DIM, HEAD_DIM), F32)],
        scratch_shapes=[pltpu.VMEM((HEADS, HEAD_DIM, HEAD_DIM), F32)],
        compiler_params=_params(("arbitrary",)),
    )(q, k, v, bg)


def _dn_out(o, z, gn):
    outs, ohs, rs = [], [], []
    for hh in range(HEADS):
        oh = o[:, HEAD_DIM * hh:HEAD_DIM * (hh + 1)]
        r = lax.rsqrt(jnp.mean(oh * oh, axis=-1, keepdims=True) + EPS)
        ohs.append(oh * r)
        rs.append(r)
    sz = _sigmoid(z)
    oh = jnp.concatenate(ohs, axis=1)
    gn4 = jnp.concatenate([gn] * HEADS, axis=1)
    return oh * gn4 * (z * sz), oh, rs, sz, gn4


def _sc_fwd(sc_in, halo, cw, tb):
    xc = jnp.concatenate([halo, sc_in], axis=0)
    u = xc[:, SC_WIDTH:2 * SC_WIDTH] * xc[:, 2 * SC_WIDTH:]
    cv = _taps(u, cw, 3, tb, 6)
    gate_b = sc_in[:, :SC_WIDTH]
    y = gate_b * cv
    gw = SC_WIDTH // SC_GROUPS
    yhs, rs = [], []
    for gi in range(SC_GROUPS):
        yg = y[:, gw * gi:gw * (gi + 1)]
        r = lax.rsqrt(jnp.mean(yg * yg, axis=-1, keepdims=True) + EPS)
        yhs.append(yg * r)
        rs.append(r)
    return u, cv, gate_b, jnp.concatenate(yhs, axis=1), rs


def _shard_rows(land, first, rows):
    assert first % rows == 0 and land.shape[0] == N_CHIPS
    return pl.BlockSpec((N_CHIPS, rows, land.shape[2]), lambda i: (0, first // rows, 0))


def _whole(w_ref):
    n, rows, cols = w_ref.shape
    return w_ref[...].reshape(n * rows, cols)


def _mix_out(o, z, sc_in, x, land_a, gn, scw, gs):
    T = x.shape[0]
    tb = 256

    def body(o_ref, z_ref, sc_ref, halo_ref, x_ref, w_ref, gn_ref, scw_ref, gs_ref, x1_ref, mix_ref):
        o_n = _dn_out(o_ref[...], z_ref[...], gn_ref[...])[0]
        halo = jnp.where(pl.program_id(0) > 0, halo_ref[...], 0.0)
        yh = _sc_fwd(sc_ref[...], halo, scw_ref[...], tb)[3]
        mix = jnp.concatenate([o_n, yh * gs_ref[...]], axis=1).astype(BF16)
        x1_ref[...] = x_ref[...] + jnp.dot(mix, _whole(w_ref), preferred_element_type=F32)
        mix_ref[...] = mix

    tok = lambda w: pl.BlockSpec((tb, w), lambda i: (i, 0))
    full = lambda a: pl.BlockSpec(a.shape, lambda i: (0, 0))
    return pl.pallas_call(
        body, name="mix_out", grid=(T // tb,),
        in_specs=[tok(DN_WIDTH), tok(DN_WIDTH), tok(3 * SC_WIDTH), pl.BlockSpec((8, 3 * SC_WIDTH), _before_halo(tb)),
                  tok(D_MODEL), _shard_rows(land_a, A_OUT_AT, OUT_SHARD), full(gn), full(scw), full(gs)],
        out_specs=[tok(D_MODEL), tok(D_MODEL)],
        out_shape=[jax.ShapeDtypeStruct((T, D_MODEL), F32), jax.ShapeDtypeStruct((T, D_MODEL), BF16)],
        compiler_params=_params(("parallel",)),
    )(o, z, sc_in, sc_in, x, land_a, gn, scw, gs)


def _ffn(x1, g2, land_b):
    T = x1.shape[0]
    tb = 256

    def body(x_ref, g_ref, wgt_ref, wut_ref, wd_ref, x2_ref, a_ref, b_ref, h_ref):
        xv = x_ref[...]
        r = lax.rsqrt(jnp.mean(xv * xv, axis=-1, keepdims=True) + EPS)
        h = (xv * r * g_ref[...]).astype(BF16)
        a = lax.dot_general(h, _whole(wgt_ref), NT, preferred_element_type=F32)
        b = lax.dot_general(h, _whole(wut_ref), NT, preferred_element_type=F32)
        act = (a * _sigmoid(a) * b).astype(BF16)
        x2_ref[...] = xv + jnp.dot(act, _whole(wd_ref), preferred_element_type=F32)
        a_ref[...] = a.astype(BF16)
        b_ref[...] = b.astype(BF16)
        h_ref[...] = h

    tok = lambda w: pl.BlockSpec((tb, w), lambda i: (i, 0))
    return pl.pallas_call(
        body, name="ffn", grid=(T // tb,),
        in_specs=[tok(D_MODEL), pl.BlockSpec(g2.shape, lambda i: (0, 0)), _shard_rows(land_b, 0, FF_SHARD),
                  _shard_rows(land_b, FF_SHARD, FF_SHARD), _shard_rows(land_b, 2 * FF_SHARD, FF_SHARD)],
        out_specs=[tok(D_MODEL), tok(D_FF), tok(D_FF), tok(D_MODEL)],
        out_shape=[jax.ShapeDtypeStruct((T, D_MODEL), F32), jax.ShapeDtypeStruct((T, D_FF), BF16),
                   jax.ShapeDtypeStruct((T, D_FF), BF16), jax.ShapeDtypeStruct((T, D_MODEL), BF16)],
        compiler_params=_params(("parallel",)),
    )(x1, g2, land_b, land_b, land_b)


def _loss_head(x, gf, target):
    T = x.shape[0]
    tb = 512

    def body(x_ref, g_ref, t_ref, dx_ref, dxb_ref, loss_ref, dg_ref):
        @pl.when(pl.program_id(0) == 0)
        def _():
            loss_ref[...] = jnp.zeros_like(loss_ref)
            dg_ref[...] = jnp.zeros_like(dg_ref)

        xv = x_ref[...]
        r = lax.rsqrt(jnp.mean(xv * xv, axis=-1, keepdims=True) + EPS)
        xh = xv * r
        err = xh * g_ref[...] - t_ref[...]
        per_tok = jnp.mean(err * err, axis=-1, keepdims=True)
        loss_ref[...] += 0.5 * jnp.sum(per_tok, axis=0, keepdims=True)
        dy = err * (1.0 / D_MODEL)
        _row_acc(dg_ref, dy * xh)
        dx = _rms_bwd(dy, xh, r, g_ref[...])
        dx_ref[...] = dx
        dxb_ref[...] = dx.astype(BF16)

    tok = pl.BlockSpec((tb, D_MODEL), lambda i: (i, 0))
    return pl.pallas_call(
        body, name="loss_head", grid=(T // tb,),
        in_specs=[tok, pl.BlockSpec(gf.shape, lambda i: (0, 0)), tok],
        out_specs=[tok, tok, pl.BlockSpec((8, LANES), lambda i: (0, 0)), pl.BlockSpec((8, D_MODEL), lambda i: (0, 0))],
        out_shape=[jax.ShapeDtypeStruct((T, D_MODEL), F32), jax.ShapeDtypeStruct((T, D_MODEL), BF16),
                   jax.ShapeDtypeStruct((8, LANES), F32), jax.ShapeDtypeStruct((8, D_MODEL), F32)],
        compiler_params=_params(("arbitrary",)),
    )(x, gf, target)


def _ffn_bwd(dx2, x1, a, b, g2, land_b):
    T = x1.shape[0]
    tb = 256

    def body(dx2_ref, x_ref, a_ref, b_ref, g_ref, wgt_ref, wut_ref, wd_ref,
             dx1_ref, dx1b_ref, da_ref, db_ref, act_ref, dg_ref):
        @pl.when(pl.program_id(0) == 0)
        def _():
            dg_ref[...] = jnp.zeros_like(dg_ref)

        dx2v = dx2_ref[...]
        av = a_ref[...].astype(F32)
        bv = b_ref[...].astype(F32)
        dact = _mm(dx2v, _whole(wd_ref), NT)
        sa = _sigmoid(av)
        silu = av * sa
        da = (dact * bv * (sa * (1.0 + av * (1.0 - sa)))).astype(BF16)
        db = (dact * silu).astype(BF16)
        dh = _mm(da, _whole(wgt_ref)) + _mm(db, _whole(wut_ref))
        xv = x_ref[...]
        r = lax.rsqrt(jnp.mean(xv * xv, axis=-1, keepdims=True) + EPS)
        xh = xv * r
        _row_acc(dg_ref, dh * xh)
        dx1 = dx2v + _rms_bwd(dh, xh, r, g_ref[...])
        dx1_ref[...] = dx1
        dx1b_ref[...] = dx1.astype(BF16)
        da_ref[...] = da
        db_ref[...] = db
        act_ref[...] = (silu * bv).astype(BF16)

    tok = lambda w: pl.BlockSpec((tb, w), lambda i: (i, 0))
    return pl.pallas_call(
        body, name="ffn_bwd", grid=(T // tb,),
        in_specs=[tok(D_MODEL), tok(D_MODEL), tok(D_FF), tok(D_FF), pl.BlockSpec(g2.shape, lambda i: (0, 0)),
                  _shard_rows(land_b, 0, FF_SHARD), _shard_rows(land_b, FF_SHARD, FF_SHARD),
                  _shard_rows(land_b, 2 * FF_SHARD, FF_SHARD)],
        out_specs=[tok(D_MODEL), tok(D_MODEL), tok(D_FF), tok(D_FF), tok(D_FF), pl.BlockSpec((8, D_MODEL), lambda i: (0, 0))],
        out_shape=[jax.ShapeDtypeStruct((T, D_MODEL), F32), jax.ShapeDtypeStruct((T, D_MODEL), BF16)]
        + [jax.ShapeDtypeStruct((T, D_FF), BF16)] * 3 + [jax.ShapeDtypeStruct((8, D_MODEL), F32)],
        compiler_params=_params(("arbitrary",)),
    )(dx2, x1, a, b, g2, land_b, land_b, land_b)


def _wgrad_share(a, b, parts, first, name):
    T = b.shape[0]
    rows = a.shape[1] // N_CHIPS
    assert first % rows == 0 and b.shape[1] == parts.shape[2]
    bk = min(T, 1024)
    n_k = T // bk
    group = 2
    assert (group * rows) % LANES == 0

    def body(a_ref, b_ref, parts_ref, o_ref, acc_ref):
        kk = pl.program_id(1)

        @pl.when(kk == 0)
        def _():
            acc_ref[...] = jnp.zeros_like(acc_ref)

        acc_ref[...] += lax.dot_general(a_ref[...], b_ref[...], TN, preferred_element_type=F32)

        @pl.when(kk == n_k - 1)
        def _():
            for s in range(group):
                o_ref[s] = acc_ref[rows * s:rows * (s + 1), :].astype(BF16)

    return pl.pallas_call(
        body, name=name, grid=(N_CHIPS // group, n_k),
        in_specs=[pl.BlockSpec((bk, group * rows), lambda i, kk: (kk, i)),
                  pl.BlockSpec((bk, b.shape[1]), lambda i, kk: (kk, 0)), _ANY],
        out_specs=pl.BlockSpec((group, rows, b.shape[1]), lambda i, kk: (i, first // rows, 0)),
        out_shape=jax.ShapeDtypeStruct(parts.shape, BF16),
        scratch_shapes=[pltpu.VMEM((group * rows, b.shape[1]), F32)],
        input_output_aliases={2: 0},
        compiler_params=_params(("parallel", "arbitrary")),
    )(a, b, parts)


def _mix_out_bwd(dx1, o, z, sc_in, land_a, gn, scw, gs, dp):
    T = dx1.shape[0]
    tb = 256

    def body(dx_ref, o_ref, z_ref, sc_ref, halo_ref, w_ref, gn_ref, scw_ref, gs_ref, dp_ref,
             do_ref, dz_ref, dgb_ref, dcv_ref, dgn_ref, dgs_ref, dscw_ref):
        @pl.when(pl.program_id(0) == 0)
        def _():
            dgn_ref[...] = jnp.zeros_like(dgn_ref)
            dgs_ref[...] = jnp.zeros_like(dgs_ref)
            dscw_ref[...] = jnp.zeros_like(dscw_ref)

        dmix = _mm(dx_ref[...], _whole(w_ref), NT)
        don = dmix[:, :DN_WIDTH]
        dosc = dmix[:, DN_WIDTH:]
        zv = z_ref[...]
        _, oh, rs, sz, gn4 = _dn_out(o_ref[...], zv, gn_ref[...])
        silu_z = zv * sz
        dgn_full = don * oh * silu_z
        dgn_ref[0:1, :] += jnp.sum(sum(dgn_full[:, HEAD_DIM * hh:HEAD_DIM * (hh + 1)] for hh in range(HEADS)),
                                   axis=0, keepdims=True)
        dz_ref[...] = (don * oh * gn4 * (sz * (1.0 + zv * (1.0 - sz)))).astype(BF16)
        t = don * gn4 * silu_z
        for hh in range(HEADS):
            sl = slice(HEAD_DIM * hh, HEAD_DIM * (hh + 1))
            th, ohh = t[:, sl], oh[:, sl]
            do_ref[:, sl] = rs[hh] * (th - ohh * jnp.mean(th * ohh, axis=-1, keepdims=True))
        halo = jnp.where(pl.program_id(0) > 0, halo_ref[...], 0.0)
        u, cv, gate_b, yh, rys = _sc_fwd(sc_ref[...], halo, scw_ref[...], tb)
        _row_acc(dgs_ref, dosc * yh)
        ty = dosc * gs_ref[...]
        gw = SC_WIDTH // SC_GROUPS
        dys = []
        for gi in range(SC_GROUPS):
            sl = slice(gw * gi, gw * (gi + 1))
            tg, yg = ty[:, sl], yh[:, sl]
            dys.append(rys[gi] * (tg - yg * jnp.mean(tg * yg, axis=-1, keepdims=True)))
        dy = jnp.concatenate(dys, axis=1)
        dgb_ref[...] = dy * cv
        dcv = dy * gate_b
        dcv_ref[...] = dcv
        for j in range(3):
            dscw_ref[j:j + 1, :] += jnp.sum(dcv * _rows_from(u, 6 + j, tb), axis=0, keepdims=True)

    tok = lambda w: pl.BlockSpec((tb, w), lambda i: (i, 0))
    full = lambda t: pl.BlockSpec(t.shape, lambda i: (0, 0))
    acc = lambda w: pl.BlockSpec((8, w), lambda i: (0, 0))
    return pl.pallas_call(
        body, name="mix_out_bwd", grid=(T // tb,),
        in_specs=[tok(D_MODEL), tok(DN_WIDTH), tok(DN_WIDTH), tok(3 * SC_WIDTH),
                  pl.BlockSpec((8, 3 * SC_WIDTH), _before_halo(tb)), _shard_rows(land_a, A_OUT_AT, OUT_SHARD),
                  full(gn), full(scw), full(gs), _ANY],
        out_specs=[tok(DN_WIDTH), _dp_block(tb, P_Z, DN_WIDTH), tok(SC_WIDTH), tok(SC_WIDTH),
                   acc(HEAD_DIM), acc(SC_WIDTH), acc(SC_WIDTH)],
        out_shape=[jax.ShapeDtypeStruct((T, DN_WIDTH), F32), jax.ShapeDtypeStruct(dp.shape, BF16),
                   jax.ShapeDtypeStruct((T, SC_WIDTH), F32), jax.ShapeDtypeStruct((T, SC_WIDTH), F32),
                   jax.ShapeDtypeStruct((8, HEAD_DIM), F32), jax.ShapeDtypeStruct((8, SC_WIDTH), F32),
                   jax.ShapeDtypeStruct((8, SC_WIDTH), F32)],
        input_output_aliases={9: 1},
        compiler_params=_params(("arbitrary",)),
    )(dx1, o, z, sc_in, sc_in, land_a, gn, scw, gs, dp)


def _sc_conv_bwd(dcv, dgb, sc_in, scw, dp):
    T = dcv.shape[0]
    tb = 512

    def body(dcv_ref, halo_ref, dgb_ref, sc_ref, w_ref, dp_ref, out_ref):
        last = pl.program_id(0) == pl.num_programs(0) - 1
        halo = jnp.where(last, 0.0, halo_ref[...])
        xc = jnp.concatenate([dcv_ref[...], halo], axis=0)
        w = w_ref[...]
        du = w[2:3, :] * xc[0:tb, :] + w[1:2, :] * _rows_from(xc, 1, tb) + w[0:1, :] * _rows_from(xc, 2, tb)
        sc = sc_ref[...]
        out_ref[:, :SC_WIDTH] = dgb_ref[...].astype(BF16)
        out_ref[:, SC_WIDTH:2 * SC_WIDTH] = (du * sc[:, 2 * SC_WIDTH:]).astype(BF16)
        out_ref[:, 2 * SC_WIDTH:] = (du * sc[:, SC_WIDTH:2 * SC_WIDTH]).astype(BF16)

    tok = lambda w: pl.BlockSpec((tb, w), lambda i: (i, 0))
    return pl.pallas_call(
        body, name="sc_conv_bwd", grid=(T // tb,),
        in_specs=[tok(SC_WIDTH), pl.BlockSpec((8, SC_WIDTH), _after_halo(tb, T)), tok(SC_WIDTH), tok(3 * SC_WIDTH),
                  pl.BlockSpec(scw.shape, lambda i: (0, 0)), _ANY],
        out_specs=_dp_block(tb, P_SC, 3 * SC_WIDTH),
        out_shape=jax.ShapeDtypeStruct(dp.shape, BF16),
        input_output_aliases={5: 0},
        compiler_params=_params(("parallel",)),
    )(dcv, dcv, dgb, sc_in, scw, dp)


def _delta_bwd(q, k, v, bg, states, do):
    T = q.shape[0]
    tb = 512
    n_chunk = tb // CHUNK
    nb = T // tb

    def body(q_ref, k_ref, v_ref, bg_ref, st_ref, do_ref, dq_ref, dk_ref, dv_ref, dbg_ref, ds_ref):
        @pl.when(pl.program_id(0) == 0)
        def _():
            ds_ref[...] = jnp.zeros_like(ds_ref)

        masks = _chunk_masks()
        causal, strict = masks
        lane = lax.broadcasted_iota(jnp.int32, (CHUNK, LANES), 1)
        last_row = lax.broadcasted_iota(jnp.int32, (CHUNK, 1), 0) == CHUNK - 1
        cat = jnp.concatenate
        heads = range(HEADS)

        def open_chunk(ci, loc):
            rows = pl.ds(pl.multiple_of(ci * CHUNK, CHUNK), CHUNK)
            dov = do_ref[rows, :]
            return dict(rows=rows, loc=loc, do=[dov[:, HEAD_DIM * h:HEAD_DIM * (h + 1)] for h in heads],
                        state=[st_ref[ci, h] for h in heads])

        def a_free(c):
            loc, do, state = c["loc"], c["do"], c["state"]
            w_s = [_mm(p["w"], s) for p, s in zip(loc, state)]
            c["dq_dec"] = [_mm(d, s, NT) for d, s in zip(do, state)]
            c["qk_do"] = [_mm(p["qk"], d, TN) for p, d in zip(loc, do)]
            c["qd_do"] = [_mm(p["q_dec"], d, TN) for p, d in zip(loc, do)]
            c["v_new"] = [p["u"] - t for p, t in zip(loc, w_s)]
            c["dqk"] = [jnp.where(causal, _mm(d, vn, NT), 0.0) for d, vn in zip(do, c["v_new"])]

        def a_state(c, ds_next):
            c["ds_next"] = ds_next
            kd_ds = [_mm(p["k_dec"], d) for p, d in zip(c["loc"], ds_next)]
            c["dk_dec"] = [_mm(vn, d, NT) for vn, d in zip(c["v_new"], ds_next)]
            c["dv_new"] = [a + b for a, b in zip(c["qk_do"], kd_ds)]

        def b_state(c):
            loc = c["loc"]
            w_dv = [_mm(p["w"], dvn, TN) for p, dvn in zip(loc, c["dv_new"])]
            c["dw"] = [-_mm(dvn, s, NT) for dvn, s in zip(c["dv_new"], c["state"])]
            return [loc[h]["gl"] * c["ds_next"][h] + c["qd_do"][h] - w_dv[h] for h in heads]

        def c_solve(c):
            loc, dv_new, dw = c["loc"], c["dv_new"], c["dw"]
            c["dtm"] = [_mm(cat([dvn, d], axis=1), cat([p["vb"], p["kbg"]], axis=1), NT) for dvn, d, p in zip(dv_new, dw, loc)]
            x_t = [_mm(p["xm"], cat([dvn, d], axis=1), TN) for p, dvn, d in zip(loc, dv_new, dw)]
            c["dvb"] = [dvn + t[:, :HEAD_DIM] for dvn, t in zip(dv_new, x_t)]
            c["dkbg"] = [d + t[:, HEAD_DIM:] for d, t in zip(dw, x_t)]

        def d_solve(c):
            c["y"] = [t + _mm(p["xm"], t, TN) for p, t in zip(c["loc"], c["dtm"])]

        def e_solve(c):
            c["dlow"] = [jnp.where(strict, -(t + _mm(t, p["xm"], NT)), 0.0) for p, t in zip(c["loc"], c["y"])]

        def f_close(c):
            loc, rows = c["loc"], c["rows"]
            dmm = [d * p["decay"] for d, p in zip(c["dlow"], loc)]
            dnn = [d * p["decay"] for d, p in zip(c["dqk"], loc)]
            by_k = [_mm(cat([a, b], axis=0), p["k"]) for a, b, p in zip(dmm, dnn, loc)]
            dk_mm = [_mm(cat([a, b], axis=0), cat([p["kb"], p["q"]], axis=0), TN) for a, b, p in zip(dmm, dnn, loc)]
            dq_out, dk_out, dv_out = [], [], []
            dbeta_all = jnp.zeros((CHUNK, LANES), F32)
            dgc_all = jnp.zeros((CHUNK, LANES), F32)
            for h in heads:
                p = loc[h]
                dkb = by_k[h][:CHUNK] + c["dkbg"][h] * p["eg"]
                dq_out.append(by_k[h][CHUNK:] + c["dq_dec"][h] * p["eg"])
                dk_out.append(dk_mm[h] + c["dk_dec"][h] * p["ek"] + dkb * p["beta"])
                dv_out.append(c["dvb"][h] * p["beta"])
                dbeta = jnp.sum(dkb * p["k"] + c["dvb"][h] * p["v"], axis=1, keepdims=True)
                total = lambda t: jnp.sum(jnp.sum(t, axis=0, keepdims=True), axis=1, keepdims=True)
                e = c["dlow"][h] * p["low"] + c["dqk"][h] * p["qk"]
                kd = c["dk_dec"][h] * p["k_dec"]
                dgc = (jnp.sum(e, axis=1, keepdims=True) - jnp.sum(e.T, axis=1, keepdims=True)
                       + jnp.sum(c["dq_dec"][h] * p["q_dec"] + c["dkbg"][h] * p["kbg"] - kd, axis=1, keepdims=True))
                d_last = total(kd) + total(c["ds_next"][h] * c["state"][h]) * p["gl"]
                dgc = dgc + jnp.where(last_row, d_last, 0.0)
                dbeta_all = jnp.where(lane == h, dbeta, dbeta_all)
                dgc_all = jnp.where(lane == h + HEADS, dgc, dgc_all)
            dq_ref[rows, :] = cat(dq_out, axis=1)
            dk_ref[rows, :] = cat(dk_out, axis=1)
            dv_ref[rows, :] = cat(dv_out, axis=1)
            dbg_ref[rows, :] = dbeta_all + dgc_all

        def pair(pj, carry):
            hi = n_chunk - 1 - 2 * pj
            lo = hi - 1
            rows = [pl.ds(pl.multiple_of(ci * CHUNK, CHUNK), CHUNK) for ci in (hi, lo)]
            loc = _units_local(_chunk_units(q_ref, k_ref, v_ref, bg_ref, rows[0])
                               + _chunk_units(q_ref, k_ref, v_ref, bg_ref, rows[1]), masks)
            c_hi, c_lo = open_chunk(hi, loc[:HEADS]), open_chunk(lo, loc[HEADS:])
            a_free(c_hi)
            a_free(c_lo)
            a_state(c_hi, [ds_ref[h] for h in heads])
            ds_mid = b_state(c_hi)
            a_state(c_lo, ds_mid)
            c_solve(c_hi)
            ds_out = b_state(c_lo)
            for h in heads:
                ds_ref[h] = ds_out[h]
            d_solve(c_hi)
            c_solve(c_lo)
            e_solve(c_hi)
            d_solve(c_lo)
            f_close(c_hi)
            e_solve(c_lo)
            f_close(c_lo)
            return carry

        lax.fori_loop(0, n_chunk // 2, pair, 0)

    tok = lambda w: pl.BlockSpec((tb, w), lambda i: (nb - 1 - i, 0))
    return pl.pallas_call(
        body, name="delta_bwd", grid=(nb,),
        in_specs=[tok(DN_WIDTH), tok(DN_WIDTH), tok(DN_WIDTH), tok(LANES),
                  pl.BlockSpec((n_chunk, HEADS, HEAD_DIM, HEAD_DIM), lambda i: (nb - 1 - i, 0, 0, 0)), tok(DN_WIDTH)],
        out_specs=[tok(DN_WIDTH), tok(DN_WIDTH), tok(DN_WIDTH), tok(LANES)],
        out_shape=[jax.ShapeDtypeStruct((T, DN_WIDTH), F32)] * 3 + [jax.ShapeDtypeStruct((T, LANES), F32)],
        scratch_shapes=[pltpu.VMEM((HEADS, HEAD_DIM, HEAD_DIM), F32)],
        compiler_params=_params(("arbitrary",)),
    )(q, k, v, bg, states, do)


def _dn_prep_bwd(dq, dk, dv, dbg, qkv, cw, bd, al_row, dt_row, dp):
    T = qkv.shape[0]
    tb = 256

    def body(dq_ref, dk_ref, dv_ref, dbg_ref, pre_ref, halo_ref, cw_ref, bd_ref, al_ref, dt_ref, dp_ref,
             dc_ref, dbd_ref, dcw_ref, dal_ref, ddt_ref):
        @pl.when(pl.program_id(0) == 0)
        def _():
            dcw_ref[...] = jnp.zeros_like(dcw_ref)
            dal_ref[...] = jnp.zeros_like(dal_ref)
            ddt_ref[...] = jnp.zeros_like(ddt_ref)

        halo = jnp.where(pl.program_id(0) > 0, halo_ref[...], 0.0)
        xc, c, sg, a = _dn_act(pre_ref[...], halo, cw_ref[...], tb)
        dsilu = sg * (1.0 + c * (1.0 - sg))
        for hh in range(HEADS):
            sl = slice(HEAD_DIM * hh, HEAD_DIM * (hh + 1))
            for base, g_ref, scale in ((0, dq_ref, Q_SCALE), (DN_WIDTH, dk_ref, 1.0)):
                sa = slice(base + HEAD_DIM * hh, base + HEAD_DIM * (hh + 1))
                raw = a[:, sa]
                r = lax.rsqrt(jnp.sum(raw * raw, axis=-1, keepdims=True) + EPS)
                nrm = raw * r
                gn_ = g_ref[:, sl] * scale
                dc_ref[:, sa] = r * (gn_ - nrm * jnp.sum(gn_ * nrm, axis=-1, keepdims=True)) * dsilu[:, sa]
        dc_ref[:, 2 * DN_WIDTH:] = dv_ref[...] * dsilu[:, 2 * DN_WIDTH:]
        dc = dc_ref[...]
        for j in range(4):
            dcw_ref[j:j + 1, :] += jnp.sum(dc * _rows_from(xc, 5 + j, tb), axis=0, keepdims=True)
        bdv = bd_ref[...]
        lane = lax.broadcasted_iota(jnp.int32, bdv.shape, 1)
        is_b = lane < HEADS
        dbg_in = dbg_ref[...]
        dbgv = jnp.where(is_b, dbg_in, _mm32(_chunk_cumsum_matrix(tb), dbg_in, TN))
        is_g = jnp.logical_and(lane >= HEADS, lane < 2 * HEADS)
        beta = _sigmoid(bdv)
        neg_a = -jnp.exp(al_ref[...])
        pre_sp = bdv + dt_ref[...]
        g = neg_a * _softplus(pre_sp)
        da_in = dbgv * neg_a * _sigmoid(pre_sp)
        dbd_ref[...] = jnp.where(is_b, dbgv * beta * (1.0 - beta), jnp.where(is_g, da_in, 0.0)).astype(BF16)
        _row_acc(dal_ref, jnp.where(is_g, dbgv * g, 0.0))
        _row_acc(ddt_ref, jnp.where(is_g, da_in, 0.0))

    tok = lambda w: pl.BlockSpec((tb, w), lambda i: (i, 0))
    full = lambda t: pl.BlockSpec(t.shape, lambda i: (0, 0))
    acc = lambda w: pl.BlockSpec((8, w), lambda i: (0, 0))
    return pl.pallas_call(
        body, name="dn_prep_bwd", grid=(T // tb,),
        in_specs=[tok(DN_WIDTH), tok(DN_WIDTH), tok(DN_WIDTH), tok(LANES),
                  tok(QKV), pl.BlockSpec((8, QKV), _before_halo(tb)), full(cw), tok(LANES), full(al_row), full(dt_row), _ANY],
        out_specs=[tok(QKV), _dp_block(tb, P_BD, LANES), acc(QKV), acc(LANES), acc(LANES)],
        out_shape=[jax.ShapeDtypeStruct((T, QKV), F32), jax.ShapeDtypeStruct(dp.shape, BF16),
                   jax.ShapeDtypeStruct((8, QKV), F32), jax.ShapeDtypeStruct((8, LANES), F32),
                   jax.ShapeDtypeStruct((8, LANES), F32)],
        input_output_aliases={10: 1},
        compiler_params=_params(("arbitrary",)),
    )(dq, dk, dv, dbg, qkv, qkv, cw, bd, al_row, dt_row, dp)


def _dn_conv_bwd(dc, cw, dp):
    T = dc.shape[0]
    tb = 512

    def body(dc_ref, halo_ref, w_ref, dp_ref, out_ref):
        last = pl.program_id(0) == pl.num_programs(0) - 1
        halo = jnp.where(last, 0.0, halo_ref[...])
        xc = jnp.concatenate([dc_ref[...], halo], axis=0)
        w = w_ref[...]
        acc = w[3:4, :] * xc[0:tb, :]
        for j in range(3):
            acc = acc + w[j:j + 1, :] * _rows_from(xc, 3 - j, tb)
        out_ref[...] = acc.astype(BF16)

    tok = pl.BlockSpec((tb, QKV), lambda i: (i, 0))
    return pl.pallas_call(
        body, name="dn_conv_bwd", grid=(T // tb,),
        in_specs=[tok, pl.BlockSpec((8, QKV), _after_halo(tb, T)), pl.BlockSpec(cw.shape, lambda i: (0, 0)), _ANY],
        out_specs=_dp_block(tb, 0, QKV),
        out_shape=jax.ShapeDtypeStruct(dp.shape, BF16),
        input_output_aliases={3: 0},
        compiler_params=_params(("parallel",)),
    )(dc, dc, cw, dp)


def _dp_of_chip(dp, s):
    lo, hi = IN_SHARD * s, IN_SHARD * (s + 1)
    pieces = []
    for w_at, w_end, p_at in ((0, W_Z, 0), (W_Z, W_BD, P_Z), (W_BD, W_SC, P_BD), (W_SC, W_IN_COLS, P_SC)):
        a, b = max(lo, w_at), min(hi, w_end)
        if a < b:
            pieces.append(dp[:, p_at + a - w_at:p_at + b - w_at])
    pieces.append(jnp.zeros((dp.shape[0], D_MODEL - IN_SHARD), dp.dtype))
    return jnp.concatenate(pieces, axis=1)


def _in_proj_bwd(dp, dx1, x, g1, land_a):
    T = x.shape[0]
    tb = 256

    def body(dp_ref, dx1_ref, x_ref, g_ref, w_ref, dx_ref, dxb_ref, dps_ref, dg_ref):
        @pl.when(pl.program_id(0) == 0)
        def _():
            dg_ref[...] = jnp.zeros_like(dg_ref)

        dpv = dp_ref[...]
        dh = jnp.zeros((tb, D_MODEL), F32)
        for s in range(N_CHIPS):
            dps = _dp_of_chip(dpv, s)
            dps_ref[:, D_MODEL * s:D_MODEL * (s + 1)] = dps
            dh = dh + lax.dot_general(dps, w_ref[s], NT, preferred_element_type=F32)
        xv = x_ref[...]
        r = lax.rsqrt(jnp.mean(xv * xv, axis=-1, keepdims=True) + EPS)
        xh = xv * r
        _row_acc(dg_ref, dh * xh)
        dx = dx1_ref[...] + _rms_bwd(dh, xh, r, g_ref[...])
        dx_ref[...] = dx
        dxb_ref[...] = dx.astype(BF16)

    tok = lambda w: pl.BlockSpec((tb, w), lambda i: (i, 0))
    return pl.pallas_call(
        body, name="in_proj_bwd", grid=(T // tb,),
        in_specs=[tok(P_COLS), tok(D_MODEL), tok(D_MODEL), pl.BlockSpec(g1.shape, lambda i: (0, 0)),
                  _shard_rows(land_a, 0, D_MODEL)],
        out_specs=[tok(D_MODEL), tok(D_MODEL), tok(N_CHIPS * D_MODEL), pl.BlockSpec((8, D_MODEL), lambda i: (0, 0))],
        out_shape=[jax.ShapeDtypeStruct((T, D_MODEL), F32), jax.ShapeDtypeStruct((T, D_MODEL), BF16),
                   jax.ShapeDtypeStruct((T, N_CHIPS * D_MODEL), BF16), jax.ShapeDtypeStruct((8, D_MODEL), F32)],
        compiler_params=_params(("arbitrary",)),
    )(dp, dx1, x, g1, land_a)


def _wgrad_in_share(h, dps, parts, name):
    T = h.shape[0]
    bk = min(T, 1024)
    n_k = T // bk

    def body(a_ref, b_ref, parts_ref, o_ref, acc_ref):
        kk = pl.program_id(1)

        @pl.when(kk == 0)
        def _():
            acc_ref[...] = jnp.zeros_like(acc_ref)

        acc_ref[...] += lax.dot_general(a_ref[...], b_ref[...], TN, preferred_element_type=F32)

        @pl.when(kk == n_k - 1)
        def _():
            o_ref[0] = acc_ref[...].astype(BF16)

    return pl.pallas_call(
        body, name=name, grid=(N_CHIPS, n_k),
        in_specs=[pl.BlockSpec((bk, D_MODEL), lambda j, kk: (kk, 0)), pl.BlockSpec((bk, D_MODEL), lambda j, kk: (kk, j)), _ANY],
        out_specs=pl.BlockSpec((1, D_MODEL, D_MODEL), lambda j, kk: (j, 0, 0)),
        out_shape=jax.ShapeDtypeStruct(parts.shape, BF16),
        scratch_shapes=[pltpu.VMEM((D_MODEL, D_MODEL), F32)],
        input_output_aliases={2: 0},
        compiler_params=_params(("parallel", "arbitrary")),
    )(h, dps, parts)


def _pad_rows(a, rows=8):
    return jnp.pad(a, ((0, rows - a.shape[0]), (0, 0)))


def _gate_rows(a_log, dt_bias):
    put = lambda t: jnp.pad(t.reshape(1, HEADS), ((0, 0), (HEADS, LANES - 2 * HEADS)))
    return put(a_log), put(dt_bias)


def _mixer_fwd(x, p):
    qkv, z, sc_in, bd, h = _in_proj(x, p["g1"], p["land_a"])
    q, k, v, bg = _dn_prep(qkv, p["cw"], bd, p["al"], p["dt"])
    o, states = _delta_fwd(q, k, v, bg)
    x1, mix = _mix_out(o, z, sc_in, x, p["land_a"], p["gn"], p["scw"], p["gs"])
    return x1, dict(x=x, qkv=qkv, z=z, sc_in=sc_in, bd=bd, h=h, q=q, k=k, v=v, bg=bg, o=o, states=states, mix=mix)


def _ffn_fwd(x1, p, land_b):
    x2, a, b, h2 = _ffn(x1, p["g2"], land_b)
    return x2, dict(x1=x1, a=a, b=b, h2=h2)


def _ffn_back(dx2, dx2_bf16, s, p, land_b):
    dx1, dx1_bf16, da, db, act, dg2 = _ffn_bwd(dx2, s["x1"], s["a"], s["b"], p["g2"], land_b)
    parts = lax.empty((N_CHIPS, B_ROWS, D_MODEL), BF16)
    parts = _wgrad_share(act, dx2_bf16, parts, 2 * FF_SHARD, "wgrad_down")
    parts = _wgrad_share(da, s["h2"], parts, 0, "wgrad_gate")
    parts = _wgrad_share(db, s["h2"], parts, FF_SHARD, "wgrad_up")
    return dx1, dx1_bf16, parts, dg2[0]


def _mixer_bwd(dx1, dx1_bf16, s, p):
    dp = lax.empty((dx1.shape[0], P_COLS), BF16)
    do, dp, dgb, dcv, dgn, dgs, dscw = _mix_out_bwd(dx1, s["o"], s["z"], s["sc_in"], p["land_a"], p["gn"], p["scw"], p["gs"], dp)
    dp = _sc_conv_bwd(dcv, dgb, s["sc_in"], p["scw"], dp)
    dq, dk, dv, dbg = _delta_bwd(s["q"], s["k"], s["v"], s["bg"], s["states"], do)
    dc, dp, dcw, dal, ddt = _dn_prep_bwd(dq, dk, dv, dbg, s["qkv"], p["cw"], s["bd"], p["al"], p["dt"], dp)
    dp = _dn_conv_bwd(dc, p["cw"], dp)
    dx, dx_bf16, dps, dg1 = _in_proj_bwd(dp, dx1, s["x"], p["g1"], p["land_a"])
    parts = lax.empty((N_CHIPS, A_ROWS, D_MODEL), BF16)
    parts = _wgrad_in_share(s["h"], dps, parts, "wgrad_in")
    parts = _wgrad_share(s["mix"], dx1_bf16, parts, A_OUT_AT, "wgrad_out")
    g = dict(g1=dg1[0], gn=dgn[0], gs=dgs[0], scw=dscw[:3], cw=dcw[:4], al=dal[0, HEADS:2 * HEADS], dt=ddt[0, HEADS:2 * HEADS])
    return dx, dx_bf16, parts, g


def _place():
    return lax.axis_index("x"), lax.axis_index("y"), lax.axis_index("c")


def _other_chips(x, y):
    return [(1 - x, y), (x, 1 - y), (1 - x, 1 - y)]


_HBM = pl.BlockSpec(memory_space=pltpu.HBM)


def _chip_exchange(arrs, name, gather):
    n = len(arrs)

    def body(*refs):
        ins, outs = refs[:n], refs[n:2 * n]
        send_sems, recv_sems, local_sems = refs[2 * n:]
        x, y, c = _place()
        me = 2 * x + y
        others = _other_chips(x, y)

        def remote(k, j, landing):
            px, py = others[j]
            src = ins[k] if gather else ins[k].at[2 * px + py]
            return pltpu.make_async_remote_copy(src_ref=src, dst_ref=outs[k].at[landing], send_sem=send_sems.at[k, j],
                                                recv_sem=recv_sems.at[k, j], device_id=(px, py, c), device_id_type=MESH)

        local = [pltpu.make_async_copy(ins[k] if gather else ins[k].at[me], outs[k].at[me], local_sems.at[k])
                 for k in range(n)]
        sends = [remote(k, j, me) for k in range(n) for j in range(3)]
        for cp in local + sends:
            cp.start()
        for k in range(n):
            for j, (px, py) in enumerate(others):
                remote(k, j, 2 * px + py).wait_recv()
        for cp in sends:
            cp.wait_send()
        for cp in local:
            cp.wait()

    shapes = [jax.ShapeDtypeStruct(((N_CHIPS,) + a.shape) if gather else a.shape, a.dtype) for a in arrs]
    return pl.pallas_call(
        body, name=name, in_specs=[_HBM] * n, out_specs=[_HBM] * n, out_shape=shapes,
        scratch_shapes=[pltpu.SemaphoreType.DMA((n, 3)), pltpu.SemaphoreType.DMA((n, 3)), pltpu.SemaphoreType.DMA((n,))],
    )(*arrs)


_SEM = pl.BlockSpec(memory_space=pltpu.SEMAPHORE)
_ANY = pl.BlockSpec(memory_space=pl.ANY)
_EFFECT = pltpu.SideEffectType.DATAFLOW_SIDE_EFFECTING


def _split_copies(src_ref, land_ref, send_sems, recv_sems, gather, sending):
    x, y, c = _place()
    me = 2 * x + y
    copies = []
    for j, (px, py) in enumerate(_other_chips(x, y)):
        peer = 2 * px + py
        copies.append(pltpu.make_async_remote_copy(
            src_ref=src_ref if gather else src_ref.at[peer], dst_ref=land_ref.at[me if sending else peer],
            send_sem=send_sems.at[j], recv_sem=recv_sems.at[j], device_id=(px, py, c), device_id_type=MESH))
    return copies


def _own_slot(share):
    chip = 2 * lax.axis_index("x") + lax.axis_index("y")
    return lax.dynamic_update_slice(lax.empty((N_CHIPS,) + share.shape, share.dtype), share[None], (chip, 0, 0))


def _own_part(parts):
    chip = 2 * lax.axis_index("x") + lax.axis_index("y")
    own = lax.dynamic_index_in_dim(parts, chip, 0, keepdims=True)
    return lax.dynamic_update_slice(lax.empty(parts.shape, parts.dtype), own, (chip, 0, 0))


def _exchange_start(src, land, after, name, gather):
    def body(src_ref, land_ref, after_ref, send_sems, recv_sems, src_thru, land_thru, token):
        for cp in _split_copies(src_ref, land_ref, send_sems, recv_sems, gather, sending=True):
            cp.start()
        token[...] = jnp.zeros_like(token)

    hbm = lambda t: pltpu.with_memory_space_constraint(t, pltpu.HBM)
    return pl.pallas_call(
        body, name=name,
        out_shape=(pltpu.SemaphoreType.DMA((3,)), pltpu.SemaphoreType.DMA((3,)), pltpu.HBM(src.shape, src.dtype),
                   pltpu.HBM(land.shape, land.dtype), jax.ShapeDtypeStruct((8, LANES), F32)),
        in_specs=(_HBM, _HBM, _ANY), out_specs=(_SEM, _SEM, _HBM, _HBM, pl.BlockSpec(memory_space=pltpu.VMEM)),
        input_output_aliases={0: 2, 1: 3},
        compiler_params=pltpu.CompilerParams(has_side_effects=_EFFECT),
    )(hbm(src), hbm(land), after)


def _exchange_wait(started, after, name, gather):
    send_sems, recv_sems, src_thru, land_thru, _ = started

    def body(src_ref, land_ref, send_sems, recv_sems, after_ref, src_dead, got_ref):
        for cp in _split_copies(src_ref, land_ref, send_sems, recv_sems, gather, sending=False):
            cp.wait_send()
            cp.wait_recv()

    return pl.pallas_call(
        body, name=name,
        out_shape=(pltpu.HBM(src_thru.shape, src_thru.dtype), pltpu.HBM(land_thru.shape, land_thru.dtype)),
        in_specs=(_HBM, _HBM, _SEM, _SEM, _ANY), out_specs=(_HBM, _HBM), input_output_aliases={0: 0, 1: 1},
        compiler_params=pltpu.CompilerParams(has_side_effects=_EFFECT),
    )(src_thru, land_thru, send_sems, recv_sems, after)[1]


def _swap_sibling(arrs, name):
    n = len(arrs)

    def body(*refs):
        ins, outs = refs[:n], refs[n:2 * n]
        send_sems, recv_sems = refs[2 * n:]
        x, y, c = _place()
        copies = [pltpu.make_async_remote_copy(src_ref=ins[k], dst_ref=outs[k], send_sem=send_sems.at[k],
                                               recv_sem=recv_sems.at[k], device_id=(x, y, 1 - c), device_id_type=MESH)
                  for k in range(n)]
        for cp in copies:
            cp.start()
        for cp in copies:
            cp.wait()

    return pl.pallas_call(
        body, name=name, in_specs=[_HBM] * n, out_specs=[_HBM] * n,
        out_shape=[jax.ShapeDtypeStruct(a.shape, a.dtype) for a in arrs],
        scratch_shapes=[pltpu.SemaphoreType.DMA((n,)), pltpu.SemaphoreType.DMA((n,))],
    )(*arrs)


def _all_reduce_small(v):
    rows = v.shape[0]
    flips = [(a, b, cc) for a in (0, 1) for b in (0, 1) for cc in (0, 1)][1:]

    def body(v_ref, out_ref, buf_ref, send_sems, recv_sems):
        x, y, c = _place()
        me = 4 * x + 2 * y + c
        peers = [((1 - x) if a else x, (1 - y) if b else y, (1 - c) if cc else c) for a, b, cc in flips]

        def copy(j, landing):
            return pltpu.make_async_remote_copy(src_ref=v_ref, dst_ref=buf_ref.at[landing], send_sem=send_sems.at[j],
                                                recv_sem=recv_sems.at[j], device_id=peers[j], device_id_type=MESH)

        sends = [copy(j, me) for j in range(N_DEV - 1)]
        for cp in sends:
            cp.start()
        buf_ref[me] = v_ref[...]
        for j, (px, py, pc) in enumerate(peers):
            copy(j, 4 * px + 2 * py + pc).wait_recv()
        for cp in sends:
            cp.wait_send()
        acc = buf_ref[0]
        for d in range(1, N_DEV):
            acc = acc + buf_ref[d]
        out_ref[...] = acc

    vmem = pl.BlockSpec(memory_space=pltpu.VMEM)
    return pl.pallas_call(
        body, name="all_reduce_small", in_specs=[vmem], out_specs=vmem,
        out_shape=jax.ShapeDtypeStruct(v.shape, F32),
        scratch_shapes=[pltpu.VMEM((N_DEV, rows, LANES), F32), pltpu.SemaphoreType.DMA((N_DEV - 1,)),
                        pltpu.SemaphoreType.DMA((N_DEV - 1,))],
    )(v)


def _row_block(*sizes):
    return next(t for t in (256, 192, 128, 64) if all(s % t == 0 for s in sizes))


def _sum_chips(parts, name):
    _, rows, cols = parts[0].shape
    n = len(parts)
    tr = _row_block(rows)

    def body(*refs):
        o_ref = refs[n]
        for l in range(n):
            @pl.when(pl.program_id(0) == l)
            def _(p_ref=refs[l]):
                acc = p_ref[0].astype(F32)
                for s in range(1, N_CHIPS):
                    acc = acc + p_ref[s].astype(F32)
                o_ref[0] = acc

    return pl.pallas_call(
        body, name=name, grid=(n, rows // tr),
        in_specs=[pl.BlockSpec((N_CHIPS, tr, cols), lambda l, i, k=k: (0, jnp.where(l == k, i, 0), 0)) for k in range(n)],
        out_specs=pl.BlockSpec((1, tr, cols), lambda l, i: (l, i, 0)),
        out_shape=jax.ShapeDtypeStruct((n, rows, cols), F32),
        compiler_params=_params(("arbitrary", "arbitrary")),
    )(*parts)


def _adam_update(w, m, v, g):
    r1 = 1.0 / (1.0 - ADAM_B1 ** ADAM_STEP)
    r2 = 1.0 / (1.0 - ADAM_B2 ** ADAM_STEP)
    m_new = ADAM_B1 * m + (1.0 - ADAM_B1) * g
    v_new = ADAM_B2 * v + (1.0 - ADAM_B2) * (g * g)
    return -ADAM_LR * ((m_new * r1) / (jnp.sqrt(v_new * r2) + ADAM_EPS) + ADAM_WD * w), m_new, v_new


def _adamw_rows(w, m, v, g_parts, first, name):
    n_layers, rows, cols = w.shape
    tr = _row_block(rows, first)
    n = len(g_parts)

    def body(*refs):
        w_ref, m_ref, v_ref = refs[:3]
        g_out, d_out, m_out, v_out = refs[3 + n:]
        g = refs[3][...]
        for r in refs[4:3 + n]:
            g = g + r[...]
        g = g[:, :, :cols]
        d_out[...], m_out[...], v_out[...] = _adam_update(w_ref[...], m_ref[...], v_ref[...], g)
        g_out[...] = g

    blk = pl.BlockSpec((1, tr, cols), lambda l, i: (l, i, 0))
    g_blk = pl.BlockSpec((1, tr, g_parts[0].shape[2]), lambda l, i: (l, first // tr + i, 0))
    return pl.pallas_call(
        body, name=name, grid=(n_layers, rows // tr),
        in_specs=[blk] * 3 + [g_blk] * n, out_specs=[blk] * 4,
        out_shape=[jax.ShapeDtypeStruct(w.shape, F32)] * 4,
        compiler_params=_params(("parallel", "parallel")),
    )(w, m, v, *g_parts)


def _adamw(w, m, v, g_parts, name):
    rows, cols = w.shape
    tr = min(rows, 256)
    n = len(g_parts)

    def body(*refs):
        w_ref, m_ref, v_ref = refs[:3]
        g_refs = refs[3:3 + n]
        g_out, d_out, m_out, v_out = refs[3 + n:]
        g = g_refs[0][...]
        for r in g_refs[1:]:
            g = g + r[...]
        d_out[...], m_out[...], v_out[...] = _adam_update(w_ref[...], m_ref[...], v_ref[...], g)
        g_out[...] = g

    blk = pl.BlockSpec((tr, cols), lambda i: (i, 0))
    return pl.pallas_call(
        body, name=name, grid=(rows // tr,),
        in_specs=[blk] * (3 + n), out_specs=[blk] * 4,
        out_shape=[jax.ShapeDtypeStruct((rows, cols), F32)] * 4,
        compiler_params=_params(("parallel",)),
    )(w, m, v, *g_parts)


def _pack(parts, rows, fill=0.0):
    flat = jnp.concatenate([p.reshape(-1) for p in parts])
    return jnp.pad(flat, (0, rows * LANES - flat.shape[0]), constant_values=fill).reshape(rows, LANES)


def _unpack(packed, shapes):
    flat = packed.reshape(-1)
    out, at = [], 0
    for shp in shapes:
        size = 1
        for s in shp:
            size *= s
        out.append(flat[at:at + size].reshape(shp))
        at += size
    return out


def _packed_rows(shapes):
    total = 0
    for shp in shapes:
        size = 1
        for s in shp:
            size *= s
        total += size
    return -(-total // (8 * LANES)) * 8


def _cols_full(g, l):
    t = g[:, l]
    return jnp.moveaxis(t, 0, 1).reshape(t.shape[1], N_CHIPS * t.shape[2])


def _pad_cols(t):
    return jnp.pad(t, ((0, 0),) * (t.ndim - 1) + ((0, D_MODEL - t.shape[-1]),))


def kernel(x, norm1_g, w_in, dn_conv_w, dn_a_log, dn_dt_bias, dn_norm_g, sc_conv_w, sc_norm_g, w_out, norm2_g, ffn_w_gate, ffn_w_up, ffn_w_down, final_norm_g, loss_target, m_norm1_g, m_w_in, m_dn_conv_w, m_dn_a_log, m_dn_dt_bias, m_dn_norm_g, m_sc_conv_w, m_sc_norm_g, m_w_out, m_norm2_g, m_ffn_w_gate, m_ffn_w_up, m_ffn_w_down, m_final_norm_g, v_norm1_g, v_w_in, v_dn_conv_w, v_dn_a_log, v_dn_dt_bias, v_dn_norm_g, v_sc_conv_w, v_sc_norm_g, v_w_out, v_norm2_g, v_ffn_w_gate, v_ffn_w_up, v_ffn_w_down, v_final_norm_g):
    chip = 2 * lax.axis_index("x") + lax.axis_index("y")

    g_cw, g_scw = _chip_exchange([dn_conv_w, sc_conv_w], "gather_conv", gather=True)

    t_last = lambda t: jnp.swapaxes(t, -1, -2)
    gate_t, up_t = t_last(ffn_w_gate), t_last(ffn_w_up)
    zero_token = jnp.zeros((8, LANES), F32)

    def shares(l, tie):
        share_a = jnp.concatenate([_pad_cols(w_in[l] + tie), w_out[l]], axis=0).astype(BF16)
        share_b = jnp.concatenate([gate_t[l] + tie, up_t[l], ffn_w_down[l]], axis=0).astype(BF16)
        return share_a, _own_slot(share_a), share_b, _own_slot(share_b)

    def gather_start(l, packed, after):
        a = _exchange_start(packed[0], packed[1], after, "gather_a_start_%d" % l, gather=True)
        b = _exchange_start(packed[2], packed[3], a[4], "gather_b_start_%d" % l, gather=True)
        return a, b

    ga, gb = gather_start(0, shares(0, 0.0), g_cw)
    packed = [None] + [shares(l, gb[4][0, 0]) for l in range(1, DEPTH)]
    packed_all = sum(t[0, 0].astype(F32) for p in packed[1:] for t in (p[0], p[2]))
    land_a = _exchange_wait(ga, zero_token + packed_all, "gather_a_wait_0", gather=True)
    act = x[0]
    layers, saved_m, saved_f, lands_b = [], [], [], []
    for l in range(DEPTH):
        hold = 0.0
        if l + 1 < DEPTH:
            ga, gb_next = gather_start(l + 1, packed[l + 1], land_a)
            hold = gb_next[4][0:1, 0:1]
        al, dt = _gate_rows(dn_a_log[l], dn_dt_bias[l])
        layers.append(dict(
            g1=norm1_g[l][None] + hold, cw=_pad_rows(_cols_full(g_cw, l)), al=al, dt=dt,
            gn=dn_norm_g[l][None], scw=_pad_rows(_cols_full(g_scw, l)), gs=sc_norm_g[l][None],
            land_a=land_a, g2=norm2_g[l][None]))
        x1, s = _mixer_fwd(act, layers[l])
        saved_m.append(s)
        lands_b.append(_exchange_wait(gb, x1, "gather_b_wait_%d" % l, gather=True))
        act, s = _ffn_fwd(x1, layers[l], lands_b[l])
        saved_f.append(s)
        if l + 1 < DEPTH:
            land_a = _exchange_wait(ga, act, "gather_a_wait_%d" % (l + 1), gather=True)
            gb = gb_next

    dact, dact_bf16, loss_part, d_final = _loss_head(act, final_norm_g[None], loss_target[0])
    grads, reduce_a, reduce_b = [None] * DEPTH, [None] * DEPTH, [None] * DEPTH
    hold = 0.0
    for l in reversed(range(DEPTH)):
        p = layers[l]
        dx1, dx1_bf16, parts, dg2 = _ffn_back(dact, dact_bf16, saved_f[l], dict(p, g2=p["g2"] + hold), lands_b[l])
        reduce_b[l] = _exchange_start(parts, _own_part(parts), zero_token, "reduce_b_start_%d" % l, gather=False)
        dact, dact_bf16, parts, gm = _mixer_bwd(dx1, dx1_bf16, saved_m[l], dict(p, gn=p["gn"] + reduce_b[l][4][0:1, 0:1]))
        reduce_a[l] = _exchange_start(parts, _own_part(parts), zero_token, "reduce_a_start_%d" % l, gather=False)
        hold = reduce_a[l][4][0:1, 0:1]
        grads[l] = dict(gm, g2=dg2)
    loss = lax.psum(loss_part[0, 0], ("x", "y", "c"))
    stack = lambda key: jnp.stack([grads[l][key] for l in range(DEPTH)])

    got_b = [_exchange_wait(reduce_b[l], reduce_a[0][4], "reduce_b_wait_%d" % l, gather=False)
             for l in reversed(range(DEPTH))][::-1]
    sum_b = _sum_chips(got_b, "sum_chips_b")
    other_b, = _swap_sibling([sum_b], "swap_sibling_b")
    big = dict(
        ffn_w_gate=[t_last(o) for o in _adamw_rows(gate_t, t_last(m_ffn_w_gate), t_last(v_ffn_w_gate),
                                                   [sum_b, other_b], 0, "adamw_gate")],
        ffn_w_up=[t_last(o) for o in _adamw_rows(up_t, t_last(m_ffn_w_up), t_last(v_ffn_w_up),
                                                 [sum_b, other_b], FF_SHARD, "adamw_up")],
        ffn_w_down=_adamw_rows(ffn_w_down, m_ffn_w_down, v_ffn_w_down, [sum_b, other_b], 2 * FF_SHARD, "adamw_down"))
    after_b = big["ffn_w_down"][1]
    got_a = [_exchange_wait(reduce_a[l], after_b, "reduce_a_wait_%d" % l, gather=False) for l in reversed(range(DEPTH))][::-1]
    sum_a = _sum_chips(got_a, "sum_chips_a")
    other_a, = _swap_sibling([sum_a], "swap_sibling_a")
    big.update(
        w_in=_adamw_rows(w_in, m_w_in, v_w_in, [sum_a, other_a], 0, "adamw_w_in"),
        w_out=_adamw_rows(w_out, m_w_out, v_w_out, [sum_a, other_a], A_OUT_AT, "adamw_w_out"))

    full_shapes = [(DEPTH, D_MODEL), (DEPTH, D_MODEL), (DEPTH, HEAD_DIM), (DEPTH, SC_WIDTH), (DEPTH, HEADS),
                   (DEPTH, HEADS), (D_MODEL,), (DEPTH, 4, QKV), (DEPTH, 3, SC_WIDTH)]
    small_keys = ("g1", "g2", "gn", "gs", "al", "dt")
    packed = _pack([stack(k) for k in small_keys] + [d_final[0], stack("cw"), stack("scw")], _packed_rows(full_shapes))
    sg = _unpack(_all_reduce_small(packed), full_shapes)
    sg[7] = lax.dynamic_slice_in_dim(sg[7], chip * (QKV // N_CHIPS), QKV // N_CHIPS, axis=2)
    sg[8] = lax.dynamic_slice_in_dim(sg[8], chip * (SC_WIDTH // N_CHIPS), SC_WIDTH // N_CHIPS, axis=2)
    small_names = ("norm1_g", "norm2_g", "dn_norm_g", "sc_norm_g", "dn_a_log", "dn_dt_bias", "final_norm_g",
                   "dn_conv_w", "sc_conv_w")
    sw = (norm1_g, norm2_g, dn_norm_g, sc_norm_g, dn_a_log, dn_dt_bias, final_norm_g, dn_conv_w, sc_conv_w)
    sm = (m_norm1_g, m_norm2_g, m_dn_norm_g, m_sc_norm_g, m_dn_a_log, m_dn_dt_bias, m_final_norm_g, m_dn_conv_w, m_sc_conv_w)
    sv = (v_norm1_g, v_norm2_g, v_dn_norm_g, v_sc_norm_g, v_dn_a_log, v_dn_dt_bias, v_final_norm_g, v_dn_conv_w, v_sc_conv_w)
    shard_shapes = [t.shape for t in sw]
    rows = _packed_rows(shard_shapes)
    outs = _adamw(_pack(sw, rows), _pack(sm, rows), _pack(sv, rows, fill=1.0), [_pack(sg, rows)], "adamw_small")
    small = {name: [] for name in small_names}
    for o in outs:
        for name, t in zip(small_names, _unpack(o, shard_shapes)):
            small[name].append(t)

    order = ("norm1_g", "w_in", "dn_conv_w", "dn_a_log", "dn_dt_bias", "dn_norm_g", "sc_conv_w", "sc_norm_g", "w_out",
             "norm2_g", "ffn_w_gate", "ffn_w_up", "ffn_w_down", "final_norm_g")
    result = {**big, **small}
    return (loss, dact[None], *[result[n][0] for n in order], *[result[n][1] for n in order],
            *[result[n][2] for n in order], *[result[n][3] for n in order])
```

```python
import jax
import jax.numpy as jnp
from jax import lax
from jax.experimental import pallas as pl
from jax.experimental.pallas import tpu as pltpu

F32 = jnp.float32
BF16 = jnp.bfloat16
MESH = pl.DeviceIdType.MESH

D_MODEL = 1024
DEPTH = 4
HEADS = 4
HEAD_DIM = 128
DN_WIDTH = HEADS * HEAD_DIM
SC_WIDTH = 512
SC_GROUPS = 4
D_FF = 2816
CHUNK = 64
QKV = 3 * DN_WIDTH
W_IN_COLS = 4 * DN_WIDTH + 2 * HEADS + 3 * SC_WIDTH
WA_COLS = QKV + DN_WIDTH + 3 * SC_WIDTH
LANES = 128
EPS = 1e-6
Q_SCALE = HEAD_DIM ** -0.5
N_CHIPS = 4
N_DEV = 8
IN_SHARD = W_IN_COLS // N_CHIPS
OUT_SHARD = D_MODEL // N_CHIPS
FF_SHARD = D_FF // N_CHIPS
A_OUT_AT = D_MODEL
A_ROWS = D_MODEL + OUT_SHARD
B_ROWS = 3 * FF_SHARD

ADAM_LR = 0.001
ADAM_B1 = 0.9
ADAM_B2 = 0.999
ADAM_EPS = 1e-08
ADAM_WD = 0.01
ADAM_STEP = 10

VMEM_LIMIT = 56 * 1024 * 1024

NN = (((1,), (0,)), ((), ()))
NT = (((1,), (1,)), ((), ()))
TN = (((0,), (0,)), ((), ()))


def _mm(a, b, dims=NN):
    return lax.dot_general(a.astype(BF16), b.astype(BF16), dims, preferred_element_type=F32)


def _mm32(a, b, dims=NN):
    return lax.dot_general(a, b, dims, preferred_element_type=F32, precision=lax.Precision.HIGHEST)


def _params(sem, vmem=VMEM_LIMIT):
    return pltpu.CompilerParams(dimension_semantics=sem, vmem_limit_bytes=vmem)


def _sigmoid(x):
    return 0.5 * jnp.tanh(0.5 * x) + 0.5


def _softplus(x):
    return jnp.maximum(x, 0.0) + jnp.log1p(jnp.exp(-jnp.abs(x)))


def _row_acc(acc_ref, val):
    acc_ref[0:1, :] += jnp.sum(val, axis=0, keepdims=True)


def _rms_bwd(dh, xh, r, gain):
    dxh = dh * gain
    return r * (dxh - xh * jnp.mean(dxh * xh, axis=-1, keepdims=True))


def _before_halo(tb):
    return lambda i: (jnp.maximum(i * (tb // 8) - 1, 0), 0)


def _after_halo(tb, n_rows):
    last = n_rows // 8 - 1
    return lambda i: (jnp.minimum((i + 1) * (tb // 8), last), 0)


def _rows_from(xc, offset, tb):
    part = offset % 8
    if part:
        xc = pltpu.roll(xc, xc.shape[0] - part, 0)
    return xc[offset - part:offset - part + tb, :]


def _taps(xc, w, n_taps, tb, first):
    out = w[0:1, :] * _rows_from(xc, first, tb)
    for j in range(1, n_taps):
        out = out + w[j:j + 1, :] * _rows_from(xc, first + j, tb)
    return out


W_Z = QKV
W_BD = W_Z + DN_WIDTH
W_SC = W_BD + 2 * HEADS
P_SC = QKV
P_Z = P_SC + 3 * SC_WIDTH
P_BD = P_Z + DN_WIDTH
P_COLS = P_BD + LANES


def _w_in_cols(shards, lo, hi):
    pieces = []
    for s in range(N_CHIPS):
        a, b = max(lo, IN_SHARD * s), min(hi, IN_SHARD * (s + 1))
        if a < b:
            pieces.append(shards[s][:, a - IN_SHARD * s:b - IN_SHARD * s])
    return pieces[0] if len(pieces) == 1 else jnp.concatenate(pieces, axis=1)


def _in_proj(x, g1, land_a):
    T = x.shape[0]
    tb = 256

    def body(x_ref, g_ref, w_ref, qkv_ref, z_ref, sc_ref, bd_ref, h_ref):
        xv = x_ref[...]
        r = lax.rsqrt(jnp.mean(xv * xv, axis=-1, keepdims=True) + EPS)
        h = (xv * r * g_ref[...]).astype(BF16)
        shards = [jnp.dot(h, w_ref[s], preferred_element_type=F32) for s in range(N_CHIPS)]
        qkv_ref[...] = _w_in_cols(shards, 0, W_Z)
        z_ref[...] = _w_in_cols(shards, W_Z, W_BD)
        bd_ref[...] = jnp.concatenate([_w_in_cols(shards, W_BD, W_SC), jnp.zeros((tb, LANES - 2 * HEADS), F32)], axis=1)
        sc_ref[...] = _w_in_cols(shards, W_SC, W_IN_COLS)
        h_ref[...] = h

    tok = lambda w: pl.BlockSpec((tb, w), lambda i: (i, 0))
    return pl.pallas_call(
        body, name="in_proj", grid=(T // tb,),
        in_specs=[tok(D_MODEL), pl.BlockSpec(g1.shape, lambda i: (0, 0)), _shard_rows(land_a, 0, D_MODEL)],
        out_specs=[tok(QKV), tok(DN_WIDTH), tok(3 * SC_WIDTH), tok(LANES), tok(D_MODEL)],
        out_shape=[jax.ShapeDtypeStruct((T, QKV), F32), jax.ShapeDtypeStruct((T, DN_WIDTH), F32),
                   jax.ShapeDtypeStruct((T, 3 * SC_WIDTH), F32), jax.ShapeDtypeStruct((T, LANES), F32),
                   jax.ShapeDtypeStruct((T, D_MODEL), BF16)],
        compiler_params=_params(("parallel",)),
    )(x, g1, land_a)


def _dp_block(tb, first, width, index=lambda i: i):
    assert first % width == 0
    return pl.BlockSpec((tb, width), lambda i: (index(i), first // width))


def _dn_act(pre, halo, cw, tb):
    xc = jnp.concatenate([halo, pre], axis=0)
    c = _taps(xc, cw, 4, tb, 5)
    sg = _sigmoid(c)
    return xc, c, sg, c * sg


def _gates(bd, al_row, dt_row):
    lane = lax.broadcasted_iota(jnp.int32, bd.shape, 1)
    beta = _sigmoid(bd)
    g = -jnp.exp(al_row) * _softplus(bd + dt_row)
    return jnp.where(lane < HEADS, beta, jnp.where(lane < 2 * HEADS, g, 0.0))


def _dn_prep(qkv, cw, bd, al_row, dt_row):
    T = qkv.shape[0]
    tb = 512

    def body(pre_ref, halo_ref, cw_ref, bd_ref, al_ref, dt_ref, q_ref, k_ref, v_ref, bg_ref):
        halo = jnp.where(pl.program_id(0) > 0, halo_ref[...], 0.0)
        _, _, _, a = _dn_act(pre_ref[...], halo, cw_ref[...], tb)
        for hh in range(HEADS):
            sl = slice(HEAD_DIM * hh, HEAD_DIM * (hh + 1))
            qs = a[:, sl]
            q_ref[:, sl] = qs * (lax.rsqrt(jnp.sum(qs * qs, axis=-1, keepdims=True) + EPS) * Q_SCALE)
            ks = a[:, DN_WIDTH + HEAD_DIM * hh:DN_WIDTH + HEAD_DIM * (hh + 1)]
            k_ref[:, sl] = ks * lax.rsqrt(jnp.sum(ks * ks, axis=-1, keepdims=True) + EPS)
        v_ref[...] = a[:, 2 * DN_WIDTH:]
        gates = _gates(bd_ref[...], al_ref[...], dt_ref[...])
        lane = lax.broadcasted_iota(jnp.int32, gates.shape, 1)
        bg_ref[...] = jnp.where(lane < HEADS, gates, _mm32(_chunk_cumsum_matrix(tb), gates))

    tok = lambda w: pl.BlockSpec((tb, w), lambda i: (i, 0))
    full = lambda a: pl.BlockSpec(a.shape, lambda i: (0, 0))
    return pl.pallas_call(
        body, name="dn_prep", grid=(T // tb,),
        in_specs=[tok(QKV), pl.BlockSpec((8, QKV), _before_halo(tb)), full(cw), tok(LANES), full(al_row), full(dt_row)],
        out_specs=[tok(DN_WIDTH), tok(DN_WIDTH), tok(DN_WIDTH), tok(LANES)],
        out_shape=[jax.ShapeDtypeStruct((T, DN_WIDTH), F32)] * 3 + [jax.ShapeDtypeStruct((T, LANES), F32)],
        compiler_params=_params(("parallel",)),
    )(qkv, qkv, cw, bd, al_row, dt_row)


def _chunk_masks():
    row = lax.broadcasted_iota(jnp.int32, (CHUNK, CHUNK), 0)
    col = lax.broadcasted_iota(jnp.int32, (CHUNK, CHUNK), 1)
    return row >= col, row > col


def _chunk_cumsum_matrix(n):
    row = lax.broadcasted_iota(jnp.int32, (n, n), 0)
    col = lax.broadcasted_iota(jnp.int32, (n, n), 1)
    return jnp.logical_and(row >= col, row // CHUNK == col // CHUNK).astype(F32)


def _chunk_units(q_ref, k_ref, v_ref, bg_ref, rows):
    bgc = bg_ref[rows, :]
    bg_t = bgc.T
    qv, kv, vv = q_ref[rows, :], k_ref[rows, :], v_ref[rows, :]
    units = []
    for h in range(HEADS):
        sl = slice(HEAD_DIM * h, HEAD_DIM * (h + 1))
        units.append((qv[:, sl], kv[:, sl], vv[:, sl], bgc[:, h:h + 1], bgc[:, HEADS + h:HEADS + h + 1],
                      bg_t[HEADS + h:HEADS + h + 1, :]))
    return units


def _units_local(units, masks):
    causal, strict = masks
    pre = []
    for q, k, v, beta, gc, gr in units:
        kb = k * beta
        eg = jnp.exp(gc)
        g_last = gc[CHUNK - 1:CHUNK, :]
        ek = jnp.exp(g_last - gc)
        pre.append(dict(q=q, k=k, v=v, beta=beta, decay=jnp.exp(jnp.where(causal, gc - gr, -1e30)), kb=kb, vb=v * beta,
                        eg=eg, kbg=kb * eg, ek=ek, gl=jnp.exp(g_last), q_dec=q * eg, k_dec=k * ek))
    both = [_mm(jnp.concatenate([p["kb"], p["q"]], axis=0), p["k"], NT) for p in pre]
    for p, b in zip(pre, both):
        p["low"] = jnp.where(strict, b[:CHUNK] * p["decay"], 0.0)
        p["qk"] = jnp.where(causal, b[CHUNK:] * p["decay"], 0.0)
    xs = [-p["low"] for p in pre]
    pw = [_mm(p["low"], p["low"]) for p in pre]
    for _ in range(4):
        both = [_mm(jnp.concatenate([pp, x], axis=0), pp) for pp, x in zip(pw, xs)]
        xs = [x + pp + b[CHUNK:] for x, pp, b in zip(xs, pw, both)]
        pw = [b[:CHUNK] for b in both]
    last = [_mm(x, pp) for x, pp in zip(xs, pw)]
    xs = [x + pp + b for x, pp, b in zip(xs, pw, last)]
    uw = [_mm(x, jnp.concatenate([p["vb"], p["kbg"]], axis=1)) for x, p in zip(xs, pre)]
    for p, x, b in zip(pre, xs, uw):
        p["xm"] = x
        p["u"] = p["vb"] + b[:, :HEAD_DIM]
        p["w"] = p["kbg"] + b[:, HEAD_DIM:]
    return pre


def _delta_fwd(q, k, v, bg):
    T = q.shape[0]
    tb = 512
    n_chunk = tb // CHUNK

    def body(q_ref, k_ref, v_ref, bg_ref, o_ref, st_ref, s_ref):
        @pl.when(pl.program_id(0) == 0)
        def _():
            s_ref[...] = jnp.zeros_like(s_ref)

        masks = _chunk_masks()

        def pair(pi, carry):
            rows = [pl.ds(pl.multiple_of((2 * pi + j) * CHUNK, CHUNK), CHUNK) for j in range(2)]
            loc = _units_local(_chunk_units(q_ref, k_ref, v_ref, bg_ref, rows[0])
                               + _chunk_units(q_ref, k_ref, v_ref, bg_ref, rows[1]), masks)
            states = [s_ref[h] for h in range(HEADS)]
            for j in range(2):
                lj = loc[HEADS * j:HEADS * (j + 1)]
                ws = [_mm(jnp.concatenate([p["w"], p["q_dec"]], axis=0), s) for p, s in zip(lj, states)]
                v_new = [p["u"] - b[:CHUNK] for p, b in zip(lj, ws)]
                intra = [_mm(p["qk"], vn) for p, vn in zip(lj, v_new)]
                upd = [_mm(p["k_dec"], vn, TN) for p, vn in zip(lj, v_new)]
                o_ref[rows[j], :] = jnp.concatenate([b[CHUNK:] + a for b, a in zip(ws, intra)], axis=1)
                for h in range(HEADS):
                    st_ref[2 * pi + j, h] = states[h]
                states = [p["gl"] * s + d for p, s, d in zip(lj, states, upd)]
            for h in range(HEADS):
                s_ref[h] = states[h]
            return carry

        lax.fori_loop(0, n_chunk // 2, pair, 0)

    tok = lambda w: pl.BlockSpec((tb, w), lambda i: (i, 0))
    return pl.pallas_call(
        body, name="delta_fwd", grid=(T // tb,),
        in_specs=[tok(DN_WIDTH), tok(DN_WIDTH), tok(DN_WIDTH), tok(LANES)],
        out_specs=[tok(DN_WIDTH), pl.BlockSpec((n_chunk, HEADS, HEAD_DIM, HEAD_DIM), lambda i: (i, 0, 0, 0))],
        out_shape=[jax.ShapeDtypeStruct((T, DN_WIDTH), F32),
                   jax.ShapeDtypeStruct((T // CHUNK, HEADS, HEAD_DIM, HEAD_DIM), F32)],
        scratch_shapes=[pltpu.VMEM((HEADS, HEAD_DIM, HEAD_DIM), F32)],
        compiler_params=_params(("arbitrary",)),
    )(q, k, v, bg)


def _dn_out(o, z, gn):
    outs, ohs, rs = [], [], []
    for hh in range(HEADS):
        oh = o[:, HEAD_DIM * hh:HEAD_DIM * (hh + 1)]
        r = lax.rsqrt(jnp.mean(oh * oh, axis=-1, keepdims=True) + EPS)
        ohs.append(oh * r)
        rs.append(r)
    sz = _sigmoid(z)
    oh = jnp.concatenate(ohs, axis=1)
    gn4 = jnp.concatenate([gn] * HEADS, axis=1)
    return oh * gn4 * (z * sz), oh, rs, sz, gn4


def _sc_fwd(sc_in, halo, cw, tb):
    xc = jnp.concatenate([halo, sc_in], axis=0)
    u = xc[:, SC_WIDTH:2 * SC_WIDTH] * xc[:, 2 * SC_WIDTH:]
    cv = _taps(u, cw, 3, tb, 6)
    gate_b = sc_in[:, :SC_WIDTH]
    y = gate_b * cv
    gw = SC_WIDTH // SC_GROUPS
    yhs, rs = [], []
    for gi in range(SC_GROUPS):
        yg = y[:, gw * gi:gw * (gi + 1)]
        r = lax.rsqrt(jnp.mean(yg * yg, axis=-1, keepdims=True) + EPS)
        yhs.append(yg * r)
        rs.append(r)
    return u, cv, gate_b, jnp.concatenate(yhs, axis=1), rs


def _shard_rows(land, first, rows):
    assert first % rows == 0 and land.shape[0] == N_CHIPS
    return pl.BlockSpec((N_CHIPS, rows, land.shape[2]), lambda i: (0, first // rows, 0))


def _whole(w_ref):
    n, rows, cols = w_ref.shape
    return w_ref[...].reshape(n * rows, cols)


def _mix_out(o, z, sc_in, x, land_a, gn, scw, gs):
    T = x.shape[0]
    tb = 256

    def body(o_ref, z_ref, sc_ref, halo_ref, x_ref, w_ref, gn_ref, scw_ref, gs_ref, x1_ref, mix_ref):
        o_n = _dn_out(o_ref[...], z_ref[...], gn_ref[...])[0]
        halo = jnp.where(pl.program_id(0) > 0, halo_ref[...], 0.0)
        yh = _sc_fwd(sc_ref[...], halo, scw_ref[...], tb)[3]
        mix = jnp.concatenate([o_n, yh * gs_ref[...]], axis=1).astype(BF16)
        x1_ref[...] = x_ref[...] + jnp.dot(mix, _whole(w_ref), preferred_element_type=F32)
        mix_ref[...] = mix

    tok = lambda w: pl.BlockSpec((tb, w), lambda i: (i, 0))
    full = lambda a: pl.BlockSpec(a.shape, lambda i: (0, 0))
    return pl.pallas_call(
        body, name="mix_out", grid=(T // tb,),
        in_specs=[tok(DN_WIDTH), tok(DN_WIDTH), tok(3 * SC_WIDTH), pl.BlockSpec((8, 3 * SC_WIDTH), _before_halo(tb)),
                  tok(D_MODEL), _shard_rows(land_a, A_OUT_AT, OUT_SHARD), full(gn), full(scw), full(gs)],
        out_specs=[tok(D_MODEL), tok(D_MODEL)],
        out_shape=[jax.ShapeDtypeStruct((T, D_MODEL), F32), jax.ShapeDtypeStruct((T, D_MODEL), BF16)],
        compiler_params=_params(("parallel",)),
    )(o, z, sc_in, sc_in, x, land_a, gn, scw, gs)


def _ffn(x1, g2, land_b):
    T = x1.shape[0]
    tb = 256

    def body(x_ref, g_ref, wgt_ref, wut_ref, wd_ref, x2_ref, a_ref, b_ref, h_ref):
        xv = x_ref[...]
        r = lax.rsqrt(jnp.mean(xv * xv, axis=-1, keepdims=True) + EPS)
        h = (xv * r * g_ref[...]).astype(BF16)
        a = lax.dot_general(h, _whole(wgt_ref), NT, preferred_element_type=F32)
        b = lax.dot_general(h, _whole(wut_ref), NT, preferred_element_type=F32)
        act = (a * _sigmoid(a) * b).astype(BF16)
        x2_ref[...] = xv + jnp.dot(act, _whole(wd_ref), preferred_element_type=F32)
        a_ref[...] = a.astype(BF16)
        b_ref[...] = b.astype(BF16)
        h_ref[...] = h

    tok = lambda w: pl.BlockSpec((tb, w), lambda i: (i, 0))
    return pl.pallas_call(
        body, name="ffn", grid=(T // tb,),
        in_specs=[tok(D_MODEL), pl.BlockSpec(g2.shape, lambda i: (0, 0)), _shard_rows(land_b, 0, FF_SHARD),
                  _shard_rows(land_b, FF_SHARD, FF_SHARD), _shard_rows(land_b, 2 * FF_SHARD, FF_SHARD)],
        out_specs=[tok(D_MODEL), tok(D_FF), tok(D_FF), tok(D_MODEL)],
        out_shape=[jax.ShapeDtypeStruct((T, D_MODEL), F32), jax.ShapeDtypeStruct((T, D_FF), BF16),
                   jax.ShapeDtypeStruct((T, D_FF), BF16), jax.ShapeDtypeStruct((T, D_MODEL), BF16)],
        compiler_params=_params(("parallel",)),
    )(x1, g2, land_b, land_b, land_b)


def _loss_head(x, gf, target):
    T = x.shape[0]
    tb = 512

    def body(x_ref, g_ref, t_ref, dx_ref, dxb_ref, loss_ref, dg_ref):
        @pl.when(pl.program_id(0) == 0)
        def _():
            loss_ref[...] = jnp.zeros_like(loss_ref)
            dg_ref[...] = jnp.zeros_like(dg_ref)

        xv = x_ref[...]
        r = lax.rsqrt(jnp.mean(xv * xv, axis=-1, keepdims=True) + EPS)
        xh = xv * r
        err = xh * g_ref[...] - t_ref[...]
        per_tok = jnp.mean(err * err, axis=-1, keepdims=True)
        loss_ref[...] += 0.5 * jnp.sum(per_tok, axis=0, keepdims=True)
        dy = err * (1.0 / D_MODEL)
        _row_acc(dg_ref, dy * xh)
        dx = _rms_bwd(dy, xh, r, g_ref[...])
        dx_ref[...] = dx
        dxb_ref[...] = dx.astype(BF16)

    tok = pl.BlockSpec((tb, D_MODEL), lambda i: (i, 0))
    return pl.pallas_call(
        body, name="loss_head", grid=(T // tb,),
        in_specs=[tok, pl.BlockSpec(gf.shape, lambda i: (0, 0)), tok],
        out_specs=[tok, tok, pl.BlockSpec((8, LANES), lambda i: (0, 0)), pl.BlockSpec((8, D_MODEL), lambda i: (0, 0))],
        out_shape=[jax.ShapeDtypeStruct((T, D_MODEL), F32), jax.ShapeDtypeStruct((T, D_MODEL), BF16),
                   jax.ShapeDtypeStruct((8, LANES), F32), jax.ShapeDtypeStruct((8, D_MODEL), F32)],
        compiler_params=_params(("arbitrary",)),
    )(x, gf, target)


def _ffn_bwd(dx2, x1, a, b, g2, land_b):
    T = x1.shape[0]
    tb = 256

    def body(dx2_ref, x_ref, a_ref, b_ref, g_ref, wgt_ref, wut_ref, wd_ref,
             dx1_ref, dx1b_ref, da_ref, db_ref, act_ref, dg_ref):
        @pl.when(pl.program_id(0) == 0)
        def _():
            dg_ref[...] = jnp.zeros_like(dg_ref)

        dx2v = dx2_ref[...]
        av = a_ref[...].astype(F32)
        bv = b_ref[...].astype(F32)
        dact = _mm(dx2v, _whole(wd_ref), NT)
        sa = _sigmoid(av)
        silu = av * sa
        da = (dact * bv * (sa * (1.0 + av * (1.0 - sa)))).astype(BF16)
        db = (dact * silu).astype(BF16)
        dh = _mm(da, _whole(wgt_ref)) + _mm(db, _whole(wut_ref))
        xv = x_ref[...]
        r = lax.rsqrt(jnp.mean(xv * xv, axis=-1, keepdims=True) + EPS)
        xh = xv * r
        _row_acc(dg_ref, dh * xh)
        dx1 = dx2v + _rms_bwd(dh, xh, r, g_ref[...])
        dx1_ref[...] = dx1
        dx1b_ref[...] = dx1.astype(BF16)
        da_ref[...] = da
        db_ref[...] = db
        act_ref[...] = (silu * bv).astype(BF16)

    tok = lambda w: pl.BlockSpec((tb, w), lambda i: (i, 0))
    return pl.pallas_call(
        body, name="ffn_bwd", grid=(T // tb,),
        in_specs=[tok(D_MODEL), tok(D_MODEL), tok(D_FF), tok(D_FF), pl.BlockSpec(g2.shape, lambda i: (0, 0)),
                  _shard_rows(land_b, 0, FF_SHARD), _shard_rows(land_b, FF_SHARD, FF_SHARD),
                  _shard_rows(land_b, 2 * FF_SHARD, FF_SHARD)],
        out_specs=[tok(D_MODEL), tok(D_MODEL), tok(D_FF), tok(D_FF), tok(D_FF), pl.BlockSpec((8, D_MODEL), lambda i: (0, 0))],
        out_shape=[jax.ShapeDtypeStruct((T, D_MODEL), F32), jax.ShapeDtypeStruct((T, D_MODEL), BF16)]
        + [jax.ShapeDtypeStruct((T, D_FF), BF16)] * 3 + [jax.ShapeDtypeStruct((8, D_MODEL), F32)],
        compiler_params=_params(("arbitrary",)),
    )(dx2, x1, a, b, g2, land_b, land_b, land_b)


def _wgrad_share(a, b, parts, first, name):
    T = b.shape[0]
    rows = a.shape[1] // N_CHIPS
    assert first % rows == 0 and b.shape[1] == parts.shape[2]
    bk = min(T, 1024)
    n_k = T // bk
    group = 2
    assert (group * rows) % LANES == 0

    def body(a_ref, b_ref, parts_ref, o_ref, acc_ref):
        kk = pl.program_id(1)

        @pl.when(kk == 0)
        def _():
            acc_ref[...] = jnp.zeros_like(acc_ref)

        acc_ref[...] += lax.dot_general(a_ref[...], b_ref[...], TN, preferred_element_type=F32)

        @pl.when(kk == n_k - 1)
        def _():
            for s in range(group):
                o_ref[s] = acc_ref[rows * s:rows * (s + 1), :].astype(BF16)

    return pl.pallas_call(
        body, name=name, grid=(N_CHIPS // group, n_k),
        in_specs=[pl.BlockSpec((bk, group * rows), lambda i, kk: (kk, i)),
                  pl.BlockSpec((bk, b.shape[1]), lambda i, kk: (kk, 0)), _ANY],
        out_specs=pl.BlockSpec((group, rows, b.shape[1]), lambda i, kk: (i, first // rows, 0)),
        out_shape=jax.ShapeDtypeStruct(parts.shape, BF16),
        scratch_shapes=[pltpu.VMEM((group * rows, b.shape[1]), F32)],
        input_output_aliases={2: 0},
        compiler_params=_params(("parallel", "arbitrary")),
    )(a, b, parts)


def _mix_out_bwd(dx1, o, z, sc_in, land_a, gn, scw, gs, dp):
    T = dx1.shape[0]
    tb = 256

    def body(dx_ref, o_ref, z_ref, sc_ref, halo_ref, w_ref, gn_ref, scw_ref, gs_ref, dp_ref,
             do_ref, dz_ref, dgb_ref, dcv_ref, dgn_ref, dgs_ref, dscw_ref):
        @pl.when(pl.program_id(0) == 0)
        def _():
            dgn_ref[...] = jnp.zeros_like(dgn_ref)
            dgs_ref[...] = jnp.zeros_like(dgs_ref)
            dscw_ref[...] = jnp.zeros_like(dscw_ref)

        dmix = _mm(dx_ref[...], _whole(w_ref), NT)
        don = dmix[:, :DN_WIDTH]
        dosc = dmix[:, DN_WIDTH:]
        zv = z_ref[...]
        _, oh, rs, sz, gn4 = _dn_out(o_ref[...], zv, gn_ref[...])
        silu_z = zv * sz
        dgn_full = don * oh * silu_z
        dgn_ref[0:1, :] += jnp.sum(sum(dgn_full[:, HEAD_DIM * hh:HEAD_DIM * (hh + 1)] for hh in range(HEADS)),
                                   axis=0, keepdims=True)
        dz_ref[...] = (don * oh * gn4 * (sz * (1.0 + zv * (1.0 - sz)))).astype(BF16)
        t = don * gn4 * silu_z
        for hh in range(HEADS):
            sl = slice(HEAD_DIM * hh, HEAD_DIM * (hh + 1))
            th, ohh = t[:, sl], oh[:, sl]
            do_ref[:, sl] = rs[hh] * (th - ohh * jnp.mean(th * ohh, axis=-1, keepdims=True))
        halo = jnp.where(pl.program_id(0) > 0, halo_ref[...], 0.0)
        u, cv, gate_b, yh, rys = _sc_fwd(sc_ref[...], halo, scw_ref[...], tb)
        _row_acc(dgs_ref, dosc * yh)
        ty = dosc * gs_ref[...]
        gw = SC_WIDTH // SC_GROUPS
        dys = []
        for gi in range(SC_GROUPS):
            sl = slice(gw * gi, gw * (gi + 1))
            tg, yg = ty[:, sl], yh[:, sl]
            dys.append(rys[gi] * (tg - yg * jnp.mean(tg * yg, axis=-1, keepdims=True)))
        dy = jnp.concatenate(dys, axis=1)
        dgb_ref[...] = dy * cv
        dcv = dy * gate_b
        dcv_ref[...] = dcv
        for j in range(3):
            dscw_ref[j:j + 1, :] += jnp.sum(dcv * _rows_from(u, 6 + j, tb), axis=0, keepdims=True)

    tok = lambda w: pl.BlockSpec((tb, w), lambda i: (i, 0))
    full = lambda t: pl.BlockSpec(t.shape, lambda i: (0, 0))
    acc = lambda w: pl.BlockSpec((8, w), lambda i: (0, 0))
    return pl.pallas_call(
        body, name="mix_out_bwd", grid=(T // tb,),
        in_specs=[tok(D_MODEL), tok(DN_WIDTH), tok(DN_WIDTH), tok(3 * SC_WIDTH),
                  pl.BlockSpec((8, 3 * SC_WIDTH), _before_halo(tb)), _shard_rows(land_a, A_OUT_AT, OUT_SHARD),
                  full(gn), full(scw), full(gs), _ANY],
        out_specs=[tok(DN_WIDTH), _dp_block(tb, P_Z, DN_WIDTH), tok(SC_WIDTH), tok(SC_WIDTH),
                   acc(HEAD_DIM), acc(SC_WIDTH), acc(SC_WIDTH)],
        out_shape=[jax.ShapeDtypeStruct((T, DN_WIDTH), F32), jax.ShapeDtypeStruct(dp.shape, BF16),
                   jax.ShapeDtypeStruct((T, SC_WIDTH), F32), jax.ShapeDtypeStruct((T, SC_WIDTH), F32),
                   jax.ShapeDtypeStruct((8, HEAD_DIM), F32), jax.ShapeDtypeStruct((8, SC_WIDTH), F32),
                   jax.ShapeDtypeStruct((8, SC_WIDTH), F32)],
        input_output_aliases={9: 1},
        compiler_params=_params(("arbitrary",)),
    )(dx1, o, z, sc_in, sc_in, land_a, gn, scw, gs, dp)


def _sc_conv_bwd(dcv, dgb, sc_in, scw, dp):
    T = dcv.shape[0]
    tb = 512

    def body(dcv_ref, halo_ref, dgb_ref, sc_ref, w_ref, dp_ref, out_ref):
        last = pl.program_id(0) == pl.num_programs(0) - 1
        halo = jnp.where(last, 0.0, halo_ref[...])
        xc = jnp.concatenate([dcv_ref[...], halo], axis=0)
        w = w_ref[...]
        du = w[2:3, :] * xc[0:tb, :] + w[1:2, :] * _rows_from(xc, 1, tb) + w[0:1, :] * _rows_from(xc, 2, tb)
        sc = sc_ref[...]
        out_ref[:, :SC_WIDTH] = dgb_ref[...].astype(BF16)
        out_ref[:, SC_WIDTH:2 * SC_WIDTH] = (du * sc[:, 2 * SC_WIDTH:]).astype(BF16)
        out_ref[:, 2 * SC_WIDTH:] = (du * sc[:, SC_WIDTH:2 * SC_WIDTH]).astype(BF16)

    tok = lambda w: pl.BlockSpec((tb, w), lambda i: (i, 0))
    return pl.pallas_call(
        body, name="sc_conv_bwd", grid=(T // tb,),
        in_specs=[tok(SC_WIDTH), pl.BlockSpec((8, SC_WIDTH), _after_halo(tb, T)), tok(SC_WIDTH), tok(3 * SC_WIDTH),
                  pl.BlockSpec(scw.shape, lambda i: (0, 0)), _ANY],
        out_specs=_dp_block(tb, P_SC, 3 * SC_WIDTH),
        out_shape=jax.ShapeDtypeStruct(dp.shape, BF16),
        input_output_aliases={5: 0},
        compiler_params=_params(("parallel",)),
    )(dcv, dcv, dgb, sc_in, scw, dp)


def _delta_bwd(q, k, v, bg, states, do):
    T = q.shape[0]
    tb = 512
    n_chunk = tb // CHUNK
    nb = T // tb

    def body(q_ref, k_ref, v_ref, bg_ref, st_ref, do_ref, dq_ref, dk_ref, dv_ref, dbg_ref, ds_ref):
        @pl.when(pl.program_id(0) == 0)
        def _():
            ds_ref[...] = jnp.zeros_like(ds_ref)

        masks = _chunk_masks()
        causal, strict = masks
        lane = lax.broadcasted_iota(jnp.int32, (CHUNK, LANES), 1)
        last_row = lax.broadcasted_iota(jnp.int32, (CHUNK, 1), 0) == CHUNK - 1
        cat = jnp.concatenate
        heads = range(HEADS)

        def open_chunk(ci, loc):
            rows = pl.ds(pl.multiple_of(ci * CHUNK, CHUNK), CHUNK)
            dov = do_ref[rows, :]
            return dict(rows=rows, loc=loc, do=[dov[:, HEAD_DIM * h:HEAD_DIM * (h + 1)] for h in heads],
                        state=[st_ref[ci, h] for h in heads])

        def a_free(c):
            loc, do, state = c["loc"], c["do"], c["state"]
            w_s = [_mm(p["w"], s) for p, s in zip(loc, state)]
            c["dq_dec"] = [_mm(d, s, NT) for d, s in zip(do, state)]
            c["qk_do"] = [_mm(p["qk"], d, TN) for p, d in zip(loc, do)]
            c["qd_do"] = [_mm(p["q_dec"], d, TN) for p, d in zip(loc, do)]
            c["v_new"] = [p["u"] - t for p, t in zip(loc, w_s)]
            c["dqk"] = [jnp.where(causal, _mm(d, vn, NT), 0.0) for d, vn in zip(do, c["v_new"])]

        def a_state(c, ds_next):
            c["ds_next"] = ds_next
            kd_ds = [_mm(p["k_dec"], d) for p, d in zip(c["loc"], ds_next)]
            c["dk_dec"] = [_mm(vn, d, NT) for vn, d in zip(c["v_new"], ds_next)]
            c["dv_new"] = [a + b for a, b in zip(c["qk_do"], kd_ds)]

        def b_state(c):
            loc = c["loc"]
            w_dv = [_mm(p["w"], dvn, TN) for p, dvn in zip(loc, c["dv_new"])]
            c["dw"] = [-_mm(dvn, s, NT) for dvn, s in zip(c["dv_new"], c["state"])]
            return [loc[h]["gl"] * c["ds_next"][h] + c["qd_do"][h] - w_dv[h] for h in heads]

        def c_solve(c):
            loc, dv_new, dw = c["loc"], c["dv_new"], c["dw"]
            c["dtm"] = [_mm(cat([dvn, d], axis=1), cat([p["vb"], p["kbg"]], axis=1), NT) for dvn, d, p in zip(dv_new, dw, loc)]
            x_t = [_mm(p["xm"], cat([dvn, d], axis=1), TN) for p, dvn, d in zip(loc, dv_new, dw)]
            c["dvb"] = [dvn + t[:, :HEAD_DIM] for dvn, t in zip(dv_new, x_t)]
            c["dkbg"] = [d + t[:, HEAD_DIM:] for d, t in zip(dw, x_t)]

        def d_solve(c):
            c["y"] = [t + _mm(p["xm"], t, TN) for p, t in zip(c["loc"], c["dtm"])]

        def e_solve(c):
            c["dlow"] = [jnp.where(strict, -(t + _mm(t, p["xm"], NT)), 0.0) for p, t in zip(c["loc"], c["y"])]

        def f_close(c):
            loc, rows = c["loc"], c["rows"]
            dmm = [d * p["decay"] for d, p in zip(c["dlow"], loc)]
            dnn = [d * p["decay"] for d, p in zip(c["dqk"], loc)]
            by_k = [_mm(cat([a, b], axis=0), p["k"]) for a, b, p in zip(dmm, dnn, loc)]
            dk_mm = [_mm(cat([a, b], axis=0), cat([p["kb"], p["q"]], axis=0), TN) for a, b, p in zip(dmm, dnn, loc)]
            dq_out, dk_out, dv_out = [], [], []
            dbeta_all = jnp.zeros((CHUNK, LANES), F32)
            dgc_all = jnp.zeros((CHUNK, LANES), F32)
            for h in heads:
                p = loc[h]
                dkb = by_k[h][:CHUNK] + c["dkbg"][h] * p["eg"]
                dq_out.append(by_k[h][CHUNK:] + c["dq_dec"][h] * p["eg"])
                dk_out.append(dk_mm[h] + c["dk_dec"][h] * p["ek"] + dkb * p["beta"])
                dv_out.append(c["dvb"][h] * p["beta"])
                dbeta = jnp.sum(dkb * p["k"] + c["dvb"][h] * p["v"], axis=1, keepdims=True)
                total = lambda t: jnp.sum(jnp.sum(t, axis=0, keepdims=True), axis=1, keepdims=True)
                e = c["dlow"][h] * p["low"] + c["dqk"][h] * p["qk"]
                kd = c["dk_dec"][h] * p["k_dec"]
                dgc = (jnp.sum(e, axis=1, keepdims=True) - jnp.sum(e.T, axis=1, keepdims=True)
                       + jnp.sum(c["dq_dec"][h] * p["q_dec"] + c["dkbg"][h] * p["kbg"] - kd, axis=1, keepdims=True))
                d_last = total(kd) + total(c["ds_next"][h] * c["state"][h]) * p["gl"]
                dgc = dgc + jnp.where(last_row, d_last, 0.0)
                dbeta_all = jnp.where(lane == h, dbeta, dbeta_all)
                dgc_all = jnp.where(lane == h + HEADS, dgc, dgc_all)
            dq_ref[rows, :] = cat(dq_out, axis=1)
            dk_ref[rows, :] = cat(dk_out, axis=1)
            dv_ref[rows, :] = cat(dv_out, axis=1)
            dbg_ref[rows, :] = dbeta_all + dgc_all

        def pair(pj, carry):
            hi = n_chunk - 1 - 2 * pj
            lo = hi - 1
            rows = [pl.ds(pl.multiple_of(ci * CHUNK, CHUNK), CHUNK) for ci in (hi, lo)]
            loc = _units_local(_chunk_units(q_ref, k_ref, v_ref, bg_ref, rows[0])
                               + _chunk_units(q_ref, k_ref, v_ref, bg_ref, rows[1]), masks)
            c_hi, c_lo = open_chunk(hi, loc[:HEADS]), open_chunk(lo, loc[HEADS:])
            a_free(c_hi)
            a_free(c_lo)
            a_state(c_hi, [ds_ref[h] for h in heads])
            ds_mid = b_state(c_hi)
            a_state(c_lo, ds_mid)
            c_solve(c_hi)
            ds_out = b_state(c_lo)
            for h in heads:
                ds_ref[h] = ds_out[h]
            d_solve(c_hi)
            c_solve(c_lo)
            e_solve(c_hi)
            d_solve(c_lo)
            f_close(c_hi)
            e_solve(c_lo)
            f_close(c_lo)
            return carry

        lax.fori_loop(0, n_chunk // 2, pair, 0)

    tok = lambda w: pl.BlockSpec((tb, w), lambda i: (nb - 1 - i, 0))
    return pl.pallas_call(
        body, name="delta_bwd", grid=(nb,),
        in_specs=[tok(DN_WIDTH), tok(DN_WIDTH), tok(DN_WIDTH), tok(LANES),
                  pl.BlockSpec((n_chunk, HEADS, HEAD_DIM, HEAD_DIM), lambda i: (nb - 1 - i, 0, 0, 0)), tok(DN_WIDTH)],
        out_specs=[tok(DN_WIDTH), tok(DN_WIDTH), tok(DN_WIDTH), tok(LANES)],
        out_shape=[jax.ShapeDtypeStruct((T, DN_WIDTH), F32)] * 3 + [jax.ShapeDtypeStruct((T, LANES), F32)],
        scratch_shapes=[pltpu.VMEM((HEADS, HEAD_DIM, HEAD_DIM), F32)],
        compiler_params=_params(("arbitrary",)),
    )(q, k, v, bg, states, do)


def _dn_prep_bwd(dq, dk, dv, dbg, qkv, cw, bd, al_row, dt_row, dp):
    T = qkv.shape[0]
    tb = 256

    def body(dq_ref, dk_ref, dv_ref, dbg_ref, pre_ref, halo_ref, cw_ref, bd_ref, al_ref, dt_ref, dp_ref,
             dc_ref, dbd_ref, dcw_ref, dal_ref, ddt_ref):
        @pl.when(pl.program_id(0) == 0)
        def _():
            dcw_ref[...] = jnp.zeros_like(dcw_ref)
            dal_ref[...] = jnp.zeros_like(dal_ref)
            ddt_ref[...] = jnp.zeros_like(ddt_ref)

        halo = jnp.where(pl.program_id(0) > 0, halo_ref[...], 0.0)
        xc, c, sg, a = _dn_act(pre_ref[...], halo, cw_ref[...], tb)
        dsilu = sg * (1.0 + c * (1.0 - sg))
        for hh in range(HEADS):
            sl = slice(HEAD_DIM * hh, HEAD_DIM * (hh + 1))
            for base, g_ref, scale in ((0, dq_ref, Q_SCALE), (DN_WIDTH, dk_ref, 1.0)):
                sa = slice(base + HEAD_DIM * hh, base + HEAD_DIM * (hh + 1))
                raw = a[:, sa]
                r = lax.rsqrt(jnp.sum(raw * raw, axis=-1, keepdims=True) + EPS)
                nrm = raw * r
                gn_ = g_ref[:, sl] * scale
                dc_ref[:, sa] = r * (gn_ - nrm * jnp.sum(gn_ * nrm, axis=-1, keepdims=True)) * dsilu[:, sa]
        dc_ref[:, 2 * DN_WIDTH:] = dv_ref[...] * dsilu[:, 2 * DN_WIDTH:]
        dc = dc_ref[...]
        for j in range(4):
            dcw_ref[j:j + 1, :] += jnp.sum(dc * _rows_from(xc, 5 + j, tb), axis=0, keepdims=True)
        bdv = bd_ref[...]
        lane = lax.broadcasted_iota(jnp.int32, bdv.shape, 1)
        is_b = lane < HEADS
        dbg_in = dbg_ref[...]
        dbgv = jnp.where(is_b, dbg_in, _mm32(_chunk_cumsum_matrix(tb), dbg_in, TN))
        is_g = jnp.logical_and(lane >= HEADS, lane < 2 * HEADS)
        beta = _sigmoid(bdv)
        neg_a = -jnp.exp(al_ref[...])
        pre_sp = bdv + dt_ref[...]
        g = neg_a * _softplus(pre_sp)
        da_in = dbgv * neg_a * _sigmoid(pre_sp)
        dbd_ref[...] = jnp.where(is_b, dbgv * beta * (1.0 - beta), jnp.where(is_g, da_in, 0.0)).astype(BF16)
        _row_acc(dal_ref, jnp.where(is_g, dbgv * g, 0.0))
        _row_acc(ddt_ref, jnp.where(is_g, da_in, 0.0))

    tok = lambda w: pl.BlockSpec((tb, w), lambda i: (i, 0))
    full = lambda t: pl.BlockSpec(t.shape, lambda i: (0, 0))
    acc = lambda w: pl.BlockSpec((8, w), lambda i: (0, 0))
    return pl.pallas_call(
        body, name="dn_prep_bwd", grid=(T // tb,),
        in_specs=[tok(DN_WIDTH), tok(DN_WIDTH), tok(DN_WIDTH), tok(LANES),
                  tok(QKV), pl.BlockSpec((8, QKV), _before_halo(tb)), full(cw), tok(LANES), full(al_row), full(dt_row), _ANY],
        out_specs=[tok(QKV), _dp_block(tb, P_BD, LANES), acc(QKV), acc(LANES), acc(LANES)],
        out_shape=[jax.ShapeDtypeStruct((T, QKV), F32), jax.ShapeDtypeStruct(dp.shape, BF16),
                   jax.ShapeDtypeStruct((8, QKV), F32), jax.ShapeDtypeStruct((8, LANES), F32),
                   jax.ShapeDtypeStruct((8, LANES), F32)],
        input_output_aliases={10: 1},
        compiler_params=_params(("arbitrary",)),
    )(dq, dk, dv, dbg, qkv, qkv, cw, bd, al_row, dt_row, dp)


def _dn_conv_bwd(dc, cw, dp):
    T = dc.shape[0]
    tb = 512

    def body(dc_ref, halo_ref, w_ref, dp_ref, out_ref):
        last = pl.program_id(0) == pl.num_programs(0) - 1
        halo = jnp.where(last, 0.0, halo_ref[...])
        xc = jnp.concatenate([dc_ref[...], halo], axis=0)
        w = w_ref[...]
        acc = w[3:4, :] * xc[0:tb, :]
        for j in range(3):
            acc = acc + w[j:j + 1, :] * _rows_from(xc, 3 - j, tb)
        out_ref[...] = acc.astype(BF16)

    tok = pl.BlockSpec((tb, QKV), lambda i: (i, 0))
    return pl.pallas_call(
        body, name="dn_conv_bwd", grid=(T // tb,),
        in_specs=[tok, pl.BlockSpec((8, QKV), _after_halo(tb, T)), pl.BlockSpec(cw.shape, lambda i: (0, 0)), _ANY],
        out_specs=_dp_block(tb, 0, QKV),
        out_shape=jax.ShapeDtypeStruct(dp.shape, BF16),
        input_output_aliases={3: 0},
        compiler_params=_params(("parallel",)),
    )(dc, dc, cw, dp)


def _dp_of_chip(dp, s):
    lo, hi = IN_SHARD * s, IN_SHARD * (s + 1)
    pieces = []
    for w_at, w_end, p_at in ((0, W_Z, 0), (W_Z, W_BD, P_Z), (W_BD, W_SC, P_BD), (W_SC, W_IN_COLS, P_SC)):
        a, b = max(lo, w_at), min(hi, w_end)
        if a < b:
            pieces.append(dp[:, p_at + a - w_at:p_at + b - w_at])
    pieces.append(jnp.zeros((dp.shape[0], D_MODEL - IN_SHARD), dp.dtype))
    return jnp.concatenate(pieces, axis=1)


def _in_proj_bwd(dp, dx1, x, g1, land_a):
    T = x.shape[0]
    tb = 256

    def body(dp_ref, dx1_ref, x_ref, g_ref, w_ref, dx_ref, dxb_ref, dps_ref, dg_ref):
        @pl.when(pl.program_id(0) == 0)
        def _():
            dg_ref[...] = jnp.zeros_like(dg_ref)

        dpv = dp_ref[...]
        dh = jnp.zeros((tb, D_MODEL), F32)
        for s in range(N_CHIPS):
            dps = _dp_of_chip(dpv, s)
            dps_ref[:, D_MODEL * s:D_MODEL * (s + 1)] = dps
            dh = dh + lax.dot_general(dps, w_ref[s], NT, preferred_element_type=F32)
        xv = x_ref[...]
        r = lax.rsqrt(jnp.mean(xv * xv, axis=-1, keepdims=True) + EPS)
        xh = xv * r
        _row_acc(dg_ref, dh * xh)
        dx = dx1_ref[...] + _rms_bwd(dh, xh, r, g_ref[...])
        dx_ref[...] = dx
        dxb_ref[...] = dx.astype(BF16)

    tok = lambda w: pl.BlockSpec((tb, w), lambda i: (i, 0))
    return pl.pallas_call(
        body, name="in_proj_bwd", grid=(T // tb,),
        in_specs=[tok(P_COLS), tok(D_MODEL), tok(D_MODEL), pl.BlockSpec(g1.shape, lambda i: (0, 0)),
                  _shard_rows(land_a, 0, D_MODEL)],
        out_specs=[tok(D_MODEL), tok(D_MODEL), tok(N_CHIPS * D_MODEL), pl.BlockSpec((8, D_MODEL), lambda i: (0, 0))],
        out_shape=[jax.ShapeDtypeStruct((T, D_MODEL), F32), jax.ShapeDtypeStruct((T, D_MODEL), BF16),
                   jax.ShapeDtypeStruct((T, N_CHIPS * D_MODEL), BF16), jax.ShapeDtypeStruct((8, D_MODEL), F32)],
        compiler_params=_params(("arbitrary",)),
    )(dp, dx1, x, g1, land_a)


def _wgrad_in_share(h, dps, parts, name):
    T = h.shape[0]
    bk = min(T, 1024)
    n_k = T // bk

    def body(a_ref, b_ref, parts_ref, o_ref, acc_ref):
        kk = pl.program_id(1)

        @pl.when(kk == 0)
        def _():
            acc_ref[...] = jnp.zeros_like(acc_ref)

        acc_ref[...] += lax.dot_general(a_ref[...], b_ref[...], TN, preferred_element_type=F32)

        @pl.when(kk == n_k - 1)
        def _():
            o_ref[0] = acc_ref[...].astype(BF16)

    return pl.pallas_call(
        body, name=name, grid=(N_CHIPS, n_k),
        in_specs=[pl.BlockSpec((bk, D_MODEL), lambda j, kk: (kk, 0)), pl.BlockSpec((bk, D_MODEL), lambda j, kk: (kk, j)), _ANY],
        out_specs=pl.BlockSpec((1, D_MODEL, D_MODEL), lambda j, kk: (j, 0, 0)),
        out_shape=jax.ShapeDtypeStruct(parts.shape, BF16),
        scratch_shapes=[pltpu.VMEM((D_MODEL, D_MODEL), F32)],
        input_output_aliases={2: 0},
        compiler_params=_params(("parallel", "arbitrary")),
    )(h, dps, parts)


def _pad_rows(a, rows=8):
    return jnp.pad(a, ((0, rows - a.shape[0]), (0, 0)))


def _gate_rows(a_log, dt_bias):
    put = lambda t: jnp.pad(t.reshape(1, HEADS), ((0, 0), (HEADS, LANES - 2 * HEADS)))
    return put(a_log), put(dt_bias)


def _mixer_fwd(x, p):
    qkv, z, sc_in, bd, h = _in_proj(x, p["g1"], p["land_a"])
    q, k, v, bg = _dn_prep(qkv, p["cw"], bd, p["al"], p["dt"])
    o, states = _delta_fwd(q, k, v, bg)
    x1, mix = _mix_out(o, z, sc_in, x, p["land_a"], p["gn"], p["scw"], p["gs"])
    return x1, dict(x=x, qkv=qkv, z=z, sc_in=sc_in, bd=bd, h=h, q=q, k=k, v=v, bg=bg, o=o, states=states, mix=mix)


def _ffn_fwd(x1, p, land_b):
    x2, a, b, h2 = _ffn(x1, p["g2"], land_b)
    return x2, dict(x1=x1, a=a, b=b, h2=h2)


def _ffn_back(dx2, dx2_bf16, s, p, land_b):
    dx1, dx1_bf16, da, db, act, dg2 = _ffn_bwd(dx2, s["x1"], s["a"], s["b"], p["g2"], land_b)
    parts = lax.empty((N_CHIPS, B_ROWS, D_MODEL), BF16)
    parts = _wgrad_share(act, dx2_bf16, parts, 2 * FF_SHARD, "wgrad_down")
    parts = _wgrad_share(da, s["h2"], parts, 0, "wgrad_gate")
    parts = _wgrad_share(db, s["h2"], parts, FF_SHARD, "wgrad_up")
    return dx1, dx1_bf16, parts, dg2[0]


def _mixer_bwd(dx1, dx1_bf16, s, p):
    dp = lax.empty((dx1.shape[0], P_COLS), BF16)
    do, dp, dgb, dcv, dgn, dgs, dscw = _mix_out_bwd(dx1, s["o"], s["z"], s["sc_in"], p["land_a"], p["gn"], p["scw"], p["gs"], dp)
    dp = _sc_conv_bwd(dcv, dgb, s["sc_in"], p["scw"], dp)
    dq, dk, dv, dbg = _delta_bwd(s["q"], s["k"], s["v"], s["bg"], s["states"], do)
    dc, dp, dcw, dal, ddt = _dn_prep_bwd(dq, dk, dv, dbg, s["qkv"], p["cw"], s["bd"], p["al"], p["dt"], dp)
    dp = _dn_conv_bwd(dc, p["cw"], dp)
    dx, dx_bf16, dps, dg1 = _in_proj_bwd(dp, dx1, s["x"], p["g1"], p["land_a"])
    parts = lax.empty((N_CHIPS, A_ROWS, D_MODEL), BF16)
    parts = _wgrad_in_share(s["h"], dps, parts, "wgrad_in")
    parts = _wgrad_share(s["mix"], dx1_bf16, parts, A_OUT_AT, "wgrad_out")
    g = dict(g1=dg1[0], gn=dgn[0], gs=dgs[0], scw=dscw[:3], cw=dcw[:4], al=dal[0, HEADS:2 * HEADS], dt=ddt[0, HEADS:2 * HEADS])
    return dx, dx_bf16, parts, g


def _place():
    return lax.axis_index("x"), lax.axis_index("y"), lax.axis_index("c")


def _other_chips(x, y):
    return [(1 - x, y), (x, 1 - y), (1 - x, 1 - y)]


_HBM = pl.BlockSpec(memory_space=pltpu.HBM)


def _chip_exchange(arrs, name, gather):
    n = len(arrs)

    def body(*refs):
        ins, outs = refs[:n], refs[n:2 * n]
        send_sems, recv_sems, local_sems = refs[2 * n:]
        x, y, c = _place()
        me = 2 * x + y
        others = _other_chips(x, y)

        def remote(k, j, landing):
            px, py = others[j]
            src = ins[k] if gather else ins[k].at[2 * px + py]
            return pltpu.make_async_remote_copy(src_ref=src, dst_ref=outs[k].at[landing], send_sem=send_sems.at[k, j],
                                                recv_sem=recv_sems.at[k, j], device_id=(px, py, c), device_id_type=MESH)

        local = [pltpu.make_async_copy(ins[k] if gather else ins[k].at[me], outs[k].at[me], local_sems.at[k])
                 for k in range(n)]
        sends = [remote(k, j, me) for k in range(n) for j in range(3)]
        for cp in local + sends:
            cp.start()
        for k in range(n):
            for j, (px, py) in enumerate(others):
                remote(k, j, 2 * px + py).wait_recv()
        for cp in sends:
            cp.wait_send()
        for cp in local:
            cp.wait()

    shapes = [jax.ShapeDtypeStruct(((N_CHIPS,) + a.shape) if gather else a.shape, a.dtype) for a in arrs]
    return pl.pallas_call(
        body, name=name, in_specs=[_HBM] * n, out_specs=[_HBM] * n, out_shape=shapes,
        scratch_shapes=[pltpu.SemaphoreType.DMA((n, 3)), pltpu.SemaphoreType.DMA((n, 3)), pltpu.SemaphoreType.DMA((n,))],
    )(*arrs)


_SEM = pl.BlockSpec(memory_space=pltpu.SEMAPHORE)
_ANY = pl.BlockSpec(memory_space=pl.ANY)
_EFFECT = pltpu.SideEffectType.DATAFLOW_SIDE_EFFECTING


_FLIPS = [(a, b, cc) for a in (0, 1) for b in (0, 1) for cc in (0, 1)][1:]


def _split_copies(src_ref, land_ref, send_sems, recv_sems, gather, sending):
    x, y, c = _place()
    copies = []
    if gather:
        me = 2 * x + y
        for j, (px, py) in enumerate(_other_chips(x, y)):
            copies.append(pltpu.make_async_remote_copy(
                src_ref=src_ref, dst_ref=land_ref.at[me if sending else 2 * px + py],
                send_sem=send_sems.at[j], recv_sem=recv_sems.at[j], device_id=(px, py, c), device_id_type=MESH))
        return copies
    me = 4 * x + 2 * y + c
    for j, (a, b, cc) in enumerate(_FLIPS):
        px, py, pc = (1 - x) if a else x, (1 - y) if b else y, (1 - c) if cc else c
        copies.append(pltpu.make_async_remote_copy(
            src_ref=src_ref.at[2 * px + py], dst_ref=land_ref.at[me if sending else 4 * px + 2 * py + pc],
            send_sem=send_sems.at[j], recv_sem=recv_sems.at[j], device_id=(px, py, pc), device_id_type=MESH))
    return copies


def _own_slot(share):
    chip = 2 * lax.axis_index("x") + lax.axis_index("y")
    return lax.dynamic_update_slice(lax.empty((N_CHIPS,) + share.shape, share.dtype), share[None], (chip, 0, 0))


def _own_part(parts):
    chip = 2 * lax.axis_index("x") + lax.axis_index("y")
    own = lax.dynamic_index_in_dim(parts, chip, 0, keepdims=True)
    return lax.dynamic_update_slice(lax.empty((N_DEV,) + parts.shape[1:], parts.dtype), own,
                                    (2 * chip + lax.axis_index("c"), 0, 0))


def _exchange_start(src, land, after, name, gather):
    def body(src_ref, land_ref, after_ref, send_sems, recv_sems, src_thru, land_thru, token):
        for cp in _split_copies(src_ref, land_ref, send_sems, recv_sems, gather, sending=True):
            cp.start()
        token[...] = jnp.zeros_like(token)

    hbm = lambda t: pltpu.with_memory_space_constraint(t, pltpu.HBM)
    n_copies = N_CHIPS - 1 if gather else N_DEV - 1
    return pl.pallas_call(
        body, name=name,
        out_shape=(pltpu.SemaphoreType.DMA((n_copies,)), pltpu.SemaphoreType.DMA((n_copies,)), pltpu.HBM(src.shape, src.dtype),
                   pltpu.HBM(land.shape, land.dtype), jax.ShapeDtypeStruct((8, LANES), F32)),
        in_specs=(_HBM, _HBM, _ANY), out_specs=(_SEM, _SEM, _HBM, _HBM, pl.BlockSpec(memory_space=pltpu.VMEM)),
        input_output_aliases={0: 2, 1: 3},
        compiler_params=pltpu.CompilerParams(has_side_effects=_EFFECT),
    )(hbm(src), hbm(land), after)


def _exchange_wait(started, after, name, gather):
    send_sems, recv_sems, src_thru, land_thru, _ = started

    def body(src_ref, land_ref, send_sems, recv_sems, after_ref, src_dead, got_ref):
        for cp in _split_copies(src_ref, land_ref, send_sems, recv_sems, gather, sending=False):
            cp.wait_send()
            cp.wait_recv()

    return pl.pallas_call(
        body, name=name,
        out_shape=(pltpu.HBM(src_thru.shape, src_thru.dtype), pltpu.HBM(land_thru.shape, land_thru.dtype)),
        in_specs=(_HBM, _HBM, _SEM, _SEM, _ANY), out_specs=(_HBM, _HBM), input_output_aliases={0: 0, 1: 1},
        compiler_params=pltpu.CompilerParams(has_side_effects=_EFFECT),
    )(src_thru, land_thru, send_sems, recv_sems, after)[1]


def _all_reduce_small(v):
    rows = v.shape[0]
    flips = [(a, b, cc) for a in (0, 1) for b in (0, 1) for cc in (0, 1)][1:]

    def body(v_ref, out_ref, buf_ref, send_sems, recv_sems):
        x, y, c = _place()
        me = 4 * x + 2 * y + c
        peers = [((1 - x) if a else x, (1 - y) if b else y, (1 - c) if cc else c) for a, b, cc in flips]

        def copy(j, landing):
            return pltpu.make_async_remote_copy(src_ref=v_ref, dst_ref=buf_ref.at[landing], send_sem=send_sems.at[j],
                                                recv_sem=recv_sems.at[j], device_id=peers[j], device_id_type=MESH)

        sends = [copy(j, me) for j in range(N_DEV - 1)]
        for cp in sends:
            cp.start()
        buf_ref[me] = v_ref[...]
        for j, (px, py, pc) in enumerate(peers):
            copy(j, 4 * px + 2 * py + pc).wait_recv()
        for cp in sends:
            cp.wait_send()
        acc = buf_ref[0]
        for d in range(1, N_DEV):
            acc = acc + buf_ref[d]
        out_ref[...] = acc

    vmem = pl.BlockSpec(memory_space=pltpu.VMEM)
    return pl.pallas_call(
        body, name="all_reduce_small", in_specs=[vmem], out_specs=vmem,
        out_shape=jax.ShapeDtypeStruct(v.shape, F32),
        scratch_shapes=[pltpu.VMEM((N_DEV, rows, LANES), F32), pltpu.SemaphoreType.DMA((N_DEV - 1,)),
                        pltpu.SemaphoreType.DMA((N_DEV - 1,))],
    )(v)


def _row_block(*sizes):
    return next(t for t in (128, 64) if all(s % t == 0 for s in sizes))


def _adam_update(w, m, v, g):
    r1 = 1.0 / (1.0 - ADAM_B1 ** ADAM_STEP)
    r2 = 1.0 / (1.0 - ADAM_B2 ** ADAM_STEP)
    m_new = ADAM_B1 * m + (1.0 - ADAM_B1) * g
    v_new = ADAM_B2 * v + (1.0 - ADAM_B2) * (g * g)
    return -ADAM_LR * ((m_new * r1) / (jnp.sqrt(v_new * r2) + ADAM_EPS) + ADAM_WD * w), m_new, v_new


def _adamw_rows(w, m, v, got, first, name):
    n_layers, rows, cols = w.shape
    tr = _row_block(rows, first)

    def body(*refs):
        w_ref, m_ref, v_ref = refs[:3]
        g_out, d_out, m_out, v_out = refs[3 + n_layers:]
        for k in range(n_layers):
            @pl.when(pl.program_id(0) == k)
            def _(p_ref=refs[3 + k]):
                g = p_ref[0].astype(F32)
                for d in range(1, N_DEV):
                    g = g + p_ref[d].astype(F32)
                g = g[:, :cols]
                d_out[0], m_out[0], v_out[0] = _adam_update(w_ref[0], m_ref[0], v_ref[0], g)
                g_out[0] = g

    blk = pl.BlockSpec((1, tr, cols), lambda l, i: (l, i, 0))
    parts = [pl.BlockSpec((N_DEV, tr, got[0].shape[2]), lambda l, i, k=k: (0, jnp.where(l == k, first // tr + i, 0), 0))
             for k in range(n_layers)]
    return pl.pallas_call(
        body, name=name, grid=(n_layers, rows // tr),
        in_specs=[blk] * 3 + parts, out_specs=[blk] * 4,
        out_shape=[jax.ShapeDtypeStruct(w.shape, F32)] * 4,
        compiler_params=_params(("arbitrary", "arbitrary")),
    )(w, m, v, *got)


def _adamw(w, m, v, g_parts, name):
    rows, cols = w.shape
    tr = min(rows, 256)
    n = len(g_parts)

    def body(*refs):
        w_ref, m_ref, v_ref = refs[:3]
        g_refs = refs[3:3 + n]
        g_out, d_out, m_out, v_out = refs[3 + n:]
        g = g_refs[0][...]
        for r in g_refs[1:]:
            g = g + r[...]
        d_out[...], m_out[...], v_out[...] = _adam_update(w_ref[...], m_ref[...], v_ref[...], g)
        g_out[...] = g

    blk = pl.BlockSpec((tr, cols), lambda i: (i, 0))
    return pl.pallas_call(
        body, name=name, grid=(rows // tr,),
        in_specs=[blk] * (3 + n), out_specs=[blk] * 4,
        out_shape=[jax.ShapeDtypeStruct((rows, cols), F32)] * 4,
        compiler_params=_params(("parallel",)),
    )(w, m, v, *g_parts)


def _pack(parts, rows, fill=0.0):
    flat = jnp.concatenate([p.reshape(-1) for p in parts])
    return jnp.pad(flat, (0, rows * LANES - flat.shape[0]), constant_values=fill).reshape(rows, LANES)


def _unpack(packed, shapes):
    flat = packed.reshape(-1)
    out, at = [], 0
    for shp in shapes:
        size = 1
        for s in shp:
            size *= s
        out.append(flat[at:at + size].reshape(shp))
        at += size
    return out


def _packed_rows(shapes):
    total = 0
    for shp in shapes:
        size = 1
        for s in shp:
            size *= s
        total += size
    return -(-total // (8 * LANES)) * 8


def _cols_full(g, l):
    t = g[:, l]
    return jnp.moveaxis(t, 0, 1).reshape(t.shape[1], N_CHIPS * t.shape[2])


def _pad_cols(t):
    return jnp.pad(t, ((0, 0),) * (t.ndim - 1) + ((0, D_MODEL - t.shape[-1]),))


def kernel(x, norm1_g, w_in, dn_conv_w, dn_a_log, dn_dt_bias, dn_norm_g, sc_conv_w, sc_norm_g, w_out, norm2_g, ffn_w_gate, ffn_w_up, ffn_w_down, final_norm_g, loss_target, m_norm1_g, m_w_in, m_dn_conv_w, m_dn_a_log, m_dn_dt_bias, m_dn_norm_g, m_sc_conv_w, m_sc_norm_g, m_w_out, m_norm2_g, m_ffn_w_gate, m_ffn_w_up, m_ffn_w_down, m_final_norm_g, v_norm1_g, v_w_in, v_dn_conv_w, v_dn_a_log, v_dn_dt_bias, v_dn_norm_g, v_sc_conv_w, v_sc_norm_g, v_w_out, v_norm2_g, v_ffn_w_gate, v_ffn_w_up, v_ffn_w_down, v_final_norm_g):
    chip = 2 * lax.axis_index("x") + lax.axis_index("y")

    g_cw, g_scw = _chip_exchange([dn_conv_w, sc_conv_w], "gather_conv", gather=True)

    t_last = lambda t: jnp.swapaxes(t, -1, -2)
    gate_t, up_t = t_last(ffn_w_gate), t_last(ffn_w_up)
    zero_token = jnp.zeros((8, LANES), F32)

    def shares(l, tie):
        share_a = jnp.concatenate([_pad_cols(w_in[l] + tie), w_out[l]], axis=0).astype(BF16)
        share_b = jnp.concatenate([gate_t[l] + tie, up_t[l], ffn_w_down[l]], axis=0).astype(BF16)
        return share_a, _own_slot(share_a), share_b, _own_slot(share_b)

    def gather_start(l, packed, after):
        a = _exchange_start(packed[0], packed[1], after, "gather_a_start_%d" % l, gather=True)
        b = _exchange_start(packed[2], packed[3], a[4], "gather_b_start_%d" % l, gather=True)
        return a, b

    ga, gb = gather_start(0, shares(0, 0.0), g_cw)
    packed = [None] + [shares(l, gb[4][0, 0]) for l in range(1, DEPTH)]
    packed_all = sum(t[0, 0].astype(F32) for p in packed[1:] for t in (p[0], p[2]))
    land_a = _exchange_wait(ga, zero_token + packed_all, "gather_a_wait_0", gather=True)
    act = x[0]
    layers, saved_m, saved_f, lands_b = [], [], [], []
    for l in range(DEPTH):
        hold = 0.0
        if l + 1 < DEPTH:
            ga, gb_next = gather_start(l + 1, packed[l + 1], land_a)
            hold = gb_next[4][0:1, 0:1]
        al, dt = _gate_rows(dn_a_log[l], dn_dt_bias[l])
        layers.append(dict(
            g1=norm1_g[l][None] + hold, cw=_pad_rows(_cols_full(g_cw, l)), al=al, dt=dt,
            gn=dn_norm_g[l][None], scw=_pad_rows(_cols_full(g_scw, l)), gs=sc_norm_g[l][None],
            land_a=land_a, g2=norm2_g[l][None]))
        x1, s = _mixer_fwd(act, layers[l])
        saved_m.append(s)
        lands_b.append(_exchange_wait(gb, x1, "gather_b_wait_%d" % l, gather=True))
        act, s = _ffn_fwd(x1, layers[l], lands_b[l])
        saved_f.append(s)
        if l + 1 < DEPTH:
            land_a = _exchange_wait(ga, act, "gather_a_wait_%d" % (l + 1), gather=True)
            gb = gb_next

    dact, dact_bf16, loss_part, d_final = _loss_head(act, final_norm_g[None], loss_target[0])
    grads, reduce_a, reduce_b = [None] * DEPTH, [None] * DEPTH, [None] * DEPTH
    hold = 0.0
    for l in reversed(range(DEPTH)):
        p = layers[l]
        dx1, dx1_bf16, parts, dg2 = _ffn_back(dact, dact_bf16, saved_f[l], dict(p, g2=p["g2"] + hold), lands_b[l])
        reduce_b[l] = _exchange_start(parts, _own_part(parts), zero_token, "reduce_b_start_%d" % l, gather=False)
        dact, dact_bf16, parts, gm = _mixer_bwd(dx1, dx1_bf16, saved_m[l], dict(p, gn=p["gn"] + reduce_b[l][4][0:1, 0:1]))
        reduce_a[l] = _exchange_start(parts, _own_part(parts), zero_token, "reduce_a_start_%d" % l, gather=False)
        hold = reduce_a[l][4][0:1, 0:1]
        grads[l] = dict(gm, g2=dg2)
    loss = lax.psum(loss_part[0, 0], ("x", "y", "c"))
    stack = lambda key: jnp.stack([grads[l][key] for l in range(DEPTH)])

    got_b = [_exchange_wait(reduce_b[l], reduce_a[0][4], "reduce_b_wait_%d" % l, gather=False)
             for l in reversed(range(DEPTH))][::-1]
    big = dict(
        ffn_w_gate=[t_last(o) for o in _adamw_rows(gate_t, t_last(m_ffn_w_gate), t_last(v_ffn_w_gate), got_b, 0, "adamw_gate")],
        ffn_w_up=[t_last(o) for o in _adamw_rows(up_t, t_last(m_ffn_w_up), t_last(v_ffn_w_up), got_b, FF_SHARD, "adamw_up")],
        ffn_w_down=_adamw_rows(ffn_w_down, m_ffn_w_down, v_ffn_w_down, got_b, 2 * FF_SHARD, "adamw_down"))
    after_b = big["ffn_w_down"][1]
    got_a = [_exchange_wait(reduce_a[l], after_b, "reduce_a_wait_%d" % l, gather=False) for l in reversed(range(DEPTH))][::-1]
    big.update(
        w_in=_adamw_rows(w_in, m_w_in, v_w_in, got_a, 0, "adamw_w_in"),
        w_out=_adamw_rows(w_out, m_w_out, v_w_out, got_a, A_OUT_AT, "adamw_w_out"))

    full_shapes = [(DEPTH, D_MODEL), (DEPTH, D_MODEL), (DEPTH, HEAD_DIM), (DEPTH, SC_WIDTH), (DEPTH, HEADS),
                   (DEPTH, HEADS), (D_MODEL,), (DEPTH, 4, QKV), (DEPTH, 3, SC_WIDTH)]
    small_keys = ("g1", "g2", "gn", "gs", "al", "dt")
    packed = _pack([stack(k) for k in small_keys] + [d_final[0], stack("cw"), stack("scw")], _packed_rows(full_shapes))
    sg = _unpack(_all_reduce_small(packed), full_shapes)
    sg[7] = lax.dynamic_slice_in_dim(sg[7], chip * (QKV // N_CHIPS), QKV // N_CHIPS, axis=2)
    sg[8] = lax.dynamic_slice_in_dim(sg[8], chip * (SC_WIDTH // N_CHIPS), SC_WIDTH // N_CHIPS, axis=2)
    small_names = ("norm1_g", "norm2_g", "dn_norm_g", "sc_norm_g", "dn_a_log", "dn_dt_bias", "final_norm_g",
                   "dn_conv_w", "sc_conv_w")
    sw = (norm1_g, norm2_g, dn_norm_g, sc_norm_g, dn_a_log, dn_dt_bias, final_norm_g, dn_conv_w, sc_conv_w)
    sm = (m_norm1_g, m_norm2_g, m_dn_norm_g, m_sc_norm_g, m_dn_a_log, m_dn_dt_bias, m_final_norm_g, m_dn_conv_w, m_sc_conv_w)
    sv = (v_norm1_g, v_norm2_g, v_dn_norm_g, v_sc_norm_g, v_dn_a_log, v_dn_dt_bias, v_final_norm_g, v_dn_conv_w, v_sc_conv_w)
    shard_shapes = [t.shape for t in sw]
    rows = _packed_rows(shard_shapes)
    outs = _adamw(_pack(sw, rows), _pack(sm, rows), _pack(sv, rows, fill=1.0), [_pack(sg, rows)], "adamw_small")
    small = {name: [] for name in small_names}
    for o in outs:
        for name, t in zip(small_names, _unpack(o, shard_shapes)):
            small[name].append(t)

    order = ("norm1_g", "w_in", "dn_conv_w", "dn_a_log", "dn_dt_bias", "dn_norm_g", "sc_conv_w", "sc_norm_g", "w_out",
             "norm2_g", "ffn_w_gate", "ffn_w_up", "ffn_w_down", "final_norm_g")
    result = {**big, **small}
    return (loss, dact[None], *[result[n][0] for n in order], *[result[n][1] for n in order],
            *[result[n][2] for n in order], *[result[n][3] for n in order])
```

```python
import jax
import jax.numpy as jnp
from jax import lax
from jax.experimental import pallas as pl
from jax.experimental.pallas import tpu as pltpu

F32 = jnp.float32
BF16 = jnp.bfloat16
MESH = pl.DeviceIdType.MESH

D_MODEL = 1024
DEPTH = 4
HEADS = 4
HEAD_DIM = 128
DN_WIDTH = HEADS * HEAD_DIM
SC_WIDTH = 512
SC_GROUPS = 4
D_FF = 2816
CHUNK = 64
QKV = 3 * DN_WIDTH
W_IN_COLS = 4 * DN_WIDTH + 2 * HEADS + 3 * SC_WIDTH
WA_COLS = QKV + DN_WIDTH + 3 * SC_WIDTH
LANES = 128
EPS = 1e-6
Q_SCALE = HEAD_DIM ** -0.5
N_CHIPS = 4
N_DEV = 8
IN_SHARD = W_IN_COLS // N_CHIPS
OUT_SHARD = D_MODEL // N_CHIPS
FF_SHARD = D_FF // N_CHIPS
A_OUT_AT = D_MODEL
A_ROWS = D_MODEL + OUT_SHARD
B_ROWS = 3 * FF_SHARD

ADAM_LR = 0.001
ADAM_B1 = 0.9
ADAM_B2 = 0.999
ADAM_EPS = 1e-08
ADAM_WD = 0.01
ADAM_STEP = 10

VMEM_LIMIT = 56 * 1024 * 1024

NN = (((1,), (0,)), ((), ()))
NT = (((1,), (1,)), ((), ()))
TN = (((0,), (0,)), ((), ()))


def _mm(a, b, dims=NN):
    return lax.dot_general(a.astype(BF16), b.astype(BF16), dims, preferred_element_type=F32)


def _mm32(a, b, dims=NN):
    return lax.dot_general(a, b, dims, preferred_element_type=F32, precision=lax.Precision.HIGHEST)


def _params(sem, vmem=VMEM_LIMIT):
    return pltpu.CompilerParams(dimension_semantics=sem, vmem_limit_bytes=vmem)


def _sigmoid(x):
    return 0.5 * jnp.tanh(0.5 * x) + 0.5


def _softplus(x):
    return jnp.maximum(x, 0.0) + jnp.log1p(jnp.exp(-jnp.abs(x)))


def _row_acc(acc_ref, val):
    acc_ref[0:1, :] += jnp.sum(val, axis=0, keepdims=True)


def _rms_bwd(dh, xh, r, gain):
    dxh = dh * gain
    return r * (dxh - xh * jnp.mean(dxh * xh, axis=-1, keepdims=True))


def _before_halo(tb):
    return lambda i: (jnp.maximum(i * (tb // 8) - 1, 0), 0)


def _after_halo(tb, n_rows):
    last = n_rows // 8 - 1
    return lambda i: (jnp.minimum((i + 1) * (tb // 8), last), 0)


def _rows_from(xc, offset, tb):
    part = offset % 8
    if part:
        xc = pltpu.roll(xc, xc.shape[0] - part, 0)
    return xc[offset - part:offset - part + tb, :]


def _taps(xc, w, n_taps, tb, first):
    out = w[0:1, :] * _rows_from(xc, first, tb)
    for j in range(1, n_taps):
        out = out + w[j:j + 1, :] * _rows_from(xc, first + j, tb)
    return out


W_Z = QKV
W_BD = W_Z + DN_WIDTH
W_SC = W_BD + 2 * HEADS
P_SC = QKV
P_Z = P_SC + 3 * SC_WIDTH
P_BD = P_Z + DN_WIDTH
P_COLS = P_BD + LANES


def _w_in_cols(shards, lo, hi):
    pieces = []
    for s in range(N_CHIPS):
        a, b = max(lo, IN_SHARD * s), min(hi, IN_SHARD * (s + 1))
        if a < b:
            pieces.append(shards[s][:, a - IN_SHARD * s:b - IN_SHARD * s])
    return pieces[0] if len(pieces) == 1 else jnp.concatenate(pieces, axis=1)


def _in_proj(x, g1, land_a):
    T = x.shape[0]
    tb = 256

    def body(x_ref, g_ref, w_ref, qkv_ref, z_ref, sc_ref, bd_ref, h_ref):
        xv = x_ref[...]
        r = lax.rsqrt(jnp.mean(xv * xv, axis=-1, keepdims=True) + EPS)
        h = (xv * r * g_ref[...]).astype(BF16)
        shards = [jnp.dot(h, w_ref[s], preferred_element_type=F32) for s in range(N_CHIPS)]
        qkv_ref[...] = _w_in_cols(shards, 0, W_Z)
        z_ref[...] = _w_in_cols(shards, W_Z, W_BD)
        bd_ref[...] = jnp.concatenate([_w_in_cols(shards, W_BD, W_SC), jnp.zeros((tb, LANES - 2 * HEADS), F32)], axis=1)
        sc_ref[...] = _w_in_cols(shards, W_SC, W_IN_COLS)
        h_ref[...] = h

    tok = lambda w: pl.BlockSpec((tb, w), lambda i: (i, 0))
    return pl.pallas_call(
        body, name="in_proj", grid=(T // tb,),
        in_specs=[tok(D_MODEL), pl.BlockSpec(g1.shape, lambda i: (0, 0)), _shard_rows(land_a, 0, D_MODEL)],
        out_specs=[tok(QKV), tok(DN_WIDTH), tok(3 * SC_WIDTH), tok(LANES), tok(D_MODEL)],
        out_shape=[jax.ShapeDtypeStruct((T, QKV), F32), jax.ShapeDtypeStruct((T, DN_WIDTH), F32),
                   jax.ShapeDtypeStruct((T, 3 * SC_WIDTH), F32), jax.ShapeDtypeStruct((T, LANES), F32),
                   jax.ShapeDtypeStruct((T, D_MODEL), BF16)],
        compiler_params=_params(("parallel",)),
    )(x, g1, land_a)


def _dp_block(tb, first, width, index=lambda i: i):
    assert first % width == 0
    return pl.BlockSpec((tb, width), lambda i: (index(i), first // width))


def _dn_act(pre, halo, cw, tb):
    xc = jnp.concatenate([halo, pre], axis=0)
    c = _taps(xc, cw, 4, tb, 5)
    sg = _sigmoid(c)
    return xc, c, sg, c * sg


def _gates(bd, al_row, dt_row):
    lane = lax.broadcasted_iota(jnp.int32, bd.shape, 1)
    beta = _sigmoid(bd)
    g = -jnp.exp(al_row) * _softplus(bd + dt_row)
    return jnp.where(lane < HEADS, beta, jnp.where(lane < 2 * HEADS, g, 0.0))


def _dn_prep(qkv, cw, bd, al_row, dt_row):
    T = qkv.shape[0]
    tb = 512

    def body(pre_ref, halo_ref, cw_ref, bd_ref, al_ref, dt_ref, q_ref, k_ref, v_ref, bg_ref):
        halo = jnp.where(pl.program_id(0) > 0, halo_ref[...], 0.0)
        _, _, _, a = _dn_act(pre_ref[...], halo, cw_ref[...], tb)
        for hh in range(HEADS):
            sl = slice(HEAD_DIM * hh, HEAD_DIM * (hh + 1))
            qs = a[:, sl]
            q_ref[:, sl] = qs * (lax.rsqrt(jnp.sum(qs * qs, axis=-1, keepdims=True) + EPS) * Q_SCALE)
            ks = a[:, DN_WIDTH + HEAD_DIM * hh:DN_WIDTH + HEAD_DIM * (hh + 1)]
            k_ref[:, sl] = ks * lax.rsqrt(jnp.sum(ks * ks, axis=-1, keepdims=True) + EPS)
        v_ref[...] = a[:, 2 * DN_WIDTH:]
        gates = _gates(bd_ref[...], al_ref[...], dt_ref[...])
        lane = lax.broadcasted_iota(jnp.int32, gates.shape, 1)
        bg_ref[...] = jnp.where(lane < HEADS, gates, _mm32(_chunk_cumsum_matrix(tb), gates))

    tok = lambda w: pl.BlockSpec((tb, w), lambda i: (i, 0))
    full = lambda a: pl.BlockSpec(a.shape, lambda i: (0, 0))
    return pl.pallas_call(
        body, name="dn_prep", grid=(T // tb,),
        in_specs=[tok(QKV), pl.BlockSpec((8, QKV), _before_halo(tb)), full(cw), tok(LANES), full(al_row), full(dt_row)],
        out_specs=[tok(DN_WIDTH), tok(DN_WIDTH), tok(DN_WIDTH), tok(LANES)],
        out_shape=[jax.ShapeDtypeStruct((T, DN_WIDTH), F32)] * 3 + [jax.ShapeDtypeStruct((T, LANES), F32)],
        compiler_params=_params(("parallel",)),
    )(qkv, qkv, cw, bd, al_row, dt_row)


def _chunk_masks():
    row = lax.broadcasted_iota(jnp.int32, (CHUNK, CHUNK), 0)
    col = lax.broadcasted_iota(jnp.int32, (CHUNK, CHUNK), 1)
    return row >= col, row > col


def _chunk_cumsum_matrix(n):
    row = lax.broadcasted_iota(jnp.int32, (n, n), 0)
    col = lax.broadcasted_iota(jnp.int32, (n, n), 1)
    return jnp.logical_and(row >= col, row // CHUNK == col // CHUNK).astype(F32)


def _chunk_units(q_ref, k_ref, v_ref, bg_ref, rows):
    bgc = bg_ref[rows, :]
    bg_t = bgc.T
    qv, kv, vv = q_ref[rows, :], k_ref[rows, :], v_ref[rows, :]
    units = []
    for h in range(HEADS):
        sl = slice(HEAD_DIM * h, HEAD_DIM * (h + 1))
        units.append((qv[:, sl], kv[:, sl], vv[:, sl], bgc[:, h:h + 1], bgc[:, HEADS + h:HEADS + h + 1],
                      bg_t[HEADS + h:HEADS + h + 1, :]))
    return units


def _units_local(units, masks):
    causal, strict = masks
    pre = []
    for q, k, v, beta, gc, gr in units:
        kb = k * beta
        eg = jnp.exp(gc)
        g_last = gc[CHUNK - 1:CHUNK, :]
        ek = jnp.exp(g_last - gc)
        pre.append(dict(q=q, k=k, v=v, beta=beta, decay=jnp.exp(jnp.where(causal, gc - gr, -1e30)), kb=kb, vb=v * beta,
                        eg=eg, kbg=kb * eg, ek=ek, gl=jnp.exp(g_last), q_dec=q * eg, k_dec=k * ek))
    both = [_mm(jnp.concatenate([p["kb"], p["q"]], axis=0), p["k"], NT) for p in pre]
    for p, b in zip(pre, both):
        p["low"] = jnp.where(strict, b[:CHUNK] * p["decay"], 0.0)
        p["qk"] = jnp.where(causal, b[CHUNK:] * p["decay"], 0.0)
    xs = [-p["low"] for p in pre]
    pw = [_mm(p["low"], p["low"]) for p in pre]
    for _ in range(4):
        both = [_mm(jnp.concatenate([pp, x], axis=0), pp) for pp, x in zip(pw, xs)]
        xs = [x + pp + b[CHUNK:] for x, pp, b in zip(xs, pw, both)]
        pw = [b[:CHUNK] for b in both]
    last = [_mm(x, pp) for x, pp in zip(xs, pw)]
    xs = [x + pp + b for x, pp, b in zip(xs, pw, last)]
    uw = [_mm(x, jnp.concatenate([p["vb"], p["kbg"]], axis=1)) for x, p in zip(xs, pre)]
    for p, x, b in zip(pre, xs, uw):
        p["xm"] = x
        p["u"] = p["vb"] + b[:, :HEAD_DIM]
        p["w"] = p["kbg"] + b[:, HEAD_DIM:]
    return pre


def _delta_fwd(q, k, v, bg):
    T = q.shape[0]
    tb = 512
    n_chunk = tb // CHUNK

    def body(q_ref, k_ref, v_ref, bg_ref, o_ref, st_ref, s_ref):
        @pl.when(pl.program_id(0) == 0)
        def _():
            s_ref[...] = jnp.zeros_like(s_ref)

        masks = _chunk_masks()

        def pair(pi, carry):
            rows = [pl.ds(pl.multiple_of((2 * pi + j) * CHUNK, CHUNK), CHUNK) for j in range(2)]
            loc = _units_local(_chunk_units(q_ref, k_ref, v_ref, bg_ref, rows[0])
                               + _chunk_units(q_ref, k_ref, v_ref, bg_ref, rows[1]), masks)
            states = [s_ref[h] for h in range(HEADS)]
            for j in range(2):
                lj = loc[HEADS * j:HEADS * (j + 1)]
                ws = [_mm(jnp.concatenate([p["w"], p["q_dec"]], axis=0), s) for p, s in zip(lj, states)]
                v_new = [p["u"] - b[:CHUNK] for p, b in zip(lj, ws)]
                intra = [_mm(p["qk"], vn) for p, vn in zip(lj, v_new)]
                upd = [_mm(p["k_dec"], vn, TN) for p, vn in zip(lj, v_new)]
                o_ref[rows[j], :] = jnp.concatenate([b[CHUNK:] + a for b, a in zip(ws, intra)], axis=1)
                for h in range(HEADS):
                    st_ref[2 * pi + j, h] = states[h]
                states = [p["gl"] * s + d for p, s, d in zip(lj, states, upd)]
            for h in range(HEADS):
                s_ref[h] = states[h]
            return carry

        lax.fori_loop(0, n_chunk // 2, pair, 0)

    tok = lambda w: pl.BlockSpec((tb, w), lambda i: (i, 0))
    return pl.pallas_call(
        body, name="delta_fwd", grid=(T // tb,),
        in_specs=[tok(DN_WIDTH), tok(DN_WIDTH), tok(DN_WIDTH), tok(LANES)],
        out_specs=[tok(DN_WIDTH), pl.BlockSpec((n_chunk, HEADS, HEAD_DIM, HEAD_DIM), lambda i: (i, 0, 0, 0))],
        out_shape=[jax.ShapeDtypeStruct((T, DN_WIDTH), F32),
                   jax.ShapeDtypeStruct((T // CHUNK, HEADS, HEAD_DIM, HEAD_DIM), F32)],
        scratch_shapes=[pltpu.VMEM((HEADS, HEAD_DIM, HEAD_DIM), F32)],
        compiler_params=_params(("arbitrary",)),
    )(q, k, v, bg)


def _dn_out(o, z, gn):
    outs, ohs, rs = [], [], []
    for hh in range(HEADS):
        oh = o[:, HEAD_DIM * hh:HEAD_DIM * (hh + 1)]
        r = lax.rsqrt(jnp.mean(oh * oh, axis=-1, keepdims=True) + EPS)
        ohs.append(oh * r)
        rs.append(r)
    sz = _sigmoid(z)
    oh = jnp.concatenate(ohs, axis=1)
    gn4 = jnp.concatenate([gn] * HEADS, axis=1)
    return oh * gn4 * (z * sz), oh, rs, sz, gn4


def _sc_fwd(sc_in, halo, cw, tb):
    xc = jnp.concatenate([halo, sc_in], axis=0)
    u = xc[:, SC_WIDTH:2 * SC_WIDTH] * xc[:, 2 * SC_WIDTH:]
    cv = _taps(u, cw, 3, tb, 6)
    gate_b = sc_in[:, :SC_WIDTH]
    y = gate_b * cv
    gw = SC_WIDTH // SC_GROUPS
    yhs, rs = [], []
    for gi in range(SC_GROUPS):
        yg = y[:, gw * gi:gw * (gi + 1)]
        r = lax.rsqrt(jnp.mean(yg * yg, axis=-1, keepdims=True) + EPS)
        yhs.append(yg * r)
        rs.append(r)
    return u, cv, gate_b, jnp.concatenate(yhs, axis=1), rs


def _shard_rows(land, first, rows):
    assert first % rows == 0 and land.shape[0] == N_CHIPS
    return pl.BlockSpec((N_CHIPS, rows, land.shape[2]), lambda i: (0, first // rows, 0))


def _whole(w_ref):
    n, rows, cols = w_ref.shape
    return w_ref[...].reshape(n * rows, cols)


def _mix_out(o, z, sc_in, x, land_a, gn, scw, gs):
    T = x.shape[0]
    tb = 256

    def body(o_ref, z_ref, sc_ref, halo_ref, x_ref, w_ref, gn_ref, scw_ref, gs_ref, x1_ref, mix_ref):
        o_n = _dn_out(o_ref[...], z_ref[...], gn_ref[...])[0]
        halo = jnp.where(pl.program_id(0) > 0, halo_ref[...], 0.0)
        yh = _sc_fwd(sc_ref[...], halo, scw_ref[...], tb)[3]
        mix = jnp.concatenate([o_n, yh * gs_ref[...]], axis=1).astype(BF16)
        x1_ref[...] = x_ref[...] + jnp.dot(mix, _whole(w_ref), preferred_element_type=F32)
        mix_ref[...] = mix

    tok = lambda w: pl.BlockSpec((tb, w), lambda i: (i, 0))
    full = lambda a: pl.BlockSpec(a.shape, lambda i: (0, 0))
    return pl.pallas_call(
        body, name="mix_out", grid=(T // tb,),
        in_specs=[tok(DN_WIDTH), tok(DN_WIDTH), tok(3 * SC_WIDTH), pl.BlockSpec((8, 3 * SC_WIDTH), _before_halo(tb)),
                  tok(D_MODEL), _shard_rows(land_a, A_OUT_AT, OUT_SHARD), full(gn), full(scw), full(gs)],
        out_specs=[tok(D_MODEL), tok(D_MODEL)],
        out_shape=[jax.ShapeDtypeStruct((T, D_MODEL), F32), jax.ShapeDtypeStruct((T, D_MODEL), BF16)],
        compiler_params=_params(("parallel",)),
    )(o, z, sc_in, sc_in, x, land_a, gn, scw, gs)


def _ffn(x1, g2, land_b):
    T = x1.shape[0]
    tb = 256

    def body(x_ref, g_ref, wgt_ref, wut_ref, wd_ref, x2_ref, a_ref, b_ref, h_ref):
        xv = x_ref[...]
        r = lax.rsqrt(jnp.mean(xv * xv, axis=-1, keepdims=True) + EPS)
        h = (xv * r * g_ref[...]).astype(BF16)
        a = lax.dot_general(h, _whole(wgt_ref), NT, preferred_element_type=F32)
        b = lax.dot_general(h, _whole(wut_ref), NT, preferred_element_type=F32)
        act = (a * _sigmoid(a) * b).astype(BF16)
        x2_ref[...] = xv + jnp.dot(act, _whole(wd_ref), preferred_element_type=F32)
        a_ref[...] = a.astype(BF16)
        b_ref[...] = b.astype(BF16)
        h_ref[...] = h

    tok = lambda w: pl.BlockSpec((tb, w), lambda i: (i, 0))
    return pl.pallas_call(
        body, name="ffn", grid=(T // tb,),
        in_specs=[tok(D_MODEL), pl.BlockSpec(g2.shape, lambda i: (0, 0)), _shard_rows(land_b, 0, FF_SHARD),
                  _shard_rows(land_b, FF_SHARD, FF_SHARD), _shard_rows(land_b, 2 * FF_SHARD, FF_SHARD)],
        out_specs=[tok(D_MODEL), tok(D_FF), tok(D_FF), tok(D_MODEL)],
        out_shape=[jax.ShapeDtypeStruct((T, D_MODEL), F32), jax.ShapeDtypeStruct((T, D_FF), BF16),
                   jax.ShapeDtypeStruct((T, D_FF), BF16), jax.ShapeDtypeStruct((T, D_MODEL), BF16)],
        compiler_params=_params(("parallel",)),
    )(x1, g2, land_b, land_b, land_b)


def _loss_head(x, gf, target):
    T = x.shape[0]
    tb = 512

    def body(x_ref, g_ref, t_ref, dx_ref, dxb_ref, loss_ref, dg_ref):
        @pl.when(pl.program_id(0) == 0)
        def _():
            loss_ref[...] = jnp.zeros_like(loss_ref)
            dg_ref[...] = jnp.zeros_like(dg_ref)

        xv = x_ref[...]
        r = lax.rsqrt(jnp.mean(xv * xv, axis=-1, keepdims=True) + EPS)
        xh = xv * r
        err = xh * g_ref[...] - t_ref[...]
        per_tok = jnp.mean(err * err, axis=-1, keepdims=True)
        loss_ref[...] += 0.5 * jnp.sum(per_tok, axis=0, keepdims=True)
        dy = err * (1.0 / D_MODEL)
        _row_acc(dg_ref, dy * xh)
        dx = _rms_bwd(dy, xh, r, g_ref[...])
        dx_ref[...] = dx
        dxb_ref[...] = dx.astype(BF16)

    tok = pl.BlockSpec((tb, D_MODEL), lambda i: (i, 0))
    return pl.pallas_call(
        body, name="loss_head", grid=(T // tb,),
        in_specs=[tok, pl.BlockSpec(gf.shape, lambda i: (0, 0)), tok],
        out_specs=[tok, tok, pl.BlockSpec((8, LANES), lambda i: (0, 0)), pl.BlockSpec((8, D_MODEL), lambda i: (0, 0))],
        out_shape=[jax.ShapeDtypeStruct((T, D_MODEL), F32), jax.ShapeDtypeStruct((T, D_MODEL), BF16),
                   jax.ShapeDtypeStruct((8, LANES), F32), jax.ShapeDtypeStruct((8, D_MODEL), F32)],
        compiler_params=_params(("arbitrary",)),
    )(x, gf, target)


def _ffn_bwd(dx2, x1, a, b, g2, land_b):
    T = x1.shape[0]
    tb = 256

    def body(dx2_ref, x_ref, a_ref, b_ref, g_ref, wgt_ref, wut_ref, wd_ref,
             dx1_ref, dx1b_ref, da_ref, db_ref, act_ref, dg_ref):
        @pl.when(pl.program_id(0) == 0)
        def _():
            dg_ref[...] = jnp.zeros_like(dg_ref)

        dx2v = dx2_ref[...]
        av = a_ref[...].astype(F32)
        bv = b_ref[...].astype(F32)
        dact = _mm(dx2v, _whole(wd_ref), NT)
        sa = _sigmoid(av)
        silu = av * sa
        da = (dact * bv * (sa * (1.0 + av * (1.0 - sa)))).astype(BF16)
        db = (dact * silu).astype(BF16)
        dh = _mm(da, _whole(wgt_ref)) + _mm(db, _whole(wut_ref))
        xv = x_ref[...]
        r = lax.rsqrt(jnp.mean(xv * xv, axis=-1, keepdims=True) + EPS)
        xh = xv * r
        _row_acc(dg_ref, dh * xh)
        dx1 = dx2v + _rms_bwd(dh, xh, r, g_ref[...])
        dx1_ref[...] = dx1
        dx1b_ref[...] = dx1.astype(BF16)
        da_ref[...] = da
        db_ref[...] = db
        act_ref[...] = (silu * bv).astype(BF16)

    tok = lambda w: pl.BlockSpec((tb, w), lambda i: (i, 0))
    return pl.pallas_call(
        body, name="ffn_bwd", grid=(T // tb,),
        in_specs=[tok(D_MODEL), tok(D_MODEL), tok(D_FF), tok(D_FF), pl.BlockSpec(g2.shape, lambda i: (0, 0)),
                  _shard_rows(land_b, 0, FF_SHARD), _shard_rows(land_b, FF_SHARD, FF_SHARD),
                  _shard_rows(land_b, 2 * FF_SHARD, FF_SHARD)],
        out_specs=[tok(D_MODEL), tok(D_MODEL), tok(D_FF), tok(D_FF), tok(D_FF), pl.BlockSpec((8, D_MODEL), lambda i: (0, 0))],
        out_shape=[jax.ShapeDtypeStruct((T, D_MODEL), F32), jax.ShapeDtypeStruct((T, D_MODEL), BF16)]
        + [jax.ShapeDtypeStruct((T, D_FF), BF16)] * 3 + [jax.ShapeDtypeStruct((8, D_MODEL), F32)],
        compiler_params=_params(("arbitrary",)),
    )(dx2, x1, a, b, g2, land_b, land_b, land_b)


def _wgrad_share(a, b, parts, first, name):
    T = b.shape[0]
    rows = a.shape[1] // N_CHIPS
    assert first % rows == 0 and b.shape[1] == parts.shape[2]
    bk = min(T, 1024)
    n_k = T // bk
    group = 2
    assert (group * rows) % LANES == 0

    def body(a_ref, b_ref, parts_ref, o_ref, acc_ref):
        kk = pl.program_id(1)

        @pl.when(kk == 0)
        def _():
            acc_ref[...] = jnp.zeros_like(acc_ref)

        acc_ref[...] += lax.dot_general(a_ref[...], b_ref[...], TN, preferred_element_type=F32)

        @pl.when(kk == n_k - 1)
        def _():
            for s in range(group):
                o_ref[s] = acc_ref[rows * s:rows * (s + 1), :].astype(BF16)

    return pl.pallas_call(
        body, name=name, grid=(N_CHIPS // group, n_k),
        in_specs=[pl.BlockSpec((bk, group * rows), lambda i, kk: (kk, i)),
                  pl.BlockSpec((bk, b.shape[1]), lambda i, kk: (kk, 0)), _ANY],
        out_specs=pl.BlockSpec((group, rows, b.shape[1]), lambda i, kk: (i, first // rows, 0)),
        out_shape=jax.ShapeDtypeStruct(parts.shape, BF16),
        scratch_shapes=[pltpu.VMEM((group * rows, b.shape[1]), F32)],
        input_output_aliases={2: 0},
        compiler_params=_params(("parallel", "arbitrary")),
    )(a, b, parts)


def _mix_out_bwd(dx1, o, z, sc_in, land_a, gn, scw, gs, dp):
    T = dx1.shape[0]
    tb = 256

    def body(dx_ref, o_ref, z_ref, sc_ref, halo_ref, w_ref, gn_ref, scw_ref, gs_ref, dp_ref,
             do_ref, dz_ref, dgb_ref, dcv_ref, dgn_ref, dgs_ref, dscw_ref):
        @pl.when(pl.program_id(0) == 0)
        def _():
            dgn_ref[...] = jnp.zeros_like(dgn_ref)
            dgs_ref[...] = jnp.zeros_like(dgs_ref)
            dscw_ref[...] = jnp.zeros_like(dscw_ref)

        dmix = _mm(dx_ref[...], _whole(w_ref), NT)
        don = dmix[:, :DN_WIDTH]
        dosc = dmix[:, DN_WIDTH:]
        zv = z_ref[...]
        _, oh, rs, sz, gn4 = _dn_out(o_ref[...], zv, gn_ref[...])
        silu_z = zv * sz
        dgn_full = don * oh * silu_z
        dgn_ref[0:1, :] += jnp.sum(sum(dgn_full[:, HEAD_DIM * hh:HEAD_DIM * (hh + 1)] for hh in range(HEADS)),
                                   axis=0, keepdims=True)
        dz_ref[...] = (don * oh * gn4 * (sz * (1.0 + zv * (1.0 - sz)))).astype(BF16)
        t = don * gn4 * silu_z
        for hh in range(HEADS):
            sl = slice(HEAD_DIM * hh, HEAD_DIM * (hh + 1))
            th, ohh = t[:, sl], oh[:, sl]
            do_ref[:, sl] = rs[hh] * (th - ohh * jnp.mean(th * ohh, axis=-1, keepdims=True))
        halo = jnp.where(pl.program_id(0) > 0, halo_ref[...], 0.0)
        u, cv, gate_b, yh, rys = _sc_fwd(sc_ref[...], halo, scw_ref[...], tb)
        _row_acc(dgs_ref, dosc * yh)
        ty = dosc * gs_ref[...]
        gw = SC_WIDTH // SC_GROUPS
        dys = []
        for gi in range(SC_GROUPS):
            sl = slice(gw * gi, gw * (gi + 1))
            tg, yg = ty[:, sl], yh[:, sl]
            dys.append(rys[gi] * (tg - yg * jnp.mean(tg * yg, axis=-1, keepdims=True)))
        dy = jnp.concatenate(dys, axis=1)
        dgb_ref[...] = dy * cv
        dcv = dy * gate_b
        dcv_ref[...] = dcv
        for j in range(3):
            dscw_ref[j:j + 1, :] += jnp.sum(dcv * _rows_from(u, 6 + j, tb), axis=0, keepdims=True)

    tok = lambda w: pl.BlockSpec((tb, w), lambda i: (i, 0))
    full = lambda t: pl.BlockSpec(t.shape, lambda i: (0, 0))
    acc = lambda w: pl.BlockSpec((8, w), lambda i: (0, 0))
    return pl.pallas_call(
        body, name="mix_out_bwd", grid=(T // tb,),
        in_specs=[tok(D_MODEL), tok(DN_WIDTH), tok(DN_WIDTH), tok(3 * SC_WIDTH),
                  pl.BlockSpec((8, 3 * SC_WIDTH), _before_halo(tb)), _shard_rows(land_a, A_OUT_AT, OUT_SHARD),
                  full(gn), full(scw), full(gs), _ANY],
        out_specs=[tok(DN_WIDTH), _dp_block(tb, P_Z, DN_WIDTH), tok(SC_WIDTH), tok(SC_WIDTH),
                   acc(HEAD_DIM), acc(SC_WIDTH), acc(SC_WIDTH)],
        out_shape=[jax.ShapeDtypeStruct((T, DN_WIDTH), F32), jax.ShapeDtypeStruct(dp.shape, BF16),
                   jax.ShapeDtypeStruct((T, SC_WIDTH), F32), jax.ShapeDtypeStruct((T, SC_WIDTH), F32),
                   jax.ShapeDtypeStruct((8, HEAD_DIM), F32), jax.ShapeDtypeStruct((8, SC_WIDTH), F32),
                   jax.ShapeDtypeStruct((8, SC_WIDTH), F32)],
        input_output_aliases={9: 1},
        compiler_params=_params(("arbitrary",)),
    )(dx1, o, z, sc_in, sc_in, land_a, gn, scw, gs, dp)


def _sc_conv_bwd(dcv, dgb, sc_in, scw, dp):
    T = dcv.shape[0]
    tb = 512

    def body(dcv_ref, halo_ref, dgb_ref, sc_ref, w_ref, dp_ref, out_ref):
        last = pl.program_id(0) == pl.num_programs(0) - 1
        halo = jnp.where(last, 0.0, halo_ref[...])
        xc = jnp.concatenate([dcv_ref[...], halo], axis=0)
        w = w_ref[...]
        du = w[2:3, :] * xc[0:tb, :] + w[1:2, :] * _rows_from(xc, 1, tb) + w[0:1, :] * _rows_from(xc, 2, tb)
        sc = sc_ref[...]
        out_ref[:, :SC_WIDTH] = dgb_ref[...].astype(BF16)
        out_ref[:, SC_WIDTH:2 * SC_WIDTH] = (du * sc[:, 2 * SC_WIDTH:]).astype(BF16)
        out_ref[:, 2 * SC_WIDTH:] = (du * sc[:, SC_WIDTH:2 * SC_WIDTH]).astype(BF16)

    tok = lambda w: pl.BlockSpec((tb, w), lambda i: (i, 0))
    return pl.pallas_call(
        body, name="sc_conv_bwd", grid=(T // tb,),
        in_specs=[tok(SC_WIDTH), pl.BlockSpec((8, SC_WIDTH), _after_halo(tb, T)), tok(SC_WIDTH), tok(3 * SC_WIDTH),
                  pl.BlockSpec(scw.shape, lambda i: (0, 0)), _ANY],
        out_specs=_dp_block(tb, P_SC, 3 * SC_WIDTH),
        out_shape=jax.ShapeDtypeStruct(dp.shape, BF16),
        input_output_aliases={5: 0},
        compiler_params=_params(("parallel",)),
    )(dcv, dcv, dgb, sc_in, scw, dp)


def _delta_bwd(q, k, v, bg, states, do):
    T = q.shape[0]
    tb = 512
    n_chunk = tb // CHUNK
    nb = T // tb

    def body(q_ref, k_ref, v_ref, bg_ref, st_ref, do_ref, dq_ref, dk_ref, dv_ref, dbg_ref, ds_ref):
        @pl.when(pl.program_id(0) == 0)
        def _():
            ds_ref[...] = jnp.zeros_like(ds_ref)

        masks = _chunk_masks()
        causal, strict = masks
        lane = lax.broadcasted_iota(jnp.int32, (CHUNK, LANES), 1)
        last_row = lax.broadcasted_iota(jnp.int32, (CHUNK, 1), 0) == CHUNK - 1
        cat = jnp.concatenate
        heads = range(HEADS)

        def open_chunk(ci, loc):
            rows = pl.ds(pl.multiple_of(ci * CHUNK, CHUNK), CHUNK)
            dov = do_ref[rows, :]
            return dict(rows=rows, loc=loc, do=[dov[:, HEAD_DIM * h:HEAD_DIM * (h + 1)] for h in heads],
                        state=[st_ref[ci, h] for h in heads])

        def a_free(c):
            loc, do, state = c["loc"], c["do"], c["state"]
            w_s = [_mm(p["w"], s) for p, s in zip(loc, state)]
            c["dq_dec"] = [_mm(d, s, NT) for d, s in zip(do, state)]
            c["qk_do"] = [_mm(p["qk"], d, TN) for p, d in zip(loc, do)]
            c["qd_do"] = [_mm(p["q_dec"], d, TN) for p, d in zip(loc, do)]
            c["v_new"] = [p["u"] - t for p, t in zip(loc, w_s)]
            c["dqk"] = [jnp.where(causal, _mm(d, vn, NT), 0.0) for d, vn in zip(do, c["v_new"])]

        def a_state(c, ds_next):
            c["ds_next"] = ds_next
            kd_ds = [_mm(p["k_dec"], d) for p, d in zip(c["loc"], ds_next)]
            c["dk_dec"] = [_mm(vn, d, NT) for vn, d in zip(c["v_new"], ds_next)]
            c["dv_new"] = [a + b for a, b in zip(c["qk_do"], kd_ds)]

        def b_state(c):
            loc = c["loc"]
            w_dv = [_mm(p["w"], dvn, TN) for p, dvn in zip(loc, c["dv_new"])]
            c["dw"] = [-_mm(dvn, s, NT) for dvn, s in zip(c["dv_new"], c["state"])]
            return [loc[h]["gl"] * c["ds_next"][h] + c["qd_do"][h] - w_dv[h] for h in heads]

        def c_solve(c):
            loc, dv_new, dw = c["loc"], c["dv_new"], c["dw"]
            c["dtm"] = [_mm(cat([dvn, d], axis=1), cat([p["vb"], p["kbg"]], axis=1), NT) for dvn, d, p in zip(dv_new, dw, loc)]
            x_t = [_mm(p["xm"], cat([dvn, d], axis=1), TN) for p, dvn, d in zip(loc, dv_new, dw)]
            c["dvb"] = [dvn + t[:, :HEAD_DIM] for dvn, t in zip(dv_new, x_t)]
            c["dkbg"] = [d + t[:, HEAD_DIM:] for d, t in zip(dw, x_t)]

        def d_solve(c):
            c["y"] = [t + _mm(p["xm"], t, TN) for p, t in zip(c["loc"], c["dtm"])]

        def e_solve(c):
            c["dlow"] = [jnp.where(strict, -(t + _mm(t, p["xm"], NT)), 0.0) for p, t in zip(c["loc"], c["y"])]

        def f_close(c):
            loc, rows = c["loc"], c["rows"]
            dmm = [d * p["decay"] for d, p in zip(c["dlow"], loc)]
            dnn = [d * p["decay"] for d, p in zip(c["dqk"], loc)]
            by_k = [_mm(cat([a, b], axis=0), p["k"]) for a, b, p in zip(dmm, dnn, loc)]
            dk_mm = [_mm(cat([a, b], axis=0), cat([p["kb"], p["q"]], axis=0), TN) for a, b, p in zip(dmm, dnn, loc)]
            dq_out, dk_out, dv_out = [], [], []
            dbeta_all = jnp.zeros((CHUNK, LANES), F32)
            dgc_all = jnp.zeros((CHUNK, LANES), F32)
            for h in heads:
                p = loc[h]
                dkb = by_k[h][:CHUNK] + c["dkbg"][h] * p["eg"]
                dq_out.append(by_k[h][CHUNK:] + c["dq_dec"][h] * p["eg"])
                dk_out.append(dk_mm[h] + c["dk_dec"][h] * p["ek"] + dkb * p["beta"])
                dv_out.append(c["dvb"][h] * p["beta"])
                dbeta = jnp.sum(dkb * p["k"] + c["dvb"][h] * p["v"], axis=1, keepdims=True)
                e = c["dlow"][h] * p["low"] + c["dqk"][h] * p["qk"]
                kd = jnp.sum(c["dk_dec"][h] * p["k_dec"], axis=1, keepdims=True)
                dgc = (jnp.sum(e, axis=1, keepdims=True) - jnp.sum(e.T, axis=1, keepdims=True)
                       + jnp.sum(c["dq_dec"][h] * p["q_dec"], axis=1, keepdims=True) - kd
                       + jnp.sum(c["dkbg"][h] * p["kbg"], axis=1, keepdims=True))
                dgl = jnp.sum(jnp.sum(c["ds_next"][h] * c["state"][h], axis=1, keepdims=True), axis=0, keepdims=True)
                d_last = jnp.sum(kd, axis=0, keepdims=True) + dgl * p["gl"]
                dgc = dgc + jnp.where(last_row, d_last, 0.0)
                dbeta_all = jnp.where(lane == h, dbeta, dbeta_all)
                dgc_all = jnp.where(lane == h + HEADS, dgc, dgc_all)
            dq_ref[rows, :] = cat(dq_out, axis=1)
            dk_ref[rows, :] = cat(dk_out, axis=1)
            dv_ref[rows, :] = cat(dv_out, axis=1)
            dbg_ref[rows, :] = dbeta_all + dgc_all

        def pair(pj, carry):
            hi = n_chunk - 1 - 2 * pj
            lo = hi - 1
            rows = [pl.ds(pl.multiple_of(ci * CHUNK, CHUNK), CHUNK) for ci in (hi, lo)]
            loc = _units_local(_chunk_units(q_ref, k_ref, v_ref, bg_ref, rows[0])
                               + _chunk_units(q_ref, k_ref, v_ref, bg_ref, rows[1]), masks)
            c_hi, c_lo = open_chunk(hi, loc[:HEADS]), open_chunk(lo, loc[HEADS:])
            a_free(c_hi)
            a_free(c_lo)
            a_state(c_hi, [ds_ref[h] for h in heads])
            ds_mid = b_state(c_hi)
            a_state(c_lo, ds_mid)
            c_solve(c_hi)
            ds_out = b_state(c_lo)
            for h in heads:
                ds_ref[h] = ds_out[h]
            d_solve(c_hi)
            c_solve(c_lo)
            e_solve(c_hi)
            d_solve(c_lo)
            f_close(c_hi)
            e_solve(c_lo)
            f_close(c_lo)
            return carry

        lax.fori_loop(0, n_chunk // 2, pair, 0)

    tok = lambda w: pl.BlockSpec((tb, w), lambda i: (nb - 1 - i, 0))
    return pl.pallas_call(
        body, name="delta_bwd", grid=(nb,),
        in_specs=[tok(DN_WIDTH), tok(DN_WIDTH), tok(DN_WIDTH), tok(LANES),
                  pl.BlockSpec((n_chunk, HEADS, HEAD_DIM, HEAD_DIM), lambda i: (nb - 1 - i, 0, 0, 0)), tok(DN_WIDTH)],
        out_specs=[tok(DN_WIDTH), tok(DN_WIDTH), tok(DN_WIDTH), tok(LANES)],
        out_shape=[jax.ShapeDtypeStruct((T, DN_WIDTH), F32)] * 3 + [jax.ShapeDtypeStruct((T, LANES), F32)],
        scratch_shapes=[pltpu.VMEM((HEADS, HEAD_DIM, HEAD_DIM), F32)],
        compiler_params=_params(("arbitrary",)),
    )(q, k, v, bg, states, do)


def _dn_prep_bwd(dq, dk, dv, dbg, qkv, cw, bd, al_row, dt_row, dp):
    T = qkv.shape[0]
    tb = 256

    def body(dq_ref, dk_ref, dv_ref, dbg_ref, pre_ref, halo_ref, cw_ref, bd_ref, al_ref, dt_ref, dp_ref,
             dc_ref, dbd_ref, dcw_ref, dal_ref, ddt_ref):
        @pl.when(pl.program_id(0) == 0)
        def _():
            dcw_ref[...] = jnp.zeros_like(dcw_ref)
            dal_ref[...] = jnp.zeros_like(dal_ref)
            ddt_ref[...] = jnp.zeros_like(ddt_ref)

        halo = jnp.where(pl.program_id(0) > 0, halo_ref[...], 0.0)
        xc, c, sg, a = _dn_act(pre_ref[...], halo, cw_ref[...], tb)
        dsilu = sg * (1.0 + c * (1.0 - sg))
        for hh in range(HEADS):
            sl = slice(HEAD_DIM * hh, HEAD_DIM * (hh + 1))
            for base, g_ref, scale in ((0, dq_ref, Q_SCALE), (DN_WIDTH, dk_ref, 1.0)):
                sa = slice(base + HEAD_DIM * hh, base + HEAD_DIM * (hh + 1))
                raw = a[:, sa]
                r = lax.rsqrt(jnp.sum(raw * raw, axis=-1, keepdims=True) + EPS)
                nrm = raw * r
                gn_ = g_ref[:, sl] * scale
                dc_ref[:, sa] = r * (gn_ - nrm * jnp.sum(gn_ * nrm, axis=-1, keepdims=True)) * dsilu[:, sa]
        dc_ref[:, 2 * DN_WIDTH:] = dv_ref[...] * dsilu[:, 2 * DN_WIDTH:]
        dc = dc_ref[...]
        for j in range(4):
            dcw_ref[j:j + 1, :] += jnp.sum(dc * _rows_from(xc, 5 + j, tb), axis=0, keepdims=True)
        bdv = bd_ref[...]
        lane = lax.broadcasted_iota(jnp.int32, bdv.shape, 1)
        is_b = lane < HEADS
        dbg_in = dbg_ref[...]
        dbgv = jnp.where(is_b, dbg_in, _mm32(_chunk_cumsum_matrix(tb), dbg_in, TN))
        is_g = jnp.logical_and(lane >= HEADS, lane < 2 * HEADS)
        beta = _sigmoid(bdv)
        neg_a = -jnp.exp(al_ref[...])
        pre_sp = bdv + dt_ref[...]
        g = neg_a * _softplus(pre_sp)
        da_in = dbgv * neg_a * _sigmoid(pre_sp)
        dbd_ref[...] = jnp.where(is_b, dbgv * beta * (1.0 - beta), jnp.where(is_g, da_in, 0.0)).astype(BF16)
        _row_acc(dal_ref, jnp.where(is_g, dbgv * g, 0.0))
        _row_acc(ddt_ref, jnp.where(is_g, da_in, 0.0))

    tok = lambda w: pl.BlockSpec((tb, w), lambda i: (i, 0))
    full = lambda t: pl.BlockSpec(t.shape, lambda i: (0, 0))
    acc = lambda w: pl.BlockSpec((8, w), lambda i: (0, 0))
    return pl.pallas_call(
        body, name="dn_prep_bwd", grid=(T // tb,),
        in_specs=[tok(DN_WIDTH), tok(DN_WIDTH), tok(DN_WIDTH), tok(LANES),
                  tok(QKV), pl.BlockSpec((8, QKV), _before_halo(tb)), full(cw), tok(LANES), full(al_row), full(dt_row), _ANY],
        out_specs=[tok(QKV), _dp_block(tb, P_BD, LANES), acc(QKV), acc(LANES), acc(LANES)],
        out_shape=[jax.ShapeDtypeStruct((T, QKV), F32), jax.ShapeDtypeStruct(dp.shape, BF16),
                   jax.ShapeDtypeStruct((8, QKV), F32), jax.ShapeDtypeStruct((8, LANES), F32),
                   jax.ShapeDtypeStruct((8, LANES), F32)],
        input_output_aliases={10: 1},
        compiler_params=_params(("arbitrary",)),
    )(dq, dk, dv, dbg, qkv, qkv, cw, bd, al_row, dt_row, dp)


def _dn_conv_bwd(dc, cw, dp):
    T = dc.shape[0]
    tb = 512

    def body(dc_ref, halo_ref, w_ref, dp_ref, out_ref):
        last = pl.program_id(0) == pl.num_programs(0) - 1
        halo = jnp.where(last, 0.0, halo_ref[...])
        xc = jnp.concatenate([dc_ref[...], halo], axis=0)
        w = w_ref[...]
        acc = w[3:4, :] * xc[0:tb, :]
        for j in range(3):
            acc = acc + w[j:j + 1, :] * _rows_from(xc, 3 - j, tb)
        out_ref[...] = acc.astype(BF16)

    tok = pl.BlockSpec((tb, QKV), lambda i: (i, 0))
    return pl.pallas_call(
        body, name="dn_conv_bwd", grid=(T // tb,),
        in_specs=[tok, pl.BlockSpec((8, QKV), _after_halo(tb, T)), pl.BlockSpec(cw.shape, lambda i: (0, 0)), _ANY],
        out_specs=_dp_block(tb, 0, QKV),
        out_shape=jax.ShapeDtypeStruct(dp.shape, BF16),
        input_output_aliases={3: 0},
        compiler_params=_params(("parallel",)),
    )(dc, dc, cw, dp)


def _dp_of_chip(dp, s):
    lo, hi = IN_SHARD * s, IN_SHARD * (s + 1)
    pieces = []
    for w_at, w_end, p_at in ((0, W_Z, 0), (W_Z, W_BD, P_Z), (W_BD, W_SC, P_BD), (W_SC, W_IN_COLS, P_SC)):
        a, b = max(lo, w_at), min(hi, w_end)
        if a < b:
            pieces.append(dp[:, p_at + a - w_at:p_at + b - w_at])
    pieces.append(jnp.zeros((dp.shape[0], D_MODEL - IN_SHARD), dp.dtype))
    return jnp.concatenate(pieces, axis=1)


def _in_proj_bwd(dp, dx1, x, g1, land_a):
    T = x.shape[0]
    tb = 256

    def body(dp_ref, dx1_ref, x_ref, g_ref, w_ref, dx_ref, dxb_ref, dps_ref, dg_ref):
        @pl.when(pl.program_id(0) == 0)
        def _():
            dg_ref[...] = jnp.zeros_like(dg_ref)

        dpv = dp_ref[...]
        dh = jnp.zeros((tb, D_MODEL), F32)
        for s in range(N_CHIPS):
            dps = _dp_of_chip(dpv, s)
            dps_ref[:, D_MODEL * s:D_MODEL * (s + 1)] = dps
            dh = dh + lax.dot_general(dps, w_ref[s], NT, preferred_element_type=F32)
        xv = x_ref[...]
        r = lax.rsqrt(jnp.mean(xv * xv, axis=-1, keepdims=True) + EPS)
        xh = xv * r
        _row_acc(dg_ref, dh * xh)
        dx = dx1_ref[...] + _rms_bwd(dh, xh, r, g_ref[...])
        dx_ref[...] = dx
        dxb_ref[...] = dx.astype(BF16)

    tok = lambda w: pl.BlockSpec((tb, w), lambda i: (i, 0))
    return pl.pallas_call(
        body, name="in_proj_bwd", grid=(T // tb,),
        in_specs=[tok(P_COLS), tok(D_MODEL), tok(D_MODEL), pl.BlockSpec(g1.shape, lambda i: (0, 0)),
                  _shard_rows(land_a, 0, D_MODEL)],
        out_specs=[tok(D_MODEL), tok(D_MODEL), tok(N_CHIPS * D_MODEL), pl.BlockSpec((8, D_MODEL), lambda i: (0, 0))],
        out_shape=[jax.ShapeDtypeStruct((T, D_MODEL), F32), jax.ShapeDtypeStruct((T, D_MODEL), BF16),
                   jax.ShapeDtypeStruct((T, N_CHIPS * D_MODEL), BF16), jax.ShapeDtypeStruct((8, D_MODEL), F32)],
        compiler_params=_params(("arbitrary",)),
    )(dp, dx1, x, g1, land_a)


def _wgrad_in_share(h, dps, parts, name):
    T = h.shape[0]
    bk = min(T, 1024)
    n_k = T // bk

    def body(a_ref, b_ref, parts_ref, o_ref, acc_ref):
        kk = pl.program_id(1)

        @pl.when(kk == 0)
        def _():
            acc_ref[...] = jnp.zeros_like(acc_ref)

        acc_ref[...] += lax.dot_general(a_ref[...], b_ref[...], TN, preferred_element_type=F32)

        @pl.when(kk == n_k - 1)
        def _():
            o_ref[0] = acc_ref[...].astype(BF16)

    return pl.pallas_call(
        body, name=name, grid=(N_CHIPS, n_k),
        in_specs=[pl.BlockSpec((bk, D_MODEL), lambda j, kk: (kk, 0)), pl.BlockSpec((bk, D_MODEL), lambda j, kk: (kk, j)), _ANY],
        out_specs=pl.BlockSpec((1, D_MODEL, D_MODEL), lambda j, kk: (j, 0, 0)),
        out_shape=jax.ShapeDtypeStruct(parts.shape, BF16),
        scratch_shapes=[pltpu.VMEM((D_MODEL, D_MODEL), F32)],
        input_output_aliases={2: 0},
        compiler_params=_params(("parallel", "arbitrary")),
    )(h, dps, parts)


def _pad_rows(a, rows=8):
    return jnp.pad(a, ((0, rows - a.shape[0]), (0, 0)))


def _gate_rows(a_log, dt_bias):
    put = lambda t: jnp.pad(t.reshape(1, HEADS), ((0, 0), (HEADS, LANES - 2 * HEADS)))
    return put(a_log), put(dt_bias)


def _mixer_fwd(x, p):
    qkv, z, sc_in, bd, h = _in_proj(x, p["g1"], p["land_a"])
    q, k, v, bg = _dn_prep(qkv, p["cw"], bd, p["al"], p["dt"])
    o, states = _delta_fwd(q, k, v, bg)
    x1, mix = _mix_out(o, z, sc_in, x, p["land_a"], p["gn"], p["scw"], p["gs"])
    return x1, dict(x=x, qkv=qkv, z=z, sc_in=sc_in, bd=bd, h=h, q=q, k=k, v=v, bg=bg, o=o, states=states, mix=mix)


def _ffn_fwd(x1, p, land_b):
    x2, a, b, h2 = _ffn(x1, p["g2"], land_b)
    return x2, dict(x1=x1, a=a, b=b, h2=h2)


def _ffn_back(dx2, dx2_bf16, s, p, land_b):
    dx1, dx1_bf16, da, db, act, dg2 = _ffn_bwd(dx2, s["x1"], s["a"], s["b"], p["g2"], land_b)
    parts = lax.empty((N_CHIPS, B_ROWS, D_MODEL), BF16)
    parts = _wgrad_share(act, dx2_bf16, parts, 2 * FF_SHARD, "wgrad_down")
    parts = _wgrad_share(da, s["h2"], parts, 0, "wgrad_gate")
    parts = _wgrad_share(db, s["h2"], parts, FF_SHARD, "wgrad_up")
    return dx1, dx1_bf16, parts, dg2[0]


def _mixer_bwd(dx1, dx1_bf16, s, p):
    dp = lax.empty((dx1.shape[0], P_COLS), BF16)
    do, dp, dgb, dcv, dgn, dgs, dscw = _mix_out_bwd(dx1, s["o"], s["z"], s["sc_in"], p["land_a"], p["gn"], p["scw"], p["gs"], dp)
    dp = _sc_conv_bwd(dcv, dgb, s["sc_in"], p["scw"], dp)
    dq, dk, dv, dbg = _delta_bwd(s["q"], s["k"], s["v"], s["bg"], s["states"], do)
    dc, dp, dcw, dal, ddt = _dn_prep_bwd(dq, dk, dv, dbg, s["qkv"], p["cw"], s["bd"], p["al"], p["dt"], dp)
    dp = _dn_conv_bwd(dc, p["cw"], dp)
    dx, dx_bf16, dps, dg1 = _in_proj_bwd(dp, dx1, s["x"], p["g1"], p["land_a"])
    parts = lax.empty((N_CHIPS, A_ROWS, D_MODEL), BF16)
    parts = _wgrad_in_share(s["h"], dps, parts, "wgrad_in")
    parts = _wgrad_share(s["mix"], dx1_bf16, parts, A_OUT_AT, "wgrad_out")
    g = dict(g1=dg1[0], gn=dgn[0], gs=dgs[0], scw=dscw[:3], cw=dcw[:4], al=dal[0, HEADS:2 * HEADS], dt=ddt[0, HEADS:2 * HEADS])
    return dx, dx_bf16, parts, g


def _place():
    return lax.axis_index("x"), lax.axis_index("y"), lax.axis_index("c")


def _other_chips(x, y):
    return [(1 - x, y), (x, 1 - y), (1 - x, 1 - y)]


_HBM = pl.BlockSpec(memory_space=pltpu.HBM)


def _chip_exchange(arrs, name, gather):
    n = len(arrs)

    def body(*refs):
        ins, outs = refs[:n], refs[n:2 * n]
        send_sems, recv_sems, local_sems = refs[2 * n:]
        x, y, c = _place()
        me = 2 * x + y
        others = _other_chips(x, y)

        def remote(k, j, landing):
            px, py = others[j]
            src = ins[k] if gather else ins[k].at[2 * px + py]
            return pltpu.make_async_remote_copy(src_ref=src, dst_ref=outs[k].at[landing], send_sem=send_sems.at[k, j],
                                                recv_sem=recv_sems.at[k, j], device_id=(px, py, c), device_id_type=MESH)

        local = [pltpu.make_async_copy(ins[k] if gather else ins[k].at[me], outs[k].at[me], local_sems.at[k])
                 for k in range(n)]
        sends = [remote(k, j, me) for k in range(n) for j in range(3)]
        for cp in local + sends:
            cp.start()
        for k in range(n):
            for j, (px, py) in enumerate(others):
                remote(k, j, 2 * px + py).wait_recv()
        for cp in sends:
            cp.wait_send()
        for cp in local:
            cp.wait()

    shapes = [jax.ShapeDtypeStruct(((N_CHIPS,) + a.shape) if gather else a.shape, a.dtype) for a in arrs]
    return pl.pallas_call(
        body, name=name, in_specs=[_HBM] * n, out_specs=[_HBM] * n, out_shape=shapes,
        scratch_shapes=[pltpu.SemaphoreType.DMA((n, 3)), pltpu.SemaphoreType.DMA((n, 3)), pltpu.SemaphoreType.DMA((n,))],
    )(*arrs)


_SEM = pl.BlockSpec(memory_space=pltpu.SEMAPHORE)
_ANY = pl.BlockSpec(memory_space=pl.ANY)
_EFFECT = pltpu.SideEffectType.DATAFLOW_SIDE_EFFECTING


_FLIPS = [(a, b, cc) for a in (0, 1) for b in (0, 1) for cc in (0, 1)][1:]


def _split_copies(src_ref, land_ref, send_sems, recv_sems, gather, sending):
    x, y, c = _place()
    copies = []
    if gather:
        me = 2 * x + y
        for j, (px, py) in enumerate(_other_chips(x, y)):
            copies.append(pltpu.make_async_remote_copy(
                src_ref=src_ref, dst_ref=land_ref.at[me if sending else 2 * px + py],
                send_sem=send_sems.at[j], recv_sem=recv_sems.at[j], device_id=(px, py, c), device_id_type=MESH))
        return copies
    me = 4 * x + 2 * y + c
    for j, (a, b, cc) in enumerate(_FLIPS):
        px, py, pc = (1 - x) if a else x, (1 - y) if b else y, (1 - c) if cc else c
        copies.append(pltpu.make_async_remote_copy(
            src_ref=src_ref.at[2 * px + py], dst_ref=land_ref.at[me if sending else 4 * px + 2 * py + pc],
            send_sem=send_sems.at[j], recv_sem=recv_sems.at[j], device_id=(px, py, pc), device_id_type=MESH))
    return copies


def _own_slot(share):
    chip = 2 * lax.axis_index("x") + lax.axis_index("y")
    return lax.dynamic_update_slice(lax.empty((N_CHIPS,) + share.shape, share.dtype), share[None], (chip, 0, 0))


def _own_part(parts):
    chip = 2 * lax.axis_index("x") + lax.axis_index("y")
    own = lax.dynamic_index_in_dim(parts, chip, 0, keepdims=True)
    return lax.dynamic_update_slice(lax.empty((N_DEV,) + parts.shape[1:], parts.dtype), own,
                                    (2 * chip + lax.axis_index("c"), 0, 0))


def _exchange_start(src, land, after, name, gather):
    def body(src_ref, land_ref, after_ref, send_sems, recv_sems, src_thru, land_thru, token):
        for cp in _split_copies(src_ref, land_ref, send_sems, recv_sems, gather, sending=True):
            cp.start()
        token[...] = jnp.zeros_like(token)

    hbm = lambda t: pltpu.with_memory_space_constraint(t, pltpu.HBM)
    n_copies = N_CHIPS - 1 if gather else N_DEV - 1
    return pl.pallas_call(
        body, name=name,
        out_shape=(pltpu.SemaphoreType.DMA((n_copies,)), pltpu.SemaphoreType.DMA((n_copies,)), pltpu.HBM(src.shape, src.dtype),
                   pltpu.HBM(land.shape, land.dtype), jax.ShapeDtypeStruct((8, LANES), F32)),
        in_specs=(_HBM, _HBM, _ANY), out_specs=(_SEM, _SEM, _HBM, _HBM, pl.BlockSpec(memory_space=pltpu.VMEM)),
        input_output_aliases={0: 2, 1: 3},
        compiler_params=pltpu.CompilerParams(has_side_effects=_EFFECT),
    )(hbm(src), hbm(land), after)


def _exchange_wait(started, after, name, gather):
    send_sems, recv_sems, src_thru, land_thru, _ = started

    def body(src_ref, land_ref, send_sems, recv_sems, after_ref, src_dead, got_ref):
        for cp in _split_copies(src_ref, land_ref, send_sems, recv_sems, gather, sending=False):
            cp.wait_send()
            cp.wait_recv()

    return pl.pallas_call(
        body, name=name,
        out_shape=(pltpu.HBM(src_thru.shape, src_thru.dtype), pltpu.HBM(land_thru.shape, land_thru.dtype)),
        in_specs=(_HBM, _HBM, _SEM, _SEM, _ANY), out_specs=(_HBM, _HBM), input_output_aliases={0: 0, 1: 1},
        compiler_params=pltpu.CompilerParams(has_side_effects=_EFFECT),
    )(src_thru, land_thru, send_sems, recv_sems, after)[1]


def _all_reduce_small(v):
    rows = v.shape[0]
    flips = [(a, b, cc) for a in (0, 1) for b in (0, 1) for cc in (0, 1)][1:]

    def body(v_ref, out_ref, buf_ref, send_sems, recv_sems):
        x, y, c = _place()
        me = 4 * x + 2 * y + c
        peers = [((1 - x) if a else x, (1 - y) if b else y, (1 - c) if cc else c) for a, b, cc in flips]

        def copy(j, landing):
            return pltpu.make_async_remote_copy(src_ref=v_ref, dst_ref=buf_ref.at[landing], send_sem=send_sems.at[j],
                                                recv_sem=recv_sems.at[j], device_id=peers[j], device_id_type=MESH)

        sends = [copy(j, me) for j in range(N_DEV - 1)]
        for cp in sends:
            cp.start()
        buf_ref[me] = v_ref[...]
        for j, (px, py, pc) in enumerate(peers):
            copy(j, 4 * px + 2 * py + pc).wait_recv()
        for cp in sends:
            cp.wait_send()
        acc = buf_ref[0]
        for d in range(1, N_DEV):
            acc = acc + buf_ref[d]
        out_ref[...] = acc

    vmem = pl.BlockSpec(memory_space=pltpu.VMEM)
    return pl.pallas_call(
        body, name="all_reduce_small", in_specs=[vmem], out_specs=vmem,
        out_shape=jax.ShapeDtypeStruct(v.shape, F32),
        scratch_shapes=[pltpu.VMEM((N_DEV, rows, LANES), F32), pltpu.SemaphoreType.DMA((N_DEV - 1,)),
                        pltpu.SemaphoreType.DMA((N_DEV - 1,))],
    )(v)


def _row_block(*sizes):
    return next(t for t in (128, 64) if all(s % t == 0 for s in sizes))


def _adam_update(w, m, v, g):
    r1 = 1.0 / (1.0 - ADAM_B1 ** ADAM_STEP)
    r2 = 1.0 / (1.0 - ADAM_B2 ** ADAM_STEP)
    m_new = ADAM_B1 * m + (1.0 - ADAM_B1) * g
    v_new = ADAM_B2 * v + (1.0 - ADAM_B2) * (g * g)
    return -ADAM_LR * ((m_new * r1) / (jnp.sqrt(v_new * r2) + ADAM_EPS) + ADAM_WD * w), m_new, v_new


def _adamw_rows(w, m, v, got, first, name):
    n_layers, rows, cols = w.shape
    tr = _row_block(rows, first)

    def body(*refs):
        w_ref, m_ref, v_ref = refs[:3]
        g_out, d_out, m_out, v_out = refs[3 + n_layers:]
        for k in range(n_layers):
            @pl.when(pl.program_id(0) == k)
            def _(p_ref=refs[3 + k]):
                g = p_ref[0].astype(F32)
                for d in range(1, N_DEV):
                    g = g + p_ref[d].astype(F32)
                g = g[:, :cols]
                d_out[0], m_out[0], v_out[0] = _adam_update(w_ref[0], m_ref[0], v_ref[0], g)
                g_out[0] = g

    blk = pl.BlockSpec((1, tr, cols), lambda l, i: (l, i, 0))
    parts = [pl.BlockSpec((N_DEV, tr, got[0].shape[2]), lambda l, i, k=k: (0, jnp.where(l == k, first // tr + i, 0), 0))
             for k in range(n_layers)]
    return pl.pallas_call(
        body, name=name, grid=(n_layers, rows // tr),
        in_specs=[blk] * 3 + parts, out_specs=[blk] * 4,
        out_shape=[jax.ShapeDtypeStruct(w.shape, F32)] * 4,
        compiler_params=_params(("arbitrary", "arbitrary")),
    )(w, m, v, *got)


def _adamw(w, m, v, g_parts, name):
    rows, cols = w.shape
    tr = min(rows, 256)
    n = len(g_parts)

    def body(*refs):
        w_ref, m_ref, v_ref = refs[:3]
        g_refs = refs[3:3 + n]
        g_out, d_out, m_out, v_out = refs[3 + n:]
        g = g_refs[0][...]
        for r in g_refs[1:]:
            g = g + r[...]
        d_out[...], m_out[...], v_out[...] = _adam_update(w_ref[...], m_ref[...], v_ref[...], g)
        g_out[...] = g

    blk = pl.BlockSpec((tr, cols), lambda i: (i, 0))
    return pl.pallas_call(
        body, name=name, grid=(rows // tr,),
        in_specs=[blk] * (3 + n), out_specs=[blk] * 4,
        out_shape=[jax.ShapeDtypeStruct((rows, cols), F32)] * 4,
        compiler_params=_params(("parallel",)),
    )(w, m, v, *g_parts)


def _pack(parts, rows, fill=0.0):
    flat = jnp.concatenate([p.reshape(-1) for p in parts])
    return jnp.pad(flat, (0, rows * LANES - flat.shape[0]), constant_values=fill).reshape(rows, LANES)


def _unpack(packed, shapes):
    flat = packed.reshape(-1)
    out, at = [], 0
    for shp in shapes:
        size = 1
        for s in shp:
            size *= s
        out.append(flat[at:at + size].reshape(shp))
        at += size
    return out


def _packed_rows(shapes):
    total = 0
    for shp in shapes:
        size = 1
        for s in shp:
            size *= s
        total += size
    return -(-total // (8 * LANES)) * 8


def _cols_full(g, l):
    t = g[:, l]
    return jnp.moveaxis(t, 0, 1).reshape(t.shape[1], N_CHIPS * t.shape[2])


def _pad_cols(t):
    return jnp.pad(t, ((0, 0),) * (t.ndim - 1) + ((0, D_MODEL - t.shape[-1]),))


def kernel(x, norm1_g, w_in, dn_conv_w, dn_a_log, dn_dt_bias, dn_norm_g, sc_conv_w, sc_norm_g, w_out, norm2_g, ffn_w_gate, ffn_w_up, ffn_w_down, final_norm_g, loss_target, m_norm1_g, m_w_in, m_dn_conv_w, m_dn_a_log, m_dn_dt_bias, m_dn_norm_g, m_sc_conv_w, m_sc_norm_g, m_w_out, m_norm2_g, m_ffn_w_gate, m_ffn_w_up, m_ffn_w_down, m_final_norm_g, v_norm1_g, v_w_in, v_dn_conv_w, v_dn_a_log, v_dn_dt_bias, v_dn_norm_g, v_sc_conv_w, v_sc_norm_g, v_w_out, v_norm2_g, v_ffn_w_gate, v_ffn_w_up, v_ffn_w_down, v_final_norm_g):
    chip = 2 * lax.axis_index("x") + lax.axis_index("y")

    g_cw, g_scw = _chip_exchange([dn_conv_w, sc_conv_w], "gather_conv", gather=True)

    t_last = lambda t: jnp.swapaxes(t, -1, -2)
    gate_t, up_t = t_last(ffn_w_gate), t_last(ffn_w_up)
    zero_token = jnp.zeros((8, LANES), F32)

    def shares(l, tie):
        share_a = jnp.concatenate([_pad_cols(w_in[l] + tie), w_out[l]], axis=0).astype(BF16)
        share_b = jnp.concatenate([gate_t[l] + tie, up_t[l], ffn_w_down[l]], axis=0).astype(BF16)
        return share_a, _own_slot(share_a), share_b, _own_slot(share_b)

    def gather_start(l, packed, after):
        a = _exchange_start(packed[0], packed[1], after, "gather_a_start_%d" % l, gather=True)
        b = _exchange_start(packed[2], packed[3], a[4], "gather_b_start_%d" % l, gather=True)
        return a, b

    ga, gb = gather_start(0, shares(0, 0.0), g_cw)
    packed = [None] + [shares(l, gb[4][0, 0]) for l in range(1, DEPTH)]
    packed_all = sum(t[0, 0].astype(F32) for p in packed[1:] for t in (p[0], p[2]))
    land_a = _exchange_wait(ga, zero_token + packed_all, "gather_a_wait_0", gather=True)
    act = x[0]
    layers, saved_m, saved_f, lands_b = [], [], [], []
    for l in range(DEPTH):
        hold = 0.0
        if l + 1 < DEPTH:
            ga, gb_next = gather_start(l + 1, packed[l + 1], land_a)
            hold = gb_next[4][0:1, 0:1]
        al, dt = _gate_rows(dn_a_log[l], dn_dt_bias[l])
        layers.append(dict(
            g1=norm1_g[l][None] + hold, cw=_pad_rows(_cols_full(g_cw, l)), al=al, dt=dt,
            gn=dn_norm_g[l][None], scw=_pad_rows(_cols_full(g_scw, l)), gs=sc_norm_g[l][None],
            land_a=land_a, g2=norm2_g[l][None]))
        x1, s = _mixer_fwd(act, layers[l])
        saved_m.append(s)
        lands_b.append(_exchange_wait(gb, x1, "gather_b_wait_%d" % l, gather=True))
        act, s = _ffn_fwd(x1, layers[l], lands_b[l])
        saved_f.append(s)
        if l + 1 < DEPTH:
            land_a = _exchange_wait(ga, act, "gather_a_wait_%d" % (l + 1), gather=True)
            gb = gb_next

    dact, dact_bf16, loss_part, d_final = _loss_head(act, final_norm_g[None], loss_target[0])
    grads, reduce_a, reduce_b = [None] * DEPTH, [None] * DEPTH, [None] * DEPTH
    hold = 0.0
    for l in reversed(range(DEPTH)):
        p = layers[l]
        dx1, dx1_bf16, parts, dg2 = _ffn_back(dact, dact_bf16, saved_f[l], dict(p, g2=p["g2"] + hold), lands_b[l])
        reduce_b[l] = _exchange_start(parts, _own_part(parts), zero_token, "reduce_b_start_%d" % l, gather=False)
        dact, dact_bf16, parts, gm = _mixer_bwd(dx1, dx1_bf16, saved_m[l], dict(p, gn=p["gn"] + reduce_b[l][4][0:1, 0:1]))
        reduce_a[l] = _exchange_start(parts, _own_part(parts), zero_token, "reduce_a_start_%d" % l, gather=False)
        hold = reduce_a[l][4][0:1, 0:1]
        grads[l] = dict(gm, g2=dg2)
    loss = lax.psum(loss_part[0, 0], ("x", "y", "c"))
    stack = lambda key: jnp.stack([grads[l][key] for l in range(DEPTH)])

    got_b = [_exchange_wait(reduce_b[l], reduce_a[0][4], "reduce_b_wait_%d" % l, gather=False)
             for l in reversed(range(DEPTH))][::-1]
    big = dict(
        ffn_w_gate=[t_last(o) for o in _adamw_rows(gate_t, t_last(m_ffn_w_gate), t_last(v_ffn_w_gate), got_b, 0, "adamw_gate")],
        ffn_w_up=[t_last(o) for o in _adamw_rows(up_t, t_last(m_ffn_w_up), t_last(v_ffn_w_up), got_b, FF_SHARD, "adamw_up")],
        ffn_w_down=_adamw_rows(ffn_w_down, m_ffn_w_down, v_ffn_w_down, got_b, 2 * FF_SHARD, "adamw_down"))
    after_b = zero_token + sum(big[n][1][0, 0, 0] for n in ("ffn_w_gate", "ffn_w_up", "ffn_w_down"))
    got_a = [_exchange_wait(reduce_a[l], after_b, "reduce_a_wait_%d" % l, gather=False) for l in reversed(range(DEPTH))][::-1]
    big.update(
        w_in=_adamw_rows(w_in, m_w_in, v_w_in, got_a, 0, "adamw_w_in"),
        w_out=_adamw_rows(w_out, m_w_out, v_w_out, got_a, A_OUT_AT, "adamw_w_out"))

    full_shapes = [(DEPTH, D_MODEL), (DEPTH, D_MODEL), (DEPTH, HEAD_DIM), (DEPTH, SC_WIDTH), (DEPTH, HEADS),
                   (DEPTH, HEADS), (D_MODEL,), (DEPTH, 4, QKV), (DEPTH, 3, SC_WIDTH)]
    small_keys = ("g1", "g2", "gn", "gs", "al", "dt")
    packed = _pack([stack(k) for k in small_keys] + [d_final[0], stack("cw"), stack("scw")], _packed_rows(full_shapes))
    sg = _unpack(_all_reduce_small(packed), full_shapes)
    sg[7] = lax.dynamic_slice_in_dim(sg[7], chip * (QKV // N_CHIPS), QKV // N_CHIPS, axis=2)
    sg[8] = lax.dynamic_slice_in_dim(sg[8], chip * (SC_WIDTH // N_CHIPS), SC_WIDTH // N_CHIPS, axis=2)
    small_names = ("norm1_g", "norm2_g", "dn_norm_g", "sc_norm_g", "dn_a_log", "dn_dt_bias", "final_norm_g",
                   "dn_conv_w", "sc_conv_w")
    sw = (norm1_g, norm2_g, dn_norm_g, sc_norm_g, dn_a_log, dn_dt_bias, final_norm_g, dn_conv_w, sc_conv_w)
    sm = (m_norm1_g, m_norm2_g, m_dn_norm_g, m_sc_norm_g, m_dn_a_log, m_dn_dt_bias, m_final_norm_g, m_dn_conv_w, m_sc_conv_w)
    sv = (v_norm1_g, v_norm2_g, v_dn_norm_g, v_sc_norm_g, v_dn_a_log, v_dn_dt_bias, v_final_norm_g, v_dn_conv_w, v_sc_conv_w)
    shard_shapes = [t.shape for t in sw]
    rows = _packed_rows(shard_shapes)
    outs = _adamw(_pack(sw, rows), _pack(sm, rows), _pack(sv, rows, fill=1.0), [_pack(sg, rows)], "adamw_small")
    small = {name: [] for name in small_names}
    for o in outs:
        for name, t in zip(small_names, _unpack(o, shard_shapes)):
            small[name].append(t)

    order = ("norm1_g", "w_in", "dn_conv_w", "dn_a_log", "dn_dt_bias", "dn_norm_g", "sc_conv_w", "sc_norm_g", "w_out",
             "norm2_g", "ffn_w_gate", "ffn_w_up", "ffn_w_down", "final_norm_g")
    result = {**big, **small}
    return (loss, dact[None], *[result[n][0] for n in order], *[result[n][1] for n in order],
            *[result[n][2] for n in order], *[result[n][3] for n in order])
```

```python
import jax
import jax.numpy as jnp
from jax import lax
from jax.experimental import pallas as pl
from jax.experimental.pallas import tpu as pltpu

F32 = jnp.float32
BF16 = jnp.bfloat16
MESH = pl.DeviceIdType.MESH

D_MODEL = 1024
DEPTH = 4
HEADS = 4
HEAD_DIM = 128
DN_WIDTH = HEADS * HEAD_DIM
SC_WIDTH = 512
SC_GROUPS = 4
D_FF = 2816
CHUNK = 64
QKV = 3 * DN_WIDTH
W_IN_COLS = 4 * DN_WIDTH + 2 * HEADS + 3 * SC_WIDTH
WA_COLS = QKV + DN_WIDTH + 3 * SC_WIDTH
LANES = 128
EPS = 1e-6
Q_SCALE = HEAD_DIM ** -0.5
N_CHIPS = 4
N_DEV = 8
IN_SHARD = W_IN_COLS // N_CHIPS
OUT_SHARD = D_MODEL // N_CHIPS
FF_SHARD = D_FF // N_CHIPS
A_OUT_AT = D_MODEL
A_ROWS = D_MODEL + OUT_SHARD
B_ROWS = 3 * FF_SHARD

ADAM_LR = 0.001
ADAM_B1 = 0.9
ADAM_B2 = 0.999
ADAM_EPS = 1e-08
ADAM_WD = 0.01
ADAM_STEP = 10

VMEM_LIMIT = 56 * 1024 * 1024

NN = (((1,), (0,)), ((), ()))
NT = (((1,), (1,)), ((), ()))
TN = (((0,), (0,)), ((), ()))


def _mm(a, b, dims=NN):
    return lax.dot_general(a.astype(BF16), b.astype(BF16), dims, preferred_element_type=F32)


def _mm32(a, b, dims=NN):
    return lax.dot_general(a, b, dims, preferred_element_type=F32, precision=lax.Precision.HIGHEST)


def _params(sem, vmem=VMEM_LIMIT):
    return pltpu.CompilerParams(dimension_semantics=sem, vmem_limit_bytes=vmem)


def _sigmoid(x):
    return 0.5 * jnp.tanh(0.5 * x) + 0.5


def _softplus(x):
    return jnp.maximum(x, 0.0) + jnp.log1p(jnp.exp(-jnp.abs(x)))


def _row_acc(acc_ref, val):
    acc_ref[0:1, :] += jnp.sum(val, axis=0, keepdims=True)


def _rms_bwd(dh, xh, r, gain):
    dxh = dh * gain
    return r * (dxh - xh * jnp.mean(dxh * xh, axis=-1, keepdims=True))


def _before_halo(tb):
    return lambda i: (jnp.maximum(i * (tb // 8) - 1, 0), 0)


def _after_halo(tb, n_rows):
    last = n_rows // 8 - 1
    return lambda i: (jnp.minimum((i + 1) * (tb // 8), last), 0)


def _rows_from(xc, offset, tb):
    part = offset % 8
    if part:
        xc = pltpu.roll(xc, xc.shape[0] - part, 0)
    return xc[offset - part:offset - part + tb, :]


def _taps(xc, w, n_taps, tb, first):
    out = w[0:1, :] * _rows_from(xc, first, tb)
    for j in range(1, n_taps):
        out = out + w[j:j + 1, :] * _rows_from(xc, first + j, tb)
    return out


W_Z = QKV
W_BD = W_Z + DN_WIDTH
W_SC = W_BD + 2 * HEADS

def _w_in_cols(shards, lo, hi):
    pieces = []
    for s in range(N_CHIPS):
        a, b = max(lo, IN_SHARD * s), min(hi, IN_SHARD * (s + 1))
        if a < b:
            pieces.append(shards[s][:, a - IN_SHARD * s:b - IN_SHARD * s])
    return pieces[0] if len(pieces) == 1 else jnp.concatenate(pieces, axis=1)


def _in_proj(x, g1, land_a):
    T = x.shape[0]
    tb = 256

    def body(x_ref, g_ref, w_ref, qkv_ref, z_ref, sc_ref, bd_ref, h_ref):
        xv = x_ref[...]
        r = lax.rsqrt(jnp.mean(xv * xv, axis=-1, keepdims=True) + EPS)
        h = (xv * r * g_ref[...]).astype(BF16)
        shards = [jnp.dot(h, w_ref[s], preferred_element_type=F32) for s in range(N_CHIPS)]
        qkv_ref[...] = _w_in_cols(shards, 0, W_Z)
        z_ref[...] = _w_in_cols(shards, W_Z, W_BD)
        bd_ref[...] = jnp.concatenate([_w_in_cols(shards, W_BD, W_SC), jnp.zeros((tb, LANES - 2 * HEADS), F32)], axis=1)
        sc_ref[...] = _w_in_cols(shards, W_SC, W_IN_COLS)
        h_ref[...] = h

    tok = lambda w: pl.BlockSpec((tb, w), lambda i: (i, 0))
    return pl.pallas_call(
        body, name="in_proj", grid=(T // tb,),
        in_specs=[tok(D_MODEL), pl.BlockSpec(g1.shape, lambda i: (0, 0)), _shard_rows(land_a, 0, D_MODEL)],
        out_specs=[tok(QKV), tok(DN_WIDTH), tok(3 * SC_WIDTH), tok(LANES), tok(D_MODEL)],
        out_shape=[jax.ShapeDtypeStruct((T, QKV), F32), jax.ShapeDtypeStruct((T, DN_WIDTH), F32),
                   jax.ShapeDtypeStruct((T, 3 * SC_WIDTH), F32), jax.ShapeDtypeStruct((T, LANES), F32),
                   jax.ShapeDtypeStruct((T, D_MODEL), BF16)],
        compiler_params=_params(("parallel",)),
    )(x, g1, land_a)


def _dn_act(pre, halo, cw, tb):
    xc = jnp.concatenate([halo, pre], axis=0)
    c = _taps(xc, cw, 4, tb, 5)
    sg = _sigmoid(c)
    return xc, c, sg, c * sg


def _gates(bd, al_row, dt_row):
    lane = lax.broadcasted_iota(jnp.int32, bd.shape, 1)
    beta = _sigmoid(bd)
    g = -jnp.exp(al_row) * _softplus(bd + dt_row)
    return jnp.where(lane < HEADS, beta, jnp.where(lane < 2 * HEADS, g, 0.0))


def _dn_prep(qkv, cw, bd, al_row, dt_row):
    T = qkv.shape[0]
    tb = 512

    def body(pre_ref, halo_ref, cw_ref, bd_ref, al_ref, dt_ref, q_ref, k_ref, v_ref, bg_ref):
        halo = jnp.where(pl.program_id(0) > 0, halo_ref[...], 0.0)
        _, _, _, a = _dn_act(pre_ref[...], halo, cw_ref[...], tb)
        for hh in range(HEADS):
            sl = slice(HEAD_DIM * hh, HEAD_DIM * (hh + 1))
            qs = a[:, sl]
            q_ref[:, sl] = qs * (lax.rsqrt(jnp.sum(qs * qs, axis=-1, keepdims=True) + EPS) * Q_SCALE)
            ks = a[:, DN_WIDTH + HEAD_DIM * hh:DN_WIDTH + HEAD_DIM * (hh + 1)]
            k_ref[:, sl] = ks * lax.rsqrt(jnp.sum(ks * ks, axis=-1, keepdims=True) + EPS)
        v_ref[...] = a[:, 2 * DN_WIDTH:]
        gates = _gates(bd_ref[...], al_ref[...], dt_ref[...])
        lane = lax.broadcasted_iota(jnp.int32, gates.shape, 1)
        bg_ref[...] = jnp.where(lane < HEADS, gates, _mm32(_chunk_cumsum_matrix(tb), gates))

    tok = lambda w: pl.BlockSpec((tb, w), lambda i: (i, 0))
    full = lambda a: pl.BlockSpec(a.shape, lambda i: (0, 0))
    return pl.pallas_call(
        body, name="dn_prep", grid=(T // tb,),
        in_specs=[tok(QKV), pl.BlockSpec((8, QKV), _before_halo(tb)), full(cw), tok(LANES), full(al_row), full(dt_row)],
        out_specs=[tok(DN_WIDTH), tok(DN_WIDTH), tok(DN_WIDTH), tok(LANES)],
        out_shape=[jax.ShapeDtypeStruct((T, DN_WIDTH), F32)] * 3 + [jax.ShapeDtypeStruct((T, LANES), F32)],
        compiler_params=_params(("parallel",)),
    )(qkv, qkv, cw, bd, al_row, dt_row)


def _chunk_masks():
    row = lax.broadcasted_iota(jnp.int32, (CHUNK, CHUNK), 0)
    col = lax.broadcasted_iota(jnp.int32, (CHUNK, CHUNK), 1)
    return row >= col, row > col


def _chunk_cumsum_matrix(n):
    row = lax.broadcasted_iota(jnp.int32, (n, n), 0)
    col = lax.broadcasted_iota(jnp.int32, (n, n), 1)
    return jnp.logical_and(row >= col, row // CHUNK == col // CHUNK).astype(F32)


def _chunk_units(q_ref, k_ref, v_ref, bg_ref, rows):
    bgc = bg_ref[rows, :]
    bg_t = bgc.T
    qv, kv, vv = q_ref[rows, :], k_ref[rows, :], v_ref[rows, :]
    units = []
    for h in range(HEADS):
        sl = slice(HEAD_DIM * h, HEAD_DIM * (h + 1))
        units.append((qv[:, sl], kv[:, sl], vv[:, sl], bgc[:, h:h + 1], bgc[:, HEADS + h:HEADS + h + 1],
                      bg_t[HEADS + h:HEADS + h + 1, :]))
    return units


def _units_local(units, masks):
    causal, strict = masks
    pre = []
    for q, k, v, beta, gc, gr in units:
        kb = k * beta
        eg = jnp.exp(gc)
        g_last = gc[CHUNK - 1:CHUNK, :]
        ek = jnp.exp(g_last - gc)
        pre.append(dict(q=q, k=k, v=v, beta=beta, decay=jnp.exp(jnp.where(causal, gc - gr, -1e30)), kb=kb, vb=v * beta,
                        eg=eg, kbg=kb * eg, ek=ek, gl=jnp.exp(g_last), q_dec=q * eg, k_dec=k * ek))
    both = [_mm(jnp.concatenate([p["kb"], p["q"]], axis=0), p["k"], NT) for p in pre]
    for p, b in zip(pre, both):
        p["low"] = jnp.where(strict, b[:CHUNK] * p["decay"], 0.0)
        p["qk"] = jnp.where(causal, b[CHUNK:] * p["decay"], 0.0)
    xs = [-p["low"] for p in pre]
    pw = [_mm(p["low"], p["low"]) for p in pre]
    for _ in range(4):
        both = [_mm(jnp.concatenate([pp, x], axis=0), pp) for pp, x in zip(pw, xs)]
        xs = [x + pp + b[CHUNK:] for x, pp, b in zip(xs, pw, both)]
        pw = [b[:CHUNK] for b in both]
    last = [_mm(x, pp) for x, pp in zip(xs, pw)]
    xs = [x + pp + b for x, pp, b in zip(xs, pw, last)]
    uw = [_mm(x, jnp.concatenate([p["vb"], p["kbg"]], axis=1)) for x, p in zip(xs, pre)]
    for p, x, b in zip(pre, xs, uw):
        p["xm"] = x
        p["u"] = p["vb"] + b[:, :HEAD_DIM]
        p["w"] = p["kbg"] + b[:, HEAD_DIM:]
    return pre


def _delta_fwd(q, k, v, bg):
    T = q.shape[0]
    tb = 512
    n_chunk = tb // CHUNK

    def body(q_ref, k_ref, v_ref, bg_ref, o_ref, st_ref, s_ref):
        @pl.when(pl.program_id(0) == 0)
        def _():
            s_ref[...] = jnp.zeros_like(s_ref)

        masks = _chunk_masks()

        def pair(pi, carry):
            rows = [pl.ds(pl.multiple_of((2 * pi + j) * CHUNK, CHUNK), CHUNK) for j in range(2)]
            loc = _units_local(_chunk_units(q_ref, k_ref, v_ref, bg_ref, rows[0])
                               + _chunk_units(q_ref, k_ref, v_ref, bg_ref, rows[1]), masks)
            states = [s_ref[h] for h in range(HEADS)]
            for j in range(2):
                lj = loc[HEADS * j:HEADS * (j + 1)]
                ws = [_mm(jnp.concatenate([p["w"], p["q_dec"]], axis=0), s) for p, s in zip(lj, states)]
                v_new = [p["u"] - b[:CHUNK] for p, b in zip(lj, ws)]
                intra = [_mm(p["qk"], vn) for p, vn in zip(lj, v_new)]
                upd = [_mm(p["k_dec"], vn, TN) for p, vn in zip(lj, v_new)]
                o_ref[rows[j], :] = jnp.concatenate([b[CHUNK:] + a for b, a in zip(ws, intra)], axis=1)
                for h in range(HEADS):
                    st_ref[2 * pi + j, h] = states[h]
                states = [p["gl"] * s + d for p, s, d in zip(lj, states, upd)]
            for h in range(HEADS):
                s_ref[h] = states[h]
            return carry

        lax.fori_loop(0, n_chunk // 2, pair, 0)

    tok = lambda w: pl.BlockSpec((tb, w), lambda i: (i, 0))
    return pl.pallas_call(
        body, name="delta_fwd", grid=(T // tb,),
        in_specs=[tok(DN_WIDTH), tok(DN_WIDTH), tok(DN_WIDTH), tok(LANES)],
        out_specs=[tok(DN_WIDTH), pl.BlockSpec((n_chunk, HEADS, HEAD_DIM, HEAD_DIM), lambda i: (i, 0, 0, 0))],
        out_shape=[jax.ShapeDtypeStruct((T, DN_WIDTH), F32),
                   jax.ShapeDtypeStruct((T // CHUNK, HEADS, HEAD_DIM, HEAD_DIM), F32)],
        scratch_shapes=[pltpu.VMEM((HEADS, HEAD_DIM, HEAD_DIM), F32)],
        compiler_params=_params(("arbitrary",)),
    )(q, k, v, bg)


def _dn_out(o, z, gn):
    outs, ohs, rs = [], [], []
    for hh in range(HEADS):
        oh = o[:, HEAD_DIM * hh:HEAD_DIM * (hh + 1)]
        r = lax.rsqrt(jnp.mean(oh * oh, axis=-1, keepdims=True) + EPS)
        ohs.append(oh * r)
        rs.append(r)
    sz = _sigmoid(z)
    oh = jnp.concatenate(ohs, axis=1)
    gn4 = jnp.concatenate([gn] * HEADS, axis=1)
    return oh * gn4 * (z * sz), oh, rs, sz, gn4


def _sc_fwd(sc_in, halo, cw, tb):
    xc = jnp.concatenate([halo, sc_in], axis=0)
    u = xc[:, SC_WIDTH:2 * SC_WIDTH] * xc[:, 2 * SC_WIDTH:]
    cv = _taps(u, cw, 3, tb, 6)
    gate_b = sc_in[:, :SC_WIDTH]
    y = gate_b * cv
    gw = SC_WIDTH // SC_GROUPS
    yhs, rs = [], []
    for gi in range(SC_GROUPS):
        yg = y[:, gw * gi:gw * (gi + 1)]
        r = lax.rsqrt(jnp.mean(yg * yg, axis=-1, keepdims=True) + EPS)
        yhs.append(yg * r)
        rs.append(r)
    return u, cv, gate_b, jnp.concatenate(yhs, axis=1), rs


def _shard_rows(land, first, rows):
    assert first % rows == 0 and land.shape[0] == N_CHIPS
    return pl.BlockSpec((N_CHIPS, rows, land.shape[2]), lambda i: (0, first // rows, 0))


def _whole(w_ref):
    n, rows, cols = w_ref.shape
    return w_ref[...].reshape(n * rows, cols)


def _mix_out(o, z, sc_in, x, land_a, gn, scw, gs):
    T = x.shape[0]
    tb = 256

    def body(o_ref, z_ref, sc_ref, halo_ref, x_ref, w_ref, gn_ref, scw_ref, gs_ref, x1_ref, mix_ref):
        o_n = _dn_out(o_ref[...], z_ref[...], gn_ref[...])[0]
        halo = jnp.where(pl.program_id(0) > 0, halo_ref[...], 0.0)
        yh = _sc_fwd(sc_ref[...], halo, scw_ref[...], tb)[3]
        mix = jnp.concatenate([o_n, yh * gs_ref[...]], axis=1).astype(BF16)
        x1_ref[...] = x_ref[...] + jnp.dot(mix, _whole(w_ref), preferred_element_type=F32)
        mix_ref[...] = mix

    tok = lambda w: pl.BlockSpec((tb, w), lambda i: (i, 0))
    full = lambda a: pl.BlockSpec(a.shape, lambda i: (0, 0))
    return pl.pallas_call(
        body, name="mix_out", grid=(T // tb,),
        in_specs=[tok(DN_WIDTH), tok(DN_WIDTH), tok(3 * SC_WIDTH), pl.BlockSpec((8, 3 * SC_WIDTH), _before_halo(tb)),
                  tok(D_MODEL), _shard_rows(land_a, A_OUT_AT, OUT_SHARD), full(gn), full(scw), full(gs)],
        out_specs=[tok(D_MODEL), tok(D_MODEL)],
        out_shape=[jax.ShapeDtypeStruct((T, D_MODEL), F32), jax.ShapeDtypeStruct((T, D_MODEL), BF16)],
        compiler_params=_params(("parallel",)),
    )(o, z, sc_in, sc_in, x, land_a, gn, scw, gs)


def _ffn(x1, g2, land_b):
    T = x1.shape[0]
    tb = 256

    def body(x_ref, g_ref, wgt_ref, wut_ref, wd_ref, x2_ref, a_ref, b_ref, h_ref):
        xv = x_ref[...]
        r = lax.rsqrt(jnp.mean(xv * xv, axis=-1, keepdims=True) + EPS)
        h = (xv * r * g_ref[...]).astype(BF16)
        a = lax.dot_general(h, _whole(wgt_ref), NT, preferred_element_type=F32)
        b = lax.dot_general(h, _whole(wut_ref), NT, preferred_element_type=F32)
        act = (a * _sigmoid(a) * b).astype(BF16)
        x2_ref[...] = xv + jnp.dot(act, _whole(wd_ref), preferred_element_type=F32)
        a_ref[...] = a.astype(BF16)
        b_ref[...] = b.astype(BF16)
        h_ref[...] = h

    tok = lambda w: pl.BlockSpec((tb, w), lambda i: (i, 0))
    return pl.pallas_call(
        body, name="ffn", grid=(T // tb,),
        in_specs=[tok(D_MODEL), pl.BlockSpec(g2.shape, lambda i: (0, 0)), _shard_rows(land_b, 0, FF_SHARD),
                  _shard_rows(land_b, FF_SHARD, FF_SHARD), _shard_rows(land_b, 2 * FF_SHARD, FF_SHARD)],
        out_specs=[tok(D_MODEL), tok(D_FF), tok(D_FF), tok(D_MODEL)],
        out_shape=[jax.ShapeDtypeStruct((T, D_MODEL), F32), jax.ShapeDtypeStruct((T, D_FF), BF16),
                   jax.ShapeDtypeStruct((T, D_FF), BF16), jax.ShapeDtypeStruct((T, D_MODEL), BF16)],
        compiler_params=_params(("parallel",)),
    )(x1, g2, land_b, land_b, land_b)


def _loss_head(x, gf, target):
    T = x.shape[0]
    tb = 512

    def body(x_ref, g_ref, t_ref, dx_ref, dxb_ref, loss_ref, dg_ref):
        @pl.when(pl.program_id(0) == 0)
        def _():
            loss_ref[...] = jnp.zeros_like(loss_ref)
            dg_ref[...] = jnp.zeros_like(dg_ref)

        xv = x_ref[...]
        r = lax.rsqrt(jnp.mean(xv * xv, axis=-1, keepdims=True) + EPS)
        xh = xv * r
        err = xh * g_ref[...] - t_ref[...]
        per_tok = jnp.mean(err * err, axis=-1, keepdims=True)
        loss_ref[...] += 0.5 * jnp.sum(per_tok, axis=0, keepdims=True)
        dy = err * (1.0 / D_MODEL)
        _row_acc(dg_ref, dy * xh)
        dx = _rms_bwd(dy, xh, r, g_ref[...])
        dx_ref[...] = dx
        dxb_ref[...] = dx.astype(BF16)

    tok = pl.BlockSpec((tb, D_MODEL), lambda i: (i, 0))
    return pl.pallas_call(
        body, name="loss_head", grid=(T // tb,),
        in_specs=[tok, pl.BlockSpec(gf.shape, lambda i: (0, 0)), tok],
        out_specs=[tok, tok, pl.BlockSpec((8, LANES), lambda i: (0, 0)), pl.BlockSpec((8, D_MODEL), lambda i: (0, 0))],
        out_shape=[jax.ShapeDtypeStruct((T, D_MODEL), F32), jax.ShapeDtypeStruct((T, D_MODEL), BF16),
                   jax.ShapeDtypeStruct((8, LANES), F32), jax.ShapeDtypeStruct((8, D_MODEL), F32)],
        compiler_params=_params(("arbitrary",)),
    )(x, gf, target)


def _ffn_bwd(dx2, x1, a, b, g2, land_b):
    T = x1.shape[0]
    tb = 256

    def body(dx2_ref, x_ref, a_ref, b_ref, g_ref, wgt_ref, wut_ref, wd_ref,
             dx1_ref, dx1b_ref, da_ref, db_ref, act_ref, dg_ref):
        @pl.when(pl.program_id(0) == 0)
        def _():
            dg_ref[...] = jnp.zeros_like(dg_ref)

        dx2v = dx2_ref[...]
        av = a_ref[...].astype(F32)
        bv = b_ref[...].astype(F32)
        dact = _mm(dx2v, _whole(wd_ref), NT)
        sa = _sigmoid(av)
        silu = av * sa
        da = (dact * bv * (sa * (1.0 + av * (1.0 - sa)))).astype(BF16)
        db = (dact * silu).astype(BF16)
        dh = _mm(da, _whole(wgt_ref)) + _mm(db, _whole(wut_ref))
        xv = x_ref[...]
        r = lax.rsqrt(jnp.mean(xv * xv, axis=-1, keepdims=True) + EPS)
        xh = xv * r
        _row_acc(dg_ref, dh * xh)
        dx1 = dx2v + _rms_bwd(dh, xh, r, g_ref[...])
        dx1_ref[...] = dx1
        dx1b_ref[...] = dx1.astype(BF16)
        da_ref[...] = da
        db_ref[...] = db
        act_ref[...] = (silu * bv).astype(BF16)

    tok = lambda w: pl.BlockSpec((tb, w), lambda i: (i, 0))
    return pl.pallas_call(
        body, name="ffn_bwd", grid=(T // tb,),
        in_specs=[tok(D_MODEL), tok(D_MODEL), tok(D_FF), tok(D_FF), pl.BlockSpec(g2.shape, lambda i: (0, 0)),
                  _shard_rows(land_b, 0, FF_SHARD), _shard_rows(land_b, FF_SHARD, FF_SHARD),
                  _shard_rows(land_b, 2 * FF_SHARD, FF_SHARD)],
        out_specs=[tok(D_MODEL), tok(D_MODEL), tok(D_FF), tok(D_FF), tok(D_FF), pl.BlockSpec((8, D_MODEL), lambda i: (0, 0))],
        out_shape=[jax.ShapeDtypeStruct((T, D_MODEL), F32), jax.ShapeDtypeStruct((T, D_MODEL), BF16)]
        + [jax.ShapeDtypeStruct((T, D_FF), BF16)] * 3 + [jax.ShapeDtypeStruct((8, D_MODEL), F32)],
        compiler_params=_params(("arbitrary",)),
    )(dx2, x1, a, b, g2, land_b, land_b, land_b)


def _wgrad_share(a, b, parts, first, name):
    T = b.shape[0]
    rows = a.shape[1] // N_CHIPS
    assert first % rows == 0 and b.shape[1] == parts.shape[2]
    bk = min(T, 1024)
    n_k = T // bk
    group = 2
    assert (group * rows) % LANES == 0

    def body(a_ref, b_ref, parts_ref, o_ref, acc_ref):
        kk = pl.program_id(1)

        @pl.when(kk == 0)
        def _():
            acc_ref[...] = jnp.zeros_like(acc_ref)

        acc_ref[...] += lax.dot_general(a_ref[...], b_ref[...], TN, preferred_element_type=F32)

        @pl.when(kk == n_k - 1)
        def _():
            for s in range(group):
                o_ref[s] = acc_ref[rows * s:rows * (s + 1), :].astype(BF16)

    return pl.pallas_call(
        body, name=name, grid=(N_CHIPS // group, n_k),
        in_specs=[pl.BlockSpec((bk, group * rows), lambda i, kk: (kk, i)),
                  pl.BlockSpec((bk, b.shape[1]), lambda i, kk: (kk, 0)), _ANY],
        out_specs=pl.BlockSpec((group, rows, b.shape[1]), lambda i, kk: (i, first // rows, 0)),
        out_shape=jax.ShapeDtypeStruct(parts.shape, BF16),
        scratch_shapes=[pltpu.VMEM((group * rows, b.shape[1]), F32)],
        input_output_aliases={2: 0},
        compiler_params=_params(("parallel", "arbitrary")),
    )(a, b, parts)


def _mix_out_bwd(dx1, o, z, sc_in, land_a, gn, scw, gs):
    T = dx1.shape[0]
    tb = 256

    def body(dx_ref, o_ref, z_ref, sc_ref, halo_ref, w_ref, gn_ref, scw_ref, gs_ref,
             do_ref, dz_ref, dgb_ref, dcv_ref, dgn_ref, dgs_ref, dscw_ref):
        @pl.when(pl.program_id(0) == 0)
        def _():
            dgn_ref[...] = jnp.zeros_like(dgn_ref)
            dgs_ref[...] = jnp.zeros_like(dgs_ref)
            dscw_ref[...] = jnp.zeros_like(dscw_ref)

        dmix = _mm(dx_ref[...], _whole(w_ref), NT)
        don = dmix[:, :DN_WIDTH]
        dosc = dmix[:, DN_WIDTH:]
        zv = z_ref[...]
        _, oh, rs, sz, gn4 = _dn_out(o_ref[...], zv, gn_ref[...])
        silu_z = zv * sz
        dgn_full = don * oh * silu_z
        dgn_ref[0:1, :] += jnp.sum(sum(dgn_full[:, HEAD_DIM * hh:HEAD_DIM * (hh + 1)] for hh in range(HEADS)),
                                   axis=0, keepdims=True)
        dz_ref[...] = (don * oh * gn4 * (sz * (1.0 + zv * (1.0 - sz)))).astype(BF16)
        t = don * gn4 * silu_z
        for hh in range(HEADS):
            sl = slice(HEAD_DIM * hh, HEAD_DIM * (hh + 1))
            th, ohh = t[:, sl], oh[:, sl]
            do_ref[:, sl] = rs[hh] * (th - ohh * jnp.mean(th * ohh, axis=-1, keepdims=True))
        halo = jnp.where(pl.program_id(0) > 0, halo_ref[...], 0.0)
        u, cv, gate_b, yh, rys = _sc_fwd(sc_ref[...], halo, scw_ref[...], tb)
        _row_acc(dgs_ref, dosc * yh)
        ty = dosc * gs_ref[...]
        gw = SC_WIDTH // SC_GROUPS
        dys = []
        for gi in range(SC_GROUPS):
            sl = slice(gw * gi, gw * (gi + 1))
            tg, yg = ty[:, sl], yh[:, sl]
            dys.append(rys[gi] * (tg - yg * jnp.mean(tg * yg, axis=-1, keepdims=True)))
        dy = jnp.concatenate(dys, axis=1)
        dgb_ref[...] = dy * cv
        dcv = dy * gate_b
        dcv_ref[...] = dcv
        for j in range(3):
            dscw_ref[j:j + 1, :] += jnp.sum(dcv * _rows_from(u, 6 + j, tb), axis=0, keepdims=True)

    tok = lambda w: pl.BlockSpec((tb, w), lambda i: (i, 0))
    full = lambda t: pl.BlockSpec(t.shape, lambda i: (0, 0))
    acc = lambda w: pl.BlockSpec((8, w), lambda i: (0, 0))
    return pl.pallas_call(
        body, name="mix_out_bwd", grid=(T // tb,),
        in_specs=[tok(D_MODEL), tok(DN_WIDTH), tok(DN_WIDTH), tok(3 * SC_WIDTH),
                  pl.BlockSpec((8, 3 * SC_WIDTH), _before_halo(tb)), _shard_rows(land_a, A_OUT_AT, OUT_SHARD),
                  full(gn), full(scw), full(gs)],
        out_specs=[tok(DN_WIDTH), tok(DN_WIDTH), tok(SC_WIDTH), tok(SC_WIDTH), acc(HEAD_DIM), acc(SC_WIDTH), acc(SC_WIDTH)],
        out_shape=[jax.ShapeDtypeStruct((T, DN_WIDTH), F32), jax.ShapeDtypeStruct((T, DN_WIDTH), BF16),
                   jax.ShapeDtypeStruct((T, SC_WIDTH), F32), jax.ShapeDtypeStruct((T, SC_WIDTH), F32),
                   jax.ShapeDtypeStruct((8, HEAD_DIM), F32), jax.ShapeDtypeStruct((8, SC_WIDTH), F32),
                   jax.ShapeDtypeStruct((8, SC_WIDTH), F32)],
        compiler_params=_params(("arbitrary",)),
    )(dx1, o, z, sc_in, sc_in, land_a, gn, scw, gs)


def _delta_bwd(q, k, v, bg, states, do):
    T = q.shape[0]
    tb = 512
    n_chunk = tb // CHUNK
    nb = T // tb

    def body(q_ref, k_ref, v_ref, bg_ref, st_ref, do_ref, dq_ref, dk_ref, dv_ref, dbg_ref, ds_ref):
        @pl.when(pl.program_id(0) == 0)
        def _():
            ds_ref[...] = jnp.zeros_like(ds_ref)

        masks = _chunk_masks()
        causal, strict = masks
        lane = lax.broadcasted_iota(jnp.int32, (CHUNK, LANES), 1)
        last_row = lax.broadcasted_iota(jnp.int32, (CHUNK, 1), 0) == CHUNK - 1
        cat = jnp.concatenate
        heads = range(HEADS)

        def open_chunk(ci, loc):
            rows = pl.ds(pl.multiple_of(ci * CHUNK, CHUNK), CHUNK)
            dov = do_ref[rows, :]
            return dict(rows=rows, loc=loc, do=[dov[:, HEAD_DIM * h:HEAD_DIM * (h + 1)] for h in heads],
                        state=[st_ref[ci, h] for h in heads])

        def a_free(c):
            loc, do, state = c["loc"], c["do"], c["state"]
            w_s = [_mm(p["w"], s) for p, s in zip(loc, state)]
            c["dq_dec"] = [_mm(d, s, NT) for d, s in zip(do, state)]
            c["qk_do"] = [_mm(p["qk"], d, TN) for p, d in zip(loc, do)]
            c["qd_do"] = [_mm(p["q_dec"], d, TN) for p, d in zip(loc, do)]
            c["v_new"] = [p["u"] - t for p, t in zip(loc, w_s)]
            c["dqk"] = [jnp.where(causal, _mm(d, vn, NT), 0.0) for d, vn in zip(do, c["v_new"])]

        def a_state(c, ds_next):
            c["ds_next"] = ds_next
            kd_ds = [_mm(p["k_dec"], d) for p, d in zip(c["loc"], ds_next)]
            c["dk_dec"] = [_mm(vn, d, NT) for vn, d in zip(c["v_new"], ds_next)]
            c["dv_new"] = [a + b for a, b in zip(c["qk_do"], kd_ds)]

        def b_state(c):
            loc = c["loc"]
            w_dv = [_mm(p["w"], dvn, TN) for p, dvn in zip(loc, c["dv_new"])]
            c["dw"] = [-_mm(dvn, s, NT) for dvn, s in zip(c["dv_new"], c["state"])]
            return [loc[h]["gl"] * c["ds_next"][h] + c["qd_do"][h] - w_dv[h] for h in heads]

        def c_solve(c):
            loc, dv_new, dw = c["loc"], c["dv_new"], c["dw"]
            c["dtm"] = [_mm(cat([dvn, d], axis=1), cat([p["vb"], p["kbg"]], axis=1), NT) for dvn, d, p in zip(dv_new, dw, loc)]
            x_t = [_mm(p["xm"], cat([dvn, d], axis=1), TN) for p, dvn, d in zip(loc, dv_new, dw)]
            c["dvb"] = [dvn + t[:, :HEAD_DIM] for dvn, t in zip(dv_new, x_t)]
            c["dkbg"] = [d + t[:, HEAD_DIM:] for d, t in zip(dw, x_t)]

        def d_solve(c):
            c["y"] = [t + _mm(p["xm"], t, TN) for p, t in zip(c["loc"], c["dtm"])]

        def e_solve(c):
            c["dlow"] = [jnp.where(strict, -(t + _mm(t, p["xm"], NT)), 0.0) for p, t in zip(c["loc"], c["y"])]

        def f_close(c):
            loc, rows = c["loc"], c["rows"]
            dmm = [d * p["decay"] for d, p in zip(c["dlow"], loc)]
            dnn = [d * p["decay"] for d, p in zip(c["dqk"], loc)]
            by_k = [_mm(cat([a, b], axis=0), p["k"]) for a, b, p in zip(dmm, dnn, loc)]
            dk_mm = [_mm(cat([a, b], axis=0), cat([p["kb"], p["q"]], axis=0), TN) for a, b, p in zip(dmm, dnn, loc)]
            dq_out, dk_out, dv_out = [], [], []
            dbeta_all = jnp.zeros((CHUNK, LANES), F32)
            dgc_all = jnp.zeros((CHUNK, LANES), F32)
            for h in heads:
                p = loc[h]
                dkb = by_k[h][:CHUNK] + c["dkbg"][h] * p["eg"]
                dq_out.append(by_k[h][CHUNK:] + c["dq_dec"][h] * p["eg"])
                dk_out.append(dk_mm[h] + c["dk_dec"][h] * p["ek"] + dkb * p["beta"])
                dv_out.append(c["dvb"][h] * p["beta"])
                dbeta = jnp.sum(dkb * p["k"] + c["dvb"][h] * p["v"], axis=1, keepdims=True)
                e = c["dlow"][h] * p["low"] + c["dqk"][h] * p["qk"]
                kd = jnp.sum(c["dk_dec"][h] * p["k_dec"], axis=1, keepdims=True)
                dgc = (jnp.sum(e, axis=1, keepdims=True) - jnp.sum(e.T, axis=1, keepdims=True)
                       + jnp.sum(c["dq_dec"][h] * p["q_dec"], axis=1, keepdims=True) - kd
                       + jnp.sum(c["dkbg"][h] * p["kbg"], axis=1, keepdims=True))
                dgl = jnp.sum(jnp.sum(c["ds_next"][h] * c["state"][h], axis=1, keepdims=True), axis=0, keepdims=True)
                d_last = jnp.sum(kd, axis=0, keepdims=True) + dgl * p["gl"]
                dgc = dgc + jnp.where(last_row, d_last, 0.0)
                dbeta_all = jnp.where(lane == h, dbeta, dbeta_all)
                dgc_all = jnp.where(lane == h + HEADS, dgc, dgc_all)
            dq_ref[rows, :] = cat(dq_out, axis=1)
            dk_ref[rows, :] = cat(dk_out, axis=1)
            dv_ref[rows, :] = cat(dv_out, axis=1)
            dbg_ref[rows, :] = dbeta_all + dgc_all

        def pair(pj, carry):
            hi = n_chunk - 1 - 2 * pj
            lo = hi - 1
            rows = [pl.ds(pl.multiple_of(ci * CHUNK, CHUNK), CHUNK) for ci in (hi, lo)]
            loc = _units_local(_chunk_units(q_ref, k_ref, v_ref, bg_ref, rows[0])
                               + _chunk_units(q_ref, k_ref, v_ref, bg_ref, rows[1]), masks)
            c_hi, c_lo = open_chunk(hi, loc[:HEADS]), open_chunk(lo, loc[HEADS:])
            a_free(c_hi)
            a_free(c_lo)
            a_state(c_hi, [ds_ref[h] for h in heads])
            ds_mid = b_state(c_hi)
            a_state(c_lo, ds_mid)
            c_solve(c_hi)
            ds_out = b_state(c_lo)
            for h in heads:
                ds_ref[h] = ds_out[h]
            d_solve(c_hi)
            c_solve(c_lo)
            e_solve(c_hi)
            d_solve(c_lo)
            f_close(c_hi)
            e_solve(c_lo)
            f_close(c_lo)
            return carry

        lax.fori_loop(0, n_chunk // 2, pair, 0)

    tok = lambda w: pl.BlockSpec((tb, w), lambda i: (nb - 1 - i, 0))
    return pl.pallas_call(
        body, name="delta_bwd", grid=(nb,),
        in_specs=[tok(DN_WIDTH), tok(DN_WIDTH), tok(DN_WIDTH), tok(LANES),
                  pl.BlockSpec((n_chunk, HEADS, HEAD_DIM, HEAD_DIM), lambda i: (nb - 1 - i, 0, 0, 0)), tok(DN_WIDTH)],
        out_specs=[tok(DN_WIDTH), tok(DN_WIDTH), tok(DN_WIDTH), tok(LANES)],
        out_shape=[jax.ShapeDtypeStruct((T, DN_WIDTH), F32)] * 3 + [jax.ShapeDtypeStruct((T, LANES), F32)],
        scratch_shapes=[pltpu.VMEM((HEADS, HEAD_DIM, HEAD_DIM), F32)],
        compiler_params=_params(("arbitrary",)),
    )(q, k, v, bg, states, do)


def _dn_prep_bwd(dq, dk, dv, dbg, qkv, cw, bd, al_row, dt_row):
    T = qkv.shape[0]
    tb = 256

    def body(dq_ref, dk_ref, dv_ref, dbg_ref, pre_ref, halo_ref, cw_ref, bd_ref, al_ref, dt_ref,
             dc_ref, dbd_ref, dcw_ref, dal_ref, ddt_ref):
        @pl.when(pl.program_id(0) == 0)
        def _():
            dcw_ref[...] = jnp.zeros_like(dcw_ref)
            dal_ref[...] = jnp.zeros_like(dal_ref)
            ddt_ref[...] = jnp.zeros_like(ddt_ref)

        halo = jnp.where(pl.program_id(0) > 0, halo_ref[...], 0.0)
        xc, c, sg, a = _dn_act(pre_ref[...], halo, cw_ref[...], tb)
        dsilu = sg * (1.0 + c * (1.0 - sg))
        for hh in range(HEADS):
            sl = slice(HEAD_DIM * hh, HEAD_DIM * (hh + 1))
            for base, g_ref, scale in ((0, dq_ref, Q_SCALE), (DN_WIDTH, dk_ref, 1.0)):
                sa = slice(base + HEAD_DIM * hh, base + HEAD_DIM * (hh + 1))
                raw = a[:, sa]
                r = lax.rsqrt(jnp.sum(raw * raw, axis=-1, keepdims=True) + EPS)
                nrm = raw * r
                gn_ = g_ref[:, sl] * scale
                dc_ref[:, sa] = r * (gn_ - nrm * jnp.sum(gn_ * nrm, axis=-1, keepdims=True)) * dsilu[:, sa]
        dc_ref[:, 2 * DN_WIDTH:] = dv_ref[...] * dsilu[:, 2 * DN_WIDTH:]
        dc = dc_ref[...]
        for j in range(4):
            dcw_ref[j:j + 1, :] += jnp.sum(dc * _rows_from(xc, 5 + j, tb), axis=0, keepdims=True)
        bdv = bd_ref[...]
        lane = lax.broadcasted_iota(jnp.int32, bdv.shape, 1)
        is_b = lane < HEADS
        dbg_in = dbg_ref[...]
        dbgv = jnp.where(is_b, dbg_in, _mm32(_chunk_cumsum_matrix(tb), dbg_in, TN))
        is_g = jnp.logical_and(lane >= HEADS, lane < 2 * HEADS)
        beta = _sigmoid(bdv)
        neg_a = -jnp.exp(al_ref[...])
        pre_sp = bdv + dt_ref[...]
        g = neg_a * _softplus(pre_sp)
        da_in = dbgv * neg_a * _sigmoid(pre_sp)
        dbd_ref[...] = jnp.where(is_b, dbgv * beta * (1.0 - beta), jnp.where(is_g, da_in, 0.0)).astype(BF16)
        _row_acc(dal_ref, jnp.where(is_g, dbgv * g, 0.0))
        _row_acc(ddt_ref, jnp.where(is_g, da_in, 0.0))

    tok = lambda w: pl.BlockSpec((tb, w), lambda i: (i, 0))
    full = lambda t: pl.BlockSpec(t.shape, lambda i: (0, 0))
    acc = lambda w: pl.BlockSpec((8, w), lambda i: (0, 0))
    return pl.pallas_call(
        body, name="dn_prep_bwd", grid=(T // tb,),
        in_specs=[tok(DN_WIDTH), tok(DN_WIDTH), tok(DN_WIDTH), tok(LANES),
                  tok(QKV), pl.BlockSpec((8, QKV), _before_halo(tb)), full(cw), tok(LANES), full(al_row), full(dt_row)],
        out_specs=[tok(QKV), tok(LANES), acc(QKV), acc(LANES), acc(LANES)],
        out_shape=[jax.ShapeDtypeStruct((T, QKV), F32), jax.ShapeDtypeStruct((T, LANES), BF16),
                   jax.ShapeDtypeStruct((8, QKV), F32), jax.ShapeDtypeStruct((8, LANES), F32),
                   jax.ShapeDtypeStruct((8, LANES), F32)],
        compiler_params=_params(("arbitrary",)),
    )(dq, dk, dv, dbg, qkv, qkv, cw, bd, al_row, dt_row)


def _dp_of_chip(dqkv, dz, dbd, dsc, s):
    lo, hi = IN_SHARD * s, IN_SHARD * (s + 1)
    pieces = []
    for w_at, w_end, block in ((0, W_Z, dqkv), (W_Z, W_BD, dz), (W_BD, W_SC, dbd), (W_SC, W_IN_COLS, dsc)):
        a, b = max(lo, w_at), min(hi, w_end)
        if a < b:
            pieces.append(block[:, a - w_at:b - w_at])
    pieces.append(jnp.zeros((dqkv.shape[0], D_MODEL - IN_SHARD), dqkv.dtype))
    return jnp.concatenate(pieces, axis=1)


def _in_proj_bwd(dc, dcv, dgb, sc_in, dz, dbd, cw, scw, dx1, x, g1, land_a):
    T = x.shape[0]
    tb = 256

    def body(dc_ref, dc_halo_ref, dcv_ref, dcv_halo_ref, dgb_ref, sc_ref, dz_ref, dbd_ref, cw_ref, scw_ref,
             dx1_ref, x_ref, g_ref, w_ref, dx_ref, dxb_ref, dps_ref, dg_ref):
        @pl.when(pl.program_id(0) == 0)
        def _():
            dg_ref[...] = jnp.zeros_like(dg_ref)

        last = pl.program_id(0) == pl.num_programs(0) - 1
        xc = jnp.concatenate([dc_ref[...], jnp.where(last, 0.0, dc_halo_ref[...])], axis=0)
        w4 = cw_ref[...]
        dqkv = w4[3:4, :] * xc[0:tb, :]
        for j in range(3):
            dqkv = dqkv + w4[j:j + 1, :] * _rows_from(xc, 3 - j, tb)
        yc = jnp.concatenate([dcv_ref[...], jnp.where(last, 0.0, dcv_halo_ref[...])], axis=0)
        w3 = scw_ref[...]
        du = w3[2:3, :] * yc[0:tb, :] + w3[1:2, :] * _rows_from(yc, 1, tb) + w3[0:1, :] * _rows_from(yc, 2, tb)
        sc = sc_ref[...]
        dsc = jnp.concatenate([dgb_ref[...], du * sc[:, 2 * SC_WIDTH:], du * sc[:, SC_WIDTH:2 * SC_WIDTH]], axis=1)
        blocks = (dqkv.astype(BF16), dz_ref[...], dbd_ref[...], dsc.astype(BF16))
        dh = jnp.zeros((tb, D_MODEL), F32)
        for s in range(N_CHIPS):
            dps = _dp_of_chip(*blocks, s)
            dps_ref[:, D_MODEL * s:D_MODEL * (s + 1)] = dps
            dh = dh + lax.dot_general(dps, w_ref[s], NT, preferred_element_type=F32)
        xv = x_ref[...]
        r = lax.rsqrt(jnp.mean(xv * xv, axis=-1, keepdims=True) + EPS)
        xh = xv * r
        _row_acc(dg_ref, dh * xh)
        dx = dx1_ref[...] + _rms_bwd(dh, xh, r, g_ref[...])
        dx_ref[...] = dx
        dxb_ref[...] = dx.astype(BF16)

    tok = lambda w: pl.BlockSpec((tb, w), lambda i: (i, 0))
    full = lambda t: pl.BlockSpec(t.shape, lambda i: (0, 0))
    halo = lambda w: pl.BlockSpec((8, w), _after_halo(tb, T))
    return pl.pallas_call(
        body, name="in_proj_bwd", grid=(T // tb,),
        in_specs=[tok(QKV), halo(QKV), tok(SC_WIDTH), halo(SC_WIDTH), tok(SC_WIDTH), tok(3 * SC_WIDTH), tok(DN_WIDTH),
                  tok(LANES), full(cw), full(scw), tok(D_MODEL), tok(D_MODEL), full(g1), _shard_rows(land_a, 0, D_MODEL)],
        out_specs=[tok(D_MODEL), tok(D_MODEL), tok(N_CHIPS * D_MODEL), pl.BlockSpec((8, D_MODEL), lambda i: (0, 0))],
        out_shape=[jax.ShapeDtypeStruct((T, D_MODEL), F32), jax.ShapeDtypeStruct((T, D_MODEL), BF16),
                   jax.ShapeDtypeStruct((T, N_CHIPS * D_MODEL), BF16), jax.ShapeDtypeStruct((8, D_MODEL), F32)],
        compiler_params=_params(("arbitrary",)),
    )(dc, dc, dcv, dcv, dgb, sc_in, dz, dbd, cw, scw, dx1, x, g1, land_a)


def _wgrad_in_share(h, dps, parts, name):
    T = h.shape[0]
    bk = min(T, 1024)
    n_k = T // bk

    def body(a_ref, b_ref, parts_ref, o_ref, acc_ref):
        kk = pl.program_id(1)

        @pl.when(kk == 0)
        def _():
            acc_ref[...] = jnp.zeros_like(acc_ref)

        acc_ref[...] += lax.dot_general(a_ref[...], b_ref[...], TN, preferred_element_type=F32)

        @pl.when(kk == n_k - 1)
        def _():
            o_ref[0] = acc_ref[...].astype(BF16)

    return pl.pallas_call(
        body, name=name, grid=(N_CHIPS, n_k),
        in_specs=[pl.BlockSpec((bk, D_MODEL), lambda j, kk: (kk, 0)), pl.BlockSpec((bk, D_MODEL), lambda j, kk: (kk, j)), _ANY],
        out_specs=pl.BlockSpec((1, D_MODEL, D_MODEL), lambda j, kk: (j, 0, 0)),
        out_shape=jax.ShapeDtypeStruct(parts.shape, BF16),
        scratch_shapes=[pltpu.VMEM((D_MODEL, D_MODEL), F32)],
        input_output_aliases={2: 0},
        compiler_params=_params(("parallel", "arbitrary")),
    )(h, dps, parts)


def _pad_rows(a, rows=8):
    return jnp.pad(a, ((0, rows - a.shape[0]), (0, 0)))


def _gate_rows(a_log, dt_bias):
    put = lambda t: jnp.pad(t.reshape(1, HEADS), ((0, 0), (HEADS, LANES - 2 * HEADS)))
    return put(a_log), put(dt_bias)


def _mixer_fwd(x, p):
    qkv, z, sc_in, bd, h = _in_proj(x, p["g1"], p["land_a"])
    q, k, v, bg = _dn_prep(qkv, p["cw"], bd, p["al"], p["dt"])
    o, states = _delta_fwd(q, k, v, bg)
    x1, mix = _mix_out(o, z, sc_in, x, p["land_a"], p["gn"], p["scw"], p["gs"])
    return x1, dict(x=x, qkv=qkv, z=z, sc_in=sc_in, bd=bd, h=h, q=q, k=k, v=v, bg=bg, o=o, states=states, mix=mix)


def _ffn_fwd(x1, p, land_b):
    x2, a, b, h2 = _ffn(x1, p["g2"], land_b)
    return x2, dict(x1=x1, a=a, b=b, h2=h2)


def _ffn_back(dx2, dx2_bf16, s, p, land_b):
    dx1, dx1_bf16, da, db, act, dg2 = _ffn_bwd(dx2, s["x1"], s["a"], s["b"], p["g2"], land_b)
    parts = lax.empty((N_CHIPS, B_ROWS, D_MODEL), BF16)
    parts = _wgrad_share(act, dx2_bf16, parts, 2 * FF_SHARD, "wgrad_down")
    parts = _wgrad_share(da, s["h2"], parts, 0, "wgrad_gate")
    parts = _wgrad_share(db, s["h2"], parts, FF_SHARD, "wgrad_up")
    return dx1, dx1_bf16, parts, dg2[0]


def _mixer_bwd(dx1, dx1_bf16, s, p):
    do, dz, dgb, dcv, dgn, dgs, dscw = _mix_out_bwd(dx1, s["o"], s["z"], s["sc_in"], p["land_a"], p["gn"], p["scw"], p["gs"])
    dq, dk, dv, dbg = _delta_bwd(s["q"], s["k"], s["v"], s["bg"], s["states"], do)
    dc, dbd, dcw, dal, ddt = _dn_prep_bwd(dq, dk, dv, dbg, s["qkv"], p["cw"], s["bd"], p["al"], p["dt"])
    dx, dx_bf16, dps, dg1 = _in_proj_bwd(dc, dcv, dgb, s["sc_in"], dz, dbd, p["cw"], p["scw"], dx1, s["x"], p["g1"], p["land_a"])
    parts = lax.empty((N_CHIPS, A_ROWS, D_MODEL), BF16)
    parts = _wgrad_in_share(s["h"], dps, parts, "wgrad_in")
    parts = _wgrad_share(s["mix"], dx1_bf16, parts, A_OUT_AT, "wgrad_out")
    g = dict(g1=dg1[0], gn=dgn[0], gs=dgs[0], scw=dscw[:3], cw=dcw[:4], al=dal[0, HEADS:2 * HEADS], dt=ddt[0, HEADS:2 * HEADS])
    return dx, dx_bf16, parts, g


def _place():
    return lax.axis_index("x"), lax.axis_index("y"), lax.axis_index("c")


def _other_chips(x, y):
    return [(1 - x, y), (x, 1 - y), (1 - x, 1 - y)]


_HBM = pl.BlockSpec(memory_space=pltpu.HBM)


def _chip_exchange(arrs, name, gather):
    n = len(arrs)

    def body(*refs):
        ins, outs = refs[:n], refs[n:2 * n]
        send_sems, recv_sems, local_sems = refs[2 * n:]
        x, y, c = _place()
        me = 2 * x + y
        others = _other_chips(x, y)

        def remote(k, j, landing):
            px, py = others[j]
            src = ins[k] if gather else ins[k].at[2 * px + py]
            return pltpu.make_async_remote_copy(src_ref=src, dst_ref=outs[k].at[landing], send_sem=send_sems.at[k, j],
                                                recv_sem=recv_sems.at[k, j], device_id=(px, py, c), device_id_type=MESH)

        local = [pltpu.make_async_copy(ins[k] if gather else ins[k].at[me], outs[k].at[me], local_sems.at[k])
                 for k in range(n)]
        sends = [remote(k, j, me) for k in range(n) for j in range(3)]
        for cp in local + sends:
            cp.start()
        for k in range(n):
            for j, (px, py) in enumerate(others):
                remote(k, j, 2 * px + py).wait_recv()
        for cp in sends:
            cp.wait_send()
        for cp in local:
            cp.wait()

    shapes = [jax.ShapeDtypeStruct(((N_CHIPS,) + a.shape) if gather else a.shape, a.dtype) for a in arrs]
    return pl.pallas_call(
        body, name=name, in_specs=[_HBM] * n, out_specs=[_HBM] * n, out_shape=shapes,
        scratch_shapes=[pltpu.SemaphoreType.DMA((n, 3)), pltpu.SemaphoreType.DMA((n, 3)), pltpu.SemaphoreType.DMA((n,))],
    )(*arrs)


_SEM = pl.BlockSpec(memory_space=pltpu.SEMAPHORE)
_ANY = pl.BlockSpec(memory_space=pl.ANY)
_EFFECT = pltpu.SideEffectType.DATAFLOW_SIDE_EFFECTING


_FLIPS = [(a, b, cc) for a in (0, 1) for b in (0, 1) for cc in (0, 1)][1:]


def _split_copies(src_ref, land_ref, send_sems, recv_sems, gather, sending):
    x, y, c = _place()
    copies = []
    if gather:
        me = 2 * x + y
        for j, (px, py) in enumerate(_other_chips(x, y)):
            copies.append(pltpu.make_async_remote_copy(
                src_ref=src_ref, dst_ref=land_ref.at[me if sending else 2 * px + py],
                send_sem=send_sems.at[j], recv_sem=recv_sems.at[j], device_id=(px, py, c), device_id_type=MESH))
        return copies
    me = 4 * x + 2 * y + c
    for j, (a, b, cc) in enumerate(_FLIPS):
        px, py, pc = (1 - x) if a else x, (1 - y) if b else y, (1 - c) if cc else c
        copies.append(pltpu.make_async_remote_copy(
            src_ref=src_ref.at[2 * px + py], dst_ref=land_ref.at[me if sending else 4 * px + 2 * py + pc],
            send_sem=send_sems.at[j], recv_sem=recv_sems.at[j], device_id=(px, py, pc), device_id_type=MESH))
    return copies


def _own_slot(share):
    chip = 2 * lax.axis_index("x") + lax.axis_index("y")
    return lax.dynamic_update_slice(lax.empty((N_CHIPS,) + share.shape, share.dtype), share[None], (chip, 0, 0))


def _own_part(parts):
    chip = 2 * lax.axis_index("x") + lax.axis_index("y")
    own = lax.dynamic_index_in_dim(parts, chip, 0, keepdims=True)
    return lax.dynamic_update_slice(lax.empty((N_DEV,) + parts.shape[1:], parts.dtype), own,
                                    (2 * chip + lax.axis_index("c"), 0, 0))


def _exchange_start(src, land, after, name, gather):
    def body(src_ref, land_ref, after_ref, send_sems, recv_sems, src_thru, land_thru, token):
        for cp in _split_copies(src_ref, land_ref, send_sems, recv_sems, gather, sending=True):
            cp.start()
        token[...] = jnp.zeros_like(token)

    hbm = lambda t: pltpu.with_memory_space_constraint(t, pltpu.HBM)
    n_copies = N_CHIPS - 1 if gather else N_DEV - 1
    return pl.pallas_call(
        body, name=name,
        out_shape=(pltpu.SemaphoreType.DMA((n_copies,)), pltpu.SemaphoreType.DMA((n_copies,)), pltpu.HBM(src.shape, src.dtype),
                   pltpu.HBM(land.shape, land.dtype), jax.ShapeDtypeStruct((8, LANES), F32)),
        in_specs=(_HBM, _HBM, _ANY), out_specs=(_SEM, _SEM, _HBM, _HBM, pl.BlockSpec(memory_space=pltpu.VMEM)),
        input_output_aliases={0: 2, 1: 3},
        compiler_params=pltpu.CompilerParams(has_side_effects=_EFFECT),
    )(hbm(src), hbm(land), after)


def _exchange_wait(started, after, name, gather):
    send_sems, recv_sems, src_thru, land_thru, _ = started

    def body(src_ref, land_ref, send_sems, recv_sems, after_ref, src_dead, got_ref):
        for cp in _split_copies(src_ref, land_ref, send_sems, recv_sems, gather, sending=False):
            cp.wait_send()
            cp.wait_recv()

    return pl.pallas_call(
        body, name=name,
        out_shape=(pltpu.HBM(src_thru.shape, src_thru.dtype), pltpu.HBM(land_thru.shape, land_thru.dtype)),
        in_specs=(_HBM, _HBM, _SEM, _SEM, _ANY), out_specs=(_HBM, _HBM), input_output_aliases={0: 0, 1: 1},
        compiler_params=pltpu.CompilerParams(has_side_effects=_EFFECT),
    )(src_thru, land_thru, send_sems, recv_sems, after)[1]


def _all_reduce_small(v):
    rows = v.shape[0]
    flips = [(a, b, cc) for a in (0, 1) for b in (0, 1) for cc in (0, 1)][1:]

    def body(v_ref, out_ref, buf_ref, send_sems, recv_sems):
        x, y, c = _place()
        me = 4 * x + 2 * y + c
        peers = [((1 - x) if a else x, (1 - y) if b else y, (1 - c) if cc else c) for a, b, cc in flips]

        def copy(j, landing):
            return pltpu.make_async_remote_copy(src_ref=v_ref, dst_ref=buf_ref.at[landing], send_sem=send_sems.at[j],
                                                recv_sem=recv_sems.at[j], device_id=peers[j], device_id_type=MESH)

        sends = [copy(j, me) for j in range(N_DEV - 1)]
        for cp in sends:
            cp.start()
        buf_ref[me] = v_ref[...]
        for j, (px, py, pc) in enumerate(peers):
            copy(j, 4 * px + 2 * py + pc).wait_recv()
        for cp in sends:
            cp.wait_send()
        acc = buf_ref[0]
        for d in range(1, N_DEV):
            acc = acc + buf_ref[d]
        out_ref[...] = acc

    vmem = pl.BlockSpec(memory_space=pltpu.VMEM)
    return pl.pallas_call(
        body, name="all_reduce_small", in_specs=[vmem], out_specs=vmem,
        out_shape=jax.ShapeDtypeStruct(v.shape, F32),
        scratch_shapes=[pltpu.VMEM((N_DEV, rows, LANES), F32), pltpu.SemaphoreType.DMA((N_DEV - 1,)),
                        pltpu.SemaphoreType.DMA((N_DEV - 1,))],
    )(v)


def _row_block(*sizes):
    return next(t for t in (128, 64) if all(s % t == 0 for s in sizes))


def _adam_update(w, m, v, g):
    r1 = 1.0 / (1.0 - ADAM_B1 ** ADAM_STEP)
    r2 = 1.0 / (1.0 - ADAM_B2 ** ADAM_STEP)
    m_new = ADAM_B1 * m + (1.0 - ADAM_B1) * g
    v_new = ADAM_B2 * v + (1.0 - ADAM_B2) * (g * g)
    return -ADAM_LR * ((m_new * r1) / (jnp.sqrt(v_new * r2) + ADAM_EPS) + ADAM_WD * w), m_new, v_new


def _adamw_rows(w, m, v, got, first, name):
    n_layers, rows, cols = w.shape
    tr = _row_block(rows, first)

    def body(*refs):
        w_ref, m_ref, v_ref = refs[:3]
        g_out, d_out, m_out, v_out = refs[3 + n_layers:]
        for k in range(n_layers):
            @pl.when(pl.program_id(0) == k)
            def _(p_ref=refs[3 + k]):
                g = p_ref[0].astype(F32)
                for d in range(1, N_DEV):
                    g = g + p_ref[d].astype(F32)
                g = g[:, :cols]
                d_out[0], m_out[0], v_out[0] = _adam_update(w_ref[0], m_ref[0], v_ref[0], g)
                g_out[0] = g

    blk = pl.BlockSpec((1, tr, cols), lambda l, i: (l, i, 0))
    parts = [pl.BlockSpec((N_DEV, tr, got[0].shape[2]), lambda l, i, k=k: (0, jnp.where(l == k, first // tr + i, 0), 0))
             for k in range(n_layers)]
    return pl.pallas_call(
        body, name=name, grid=(n_layers, rows // tr),
        in_specs=[blk] * 3 + parts, out_specs=[blk] * 4,
        out_shape=[jax.ShapeDtypeStruct(w.shape, F32)] * 4,
        compiler_params=_params(("arbitrary", "arbitrary")),
    )(w, m, v, *got)


def _adamw(w, m, v, g_parts, name):
    rows, cols = w.shape
    tr = min(rows, 256)
    n = len(g_parts)

    def body(*refs):
        w_ref, m_ref, v_ref = refs[:3]
        g_refs = refs[3:3 + n]
        g_out, d_out, m_out, v_out = refs[3 + n:]
        g = g_refs[0][...]
        for r in g_refs[1:]:
            g = g + r[...]
        d_out[...], m_out[...], v_out[...] = _adam_update(w_ref[...], m_ref[...], v_ref[...], g)
        g_out[...] = g

    blk = pl.BlockSpec((tr, cols), lambda i: (i, 0))
    return pl.pallas_call(
        body, name=name, grid=(rows // tr,),
        in_specs=[blk] * (3 + n), out_specs=[blk] * 4,
        out_shape=[jax.ShapeDtypeStruct((rows, cols), F32)] * 4,
        compiler_params=_params(("parallel",)),
    )(w, m, v, *g_parts)


def _pack(parts, rows, fill=0.0):
    flat = jnp.concatenate([p.reshape(-1) for p in parts])
    return jnp.pad(flat, (0, rows * LANES - flat.shape[0]), constant_values=fill).reshape(rows, LANES)


def _unpack(packed, shapes):
    flat = packed.reshape(-1)
    out, at = [], 0
    for shp in shapes:
        size = 1
        for s in shp:
            size *= s
        out.append(flat[at:at + size].reshape(shp))
        at += size
    return out


def _packed_rows(shapes):
    total = 0
    for shp in shapes:
        size = 1
        for s in shp:
            size *= s
        total += size
    return -(-total // (8 * LANES)) * 8


def _cols_full(g, l):
    t = g[:, l]
    return jnp.moveaxis(t, 0, 1).reshape(t.shape[1], N_CHIPS * t.shape[2])


def _pad_cols(t):
    return jnp.pad(t, ((0, 0),) * (t.ndim - 1) + ((0, D_MODEL - t.shape[-1]),))


def kernel(x, norm1_g, w_in, dn_conv_w, dn_a_log, dn_dt_bias, dn_norm_g, sc_conv_w, sc_norm_g, w_out, norm2_g, ffn_w_gate, ffn_w_up, ffn_w_down, final_norm_g, loss_target, m_norm1_g, m_w_in, m_dn_conv_w, m_dn_a_log, m_dn_dt_bias, m_dn_norm_g, m_sc_conv_w, m_sc_norm_g, m_w_out, m_norm2_g, m_ffn_w_gate, m_ffn_w_up, m_ffn_w_down, m_final_norm_g, v_norm1_g, v_w_in, v_dn_conv_w, v_dn_a_log, v_dn_dt_bias, v_dn_norm_g, v_sc_conv_w, v_sc_norm_g, v_w_out, v_norm2_g, v_ffn_w_gate, v_ffn_w_up, v_ffn_w_down, v_final_norm_g):
    chip = 2 * lax.axis_index("x") + lax.axis_index("y")

    g_cw, g_scw = _chip_exchange([dn_conv_w, sc_conv_w], "gather_conv", gather=True)

    t_last = lambda t: jnp.swapaxes(t, -1, -2)
    gate_t, up_t = t_last(ffn_w_gate), t_last(ffn_w_up)
    zero_token = jnp.zeros((8, LANES), F32)

    def shares(l, tie):
        share_a = jnp.concatenate([_pad_cols(w_in[l] + tie), w_out[l]], axis=0).astype(BF16)
        share_b = jnp.concatenate([gate_t[l] + tie, up_t[l], ffn_w_down[l]], axis=0).astype(BF16)
        return share_a, _own_slot(share_a), share_b, _own_slot(share_b)

    def gather_start(l, packed, after):
        a = _exchange_start(packed[0], packed[1], after, "gather_a_start_%d" % l, gather=True)
        b = _exchange_start(packed[2], packed[3], a[4], "gather_b_start_%d" % l, gather=True)
        return a, b

    ga, gb = gather_start(0, shares(0, 0.0), g_cw)
    packed = [None] + [shares(l, gb[4][0, 0]) for l in range(1, DEPTH)]
    packed_all = sum(t[0, 0].astype(F32) for p in packed[1:] for t in (p[0], p[2]))
    land_a = _exchange_wait(ga, zero_token + packed_all, "gather_a_wait_0", gather=True)
    act = x[0]
    layers, saved_m, saved_f, lands_b = [], [], [], []
    for l in range(DEPTH):
        hold = 0.0
        if l + 1 < DEPTH:
            ga, gb_next = gather_start(l + 1, packed[l + 1], land_a)
            hold = gb_next[4][0:1, 0:1]
        al, dt = _gate_rows(dn_a_log[l], dn_dt_bias[l])
        layers.append(dict(
            g1=norm1_g[l][None] + hold, cw=_pad_rows(_cols_full(g_cw, l)), al=al, dt=dt,
            gn=dn_norm_g[l][None], scw=_pad_rows(_cols_full(g_scw, l)), gs=sc_norm_g[l][None],
            land_a=land_a, g2=norm2_g[l][None]))
        x1, s = _mixer_fwd(act, layers[l])
        saved_m.append(s)
        lands_b.append(_exchange_wait(gb, x1, "gather_b_wait_%d" % l, gather=True))
        act, s = _ffn_fwd(x1, layers[l], lands_b[l])
        saved_f.append(s)
        if l + 1 < DEPTH:
            land_a = _exchange_wait(ga, act, "gather_a_wait_%d" % (l + 1), gather=True)
            gb = gb_next

    dact, dact_bf16, loss_part, d_final = _loss_head(act, final_norm_g[None], loss_target[0])
    grads, reduce_a, reduce_b = [None] * DEPTH, [None] * DEPTH, [None] * DEPTH
    hold = 0.0
    for l in reversed(range(DEPTH)):
        p = layers[l]
        dx1, dx1_bf16, parts, dg2 = _ffn_back(dact, dact_bf16, saved_f[l], dict(p, g2=p["g2"] + hold), lands_b[l])
        reduce_b[l] = _exchange_start(parts, _own_part(parts), zero_token, "reduce_b_start_%d" % l, gather=False)
        dact, dact_bf16, parts, gm = _mixer_bwd(dx1, dx1_bf16, saved_m[l], dict(p, gn=p["gn"] + reduce_b[l][4][0:1, 0:1]))
        reduce_a[l] = _exchange_start(parts, _own_part(parts), zero_token, "reduce_a_start_%d" % l, gather=False)
        hold = reduce_a[l][4][0:1, 0:1]
        grads[l] = dict(gm, g2=dg2)
    loss = lax.psum(loss_part[0, 0], ("x", "y", "c"))
    stack = lambda key: jnp.stack([grads[l][key] for l in range(DEPTH)])

    got_b = [_exchange_wait(reduce_b[l], reduce_a[0][4], "reduce_b_wait_%d" % l, gather=False)
             for l in reversed(range(DEPTH))][::-1]
    big = dict(
        ffn_w_gate=[t_last(o) for o in _adamw_rows(gate_t, t_last(m_ffn_w_gate), t_last(v_ffn_w_gate), got_b, 0, "adamw_gate")],
        ffn_w_up=[t_last(o) for o in _adamw_rows(up_t, t_last(m_ffn_w_up), t_last(v_ffn_w_up), got_b, FF_SHARD, "adamw_up")],
        ffn_w_down=_adamw_rows(ffn_w_down, m_ffn_w_down, v_ffn_w_down, got_b, 2 * FF_SHARD, "adamw_down"))
    after_b = zero_token + sum(big[n][1][0, 0, 0] for n in ("ffn_w_gate", "ffn_w_up", "ffn_w_down"))
    got_a = [_exchange_wait(reduce_a[l], after_b, "reduce_a_wait_%d" % l, gather=False) for l in reversed(range(DEPTH))][::-1]
    big.update(
        w_in=_adamw_rows(w_in, m_w_in, v_w_in, got_a, 0, "adamw_w_in"),
        w_out=_adamw_rows(w_out, m_w_out, v_w_out, got_a, A_OUT_AT, "adamw_w_out"))

    full_shapes = [(DEPTH, D_MODEL), (DEPTH, D_MODEL), (DEPTH, HEAD_DIM), (DEPTH, SC_WIDTH), (DEPTH, HEADS),
                   (DEPTH, HEADS), (D_MODEL,), (DEPTH, 4, QKV), (DEPTH, 3, SC_WIDTH)]
    small_keys = ("g1", "g2", "gn", "gs", "al", "dt")
    packed = _pack([stack(k) for k in small_keys] + [d_final[0], stack("cw"), stack("scw")], _packed_rows(full_shapes))
    sg = _unpack(_all_reduce_small(packed), full_shapes)
    sg[7] = lax.dynamic_slice_in_dim(sg[7], chip * (QKV // N_CHIPS), QKV // N_CHIPS, axis=2)
    sg[8] = lax.dynamic_slice_in_dim(sg[8], chip * (SC_WIDTH // N_CHIPS), SC_WIDTH // N_CHIPS, axis=2)
    small_names = ("norm1_g", "norm2_g", "dn_norm_g", "sc_norm_g", "dn_a_log", "dn_dt_bias", "final_norm_g",
                   "dn_conv_w", "sc_conv_w")
    sw = (norm1_g, norm2_g, dn_norm_g, sc_norm_g, dn_a_log, dn_dt_bias, final_norm_g, dn_conv_w, sc_conv_w)
    sm = (m_norm1_g, m_norm2_g, m_dn_norm_g, m_sc_norm_g, m_dn_a_log, m_dn_dt_bias, m_final_norm_g, m_dn_conv_w, m_sc_conv_w)
    sv = (v_norm1_g, v_norm2_g, v_dn_norm_g, v_sc_norm_g, v_dn_a_log, v_dn_dt_bias, v_final_norm_g, v_dn_conv_w, v_sc_conv_w)
    shard_shapes = [t.shape for t in sw]
    rows = _packed_rows(shard_shapes)
    outs = _adamw(_pack(sw, rows), _pack(sm, rows), _pack(sv, rows, fill=1.0), [_pack(sg, rows)], "adamw_small")
    small = {name: [] for name in small_names}
    for o in outs:
        for name, t in zip(small_names, _unpack(o, shard_shapes)):
            small[name].append(t)

    order = ("norm1_g", "w_in", "dn_conv_w", "dn_a_log", "dn_dt_bias", "dn_norm_g", "sc_conv_w", "sc_norm_g", "w_out",
             "norm2_g", "ffn_w_gate", "ffn_w_up", "ffn_w_down", "final_norm_g")
    result = {**big, **small}
    return (loss, dact[None], *[result[n][0] for n in order], *[result[n][1] for n in order],
            *[result[n][2] for n in order], *[result[n][3] for n in order])
```

```python
import jax
import jax.numpy as jnp
from jax import lax
from jax.experimental import pallas as pl
from jax.experimental.pallas import tpu as pltpu

F32 = jnp.float32
BF16 = jnp.bfloat16
MESH = pl.DeviceIdType.MESH

D_MODEL = 1024
DEPTH = 4
HEADS = 4
HEAD_DIM = 128
DN_WIDTH = HEADS * HEAD_DIM
SC_WIDTH = 512
SC_GROUPS = 4
D_FF = 2816
CHUNK = 64
QKV = 3 * DN_WIDTH
W_IN_COLS = 4 * DN_WIDTH + 2 * HEADS + 3 * SC_WIDTH
WA_COLS = QKV + DN_WIDTH + 3 * SC_WIDTH
LANES = 128
EPS = 1e-6
Q_SCALE = HEAD_DIM ** -0.5
N_CHIPS = 4
N_DEV = 8
IN_SHARD = W_IN_COLS // N_CHIPS
OUT_SHARD = D_MODEL // N_CHIPS
FF_SHARD = D_FF // N_CHIPS
A_OUT_AT = D_MODEL
A_ROWS = D_MODEL + OUT_SHARD
B_ROWS = 3 * FF_SHARD

ADAM_LR = 0.001
ADAM_B1 = 0.9
ADAM_B2 = 0.999
ADAM_EPS = 1e-08
ADAM_WD = 0.01
ADAM_STEP = 10

VMEM_LIMIT = 56 * 1024 * 1024

NN = (((1,), (0,)), ((), ()))
NT = (((1,), (1,)), ((), ()))
TN = (((0,), (0,)), ((), ()))


def _mm(a, b, dims=NN):
    return lax.dot_general(a.astype(BF16), b.astype(BF16), dims, preferred_element_type=F32)


def _mm32(a, b, dims=NN):
    return lax.dot_general(a, b, dims, preferred_element_type=F32, precision=lax.Precision.HIGHEST)


def _params(sem, vmem=VMEM_LIMIT):
    return pltpu.CompilerParams(dimension_semantics=sem, vmem_limit_bytes=vmem)


def _sigmoid(x):
    return 0.5 * jnp.tanh(0.5 * x) + 0.5


def _softplus(x):
    return jnp.maximum(x, 0.0) + jnp.log1p(jnp.exp(-jnp.abs(x)))


def _row_acc(acc_ref, val):
    acc_ref[0:1, :] += jnp.sum(val, axis=0, keepdims=True)


def _rms_bwd(dh, xh, r, gain):
    dxh = dh * gain
    return r * (dxh - xh * jnp.mean(dxh * xh, axis=-1, keepdims=True))


def _before_halo(tb):
    return lambda i: (jnp.maximum(i * (tb // 8) - 1, 0), 0)


def _after_halo(tb, n_rows):
    last = n_rows // 8 - 1
    return lambda i: (jnp.minimum((i + 1) * (tb // 8), last), 0)


def _rows_from(xc, offset, tb):
    part = offset % 8
    if part:
        xc = pltpu.roll(xc, xc.shape[0] - part, 0)
    return xc[offset - part:offset - part + tb, :]


def _taps(xc, w, n_taps, tb, first):
    out = w[0:1, :] * _rows_from(xc, first, tb)
    for j in range(1, n_taps):
        out = out + w[j:j + 1, :] * _rows_from(xc, first + j, tb)
    return out


W_Z = QKV
W_BD = W_Z + DN_WIDTH
W_SC = W_BD + 2 * HEADS

def _w_in_cols(shards, lo, hi):
    pieces = []
    for s in range(N_CHIPS):
        a, b = max(lo, IN_SHARD * s), min(hi, IN_SHARD * (s + 1))
        if a < b:
            pieces.append(shards[s][:, a - IN_SHARD * s:b - IN_SHARD * s])
    return pieces[0] if len(pieces) == 1 else jnp.concatenate(pieces, axis=1)


def _in_proj(x, g1, land_a, cw, al_row, dt_row):
    T = x.shape[0]
    tb = 256

    def body(x_ref, g_ref, w_ref, cw_ref, al_ref, dt_ref,
             qkv_ref, z_ref, sc_ref, bd_ref, h_ref, q_ref, k_ref, v_ref, bg_ref, tail_ref):
        @pl.when(pl.program_id(0) == 0)
        def _():
            tail_ref[...] = jnp.zeros_like(tail_ref)

        xv = x_ref[...]
        h = (xv * lax.rsqrt(jnp.mean(xv * xv, axis=-1, keepdims=True) + EPS) * g_ref[...]).astype(BF16)
        shards = [jnp.dot(h, w_ref[s], preferred_element_type=F32) for s in range(N_CHIPS)]
        qkv = _w_in_cols(shards, 0, W_Z)
        bd = jnp.concatenate([_w_in_cols(shards, W_BD, W_SC), jnp.zeros((tb, LANES - 2 * HEADS), F32)], axis=1)
        qkv_ref[...] = qkv
        z_ref[...] = _w_in_cols(shards, W_Z, W_BD)
        bd_ref[...] = bd
        sc_ref[...] = _w_in_cols(shards, W_SC, W_IN_COLS)
        h_ref[...] = h
        halo = tail_ref[...]
        tail_ref[...] = qkv[tb - 8:, :]
        _, _, _, a = _dn_act(qkv, halo, cw_ref[...], tb)
        for hd in range(HEADS):
            sl = slice(HEAD_DIM * hd, HEAD_DIM * (hd + 1))
            qs = a[:, sl]
            q_ref[:, sl] = qs * (lax.rsqrt(jnp.sum(qs * qs, axis=-1, keepdims=True) + EPS) * Q_SCALE)
            ks = a[:, DN_WIDTH + HEAD_DIM * hd:DN_WIDTH + HEAD_DIM * (hd + 1)]
            k_ref[:, sl] = ks * lax.rsqrt(jnp.sum(ks * ks, axis=-1, keepdims=True) + EPS)
        v_ref[...] = a[:, 2 * DN_WIDTH:]
        gates = _gates(bd, al_ref[...], dt_ref[...])
        lane = lax.broadcasted_iota(jnp.int32, gates.shape, 1)
        bg_ref[...] = jnp.where(lane < HEADS, gates, _mm32(_chunk_cumsum_matrix(tb), gates))

    tok = lambda w: pl.BlockSpec((tb, w), lambda i: (i, 0))
    full = lambda t: pl.BlockSpec(t.shape, lambda i: (0, 0))
    return pl.pallas_call(
        body, name="in_proj", grid=(T // tb,),
        in_specs=[tok(D_MODEL), full(g1), _shard_rows(land_a, 0, D_MODEL), full(cw), full(al_row), full(dt_row)],
        out_specs=[tok(QKV), tok(DN_WIDTH), tok(3 * SC_WIDTH), tok(LANES), tok(D_MODEL),
                   tok(DN_WIDTH), tok(DN_WIDTH), tok(DN_WIDTH), tok(LANES)],
        out_shape=[jax.ShapeDtypeStruct((T, QKV), F32), jax.ShapeDtypeStruct((T, DN_WIDTH), F32),
                   jax.ShapeDtypeStruct((T, 3 * SC_WIDTH), F32), jax.ShapeDtypeStruct((T, LANES), F32),
                   jax.ShapeDtypeStruct((T, D_MODEL), BF16)]
        + [jax.ShapeDtypeStruct((T, DN_WIDTH), F32)] * 3 + [jax.ShapeDtypeStruct((T, LANES), F32)],
        scratch_shapes=[pltpu.VMEM((8, QKV), F32)],
        compiler_params=_params(("arbitrary",)),
    )(x, g1, land_a, cw, al_row, dt_row)


def _dn_act(pre, halo, cw, tb):
    xc = jnp.concatenate([halo, pre], axis=0)
    c = _taps(xc, cw, 4, tb, 5)
    sg = _sigmoid(c)
    return xc, c, sg, c * sg


def _gates(bd, al_row, dt_row):
    lane = lax.broadcasted_iota(jnp.int32, bd.shape, 1)
    beta = _sigmoid(bd)
    g = -jnp.exp(al_row) * _softplus(bd + dt_row)
    return jnp.where(lane < HEADS, beta, jnp.where(lane < 2 * HEADS, g, 0.0))


def _chunk_masks():
    row = lax.broadcasted_iota(jnp.int32, (CHUNK, CHUNK), 0)
    col = lax.broadcasted_iota(jnp.int32, (CHUNK, CHUNK), 1)
    return row >= col, row > col


def _chunk_cumsum_matrix(n):
    row = lax.broadcasted_iota(jnp.int32, (n, n), 0)
    col = lax.broadcasted_iota(jnp.int32, (n, n), 1)
    return jnp.logical_and(row >= col, row // CHUNK == col // CHUNK).astype(F32)


def _chunk_units(q_ref, k_ref, v_ref, bg_ref, rows):
    bgc = bg_ref[rows, :]
    bg_t = bgc.T
    qv, kv, vv = q_ref[rows, :], k_ref[rows, :], v_ref[rows, :]
    units = []
    for h in range(HEADS):
        sl = slice(HEAD_DIM * h, HEAD_DIM * (h + 1))
        units.append((qv[:, sl], kv[:, sl], vv[:, sl], bgc[:, h:h + 1], bgc[:, HEADS + h:HEADS + h + 1],
                      bg_t[HEADS + h:HEADS + h + 1, :]))
    return units


def _units_local(units, masks):
    causal, strict = masks
    pre = []
    for q, k, v, beta, gc, gr in units:
        kb = k * beta
        eg = jnp.exp(gc)
        g_last = gc[CHUNK - 1:CHUNK, :]
        ek = jnp.exp(g_last - gc)
        pre.append(dict(q=q, k=k, v=v, beta=beta, decay=jnp.exp(jnp.where(causal, gc - gr, -1e30)), kb=kb, vb=v * beta,
                        eg=eg, kbg=kb * eg, ek=ek, gl=jnp.exp(g_last), q_dec=q * eg, k_dec=k * ek))
    both = [_mm(jnp.concatenate([p["kb"], p["q"]], axis=0), p["k"], NT) for p in pre]
    for p, b in zip(pre, both):
        p["low"] = jnp.where(strict, b[:CHUNK] * p["decay"], 0.0)
        p["qk"] = jnp.where(causal, b[CHUNK:] * p["decay"], 0.0)
    xs = [-p["low"] for p in pre]
    pw = [_mm(p["low"], p["low"]) for p in pre]
    for _ in range(4):
        both = [_mm(jnp.concatenate([pp, x], axis=0), pp) for pp, x in zip(pw, xs)]
        xs = [x + pp + b[CHUNK:] for x, pp, b in zip(xs, pw, both)]
        pw = [b[:CHUNK] for b in both]
    last = [_mm(x, pp) for x, pp in zip(xs, pw)]
    xs = [x + pp + b for x, pp, b in zip(xs, pw, last)]
    uw = [_mm(x, jnp.concatenate([p["vb"], p["kbg"]], axis=1)) for x, p in zip(xs, pre)]
    for p, x, b in zip(pre, xs, uw):
        p["xm"] = x
        p["u"] = p["vb"] + b[:, :HEAD_DIM]
        p["w"] = p["kbg"] + b[:, HEAD_DIM:]
    return pre


def _delta_fwd(q, k, v, bg):
    T = q.shape[0]
    tb = 512
    n_chunk = tb // CHUNK

    def body(q_ref, k_ref, v_ref, bg_ref, o_ref, st_ref, s_ref):
        @pl.when(pl.program_id(0) == 0)
        def _():
            s_ref[...] = jnp.zeros_like(s_ref)

        masks = _chunk_masks()

        def pair(pi, carry):
            rows = [pl.ds(pl.multiple_of((2 * pi + j) * CHUNK, CHUNK), CHUNK) for j in range(2)]
            loc = _units_local(_chunk_units(q_ref, k_ref, v_ref, bg_ref, rows[0])
                               + _chunk_units(q_ref, k_ref, v_ref, bg_ref, rows[1]), masks)
            states = [s_ref[h] for h in range(HEADS)]
            for j in range(2):
                lj = loc[HEADS * j:HEADS * (j + 1)]
                ws = [_mm(jnp.concatenate([p["w"], p["q_dec"]], axis=0), s) for p, s in zip(lj, states)]
                v_new = [p["u"] - b[:CHUNK] for p, b in zip(lj, ws)]
                intra = [_mm(p["qk"], vn) for p, vn in zip(lj, v_new)]
                upd = [_mm(p["k_dec"], vn, TN) for p, vn in zip(lj, v_new)]
                o_ref[rows[j], :] = jnp.concatenate([b[CHUNK:] + a for b, a in zip(ws, intra)], axis=1)
                for h in range(HEADS):
                    st_ref[2 * pi + j, h] = states[h]
                states = [p["gl"] * s + d for p, s, d in zip(lj, states, upd)]
            for h in range(HEADS):
                s_ref[h] = states[h]
            return carry

        lax.fori_loop(0, n_chunk // 2, pair, 0)

    tok = lambda w: pl.BlockSpec((tb, w), lambda i: (i, 0))
    return pl.pallas_call(
        body, name="delta_fwd", grid=(T // tb,),
        in_specs=[tok(DN_WIDTH), tok(DN_WIDTH), tok(DN_WIDTH), tok(LANES)],
        out_specs=[tok(DN_WIDTH), pl.BlockSpec((n_chunk, HEADS, HEAD_DIM, HEAD_DIM), lambda i: (i, 0, 0, 0))],
        out_shape=[jax.ShapeDtypeStruct((T, DN_WIDTH), F32),
                   jax.ShapeDtypeStruct((T // CHUNK, HEADS, HEAD_DIM, HEAD_DIM), F32)],
        scratch_shapes=[pltpu.VMEM((HEADS, HEAD_DIM, HEAD_DIM), F32)],
        compiler_params=_params(("arbitrary",)),
    )(q, k, v, bg)


def _dn_out(o, z, gn):
    outs, ohs, rs = [], [], []
    for hh in range(HEADS):
        oh = o[:, HEAD_DIM * hh:HEAD_DIM * (hh + 1)]
        r = lax.rsqrt(jnp.mean(oh * oh, axis=-1, keepdims=True) + EPS)
        ohs.append(oh * r)
        rs.append(r)
    sz = _sigmoid(z)
    oh = jnp.concatenate(ohs, axis=1)
    gn4 = jnp.concatenate([gn] * HEADS, axis=1)
    return oh * gn4 * (z * sz), oh, rs, sz, gn4


def _sc_fwd(sc_in, halo, cw, tb):
    xc = jnp.concatenate([halo, sc_in], axis=0)
    u = xc[:, SC_WIDTH:2 * SC_WIDTH] * xc[:, 2 * SC_WIDTH:]
    cv = _taps(u, cw, 3, tb, 6)
    gate_b = sc_in[:, :SC_WIDTH]
    y = gate_b * cv
    gw = SC_WIDTH // SC_GROUPS
    yhs, rs = [], []
    for gi in range(SC_GROUPS):
        yg = y[:, gw * gi:gw * (gi + 1)]
        r = lax.rsqrt(jnp.mean(yg * yg, axis=-1, keepdims=True) + EPS)
        yhs.append(yg * r)
        rs.append(r)
    return u, cv, gate_b, jnp.concatenate(yhs, axis=1), rs


def _shard_rows(land, first, rows):
    assert first % rows == 0 and land.shape[0] == N_CHIPS
    return pl.BlockSpec((N_CHIPS, rows, land.shape[2]), lambda i: (0, first // rows, 0))


def _whole(w_ref):
    n, rows, cols = w_ref.shape
    return w_ref[...].reshape(n * rows, cols)


def _mix_out(o, z, sc_in, x, land_a, gn, scw, gs):
    T = x.shape[0]
    tb = 256

    def body(o_ref, z_ref, sc_ref, halo_ref, x_ref, w_ref, gn_ref, scw_ref, gs_ref, x1_ref, mix_ref):
        o_n = _dn_out(o_ref[...], z_ref[...], gn_ref[...])[0]
        halo = jnp.where(pl.program_id(0) > 0, halo_ref[...], 0.0)
        yh = _sc_fwd(sc_ref[...], halo, scw_ref[...], tb)[3]
        mix = jnp.concatenate([o_n, yh * gs_ref[...]], axis=1).astype(BF16)
        x1_ref[...] = x_ref[...] + jnp.dot(mix, _whole(w_ref), preferred_element_type=F32)
        mix_ref[...] = mix

    tok = lambda w: pl.BlockSpec((tb, w), lambda i: (i, 0))
    full = lambda a: pl.BlockSpec(a.shape, lambda i: (0, 0))
    return pl.pallas_call(
        body, name="mix_out", grid=(T // tb,),
        in_specs=[tok(DN_WIDTH), tok(DN_WIDTH), tok(3 * SC_WIDTH), pl.BlockSpec((8, 3 * SC_WIDTH), _before_halo(tb)),
                  tok(D_MODEL), _shard_rows(land_a, A_OUT_AT, OUT_SHARD), full(gn), full(scw), full(gs)],
        out_specs=[tok(D_MODEL), tok(D_MODEL)],
        out_shape=[jax.ShapeDtypeStruct((T, D_MODEL), F32), jax.ShapeDtypeStruct((T, D_MODEL), BF16)],
        compiler_params=_params(("parallel",)),
    )(o, z, sc_in, sc_in, x, land_a, gn, scw, gs)


def _ffn(x1, g2, land_b):
    T = x1.shape[0]
    tb = 256

    def body(x_ref, g_ref, wgt_ref, wut_ref, wd_ref, x2_ref, a_ref, b_ref, h_ref):
        xv = x_ref[...]
        r = lax.rsqrt(jnp.mean(xv * xv, axis=-1, keepdims=True) + EPS)
        h = (xv * r * g_ref[...]).astype(BF16)
        a = lax.dot_general(h, _whole(wgt_ref), NT, preferred_element_type=F32)
        b = lax.dot_general(h, _whole(wut_ref), NT, preferred_element_type=F32)
        act = (a * _sigmoid(a) * b).astype(BF16)
        x2_ref[...] = xv + jnp.dot(act, _whole(wd_ref), preferred_element_type=F32)
        a_ref[...] = a.astype(BF16)
        b_ref[...] = b.astype(BF16)
        h_ref[...] = h

    tok = lambda w: pl.BlockSpec((tb, w), lambda i: (i, 0))
    return pl.pallas_call(
        body, name="ffn", grid=(T // tb,),
        in_specs=[tok(D_MODEL), pl.BlockSpec(g2.shape, lambda i: (0, 0)), _shard_rows(land_b, 0, FF_SHARD),
                  _shard_rows(land_b, FF_SHARD, FF_SHARD), _shard_rows(land_b, 2 * FF_SHARD, FF_SHARD)],
        out_specs=[tok(D_MODEL), tok(D_FF), tok(D_FF), tok(D_MODEL)],
        out_shape=[jax.ShapeDtypeStruct((T, D_MODEL), F32), jax.ShapeDtypeStruct((T, D_FF), BF16),
                   jax.ShapeDtypeStruct((T, D_FF), BF16), jax.ShapeDtypeStruct((T, D_MODEL), BF16)],
        compiler_params=_params(("parallel",)),
    )(x1, g2, land_b, land_b, land_b)


def _loss_head(x, gf, target):
    T = x.shape[0]
    tb = 512

    def body(x_ref, g_ref, t_ref, dx_ref, dxb_ref, loss_ref, dg_ref):
        @pl.when(pl.program_id(0) == 0)
        def _():
            loss_ref[...] = jnp.zeros_like(loss_ref)
            dg_ref[...] = jnp.zeros_like(dg_ref)

        xv = x_ref[...]
        r = lax.rsqrt(jnp.mean(xv * xv, axis=-1, keepdims=True) + EPS)
        xh = xv * r
        err = xh * g_ref[...] - t_ref[...]
        per_tok = jnp.mean(err * err, axis=-1, keepdims=True)
        loss_ref[...] += 0.5 * jnp.sum(per_tok, axis=0, keepdims=True)
        dy = err * (1.0 / D_MODEL)
        _row_acc(dg_ref, dy * xh)
        dx = _rms_bwd(dy, xh, r, g_ref[...])
        dx_ref[...] = dx
        dxb_ref[...] = dx.astype(BF16)

    tok = pl.BlockSpec((tb, D_MODEL), lambda i: (i, 0))
    return pl.pallas_call(
        body, name="loss_head", grid=(T // tb,),
        in_specs=[tok, pl.BlockSpec(gf.shape, lambda i: (0, 0)), tok],
        out_specs=[tok, tok, pl.BlockSpec((8, LANES), lambda i: (0, 0)), pl.BlockSpec((8, D_MODEL), lambda i: (0, 0))],
        out_shape=[jax.ShapeDtypeStruct((T, D_MODEL), F32), jax.ShapeDtypeStruct((T, D_MODEL), BF16),
                   jax.ShapeDtypeStruct((8, LANES), F32), jax.ShapeDtypeStruct((8, D_MODEL), F32)],
        compiler_params=_params(("arbitrary",)),
    )(x, gf, target)


def _ffn_bwd(dx2, x1, a, b, g2, land_b):
    T = x1.shape[0]
    tb = 256

    def body(dx2_ref, x_ref, a_ref, b_ref, g_ref, wgt_ref, wut_ref, wd_ref,
             dx1_ref, dx1b_ref, da_ref, db_ref, act_ref, dg_ref):
        @pl.when(pl.program_id(0) == 0)
        def _():
            dg_ref[...] = jnp.zeros_like(dg_ref)

        dx2v = dx2_ref[...]
        av = a_ref[...].astype(F32)
        bv = b_ref[...].astype(F32)
        dact = _mm(dx2v, _whole(wd_ref), NT)
        sa = _sigmoid(av)
        silu = av * sa
        da = (dact * bv * (sa * (1.0 + av * (1.0 - sa)))).astype(BF16)
        db = (dact * silu).astype(BF16)
        dh = _mm(da, _whole(wgt_ref)) + _mm(db, _whole(wut_ref))
        xv = x_ref[...]
        r = lax.rsqrt(jnp.mean(xv * xv, axis=-1, keepdims=True) + EPS)
        xh = xv * r
        _row_acc(dg_ref, dh * xh)
        dx1 = dx2v + _rms_bwd(dh, xh, r, g_ref[...])
        dx1_ref[...] = dx1
        dx1b_ref[...] = dx1.astype(BF16)
        da_ref[...] = da
        db_ref[...] = db
        act_ref[...] = (silu * bv).astype(BF16)

    tok = lambda w: pl.BlockSpec((tb, w), lambda i: (i, 0))
    return pl.pallas_call(
        body, name="ffn_bwd", grid=(T // tb,),
        in_specs=[tok(D_MODEL), tok(D_MODEL), tok(D_FF), tok(D_FF), pl.BlockSpec(g2.shape, lambda i: (0, 0)),
                  _shard_rows(land_b, 0, FF_SHARD), _shard_rows(land_b, FF_SHARD, FF_SHARD),
                  _shard_rows(land_b, 2 * FF_SHARD, FF_SHARD)],
        out_specs=[tok(D_MODEL), tok(D_MODEL), tok(D_FF), tok(D_FF), tok(D_FF), pl.BlockSpec((8, D_MODEL), lambda i: (0, 0))],
        out_shape=[jax.ShapeDtypeStruct((T, D_MODEL), F32), jax.ShapeDtypeStruct((T, D_MODEL), BF16)]
        + [jax.ShapeDtypeStruct((T, D_FF), BF16)] * 3 + [jax.ShapeDtypeStruct((8, D_MODEL), F32)],
        compiler_params=_params(("arbitrary",)),
    )(dx2, x1, a, b, g2, land_b, land_b, land_b)


def _wgrad_share(a, b, parts, first, name):
    T = b.shape[0]
    rows = a.shape[1] // N_CHIPS
    assert first % rows == 0 and b.shape[1] == parts.shape[2]
    bk = min(T, 1024)
    n_k = T // bk
    group = 2
    assert (group * rows) % LANES == 0

    def body(a_ref, b_ref, parts_ref, o_ref, acc_ref):
        kk = pl.program_id(1)

        @pl.when(kk == 0)
        def _():
            acc_ref[...] = jnp.zeros_like(acc_ref)

        acc_ref[...] += lax.dot_general(a_ref[...], b_ref[...], TN, preferred_element_type=F32)

        @pl.when(kk == n_k - 1)
        def _():
            for s in range(group):
                o_ref[s] = acc_ref[rows * s:rows * (s + 1), :].astype(BF16)

    return pl.pallas_call(
        body, name=name, grid=(N_CHIPS // group, n_k),
        in_specs=[pl.BlockSpec((bk, group * rows), lambda i, kk: (kk, i)),
                  pl.BlockSpec((bk, b.shape[1]), lambda i, kk: (kk, 0)), _ANY],
        out_specs=pl.BlockSpec((group, rows, b.shape[1]), lambda i, kk: (i, first // rows, 0)),
        out_shape=jax.ShapeDtypeStruct(parts.shape, BF16),
        scratch_shapes=[pltpu.VMEM((group * rows, b.shape[1]), F32)],
        input_output_aliases={2: 0},
        compiler_params=_params(("parallel", "arbitrary")),
    )(a, b, parts)


def _mix_out_bwd(dx1, o, z, sc_in, land_a, gn, scw, gs):
    T = dx1.shape[0]
    tb = 256

    def body(dx_ref, o_ref, z_ref, sc_ref, halo_ref, w_ref, gn_ref, scw_ref, gs_ref,
             do_ref, dz_ref, dgb_ref, dcv_ref, dgn_ref, dgs_ref, dscw_ref):
        @pl.when(pl.program_id(0) == 0)
        def _():
            dgn_ref[...] = jnp.zeros_like(dgn_ref)
            dgs_ref[...] = jnp.zeros_like(dgs_ref)
            dscw_ref[...] = jnp.zeros_like(dscw_ref)

        dmix = _mm(dx_ref[...], _whole(w_ref), NT)
        don = dmix[:, :DN_WIDTH]
        dosc = dmix[:, DN_WIDTH:]
        zv = z_ref[...]
        _, oh, rs, sz, gn4 = _dn_out(o_ref[...], zv, gn_ref[...])
        silu_z = zv * sz
        dgn_full = don * oh * silu_z
        dgn_ref[0:1, :] += jnp.sum(sum(dgn_full[:, HEAD_DIM * hh:HEAD_DIM * (hh + 1)] for hh in range(HEADS)),
                                   axis=0, keepdims=True)
        dz_ref[...] = (don * oh * gn4 * (sz * (1.0 + zv * (1.0 - sz)))).astype(BF16)
        t = don * gn4 * silu_z
        for hh in range(HEADS):
            sl = slice(HEAD_DIM * hh, HEAD_DIM * (hh + 1))
            th, ohh = t[:, sl], oh[:, sl]
            do_ref[:, sl] = rs[hh] * (th - ohh * jnp.mean(th * ohh, axis=-1, keepdims=True))
        halo = jnp.where(pl.program_id(0) > 0, halo_ref[...], 0.0)
        u, cv, gate_b, yh, rys = _sc_fwd(sc_ref[...], halo, scw_ref[...], tb)
        _row_acc(dgs_ref, dosc * yh)
        ty = dosc * gs_ref[...]
        gw = SC_WIDTH // SC_GROUPS
        dys = []
        for gi in range(SC_GROUPS):
            sl = slice(gw * gi, gw * (gi + 1))
            tg, yg = ty[:, sl], yh[:, sl]
            dys.append(rys[gi] * (tg - yg * jnp.mean(tg * yg, axis=-1, keepdims=True)))
        dy = jnp.concatenate(dys, axis=1)
        dgb_ref[...] = dy * cv
        dcv = dy * gate_b
        dcv_ref[...] = dcv
        for j in range(3):
            dscw_ref[j:j + 1, :] += jnp.sum(dcv * _rows_from(u, 6 + j, tb), axis=0, keepdims=True)

    tok = lambda w: pl.BlockSpec((tb, w), lambda i: (i, 0))
    full = lambda t: pl.BlockSpec(t.shape, lambda i: (0, 0))
    acc = lambda w: pl.BlockSpec((8, w), lambda i: (0, 0))
    return pl.pallas_call(
        body, name="mix_out_bwd", grid=(T // tb,),
        in_specs=[tok(D_MODEL), tok(DN_WIDTH), tok(DN_WIDTH), tok(3 * SC_WIDTH),
                  pl.BlockSpec((8, 3 * SC_WIDTH), _before_halo(tb)), _shard_rows(land_a, A_OUT_AT, OUT_SHARD),
                  full(gn), full(scw), full(gs)],
        out_specs=[tok(DN_WIDTH), tok(DN_WIDTH), tok(SC_WIDTH), tok(SC_WIDTH), acc(HEAD_DIM), acc(SC_WIDTH), acc(SC_WIDTH)],
        out_shape=[jax.ShapeDtypeStruct((T, DN_WIDTH), F32), jax.ShapeDtypeStruct((T, DN_WIDTH), BF16),
                   jax.ShapeDtypeStruct((T, SC_WIDTH), F32), jax.ShapeDtypeStruct((T, SC_WIDTH), F32),
                   jax.ShapeDtypeStruct((8, HEAD_DIM), F32), jax.ShapeDtypeStruct((8, SC_WIDTH), F32),
                   jax.ShapeDtypeStruct((8, SC_WIDTH), F32)],
        compiler_params=_params(("arbitrary",)),
    )(dx1, o, z, sc_in, sc_in, land_a, gn, scw, gs)


def _delta_bwd(q, k, v, bg, states, do):
    T = q.shape[0]
    tb = 512
    n_chunk = tb // CHUNK
    nb = T // tb

    def body(q_ref, k_ref, v_ref, bg_ref, st_ref, do_ref, dq_ref, dk_ref, dv_ref, dbg_ref, ds_ref):
        @pl.when(pl.program_id(0) == 0)
        def _():
            ds_ref[...] = jnp.zeros_like(ds_ref)

        masks = _chunk_masks()
        causal, strict = masks
        lane = lax.broadcasted_iota(jnp.int32, (CHUNK, LANES), 1)
        last_row = lax.broadcasted_iota(jnp.int32, (CHUNK, 1), 0) == CHUNK - 1
        cat = jnp.concatenate
        heads = range(HEADS)

        def open_chunk(ci, loc):
            rows = pl.ds(pl.multiple_of(ci * CHUNK, CHUNK), CHUNK)
            dov = do_ref[rows, :]
            return dict(rows=rows, loc=loc, do=[dov[:, HEAD_DIM * h:HEAD_DIM * (h + 1)] for h in heads],
                        state=[st_ref[ci, h] for h in heads])

        def a_free(c):
            loc, do, state = c["loc"], c["do"], c["state"]
            w_s = [_mm(p["w"], s) for p, s in zip(loc, state)]
            c["dq_dec"] = [_mm(d, s, NT) for d, s in zip(do, state)]
            c["qk_do"] = [_mm(p["qk"], d, TN) for p, d in zip(loc, do)]
            c["qd_do"] = [_mm(p["q_dec"], d, TN) for p, d in zip(loc, do)]
            c["v_new"] = [p["u"] - t for p, t in zip(loc, w_s)]
            c["dqk"] = [jnp.where(causal, _mm(d, vn, NT), 0.0) for d, vn in zip(do, c["v_new"])]

        def a_state(c, ds_next):
            c["ds_next"] = ds_next
            kd_ds = [_mm(p["k_dec"], d) for p, d in zip(c["loc"], ds_next)]
            c["dk_dec"] = [_mm(vn, d, NT) for vn, d in zip(c["v_new"], ds_next)]
            c["dv_new"] = [a + b for a, b in zip(c["qk_do"], kd_ds)]

        def b_state(c):
            loc = c["loc"]
            w_dv = [_mm(p["w"], dvn, TN) for p, dvn in zip(loc, c["dv_new"])]
            c["dw"] = [-_mm(dvn, s, NT) for dvn, s in zip(c["dv_new"], c["state"])]
            return [loc[h]["gl"] * c["ds_next"][h] + c["qd_do"][h] - w_dv[h] for h in heads]

        def c_solve(c):
            loc, dv_new, dw = c["loc"], c["dv_new"], c["dw"]
            c["dtm"] = [_mm(cat([dvn, d], axis=1), cat([p["vb"], p["kbg"]], axis=1), NT) for dvn, d, p in zip(dv_new, dw, loc)]
            x_t = [_mm(p["xm"], cat([dvn, d], axis=1), TN) for p, dvn, d in zip(loc, dv_new, dw)]
            c["dvb"] = [dvn + t[:, :HEAD_DIM] for dvn, t in zip(dv_new, x_t)]
            c["dkbg"] = [d + t[:, HEAD_DIM:] for d, t in zip(dw, x_t)]

        def d_solve(c):
            c["y"] = [t + _mm(p["xm"], t, TN) for p, t in zip(c["loc"], c["dtm"])]

        def e_solve(c):
            c["dlow"] = [jnp.where(strict, -(t + _mm(t, p["xm"], NT)), 0.0) for p, t in zip(c["loc"], c["y"])]

        def f_close(c):
            loc, rows = c["loc"], c["rows"]
            dmm = [d * p["decay"] for d, p in zip(c["dlow"], loc)]
            dnn = [d * p["decay"] for d, p in zip(c["dqk"], loc)]
            by_k = [_mm(cat([a, b], axis=0), p["k"]) for a, b, p in zip(dmm, dnn, loc)]
            dk_mm = [_mm(cat([a, b], axis=0), cat([p["kb"], p["q"]], axis=0), TN) for a, b, p in zip(dmm, dnn, loc)]
            dq_out, dk_out, dv_out = [], [], []
            dbeta_all = jnp.zeros((CHUNK, LANES), F32)
            dgc_all = jnp.zeros((CHUNK, LANES), F32)
            for h in heads:
                p = loc[h]
                dkb = by_k[h][:CHUNK] + c["dkbg"][h] * p["eg"]
                dq_out.append(by_k[h][CHUNK:] + c["dq_dec"][h] * p["eg"])
                dk_out.append(dk_mm[h] + c["dk_dec"][h] * p["ek"] + dkb * p["beta"])
                dv_out.append(c["dvb"][h] * p["beta"])
                dbeta = jnp.sum(dkb * p["k"] + c["dvb"][h] * p["v"], axis=1, keepdims=True)
                e = c["dlow"][h] * p["low"] + c["dqk"][h] * p["qk"]
                kd = jnp.sum(c["dk_dec"][h] * p["k_dec"], axis=1, keepdims=True)
                dgc = (jnp.sum(e, axis=1, keepdims=True) - jnp.sum(e.T, axis=1, keepdims=True)
                       + jnp.sum(c["dq_dec"][h] * p["q_dec"], axis=1, keepdims=True) - kd
                       + jnp.sum(c["dkbg"][h] * p["kbg"], axis=1, keepdims=True))
                dgl = jnp.sum(jnp.sum(c["ds_next"][h] * c["state"][h], axis=1, keepdims=True), axis=0, keepdims=True)
                d_last = jnp.sum(kd, axis=0, keepdims=True) + dgl * p["gl"]
                dgc = dgc + jnp.where(last_row, d_last, 0.0)
                dbeta_all = jnp.where(lane == h, dbeta, dbeta_all)
                dgc_all = jnp.where(lane == h + HEADS, dgc, dgc_all)
            dq_ref[rows, :] = cat(dq_out, axis=1)
            dk_ref[rows, :] = cat(dk_out, axis=1)
            dv_ref[rows, :] = cat(dv_out, axis=1)
            dbg_ref[rows, :] = dbeta_all + dgc_all

        def pair(pj, carry):
            hi = n_chunk - 1 - 2 * pj
            lo = hi - 1
            rows = [pl.ds(pl.multiple_of(ci * CHUNK, CHUNK), CHUNK) for ci in (hi, lo)]
            loc = _units_local(_chunk_units(q_ref, k_ref, v_ref, bg_ref, rows[0])
                               + _chunk_units(q_ref, k_ref, v_ref, bg_ref, rows[1]), masks)
            c_hi, c_lo = open_chunk(hi, loc[:HEADS]), open_chunk(lo, loc[HEADS:])
            a_free(c_hi)
            a_free(c_lo)
            a_state(c_hi, [ds_ref[h] for h in heads])
            ds_mid = b_state(c_hi)
            a_state(c_lo, ds_mid)
            c_solve(c_hi)
            ds_out = b_state(c_lo)
            for h in heads:
                ds_ref[h] = ds_out[h]
            d_solve(c_hi)
            c_solve(c_lo)
            e_solve(c_hi)
            d_solve(c_lo)
            f_close(c_hi)
            e_solve(c_lo)
            f_close(c_lo)
            return carry

        lax.fori_loop(0, n_chunk // 2, pair, 0)

    tok = lambda w: pl.BlockSpec((tb, w), lambda i: (nb - 1 - i, 0))
    return pl.pallas_call(
        body, name="delta_bwd", grid=(nb,),
        in_specs=[tok(DN_WIDTH), tok(DN_WIDTH), tok(DN_WIDTH), tok(LANES),
                  pl.BlockSpec((n_chunk, HEADS, HEAD_DIM, HEAD_DIM), lambda i: (nb - 1 - i, 0, 0, 0)), tok(DN_WIDTH)],
        out_specs=[tok(DN_WIDTH), tok(DN_WIDTH), tok(DN_WIDTH), tok(LANES)],
        out_shape=[jax.ShapeDtypeStruct((T, DN_WIDTH), F32)] * 3 + [jax.ShapeDtypeStruct((T, LANES), F32)],
        scratch_shapes=[pltpu.VMEM((HEADS, HEAD_DIM, HEAD_DIM), F32)],
        compiler_params=_params(("arbitrary",)),
    )(q, k, v, bg, states, do)


def _dn_prep_bwd(dq, dk, dv, dbg, qkv, cw, bd, al_row, dt_row):
    T = qkv.shape[0]
    tb = 256

    def body(dq_ref, dk_ref, dv_ref, dbg_ref, pre_ref, halo_ref, cw_ref, bd_ref, al_ref, dt_ref,
             dc_ref, dbd_ref, dcw_ref, dal_ref, ddt_ref):
        @pl.when(pl.program_id(0) == 0)
        def _():
            dcw_ref[...] = jnp.zeros_like(dcw_ref)
            dal_ref[...] = jnp.zeros_like(dal_ref)
            ddt_ref[...] = jnp.zeros_like(ddt_ref)

        halo = jnp.where(pl.program_id(0) > 0, halo_ref[...], 0.0)
        xc, c, sg, a = _dn_act(pre_ref[...], halo, cw_ref[...], tb)
        dsilu = sg * (1.0 + c * (1.0 - sg))
        for hh in range(HEADS):
            sl = slice(HEAD_DIM * hh, HEAD_DIM * (hh + 1))
            for base, g_ref, scale in ((0, dq_ref, Q_SCALE), (DN_WIDTH, dk_ref, 1.0)):
                sa = slice(base + HEAD_DIM * hh, base + HEAD_DIM * (hh + 1))
                raw = a[:, sa]
                r = lax.rsqrt(jnp.sum(raw * raw, axis=-1, keepdims=True) + EPS)
                nrm = raw * r
                gn_ = g_ref[:, sl] * scale
                dc_ref[:, sa] = r * (gn_ - nrm * jnp.sum(gn_ * nrm, axis=-1, keepdims=True)) * dsilu[:, sa]
        dc_ref[:, 2 * DN_WIDTH:] = dv_ref[...] * dsilu[:, 2 * DN_WIDTH:]
        dc = dc_ref[...]
        for j in range(4):
            dcw_ref[j:j + 1, :] += jnp.sum(dc * _rows_from(xc, 5 + j, tb), axis=0, keepdims=True)
        bdv = bd_ref[...]
        lane = lax.broadcasted_iota(jnp.int32, bdv.shape, 1)
        is_b = lane < HEADS
        dbg_in = dbg_ref[...]
        dbgv = jnp.where(is_b, dbg_in, _mm32(_chunk_cumsum_matrix(tb), dbg_in, TN))
        is_g = jnp.logical_and(lane >= HEADS, lane < 2 * HEADS)
        beta = _sigmoid(bdv)
        neg_a = -jnp.exp(al_ref[...])
        pre_sp = bdv + dt_ref[...]
        g = neg_a * _softplus(pre_sp)
        da_in = dbgv * neg_a * _sigmoid(pre_sp)
        dbd_ref[...] = jnp.where(is_b, dbgv * beta * (1.0 - beta), jnp.where(is_g, da_in, 0.0)).astype(BF16)
        _row_acc(dal_ref, jnp.where(is_g, dbgv * g, 0.0))
        _row_acc(ddt_ref, jnp.where(is_g, da_in, 0.0))

    tok = lambda w: pl.BlockSpec((tb, w), lambda i: (i, 0))
    full = lambda t: pl.BlockSpec(t.shape, lambda i: (0, 0))
    acc = lambda w: pl.BlockSpec((8, w), lambda i: (0, 0))
    return pl.pallas_call(
        body, name="dn_prep_bwd", grid=(T // tb,),
        in_specs=[tok(DN_WIDTH), tok(DN_WIDTH), tok(DN_WIDTH), tok(LANES),
                  tok(QKV), pl.BlockSpec((8, QKV), _before_halo(tb)), full(cw), tok(LANES), full(al_row), full(dt_row)],
        out_specs=[tok(QKV), tok(LANES), acc(QKV), acc(LANES), acc(LANES)],
        out_shape=[jax.ShapeDtypeStruct((T, QKV), F32), jax.ShapeDtypeStruct((T, LANES), BF16),
                   jax.ShapeDtypeStruct((8, QKV), F32), jax.ShapeDtypeStruct((8, LANES), F32),
                   jax.ShapeDtypeStruct((8, LANES), F32)],
        compiler_params=_params(("arbitrary",)),
    )(dq, dk, dv, dbg, qkv, qkv, cw, bd, al_row, dt_row)


def _dp_of_chip(dqkv, dz, dbd, dsc, s):
    lo, hi = IN_SHARD * s, IN_SHARD * (s + 1)
    pieces = []
    for w_at, w_end, block in ((0, W_Z, dqkv), (W_Z, W_BD, dz), (W_BD, W_SC, dbd), (W_SC, W_IN_COLS, dsc)):
        a, b = max(lo, w_at), min(hi, w_end)
        if a < b:
            pieces.append(block[:, a - w_at:b - w_at])
    pieces.append(jnp.zeros((dqkv.shape[0], D_MODEL - IN_SHARD), dqkv.dtype))
    return jnp.concatenate(pieces, axis=1)


def _in_proj_bwd(dc, dcv, dgb, sc_in, dz, dbd, cw, scw, dx1, x, g1, land_a):
    T = x.shape[0]
    tb = 256

    def body(dc_ref, dc_halo_ref, dcv_ref, dcv_halo_ref, dgb_ref, sc_ref, dz_ref, dbd_ref, cw_ref, scw_ref,
             dx1_ref, x_ref, g_ref, w_ref, dx_ref, dxb_ref, dps_ref, dg_ref):
        @pl.when(pl.program_id(0) == 0)
        def _():
            dg_ref[...] = jnp.zeros_like(dg_ref)

        last = pl.program_id(0) == pl.num_programs(0) - 1
        xc = jnp.concatenate([dc_ref[...], jnp.where(last, 0.0, dc_halo_ref[...])], axis=0)
        w4 = cw_ref[...]
        dqkv = w4[3:4, :] * xc[0:tb, :]
        for j in range(3):
            dqkv = dqkv + w4[j:j + 1, :] * _rows_from(xc, 3 - j, tb)
        yc = jnp.concatenate([dcv_ref[...], jnp.where(last, 0.0, dcv_halo_ref[...])], axis=0)
        w3 = scw_ref[...]
        du = w3[2:3, :] * yc[0:tb, :] + w3[1:2, :] * _rows_from(yc, 1, tb) + w3[0:1, :] * _rows_from(yc, 2, tb)
        sc = sc_ref[...]
        dsc = jnp.concatenate([dgb_ref[...], du * sc[:, 2 * SC_WIDTH:], du * sc[:, SC_WIDTH:2 * SC_WIDTH]], axis=1)
        blocks = (dqkv.astype(BF16), dz_ref[...], dbd_ref[...], dsc.astype(BF16))
        dh = jnp.zeros((tb, D_MODEL), F32)
        for s in range(N_CHIPS):
            dps = _dp_of_chip(*blocks, s)
            dps_ref[:, D_MODEL * s:D_MODEL * (s + 1)] = dps
            dh = dh + lax.dot_general(dps, w_ref[s], NT, preferred_element_type=F32)
        xv = x_ref[...]
        r = lax.rsqrt(jnp.mean(xv * xv, axis=-1, keepdims=True) + EPS)
        xh = xv * r
        _row_acc(dg_ref, dh * xh)
        dx = dx1_ref[...] + _rms_bwd(dh, xh, r, g_ref[...])
        dx_ref[...] = dx
        dxb_ref[...] = dx.astype(BF16)

    tok = lambda w: pl.BlockSpec((tb, w), lambda i: (i, 0))
    full = lambda t: pl.BlockSpec(t.shape, lambda i: (0, 0))
    halo = lambda w: pl.BlockSpec((8, w), _after_halo(tb, T))
    return pl.pallas_call(
        body, name="in_proj_bwd", grid=(T // tb,),
        in_specs=[tok(QKV), halo(QKV), tok(SC_WIDTH), halo(SC_WIDTH), tok(SC_WIDTH), tok(3 * SC_WIDTH), tok(DN_WIDTH),
                  tok(LANES), full(cw), full(scw), tok(D_MODEL), tok(D_MODEL), full(g1), _shard_rows(land_a, 0, D_MODEL)],
        out_specs=[tok(D_MODEL), tok(D_MODEL), tok(N_CHIPS * D_MODEL), pl.BlockSpec((8, D_MODEL), lambda i: (0, 0))],
        out_shape=[jax.ShapeDtypeStruct((T, D_MODEL), F32), jax.ShapeDtypeStruct((T, D_MODEL), BF16),
                   jax.ShapeDtypeStruct((T, N_CHIPS * D_MODEL), BF16), jax.ShapeDtypeStruct((8, D_MODEL), F32)],
        compiler_params=_params(("arbitrary",)),
    )(dc, dc, dcv, dcv, dgb, sc_in, dz, dbd, cw, scw, dx1, x, g1, land_a)


def _wgrad_in_share(h, dps, parts, name):
    T = h.shape[0]
    bk = min(T, 1024)
    n_k = T // bk

    def body(a_ref, b_ref, parts_ref, o_ref, acc_ref):
        kk = pl.program_id(1)

        @pl.when(kk == 0)
        def _():
            acc_ref[...] = jnp.zeros_like(acc_ref)

        acc_ref[...] += lax.dot_general(a_ref[...], b_ref[...], TN, preferred_element_type=F32)

        @pl.when(kk == n_k - 1)
        def _():
            o_ref[0] = acc_ref[...].astype(BF16)

    return pl.pallas_call(
        body, name=name, grid=(N_CHIPS, n_k),
        in_specs=[pl.BlockSpec((bk, D_MODEL), lambda j, kk: (kk, 0)), pl.BlockSpec((bk, D_MODEL), lambda j, kk: (kk, j)), _ANY],
        out_specs=pl.BlockSpec((1, D_MODEL, D_MODEL), lambda j, kk: (j, 0, 0)),
        out_shape=jax.ShapeDtypeStruct(parts.shape, BF16),
        scratch_shapes=[pltpu.VMEM((D_MODEL, D_MODEL), F32)],
        input_output_aliases={2: 0},
        compiler_params=_params(("parallel", "arbitrary")),
    )(h, dps, parts)


def _pad_rows(a, rows=8):
    return jnp.pad(a, ((0, rows - a.shape[0]), (0, 0)))


def _gate_rows(a_log, dt_bias):
    put = lambda t: jnp.pad(t.reshape(1, HEADS), ((0, 0), (HEADS, LANES - 2 * HEADS)))
    return put(a_log), put(dt_bias)


def _mixer_fwd(x, p):
    qkv, z, sc_in, bd, h, q, k, v, bg = _in_proj(x, p["g1"], p["land_a"], p["cw"], p["al"], p["dt"])
    o, states = _delta_fwd(q, k, v, bg)
    x1, mix = _mix_out(o, z, sc_in, x, p["land_a"], p["gn"], p["scw"], p["gs"])
    return x1, dict(x=x, qkv=qkv, z=z, sc_in=sc_in, bd=bd, h=h, q=q, k=k, v=v, bg=bg, o=o, states=states, mix=mix)


def _ffn_fwd(x1, p, land_b):
    x2, a, b, h2 = _ffn(x1, p["g2"], land_b)
    return x2, dict(x1=x1, a=a, b=b, h2=h2)


def _ffn_back(dx2, dx2_bf16, s, p, land_b):
    dx1, dx1_bf16, da, db, act, dg2 = _ffn_bwd(dx2, s["x1"], s["a"], s["b"], p["g2"], land_b)
    parts = lax.empty((N_CHIPS, B_ROWS, D_MODEL), BF16)
    parts = _wgrad_share(act, dx2_bf16, parts, 2 * FF_SHARD, "wgrad_down")
    parts = _wgrad_share(da, s["h2"], parts, 0, "wgrad_gate")
    parts = _wgrad_share(db, s["h2"], parts, FF_SHARD, "wgrad_up")
    return dx1, dx1_bf16, parts, dg2[0]


def _mixer_bwd(dx1, dx1_bf16, s, p):
    do, dz, dgb, dcv, dgn, dgs, dscw = _mix_out_bwd(dx1, s["o"], s["z"], s["sc_in"], p["land_a"], p["gn"], p["scw"], p["gs"])
    dq, dk, dv, dbg = _delta_bwd(s["q"], s["k"], s["v"], s["bg"], s["states"], do)
    dc, dbd, dcw, dal, ddt = _dn_prep_bwd(dq, dk, dv, dbg, s["qkv"], p["cw"], s["bd"], p["al"], p["dt"])
    dx, dx_bf16, dps, dg1 = _in_proj_bwd(dc, dcv, dgb, s["sc_in"], dz, dbd, p["cw"], p["scw"], dx1, s["x"], p["g1"], p["land_a"])
    parts = lax.empty((N_CHIPS, A_ROWS, D_MODEL), BF16)
    parts = _wgrad_in_share(s["h"], dps, parts, "wgrad_in")
    parts = _wgrad_share(s["mix"], dx1_bf16, parts, A_OUT_AT, "wgrad_out")
    g = dict(g1=dg1[0], gn=dgn[0], gs=dgs[0], scw=dscw[:3], cw=dcw[:4], al=dal[0, HEADS:2 * HEADS], dt=ddt[0, HEADS:2 * HEADS])
    return dx, dx_bf16, parts, g


def _place():
    return lax.axis_index("x"), lax.axis_index("y"), lax.axis_index("c")


def _other_chips(x, y):
    return [(1 - x, y), (x, 1 - y), (1 - x, 1 - y)]


_HBM = pl.BlockSpec(memory_space=pltpu.HBM)


def _chip_exchange(arrs, name, gather):
    n = len(arrs)

    def body(*refs):
        ins, outs = refs[:n], refs[n:2 * n]
        send_sems, recv_sems, local_sems = refs[2 * n:]
        x, y, c = _place()
        me = 2 * x + y
        others = _other_chips(x, y)

        def remote(k, j, landing):
            px, py = others[j]
            src = ins[k] if gather else ins[k].at[2 * px + py]
            return pltpu.make_async_remote_copy(src_ref=src, dst_ref=outs[k].at[landing], send_sem=send_sems.at[k, j],
                                                recv_sem=recv_sems.at[k, j], device_id=(px, py, c), device_id_type=MESH)

        local = [pltpu.make_async_copy(ins[k] if gather else ins[k].at[me], outs[k].at[me], local_sems.at[k])
                 for k in range(n)]
        sends = [remote(k, j, me) for k in range(n) for j in range(3)]
        for cp in local + sends:
            cp.start()
        for k in range(n):
            for j, (px, py) in enumerate(others):
                remote(k, j, 2 * px + py).wait_recv()
        for cp in sends:
            cp.wait_send()
        for cp in local:
            cp.wait()

    shapes = [jax.ShapeDtypeStruct(((N_CHIPS,) + a.shape) if gather else a.shape, a.dtype) for a in arrs]
    return pl.pallas_call(
        body, name=name, in_specs=[_HBM] * n, out_specs=[_HBM] * n, out_shape=shapes,
        scratch_shapes=[pltpu.SemaphoreType.DMA((n, 3)), pltpu.SemaphoreType.DMA((n, 3)), pltpu.SemaphoreType.DMA((n,))],
    )(*arrs)


_SEM = pl.BlockSpec(memory_space=pltpu.SEMAPHORE)
_ANY = pl.BlockSpec(memory_space=pl.ANY)
_EFFECT = pltpu.SideEffectType.DATAFLOW_SIDE_EFFECTING


_FLIPS = [(a, b, cc) for a in (0, 1) for b in (0, 1) for cc in (0, 1)][1:]


def _split_copies(src_ref, land_ref, send_sems, recv_sems, gather, sending):
    x, y, c = _place()
    copies = []
    if gather:
        me = 2 * x + y
        for j, (px, py) in enumerate(_other_chips(x, y)):
            copies.append(pltpu.make_async_remote_copy(
                src_ref=src_ref, dst_ref=land_ref.at[me if sending else 2 * px + py],
                send_sem=send_sems.at[j], recv_sem=recv_sems.at[j], device_id=(px, py, c), device_id_type=MESH))
        return copies
    me = 4 * x + 2 * y + c
    for j, (a, b, cc) in enumerate(_FLIPS):
        px, py, pc = (1 - x) if a else x, (1 - y) if b else y, (1 - c) if cc else c
        copies.append(pltpu.make_async_remote_copy(
            src_ref=src_ref.at[2 * px + py], dst_ref=land_ref.at[me if sending else 4 * px + 2 * py + pc],
            send_sem=send_sems.at[j], recv_sem=recv_sems.at[j], device_id=(px, py, pc), device_id_type=MESH))
    return copies


def _own_slot(share):
    chip = 2 * lax.axis_index("x") + lax.axis_index("y")
    return lax.dynamic_update_slice(lax.empty((N_CHIPS,) + share.shape, share.dtype), share[None], (chip, 0, 0))


def _own_part(parts):
    chip = 2 * lax.axis_index("x") + lax.axis_index("y")
    own = lax.dynamic_index_in_dim(parts, chip, 0, keepdims=True)
    return lax.dynamic_update_slice(lax.empty((N_DEV,) + parts.shape[1:], parts.dtype), own,
                                    (2 * chip + lax.axis_index("c"), 0, 0))


def _exchange_start(src, land, after, name, gather):
    def body(src_ref, land_ref, after_ref, send_sems, recv_sems, src_thru, land_thru, token):
        for cp in _split_copies(src_ref, land_ref, send_sems, recv_sems, gather, sending=True):
            cp.start()
        token[...] = jnp.zeros_like(token)

    hbm = lambda t: pltpu.with_memory_space_constraint(t, pltpu.HBM)
    n_copies = N_CHIPS - 1 if gather else N_DEV - 1
    return pl.pallas_call(
        body, name=name,
        out_shape=(pltpu.SemaphoreType.DMA((n_copies,)), pltpu.SemaphoreType.DMA((n_copies,)), pltpu.HBM(src.shape, src.dtype),
                   pltpu.HBM(land.shape, land.dtype), jax.ShapeDtypeStruct((8, LANES), F32)),
        in_specs=(_HBM, _HBM, _ANY), out_specs=(_SEM, _SEM, _HBM, _HBM, pl.BlockSpec(memory_space=pltpu.VMEM)),
        input_output_aliases={0: 2, 1: 3},
        compiler_params=pltpu.CompilerParams(has_side_effects=_EFFECT),
    )(hbm(src), hbm(land), after)


def _exchange_wait(started, after, name, gather):
    send_sems, recv_sems, src_thru, land_thru, _ = started

    def body(src_ref, land_ref, send_sems, recv_sems, after_ref, src_dead, got_ref):
        for cp in _split_copies(src_ref, land_ref, send_sems, recv_sems, gather, sending=False):
            cp.wait_send()
            cp.wait_recv()

    return pl.pallas_call(
        body, name=name,
        out_shape=(pltpu.HBM(src_thru.shape, src_thru.dtype), pltpu.HBM(land_thru.shape, land_thru.dtype)),
        in_specs=(_HBM, _HBM, _SEM, _SEM, _ANY), out_specs=(_HBM, _HBM), input_output_aliases={0: 0, 1: 1},
        compiler_params=pltpu.CompilerParams(has_side_effects=_EFFECT),
    )(src_thru, land_thru, send_sems, recv_sems, after)[1]


def _all_reduce_small(v):
    rows = v.shape[0]
    flips = [(a, b, cc) for a in (0, 1) for b in (0, 1) for cc in (0, 1)][1:]

    def body(v_ref, out_ref, buf_ref, send_sems, recv_sems):
        x, y, c = _place()
        me = 4 * x + 2 * y + c
        peers = [((1 - x) if a else x, (1 - y) if b else y, (1 - c) if cc else c) for a, b, cc in flips]

        def copy(j, landing):
            return pltpu.make_async_remote_copy(src_ref=v_ref, dst_ref=buf_ref.at[landing], send_sem=send_sems.at[j],
                                                recv_sem=recv_sems.at[j], device_id=peers[j], device_id_type=MESH)

        sends = [copy(j, me) for j in range(N_DEV - 1)]
        for cp in sends:
            cp.start()
        buf_ref[me] = v_ref[...]
        for j, (px, py, pc) in enumerate(peers):
            copy(j, 4 * px + 2 * py + pc).wait_recv()
        for cp in sends:
            cp.wait_send()
        acc = buf_ref[0]
        for d in range(1, N_DEV):
            acc = acc + buf_ref[d]
        out_ref[...] = acc

    vmem = pl.BlockSpec(memory_space=pltpu.VMEM)
    return pl.pallas_call(
        body, name="all_reduce_small", in_specs=[vmem], out_specs=vmem,
        out_shape=jax.ShapeDtypeStruct(v.shape, F32),
        scratch_shapes=[pltpu.VMEM((N_DEV, rows, LANES), F32), pltpu.SemaphoreType.DMA((N_DEV - 1,)),
                        pltpu.SemaphoreType.DMA((N_DEV - 1,))],
    )(v)


def _row_block(*sizes):
    return next(t for t in (128, 64) if all(s % t == 0 for s in sizes))


def _adam_update(w, m, v, g):
    r1 = 1.0 / (1.0 - ADAM_B1 ** ADAM_STEP)
    r2 = 1.0 / (1.0 - ADAM_B2 ** ADAM_STEP)
    m_new = ADAM_B1 * m + (1.0 - ADAM_B1) * g
    v_new = ADAM_B2 * v + (1.0 - ADAM_B2) * (g * g)
    return -ADAM_LR * ((m_new * r1) / (jnp.sqrt(v_new * r2) + ADAM_EPS) + ADAM_WD * w), m_new, v_new


def _adamw_rows(w, m, v, got, first, name):
    n_layers, rows, cols = w.shape
    tr = _row_block(rows, first)

    def body(*refs):
        w_ref, m_ref, v_ref = refs[:3]
        g_out, d_out, m_out, v_out = refs[3 + n_layers:]
        for k in range(n_layers):
            @pl.when(pl.program_id(0) == k)
            def _(p_ref=refs[3 + k]):
                g = p_ref[0].astype(F32)
                for d in range(1, N_DEV):
                    g = g + p_ref[d].astype(F32)
                g = g[:, :cols]
                d_out[0], m_out[0], v_out[0] = _adam_update(w_ref[0], m_ref[0], v_ref[0], g)
                g_out[0] = g

    blk = pl.BlockSpec((1, tr, cols), lambda l, i: (l, i, 0))
    parts = [pl.BlockSpec((N_DEV, tr, got[0].shape[2]), lambda l, i, k=k: (0, jnp.where(l == k, first // tr + i, 0), 0))
             for k in range(n_layers)]
    return pl.pallas_call(
        body, name=name, grid=(n_layers, rows // tr),
        in_specs=[blk] * 3 + parts, out_specs=[blk] * 4,
        out_shape=[jax.ShapeDtypeStruct(w.shape, F32)] * 4,
        compiler_params=_params(("arbitrary", "arbitrary")),
    )(w, m, v, *got)


def _adamw(w, m, v, g_parts, name):
    rows, cols = w.shape
    tr = min(rows, 256)
    n = len(g_parts)

    def body(*refs):
        w_ref, m_ref, v_ref = refs[:3]
        g_refs = refs[3:3 + n]
        g_out, d_out, m_out, v_out = refs[3 + n:]
        g = g_refs[0][...]
        for r in g_refs[1:]:
            g = g + r[...]
        d_out[...], m_out[...], v_out[...] = _adam_update(w_ref[...], m_ref[...], v_ref[...], g)
        g_out[...] = g

    blk = pl.BlockSpec((tr, cols), lambda i: (i, 0))
    return pl.pallas_call(
        body, name=name, grid=(rows // tr,),
        in_specs=[blk] * (3 + n), out_specs=[blk] * 4,
        out_shape=[jax.ShapeDtypeStruct((rows, cols), F32)] * 4,
        compiler_params=_params(("parallel",)),
    )(w, m, v, *g_parts)


def _pack(parts, rows, fill=0.0):
    flat = jnp.concatenate([p.reshape(-1) for p in parts])
    return jnp.pad(flat, (0, rows * LANES - flat.shape[0]), constant_values=fill).reshape(rows, LANES)


def _unpack(packed, shapes):
    flat = packed.reshape(-1)
    out, at = [], 0
    for shp in shapes:
        size = 1
        for s in shp:
            size *= s
        out.append(flat[at:at + size].reshape(shp))
        at += size
    return out


def _packed_rows(shapes):
    total = 0
    for shp in shapes:
        size = 1
        for s in shp:
            size *= s
        total += size
    return -(-total // (8 * LANES)) * 8


def _cols_full(g, l):
    t = g[:, l]
    return jnp.moveaxis(t, 0, 1).reshape(t.shape[1], N_CHIPS * t.shape[2])


def _pad_cols(t):
    return jnp.pad(t, ((0, 0),) * (t.ndim - 1) + ((0, D_MODEL - t.shape[-1]),))


def kernel(x, norm1_g, w_in, dn_conv_w, dn_a_log, dn_dt_bias, dn_norm_g, sc_conv_w, sc_norm_g, w_out, norm2_g, ffn_w_gate, ffn_w_up, ffn_w_down, final_norm_g, loss_target, m_norm1_g, m_w_in, m_dn_conv_w, m_dn_a_log, m_dn_dt_bias, m_dn_norm_g, m_sc_conv_w, m_sc_norm_g, m_w_out, m_norm2_g, m_ffn_w_gate, m_ffn_w_up, m_ffn_w_down, m_final_norm_g, v_norm1_g, v_w_in, v_dn_conv_w, v_dn_a_log, v_dn_dt_bias, v_dn_norm_g, v_sc_conv_w, v_sc_norm_g, v_w_out, v_norm2_g, v_ffn_w_gate, v_ffn_w_up, v_ffn_w_down, v_final_norm_g):
    chip = 2 * lax.axis_index("x") + lax.axis_index("y")

    g_cw, g_scw = _chip_exchange([dn_conv_w, sc_conv_w], "gather_conv", gather=True)

    t_last = lambda t: jnp.swapaxes(t, -1, -2)
    gate_t, up_t = t_last(ffn_w_gate), t_last(ffn_w_up)
    zero_token = jnp.zeros((8, LANES), F32)

    def shares(l, tie):
        share_a = jnp.concatenate([_pad_cols(w_in[l] + tie), w_out[l]], axis=0).astype(BF16)
        share_b = jnp.concatenate([gate_t[l] + tie, up_t[l], ffn_w_down[l]], axis=0).astype(BF16)
        return share_a, _own_slot(share_a), share_b, _own_slot(share_b)

    def gather_start(l, packed, after):
        a = _exchange_start(packed[0], packed[1], after, "gather_a_start_%d" % l, gather=True)
        b = _exchange_start(packed[2], packed[3], a[4], "gather_b_start_%d" % l, gather=True)
        return a, b

    ga, gb = gather_start(0, shares(0, 0.0), g_cw)
    packed = [None] + [shares(l, gb[4][0, 0]) for l in range(1, DEPTH)]
    packed_all = sum(t[0, 0].astype(F32) for p in packed[1:] for t in (p[0], p[2]))
    land_a = _exchange_wait(ga, zero_token + packed_all, "gather_a_wait_0", gather=True)
    act = x[0]
    layers, saved_m, saved_f, lands_b = [], [], [], []
    for l in range(DEPTH):
        hold = 0.0
        if l + 1 < DEPTH:
            ga, gb_next = gather_start(l + 1, packed[l + 1], land_a)
            hold = gb_next[4][0:1, 0:1]
        al, dt = _gate_rows(dn_a_log[l], dn_dt_bias[l])
        layers.append(dict(
            g1=norm1_g[l][None] + hold, cw=_pad_rows(_cols_full(g_cw, l)), al=al, dt=dt,
            gn=dn_norm_g[l][None], scw=_pad_rows(_cols_full(g_scw, l)), gs=sc_norm_g[l][None],
            land_a=land_a, g2=norm2_g[l][None]))
        x1, s = _mixer_fwd(act, layers[l])
        saved_m.append(s)
        lands_b.append(_exchange_wait(gb, x1, "gather_b_wait_%d" % l, gather=True))
        act, s = _ffn_fwd(x1, layers[l], lands_b[l])
        saved_f.append(s)
        if l + 1 < DEPTH:
            land_a = _exchange_wait(ga, act, "gather_a_wait_%d" % (l + 1), gather=True)
            gb = gb_next

    dact, dact_bf16, loss_part, d_final = _loss_head(act, final_norm_g[None], loss_target[0])
    grads, reduce_a, reduce_b = [None] * DEPTH, [None] * DEPTH, [None] * DEPTH
    hold = 0.0
    for l in reversed(range(DEPTH)):
        p = layers[l]
        dx1, dx1_bf16, parts, dg2 = _ffn_back(dact, dact_bf16, saved_f[l], dict(p, g2=p["g2"] + hold), lands_b[l])
        reduce_b[l] = _exchange_start(parts, _own_part(parts), zero_token, "reduce_b_start_%d" % l, gather=False)
        dact, dact_bf16, parts, gm = _mixer_bwd(dx1, dx1_bf16, saved_m[l], dict(p, gn=p["gn"] + reduce_b[l][4][0:1, 0:1]))
        reduce_a[l] = _exchange_start(parts, _own_part(parts), zero_token, "reduce_a_start_%d" % l, gather=False)
        hold = reduce_a[l][4][0:1, 0:1]
        grads[l] = dict(gm, g2=dg2)
    loss = lax.psum(loss_part[0, 0], ("x", "y", "c"))
    stack = lambda key: jnp.stack([grads[l][key] for l in range(DEPTH)])

    got_b = [_exchange_wait(reduce_b[l], reduce_a[0][4], "reduce_b_wait_%d" % l, gather=False)
             for l in reversed(range(DEPTH))][::-1]
    big = dict(
        ffn_w_gate=[t_last(o) for o in _adamw_rows(gate_t, t_last(m_ffn_w_gate), t_last(v_ffn_w_gate), got_b, 0, "adamw_gate")],
        ffn_w_up=[t_last(o) for o in _adamw_rows(up_t, t_last(m_ffn_w_up), t_last(v_ffn_w_up), got_b, FF_SHARD, "adamw_up")],
        ffn_w_down=_adamw_rows(ffn_w_down, m_ffn_w_down, v_ffn_w_down, got_b, 2 * FF_SHARD, "adamw_down"))
    after_b = zero_token + sum(big[n][1][0, 0, 0] for n in ("ffn_w_gate", "ffn_w_up", "ffn_w_down"))
    got_a = [_exchange_wait(reduce_a[l], after_b, "reduce_a_wait_%d" % l, gather=False) for l in reversed(range(DEPTH))][::-1]
    big.update(
        w_in=_adamw_rows(w_in, m_w_in, v_w_in, got_a, 0, "adamw_w_in"),
        w_out=_adamw_rows(w_out, m_w_out, v_w_out, got_a, A_OUT_AT, "adamw_w_out"))

    full_shapes = [(DEPTH, D_MODEL), (DEPTH, D_MODEL), (DEPTH, HEAD_DIM), (DEPTH, SC_WIDTH), (DEPTH, HEADS),
                   (DEPTH, HEADS), (D_MODEL,), (DEPTH, 4, QKV), (DEPTH, 3, SC_WIDTH)]
    small_keys = ("g1", "g2", "gn", "gs", "al", "dt")
    packed = _pack([stack(k) for k in small_keys] + [d_final[0], stack("cw"), stack("scw")], _packed_rows(full_shapes))
    sg = _unpack(_all_reduce_small(packed), full_shapes)
    sg[7] = lax.dynamic_slice_in_dim(sg[7], chip * (QKV // N_CHIPS), QKV // N_CHIPS, axis=2)
    sg[8] = lax.dynamic_slice_in_dim(sg[8], chip * (SC_WIDTH // N_CHIPS), SC_WIDTH // N_CHIPS, axis=2)
    small_names = ("norm1_g", "norm2_g", "dn_norm_g", "sc_norm_g", "dn_a_log", "dn_dt_bias", "final_norm_g",
                   "dn_conv_w", "sc_conv_w")
    sw = (norm1_g, norm2_g, dn_norm_g, sc_norm_g, dn_a_log, dn_dt_bias, final_norm_g, dn_conv_w, sc_conv_w)
    sm = (m_norm1_g, m_norm2_g, m_dn_norm_g, m_sc_norm_g, m_dn_a_log, m_dn_dt_bias, m_final_norm_g, m_dn_conv_w, m_sc_conv_w)
    sv = (v_norm1_g, v_norm2_g, v_dn_norm_g, v_sc_norm_g, v_dn_a_log, v_dn_dt_bias, v_final_norm_g, v_dn_conv_w, v_sc_conv_w)
    shard_shapes = [t.shape for t in sw]
    rows = _packed_rows(shard_shapes)
    outs = _adamw(_pack(sw, rows), _pack(sm, rows), _pack(sv, rows, fill=1.0), [_pack(sg, rows)], "adamw_small")
    small = {name: [] for name in small_names}
    for o in outs:
        for name, t in zip(small_names, _unpack(o, shard_shapes)):
            small[name].append(t)

    order = ("norm1_g", "w_in", "dn_conv_w", "dn_a_log", "dn_dt_bias", "dn_norm_g", "sc_conv_w", "sc_norm_g", "w_out",
             "norm2_g", "ffn_w_gate", "ffn_w_up", "ffn_w_down", "final_norm_g")
    result = {**big, **small}
    return (loss, dact[None], *[result[n][0] for n in order], *[result[n][1] for n in order],
            *[result[n][2] for n in order], *[result[n][3] for n in order])
```

```python
import jax
import jax.numpy as jnp
from jax import lax
from jax.experimental import pallas as pl
from jax.experimental.pallas import tpu as pltpu

F32 = jnp.float32
BF16 = jnp.bfloat16
MESH = pl.DeviceIdType.MESH

D_MODEL = 1024
DEPTH = 4
HEADS = 4
HEAD_DIM = 128
DN_WIDTH = HEADS * HEAD_DIM
SC_WIDTH = 512
SC_GROUPS = 4
D_FF = 2816
CHUNK = 64
QKV = 3 * DN_WIDTH
W_IN_COLS = 4 * DN_WIDTH + 2 * HEADS + 3 * SC_WIDTH
WA_COLS = QKV + DN_WIDTH + 3 * SC_WIDTH
LANES = 128
EPS = 1e-6
Q_SCALE = HEAD_DIM ** -0.5
N_CHIPS = 4
N_DEV = 8
IN_SHARD = W_IN_COLS // N_CHIPS
OUT_SHARD = D_MODEL // N_CHIPS
FF_SHARD = D_FF // N_CHIPS
A_OUT_AT = D_MODEL
A_ROWS = D_MODEL + OUT_SHARD
B_ROWS = 3 * FF_SHARD

ADAM_LR = 0.001
ADAM_B1 = 0.9
ADAM_B2 = 0.999
ADAM_EPS = 1e-08
ADAM_WD = 0.01
ADAM_STEP = 10

VMEM_LIMIT = 56 * 1024 * 1024

NN = (((1,), (0,)), ((), ()))
NT = (((1,), (1,)), ((), ()))
TN = (((0,), (0,)), ((), ()))


def _mm(a, b, dims=NN):
    return lax.dot_general(a.astype(BF16), b.astype(BF16), dims, preferred_element_type=F32)


def _mm32(a, b, dims=NN):
    return lax.dot_general(a, b, dims, preferred_element_type=F32, precision=lax.Precision.HIGHEST)


def _params(sem, vmem=VMEM_LIMIT):
    return pltpu.CompilerParams(dimension_semantics=sem, vmem_limit_bytes=vmem)


def _sigmoid(x):
    return 0.5 * jnp.tanh(0.5 * x) + 0.5


def _softplus(x):
    return jnp.maximum(x, 0.0) + jnp.log1p(jnp.exp(-jnp.abs(x)))


def _row_acc(acc_ref, val):
    acc_ref[0:1, :] += jnp.sum(val, axis=0, keepdims=True)


def _rms_bwd(dh, xh, r, gain):
    dxh = dh * gain
    return r * (dxh - xh * jnp.mean(dxh * xh, axis=-1, keepdims=True))


def _before_halo(tb):
    return lambda i: (jnp.maximum(i * (tb // 8) - 1, 0), 0)


def _after_halo(tb, n_rows):
    last = n_rows // 8 - 1
    return lambda i: (jnp.minimum((i + 1) * (tb // 8), last), 0)


def _rows_from(xc, offset, tb):
    part = offset % 8
    if part:
        xc = pltpu.roll(xc, xc.shape[0] - part, 0)
    return xc[offset - part:offset - part + tb, :]


def _taps(xc, w, n_taps, tb, first):
    out = w[0:1, :] * _rows_from(xc, first, tb)
    for j in range(1, n_taps):
        out = out + w[j:j + 1, :] * _rows_from(xc, first + j, tb)
    return out


W_Z = QKV
W_BD = W_Z + DN_WIDTH
W_SC = W_BD + 2 * HEADS

def _w_in_cols(shards, lo, hi):
    pieces = []
    for s in range(N_CHIPS):
        a, b = max(lo, IN_SHARD * s), min(hi, IN_SHARD * (s + 1))
        if a < b:
            pieces.append(shards[s][:, a - IN_SHARD * s:b - IN_SHARD * s])
    return pieces[0] if len(pieces) == 1 else jnp.concatenate(pieces, axis=1)


def _in_proj(x, g1, land_a, cw, al_row, dt_row):
    T = x.shape[0]
    tb = 256

    def body(x_ref, g_ref, w_ref, cw_ref, al_ref, dt_ref,
             qkv_ref, z_ref, sc_ref, bd_ref, h_ref, q_ref, k_ref, v_ref, bg_ref, tail_ref):
        @pl.when(pl.program_id(0) == 0)
        def _():
            tail_ref[...] = jnp.zeros_like(tail_ref)

        xv = x_ref[...]
        h = (xv * lax.rsqrt(jnp.mean(xv * xv, axis=-1, keepdims=True) + EPS) * g_ref[...]).astype(BF16)
        shards = [jnp.dot(h, w_ref[s], preferred_element_type=F32) for s in range(N_CHIPS)]
        qkv = _w_in_cols(shards, 0, W_Z)
        bd = jnp.concatenate([_w_in_cols(shards, W_BD, W_SC), jnp.zeros((tb, LANES - 2 * HEADS), F32)], axis=1)
        qkv_ref[...] = qkv
        z_ref[...] = _w_in_cols(shards, W_Z, W_BD)
        bd_ref[...] = bd
        sc_ref[...] = _w_in_cols(shards, W_SC, W_IN_COLS)
        h_ref[...] = h
        halo = tail_ref[...]
        tail_ref[...] = qkv[tb - 8:, :]
        _, _, _, a = _dn_act(qkv, halo, cw_ref[...], tb)
        for hd in range(HEADS):
            sl = slice(HEAD_DIM * hd, HEAD_DIM * (hd + 1))
            qs = a[:, sl]
            q_ref[:, sl] = qs * (lax.rsqrt(jnp.sum(qs * qs, axis=-1, keepdims=True) + EPS) * Q_SCALE)
            ks = a[:, DN_WIDTH + HEAD_DIM * hd:DN_WIDTH + HEAD_DIM * (hd + 1)]
            k_ref[:, sl] = ks * lax.rsqrt(jnp.sum(ks * ks, axis=-1, keepdims=True) + EPS)
        v_ref[...] = a[:, 2 * DN_WIDTH:]
        gates = _gates(bd, al_ref[...], dt_ref[...])
        lane = lax.broadcasted_iota(jnp.int32, gates.shape, 1)
        bg_ref[...] = jnp.where(lane < HEADS, gates, _mm32(_chunk_cumsum_matrix(tb), gates))

    tok = lambda w: pl.BlockSpec((tb, w), lambda i: (i, 0))
    full = lambda t: pl.BlockSpec(t.shape, lambda i: (0, 0))
    return pl.pallas_call(
        body, name="in_proj", grid=(T // tb,),
        in_specs=[tok(D_MODEL), full(g1), _shard_rows(land_a, 0, D_MODEL), full(cw), full(al_row), full(dt_row)],
        out_specs=[tok(QKV), tok(DN_WIDTH), tok(3 * SC_WIDTH), tok(LANES), tok(D_MODEL),
                   tok(DN_WIDTH), tok(DN_WIDTH), tok(DN_WIDTH), tok(LANES)],
        out_shape=[jax.ShapeDtypeStruct((T, QKV), F32), jax.ShapeDtypeStruct((T, DN_WIDTH), F32),
                   jax.ShapeDtypeStruct((T, 3 * SC_WIDTH), F32), jax.ShapeDtypeStruct((T, LANES), F32),
                   jax.ShapeDtypeStruct((T, D_MODEL), BF16)]
        + [jax.ShapeDtypeStruct((T, DN_WIDTH), F32)] * 3 + [jax.ShapeDtypeStruct((T, LANES), F32)],
        scratch_shapes=[pltpu.VMEM((8, QKV), F32)],
        compiler_params=_params(("arbitrary",)),
    )(x, g1, land_a, cw, al_row, dt_row)


def _dn_act(pre, halo, cw, tb):
    xc = jnp.concatenate([halo, pre], axis=0)
    c = _taps(xc, cw, 4, tb, 5)
    sg = _sigmoid(c)
    return xc, c, sg, c * sg


def _gates(bd, al_row, dt_row):
    lane = lax.broadcasted_iota(jnp.int32, bd.shape, 1)
    beta = _sigmoid(bd)
    g = -jnp.exp(al_row) * _softplus(bd + dt_row)
    return jnp.where(lane < HEADS, beta, jnp.where(lane < 2 * HEADS, g, 0.0))


def _chunk_masks():
    row = lax.broadcasted_iota(jnp.int32, (CHUNK, CHUNK), 0)
    col = lax.broadcasted_iota(jnp.int32, (CHUNK, CHUNK), 1)
    return row >= col, row > col


def _chunk_cumsum_matrix(n):
    row = lax.broadcasted_iota(jnp.int32, (n, n), 0)
    col = lax.broadcasted_iota(jnp.int32, (n, n), 1)
    return jnp.logical_and(row >= col, row // CHUNK == col // CHUNK).astype(F32)


def _chunk_units(q_ref, k_ref, v_ref, bg_ref, rows):
    bgc = bg_ref[rows, :]
    bg_t = bgc.T
    qv, kv, vv = q_ref[rows, :], k_ref[rows, :], v_ref[rows, :]
    units = []
    for h in range(HEADS):
        sl = slice(HEAD_DIM * h, HEAD_DIM * (h + 1))
        units.append((qv[:, sl], kv[:, sl], vv[:, sl], bgc[:, h:h + 1], bgc[:, HEADS + h:HEADS + h + 1],
                      bg_t[HEADS + h:HEADS + h + 1, :]))
    return units


def _units_local(units, masks):
    causal, strict = masks
    pre = []
    for q, k, v, beta, gc, gr in units:
        kb = k * beta
        eg = jnp.exp(gc)
        g_last = gc[CHUNK - 1:CHUNK, :]
        ek = jnp.exp(g_last - gc)
        pre.append(dict(q=q, k=k, v=v, beta=beta, decay=jnp.exp(jnp.where(causal, gc - gr, -1e30)), kb=kb, vb=v * beta,
                        eg=eg, kbg=kb * eg, ek=ek, gl=jnp.exp(g_last), q_dec=q * eg, k_dec=k * ek))
    both = [_mm(jnp.concatenate([p["kb"], p["q"]], axis=0), p["k"], NT) for p in pre]
    for p, b in zip(pre, both):
        p["low"] = jnp.where(strict, b[:CHUNK] * p["decay"], 0.0)
        p["qk"] = jnp.where(causal, b[CHUNK:] * p["decay"], 0.0)
    xs = [-p["low"] for p in pre]
    pw = [_mm(p["low"], p["low"]) for p in pre]
    for _ in range(4):
        both = [_mm(jnp.concatenate([pp, x], axis=0), pp) for pp, x in zip(pw, xs)]
        xs = [x + pp + b[CHUNK:] for x, pp, b in zip(xs, pw, both)]
        pw = [b[:CHUNK] for b in both]
    last = [_mm(x, pp) for x, pp in zip(xs, pw)]
    xs = [x + pp + b for x, pp, b in zip(xs, pw, last)]
    uw = [_mm(x, jnp.concatenate([p["vb"], p["kbg"]], axis=1)) for x, p in zip(xs, pre)]
    for p, x, b in zip(pre, xs, uw):
        p["xm"] = x
        p["u"] = p["vb"] + b[:, :HEAD_DIM]
        p["w"] = p["kbg"] + b[:, HEAD_DIM:]
    return pre


def _delta_fwd(q, k, v, bg):
    T = q.shape[0]
    tb = 512
    n_chunk = tb // CHUNK

    def body(q_ref, k_ref, v_ref, bg_ref, o_ref, st_ref, s_ref):
        @pl.when(pl.program_id(0) == 0)
        def _():
            s_ref[...] = jnp.zeros_like(s_ref)

        masks = _chunk_masks()

        def pair(pi, carry):
            rows = [pl.ds(pl.multiple_of((2 * pi + j) * CHUNK, CHUNK), CHUNK) for j in range(2)]
            loc = _units_local(_chunk_units(q_ref, k_ref, v_ref, bg_ref, rows[0])
                               + _chunk_units(q_ref, k_ref, v_ref, bg_ref, rows[1]), masks)
            states = [s_ref[h] for h in range(HEADS)]
            for j in range(2):
                lj = loc[HEADS * j:HEADS * (j + 1)]
                ws = [_mm(jnp.concatenate([p["w"], p["q_dec"]], axis=0), s) for p, s in zip(lj, states)]
                v_new = [p["u"] - b[:CHUNK] for p, b in zip(lj, ws)]
                intra = [_mm(p["qk"], vn) for p, vn in zip(lj, v_new)]
                upd = [_mm(p["k_dec"], vn, TN) for p, vn in zip(lj, v_new)]
                o_ref[rows[j], :] = jnp.concatenate([b[CHUNK:] + a for b, a in zip(ws, intra)], axis=1)
                for h in range(HEADS):
                    st_ref[2 * pi + j, h] = states[h]
                states = [p["gl"] * s + d for p, s, d in zip(lj, states, upd)]
            for h in range(HEADS):
                s_ref[h] = states[h]
            return carry

        lax.fori_loop(0, n_chunk // 2, pair, 0)

    tok = lambda w: pl.BlockSpec((tb, w), lambda i: (i, 0))
    return pl.pallas_call(
        body, name="delta_fwd", grid=(T // tb,),
        in_specs=[tok(DN_WIDTH), tok(DN_WIDTH), tok(DN_WIDTH), tok(LANES)],
        out_specs=[tok(DN_WIDTH), pl.BlockSpec((n_chunk, HEADS, HEAD_DIM, HEAD_DIM), lambda i: (i, 0, 0, 0))],
        out_shape=[jax.ShapeDtypeStruct((T, DN_WIDTH), F32),
                   jax.ShapeDtypeStruct((T // CHUNK, HEADS, HEAD_DIM, HEAD_DIM), F32)],
        scratch_shapes=[pltpu.VMEM((HEADS, HEAD_DIM, HEAD_DIM), F32)],
        compiler_params=_params(("arbitrary",)),
    )(q, k, v, bg)


def _dn_out(o, z, gn):
    outs, ohs, rs = [], [], []
    for hh in range(HEADS):
        oh = o[:, HEAD_DIM * hh:HEAD_DIM * (hh + 1)]
        r = lax.rsqrt(jnp.mean(oh * oh, axis=-1, keepdims=True) + EPS)
        ohs.append(oh * r)
        rs.append(r)
    sz = _sigmoid(z)
    oh = jnp.concatenate(ohs, axis=1)
    gn4 = jnp.concatenate([gn] * HEADS, axis=1)
    return oh * gn4 * (z * sz), oh, rs, sz, gn4


def _sc_fwd(sc_in, halo, cw, tb):
    xc = jnp.concatenate([halo, sc_in], axis=0)
    u = xc[:, SC_WIDTH:2 * SC_WIDTH] * xc[:, 2 * SC_WIDTH:]
    cv = _taps(u, cw, 3, tb, 6)
    gate_b = sc_in[:, :SC_WIDTH]
    y = gate_b * cv
    gw = SC_WIDTH // SC_GROUPS
    yhs, rs = [], []
    for gi in range(SC_GROUPS):
        yg = y[:, gw * gi:gw * (gi + 1)]
        r = lax.rsqrt(jnp.mean(yg * yg, axis=-1, keepdims=True) + EPS)
        yhs.append(yg * r)
        rs.append(r)
    return u, cv, gate_b, jnp.concatenate(yhs, axis=1), rs


def _shard_rows(land, first, rows):
    assert first % rows == 0 and land.shape[0] == N_CHIPS
    return pl.BlockSpec((N_CHIPS, rows, land.shape[2]), lambda i: (0, first // rows, 0))


def _whole(w_ref):
    n, rows, cols = w_ref.shape
    return w_ref[...].reshape(n * rows, cols)


def _mix_out(o, z, sc_in, x, land_a, gn, scw, gs):
    T = x.shape[0]
    tb = 256

    def body(o_ref, z_ref, sc_ref, halo_ref, x_ref, w_ref, gn_ref, scw_ref, gs_ref, x1_ref, mix_ref):
        o_n = _dn_out(o_ref[...], z_ref[...], gn_ref[...])[0]
        halo = jnp.where(pl.program_id(0) > 0, halo_ref[...], 0.0)
        yh = _sc_fwd(sc_ref[...], halo, scw_ref[...], tb)[3]
        mix = jnp.concatenate([o_n, yh * gs_ref[...]], axis=1).astype(BF16)
        x1_ref[...] = x_ref[...] + jnp.dot(mix, _whole(w_ref), preferred_element_type=F32)
        mix_ref[...] = mix

    tok = lambda w: pl.BlockSpec((tb, w), lambda i: (i, 0))
    full = lambda a: pl.BlockSpec(a.shape, lambda i: (0, 0))
    return pl.pallas_call(
        body, name="mix_out", grid=(T // tb,),
        in_specs=[tok(DN_WIDTH), tok(DN_WIDTH), tok(3 * SC_WIDTH), pl.BlockSpec((8, 3 * SC_WIDTH), _before_halo(tb)),
                  tok(D_MODEL), _shard_rows(land_a, A_OUT_AT, OUT_SHARD), full(gn), full(scw), full(gs)],
        out_specs=[tok(D_MODEL), tok(D_MODEL)],
        out_shape=[jax.ShapeDtypeStruct((T, D_MODEL), F32), jax.ShapeDtypeStruct((T, D_MODEL), BF16)],
        compiler_params=_params(("parallel",)),
    )(o, z, sc_in, sc_in, x, land_a, gn, scw, gs)


def _ffn(x1, g2, land_b):
    T = x1.shape[0]
    tb = 256

    def body(x_ref, g_ref, wgt_ref, wut_ref, wd_ref, x2_ref, a_ref, b_ref, h_ref):
        xv = x_ref[...]
        r = lax.rsqrt(jnp.mean(xv * xv, axis=-1, keepdims=True) + EPS)
        h = (xv * r * g_ref[...]).astype(BF16)
        a = lax.dot_general(h, _whole(wgt_ref), NT, preferred_element_type=F32)
        b = lax.dot_general(h, _whole(wut_ref), NT, preferred_element_type=F32)
        act = (a * _sigmoid(a) * b).astype(BF16)
        x2_ref[...] = xv + jnp.dot(act, _whole(wd_ref), preferred_element_type=F32)
        a_ref[...] = a.astype(BF16)
        b_ref[...] = b.astype(BF16)
        h_ref[...] = h

    tok = lambda w: pl.BlockSpec((tb, w), lambda i: (i, 0))
    return pl.pallas_call(
        body, name="ffn", grid=(T // tb,),
        in_specs=[tok(D_MODEL), pl.BlockSpec(g2.shape, lambda i: (0, 0)), _shard_rows(land_b, 0, FF_SHARD),
                  _shard_rows(land_b, FF_SHARD, FF_SHARD), _shard_rows(land_b, 2 * FF_SHARD, FF_SHARD)],
        out_specs=[tok(D_MODEL), tok(D_FF), tok(D_FF), tok(D_MODEL)],
        out_shape=[jax.ShapeDtypeStruct((T, D_MODEL), F32), jax.ShapeDtypeStruct((T, D_FF), BF16),
                   jax.ShapeDtypeStruct((T, D_FF), BF16), jax.ShapeDtypeStruct((T, D_MODEL), BF16)],
        compiler_params=_params(("parallel",)),
    )(x1, g2, land_b, land_b, land_b)


def _loss_head(x, gf, target):
    T = x.shape[0]
    tb = 512

    def body(x_ref, g_ref, t_ref, dx_ref, dxb_ref, loss_ref, dg_ref):
        @pl.when(pl.program_id(0) == 0)
        def _():
            loss_ref[...] = jnp.zeros_like(loss_ref)
            dg_ref[...] = jnp.zeros_like(dg_ref)

        xv = x_ref[...]
        r = lax.rsqrt(jnp.mean(xv * xv, axis=-1, keepdims=True) + EPS)
        xh = xv * r
        err = xh * g_ref[...] - t_ref[...]
        per_tok = jnp.mean(err * err, axis=-1, keepdims=True)
        loss_ref[...] += 0.5 * jnp.sum(per_tok, axis=0, keepdims=True)
        dy = err * (1.0 / D_MODEL)
        _row_acc(dg_ref, dy * xh)
        dx = _rms_bwd(dy, xh, r, g_ref[...])
        dx_ref[...] = dx
        dxb_ref[...] = dx.astype(BF16)

    tok = pl.BlockSpec((tb, D_MODEL), lambda i: (i, 0))
    return pl.pallas_call(
        body, name="loss_head", grid=(T // tb,),
        in_specs=[tok, pl.BlockSpec(gf.shape, lambda i: (0, 0)), tok],
        out_specs=[tok, tok, pl.BlockSpec((8, LANES), lambda i: (0, 0)), pl.BlockSpec((8, D_MODEL), lambda i: (0, 0))],
        out_shape=[jax.ShapeDtypeStruct((T, D_MODEL), F32), jax.ShapeDtypeStruct((T, D_MODEL), BF16),
                   jax.ShapeDtypeStruct((8, LANES), F32), jax.ShapeDtypeStruct((8, D_MODEL), F32)],
        compiler_params=_params(("arbitrary",)),
    )(x, gf, target)


def _ffn_bwd(dx2, x1, a, b, g2, land_b):
    T = x1.shape[0]
    tb = 256

    def body(dx2_ref, x_ref, a_ref, b_ref, g_ref, wgt_ref, wut_ref, wd_ref,
             dx1_ref, dx1b_ref, da_ref, db_ref, act_ref, dg_ref):
        @pl.when(pl.program_id(0) == 0)
        def _():
            dg_ref[...] = jnp.zeros_like(dg_ref)

        dx2v = dx2_ref[...]
        av = a_ref[...].astype(F32)
        bv = b_ref[...].astype(F32)
        dact = _mm(dx2v, _whole(wd_ref), NT)
        sa = _sigmoid(av)
        silu = av * sa
        da = (dact * bv * (sa * (1.0 + av * (1.0 - sa)))).astype(BF16)
        db = (dact * silu).astype(BF16)
        dh = _mm(da, _whole(wgt_ref)) + _mm(db, _whole(wut_ref))
        xv = x_ref[...]
        r = lax.rsqrt(jnp.mean(xv * xv, axis=-1, keepdims=True) + EPS)
        xh = xv * r
        _row_acc(dg_ref, dh * xh)
        dx1 = dx2v + _rms_bwd(dh, xh, r, g_ref[...])
        dx1_ref[...] = dx1
        dx1b_ref[...] = dx1.astype(BF16)
        da_ref[...] = da
        db_ref[...] = db
        act_ref[...] = (silu * bv).astype(BF16)

    tok = lambda w: pl.BlockSpec((tb, w), lambda i: (i, 0))
    return pl.pallas_call(
        body, name="ffn_bwd", grid=(T // tb,),
        in_specs=[tok(D_MODEL), tok(D_MODEL), tok(D_FF), tok(D_FF), pl.BlockSpec(g2.shape, lambda i: (0, 0)),
                  _shard_rows(land_b, 0, FF_SHARD), _shard_rows(land_b, FF_SHARD, FF_SHARD),
                  _shard_rows(land_b, 2 * FF_SHARD, FF_SHARD)],
        out_specs=[tok(D_MODEL), tok(D_MODEL), tok(D_FF), tok(D_FF), tok(D_FF), pl.BlockSpec((8, D_MODEL), lambda i: (0, 0))],
        out_shape=[jax.ShapeDtypeStruct((T, D_MODEL), F32), jax.ShapeDtypeStruct((T, D_MODEL), BF16)]
        + [jax.ShapeDtypeStruct((T, D_FF), BF16)] * 3 + [jax.ShapeDtypeStruct((8, D_MODEL), F32)],
        compiler_params=_params(("arbitrary",)),
    )(dx2, x1, a, b, g2, land_b, land_b, land_b)


def _wgrad_share(a, b, parts, first, name):
    T = b.shape[0]
    rows = a.shape[1] // N_CHIPS
    assert first % rows == 0 and b.shape[1] == parts.shape[2]
    bk = min(T, 1024)
    n_k = T // bk
    group = 2
    assert (group * rows) % LANES == 0

    def body(a_ref, b_ref, parts_ref, o_ref, acc_ref):
        kk = pl.program_id(1)

        @pl.when(kk == 0)
        def _():
            acc_ref[...] = jnp.zeros_like(acc_ref)

        acc_ref[...] += lax.dot_general(a_ref[...], b_ref[...], TN, preferred_element_type=F32)

        @pl.when(kk == n_k - 1)
        def _():
            for s in range(group):
                o_ref[s] = acc_ref[rows * s:rows * (s + 1), :].astype(BF16)

    return pl.pallas_call(
        body, name=name, grid=(N_CHIPS // group, n_k),
        in_specs=[pl.BlockSpec((bk, group * rows), lambda i, kk: (kk, i)),
                  pl.BlockSpec((bk, b.shape[1]), lambda i, kk: (kk, 0)), _ANY],
        out_specs=pl.BlockSpec((group, rows, b.shape[1]), lambda i, kk: (i, first // rows, 0)),
        out_shape=jax.ShapeDtypeStruct(parts.shape, BF16),
        scratch_shapes=[pltpu.VMEM((group * rows, b.shape[1]), F32)],
        input_output_aliases={2: 0},
        compiler_params=_params(("parallel", "arbitrary")),
    )(a, b, parts)


def _mix_out_bwd(dx1, o, z, sc_in, land_a, gn, scw, gs):
    T = dx1.shape[0]
    tb = 256

    def body(dx_ref, o_ref, z_ref, sc_ref, halo_ref, w_ref, gn_ref, scw_ref, gs_ref,
             do_ref, dz_ref, dgb_ref, dcv_ref, dgn_ref, dgs_ref, dscw_ref):
        @pl.when(pl.program_id(0) == 0)
        def _():
            dgn_ref[...] = jnp.zeros_like(dgn_ref)
            dgs_ref[...] = jnp.zeros_like(dgs_ref)
            dscw_ref[...] = jnp.zeros_like(dscw_ref)

        dmix = _mm(dx_ref[...], _whole(w_ref), NT)
        don = dmix[:, :DN_WIDTH]
        dosc = dmix[:, DN_WIDTH:]
        zv = z_ref[...]
        _, oh, rs, sz, gn4 = _dn_out(o_ref[...], zv, gn_ref[...])
        silu_z = zv * sz
        dgn_full = don * oh * silu_z
        dgn_ref[0:1, :] += jnp.sum(sum(dgn_full[:, HEAD_DIM * hh:HEAD_DIM * (hh + 1)] for hh in range(HEADS)),
                                   axis=0, keepdims=True)
        dz_ref[...] = (don * oh * gn4 * (sz * (1.0 + zv * (1.0 - sz)))).astype(BF16)
        t = don * gn4 * silu_z
        for hh in range(HEADS):
            sl = slice(HEAD_DIM * hh, HEAD_DIM * (hh + 1))
            th, ohh = t[:, sl], oh[:, sl]
            do_ref[:, sl] = rs[hh] * (th - ohh * jnp.mean(th * ohh, axis=-1, keepdims=True))
        halo = jnp.where(pl.program_id(0) > 0, halo_ref[...], 0.0)
        u, cv, gate_b, yh, rys = _sc_fwd(sc_ref[...], halo, scw_ref[...], tb)
        _row_acc(dgs_ref, dosc * yh)
        ty = dosc * gs_ref[...]
        gw = SC_WIDTH // SC_GROUPS
        dys = []
        for gi in range(SC_GROUPS):
            sl = slice(gw * gi, gw * (gi + 1))
            tg, yg = ty[:, sl], yh[:, sl]
            dys.append(rys[gi] * (tg - yg * jnp.mean(tg * yg, axis=-1, keepdims=True)))
        dy = jnp.concatenate(dys, axis=1)
        dgb_ref[...] = dy * cv
        dcv = dy * gate_b
        dcv_ref[...] = dcv
        for j in range(3):
            dscw_ref[j:j + 1, :] += jnp.sum(dcv * _rows_from(u, 6 + j, tb), axis=0, keepdims=True)

    tok = lambda w: pl.BlockSpec((tb, w), lambda i: (i, 0))
    full = lambda t: pl.BlockSpec(t.shape, lambda i: (0, 0))
    acc = lambda w: pl.BlockSpec((8, w), lambda i: (0, 0))
    return pl.pallas_call(
        body, name="mix_out_bwd", grid=(T // tb,),
        in_specs=[tok(D_MODEL), tok(DN_WIDTH), tok(DN_WIDTH), tok(3 * SC_WIDTH),
                  pl.BlockSpec((8, 3 * SC_WIDTH), _before_halo(tb)), _shard_rows(land_a, A_OUT_AT, OUT_SHARD),
                  full(gn), full(scw), full(gs)],
        out_specs=[tok(DN_WIDTH), tok(DN_WIDTH), tok(SC_WIDTH), tok(SC_WIDTH), acc(HEAD_DIM), acc(SC_WIDTH), acc(SC_WIDTH)],
        out_shape=[jax.ShapeDtypeStruct((T, DN_WIDTH), F32), jax.ShapeDtypeStruct((T, DN_WIDTH), BF16),
                   jax.ShapeDtypeStruct((T, SC_WIDTH), F32), jax.ShapeDtypeStruct((T, SC_WIDTH), F32),
                   jax.ShapeDtypeStruct((8, HEAD_DIM), F32), jax.ShapeDtypeStruct((8, SC_WIDTH), F32),
                   jax.ShapeDtypeStruct((8, SC_WIDTH), F32)],
        compiler_params=_params(("arbitrary",)),
    )(dx1, o, z, sc_in, sc_in, land_a, gn, scw, gs)


def _delta_bwd(q, k, v, bg, states, do):
    T = q.shape[0]
    tb = 512
    n_chunk = tb // CHUNK
    nb = T // tb

    def body(q_ref, k_ref, v_ref, bg_ref, st_ref, do_ref, dq_ref, dk_ref, dv_ref, dbg_ref, ds_ref):
        @pl.when(pl.program_id(0) == 0)
        def _():
            ds_ref[...] = jnp.zeros_like(ds_ref)

        masks = _chunk_masks()
        causal, strict = masks
        lane = lax.broadcasted_iota(jnp.int32, (CHUNK, LANES), 1)
        last_row = lax.broadcasted_iota(jnp.int32, (CHUNK, 1), 0) == CHUNK - 1
        cat = jnp.concatenate
        heads = range(HEADS)

        def open_chunk(ci, loc):
            rows = pl.ds(pl.multiple_of(ci * CHUNK, CHUNK), CHUNK)
            dov = do_ref[rows, :]
            return dict(rows=rows, loc=loc, do=[dov[:, HEAD_DIM * h:HEAD_DIM * (h + 1)] for h in heads],
                        state=[st_ref[ci, h] for h in heads])

        def a_free(c):
            loc, do, state = c["loc"], c["do"], c["state"]
            w_s = [_mm(p["w"], s) for p, s in zip(loc, state)]
            c["dq_dec"] = [_mm(d, s, NT) for d, s in zip(do, state)]
            c["qk_do"] = [_mm(p["qk"], d, TN) for p, d in zip(loc, do)]
            c["qd_do"] = [_mm(p["q_dec"], d, TN) for p, d in zip(loc, do)]
            c["v_new"] = [p["u"] - t for p, t in zip(loc, w_s)]
            c["dqk"] = [jnp.where(causal, _mm(d, vn, NT), 0.0) for d, vn in zip(do, c["v_new"])]

        def a_state(c, ds_next):
            c["ds_next"] = ds_next
            kd_ds = [_mm(p["k_dec"], d) for p, d in zip(c["loc"], ds_next)]
            c["dk_dec"] = [_mm(vn, d, NT) for vn, d in zip(c["v_new"], ds_next)]
            c["dv_new"] = [a + b for a, b in zip(c["qk_do"], kd_ds)]

        def b_state(c):
            loc = c["loc"]
            w_dv = [_mm(p["w"], dvn, TN) for p, dvn in zip(loc, c["dv_new"])]
            c["dw"] = [-_mm(dvn, s, NT) for dvn, s in zip(c["dv_new"], c["state"])]
            return [loc[h]["gl"] * c["ds_next"][h] + c["qd_do"][h] - w_dv[h] for h in heads]

        def c_solve(c):
            loc, dv_new, dw = c["loc"], c["dv_new"], c["dw"]
            c["dtm"] = [_mm(cat([dvn, d], axis=1), cat([p["vb"], p["kbg"]], axis=1), NT) for dvn, d, p in zip(dv_new, dw, loc)]
            x_t = [_mm(p["xm"], cat([dvn, d], axis=1), TN) for p, dvn, d in zip(loc, dv_new, dw)]
            c["dvb"] = [dvn + t[:, :HEAD_DIM] for dvn, t in zip(dv_new, x_t)]
            c["dkbg"] = [d + t[:, HEAD_DIM:] for d, t in zip(dw, x_t)]

        def d_solve(c):
            c["y"] = [t + _mm(p["xm"], t, TN) for p, t in zip(c["loc"], c["dtm"])]

        def e_solve(c):
            c["dlow"] = [jnp.where(strict, -(t + _mm(t, p["xm"], NT)), 0.0) for p, t in zip(c["loc"], c["y"])]

        def f_close(c):
            loc, rows = c["loc"], c["rows"]
            dmm = [d * p["decay"] for d, p in zip(c["dlow"], loc)]
            dnn = [d * p["decay"] for d, p in zip(c["dqk"], loc)]
            by_k = [_mm(cat([a, b], axis=0), p["k"]) for a, b, p in zip(dmm, dnn, loc)]
            dk_mm = [_mm(cat([a, b], axis=0), cat([p["kb"], p["q"]], axis=0), TN) for a, b, p in zip(dmm, dnn, loc)]
            dq_out, dk_out, dv_out = [], [], []
            dbeta_all = jnp.zeros((CHUNK, LANES), F32)
            dgc_all = jnp.zeros((CHUNK, LANES), F32)
            for h in heads:
                p = loc[h]
                dkb = by_k[h][:CHUNK] + c["dkbg"][h] * p["eg"]
                dq_out.append(by_k[h][CHUNK:] + c["dq_dec"][h] * p["eg"])
                dk_out.append(dk_mm[h] + c["dk_dec"][h] * p["ek"] + dkb * p["beta"])
                dv_out.append(c["dvb"][h] * p["beta"])
                dbeta = jnp.sum(dkb * p["k"] + c["dvb"][h] * p["v"], axis=1, keepdims=True)
                e = c["dlow"][h] * p["low"] + c["dqk"][h] * p["qk"]
                kd = jnp.sum(c["dk_dec"][h] * p["k_dec"], axis=1, keepdims=True)
                dgc = (jnp.sum(e, axis=1, keepdims=True) - jnp.sum(e.T, axis=1, keepdims=True)
                       + jnp.sum(c["dq_dec"][h] * p["q_dec"], axis=1, keepdims=True) - kd
                       + jnp.sum(c["dkbg"][h] * p["kbg"], axis=1, keepdims=True))
                dgl = jnp.sum(jnp.sum(c["ds_next"][h] * c["state"][h], axis=1, keepdims=True), axis=0, keepdims=True)
                d_last = jnp.sum(kd, axis=0, keepdims=True) + dgl * p["gl"]
                dgc = dgc + jnp.where(last_row, d_last, 0.0)
                dbeta_all = jnp.where(lane == h, dbeta, dbeta_all)
                dgc_all = jnp.where(lane == h + HEADS, dgc, dgc_all)
            dq_ref[rows, :] = cat(dq_out, axis=1)
            dk_ref[rows, :] = cat(dk_out, axis=1)
            dv_ref[rows, :] = cat(dv_out, axis=1)
            dbg_ref[rows, :] = dbeta_all + dgc_all

        def pair(pj, carry):
            hi = n_chunk - 1 - 2 * pj
            lo = hi - 1
            rows = [pl.ds(pl.multiple_of(ci * CHUNK, CHUNK), CHUNK) for ci in (hi, lo)]
            loc = _units_local(_chunk_units(q_ref, k_ref, v_ref, bg_ref, rows[0])
                               + _chunk_units(q_ref, k_ref, v_ref, bg_ref, rows[1]), masks)
            c_hi, c_lo = open_chunk(hi, loc[:HEADS]), open_chunk(lo, loc[HEADS:])
            a_free(c_hi)
            a_free(c_lo)
            a_state(c_hi, [ds_ref[h] for h in heads])
            ds_mid = b_state(c_hi)
            a_state(c_lo, ds_mid)
            c_solve(c_hi)
            ds_out = b_state(c_lo)
            for h in heads:
                ds_ref[h] = ds_out[h]
            d_solve(c_hi)
            c_solve(c_lo)
            e_solve(c_hi)
            d_solve(c_lo)
            f_close(c_hi)
            e_solve(c_lo)
            f_close(c_lo)
            return carry

        lax.fori_loop(0, n_chunk // 2, pair, 0)

    tok = lambda w: pl.BlockSpec((tb, w), lambda i: (nb - 1 - i, 0))
    return pl.pallas_call(
        body, name="delta_bwd", grid=(nb,),
        in_specs=[tok(DN_WIDTH), tok(DN_WIDTH), tok(DN_WIDTH), tok(LANES),
                  pl.BlockSpec((n_chunk, HEADS, HEAD_DIM, HEAD_DIM), lambda i: (nb - 1 - i, 0, 0, 0)), tok(DN_WIDTH)],
        out_specs=[tok(DN_WIDTH), tok(DN_WIDTH), tok(DN_WIDTH), tok(LANES)],
        out_shape=[jax.ShapeDtypeStruct((T, DN_WIDTH), F32)] * 3 + [jax.ShapeDtypeStruct((T, LANES), F32)],
        scratch_shapes=[pltpu.VMEM((HEADS, HEAD_DIM, HEAD_DIM), F32)],
        compiler_params=_params(("arbitrary",)),
    )(q, k, v, bg, states, do)


def _dn_prep_back(dq, dk, dv, dbg, pre, halo, cw, bd, al_row, dt_row, tb):
    xc, c, sg, a = _dn_act(pre, halo, cw, tb)
    dsilu = sg * (1.0 + c * (1.0 - sg))
    pieces = [None] * (2 * HEADS)
    for hd in range(HEADS):
        sl = slice(HEAD_DIM * hd, HEAD_DIM * (hd + 1))
        for which, (base, grad, scale) in enumerate(((0, dq, Q_SCALE), (DN_WIDTH, dk, 1.0))):
            sa = slice(base + HEAD_DIM * hd, base + HEAD_DIM * (hd + 1))
            raw = a[:, sa]
            r = lax.rsqrt(jnp.sum(raw * raw, axis=-1, keepdims=True) + EPS)
            nrm = raw * r
            gn_ = grad[:, sl] * scale
            pieces[which * HEADS + hd] = r * (gn_ - nrm * jnp.sum(gn_ * nrm, axis=-1, keepdims=True)) * dsilu[:, sa]
    dc = jnp.concatenate(pieces + [dv * dsilu[:, 2 * DN_WIDTH:]], axis=1)
    dcw_rows = [jnp.sum(dc * _rows_from(xc, 5 + j, tb), axis=0, keepdims=True) for j in range(4)]
    lane = lax.broadcasted_iota(jnp.int32, bd.shape, 1)
    is_b = lane < HEADS
    is_g = jnp.logical_and(lane >= HEADS, lane < 2 * HEADS)
    dbgv = jnp.where(is_b, dbg, _mm32(_chunk_cumsum_matrix(tb), dbg, TN))
    beta = _sigmoid(bd)
    neg_a = -jnp.exp(al_row)
    pre_sp = bd + dt_row
    g = neg_a * _softplus(pre_sp)
    da_in = dbgv * neg_a * _sigmoid(pre_sp)
    dbd = jnp.where(is_b, dbgv * beta * (1.0 - beta), jnp.where(is_g, da_in, 0.0)).astype(BF16)
    dal_row = jnp.sum(jnp.where(is_g, dbgv * g, 0.0), axis=0, keepdims=True)
    ddt_row = jnp.sum(jnp.where(is_g, da_in, 0.0), axis=0, keepdims=True)
    return dc, dbd, dcw_rows, dal_row, ddt_row


def _dp_of_chip(dqkv, dz, dbd, dsc, s):
    lo, hi = IN_SHARD * s, IN_SHARD * (s + 1)
    pieces = []
    for w_at, w_end, block in ((0, W_Z, dqkv), (W_Z, W_BD, dz), (W_BD, W_SC, dbd), (W_SC, W_IN_COLS, dsc)):
        a, b = max(lo, w_at), min(hi, w_end)
        if a < b:
            pieces.append(block[:, a - w_at:b - w_at])
    pieces.append(jnp.zeros((dqkv.shape[0], D_MODEL - IN_SHARD), dqkv.dtype))
    return jnp.concatenate(pieces, axis=1)


def _in_proj_bwd(dq, dk, dv, dbg, qkv, bd, al_row, dt_row, dcv, dgb, sc_in, dz, cw, scw, dx1, x, g1, land_a):
    T = x.shape[0]
    tb = 256
    nb = T // tb

    def body(dq_ref, dk_ref, dv_ref, dbg_ref, pre_ref, pre_halo_ref, bd_ref, al_ref, dt_ref,
             dcv_ref, dcv_halo_ref, dgb_ref, sc_ref, dz_ref, cw_ref, scw_ref, dx1_ref, x_ref, g_ref, w_ref,
             dx_ref, dxb_ref, dps_ref, dg_ref, dcw_ref, dal_ref, ddt_ref, head_ref):
        @pl.when(pl.program_id(0) == 0)
        def _():
            for ref in (dg_ref, dcw_ref, dal_ref, ddt_ref, head_ref):
                ref[...] = jnp.zeros_like(ref)

        block = nb - 1 - pl.program_id(0)
        last = block == nb - 1
        pre_halo = jnp.where(block > 0, pre_halo_ref[...], 0.0)
        dc, dbd, dcw_rows, dal_row, ddt_row = _dn_prep_back(
            dq_ref[...], dk_ref[...], dv_ref[...], dbg_ref[...], pre_ref[...], pre_halo, cw_ref[...], bd_ref[...],
            al_ref[...], dt_ref[...], tb)
        for j in range(4):
            dcw_ref[j:j + 1, :] += dcw_rows[j]
        dal_ref[0:1, :] += dal_row
        ddt_ref[0:1, :] += ddt_row
        xc = jnp.concatenate([dc, head_ref[...]], axis=0)
        head_ref[...] = dc[0:8, :]
        w4 = cw_ref[...]
        dqkv = w4[3:4, :] * xc[0:tb, :]
        for j in range(3):
            dqkv = dqkv + w4[j:j + 1, :] * _rows_from(xc, 3 - j, tb)
        yc = jnp.concatenate([dcv_ref[...], jnp.where(last, 0.0, dcv_halo_ref[...])], axis=0)
        w3 = scw_ref[...]
        du = w3[2:3, :] * yc[0:tb, :] + w3[1:2, :] * _rows_from(yc, 1, tb) + w3[0:1, :] * _rows_from(yc, 2, tb)
        sc = sc_ref[...]
        dsc = jnp.concatenate([dgb_ref[...], du * sc[:, 2 * SC_WIDTH:], du * sc[:, SC_WIDTH:2 * SC_WIDTH]], axis=1)
        blocks = (dqkv.astype(BF16), dz_ref[...], dbd, dsc.astype(BF16))
        dh = jnp.zeros((tb, D_MODEL), F32)
        for s in range(N_CHIPS):
            dps = _dp_of_chip(*blocks, s)
            dps_ref[:, D_MODEL * s:D_MODEL * (s + 1)] = dps
            dh = dh + lax.dot_general(dps, w_ref[s], NT, preferred_element_type=F32)
        xv = x_ref[...]
        r = lax.rsqrt(jnp.mean(xv * xv, axis=-1, keepdims=True) + EPS)
        xh = xv * r
        _row_acc(dg_ref, dh * xh)
        dx = dx1_ref[...] + _rms_bwd(dh, xh, r, g_ref[...])
        dx_ref[...] = dx
        dxb_ref[...] = dx.astype(BF16)

    tok = lambda w: pl.BlockSpec((tb, w), lambda i: (nb - 1 - i, 0))
    full = lambda t: pl.BlockSpec(t.shape, lambda i: (0, 0))
    acc = lambda w: pl.BlockSpec((8, w), lambda i: (0, 0))
    before = lambda w: pl.BlockSpec((8, w), lambda i: _before_halo(tb)(nb - 1 - i))
    after = lambda w: pl.BlockSpec((8, w), lambda i: _after_halo(tb, T)(nb - 1 - i))
    return pl.pallas_call(
        body, name="in_proj_bwd", grid=(nb,),
        in_specs=[tok(DN_WIDTH), tok(DN_WIDTH), tok(DN_WIDTH), tok(LANES), tok(QKV), before(QKV), tok(LANES),
                  full(al_row), full(dt_row), tok(SC_WIDTH), after(SC_WIDTH), tok(SC_WIDTH), tok(3 * SC_WIDTH),
                  tok(DN_WIDTH), full(cw), full(scw), tok(D_MODEL), tok(D_MODEL), full(g1), _shard_rows(land_a, 0, D_MODEL)],
        out_specs=[tok(D_MODEL), tok(D_MODEL), tok(N_CHIPS * D_MODEL), acc(D_MODEL), acc(QKV), acc(LANES), acc(LANES)],
        out_shape=[jax.ShapeDtypeStruct((T, D_MODEL), F32), jax.ShapeDtypeStruct((T, D_MODEL), BF16),
                   jax.ShapeDtypeStruct((T, N_CHIPS * D_MODEL), BF16), jax.ShapeDtypeStruct((8, D_MODEL), F32),
                   jax.ShapeDtypeStruct((8, QKV), F32), jax.ShapeDtypeStruct((8, LANES), F32),
                   jax.ShapeDtypeStruct((8, LANES), F32)],
        scratch_shapes=[pltpu.VMEM((8, QKV), F32)],
        compiler_params=_params(("arbitrary",)),
    )(dq, dk, dv, dbg, qkv, qkv, bd, al_row, dt_row, dcv, dcv, dgb, sc_in, dz, cw, scw, dx1, x, g1, land_a)


def _wgrad_in_share(h, dps, parts, name):
    T = h.shape[0]
    bk = min(T, 1024)
    n_k = T // bk

    def body(a_ref, b_ref, parts_ref, o_ref, acc_ref):
        kk = pl.program_id(1)

        @pl.when(kk == 0)
        def _():
            acc_ref[...] = jnp.zeros_like(acc_ref)

        acc_ref[...] += lax.dot_general(a_ref[...], b_ref[...], TN, preferred_element_type=F32)

        @pl.when(kk == n_k - 1)
        def _():
            o_ref[0] = acc_ref[...].astype(BF16)

    return pl.pallas_call(
        body, name=name, grid=(N_CHIPS, n_k),
        in_specs=[pl.BlockSpec((bk, D_MODEL), lambda j, kk: (kk, 0)), pl.BlockSpec((bk, D_MODEL), lambda j, kk: (kk, j)), _ANY],
        out_specs=pl.BlockSpec((1, D_MODEL, D_MODEL), lambda j, kk: (j, 0, 0)),
        out_shape=jax.ShapeDtypeStruct(parts.shape, BF16),
        scratch_shapes=[pltpu.VMEM((D_MODEL, D_MODEL), F32)],
        input_output_aliases={2: 0},
        compiler_params=_params(("parallel", "arbitrary")),
    )(h, dps, parts)


def _pad_rows(a, rows=8):
    return jnp.pad(a, ((0, rows - a.shape[0]), (0, 0)))


def _gate_rows(a_log, dt_bias):
    put = lambda t: jnp.pad(t.reshape(1, HEADS), ((0, 0), (HEADS, LANES - 2 * HEADS)))
    return put(a_log), put(dt_bias)


def _mixer_fwd(x, p):
    qkv, z, sc_in, bd, h, q, k, v, bg = _in_proj(x, p["g1"], p["land_a"], p["cw"], p["al"], p["dt"])
    o, states = _delta_fwd(q, k, v, bg)
    x1, mix = _mix_out(o, z, sc_in, x, p["land_a"], p["gn"], p["scw"], p["gs"])
    return x1, dict(x=x, qkv=qkv, z=z, sc_in=sc_in, bd=bd, h=h, q=q, k=k, v=v, bg=bg, o=o, states=states, mix=mix)


def _ffn_fwd(x1, p, land_b):
    x2, a, b, h2 = _ffn(x1, p["g2"], land_b)
    return x2, dict(x1=x1, a=a, b=b, h2=h2)


def _ffn_back(dx2, dx2_bf16, s, p, land_b):
    dx1, dx1_bf16, da, db, act, dg2 = _ffn_bwd(dx2, s["x1"], s["a"], s["b"], p["g2"], land_b)
    parts = lax.empty((N_CHIPS, B_ROWS, D_MODEL), BF16)
    parts = _wgrad_share(act, dx2_bf16, parts, 2 * FF_SHARD, "wgrad_down")
    parts = _wgrad_share(da, s["h2"], parts, 0, "wgrad_gate")
    parts = _wgrad_share(db, s["h2"], parts, FF_SHARD, "wgrad_up")
    return dx1, dx1_bf16, parts, dg2[0]


def _mixer_bwd(dx1, dx1_bf16, s, p):
    do, dz, dgb, dcv, dgn, dgs, dscw = _mix_out_bwd(dx1, s["o"], s["z"], s["sc_in"], p["land_a"], p["gn"], p["scw"], p["gs"])
    dq, dk, dv, dbg = _delta_bwd(s["q"], s["k"], s["v"], s["bg"], s["states"], do)
    dx, dx_bf16, dps, dg1, dcw, dal, ddt = _in_proj_bwd(
        dq, dk, dv, dbg, s["qkv"], s["bd"], p["al"], p["dt"], dcv, dgb, s["sc_in"], dz, p["cw"], p["scw"], dx1, s["x"],
        p["g1"], p["land_a"])
    parts = lax.empty((N_CHIPS, A_ROWS, D_MODEL), BF16)
    parts = _wgrad_in_share(s["h"], dps, parts, "wgrad_in")
    parts = _wgrad_share(s["mix"], dx1_bf16, parts, A_OUT_AT, "wgrad_out")
    g = dict(g1=dg1[0], gn=dgn[0], gs=dgs[0], scw=dscw[:3], cw=dcw[:4], al=dal[0, HEADS:2 * HEADS], dt=ddt[0, HEADS:2 * HEADS])
    return dx, dx_bf16, parts, g


def _place():
    return lax.axis_index("x"), lax.axis_index("y"), lax.axis_index("c")


def _other_chips(x, y):
    return [(1 - x, y), (x, 1 - y), (1 - x, 1 - y)]


_HBM = pl.BlockSpec(memory_space=pltpu.HBM)


def _chip_exchange(arrs, name, gather):
    n = len(arrs)

    def body(*refs):
        ins, outs = refs[:n], refs[n:2 * n]
        send_sems, recv_sems, local_sems = refs[2 * n:]
        x, y, c = _place()
        me = 2 * x + y
        others = _other_chips(x, y)

        def remote(k, j, landing):
            px, py = others[j]
            src = ins[k] if gather else ins[k].at[2 * px + py]
            return pltpu.make_async_remote_copy(src_ref=src, dst_ref=outs[k].at[landing], send_sem=send_sems.at[k, j],
                                                recv_sem=recv_sems.at[k, j], device_id=(px, py, c), device_id_type=MESH)

        local = [pltpu.make_async_copy(ins[k] if gather else ins[k].at[me], outs[k].at[me], local_sems.at[k])
                 for k in range(n)]
        sends = [remote(k, j, me) for k in range(n) for j in range(3)]
        for cp in local + sends:
            cp.start()
        for k in range(n):
            for j, (px, py) in enumerate(others):
                remote(k, j, 2 * px + py).wait_recv()
        for cp in sends:
            cp.wait_send()
        for cp in local:
            cp.wait()

    shapes = [jax.ShapeDtypeStruct(((N_CHIPS,) + a.shape) if gather else a.shape, a.dtype) for a in arrs]
    return pl.pallas_call(
        body, name=name, in_specs=[_HBM] * n, out_specs=[_HBM] * n, out_shape=shapes,
        scratch_shapes=[pltpu.SemaphoreType.DMA((n, 3)), pltpu.SemaphoreType.DMA((n, 3)), pltpu.SemaphoreType.DMA((n,))],
    )(*arrs)


_SEM = pl.BlockSpec(memory_space=pltpu.SEMAPHORE)
_ANY = pl.BlockSpec(memory_space=pl.ANY)
_EFFECT = pltpu.SideEffectType.DATAFLOW_SIDE_EFFECTING


_FLIPS = [(a, b, cc) for a in (0, 1) for b in (0, 1) for cc in (0, 1)][1:]


def _split_copies(src_ref, land_ref, send_sems, recv_sems, gather, sending):
    x, y, c = _place()
    copies = []
    if gather:
        me = 2 * x + y
        for j, (px, py) in enumerate(_other_chips(x, y)):
            copies.append(pltpu.make_async_remote_copy(
                src_ref=src_ref, dst_ref=land_ref.at[me if sending else 2 * px + py],
                send_sem=send_sems.at[j], recv_sem=recv_sems.at[j], device_id=(px, py, c), device_id_type=MESH))
        return copies
    me = 4 * x + 2 * y + c
    for j, (a, b, cc) in enumerate(_FLIPS):
        px, py, pc = (1 - x) if a else x, (1 - y) if b else y, (1 - c) if cc else c
        copies.append(pltpu.make_async_remote_copy(
            src_ref=src_ref.at[2 * px + py], dst_ref=land_ref.at[me if sending else 4 * px + 2 * py + pc],
            send_sem=send_sems.at[j], recv_sem=recv_sems.at[j], device_id=(px, py, pc), device_id_type=MESH))
    return copies


def _own_slot(share):
    chip = 2 * lax.axis_index("x") + lax.axis_index("y")
    return lax.dynamic_update_slice(lax.empty((N_CHIPS,) + share.shape, share.dtype), share[None], (chip, 0, 0))


def _own_part(parts):
    chip = 2 * lax.axis_index("x") + lax.axis_index("y")
    own = lax.dynamic_index_in_dim(parts, chip, 0, keepdims=True)
    return lax.dynamic_update_slice(lax.empty((N_DEV,) + parts.shape[1:], parts.dtype), own,
                                    (2 * chip + lax.axis_index("c"), 0, 0))


def _exchange_start(src, land, after, name, gather):
    def body(src_ref, land_ref, after_ref, send_sems, recv_sems, src_thru, land_thru, token):
        for cp in _split_copies(src_ref, land_ref, send_sems, recv_sems, gather, sending=True):
            cp.start()
        token[...] = jnp.zeros_like(token)

    hbm = lambda t: pltpu.with_memory_space_constraint(t, pltpu.HBM)
    n_copies = N_CHIPS - 1 if gather else N_DEV - 1
    return pl.pallas_call(
        body, name=name,
        out_shape=(pltpu.SemaphoreType.DMA((n_copies,)), pltpu.SemaphoreType.DMA((n_copies,)), pltpu.HBM(src.shape, src.dtype),
                   pltpu.HBM(land.shape, land.dtype), jax.ShapeDtypeStruct((8, LANES), F32)),
        in_specs=(_HBM, _HBM, _ANY), out_specs=(_SEM, _SEM, _HBM, _HBM, pl.BlockSpec(memory_space=pltpu.VMEM)),
        input_output_aliases={0: 2, 1: 3},
        compiler_params=pltpu.CompilerParams(has_side_effects=_EFFECT),
    )(hbm(src), hbm(land), after)


def _exchange_wait(started, after, name, gather):
    send_sems, recv_sems, src_thru, land_thru, _ = started

    def body(src_ref, land_ref, send_sems, recv_sems, after_ref, src_dead, got_ref):
        for cp in _split_copies(src_ref, land_ref, send_sems, recv_sems, gather, sending=False):
            cp.wait_send()
            cp.wait_recv()

    return pl.pallas_call(
        body, name=name,
        out_shape=(pltpu.HBM(src_thru.shape, src_thru.dtype), pltpu.HBM(land_thru.shape, land_thru.dtype)),
        in_specs=(_HBM, _HBM, _SEM, _SEM, _ANY), out_specs=(_HBM, _HBM), input_output_aliases={0: 0, 1: 1},
        compiler_params=pltpu.CompilerParams(has_side_effects=_EFFECT),
    )(src_thru, land_thru, send_sems, recv_sems, after)[1]


def _all_reduce_small(v):
    rows = v.shape[0]
    flips = [(a, b, cc) for a in (0, 1) for b in (0, 1) for cc in (0, 1)][1:]

    def body(v_ref, out_ref, buf_ref, send_sems, recv_sems):
        x, y, c = _place()
        me = 4 * x + 2 * y + c
        peers = [((1 - x) if a else x, (1 - y) if b else y, (1 - c) if cc else c) for a, b, cc in flips]

        def copy(j, landing):
            return pltpu.make_async_remote_copy(src_ref=v_ref, dst_ref=buf_ref.at[landing], send_sem=send_sems.at[j],
                                                recv_sem=recv_sems.at[j], device_id=peers[j], device_id_type=MESH)

        sends = [copy(j, me) for j in range(N_DEV - 1)]
        for cp in sends:
            cp.start()
        buf_ref[me] = v_ref[...]
        for j, (px, py, pc) in enumerate(peers):
            copy(j, 4 * px + 2 * py + pc).wait_recv()
        for cp in sends:
            cp.wait_send()
        acc = buf_ref[0]
        for d in range(1, N_DEV):
            acc = acc + buf_ref[d]
        out_ref[...] = acc

    vmem = pl.BlockSpec(memory_space=pltpu.VMEM)
    return pl.pallas_call(
        body, name="all_reduce_small", in_specs=[vmem], out_specs=vmem,
        out_shape=jax.ShapeDtypeStruct(v.shape, F32),
        scratch_shapes=[pltpu.VMEM((N_DEV, rows, LANES), F32), pltpu.SemaphoreType.DMA((N_DEV - 1,)),
                        pltpu.SemaphoreType.DMA((N_DEV - 1,))],
    )(v)


def _row_block(*sizes):
    return next(t for t in (128, 64) if all(s % t == 0 for s in sizes))


def _adam_update(w, m, v, g):
    r1 = 1.0 / (1.0 - ADAM_B1 ** ADAM_STEP)
    r2 = 1.0 / (1.0 - ADAM_B2 ** ADAM_STEP)
    m_new = ADAM_B1 * m + (1.0 - ADAM_B1) * g
    v_new = ADAM_B2 * v + (1.0 - ADAM_B2) * (g * g)
    return -ADAM_LR * ((m_new * r1) / (jnp.sqrt(v_new * r2) + ADAM_EPS) + ADAM_WD * w), m_new, v_new


def _adamw_rows(w, m, v, got, first, name):
    n_layers, rows, cols = w.shape
    tr = _row_block(rows, first)

    def body(*refs):
        w_ref, m_ref, v_ref = refs[:3]
        g_out, d_out, m_out, v_out = refs[3 + n_layers:]
        for k in range(n_layers):
            @pl.when(pl.program_id(0) == k)
            def _(p_ref=refs[3 + k]):
                g = p_ref[0].astype(F32)
                for d in range(1, N_DEV):
                    g = g + p_ref[d].astype(F32)
                g = g[:, :cols]
                d_out[0], m_out[0], v_out[0] = _adam_update(w_ref[0], m_ref[0], v_ref[0], g)
                g_out[0] = g

    blk = pl.BlockSpec((1, tr, cols), lambda l, i: (l, i, 0))
    parts = [pl.BlockSpec((N_DEV, tr, got[0].shape[2]), lambda l, i, k=k: (0, jnp.where(l == k, first // tr + i, 0), 0))
             for k in range(n_layers)]
    return pl.pallas_call(
        body, name=name, grid=(n_layers, rows // tr),
        in_specs=[blk] * 3 + parts, out_specs=[blk] * 4,
        out_shape=[jax.ShapeDtypeStruct(w.shape, F32)] * 4,
        compiler_params=_params(("arbitrary", "arbitrary")),
    )(w, m, v, *got)


def _adamw(w, m, v, g_parts, name):
    rows, cols = w.shape
    tr = min(rows, 256)
    n = len(g_parts)

    def body(*refs):
        w_ref, m_ref, v_ref = refs[:3]
        g_refs = refs[3:3 + n]
        g_out, d_out, m_out, v_out = refs[3 + n:]
        g = g_refs[0][...]
        for r in g_refs[1:]:
            g = g + r[...]
        d_out[...], m_out[...], v_out[...] = _adam_update(w_ref[...], m_ref[...], v_ref[...], g)
        g_out[...] = g

    blk = pl.BlockSpec((tr, cols), lambda i: (i, 0))
    return pl.pallas_call(
        body, name=name, grid=(rows // tr,),
        in_specs=[blk] * (3 + n), out_specs=[blk] * 4,
        out_shape=[jax.ShapeDtypeStruct((rows, cols), F32)] * 4,
        compiler_params=_params(("parallel",)),
    )(w, m, v, *g_parts)


def _pack(parts, rows, fill=0.0):
    flat = jnp.concatenate([p.reshape(-1) for p in parts])
    return jnp.pad(flat, (0, rows * LANES - flat.shape[0]), constant_values=fill).reshape(rows, LANES)


def _unpack(packed, shapes):
    flat = packed.reshape(-1)
    out, at = [], 0
    for shp in shapes:
        size = 1
        for s in shp:
            size *= s
        out.append(flat[at:at + size].reshape(shp))
        at += size
    return out


def _packed_rows(shapes):
    total = 0
    for shp in shapes:
        size = 1
        for s in shp:
            size *= s
        total += size
    return -(-total // (8 * LANES)) * 8


def _cols_full(g, l):
    t = g[:, l]
    return jnp.moveaxis(t, 0, 1).reshape(t.shape[1], N_CHIPS * t.shape[2])


def _pad_cols(t):
    return jnp.pad(t, ((0, 0),) * (t.ndim - 1) + ((0, D_MODEL - t.shape[-1]),))


def kernel(x, norm1_g, w_in, dn_conv_w, dn_a_log, dn_dt_bias, dn_norm_g, sc_conv_w, sc_norm_g, w_out, norm2_g, ffn_w_gate, ffn_w_up, ffn_w_down, final_norm_g, loss_target, m_norm1_g, m_w_in, m_dn_conv_w, m_dn_a_log, m_dn_dt_bias, m_dn_norm_g, m_sc_conv_w, m_sc_norm_g, m_w_out, m_norm2_g, m_ffn_w_gate, m_ffn_w_up, m_ffn_w_down, m_final_norm_g, v_norm1_g, v_w_in, v_dn_conv_w, v_dn_a_log, v_dn_dt_bias, v_dn_norm_g, v_sc_conv_w, v_sc_norm_g, v_w_out, v_norm2_g, v_ffn_w_gate, v_ffn_w_up, v_ffn_w_down, v_final_norm_g):
    chip = 2 * lax.axis_index("x") + lax.axis_index("y")

    g_cw, g_scw = _chip_exchange([dn_conv_w, sc_conv_w], "gather_conv", gather=True)

    t_last = lambda t: jnp.swapaxes(t, -1, -2)
    gate_t, up_t = t_last(ffn_w_gate), t_last(ffn_w_up)
    zero_token = jnp.zeros((8, LANES), F32)

    def shares(l, tie):
        share_a = jnp.concatenate([_pad_cols(w_in[l] + tie), w_out[l]], axis=0).astype(BF16)
        share_b = jnp.concatenate([gate_t[l] + tie, up_t[l], ffn_w_down[l]], axis=0).astype(BF16)
        return share_a, _own_slot(share_a), share_b, _own_slot(share_b)

    def gather_start(l, packed, after):
        a = _exchange_start(packed[0], packed[1], after, "gather_a_start_%d" % l, gather=True)
        b = _exchange_start(packed[2], packed[3], a[4], "gather_b_start_%d" % l, gather=True)
        return a, b

    ga, gb = gather_start(0, shares(0, 0.0), g_cw)
    packed = [None] + [shares(l, gb[4][0, 0]) for l in range(1, DEPTH)]
    packed_all = sum(t[0, 0].astype(F32) for p in packed[1:] for t in (p[0], p[2]))
    land_a = _exchange_wait(ga, zero_token + packed_all, "gather_a_wait_0", gather=True)
    act = x[0]
    layers, saved_m, saved_f, lands_b = [], [], [], []
    for l in range(DEPTH):
        hold = 0.0
        if l + 1 < DEPTH:
            ga, gb_next = gather_start(l + 1, packed[l + 1], land_a)
            hold = gb_next[4][0:1, 0:1]
        al, dt = _gate_rows(dn_a_log[l], dn_dt_bias[l])
        layers.append(dict(
            g1=norm1_g[l][None] + hold, cw=_pad_rows(_cols_full(g_cw, l)), al=al, dt=dt,
            gn=dn_norm_g[l][None], scw=_pad_rows(_cols_full(g_scw, l)), gs=sc_norm_g[l][None],
            land_a=land_a, g2=norm2_g[l][None]))
        x1, s = _mixer_fwd(act, layers[l])
        saved_m.append(s)
        lands_b.append(_exchange_wait(gb, x1, "gather_b_wait_%d" % l, gather=True))
        act, s = _ffn_fwd(x1, layers[l], lands_b[l])
        saved_f.append(s)
        if l + 1 < DEPTH:
            land_a = _exchange_wait(ga, act, "gather_a_wait_%d" % (l + 1), gather=True)
            gb = gb_next

    dact, dact_bf16, loss_part, d_final = _loss_head(act, final_norm_g[None], loss_target[0])
    grads, reduce_a, reduce_b = [None] * DEPTH, [None] * DEPTH, [None] * DEPTH
    hold = 0.0
    for l in reversed(range(DEPTH)):
        p = layers[l]
        dx1, dx1_bf16, parts, dg2 = _ffn_back(dact, dact_bf16, saved_f[l], dict(p, g2=p["g2"] + hold), lands_b[l])
        reduce_b[l] = _exchange_start(parts, _own_part(parts), zero_token, "reduce_b_start_%d" % l, gather=False)
        dact, dact_bf16, parts, gm = _mixer_bwd(dx1, dx1_bf16, saved_m[l], dict(p, gn=p["gn"] + reduce_b[l][4][0:1, 0:1]))
        reduce_a[l] = _exchange_start(parts, _own_part(parts), zero_token, "reduce_a_start_%d" % l, gather=False)
        hold = reduce_a[l][4][0:1, 0:1]
        grads[l] = dict(gm, g2=dg2)
    loss = lax.psum(loss_part[0, 0], ("x", "y", "c"))
    stack = lambda key: jnp.stack([grads[l][key] for l in range(DEPTH)])

    got_b = [_exchange_wait(reduce_b[l], reduce_a[0][4], "reduce_b_wait_%d" % l, gather=False)
             for l in reversed(range(DEPTH))][::-1]
    big = dict(
        ffn_w_gate=[t_last(o) for o in _adamw_rows(gate_t, t_last(m_ffn_w_gate), t_last(v_ffn_w_gate), got_b, 0, "adamw_gate")],
        ffn_w_up=[t_last(o) for o in _adamw_rows(up_t, t_last(m_ffn_w_up), t_last(v_ffn_w_up), got_b, FF_SHARD, "adamw_up")],
        ffn_w_down=_adamw_rows(ffn_w_down, m_ffn_w_down, v_ffn_w_down, got_b, 2 * FF_SHARD, "adamw_down"))
    after_b = zero_token + sum(big[n][1][0, 0, 0] for n in ("ffn_w_gate", "ffn_w_up", "ffn_w_down"))
    got_a = [_exchange_wait(reduce_a[l], after_b, "reduce_a_wait_%d" % l, gather=False) for l in reversed(range(DEPTH))][::-1]
    big.update(
        w_in=_adamw_rows(w_in, m_w_in, v_w_in, got_a, 0, "adamw_w_in"),
        w_out=_adamw_rows(w_out, m_w_out, v_w_out, got_a, A_OUT_AT, "adamw_w_out"))

    full_shapes = [(DEPTH, D_MODEL), (DEPTH, D_MODEL), (DEPTH, HEAD_DIM), (DEPTH, SC_WIDTH), (DEPTH, HEADS),
                   (DEPTH, HEADS), (D_MODEL,), (DEPTH, 4, QKV), (DEPTH, 3, SC_WIDTH)]
    small_keys = ("g1", "g2", "gn", "gs", "al", "dt")
    packed = _pack([stack(k) for k in small_keys] + [d_final[0], stack("cw"), stack("scw")], _packed_rows(full_shapes))
    sg = _unpack(_all_reduce_small(packed), full_shapes)
    sg[7] = lax.dynamic_slice_in_dim(sg[7], chip * (QKV // N_CHIPS), QKV // N_CHIPS, axis=2)
    sg[8] = lax.dynamic_slice_in_dim(sg[8], chip * (SC_WIDTH // N_CHIPS), SC_WIDTH // N_CHIPS, axis=2)
    small_names = ("norm1_g", "norm2_g", "dn_norm_g", "sc_norm_g", "dn_a_log", "dn_dt_bias", "final_norm_g",
                   "dn_conv_w", "sc_conv_w")
    sw = (norm1_g, norm2_g, dn_norm_g, sc_norm_g, dn_a_log, dn_dt_bias, final_norm_g, dn_conv_w, sc_conv_w)
    sm = (m_norm1_g, m_norm2_g, m_dn_norm_g, m_sc_norm_g, m_dn_a_log, m_dn_dt_bias, m_final_norm_g, m_dn_conv_w, m_sc_conv_w)
    sv = (v_norm1_g, v_norm2_g, v_dn_norm_g, v_sc_norm_g, v_dn_a_log, v_dn_dt_bias, v_final_norm_g, v_dn_conv_w, v_sc_conv_w)
    shard_shapes = [t.shape for t in sw]
    rows = _packed_rows(shard_shapes)
    outs = _adamw(_pack(sw, rows), _pack(sm, rows), _pack(sv, rows, fill=1.0), [_pack(sg, rows)], "adamw_small")
    small = {name: [] for name in small_names}
    for o in outs:
        for name, t in zip(small_names, _unpack(o, shard_shapes)):
            small[name].append(t)

    order = ("norm1_g", "w_in", "dn_conv_w", "dn_a_log", "dn_dt_bias", "dn_norm_g", "sc_conv_w", "sc_norm_g", "w_out",
             "norm2_g", "ffn_w_gate", "ffn_w_up", "ffn_w_down", "final_norm_g")
    result = {**big, **small}
    return (loss, dact[None], *[result[n][0] for n in order], *[result[n][1] for n in order],
            *[result[n][2] for n in order], *[result[n][3] for n in order])
```

```python
import jax
import jax.numpy as jnp
from jax import lax
from jax.experimental import pallas as pl
from jax.experimental.pallas import tpu as pltpu

F32 = jnp.float32
BF16 = jnp.bfloat16
MESH = pl.DeviceIdType.MESH

D_MODEL = 1024
DEPTH = 4
HEADS = 4
HEAD_DIM = 128
DN_WIDTH = HEADS * HEAD_DIM
SC_WIDTH = 512
SC_GROUPS = 4
D_FF = 2816
CHUNK = 64
QKV = 3 * DN_WIDTH
W_IN_COLS = 4 * DN_WIDTH + 2 * HEADS + 3 * SC_WIDTH
WA_COLS = QKV + DN_WIDTH + 3 * SC_WIDTH
LANES = 128
EPS = 1e-6
Q_SCALE = HEAD_DIM ** -0.5
N_CHIPS = 4
N_DEV = 8
IN_SHARD = W_IN_COLS // N_CHIPS
OUT_SHARD = D_MODEL // N_CHIPS
FF_SHARD = D_FF // N_CHIPS
A_OUT_AT = D_MODEL
A_ROWS = D_MODEL + OUT_SHARD
B_ROWS = 3 * FF_SHARD

ADAM_LR = 0.001
ADAM_B1 = 0.9
ADAM_B2 = 0.999
ADAM_EPS = 1e-08
ADAM_WD = 0.01
ADAM_STEP = 10

VMEM_LIMIT = 56 * 1024 * 1024

NN = (((1,), (0,)), ((), ()))
NT = (((1,), (1,)), ((), ()))
TN = (((0,), (0,)), ((), ()))


def _mm(a, b, dims=NN):
    return lax.dot_general(a.astype(BF16), b.astype(BF16), dims, preferred_element_type=F32)


def _mm32(a, b, dims=NN):
    return lax.dot_general(a, b, dims, preferred_element_type=F32, precision=lax.Precision.HIGHEST)


def _params(sem, vmem=VMEM_LIMIT):
    return pltpu.CompilerParams(dimension_semantics=sem, vmem_limit_bytes=vmem)


def _sigmoid(x):
    return 0.5 * jnp.tanh(0.5 * x) + 0.5


def _softplus(x):
    return jnp.maximum(x, 0.0) + jnp.log1p(jnp.exp(-jnp.abs(x)))


def _row_acc(acc_ref, val):
    acc_ref[0:1, :] += jnp.sum(val, axis=0, keepdims=True)


def _rms_bwd(dh, xh, r, gain):
    dxh = dh * gain
    return r * (dxh - xh * jnp.mean(dxh * xh, axis=-1, keepdims=True))


def _before_halo(tb):
    return lambda i: (jnp.maximum(i * (tb // 8) - 1, 0), 0)


def _after_halo(tb, n_rows):
    last = n_rows // 8 - 1
    return lambda i: (jnp.minimum((i + 1) * (tb // 8), last), 0)


def _rows_from(xc, offset, tb):
    part = offset % 8
    if part:
        xc = pltpu.roll(xc, xc.shape[0] - part, 0)
    return xc[offset - part:offset - part + tb, :]


def _taps(xc, w, n_taps, tb, first):
    out = w[0:1, :] * _rows_from(xc, first, tb)
    for j in range(1, n_taps):
        out = out + w[j:j + 1, :] * _rows_from(xc, first + j, tb)
    return out


W_Z = QKV
W_BD = W_Z + DN_WIDTH
W_SC = W_BD + 2 * HEADS

def _w_in_cols(shards, lo, hi):
    pieces = []
    for s in range(N_CHIPS):
        a, b = max(lo, IN_SHARD * s), min(hi, IN_SHARD * (s + 1))
        if a < b:
            pieces.append(shards[s][:, a - IN_SHARD * s:b - IN_SHARD * s])
    return pieces[0] if len(pieces) == 1 else jnp.concatenate(pieces, axis=1)


def _in_proj(x, g1, land_a, cw, al_row, dt_row):
    T = x.shape[0]
    tb = 256

    def body(x_ref, g_ref, w_ref, cw_ref, al_ref, dt_ref,
             qkv_ref, z_ref, sc_ref, bd_ref, h_ref, q_ref, k_ref, v_ref, bg_ref, tail_ref):
        @pl.when(pl.program_id(0) == 0)
        def _():
            tail_ref[...] = jnp.zeros_like(tail_ref)

        xv = x_ref[...]
        h = (xv * lax.rsqrt(jnp.mean(xv * xv, axis=-1, keepdims=True) + EPS) * g_ref[...]).astype(BF16)
        shards = [jnp.dot(h, w_ref[s], preferred_element_type=F32) for s in range(N_CHIPS)]
        qkv = _w_in_cols(shards, 0, W_Z)
        bd = jnp.concatenate([_w_in_cols(shards, W_BD, W_SC), jnp.zeros((tb, LANES - 2 * HEADS), F32)], axis=1)
        qkv_ref[...] = qkv
        z_ref[...] = _w_in_cols(shards, W_Z, W_BD)
        bd_ref[...] = bd
        sc_ref[...] = _w_in_cols(shards, W_SC, W_IN_COLS)
        h_ref[...] = h
        halo = tail_ref[...]
        tail_ref[...] = qkv[tb - 8:, :]
        _, _, _, a = _dn_act(qkv, halo, cw_ref[...], tb)
        for hd in range(HEADS):
            sl = slice(HEAD_DIM * hd, HEAD_DIM * (hd + 1))
            qs = a[:, sl]
            q_ref[:, sl] = qs * (lax.rsqrt(jnp.sum(qs * qs, axis=-1, keepdims=True) + EPS) * Q_SCALE)
            ks = a[:, DN_WIDTH + HEAD_DIM * hd:DN_WIDTH + HEAD_DIM * (hd + 1)]
            k_ref[:, sl] = ks * lax.rsqrt(jnp.sum(ks * ks, axis=-1, keepdims=True) + EPS)
        v_ref[...] = a[:, 2 * DN_WIDTH:]
        gates = _gates(bd, al_ref[...], dt_ref[...])
        lane = lax.broadcasted_iota(jnp.int32, gates.shape, 1)
        bg_ref[...] = jnp.where(lane < HEADS, gates, _mm32(_chunk_cumsum_matrix(tb), gates))

    tok = lambda w: pl.BlockSpec((tb, w), lambda i: (i, 0))
    full = lambda t: pl.BlockSpec(t.shape, lambda i: (0, 0))
    return pl.pallas_call(
        body, name="in_proj", grid=(T // tb,),
        in_specs=[tok(D_MODEL), full(g1), _shard_rows(land_a, 0, D_MODEL), full(cw), full(al_row), full(dt_row)],
        out_specs=[tok(QKV), tok(DN_WIDTH), tok(3 * SC_WIDTH), tok(LANES), tok(D_MODEL),
                   tok(DN_WIDTH), tok(DN_WIDTH), tok(DN_WIDTH), tok(LANES)],
        out_shape=[jax.ShapeDtypeStruct((T, QKV), F32), jax.ShapeDtypeStruct((T, DN_WIDTH), F32),
                   jax.ShapeDtypeStruct((T, 3 * SC_WIDTH), F32), jax.ShapeDtypeStruct((T, LANES), F32),
                   jax.ShapeDtypeStruct((T, D_MODEL), BF16)]
        + [jax.ShapeDtypeStruct((T, DN_WIDTH), F32)] * 3 + [jax.ShapeDtypeStruct((T, LANES), F32)],
        scratch_shapes=[pltpu.VMEM((8, QKV), F32)],
        compiler_params=_params(("arbitrary",)),
    )(x, g1, land_a, cw, al_row, dt_row)


def _dn_act(pre, halo, cw, tb):
    xc = jnp.concatenate([halo, pre], axis=0)
    c = _taps(xc, cw, 4, tb, 5)
    sg = _sigmoid(c)
    return xc, c, sg, c * sg


def _gates(bd, al_row, dt_row):
    lane = lax.broadcasted_iota(jnp.int32, bd.shape, 1)
    beta = _sigmoid(bd)
    g = -jnp.exp(al_row) * _softplus(bd + dt_row)
    return jnp.where(lane < HEADS, beta, jnp.where(lane < 2 * HEADS, g, 0.0))


def _chunk_masks():
    row = lax.broadcasted_iota(jnp.int32, (CHUNK, CHUNK), 0)
    col = lax.broadcasted_iota(jnp.int32, (CHUNK, CHUNK), 1)
    return row >= col, row > col


def _chunk_cumsum_matrix(n):
    row = lax.broadcasted_iota(jnp.int32, (n, n), 0)
    col = lax.broadcasted_iota(jnp.int32, (n, n), 1)
    return jnp.logical_and(row >= col, row // CHUNK == col // CHUNK).astype(F32)


def _chunk_units(q_ref, k_ref, v_ref, bg_ref, rows):
    bgc = bg_ref[rows, :]
    bg_t = bgc.T
    qv, kv, vv = q_ref[rows, :], k_ref[rows, :], v_ref[rows, :]
    units = []
    for h in range(HEADS):
        sl = slice(HEAD_DIM * h, HEAD_DIM * (h + 1))
        units.append((qv[:, sl], kv[:, sl], vv[:, sl], bgc[:, h:h + 1], bgc[:, HEADS + h:HEADS + h + 1],
                      bg_t[HEADS + h:HEADS + h + 1, :]))
    return units


def _units_local(units, masks):
    causal, strict = masks
    pre = []
    for q, k, v, beta, gc, gr in units:
        kb = k * beta
        eg = jnp.exp(gc)
        g_last = gc[CHUNK - 1:CHUNK, :]
        ek = jnp.exp(g_last - gc)
        pre.append(dict(q=q, k=k, v=v, beta=beta, decay=jnp.exp(jnp.where(causal, gc - gr, -1e30)), kb=kb, vb=v * beta,
                        eg=eg, kbg=kb * eg, ek=ek, gl=jnp.exp(g_last), q_dec=q * eg, k_dec=k * ek))
    both = [_mm(jnp.concatenate([p["kb"], p["q"]], axis=0), p["k"], NT) for p in pre]
    for p, b in zip(pre, both):
        p["low"] = jnp.where(strict, b[:CHUNK] * p["decay"], 0.0)
        p["qk"] = jnp.where(causal, b[CHUNK:] * p["decay"], 0.0)
    xs = [-p["low"] for p in pre]
    pw = [_mm(p["low"], p["low"]) for p in pre]
    for _ in range(4):
        both = [_mm(jnp.concatenate([pp, x], axis=0), pp) for pp, x in zip(pw, xs)]
        xs = [x + pp + b[CHUNK:] for x, pp, b in zip(xs, pw, both)]
        pw = [b[:CHUNK] for b in both]
    last = [_mm(x, pp) for x, pp in zip(xs, pw)]
    xs = [x + pp + b for x, pp, b in zip(xs, pw, last)]
    uw = [_mm(x, jnp.concatenate([p["vb"], p["kbg"]], axis=1)) for x, p in zip(xs, pre)]
    for p, x, b in zip(pre, xs, uw):
        p["xm"] = x
        p["u"] = p["vb"] + b[:, :HEAD_DIM]
        p["w"] = p["kbg"] + b[:, HEAD_DIM:]
    return pre


FWD_GROUP = 4
BWD_GROUP = 4


def _delta_fwd(q, k, v, bg):
    T = q.shape[0]
    tb = 512
    n_chunk = tb // CHUNK

    def body(q_ref, k_ref, v_ref, bg_ref, o_ref, st_ref, s_ref):
        @pl.when(pl.program_id(0) == 0)
        def _():
            s_ref[...] = jnp.zeros_like(s_ref)

        masks = _chunk_masks()

        def group(gi, carry):
            rows = [pl.ds(pl.multiple_of((FWD_GROUP * gi + j) * CHUNK, CHUNK), CHUNK) for j in range(FWD_GROUP)]
            loc = _units_local(sum((_chunk_units(q_ref, k_ref, v_ref, bg_ref, r) for r in rows), []), masks)
            states = [s_ref[h] for h in range(HEADS)]
            for j in range(FWD_GROUP):
                lj = loc[HEADS * j:HEADS * (j + 1)]
                ws = [_mm(jnp.concatenate([p["w"], p["q_dec"]], axis=0), s) for p, s in zip(lj, states)]
                v_new = [p["u"] - b[:CHUNK] for p, b in zip(lj, ws)]
                intra = [_mm(p["qk"], vn) for p, vn in zip(lj, v_new)]
                upd = [_mm(p["k_dec"], vn, TN) for p, vn in zip(lj, v_new)]
                o_ref[rows[j], :] = jnp.concatenate([b[CHUNK:] + a for b, a in zip(ws, intra)], axis=1)
                for h in range(HEADS):
                    st_ref[FWD_GROUP * gi + j, h] = states[h]
                states = [p["gl"] * s + d for p, s, d in zip(lj, states, upd)]
            for h in range(HEADS):
                s_ref[h] = states[h]
            return carry

        lax.fori_loop(0, n_chunk // FWD_GROUP, group, 0)

    tok = lambda w: pl.BlockSpec((tb, w), lambda i: (i, 0))
    return pl.pallas_call(
        body, name="delta_fwd", grid=(T // tb,),
        in_specs=[tok(DN_WIDTH), tok(DN_WIDTH), tok(DN_WIDTH), tok(LANES)],
        out_specs=[tok(DN_WIDTH), pl.BlockSpec((n_chunk, HEADS, HEAD_DIM, HEAD_DIM), lambda i: (i, 0, 0, 0))],
        out_shape=[jax.ShapeDtypeStruct((T, DN_WIDTH), F32),
                   jax.ShapeDtypeStruct((T // CHUNK, HEADS, HEAD_DIM, HEAD_DIM), F32)],
        scratch_shapes=[pltpu.VMEM((HEADS, HEAD_DIM, HEAD_DIM), F32)],
        compiler_params=_params(("arbitrary",)),
    )(q, k, v, bg)


def _dn_out(o, z, gn):
    outs, ohs, rs = [], [], []
    for hh in range(HEADS):
        oh = o[:, HEAD_DIM * hh:HEAD_DIM * (hh + 1)]
        r = lax.rsqrt(jnp.mean(oh * oh, axis=-1, keepdims=True) + EPS)
        ohs.append(oh * r)
        rs.append(r)
    sz = _sigmoid(z)
    oh = jnp.concatenate(ohs, axis=1)
    gn4 = jnp.concatenate([gn] * HEADS, axis=1)
    return oh * gn4 * (z * sz), oh, rs, sz, gn4


def _sc_fwd(sc_in, halo, cw, tb):
    xc = jnp.concatenate([halo, sc_in], axis=0)
    u = xc[:, SC_WIDTH:2 * SC_WIDTH] * xc[:, 2 * SC_WIDTH:]
    cv = _taps(u, cw, 3, tb, 6)
    gate_b = sc_in[:, :SC_WIDTH]
    y = gate_b * cv
    gw = SC_WIDTH // SC_GROUPS
    yhs, rs = [], []
    for gi in range(SC_GROUPS):
        yg = y[:, gw * gi:gw * (gi + 1)]
        r = lax.rsqrt(jnp.mean(yg * yg, axis=-1, keepdims=True) + EPS)
        yhs.append(yg * r)
        rs.append(r)
    return u, cv, gate_b, jnp.concatenate(yhs, axis=1), rs


def _shard_rows(land, first, rows):
    assert first % rows == 0 and land.shape[0] == N_CHIPS
    return pl.BlockSpec((N_CHIPS, rows, land.shape[2]), lambda i: (0, first // rows, 0))


def _whole(w_ref):
    n, rows, cols = w_ref.shape
    return w_ref[...].reshape(n * rows, cols)


def _mix_out(o, z, sc_in, x, land_a, gn, scw, gs):
    T = x.shape[0]
    tb = 256

    def body(o_ref, z_ref, sc_ref, halo_ref, x_ref, w_ref, gn_ref, scw_ref, gs_ref, x1_ref, mix_ref):
        o_n = _dn_out(o_ref[...], z_ref[...], gn_ref[...])[0]
        halo = jnp.where(pl.program_id(0) > 0, halo_ref[...], 0.0)
        yh = _sc_fwd(sc_ref[...], halo, scw_ref[...], tb)[3]
        mix = jnp.concatenate([o_n, yh * gs_ref[...]], axis=1).astype(BF16)
        x1_ref[...] = x_ref[...] + jnp.dot(mix, _whole(w_ref), preferred_element_type=F32)
        mix_ref[...] = mix

    tok = lambda w: pl.BlockSpec((tb, w), lambda i: (i, 0))
    full = lambda a: pl.BlockSpec(a.shape, lambda i: (0, 0))
    return pl.pallas_call(
        body, name="mix_out", grid=(T // tb,),
        in_specs=[tok(DN_WIDTH), tok(DN_WIDTH), tok(3 * SC_WIDTH), pl.BlockSpec((8, 3 * SC_WIDTH), _before_halo(tb)),
                  tok(D_MODEL), _shard_rows(land_a, A_OUT_AT, OUT_SHARD), full(gn), full(scw), full(gs)],
        out_specs=[tok(D_MODEL), tok(D_MODEL)],
        out_shape=[jax.ShapeDtypeStruct((T, D_MODEL), F32), jax.ShapeDtypeStruct((T, D_MODEL), BF16)],
        compiler_params=_params(("parallel",)),
    )(o, z, sc_in, sc_in, x, land_a, gn, scw, gs)


def _ffn(x1, g2, land_b):
    T = x1.shape[0]
    tb = 256

    def body(x_ref, g_ref, wgt_ref, wut_ref, wd_ref, x2_ref, a_ref, b_ref, h_ref):
        xv = x_ref[...]
        r = lax.rsqrt(jnp.mean(xv * xv, axis=-1, keepdims=True) + EPS)
        h = (xv * r * g_ref[...]).astype(BF16)
        a = lax.dot_general(h, _whole(wgt_ref), NT, preferred_element_type=F32)
        b = lax.dot_general(h, _whole(wut_ref), NT, preferred_element_type=F32)
        act = (a * _sigmoid(a) * b).astype(BF16)
        x2_ref[...] = xv + jnp.dot(act, _whole(wd_ref), preferred_element_type=F32)
        a_ref[...] = a.astype(BF16)
        b_ref[...] = b.astype(BF16)
        h_ref[...] = h

    tok = lambda w: pl.BlockSpec((tb, w), lambda i: (i, 0))
    return pl.pallas_call(
        body, name="ffn", grid=(T // tb,),
        in_specs=[tok(D_MODEL), pl.BlockSpec(g2.shape, lambda i: (0, 0)), _shard_rows(land_b, 0, FF_SHARD),
                  _shard_rows(land_b, FF_SHARD, FF_SHARD), _shard_rows(land_b, 2 * FF_SHARD, FF_SHARD)],
        out_specs=[tok(D_MODEL), tok(D_FF), tok(D_FF), tok(D_MODEL)],
        out_shape=[jax.ShapeDtypeStruct((T, D_MODEL), F32), jax.ShapeDtypeStruct((T, D_FF), BF16),
                   jax.ShapeDtypeStruct((T, D_FF), BF16), jax.ShapeDtypeStruct((T, D_MODEL), BF16)],
        compiler_params=_params(("parallel",)),
    )(x1, g2, land_b, land_b, land_b)


def _loss_head(x, gf, target):
    T = x.shape[0]
    tb = 512

    def body(x_ref, g_ref, t_ref, dx_ref, dxb_ref, loss_ref, dg_ref):
        @pl.when(pl.program_id(0) == 0)
        def _():
            loss_ref[...] = jnp.zeros_like(loss_ref)
            dg_ref[...] = jnp.zeros_like(dg_ref)

        xv = x_ref[...]
        r = lax.rsqrt(jnp.mean(xv * xv, axis=-1, keepdims=True) + EPS)
        xh = xv * r
        err = xh * g_ref[...] - t_ref[...]
        per_tok = jnp.mean(err * err, axis=-1, keepdims=True)
        loss_ref[...] += 0.5 * jnp.sum(per_tok, axis=0, keepdims=True)
        dy = err * (1.0 / D_MODEL)
        _row_acc(dg_ref, dy * xh)
        dx = _rms_bwd(dy, xh, r, g_ref[...])
        dx_ref[...] = dx
        dxb_ref[...] = dx.astype(BF16)

    tok = pl.BlockSpec((tb, D_MODEL), lambda i: (i, 0))
    return pl.pallas_call(
        body, name="loss_head", grid=(T // tb,),
        in_specs=[tok, pl.BlockSpec(gf.shape, lambda i: (0, 0)), tok],
        out_specs=[tok, tok, pl.BlockSpec((8, LANES), lambda i: (0, 0)), pl.BlockSpec((8, D_MODEL), lambda i: (0, 0))],
        out_shape=[jax.ShapeDtypeStruct((T, D_MODEL), F32), jax.ShapeDtypeStruct((T, D_MODEL), BF16),
                   jax.ShapeDtypeStruct((8, LANES), F32), jax.ShapeDtypeStruct((8, D_MODEL), F32)],
        compiler_params=_params(("arbitrary",)),
    )(x, gf, target)


def _ffn_bwd(dx2, x1, a, b, g2, land_b):
    T = x1.shape[0]
    tb = 256

    def body(dx2_ref, x_ref, a_ref, b_ref, g_ref, wgt_ref, wut_ref, wd_ref,
             dx1_ref, dx1b_ref, da_ref, db_ref, act_ref, dg_ref):
        @pl.when(pl.program_id(0) == 0)
        def _():
            dg_ref[...] = jnp.zeros_like(dg_ref)

        dx2v = dx2_ref[...]
        av = a_ref[...].astype(F32)
        bv = b_ref[...].astype(F32)
        dact = _mm(dx2v, _whole(wd_ref), NT)
        sa = _sigmoid(av)
        silu = av * sa
        da = (dact * bv * (sa * (1.0 + av * (1.0 - sa)))).astype(BF16)
        db = (dact * silu).astype(BF16)
        dh = _mm(da, _whole(wgt_ref)) + _mm(db, _whole(wut_ref))
        xv = x_ref[...]
        r = lax.rsqrt(jnp.mean(xv * xv, axis=-1, keepdims=True) + EPS)
        xh = xv * r
        _row_acc(dg_ref, dh * xh)
        dx1 = dx2v + _rms_bwd(dh, xh, r, g_ref[...])
        dx1_ref[...] = dx1
        dx1b_ref[...] = dx1.astype(BF16)
        da_ref[...] = da
        db_ref[...] = db
        act_ref[...] = (silu * bv).astype(BF16)

    tok = lambda w: pl.BlockSpec((tb, w), lambda i: (i, 0))
    return pl.pallas_call(
        body, name="ffn_bwd", grid=(T // tb,),
        in_specs=[tok(D_MODEL), tok(D_MODEL), tok(D_FF), tok(D_FF), pl.BlockSpec(g2.shape, lambda i: (0, 0)),
                  _shard_rows(land_b, 0, FF_SHARD), _shard_rows(land_b, FF_SHARD, FF_SHARD),
                  _shard_rows(land_b, 2 * FF_SHARD, FF_SHARD)],
        out_specs=[tok(D_MODEL), tok(D_MODEL), tok(D_FF), tok(D_FF), tok(D_FF), pl.BlockSpec((8, D_MODEL), lambda i: (0, 0))],
        out_shape=[jax.ShapeDtypeStruct((T, D_MODEL), F32), jax.ShapeDtypeStruct((T, D_MODEL), BF16)]
        + [jax.ShapeDtypeStruct((T, D_FF), BF16)] * 3 + [jax.ShapeDtypeStruct((8, D_MODEL), F32)],
        compiler_params=_params(("arbitrary",)),
    )(dx2, x1, a, b, g2, land_b, land_b, land_b)


def _wgrad_share(a, b, parts, first, name):
    T = b.shape[0]
    rows = a.shape[1] // N_CHIPS
    assert first % rows == 0 and b.shape[1] == parts.shape[2]
    bk = min(T, 1024)
    n_k = T // bk
    group = 2
    assert (group * rows) % LANES == 0

    def body(a_ref, b_ref, parts_ref, o_ref, acc_ref):
        kk = pl.program_id(1)

        @pl.when(kk == 0)
        def _():
            acc_ref[...] = jnp.zeros_like(acc_ref)

        acc_ref[...] += lax.dot_general(a_ref[...], b_ref[...], TN, preferred_element_type=F32)

        @pl.when(kk == n_k - 1)
        def _():
            for s in range(group):
                o_ref[s] = acc_ref[rows * s:rows * (s + 1), :].astype(BF16)

    return pl.pallas_call(
        body, name=name, grid=(N_CHIPS // group, n_k),
        in_specs=[pl.BlockSpec((bk, group * rows), lambda i, kk: (kk, i)),
                  pl.BlockSpec((bk, b.shape[1]), lambda i, kk: (kk, 0)), _ANY],
        out_specs=pl.BlockSpec((group, rows, b.shape[1]), lambda i, kk: (i, first // rows, 0)),
        out_shape=jax.ShapeDtypeStruct(parts.shape, BF16),
        scratch_shapes=[pltpu.VMEM((group * rows, b.shape[1]), F32)],
        input_output_aliases={2: 0},
        compiler_params=_params(("parallel", "arbitrary")),
    )(a, b, parts)


def _mix_out_bwd(dx1, o, z, sc_in, land_a, gn, scw, gs):
    T = dx1.shape[0]
    tb = 256

    def body(dx_ref, o_ref, z_ref, sc_ref, halo_ref, w_ref, gn_ref, scw_ref, gs_ref,
             do_ref, dz_ref, dgb_ref, dcv_ref, dgn_ref, dgs_ref, dscw_ref):
        @pl.when(pl.program_id(0) == 0)
        def _():
            dgn_ref[...] = jnp.zeros_like(dgn_ref)
            dgs_ref[...] = jnp.zeros_like(dgs_ref)
            dscw_ref[...] = jnp.zeros_like(dscw_ref)

        dmix = _mm(dx_ref[...], _whole(w_ref), NT)
        don = dmix[:, :DN_WIDTH]
        dosc = dmix[:, DN_WIDTH:]
        zv = z_ref[...]
        _, oh, rs, sz, gn4 = _dn_out(o_ref[...], zv, gn_ref[...])
        silu_z = zv * sz
        dgn_full = don * oh * silu_z
        dgn_ref[0:1, :] += jnp.sum(sum(dgn_full[:, HEAD_DIM * hh:HEAD_DIM * (hh + 1)] for hh in range(HEADS)),
                                   axis=0, keepdims=True)
        dz_ref[...] = (don * oh * gn4 * (sz * (1.0 + zv * (1.0 - sz)))).astype(BF16)
        t = don * gn4 * silu_z
        for hh in range(HEADS):
            sl = slice(HEAD_DIM * hh, HEAD_DIM * (hh + 1))
            th, ohh = t[:, sl], oh[:, sl]
            do_ref[:, sl] = rs[hh] * (th - ohh * jnp.mean(th * ohh, axis=-1, keepdims=True))
        halo = jnp.where(pl.program_id(0) > 0, halo_ref[...], 0.0)
        u, cv, gate_b, yh, rys = _sc_fwd(sc_ref[...], halo, scw_ref[...], tb)
        _row_acc(dgs_ref, dosc * yh)
        ty = dosc * gs_ref[...]
        gw = SC_WIDTH // SC_GROUPS
        dys = []
        for gi in range(SC_GROUPS):
            sl = slice(gw * gi, gw * (gi + 1))
            tg, yg = ty[:, sl], yh[:, sl]
            dys.append(rys[gi] * (tg - yg * jnp.mean(tg * yg, axis=-1, keepdims=True)))
        dy = jnp.concatenate(dys, axis=1)
        dgb_ref[...] = dy * cv
        dcv = dy * gate_b
        dcv_ref[...] = dcv
        for j in range(3):
            dscw_ref[j:j + 1, :] += jnp.sum(dcv * _rows_from(u, 6 + j, tb), axis=0, keepdims=True)

    tok = lambda w: pl.BlockSpec((tb, w), lambda i: (i, 0))
    full = lambda t: pl.BlockSpec(t.shape, lambda i: (0, 0))
    acc = lambda w: pl.BlockSpec((8, w), lambda i: (0, 0))
    return pl.pallas_call(
        body, name="mix_out_bwd", grid=(T // tb,),
        in_specs=[tok(D_MODEL), tok(DN_WIDTH), tok(DN_WIDTH), tok(3 * SC_WIDTH),
                  pl.BlockSpec((8, 3 * SC_WIDTH), _before_halo(tb)), _shard_rows(land_a, A_OUT_AT, OUT_SHARD),
                  full(gn), full(scw), full(gs)],
        out_specs=[tok(DN_WIDTH), tok(DN_WIDTH), tok(SC_WIDTH), tok(SC_WIDTH), acc(HEAD_DIM), acc(SC_WIDTH), acc(SC_WIDTH)],
        out_shape=[jax.ShapeDtypeStruct((T, DN_WIDTH), F32), jax.ShapeDtypeStruct((T, DN_WIDTH), BF16),
                   jax.ShapeDtypeStruct((T, SC_WIDTH), F32), jax.ShapeDtypeStruct((T, SC_WIDTH), F32),
                   jax.ShapeDtypeStruct((8, HEAD_DIM), F32), jax.ShapeDtypeStruct((8, SC_WIDTH), F32),
                   jax.ShapeDtypeStruct((8, SC_WIDTH), F32)],
        compiler_params=_params(("arbitrary",)),
    )(dx1, o, z, sc_in, sc_in, land_a, gn, scw, gs)


def _delta_bwd(q, k, v, bg, states, do):
    T = q.shape[0]
    tb = 512
    n_chunk = tb // CHUNK
    nb = T // tb

    def body(q_ref, k_ref, v_ref, bg_ref, st_ref, do_ref, dq_ref, dk_ref, dv_ref, dbg_ref, ds_ref):
        @pl.when(pl.program_id(0) == 0)
        def _():
            ds_ref[...] = jnp.zeros_like(ds_ref)

        masks = _chunk_masks()
        causal, strict = masks
        lane = lax.broadcasted_iota(jnp.int32, (CHUNK, LANES), 1)
        last_row = lax.broadcasted_iota(jnp.int32, (CHUNK, 1), 0) == CHUNK - 1
        cat = jnp.concatenate
        heads = range(HEADS)

        def open_chunk(ci, loc):
            rows = pl.ds(pl.multiple_of(ci * CHUNK, CHUNK), CHUNK)
            dov = do_ref[rows, :]
            return dict(rows=rows, loc=loc, do=[dov[:, HEAD_DIM * h:HEAD_DIM * (h + 1)] for h in heads],
                        state=[st_ref[ci, h] for h in heads])

        def a_free(c):
            loc, do, state = c["loc"], c["do"], c["state"]
            w_s = [_mm(p["w"], s) for p, s in zip(loc, state)]
            c["dq_dec"] = [_mm(d, s, NT) for d, s in zip(do, state)]
            c["qk_do"] = [_mm(p["qk"], d, TN) for p, d in zip(loc, do)]
            c["qd_do"] = [_mm(p["q_dec"], d, TN) for p, d in zip(loc, do)]
            c["v_new"] = [p["u"] - t for p, t in zip(loc, w_s)]
            c["dqk"] = [jnp.where(causal, _mm(d, vn, NT), 0.0) for d, vn in zip(do, c["v_new"])]

        def a_state(c, ds_next):
            c["ds_next"] = ds_next
            kd_ds = [_mm(p["k_dec"], d) for p, d in zip(c["loc"], ds_next)]
            c["dk_dec"] = [_mm(vn, d, NT) for vn, d in zip(c["v_new"], ds_next)]
            c["dv_new"] = [a + b for a, b in zip(c["qk_do"], kd_ds)]

        def b_state(c):
            loc = c["loc"]
            w_dv = [_mm(p["w"], dvn, TN) for p, dvn in zip(loc, c["dv_new"])]
            c["dw"] = [-_mm(dvn, s, NT) for dvn, s in zip(c["dv_new"], c["state"])]
            return [loc[h]["gl"] * c["ds_next"][h] + c["qd_do"][h] - w_dv[h] for h in heads]

        def c_solve(c):
            loc, dv_new, dw = c["loc"], c["dv_new"], c["dw"]
            c["dtm"] = [_mm(cat([dvn, d], axis=1), cat([p["vb"], p["kbg"]], axis=1), NT) for dvn, d, p in zip(dv_new, dw, loc)]
            x_t = [_mm(p["xm"], cat([dvn, d], axis=1), TN) for p, dvn, d in zip(loc, dv_new, dw)]
            c["dvb"] = [dvn + t[:, :HEAD_DIM] for dvn, t in zip(dv_new, x_t)]
            c["dkbg"] = [d + t[:, HEAD_DIM:] for d, t in zip(dw, x_t)]

        def d_solve(c):
            c["y"] = [t + _mm(p["xm"], t, TN) for p, t in zip(c["loc"], c["dtm"])]

        def e_solve(c):
            c["dlow"] = [jnp.where(strict, -(t + _mm(t, p["xm"], NT)), 0.0) for p, t in zip(c["loc"], c["y"])]

        def f_close(c):
            loc, rows = c["loc"], c["rows"]
            dmm = [d * p["decay"] for d, p in zip(c["dlow"], loc)]
            dnn = [d * p["decay"] for d, p in zip(c["dqk"], loc)]
            by_k = [_mm(cat([a, b], axis=0), p["k"]) for a, b, p in zip(dmm, dnn, loc)]
            dk_mm = [_mm(cat([a, b], axis=0), cat([p["kb"], p["q"]], axis=0), TN) for a, b, p in zip(dmm, dnn, loc)]
            dq_out, dk_out, dv_out = [], [], []
            dbeta_all = jnp.zeros((CHUNK, LANES), F32)
            dgc_all = jnp.zeros((CHUNK, LANES), F32)
            for h in heads:
                p = loc[h]
                dkb = by_k[h][:CHUNK] + c["dkbg"][h] * p["eg"]
                dq_out.append(by_k[h][CHUNK:] + c["dq_dec"][h] * p["eg"])
                dk_out.append(dk_mm[h] + c["dk_dec"][h] * p["ek"] + dkb * p["beta"])
                dv_out.append(c["dvb"][h] * p["beta"])
                dbeta = jnp.sum(dkb * p["k"] + c["dvb"][h] * p["v"], axis=1, keepdims=True)
                e = c["dlow"][h] * p["low"] + c["dqk"][h] * p["qk"]
                kd = jnp.sum(c["dk_dec"][h] * p["k_dec"], axis=1, keepdims=True)
                dgc = (jnp.sum(e, axis=1, keepdims=True) - jnp.sum(e.T, axis=1, keepdims=True)
                       + jnp.sum(c["dq_dec"][h] * p["q_dec"], axis=1, keepdims=True) - kd
                       + jnp.sum(c["dkbg"][h] * p["kbg"], axis=1, keepdims=True))
                dgl = jnp.sum(jnp.sum(c["ds_next"][h] * c["state"][h], axis=1, keepdims=True), axis=0, keepdims=True)
                d_last = jnp.sum(kd, axis=0, keepdims=True) + dgl * p["gl"]
                dgc = dgc + jnp.where(last_row, d_last, 0.0)
                dbeta_all = jnp.where(lane == h, dbeta, dbeta_all)
                dgc_all = jnp.where(lane == h + HEADS, dgc, dgc_all)
            dq_ref[rows, :] = cat(dq_out, axis=1)
            dk_ref[rows, :] = cat(dk_out, axis=1)
            dv_ref[rows, :] = cat(dv_out, axis=1)
            dbg_ref[rows, :] = dbeta_all + dgc_all

        def group(gj, carry):
            first = n_chunk - 1 - BWD_GROUP * gj
            ids = [first - j for j in range(BWD_GROUP)]
            rows = [pl.ds(pl.multiple_of(ci * CHUNK, CHUNK), CHUNK) for ci in ids]
            loc = _units_local(sum((_chunk_units(q_ref, k_ref, v_ref, bg_ref, r) for r in rows), []), masks)
            chunks = [open_chunk(ci, loc[HEADS * j:HEADS * (j + 1)]) for j, ci in enumerate(ids)]
            for c in chunks:
                a_free(c)
            ds_cur = [ds_ref[h] for h in heads]
            later = (c_solve, d_solve, e_solve, f_close)
            for t in range(2 * (BWD_GROUP - 1) + 2 + len(later)):
                for j, c in enumerate(chunks):
                    stage = t - 2 * j
                    if stage == 0:
                        a_state(c, ds_cur)
                    elif stage == 1:
                        ds_cur = b_state(c)
                    elif 2 <= stage < 2 + len(later):
                        later[stage - 2](c)
            for h in heads:
                ds_ref[h] = ds_cur[h]
            return carry

        lax.fori_loop(0, n_chunk // BWD_GROUP, group, 0)

    tok = lambda w: pl.BlockSpec((tb, w), lambda i: (nb - 1 - i, 0))
    return pl.pallas_call(
        body, name="delta_bwd", grid=(nb,),
        in_specs=[tok(DN_WIDTH), tok(DN_WIDTH), tok(DN_WIDTH), tok(LANES),
                  pl.BlockSpec((n_chunk, HEADS, HEAD_DIM, HEAD_DIM), lambda i: (nb - 1 - i, 0, 0, 0)), tok(DN_WIDTH)],
        out_specs=[tok(DN_WIDTH), tok(DN_WIDTH), tok(DN_WIDTH), tok(LANES)],
        out_shape=[jax.ShapeDtypeStruct((T, DN_WIDTH), F32)] * 3 + [jax.ShapeDtypeStruct((T, LANES), F32)],
        scratch_shapes=[pltpu.VMEM((HEADS, HEAD_DIM, HEAD_DIM), F32)],
        compiler_params=_params(("arbitrary",)),
    )(q, k, v, bg, states, do)


def _dn_prep_back(dq, dk, dv, dbg, pre, halo, cw, bd, al_row, dt_row, tb):
    xc, c, sg, a = _dn_act(pre, halo, cw, tb)
    dsilu = sg * (1.0 + c * (1.0 - sg))
    pieces = [None] * (2 * HEADS)
    for hd in range(HEADS):
        sl = slice(HEAD_DIM * hd, HEAD_DIM * (hd + 1))
        for which, (base, grad, scale) in enumerate(((0, dq, Q_SCALE), (DN_WIDTH, dk, 1.0))):
            sa = slice(base + HEAD_DIM * hd, base + HEAD_DIM * (hd + 1))
            raw = a[:, sa]
            r = lax.rsqrt(jnp.sum(raw * raw, axis=-1, keepdims=True) + EPS)
            nrm = raw * r
            gn_ = grad[:, sl] * scale
            pieces[which * HEADS + hd] = r * (gn_ - nrm * jnp.sum(gn_ * nrm, axis=-1, keepdims=True)) * dsilu[:, sa]
    dc = jnp.concatenate(pieces + [dv * dsilu[:, 2 * DN_WIDTH:]], axis=1)
    dcw_rows = [jnp.sum(dc * _rows_from(xc, 5 + j, tb), axis=0, keepdims=True) for j in range(4)]
    lane = lax.broadcasted_iota(jnp.int32, bd.shape, 1)
    is_b = lane < HEADS
    is_g = jnp.logical_and(lane >= HEADS, lane < 2 * HEADS)
    dbgv = jnp.where(is_b, dbg, _mm32(_chunk_cumsum_matrix(tb), dbg, TN))
    beta = _sigmoid(bd)
    neg_a = -jnp.exp(al_row)
    pre_sp = bd + dt_row
    g = neg_a * _softplus(pre_sp)
    da_in = dbgv * neg_a * _sigmoid(pre_sp)
    dbd = jnp.where(is_b, dbgv * beta * (1.0 - beta), jnp.where(is_g, da_in, 0.0)).astype(BF16)
    dal_row = jnp.sum(jnp.where(is_g, dbgv * g, 0.0), axis=0, keepdims=True)
    ddt_row = jnp.sum(jnp.where(is_g, da_in, 0.0), axis=0, keepdims=True)
    return dc, dbd, dcw_rows, dal_row, ddt_row


def _dp_of_chip(dqkv, dz, dbd, dsc, s):
    lo, hi = IN_SHARD * s, IN_SHARD * (s + 1)
    pieces = []
    for w_at, w_end, block in ((0, W_Z, dqkv), (W_Z, W_BD, dz), (W_BD, W_SC, dbd), (W_SC, W_IN_COLS, dsc)):
        a, b = max(lo, w_at), min(hi, w_end)
        if a < b:
            pieces.append(block[:, a - w_at:b - w_at])
    pieces.append(jnp.zeros((dqkv.shape[0], D_MODEL - IN_SHARD), dqkv.dtype))
    return jnp.concatenate(pieces, axis=1)


def _in_proj_bwd(dq, dk, dv, dbg, qkv, bd, al_row, dt_row, dcv, dgb, sc_in, dz, cw, scw, dx1, x, g1, land_a):
    T = x.shape[0]
    tb = 256
    nb = T // tb

    def body(dq_ref, dk_ref, dv_ref, dbg_ref, pre_ref, pre_halo_ref, bd_ref, al_ref, dt_ref,
             dcv_ref, dcv_halo_ref, dgb_ref, sc_ref, dz_ref, cw_ref, scw_ref, dx1_ref, x_ref, g_ref, w_ref,
             dx_ref, dxb_ref, dps_ref, dg_ref, dcw_ref, dal_ref, ddt_ref, head_ref):
        @pl.when(pl.program_id(0) == 0)
        def _():
            for ref in (dg_ref, dcw_ref, dal_ref, ddt_ref, head_ref):
                ref[...] = jnp.zeros_like(ref)

        block = nb - 1 - pl.program_id(0)
        last = block == nb - 1
        pre_halo = jnp.where(block > 0, pre_halo_ref[...], 0.0)
        dc, dbd, dcw_rows, dal_row, ddt_row = _dn_prep_back(
            dq_ref[...], dk_ref[...], dv_ref[...], dbg_ref[...], pre_ref[...], pre_halo, cw_ref[...], bd_ref[...],
            al_ref[...], dt_ref[...], tb)
        for j in range(4):
            dcw_ref[j:j + 1, :] += dcw_rows[j]
        dal_ref[0:1, :] += dal_row
        ddt_ref[0:1, :] += ddt_row
        xc = jnp.concatenate([dc, head_ref[...]], axis=0)
        head_ref[...] = dc[0:8, :]
        w4 = cw_ref[...]
        dqkv = w4[3:4, :] * xc[0:tb, :]
        for j in range(3):
            dqkv = dqkv + w4[j:j + 1, :] * _rows_from(xc, 3 - j, tb)
        yc = jnp.concatenate([dcv_ref[...], jnp.where(last, 0.0, dcv_halo_ref[...])], axis=0)
        w3 = scw_ref[...]
        du = w3[2:3, :] * yc[0:tb, :] + w3[1:2, :] * _rows_from(yc, 1, tb) + w3[0:1, :] * _rows_from(yc, 2, tb)
        sc = sc_ref[...]
        dsc = jnp.concatenate([dgb_ref[...], du * sc[:, 2 * SC_WIDTH:], du * sc[:, SC_WIDTH:2 * SC_WIDTH]], axis=1)
        blocks = (dqkv.astype(BF16), dz_ref[...], dbd, dsc.astype(BF16))
        dh = jnp.zeros((tb, D_MODEL), F32)
        for s in range(N_CHIPS):
            dps = _dp_of_chip(*blocks, s)
            dps_ref[:, D_MODEL * s:D_MODEL * (s + 1)] = dps
            dh = dh + lax.dot_general(dps, w_ref[s], NT, preferred_element_type=F32)
        xv = x_ref[...]
        r = lax.rsqrt(jnp.mean(xv * xv, axis=-1, keepdims=True) + EPS)
        xh = xv * r
        _row_acc(dg_ref, dh * xh)
        dx = dx1_ref[...] + _rms_bwd(dh, xh, r, g_ref[...])
        dx_ref[...] = dx
        dxb_ref[...] = dx.astype(BF16)

    tok = lambda w: pl.BlockSpec((tb, w), lambda i: (nb - 1 - i, 0))
    full = lambda t: pl.BlockSpec(t.shape, lambda i: (0, 0))
    acc = lambda w: pl.BlockSpec((8, w), lambda i: (0, 0))
    before = lambda w: pl.BlockSpec((8, w), lambda i: _before_halo(tb)(nb - 1 - i))
    after = lambda w: pl.BlockSpec((8, w), lambda i: _after_halo(tb, T)(nb - 1 - i))
    return pl.pallas_call(
        body, name="in_proj_bwd", grid=(nb,),
        in_specs=[tok(DN_WIDTH), tok(DN_WIDTH), tok(DN_WIDTH), tok(LANES), tok(QKV), before(QKV), tok(LANES),
                  full(al_row), full(dt_row), tok(SC_WIDTH), after(SC_WIDTH), tok(SC_WIDTH), tok(3 * SC_WIDTH),
                  tok(DN_WIDTH), full(cw), full(scw), tok(D_MODEL), tok(D_MODEL), full(g1), _shard_rows(land_a, 0, D_MODEL)],
        out_specs=[tok(D_MODEL), tok(D_MODEL), tok(N_CHIPS * D_MODEL), acc(D_MODEL), acc(QKV), acc(LANES), acc(LANES)],
        out_shape=[jax.ShapeDtypeStruct((T, D_MODEL), F32), jax.ShapeDtypeStruct((T, D_MODEL), BF16),
                   jax.ShapeDtypeStruct((T, N_CHIPS * D_MODEL), BF16), jax.ShapeDtypeStruct((8, D_MODEL), F32),
                   jax.ShapeDtypeStruct((8, QKV), F32), jax.ShapeDtypeStruct((8, LANES), F32),
                   jax.ShapeDtypeStruct((8, LANES), F32)],
        scratch_shapes=[pltpu.VMEM((8, QKV), F32)],
        compiler_params=_params(("arbitrary",)),
    )(dq, dk, dv, dbg, qkv, qkv, bd, al_row, dt_row, dcv, dcv, dgb, sc_in, dz, cw, scw, dx1, x, g1, land_a)


def _wgrad_in_share(h, dps, parts, name):
    T = h.shape[0]
    bk = min(T, 1024)
    n_k = T // bk

    def body(a_ref, b_ref, parts_ref, o_ref, acc_ref):
        kk = pl.program_id(1)

        @pl.when(kk == 0)
        def _():
            acc_ref[...] = jnp.zeros_like(acc_ref)

        acc_ref[...] += lax.dot_general(a_ref[...], b_ref[...], TN, preferred_element_type=F32)

        @pl.when(kk == n_k - 1)
        def _():
            o_ref[0] = acc_ref[...].astype(BF16)

    return pl.pallas_call(
        body, name=name, grid=(N_CHIPS, n_k),
        in_specs=[pl.BlockSpec((bk, D_MODEL), lambda j, kk: (kk, 0)), pl.BlockSpec((bk, D_MODEL), lambda j, kk: (kk, j)), _ANY],
        out_specs=pl.BlockSpec((1, D_MODEL, D_MODEL), lambda j, kk: (j, 0, 0)),
        out_shape=jax.ShapeDtypeStruct(parts.shape, BF16),
        scratch_shapes=[pltpu.VMEM((D_MODEL, D_MODEL), F32)],
        input_output_aliases={2: 0},
        compiler_params=_params(("parallel", "arbitrary")),
    )(h, dps, parts)


def _pad_rows(a, rows=8):
    return jnp.pad(a, ((0, rows - a.shape[0]), (0, 0)))


def _gate_rows(a_log, dt_bias):
    put = lambda t: jnp.pad(t.reshape(1, HEADS), ((0, 0), (HEADS, LANES - 2 * HEADS)))
    return put(a_log), put(dt_bias)


def _mixer_fwd(x, p):
    qkv, z, sc_in, bd, h, q, k, v, bg = _in_proj(x, p["g1"], p["land_a"], p["cw"], p["al"], p["dt"])
    o, states = _delta_fwd(q, k, v, bg)
    x1, mix = _mix_out(o, z, sc_in, x, p["land_a"], p["gn"], p["scw"], p["gs"])
    return x1, dict(x=x, qkv=qkv, z=z, sc_in=sc_in, bd=bd, h=h, q=q, k=k, v=v, bg=bg, o=o, states=states, mix=mix)


def _ffn_fwd(x1, p, land_b):
    x2, a, b, h2 = _ffn(x1, p["g2"], land_b)
    return x2, dict(x1=x1, a=a, b=b, h2=h2)


def _ffn_back(dx2, dx2_bf16, s, p, land_b):
    dx1, dx1_bf16, da, db, act, dg2 = _ffn_bwd(dx2, s["x1"], s["a"], s["b"], p["g2"], land_b)
    parts = lax.empty((N_CHIPS, B_ROWS, D_MODEL), BF16)
    parts = _wgrad_share(act, dx2_bf16, parts, 2 * FF_SHARD, "wgrad_down")
    parts = _wgrad_share(da, s["h2"], parts, 0, "wgrad_gate")
    parts = _wgrad_share(db, s["h2"], parts, FF_SHARD, "wgrad_up")
    return dx1, dx1_bf16, parts, dg2[0]


def _mixer_bwd(dx1, dx1_bf16, s, p):
    do, dz, dgb, dcv, dgn, dgs, dscw = _mix_out_bwd(dx1, s["o"], s["z"], s["sc_in"], p["land_a"], p["gn"], p["scw"], p["gs"])
    dq, dk, dv, dbg = _delta_bwd(s["q"], s["k"], s["v"], s["bg"], s["states"], do)
    dx, dx_bf16, dps, dg1, dcw, dal, ddt = _in_proj_bwd(
        dq, dk, dv, dbg, s["qkv"], s["bd"], p["al"], p["dt"], dcv, dgb, s["sc_in"], dz, p["cw"], p["scw"], dx1, s["x"],
        p["g1"], p["land_a"])
    parts = lax.empty((N_CHIPS, A_ROWS, D_MODEL), BF16)
    parts = _wgrad_in_share(s["h"], dps, parts, "wgrad_in")
    parts = _wgrad_share(s["mix"], dx1_bf16, parts, A_OUT_AT, "wgrad_out")
    g = dict(g1=dg1[0], gn=dgn[0], gs=dgs[0], scw=dscw[:3], cw=dcw[:4], al=dal[0, HEADS:2 * HEADS], dt=ddt[0, HEADS:2 * HEADS])
    return dx, dx_bf16, parts, g


def _place():
    return lax.axis_index("x"), lax.axis_index("y"), lax.axis_index("c")


def _other_chips(x, y):
    return [(1 - x, y), (x, 1 - y), (1 - x, 1 - y)]


_HBM = pl.BlockSpec(memory_space=pltpu.HBM)


def _chip_exchange(arrs, name, gather):
    n = len(arrs)

    def body(*refs):
        ins, outs = refs[:n], refs[n:2 * n]
        send_sems, recv_sems, local_sems = refs[2 * n:]
        x, y, c = _place()
        me = 2 * x + y
        others = _other_chips(x, y)

        def remote(k, j, landing):
            px, py = others[j]
            src = ins[k] if gather else ins[k].at[2 * px + py]
            return pltpu.make_async_remote_copy(src_ref=src, dst_ref=outs[k].at[landing], send_sem=send_sems.at[k, j],
                                                recv_sem=recv_sems.at[k, j], device_id=(px, py, c), device_id_type=MESH)

        local = [pltpu.make_async_copy(ins[k] if gather else ins[k].at[me], outs[k].at[me], local_sems.at[k])
                 for k in range(n)]
        sends = [remote(k, j, me) for k in range(n) for j in range(3)]
        for cp in local + sends:
            cp.start()
        for k in range(n):
            for j, (px, py) in enumerate(others):
                remote(k, j, 2 * px + py).wait_recv()
        for cp in sends:
            cp.wait_send()
        for cp in local:
            cp.wait()

    shapes = [jax.ShapeDtypeStruct(((N_CHIPS,) + a.shape) if gather else a.shape, a.dtype) for a in arrs]
    return pl.pallas_call(
        body, name=name, in_specs=[_HBM] * n, out_specs=[_HBM] * n, out_shape=shapes,
        scratch_shapes=[pltpu.SemaphoreType.DMA((n, 3)), pltpu.SemaphoreType.DMA((n, 3)), pltpu.SemaphoreType.DMA((n,))],
    )(*arrs)


_SEM = pl.BlockSpec(memory_space=pltpu.SEMAPHORE)
_ANY = pl.BlockSpec(memory_space=pl.ANY)
_EFFECT = pltpu.SideEffectType.DATAFLOW_SIDE_EFFECTING


_FLIPS = [(a, b, cc) for a in (0, 1) for b in (0, 1) for cc in (0, 1)][1:]


def _split_copies(src_ref, land_ref, send_sems, recv_sems, gather, sending):
    x, y, c = _place()
    copies = []
    if gather:
        me = 2 * x + y
        for j, (px, py) in enumerate(_other_chips(x, y)):
            copies.append(pltpu.make_async_remote_copy(
                src_ref=src_ref, dst_ref=land_ref.at[me if sending else 2 * px + py],
                send_sem=send_sems.at[j], recv_sem=recv_sems.at[j], device_id=(px, py, c), device_id_type=MESH))
        return copies
    me = 4 * x + 2 * y + c
    for j, (a, b, cc) in enumerate(_FLIPS):
        px, py, pc = (1 - x) if a else x, (1 - y) if b else y, (1 - c) if cc else c
        copies.append(pltpu.make_async_remote_copy(
            src_ref=src_ref.at[2 * px + py], dst_ref=land_ref.at[me if sending else 4 * px + 2 * py + pc],
            send_sem=send_sems.at[j], recv_sem=recv_sems.at[j], device_id=(px, py, pc), device_id_type=MESH))
    return copies


def _own_slot(share):
    chip = 2 * lax.axis_index("x") + lax.axis_index("y")
    return lax.dynamic_update_slice(lax.empty((N_CHIPS,) + share.shape, share.dtype), share[None], (chip, 0, 0))


def _own_part(parts):
    chip = 2 * lax.axis_index("x") + lax.axis_index("y")
    own = lax.dynamic_index_in_dim(parts, chip, 0, keepdims=True)
    return lax.dynamic_update_slice(lax.empty((N_DEV,) + parts.shape[1:], parts.dtype), own,
                                    (2 * chip + lax.axis_index("c"), 0, 0))


def _exchange_start(src, land, after, name, gather):
    def body(src_ref, land_ref, after_ref, send_sems, recv_sems, src_thru, land_thru, token):
        for cp in _split_copies(src_ref, land_ref, send_sems, recv_sems, gather, sending=True):
            cp.start()
        token[...] = jnp.zeros_like(token)

    hbm = lambda t: pltpu.with_memory_space_constraint(t, pltpu.HBM)
    n_copies = N_CHIPS - 1 if gather else N_DEV - 1
    return pl.pallas_call(
        body, name=name,
        out_shape=(pltpu.SemaphoreType.DMA((n_copies,)), pltpu.SemaphoreType.DMA((n_copies,)), pltpu.HBM(src.shape, src.dtype),
                   pltpu.HBM(land.shape, land.dtype), jax.ShapeDtypeStruct((8, LANES), F32)),
        in_specs=(_HBM, _HBM, _ANY), out_specs=(_SEM, _SEM, _HBM, _HBM, pl.BlockSpec(memory_space=pltpu.VMEM)),
        input_output_aliases={0: 2, 1: 3},
        compiler_params=pltpu.CompilerParams(has_side_effects=_EFFECT),
    )(hbm(src), hbm(land), after)


def _exchange_wait(started, after, name, gather):
    send_sems, recv_sems, src_thru, land_thru, _ = started

    def body(src_ref, land_ref, send_sems, recv_sems, after_ref, src_dead, got_ref):
        for cp in _split_copies(src_ref, land_ref, send_sems, recv_sems, gather, sending=False):
            cp.wait_send()
            cp.wait_recv()

    return pl.pallas_call(
        body, name=name,
        out_shape=(pltpu.HBM(src_thru.shape, src_thru.dtype), pltpu.HBM(land_thru.shape, land_thru.dtype)),
        in_specs=(_HBM, _HBM, _SEM, _SEM, _ANY), out_specs=(_HBM, _HBM), input_output_aliases={0: 0, 1: 1},
        compiler_params=pltpu.CompilerParams(has_side_effects=_EFFECT),
    )(src_thru, land_thru, send_sems, recv_sems, after)[1]


def _all_reduce_small(v):
    rows = v.shape[0]
    flips = [(a, b, cc) for a in (0, 1) for b in (0, 1) for cc in (0, 1)][1:]

    def body(v_ref, out_ref, buf_ref, send_sems, recv_sems):
        x, y, c = _place()
        me = 4 * x + 2 * y + c
        peers = [((1 - x) if a else x, (1 - y) if b else y, (1 - c) if cc else c) for a, b, cc in flips]

        def copy(j, landing):
            return pltpu.make_async_remote_copy(src_ref=v_ref, dst_ref=buf_ref.at[landing], send_sem=send_sems.at[j],
                                                recv_sem=recv_sems.at[j], device_id=peers[j], device_id_type=MESH)

        sends = [copy(j, me) for j in range(N_DEV - 1)]
        for cp in sends:
            cp.start()
        buf_ref[me] = v_ref[...]
        for j, (px, py, pc) in enumerate(peers):
            copy(j, 4 * px + 2 * py + pc).wait_recv()
        for cp in sends:
            cp.wait_send()
        acc = buf_ref[0]
        for d in range(1, N_DEV):
            acc = acc + buf_ref[d]
        out_ref[...] = acc

    vmem = pl.BlockSpec(memory_space=pltpu.VMEM)
    return pl.pallas_call(
        body, name="all_reduce_small", in_specs=[vmem], out_specs=vmem,
        out_shape=jax.ShapeDtypeStruct(v.shape, F32),
        scratch_shapes=[pltpu.VMEM((N_DEV, rows, LANES), F32), pltpu.SemaphoreType.DMA((N_DEV - 1,)),
                        pltpu.SemaphoreType.DMA((N_DEV - 1,))],
    )(v)


def _row_block(*sizes):
    return next(t for t in (176, 128, 64) if all(s % t == 0 for s in sizes))


def _adam_update(w, m, v, g):
    r1 = 1.0 / (1.0 - ADAM_B1 ** ADAM_STEP)
    r2 = 1.0 / (1.0 - ADAM_B2 ** ADAM_STEP)
    m_new = ADAM_B1 * m + (1.0 - ADAM_B1) * g
    v_new = ADAM_B2 * v + (1.0 - ADAM_B2) * (g * g)
    return -ADAM_LR * ((m_new * r1) / (jnp.sqrt(v_new * r2) + ADAM_EPS) + ADAM_WD * w), m_new, v_new


def _adamw_rows(w, m, v, got, first, name):
    n_layers, rows, cols = w.shape
    tr = _row_block(rows, first)

    def body(*refs):
        w_ref, m_ref, v_ref = refs[:3]
        g_out, d_out, m_out, v_out = refs[3 + n_layers:]
        for k in range(n_layers):
            @pl.when(pl.program_id(0) == k)
            def _(p_ref=refs[3 + k]):
                g = p_ref[0].astype(F32)
                for d in range(1, N_DEV):
                    g = g + p_ref[d].astype(F32)
                g = g[:, :cols]
                d_out[0], m_out[0], v_out[0] = _adam_update(w_ref[0], m_ref[0], v_ref[0], g)
                g_out[0] = g

    blk = pl.BlockSpec((1, tr, cols), lambda l, i: (l, i, 0))
    parts = [pl.BlockSpec((N_DEV, tr, got[0].shape[2]), lambda l, i, k=k: (0, jnp.where(l == k, first // tr + i, 0), 0))
             for k in range(n_layers)]
    return pl.pallas_call(
        body, name=name, grid=(n_layers, rows // tr),
        in_specs=[blk] * 3 + parts, out_specs=[blk] * 4,
        out_shape=[jax.ShapeDtypeStruct(w.shape, F32)] * 4,
        compiler_params=_params(("arbitrary", "arbitrary")),
    )(w, m, v, *got)


def _adamw(w, m, v, g_parts, name):
    rows, cols = w.shape
    tr = min(rows, 256)
    n = len(g_parts)

    def body(*refs):
        w_ref, m_ref, v_ref = refs[:3]
        g_refs = refs[3:3 + n]
        g_out, d_out, m_out, v_out = refs[3 + n:]
        g = g_refs[0][...]
        for r in g_refs[1:]:
            g = g + r[...]
        d_out[...], m_out[...], v_out[...] = _adam_update(w_ref[...], m_ref[...], v_ref[...], g)
        g_out[...] = g

    blk = pl.BlockSpec((tr, cols), lambda i: (i, 0))
    return pl.pallas_call(
        body, name=name, grid=(rows // tr,),
        in_specs=[blk] * (3 + n), out_specs=[blk] * 4,
        out_shape=[jax.ShapeDtypeStruct((rows, cols), F32)] * 4,
        compiler_params=_params(("parallel",)),
    )(w, m, v, *g_parts)


def _pack(parts, rows, fill=0.0):
    flat = jnp.concatenate([p.reshape(-1) for p in parts])
    return jnp.pad(flat, (0, rows * LANES - flat.shape[0]), constant_values=fill).reshape(rows, LANES)


def _unpack(packed, shapes):
    flat = packed.reshape(-1)
    out, at = [], 0
    for shp in shapes:
        size = 1
        for s in shp:
            size *= s
        out.append(flat[at:at + size].reshape(shp))
        at += size
    return out


def _packed_rows(shapes):
    total = 0
    for shp in shapes:
        size = 1
        for s in shp:
            size *= s
        total += size
    return -(-total // (8 * LANES)) * 8


def _cols_full(g, l):
    t = g[:, l]
    return jnp.moveaxis(t, 0, 1).reshape(t.shape[1], N_CHIPS * t.shape[2])


def _pad_cols(t):
    return jnp.pad(t, ((0, 0),) * (t.ndim - 1) + ((0, D_MODEL - t.shape[-1]),))


def kernel(x, norm1_g, w_in, dn_conv_w, dn_a_log, dn_dt_bias, dn_norm_g, sc_conv_w, sc_norm_g, w_out, norm2_g, ffn_w_gate, ffn_w_up, ffn_w_down, final_norm_g, loss_target, m_norm1_g, m_w_in, m_dn_conv_w, m_dn_a_log, m_dn_dt_bias, m_dn_norm_g, m_sc_conv_w, m_sc_norm_g, m_w_out, m_norm2_g, m_ffn_w_gate, m_ffn_w_up, m_ffn_w_down, m_final_norm_g, v_norm1_g, v_w_in, v_dn_conv_w, v_dn_a_log, v_dn_dt_bias, v_dn_norm_g, v_sc_conv_w, v_sc_norm_g, v_w_out, v_norm2_g, v_ffn_w_gate, v_ffn_w_up, v_ffn_w_down, v_final_norm_g):
    chip = 2 * lax.axis_index("x") + lax.axis_index("y")

    g_cw, g_scw = _chip_exchange([dn_conv_w, sc_conv_w], "gather_conv", gather=True)

    t_last = lambda t: jnp.swapaxes(t, -1, -2)
    gate_t, up_t = t_last(ffn_w_gate), t_last(ffn_w_up)
    zero_token = jnp.zeros((8, LANES), F32)

    def shares(l, tie):
        share_a = jnp.concatenate([_pad_cols(w_in[l] + tie), w_out[l]], axis=0).astype(BF16)
        share_b = jnp.concatenate([gate_t[l] + tie, up_t[l], ffn_w_down[l]], axis=0).astype(BF16)
        return share_a, _own_slot(share_a), share_b, _own_slot(share_b)

    def gather_start(l, packed, after):
        a = _exchange_start(packed[0], packed[1], after, "gather_a_start_%d" % l, gather=True)
        b = _exchange_start(packed[2], packed[3], a[4], "gather_b_start_%d" % l, gather=True)
        return a, b

    ga, gb = gather_start(0, shares(0, 0.0), g_cw)
    packed = [None] + [shares(l, gb[4][0, 0]) for l in range(1, DEPTH)]
    packed_all = sum(t[0, 0].astype(F32) for p in packed[1:] for t in (p[0], p[2]))
    land_a = _exchange_wait(ga, zero_token + packed_all, "gather_a_wait_0", gather=True)
    act = x[0]
    layers, saved_m, saved_f, lands_b = [], [], [], []
    for l in range(DEPTH):
        hold = 0.0
        if l + 1 < DEPTH:
            ga, gb_next = gather_start(l + 1, packed[l + 1], land_a)
            hold = gb_next[4][0:1, 0:1]
        al, dt = _gate_rows(dn_a_log[l], dn_dt_bias[l])
        layers.append(dict(
            g1=norm1_g[l][None] + hold, cw=_pad_rows(_cols_full(g_cw, l)), al=al, dt=dt,
            gn=dn_norm_g[l][None], scw=_pad_rows(_cols_full(g_scw, l)), gs=sc_norm_g[l][None],
            land_a=land_a, g2=norm2_g[l][None]))
        x1, s = _mixer_fwd(act, layers[l])
        saved_m.append(s)
        lands_b.append(_exchange_wait(gb, x1, "gather_b_wait_%d" % l, gather=True))
        act, s = _ffn_fwd(x1, layers[l], lands_b[l])
        saved_f.append(s)
        if l + 1 < DEPTH:
            land_a = _exchange_wait(ga, act, "gather_a_wait_%d" % (l + 1), gather=True)
            gb = gb_next

    dact, dact_bf16, loss_part, d_final = _loss_head(act, final_norm_g[None], loss_target[0])
    grads, reduce_a, reduce_b = [None] * DEPTH, [None] * DEPTH, [None] * DEPTH
    hold = 0.0
    for l in reversed(range(DEPTH)):
        p = layers[l]
        dx1, dx1_bf16, parts, dg2 = _ffn_back(dact, dact_bf16, saved_f[l], dict(p, g2=p["g2"] + hold), lands_b[l])
        reduce_b[l] = _exchange_start(parts, _own_part(parts), zero_token, "reduce_b_start_%d" % l, gather=False)
        dact, dact_bf16, parts, gm = _mixer_bwd(dx1, dx1_bf16, saved_m[l], dict(p, gn=p["gn"] + reduce_b[l][4][0:1, 0:1]))
        reduce_a[l] = _exchange_start(parts, _own_part(parts), zero_token, "reduce_a_start_%d" % l, gather=False)
        hold = reduce_a[l][4][0:1, 0:1]
        grads[l] = dict(gm, g2=dg2)
    loss = lax.psum(loss_part[0, 0], ("x", "y", "c"))
    stack = lambda key: jnp.stack([grads[l][key] for l in range(DEPTH)])

    got_b = [_exchange_wait(reduce_b[l], reduce_a[0][4], "reduce_b_wait_%d" % l, gather=False)
             for l in reversed(range(DEPTH))][::-1]
    big = dict(
        ffn_w_gate=[t_last(o) for o in _adamw_rows(gate_t, t_last(m_ffn_w_gate), t_last(v_ffn_w_gate), got_b, 0, "adamw_gate")],
        ffn_w_up=[t_last(o) for o in _adamw_rows(up_t, t_last(m_ffn_w_up), t_last(v_ffn_w_up), got_b, FF_SHARD, "adamw_up")],
        ffn_w_down=_adamw_rows(ffn_w_down, m_ffn_w_down, v_ffn_w_down, got_b, 2 * FF_SHARD, "adamw_down"))
    after_b = zero_token + sum(big[n][1][0, 0, 0] for n in ("ffn_w_gate", "ffn_w_up", "ffn_w_down"))
    got_a = [_exchange_wait(reduce_a[l], after_b, "reduce_a_wait_%d" % l, gather=False) for l in reversed(range(DEPTH))][::-1]
    big.update(
        w_in=_adamw_rows(w_in, m_w_in, v_w_in, got_a, 0, "adamw_w_in"),
        w_out=_adamw_rows(w_out, m_w_out, v_w_out, got_a, A_OUT_AT, "adamw_w_out"))

    full_shapes = [(DEPTH, D_MODEL), (DEPTH, D_MODEL), (DEPTH, HEAD_DIM), (DEPTH, SC_WIDTH), (DEPTH, HEADS),
                   (DEPTH, HEADS), (D_MODEL,), (DEPTH, 4, QKV), (DEPTH, 3, SC_WIDTH)]
    small_keys = ("g1", "g2", "gn", "gs", "al", "dt")
    packed = _pack([stack(k) for k in small_keys] + [d_final[0], stack("cw"), stack("scw")], _packed_rows(full_shapes))
    sg = _unpack(_all_reduce_small(packed), full_shapes)
    sg[7] = lax.dynamic_slice_in_dim(sg[7], chip * (QKV // N_CHIPS), QKV // N_CHIPS, axis=2)
    sg[8] = lax.dynamic_slice_in_dim(sg[8], chip * (SC_WIDTH // N_CHIPS), SC_WIDTH // N_CHIPS, axis=2)
    small_names = ("norm1_g", "norm2_g", "dn_norm_g", "sc_norm_g", "dn_a_log", "dn_dt_bias", "final_norm_g",
                   "dn_conv_w", "sc_conv_w")
    sw = (norm1_g, norm2_g, dn_norm_g, sc_norm_g, dn_a_log, dn_dt_bias, final_norm_g, dn_conv_w, sc_conv_w)
    sm = (m_norm1_g, m_norm2_g, m_dn_norm_g, m_sc_norm_g, m_dn_a_log, m_dn_dt_bias, m_final_norm_g, m_dn_conv_w, m_sc_conv_w)
    sv = (v_norm1_g, v_norm2_g, v_dn_norm_g, v_sc_norm_g, v_dn_a_log, v_dn_dt_bias, v_final_norm_g, v_dn_conv_w, v_sc_conv_w)
    shard_shapes = [t.shape for t in sw]
    rows = _packed_rows(shard_shapes)
    outs = _adamw(_pack(sw, rows), _pack(sm, rows), _pack(sv, rows, fill=1.0), [_pack(sg, rows)], "adamw_small")
    small = {name: [] for name in small_names}
    for o in outs:
        for name, t in zip(small_names, _unpack(o, shard_shapes)):
            small[name].append(t)

    order = ("norm1_g", "w_in", "dn_conv_w", "dn_a_log", "dn_dt_bias", "dn_norm_g", "sc_conv_w", "sc_norm_g", "w_out",
             "norm2_g", "ffn_w_gate", "ffn_w_up", "ffn_w_down", "final_norm_g")
    result = {**big, **small}
    return (loss, dact[None], *[result[n][0] for n in order], *[result[n][1] for n in order],
            *[result[n][2] for n in order], *[result[n][3] for n in order])
```

```python
import jax
import jax.numpy as jnp
from jax import lax
from jax.experimental import pallas as pl
from jax.experimental.pallas import tpu as pltpu

F32 = jnp.float32
BF16 = jnp.bfloat16
MESH = pl.DeviceIdType.MESH

D_MODEL = 1024
DEPTH = 4
HEADS = 4
HEAD_DIM = 128
DN_WIDTH = HEADS * HEAD_DIM
SC_WIDTH = 512
SC_GROUPS = 4
D_FF = 2816
CHUNK = 64
QKV = 3 * DN_WIDTH
W_IN_COLS = 4 * DN_WIDTH + 2 * HEADS + 3 * SC_WIDTH
WA_COLS = QKV + DN_WIDTH + 3 * SC_WIDTH
LANES = 128
EPS = 1e-6
Q_SCALE = HEAD_DIM ** -0.5
N_CHIPS = 4
N_DEV = 8
IN_SHARD = W_IN_COLS // N_CHIPS
OUT_SHARD = D_MODEL // N_CHIPS
FF_SHARD = D_FF // N_CHIPS
A_OUT_AT = D_MODEL
A_ROWS = D_MODEL + OUT_SHARD
B_ROWS = 3 * FF_SHARD

ADAM_LR = 0.001
ADAM_B1 = 0.9
ADAM_B2 = 0.999
ADAM_EPS = 1e-08
ADAM_WD = 0.01
ADAM_STEP = 10

VMEM_LIMIT = 56 * 1024 * 1024

NN = (((1,), (0,)), ((), ()))
NT = (((1,), (1,)), ((), ()))
TN = (((0,), (0,)), ((), ()))


def _mm(a, b, dims=NN):
    return lax.dot_general(a.astype(BF16), b.astype(BF16), dims, preferred_element_type=F32)


def _mm32(a, b, dims=NN):
    return lax.dot_general(a, b, dims, preferred_element_type=F32, precision=lax.Precision.HIGHEST)


def _params(sem, vmem=VMEM_LIMIT):
    return pltpu.CompilerParams(dimension_semantics=sem, vmem_limit_bytes=vmem)


def _sigmoid(x):
    return 0.5 * jnp.tanh(0.5 * x) + 0.5


def _softplus(x):
    return jnp.maximum(x, 0.0) + jnp.log1p(jnp.exp(-jnp.abs(x)))


def _row_acc(acc_ref, val):
    acc_ref[0:1, :] += jnp.sum(val, axis=0, keepdims=True)


def _rms_bwd(dh, xh, r, gain):
    dxh = dh * gain
    return r * (dxh - xh * jnp.mean(dxh * xh, axis=-1, keepdims=True))


def _before_halo(tb):
    return lambda i: (jnp.maximum(i * (tb // 8) - 1, 0), 0)


def _after_halo(tb, n_rows):
    last = n_rows // 8 - 1
    return lambda i: (jnp.minimum((i + 1) * (tb // 8), last), 0)


def _rows_from(xc, offset, tb):
    part = offset % 8
    if part:
        xc = pltpu.roll(xc, xc.shape[0] - part, 0)
    return xc[offset - part:offset - part + tb, :]


def _taps(xc, w, n_taps, tb, first):
    out = w[0:1, :] * _rows_from(xc, first, tb)
    for j in range(1, n_taps):
        out = out + w[j:j + 1, :] * _rows_from(xc, first + j, tb)
    return out


W_Z = QKV
W_BD = W_Z + DN_WIDTH
W_SC = W_BD + 2 * HEADS

def _w_in_cols(shards, lo, hi):
    pieces = []
    for s in range(N_CHIPS):
        a, b = max(lo, IN_SHARD * s), min(hi, IN_SHARD * (s + 1))
        if a < b:
            pieces.append(shards[s][:, a - IN_SHARD * s:b - IN_SHARD * s])
    return pieces[0] if len(pieces) == 1 else jnp.concatenate(pieces, axis=1)


def _in_proj(x, g1, land_a, cw, al_row, dt_row):
    T = x.shape[0]
    tb = 256

    def body(x_ref, g_ref, w_ref, cw_ref, al_ref, dt_ref,
             qkv_ref, z_ref, sc_ref, bd_ref, h_ref, q_ref, k_ref, v_ref, bg_ref, tail_ref):
        @pl.when(pl.program_id(0) == 0)
        def _():
            tail_ref[...] = jnp.zeros_like(tail_ref)

        xv = x_ref[...]
        h = (xv * lax.rsqrt(jnp.mean(xv * xv, axis=-1, keepdims=True) + EPS) * g_ref[...]).astype(BF16)
        shards = [jnp.dot(h, w_ref[s], preferred_element_type=F32) for s in range(N_CHIPS)]
        qkv = _w_in_cols(shards, 0, W_Z)
        bd = jnp.concatenate([_w_in_cols(shards, W_BD, W_SC), jnp.zeros((tb, LANES - 2 * HEADS), F32)], axis=1)
        qkv_ref[...] = qkv
        z_ref[...] = _w_in_cols(shards, W_Z, W_BD)
        bd_ref[...] = bd
        sc_ref[...] = _w_in_cols(shards, W_SC, W_IN_COLS)
        h_ref[...] = h
        halo = tail_ref[...]
        tail_ref[...] = qkv[tb - 8:, :]
        _, _, _, a = _dn_act(qkv, halo, cw_ref[...], tb)
        for hd in range(HEADS):
            sl = slice(HEAD_DIM * hd, HEAD_DIM * (hd + 1))
            qs = a[:, sl]
            q_ref[:, sl] = qs * (lax.rsqrt(jnp.sum(qs * qs, axis=-1, keepdims=True) + EPS) * Q_SCALE)
            ks = a[:, DN_WIDTH + HEAD_DIM * hd:DN_WIDTH + HEAD_DIM * (hd + 1)]
            k_ref[:, sl] = ks * lax.rsqrt(jnp.sum(ks * ks, axis=-1, keepdims=True) + EPS)
        v_ref[...] = a[:, 2 * DN_WIDTH:]
        gates = _gates(bd, al_ref[...], dt_ref[...])
        lane = lax.broadcasted_iota(jnp.int32, gates.shape, 1)
        bg_ref[...] = jnp.where(lane < HEADS, gates, _mm32(_chunk_cumsum_matrix(tb), gates))

    tok = lambda w: pl.BlockSpec((tb, w), lambda i: (i, 0))
    full = lambda t: pl.BlockSpec(t.shape, lambda i: (0, 0))
    return pl.pallas_call(
        body, name="in_proj", grid=(T // tb,),
        in_specs=[tok(D_MODEL), full(g1), _shard_rows(land_a, 0, D_MODEL), full(cw), full(al_row), full(dt_row)],
        out_specs=[tok(QKV), tok(DN_WIDTH), tok(3 * SC_WIDTH), tok(LANES), tok(D_MODEL),
                   tok(DN_WIDTH), tok(DN_WIDTH), tok(DN_WIDTH), tok(LANES)],
        out_shape=[jax.ShapeDtypeStruct((T, QKV), F32), jax.ShapeDtypeStruct((T, DN_WIDTH), F32),
                   jax.ShapeDtypeStruct((T, 3 * SC_WIDTH), F32), jax.ShapeDtypeStruct((T, LANES), F32),
                   jax.ShapeDtypeStruct((T, D_MODEL), BF16)]
        + [jax.ShapeDtypeStruct((T, DN_WIDTH), F32)] * 3 + [jax.ShapeDtypeStruct((T, LANES), F32)],
        scratch_shapes=[pltpu.VMEM((8, QKV), F32)],
        compiler_params=_params(("arbitrary",)),
    )(x, g1, land_a, cw, al_row, dt_row)


def _dn_act(pre, halo, cw, tb):
    xc = jnp.concatenate([halo, pre], axis=0)
    c = _taps(xc, cw, 4, tb, 5)
    sg = _sigmoid(c)
    return xc, c, sg, c * sg


def _gates(bd, al_row, dt_row):
    lane = lax.broadcasted_iota(jnp.int32, bd.shape, 1)
    beta = _sigmoid(bd)
    g = -jnp.exp(al_row) * _softplus(bd + dt_row)
    return jnp.where(lane < HEADS, beta, jnp.where(lane < 2 * HEADS, g, 0.0))


def _chunk_masks():
    row = lax.broadcasted_iota(jnp.int32, (CHUNK, CHUNK), 0)
    col = lax.broadcasted_iota(jnp.int32, (CHUNK, CHUNK), 1)
    return row >= col, row > col


def _chunk_cumsum_matrix(n):
    row = lax.broadcasted_iota(jnp.int32, (n, n), 0)
    col = lax.broadcasted_iota(jnp.int32, (n, n), 1)
    return jnp.logical_and(row >= col, row // CHUNK == col // CHUNK).astype(F32)


def _chunk_units(q_ref, k_ref, v_ref, bg_ref, rows):
    bgc = bg_ref[rows, :]
    bg_t = bgc.T
    qv, kv, vv = q_ref[rows, :], k_ref[rows, :], v_ref[rows, :]
    units = []
    for h in range(HEADS):
        sl = slice(HEAD_DIM * h, HEAD_DIM * (h + 1))
        units.append((qv[:, sl], kv[:, sl], vv[:, sl], bgc[:, h:h + 1], bgc[:, HEADS + h:HEADS + h + 1],
                      bg_t[HEADS + h:HEADS + h + 1, :]))
    return units


def _units_local(units, masks, xms=None):
    causal, strict = masks
    pre = []
    for q, k, v, beta, gc, gr in units:
        kb = k * beta
        eg = jnp.exp(gc)
        g_last = gc[CHUNK - 1:CHUNK, :]
        ek = jnp.exp(g_last - gc)
        pre.append(dict(q=q, k=k, v=v, beta=beta, decay=jnp.exp(jnp.where(causal, gc - gr, -1e30)), kb=kb, vb=v * beta,
                        eg=eg, kbg=kb * eg, ek=ek, gl=jnp.exp(g_last), q_dec=q * eg, k_dec=k * ek))
    both = [_mm(jnp.concatenate([p["kb"], p["q"]], axis=0), p["k"], NT) for p in pre]
    for p, b in zip(pre, both):
        p["low"] = jnp.where(strict, b[:CHUNK] * p["decay"], 0.0)
        p["qk"] = jnp.where(causal, b[CHUNK:] * p["decay"], 0.0)
    xs = xms
    if xs is None:
        xs = [-p["low"] for p in pre]
        pw = [_mm(p["low"], p["low"]) for p in pre]
        for _ in range(4):
            both = [_mm(jnp.concatenate([pp, x], axis=0), pp) for pp, x in zip(pw, xs)]
            xs = [x + pp + b[CHUNK:] for x, pp, b in zip(xs, pw, both)]
            pw = [b[:CHUNK] for b in both]
        last = [_mm(x, pp) for x, pp in zip(xs, pw)]
        xs = [x + pp + b for x, pp, b in zip(xs, pw, last)]
    uw = [_mm(x, jnp.concatenate([p["vb"], p["kbg"]], axis=1)) for x, p in zip(xs, pre)]
    for p, x, b in zip(pre, xs, uw):
        p["xm"] = x
        p["u"] = p["vb"] + b[:, :HEAD_DIM]
        p["w"] = p["kbg"] + b[:, HEAD_DIM:]
    return pre


FWD_GROUP = 8
BWD_GROUP = 8


def _delta_fwd(q, k, v, bg):
    T = q.shape[0]
    tb = 512
    n_chunk = tb // CHUNK

    def body(q_ref, k_ref, v_ref, bg_ref, o_ref, st_ref, xm_ref, s_ref):
        @pl.when(pl.program_id(0) == 0)
        def _():
            s_ref[...] = jnp.zeros_like(s_ref)

        masks = _chunk_masks()

        def group(gi, carry):
            rows = [pl.ds(pl.multiple_of((FWD_GROUP * gi + j) * CHUNK, CHUNK), CHUNK) for j in range(FWD_GROUP)]
            loc = _units_local(sum((_chunk_units(q_ref, k_ref, v_ref, bg_ref, r) for r in rows), []), masks)
            states = [s_ref[h] for h in range(HEADS)]
            for j in range(FWD_GROUP):
                lj = loc[HEADS * j:HEADS * (j + 1)]
                ws = [_mm(jnp.concatenate([p["w"], p["q_dec"]], axis=0), s) for p, s in zip(lj, states)]
                v_new = [p["u"] - b[:CHUNK] for p, b in zip(lj, ws)]
                intra = [_mm(p["qk"], vn) for p, vn in zip(lj, v_new)]
                upd = [_mm(p["k_dec"], vn, TN) for p, vn in zip(lj, v_new)]
                o_ref[rows[j], :] = jnp.concatenate([b[CHUNK:] + a for b, a in zip(ws, intra)], axis=1)
                for h in range(HEADS):
                    st_ref[FWD_GROUP * gi + j, h] = states[h]
                    xm_ref[FWD_GROUP * gi + j, h] = lj[h]["xm"]
                states = [p["gl"] * s + d for p, s, d in zip(lj, states, upd)]
            for h in range(HEADS):
                s_ref[h] = states[h]
            return carry

        lax.fori_loop(0, n_chunk // FWD_GROUP, group, 0)

    tok = lambda w: pl.BlockSpec((tb, w), lambda i: (i, 0))
    return pl.pallas_call(
        body, name="delta_fwd", grid=(T // tb,),
        in_specs=[tok(DN_WIDTH), tok(DN_WIDTH), tok(DN_WIDTH), tok(LANES)],
        out_specs=[tok(DN_WIDTH), pl.BlockSpec((n_chunk, HEADS, HEAD_DIM, HEAD_DIM), lambda i: (i, 0, 0, 0)),
                   pl.BlockSpec((n_chunk, HEADS, CHUNK, CHUNK), lambda i: (i, 0, 0, 0))],
        out_shape=[jax.ShapeDtypeStruct((T, DN_WIDTH), F32),
                   jax.ShapeDtypeStruct((T // CHUNK, HEADS, HEAD_DIM, HEAD_DIM), F32),
                   jax.ShapeDtypeStruct((T // CHUNK, HEADS, CHUNK, CHUNK), F32)],
        scratch_shapes=[pltpu.VMEM((HEADS, HEAD_DIM, HEAD_DIM), F32)],
        compiler_params=_params(("arbitrary",)),
    )(q, k, v, bg)


def _dn_out(o, z, gn):
    outs, ohs, rs = [], [], []
    for hh in range(HEADS):
        oh = o[:, HEAD_DIM * hh:HEAD_DIM * (hh + 1)]
        r = lax.rsqrt(jnp.mean(oh * oh, axis=-1, keepdims=True) + EPS)
        ohs.append(oh * r)
        rs.append(r)
    sz = _sigmoid(z)
    oh = jnp.concatenate(ohs, axis=1)
    gn4 = jnp.concatenate([gn] * HEADS, axis=1)
    return oh * gn4 * (z * sz), oh, rs, sz, gn4


def _sc_fwd(sc_in, halo, cw, tb):
    xc = jnp.concatenate([halo, sc_in], axis=0)
    u = xc[:, SC_WIDTH:2 * SC_WIDTH] * xc[:, 2 * SC_WIDTH:]
    cv = _taps(u, cw, 3, tb, 6)
    gate_b = sc_in[:, :SC_WIDTH]
    y = gate_b * cv
    gw = SC_WIDTH // SC_GROUPS
    yhs, rs = [], []
    for gi in range(SC_GROUPS):
        yg = y[:, gw * gi:gw * (gi + 1)]
        r = lax.rsqrt(jnp.mean(yg * yg, axis=-1, keepdims=True) + EPS)
        yhs.append(yg * r)
        rs.append(r)
    return u, cv, gate_b, jnp.concatenate(yhs, axis=1), rs


def _shard_rows(land, first, rows):
    assert first % rows == 0 and land.shape[0] == N_CHIPS
    return pl.BlockSpec((N_CHIPS, rows, land.shape[2]), lambda i: (0, first // rows, 0))


def _whole(w_ref):
    n, rows, cols = w_ref.shape
    return w_ref[...].reshape(n * rows, cols)


def _mix_out(o, z, sc_in, x, land_a, gn, scw, gs):
    T = x.shape[0]
    tb = 256

    def body(o_ref, z_ref, sc_ref, halo_ref, x_ref, w_ref, gn_ref, scw_ref, gs_ref, x1_ref, mix_ref):
        o_n = _dn_out(o_ref[...], z_ref[...], gn_ref[...])[0]
        halo = jnp.where(pl.program_id(0) > 0, halo_ref[...], 0.0)
        yh = _sc_fwd(sc_ref[...], halo, scw_ref[...], tb)[3]
        mix = jnp.concatenate([o_n, yh * gs_ref[...]], axis=1).astype(BF16)
        x1_ref[...] = x_ref[...] + jnp.dot(mix, _whole(w_ref), preferred_element_type=F32)
        mix_ref[...] = mix

    tok = lambda w: pl.BlockSpec((tb, w), lambda i: (i, 0))
    full = lambda a: pl.BlockSpec(a.shape, lambda i: (0, 0))
    return pl.pallas_call(
        body, name="mix_out", grid=(T // tb,),
        in_specs=[tok(DN_WIDTH), tok(DN_WIDTH), tok(3 * SC_WIDTH), pl.BlockSpec((8, 3 * SC_WIDTH), _before_halo(tb)),
                  tok(D_MODEL), _shard_rows(land_a, A_OUT_AT, OUT_SHARD), full(gn), full(scw), full(gs)],
        out_specs=[tok(D_MODEL), tok(D_MODEL)],
        out_shape=[jax.ShapeDtypeStruct((T, D_MODEL), F32), jax.ShapeDtypeStruct((T, D_MODEL), BF16)],
        compiler_params=_params(("parallel",)),
    )(o, z, sc_in, sc_in, x, land_a, gn, scw, gs)


def _ffn(x1, g2, land_b):
    T = x1.shape[0]
    tb = 256

    def body(x_ref, g_ref, wgt_ref, wut_ref, wd_ref, x2_ref, a_ref, b_ref, h_ref):
        xv = x_ref[...]
        r = lax.rsqrt(jnp.mean(xv * xv, axis=-1, keepdims=True) + EPS)
        h = (xv * r * g_ref[...]).astype(BF16)
        a = lax.dot_general(h, _whole(wgt_ref), NT, preferred_element_type=F32)
        b = lax.dot_general(h, _whole(wut_ref), NT, preferred_element_type=F32)
        act = (a * _sigmoid(a) * b).astype(BF16)
        x2_ref[...] = xv + jnp.dot(act, _whole(wd_ref), preferred_element_type=F32)
        a_ref[...] = a.astype(BF16)
        b_ref[...] = b.astype(BF16)
        h_ref[...] = h

    tok = lambda w: pl.BlockSpec((tb, w), lambda i: (i, 0))
    return pl.pallas_call(
        body, name="ffn", grid=(T // tb,),
        in_specs=[tok(D_MODEL), pl.BlockSpec(g2.shape, lambda i: (0, 0)), _shard_rows(land_b, 0, FF_SHARD),
                  _shard_rows(land_b, FF_SHARD, FF_SHARD), _shard_rows(land_b, 2 * FF_SHARD, FF_SHARD)],
        out_specs=[tok(D_MODEL), tok(D_FF), tok(D_FF), tok(D_MODEL)],
        out_shape=[jax.ShapeDtypeStruct((T, D_MODEL), F32), jax.ShapeDtypeStruct((T, D_FF), BF16),
                   jax.ShapeDtypeStruct((T, D_FF), BF16), jax.ShapeDtypeStruct((T, D_MODEL), BF16)],
        compiler_params=_params(("parallel",)),
    )(x1, g2, land_b, land_b, land_b)


def _loss_head(x, gf, target):
    T = x.shape[0]
    tb = 512

    def body(x_ref, g_ref, t_ref, dx_ref, dxb_ref, loss_ref, dg_ref):
        @pl.when(pl.program_id(0) == 0)
        def _():
            loss_ref[...] = jnp.zeros_like(loss_ref)
            dg_ref[...] = jnp.zeros_like(dg_ref)

        xv = x_ref[...]
        r = lax.rsqrt(jnp.mean(xv * xv, axis=-1, keepdims=True) + EPS)
        xh = xv * r
        err = xh * g_ref[...] - t_ref[...]
        per_tok = jnp.mean(err * err, axis=-1, keepdims=True)
        loss_ref[...] += 0.5 * jnp.sum(per_tok, axis=0, keepdims=True)
        dy = err * (1.0 / D_MODEL)
        _row_acc(dg_ref, dy * xh)
        dx = _rms_bwd(dy, xh, r, g_ref[...])
        dx_ref[...] = dx
        dxb_ref[...] = dx.astype(BF16)

    tok = pl.BlockSpec((tb, D_MODEL), lambda i: (i, 0))
    return pl.pallas_call(
        body, name="loss_head", grid=(T // tb,),
        in_specs=[tok, pl.BlockSpec(gf.shape, lambda i: (0, 0)), tok],
        out_specs=[tok, tok, pl.BlockSpec((8, LANES), lambda i: (0, 0)), pl.BlockSpec((8, D_MODEL), lambda i: (0, 0))],
        out_shape=[jax.ShapeDtypeStruct((T, D_MODEL), F32), jax.ShapeDtypeStruct((T, D_MODEL), BF16),
                   jax.ShapeDtypeStruct((8, LANES), F32), jax.ShapeDtypeStruct((8, D_MODEL), F32)],
        compiler_params=_params(("arbitrary",)),
    )(x, gf, target)


def _ffn_bwd(dx2, x1, a, b, g2, land_b):
    T = x1.shape[0]
    tb = 256

    def body(dx2_ref, x_ref, a_ref, b_ref, g_ref, wgt_ref, wut_ref, wd_ref,
             dx1_ref, dx1b_ref, da_ref, db_ref, act_ref, dg_ref):
        @pl.when(pl.program_id(0) == 0)
        def _():
            dg_ref[...] = jnp.zeros_like(dg_ref)

        dx2v = dx2_ref[...]
        av = a_ref[...].astype(F32)
        bv = b_ref[...].astype(F32)
        dact = _mm(dx2v, _whole(wd_ref), NT)
        sa = _sigmoid(av)
        silu = av * sa
        da = (dact * bv * (sa * (1.0 + av * (1.0 - sa)))).astype(BF16)
        db = (dact * silu).astype(BF16)
        dh = _mm(da, _whole(wgt_ref)) + _mm(db, _whole(wut_ref))
        xv = x_ref[...]
        r = lax.rsqrt(jnp.mean(xv * xv, axis=-1, keepdims=True) + EPS)
        xh = xv * r
        _row_acc(dg_ref, dh * xh)
        dx1 = dx2v + _rms_bwd(dh, xh, r, g_ref[...])
        dx1_ref[...] = dx1
        dx1b_ref[...] = dx1.astype(BF16)
        da_ref[...] = da
        db_ref[...] = db
        act_ref[...] = (silu * bv).astype(BF16)

    tok = lambda w: pl.BlockSpec((tb, w), lambda i: (i, 0))
    return pl.pallas_call(
        body, name="ffn_bwd", grid=(T // tb,),
        in_specs=[tok(D_MODEL), tok(D_MODEL), tok(D_FF), tok(D_FF), pl.BlockSpec(g2.shape, lambda i: (0, 0)),
                  _shard_rows(land_b, 0, FF_SHARD), _shard_rows(land_b, FF_SHARD, FF_SHARD),
                  _shard_rows(land_b, 2 * FF_SHARD, FF_SHARD)],
        out_specs=[tok(D_MODEL), tok(D_MODEL), tok(D_FF), tok(D_FF), tok(D_FF), pl.BlockSpec((8, D_MODEL), lambda i: (0, 0))],
        out_shape=[jax.ShapeDtypeStruct((T, D_MODEL), F32), jax.ShapeDtypeStruct((T, D_MODEL), BF16)]
        + [jax.ShapeDtypeStruct((T, D_FF), BF16)] * 3 + [jax.ShapeDtypeStruct((8, D_MODEL), F32)],
        compiler_params=_params(("arbitrary",)),
    )(dx2, x1, a, b, g2, land_b, land_b, land_b)


def _wgrad_share(a, b, parts, first, name):
    T = b.shape[0]
    rows = a.shape[1] // N_CHIPS
    assert first % rows == 0 and b.shape[1] == parts.shape[2]
    bk = min(T, 1024)
    n_k = T // bk
    group = 2
    assert (group * rows) % LANES == 0

    def body(a_ref, b_ref, parts_ref, o_ref, acc_ref):
        kk = pl.program_id(1)

        @pl.when(kk == 0)
        def _():
            acc_ref[...] = jnp.zeros_like(acc_ref)

        acc_ref[...] += lax.dot_general(a_ref[...], b_ref[...], TN, preferred_element_type=F32)

        @pl.when(kk == n_k - 1)
        def _():
            for s in range(group):
                o_ref[s] = acc_ref[rows * s:rows * (s + 1), :].astype(BF16)

    return pl.pallas_call(
        body, name=name, grid=(N_CHIPS // group, n_k),
        in_specs=[pl.BlockSpec((bk, group * rows), lambda i, kk: (kk, i)),
                  pl.BlockSpec((bk, b.shape[1]), lambda i, kk: (kk, 0)), _ANY],
        out_specs=pl.BlockSpec((group, rows, b.shape[1]), lambda i, kk: (i, first // rows, 0)),
        out_shape=jax.ShapeDtypeStruct(parts.shape, BF16),
        scratch_shapes=[pltpu.VMEM((group * rows, b.shape[1]), F32)],
        input_output_aliases={2: 0},
        compiler_params=_params(("parallel", "arbitrary")),
    )(a, b, parts)


def _mix_out_bwd(dx1, o, z, sc_in, land_a, gn, scw, gs):
    T = dx1.shape[0]
    tb = 256

    def body(dx_ref, o_ref, z_ref, sc_ref, halo_ref, w_ref, gn_ref, scw_ref, gs_ref,
             do_ref, dz_ref, dgb_ref, dcv_ref, dgn_ref, dgs_ref, dscw_ref):
        @pl.when(pl.program_id(0) == 0)
        def _():
            dgn_ref[...] = jnp.zeros_like(dgn_ref)
            dgs_ref[...] = jnp.zeros_like(dgs_ref)
            dscw_ref[...] = jnp.zeros_like(dscw_ref)

        dmix = _mm(dx_ref[...], _whole(w_ref), NT)
        don = dmix[:, :DN_WIDTH]
        dosc = dmix[:, DN_WIDTH:]
        zv = z_ref[...]
        _, oh, rs, sz, gn4 = _dn_out(o_ref[...], zv, gn_ref[...])
        silu_z = zv * sz
        dgn_full = don * oh * silu_z
        dgn_ref[0:1, :] += jnp.sum(sum(dgn_full[:, HEAD_DIM * hh:HEAD_DIM * (hh + 1)] for hh in range(HEADS)),
                                   axis=0, keepdims=True)
        dz_ref[...] = (don * oh * gn4 * (sz * (1.0 + zv * (1.0 - sz)))).astype(BF16)
        t = don * gn4 * silu_z
        for hh in range(HEADS):
            sl = slice(HEAD_DIM * hh, HEAD_DIM * (hh + 1))
            th, ohh = t[:, sl], oh[:, sl]
            do_ref[:, sl] = rs[hh] * (th - ohh * jnp.mean(th * ohh, axis=-1, keepdims=True))
        halo = jnp.where(pl.program_id(0) > 0, halo_ref[...], 0.0)
        u, cv, gate_b, yh, rys = _sc_fwd(sc_ref[...], halo, scw_ref[...], tb)
        _row_acc(dgs_ref, dosc * yh)
        ty = dosc * gs_ref[...]
        gw = SC_WIDTH // SC_GROUPS
        dys = []
        for gi in range(SC_GROUPS):
            sl = slice(gw * gi, gw * (gi + 1))
            tg, yg = ty[:, sl], yh[:, sl]
            dys.append(rys[gi] * (tg - yg * jnp.mean(tg * yg, axis=-1, keepdims=True)))
        dy = jnp.concatenate(dys, axis=1)
        dgb_ref[...] = dy * cv
        dcv = dy * gate_b
        dcv_ref[...] = dcv
        for j in range(3):
            dscw_ref[j:j + 1, :] += jnp.sum(dcv * _rows_from(u, 6 + j, tb), axis=0, keepdims=True)

    tok = lambda w: pl.BlockSpec((tb, w), lambda i: (i, 0))
    full = lambda t: pl.BlockSpec(t.shape, lambda i: (0, 0))
    acc = lambda w: pl.BlockSpec((8, w), lambda i: (0, 0))
    return pl.pallas_call(
        body, name="mix_out_bwd", grid=(T // tb,),
        in_specs=[tok(D_MODEL), tok(DN_WIDTH), tok(DN_WIDTH), tok(3 * SC_WIDTH),
                  pl.BlockSpec((8, 3 * SC_WIDTH), _before_halo(tb)), _shard_rows(land_a, A_OUT_AT, OUT_SHARD),
                  full(gn), full(scw), full(gs)],
        out_specs=[tok(DN_WIDTH), tok(DN_WIDTH), tok(SC_WIDTH), tok(SC_WIDTH), acc(HEAD_DIM), acc(SC_WIDTH), acc(SC_WIDTH)],
        out_shape=[jax.ShapeDtypeStruct((T, DN_WIDTH), F32), jax.ShapeDtypeStruct((T, DN_WIDTH), BF16),
                   jax.ShapeDtypeStruct((T, SC_WIDTH), F32), jax.ShapeDtypeStruct((T, SC_WIDTH), F32),
                   jax.ShapeDtypeStruct((8, HEAD_DIM), F32), jax.ShapeDtypeStruct((8, SC_WIDTH), F32),
                   jax.ShapeDtypeStruct((8, SC_WIDTH), F32)],
        compiler_params=_params(("arbitrary",)),
    )(dx1, o, z, sc_in, sc_in, land_a, gn, scw, gs)


def _delta_bwd(q, k, v, bg, states, xms, do):
    T = q.shape[0]
    tb = 512
    n_chunk = tb // CHUNK
    nb = T // tb

    def body(q_ref, k_ref, v_ref, bg_ref, st_ref, xm_ref, do_ref, dq_ref, dk_ref, dv_ref, dbg_ref, ds_ref):
        @pl.when(pl.program_id(0) == 0)
        def _():
            ds_ref[...] = jnp.zeros_like(ds_ref)

        masks = _chunk_masks()
        causal, strict = masks
        lane = lax.broadcasted_iota(jnp.int32, (CHUNK, LANES), 1)
        last_row = lax.broadcasted_iota(jnp.int32, (CHUNK, 1), 0) == CHUNK - 1
        cat = jnp.concatenate
        heads = range(HEADS)

        def open_chunk(ci, loc):
            rows = pl.ds(pl.multiple_of(ci * CHUNK, CHUNK), CHUNK)
            dov = do_ref[rows, :]
            return dict(rows=rows, loc=loc, do=[dov[:, HEAD_DIM * h:HEAD_DIM * (h + 1)] for h in heads],
                        state=[st_ref[ci, h] for h in heads])

        def a_free(c):
            loc, do, state = c["loc"], c["do"], c["state"]
            w_s = [_mm(p["w"], s) for p, s in zip(loc, state)]
            c["dq_dec"] = [_mm(d, s, NT) for d, s in zip(do, state)]
            c["qk_do"] = [_mm(p["qk"], d, TN) for p, d in zip(loc, do)]
            c["qd_do"] = [_mm(p["q_dec"], d, TN) for p, d in zip(loc, do)]
            c["v_new"] = [p["u"] - t for p, t in zip(loc, w_s)]
            c["dqk"] = [jnp.where(causal, _mm(d, vn, NT), 0.0) for d, vn in zip(do, c["v_new"])]

        def a_state(c, ds_next):
            c["ds_next"] = ds_next
            kd_ds = [_mm(p["k_dec"], d) for p, d in zip(c["loc"], ds_next)]
            c["dk_dec"] = [_mm(vn, d, NT) for vn, d in zip(c["v_new"], ds_next)]
            c["dv_new"] = [a + b for a, b in zip(c["qk_do"], kd_ds)]

        def b_state(c):
            loc = c["loc"]
            w_dv = [_mm(p["w"], dvn, TN) for p, dvn in zip(loc, c["dv_new"])]
            c["dw"] = [-_mm(dvn, s, NT) for dvn, s in zip(c["dv_new"], c["state"])]
            return [loc[h]["gl"] * c["ds_next"][h] + c["qd_do"][h] - w_dv[h] for h in heads]

        def c_solve(c):
            loc, dv_new, dw = c["loc"], c["dv_new"], c["dw"]
            c["dtm"] = [_mm(cat([dvn, d], axis=1), cat([p["vb"], p["kbg"]], axis=1), NT) for dvn, d, p in zip(dv_new, dw, loc)]
            x_t = [_mm(p["xm"], cat([dvn, d], axis=1), TN) for p, dvn, d in zip(loc, dv_new, dw)]
            c["dvb"] = [dvn + t[:, :HEAD_DIM] for dvn, t in zip(dv_new, x_t)]
            c["dkbg"] = [d + t[:, HEAD_DIM:] for d, t in zip(dw, x_t)]

        def d_solve(c):
            c["y"] = [t + _mm(p["xm"], t, TN) for p, t in zip(c["loc"], c["dtm"])]

        def e_solve(c):
            c["dlow"] = [jnp.where(strict, -(t + _mm(t, p["xm"], NT)), 0.0) for p, t in zip(c["loc"], c["y"])]

        def f_close(c):
            loc, rows = c["loc"], c["rows"]
            dmm = [d * p["decay"] for d, p in zip(c["dlow"], loc)]
            dnn = [d * p["decay"] for d, p in zip(c["dqk"], loc)]
            by_k = [_mm(cat([a, b], axis=0), p["k"]) for a, b, p in zip(dmm, dnn, loc)]
            dk_mm = [_mm(cat([a, b], axis=0), cat([p["kb"], p["q"]], axis=0), TN) for a, b, p in zip(dmm, dnn, loc)]
            dq_out, dk_out, dv_out = [], [], []
            dbeta_all = jnp.zeros((CHUNK, LANES), F32)
            dgc_all = jnp.zeros((CHUNK, LANES), F32)
            for h in heads:
                p = loc[h]
                dkb = by_k[h][:CHUNK] + c["dkbg"][h] * p["eg"]
                dq_out.append(by_k[h][CHUNK:] + c["dq_dec"][h] * p["eg"])
                dk_out.append(dk_mm[h] + c["dk_dec"][h] * p["ek"] + dkb * p["beta"])
                dv_out.append(c["dvb"][h] * p["beta"])
                dbeta = jnp.sum(dkb * p["k"] + c["dvb"][h] * p["v"], axis=1, keepdims=True)
                e = c["dlow"][h] * p["low"] + c["dqk"][h] * p["qk"]
                kd = jnp.sum(c["dk_dec"][h] * p["k_dec"], axis=1, keepdims=True)
                dgc = (jnp.sum(e, axis=1, keepdims=True) - jnp.sum(e.T, axis=1, keepdims=True)
                       + jnp.sum(c["dq_dec"][h] * p["q_dec"], axis=1, keepdims=True) - kd
                       + jnp.sum(c["dkbg"][h] * p["kbg"], axis=1, keepdims=True))
                dgl = jnp.sum(jnp.sum(c["ds_next"][h] * c["state"][h], axis=1, keepdims=True), axis=0, keepdims=True)
                d_last = jnp.sum(kd, axis=0, keepdims=True) + dgl * p["gl"]
                dgc = dgc + jnp.where(last_row, d_last, 0.0)
                dbeta_all = jnp.where(lane == h, dbeta, dbeta_all)
                dgc_all = jnp.where(lane == h + HEADS, dgc, dgc_all)
            dq_ref[rows, :] = cat(dq_out, axis=1)
            dk_ref[rows, :] = cat(dk_out, axis=1)
            dv_ref[rows, :] = cat(dv_out, axis=1)
            dbg_ref[rows, :] = dbeta_all + dgc_all

        def group(gj, carry):
            first = n_chunk - 1 - BWD_GROUP * gj
            ids = [first - j for j in range(BWD_GROUP)]
            rows = [pl.ds(pl.multiple_of(ci * CHUNK, CHUNK), CHUNK) for ci in ids]
            loc = _units_local(sum((_chunk_units(q_ref, k_ref, v_ref, bg_ref, r) for r in rows), []), masks,
                               xms=[xm_ref[ci, h] for ci in ids for h in heads])
            chunks = [open_chunk(ci, loc[HEADS * j:HEADS * (j + 1)]) for j, ci in enumerate(ids)]
            for c in chunks:
                a_free(c)
            ds_cur = [ds_ref[h] for h in heads]
            later = (c_solve, d_solve, e_solve, f_close)
            for t in range(2 * (BWD_GROUP - 1) + 2 + len(later)):
                for j, c in enumerate(chunks):
                    stage = t - 2 * j
                    if stage == 0:
                        a_state(c, ds_cur)
                    elif stage == 1:
                        ds_cur = b_state(c)
                    elif 2 <= stage < 2 + len(later):
                        later[stage - 2](c)
            for h in heads:
                ds_ref[h] = ds_cur[h]
            return carry

        lax.fori_loop(0, n_chunk // BWD_GROUP, group, 0)

    tok = lambda w: pl.BlockSpec((tb, w), lambda i: (nb - 1 - i, 0))
    return pl.pallas_call(
        body, name="delta_bwd", grid=(nb,),
        in_specs=[tok(DN_WIDTH), tok(DN_WIDTH), tok(DN_WIDTH), tok(LANES),
                  pl.BlockSpec((n_chunk, HEADS, HEAD_DIM, HEAD_DIM), lambda i: (nb - 1 - i, 0, 0, 0)),
                  pl.BlockSpec((n_chunk, HEADS, CHUNK, CHUNK), lambda i: (nb - 1 - i, 0, 0, 0)), tok(DN_WIDTH)],
        out_specs=[tok(DN_WIDTH), tok(DN_WIDTH), tok(DN_WIDTH), tok(LANES)],
        out_shape=[jax.ShapeDtypeStruct((T, DN_WIDTH), F32)] * 3 + [jax.ShapeDtypeStruct((T, LANES), F32)],
        scratch_shapes=[pltpu.VMEM((HEADS, HEAD_DIM, HEAD_DIM), F32)],
        compiler_params=_params(("arbitrary",)),
    )(q, k, v, bg, states, xms, do)


def _dn_prep_back(dq, dk, dv, dbg, pre, halo, cw, bd, al_row, dt_row, tb):
    xc, c, sg, a = _dn_act(pre, halo, cw, tb)
    dsilu = sg * (1.0 + c * (1.0 - sg))
    pieces = [None] * (2 * HEADS)
    for hd in range(HEADS):
        sl = slice(HEAD_DIM * hd, HEAD_DIM * (hd + 1))
        for which, (base, grad, scale) in enumerate(((0, dq, Q_SCALE), (DN_WIDTH, dk, 1.0))):
            sa = slice(base + HEAD_DIM * hd, base + HEAD_DIM * (hd + 1))
            raw = a[:, sa]
            r = lax.rsqrt(jnp.sum(raw * raw, axis=-1, keepdims=True) + EPS)
            nrm = raw * r
            gn_ = grad[:, sl] * scale
            pieces[which * HEADS + hd] = r * (gn_ - nrm * jnp.sum(gn_ * nrm, axis=-1, keepdims=True)) * dsilu[:, sa]
    dc = jnp.concatenate(pieces + [dv * dsilu[:, 2 * DN_WIDTH:]], axis=1)
    dcw_rows = [jnp.sum(dc * _rows_from(xc, 5 + j, tb), axis=0, keepdims=True) for j in range(4)]
    lane = lax.broadcasted_iota(jnp.int32, bd.shape, 1)
    is_b = lane < HEADS
    is_g = jnp.logical_and(lane >= HEADS, lane < 2 * HEADS)
    dbgv = jnp.where(is_b, dbg, _mm32(_chunk_cumsum_matrix(tb), dbg, TN))
    beta = _sigmoid(bd)
    neg_a = -jnp.exp(al_row)
    pre_sp = bd + dt_row
    g = neg_a * _softplus(pre_sp)
    da_in = dbgv * neg_a * _sigmoid(pre_sp)
    dbd = jnp.where(is_b, dbgv * beta * (1.0 - beta), jnp.where(is_g, da_in, 0.0)).astype(BF16)
    dal_row = jnp.sum(jnp.where(is_g, dbgv * g, 0.0), axis=0, keepdims=True)
    ddt_row = jnp.sum(jnp.where(is_g, da_in, 0.0), axis=0, keepdims=True)
    return dc, dbd, dcw_rows, dal_row, ddt_row


def _dp_of_chip(dqkv, dz, dbd, dsc, s):
    lo, hi = IN_SHARD * s, IN_SHARD * (s + 1)
    pieces = []
    for w_at, w_end, block in ((0, W_Z, dqkv), (W_Z, W_BD, dz), (W_BD, W_SC, dbd), (W_SC, W_IN_COLS, dsc)):
        a, b = max(lo, w_at), min(hi, w_end)
        if a < b:
            pieces.append(block[:, a - w_at:b - w_at])
    pieces.append(jnp.zeros((dqkv.shape[0], D_MODEL - IN_SHARD), dqkv.dtype))
    return jnp.concatenate(pieces, axis=1)


def _in_proj_bwd(dq, dk, dv, dbg, qkv, bd, al_row, dt_row, dcv, dgb, sc_in, dz, cw, scw, dx1, x, g1, land_a):
    T = x.shape[0]
    tb = 256
    nb = T // tb

    def body(dq_ref, dk_ref, dv_ref, dbg_ref, pre_ref, pre_halo_ref, bd_ref, al_ref, dt_ref,
             dcv_ref, dcv_halo_ref, dgb_ref, sc_ref, dz_ref, cw_ref, scw_ref, dx1_ref, x_ref, g_ref, w_ref,
             dx_ref, dxb_ref, dps_ref, dg_ref, dcw_ref, dal_ref, ddt_ref, head_ref):
        @pl.when(pl.program_id(0) == 0)
        def _():
            for ref in (dg_ref, dcw_ref, dal_ref, ddt_ref, head_ref):
                ref[...] = jnp.zeros_like(ref)

        block = nb - 1 - pl.program_id(0)
        last = block == nb - 1
        pre_halo = jnp.where(block > 0, pre_halo_ref[...], 0.0)
        dc, dbd, dcw_rows, dal_row, ddt_row = _dn_prep_back(
            dq_ref[...], dk_ref[...], dv_ref[...], dbg_ref[...], pre_ref[...], pre_halo, cw_ref[...], bd_ref[...],
            al_ref[...], dt_ref[...], tb)
        for j in range(4):
            dcw_ref[j:j + 1, :] += dcw_rows[j]
        dal_ref[0:1, :] += dal_row
        ddt_ref[0:1, :] += ddt_row
        xc = jnp.concatenate([dc, head_ref[...]], axis=0)
        head_ref[...] = dc[0:8, :]
        w4 = cw_ref[...]
        dqkv = w4[3:4, :] * xc[0:tb, :]
        for j in range(3):
            dqkv = dqkv + w4[j:j + 1, :] * _rows_from(xc, 3 - j, tb)
        yc = jnp.concatenate([dcv_ref[...], jnp.where(last, 0.0, dcv_halo_ref[...])], axis=0)
        w3 = scw_ref[...]
        du = w3[2:3, :] * yc[0:tb, :] + w3[1:2, :] * _rows_from(yc, 1, tb) + w3[0:1, :] * _rows_from(yc, 2, tb)
        sc = sc_ref[...]
        dsc = jnp.concatenate([dgb_ref[...], du * sc[:, 2 * SC_WIDTH:], du * sc[:, SC_WIDTH:2 * SC_WIDTH]], axis=1)
        blocks = (dqkv.astype(BF16), dz_ref[...], dbd, dsc.astype(BF16))
        dh = jnp.zeros((tb, D_MODEL), F32)
        for s in range(N_CHIPS):
            dps = _dp_of_chip(*blocks, s)
            dps_ref[:, D_MODEL * s:D_MODEL * (s + 1)] = dps
            dh = dh + lax.dot_general(dps, w_ref[s], NT, preferred_element_type=F32)
        xv = x_ref[...]
        r = lax.rsqrt(jnp.mean(xv * xv, axis=-1, keepdims=True) + EPS)
        xh = xv * r
        _row_acc(dg_ref, dh * xh)
        dx = dx1_ref[...] + _rms_bwd(dh, xh, r, g_ref[...])
        dx_ref[...] = dx
        dxb_ref[...] = dx.astype(BF16)

    tok = lambda w: pl.BlockSpec((tb, w), lambda i: (nb - 1 - i, 0))
    full = lambda t: pl.BlockSpec(t.shape, lambda i: (0, 0))
    acc = lambda w: pl.BlockSpec((8, w), lambda i: (0, 0))
    before = lambda w: pl.BlockSpec((8, w), lambda i: _before_halo(tb)(nb - 1 - i))
    after = lambda w: pl.BlockSpec((8, w), lambda i: _after_halo(tb, T)(nb - 1 - i))
    return pl.pallas_call(
        body, name="in_proj_bwd", grid=(nb,),
        in_specs=[tok(DN_WIDTH), tok(DN_WIDTH), tok(DN_WIDTH), tok(LANES), tok(QKV), before(QKV), tok(LANES),
                  full(al_row), full(dt_row), tok(SC_WIDTH), after(SC_WIDTH), tok(SC_WIDTH), tok(3 * SC_WIDTH),
                  tok(DN_WIDTH), full(cw), full(scw), tok(D_MODEL), tok(D_MODEL), full(g1), _shard_rows(land_a, 0, D_MODEL)],
        out_specs=[tok(D_MODEL), tok(D_MODEL), tok(N_CHIPS * D_MODEL), acc(D_MODEL), acc(QKV), acc(LANES), acc(LANES)],
        out_shape=[jax.ShapeDtypeStruct((T, D_MODEL), F32), jax.ShapeDtypeStruct((T, D_MODEL), BF16),
                   jax.ShapeDtypeStruct((T, N_CHIPS * D_MODEL), BF16), jax.ShapeDtypeStruct((8, D_MODEL), F32),
                   jax.ShapeDtypeStruct((8, QKV), F32), jax.ShapeDtypeStruct((8, LANES), F32),
                   jax.ShapeDtypeStruct((8, LANES), F32)],
        scratch_shapes=[pltpu.VMEM((8, QKV), F32)],
        compiler_params=_params(("arbitrary",)),
    )(dq, dk, dv, dbg, qkv, qkv, bd, al_row, dt_row, dcv, dcv, dgb, sc_in, dz, cw, scw, dx1, x, g1, land_a)


def _wgrad_in_share(h, dps, parts, name):
    T = h.shape[0]
    bk = min(T, 1024)
    n_k = T // bk

    def body(a_ref, b_ref, parts_ref, o_ref, acc_ref):
        kk = pl.program_id(1)

        @pl.when(kk == 0)
        def _():
            acc_ref[...] = jnp.zeros_like(acc_ref)

        acc_ref[...] += lax.dot_general(a_ref[...], b_ref[...], TN, preferred_element_type=F32)

        @pl.when(kk == n_k - 1)
        def _():
            o_ref[0] = acc_ref[...].astype(BF16)

    return pl.pallas_call(
        body, name=name, grid=(N_CHIPS, n_k),
        in_specs=[pl.BlockSpec((bk, D_MODEL), lambda j, kk: (kk, 0)), pl.BlockSpec((bk, D_MODEL), lambda j, kk: (kk, j)), _ANY],
        out_specs=pl.BlockSpec((1, D_MODEL, D_MODEL), lambda j, kk: (j, 0, 0)),
        out_shape=jax.ShapeDtypeStruct(parts.shape, BF16),
        scratch_shapes=[pltpu.VMEM((D_MODEL, D_MODEL), F32)],
        input_output_aliases={2: 0},
        compiler_params=_params(("parallel", "arbitrary")),
    )(h, dps, parts)


def _pad_rows(a, rows=8):
    return jnp.pad(a, ((0, rows - a.shape[0]), (0, 0)))


def _gate_rows(a_log, dt_bias):
    put = lambda t: jnp.pad(t.reshape(1, HEADS), ((0, 0), (HEADS, LANES - 2 * HEADS)))
    return put(a_log), put(dt_bias)


def _mixer_fwd(x, p):
    qkv, z, sc_in, bd, h, q, k, v, bg = _in_proj(x, p["g1"], p["land_a"], p["cw"], p["al"], p["dt"])
    o, states, xms = _delta_fwd(q, k, v, bg)
    x1, mix = _mix_out(o, z, sc_in, x, p["land_a"], p["gn"], p["scw"], p["gs"])
    return x1, dict(x=x, qkv=qkv, z=z, sc_in=sc_in, bd=bd, h=h, q=q, k=k, v=v, bg=bg, o=o, states=states, xms=xms, mix=mix)


def _ffn_fwd(x1, p, land_b):
    x2, a, b, h2 = _ffn(x1, p["g2"], land_b)
    return x2, dict(x1=x1, a=a, b=b, h2=h2)


def _ffn_back(dx2, dx2_bf16, s, p, land_b):
    dx1, dx1_bf16, da, db, act, dg2 = _ffn_bwd(dx2, s["x1"], s["a"], s["b"], p["g2"], land_b)
    parts = lax.empty((N_CHIPS, B_ROWS, D_MODEL), BF16)
    parts = _wgrad_share(act, dx2_bf16, parts, 2 * FF_SHARD, "wgrad_down")
    parts = _wgrad_share(da, s["h2"], parts, 0, "wgrad_gate")
    parts = _wgrad_share(db, s["h2"], parts, FF_SHARD, "wgrad_up")
    return dx1, dx1_bf16, parts, dg2[0]


def _mixer_bwd(dx1, dx1_bf16, s, p):
    do, dz, dgb, dcv, dgn, dgs, dscw = _mix_out_bwd(dx1, s["o"], s["z"], s["sc_in"], p["land_a"], p["gn"], p["scw"], p["gs"])
    dq, dk, dv, dbg = _delta_bwd(s["q"], s["k"], s["v"], s["bg"], s["states"], s["xms"], do)
    dx, dx_bf16, dps, dg1, dcw, dal, ddt = _in_proj_bwd(
        dq, dk, dv, dbg, s["qkv"], s["bd"], p["al"], p["dt"], dcv, dgb, s["sc_in"], dz, p["cw"], p["scw"], dx1, s["x"],
        p["g1"], p["land_a"])
    parts = lax.empty((N_CHIPS, A_ROWS, D_MODEL), BF16)
    parts = _wgrad_in_share(s["h"], dps, parts, "wgrad_in")
    parts = _wgrad_share(s["mix"], dx1_bf16, parts, A_OUT_AT, "wgrad_out")
    g = dict(g1=dg1[0], gn=dgn[0], gs=dgs[0], scw=dscw[:3], cw=dcw[:4], al=dal[0, HEADS:2 * HEADS], dt=ddt[0, HEADS:2 * HEADS])
    return dx, dx_bf16, parts, g


def _place():
    return lax.axis_index("x"), lax.axis_index("y"), lax.axis_index("c")


def _other_chips(x, y):
    return [(1 - x, y), (x, 1 - y), (1 - x, 1 - y)]


_HBM = pl.BlockSpec(memory_space=pltpu.HBM)


def _chip_exchange(arrs, name, gather):
    n = len(arrs)

    def body(*refs):
        ins, outs = refs[:n], refs[n:2 * n]
        send_sems, recv_sems, local_sems = refs[2 * n:]
        x, y, c = _place()
        me = 2 * x + y
        others = _other_chips(x, y)

        def remote(k, j, landing):
            px, py = others[j]
            src = ins[k] if gather else ins[k].at[2 * px + py]
            return pltpu.make_async_remote_copy(src_ref=src, dst_ref=outs[k].at[landing], send_sem=send_sems.at[k, j],
                                                recv_sem=recv_sems.at[k, j], device_id=(px, py, c), device_id_type=MESH)

        local = [pltpu.make_async_copy(ins[k] if gather else ins[k].at[me], outs[k].at[me], local_sems.at[k])
                 for k in range(n)]
        sends = [remote(k, j, me) for k in range(n) for j in range(3)]
        for cp in local + sends:
            cp.start()
        for k in range(n):
            for j, (px, py) in enumerate(others):
                remote(k, j, 2 * px + py).wait_recv()
        for cp in sends:
            cp.wait_send()
        for cp in local:
            cp.wait()

    shapes = [jax.ShapeDtypeStruct(((N_CHIPS,) + a.shape) if gather else a.shape, a.dtype) for a in arrs]
    return pl.pallas_call(
        body, name=name, in_specs=[_HBM] * n, out_specs=[_HBM] * n, out_shape=shapes,
        scratch_shapes=[pltpu.SemaphoreType.DMA((n, 3)), pltpu.SemaphoreType.DMA((n, 3)), pltpu.SemaphoreType.DMA((n,))],
    )(*arrs)


_SEM = pl.BlockSpec(memory_space=pltpu.SEMAPHORE)
_ANY = pl.BlockSpec(memory_space=pl.ANY)
_EFFECT = pltpu.SideEffectType.DATAFLOW_SIDE_EFFECTING


_FLIPS = [(a, b, cc) for a in (0, 1) for b in (0, 1) for cc in (0, 1)][1:]


def _split_copies(src_ref, land_ref, send_sems, recv_sems, gather, sending):
    x, y, c = _place()
    copies = []
    if gather:
        me = 2 * x + y
        for j, (px, py) in enumerate(_other_chips(x, y)):
            copies.append(pltpu.make_async_remote_copy(
                src_ref=src_ref, dst_ref=land_ref.at[me if sending else 2 * px + py],
                send_sem=send_sems.at[j], recv_sem=recv_sems.at[j], device_id=(px, py, c), device_id_type=MESH))
        return copies
    me = 4 * x + 2 * y + c
    for j, (a, b, cc) in enumerate(_FLIPS):
        px, py, pc = (1 - x) if a else x, (1 - y) if b else y, (1 - c) if cc else c
        copies.append(pltpu.make_async_remote_copy(
            src_ref=src_ref.at[2 * px + py], dst_ref=land_ref.at[me if sending else 4 * px + 2 * py + pc],
            send_sem=send_sems.at[j], recv_sem=recv_sems.at[j], device_id=(px, py, pc), device_id_type=MESH))
    return copies


def _own_slot(share):
    chip = 2 * lax.axis_index("x") + lax.axis_index("y")
    return lax.dynamic_update_slice(lax.empty((N_CHIPS,) + share.shape, share.dtype), share[None], (chip, 0, 0))


def _own_part(parts):
    chip = 2 * lax.axis_index("x") + lax.axis_index("y")
    own = lax.dynamic_index_in_dim(parts, chip, 0, keepdims=True)
    return lax.dynamic_update_slice(lax.empty((N_DEV,) + parts.shape[1:], parts.dtype), own,
                                    (2 * chip + lax.axis_index("c"), 0, 0))


def _exchange_start(src, land, after, name, gather):
    def body(src_ref, land_ref, after_ref, send_sems, recv_sems, src_thru, land_thru, token):
        for cp in _split_copies(src_ref, land_ref, send_sems, recv_sems, gather, sending=True):
            cp.start()
        token[...] = jnp.zeros_like(token)

    hbm = lambda t: pltpu.with_memory_space_constraint(t, pltpu.HBM)
    n_copies = N_CHIPS - 1 if gather else N_DEV - 1
    return pl.pallas_call(
        body, name=name,
        out_shape=(pltpu.SemaphoreType.DMA((n_copies,)), pltpu.SemaphoreType.DMA((n_copies,)), pltpu.HBM(src.shape, src.dtype),
                   pltpu.HBM(land.shape, land.dtype), jax.ShapeDtypeStruct((8, LANES), F32)),
        in_specs=(_HBM, _HBM, _ANY), out_specs=(_SEM, _SEM, _HBM, _HBM, pl.BlockSpec(memory_space=pltpu.VMEM)),
        input_output_aliases={0: 2, 1: 3},
        compiler_params=pltpu.CompilerParams(has_side_effects=_EFFECT),
    )(hbm(src), hbm(land), after)


def _exchange_wait(started, after, name, gather):
    send_sems, recv_sems, src_thru, land_thru, _ = started

    def body(src_ref, land_ref, send_sems, recv_sems, after_ref, src_dead, got_ref):
        for cp in _split_copies(src_ref, land_ref, send_sems, recv_sems, gather, sending=False):
            cp.wait_send()
            cp.wait_recv()

    return pl.pallas_call(
        body, name=name,
        out_shape=(pltpu.HBM(src_thru.shape, src_thru.dtype), pltpu.HBM(land_thru.shape, land_thru.dtype)),
        in_specs=(_HBM, _HBM, _SEM, _SEM, _ANY), out_specs=(_HBM, _HBM), input_output_aliases={0: 0, 1: 1},
        compiler_params=pltpu.CompilerParams(has_side_effects=_EFFECT),
    )(src_thru, land_thru, send_sems, recv_sems, after)[1]


def _all_reduce_small(v):
    rows = v.shape[0]
    flips = [(a, b, cc) for a in (0, 1) for b in (0, 1) for cc in (0, 1)][1:]

    def body(v_ref, out_ref, buf_ref, send_sems, recv_sems):
        x, y, c = _place()
        me = 4 * x + 2 * y + c
        peers = [((1 - x) if a else x, (1 - y) if b else y, (1 - c) if cc else c) for a, b, cc in flips]

        def copy(j, landing):
            return pltpu.make_async_remote_copy(src_ref=v_ref, dst_ref=buf_ref.at[landing], send_sem=send_sems.at[j],
                                                recv_sem=recv_sems.at[j], device_id=peers[j], device_id_type=MESH)

        sends = [copy(j, me) for j in range(N_DEV - 1)]
        for cp in sends:
            cp.start()
        buf_ref[me] = v_ref[...]
        for j, (px, py, pc) in enumerate(peers):
            copy(j, 4 * px + 2 * py + pc).wait_recv()
        for cp in sends:
            cp.wait_send()
        acc = buf_ref[0]
        for d in range(1, N_DEV):
            acc = acc + buf_ref[d]
        out_ref[...] = acc

    vmem = pl.BlockSpec(memory_space=pltpu.VMEM)
    return pl.pallas_call(
        body, name="all_reduce_small", in_specs=[vmem], out_specs=vmem,
        out_shape=jax.ShapeDtypeStruct(v.shape, F32),
        scratch_shapes=[pltpu.VMEM((N_DEV, rows, LANES), F32), pltpu.SemaphoreType.DMA((N_DEV - 1,)),
                        pltpu.SemaphoreType.DMA((N_DEV - 1,))],
    )(v)


def _row_block(*sizes):
    return next(t for t in (176, 128, 64) if all(s % t == 0 for s in sizes))


def _adam_update(w, m, v, g):
    r1 = 1.0 / (1.0 - ADAM_B1 ** ADAM_STEP)
    r2 = 1.0 / (1.0 - ADAM_B2 ** ADAM_STEP)
    m_new = ADAM_B1 * m + (1.0 - ADAM_B1) * g
    v_new = ADAM_B2 * v + (1.0 - ADAM_B2) * (g * g)
    return -ADAM_LR * ((m_new * r1) / (jnp.sqrt(v_new * r2) + ADAM_EPS) + ADAM_WD * w), m_new, v_new


def _adamw_rows(w, m, v, got, first, name):
    n_layers, rows, cols = w.shape
    tr = _row_block(rows, first)

    def body(*refs):
        w_ref, m_ref, v_ref = refs[:3]
        g_out, d_out, m_out, v_out = refs[3 + n_layers:]
        for k in range(n_layers):
            @pl.when(pl.program_id(0) == k)
            def _(p_ref=refs[3 + k]):
                g = p_ref[0].astype(F32)
                for d in range(1, N_DEV):
                    g = g + p_ref[d].astype(F32)
                g = g[:, :cols]
                d_out[0], m_out[0], v_out[0] = _adam_update(w_ref[0], m_ref[0], v_ref[0], g)
                g_out[0] = g

    blk = pl.BlockSpec((1, tr, cols), lambda l, i: (l, i, 0))
    parts = [pl.BlockSpec((N_DEV, tr, got[0].shape[2]), lambda l, i, k=k: (0, jnp.where(l == k, first // tr + i, 0), 0))
             for k in range(n_layers)]
    return pl.pallas_call(
        body, name=name, grid=(n_layers, rows // tr),
        in_specs=[blk] * 3 + parts, out_specs=[blk] * 4,
        out_shape=[jax.ShapeDtypeStruct(w.shape, F32)] * 4,
        compiler_params=_params(("arbitrary", "arbitrary")),
    )(w, m, v, *got)


def _adamw(w, m, v, g_parts, name):
    rows, cols = w.shape
    tr = min(rows, 256)
    n = len(g_parts)

    def body(*refs):
        w_ref, m_ref, v_ref = refs[:3]
        g_refs = refs[3:3 + n]
        g_out, d_out, m_out, v_out = refs[3 + n:]
        g = g_refs[0][...]
        for r in g_refs[1:]:
            g = g + r[...]
        d_out[...], m_out[...], v_out[...] = _adam_update(w_ref[...], m_ref[...], v_ref[...], g)
        g_out[...] = g

    blk = pl.BlockSpec((tr, cols), lambda i: (i, 0))
    return pl.pallas_call(
        body, name=name, grid=(rows // tr,),
        in_specs=[blk] * (3 + n), out_specs=[blk] * 4,
        out_shape=[jax.ShapeDtypeStruct((rows, cols), F32)] * 4,
        compiler_params=_params(("parallel",)),
    )(w, m, v, *g_parts)


def _pack(parts, rows, fill=0.0):
    flat = jnp.concatenate([p.reshape(-1) for p in parts])
    return jnp.pad(flat, (0, rows * LANES - flat.shape[0]), constant_values=fill).reshape(rows, LANES)


def _unpack(packed, shapes):
    flat = packed.reshape(-1)
    out, at = [], 0
    for shp in shapes:
        size = 1
        for s in shp:
            size *= s
        out.append(flat[at:at + size].reshape(shp))
        at += size
    return out


def _packed_rows(shapes):
    total = 0
    for shp in shapes:
        size = 1
        for s in shp:
            size *= s
        total += size
    return -(-total // (8 * LANES)) * 8


def _cols_full(g, l):
    t = g[:, l]
    return jnp.moveaxis(t, 0, 1).reshape(t.shape[1], N_CHIPS * t.shape[2])


def _pad_cols(t):
    return jnp.pad(t, ((0, 0),) * (t.ndim - 1) + ((0, D_MODEL - t.shape[-1]),))


def kernel(x, norm1_g, w_in, dn_conv_w, dn_a_log, dn_dt_bias, dn_norm_g, sc_conv_w, sc_norm_g, w_out, norm2_g, ffn_w_gate, ffn_w_up, ffn_w_down, final_norm_g, loss_target, m_norm1_g, m_w_in, m_dn_conv_w, m_dn_a_log, m_dn_dt_bias, m_dn_norm_g, m_sc_conv_w, m_sc_norm_g, m_w_out, m_norm2_g, m_ffn_w_gate, m_ffn_w_up, m_ffn_w_down, m_final_norm_g, v_norm1_g, v_w_in, v_dn_conv_w, v_dn_a_log, v_dn_dt_bias, v_dn_norm_g, v_sc_conv_w, v_sc_norm_g, v_w_out, v_norm2_g, v_ffn_w_gate, v_ffn_w_up, v_ffn_w_down, v_final_norm_g):
    chip = 2 * lax.axis_index("x") + lax.axis_index("y")

    g_cw, g_scw = _chip_exchange([dn_conv_w, sc_conv_w], "gather_conv", gather=True)

    t_last = lambda t: jnp.swapaxes(t, -1, -2)
    gate_t, up_t = t_last(ffn_w_gate), t_last(ffn_w_up)
    zero_token = jnp.zeros((8, LANES), F32)

    def shares(l, tie):
        share_a = jnp.concatenate([_pad_cols(w_in[l] + tie), w_out[l]], axis=0).astype(BF16)
        share_b = jnp.concatenate([gate_t[l] + tie, up_t[l], ffn_w_down[l]], axis=0).astype(BF16)
        return share_a, _own_slot(share_a), share_b, _own_slot(share_b)

    def gather_start(l, packed, after):
        a = _exchange_start(packed[0], packed[1], after, "gather_a_start_%d" % l, gather=True)
        b = _exchange_start(packed[2], packed[3], a[4], "gather_b_start_%d" % l, gather=True)
        return a, b

    ga, gb = gather_start(0, shares(0, 0.0), g_cw)
    packed = [None] + [shares(l, gb[4][0, 0]) for l in range(1, DEPTH)]
    packed_all = sum(t[0, 0].astype(F32) for p in packed[1:] for t in (p[0], p[2]))
    land_a = _exchange_wait(ga, zero_token + packed_all, "gather_a_wait_0", gather=True)
    act = x[0]
    layers, saved_m, saved_f, lands_b = [], [], [], []
    for l in range(DEPTH):
        hold = 0.0
        if l + 1 < DEPTH:
            ga, gb_next = gather_start(l + 1, packed[l + 1], land_a)
            hold = gb_next[4][0:1, 0:1]
        al, dt = _gate_rows(dn_a_log[l], dn_dt_bias[l])
        layers.append(dict(
            g1=norm1_g[l][None] + hold, cw=_pad_rows(_cols_full(g_cw, l)), al=al, dt=dt,
            gn=dn_norm_g[l][None], scw=_pad_rows(_cols_full(g_scw, l)), gs=sc_norm_g[l][None],
            land_a=land_a, g2=norm2_g[l][None]))
        x1, s = _mixer_fwd(act, layers[l])
        saved_m.append(s)
        lands_b.append(_exchange_wait(gb, x1, "gather_b_wait_%d" % l, gather=True))
        act, s = _ffn_fwd(x1, layers[l], lands_b[l])
        saved_f.append(s)
        if l + 1 < DEPTH:
            land_a = _exchange_wait(ga, act, "gather_a_wait_%d" % (l + 1), gather=True)
            gb = gb_next

    dact, dact_bf16, loss_part, d_final = _loss_head(act, final_norm_g[None], loss_target[0])
    grads, reduce_a, reduce_b = [None] * DEPTH, [None] * DEPTH, [None] * DEPTH
    hold = 0.0
    for l in reversed(range(DEPTH)):
        p = layers[l]
        dx1, dx1_bf16, parts, dg2 = _ffn_back(dact, dact_bf16, saved_f[l], dict(p, g2=p["g2"] + hold), lands_b[l])
        reduce_b[l] = _exchange_start(parts, _own_part(parts), zero_token, "reduce_b_start_%d" % l, gather=False)
        dact, dact_bf16, parts, gm = _mixer_bwd(dx1, dx1_bf16, saved_m[l], dict(p, gn=p["gn"] + reduce_b[l][4][0:1, 0:1]))
        reduce_a[l] = _exchange_start(parts, _own_part(parts), zero_token, "reduce_a_start_%d" % l, gather=False)
        hold = reduce_a[l][4][0:1, 0:1]
        grads[l] = dict(gm, g2=dg2)
    loss = lax.psum(loss_part[0, 0], ("x", "y", "c"))
    stack = lambda key: jnp.stack([grads[l][key] for l in range(DEPTH)])

    got_b = [_exchange_wait(reduce_b[l], reduce_a[0][4], "reduce_b_wait_%d" % l, gather=False)
             for l in reversed(range(DEPTH))][::-1]
    big = dict(
        ffn_w_gate=[t_last(o) for o in _adamw_rows(gate_t, t_last(m_ffn_w_gate), t_last(v_ffn_w_gate), got_b, 0, "adamw_gate")],
        ffn_w_up=[t_last(o) for o in _adamw_rows(up_t, t_last(m_ffn_w_up), t_last(v_ffn_w_up), got_b, FF_SHARD, "adamw_up")],
        ffn_w_down=_adamw_rows(ffn_w_down, m_ffn_w_down, v_ffn_w_down, got_b, 2 * FF_SHARD, "adamw_down"))
    after_b = zero_token + sum(big[n][1][0, 0, 0] for n in ("ffn_w_gate", "ffn_w_up", "ffn_w_down"))
    got_a = [_exchange_wait(reduce_a[l], after_b, "reduce_a_wait_%d" % l, gather=False) for l in reversed(range(DEPTH))][::-1]
    big.update(
        w_in=_adamw_rows(w_in, m_w_in, v_w_in, got_a, 0, "adamw_w_in"),
        w_out=_adamw_rows(w_out, m_w_out, v_w_out, got_a, A_OUT_AT, "adamw_w_out"))

    full_shapes = [(DEPTH, D_MODEL), (DEPTH, D_MODEL), (DEPTH, HEAD_DIM), (DEPTH, SC_WIDTH), (DEPTH, HEADS),
                   (DEPTH, HEADS), (D_MODEL,), (DEPTH, 4, QKV), (DEPTH, 3, SC_WIDTH)]
    small_keys = ("g1", "g2", "gn", "gs", "al", "dt")
    packed = _pack([stack(k) for k in small_keys] + [d_final[0], stack("cw"), stack("scw")], _packed_rows(full_shapes))
    sg = _unpack(_all_reduce_small(packed), full_shapes)
    sg[7] = lax.dynamic_slice_in_dim(sg[7], chip * (QKV // N_CHIPS), QKV // N_CHIPS, axis=2)
    sg[8] = lax.dynamic_slice_in_dim(sg[8], chip * (SC_WIDTH // N_CHIPS), SC_WIDTH // N_CHIPS, axis=2)
    small_names = ("norm1_g", "norm2_g", "dn_norm_g", "sc_norm_g", "dn_a_log", "dn_dt_bias", "final_norm_g",
                   "dn_conv_w", "sc_conv_w")
    sw = (norm1_g, norm2_g, dn_norm_g, sc_norm_g, dn_a_log, dn_dt_bias, final_norm_g, dn_conv_w, sc_conv_w)
    sm = (m_norm1_g, m_norm2_g, m_dn_norm_g, m_sc_norm_g, m_dn_a_log, m_dn_dt_bias, m_final_norm_g, m_dn_conv_w, m_sc_conv_w)
    sv = (v_norm1_g, v_norm2_g, v_dn_norm_g, v_sc_norm_g, v_dn_a_log, v_dn_dt_bias, v_final_norm_g, v_dn_conv_w, v_sc_conv_w)
    shard_shapes = [t.shape for t in sw]
    rows = _packed_rows(shard_shapes)
    outs = _adamw(_pack(sw, rows), _pack(sm, rows), _pack(sv, rows, fill=1.0), [_pack(sg, rows)], "adamw_small")
    small = {name: [] for name in small_names}
    for o in outs:
        for name, t in zip(small_names, _unpack(o, shard_shapes)):
            small[name].append(t)

    order = ("norm1_g", "w_in", "dn_conv_w", "dn_a_log", "dn_dt_bias", "dn_norm_g", "sc_conv_w", "sc_norm_g", "w_out",
             "norm2_g", "ffn_w_gate", "ffn_w_up", "ffn_w_down", "final_norm_g")
    result = {**big, **small}
    return (loss, dact[None], *[result[n][0] for n in order], *[result[n][1] for n in order],
            *[result[n][2] for n in order], *[result[n][3] for n in order])
```

```python
import jax
import jax.numpy as jnp
from jax import lax
from jax.experimental import pallas as pl
from jax.experimental.pallas import tpu as pltpu

F32 = jnp.float32
BF16 = jnp.bfloat16
MESH = pl.DeviceIdType.MESH

D_MODEL = 1024
DEPTH = 4
HEADS = 4
HEAD_DIM = 128
DN_WIDTH = HEADS * HEAD_DIM
SC_WIDTH = 512
SC_GROUPS = 4
D_FF = 2816
CHUNK = 64
QKV = 3 * DN_WIDTH
W_IN_COLS = 4 * DN_WIDTH + 2 * HEADS + 3 * SC_WIDTH
WA_COLS = QKV + DN_WIDTH + 3 * SC_WIDTH
LANES = 128
EPS = 1e-6
Q_SCALE = HEAD_DIM ** -0.5
N_CHIPS = 4
N_DEV = 8
IN_SHARD = W_IN_COLS // N_CHIPS
OUT_SHARD = D_MODEL // N_CHIPS
FF_SHARD = D_FF // N_CHIPS
A_OUT_AT = D_MODEL
A_ROWS = D_MODEL + OUT_SHARD
B_ROWS = 3 * FF_SHARD

ADAM_LR = 0.001
ADAM_B1 = 0.9
ADAM_B2 = 0.999
ADAM_EPS = 1e-08
ADAM_WD = 0.01
ADAM_STEP = 10

VMEM_LIMIT = 56 * 1024 * 1024

NN = (((1,), (0,)), ((), ()))
NT = (((1,), (1,)), ((), ()))
TN = (((0,), (0,)), ((), ()))


def _mm(a, b, dims=NN):
    return lax.dot_general(a.astype(BF16), b.astype(BF16), dims, preferred_element_type=F32)


def _mm32(a, b, dims=NN):
    return lax.dot_general(a, b, dims, preferred_element_type=F32, precision=lax.Precision.HIGHEST)


def _params(sem, vmem=VMEM_LIMIT):
    return pltpu.CompilerParams(dimension_semantics=sem, vmem_limit_bytes=vmem)


def _sigmoid(x):
    return 0.5 * jnp.tanh(0.5 * x) + 0.5


def _softplus(x):
    return jnp.maximum(x, 0.0) + jnp.log1p(jnp.exp(-jnp.abs(x)))


def _row_acc(acc_ref, val):
    acc_ref[0:1, :] += jnp.sum(val, axis=0, keepdims=True)


def _rms_bwd(dh, xh, r, gain):
    dxh = dh * gain
    return r * (dxh - xh * jnp.mean(dxh * xh, axis=-1, keepdims=True))


def _before_halo(tb):
    return lambda i: (jnp.maximum(i * (tb // 8) - 1, 0), 0)


def _after_halo(tb, n_rows):
    last = n_rows // 8 - 1
    return lambda i: (jnp.minimum((i + 1) * (tb // 8), last), 0)


def _rows_from(xc, offset, tb):
    part = offset % 8
    if part:
        xc = pltpu.roll(xc, xc.shape[0] - part, 0)
    return xc[offset - part:offset - part + tb, :]


def _taps(xc, w, n_taps, tb, first):
    out = w[0:1, :] * _rows_from(xc, first, tb)
    for j in range(1, n_taps):
        out = out + w[j:j + 1, :] * _rows_from(xc, first + j, tb)
    return out


W_Z = QKV
W_BD = W_Z + DN_WIDTH
W_SC = W_BD + 2 * HEADS

def _w_in_cols(shards, lo, hi):
    pieces = []
    for s in range(N_CHIPS):
        a, b = max(lo, IN_SHARD * s), min(hi, IN_SHARD * (s + 1))
        if a < b:
            pieces.append(shards[s][:, a - IN_SHARD * s:b - IN_SHARD * s])
    return pieces[0] if len(pieces) == 1 else jnp.concatenate(pieces, axis=1)


def _in_proj(x, g1, land_a, cw, al_row, dt_row):
    T = x.shape[0]
    tb = 256

    def body(x_ref, g_ref, w_ref, cw_ref, al_ref, dt_ref,
             qkv_ref, z_ref, sc_ref, bd_ref, h_ref, q_ref, k_ref, v_ref, bg_ref, tail_ref):
        @pl.when(pl.program_id(0) == 0)
        def _():
            tail_ref[...] = jnp.zeros_like(tail_ref)

        xv = x_ref[...]
        h = (xv * lax.rsqrt(jnp.mean(xv * xv, axis=-1, keepdims=True) + EPS) * g_ref[...]).astype(BF16)
        shards = [jnp.dot(h, w_ref[s], preferred_element_type=F32) for s in range(N_CHIPS)]
        qkv = _w_in_cols(shards, 0, W_Z)
        bd = jnp.concatenate([_w_in_cols(shards, W_BD, W_SC), jnp.zeros((tb, LANES - 2 * HEADS), F32)], axis=1)
        qkv_ref[...] = qkv
        z_ref[...] = _w_in_cols(shards, W_Z, W_BD)
        bd_ref[...] = bd
        sc_ref[...] = _w_in_cols(shards, W_SC, W_IN_COLS)
        h_ref[...] = h
        halo = tail_ref[...]
        tail_ref[...] = qkv[tb - 8:, :]
        _, _, _, a = _dn_act(qkv, halo, cw_ref[...], tb)
        for hd in range(HEADS):
            sl = slice(HEAD_DIM * hd, HEAD_DIM * (hd + 1))
            qs = a[:, sl]
            q_ref[:, sl] = qs * (lax.rsqrt(jnp.sum(qs * qs, axis=-1, keepdims=True) + EPS) * Q_SCALE)
            ks = a[:, DN_WIDTH + HEAD_DIM * hd:DN_WIDTH + HEAD_DIM * (hd + 1)]
            k_ref[:, sl] = ks * lax.rsqrt(jnp.sum(ks * ks, axis=-1, keepdims=True) + EPS)
        v_ref[...] = a[:, 2 * DN_WIDTH:]
        gates = _gates(bd, al_ref[...], dt_ref[...])
        lane = lax.broadcasted_iota(jnp.int32, gates.shape, 1)
        bg_ref[...] = jnp.where(lane < HEADS, gates, _mm32(_chunk_cumsum_matrix(tb), gates))

    tok = lambda w: pl.BlockSpec((tb, w), lambda i: (i, 0))
    full = lambda t: pl.BlockSpec(t.shape, lambda i: (0, 0))
    return pl.pallas_call(
        body, name="in_proj", grid=(T // tb,),
        in_specs=[tok(D_MODEL), full(g1), _shard_rows(land_a, 0, D_MODEL), full(cw), full(al_row), full(dt_row)],
        out_specs=[tok(QKV), tok(DN_WIDTH), tok(3 * SC_WIDTH), tok(LANES), tok(D_MODEL),
                   tok(DN_WIDTH), tok(DN_WIDTH), tok(DN_WIDTH), tok(LANES)],
        out_shape=[jax.ShapeDtypeStruct((T, QKV), F32), jax.ShapeDtypeStruct((T, DN_WIDTH), F32),
                   jax.ShapeDtypeStruct((T, 3 * SC_WIDTH), F32), jax.ShapeDtypeStruct((T, LANES), F32),
                   jax.ShapeDtypeStruct((T, D_MODEL), BF16)]
        + [jax.ShapeDtypeStruct((T, DN_WIDTH), F32)] * 3 + [jax.ShapeDtypeStruct((T, LANES), F32)],
        scratch_shapes=[pltpu.VMEM((8, QKV), F32)],
        compiler_params=_params(("arbitrary",)),
    )(x, g1, land_a, cw, al_row, dt_row)


def _dn_act(pre, halo, cw, tb):
    xc = jnp.concatenate([halo, pre], axis=0)
    c = _taps(xc, cw, 4, tb, 5)
    sg = _sigmoid(c)
    return xc, c, sg, c * sg


def _gates(bd, al_row, dt_row):
    lane = lax.broadcasted_iota(jnp.int32, bd.shape, 1)
    beta = _sigmoid(bd)
    g = -jnp.exp(al_row) * _softplus(bd + dt_row)
    return jnp.where(lane < HEADS, beta, jnp.where(lane < 2 * HEADS, g, 0.0))


def _chunk_masks():
    row = lax.broadcasted_iota(jnp.int32, (CHUNK, CHUNK), 0)
    col = lax.broadcasted_iota(jnp.int32, (CHUNK, CHUNK), 1)
    return row >= col, row > col


def _chunk_cumsum_matrix(n):
    row = lax.broadcasted_iota(jnp.int32, (n, n), 0)
    col = lax.broadcasted_iota(jnp.int32, (n, n), 1)
    return jnp.logical_and(row >= col, row // CHUNK == col // CHUNK).astype(F32)


def _chunk_units(q_ref, k_ref, v_ref, bg_ref, rows):
    bgc = bg_ref[rows, :]
    bg_t = bgc.T
    qv, kv, vv = q_ref[rows, :], k_ref[rows, :], v_ref[rows, :]
    units = []
    for h in range(HEADS):
        sl = slice(HEAD_DIM * h, HEAD_DIM * (h + 1))
        units.append((qv[:, sl], kv[:, sl], vv[:, sl], bgc[:, h:h + 1], bgc[:, HEADS + h:HEADS + h + 1],
                      bg_t[HEADS + h:HEADS + h + 1, :]))
    return units


def _units_local(units, masks, xms=None):
    causal, strict = masks
    pre = []
    for q, k, v, beta, gc, gr in units:
        kb = k * beta
        eg = jnp.exp(gc)
        g_last = gc[CHUNK - 1:CHUNK, :]
        ek = jnp.exp(g_last - gc)
        pre.append(dict(q=q, k=k, v=v, beta=beta, decay=jnp.exp(jnp.where(causal, gc - gr, -1e30)), kb=kb, vb=v * beta,
                        eg=eg, kbg=kb * eg, ek=ek, gl=jnp.exp(g_last), q_dec=q * eg, k_dec=k * ek))
    both = [_mm(jnp.concatenate([p["kb"], p["q"]], axis=0), p["k"], NT) for p in pre]
    for p, b in zip(pre, both):
        p["low"] = jnp.where(strict, b[:CHUNK] * p["decay"], 0.0)
        p["qk"] = jnp.where(causal, b[CHUNK:] * p["decay"], 0.0)
    xs = xms
    if xs is None:
        xs = [-p["low"] for p in pre]
        pw = [_mm(p["low"], p["low"]) for p in pre]
        for _ in range(4):
            both = [_mm(jnp.concatenate([pp, x], axis=0), pp) for pp, x in zip(pw, xs)]
            xs = [x + pp + b[CHUNK:] for x, pp, b in zip(xs, pw, both)]
            pw = [b[:CHUNK] for b in both]
        last = [_mm(x, pp) for x, pp in zip(xs, pw)]
        xs = [x + pp + b for x, pp, b in zip(xs, pw, last)]
    uw = [_mm(x, jnp.concatenate([p["vb"], p["kbg"]], axis=1)) for x, p in zip(xs, pre)]
    for p, x, b in zip(pre, xs, uw):
        p["xm"] = x
        p["u"] = p["vb"] + b[:, :HEAD_DIM]
        p["w"] = p["kbg"] + b[:, HEAD_DIM:]
    return pre


FWD_GROUP = 8
BWD_GROUP = 8


def _delta_fwd(q, k, v, bg):
    T = q.shape[0]
    tb = 512
    n_chunk = tb // CHUNK

    def body(q_ref, k_ref, v_ref, bg_ref, o_ref, st_ref, xm_ref, s_ref):
        @pl.when(pl.program_id(0) == 0)
        def _():
            s_ref[...] = jnp.zeros_like(s_ref)

        masks = _chunk_masks()

        def group(gi, carry):
            rows = [pl.ds(pl.multiple_of((FWD_GROUP * gi + j) * CHUNK, CHUNK), CHUNK) for j in range(FWD_GROUP)]
            loc = _units_local(sum((_chunk_units(q_ref, k_ref, v_ref, bg_ref, r) for r in rows), []), masks)
            states = [s_ref[h] for h in range(HEADS)]
            for j in range(FWD_GROUP):
                lj = loc[HEADS * j:HEADS * (j + 1)]
                ws = [_mm(jnp.concatenate([p["w"], p["q_dec"]], axis=0), s) for p, s in zip(lj, states)]
                v_new = [p["u"] - b[:CHUNK] for p, b in zip(lj, ws)]
                intra = [_mm(p["qk"], vn) for p, vn in zip(lj, v_new)]
                upd = [_mm(p["k_dec"], vn, TN) for p, vn in zip(lj, v_new)]
                o_ref[rows[j], :] = jnp.concatenate([b[CHUNK:] + a for b, a in zip(ws, intra)], axis=1)
                for h in range(HEADS):
                    st_ref[FWD_GROUP * gi + j, h] = states[h]
                    xm_ref[FWD_GROUP * gi + j, h] = lj[h]["xm"]
                states = [p["gl"] * s + d for p, s, d in zip(lj, states, upd)]
            for h in range(HEADS):
                s_ref[h] = states[h]
            return carry

        lax.fori_loop(0, n_chunk // FWD_GROUP, group, 0)

    tok = lambda w: pl.BlockSpec((tb, w), lambda i: (i, 0))
    return pl.pallas_call(
        body, name="delta_fwd", grid=(T // tb,),
        in_specs=[tok(DN_WIDTH), tok(DN_WIDTH), tok(DN_WIDTH), tok(LANES)],
        out_specs=[tok(DN_WIDTH), pl.BlockSpec((n_chunk, HEADS, HEAD_DIM, HEAD_DIM), lambda i: (i, 0, 0, 0)),
                   pl.BlockSpec((n_chunk, HEADS, CHUNK, CHUNK), lambda i: (i, 0, 0, 0))],
        out_shape=[jax.ShapeDtypeStruct((T, DN_WIDTH), F32),
                   jax.ShapeDtypeStruct((T // CHUNK, HEADS, HEAD_DIM, HEAD_DIM), F32),
                   jax.ShapeDtypeStruct((T // CHUNK, HEADS, CHUNK, CHUNK), F32)],
        scratch_shapes=[pltpu.VMEM((HEADS, HEAD_DIM, HEAD_DIM), F32)],
        compiler_params=_params(("arbitrary",)),
    )(q, k, v, bg)


def _dn_out(o, z, gn):
    outs, ohs, rs = [], [], []
    for hh in range(HEADS):
        oh = o[:, HEAD_DIM * hh:HEAD_DIM * (hh + 1)]
        r = lax.rsqrt(jnp.mean(oh * oh, axis=-1, keepdims=True) + EPS)
        ohs.append(oh * r)
        rs.append(r)
    sz = _sigmoid(z)
    oh = jnp.concatenate(ohs, axis=1)
    gn4 = jnp.concatenate([gn] * HEADS, axis=1)
    return oh * gn4 * (z * sz), oh, rs, sz, gn4


def _sc_fwd(sc_in, halo, cw, tb):
    xc = jnp.concatenate([halo, sc_in], axis=0)
    u = xc[:, SC_WIDTH:2 * SC_WIDTH] * xc[:, 2 * SC_WIDTH:]
    cv = _taps(u, cw, 3, tb, 6)
    gate_b = sc_in[:, :SC_WIDTH]
    y = gate_b * cv
    gw = SC_WIDTH // SC_GROUPS
    yhs, rs = [], []
    for gi in range(SC_GROUPS):
        yg = y[:, gw * gi:gw * (gi + 1)]
        r = lax.rsqrt(jnp.mean(yg * yg, axis=-1, keepdims=True) + EPS)
        yhs.append(yg * r)
        rs.append(r)
    return u, cv, gate_b, jnp.concatenate(yhs, axis=1), rs


def _shard_rows(land, first, rows, single_buffer=False):
    assert first % rows == 0 and land.shape[0] == N_CHIPS
    mode = dict(pipeline_mode=pl.Buffered(1)) if single_buffer else {}
    return pl.BlockSpec((N_CHIPS, rows, land.shape[2]), lambda i: (0, first // rows, 0), **mode)


def _whole(w_ref):
    n, rows, cols = w_ref.shape
    return w_ref[...].reshape(n * rows, cols)


def _mix_ffn(o, z, sc_in, x, land_a, gn, scw, gs, g2, land_b):
    T = x.shape[0]
    tb = 256

    def body(o_ref, z_ref, sc_ref, halo_ref, x_ref, wo_ref, gn_ref, scw_ref, gs_ref, g2_ref, wgt_ref, wut_ref, wd_ref,
             x1_ref, mix_ref, x2_ref, a_ref, b_ref, h_ref):
        o_n = _dn_out(o_ref[...], z_ref[...], gn_ref[...])[0]
        halo = jnp.where(pl.program_id(0) > 0, halo_ref[...], 0.0)
        yh = _sc_fwd(sc_ref[...], halo, scw_ref[...], tb)[3]
        mix = jnp.concatenate([o_n, yh * gs_ref[...]], axis=1).astype(BF16)
        x1 = x_ref[...] + jnp.dot(mix, _whole(wo_ref), preferred_element_type=F32)
        x1_ref[...] = x1
        mix_ref[...] = mix
        r = lax.rsqrt(jnp.mean(x1 * x1, axis=-1, keepdims=True) + EPS)
        h = (x1 * r * g2_ref[...]).astype(BF16)
        a = lax.dot_general(h, _whole(wgt_ref), NT, preferred_element_type=F32)
        b = lax.dot_general(h, _whole(wut_ref), NT, preferred_element_type=F32)
        act = (a * _sigmoid(a) * b).astype(BF16)
        x2_ref[...] = x1 + jnp.dot(act, _whole(wd_ref), preferred_element_type=F32)
        a_ref[...] = a.astype(BF16)
        b_ref[...] = b.astype(BF16)
        h_ref[...] = h

    tok = lambda w: pl.BlockSpec((tb, w), lambda i: (i, 0))
    full = lambda t: pl.BlockSpec(t.shape, lambda i: (0, 0))
    once = lambda land, first, rows: _shard_rows(land, first, rows, single_buffer=True)
    return pl.pallas_call(
        body, name="mix_ffn", grid=(T // tb,),
        in_specs=[tok(DN_WIDTH), tok(DN_WIDTH), tok(3 * SC_WIDTH), pl.BlockSpec((8, 3 * SC_WIDTH), _before_halo(tb)),
                  tok(D_MODEL), once(land_a, A_OUT_AT, OUT_SHARD), full(gn), full(scw), full(gs), full(g2),
                  once(land_b, 0, FF_SHARD), once(land_b, FF_SHARD, FF_SHARD), once(land_b, 2 * FF_SHARD, FF_SHARD)],
        out_specs=[tok(D_MODEL), tok(D_MODEL), tok(D_MODEL), tok(D_FF), tok(D_FF), tok(D_MODEL)],
        out_shape=[jax.ShapeDtypeStruct((T, D_MODEL), F32), jax.ShapeDtypeStruct((T, D_MODEL), BF16),
                   jax.ShapeDtypeStruct((T, D_MODEL), F32), jax.ShapeDtypeStruct((T, D_FF), BF16),
                   jax.ShapeDtypeStruct((T, D_FF), BF16), jax.ShapeDtypeStruct((T, D_MODEL), BF16)],
        compiler_params=_params(("parallel",)),
    )(o, z, sc_in, sc_in, x, land_a, gn, scw, gs, g2, land_b, land_b, land_b)


def _loss_head(x, gf, target):
    T = x.shape[0]
    tb = 512

    def body(x_ref, g_ref, t_ref, dx_ref, dxb_ref, loss_ref, dg_ref):
        @pl.when(pl.program_id(0) == 0)
        def _():
            loss_ref[...] = jnp.zeros_like(loss_ref)
            dg_ref[...] = jnp.zeros_like(dg_ref)

        xv = x_ref[...]
        r = lax.rsqrt(jnp.mean(xv * xv, axis=-1, keepdims=True) + EPS)
        xh = xv * r
        err = xh * g_ref[...] - t_ref[...]
        per_tok = jnp.mean(err * err, axis=-1, keepdims=True)
        loss_ref[...] += 0.5 * jnp.sum(per_tok, axis=0, keepdims=True)
        dy = err * (1.0 / D_MODEL)
        _row_acc(dg_ref, dy * xh)
        dx = _rms_bwd(dy, xh, r, g_ref[...])
        dx_ref[...] = dx
        dxb_ref[...] = dx.astype(BF16)

    tok = pl.BlockSpec((tb, D_MODEL), lambda i: (i, 0))
    return pl.pallas_call(
        body, name="loss_head", grid=(T // tb,),
        in_specs=[tok, pl.BlockSpec(gf.shape, lambda i: (0, 0)), tok],
        out_specs=[tok, tok, pl.BlockSpec((8, LANES), lambda i: (0, 0)), pl.BlockSpec((8, D_MODEL), lambda i: (0, 0))],
        out_shape=[jax.ShapeDtypeStruct((T, D_MODEL), F32), jax.ShapeDtypeStruct((T, D_MODEL), BF16),
                   jax.ShapeDtypeStruct((8, LANES), F32), jax.ShapeDtypeStruct((8, D_MODEL), F32)],
        compiler_params=_params(("arbitrary",)),
    )(x, gf, target)


def _ffn_bwd(dx2, x1, a, b, g2, land_b):
    T = x1.shape[0]
    tb = 256

    def body(dx2_ref, x_ref, a_ref, b_ref, g_ref, wgt_ref, wut_ref, wd_ref,
             dx1_ref, da_ref, db_ref, act_ref, dg_ref):
        @pl.when(pl.program_id(0) == 0)
        def _():
            dg_ref[...] = jnp.zeros_like(dg_ref)

        dx2v = dx2_ref[...]
        av = a_ref[...].astype(F32)
        bv = b_ref[...].astype(F32)
        dact = _mm(dx2v, _whole(wd_ref), NT)
        sa = _sigmoid(av)
        silu = av * sa
        da = (dact * bv * (sa * (1.0 + av * (1.0 - sa)))).astype(BF16)
        db = (dact * silu).astype(BF16)
        dh = _mm(da, _whole(wgt_ref)) + _mm(db, _whole(wut_ref))
        xv = x_ref[...]
        r = lax.rsqrt(jnp.mean(xv * xv, axis=-1, keepdims=True) + EPS)
        xh = xv * r
        _row_acc(dg_ref, dh * xh)
        dx1 = dx2v + _rms_bwd(dh, xh, r, g_ref[...])
        dx1_ref[...] = dx1
        da_ref[...] = da
        db_ref[...] = db
        act_ref[...] = (silu * bv).astype(BF16)

    tok = lambda w: pl.BlockSpec((tb, w), lambda i: (i, 0))
    return pl.pallas_call(
        body, name="ffn_bwd", grid=(T // tb,),
        in_specs=[tok(D_MODEL), tok(D_MODEL), tok(D_FF), tok(D_FF), pl.BlockSpec(g2.shape, lambda i: (0, 0)),
                  _shard_rows(land_b, 0, FF_SHARD), _shard_rows(land_b, FF_SHARD, FF_SHARD),
                  _shard_rows(land_b, 2 * FF_SHARD, FF_SHARD)],
        out_specs=[tok(D_MODEL), tok(D_FF), tok(D_FF), tok(D_FF), pl.BlockSpec((8, D_MODEL), lambda i: (0, 0))],
        out_shape=[jax.ShapeDtypeStruct((T, D_MODEL), F32)]
        + [jax.ShapeDtypeStruct((T, D_FF), BF16)] * 3 + [jax.ShapeDtypeStruct((8, D_MODEL), F32)],
        compiler_params=_params(("arbitrary",)),
    )(dx2, x1, a, b, g2, land_b, land_b, land_b)


def _wgrad_share(a, b, parts, first, name):
    T = b.shape[0]
    rows = a.shape[1] // N_CHIPS
    assert first % rows == 0 and b.shape[1] == parts.shape[2]
    bk = min(T, 1024)
    n_k = T // bk
    group = 2
    assert (group * rows) % LANES == 0

    def body(a_ref, b_ref, parts_ref, o_ref, acc_ref):
        kk = pl.program_id(1)

        @pl.when(kk == 0)
        def _():
            acc_ref[...] = jnp.zeros_like(acc_ref)

        acc_ref[...] += lax.dot_general(a_ref[...], b_ref[...], TN, preferred_element_type=F32)

        @pl.when(kk == n_k - 1)
        def _():
            for s in range(group):
                o_ref[s] = acc_ref[rows * s:rows * (s + 1), :].astype(BF16)

    return pl.pallas_call(
        body, name=name, grid=(N_CHIPS // group, n_k),
        in_specs=[pl.BlockSpec((bk, group * rows), lambda i, kk: (kk, i)),
                  pl.BlockSpec((bk, b.shape[1]), lambda i, kk: (kk, 0)), _ANY],
        out_specs=pl.BlockSpec((group, rows, b.shape[1]), lambda i, kk: (i, first // rows, 0)),
        out_shape=jax.ShapeDtypeStruct(parts.shape, BF16),
        scratch_shapes=[pltpu.VMEM((group * rows, b.shape[1]), F32)],
        input_output_aliases={2: 0},
        compiler_params=_params(("parallel", "arbitrary")),
    )(a, b, parts)


def _mix_out_bwd(dx1, o, z, sc_in, land_a, gn, scw, gs):
    T = dx1.shape[0]
    tb = 256

    def body(dx_ref, o_ref, z_ref, sc_ref, halo_ref, w_ref, gn_ref, scw_ref, gs_ref,
             do_ref, dz_ref, dgb_ref, dcv_ref, dxb_ref, dgn_ref, dgs_ref, dscw_ref):
        @pl.when(pl.program_id(0) == 0)
        def _():
            dgn_ref[...] = jnp.zeros_like(dgn_ref)
            dgs_ref[...] = jnp.zeros_like(dgs_ref)
            dscw_ref[...] = jnp.zeros_like(dscw_ref)

        dx_bf16 = dx_ref[...].astype(BF16)
        dxb_ref[...] = dx_bf16
        dmix = lax.dot_general(dx_bf16, _whole(w_ref), NT, preferred_element_type=F32)
        don = dmix[:, :DN_WIDTH]
        dosc = dmix[:, DN_WIDTH:]
        zv = z_ref[...]
        _, oh, rs, sz, gn4 = _dn_out(o_ref[...], zv, gn_ref[...])
        silu_z = zv * sz
        dgn_full = don * oh * silu_z
        dgn_ref[0:1, :] += jnp.sum(sum(dgn_full[:, HEAD_DIM * hh:HEAD_DIM * (hh + 1)] for hh in range(HEADS)),
                                   axis=0, keepdims=True)
        dz_ref[...] = (don * oh * gn4 * (sz * (1.0 + zv * (1.0 - sz)))).astype(BF16)
        t = don * gn4 * silu_z
        for hh in range(HEADS):
            sl = slice(HEAD_DIM * hh, HEAD_DIM * (hh + 1))
            th, ohh = t[:, sl], oh[:, sl]
            do_ref[:, sl] = rs[hh] * (th - ohh * jnp.mean(th * ohh, axis=-1, keepdims=True))
        halo = jnp.where(pl.program_id(0) > 0, halo_ref[...], 0.0)
        u, cv, gate_b, yh, rys = _sc_fwd(sc_ref[...], halo, scw_ref[...], tb)
        _row_acc(dgs_ref, dosc * yh)
        ty = dosc * gs_ref[...]
        gw = SC_WIDTH // SC_GROUPS
        dys = []
        for gi in range(SC_GROUPS):
            sl = slice(gw * gi, gw * (gi + 1))
            tg, yg = ty[:, sl], yh[:, sl]
            dys.append(rys[gi] * (tg - yg * jnp.mean(tg * yg, axis=-1, keepdims=True)))
        dy = jnp.concatenate(dys, axis=1)
        dgb_ref[...] = dy * cv
        dcv = dy * gate_b
        dcv_ref[...] = dcv
        for j in range(3):
            dscw_ref[j:j + 1, :] += jnp.sum(dcv * _rows_from(u, 6 + j, tb), axis=0, keepdims=True)

    tok = lambda w: pl.BlockSpec((tb, w), lambda i: (i, 0))
    full = lambda t: pl.BlockSpec(t.shape, lambda i: (0, 0))
    acc = lambda w: pl.BlockSpec((8, w), lambda i: (0, 0))
    return pl.pallas_call(
        body, name="mix_out_bwd", grid=(T // tb,),
        in_specs=[tok(D_MODEL), tok(DN_WIDTH), tok(DN_WIDTH), tok(3 * SC_WIDTH),
                  pl.BlockSpec((8, 3 * SC_WIDTH), _before_halo(tb)), _shard_rows(land_a, A_OUT_AT, OUT_SHARD),
                  full(gn), full(scw), full(gs)],
        out_specs=[tok(DN_WIDTH), tok(DN_WIDTH), tok(SC_WIDTH), tok(SC_WIDTH), tok(D_MODEL),
                   acc(HEAD_DIM), acc(SC_WIDTH), acc(SC_WIDTH)],
        out_shape=[jax.ShapeDtypeStruct((T, DN_WIDTH), F32), jax.ShapeDtypeStruct((T, DN_WIDTH), BF16),
                   jax.ShapeDtypeStruct((T, SC_WIDTH), F32), jax.ShapeDtypeStruct((T, SC_WIDTH), F32),
                   jax.ShapeDtypeStruct((T, D_MODEL), BF16),
                   jax.ShapeDtypeStruct((8, HEAD_DIM), F32), jax.ShapeDtypeStruct((8, SC_WIDTH), F32),
                   jax.ShapeDtypeStruct((8, SC_WIDTH), F32)],
        compiler_params=_params(("arbitrary",)),
    )(dx1, o, z, sc_in, sc_in, land_a, gn, scw, gs)


def _delta_bwd(q, k, v, bg, states, xms, do):
    T = q.shape[0]
    tb = 512
    n_chunk = tb // CHUNK
    nb = T // tb

    def body(q_ref, k_ref, v_ref, bg_ref, st_ref, xm_ref, do_ref, dq_ref, dk_ref, dv_ref, dbg_ref, ds_ref):
        @pl.when(pl.program_id(0) == 0)
        def _():
            ds_ref[...] = jnp.zeros_like(ds_ref)

        masks = _chunk_masks()
        causal, strict = masks
        lane = lax.broadcasted_iota(jnp.int32, (CHUNK, LANES), 1)
        last_row = lax.broadcasted_iota(jnp.int32, (CHUNK, 1), 0) == CHUNK - 1
        cat = jnp.concatenate
        heads = range(HEADS)

        def open_chunk(ci, loc):
            rows = pl.ds(pl.multiple_of(ci * CHUNK, CHUNK), CHUNK)
            dov = do_ref[rows, :]
            return dict(rows=rows, loc=loc, do=[dov[:, HEAD_DIM * h:HEAD_DIM * (h + 1)] for h in heads],
                        state=[st_ref[ci, h] for h in heads])

        def a_free(c):
            loc, do, state = c["loc"], c["do"], c["state"]
            w_s = [_mm(p["w"], s) for p, s in zip(loc, state)]
            c["dq_dec"] = [_mm(d, s, NT) for d, s in zip(do, state)]
            c["qk_do"] = [_mm(p["qk"], d, TN) for p, d in zip(loc, do)]
            c["qd_do"] = [_mm(p["q_dec"], d, TN) for p, d in zip(loc, do)]
            c["v_new"] = [p["u"] - t for p, t in zip(loc, w_s)]
            c["dqk"] = [jnp.where(causal, _mm(d, vn, NT), 0.0) for d, vn in zip(do, c["v_new"])]

        def a_state(c, ds_next):
            c["ds_next"] = ds_next
            kd_ds = [_mm(p["k_dec"], d) for p, d in zip(c["loc"], ds_next)]
            c["dk_dec"] = [_mm(vn, d, NT) for vn, d in zip(c["v_new"], ds_next)]
            c["dv_new"] = [a + b for a, b in zip(c["qk_do"], kd_ds)]

        def b_state(c):
            loc = c["loc"]
            w_dv = [_mm(p["w"], dvn, TN) for p, dvn in zip(loc, c["dv_new"])]
            c["dw"] = [-_mm(dvn, s, NT) for dvn, s in zip(c["dv_new"], c["state"])]
            return [loc[h]["gl"] * c["ds_next"][h] + c["qd_do"][h] - w_dv[h] for h in heads]

        def c_solve(c):
            loc, dv_new, dw = c["loc"], c["dv_new"], c["dw"]
            c["dtm"] = [_mm(cat([dvn, d], axis=1), cat([p["vb"], p["kbg"]], axis=1), NT) for dvn, d, p in zip(dv_new, dw, loc)]
            x_t = [_mm(p["xm"], cat([dvn, d], axis=1), TN) for p, dvn, d in zip(loc, dv_new, dw)]
            c["dvb"] = [dvn + t[:, :HEAD_DIM] for dvn, t in zip(dv_new, x_t)]
            c["dkbg"] = [d + t[:, HEAD_DIM:] for d, t in zip(dw, x_t)]

        def d_solve(c):
            c["y"] = [t + _mm(p["xm"], t, TN) for p, t in zip(c["loc"], c["dtm"])]

        def e_solve(c):
            c["dlow"] = [jnp.where(strict, -(t + _mm(t, p["xm"], NT)), 0.0) for p, t in zip(c["loc"], c["y"])]

        def f_close(c):
            loc, rows = c["loc"], c["rows"]
            dmm = [d * p["decay"] for d, p in zip(c["dlow"], loc)]
            dnn = [d * p["decay"] for d, p in zip(c["dqk"], loc)]
            by_k = [_mm(cat([a, b], axis=0), p["k"]) for a, b, p in zip(dmm, dnn, loc)]
            dk_mm = [_mm(cat([a, b], axis=0), cat([p["kb"], p["q"]], axis=0), TN) for a, b, p in zip(dmm, dnn, loc)]
            dq_out, dk_out, dv_out = [], [], []
            dbeta_all = jnp.zeros((CHUNK, LANES), F32)
            dgc_all = jnp.zeros((CHUNK, LANES), F32)
            for h in heads:
                p = loc[h]
                dkb = by_k[h][:CHUNK] + c["dkbg"][h] * p["eg"]
                dq_out.append(by_k[h][CHUNK:] + c["dq_dec"][h] * p["eg"])
                dk_out.append(dk_mm[h] + c["dk_dec"][h] * p["ek"] + dkb * p["beta"])
                dv_out.append(c["dvb"][h] * p["beta"])
                dbeta = jnp.sum(dkb * p["k"] + c["dvb"][h] * p["v"], axis=1, keepdims=True)
                e = c["dlow"][h] * p["low"] + c["dqk"][h] * p["qk"]
                kd = jnp.sum(c["dk_dec"][h] * p["k_dec"], axis=1, keepdims=True)
                dgc = (jnp.sum(e, axis=1, keepdims=True) - jnp.sum(e.T, axis=1, keepdims=True)
                       + jnp.sum(c["dq_dec"][h] * p["q_dec"], axis=1, keepdims=True) - kd
                       + jnp.sum(c["dkbg"][h] * p["kbg"], axis=1, keepdims=True))
                dgl = jnp.sum(jnp.sum(c["ds_next"][h] * c["state"][h], axis=1, keepdims=True), axis=0, keepdims=True)
                d_last = jnp.sum(kd, axis=0, keepdims=True) + dgl * p["gl"]
                dgc = dgc + jnp.where(last_row, d_last, 0.0)
                dbeta_all = jnp.where(lane == h, dbeta, dbeta_all)
                dgc_all = jnp.where(lane == h + HEADS, dgc, dgc_all)
            dq_ref[rows, :] = cat(dq_out, axis=1)
            dk_ref[rows, :] = cat(dk_out, axis=1)
            dv_ref[rows, :] = cat(dv_out, axis=1)
            dbg_ref[rows, :] = dbeta_all + dgc_all

        def group(gj, carry):
            first = n_chunk - 1 - BWD_GROUP * gj
            ids = [first - j for j in range(BWD_GROUP)]
            rows = [pl.ds(pl.multiple_of(ci * CHUNK, CHUNK), CHUNK) for ci in ids]
            loc = _units_local(sum((_chunk_units(q_ref, k_ref, v_ref, bg_ref, r) for r in rows), []), masks,
                               xms=[xm_ref[ci, h] for ci in ids for h in heads])
            chunks = [open_chunk(ci, loc[HEADS * j:HEADS * (j + 1)]) for j, ci in enumerate(ids)]
            for c in chunks:
                a_free(c)
            ds_cur = [ds_ref[h] for h in heads]
            later = (c_solve, d_solve, e_solve, f_close)
            for t in range(2 * (BWD_GROUP - 1) + 2 + len(later)):
                for j, c in enumerate(chunks):
                    stage = t - 2 * j
                    if stage == 0:
                        a_state(c, ds_cur)
                    elif stage == 1:
                        ds_cur = b_state(c)
                    elif 2 <= stage < 2 + len(later):
                        later[stage - 2](c)
            for h in heads:
                ds_ref[h] = ds_cur[h]
            return carry

        lax.fori_loop(0, n_chunk // BWD_GROUP, group, 0)

    tok = lambda w: pl.BlockSpec((tb, w), lambda i: (nb - 1 - i, 0))
    return pl.pallas_call(
        body, name="delta_bwd", grid=(nb,),
        in_specs=[tok(DN_WIDTH), tok(DN_WIDTH), tok(DN_WIDTH), tok(LANES),
                  pl.BlockSpec((n_chunk, HEADS, HEAD_DIM, HEAD_DIM), lambda i: (nb - 1 - i, 0, 0, 0)),
                  pl.BlockSpec((n_chunk, HEADS, CHUNK, CHUNK), lambda i: (nb - 1 - i, 0, 0, 0)), tok(DN_WIDTH)],
        out_specs=[tok(DN_WIDTH), tok(DN_WIDTH), tok(DN_WIDTH), tok(LANES)],
        out_shape=[jax.ShapeDtypeStruct((T, DN_WIDTH), F32)] * 3 + [jax.ShapeDtypeStruct((T, LANES), F32)],
        scratch_shapes=[pltpu.VMEM((HEADS, HEAD_DIM, HEAD_DIM), F32)],
        compiler_params=_params(("arbitrary",)),
    )(q, k, v, bg, states, xms, do)


def _dn_prep_back(dq, dk, dv, dbg, pre, halo, cw, bd, al_row, dt_row, tb):
    xc, c, sg, a = _dn_act(pre, halo, cw, tb)
    dsilu = sg * (1.0 + c * (1.0 - sg))
    pieces = [None] * (2 * HEADS)
    for hd in range(HEADS):
        sl = slice(HEAD_DIM * hd, HEAD_DIM * (hd + 1))
        for which, (base, grad, scale) in enumerate(((0, dq, Q_SCALE), (DN_WIDTH, dk, 1.0))):
            sa = slice(base + HEAD_DIM * hd, base + HEAD_DIM * (hd + 1))
            raw = a[:, sa]
            r = lax.rsqrt(jnp.sum(raw * raw, axis=-1, keepdims=True) + EPS)
            nrm = raw * r
            gn_ = grad[:, sl] * scale
            pieces[which * HEADS + hd] = r * (gn_ - nrm * jnp.sum(gn_ * nrm, axis=-1, keepdims=True)) * dsilu[:, sa]
    dc = jnp.concatenate(pieces + [dv * dsilu[:, 2 * DN_WIDTH:]], axis=1)
    dcw_rows = [jnp.sum(dc * _rows_from(xc, 5 + j, tb), axis=0, keepdims=True) for j in range(4)]
    lane = lax.broadcasted_iota(jnp.int32, bd.shape, 1)
    is_b = lane < HEADS
    is_g = jnp.logical_and(lane >= HEADS, lane < 2 * HEADS)
    dbgv = jnp.where(is_b, dbg, _mm32(_chunk_cumsum_matrix(tb), dbg, TN))
    beta = _sigmoid(bd)
    neg_a = -jnp.exp(al_row)
    pre_sp = bd + dt_row
    g = neg_a * _softplus(pre_sp)
    da_in = dbgv * neg_a * _sigmoid(pre_sp)
    dbd = jnp.where(is_b, dbgv * beta * (1.0 - beta), jnp.where(is_g, da_in, 0.0)).astype(BF16)
    dal_row = jnp.sum(jnp.where(is_g, dbgv * g, 0.0), axis=0, keepdims=True)
    ddt_row = jnp.sum(jnp.where(is_g, da_in, 0.0), axis=0, keepdims=True)
    return dc, dbd, dcw_rows, dal_row, ddt_row


def _dp_of_chip(dqkv, dz, dbd, dsc, s):
    lo, hi = IN_SHARD * s, IN_SHARD * (s + 1)
    pieces = []
    for w_at, w_end, block in ((0, W_Z, dqkv), (W_Z, W_BD, dz), (W_BD, W_SC, dbd), (W_SC, W_IN_COLS, dsc)):
        a, b = max(lo, w_at), min(hi, w_end)
        if a < b:
            pieces.append(block[:, a - w_at:b - w_at])
    pieces.append(jnp.zeros((dqkv.shape[0], D_MODEL - IN_SHARD), dqkv.dtype))
    return jnp.concatenate(pieces, axis=1)


def _in_proj_bwd(dq, dk, dv, dbg, qkv, bd, al_row, dt_row, dcv, dgb, sc_in, dz, cw, scw, dx1, x, g1, land_a):
    T = x.shape[0]
    tb = 256
    nb = T // tb

    def body(dq_ref, dk_ref, dv_ref, dbg_ref, pre_ref, pre_halo_ref, bd_ref, al_ref, dt_ref,
             dcv_ref, dcv_halo_ref, dgb_ref, sc_ref, dz_ref, cw_ref, scw_ref, dx1_ref, x_ref, g_ref, w_ref,
             dx_ref, dxb_ref, dps_ref, dg_ref, dcw_ref, dal_ref, ddt_ref, head_ref):
        @pl.when(pl.program_id(0) == 0)
        def _():
            for ref in (dg_ref, dcw_ref, dal_ref, ddt_ref, head_ref):
                ref[...] = jnp.zeros_like(ref)

        block = nb - 1 - pl.program_id(0)
        last = block == nb - 1
        pre_halo = jnp.where(block > 0, pre_halo_ref[...], 0.0)
        dc, dbd, dcw_rows, dal_row, ddt_row = _dn_prep_back(
            dq_ref[...], dk_ref[...], dv_ref[...], dbg_ref[...], pre_ref[...], pre_halo, cw_ref[...], bd_ref[...],
            al_ref[...], dt_ref[...], tb)
        for j in range(4):
            dcw_ref[j:j + 1, :] += dcw_rows[j]
        dal_ref[0:1, :] += dal_row
        ddt_ref[0:1, :] += ddt_row
        xc = jnp.concatenate([dc, head_ref[...]], axis=0)
        head_ref[...] = dc[0:8, :]
        w4 = cw_ref[...]
        dqkv = w4[3:4, :] * xc[0:tb, :]
        for j in range(3):
            dqkv = dqkv + w4[j:j + 1, :] * _rows_from(xc, 3 - j, tb)
        yc = jnp.concatenate([dcv_ref[...], jnp.where(last, 0.0, dcv_halo_ref[...])], axis=0)
        w3 = scw_ref[...]
        du = w3[2:3, :] * yc[0:tb, :] + w3[1:2, :] * _rows_from(yc, 1, tb) + w3[0:1, :] * _rows_from(yc, 2, tb)
        sc = sc_ref[...]
        dsc = jnp.concatenate([dgb_ref[...], du * sc[:, 2 * SC_WIDTH:], du * sc[:, SC_WIDTH:2 * SC_WIDTH]], axis=1)
        blocks = (dqkv.astype(BF16), dz_ref[...], dbd, dsc.astype(BF16))
        dh = jnp.zeros((tb, D_MODEL), F32)
        for s in range(N_CHIPS):
            dps = _dp_of_chip(*blocks, s)
            dps_ref[:, D_MODEL * s:D_MODEL * (s + 1)] = dps
            dh = dh + lax.dot_general(dps, w_ref[s], NT, preferred_element_type=F32)
        xv = x_ref[...]
        r = lax.rsqrt(jnp.mean(xv * xv, axis=-1, keepdims=True) + EPS)
        xh = xv * r
        _row_acc(dg_ref, dh * xh)
        dx = dx1_ref[...] + _rms_bwd(dh, xh, r, g_ref[...])
        dx_ref[...] = dx
        dxb_ref[...] = dx.astype(BF16)

    tok = lambda w: pl.BlockSpec((tb, w), lambda i: (nb - 1 - i, 0))
    full = lambda t: pl.BlockSpec(t.shape, lambda i: (0, 0))
    acc = lambda w: pl.BlockSpec((8, w), lambda i: (0, 0))
    before = lambda w: pl.BlockSpec((8, w), lambda i: _before_halo(tb)(nb - 1 - i))
    after = lambda w: pl.BlockSpec((8, w), lambda i: _after_halo(tb, T)(nb - 1 - i))
    return pl.pallas_call(
        body, name="in_proj_bwd", grid=(nb,),
        in_specs=[tok(DN_WIDTH), tok(DN_WIDTH), tok(DN_WIDTH), tok(LANES), tok(QKV), before(QKV), tok(LANES),
                  full(al_row), full(dt_row), tok(SC_WIDTH), after(SC_WIDTH), tok(SC_WIDTH), tok(3 * SC_WIDTH),
                  tok(DN_WIDTH), full(cw), full(scw), tok(D_MODEL), tok(D_MODEL), full(g1), _shard_rows(land_a, 0, D_MODEL)],
        out_specs=[tok(D_MODEL), tok(D_MODEL), tok(N_CHIPS * D_MODEL), acc(D_MODEL), acc(QKV), acc(LANES), acc(LANES)],
        out_shape=[jax.ShapeDtypeStruct((T, D_MODEL), F32), jax.ShapeDtypeStruct((T, D_MODEL), BF16),
                   jax.ShapeDtypeStruct((T, N_CHIPS * D_MODEL), BF16), jax.ShapeDtypeStruct((8, D_MODEL), F32),
                   jax.ShapeDtypeStruct((8, QKV), F32), jax.ShapeDtypeStruct((8, LANES), F32),
                   jax.ShapeDtypeStruct((8, LANES), F32)],
        scratch_shapes=[pltpu.VMEM((8, QKV), F32)],
        compiler_params=_params(("arbitrary",)),
    )(dq, dk, dv, dbg, qkv, qkv, bd, al_row, dt_row, dcv, dcv, dgb, sc_in, dz, cw, scw, dx1, x, g1, land_a)


def _wgrad_in_share(h, dps, parts, name):
    T = h.shape[0]
    bk = min(T, 1024)
    n_k = T // bk

    def body(a_ref, b_ref, parts_ref, o_ref, acc_ref):
        kk = pl.program_id(1)

        @pl.when(kk == 0)
        def _():
            acc_ref[...] = jnp.zeros_like(acc_ref)

        acc_ref[...] += lax.dot_general(a_ref[...], b_ref[...], TN, preferred_element_type=F32)

        @pl.when(kk == n_k - 1)
        def _():
            o_ref[0] = acc_ref[...].astype(BF16)

    return pl.pallas_call(
        body, name=name, grid=(N_CHIPS, n_k),
        in_specs=[pl.BlockSpec((bk, D_MODEL), lambda j, kk: (kk, 0)), pl.BlockSpec((bk, D_MODEL), lambda j, kk: (kk, j)), _ANY],
        out_specs=pl.BlockSpec((1, D_MODEL, D_MODEL), lambda j, kk: (j, 0, 0)),
        out_shape=jax.ShapeDtypeStruct(parts.shape, BF16),
        scratch_shapes=[pltpu.VMEM((D_MODEL, D_MODEL), F32)],
        input_output_aliases={2: 0},
        compiler_params=_params(("parallel", "arbitrary")),
    )(h, dps, parts)


def _pad_rows(a, rows=8):
    return jnp.pad(a, ((0, rows - a.shape[0]), (0, 0)))


def _gate_rows(a_log, dt_bias):
    put = lambda t: jnp.pad(t.reshape(1, HEADS), ((0, 0), (HEADS, LANES - 2 * HEADS)))
    return put(a_log), put(dt_bias)


def _mixer_fwd(x, p):
    qkv, z, sc_in, bd, h, q, k, v, bg = _in_proj(x, p["g1"], p["land_a"], p["cw"], p["al"], p["dt"])
    o, states, xms = _delta_fwd(q, k, v, bg)
    return dict(x=x, qkv=qkv, z=z, sc_in=sc_in, bd=bd, h=h, q=q, k=k, v=v, bg=bg, o=o, states=states, xms=xms)


def _tail_fwd(s, p, land_b):
    x1, mix, x2, a, b, h2 = _mix_ffn(s["o"], s["z"], s["sc_in"], s["x"], p["land_a"], p["gn"], p["scw"], p["gs"],
                                     p["g2"], land_b)
    return x2, dict(s, mix=mix), dict(x1=x1, a=a, b=b, h2=h2)


def _ffn_back(dx2, dx2_bf16, s, p, land_b):
    dx1, da, db, act, dg2 = _ffn_bwd(dx2, s["x1"], s["a"], s["b"], p["g2"], land_b)
    parts = lax.empty((N_CHIPS, B_ROWS, D_MODEL), BF16)
    parts = _wgrad_share(act, dx2_bf16, parts, 2 * FF_SHARD, "wgrad_down")
    parts = _wgrad_share(da, s["h2"], parts, 0, "wgrad_gate")
    parts = _wgrad_share(db, s["h2"], parts, FF_SHARD, "wgrad_up")
    return dx1, parts, dg2[0]


def _mixer_bwd(dx1, s, p):
    do, dz, dgb, dcv, dx1_bf16, dgn, dgs, dscw = _mix_out_bwd(dx1, s["o"], s["z"], s["sc_in"], p["land_a"], p["gn"],
                                                              p["scw"], p["gs"])
    dq, dk, dv, dbg = _delta_bwd(s["q"], s["k"], s["v"], s["bg"], s["states"], s["xms"], do)
    dx, dx_bf16, dps, dg1, dcw, dal, ddt = _in_proj_bwd(
        dq, dk, dv, dbg, s["qkv"], s["bd"], p["al"], p["dt"], dcv, dgb, s["sc_in"], dz, p["cw"], p["scw"], dx1, s["x"],
        p["g1"], p["land_a"])
    parts = lax.empty((N_CHIPS, A_ROWS, D_MODEL), BF16)
    parts = _wgrad_in_share(s["h"], dps, parts, "wgrad_in")
    parts = _wgrad_share(s["mix"], dx1_bf16, parts, A_OUT_AT, "wgrad_out")
    g = dict(g1=dg1[0], gn=dgn[0], gs=dgs[0], scw=dscw[:3], cw=dcw[:4], al=dal[0, HEADS:2 * HEADS], dt=ddt[0, HEADS:2 * HEADS])
    return dx, dx_bf16, parts, g


def _place():
    return lax.axis_index("x"), lax.axis_index("y"), lax.axis_index("c")


def _other_chips(x, y):
    return [(1 - x, y), (x, 1 - y), (1 - x, 1 - y)]


_HBM = pl.BlockSpec(memory_space=pltpu.HBM)


def _chip_exchange(arrs, name, gather):
    n = len(arrs)

    def body(*refs):
        ins, outs = refs[:n], refs[n:2 * n]
        send_sems, recv_sems, local_sems = refs[2 * n:]
        x, y, c = _place()
        me = 2 * x + y
        others = _other_chips(x, y)

        def remote(k, j, landing):
            px, py = others[j]
            src = ins[k] if gather else ins[k].at[2 * px + py]
            return pltpu.make_async_remote_copy(src_ref=src, dst_ref=outs[k].at[landing], send_sem=send_sems.at[k, j],
                                                recv_sem=recv_sems.at[k, j], device_id=(px, py, c), device_id_type=MESH)

        local = [pltpu.make_async_copy(ins[k] if gather else ins[k].at[me], outs[k].at[me], local_sems.at[k])
                 for k in range(n)]
        sends = [remote(k, j, me) for k in range(n) for j in range(3)]
        for cp in local + sends:
            cp.start()
        for k in range(n):
            for j, (px, py) in enumerate(others):
                remote(k, j, 2 * px + py).wait_recv()
        for cp in sends:
            cp.wait_send()
        for cp in local:
            cp.wait()

    shapes = [jax.ShapeDtypeStruct(((N_CHIPS,) + a.shape) if gather else a.shape, a.dtype) for a in arrs]
    return pl.pallas_call(
        body, name=name, in_specs=[_HBM] * n, out_specs=[_HBM] * n, out_shape=shapes,
        scratch_shapes=[pltpu.SemaphoreType.DMA((n, 3)), pltpu.SemaphoreType.DMA((n, 3)), pltpu.SemaphoreType.DMA((n,))],
    )(*arrs)


_SEM = pl.BlockSpec(memory_space=pltpu.SEMAPHORE)
_ANY = pl.BlockSpec(memory_space=pl.ANY)
_EFFECT = pltpu.SideEffectType.DATAFLOW_SIDE_EFFECTING


_FLIPS = [(a, b, cc) for a in (0, 1) for b in (0, 1) for cc in (0, 1)][1:]


def _split_copies(src_ref, land_ref, send_sems, recv_sems, gather, sending):
    x, y, c = _place()
    copies = []
    if gather:
        me = 2 * x + y
        for j, (px, py) in enumerate(_other_chips(x, y)):
            copies.append(pltpu.make_async_remote_copy(
                src_ref=src_ref, dst_ref=land_ref.at[me if sending else 2 * px + py],
                send_sem=send_sems.at[j], recv_sem=recv_sems.at[j], device_id=(px, py, c), device_id_type=MESH))
        return copies
    me = 4 * x + 2 * y + c
    for j, (a, b, cc) in enumerate(_FLIPS):
        px, py, pc = (1 - x) if a else x, (1 - y) if b else y, (1 - c) if cc else c
        copies.append(pltpu.make_async_remote_copy(
            src_ref=src_ref.at[2 * px + py], dst_ref=land_ref.at[me if sending else 4 * px + 2 * py + pc],
            send_sem=send_sems.at[j], recv_sem=recv_sems.at[j], device_id=(px, py, pc), device_id_type=MESH))
    return copies


def _own_slot(share):
    chip = 2 * lax.axis_index("x") + lax.axis_index("y")
    return lax.dynamic_update_slice(lax.empty((N_CHIPS,) + share.shape, share.dtype), share[None], (chip, 0, 0))


def _own_part(parts):
    chip = 2 * lax.axis_index("x") + lax.axis_index("y")
    own = lax.dynamic_index_in_dim(parts, chip, 0, keepdims=True)
    return lax.dynamic_update_slice(lax.empty((N_DEV,) + parts.shape[1:], parts.dtype), own,
                                    (2 * chip + lax.axis_index("c"), 0, 0))


def _exchange_start(src, land, after, name, gather):
    def body(src_ref, land_ref, after_ref, send_sems, recv_sems, src_thru, land_thru, token):
        for cp in _split_copies(src_ref, land_ref, send_sems, recv_sems, gather, sending=True):
            cp.start()
        token[...] = jnp.zeros_like(token)

    hbm = lambda t: pltpu.with_memory_space_constraint(t, pltpu.HBM)
    n_copies = N_CHIPS - 1 if gather else N_DEV - 1
    return pl.pallas_call(
        body, name=name,
        out_shape=(pltpu.SemaphoreType.DMA((n_copies,)), pltpu.SemaphoreType.DMA((n_copies,)), pltpu.HBM(src.shape, src.dtype),
                   pltpu.HBM(land.shape, land.dtype), jax.ShapeDtypeStruct((8, LANES), F32)),
        in_specs=(_HBM, _HBM, _ANY), out_specs=(_SEM, _SEM, _HBM, _HBM, pl.BlockSpec(memory_space=pltpu.VMEM)),
        input_output_aliases={0: 2, 1: 3},
        compiler_params=pltpu.CompilerParams(has_side_effects=_EFFECT),
    )(hbm(src), hbm(land), after)


def _exchange_wait(started, after, name, gather):
    send_sems, recv_sems, src_thru, land_thru, _ = started

    def body(src_ref, land_ref, send_sems, recv_sems, after_ref, src_dead, got_ref):
        for cp in _split_copies(src_ref, land_ref, send_sems, recv_sems, gather, sending=False):
            cp.wait_send()
            cp.wait_recv()

    return pl.pallas_call(
        body, name=name,
        out_shape=(pltpu.HBM(src_thru.shape, src_thru.dtype), pltpu.HBM(land_thru.shape, land_thru.dtype)),
        in_specs=(_HBM, _HBM, _SEM, _SEM, _ANY), out_specs=(_HBM, _HBM), input_output_aliases={0: 0, 1: 1},
        compiler_params=pltpu.CompilerParams(has_side_effects=_EFFECT),
    )(src_thru, land_thru, send_sems, recv_sems, after)[1]


def _all_reduce_small(v):
    rows = v.shape[0]
    flips = [(a, b, cc) for a in (0, 1) for b in (0, 1) for cc in (0, 1)][1:]

    def body(v_ref, out_ref, buf_ref, send_sems, recv_sems):
        x, y, c = _place()
        me = 4 * x + 2 * y + c
        peers = [((1 - x) if a else x, (1 - y) if b else y, (1 - c) if cc else c) for a, b, cc in flips]

        def copy(j, landing):
            return pltpu.make_async_remote_copy(src_ref=v_ref, dst_ref=buf_ref.at[landing], send_sem=send_sems.at[j],
                                                recv_sem=recv_sems.at[j], device_id=peers[j], device_id_type=MESH)

        sends = [copy(j, me) for j in range(N_DEV - 1)]
        for cp in sends:
            cp.start()
        buf_ref[me] = v_ref[...]
        for j, (px, py, pc) in enumerate(peers):
            copy(j, 4 * px + 2 * py + pc).wait_recv()
        for cp in sends:
            cp.wait_send()
        acc = buf_ref[0]
        for d in range(1, N_DEV):
            acc = acc + buf_ref[d]
        out_ref[...] = acc

    vmem = pl.BlockSpec(memory_space=pltpu.VMEM)
    return pl.pallas_call(
        body, name="all_reduce_small", in_specs=[vmem], out_specs=vmem,
        out_shape=jax.ShapeDtypeStruct(v.shape, F32),
        scratch_shapes=[pltpu.VMEM((N_DEV, rows, LANES), F32), pltpu.SemaphoreType.DMA((N_DEV - 1,)),
                        pltpu.SemaphoreType.DMA((N_DEV - 1,))],
    )(v)


def _row_block(*sizes):
    return next(t for t in (176, 128, 64) if all(s % t == 0 for s in sizes))


def _adam_update(w, m, v, g):
    r1 = 1.0 / (1.0 - ADAM_B1 ** ADAM_STEP)
    r2 = 1.0 / (1.0 - ADAM_B2 ** ADAM_STEP)
    m_new = ADAM_B1 * m + (1.0 - ADAM_B1) * g
    v_new = ADAM_B2 * v + (1.0 - ADAM_B2) * (g * g)
    return -ADAM_LR * ((m_new * r1) / (jnp.sqrt(v_new * r2) + ADAM_EPS) + ADAM_WD * w), m_new, v_new


def _adamw_rows(w, m, v, got, first, name):
    n_layers, rows, cols = w.shape
    tr = _row_block(rows, first)

    def body(*refs):
        w_ref, m_ref, v_ref = refs[:3]
        g_out, d_out, m_out, v_out = refs[3 + n_layers:]
        for k in range(n_layers):
            @pl.when(pl.program_id(0) == k)
            def _(p_ref=refs[3 + k]):
                g = p_ref[0].astype(F32)
                for d in range(1, N_DEV):
                    g = g + p_ref[d].astype(F32)
                g = g[:, :cols]
                d_out[0], m_out[0], v_out[0] = _adam_update(w_ref[0], m_ref[0], v_ref[0], g)
                g_out[0] = g

    blk = pl.BlockSpec((1, tr, cols), lambda l, i: (l, i, 0))
    parts = [pl.BlockSpec((N_DEV, tr, got[0].shape[2]), lambda l, i, k=k: (0, jnp.where(l == k, first // tr + i, 0), 0))
             for k in range(n_layers)]
    return pl.pallas_call(
        body, name=name, grid=(n_layers, rows // tr),
        in_specs=[blk] * 3 + parts, out_specs=[blk] * 4,
        out_shape=[jax.ShapeDtypeStruct(w.shape, F32)] * 4,
        compiler_params=_params(("arbitrary", "arbitrary")),
    )(w, m, v, *got)


def _adamw(w, m, v, g_parts, name):
    rows, cols = w.shape
    tr = min(rows, 256)
    n = len(g_parts)

    def body(*refs):
        w_ref, m_ref, v_ref = refs[:3]
        g_refs = refs[3:3 + n]
        g_out, d_out, m_out, v_out = refs[3 + n:]
        g = g_refs[0][...]
        for r in g_refs[1:]:
            g = g + r[...]
        d_out[...], m_out[...], v_out[...] = _adam_update(w_ref[...], m_ref[...], v_ref[...], g)
        g_out[...] = g

    blk = pl.BlockSpec((tr, cols), lambda i: (i, 0))
    return pl.pallas_call(
        body, name=name, grid=(rows // tr,),
        in_specs=[blk] * (3 + n), out_specs=[blk] * 4,
        out_shape=[jax.ShapeDtypeStruct((rows, cols), F32)] * 4,
        compiler_params=_params(("parallel",)),
    )(w, m, v, *g_parts)


def _pack(parts, rows, fill=0.0):
    flat = jnp.concatenate([p.reshape(-1) for p in parts])
    return jnp.pad(flat, (0, rows * LANES - flat.shape[0]), constant_values=fill).reshape(rows, LANES)


def _unpack(packed, shapes):
    flat = packed.reshape(-1)
    out, at = [], 0
    for shp in shapes:
        size = 1
        for s in shp:
            size *= s
        out.append(flat[at:at + size].reshape(shp))
        at += size
    return out


def _packed_rows(shapes):
    total = 0
    for shp in shapes:
        size = 1
        for s in shp:
            size *= s
        total += size
    return -(-total // (8 * LANES)) * 8


def _cols_full(g, l):
    t = g[:, l]
    return jnp.moveaxis(t, 0, 1).reshape(t.shape[1], N_CHIPS * t.shape[2])


def _pad_cols(t):
    return jnp.pad(t, ((0, 0),) * (t.ndim - 1) + ((0, D_MODEL - t.shape[-1]),))


def kernel(x, norm1_g, w_in, dn_conv_w, dn_a_log, dn_dt_bias, dn_norm_g, sc_conv_w, sc_norm_g, w_out, norm2_g, ffn_w_gate, ffn_w_up, ffn_w_down, final_norm_g, loss_target, m_norm1_g, m_w_in, m_dn_conv_w, m_dn_a_log, m_dn_dt_bias, m_dn_norm_g, m_sc_conv_w, m_sc_norm_g, m_w_out, m_norm2_g, m_ffn_w_gate, m_ffn_w_up, m_ffn_w_down, m_final_norm_g, v_norm1_g, v_w_in, v_dn_conv_w, v_dn_a_log, v_dn_dt_bias, v_dn_norm_g, v_sc_conv_w, v_sc_norm_g, v_w_out, v_norm2_g, v_ffn_w_gate, v_ffn_w_up, v_ffn_w_down, v_final_norm_g):
    chip = 2 * lax.axis_index("x") + lax.axis_index("y")

    g_cw, g_scw = _chip_exchange([dn_conv_w, sc_conv_w], "gather_conv", gather=True)

    t_last = lambda t: jnp.swapaxes(t, -1, -2)
    gate_t, up_t = t_last(ffn_w_gate), t_last(ffn_w_up)
    zero_token = jnp.zeros((8, LANES), F32)

    def shares(l, tie):
        share_a = jnp.concatenate([_pad_cols(w_in[l] + tie), w_out[l]], axis=0).astype(BF16)
        share_b = jnp.concatenate([gate_t[l] + tie, up_t[l], ffn_w_down[l]], axis=0).astype(BF16)
        return share_a, _own_slot(share_a), share_b, _own_slot(share_b)

    def gather_start(l, packed, after):
        a = _exchange_start(packed[0], packed[1], after, "gather_a_start_%d" % l, gather=True)
        b = _exchange_start(packed[2], packed[3], a[4], "gather_b_start_%d" % l, gather=True)
        return a, b

    ga, gb = gather_start(0, shares(0, 0.0), g_cw)
    packed = [None] + [shares(l, gb[4][0, 0]) for l in range(1, DEPTH)]
    packed_all = sum(t[0, 0].astype(F32) for p in packed[1:] for t in (p[0], p[2]))
    land_a = _exchange_wait(ga, zero_token + packed_all, "gather_a_wait_0", gather=True)
    act = x[0]
    layers, saved_m, saved_f, lands_b = [], [], [], []
    for l in range(DEPTH):
        hold = 0.0
        if l + 1 < DEPTH:
            ga, gb_next = gather_start(l + 1, packed[l + 1], land_a)
            hold = gb_next[4][0:1, 0:1]
        al, dt = _gate_rows(dn_a_log[l], dn_dt_bias[l])
        layers.append(dict(
            g1=norm1_g[l][None] + hold, cw=_pad_rows(_cols_full(g_cw, l)), al=al, dt=dt,
            gn=dn_norm_g[l][None], scw=_pad_rows(_cols_full(g_scw, l)), gs=sc_norm_g[l][None],
            land_a=land_a, g2=norm2_g[l][None]))
        s = _mixer_fwd(act, layers[l])
        lands_b.append(_exchange_wait(gb, s["o"], "gather_b_wait_%d" % l, gather=True))
        act, s, sf = _tail_fwd(s, layers[l], lands_b[l])
        saved_m.append(s)
        saved_f.append(sf)
        if l + 1 < DEPTH:
            land_a = _exchange_wait(ga, act, "gather_a_wait_%d" % (l + 1), gather=True)
            gb = gb_next

    dact, dact_bf16, loss_part, d_final = _loss_head(act, final_norm_g[None], loss_target[0])
    grads, reduce_a, reduce_b = [None] * DEPTH, [None] * DEPTH, [None] * DEPTH
    hold = 0.0
    for l in reversed(range(DEPTH)):
        p = layers[l]
        dx1, parts, dg2 = _ffn_back(dact, dact_bf16, saved_f[l], dict(p, g2=p["g2"] + hold), lands_b[l])
        reduce_b[l] = _exchange_start(parts, _own_part(parts), zero_token, "reduce_b_start_%d" % l, gather=False)
        dact, dact_bf16, parts, gm = _mixer_bwd(dx1, saved_m[l], dict(p, gn=p["gn"] + reduce_b[l][4][0:1, 0:1]))
        reduce_a[l] = _exchange_start(parts, _own_part(parts), zero_token, "reduce_a_start_%d" % l, gather=False)
        hold = reduce_a[l][4][0:1, 0:1]
        grads[l] = dict(gm, g2=dg2)
    loss = lax.psum(loss_part[0, 0], ("x", "y", "c"))
    stack = lambda key: jnp.stack([grads[l][key] for l in range(DEPTH)])

    got_b = [_exchange_wait(reduce_b[l], reduce_a[0][4], "reduce_b_wait_%d" % l, gather=False)
             for l in reversed(range(DEPTH))][::-1]
    big = dict(
        ffn_w_gate=[t_last(o) for o in _adamw_rows(gate_t, t_last(m_ffn_w_gate), t_last(v_ffn_w_gate), got_b, 0, "adamw_gate")],
        ffn_w_up=[t_last(o) for o in _adamw_rows(up_t, t_last(m_ffn_w_up), t_last(v_ffn_w_up), got_b, FF_SHARD, "adamw_up")],
        ffn_w_down=_adamw_rows(ffn_w_down, m_ffn_w_down, v_ffn_w_down, got_b, 2 * FF_SHARD, "adamw_down"))
    after_b = zero_token + sum(big[n][1][0, 0, 0] for n in ("ffn_w_gate", "ffn_w_up", "ffn_w_down"))
    got_a = [_exchange_wait(reduce_a[l], after_b, "reduce_a_wait_%d" % l, gather=False) for l in reversed(range(DEPTH))][::-1]
    big.update(
        w_in=_adamw_rows(w_in, m_w_in, v_w_in, got_a, 0, "adamw_w_in"),
        w_out=_adamw_rows(w_out, m_w_out, v_w_out, got_a, A_OUT_AT, "adamw_w_out"))

    full_shapes = [(DEPTH, D_MODEL), (DEPTH, D_MODEL), (DEPTH, HEAD_DIM), (DEPTH, SC_WIDTH), (DEPTH, HEADS),
                   (DEPTH, HEADS), (D_MODEL,), (DEPTH, 4, QKV), (DEPTH, 3, SC_WIDTH)]
    small_keys = ("g1", "g2", "gn", "gs", "al", "dt")
    packed = _pack([stack(k) for k in small_keys] + [d_final[0], stack("cw"), stack("scw")], _packed_rows(full_shapes))
    sg = _unpack(_all_reduce_small(packed), full_shapes)
    sg[7] = lax.dynamic_slice_in_dim(sg[7], chip * (QKV // N_CHIPS), QKV // N_CHIPS, axis=2)
    sg[8] = lax.dynamic_slice_in_dim(sg[8], chip * (SC_WIDTH // N_CHIPS), SC_WIDTH // N_CHIPS, axis=2)
    small_names = ("norm1_g", "norm2_g", "dn_norm_g", "sc_norm_g", "dn_a_log", "dn_dt_bias", "final_norm_g",
                   "dn_conv_w", "sc_conv_w")
    sw = (norm1_g, norm2_g, dn_norm_g, sc_norm_g, dn_a_log, dn_dt_bias, final_norm_g, dn_conv_w, sc_conv_w)
    sm = (m_norm1_g, m_norm2_g, m_dn_norm_g, m_sc_norm_g, m_dn_a_log, m_dn_dt_bias, m_final_norm_g, m_dn_conv_w, m_sc_conv_w)
    sv = (v_norm1_g, v_norm2_g, v_dn_norm_g, v_sc_norm_g, v_dn_a_log, v_dn_dt_bias, v_final_norm_g, v_dn_conv_w, v_sc_conv_w)
    shard_shapes = [t.shape for t in sw]
    rows = _packed_rows(shard_shapes)
    outs = _adamw(_pack(sw, rows), _pack(sm, rows), _pack(sv, rows, fill=1.0), [_pack(sg, rows)], "adamw_small")
    small = {name: [] for name in small_names}
    for o in outs:
        for name, t in zip(small_names, _unpack(o, shard_shapes)):
            small[name].append(t)

    order = ("norm1_g", "w_in", "dn_conv_w", "dn_a_log", "dn_dt_bias", "dn_norm_g", "sc_conv_w", "sc_norm_g", "w_out",
             "norm2_g", "ffn_w_gate", "ffn_w_up", "ffn_w_down", "final_norm_g")
    result = {**big, **small}
    return (loss, dact[None], *[result[n][0] for n in order], *[result[n][1] for n in order],
            *[result[n][2] for n in order], *[result[n][3] for n in order])
```

```python
import jax
import jax.numpy as jnp
from jax import lax
from jax.experimental import pallas as pl
from jax.experimental.pallas import tpu as pltpu

F32 = jnp.float32
BF16 = jnp.bfloat16
MESH = pl.DeviceIdType.MESH

D_MODEL = 1024
DEPTH = 4
HEADS = 4
HEAD_DIM = 128
DN_WIDTH = HEADS * HEAD_DIM
SC_WIDTH = 512
SC_GROUPS = 4
D_FF = 2816
CHUNK = 64
QKV = 3 * DN_WIDTH
W_IN_COLS = 4 * DN_WIDTH + 2 * HEADS + 3 * SC_WIDTH
WA_COLS = QKV + DN_WIDTH + 3 * SC_WIDTH
LANES = 128
EPS = 1e-6
Q_SCALE = HEAD_DIM ** -0.5
N_CHIPS = 4
N_DEV = 8
IN_SHARD = W_IN_COLS // N_CHIPS
OUT_SHARD = D_MODEL // N_CHIPS
FF_SHARD = D_FF // N_CHIPS
A_OUT_AT = D_MODEL
A_ROWS = D_MODEL + OUT_SHARD
B_ROWS = 3 * FF_SHARD

ADAM_LR = 0.001
ADAM_B1 = 0.9
ADAM_B2 = 0.999
ADAM_EPS = 1e-08
ADAM_WD = 0.01
ADAM_STEP = 10

VMEM_LIMIT = 56 * 1024 * 1024

NN = (((1,), (0,)), ((), ()))
NT = (((1,), (1,)), ((), ()))
TN = (((0,), (0,)), ((), ()))


def _mm(a, b, dims=NN):
    return lax.dot_general(a.astype(BF16), b.astype(BF16), dims, preferred_element_type=F32)


def _mm32(a, b, dims=NN):
    return lax.dot_general(a, b, dims, preferred_element_type=F32, precision=lax.Precision.HIGHEST)


def _params(sem, vmem=VMEM_LIMIT):
    return pltpu.CompilerParams(dimension_semantics=sem, vmem_limit_bytes=vmem)


def _sigmoid(x):
    return 0.5 * jnp.tanh(0.5 * x) + 0.5


def _softplus(x):
    return jnp.maximum(x, 0.0) + jnp.log1p(jnp.exp(-jnp.abs(x)))


def _row_acc(acc_ref, val):
    acc_ref[0:1, :] += jnp.sum(val, axis=0, keepdims=True)


def _rms_bwd(dh, xh, r, gain):
    dxh = dh * gain
    return r * (dxh - xh * jnp.mean(dxh * xh, axis=-1, keepdims=True))


def _before_halo(tb):
    return lambda i: (jnp.maximum(i * (tb // 8) - 1, 0), 0)


def _after_halo(tb, n_rows):
    last = n_rows // 8 - 1
    return lambda i: (jnp.minimum((i + 1) * (tb // 8), last), 0)


def _rows_from(xc, offset, tb):
    part = offset % 8
    if part:
        xc = pltpu.roll(xc, xc.shape[0] - part, 0)
    return xc[offset - part:offset - part + tb, :]


def _taps(xc, w, n_taps, tb, first):
    out = w[0:1, :] * _rows_from(xc, first, tb)
    for j in range(1, n_taps):
        out = out + w[j:j + 1, :] * _rows_from(xc, first + j, tb)
    return out


W_Z = QKV
W_BD = W_Z + DN_WIDTH
W_SC = W_BD + 2 * HEADS

def _w_in_cols(shards, lo, hi):
    pieces = []
    for s in range(N_CHIPS):
        a, b = max(lo, IN_SHARD * s), min(hi, IN_SHARD * (s + 1))
        if a < b:
            pieces.append(shards[s][:, a - IN_SHARD * s:b - IN_SHARD * s])
    return pieces[0] if len(pieces) == 1 else jnp.concatenate(pieces, axis=1)


def _in_proj(x, g1, land_a, cw, al_row, dt_row):
    T = x.shape[0]
    tb = 256

    def body(x_ref, g_ref, w_ref, cw_ref, al_ref, dt_ref,
             qkv_ref, z_ref, sc_ref, bd_ref, h_ref, q_ref, k_ref, v_ref, bg_ref, tail_ref):
        @pl.when(pl.program_id(0) == 0)
        def _():
            tail_ref[...] = jnp.zeros_like(tail_ref)

        xv = x_ref[...]
        h = (xv * lax.rsqrt(jnp.mean(xv * xv, axis=-1, keepdims=True) + EPS) * g_ref[...]).astype(BF16)
        shards = [jnp.dot(h, w_ref[s], preferred_element_type=F32) for s in range(N_CHIPS)]
        qkv = _w_in_cols(shards, 0, W_Z)
        bd = jnp.concatenate([_w_in_cols(shards, W_BD, W_SC), jnp.zeros((tb, LANES - 2 * HEADS), F32)], axis=1)
        qkv_ref[...] = qkv
        z_ref[...] = _w_in_cols(shards, W_Z, W_BD)
        bd_ref[...] = bd
        sc_ref[...] = _w_in_cols(shards, W_SC, W_IN_COLS)
        h_ref[...] = h
        halo = tail_ref[...]
        tail_ref[...] = qkv[tb - 8:, :]
        _, _, _, a = _dn_act(qkv, halo, cw_ref[...], tb)
        for hd in range(HEADS):
            sl = slice(HEAD_DIM * hd, HEAD_DIM * (hd + 1))
            qs = a[:, sl]
            q_ref[:, sl] = qs * (lax.rsqrt(jnp.sum(qs * qs, axis=-1, keepdims=True) + EPS) * Q_SCALE)
            ks = a[:, DN_WIDTH + HEAD_DIM * hd:DN_WIDTH + HEAD_DIM * (hd + 1)]
            k_ref[:, sl] = ks * lax.rsqrt(jnp.sum(ks * ks, axis=-1, keepdims=True) + EPS)
        v_ref[...] = a[:, 2 * DN_WIDTH:]
        gates = _gates(bd, al_ref[...], dt_ref[...])
        lane = lax.broadcasted_iota(jnp.int32, gates.shape, 1)
        bg_ref[...] = jnp.where(lane < HEADS, gates, _mm32(_chunk_cumsum_matrix(tb), gates))

    tok = lambda w: pl.BlockSpec((tb, w), lambda i: (i, 0))
    full = lambda t: pl.BlockSpec(t.shape, lambda i: (0, 0))
    return pl.pallas_call(
        body, name="in_proj", grid=(T // tb,),
        in_specs=[tok(D_MODEL), full(g1), _shard_rows(land_a, 0, D_MODEL), full(cw), full(al_row), full(dt_row)],
        out_specs=[tok(QKV), tok(DN_WIDTH), tok(3 * SC_WIDTH), tok(LANES), tok(D_MODEL),
                   tok(DN_WIDTH), tok(DN_WIDTH), tok(DN_WIDTH), tok(LANES)],
        out_shape=[jax.ShapeDtypeStruct((T, QKV), F32), jax.ShapeDtypeStruct((T, DN_WIDTH), F32),
                   jax.ShapeDtypeStruct((T, 3 * SC_WIDTH), F32), jax.ShapeDtypeStruct((T, LANES), F32),
                   jax.ShapeDtypeStruct((T, D_MODEL), BF16)]
        + [jax.ShapeDtypeStruct((T, DN_WIDTH), F32)] * 3 + [jax.ShapeDtypeStruct((T, LANES), F32)],
        scratch_shapes=[pltpu.VMEM((8, QKV), F32)],
        compiler_params=_params(("arbitrary",)),
    )(x, g1, land_a, cw, al_row, dt_row)


def _dn_act(pre, halo, cw, tb):
    xc = jnp.concatenate([halo, pre], axis=0)
    c = _taps(xc, cw, 4, tb, 5)
    sg = _sigmoid(c)
    return xc, c, sg, c * sg


def _gates(bd, al_row, dt_row):
    lane = lax.broadcasted_iota(jnp.int32, bd.shape, 1)
    beta = _sigmoid(bd)
    g = -jnp.exp(al_row) * _softplus(bd + dt_row)
    return jnp.where(lane < HEADS, beta, jnp.where(lane < 2 * HEADS, g, 0.0))


def _chunk_masks():
    row = lax.broadcasted_iota(jnp.int32, (CHUNK, CHUNK), 0)
    col = lax.broadcasted_iota(jnp.int32, (CHUNK, CHUNK), 1)
    return row >= col, row > col


def _chunk_cumsum_matrix(n):
    row = lax.broadcasted_iota(jnp.int32, (n, n), 0)
    col = lax.broadcasted_iota(jnp.int32, (n, n), 1)
    return jnp.logical_and(row >= col, row // CHUNK == col // CHUNK).astype(F32)


def _chunk_units(q_ref, k_ref, v_ref, bg_ref, rows):
    bgc = bg_ref[rows, :]
    bg_t = bgc.T
    qv, kv, vv = q_ref[rows, :], k_ref[rows, :], v_ref[rows, :]
    units = []
    for h in range(HEADS):
        sl = slice(HEAD_DIM * h, HEAD_DIM * (h + 1))
        units.append((qv[:, sl], kv[:, sl], vv[:, sl], bgc[:, h:h + 1], bgc[:, HEADS + h:HEADS + h + 1],
                      bg_t[HEADS + h:HEADS + h + 1, :]))
    return units


def _units_local(units, masks, xms=None):
    causal, strict = masks
    pre = []
    for q, k, v, beta, gc, gr in units:
        kb = k * beta
        eg = jnp.exp(gc)
        g_last = gc[CHUNK - 1:CHUNK, :]
        ek = jnp.exp(g_last - gc)
        pre.append(dict(q=q, k=k, v=v, beta=beta, decay=jnp.exp(jnp.where(causal, gc - gr, -1e30)), kb=kb, vb=v * beta,
                        eg=eg, kbg=kb * eg, ek=ek, gl=jnp.exp(g_last), q_dec=q * eg, k_dec=k * ek))
    both = [_mm(jnp.concatenate([p["kb"], p["q"]], axis=0), p["k"], NT) for p in pre]
    for p, b in zip(pre, both):
        p["low"] = jnp.where(strict, b[:CHUNK] * p["decay"], 0.0)
        p["qk"] = jnp.where(causal, b[CHUNK:] * p["decay"], 0.0)
    xs = xms
    if xs is None:
        xs = [-p["low"] for p in pre]
        pw = [_mm(p["low"], p["low"]) for p in pre]
        for _ in range(4):
            both = [_mm(jnp.concatenate([pp, x], axis=0), pp) for pp, x in zip(pw, xs)]
            xs = [x + pp + b[CHUNK:] for x, pp, b in zip(xs, pw, both)]
            pw = [b[:CHUNK] for b in both]
        last = [_mm(x, pp) for x, pp in zip(xs, pw)]
        xs = [x + pp + b for x, pp, b in zip(xs, pw, last)]
    uw = [_mm(x, jnp.concatenate([p["vb"], p["kbg"]], axis=1)) for x, p in zip(xs, pre)]
    for p, x, b in zip(pre, xs, uw):
        p["xm"] = x
        p["u"] = p["vb"] + b[:, :HEAD_DIM]
        p["w"] = p["kbg"] + b[:, HEAD_DIM:]
    return pre


FWD_GROUP = 8
BWD_GROUP = 8


def _delta_fwd(q, k, v, bg):
    T = q.shape[0]
    tb = 512
    n_chunk = tb // CHUNK

    def body(q_ref, k_ref, v_ref, bg_ref, o_ref, st_ref, xm_ref, s_ref):
        @pl.when(pl.program_id(0) == 0)
        def _():
            s_ref[...] = jnp.zeros_like(s_ref)

        masks = _chunk_masks()

        def group(gi, carry):
            rows = [pl.ds(pl.multiple_of((FWD_GROUP * gi + j) * CHUNK, CHUNK), CHUNK) for j in range(FWD_GROUP)]
            loc = _units_local(sum((_chunk_units(q_ref, k_ref, v_ref, bg_ref, r) for r in rows), []), masks)
            states = [s_ref[h] for h in range(HEADS)]
            for j in range(FWD_GROUP):
                lj = loc[HEADS * j:HEADS * (j + 1)]
                ws = [_mm(jnp.concatenate([p["w"], p["q_dec"]], axis=0), s) for p, s in zip(lj, states)]
                v_new = [p["u"] - b[:CHUNK] for p, b in zip(lj, ws)]
                intra = [_mm(p["qk"], vn) for p, vn in zip(lj, v_new)]
                upd = [_mm(p["k_dec"], vn, TN) for p, vn in zip(lj, v_new)]
                o_ref[rows[j], :] = jnp.concatenate([b[CHUNK:] + a for b, a in zip(ws, intra)], axis=1)
                for h in range(HEADS):
                    st_ref[FWD_GROUP * gi + j, h] = states[h]
                    xm_ref[FWD_GROUP * gi + j, h] = lj[h]["xm"]
                states = [p["gl"] * s + d for p, s, d in zip(lj, states, upd)]
            for h in range(HEADS):
                s_ref[h] = states[h]
            return carry

        lax.fori_loop(0, n_chunk // FWD_GROUP, group, 0)

    tok = lambda w: pl.BlockSpec((tb, w), lambda i: (i, 0))
    return pl.pallas_call(
        body, name="delta_fwd", grid=(T // tb,),
        in_specs=[tok(DN_WIDTH), tok(DN_WIDTH), tok(DN_WIDTH), tok(LANES)],
        out_specs=[tok(DN_WIDTH), pl.BlockSpec((n_chunk, HEADS, HEAD_DIM, HEAD_DIM), lambda i: (i, 0, 0, 0)),
                   pl.BlockSpec((n_chunk, HEADS, CHUNK, CHUNK), lambda i: (i, 0, 0, 0))],
        out_shape=[jax.ShapeDtypeStruct((T, DN_WIDTH), F32),
                   jax.ShapeDtypeStruct((T // CHUNK, HEADS, HEAD_DIM, HEAD_DIM), F32),
                   jax.ShapeDtypeStruct((T // CHUNK, HEADS, CHUNK, CHUNK), F32)],
        scratch_shapes=[pltpu.VMEM((HEADS, HEAD_DIM, HEAD_DIM), F32)],
        compiler_params=_params(("arbitrary",)),
    )(q, k, v, bg)


def _dn_out(o, z, gn):
    outs, ohs, rs = [], [], []
    for hh in range(HEADS):
        oh = o[:, HEAD_DIM * hh:HEAD_DIM * (hh + 1)]
        r = lax.rsqrt(jnp.mean(oh * oh, axis=-1, keepdims=True) + EPS)
        ohs.append(oh * r)
        rs.append(r)
    sz = _sigmoid(z)
    oh = jnp.concatenate(ohs, axis=1)
    gn4 = jnp.concatenate([gn] * HEADS, axis=1)
    return oh * gn4 * (z * sz), oh, rs, sz, gn4


def _sc_fwd(sc_in, halo, cw, tb):
    xc = jnp.concatenate([halo, sc_in], axis=0)
    u = xc[:, SC_WIDTH:2 * SC_WIDTH] * xc[:, 2 * SC_WIDTH:]
    cv = _taps(u, cw, 3, tb, 6)
    gate_b = sc_in[:, :SC_WIDTH]
    y = gate_b * cv
    gw = SC_WIDTH // SC_GROUPS
    yhs, rs = [], []
    for gi in range(SC_GROUPS):
        yg = y[:, gw * gi:gw * (gi + 1)]
        r = lax.rsqrt(jnp.mean(yg * yg, axis=-1, keepdims=True) + EPS)
        yhs.append(yg * r)
        rs.append(r)
    return u, cv, gate_b, jnp.concatenate(yhs, axis=1), rs


def _shard_rows(land, first, rows, single_buffer=False):
    assert first % rows == 0 and land.shape[0] == N_CHIPS
    mode = dict(pipeline_mode=pl.Buffered(1)) if single_buffer else {}
    return pl.BlockSpec((N_CHIPS, rows, land.shape[2]), lambda i: (0, first // rows, 0), **mode)


def _whole(w_ref):
    n, rows, cols = w_ref.shape
    return w_ref[...].reshape(n * rows, cols)


def _mix_ffn(o, z, sc_in, x, land_a, gn, scw, gs, g2, land_b):
    T = x.shape[0]
    tb = 256

    def body(o_ref, z_ref, sc_ref, halo_ref, x_ref, wo_ref, gn_ref, scw_ref, gs_ref, g2_ref, wgt_ref, wut_ref, wd_ref,
             x1_ref, mix_ref, x2_ref, a_ref, b_ref, h_ref):
        o_n = _dn_out(o_ref[...], z_ref[...], gn_ref[...])[0]
        halo = jnp.where(pl.program_id(0) > 0, halo_ref[...], 0.0)
        yh = _sc_fwd(sc_ref[...], halo, scw_ref[...], tb)[3]
        mix = jnp.concatenate([o_n, yh * gs_ref[...]], axis=1).astype(BF16)
        x1 = x_ref[...] + jnp.dot(mix, _whole(wo_ref), preferred_element_type=F32)
        x1_ref[...] = x1
        mix_ref[...] = mix
        r = lax.rsqrt(jnp.mean(x1 * x1, axis=-1, keepdims=True) + EPS)
        h = (x1 * r * g2_ref[...]).astype(BF16)
        a = lax.dot_general(h, _whole(wgt_ref), NT, preferred_element_type=F32)
        b = lax.dot_general(h, _whole(wut_ref), NT, preferred_element_type=F32)
        act = (a * _sigmoid(a) * b).astype(BF16)
        x2_ref[...] = x1 + jnp.dot(act, _whole(wd_ref), preferred_element_type=F32)
        a_ref[...] = a.astype(BF16)
        b_ref[...] = b.astype(BF16)
        h_ref[...] = h

    tok = lambda w: pl.BlockSpec((tb, w), lambda i: (i, 0))
    full = lambda t: pl.BlockSpec(t.shape, lambda i: (0, 0))
    once = lambda land, first, rows: _shard_rows(land, first, rows, single_buffer=True)
    return pl.pallas_call(
        body, name="mix_ffn", grid=(T // tb,),
        in_specs=[tok(DN_WIDTH), tok(DN_WIDTH), tok(3 * SC_WIDTH), pl.BlockSpec((8, 3 * SC_WIDTH), _before_halo(tb)),
                  tok(D_MODEL), once(land_a, A_OUT_AT, OUT_SHARD), full(gn), full(scw), full(gs), full(g2),
                  once(land_b, 0, FF_SHARD), once(land_b, FF_SHARD, FF_SHARD), once(land_b, 2 * FF_SHARD, FF_SHARD)],
        out_specs=[tok(D_MODEL), tok(D_MODEL), tok(D_MODEL), tok(D_FF), tok(D_FF), tok(D_MODEL)],
        out_shape=[jax.ShapeDtypeStruct((T, D_MODEL), F32), jax.ShapeDtypeStruct((T, D_MODEL), BF16),
                   jax.ShapeDtypeStruct((T, D_MODEL), F32), jax.ShapeDtypeStruct((T, D_FF), BF16),
                   jax.ShapeDtypeStruct((T, D_FF), BF16), jax.ShapeDtypeStruct((T, D_MODEL), BF16)],
        compiler_params=_params(("parallel",)),
    )(o, z, sc_in, sc_in, x, land_a, gn, scw, gs, g2, land_b, land_b, land_b)


def _loss_head(x, gf, target):
    T = x.shape[0]
    tb = 512

    def body(x_ref, g_ref, t_ref, dx_ref, dxb_ref, loss_ref, dg_ref):
        @pl.when(pl.program_id(0) == 0)
        def _():
            loss_ref[...] = jnp.zeros_like(loss_ref)
            dg_ref[...] = jnp.zeros_like(dg_ref)

        xv = x_ref[...]
        r = lax.rsqrt(jnp.mean(xv * xv, axis=-1, keepdims=True) + EPS)
        xh = xv * r
        err = xh * g_ref[...] - t_ref[...]
        per_tok = jnp.mean(err * err, axis=-1, keepdims=True)
        loss_ref[...] += 0.5 * jnp.sum(per_tok, axis=0, keepdims=True)
        dy = err * (1.0 / D_MODEL)
        _row_acc(dg_ref, dy * xh)
        dx = _rms_bwd(dy, xh, r, g_ref[...])
        dx_ref[...] = dx
        dxb_ref[...] = dx.astype(BF16)

    tok = pl.BlockSpec((tb, D_MODEL), lambda i: (i, 0))
    return pl.pallas_call(
        body, name="loss_head", grid=(T // tb,),
        in_specs=[tok, pl.BlockSpec(gf.shape, lambda i: (0, 0)), tok],
        out_specs=[tok, tok, pl.BlockSpec((8, LANES), lambda i: (0, 0)), pl.BlockSpec((8, D_MODEL), lambda i: (0, 0))],
        out_shape=[jax.ShapeDtypeStruct((T, D_MODEL), F32), jax.ShapeDtypeStruct((T, D_MODEL), BF16),
                   jax.ShapeDtypeStruct((8, LANES), F32), jax.ShapeDtypeStruct((8, D_MODEL), F32)],
        compiler_params=_params(("arbitrary",)),
    )(x, gf, target)


def _ffn_bwd(dx2, x1, a, b, g2, land_b):
    T = x1.shape[0]
    tb = 256

    def body(dx2_ref, x_ref, a_ref, b_ref, g_ref, wgt_ref, wut_ref, wd_ref,
             dx1_ref, da_ref, db_ref, act_ref, dg_ref):
        @pl.when(pl.program_id(0) == 0)
        def _():
            dg_ref[...] = jnp.zeros_like(dg_ref)

        dx2v = dx2_ref[...]
        av = a_ref[...].astype(F32)
        bv = b_ref[...].astype(F32)
        dact = _mm(dx2v, _whole(wd_ref), NT)
        sa = _sigmoid(av)
        silu = av * sa
        da = (dact * bv * (sa * (1.0 + av * (1.0 - sa)))).astype(BF16)
        db = (dact * silu).astype(BF16)
        dh = _mm(da, _whole(wgt_ref)) + _mm(db, _whole(wut_ref))
        xv = x_ref[...]
        r = lax.rsqrt(jnp.mean(xv * xv, axis=-1, keepdims=True) + EPS)
        xh = xv * r
        _row_acc(dg_ref, dh * xh)
        dx1 = dx2v + _rms_bwd(dh, xh, r, g_ref[...])
        dx1_ref[...] = dx1
        da_ref[...] = da
        db_ref[...] = db
        act_ref[...] = (silu * bv).astype(BF16)

    tok = lambda w: pl.BlockSpec((tb, w), lambda i: (i, 0))
    return pl.pallas_call(
        body, name="ffn_bwd", grid=(T // tb,),
        in_specs=[tok(D_MODEL), tok(D_MODEL), tok(D_FF), tok(D_FF), pl.BlockSpec(g2.shape, lambda i: (0, 0)),
                  _shard_rows(land_b, 0, FF_SHARD), _shard_rows(land_b, FF_SHARD, FF_SHARD),
                  _shard_rows(land_b, 2 * FF_SHARD, FF_SHARD)],
        out_specs=[tok(D_MODEL), tok(D_FF), tok(D_FF), tok(D_FF), pl.BlockSpec((8, D_MODEL), lambda i: (0, 0))],
        out_shape=[jax.ShapeDtypeStruct((T, D_MODEL), F32)]
        + [jax.ShapeDtypeStruct((T, D_FF), BF16)] * 3 + [jax.ShapeDtypeStruct((8, D_MODEL), F32)],
        compiler_params=_params(("arbitrary",)),
    )(dx2, x1, a, b, g2, land_b, land_b, land_b)


WGRAD_TOKENS = 2048


def _wgrad_share(a, b, parts, first, name):
    T = b.shape[0]
    rows = a.shape[1] // N_CHIPS
    assert first % rows == 0 and b.shape[1] == parts.shape[2]
    bk = min(T, WGRAD_TOKENS)
    n_k = T // bk
    group = 2
    assert (group * rows) % LANES == 0

    def body(a_ref, b_ref, parts_ref, o_ref, acc_ref):
        kk = pl.program_id(1)

        @pl.when(kk == 0)
        def _():
            acc_ref[...] = jnp.zeros_like(acc_ref)

        acc_ref[...] += lax.dot_general(a_ref[...], b_ref[...], TN, preferred_element_type=F32)

        @pl.when(kk == n_k - 1)
        def _():
            for s in range(group):
                o_ref[s] = acc_ref[rows * s:rows * (s + 1), :].astype(BF16)

    return pl.pallas_call(
        body, name=name, grid=(N_CHIPS // group, n_k),
        in_specs=[pl.BlockSpec((bk, group * rows), lambda i, kk: (kk, i)),
                  pl.BlockSpec((bk, b.shape[1]), lambda i, kk: (kk, 0)), _ANY],
        out_specs=pl.BlockSpec((group, rows, b.shape[1]), lambda i, kk: (i, first // rows, 0)),
        out_shape=jax.ShapeDtypeStruct(parts.shape, BF16),
        scratch_shapes=[pltpu.VMEM((group * rows, b.shape[1]), F32)],
        input_output_aliases={2: 0},
        compiler_params=_params(("parallel", "arbitrary")),
    )(a, b, parts)


def _mix_out_bwd(dx1, o, z, sc_in, land_a, gn, scw, gs):
    T = dx1.shape[0]
    tb = 256

    def body(dx_ref, o_ref, z_ref, sc_ref, halo_ref, w_ref, gn_ref, scw_ref, gs_ref,
             do_ref, dz_ref, dgb_ref, dcv_ref, dxb_ref, dgn_ref, dgs_ref, dscw_ref):
        @pl.when(pl.program_id(0) == 0)
        def _():
            dgn_ref[...] = jnp.zeros_like(dgn_ref)
            dgs_ref[...] = jnp.zeros_like(dgs_ref)
            dscw_ref[...] = jnp.zeros_like(dscw_ref)

        dx_bf16 = dx_ref[...].astype(BF16)
        dxb_ref[...] = dx_bf16
        dmix = lax.dot_general(dx_bf16, _whole(w_ref), NT, preferred_element_type=F32)
        don = dmix[:, :DN_WIDTH]
        dosc = dmix[:, DN_WIDTH:]
        zv = z_ref[...]
        _, oh, rs, sz, gn4 = _dn_out(o_ref[...], zv, gn_ref[...])
        silu_z = zv * sz
        dgn_full = don * oh * silu_z
        dgn_ref[0:1, :] += jnp.sum(sum(dgn_full[:, HEAD_DIM * hh:HEAD_DIM * (hh + 1)] for hh in range(HEADS)),
                                   axis=0, keepdims=True)
        dz_ref[...] = (don * oh * gn4 * (sz * (1.0 + zv * (1.0 - sz)))).astype(BF16)
        t = don * gn4 * silu_z
        for hh in range(HEADS):
            sl = slice(HEAD_DIM * hh, HEAD_DIM * (hh + 1))
            th, ohh = t[:, sl], oh[:, sl]
            do_ref[:, sl] = rs[hh] * (th - ohh * jnp.mean(th * ohh, axis=-1, keepdims=True))
        halo = jnp.where(pl.program_id(0) > 0, halo_ref[...], 0.0)
        u, cv, gate_b, yh, rys = _sc_fwd(sc_ref[...], halo, scw_ref[...], tb)
        _row_acc(dgs_ref, dosc * yh)
        ty = dosc * gs_ref[...]
        gw = SC_WIDTH // SC_GROUPS
        dys = []
        for gi in range(SC_GROUPS):
            sl = slice(gw * gi, gw * (gi + 1))
            tg, yg = ty[:, sl], yh[:, sl]
            dys.append(rys[gi] * (tg - yg * jnp.mean(tg * yg, axis=-1, keepdims=True)))
        dy = jnp.concatenate(dys, axis=1)
        dgb_ref[...] = dy * cv
        dcv = dy * gate_b
        dcv_ref[...] = dcv
        for j in range(3):
            dscw_ref[j:j + 1, :] += jnp.sum(dcv * _rows_from(u, 6 + j, tb), axis=0, keepdims=True)

    tok = lambda w: pl.BlockSpec((tb, w), lambda i: (i, 0))
    full = lambda t: pl.BlockSpec(t.shape, lambda i: (0, 0))
    acc = lambda w: pl.BlockSpec((8, w), lambda i: (0, 0))
    return pl.pallas_call(
        body, name="mix_out_bwd", grid=(T // tb,),
        in_specs=[tok(D_MODEL), tok(DN_WIDTH), tok(DN_WIDTH), tok(3 * SC_WIDTH),
                  pl.BlockSpec((8, 3 * SC_WIDTH), _before_halo(tb)), _shard_rows(land_a, A_OUT_AT, OUT_SHARD),
                  full(gn), full(scw), full(gs)],
        out_specs=[tok(DN_WIDTH), tok(DN_WIDTH), tok(SC_WIDTH), tok(SC_WIDTH), tok(D_MODEL),
                   acc(HEAD_DIM), acc(SC_WIDTH), acc(SC_WIDTH)],
        out_shape=[jax.ShapeDtypeStruct((T, DN_WIDTH), F32), jax.ShapeDtypeStruct((T, DN_WIDTH), BF16),
                   jax.ShapeDtypeStruct((T, SC_WIDTH), F32), jax.ShapeDtypeStruct((T, SC_WIDTH), F32),
                   jax.ShapeDtypeStruct((T, D_MODEL), BF16),
                   jax.ShapeDtypeStruct((8, HEAD_DIM), F32), jax.ShapeDtypeStruct((8, SC_WIDTH), F32),
                   jax.ShapeDtypeStruct((8, SC_WIDTH), F32)],
        compiler_params=_params(("arbitrary",)),
    )(dx1, o, z, sc_in, sc_in, land_a, gn, scw, gs)


def _delta_bwd(q, k, v, bg, states, xms, do):
    T = q.shape[0]
    tb = 512
    n_chunk = tb // CHUNK
    nb = T // tb

    def body(q_ref, k_ref, v_ref, bg_ref, st_ref, xm_ref, do_ref, dq_ref, dk_ref, dv_ref, dbg_ref, ds_ref):
        @pl.when(pl.program_id(0) == 0)
        def _():
            ds_ref[...] = jnp.zeros_like(ds_ref)

        masks = _chunk_masks()
        causal, strict = masks
        lane = lax.broadcasted_iota(jnp.int32, (CHUNK, LANES), 1)
        last_row = lax.broadcasted_iota(jnp.int32, (CHUNK, 1), 0) == CHUNK - 1
        cat = jnp.concatenate
        heads = range(HEADS)

        def open_chunk(ci, loc):
            rows = pl.ds(pl.multiple_of(ci * CHUNK, CHUNK), CHUNK)
            dov = do_ref[rows, :]
            return dict(rows=rows, loc=loc, do=[dov[:, HEAD_DIM * h:HEAD_DIM * (h + 1)] for h in heads],
                        state=[st_ref[ci, h] for h in heads])

        def a_free(c):
            loc, do, state = c["loc"], c["do"], c["state"]
            w_s = [_mm(p["w"], s) for p, s in zip(loc, state)]
            c["dq_dec"] = [_mm(d, s, NT) for d, s in zip(do, state)]
            c["qk_do"] = [_mm(p["qk"], d, TN) for p, d in zip(loc, do)]
            c["qd_do"] = [_mm(p["q_dec"], d, TN) for p, d in zip(loc, do)]
            c["v_new"] = [p["u"] - t for p, t in zip(loc, w_s)]
            c["dqk"] = [jnp.where(causal, _mm(d, vn, NT), 0.0) for d, vn in zip(do, c["v_new"])]

        def a_state(c, ds_next):
            c["ds_next"] = ds_next
            kd_ds = [_mm(p["k_dec"], d) for p, d in zip(c["loc"], ds_next)]
            c["dk_dec"] = [_mm(vn, d, NT) for vn, d in zip(c["v_new"], ds_next)]
            c["dv_new"] = [a + b for a, b in zip(c["qk_do"], kd_ds)]

        def b_state(c):
            loc = c["loc"]
            w_dv = [_mm(p["w"], dvn, TN) for p, dvn in zip(loc, c["dv_new"])]
            c["dw"] = [-_mm(dvn, s, NT) for dvn, s in zip(c["dv_new"], c["state"])]
            return [loc[h]["gl"] * c["ds_next"][h] + c["qd_do"][h] - w_dv[h] for h in heads]

        def c_solve(c):
            loc, dv_new, dw = c["loc"], c["dv_new"], c["dw"]
            c["dtm"] = [_mm(cat([dvn, d], axis=1), cat([p["vb"], p["kbg"]], axis=1), NT) for dvn, d, p in zip(dv_new, dw, loc)]
            x_t = [_mm(p["xm"], cat([dvn, d], axis=1), TN) for p, dvn, d in zip(loc, dv_new, dw)]
            c["dvb"] = [dvn + t[:, :HEAD_DIM] for dvn, t in zip(dv_new, x_t)]
            c["dkbg"] = [d + t[:, HEAD_DIM:] for d, t in zip(dw, x_t)]

        def d_solve(c):
            c["y"] = [t + _mm(p["xm"], t, TN) for p, t in zip(c["loc"], c["dtm"])]

        def e_solve(c):
            c["dlow"] = [jnp.where(strict, -(t + _mm(t, p["xm"], NT)), 0.0) for p, t in zip(c["loc"], c["y"])]

        def f_close(c):
            loc, rows = c["loc"], c["rows"]
            dmm = [d * p["decay"] for d, p in zip(c["dlow"], loc)]
            dnn = [d * p["decay"] for d, p in zip(c["dqk"], loc)]
            by_k = [_mm(cat([a, b], axis=0), p["k"]) for a, b, p in zip(dmm, dnn, loc)]
            dk_mm = [_mm(cat([a, b], axis=0), cat([p["kb"], p["q"]], axis=0), TN) for a, b, p in zip(dmm, dnn, loc)]
            dq_out, dk_out, dv_out = [], [], []
            dbeta_all = jnp.zeros((CHUNK, LANES), F32)
            dgc_all = jnp.zeros((CHUNK, LANES), F32)
            for h in heads:
                p = loc[h]
                dkb = by_k[h][:CHUNK] + c["dkbg"][h] * p["eg"]
                dq_out.append(by_k[h][CHUNK:] + c["dq_dec"][h] * p["eg"])
                dk_out.append(dk_mm[h] + c["dk_dec"][h] * p["ek"] + dkb * p["beta"])
                dv_out.append(c["dvb"][h] * p["beta"])
                dbeta = jnp.sum(dkb * p["k"] + c["dvb"][h] * p["v"], axis=1, keepdims=True)
                e = c["dlow"][h] * p["low"] + c["dqk"][h] * p["qk"]
                kd = jnp.sum(c["dk_dec"][h] * p["k_dec"], axis=1, keepdims=True)
                dgc = (jnp.sum(e, axis=1, keepdims=True) - jnp.sum(e.T, axis=1, keepdims=True)
                       + jnp.sum(c["dq_dec"][h] * p["q_dec"], axis=1, keepdims=True) - kd
                       + jnp.sum(c["dkbg"][h] * p["kbg"], axis=1, keepdims=True))
                dgl = jnp.sum(jnp.sum(c["ds_next"][h] * c["state"][h], axis=1, keepdims=True), axis=0, keepdims=True)
                d_last = jnp.sum(kd, axis=0, keepdims=True) + dgl * p["gl"]
                dgc = dgc + jnp.where(last_row, d_last, 0.0)
                dbeta_all = jnp.where(lane == h, dbeta, dbeta_all)
                dgc_all = jnp.where(lane == h + HEADS, dgc, dgc_all)
            dq_ref[rows, :] = cat(dq_out, axis=1)
            dk_ref[rows, :] = cat(dk_out, axis=1)
            dv_ref[rows, :] = cat(dv_out, axis=1)
            dbg_ref[rows, :] = dbeta_all + dgc_all

        def group(gj, carry):
            first = n_chunk - 1 - BWD_GROUP * gj
            ids = [first - j for j in range(BWD_GROUP)]
            rows = [pl.ds(pl.multiple_of(ci * CHUNK, CHUNK), CHUNK) for ci in ids]
            loc = _units_local(sum((_chunk_units(q_ref, k_ref, v_ref, bg_ref, r) for r in rows), []), masks,
                               xms=[xm_ref[ci, h] for ci in ids for h in heads])
            chunks = [open_chunk(ci, loc[HEADS * j:HEADS * (j + 1)]) for j, ci in enumerate(ids)]
            for c in chunks:
                a_free(c)
            ds_cur = [ds_ref[h] for h in heads]
            later = (c_solve, d_solve, e_solve, f_close)
            for t in range(2 * (BWD_GROUP - 1) + 2 + len(later)):
                for j, c in enumerate(chunks):
                    stage = t - 2 * j
                    if stage == 0:
                        a_state(c, ds_cur)
                    elif stage == 1:
                        ds_cur = b_state(c)
                    elif 2 <= stage < 2 + len(later):
                        later[stage - 2](c)
            for h in heads:
                ds_ref[h] = ds_cur[h]
            return carry

        lax.fori_loop(0, n_chunk // BWD_GROUP, group, 0)

    tok = lambda w: pl.BlockSpec((tb, w), lambda i: (nb - 1 - i, 0))
    return pl.pallas_call(
        body, name="delta_bwd", grid=(nb,),
        in_specs=[tok(DN_WIDTH), tok(DN_WIDTH), tok(DN_WIDTH), tok(LANES),
                  pl.BlockSpec((n_chunk, HEADS, HEAD_DIM, HEAD_DIM), lambda i: (nb - 1 - i, 0, 0, 0)),
                  pl.BlockSpec((n_chunk, HEADS, CHUNK, CHUNK), lambda i: (nb - 1 - i, 0, 0, 0)), tok(DN_WIDTH)],
        out_specs=[tok(DN_WIDTH), tok(DN_WIDTH), tok(DN_WIDTH), tok(LANES)],
        out_shape=[jax.ShapeDtypeStruct((T, DN_WIDTH), F32)] * 3 + [jax.ShapeDtypeStruct((T, LANES), F32)],
        scratch_shapes=[pltpu.VMEM((HEADS, HEAD_DIM, HEAD_DIM), F32)],
        compiler_params=_params(("arbitrary",)),
    )(q, k, v, bg, states, xms, do)


def _dn_prep_back(dq, dk, dv, dbg, pre, halo, cw, bd, al_row, dt_row, tb):
    xc, c, sg, a = _dn_act(pre, halo, cw, tb)
    dsilu = sg * (1.0 + c * (1.0 - sg))
    pieces = [None] * (2 * HEADS)
    for hd in range(HEADS):
        sl = slice(HEAD_DIM * hd, HEAD_DIM * (hd + 1))
        for which, (base, grad, scale) in enumerate(((0, dq, Q_SCALE), (DN_WIDTH, dk, 1.0))):
            sa = slice(base + HEAD_DIM * hd, base + HEAD_DIM * (hd + 1))
            raw = a[:, sa]
            r = lax.rsqrt(jnp.sum(raw * raw, axis=-1, keepdims=True) + EPS)
            nrm = raw * r
            gn_ = grad[:, sl] * scale
            pieces[which * HEADS + hd] = r * (gn_ - nrm * jnp.sum(gn_ * nrm, axis=-1, keepdims=True)) * dsilu[:, sa]
    dc = jnp.concatenate(pieces + [dv * dsilu[:, 2 * DN_WIDTH:]], axis=1)
    dcw_rows = [jnp.sum(dc * _rows_from(xc, 5 + j, tb), axis=0, keepdims=True) for j in range(4)]
    lane = lax.broadcasted_iota(jnp.int32, bd.shape, 1)
    is_b = lane < HEADS
    is_g = jnp.logical_and(lane >= HEADS, lane < 2 * HEADS)
    dbgv = jnp.where(is_b, dbg, _mm32(_chunk_cumsum_matrix(tb), dbg, TN))
    beta = _sigmoid(bd)
    neg_a = -jnp.exp(al_row)
    pre_sp = bd + dt_row
    g = neg_a * _softplus(pre_sp)
    da_in = dbgv * neg_a * _sigmoid(pre_sp)
    dbd = jnp.where(is_b, dbgv * beta * (1.0 - beta), jnp.where(is_g, da_in, 0.0)).astype(BF16)
    dal_row = jnp.sum(jnp.where(is_g, dbgv * g, 0.0), axis=0, keepdims=True)
    ddt_row = jnp.sum(jnp.where(is_g, da_in, 0.0), axis=0, keepdims=True)
    return dc, dbd, dcw_rows, dal_row, ddt_row


def _dp_of_chip(dqkv, dz, dbd, dsc, s):
    lo, hi = IN_SHARD * s, IN_SHARD * (s + 1)
    pieces = []
    for w_at, w_end, block in ((0, W_Z, dqkv), (W_Z, W_BD, dz), (W_BD, W_SC, dbd), (W_SC, W_IN_COLS, dsc)):
        a, b = max(lo, w_at), min(hi, w_end)
        if a < b:
            pieces.append(block[:, a - w_at:b - w_at])
    pieces.append(jnp.zeros((dqkv.shape[0], D_MODEL - IN_SHARD), dqkv.dtype))
    return jnp.concatenate(pieces, axis=1)


def _in_proj_bwd(dq, dk, dv, dbg, qkv, bd, al_row, dt_row, dcv, dgb, sc_in, dz, cw, scw, dx1, x, g1, land_a):
    T = x.shape[0]
    tb = 256
    nb = T // tb

    def body(dq_ref, dk_ref, dv_ref, dbg_ref, pre_ref, pre_halo_ref, bd_ref, al_ref, dt_ref,
             dcv_ref, dcv_halo_ref, dgb_ref, sc_ref, dz_ref, cw_ref, scw_ref, dx1_ref, x_ref, g_ref, w_ref,
             dx_ref, dxb_ref, dps_ref, dg_ref, dcw_ref, dal_ref, ddt_ref, head_ref):
        @pl.when(pl.program_id(0) == 0)
        def _():
            for ref in (dg_ref, dcw_ref, dal_ref, ddt_ref, head_ref):
                ref[...] = jnp.zeros_like(ref)

        block = nb - 1 - pl.program_id(0)
        last = block == nb - 1
        pre_halo = jnp.where(block > 0, pre_halo_ref[...], 0.0)
        dc, dbd, dcw_rows, dal_row, ddt_row = _dn_prep_back(
            dq_ref[...], dk_ref[...], dv_ref[...], dbg_ref[...], pre_ref[...], pre_halo, cw_ref[...], bd_ref[...],
            al_ref[...], dt_ref[...], tb)
        for j in range(4):
            dcw_ref[j:j + 1, :] += dcw_rows[j]
        dal_ref[0:1, :] += dal_row
        ddt_ref[0:1, :] += ddt_row
        xc = jnp.concatenate([dc, head_ref[...]], axis=0)
        head_ref[...] = dc[0:8, :]
        w4 = cw_ref[...]
        dqkv = w4[3:4, :] * xc[0:tb, :]
        for j in range(3):
            dqkv = dqkv + w4[j:j + 1, :] * _rows_from(xc, 3 - j, tb)
        yc = jnp.concatenate([dcv_ref[...], jnp.where(last, 0.0, dcv_halo_ref[...])], axis=0)
        w3 = scw_ref[...]
        du = w3[2:3, :] * yc[0:tb, :] + w3[1:2, :] * _rows_from(yc, 1, tb) + w3[0:1, :] * _rows_from(yc, 2, tb)
        sc = sc_ref[...]
        dsc = jnp.concatenate([dgb_ref[...], du * sc[:, 2 * SC_WIDTH:], du * sc[:, SC_WIDTH:2 * SC_WIDTH]], axis=1)
        blocks = (dqkv.astype(BF16), dz_ref[...], dbd, dsc.astype(BF16))
        dh = jnp.zeros((tb, D_MODEL), F32)
        for s in range(N_CHIPS):
            dps = _dp_of_chip(*blocks, s)
            dps_ref[:, D_MODEL * s:D_MODEL * (s + 1)] = dps
            dh = dh + lax.dot_general(dps, w_ref[s], NT, preferred_element_type=F32)
        xv = x_ref[...]
        r = lax.rsqrt(jnp.mean(xv * xv, axis=-1, keepdims=True) + EPS)
        xh = xv * r
        _row_acc(dg_ref, dh * xh)
        dx = dx1_ref[...] + _rms_bwd(dh, xh, r, g_ref[...])
        dx_ref[...] = dx
        dxb_ref[...] = dx.astype(BF16)

    tok = lambda w: pl.BlockSpec((tb, w), lambda i: (nb - 1 - i, 0))
    full = lambda t: pl.BlockSpec(t.shape, lambda i: (0, 0))
    acc = lambda w: pl.BlockSpec((8, w), lambda i: (0, 0))
    before = lambda w: pl.BlockSpec((8, w), lambda i: _before_halo(tb)(nb - 1 - i))
    after = lambda w: pl.BlockSpec((8, w), lambda i: _after_halo(tb, T)(nb - 1 - i))
    return pl.pallas_call(
        body, name="in_proj_bwd", grid=(nb,),
        in_specs=[tok(DN_WIDTH), tok(DN_WIDTH), tok(DN_WIDTH), tok(LANES), tok(QKV), before(QKV), tok(LANES),
                  full(al_row), full(dt_row), tok(SC_WIDTH), after(SC_WIDTH), tok(SC_WIDTH), tok(3 * SC_WIDTH),
                  tok(DN_WIDTH), full(cw), full(scw), tok(D_MODEL), tok(D_MODEL), full(g1), _shard_rows(land_a, 0, D_MODEL)],
        out_specs=[tok(D_MODEL), tok(D_MODEL), tok(N_CHIPS * D_MODEL), acc(D_MODEL), acc(QKV), acc(LANES), acc(LANES)],
        out_shape=[jax.ShapeDtypeStruct((T, D_MODEL), F32), jax.ShapeDtypeStruct((T, D_MODEL), BF16),
                   jax.ShapeDtypeStruct((T, N_CHIPS * D_MODEL), BF16), jax.ShapeDtypeStruct((8, D_MODEL), F32),
                   jax.ShapeDtypeStruct((8, QKV), F32), jax.ShapeDtypeStruct((8, LANES), F32),
                   jax.ShapeDtypeStruct((8, LANES), F32)],
        scratch_shapes=[pltpu.VMEM((8, QKV), F32)],
        compiler_params=_params(("arbitrary",)),
    )(dq, dk, dv, dbg, qkv, qkv, bd, al_row, dt_row, dcv, dcv, dgb, sc_in, dz, cw, scw, dx1, x, g1, land_a)


def _wgrad_in_share(h, dps, parts, name):
    T = h.shape[0]
    bk = min(T, WGRAD_TOKENS)
    n_k = T // bk

    def body(a_ref, b_ref, parts_ref, o_ref, acc_ref):
        kk = pl.program_id(1)

        @pl.when(kk == 0)
        def _():
            acc_ref[...] = jnp.zeros_like(acc_ref)

        acc_ref[...] += lax.dot_general(a_ref[...], b_ref[...], TN, preferred_element_type=F32)

        @pl.when(kk == n_k - 1)
        def _():
            o_ref[0] = acc_ref[...].astype(BF16)

    return pl.pallas_call(
        body, name=name, grid=(N_CHIPS, n_k),
        in_specs=[pl.BlockSpec((bk, D_MODEL), lambda j, kk: (kk, 0)), pl.BlockSpec((bk, D_MODEL), lambda j, kk: (kk, j)), _ANY],
        out_specs=pl.BlockSpec((1, D_MODEL, D_MODEL), lambda j, kk: (j, 0, 0)),
        out_shape=jax.ShapeDtypeStruct(parts.shape, BF16),
        scratch_shapes=[pltpu.VMEM((D_MODEL, D_MODEL), F32)],
        input_output_aliases={2: 0},
        compiler_params=_params(("parallel", "arbitrary")),
    )(h, dps, parts)


def _pad_rows(a, rows=8):
    return jnp.pad(a, ((0, rows - a.shape[0]), (0, 0)))


def _gate_rows(a_log, dt_bias):
    put = lambda t: jnp.pad(t.reshape(1, HEADS), ((0, 0), (HEADS, LANES - 2 * HEADS)))
    return put(a_log), put(dt_bias)


def _mixer_fwd(x, p):
    qkv, z, sc_in, bd, h, q, k, v, bg = _in_proj(x, p["g1"], p["land_a"], p["cw"], p["al"], p["dt"])
    o, states, xms = _delta_fwd(q, k, v, bg)
    return dict(x=x, qkv=qkv, z=z, sc_in=sc_in, bd=bd, h=h, q=q, k=k, v=v, bg=bg, o=o, states=states, xms=xms)


def _tail_fwd(s, p, land_b):
    x1, mix, x2, a, b, h2 = _mix_ffn(s["o"], s["z"], s["sc_in"], s["x"], p["land_a"], p["gn"], p["scw"], p["gs"],
                                     p["g2"], land_b)
    return x2, dict(s, mix=mix), dict(x1=x1, a=a, b=b, h2=h2)


def _ffn_back(dx2, dx2_bf16, s, p, land_b):
    dx1, da, db, act, dg2 = _ffn_bwd(dx2, s["x1"], s["a"], s["b"], p["g2"], land_b)
    parts = lax.empty((N_CHIPS, B_ROWS, D_MODEL), BF16)
    parts = _wgrad_share(act, dx2_bf16, parts, 2 * FF_SHARD, "wgrad_down")
    parts = _wgrad_share(da, s["h2"], parts, 0, "wgrad_gate")
    parts = _wgrad_share(db, s["h2"], parts, FF_SHARD, "wgrad_up")
    return dx1, parts, dg2[0]


def _mixer_bwd(dx1, s, p):
    do, dz, dgb, dcv, dx1_bf16, dgn, dgs, dscw = _mix_out_bwd(dx1, s["o"], s["z"], s["sc_in"], p["land_a"], p["gn"],
                                                              p["scw"], p["gs"])
    dq, dk, dv, dbg = _delta_bwd(s["q"], s["k"], s["v"], s["bg"], s["states"], s["xms"], do)
    dx, dx_bf16, dps, dg1, dcw, dal, ddt = _in_proj_bwd(
        dq, dk, dv, dbg, s["qkv"], s["bd"], p["al"], p["dt"], dcv, dgb, s["sc_in"], dz, p["cw"], p["scw"], dx1, s["x"],
        p["g1"], p["land_a"])
    parts = lax.empty((N_CHIPS, A_ROWS, D_MODEL), BF16)
    parts = _wgrad_in_share(s["h"], dps, parts, "wgrad_in")
    parts = _wgrad_share(s["mix"], dx1_bf16, parts, A_OUT_AT, "wgrad_out")
    g = dict(g1=dg1[0], gn=dgn[0], gs=dgs[0], scw=dscw[:3], cw=dcw[:4], al=dal[0, HEADS:2 * HEADS], dt=ddt[0, HEADS:2 * HEADS])
    return dx, dx_bf16, parts, g


def _place():
    return lax.axis_index("x"), lax.axis_index("y"), lax.axis_index("c")


def _other_chips(x, y):
    return [(1 - x, y), (x, 1 - y), (1 - x, 1 - y)]


_HBM = pl.BlockSpec(memory_space=pltpu.HBM)


def _chip_exchange(arrs, name, gather):
    n = len(arrs)

    def body(*refs):
        ins, outs = refs[:n], refs[n:2 * n]
        send_sems, recv_sems, local_sems = refs[2 * n:]
        x, y, c = _place()
        me = 2 * x + y
        others = _other_chips(x, y)

        def remote(k, j, landing):
            px, py = others[j]
            src = ins[k] if gather else ins[k].at[2 * px + py]
            return pltpu.make_async_remote_copy(src_ref=src, dst_ref=outs[k].at[landing], send_sem=send_sems.at[k, j],
                                                recv_sem=recv_sems.at[k, j], device_id=(px, py, c), device_id_type=MESH)

        local = [pltpu.make_async_copy(ins[k] if gather else ins[k].at[me], outs[k].at[me], local_sems.at[k])
                 for k in range(n)]
        sends = [remote(k, j, me) for k in range(n) for j in range(3)]
        for cp in local + sends:
            cp.start()
        for k in range(n):
            for j, (px, py) in enumerate(others):
                remote(k, j, 2 * px + py).wait_recv()
        for cp in sends:
            cp.wait_send()
        for cp in local:
            cp.wait()

    shapes = [jax.ShapeDtypeStruct(((N_CHIPS,) + a.shape) if gather else a.shape, a.dtype) for a in arrs]
    return pl.pallas_call(
        body, name=name, in_specs=[_HBM] * n, out_specs=[_HBM] * n, out_shape=shapes,
        scratch_shapes=[pltpu.SemaphoreType.DMA((n, 3)), pltpu.SemaphoreType.DMA((n, 3)), pltpu.SemaphoreType.DMA((n,))],
    )(*arrs)


_SEM = pl.BlockSpec(memory_space=pltpu.SEMAPHORE)
_ANY = pl.BlockSpec(memory_space=pl.ANY)
_EFFECT = pltpu.SideEffectType.DATAFLOW_SIDE_EFFECTING


_FLIPS = [(a, b, cc) for a in (0, 1) for b in (0, 1) for cc in (0, 1)][1:]


def _split_copies(src_ref, land_ref, send_sems, recv_sems, gather, sending):
    x, y, c = _place()
    copies = []
    if gather:
        me = 2 * x + y
        for j, (px, py) in enumerate(_other_chips(x, y)):
            copies.append(pltpu.make_async_remote_copy(
                src_ref=src_ref, dst_ref=land_ref.at[me if sending else 2 * px + py],
                send_sem=send_sems.at[j], recv_sem=recv_sems.at[j], device_id=(px, py, c), device_id_type=MESH))
        return copies
    me = 4 * x + 2 * y + c
    for j, (a, b, cc) in enumerate(_FLIPS):
        px, py, pc = (1 - x) if a else x, (1 - y) if b else y, (1 - c) if cc else c
        copies.append(pltpu.make_async_remote_copy(
            src_ref=src_ref.at[2 * px + py], dst_ref=land_ref.at[me if sending else 4 * px + 2 * py + pc],
            send_sem=send_sems.at[j], recv_sem=recv_sems.at[j], device_id=(px, py, pc), device_id_type=MESH))
    return copies


def _own_slot(share):
    chip = 2 * lax.axis_index("x") + lax.axis_index("y")
    return lax.dynamic_update_slice(lax.empty((N_CHIPS,) + share.shape, share.dtype), share[None], (chip, 0, 0))


def _own_part(parts):
    chip = 2 * lax.axis_index("x") + lax.axis_index("y")
    own = lax.dynamic_index_in_dim(parts, chip, 0, keepdims=True)
    return lax.dynamic_update_slice(lax.empty((N_DEV,) + parts.shape[1:], parts.dtype), own,
                                    (2 * chip + lax.axis_index("c"), 0, 0))


def _exchange_start(src, land, after, name, gather):
    def body(src_ref, land_ref, after_ref, send_sems, recv_sems, src_thru, land_thru, token):
        for cp in _split_copies(src_ref, land_ref, send_sems, recv_sems, gather, sending=True):
            cp.start()
        token[...] = jnp.zeros_like(token)

    hbm = lambda t: pltpu.with_memory_space_constraint(t, pltpu.HBM)
    n_copies = N_CHIPS - 1 if gather else N_DEV - 1
    return pl.pallas_call(
        body, name=name,
        out_shape=(pltpu.SemaphoreType.DMA((n_copies,)), pltpu.SemaphoreType.DMA((n_copies,)), pltpu.HBM(src.shape, src.dtype),
                   pltpu.HBM(land.shape, land.dtype), jax.ShapeDtypeStruct((8, LANES), F32)),
        in_specs=(_HBM, _HBM, _ANY), out_specs=(_SEM, _SEM, _HBM, _HBM, pl.BlockSpec(memory_space=pltpu.VMEM)),
        input_output_aliases={0: 2, 1: 3},
        compiler_params=pltpu.CompilerParams(has_side_effects=_EFFECT),
    )(hbm(src), hbm(land), after)


def _exchange_wait(started, after, name, gather):
    send_sems, recv_sems, src_thru, land_thru, _ = started

    def body(src_ref, land_ref, send_sems, recv_sems, after_ref, src_dead, got_ref):
        for cp in _split_copies(src_ref, land_ref, send_sems, recv_sems, gather, sending=False):
            cp.wait_send()
            cp.wait_recv()

    return pl.pallas_call(
        body, name=name,
        out_shape=(pltpu.HBM(src_thru.shape, src_thru.dtype), pltpu.HBM(land_thru.shape, land_thru.dtype)),
        in_specs=(_HBM, _HBM, _SEM, _SEM, _ANY), out_specs=(_HBM, _HBM), input_output_aliases={0: 0, 1: 1},
        compiler_params=pltpu.CompilerParams(has_side_effects=_EFFECT),
    )(src_thru, land_thru, send_sems, recv_sems, after)[1]


def _all_reduce_small(v):
    rows = v.shape[0]
    flips = [(a, b, cc) for a in (0, 1) for b in (0, 1) for cc in (0, 1)][1:]

    def body(v_ref, out_ref, buf_ref, send_sems, recv_sems):
        x, y, c = _place()
        me = 4 * x + 2 * y + c
        peers = [((1 - x) if a else x, (1 - y) if b else y, (1 - c) if cc else c) for a, b, cc in flips]

        def copy(j, landing):
            return pltpu.make_async_remote_copy(src_ref=v_ref, dst_ref=buf_ref.at[landing], send_sem=send_sems.at[j],
                                                recv_sem=recv_sems.at[j], device_id=peers[j], device_id_type=MESH)

        sends = [copy(j, me) for j in range(N_DEV - 1)]
        for cp in sends:
            cp.start()
        buf_ref[me] = v_ref[...]
        for j, (px, py, pc) in enumerate(peers):
            copy(j, 4 * px + 2 * py + pc).wait_recv()
        for cp in sends:
            cp.wait_send()
        acc = buf_ref[0]
        for d in range(1, N_DEV):
            acc = acc + buf_ref[d]
        out_ref[...] = acc

    vmem = pl.BlockSpec(memory_space=pltpu.VMEM)
    return pl.pallas_call(
        body, name="all_reduce_small", in_specs=[vmem], out_specs=vmem,
        out_shape=jax.ShapeDtypeStruct(v.shape, F32),
        scratch_shapes=[pltpu.VMEM((N_DEV, rows, LANES), F32), pltpu.SemaphoreType.DMA((N_DEV - 1,)),
                        pltpu.SemaphoreType.DMA((N_DEV - 1,))],
    )(v)


def _row_block(*sizes):
    return next(t for t in (176, 128, 64) if all(s % t == 0 for s in sizes))


def _adam_update(w, m, v, g):
    r1 = 1.0 / (1.0 - ADAM_B1 ** ADAM_STEP)
    r2 = 1.0 / (1.0 - ADAM_B2 ** ADAM_STEP)
    m_new = ADAM_B1 * m + (1.0 - ADAM_B1) * g
    v_new = ADAM_B2 * v + (1.0 - ADAM_B2) * (g * g)
    return -ADAM_LR * ((m_new * r1) / (jnp.sqrt(v_new * r2) + ADAM_EPS) + ADAM_WD * w), m_new, v_new


def _adamw_rows(w, m, v, got, first, name):
    n_layers, rows, cols = w.shape
    tr = _row_block(rows, first)

    def body(*refs):
        w_ref, m_ref, v_ref = refs[:3]
        g_out, d_out, m_out, v_out = refs[3 + n_layers:]
        for k in range(n_layers):
            @pl.when(pl.program_id(0) == k)
            def _(p_ref=refs[3 + k]):
                g = p_ref[0].astype(F32)
                for d in range(1, N_DEV):
                    g = g + p_ref[d].astype(F32)
                g = g[:, :cols]
                d_out[0], m_out[0], v_out[0] = _adam_update(w_ref[0], m_ref[0], v_ref[0], g)
                g_out[0] = g

    blk = pl.BlockSpec((1, tr, cols), lambda l, i: (l, i, 0))
    parts = [pl.BlockSpec((N_DEV, tr, got[0].shape[2]), lambda l, i, k=k: (0, jnp.where(l == k, first // tr + i, 0), 0))
             for k in range(n_layers)]
    return pl.pallas_call(
        body, name=name, grid=(n_layers, rows // tr),
        in_specs=[blk] * 3 + parts, out_specs=[blk] * 4,
        out_shape=[jax.ShapeDtypeStruct(w.shape, F32)] * 4,
        compiler_params=_params(("arbitrary", "arbitrary")),
    )(w, m, v, *got)


def _adamw(w, m, v, g_parts, name):
    rows, cols = w.shape
    tr = min(rows, 256)
    n = len(g_parts)

    def body(*refs):
        w_ref, m_ref, v_ref = refs[:3]
        g_refs = refs[3:3 + n]
        g_out, d_out, m_out, v_out = refs[3 + n:]
        g = g_refs[0][...]
        for r in g_refs[1:]:
            g = g + r[...]
        d_out[...], m_out[...], v_out[...] = _adam_update(w_ref[...], m_ref[...], v_ref[...], g)
        g_out[...] = g

    blk = pl.BlockSpec((tr, cols), lambda i: (i, 0))
    return pl.pallas_call(
        body, name=name, grid=(rows // tr,),
        in_specs=[blk] * (3 + n), out_specs=[blk] * 4,
        out_shape=[jax.ShapeDtypeStruct((rows, cols), F32)] * 4,
        compiler_params=_params(("parallel",)),
    )(w, m, v, *g_parts)


def _pack(parts, rows, fill=0.0):
    flat = jnp.concatenate([p.reshape(-1) for p in parts])
    return jnp.pad(flat, (0, rows * LANES - flat.shape[0]), constant_values=fill).reshape(rows, LANES)


def _unpack(packed, shapes):
    flat = packed.reshape(-1)
    out, at = [], 0
    for shp in shapes:
        size = 1
        for s in shp:
            size *= s
        out.append(flat[at:at + size].reshape(shp))
        at += size
    return out


def _packed_rows(shapes):
    total = 0
    for shp in shapes:
        size = 1
        for s in shp:
            size *= s
        total += size
    return -(-total // (8 * LANES)) * 8


def _cols_full(g, l):
    t = g[:, l]
    return jnp.moveaxis(t, 0, 1).reshape(t.shape[1], N_CHIPS * t.shape[2])


def _pad_cols(t):
    return jnp.pad(t, ((0, 0),) * (t.ndim - 1) + ((0, D_MODEL - t.shape[-1]),))


def kernel(x, norm1_g, w_in, dn_conv_w, dn_a_log, dn_dt_bias, dn_norm_g, sc_conv_w, sc_norm_g, w_out, norm2_g, ffn_w_gate, ffn_w_up, ffn_w_down, final_norm_g, loss_target, m_norm1_g, m_w_in, m_dn_conv_w, m_dn_a_log, m_dn_dt_bias, m_dn_norm_g, m_sc_conv_w, m_sc_norm_g, m_w_out, m_norm2_g, m_ffn_w_gate, m_ffn_w_up, m_ffn_w_down, m_final_norm_g, v_norm1_g, v_w_in, v_dn_conv_w, v_dn_a_log, v_dn_dt_bias, v_dn_norm_g, v_sc_conv_w, v_sc_norm_g, v_w_out, v_norm2_g, v_ffn_w_gate, v_ffn_w_up, v_ffn_w_down, v_final_norm_g):
    chip = 2 * lax.axis_index("x") + lax.axis_index("y")

    g_cw, g_scw = _chip_exchange([dn_conv_w, sc_conv_w], "gather_conv", gather=True)

    t_last = lambda t: jnp.swapaxes(t, -1, -2)
    gate_t, up_t = t_last(ffn_w_gate), t_last(ffn_w_up)
    zero_token = jnp.zeros((8, LANES), F32)

    def shares(l, tie):
        share_a = jnp.concatenate([_pad_cols(w_in[l] + tie), w_out[l]], axis=0).astype(BF16)
        share_b = jnp.concatenate([gate_t[l] + tie, up_t[l], ffn_w_down[l]], axis=0).astype(BF16)
        return share_a, _own_slot(share_a), share_b, _own_slot(share_b)

    def gather_start(l, packed, after):
        a = _exchange_start(packed[0], packed[1], after, "gather_a_start_%d" % l, gather=True)
        b = _exchange_start(packed[2], packed[3], a[4], "gather_b_start_%d" % l, gather=True)
        return a, b

    ga, gb = gather_start(0, shares(0, 0.0), g_cw)
    packed = [None] + [shares(l, gb[4][0, 0]) for l in range(1, DEPTH)]
    packed_all = sum(t[0, 0].astype(F32) for p in packed[1:] for t in (p[0], p[2]))
    land_a = _exchange_wait(ga, zero_token + packed_all, "gather_a_wait_0", gather=True)
    act = x[0]
    layers, saved_m, saved_f, lands_b = [], [], [], []
    for l in range(DEPTH):
        hold = 0.0
        if l + 1 < DEPTH:
            ga, gb_next = gather_start(l + 1, packed[l + 1], land_a)
            hold = gb_next[4][0:1, 0:1]
        al, dt = _gate_rows(dn_a_log[l], dn_dt_bias[l])
        layers.append(dict(
            g1=norm1_g[l][None] + hold, cw=_pad_rows(_cols_full(g_cw, l)), al=al, dt=dt,
            gn=dn_norm_g[l][None], scw=_pad_rows(_cols_full(g_scw, l)), gs=sc_norm_g[l][None],
            land_a=land_a, g2=norm2_g[l][None]))
        s = _mixer_fwd(act, layers[l])
        lands_b.append(_exchange_wait(gb, s["o"], "gather_b_wait_%d" % l, gather=True))
        act, s, sf = _tail_fwd(s, layers[l], lands_b[l])
        saved_m.append(s)
        saved_f.append(sf)
        if l + 1 < DEPTH:
            land_a = _exchange_wait(ga, act, "gather_a_wait_%d" % (l + 1), gather=True)
            gb = gb_next

    dact, dact_bf16, loss_part, d_final = _loss_head(act, final_norm_g[None], loss_target[0])
    grads, reduce_a, reduce_b = [None] * DEPTH, [None] * DEPTH, [None] * DEPTH
    hold = 0.0
    for l in reversed(range(DEPTH)):
        p = layers[l]
        dx1, parts, dg2 = _ffn_back(dact, dact_bf16, saved_f[l], dict(p, g2=p["g2"] + hold), lands_b[l])
        reduce_b[l] = _exchange_start(parts, _own_part(parts), zero_token, "reduce_b_start_%d" % l, gather=False)
        dact, dact_bf16, parts, gm = _mixer_bwd(dx1, saved_m[l], dict(p, gn=p["gn"] + reduce_b[l][4][0:1, 0:1]))
        reduce_a[l] = _exchange_start(parts, _own_part(parts), zero_token, "reduce_a_start_%d" % l, gather=False)
        hold = reduce_a[l][4][0:1, 0:1]
        grads[l] = dict(gm, g2=dg2)
    loss = lax.psum(loss_part[0, 0], ("x", "y", "c"))
    stack = lambda key: jnp.stack([grads[l][key] for l in range(DEPTH)])

    got_b = [_exchange_wait(reduce_b[l], reduce_a[0][4], "reduce_b_wait_%d" % l, gather=False)
             for l in reversed(range(DEPTH))][::-1]
    big = dict(
        ffn_w_gate=[t_last(o) for o in _adamw_rows(gate_t, t_last(m_ffn_w_gate), t_last(v_ffn_w_gate), got_b, 0, "adamw_gate")],
        ffn_w_up=[t_last(o) for o in _adamw_rows(up_t, t_last(m_ffn_w_up), t_last(v_ffn_w_up), got_b, FF_SHARD, "adamw_up")],
        ffn_w_down=_adamw_rows(ffn_w_down, m_ffn_w_down, v_ffn_w_down, got_b, 2 * FF_SHARD, "adamw_down"))
    after_b = zero_token + sum(big[n][1][0, 0, 0] for n in ("ffn_w_gate", "ffn_w_up", "ffn_w_down"))
    got_a = [_exchange_wait(reduce_a[l], after_b, "reduce_a_wait_%d" % l, gather=False) for l in reversed(range(DEPTH))][::-1]
    big.update(
        w_in=_adamw_rows(w_in, m_w_in, v_w_in, got_a, 0, "adamw_w_in"),
        w_out=_adamw_rows(w_out, m_w_out, v_w_out, got_a, A_OUT_AT, "adamw_w_out"))

    full_shapes = [(DEPTH, D_MODEL), (DEPTH, D_MODEL), (DEPTH, HEAD_DIM), (DEPTH, SC_WIDTH), (DEPTH, HEADS),
                   (DEPTH, HEADS), (D_MODEL,), (DEPTH, 4, QKV), (DEPTH, 3, SC_WIDTH)]
    small_keys = ("g1", "g2", "gn", "gs", "al", "dt")
    packed = _pack([stack(k) for k in small_keys] + [d_final[0], stack("cw"), stack("scw")], _packed_rows(full_shapes))
    sg = _unpack(_all_reduce_small(packed), full_shapes)
    sg[7] = lax.dynamic_slice_in_dim(sg[7], chip * (QKV // N_CHIPS), QKV // N_CHIPS, axis=2)
    sg[8] = lax.dynamic_slice_in_dim(sg[8], chip * (SC_WIDTH // N_CHIPS), SC_WIDTH // N_CHIPS, axis=2)
    small_names = ("norm1_g", "norm2_g", "dn_norm_g", "sc_norm_g", "dn_a_log", "dn_dt_bias", "final_norm_g",
                   "dn_conv_w", "sc_conv_w")
    sw = (norm1_g, norm2_g, dn_norm_g, sc_norm_g, dn_a_log, dn_dt_bias, final_norm_g, dn_conv_w, sc_conv_w)
    sm = (m_norm1_g, m_norm2_g, m_dn_norm_g, m_sc_norm_g, m_dn_a_log, m_dn_dt_bias, m_final_norm_g, m_dn_conv_w, m_sc_conv_w)
    sv = (v_norm1_g, v_norm2_g, v_dn_norm_g, v_sc_norm_g, v_dn_a_log, v_dn_dt_bias, v_final_norm_g, v_dn_conv_w, v_sc_conv_w)
    shard_shapes = [t.shape for t in sw]
    rows = _packed_rows(shard_shapes)
    outs = _adamw(_pack(sw, rows), _pack(sm, rows), _pack(sv, rows, fill=1.0), [_pack(sg, rows)], "adamw_small")
    small = {name: [] for name in small_names}
    for o in outs:
        for name, t in zip(small_names, _unpack(o, shard_shapes)):
            small[name].append(t)

    order = ("norm1_g", "w_in", "dn_conv_w", "dn_a_log", "dn_dt_bias", "dn_norm_g", "sc_conv_w", "sc_norm_g", "w_out",
             "norm2_g", "ffn_w_gate", "ffn_w_up", "ffn_w_down", "final_norm_g")
    result = {**big, **small}
    return (loss, dact[None], *[result[n][0] for n in order], *[result[n][1] for n in order],
            *[result[n][2] for n in order], *[result[n][3] for n in order])
```

```python
import jax
import jax.numpy as jnp
from jax import lax
from jax.experimental import pallas as pl
from jax.experimental.pallas import tpu as pltpu

F32 = jnp.float32
BF16 = jnp.bfloat16
MESH = pl.DeviceIdType.MESH

D_MODEL = 1024
DEPTH = 4
HEADS = 4
HEAD_DIM = 128
DN_WIDTH = HEADS * HEAD_DIM
SC_WIDTH = 512
SC_GROUPS = 4
D_FF = 2816
CHUNK = 64
QKV = 3 * DN_WIDTH
W_IN_COLS = 4 * DN_WIDTH + 2 * HEADS + 3 * SC_WIDTH
LANES = 128
EPS = 1e-6
Q_SCALE = HEAD_DIM ** -0.5
N_CHIPS = 4
N_DEV = 8
IN_SHARD = W_IN_COLS // N_CHIPS
OUT_SHARD = D_MODEL // N_CHIPS
FF_SHARD = D_FF // N_CHIPS
A_OUT_AT = D_MODEL
A_ROWS = D_MODEL + OUT_SHARD
B_ROWS = 3 * FF_SHARD

ADAM_LR = 0.001
ADAM_B1 = 0.9
ADAM_B2 = 0.999
ADAM_EPS = 1e-08
ADAM_WD = 0.01
ADAM_STEP = 10

VMEM_LIMIT = 56 * 1024 * 1024

NN = (((1,), (0,)), ((), ()))
NT = (((1,), (1,)), ((), ()))
TN = (((0,), (0,)), ((), ()))


def _mm(a, b, dims=NN):
    return lax.dot_general(a.astype(BF16), b.astype(BF16), dims, preferred_element_type=F32)


def _mm32(a, b, dims=NN):
    return lax.dot_general(a, b, dims, preferred_element_type=F32, precision=lax.Precision.HIGHEST)


def _params(sem, vmem=VMEM_LIMIT):
    return pltpu.CompilerParams(dimension_semantics=sem, vmem_limit_bytes=vmem)


def _sigmoid(x):
    return 0.5 * jnp.tanh(0.5 * x) + 0.5


def _softplus(x):
    return jnp.maximum(x, 0.0) + jnp.log1p(jnp.exp(-jnp.abs(x)))


def _row_acc(acc_ref, val):
    acc_ref[0:1, :] += jnp.sum(val, axis=0, keepdims=True)


def _rms_bwd(dh, xh, r, gain):
    dxh = dh * gain
    return r * (dxh - xh * jnp.mean(dxh * xh, axis=-1, keepdims=True))


def _before_halo(tb):
    return lambda i: (jnp.maximum(i * (tb // 8) - 1, 0), 0)


def _after_halo(tb, n_rows):
    last = n_rows // 8 - 1
    return lambda i: (jnp.minimum((i + 1) * (tb // 8), last), 0)


def _rows_from(xc, offset, tb):
    part = offset % 8
    if part:
        xc = pltpu.roll(xc, xc.shape[0] - part, 0)
    return xc[offset - part:offset - part + tb, :]


def _taps(xc, w, n_taps, tb, first):
    out = w[0:1, :] * _rows_from(xc, first, tb)
    for j in range(1, n_taps):
        out = out + w[j:j + 1, :] * _rows_from(xc, first + j, tb)
    return out


W_Z = QKV
W_BD = W_Z + DN_WIDTH
W_SC = W_BD + 2 * HEADS

def _w_in_cols(shards, lo, hi):
    pieces = []
    for s in range(N_CHIPS):
        a, b = max(lo, IN_SHARD * s), min(hi, IN_SHARD * (s + 1))
        if a < b:
            pieces.append(shards[s][:, a - IN_SHARD * s:b - IN_SHARD * s])
    return pieces[0] if len(pieces) == 1 else jnp.concatenate(pieces, axis=1)


def _in_proj(x, g1, land_a, cw, al_row, dt_row):
    T = x.shape[0]
    tb = 256

    def body(x_ref, g_ref, w_ref, cw_ref, al_ref, dt_ref,
             qkv_ref, z_ref, sc_ref, bd_ref, h_ref, q_ref, k_ref, v_ref, bg_ref, tail_ref):
        @pl.when(pl.program_id(0) == 0)
        def _():
            tail_ref[...] = jnp.zeros_like(tail_ref)

        xv = x_ref[...]
        h = (xv * lax.rsqrt(jnp.mean(xv * xv, axis=-1, keepdims=True) + EPS) * g_ref[...]).astype(BF16)
        shards = [jnp.dot(h, w_ref[s], preferred_element_type=F32) for s in range(N_CHIPS)]
        qkv = _w_in_cols(shards, 0, W_Z)
        bd = jnp.concatenate([_w_in_cols(shards, W_BD, W_SC), jnp.zeros((tb, LANES - 2 * HEADS), F32)], axis=1)
        qkv_ref[...] = qkv
        z_ref[...] = _w_in_cols(shards, W_Z, W_BD)
        bd_ref[...] = bd
        sc_ref[...] = _w_in_cols(shards, W_SC, W_IN_COLS)
        h_ref[...] = h
        halo = tail_ref[...]
        tail_ref[...] = qkv[tb - 8:, :]
        _, _, _, a = _dn_act(qkv, halo, cw_ref[...], tb)
        for hd in range(HEADS):
            sl = slice(HEAD_DIM * hd, HEAD_DIM * (hd + 1))
            qs = a[:, sl]
            q_ref[:, sl] = qs * (lax.rsqrt(jnp.sum(qs * qs, axis=-1, keepdims=True) + EPS) * Q_SCALE)
            ks = a[:, DN_WIDTH + HEAD_DIM * hd:DN_WIDTH + HEAD_DIM * (hd + 1)]
            k_ref[:, sl] = ks * lax.rsqrt(jnp.sum(ks * ks, axis=-1, keepdims=True) + EPS)
        v_ref[...] = a[:, 2 * DN_WIDTH:]
        gates = _gates(bd, al_ref[...], dt_ref[...])
        lane = lax.broadcasted_iota(jnp.int32, gates.shape, 1)
        bg_ref[...] = jnp.where(lane < HEADS, gates, _mm32(_chunk_cumsum_matrix(tb), gates))

    tok = lambda w: pl.BlockSpec((tb, w), lambda i: (i, 0))
    full = lambda t: pl.BlockSpec(t.shape, lambda i: (0, 0))
    return pl.pallas_call(
        body, name="in_proj", grid=(T // tb,),
        in_specs=[tok(D_MODEL), full(g1), _shard_rows(land_a, 0, D_MODEL), full(cw), full(al_row), full(dt_row)],
        out_specs=[tok(QKV), tok(DN_WIDTH), tok(3 * SC_WIDTH), tok(LANES), tok(D_MODEL),
                   tok(DN_WIDTH), tok(DN_WIDTH), tok(DN_WIDTH), tok(LANES)],
        out_shape=[jax.ShapeDtypeStruct((T, QKV), F32), jax.ShapeDtypeStruct((T, DN_WIDTH), F32),
                   jax.ShapeDtypeStruct((T, 3 * SC_WIDTH), F32), jax.ShapeDtypeStruct((T, LANES), F32),
                   jax.ShapeDtypeStruct((T, D_MODEL), BF16)]
        + [jax.ShapeDtypeStruct((T, DN_WIDTH), F32)] * 3 + [jax.ShapeDtypeStruct((T, LANES), F32)],
        scratch_shapes=[pltpu.VMEM((8, QKV), F32)],
        compiler_params=_params(("arbitrary",)),
    )(x, g1, land_a, cw, al_row, dt_row)


def _dn_act(pre, halo, cw, tb):
    xc = jnp.concatenate([halo, pre], axis=0)
    c = _taps(xc, cw, 4, tb, 5)
    sg = _sigmoid(c)
    return xc, c, sg, c * sg


def _gates(bd, al_row, dt_row):
    lane = lax.broadcasted_iota(jnp.int32, bd.shape, 1)
    beta = _sigmoid(bd)
    g = -jnp.exp(al_row) * _softplus(bd + dt_row)
    return jnp.where(lane < HEADS, beta, jnp.where(lane < 2 * HEADS, g, 0.0))


def _chunk_masks():
    row = lax.broadcasted_iota(jnp.int32, (CHUNK, CHUNK), 0)
    col = lax.broadcasted_iota(jnp.int32, (CHUNK, CHUNK), 1)
    return row >= col, row > col


def _chunk_cumsum_matrix(n):
    row = lax.broadcasted_iota(jnp.int32, (n, n), 0)
    col = lax.broadcasted_iota(jnp.int32, (n, n), 1)
    return jnp.logical_and(row >= col, row // CHUNK == col // CHUNK).astype(F32)


def _chunk_units(q_ref, k_ref, v_ref, bg_ref, rows):
    bgc = bg_ref[rows, :]
    bg_t = bgc.T
    qv, kv, vv = q_ref[rows, :], k_ref[rows, :], v_ref[rows, :]
    units = []
    for h in range(HEADS):
        sl = slice(HEAD_DIM * h, HEAD_DIM * (h + 1))
        units.append((qv[:, sl], kv[:, sl], vv[:, sl], bgc[:, h:h + 1], bgc[:, HEADS + h:HEADS + h + 1],
                      bg_t[HEADS + h:HEADS + h + 1, :]))
    return units


def _units_local(units, masks, xms=None):
    causal, strict = masks
    pre = []
    for q, k, v, beta, gc, gr in units:
        kb = k * beta
        eg = jnp.exp(gc)
        g_last = gc[CHUNK - 1:CHUNK, :]
        ek = jnp.exp(g_last - gc)
        pre.append(dict(q=q, k=k, v=v, beta=beta, decay=jnp.exp(jnp.where(causal, gc - gr, -1e30)), kb=kb, vb=v * beta,
                        eg=eg, kbg=kb * eg, ek=ek, gl=jnp.exp(g_last), q_dec=q * eg, k_dec=k * ek))
    both = [_mm(jnp.concatenate([p["kb"], p["q"]], axis=0), p["k"], NT) for p in pre]
    for p, b in zip(pre, both):
        p["low"] = jnp.where(strict, b[:CHUNK] * p["decay"], 0.0)
        p["qk"] = jnp.where(causal, b[CHUNK:] * p["decay"], 0.0)
    xs = xms
    if xs is None:
        xs = [-p["low"] for p in pre]
        pw = [_mm(p["low"], p["low"]) for p in pre]
        for _ in range(4):
            both = [_mm(jnp.concatenate([pp, x], axis=0), pp) for pp, x in zip(pw, xs)]
            xs = [x + pp + b[CHUNK:] for x, pp, b in zip(xs, pw, both)]
            pw = [b[:CHUNK] for b in both]
        last = [_mm(x, pp) for x, pp in zip(xs, pw)]
        xs = [x + pp + b for x, pp, b in zip(xs, pw, last)]
    uw = [_mm(x, jnp.concatenate([p["vb"], p["kbg"]], axis=1)) for x, p in zip(xs, pre)]
    for p, x, b in zip(pre, xs, uw):
        p["xm"] = x
        p["u"] = p["vb"] + b[:, :HEAD_DIM]
        p["w"] = p["kbg"] + b[:, HEAD_DIM:]
    return pre


FWD_GROUP = 8
BWD_GROUP = 8


def _delta_fwd(q, k, v, bg):
    T = q.shape[0]
    tb = 512
    n_chunk = tb // CHUNK

    def body(q_ref, k_ref, v_ref, bg_ref, o_ref, st_ref, xm_ref, s_ref):
        @pl.when(pl.program_id(0) == 0)
        def _():
            s_ref[...] = jnp.zeros_like(s_ref)

        masks = _chunk_masks()

        def group(gi, carry):
            rows = [pl.ds(pl.multiple_of((FWD_GROUP * gi + j) * CHUNK, CHUNK), CHUNK) for j in range(FWD_GROUP)]
            loc = _units_local(sum((_chunk_units(q_ref, k_ref, v_ref, bg_ref, r) for r in rows), []), masks)
            states = [s_ref[h] for h in range(HEADS)]
            for j in range(FWD_GROUP):
                lj = loc[HEADS * j:HEADS * (j + 1)]
                ws = [_mm(jnp.concatenate([p["w"], p["q_dec"]], axis=0), s) for p, s in zip(lj, states)]
                v_new = [p["u"] - b[:CHUNK] for p, b in zip(lj, ws)]
                intra = [_mm(p["qk"], vn) for p, vn in zip(lj, v_new)]
                upd = [_mm(p["k_dec"], vn, TN) for p, vn in zip(lj, v_new)]
                o_ref[rows[j], :] = jnp.concatenate([b[CHUNK:] + a for b, a in zip(ws, intra)], axis=1)
                for h in range(HEADS):
                    st_ref[FWD_GROUP * gi + j, h] = states[h]
                    xm_ref[FWD_GROUP * gi + j, h] = lj[h]["xm"]
                states = [p["gl"] * s + d for p, s, d in zip(lj, states, upd)]
            for h in range(HEADS):
                s_ref[h] = states[h]
            return carry

        lax.fori_loop(0, n_chunk // FWD_GROUP, group, 0)

    tok = lambda w: pl.BlockSpec((tb, w), lambda i: (i, 0))
    return pl.pallas_call(
        body, name="delta_fwd", grid=(T // tb,),
        in_specs=[tok(DN_WIDTH), tok(DN_WIDTH), tok(DN_WIDTH), tok(LANES)],
        out_specs=[tok(DN_WIDTH), pl.BlockSpec((n_chunk, HEADS, HEAD_DIM, HEAD_DIM), lambda i: (i, 0, 0, 0)),
                   pl.BlockSpec((n_chunk, HEADS, CHUNK, CHUNK), lambda i: (i, 0, 0, 0))],
        out_shape=[jax.ShapeDtypeStruct((T, DN_WIDTH), F32),
                   jax.ShapeDtypeStruct((T // CHUNK, HEADS, HEAD_DIM, HEAD_DIM), F32),
                   jax.ShapeDtypeStruct((T // CHUNK, HEADS, CHUNK, CHUNK), F32)],
        scratch_shapes=[pltpu.VMEM((HEADS, HEAD_DIM, HEAD_DIM), F32)],
        compiler_params=_params(("arbitrary",)),
    )(q, k, v, bg)


def _dn_out(o, z, gn):
    outs, ohs, rs = [], [], []
    for hh in range(HEADS):
        oh = o[:, HEAD_DIM * hh:HEAD_DIM * (hh + 1)]
        r = lax.rsqrt(jnp.mean(oh * oh, axis=-1, keepdims=True) + EPS)
        ohs.append(oh * r)
        rs.append(r)
    sz = _sigmoid(z)
    oh = jnp.concatenate(ohs, axis=1)
    gn4 = jnp.concatenate([gn] * HEADS, axis=1)
    return oh * gn4 * (z * sz), oh, rs, sz, gn4


def _sc_fwd(sc_in, halo, cw, tb):
    xc = jnp.concatenate([halo, sc_in], axis=0)
    u = xc[:, SC_WIDTH:2 * SC_WIDTH] * xc[:, 2 * SC_WIDTH:]
    cv = _taps(u, cw, 3, tb, 6)
    gate_b = sc_in[:, :SC_WIDTH]
    y = gate_b * cv
    gw = SC_WIDTH // SC_GROUPS
    yhs, rs = [], []
    for gi in range(SC_GROUPS):
        yg = y[:, gw * gi:gw * (gi + 1)]
        r = lax.rsqrt(jnp.mean(yg * yg, axis=-1, keepdims=True) + EPS)
        yhs.append(yg * r)
        rs.append(r)
    return u, cv, gate_b, jnp.concatenate(yhs, axis=1), rs


def _shard_rows(land, first, rows, single_buffer=False):
    assert first % rows == 0 and land.shape[0] == N_CHIPS
    mode = dict(pipeline_mode=pl.Buffered(1)) if single_buffer else {}
    return pl.BlockSpec((N_CHIPS, rows, land.shape[2]), lambda i: (0, first // rows, 0), **mode)


def _whole(w_ref):
    n, rows, cols = w_ref.shape
    return w_ref[...].reshape(n * rows, cols)


def _mix_ffn(o, z, sc_in, x, land_a, gn, scw, gs, g2, land_b):
    T = x.shape[0]
    tb = 256

    def body(o_ref, z_ref, sc_ref, halo_ref, x_ref, wo_ref, gn_ref, scw_ref, gs_ref, g2_ref, wgt_ref, wut_ref, wd_ref,
             x1_ref, mix_ref, x2_ref, a_ref, b_ref, h_ref):
        o_n = _dn_out(o_ref[...], z_ref[...], gn_ref[...])[0]
        halo = jnp.where(pl.program_id(0) > 0, halo_ref[...], 0.0)
        yh = _sc_fwd(sc_ref[...], halo, scw_ref[...], tb)[3]
        mix = jnp.concatenate([o_n, yh * gs_ref[...]], axis=1).astype(BF16)
        x1 = x_ref[...] + jnp.dot(mix, _whole(wo_ref), preferred_element_type=F32)
        x1_ref[...] = x1
        mix_ref[...] = mix
        r = lax.rsqrt(jnp.mean(x1 * x1, axis=-1, keepdims=True) + EPS)
        h = (x1 * r * g2_ref[...]).astype(BF16)
        a = lax.dot_general(h, _whole(wgt_ref), NT, preferred_element_type=F32)
        b = lax.dot_general(h, _whole(wut_ref), NT, preferred_element_type=F32)
        act = (a * _sigmoid(a) * b).astype(BF16)
        x2_ref[...] = x1 + jnp.dot(act, _whole(wd_ref), preferred_element_type=F32)
        a_ref[...] = a.astype(BF16)
        b_ref[...] = b.astype(BF16)
        h_ref[...] = h

    tok = lambda w: pl.BlockSpec((tb, w), lambda i: (i, 0))
    full = lambda t: pl.BlockSpec(t.shape, lambda i: (0, 0))
    once = lambda land, first, rows: _shard_rows(land, first, rows, single_buffer=True)
    return pl.pallas_call(
        body, name="mix_ffn", grid=(T // tb,),
        in_specs=[tok(DN_WIDTH), tok(DN_WIDTH), tok(3 * SC_WIDTH), pl.BlockSpec((8, 3 * SC_WIDTH), _before_halo(tb)),
                  tok(D_MODEL), once(land_a, A_OUT_AT, OUT_SHARD), full(gn), full(scw), full(gs), full(g2),
                  once(land_b, 0, FF_SHARD), once(land_b, FF_SHARD, FF_SHARD), once(land_b, 2 * FF_SHARD, FF_SHARD)],
        out_specs=[tok(D_MODEL), tok(D_MODEL), tok(D_MODEL), tok(D_FF), tok(D_FF), tok(D_MODEL)],
        out_shape=[jax.ShapeDtypeStruct((T, D_MODEL), F32), jax.ShapeDtypeStruct((T, D_MODEL), BF16),
                   jax.ShapeDtypeStruct((T, D_MODEL), F32), jax.ShapeDtypeStruct((T, D_FF), BF16),
                   jax.ShapeDtypeStruct((T, D_FF), BF16), jax.ShapeDtypeStruct((T, D_MODEL), BF16)],
        compiler_params=_params(("parallel",)),
    )(o, z, sc_in, sc_in, x, land_a, gn, scw, gs, g2, land_b, land_b, land_b)


def _loss_head(x, gf, target):
    T = x.shape[0]
    tb = 512

    def body(x_ref, g_ref, t_ref, dx_ref, dxb_ref, loss_ref, dg_ref):
        @pl.when(pl.program_id(0) == 0)
        def _():
            loss_ref[...] = jnp.zeros_like(loss_ref)
            dg_ref[...] = jnp.zeros_like(dg_ref)

        xv = x_ref[...]
        r = lax.rsqrt(jnp.mean(xv * xv, axis=-1, keepdims=True) + EPS)
        xh = xv * r
        err = xh * g_ref[...] - t_ref[...]
        per_tok = jnp.mean(err * err, axis=-1, keepdims=True)
        loss_ref[...] += 0.5 * jnp.sum(per_tok, axis=0, keepdims=True)
        dy = err * (1.0 / D_MODEL)
        _row_acc(dg_ref, dy * xh)
        dx = _rms_bwd(dy, xh, r, g_ref[...])
        dx_ref[...] = dx
        dxb_ref[...] = dx.astype(BF16)

    tok = pl.BlockSpec((tb, D_MODEL), lambda i: (i, 0))
    return pl.pallas_call(
        body, name="loss_head", grid=(T // tb,),
        in_specs=[tok, pl.BlockSpec(gf.shape, lambda i: (0, 0)), tok],
        out_specs=[tok, tok, pl.BlockSpec((8, LANES), lambda i: (0, 0)), pl.BlockSpec((8, D_MODEL), lambda i: (0, 0))],
        out_shape=[jax.ShapeDtypeStruct((T, D_MODEL), F32), jax.ShapeDtypeStruct((T, D_MODEL), BF16),
                   jax.ShapeDtypeStruct((8, LANES), F32), jax.ShapeDtypeStruct((8, D_MODEL), F32)],
        compiler_params=_params(("arbitrary",)),
    )(x, gf, target)


def _ffn_bwd(dx2, x1, a, b, g2, land_b):
    T = x1.shape[0]
    tb = 256

    def body(dx2_ref, x_ref, a_ref, b_ref, g_ref, wgt_ref, wut_ref, wd_ref,
             dx1_ref, da_ref, db_ref, act_ref, dg_ref):
        @pl.when(pl.program_id(0) == 0)
        def _():
            dg_ref[...] = jnp.zeros_like(dg_ref)

        dx2v = dx2_ref[...]
        av = a_ref[...].astype(F32)
        bv = b_ref[...].astype(F32)
        dact = _mm(dx2v, _whole(wd_ref), NT)
        sa = _sigmoid(av)
        silu = av * sa
        da = (dact * bv * (sa * (1.0 + av * (1.0 - sa)))).astype(BF16)
        db = (dact * silu).astype(BF16)
        dh = _mm(da, _whole(wgt_ref)) + _mm(db, _whole(wut_ref))
        xv = x_ref[...]
        r = lax.rsqrt(jnp.mean(xv * xv, axis=-1, keepdims=True) + EPS)
        xh = xv * r
        _row_acc(dg_ref, dh * xh)
        dx1 = dx2v + _rms_bwd(dh, xh, r, g_ref[...])
        dx1_ref[...] = dx1
        da_ref[...] = da
        db_ref[...] = db
        act_ref[...] = (silu * bv).astype(BF16)

    tok = lambda w: pl.BlockSpec((tb, w), lambda i: (i, 0))
    return pl.pallas_call(
        body, name="ffn_bwd", grid=(T // tb,),
        in_specs=[tok(D_MODEL), tok(D_MODEL), tok(D_FF), tok(D_FF), pl.BlockSpec(g2.shape, lambda i: (0, 0)),
                  _shard_rows(land_b, 0, FF_SHARD), _shard_rows(land_b, FF_SHARD, FF_SHARD),
                  _shard_rows(land_b, 2 * FF_SHARD, FF_SHARD)],
        out_specs=[tok(D_MODEL), tok(D_FF), tok(D_FF), tok(D_FF), pl.BlockSpec((8, D_MODEL), lambda i: (0, 0))],
        out_shape=[jax.ShapeDtypeStruct((T, D_MODEL), F32)]
        + [jax.ShapeDtypeStruct((T, D_FF), BF16)] * 3 + [jax.ShapeDtypeStruct((8, D_MODEL), F32)],
        compiler_params=_params(("arbitrary",)),
    )(dx2, x1, a, b, g2, land_b, land_b, land_b)


WGRAD_TOKENS = 2048


def _wgrad_share(a, b, parts, first, name):
    T = b.shape[0]
    rows = a.shape[1] // N_CHIPS
    assert first % rows == 0 and b.shape[1] == parts.shape[2]
    bk = min(T, WGRAD_TOKENS)
    n_k = T // bk
    group = 2
    assert (group * rows) % LANES == 0

    def body(a_ref, b_ref, parts_ref, o_ref, acc_ref):
        kk = pl.program_id(1)

        @pl.when(kk == 0)
        def _():
            acc_ref[...] = jnp.zeros_like(acc_ref)

        acc_ref[...] += lax.dot_general(a_ref[...], b_ref[...], TN, preferred_element_type=F32)

        @pl.when(kk == n_k - 1)
        def _():
            for s in range(group):
                o_ref[s] = acc_ref[rows * s:rows * (s + 1), :].astype(BF16)

    return pl.pallas_call(
        body, name=name, grid=(N_CHIPS // group, n_k),
        in_specs=[pl.BlockSpec((bk, group * rows), lambda i, kk: (kk, i)),
                  pl.BlockSpec((bk, b.shape[1]), lambda i, kk: (kk, 0)), _ANY],
        out_specs=pl.BlockSpec((group, rows, b.shape[1]), lambda i, kk: (i, first // rows, 0)),
        out_shape=jax.ShapeDtypeStruct(parts.shape, BF16),
        scratch_shapes=[pltpu.VMEM((group * rows, b.shape[1]), F32)],
        input_output_aliases={2: 0},
        compiler_params=_params(("parallel", "arbitrary")),
    )(a, b, parts)


def _mix_out_bwd(dx1, o, z, sc_in, land_a, gn, scw, gs):
    T = dx1.shape[0]
    tb = 256

    def body(dx_ref, o_ref, z_ref, sc_ref, halo_ref, w_ref, gn_ref, scw_ref, gs_ref,
             do_ref, dz_ref, dgb_ref, dcv_ref, dxb_ref, dgn_ref, dgs_ref, dscw_ref):
        @pl.when(pl.program_id(0) == 0)
        def _():
            dgn_ref[...] = jnp.zeros_like(dgn_ref)
            dgs_ref[...] = jnp.zeros_like(dgs_ref)
            dscw_ref[...] = jnp.zeros_like(dscw_ref)

        dx_bf16 = dx_ref[...].astype(BF16)
        dxb_ref[...] = dx_bf16
        dmix = lax.dot_general(dx_bf16, _whole(w_ref), NT, preferred_element_type=F32)
        don = dmix[:, :DN_WIDTH]
        dosc = dmix[:, DN_WIDTH:]
        zv = z_ref[...]
        _, oh, rs, sz, gn4 = _dn_out(o_ref[...], zv, gn_ref[...])
        silu_z = zv * sz
        dgn_full = don * oh * silu_z
        dgn_ref[0:1, :] += jnp.sum(sum(dgn_full[:, HEAD_DIM * hh:HEAD_DIM * (hh + 1)] for hh in range(HEADS)),
                                   axis=0, keepdims=True)
        dz_ref[...] = (don * oh * gn4 * (sz * (1.0 + zv * (1.0 - sz)))).astype(BF16)
        t = don * gn4 * silu_z
        for hh in range(HEADS):
            sl = slice(HEAD_DIM * hh, HEAD_DIM * (hh + 1))
            th, ohh = t[:, sl], oh[:, sl]
            do_ref[:, sl] = rs[hh] * (th - ohh * jnp.mean(th * ohh, axis=-1, keepdims=True))
        halo = jnp.where(pl.program_id(0) > 0, halo_ref[...], 0.0)
        u, cv, gate_b, yh, rys = _sc_fwd(sc_ref[...], halo, scw_ref[...], tb)
        _row_acc(dgs_ref, dosc * yh)
        ty = dosc * gs_ref[...]
        gw = SC_WIDTH // SC_GROUPS
        dys = []
        for gi in range(SC_GROUPS):
            sl = slice(gw * gi, gw * (gi + 1))
            tg, yg = ty[:, sl], yh[:, sl]
            dys.append(rys[gi] * (tg - yg * jnp.mean(tg * yg, axis=-1, keepdims=True)))
        dy = jnp.concatenate(dys, axis=1)
        dgb_ref[...] = dy * cv
        dcv = dy * gate_b
        dcv_ref[...] = dcv
        for j in range(3):
            dscw_ref[j:j + 1, :] += jnp.sum(dcv * _rows_from(u, 6 + j, tb), axis=0, keepdims=True)

    tok = lambda w: pl.BlockSpec((tb, w), lambda i: (i, 0))
    full = lambda t: pl.BlockSpec(t.shape, lambda i: (0, 0))
    acc = lambda w: pl.BlockSpec((8, w), lambda i: (0, 0))
    return pl.pallas_call(
        body, name="mix_out_bwd", grid=(T // tb,),
        in_specs=[tok(D_MODEL), tok(DN_WIDTH), tok(DN_WIDTH), tok(3 * SC_WIDTH),
                  pl.BlockSpec((8, 3 * SC_WIDTH), _before_halo(tb)), _shard_rows(land_a, A_OUT_AT, OUT_SHARD),
                  full(gn), full(scw), full(gs)],
        out_specs=[tok(DN_WIDTH), tok(DN_WIDTH), tok(SC_WIDTH), tok(SC_WIDTH), tok(D_MODEL),
                   acc(HEAD_DIM), acc(SC_WIDTH), acc(SC_WIDTH)],
        out_shape=[jax.ShapeDtypeStruct((T, DN_WIDTH), F32), jax.ShapeDtypeStruct((T, DN_WIDTH), BF16),
                   jax.ShapeDtypeStruct((T, SC_WIDTH), F32), jax.ShapeDtypeStruct((T, SC_WIDTH), F32),
                   jax.ShapeDtypeStruct((T, D_MODEL), BF16),
                   jax.ShapeDtypeStruct((8, HEAD_DIM), F32), jax.ShapeDtypeStruct((8, SC_WIDTH), F32),
                   jax.ShapeDtypeStruct((8, SC_WIDTH), F32)],
        compiler_params=_params(("arbitrary",)),
    )(dx1, o, z, sc_in, sc_in, land_a, gn, scw, gs)


def _delta_bwd(q, k, v, bg, states, xms, do):
    T = q.shape[0]
    tb = 512
    n_chunk = tb // CHUNK
    nb = T // tb

    def body(q_ref, k_ref, v_ref, bg_ref, st_ref, xm_ref, do_ref, dq_ref, dk_ref, dv_ref, dbg_ref, ds_ref):
        @pl.when(pl.program_id(0) == 0)
        def _():
            ds_ref[...] = jnp.zeros_like(ds_ref)

        masks = _chunk_masks()
        causal, strict = masks
        lane = lax.broadcasted_iota(jnp.int32, (CHUNK, LANES), 1)
        last_row = lax.broadcasted_iota(jnp.int32, (CHUNK, 1), 0) == CHUNK - 1
        cat = jnp.concatenate
        heads = range(HEADS)

        def open_chunk(ci, loc):
            rows = pl.ds(pl.multiple_of(ci * CHUNK, CHUNK), CHUNK)
            dov = do_ref[rows, :]
            return dict(rows=rows, loc=loc, do=[dov[:, HEAD_DIM * h:HEAD_DIM * (h + 1)] for h in heads],
                        state=[st_ref[ci, h] for h in heads])

        def a_free(c):
            loc, do, state = c["loc"], c["do"], c["state"]
            w_s = [_mm(p["w"], s) for p, s in zip(loc, state)]
            c["dq_dec"] = [_mm(d, s, NT) for d, s in zip(do, state)]
            c["qk_do"] = [_mm(p["qk"], d, TN) for p, d in zip(loc, do)]
            c["qd_do"] = [_mm(p["q_dec"], d, TN) for p, d in zip(loc, do)]
            c["v_new"] = [p["u"] - t for p, t in zip(loc, w_s)]
            c["dqk"] = [jnp.where(causal, _mm(d, vn, NT), 0.0) for d, vn in zip(do, c["v_new"])]

        def a_state(c, ds_next):
            c["ds_next"] = ds_next
            kd_ds = [_mm(p["k_dec"], d) for p, d in zip(c["loc"], ds_next)]
            c["dk_dec"] = [_mm(vn, d, NT) for vn, d in zip(c["v_new"], ds_next)]
            c["dv_new"] = [a + b for a, b in zip(c["qk_do"], kd_ds)]

        def b_state(c):
            loc = c["loc"]
            w_dv = [_mm(p["w"], dvn, TN) for p, dvn in zip(loc, c["dv_new"])]
            c["dw"] = [-_mm(dvn, s, NT) for dvn, s in zip(c["dv_new"], c["state"])]
            return [loc[h]["gl"] * c["ds_next"][h] + c["qd_do"][h] - w_dv[h] for h in heads]

        def c_solve(c):
            loc, dv_new, dw = c["loc"], c["dv_new"], c["dw"]
            c["dtm"] = [_mm(cat([dvn, d], axis=1), cat([p["vb"], p["kbg"]], axis=1), NT) for dvn, d, p in zip(dv_new, dw, loc)]
            x_t = [_mm(p["xm"], cat([dvn, d], axis=1), TN) for p, dvn, d in zip(loc, dv_new, dw)]
            c["dvb"] = [dvn + t[:, :HEAD_DIM] for dvn, t in zip(dv_new, x_t)]
            c["dkbg"] = [d + t[:, HEAD_DIM:] for d, t in zip(dw, x_t)]

        def d_solve(c):
            c["y"] = [t + _mm(p["xm"], t, TN) for p, t in zip(c["loc"], c["dtm"])]

        def e_solve(c):
            c["dlow"] = [jnp.where(strict, -(t + _mm(t, p["xm"], NT)), 0.0) for p, t in zip(c["loc"], c["y"])]

        def f_close(c):
            loc, rows = c["loc"], c["rows"]
            dmm = [d * p["decay"] for d, p in zip(c["dlow"], loc)]
            dnn = [d * p["decay"] for d, p in zip(c["dqk"], loc)]
            by_k = [_mm(cat([a, b], axis=0), p["k"]) for a, b, p in zip(dmm, dnn, loc)]
            dk_mm = [_mm(cat([a, b], axis=0), cat([p["kb"], p["q"]], axis=0), TN) for a, b, p in zip(dmm, dnn, loc)]
            dq_out, dk_out, dv_out = [], [], []
            dbeta_all = jnp.zeros((CHUNK, LANES), F32)
            dgc_all = jnp.zeros((CHUNK, LANES), F32)
            for h in heads:
                p = loc[h]
                dkb = by_k[h][:CHUNK] + c["dkbg"][h] * p["eg"]
                dq_out.append(by_k[h][CHUNK:] + c["dq_dec"][h] * p["eg"])
                dk_out.append(dk_mm[h] + c["dk_dec"][h] * p["ek"] + dkb * p["beta"])
                dv_out.append(c["dvb"][h] * p["beta"])
                dbeta = jnp.sum(dkb * p["k"] + c["dvb"][h] * p["v"], axis=1, keepdims=True)
                e = c["dlow"][h] * p["low"] + c["dqk"][h] * p["qk"]
                kd = jnp.sum(c["dk_dec"][h] * p["k_dec"], axis=1, keepdims=True)
                dgc = (jnp.sum(e, axis=1, keepdims=True) - jnp.sum(e.T, axis=1, keepdims=True)
                       + jnp.sum(c["dq_dec"][h] * p["q_dec"], axis=1, keepdims=True) - kd
                       + jnp.sum(c["dkbg"][h] * p["kbg"], axis=1, keepdims=True))
                dgl = jnp.sum(jnp.sum(c["ds_next"][h] * c["state"][h], axis=1, keepdims=True), axis=0, keepdims=True)
                d_last = jnp.sum(kd, axis=0, keepdims=True) + dgl * p["gl"]
                dgc = dgc + jnp.where(last_row, d_last, 0.0)
                dbeta_all = jnp.where(lane == h, dbeta, dbeta_all)
                dgc_all = jnp.where(lane == h + HEADS, dgc, dgc_all)
            dq_ref[rows, :] = cat(dq_out, axis=1)
            dk_ref[rows, :] = cat(dk_out, axis=1)
            dv_ref[rows, :] = cat(dv_out, axis=1)
            dbg_ref[rows, :] = dbeta_all + dgc_all

        def group(gj, carry):
            first = n_chunk - 1 - BWD_GROUP * gj
            ids = [first - j for j in range(BWD_GROUP)]
            rows = [pl.ds(pl.multiple_of(ci * CHUNK, CHUNK), CHUNK) for ci in ids]
            loc = _units_local(sum((_chunk_units(q_ref, k_ref, v_ref, bg_ref, r) for r in rows), []), masks,
                               xms=[xm_ref[ci, h] for ci in ids for h in heads])
            chunks = [open_chunk(ci, loc[HEADS * j:HEADS * (j + 1)]) for j, ci in enumerate(ids)]
            for c in chunks:
                a_free(c)
            ds_cur = [ds_ref[h] for h in heads]
            later = (c_solve, d_solve, e_solve, f_close)
            for t in range(2 * (BWD_GROUP - 1) + 2 + len(later)):
                for j, c in enumerate(chunks):
                    stage = t - 2 * j
                    if stage == 0:
                        a_state(c, ds_cur)
                    elif stage == 1:
                        ds_cur = b_state(c)
                    elif 2 <= stage < 2 + len(later):
                        later[stage - 2](c)
            for h in heads:
                ds_ref[h] = ds_cur[h]
            return carry

        lax.fori_loop(0, n_chunk // BWD_GROUP, group, 0)

    tok = lambda w: pl.BlockSpec((tb, w), lambda i: (nb - 1 - i, 0))
    return pl.pallas_call(
        body, name="delta_bwd", grid=(nb,),
        in_specs=[tok(DN_WIDTH), tok(DN_WIDTH), tok(DN_WIDTH), tok(LANES),
                  pl.BlockSpec((n_chunk, HEADS, HEAD_DIM, HEAD_DIM), lambda i: (nb - 1 - i, 0, 0, 0)),
                  pl.BlockSpec((n_chunk, HEADS, CHUNK, CHUNK), lambda i: (nb - 1 - i, 0, 0, 0)), tok(DN_WIDTH)],
        out_specs=[tok(DN_WIDTH), tok(DN_WIDTH), tok(DN_WIDTH), tok(LANES)],
        out_shape=[jax.ShapeDtypeStruct((T, DN_WIDTH), F32)] * 3 + [jax.ShapeDtypeStruct((T, LANES), F32)],
        scratch_shapes=[pltpu.VMEM((HEADS, HEAD_DIM, HEAD_DIM), F32)],
        compiler_params=_params(("arbitrary",)),
    )(q, k, v, bg, states, xms, do)


def _dn_prep_back(dq, dk, dv, dbg, pre, halo, cw, bd, al_row, dt_row, tb):
    xc, c, sg, a = _dn_act(pre, halo, cw, tb)
    dsilu = sg * (1.0 + c * (1.0 - sg))
    pieces = [None] * (2 * HEADS)
    for hd in range(HEADS):
        sl = slice(HEAD_DIM * hd, HEAD_DIM * (hd + 1))
        for which, (base, grad, scale) in enumerate(((0, dq, Q_SCALE), (DN_WIDTH, dk, 1.0))):
            sa = slice(base + HEAD_DIM * hd, base + HEAD_DIM * (hd + 1))
            raw = a[:, sa]
            r = lax.rsqrt(jnp.sum(raw * raw, axis=-1, keepdims=True) + EPS)
            nrm = raw * r
            gn_ = grad[:, sl] * scale
            pieces[which * HEADS + hd] = r * (gn_ - nrm * jnp.sum(gn_ * nrm, axis=-1, keepdims=True)) * dsilu[:, sa]
    dc = jnp.concatenate(pieces + [dv * dsilu[:, 2 * DN_WIDTH:]], axis=1)
    dcw_rows = [jnp.sum(dc * _rows_from(xc, 5 + j, tb), axis=0, keepdims=True) for j in range(4)]
    lane = lax.broadcasted_iota(jnp.int32, bd.shape, 1)
    is_b = lane < HEADS
    is_g = jnp.logical_and(lane >= HEADS, lane < 2 * HEADS)
    dbgv = jnp.where(is_b, dbg, _mm32(_chunk_cumsum_matrix(tb), dbg, TN))
    beta = _sigmoid(bd)
    neg_a = -jnp.exp(al_row)
    pre_sp = bd + dt_row
    g = neg_a * _softplus(pre_sp)
    da_in = dbgv * neg_a * _sigmoid(pre_sp)
    dbd = jnp.where(is_b, dbgv * beta * (1.0 - beta), jnp.where(is_g, da_in, 0.0)).astype(BF16)
    dal_row = jnp.sum(jnp.where(is_g, dbgv * g, 0.0), axis=0, keepdims=True)
    ddt_row = jnp.sum(jnp.where(is_g, da_in, 0.0), axis=0, keepdims=True)
    return dc, dbd, dcw_rows, dal_row, ddt_row


def _dp_of_chip(dqkv, dz, dbd, dsc, s):
    lo, hi = IN_SHARD * s, IN_SHARD * (s + 1)
    pieces = []
    for w_at, w_end, block in ((0, W_Z, dqkv), (W_Z, W_BD, dz), (W_BD, W_SC, dbd), (W_SC, W_IN_COLS, dsc)):
        a, b = max(lo, w_at), min(hi, w_end)
        if a < b:
            pieces.append(block[:, a - w_at:b - w_at])
    pieces.append(jnp.zeros((dqkv.shape[0], D_MODEL - IN_SHARD), dqkv.dtype))
    return jnp.concatenate(pieces, axis=1)


def _in_proj_bwd(dq, dk, dv, dbg, qkv, bd, al_row, dt_row, dcv, dgb, sc_in, dz, cw, scw, dx1, x, g1, land_a):
    T = x.shape[0]
    tb = 256
    nb = T // tb

    def body(dq_ref, dk_ref, dv_ref, dbg_ref, pre_ref, pre_halo_ref, bd_ref, al_ref, dt_ref,
             dcv_ref, dcv_halo_ref, dgb_ref, sc_ref, dz_ref, cw_ref, scw_ref, dx1_ref, x_ref, g_ref, w_ref,
             dx_ref, dxb_ref, dps_ref, dg_ref, dcw_ref, dal_ref, ddt_ref, head_ref):
        @pl.when(pl.program_id(0) == 0)
        def _():
            for ref in (dg_ref, dcw_ref, dal_ref, ddt_ref, head_ref):
                ref[...] = jnp.zeros_like(ref)

        block = nb - 1 - pl.program_id(0)
        last = block == nb - 1
        pre_halo = jnp.where(block > 0, pre_halo_ref[...], 0.0)
        dc, dbd, dcw_rows, dal_row, ddt_row = _dn_prep_back(
            dq_ref[...], dk_ref[...], dv_ref[...], dbg_ref[...], pre_ref[...], pre_halo, cw_ref[...], bd_ref[...],
            al_ref[...], dt_ref[...], tb)
        for j in range(4):
            dcw_ref[j:j + 1, :] += dcw_rows[j]
        dal_ref[0:1, :] += dal_row
        ddt_ref[0:1, :] += ddt_row
        xc = jnp.concatenate([dc, head_ref[...]], axis=0)
        head_ref[...] = dc[0:8, :]
        w4 = cw_ref[...]
        dqkv = w4[3:4, :] * xc[0:tb, :]
        for j in range(3):
            dqkv = dqkv + w4[j:j + 1, :] * _rows_from(xc, 3 - j, tb)
        yc = jnp.concatenate([dcv_ref[...], jnp.where(last, 0.0, dcv_halo_ref[...])], axis=0)
        w3 = scw_ref[...]
        du = w3[2:3, :] * yc[0:tb, :] + w3[1:2, :] * _rows_from(yc, 1, tb) + w3[0:1, :] * _rows_from(yc, 2, tb)
        sc = sc_ref[...]
        dsc = jnp.concatenate([dgb_ref[...], du * sc[:, 2 * SC_WIDTH:], du * sc[:, SC_WIDTH:2 * SC_WIDTH]], axis=1)
        blocks = (dqkv.astype(BF16), dz_ref[...], dbd, dsc.astype(BF16))
        dh = jnp.zeros((tb, D_MODEL), F32)
        for s in range(N_CHIPS):
            dps = _dp_of_chip(*blocks, s)
            dps_ref[:, D_MODEL * s:D_MODEL * (s + 1)] = dps
            dh = dh + lax.dot_general(dps, w_ref[s], NT, preferred_element_type=F32)
        xv = x_ref[...]
        r = lax.rsqrt(jnp.mean(xv * xv, axis=-1, keepdims=True) + EPS)
        xh = xv * r
        _row_acc(dg_ref, dh * xh)
        dx = dx1_ref[...] + _rms_bwd(dh, xh, r, g_ref[...])
        dx_ref[...] = dx
        dxb_ref[...] = dx.astype(BF16)

    tok = lambda w: pl.BlockSpec((tb, w), lambda i: (nb - 1 - i, 0))
    full = lambda t: pl.BlockSpec(t.shape, lambda i: (0, 0))
    acc = lambda w: pl.BlockSpec((8, w), lambda i: (0, 0))
    before = lambda w: pl.BlockSpec((8, w), lambda i: _before_halo(tb)(nb - 1 - i))
    after = lambda w: pl.BlockSpec((8, w), lambda i: _after_halo(tb, T)(nb - 1 - i))
    return pl.pallas_call(
        body, name="in_proj_bwd", grid=(nb,),
        in_specs=[tok(DN_WIDTH), tok(DN_WIDTH), tok(DN_WIDTH), tok(LANES), tok(QKV), before(QKV), tok(LANES),
                  full(al_row), full(dt_row), tok(SC_WIDTH), after(SC_WIDTH), tok(SC_WIDTH), tok(3 * SC_WIDTH),
                  tok(DN_WIDTH), full(cw), full(scw), tok(D_MODEL), tok(D_MODEL), full(g1), _shard_rows(land_a, 0, D_MODEL)],
        out_specs=[tok(D_MODEL), tok(D_MODEL), tok(N_CHIPS * D_MODEL), acc(D_MODEL), acc(QKV), acc(LANES), acc(LANES)],
        out_shape=[jax.ShapeDtypeStruct((T, D_MODEL), F32), jax.ShapeDtypeStruct((T, D_MODEL), BF16),
                   jax.ShapeDtypeStruct((T, N_CHIPS * D_MODEL), BF16), jax.ShapeDtypeStruct((8, D_MODEL), F32),
                   jax.ShapeDtypeStruct((8, QKV), F32), jax.ShapeDtypeStruct((8, LANES), F32),
                   jax.ShapeDtypeStruct((8, LANES), F32)],
        scratch_shapes=[pltpu.VMEM((8, QKV), F32)],
        compiler_params=_params(("arbitrary",)),
    )(dq, dk, dv, dbg, qkv, qkv, bd, al_row, dt_row, dcv, dcv, dgb, sc_in, dz, cw, scw, dx1, x, g1, land_a)


def _wgrad_in_share(h, dps, parts, name):
    T = h.shape[0]
    bk = min(T, WGRAD_TOKENS)
    n_k = T // bk

    def body(a_ref, b_ref, parts_ref, o_ref, acc_ref):
        kk = pl.program_id(1)

        @pl.when(kk == 0)
        def _():
            acc_ref[...] = jnp.zeros_like(acc_ref)

        acc_ref[...] += lax.dot_general(a_ref[...], b_ref[...], TN, preferred_element_type=F32)

        @pl.when(kk == n_k - 1)
        def _():
            o_ref[0] = acc_ref[...].astype(BF16)

    return pl.pallas_call(
        body, name=name, grid=(N_CHIPS, n_k),
        in_specs=[pl.BlockSpec((bk, D_MODEL), lambda j, kk: (kk, 0)), pl.BlockSpec((bk, D_MODEL), lambda j, kk: (kk, j)), _ANY],
        out_specs=pl.BlockSpec((1, D_MODEL, D_MODEL), lambda j, kk: (j, 0, 0)),
        out_shape=jax.ShapeDtypeStruct(parts.shape, BF16),
        scratch_shapes=[pltpu.VMEM((D_MODEL, D_MODEL), F32)],
        input_output_aliases={2: 0},
        compiler_params=_params(("parallel", "arbitrary")),
    )(h, dps, parts)


def _pad_rows(a, rows=8):
    return jnp.pad(a, ((0, rows - a.shape[0]), (0, 0)))


def _gate_rows(a_log, dt_bias):
    put = lambda t: jnp.pad(t.reshape(1, HEADS), ((0, 0), (HEADS, LANES - 2 * HEADS)))
    return put(a_log), put(dt_bias)


def _mixer_fwd(x, p):
    qkv, z, sc_in, bd, h, q, k, v, bg = _in_proj(x, p["g1"], p["land_a"], p["cw"], p["al"], p["dt"])
    o, states, xms = _delta_fwd(q, k, v, bg)
    return dict(x=x, qkv=qkv, z=z, sc_in=sc_in, bd=bd, h=h, q=q, k=k, v=v, bg=bg, o=o, states=states, xms=xms)


def _tail_fwd(s, p, land_b):
    x1, mix, x2, a, b, h2 = _mix_ffn(s["o"], s["z"], s["sc_in"], s["x"], p["land_a"], p["gn"], p["scw"], p["gs"],
                                     p["g2"], land_b)
    return x2, dict(s, mix=mix), dict(x1=x1, a=a, b=b, h2=h2)


def _ffn_back(dx2, dx2_bf16, s, p, land_b):
    dx1, da, db, act, dg2 = _ffn_bwd(dx2, s["x1"], s["a"], s["b"], p["g2"], land_b)
    parts = lax.empty((N_CHIPS, B_ROWS, D_MODEL), BF16)
    parts = _wgrad_share(act, dx2_bf16, parts, 2 * FF_SHARD, "wgrad_down")
    parts = _wgrad_share(da, s["h2"], parts, 0, "wgrad_gate")
    parts = _wgrad_share(db, s["h2"], parts, FF_SHARD, "wgrad_up")
    return dx1, parts, dg2[0]


def _mixer_bwd(dx1, s, p):
    do, dz, dgb, dcv, dx1_bf16, dgn, dgs, dscw = _mix_out_bwd(dx1, s["o"], s["z"], s["sc_in"], p["land_a"], p["gn"],
                                                              p["scw"], p["gs"])
    dq, dk, dv, dbg = _delta_bwd(s["q"], s["k"], s["v"], s["bg"], s["states"], s["xms"], do)
    dx, dx_bf16, dps, dg1, dcw, dal, ddt = _in_proj_bwd(
        dq, dk, dv, dbg, s["qkv"], s["bd"], p["al"], p["dt"], dcv, dgb, s["sc_in"], dz, p["cw"], p["scw"], dx1, s["x"],
        p["g1"], p["land_a"])
    parts = lax.empty((N_CHIPS, A_ROWS, D_MODEL), BF16)
    parts = _wgrad_in_share(s["h"], dps, parts, "wgrad_in")
    parts = _wgrad_share(s["mix"], dx1_bf16, parts, A_OUT_AT, "wgrad_out")
    g = dict(g1=dg1[0], gn=dgn[0], gs=dgs[0], scw=dscw[:3], cw=dcw[:4], al=dal[0, HEADS:2 * HEADS], dt=ddt[0, HEADS:2 * HEADS])
    return dx, dx_bf16, parts, g


def _place():
    return lax.axis_index("x"), lax.axis_index("y"), lax.axis_index("c")


def _other_chips(x, y):
    return [(1 - x, y), (x, 1 - y), (1 - x, 1 - y)]


_HBM = pl.BlockSpec(memory_space=pltpu.HBM)


def _gather_chips(arrs, name):
    n = len(arrs)

    def body(*refs):
        ins, outs = refs[:n], refs[n:2 * n]
        send_sems, recv_sems, local_sems = refs[2 * n:]
        x, y, c = _place()
        me = 2 * x + y
        others = _other_chips(x, y)

        def remote(k, j, landing):
            px, py = others[j]
            return pltpu.make_async_remote_copy(src_ref=ins[k], dst_ref=outs[k].at[landing], send_sem=send_sems.at[k, j],
                                                recv_sem=recv_sems.at[k, j], device_id=(px, py, c), device_id_type=MESH)

        local = [pltpu.make_async_copy(ins[k], outs[k].at[me], local_sems.at[k]) for k in range(n)]
        sends = [remote(k, j, me) for k in range(n) for j in range(3)]
        for cp in local + sends:
            cp.start()
        for k in range(n):
            for j, (px, py) in enumerate(others):
                remote(k, j, 2 * px + py).wait_recv()
        for cp in sends:
            cp.wait_send()
        for cp in local:
            cp.wait()

    shapes = [jax.ShapeDtypeStruct((N_CHIPS,) + a.shape, a.dtype) for a in arrs]
    return pl.pallas_call(
        body, name=name, in_specs=[_HBM] * n, out_specs=[_HBM] * n, out_shape=shapes,
        scratch_shapes=[pltpu.SemaphoreType.DMA((n, 3)), pltpu.SemaphoreType.DMA((n, 3)), pltpu.SemaphoreType.DMA((n,))],
    )(*arrs)


_SEM = pl.BlockSpec(memory_space=pltpu.SEMAPHORE)
_ANY = pl.BlockSpec(memory_space=pl.ANY)
_EFFECT = pltpu.SideEffectType.DATAFLOW_SIDE_EFFECTING


_FLIPS = [(a, b, cc) for a in (0, 1) for b in (0, 1) for cc in (0, 1)][1:]


def _split_copies(src_ref, land_ref, send_sems, recv_sems, gather, sending):
    x, y, c = _place()
    copies = []
    if gather:
        me = 2 * x + y
        for j, (px, py) in enumerate(_other_chips(x, y)):
            copies.append(pltpu.make_async_remote_copy(
                src_ref=src_ref, dst_ref=land_ref.at[me if sending else 2 * px + py],
                send_sem=send_sems.at[j], recv_sem=recv_sems.at[j], device_id=(px, py, c), device_id_type=MESH))
        return copies
    me = 4 * x + 2 * y + c
    for j, (a, b, cc) in enumerate(_FLIPS):
        px, py, pc = (1 - x) if a else x, (1 - y) if b else y, (1 - c) if cc else c
        copies.append(pltpu.make_async_remote_copy(
            src_ref=src_ref.at[2 * px + py], dst_ref=land_ref.at[me if sending else 4 * px + 2 * py + pc],
            send_sem=send_sems.at[j], recv_sem=recv_sems.at[j], device_id=(px, py, pc), device_id_type=MESH))
    return copies


def _own_slot(share):
    chip = 2 * lax.axis_index("x") + lax.axis_index("y")
    return lax.dynamic_update_slice(lax.empty((N_CHIPS,) + share.shape, share.dtype), share[None], (chip, 0, 0))


def _own_part(parts):
    chip = 2 * lax.axis_index("x") + lax.axis_index("y")
    own = lax.dynamic_index_in_dim(parts, chip, 0, keepdims=True)
    return lax.dynamic_update_slice(lax.empty((N_DEV,) + parts.shape[1:], parts.dtype), own,
                                    (2 * chip + lax.axis_index("c"), 0, 0))


def _exchange_start(src, land, after, name, gather):
    def body(src_ref, land_ref, after_ref, send_sems, recv_sems, src_thru, land_thru, token):
        for cp in _split_copies(src_ref, land_ref, send_sems, recv_sems, gather, sending=True):
            cp.start()
        token[...] = jnp.zeros_like(token)

    hbm = lambda t: pltpu.with_memory_space_constraint(t, pltpu.HBM)
    n_copies = N_CHIPS - 1 if gather else N_DEV - 1
    return pl.pallas_call(
        body, name=name,
        out_shape=(pltpu.SemaphoreType.DMA((n_copies,)), pltpu.SemaphoreType.DMA((n_copies,)), pltpu.HBM(src.shape, src.dtype),
                   pltpu.HBM(land.shape, land.dtype), jax.ShapeDtypeStruct((8, LANES), F32)),
        in_specs=(_HBM, _HBM, _ANY), out_specs=(_SEM, _SEM, _HBM, _HBM, pl.BlockSpec(memory_space=pltpu.VMEM)),
        input_output_aliases={0: 2, 1: 3},
        compiler_params=pltpu.CompilerParams(has_side_effects=_EFFECT),
    )(hbm(src), hbm(land), after)


def _exchange_wait(started, after, name, gather):
    send_sems, recv_sems, src_thru, land_thru, _ = started

    def body(src_ref, land_ref, send_sems, recv_sems, after_ref, src_dead, got_ref):
        for cp in _split_copies(src_ref, land_ref, send_sems, recv_sems, gather, sending=False):
            cp.wait_send()
            cp.wait_recv()

    return pl.pallas_call(
        body, name=name,
        out_shape=(pltpu.HBM(src_thru.shape, src_thru.dtype), pltpu.HBM(land_thru.shape, land_thru.dtype)),
        in_specs=(_HBM, _HBM, _SEM, _SEM, _ANY), out_specs=(_HBM, _HBM), input_output_aliases={0: 0, 1: 1},
        compiler_params=pltpu.CompilerParams(has_side_effects=_EFFECT),
    )(src_thru, land_thru, send_sems, recv_sems, after)[1]


def _all_reduce_small(v):
    rows = v.shape[0]
    flips = [(a, b, cc) for a in (0, 1) for b in (0, 1) for cc in (0, 1)][1:]

    def body(v_ref, out_ref, buf_ref, send_sems, recv_sems):
        x, y, c = _place()
        me = 4 * x + 2 * y + c
        peers = [((1 - x) if a else x, (1 - y) if b else y, (1 - c) if cc else c) for a, b, cc in flips]

        def copy(j, landing):
            return pltpu.make_async_remote_copy(src_ref=v_ref, dst_ref=buf_ref.at[landing], send_sem=send_sems.at[j],
                                                recv_sem=recv_sems.at[j], device_id=peers[j], device_id_type=MESH)

        sends = [copy(j, me) for j in range(N_DEV - 1)]
        for cp in sends:
            cp.start()
        buf_ref[me] = v_ref[...]
        for j, (px, py, pc) in enumerate(peers):
            copy(j, 4 * px + 2 * py + pc).wait_recv()
        for cp in sends:
            cp.wait_send()
        acc = buf_ref[0]
        for d in range(1, N_DEV):
            acc = acc + buf_ref[d]
        out_ref[...] = acc

    vmem = pl.BlockSpec(memory_space=pltpu.VMEM)
    return pl.pallas_call(
        body, name="all_reduce_small", in_specs=[vmem], out_specs=vmem,
        out_shape=jax.ShapeDtypeStruct(v.shape, F32),
        scratch_shapes=[pltpu.VMEM((N_DEV, rows, LANES), F32), pltpu.SemaphoreType.DMA((N_DEV - 1,)),
                        pltpu.SemaphoreType.DMA((N_DEV - 1,))],
    )(v)


def _row_block(*sizes):
    return next(t for t in (256, 176, 128, 64) if all(s % t == 0 for s in sizes))


def _adam_update(w, m, v, g):
    r1 = 1.0 / (1.0 - ADAM_B1 ** ADAM_STEP)
    r2 = 1.0 / (1.0 - ADAM_B2 ** ADAM_STEP)
    m_new = ADAM_B1 * m + (1.0 - ADAM_B1) * g
    v_new = ADAM_B2 * v + (1.0 - ADAM_B2) * (g * g)
    return -ADAM_LR * ((m_new * r1) / (jnp.sqrt(v_new * r2) + ADAM_EPS) + ADAM_WD * w), m_new, v_new


def _adamw_rows(w, m, v, got, first, name):
    n_layers, rows, cols = w.shape
    tr = _row_block(rows, first)

    def body(*refs):
        w_ref, m_ref, v_ref = refs[:3]
        g_out, d_out, m_out, v_out = refs[3 + n_layers:]
        for k in range(n_layers):
            @pl.when(pl.program_id(0) == k)
            def _(p_ref=refs[3 + k]):
                g = p_ref[0].astype(F32)
                for d in range(1, N_DEV):
                    g = g + p_ref[d].astype(F32)
                g = g[:, :cols]
                d_out[0], m_out[0], v_out[0] = _adam_update(w_ref[0], m_ref[0], v_ref[0], g)
                g_out[0] = g

    blk = pl.BlockSpec((1, tr, cols), lambda l, i: (l, i, 0))
    parts = [pl.BlockSpec((N_DEV, tr, got[0].shape[2]), lambda l, i, k=k: (0, jnp.where(l == k, first // tr + i, 0), 0))
             for k in range(n_layers)]
    return pl.pallas_call(
        body, name=name, grid=(n_layers, rows // tr),
        in_specs=[blk] * 3 + parts, out_specs=[blk] * 4,
        out_shape=[jax.ShapeDtypeStruct(w.shape, F32)] * 4,
        compiler_params=_params(("arbitrary", "arbitrary")),
    )(w, m, v, *got)


def _adamw(w, m, v, g_parts, name):
    rows, cols = w.shape
    tr = min(rows, 256)
    n = len(g_parts)

    def body(*refs):
        w_ref, m_ref, v_ref = refs[:3]
        g_refs = refs[3:3 + n]
        g_out, d_out, m_out, v_out = refs[3 + n:]
        g = g_refs[0][...]
        for r in g_refs[1:]:
            g = g + r[...]
        d_out[...], m_out[...], v_out[...] = _adam_update(w_ref[...], m_ref[...], v_ref[...], g)
        g_out[...] = g

    blk = pl.BlockSpec((tr, cols), lambda i: (i, 0))
    return pl.pallas_call(
        body, name=name, grid=(rows // tr,),
        in_specs=[blk] * (3 + n), out_specs=[blk] * 4,
        out_shape=[jax.ShapeDtypeStruct((rows, cols), F32)] * 4,
        compiler_params=_params(("parallel",)),
    )(w, m, v, *g_parts)


def _pack(parts, rows, fill=0.0):
    flat = jnp.concatenate([p.reshape(-1) for p in parts])
    return jnp.pad(flat, (0, rows * LANES - flat.shape[0]), constant_values=fill).reshape(rows, LANES)


def _unpack(packed, shapes):
    flat = packed.reshape(-1)
    out, at = [], 0
    for shp in shapes:
        size = 1
        for s in shp:
            size *= s
        out.append(flat[at:at + size].reshape(shp))
        at += size
    return out


def _packed_rows(shapes):
    total = 0
    for shp in shapes:
        size = 1
        for s in shp:
            size *= s
        total += size
    return -(-total // (8 * LANES)) * 8


def _cols_full(g, l):
    t = g[:, l]
    return jnp.moveaxis(t, 0, 1).reshape(t.shape[1], N_CHIPS * t.shape[2])


def _pad_cols(t):
    return jnp.pad(t, ((0, 0),) * (t.ndim - 1) + ((0, D_MODEL - t.shape[-1]),))


def kernel(x, norm1_g, w_in, dn_conv_w, dn_a_log, dn_dt_bias, dn_norm_g, sc_conv_w, sc_norm_g, w_out, norm2_g, ffn_w_gate, ffn_w_up, ffn_w_down, final_norm_g, loss_target, m_norm1_g, m_w_in, m_dn_conv_w, m_dn_a_log, m_dn_dt_bias, m_dn_norm_g, m_sc_conv_w, m_sc_norm_g, m_w_out, m_norm2_g, m_ffn_w_gate, m_ffn_w_up, m_ffn_w_down, m_final_norm_g, v_norm1_g, v_w_in, v_dn_conv_w, v_dn_a_log, v_dn_dt_bias, v_dn_norm_g, v_sc_conv_w, v_sc_norm_g, v_w_out, v_norm2_g, v_ffn_w_gate, v_ffn_w_up, v_ffn_w_down, v_final_norm_g):
    chip = 2 * lax.axis_index("x") + lax.axis_index("y")

    g_cw, g_scw = _gather_chips([dn_conv_w, sc_conv_w], "gather_conv")

    t_last = lambda t: jnp.swapaxes(t, -1, -2)
    gate_t, up_t = t_last(ffn_w_gate), t_last(ffn_w_up)
    zero_token = jnp.zeros((8, LANES), F32)

    def shares(l, tie):
        share_a = jnp.concatenate([_pad_cols(w_in[l] + tie), w_out[l]], axis=0).astype(BF16)
        share_b = jnp.concatenate([gate_t[l] + tie, up_t[l], ffn_w_down[l]], axis=0).astype(BF16)
        return share_a, _own_slot(share_a), share_b, _own_slot(share_b)

    def gather_start(l, packed, after):
        a = _exchange_start(packed[0], packed[1], after, "gather_a_start_%d" % l, gather=True)
        b = _exchange_start(packed[2], packed[3], a[4], "gather_b_start_%d" % l, gather=True)
        return a, b

    ga, gb = gather_start(0, shares(0, 0.0), g_cw)
    packed = [None] + [shares(l, gb[4][0, 0]) for l in range(1, DEPTH)]
    packed_all = sum(t[0, 0].astype(F32) for p in packed[1:] for t in (p[0], p[2]))
    land_a = _exchange_wait(ga, zero_token + packed_all, "gather_a_wait_0", gather=True)
    act = x[0]
    layers, saved_m, saved_f, lands_b = [], [], [], []
    for l in range(DEPTH):
        hold = 0.0
        if l + 1 < DEPTH:
            ga, gb_next = gather_start(l + 1, packed[l + 1], land_a)
            hold = gb_next[4][0:1, 0:1]
        al, dt = _gate_rows(dn_a_log[l], dn_dt_bias[l])
        layers.append(dict(
            g1=norm1_g[l][None] + hold, cw=_pad_rows(_cols_full(g_cw, l)), al=al, dt=dt,
            gn=dn_norm_g[l][None], scw=_pad_rows(_cols_full(g_scw, l)), gs=sc_norm_g[l][None],
            land_a=land_a, g2=norm2_g[l][None]))
        s = _mixer_fwd(act, layers[l])
        lands_b.append(_exchange_wait(gb, s["o"], "gather_b_wait_%d" % l, gather=True))
        act, s, sf = _tail_fwd(s, layers[l], lands_b[l])
        saved_m.append(s)
        saved_f.append(sf)
        if l + 1 < DEPTH:
            land_a = _exchange_wait(ga, act, "gather_a_wait_%d" % (l + 1), gather=True)
            gb = gb_next

    dact, dact_bf16, loss_part, d_final = _loss_head(act, final_norm_g[None], loss_target[0])
    grads, reduce_a, reduce_b = [None] * DEPTH, [None] * DEPTH, [None] * DEPTH
    hold = 0.0
    for l in reversed(range(DEPTH)):
        p = layers[l]
        dx1, parts, dg2 = _ffn_back(dact, dact_bf16, saved_f[l], dict(p, g2=p["g2"] + hold), lands_b[l])
        reduce_b[l] = _exchange_start(parts, _own_part(parts), zero_token, "reduce_b_start_%d" % l, gather=False)
        dact, dact_bf16, parts, gm = _mixer_bwd(dx1, saved_m[l], dict(p, gn=p["gn"] + reduce_b[l][4][0:1, 0:1]))
        reduce_a[l] = _exchange_start(parts, _own_part(parts), zero_token, "reduce_a_start_%d" % l, gather=False)
        hold = reduce_a[l][4][0:1, 0:1]
        grads[l] = dict(gm, g2=dg2)
    loss = lax.psum(loss_part[0, 0], ("x", "y", "c"))
    stack = lambda key: jnp.stack([grads[l][key] for l in range(DEPTH)])

    got_b = [_exchange_wait(reduce_b[l], reduce_a[0][4], "reduce_b_wait_%d" % l, gather=False)
             for l in reversed(range(DEPTH))][::-1]
    big = dict(
        ffn_w_gate=[t_last(o) for o in _adamw_rows(gate_t, t_last(m_ffn_w_gate), t_last(v_ffn_w_gate), got_b, 0, "adamw_gate")],
        ffn_w_up=[t_last(o) for o in _adamw_rows(up_t, t_last(m_ffn_w_up), t_last(v_ffn_w_up), got_b, FF_SHARD, "adamw_up")],
        ffn_w_down=_adamw_rows(ffn_w_down, m_ffn_w_down, v_ffn_w_down, got_b, 2 * FF_SHARD, "adamw_down"))
    after_b = zero_token + sum(big[n][1][0, 0, 0] for n in ("ffn_w_gate", "ffn_w_up", "ffn_w_down"))
    got_a = [_exchange_wait(reduce_a[l], after_b, "reduce_a_wait_%d" % l, gather=False) for l in reversed(range(DEPTH))][::-1]
    big.update(
        w_in=_adamw_rows(w_in, m_w_in, v_w_in, got_a, 0, "adamw_w_in"),
        w_out=_adamw_rows(w_out, m_w_out, v_w_out, got_a, A_OUT_AT, "adamw_w_out"))

    full_shapes = [(DEPTH, D_MODEL), (DEPTH, D_MODEL), (DEPTH, HEAD_DIM), (DEPTH, SC_WIDTH), (DEPTH, HEADS),
                   (DEPTH, HEADS), (D_MODEL,), (DEPTH, 4, QKV), (DEPTH, 3, SC_WIDTH)]
    small_keys = ("g1", "g2", "gn", "gs", "al", "dt")
    packed = _pack([stack(k) for k in small_keys] + [d_final[0], stack("cw"), stack("scw")], _packed_rows(full_shapes))
    sg = _unpack(_all_reduce_small(packed), full_shapes)
    sg[7] = lax.dynamic_slice_in_dim(sg[7], chip * (QKV // N_CHIPS), QKV // N_CHIPS, axis=2)
    sg[8] = lax.dynamic_slice_in_dim(sg[8], chip * (SC_WIDTH // N_CHIPS), SC_WIDTH // N_CHIPS, axis=2)
    small_names = ("norm1_g", "norm2_g", "dn_norm_g", "sc_norm_g", "dn_a_log", "dn_dt_bias", "final_norm_g",
                   "dn_conv_w", "sc_conv_w")
    sw = (norm1_g, norm2_g, dn_norm_g, sc_norm_g, dn_a_log, dn_dt_bias, final_norm_g, dn_conv_w, sc_conv_w)
    sm = (m_norm1_g, m_norm2_g, m_dn_norm_g, m_sc_norm_g, m_dn_a_log, m_dn_dt_bias, m_final_norm_g, m_dn_conv_w, m_sc_conv_w)
    sv = (v_norm1_g, v_norm2_g, v_dn_norm_g, v_sc_norm_g, v_dn_a_log, v_dn_dt_bias, v_final_norm_g, v_dn_conv_w, v_sc_conv_w)
    shard_shapes = [t.shape for t in sw]
    rows = _packed_rows(shard_shapes)
    outs = _adamw(_pack(sw, rows), _pack(sm, rows), _pack(sv, rows, fill=1.0), [_pack(sg, rows)], "adamw_small")
    small = {name: [] for name in small_names}
    for o in outs:
        for name, t in zip(small_names, _unpack(o, shard_shapes)):
            small[name].append(t)

    order = ("norm1_g", "w_in", "dn_conv_w", "dn_a_log", "dn_dt_bias", "dn_norm_g", "sc_conv_w", "sc_norm_g", "w_out",
             "norm2_g", "ffn_w_gate", "ffn_w_up", "ffn_w_down", "final_norm_g")
    result = {**big, **small}
    return (loss, dact[None], *[result[n][0] for n in order], *[result[n][1] for n in order],
            *[result[n][2] for n in order], *[result[n][3] for n in order])
```

```python
import jax
import jax.numpy as jnp
from jax import lax
from jax.experimental import pallas as pl
from jax.experimental.pallas import tpu as pltpu

F32 = jnp.float32
BF16 = jnp.bfloat16
MESH = pl.DeviceIdType.MESH

D_MODEL = 1024
DEPTH = 4
HEADS = 4
HEAD_DIM = 128
DN_WIDTH = HEADS * HEAD_DIM
SC_WIDTH = 512
SC_GROUPS = 4
D_FF = 2816
CHUNK = 64
QKV = 3 * DN_WIDTH
W_IN_COLS = 4 * DN_WIDTH + 2 * HEADS + 3 * SC_WIDTH
LANES = 128
EPS = 1e-6
Q_SCALE = HEAD_DIM ** -0.5
N_CHIPS = 4
N_DEV = 8
IN_SHARD = W_IN_COLS // N_CHIPS
OUT_SHARD = D_MODEL // N_CHIPS
FF_SHARD = D_FF // N_CHIPS
A_OUT_AT = D_MODEL
A_ROWS = D_MODEL + OUT_SHARD
B_ROWS = 3 * FF_SHARD

ADAM_LR = 0.001
ADAM_B1 = 0.9
ADAM_B2 = 0.999
ADAM_EPS = 1e-08
ADAM_WD = 0.01
ADAM_STEP = 10

VMEM_LIMIT = 56 * 1024 * 1024

NN = (((1,), (0,)), ((), ()))
NT = (((1,), (1,)), ((), ()))
TN = (((0,), (0,)), ((), ()))


def _mm(a, b, dims=NN):
    return lax.dot_general(a.astype(BF16), b.astype(BF16), dims, preferred_element_type=F32)


def _mm32(a, b, dims=NN):
    return lax.dot_general(a, b, dims, preferred_element_type=F32, precision=lax.Precision.HIGHEST)


def _params(sem, vmem=VMEM_LIMIT):
    return pltpu.CompilerParams(dimension_semantics=sem, vmem_limit_bytes=vmem)


def _sigmoid(x):
    return 0.5 * jnp.tanh(0.5 * x) + 0.5


def _softplus(x):
    return jnp.maximum(x, 0.0) + jnp.log1p(jnp.exp(-jnp.abs(x)))


def _row_acc(acc_ref, val):
    acc_ref[0:1, :] += jnp.sum(val, axis=0, keepdims=True)


def _rms_bwd(dh, xh, r, gain):
    dxh = dh * gain
    return r * (dxh - xh * jnp.mean(dxh * xh, axis=-1, keepdims=True))


def _before_halo(tb):
    return lambda i: (jnp.maximum(i * (tb // 8) - 1, 0), 0)


def _after_halo(tb, n_rows):
    last = n_rows // 8 - 1
    return lambda i: (jnp.minimum((i + 1) * (tb // 8), last), 0)


def _rows_from(xc, offset, tb):
    part = offset % 8
    if part:
        xc = pltpu.roll(xc, xc.shape[0] - part, 0)
    return xc[offset - part:offset - part + tb, :]


def _taps(xc, w, n_taps, tb, first):
    out = w[0:1, :] * _rows_from(xc, first, tb)
    for j in range(1, n_taps):
        out = out + w[j:j + 1, :] * _rows_from(xc, first + j, tb)
    return out


W_Z = QKV
W_BD = W_Z + DN_WIDTH
W_SC = W_BD + 2 * HEADS

def _w_in_cols(shards, lo, hi):
    pieces = []
    for s in range(N_CHIPS):
        a, b = max(lo, IN_SHARD * s), min(hi, IN_SHARD * (s + 1))
        if a < b:
            pieces.append(shards[s][:, a - IN_SHARD * s:b - IN_SHARD * s])
    return pieces[0] if len(pieces) == 1 else jnp.concatenate(pieces, axis=1)


def _in_proj(x, g1, land_a, cw, al_row, dt_row):
    T = x.shape[0]
    tb = 256

    def body(x_ref, g_ref, w_ref, cw_ref, al_ref, dt_ref,
             qkv_ref, z_ref, sc_ref, bd_ref, h_ref, q_ref, k_ref, v_ref, bg_ref, c_ref, tail_ref):
        @pl.when(pl.program_id(0) == 0)
        def _():
            tail_ref[...] = jnp.zeros_like(tail_ref)

        xv = x_ref[...]
        h = (xv * lax.rsqrt(jnp.mean(xv * xv, axis=-1, keepdims=True) + EPS) * g_ref[...]).astype(BF16)
        shards = [jnp.dot(h, w_ref[s], preferred_element_type=F32) for s in range(N_CHIPS)]
        qkv = _w_in_cols(shards, 0, W_Z)
        bd = jnp.concatenate([_w_in_cols(shards, W_BD, W_SC), jnp.zeros((tb, LANES - 2 * HEADS), F32)], axis=1)
        qkv_ref[...] = qkv
        z_ref[...] = _w_in_cols(shards, W_Z, W_BD)
        bd_ref[...] = bd
        sc_ref[...] = _w_in_cols(shards, W_SC, W_IN_COLS)
        h_ref[...] = h
        halo = tail_ref[...]
        tail_ref[...] = qkv[tb - 8:, :]
        _, c, _, a = _dn_act(qkv, halo, cw_ref[...], tb)
        c_ref[...] = c
        for hd in range(HEADS):
            sl = slice(HEAD_DIM * hd, HEAD_DIM * (hd + 1))
            qs = a[:, sl]
            q_ref[:, sl] = qs * (lax.rsqrt(jnp.sum(qs * qs, axis=-1, keepdims=True) + EPS) * Q_SCALE)
            ks = a[:, DN_WIDTH + HEAD_DIM * hd:DN_WIDTH + HEAD_DIM * (hd + 1)]
            k_ref[:, sl] = ks * lax.rsqrt(jnp.sum(ks * ks, axis=-1, keepdims=True) + EPS)
        v_ref[...] = a[:, 2 * DN_WIDTH:]
        gates = _gates(bd, al_ref[...], dt_ref[...])
        lane = lax.broadcasted_iota(jnp.int32, gates.shape, 1)
        bg_ref[...] = jnp.where(lane < HEADS, gates, _mm32(_chunk_cumsum_matrix(tb), gates))

    tok = lambda w: pl.BlockSpec((tb, w), lambda i: (i, 0))
    full = lambda t: pl.BlockSpec(t.shape, lambda i: (0, 0))
    return pl.pallas_call(
        body, name="in_proj", grid=(T // tb,),
        in_specs=[tok(D_MODEL), full(g1), _shard_rows(land_a, 0, D_MODEL), full(cw), full(al_row), full(dt_row)],
        out_specs=[tok(QKV), tok(DN_WIDTH), tok(3 * SC_WIDTH), tok(LANES), tok(D_MODEL),
                   tok(DN_WIDTH), tok(DN_WIDTH), tok(DN_WIDTH), tok(LANES), tok(QKV)],
        out_shape=[jax.ShapeDtypeStruct((T, QKV), F32), jax.ShapeDtypeStruct((T, DN_WIDTH), F32),
                   jax.ShapeDtypeStruct((T, 3 * SC_WIDTH), F32), jax.ShapeDtypeStruct((T, LANES), F32),
                   jax.ShapeDtypeStruct((T, D_MODEL), BF16)]
        + [jax.ShapeDtypeStruct((T, DN_WIDTH), F32)] * 3 + [jax.ShapeDtypeStruct((T, LANES), F32),
                                                              jax.ShapeDtypeStruct((T, QKV), F32)],
        scratch_shapes=[pltpu.VMEM((8, QKV), F32)],
        compiler_params=_params(("arbitrary",)),
    )(x, g1, land_a, cw, al_row, dt_row)


def _dn_act(pre, halo, cw, tb):
    xc = jnp.concatenate([halo, pre], axis=0)
    c = _taps(xc, cw, 4, tb, 5)
    sg = _sigmoid(c)
    return xc, c, sg, c * sg


def _gates(bd, al_row, dt_row):
    lane = lax.broadcasted_iota(jnp.int32, bd.shape, 1)
    beta = _sigmoid(bd)
    g = -jnp.exp(al_row) * _softplus(bd + dt_row)
    return jnp.where(lane < HEADS, beta, jnp.where(lane < 2 * HEADS, g, 0.0))


def _chunk_masks():
    row = lax.broadcasted_iota(jnp.int32, (CHUNK, CHUNK), 0)
    col = lax.broadcasted_iota(jnp.int32, (CHUNK, CHUNK), 1)
    return row >= col, row > col


def _chunk_cumsum_matrix(n):
    row = lax.broadcasted_iota(jnp.int32, (n, n), 0)
    col = lax.broadcasted_iota(jnp.int32, (n, n), 1)
    return jnp.logical_and(row >= col, row // CHUNK == col // CHUNK).astype(F32)


def _chunk_units(q_ref, k_ref, v_ref, bg_ref, rows):
    bgc = bg_ref[rows, :]
    bg_t = bgc.T
    qv, kv, vv = q_ref[rows, :], k_ref[rows, :], v_ref[rows, :]
    units = []
    for h in range(HEADS):
        sl = slice(HEAD_DIM * h, HEAD_DIM * (h + 1))
        units.append((qv[:, sl], kv[:, sl], vv[:, sl], bgc[:, h:h + 1], bgc[:, HEADS + h:HEADS + h + 1],
                      bg_t[HEADS + h:HEADS + h + 1, :]))
    return units


def _units_local(units, masks, xms=None):
    causal, strict = masks
    pre = []
    for q, k, v, beta, gc, gr in units:
        kb = k * beta
        eg = jnp.exp(gc)
        g_last = gc[CHUNK - 1:CHUNK, :]
        ek = jnp.exp(g_last - gc)
        pre.append(dict(q=q, k=k, v=v, beta=beta, decay=jnp.exp(jnp.where(causal, gc - gr, -1e30)), kb=kb, vb=v * beta,
                        eg=eg, kbg=kb * eg, ek=ek, gl=jnp.exp(g_last), q_dec=q * eg, k_dec=k * ek))
    both = [_mm(jnp.concatenate([p["kb"], p["q"]], axis=0), p["k"], NT) for p in pre]
    for p, b in zip(pre, both):
        p["low"] = jnp.where(strict, b[:CHUNK] * p["decay"], 0.0)
        p["qk"] = jnp.where(causal, b[CHUNK:] * p["decay"], 0.0)
    xs = xms
    if xs is None:
        xs = [-p["low"] for p in pre]
        pw = [_mm(p["low"], p["low"]) for p in pre]
        for _ in range(4):
            both = [_mm(jnp.concatenate([pp, x], axis=0), pp) for pp, x in zip(pw, xs)]
            xs = [x + pp + b[CHUNK:] for x, pp, b in zip(xs, pw, both)]
            pw = [b[:CHUNK] for b in both]
        last = [_mm(x, pp) for x, pp in zip(xs, pw)]
        xs = [x + pp + b for x, pp, b in zip(xs, pw, last)]
    uw = [_mm(x, jnp.concatenate([p["vb"], p["kbg"]], axis=1)) for x, p in zip(xs, pre)]
    for p, x, b in zip(pre, xs, uw):
        p["xm"] = x
        p["u"] = p["vb"] + b[:, :HEAD_DIM]
        p["w"] = p["kbg"] + b[:, HEAD_DIM:]
    return pre


FWD_GROUP = 8
BWD_GROUP = 8


def _delta_fwd(q, k, v, bg):
    T = q.shape[0]
    tb = 512
    n_chunk = tb // CHUNK

    def body(q_ref, k_ref, v_ref, bg_ref, o_ref, st_ref, xm_ref, s_ref):
        @pl.when(pl.program_id(0) == 0)
        def _():
            s_ref[...] = jnp.zeros_like(s_ref)

        masks = _chunk_masks()

        def group(gi, carry):
            rows = [pl.ds(pl.multiple_of((FWD_GROUP * gi + j) * CHUNK, CHUNK), CHUNK) for j in range(FWD_GROUP)]
            loc = _units_local(sum((_chunk_units(q_ref, k_ref, v_ref, bg_ref, r) for r in rows), []), masks)
            states = [s_ref[h] for h in range(HEADS)]
            for j in range(FWD_GROUP):
                lj = loc[HEADS * j:HEADS * (j + 1)]
                ws = [_mm(jnp.concatenate([p["w"], p["q_dec"]], axis=0), s) for p, s in zip(lj, states)]
                v_new = [p["u"] - b[:CHUNK] for p, b in zip(lj, ws)]
                intra = [_mm(p["qk"], vn) for p, vn in zip(lj, v_new)]
                upd = [_mm(p["k_dec"], vn, TN) for p, vn in zip(lj, v_new)]
                o_ref[rows[j], :] = jnp.concatenate([b[CHUNK:] + a for b, a in zip(ws, intra)], axis=1)
                for h in range(HEADS):
                    st_ref[FWD_GROUP * gi + j, h] = states[h]
                    xm_ref[FWD_GROUP * gi + j, h] = lj[h]["xm"]
                states = [p["gl"] * s + d for p, s, d in zip(lj, states, upd)]
            for h in range(HEADS):
                s_ref[h] = states[h]
            return carry

        lax.fori_loop(0, n_chunk // FWD_GROUP, group, 0)

    tok = lambda w: pl.BlockSpec((tb, w), lambda i: (i, 0))
    return pl.pallas_call(
        body, name="delta_fwd", grid=(T // tb,),
        in_specs=[tok(DN_WIDTH), tok(DN_WIDTH), tok(DN_WIDTH), tok(LANES)],
        out_specs=[tok(DN_WIDTH), pl.BlockSpec((n_chunk, HEADS, HEAD_DIM, HEAD_DIM), lambda i: (i, 0, 0, 0)),
                   pl.BlockSpec((n_chunk, HEADS, CHUNK, CHUNK), lambda i: (i, 0, 0, 0))],
        out_shape=[jax.ShapeDtypeStruct((T, DN_WIDTH), F32),
                   jax.ShapeDtypeStruct((T // CHUNK, HEADS, HEAD_DIM, HEAD_DIM), F32),
                   jax.ShapeDtypeStruct((T // CHUNK, HEADS, CHUNK, CHUNK), F32)],
        scratch_shapes=[pltpu.VMEM((HEADS, HEAD_DIM, HEAD_DIM), F32)],
        compiler_params=_params(("arbitrary",)),
    )(q, k, v, bg)


def _dn_out(o, z, gn):
    outs, ohs, rs = [], [], []
    for hh in range(HEADS):
        oh = o[:, HEAD_DIM * hh:HEAD_DIM * (hh + 1)]
        r = lax.rsqrt(jnp.mean(oh * oh, axis=-1, keepdims=True) + EPS)
        ohs.append(oh * r)
        rs.append(r)
    sz = _sigmoid(z)
    oh = jnp.concatenate(ohs, axis=1)
    gn4 = jnp.concatenate([gn] * HEADS, axis=1)
    return oh * gn4 * (z * sz), oh, rs, sz, gn4


def _sc_fwd(sc_in, halo, cw, tb):
    xc = jnp.concatenate([halo, sc_in], axis=0)
    u = xc[:, SC_WIDTH:2 * SC_WIDTH] * xc[:, 2 * SC_WIDTH:]
    cv = _taps(u, cw, 3, tb, 6)
    gate_b = sc_in[:, :SC_WIDTH]
    y = gate_b * cv
    gw = SC_WIDTH // SC_GROUPS
    yhs, rs = [], []
    for gi in range(SC_GROUPS):
        yg = y[:, gw * gi:gw * (gi + 1)]
        r = lax.rsqrt(jnp.mean(yg * yg, axis=-1, keepdims=True) + EPS)
        yhs.append(yg * r)
        rs.append(r)
    return u, cv, gate_b, jnp.concatenate(yhs, axis=1), rs


def _shard_rows(land, first, rows, single_buffer=False):
    assert first % rows == 0 and land.shape[0] == N_CHIPS
    mode = dict(pipeline_mode=pl.Buffered(1)) if single_buffer else {}
    return pl.BlockSpec((N_CHIPS, rows, land.shape[2]), lambda i: (0, first // rows, 0), **mode)


def _whole(w_ref):
    n, rows, cols = w_ref.shape
    return w_ref[...].reshape(n * rows, cols)


def _mix_ffn(o, z, sc_in, x, land_a, gn, scw, gs, g2, land_b):
    T = x.shape[0]
    tb = 256

    def body(o_ref, z_ref, sc_ref, halo_ref, x_ref, wo_ref, gn_ref, scw_ref, gs_ref, g2_ref, wgt_ref, wut_ref, wd_ref,
             x1_ref, mix_ref, x2_ref, a_ref, b_ref, h_ref):
        o_n = _dn_out(o_ref[...], z_ref[...], gn_ref[...])[0]
        halo = jnp.where(pl.program_id(0) > 0, halo_ref[...], 0.0)
        yh = _sc_fwd(sc_ref[...], halo, scw_ref[...], tb)[3]
        mix = jnp.concatenate([o_n, yh * gs_ref[...]], axis=1).astype(BF16)
        x1 = x_ref[...] + jnp.dot(mix, _whole(wo_ref), preferred_element_type=F32)
        x1_ref[...] = x1
        mix_ref[...] = mix
        r = lax.rsqrt(jnp.mean(x1 * x1, axis=-1, keepdims=True) + EPS)
        h = (x1 * r * g2_ref[...]).astype(BF16)
        a = lax.dot_general(h, _whole(wgt_ref), NT, preferred_element_type=F32)
        b = lax.dot_general(h, _whole(wut_ref), NT, preferred_element_type=F32)
        act = (a * _sigmoid(a) * b).astype(BF16)
        x2_ref[...] = x1 + jnp.dot(act, _whole(wd_ref), preferred_element_type=F32)
        a_ref[...] = a.astype(BF16)
        b_ref[...] = b.astype(BF16)
        h_ref[...] = h

    tok = lambda w: pl.BlockSpec((tb, w), lambda i: (i, 0))
    full = lambda t: pl.BlockSpec(t.shape, lambda i: (0, 0))
    once = lambda land, first, rows: _shard_rows(land, first, rows, single_buffer=True)
    return pl.pallas_call(
        body, name="mix_ffn", grid=(T // tb,),
        in_specs=[tok(DN_WIDTH), tok(DN_WIDTH), tok(3 * SC_WIDTH), pl.BlockSpec((8, 3 * SC_WIDTH), _before_halo(tb)),
                  tok(D_MODEL), once(land_a, A_OUT_AT, OUT_SHARD), full(gn), full(scw), full(gs), full(g2),
                  once(land_b, 0, FF_SHARD), once(land_b, FF_SHARD, FF_SHARD), once(land_b, 2 * FF_SHARD, FF_SHARD)],
        out_specs=[tok(D_MODEL), tok(D_MODEL), tok(D_MODEL), tok(D_FF), tok(D_FF), tok(D_MODEL)],
        out_shape=[jax.ShapeDtypeStruct((T, D_MODEL), F32), jax.ShapeDtypeStruct((T, D_MODEL), BF16),
                   jax.ShapeDtypeStruct((T, D_MODEL), F32), jax.ShapeDtypeStruct((T, D_FF), BF16),
                   jax.ShapeDtypeStruct((T, D_FF), BF16), jax.ShapeDtypeStruct((T, D_MODEL), BF16)],
        compiler_params=_params(("parallel",)),
    )(o, z, sc_in, sc_in, x, land_a, gn, scw, gs, g2, land_b, land_b, land_b)


def _loss_head(x, gf, target):
    T = x.shape[0]
    tb = 512

    def body(x_ref, g_ref, t_ref, dx_ref, dxb_ref, loss_ref, dg_ref):
        @pl.when(pl.program_id(0) == 0)
        def _():
            loss_ref[...] = jnp.zeros_like(loss_ref)
            dg_ref[...] = jnp.zeros_like(dg_ref)

        xv = x_ref[...]
        r = lax.rsqrt(jnp.mean(xv * xv, axis=-1, keepdims=True) + EPS)
        xh = xv * r
        err = xh * g_ref[...] - t_ref[...]
        per_tok = jnp.mean(err * err, axis=-1, keepdims=True)
        loss_ref[...] += 0.5 * jnp.sum(per_tok, axis=0, keepdims=True)
        dy = err * (1.0 / D_MODEL)
        _row_acc(dg_ref, dy * xh)
        dx = _rms_bwd(dy, xh, r, g_ref[...])
        dx_ref[...] = dx
        dxb_ref[...] = dx.astype(BF16)

    tok = pl.BlockSpec((tb, D_MODEL), lambda i: (i, 0))
    return pl.pallas_call(
        body, name="loss_head", grid=(T // tb,),
        in_specs=[tok, pl.BlockSpec(gf.shape, lambda i: (0, 0)), tok],
        out_specs=[tok, tok, pl.BlockSpec((8, LANES), lambda i: (0, 0)), pl.BlockSpec((8, D_MODEL), lambda i: (0, 0))],
        out_shape=[jax.ShapeDtypeStruct((T, D_MODEL), F32), jax.ShapeDtypeStruct((T, D_MODEL), BF16),
                   jax.ShapeDtypeStruct((8, LANES), F32), jax.ShapeDtypeStruct((8, D_MODEL), F32)],
        compiler_params=_params(("arbitrary",)),
    )(x, gf, target)


def _ffn_bwd(dx2, x1, a, b, g2, land_b):
    T = x1.shape[0]
    tb = 256

    def body(dx2_ref, x_ref, a_ref, b_ref, g_ref, wgt_ref, wut_ref, wd_ref,
             dx1_ref, da_ref, db_ref, act_ref, dg_ref):
        @pl.when(pl.program_id(0) == 0)
        def _():
            dg_ref[...] = jnp.zeros_like(dg_ref)

        dx2v = dx2_ref[...]
        av = a_ref[...].astype(F32)
        bv = b_ref[...].astype(F32)
        dact = _mm(dx2v, _whole(wd_ref), NT)
        sa = _sigmoid(av)
        silu = av * sa
        da = (dact * bv * (sa * (1.0 + av * (1.0 - sa)))).astype(BF16)
        db = (dact * silu).astype(BF16)
        dh = _mm(da, _whole(wgt_ref)) + _mm(db, _whole(wut_ref))
        xv = x_ref[...]
        r = lax.rsqrt(jnp.mean(xv * xv, axis=-1, keepdims=True) + EPS)
        xh = xv * r
        _row_acc(dg_ref, dh * xh)
        dx1 = dx2v + _rms_bwd(dh, xh, r, g_ref[...])
        dx1_ref[...] = dx1
        da_ref[...] = da
        db_ref[...] = db
        act_ref[...] = (silu * bv).astype(BF16)

    tok = lambda w: pl.BlockSpec((tb, w), lambda i: (i, 0))
    return pl.pallas_call(
        body, name="ffn_bwd", grid=(T // tb,),
        in_specs=[tok(D_MODEL), tok(D_MODEL), tok(D_FF), tok(D_FF), pl.BlockSpec(g2.shape, lambda i: (0, 0)),
                  _shard_rows(land_b, 0, FF_SHARD), _shard_rows(land_b, FF_SHARD, FF_SHARD),
                  _shard_rows(land_b, 2 * FF_SHARD, FF_SHARD)],
        out_specs=[tok(D_MODEL), tok(D_FF), tok(D_FF), tok(D_FF), pl.BlockSpec((8, D_MODEL), lambda i: (0, 0))],
        out_shape=[jax.ShapeDtypeStruct((T, D_MODEL), F32)]
        + [jax.ShapeDtypeStruct((T, D_FF), BF16)] * 3 + [jax.ShapeDtypeStruct((8, D_MODEL), F32)],
        compiler_params=_params(("arbitrary",)),
    )(dx2, x1, a, b, g2, land_b, land_b, land_b)


WGRAD_TOKENS = 2048


def _wgrad_share(a, b, parts, first, name):
    T = b.shape[0]
    rows = a.shape[1] // N_CHIPS
    assert first % rows == 0 and b.shape[1] == parts.shape[2]
    bk = min(T, WGRAD_TOKENS)
    n_k = T // bk
    group = 2
    assert (group * rows) % LANES == 0

    def body(a_ref, b_ref, parts_ref, o_ref, acc_ref):
        kk = pl.program_id(1)

        @pl.when(kk == 0)
        def _():
            acc_ref[...] = jnp.zeros_like(acc_ref)

        acc_ref[...] += lax.dot_general(a_ref[...], b_ref[...], TN, preferred_element_type=F32)

        @pl.when(kk == n_k - 1)
        def _():
            for s in range(group):
                o_ref[s] = acc_ref[rows * s:rows * (s + 1), :].astype(BF16)

    return pl.pallas_call(
        body, name=name, grid=(N_CHIPS // group, n_k),
        in_specs=[pl.BlockSpec((bk, group * rows), lambda i, kk: (kk, i)),
                  pl.BlockSpec((bk, b.shape[1]), lambda i, kk: (kk, 0)), _ANY],
        out_specs=pl.BlockSpec((group, rows, b.shape[1]), lambda i, kk: (i, first // rows, 0)),
        out_shape=jax.ShapeDtypeStruct(parts.shape, BF16),
        scratch_shapes=[pltpu.VMEM((group * rows, b.shape[1]), F32)],
        input_output_aliases={2: 0},
        compiler_params=_params(("parallel", "arbitrary")),
    )(a, b, parts)


def _mix_out_bwd(dx1, o, z, sc_in, land_a, gn, scw, gs):
    T = dx1.shape[0]
    tb = 256

    def body(dx_ref, o_ref, z_ref, sc_ref, halo_ref, w_ref, gn_ref, scw_ref, gs_ref,
             do_ref, dz_ref, dgb_ref, dcv_ref, dxb_ref, dgn_ref, dgs_ref, dscw_ref):
        @pl.when(pl.program_id(0) == 0)
        def _():
            dgn_ref[...] = jnp.zeros_like(dgn_ref)
            dgs_ref[...] = jnp.zeros_like(dgs_ref)
            dscw_ref[...] = jnp.zeros_like(dscw_ref)

        dx_bf16 = dx_ref[...].astype(BF16)
        dxb_ref[...] = dx_bf16
        dmix = lax.dot_general(dx_bf16, _whole(w_ref), NT, preferred_element_type=F32)
        don = dmix[:, :DN_WIDTH]
        dosc = dmix[:, DN_WIDTH:]
        zv = z_ref[...]
        _, oh, rs, sz, gn4 = _dn_out(o_ref[...], zv, gn_ref[...])
        silu_z = zv * sz
        dgn_full = don * oh * silu_z
        dgn_ref[0:1, :] += jnp.sum(sum(dgn_full[:, HEAD_DIM * hh:HEAD_DIM * (hh + 1)] for hh in range(HEADS)),
                                   axis=0, keepdims=True)
        dz_ref[...] = (don * oh * gn4 * (sz * (1.0 + zv * (1.0 - sz)))).astype(BF16)
        t = don * gn4 * silu_z
        for hh in range(HEADS):
            sl = slice(HEAD_DIM * hh, HEAD_DIM * (hh + 1))
            th, ohh = t[:, sl], oh[:, sl]
            do_ref[:, sl] = rs[hh] * (th - ohh * jnp.mean(th * ohh, axis=-1, keepdims=True))
        halo = jnp.where(pl.program_id(0) > 0, halo_ref[...], 0.0)
        u, cv, gate_b, yh, rys = _sc_fwd(sc_ref[...], halo, scw_ref[...], tb)
        _row_acc(dgs_ref, dosc * yh)
        ty = dosc * gs_ref[...]
        gw = SC_WIDTH // SC_GROUPS
        dys = []
        for gi in range(SC_GROUPS):
            sl = slice(gw * gi, gw * (gi + 1))
            tg, yg = ty[:, sl], yh[:, sl]
            dys.append(rys[gi] * (tg - yg * jnp.mean(tg * yg, axis=-1, keepdims=True)))
        dy = jnp.concatenate(dys, axis=1)
        dgb_ref[...] = dy * cv
        dcv = dy * gate_b
        dcv_ref[...] = dcv
        for j in range(3):
            dscw_ref[j:j + 1, :] += jnp.sum(dcv * _rows_from(u, 6 + j, tb), axis=0, keepdims=True)

    tok = lambda w: pl.BlockSpec((tb, w), lambda i: (i, 0))
    full = lambda t: pl.BlockSpec(t.shape, lambda i: (0, 0))
    acc = lambda w: pl.BlockSpec((8, w), lambda i: (0, 0))
    return pl.pallas_call(
        body, name="mix_out_bwd", grid=(T // tb,),
        in_specs=[tok(D_MODEL), tok(DN_WIDTH), tok(DN_WIDTH), tok(3 * SC_WIDTH),
                  pl.BlockSpec((8, 3 * SC_WIDTH), _before_halo(tb)), _shard_rows(land_a, A_OUT_AT, OUT_SHARD),
                  full(gn), full(scw), full(gs)],
        out_specs=[tok(DN_WIDTH), tok(DN_WIDTH), tok(SC_WIDTH), tok(SC_WIDTH), tok(D_MODEL),
                   acc(HEAD_DIM), acc(SC_WIDTH), acc(SC_WIDTH)],
        out_shape=[jax.ShapeDtypeStruct((T, DN_WIDTH), F32), jax.ShapeDtypeStruct((T, DN_WIDTH), BF16),
                   jax.ShapeDtypeStruct((T, SC_WIDTH), F32), jax.ShapeDtypeStruct((T, SC_WIDTH), F32),
                   jax.ShapeDtypeStruct((T, D_MODEL), BF16),
                   jax.ShapeDtypeStruct((8, HEAD_DIM), F32), jax.ShapeDtypeStruct((8, SC_WIDTH), F32),
                   jax.ShapeDtypeStruct((8, SC_WIDTH), F32)],
        compiler_params=_params(("arbitrary",)),
    )(dx1, o, z, sc_in, sc_in, land_a, gn, scw, gs)


def _delta_bwd(q, k, v, bg, states, xms, do):
    T = q.shape[0]
    tb = 512
    n_chunk = tb // CHUNK
    nb = T // tb

    def body(q_ref, k_ref, v_ref, bg_ref, st_ref, xm_ref, do_ref, dq_ref, dk_ref, dv_ref, dbg_ref, ds_ref):
        @pl.when(pl.program_id(0) == 0)
        def _():
            ds_ref[...] = jnp.zeros_like(ds_ref)

        masks = _chunk_masks()
        causal, strict = masks
        lane = lax.broadcasted_iota(jnp.int32, (CHUNK, LANES), 1)
        last_row = lax.broadcasted_iota(jnp.int32, (CHUNK, 1), 0) == CHUNK - 1
        cat = jnp.concatenate
        heads = range(HEADS)

        def open_chunk(ci, loc):
            rows = pl.ds(pl.multiple_of(ci * CHUNK, CHUNK), CHUNK)
            dov = do_ref[rows, :]
            return dict(rows=rows, loc=loc, do=[dov[:, HEAD_DIM * h:HEAD_DIM * (h + 1)] for h in heads],
                        state=[st_ref[ci, h] for h in heads])

        def a_free(c):
            loc, do, state = c["loc"], c["do"], c["state"]
            w_s = [_mm(p["w"], s) for p, s in zip(loc, state)]
            c["dq_dec"] = [_mm(d, s, NT) for d, s in zip(do, state)]
            c["qk_do"] = [_mm(p["qk"], d, TN) for p, d in zip(loc, do)]
            c["qd_do"] = [_mm(p["q_dec"], d, TN) for p, d in zip(loc, do)]
            c["v_new"] = [p["u"] - t for p, t in zip(loc, w_s)]
            c["dqk"] = [jnp.where(causal, _mm(d, vn, NT), 0.0) for d, vn in zip(do, c["v_new"])]

        def a_state(c, ds_next):
            c["ds_next"] = ds_next
            kd_ds = [_mm(p["k_dec"], d) for p, d in zip(c["loc"], ds_next)]
            c["dk_dec"] = [_mm(vn, d, NT) for vn, d in zip(c["v_new"], ds_next)]
            c["dv_new"] = [a + b for a, b in zip(c["qk_do"], kd_ds)]

        def b_state(c):
            loc = c["loc"]
            w_dv = [_mm(p["w"], dvn, TN) for p, dvn in zip(loc, c["dv_new"])]
            c["dw"] = [-_mm(dvn, s, NT) for dvn, s in zip(c["dv_new"], c["state"])]
            return [loc[h]["gl"] * c["ds_next"][h] + c["qd_do"][h] - w_dv[h] for h in heads]

        def c_solve(c):
            loc, dv_new, dw = c["loc"], c["dv_new"], c["dw"]
            c["dtm"] = [_mm(cat([dvn, d], axis=1), cat([p["vb"], p["kbg"]], axis=1), NT) for dvn, d, p in zip(dv_new, dw, loc)]
            x_t = [_mm(p["xm"], cat([dvn, d], axis=1), TN) for p, dvn, d in zip(loc, dv_new, dw)]
            c["dvb"] = [dvn + t[:, :HEAD_DIM] for dvn, t in zip(dv_new, x_t)]
            c["dkbg"] = [d + t[:, HEAD_DIM:] for d, t in zip(dw, x_t)]

        def d_solve(c):
            c["y"] = [t + _mm(p["xm"], t, TN) for p, t in zip(c["loc"], c["dtm"])]

        def e_solve(c):
            c["dlow"] = [jnp.where(strict, -(t + _mm(t, p["xm"], NT)), 0.0) for p, t in zip(c["loc"], c["y"])]

        def f_close(c):
            loc, rows = c["loc"], c["rows"]
            dmm = [d * p["decay"] for d, p in zip(c["dlow"], loc)]
            dnn = [d * p["decay"] for d, p in zip(c["dqk"], loc)]
            by_k = [_mm(cat([a, b], axis=0), p["k"]) for a, b, p in zip(dmm, dnn, loc)]
            dk_mm = [_mm(cat([a, b], axis=0), cat([p["kb"], p["q"]], axis=0), TN) for a, b, p in zip(dmm, dnn, loc)]
            dq_out, dk_out, dv_out = [], [], []
            dbeta_all = jnp.zeros((CHUNK, LANES), F32)
            dgc_all = jnp.zeros((CHUNK, LANES), F32)
            for h in heads:
                p = loc[h]
                dkb = by_k[h][:CHUNK] + c["dkbg"][h] * p["eg"]
                dq_out.append(by_k[h][CHUNK:] + c["dq_dec"][h] * p["eg"])
                dk_out.append(dk_mm[h] + c["dk_dec"][h] * p["ek"] + dkb * p["beta"])
                dv_out.append(c["dvb"][h] * p["beta"])
                dbeta = jnp.sum(dkb * p["k"] + c["dvb"][h] * p["v"], axis=1, keepdims=True)
                e = c["dlow"][h] * p["low"] + c["dqk"][h] * p["qk"]
                kd = jnp.sum(c["dk_dec"][h] * p["k_dec"], axis=1, keepdims=True)
                dgc = (jnp.sum(e, axis=1, keepdims=True) - jnp.sum(e.T, axis=1, keepdims=True)
                       + jnp.sum(c["dq_dec"][h] * p["q_dec"], axis=1, keepdims=True) - kd
                       + jnp.sum(c["dkbg"][h] * p["kbg"], axis=1, keepdims=True))
                dgl = jnp.sum(jnp.sum(c["ds_next"][h] * c["state"][h], axis=1, keepdims=True), axis=0, keepdims=True)
                d_last = jnp.sum(kd, axis=0, keepdims=True) + dgl * p["gl"]
                dgc = dgc + jnp.where(last_row, d_last, 0.0)
                dbeta_all = jnp.where(lane == h, dbeta, dbeta_all)
                dgc_all = jnp.where(lane == h + HEADS, dgc, dgc_all)
            dq_ref[rows, :] = cat(dq_out, axis=1)
            dk_ref[rows, :] = cat(dk_out, axis=1)
            dv_ref[rows, :] = cat(dv_out, axis=1)
            dbg_ref[rows, :] = dbeta_all + dgc_all

        def group(gj, carry):
            first = n_chunk - 1 - BWD_GROUP * gj
            ids = [first - j for j in range(BWD_GROUP)]
            rows = [pl.ds(pl.multiple_of(ci * CHUNK, CHUNK), CHUNK) for ci in ids]
            loc = _units_local(sum((_chunk_units(q_ref, k_ref, v_ref, bg_ref, r) for r in rows), []), masks,
                               xms=[xm_ref[ci, h] for ci in ids for h in heads])
            chunks = [open_chunk(ci, loc[HEADS * j:HEADS * (j + 1)]) for j, ci in enumerate(ids)]
            for c in chunks:
                a_free(c)
            ds_cur = [ds_ref[h] for h in heads]
            later = (c_solve, d_solve, e_solve, f_close)
            for t in range(2 * (BWD_GROUP - 1) + 2 + len(later)):
                for j, c in enumerate(chunks):
                    stage = t - 2 * j
                    if stage == 0:
                        a_state(c, ds_cur)
                    elif stage == 1:
                        ds_cur = b_state(c)
                    elif 2 <= stage < 2 + len(later):
                        later[stage - 2](c)
            for h in heads:
                ds_ref[h] = ds_cur[h]
            return carry

        lax.fori_loop(0, n_chunk // BWD_GROUP, group, 0)

    tok = lambda w: pl.BlockSpec((tb, w), lambda i: (nb - 1 - i, 0))
    return pl.pallas_call(
        body, name="delta_bwd", grid=(nb,),
        in_specs=[tok(DN_WIDTH), tok(DN_WIDTH), tok(DN_WIDTH), tok(LANES),
                  pl.BlockSpec((n_chunk, HEADS, HEAD_DIM, HEAD_DIM), lambda i: (nb - 1 - i, 0, 0, 0)),
                  pl.BlockSpec((n_chunk, HEADS, CHUNK, CHUNK), lambda i: (nb - 1 - i, 0, 0, 0)), tok(DN_WIDTH)],
        out_specs=[tok(DN_WIDTH), tok(DN_WIDTH), tok(DN_WIDTH), tok(LANES)],
        out_shape=[jax.ShapeDtypeStruct((T, DN_WIDTH), F32)] * 3 + [jax.ShapeDtypeStruct((T, LANES), F32)],
        scratch_shapes=[pltpu.VMEM((HEADS, HEAD_DIM, HEAD_DIM), F32)],
        compiler_params=_params(("arbitrary",)),
    )(q, k, v, bg, states, xms, do)


def _dn_prep_back(dq, dk, dv, dbg, c, bd, al_row, dt_row, tb):
    sg = _sigmoid(c)
    a = c * sg
    dsilu = sg * (1.0 + c * (1.0 - sg))
    pieces = [None] * (2 * HEADS)
    for hd in range(HEADS):
        sl = slice(HEAD_DIM * hd, HEAD_DIM * (hd + 1))
        for which, (base, grad, scale) in enumerate(((0, dq, Q_SCALE), (DN_WIDTH, dk, 1.0))):
            sa = slice(base + HEAD_DIM * hd, base + HEAD_DIM * (hd + 1))
            raw = a[:, sa]
            r = lax.rsqrt(jnp.sum(raw * raw, axis=-1, keepdims=True) + EPS)
            nrm = raw * r
            gn_ = grad[:, sl] * scale
            pieces[which * HEADS + hd] = r * (gn_ - nrm * jnp.sum(gn_ * nrm, axis=-1, keepdims=True)) * dsilu[:, sa]
    dc = jnp.concatenate(pieces + [dv * dsilu[:, 2 * DN_WIDTH:]], axis=1)
    lane = lax.broadcasted_iota(jnp.int32, bd.shape, 1)
    is_b = lane < HEADS
    is_g = jnp.logical_and(lane >= HEADS, lane < 2 * HEADS)
    dbgv = jnp.where(is_b, dbg, _mm32(_chunk_cumsum_matrix(tb), dbg, TN))
    beta = _sigmoid(bd)
    neg_a = -jnp.exp(al_row)
    pre_sp = bd + dt_row
    g = neg_a * _softplus(pre_sp)
    da_in = dbgv * neg_a * _sigmoid(pre_sp)
    dbd = jnp.where(is_b, dbgv * beta * (1.0 - beta), jnp.where(is_g, da_in, 0.0)).astype(BF16)
    dal_row = jnp.sum(jnp.where(is_g, dbgv * g, 0.0), axis=0, keepdims=True)
    ddt_row = jnp.sum(jnp.where(is_g, da_in, 0.0), axis=0, keepdims=True)
    return dc, dbd, dal_row, ddt_row


def _dp_of_chip(dqkv, dz, dbd, dsc, s):
    lo, hi = IN_SHARD * s, IN_SHARD * (s + 1)
    pieces = []
    for w_at, w_end, block in ((0, W_Z, dqkv), (W_Z, W_BD, dz), (W_BD, W_SC, dbd), (W_SC, W_IN_COLS, dsc)):
        a, b = max(lo, w_at), min(hi, w_end)
        if a < b:
            pieces.append(block[:, a - w_at:b - w_at])
    pieces.append(jnp.zeros((dqkv.shape[0], D_MODEL - IN_SHARD), dqkv.dtype))
    return jnp.concatenate(pieces, axis=1)


def _in_proj_bwd(dq, dk, dv, dbg, qkv, c, bd, al_row, dt_row, dcv, dgb, sc_in, dz, cw, scw, dx1, x, g1, land_a):
    T = x.shape[0]
    tb = 256
    nb = T // tb

    def body(dq_ref, dk_ref, dv_ref, dbg_ref, pre_ref, c_ref, bd_ref, al_ref, dt_ref,
             dcv_ref, dcv_halo_ref, dgb_ref, sc_ref, dz_ref, cw_ref, scw_ref, dx1_ref, x_ref, g_ref, w_ref,
             dx_ref, dxb_ref, dps_ref, dg_ref, dcw_ref, dal_ref, ddt_ref, head_ref):
        @pl.when(pl.program_id(0) == 0)
        def _():
            for ref in (dg_ref, dcw_ref, dal_ref, ddt_ref, head_ref):
                ref[...] = jnp.zeros_like(ref)

        block = nb - 1 - pl.program_id(0)
        last = block == nb - 1
        dc, dbd, dal_row, ddt_row = _dn_prep_back(
            dq_ref[...], dk_ref[...], dv_ref[...], dbg_ref[...], c_ref[...], bd_ref[...], al_ref[...], dt_ref[...], tb)
        dal_ref[0:1, :] += dal_row
        ddt_ref[0:1, :] += ddt_row
        xc = jnp.concatenate([dc, head_ref[...]], axis=0)
        head_ref[...] = dc[0:8, :]
        w4 = cw_ref[...]
        pre = pre_ref[...]
        dqkv = None
        for j in range(4):
            later = xc[0:tb, :] if j == 3 else _rows_from(xc, 3 - j, tb)
            dqkv = w4[j:j + 1, :] * later if dqkv is None else dqkv + w4[j:j + 1, :] * later
            dcw_ref[j:j + 1, :] += jnp.sum(later * pre, axis=0, keepdims=True)
        yc = jnp.concatenate([dcv_ref[...], jnp.where(last, 0.0, dcv_halo_ref[...])], axis=0)
        w3 = scw_ref[...]
        du = w3[2:3, :] * yc[0:tb, :] + w3[1:2, :] * _rows_from(yc, 1, tb) + w3[0:1, :] * _rows_from(yc, 2, tb)
        sc = sc_ref[...]
        dsc = jnp.concatenate([dgb_ref[...], du * sc[:, 2 * SC_WIDTH:], du * sc[:, SC_WIDTH:2 * SC_WIDTH]], axis=1)
        blocks = (dqkv.astype(BF16), dz_ref[...], dbd, dsc.astype(BF16))
        dh = jnp.zeros((tb, D_MODEL), F32)
        for s in range(N_CHIPS):
            dps = _dp_of_chip(*blocks, s)
            dps_ref[:, D_MODEL * s:D_MODEL * (s + 1)] = dps
            dh = dh + lax.dot_general(dps, w_ref[s], NT, preferred_element_type=F32)
        xv = x_ref[...]
        r = lax.rsqrt(jnp.mean(xv * xv, axis=-1, keepdims=True) + EPS)
        xh = xv * r
        _row_acc(dg_ref, dh * xh)
        dx = dx1_ref[...] + _rms_bwd(dh, xh, r, g_ref[...])
        dx_ref[...] = dx
        dxb_ref[...] = dx.astype(BF16)

    tok = lambda w: pl.BlockSpec((tb, w), lambda i: (nb - 1 - i, 0))
    full = lambda t: pl.BlockSpec(t.shape, lambda i: (0, 0))
    acc = lambda w: pl.BlockSpec((8, w), lambda i: (0, 0))
    after = lambda w: pl.BlockSpec((8, w), lambda i: _after_halo(tb, T)(nb - 1 - i))
    return pl.pallas_call(
        body, name="in_proj_bwd", grid=(nb,),
        in_specs=[tok(DN_WIDTH), tok(DN_WIDTH), tok(DN_WIDTH), tok(LANES), tok(QKV), tok(QKV), tok(LANES),
                  full(al_row), full(dt_row), tok(SC_WIDTH), after(SC_WIDTH), tok(SC_WIDTH), tok(3 * SC_WIDTH),
                  tok(DN_WIDTH), full(cw), full(scw), tok(D_MODEL), tok(D_MODEL), full(g1), _shard_rows(land_a, 0, D_MODEL)],
        out_specs=[tok(D_MODEL), tok(D_MODEL), tok(N_CHIPS * D_MODEL), acc(D_MODEL), acc(QKV), acc(LANES), acc(LANES)],
        out_shape=[jax.ShapeDtypeStruct((T, D_MODEL), F32), jax.ShapeDtypeStruct((T, D_MODEL), BF16),
                   jax.ShapeDtypeStruct((T, N_CHIPS * D_MODEL), BF16), jax.ShapeDtypeStruct((8, D_MODEL), F32),
                   jax.ShapeDtypeStruct((8, QKV), F32), jax.ShapeDtypeStruct((8, LANES), F32),
                   jax.ShapeDtypeStruct((8, LANES), F32)],
        scratch_shapes=[pltpu.VMEM((8, QKV), F32)],
        compiler_params=_params(("arbitrary",)),
    )(dq, dk, dv, dbg, qkv, c, bd, al_row, dt_row, dcv, dcv, dgb, sc_in, dz, cw, scw, dx1, x, g1, land_a)


def _wgrad_in_share(h, dps, parts, name):
    T = h.shape[0]
    bk = min(T, WGRAD_TOKENS)
    n_k = T // bk

    def body(a_ref, b_ref, parts_ref, o_ref, acc_ref):
        kk = pl.program_id(1)

        @pl.when(kk == 0)
        def _():
            acc_ref[...] = jnp.zeros_like(acc_ref)

        acc_ref[...] += lax.dot_general(a_ref[...], b_ref[...], TN, preferred_element_type=F32)

        @pl.when(kk == n_k - 1)
        def _():
            o_ref[0] = acc_ref[...].astype(BF16)

    return pl.pallas_call(
        body, name=name, grid=(N_CHIPS, n_k),
        in_specs=[pl.BlockSpec((bk, D_MODEL), lambda j, kk: (kk, 0)), pl.BlockSpec((bk, D_MODEL), lambda j, kk: (kk, j)), _ANY],
        out_specs=pl.BlockSpec((1, D_MODEL, D_MODEL), lambda j, kk: (j, 0, 0)),
        out_shape=jax.ShapeDtypeStruct(parts.shape, BF16),
        scratch_shapes=[pltpu.VMEM((D_MODEL, D_MODEL), F32)],
        input_output_aliases={2: 0},
        compiler_params=_params(("parallel", "arbitrary")),
    )(h, dps, parts)


def _pad_rows(a, rows=8):
    return jnp.pad(a, ((0, rows - a.shape[0]), (0, 0)))


def _gate_rows(a_log, dt_bias):
    put = lambda t: jnp.pad(t.reshape(1, HEADS), ((0, 0), (HEADS, LANES - 2 * HEADS)))
    return put(a_log), put(dt_bias)


def _mixer_fwd(x, p):
    qkv, z, sc_in, bd, h, q, k, v, bg, c = _in_proj(x, p["g1"], p["land_a"], p["cw"], p["al"], p["dt"])
    o, states, xms = _delta_fwd(q, k, v, bg)
    return dict(x=x, qkv=qkv, c=c, z=z, sc_in=sc_in, bd=bd, h=h, q=q, k=k, v=v, bg=bg, o=o, states=states, xms=xms)


def _tail_fwd(s, p, land_b):
    x1, mix, x2, a, b, h2 = _mix_ffn(s["o"], s["z"], s["sc_in"], s["x"], p["land_a"], p["gn"], p["scw"], p["gs"],
                                     p["g2"], land_b)
    return x2, dict(s, mix=mix), dict(x1=x1, a=a, b=b, h2=h2)


def _ffn_back(dx2, dx2_bf16, s, p, land_b):
    dx1, da, db, act, dg2 = _ffn_bwd(dx2, s["x1"], s["a"], s["b"], p["g2"], land_b)
    parts = lax.empty((N_CHIPS, B_ROWS, D_MODEL), BF16)
    parts = _wgrad_share(act, dx2_bf16, parts, 2 * FF_SHARD, "wgrad_down")
    parts = _wgrad_share(da, s["h2"], parts, 0, "wgrad_gate")
    parts = _wgrad_share(db, s["h2"], parts, FF_SHARD, "wgrad_up")
    return dx1, parts, dg2[0]


def _mixer_bwd(dx1, s, p):
    do, dz, dgb, dcv, dx1_bf16, dgn, dgs, dscw = _mix_out_bwd(dx1, s["o"], s["z"], s["sc_in"], p["land_a"], p["gn"],
                                                              p["scw"], p["gs"])
    dq, dk, dv, dbg = _delta_bwd(s["q"], s["k"], s["v"], s["bg"], s["states"], s["xms"], do)
    dx, dx_bf16, dps, dg1, dcw, dal, ddt = _in_proj_bwd(
        dq, dk, dv, dbg, s["qkv"], s["c"], s["bd"], p["al"], p["dt"], dcv, dgb, s["sc_in"], dz, p["cw"], p["scw"], dx1,
        s["x"], p["g1"], p["land_a"])
    parts = lax.empty((N_CHIPS, A_ROWS, D_MODEL), BF16)
    parts = _wgrad_in_share(s["h"], dps, parts, "wgrad_in")
    parts = _wgrad_share(s["mix"], dx1_bf16, parts, A_OUT_AT, "wgrad_out")
    g = dict(g1=dg1[0], gn=dgn[0], gs=dgs[0], scw=dscw[:3], cw=dcw[:4], al=dal[0, HEADS:2 * HEADS], dt=ddt[0, HEADS:2 * HEADS])
    return dx, dx_bf16, parts, g


def _place():
    return lax.axis_index("x"), lax.axis_index("y"), lax.axis_index("c")


def _other_chips(x, y):
    return [(1 - x, y), (x, 1 - y), (1 - x, 1 - y)]


_HBM = pl.BlockSpec(memory_space=pltpu.HBM)


def _gather_chips(arrs, name):
    n = len(arrs)

    def body(*refs):
        ins, outs = refs[:n], refs[n:2 * n]
        send_sems, recv_sems, local_sems = refs[2 * n:]
        x, y, c = _place()
        me = 2 * x + y
        others = _other_chips(x, y)

        def remote(k, j, landing):
            px, py = others[j]
            return pltpu.make_async_remote_copy(src_ref=ins[k], dst_ref=outs[k].at[landing], send_sem=send_sems.at[k, j],
                                                recv_sem=recv_sems.at[k, j], device_id=(px, py, c), device_id_type=MESH)

        local = [pltpu.make_async_copy(ins[k], outs[k].at[me], local_sems.at[k]) for k in range(n)]
        sends = [remote(k, j, me) for k in range(n) for j in range(3)]
        for cp in local + sends:
            cp.start()
        for k in range(n):
            for j, (px, py) in enumerate(others):
                remote(k, j, 2 * px + py).wait_recv()
        for cp in sends:
            cp.wait_send()
        for cp in local:
            cp.wait()

    shapes = [jax.ShapeDtypeStruct((N_CHIPS,) + a.shape, a.dtype) for a in arrs]
    return pl.pallas_call(
        body, name=name, in_specs=[_HBM] * n, out_specs=[_HBM] * n, out_shape=shapes,
        scratch_shapes=[pltpu.SemaphoreType.DMA((n, 3)), pltpu.SemaphoreType.DMA((n, 3)), pltpu.SemaphoreType.DMA((n,))],
    )(*arrs)


_SEM = pl.BlockSpec(memory_space=pltpu.SEMAPHORE)
_ANY = pl.BlockSpec(memory_space=pl.ANY)
_EFFECT = pltpu.SideEffectType.DATAFLOW_SIDE_EFFECTING


_FLIPS = [(a, b, cc) for a in (0, 1) for b in (0, 1) for cc in (0, 1)][1:]


def _split_copies(src_ref, land_ref, send_sems, recv_sems, gather, sending):
    x, y, c = _place()
    copies = []
    if gather:
        me = 2 * x + y
        for j, (px, py) in enumerate(_other_chips(x, y)):
            copies.append(pltpu.make_async_remote_copy(
                src_ref=src_ref, dst_ref=land_ref.at[me if sending else 2 * px + py],
                send_sem=send_sems.at[j], recv_sem=recv_sems.at[j], device_id=(px, py, c), device_id_type=MESH))
        return copies
    me = 4 * x + 2 * y + c
    for j, (a, b, cc) in enumerate(_FLIPS):
        px, py, pc = (1 - x) if a else x, (1 - y) if b else y, (1 - c) if cc else c
        copies.append(pltpu.make_async_remote_copy(
            src_ref=src_ref.at[2 * px + py], dst_ref=land_ref.at[me if sending else 4 * px + 2 * py + pc],
            send_sem=send_sems.at[j], recv_sem=recv_sems.at[j], device_id=(px, py, pc), device_id_type=MESH))
    return copies


def _own_slot(share):
    chip = 2 * lax.axis_index("x") + lax.axis_index("y")
    return lax.dynamic_update_slice(lax.empty((N_CHIPS,) + share.shape, share.dtype), share[None], (chip, 0, 0))


def _own_part(parts):
    chip = 2 * lax.axis_index("x") + lax.axis_index("y")
    own = lax.dynamic_index_in_dim(parts, chip, 0, keepdims=True)
    return lax.dynamic_update_slice(lax.empty((N_DEV,) + parts.shape[1:], parts.dtype), own,
                                    (2 * chip + lax.axis_index("c"), 0, 0))


def _exchange_start(src, land, after, name, gather):
    def body(src_ref, land_ref, after_ref, send_sems, recv_sems, src_thru, land_thru, token):
        for cp in _split_copies(src_ref, land_ref, send_sems, recv_sems, gather, sending=True):
            cp.start()
        token[...] = jnp.zeros_like(token)

    hbm = lambda t: pltpu.with_memory_space_constraint(t, pltpu.HBM)
    n_copies = N_CHIPS - 1 if gather else N_DEV - 1
    return pl.pallas_call(
        body, name=name,
        out_shape=(pltpu.SemaphoreType.DMA((n_copies,)), pltpu.SemaphoreType.DMA((n_copies,)), pltpu.HBM(src.shape, src.dtype),
                   pltpu.HBM(land.shape, land.dtype), jax.ShapeDtypeStruct((8, LANES), F32)),
        in_specs=(_HBM, _HBM, _ANY), out_specs=(_SEM, _SEM, _HBM, _HBM, pl.BlockSpec(memory_space=pltpu.VMEM)),
        input_output_aliases={0: 2, 1: 3},
        compiler_params=pltpu.CompilerParams(has_side_effects=_EFFECT),
    )(hbm(src), hbm(land), after)


def _exchange_wait(started, after, name, gather):
    send_sems, recv_sems, src_thru, land_thru, _ = started

    def body(src_ref, land_ref, send_sems, recv_sems, after_ref, src_dead, got_ref):
        for cp in _split_copies(src_ref, land_ref, send_sems, recv_sems, gather, sending=False):
            cp.wait_send()
            cp.wait_recv()

    return pl.pallas_call(
        body, name=name,
        out_shape=(pltpu.HBM(src_thru.shape, src_thru.dtype), pltpu.HBM(land_thru.shape, land_thru.dtype)),
        in_specs=(_HBM, _HBM, _SEM, _SEM, _ANY), out_specs=(_HBM, _HBM), input_output_aliases={0: 0, 1: 1},
        compiler_params=pltpu.CompilerParams(has_side_effects=_EFFECT),
    )(src_thru, land_thru, send_sems, recv_sems, after)[1]


def _all_reduce_small(v):
    rows = v.shape[0]
    flips = [(a, b, cc) for a in (0, 1) for b in (0, 1) for cc in (0, 1)][1:]

    def body(v_ref, out_ref, buf_ref, send_sems, recv_sems):
        x, y, c = _place()
        me = 4 * x + 2 * y + c
        peers = [((1 - x) if a else x, (1 - y) if b else y, (1 - c) if cc else c) for a, b, cc in flips]

        def copy(j, landing):
            return pltpu.make_async_remote_copy(src_ref=v_ref, dst_ref=buf_ref.at[landing], send_sem=send_sems.at[j],
                                                recv_sem=recv_sems.at[j], device_id=peers[j], device_id_type=MESH)

        sends = [copy(j, me) for j in range(N_DEV - 1)]
        for cp in sends:
            cp.start()
        buf_ref[me] = v_ref[...]
        for j, (px, py, pc) in enumerate(peers):
            copy(j, 4 * px + 2 * py + pc).wait_recv()
        for cp in sends:
            cp.wait_send()
        acc = buf_ref[0]
        for d in range(1, N_DEV):
            acc = acc + buf_ref[d]
        out_ref[...] = acc

    vmem = pl.BlockSpec(memory_space=pltpu.VMEM)
    return pl.pallas_call(
        body, name="all_reduce_small", in_specs=[vmem], out_specs=vmem,
        out_shape=jax.ShapeDtypeStruct(v.shape, F32),
        scratch_shapes=[pltpu.VMEM((N_DEV, rows, LANES), F32), pltpu.SemaphoreType.DMA((N_DEV - 1,)),
                        pltpu.SemaphoreType.DMA((N_DEV - 1,))],
    )(v)


def _row_block(*sizes):
    return next(t for t in (256, 176, 128, 64) if all(s % t == 0 for s in sizes))


def _adam_update(w, m, v, g):
    r1 = 1.0 / (1.0 - ADAM_B1 ** ADAM_STEP)
    r2 = 1.0 / (1.0 - ADAM_B2 ** ADAM_STEP)
    m_new = ADAM_B1 * m + (1.0 - ADAM_B1) * g
    v_new = ADAM_B2 * v + (1.0 - ADAM_B2) * (g * g)
    return -ADAM_LR * ((m_new * r1) / (jnp.sqrt(v_new * r2) + ADAM_EPS) + ADAM_WD * w), m_new, v_new


def _adamw_rows(w, m, v, got, first, name):
    n_layers, rows, cols = w.shape
    tr = _row_block(rows, first)

    def body(*refs):
        w_ref, m_ref, v_ref = refs[:3]
        g_out, d_out, m_out, v_out = refs[3 + n_layers:]
        for k in range(n_layers):
            @pl.when(pl.program_id(0) == k)
            def _(p_ref=refs[3 + k]):
                g = p_ref[0].astype(F32)
                for d in range(1, N_DEV):
                    g = g + p_ref[d].astype(F32)
                g = g[:, :cols]
                d_out[0], m_out[0], v_out[0] = _adam_update(w_ref[0], m_ref[0], v_ref[0], g)
                g_out[0] = g

    blk = pl.BlockSpec((1, tr, cols), lambda l, i: (l, i, 0))
    parts = [pl.BlockSpec((N_DEV, tr, got[0].shape[2]), lambda l, i, k=k: (0, jnp.where(l == k, first // tr + i, 0), 0))
             for k in range(n_layers)]
    return pl.pallas_call(
        body, name=name, grid=(n_layers, rows // tr),
        in_specs=[blk] * 3 + parts, out_specs=[blk] * 4,
        out_shape=[jax.ShapeDtypeStruct(w.shape, F32)] * 4,
        compiler_params=_params(("arbitrary", "arbitrary")),
    )(w, m, v, *got)


def _adamw(w, m, v, g_parts, name):
    rows, cols = w.shape
    tr = min(rows, 256)
    n = len(g_parts)

    def body(*refs):
        w_ref, m_ref, v_ref = refs[:3]
        g_refs = refs[3:3 + n]
        g_out, d_out, m_out, v_out = refs[3 + n:]
        g = g_refs[0][...]
        for r in g_refs[1:]:
            g = g + r[...]
        d_out[...], m_out[...], v_out[...] = _adam_update(w_ref[...], m_ref[...], v_ref[...], g)
        g_out[...] = g

    blk = pl.BlockSpec((tr, cols), lambda i: (i, 0))
    return pl.pallas_call(
        body, name=name, grid=(rows // tr,),
        in_specs=[blk] * (3 + n), out_specs=[blk] * 4,
        out_shape=[jax.ShapeDtypeStruct((rows, cols), F32)] * 4,
        compiler_params=_params(("parallel",)),
    )(w, m, v, *g_parts)


def _pack(parts, rows, fill=0.0):
    flat = jnp.concatenate([p.reshape(-1) for p in parts])
    return jnp.pad(flat, (0, rows * LANES - flat.shape[0]), constant_values=fill).reshape(rows, LANES)


def _unpack(packed, shapes):
    flat = packed.reshape(-1)
    out, at = [], 0
    for shp in shapes:
        size = 1
        for s in shp:
            size *= s
        out.append(flat[at:at + size].reshape(shp))
        at += size
    return out


def _packed_rows(shapes):
    total = 0
    for shp in shapes:
        size = 1
        for s in shp:
            size *= s
        total += size
    return -(-total // (8 * LANES)) * 8


def _cols_full(g, l):
    t = g[:, l]
    return jnp.moveaxis(t, 0, 1).reshape(t.shape[1], N_CHIPS * t.shape[2])


def _pad_cols(t):
    return jnp.pad(t, ((0, 0),) * (t.ndim - 1) + ((0, D_MODEL - t.shape[-1]),))


def kernel(x, norm1_g, w_in, dn_conv_w, dn_a_log, dn_dt_bias, dn_norm_g, sc_conv_w, sc_norm_g, w_out, norm2_g, ffn_w_gate, ffn_w_up, ffn_w_down, final_norm_g, loss_target, m_norm1_g, m_w_in, m_dn_conv_w, m_dn_a_log, m_dn_dt_bias, m_dn_norm_g, m_sc_conv_w, m_sc_norm_g, m_w_out, m_norm2_g, m_ffn_w_gate, m_ffn_w_up, m_ffn_w_down, m_final_norm_g, v_norm1_g, v_w_in, v_dn_conv_w, v_dn_a_log, v_dn_dt_bias, v_dn_norm_g, v_sc_conv_w, v_sc_norm_g, v_w_out, v_norm2_g, v_ffn_w_gate, v_ffn_w_up, v_ffn_w_down, v_final_norm_g):
    chip = 2 * lax.axis_index("x") + lax.axis_index("y")

    g_cw, g_scw = _gather_chips([dn_conv_w, sc_conv_w], "gather_conv")

    t_last = lambda t: jnp.swapaxes(t, -1, -2)
    gate_t, up_t = t_last(ffn_w_gate), t_last(ffn_w_up)
    zero_token = jnp.zeros((8, LANES), F32)

    def shares(l, tie):
        share_a = jnp.concatenate([_pad_cols(w_in[l] + tie), w_out[l]], axis=0).astype(BF16)
        share_b = jnp.concatenate([gate_t[l] + tie, up_t[l], ffn_w_down[l]], axis=0).astype(BF16)
        return share_a, _own_slot(share_a), share_b, _own_slot(share_b)

    def gather_start(l, packed, after):
        a = _exchange_start(packed[0], packed[1], after, "gather_a_start_%d" % l, gather=True)
        b = _exchange_start(packed[2], packed[3], a[4], "gather_b_start_%d" % l, gather=True)
        return a, b

    ga, gb = gather_start(0, shares(0, 0.0), g_cw)
    packed = [None] + [shares(l, gb[4][0, 0]) for l in range(1, DEPTH)]
    packed_all = sum(t[0, 0].astype(F32) for p in packed[1:] for t in (p[0], p[2]))
    land_a = _exchange_wait(ga, zero_token + packed_all, "gather_a_wait_0", gather=True)
    act = x[0]
    layers, saved_m, saved_f, lands_b = [], [], [], []
    for l in range(DEPTH):
        hold = 0.0
        if l + 1 < DEPTH:
            ga, gb_next = gather_start(l + 1, packed[l + 1], land_a)
            hold = gb_next[4][0:1, 0:1]
        al, dt = _gate_rows(dn_a_log[l], dn_dt_bias[l])
        layers.append(dict(
            g1=norm1_g[l][None] + hold, cw=_pad_rows(_cols_full(g_cw, l)), al=al, dt=dt,
            gn=dn_norm_g[l][None], scw=_pad_rows(_cols_full(g_scw, l)), gs=sc_norm_g[l][None],
            land_a=land_a, g2=norm2_g[l][None]))
        s = _mixer_fwd(act, layers[l])
        lands_b.append(_exchange_wait(gb, s["o"], "gather_b_wait_%d" % l, gather=True))
        act, s, sf = _tail_fwd(s, layers[l], lands_b[l])
        saved_m.append(s)
        saved_f.append(sf)
        if l + 1 < DEPTH:
            land_a = _exchange_wait(ga, act, "gather_a_wait_%d" % (l + 1), gather=True)
            gb = gb_next

    dact, dact_bf16, loss_part, d_final = _loss_head(act, final_norm_g[None], loss_target[0])
    grads, reduce_a, reduce_b = [None] * DEPTH, [None] * DEPTH, [None] * DEPTH
    hold = 0.0
    for l in reversed(range(DEPTH)):
        p = layers[l]
        dx1, parts, dg2 = _ffn_back(dact, dact_bf16, saved_f[l], dict(p, g2=p["g2"] + hold), lands_b[l])
        reduce_b[l] = _exchange_start(parts, _own_part(parts), zero_token, "reduce_b_start_%d" % l, gather=False)
        dact, dact_bf16, parts, gm = _mixer_bwd(dx1, saved_m[l], dict(p, gn=p["gn"] + reduce_b[l][4][0:1, 0:1]))
        reduce_a[l] = _exchange_start(parts, _own_part(parts), zero_token, "reduce_a_start_%d" % l, gather=False)
        hold = reduce_a[l][4][0:1, 0:1]
        grads[l] = dict(gm, g2=dg2)
    loss = lax.psum(loss_part[0, 0], ("x", "y", "c"))
    stack = lambda key: jnp.stack([grads[l][key] for l in range(DEPTH)])

    got_b = [_exchange_wait(reduce_b[l], reduce_a[0][4], "reduce_b_wait_%d" % l, gather=False)
             for l in reversed(range(DEPTH))][::-1]
    big = dict(
        ffn_w_gate=[t_last(o) for o in _adamw_rows(gate_t, t_last(m_ffn_w_gate), t_last(v_ffn_w_gate), got_b, 0, "adamw_gate")],
        ffn_w_up=[t_last(o) for o in _adamw_rows(up_t, t_last(m_ffn_w_up), t_last(v_ffn_w_up), got_b, FF_SHARD, "adamw_up")],
        ffn_w_down=_adamw_rows(ffn_w_down, m_ffn_w_down, v_ffn_w_down, got_b, 2 * FF_SHARD, "adamw_down"))
    after_b = zero_token + sum(big[n][1][0, 0, 0] for n in ("ffn_w_gate", "ffn_w_up", "ffn_w_down"))
    got_a = [_exchange_wait(reduce_a[l], after_b, "reduce_a_wait_%d" % l, gather=False) for l in reversed(range(DEPTH))][::-1]
    big.update(
        w_in=_adamw_rows(w_in, m_w_in, v_w_in, got_a, 0, "adamw_w_in"),
        w_out=_adamw_rows(w_out, m_w_out, v_w_out, got_a, A_OUT_AT, "adamw_w_out"))

    full_shapes = [(DEPTH, D_MODEL), (DEPTH, D_MODEL), (DEPTH, HEAD_DIM), (DEPTH, SC_WIDTH), (DEPTH, HEADS),
                   (DEPTH, HEADS), (D_MODEL,), (DEPTH, 4, QKV), (DEPTH, 3, SC_WIDTH)]
    small_keys = ("g1", "g2", "gn", "gs", "al", "dt")
    packed = _pack([stack(k) for k in small_keys] + [d_final[0], stack("cw"), stack("scw")], _packed_rows(full_shapes))
    sg = _unpack(_all_reduce_small(packed), full_shapes)
    sg[7] = lax.dynamic_slice_in_dim(sg[7], chip * (QKV // N_CHIPS), QKV // N_CHIPS, axis=2)
    sg[8] = lax.dynamic_slice_in_dim(sg[8], chip * (SC_WIDTH // N_CHIPS), SC_WIDTH // N_CHIPS, axis=2)
    small_names = ("norm1_g", "norm2_g", "dn_norm_g", "sc_norm_g", "dn_a_log", "dn_dt_bias", "final_norm_g",
                   "dn_conv_w", "sc_conv_w")
    sw = (norm1_g, norm2_g, dn_norm_g, sc_norm_g, dn_a_log, dn_dt_bias, final_norm_g, dn_conv_w, sc_conv_w)
    sm = (m_norm1_g, m_norm2_g, m_dn_norm_g, m_sc_norm_g, m_dn_a_log, m_dn_dt_bias, m_final_norm_g, m_dn_conv_w, m_sc_conv_w)
    sv = (v_norm1_g, v_norm2_g, v_dn_norm_g, v_sc_norm_g, v_dn_a_log, v_dn_dt_bias, v_final_norm_g, v_dn_conv_w, v_sc_conv_w)
    shard_shapes = [t.shape for t in sw]
    rows = _packed_rows(shard_shapes)
    outs = _adamw(_pack(sw, rows), _pack(sm, rows), _pack(sv, rows, fill=1.0), [_pack(sg, rows)], "adamw_small")
    small = {name: [] for name in small_names}
    for o in outs:
        for name, t in zip(small_names, _unpack(o, shard_shapes)):
            small[name].append(t)

    order = ("norm1_g", "w_in", "dn_conv_w", "dn_a_log", "dn_dt_bias", "dn_norm_g", "sc_conv_w", "sc_norm_g", "w_out",
             "norm2_g", "ffn_w_gate", "ffn_w_up", "ffn_w_down", "final_norm_g")
    result = {**big, **small}
    return (loss, dact[None], *[result[n][0] for n in order], *[result[n][1] for n in order],
            *[result[n][2] for n in order], *[result[n][3] for n in order])
```

```python
import jax
import jax.numpy as jnp
from jax import lax
from jax.experimental import pallas as pl
from jax.experimental.pallas import tpu as pltpu

F32 = jnp.float32
BF16 = jnp.bfloat16
MESH = pl.DeviceIdType.MESH

D_MODEL = 1024
DEPTH = 4
HEADS = 4
HEAD_DIM = 128
DN_WIDTH = HEADS * HEAD_DIM
SC_WIDTH = 512
SC_GROUPS = 4
D_FF = 2816
CHUNK = 64
QKV = 3 * DN_WIDTH
W_IN_COLS = 4 * DN_WIDTH + 2 * HEADS + 3 * SC_WIDTH
LANES = 128
EPS = 1e-6
Q_SCALE = HEAD_DIM ** -0.5
N_CHIPS = 4
N_DEV = 8
IN_SHARD = W_IN_COLS // N_CHIPS
OUT_SHARD = D_MODEL // N_CHIPS
FF_SHARD = D_FF // N_CHIPS
A_OUT_AT = D_MODEL
A_ROWS = D_MODEL + OUT_SHARD
B_ROWS = 3 * FF_SHARD

ADAM_LR = 0.001
ADAM_B1 = 0.9
ADAM_B2 = 0.999
ADAM_EPS = 1e-08
ADAM_WD = 0.01
ADAM_STEP = 10

VMEM_LIMIT = 60 * 1024 * 1024

NN = (((1,), (0,)), ((), ()))
NT = (((1,), (1,)), ((), ()))
TN = (((0,), (0,)), ((), ()))


def _mm(a, b, dims=NN):
    return lax.dot_general(a.astype(BF16), b.astype(BF16), dims, preferred_element_type=F32)


def _mm32(a, b, dims=NN):
    return lax.dot_general(a, b, dims, preferred_element_type=F32, precision=lax.Precision.HIGHEST)


def _params(sem, vmem=VMEM_LIMIT):
    return pltpu.CompilerParams(dimension_semantics=sem, vmem_limit_bytes=vmem)


def _sigmoid(x):
    return 0.5 * jnp.tanh(0.5 * x) + 0.5


def _softplus(x):
    return jnp.maximum(x, 0.0) + jnp.log1p(jnp.exp(-jnp.abs(x)))


def _row_acc(acc_ref, val):
    acc_ref[0:1, :] += jnp.sum(val, axis=0, keepdims=True)


def _rms_bwd(dh, xh, r, gain):
    dxh = dh * gain
    return r * (dxh - xh * jnp.mean(dxh * xh, axis=-1, keepdims=True))


def _before_halo(tb):
    return lambda i: (jnp.maximum(i * (tb // 8) - 1, 0), 0)


def _after_halo(tb, n_rows):
    last = n_rows // 8 - 1
    return lambda i: (jnp.minimum((i + 1) * (tb // 8), last), 0)


def _rows_from(xc, offset, tb):
    part = offset % 8
    if part:
        xc = pltpu.roll(xc, xc.shape[0] - part, 0)
    return xc[offset - part:offset - part + tb, :]


def _taps(xc, w, n_taps, tb, first):
    out = w[0:1, :] * _rows_from(xc, first, tb)
    for j in range(1, n_taps):
        out = out + w[j:j + 1, :] * _rows_from(xc, first + j, tb)
    return out


W_Z = QKV
W_BD = W_Z + DN_WIDTH
W_SC = W_BD + 2 * HEADS

def _w_in_cols(shards, lo, hi):
    pieces = []
    for s in range(N_CHIPS):
        a, b = max(lo, IN_SHARD * s), min(hi, IN_SHARD * (s + 1))
        if a < b:
            pieces.append(shards[s][:, a - IN_SHARD * s:b - IN_SHARD * s])
    return pieces[0] if len(pieces) == 1 else jnp.concatenate(pieces, axis=1)


def _in_proj(x, g1, land_a, cw, al_row, dt_row):
    T = x.shape[0]
    tb = 256

    def body(x_ref, g_ref, w_ref, cw_ref, al_ref, dt_ref,
             qkv_ref, z_ref, sc_ref, bd_ref, h_ref, q_ref, k_ref, v_ref, bg_ref, c_ref, tail_ref):
        @pl.when(pl.program_id(0) == 0)
        def _():
            tail_ref[...] = jnp.zeros_like(tail_ref)

        xv = x_ref[...]
        h = (xv * lax.rsqrt(jnp.mean(xv * xv, axis=-1, keepdims=True) + EPS) * g_ref[...]).astype(BF16)
        shards = [jnp.dot(h, w_ref[s], preferred_element_type=F32) for s in range(N_CHIPS)]
        qkv = _w_in_cols(shards, 0, W_Z)
        bd = jnp.concatenate([_w_in_cols(shards, W_BD, W_SC), jnp.zeros((tb, LANES - 2 * HEADS), F32)], axis=1)
        qkv_ref[...] = qkv
        z_ref[...] = _w_in_cols(shards, W_Z, W_BD)
        bd_ref[...] = bd
        sc_ref[...] = _w_in_cols(shards, W_SC, W_IN_COLS)
        h_ref[...] = h
        halo = tail_ref[...]
        tail_ref[...] = qkv[tb - 8:, :]
        _, c, _, a = _dn_act(qkv, halo, cw_ref[...], tb)
        c_ref[...] = c
        for hd in range(HEADS):
            sl = slice(HEAD_DIM * hd, HEAD_DIM * (hd + 1))
            qs = a[:, sl]
            q_ref[:, sl] = qs * (lax.rsqrt(jnp.sum(qs * qs, axis=-1, keepdims=True) + EPS) * Q_SCALE)
            ks = a[:, DN_WIDTH + HEAD_DIM * hd:DN_WIDTH + HEAD_DIM * (hd + 1)]
            k_ref[:, sl] = ks * lax.rsqrt(jnp.sum(ks * ks, axis=-1, keepdims=True) + EPS)
        v_ref[...] = a[:, 2 * DN_WIDTH:]
        gates = _gates(bd, al_ref[...], dt_ref[...])
        lane = lax.broadcasted_iota(jnp.int32, gates.shape, 1)
        bg_ref[...] = jnp.where(lane < HEADS, gates, _mm32(_chunk_cumsum_matrix(tb), gates))

    tok = lambda w: pl.BlockSpec((tb, w), lambda i: (i, 0))
    full = lambda t: pl.BlockSpec(t.shape, lambda i: (0, 0))
    return pl.pallas_call(
        body, name="in_proj", grid=(T // tb,),
        in_specs=[tok(D_MODEL), full(g1), _shard_rows(land_a, 0, D_MODEL), full(cw), full(al_row), full(dt_row)],
        out_specs=[tok(QKV), tok(DN_WIDTH), tok(3 * SC_WIDTH), tok(LANES), tok(D_MODEL),
                   tok(DN_WIDTH), tok(DN_WIDTH), tok(DN_WIDTH), tok(LANES), tok(QKV)],
        out_shape=[jax.ShapeDtypeStruct((T, QKV), F32), jax.ShapeDtypeStruct((T, DN_WIDTH), F32),
                   jax.ShapeDtypeStruct((T, 3 * SC_WIDTH), F32), jax.ShapeDtypeStruct((T, LANES), F32),
                   jax.ShapeDtypeStruct((T, D_MODEL), BF16)]
        + [jax.ShapeDtypeStruct((T, DN_WIDTH), F32)] * 3 + [jax.ShapeDtypeStruct((T, LANES), F32),
                                                              jax.ShapeDtypeStruct((T, QKV), F32)],
        scratch_shapes=[pltpu.VMEM((8, QKV), F32)],
        compiler_params=_params(("arbitrary",)),
    )(x, g1, land_a, cw, al_row, dt_row)


def _dn_act(pre, halo, cw, tb):
    xc = jnp.concatenate([halo, pre], axis=0)
    c = _taps(xc, cw, 4, tb, 5)
    sg = _sigmoid(c)
    return xc, c, sg, c * sg


def _gates(bd, al_row, dt_row):
    lane = lax.broadcasted_iota(jnp.int32, bd.shape, 1)
    beta = _sigmoid(bd)
    g = -jnp.exp(al_row) * _softplus(bd + dt_row)
    return jnp.where(lane < HEADS, beta, jnp.where(lane < 2 * HEADS, g, 0.0))


def _chunk_masks():
    row = lax.broadcasted_iota(jnp.int32, (CHUNK, CHUNK), 0)
    col = lax.broadcasted_iota(jnp.int32, (CHUNK, CHUNK), 1)
    return row >= col, row > col


def _chunk_cumsum_matrix(n):
    row = lax.broadcasted_iota(jnp.int32, (n, n), 0)
    col = lax.broadcasted_iota(jnp.int32, (n, n), 1)
    return jnp.logical_and(row >= col, row // CHUNK == col // CHUNK).astype(F32)


def _chunk_units(q_ref, k_ref, v_ref, bg_ref, rows):
    bgc = bg_ref[rows, :]
    bg_t = bgc.T
    qv, kv, vv = q_ref[rows, :], k_ref[rows, :], v_ref[rows, :]
    units = []
    for h in range(HEADS):
        sl = slice(HEAD_DIM * h, HEAD_DIM * (h + 1))
        units.append((qv[:, sl], kv[:, sl], vv[:, sl], bgc[:, h:h + 1], bgc[:, HEADS + h:HEADS + h + 1],
                      bg_t[HEADS + h:HEADS + h + 1, :]))
    return units


def _units_local(units, masks, xms=None):
    causal, strict = masks
    pre = []
    for q, k, v, beta, gc, gr in units:
        kb = k * beta
        eg = jnp.exp(gc)
        g_last = gc[CHUNK - 1:CHUNK, :]
        ek = jnp.exp(g_last - gc)
        pre.append(dict(q=q, k=k, v=v, beta=beta, decay=jnp.exp(jnp.where(causal, gc - gr, -1e30)), kb=kb, vb=v * beta,
                        eg=eg, kbg=kb * eg, ek=ek, gl=jnp.exp(g_last), q_dec=q * eg, k_dec=k * ek))
    both = [_mm(jnp.concatenate([p["kb"], p["q"]], axis=0), p["k"], NT) for p in pre]
    for p, b in zip(pre, both):
        p["low"] = jnp.where(strict, b[:CHUNK] * p["decay"], 0.0)
        p["qk"] = jnp.where(causal, b[CHUNK:] * p["decay"], 0.0)
    xs = xms
    if xs is None:
        xs = [-p["low"] for p in pre]
        pw = [_mm(p["low"], p["low"]) for p in pre]
        for _ in range(4):
            both = [_mm(jnp.concatenate([pp, x], axis=0), pp) for pp, x in zip(pw, xs)]
            xs = [x + pp + b[CHUNK:] for x, pp, b in zip(xs, pw, both)]
            pw = [b[:CHUNK] for b in both]
        last = [_mm(x, pp) for x, pp in zip(xs, pw)]
        xs = [x + pp + b for x, pp, b in zip(xs, pw, last)]
    uw = [_mm(x, jnp.concatenate([p["vb"], p["kbg"]], axis=1)) for x, p in zip(xs, pre)]
    for p, x, b in zip(pre, xs, uw):
        p["xm"] = x
        p["u"] = p["vb"] + b[:, :HEAD_DIM]
        p["w"] = p["kbg"] + b[:, HEAD_DIM:]
    return pre


FWD_GROUP = 8
BWD_GROUP = 8


def _delta_fwd(q, k, v, bg):
    T = q.shape[0]
    tb = 512
    n_chunk = tb // CHUNK

    def body(q_ref, k_ref, v_ref, bg_ref, o_ref, st_ref, xm_ref, s_ref):
        @pl.when(pl.program_id(0) == 0)
        def _():
            s_ref[...] = jnp.zeros_like(s_ref)

        masks = _chunk_masks()

        def group(gi, carry):
            rows = [pl.ds(pl.multiple_of((FWD_GROUP * gi + j) * CHUNK, CHUNK), CHUNK) for j in range(FWD_GROUP)]
            loc = _units_local(sum((_chunk_units(q_ref, k_ref, v_ref, bg_ref, r) for r in rows), []), masks)
            states = [s_ref[h] for h in range(HEADS)]
            for j in range(FWD_GROUP):
                lj = loc[HEADS * j:HEADS * (j + 1)]
                ws = [_mm(jnp.concatenate([p["w"], p["q_dec"]], axis=0), s) for p, s in zip(lj, states)]
                v_new = [p["u"] - b[:CHUNK] for p, b in zip(lj, ws)]
                intra = [_mm(p["qk"], vn) for p, vn in zip(lj, v_new)]
                upd = [_mm(p["k_dec"], vn, TN) for p, vn in zip(lj, v_new)]
                o_ref[rows[j], :] = jnp.concatenate([b[CHUNK:] + a for b, a in zip(ws, intra)], axis=1)
                for h in range(HEADS):
                    st_ref[FWD_GROUP * gi + j, h] = states[h]
                    xm_ref[FWD_GROUP * gi + j, h] = lj[h]["xm"]
                states = [p["gl"] * s + d for p, s, d in zip(lj, states, upd)]
            for h in range(HEADS):
                s_ref[h] = states[h]
            return carry

        lax.fori_loop(0, n_chunk // FWD_GROUP, group, 0)

    tok = lambda w: pl.BlockSpec((tb, w), lambda i: (i, 0))
    return pl.pallas_call(
        body, name="delta_fwd", grid=(T // tb,),
        in_specs=[tok(DN_WIDTH), tok(DN_WIDTH), tok(DN_WIDTH), tok(LANES)],
        out_specs=[tok(DN_WIDTH), pl.BlockSpec((n_chunk, HEADS, HEAD_DIM, HEAD_DIM), lambda i: (i, 0, 0, 0)),
                   pl.BlockSpec((n_chunk, HEADS, CHUNK, CHUNK), lambda i: (i, 0, 0, 0))],
        out_shape=[jax.ShapeDtypeStruct((T, DN_WIDTH), F32),
                   jax.ShapeDtypeStruct((T // CHUNK, HEADS, HEAD_DIM, HEAD_DIM), F32),
                   jax.ShapeDtypeStruct((T // CHUNK, HEADS, CHUNK, CHUNK), F32)],
        scratch_shapes=[pltpu.VMEM((HEADS, HEAD_DIM, HEAD_DIM), F32)],
        compiler_params=_params(("arbitrary",)),
    )(q, k, v, bg)


def _dn_out(o, z, gn):
    outs, ohs, rs = [], [], []
    for hh in range(HEADS):
        oh = o[:, HEAD_DIM * hh:HEAD_DIM * (hh + 1)]
        r = lax.rsqrt(jnp.mean(oh * oh, axis=-1, keepdims=True) + EPS)
        ohs.append(oh * r)
        rs.append(r)
    sz = _sigmoid(z)
    oh = jnp.concatenate(ohs, axis=1)
    gn4 = jnp.concatenate([gn] * HEADS, axis=1)
    return oh * gn4 * (z * sz), oh, rs, sz, gn4


def _sc_fwd(sc_in, halo, cw, tb):
    xc = jnp.concatenate([halo, sc_in], axis=0)
    u = xc[:, SC_WIDTH:2 * SC_WIDTH] * xc[:, 2 * SC_WIDTH:]
    cv = _taps(u, cw, 3, tb, 6)
    gate_b = sc_in[:, :SC_WIDTH]
    y = gate_b * cv
    gw = SC_WIDTH // SC_GROUPS
    yhs, rs = [], []
    for gi in range(SC_GROUPS):
        yg = y[:, gw * gi:gw * (gi + 1)]
        r = lax.rsqrt(jnp.mean(yg * yg, axis=-1, keepdims=True) + EPS)
        yhs.append(yg * r)
        rs.append(r)
    return u, cv, gate_b, jnp.concatenate(yhs, axis=1), rs


def _shard_rows(land, first, rows, single_buffer=False):
    assert first % rows == 0 and land.shape[0] == N_CHIPS
    mode = dict(pipeline_mode=pl.Buffered(1)) if single_buffer else {}
    return pl.BlockSpec((N_CHIPS, rows, land.shape[2]), lambda i: (0, first // rows, 0), **mode)


def _whole(w_ref):
    n, rows, cols = w_ref.shape
    return w_ref[...].reshape(n * rows, cols)


def _mix_ffn(o, z, sc_in, x, land_a, gn, scw, gs, g2, land_b):
    T = x.shape[0]
    tb = 256

    def body(o_ref, z_ref, sc_ref, halo_ref, x_ref, wo_ref, gn_ref, scw_ref, gs_ref, g2_ref, wgt_ref, wut_ref, wd_ref,
             x1_ref, mix_ref, x2_ref, a_ref, b_ref, h_ref):
        o_n = _dn_out(o_ref[...], z_ref[...], gn_ref[...])[0]
        halo = jnp.where(pl.program_id(0) > 0, halo_ref[...], 0.0)
        yh = _sc_fwd(sc_ref[...], halo, scw_ref[...], tb)[3]
        mix = jnp.concatenate([o_n, yh * gs_ref[...]], axis=1).astype(BF16)
        x1 = x_ref[...] + jnp.dot(mix, _whole(wo_ref), preferred_element_type=F32)
        x1_ref[...] = x1
        mix_ref[...] = mix
        r = lax.rsqrt(jnp.mean(x1 * x1, axis=-1, keepdims=True) + EPS)
        h = (x1 * r * g2_ref[...]).astype(BF16)
        a = lax.dot_general(h, _whole(wgt_ref), NT, preferred_element_type=F32)
        b = lax.dot_general(h, _whole(wut_ref), NT, preferred_element_type=F32)
        act = (a * _sigmoid(a) * b).astype(BF16)
        x2_ref[...] = x1 + jnp.dot(act, _whole(wd_ref), preferred_element_type=F32)
        a_ref[...] = a.astype(BF16)
        b_ref[...] = b.astype(BF16)
        h_ref[...] = h

    tok = lambda w: pl.BlockSpec((tb, w), lambda i: (i, 0))
    full = lambda t: pl.BlockSpec(t.shape, lambda i: (0, 0))
    once = lambda land, first, rows: _shard_rows(land, first, rows, single_buffer=True)
    return pl.pallas_call(
        body, name="mix_ffn", grid=(T // tb,),
        in_specs=[tok(DN_WIDTH), tok(DN_WIDTH), tok(3 * SC_WIDTH), pl.BlockSpec((8, 3 * SC_WIDTH), _before_halo(tb)),
                  tok(D_MODEL), once(land_a, A_OUT_AT, OUT_SHARD), full(gn), full(scw), full(gs), full(g2),
                  once(land_b, 0, FF_SHARD), once(land_b, FF_SHARD, FF_SHARD), once(land_b, 2 * FF_SHARD, FF_SHARD)],
        out_specs=[tok(D_MODEL), tok(D_MODEL), tok(D_MODEL), tok(D_FF), tok(D_FF), tok(D_MODEL)],
        out_shape=[jax.ShapeDtypeStruct((T, D_MODEL), F32), jax.ShapeDtypeStruct((T, D_MODEL), BF16),
                   jax.ShapeDtypeStruct((T, D_MODEL), F32), jax.ShapeDtypeStruct((T, D_FF), BF16),
                   jax.ShapeDtypeStruct((T, D_FF), BF16), jax.ShapeDtypeStruct((T, D_MODEL), BF16)],
        compiler_params=_params(("parallel",)),
    )(o, z, sc_in, sc_in, x, land_a, gn, scw, gs, g2, land_b, land_b, land_b)


def _loss_head(x, gf, target):
    T = x.shape[0]
    tb = 512

    def body(x_ref, g_ref, t_ref, dx_ref, dxb_ref, loss_ref, dg_ref):
        @pl.when(pl.program_id(0) == 0)
        def _():
            loss_ref[...] = jnp.zeros_like(loss_ref)
            dg_ref[...] = jnp.zeros_like(dg_ref)

        xv = x_ref[...]
        r = lax.rsqrt(jnp.mean(xv * xv, axis=-1, keepdims=True) + EPS)
        xh = xv * r
        err = xh * g_ref[...] - t_ref[...]
        per_tok = jnp.mean(err * err, axis=-1, keepdims=True)
        loss_ref[...] += 0.5 * jnp.sum(per_tok, axis=0, keepdims=True)
        dy = err * (1.0 / D_MODEL)
        _row_acc(dg_ref, dy * xh)
        dx = _rms_bwd(dy, xh, r, g_ref[...])
        dx_ref[...] = dx
        dxb_ref[...] = dx.astype(BF16)

    tok = pl.BlockSpec((tb, D_MODEL), lambda i: (i, 0))
    return pl.pallas_call(
        body, name="loss_head", grid=(T // tb,),
        in_specs=[tok, pl.BlockSpec(gf.shape, lambda i: (0, 0)), tok],
        out_specs=[tok, tok, pl.BlockSpec((8, LANES), lambda i: (0, 0)), pl.BlockSpec((8, D_MODEL), lambda i: (0, 0))],
        out_shape=[jax.ShapeDtypeStruct((T, D_MODEL), F32), jax.ShapeDtypeStruct((T, D_MODEL), BF16),
                   jax.ShapeDtypeStruct((8, LANES), F32), jax.ShapeDtypeStruct((8, D_MODEL), F32)],
        compiler_params=_params(("arbitrary",)),
    )(x, gf, target)


def _ffn_mix_bwd(dx2, x1, a, b, g2, land_b, o, z, sc_in, land_a, gn, scw, gs):
    T = x1.shape[0]
    tb = 256

    def body(dx2_ref, x_ref, a_ref, b_ref, g_ref, wgt_ref, wut_ref, wd_ref,
             o_ref, z_ref, sc_ref, halo_ref, w_ref, gn_ref, scw_ref, gs_ref,
             dx1_ref, da_ref, db_ref, act_ref, dg_ref,
             do_ref, dz_ref, dgb_ref, dcv_ref, dxb_ref, dgn_ref, dgs_ref, dscw_ref):
        @pl.when(pl.program_id(0) == 0)
        def _():
            for ref in (dg_ref, dgn_ref, dgs_ref, dscw_ref):
                ref[...] = jnp.zeros_like(ref)

        zv = z_ref[...]
        _, oh, rs, sz, gn4 = _dn_out(o_ref[...], zv, gn_ref[...])
        halo = jnp.where(pl.program_id(0) > 0, halo_ref[...], 0.0)
        u, cv, gate_b, yh, rys = _sc_fwd(sc_ref[...], halo, scw_ref[...], tb)

        dx2v = dx2_ref[...]
        av = a_ref[...].astype(F32)
        bv = b_ref[...].astype(F32)
        dact = _mm(dx2v, _whole(wd_ref), NT)
        sa = _sigmoid(av)
        silu = av * sa
        da = (dact * bv * (sa * (1.0 + av * (1.0 - sa)))).astype(BF16)
        db = (dact * silu).astype(BF16)
        dh = _mm(da, _whole(wgt_ref)) + _mm(db, _whole(wut_ref))
        xv = x_ref[...]
        r = lax.rsqrt(jnp.mean(xv * xv, axis=-1, keepdims=True) + EPS)
        xh = xv * r
        _row_acc(dg_ref, dh * xh)
        dx1 = dx2v + _rms_bwd(dh, xh, r, g_ref[...])
        dx1_ref[...] = dx1
        da_ref[...] = da
        db_ref[...] = db
        act_ref[...] = (silu * bv).astype(BF16)

        dx_bf16 = dx1.astype(BF16)
        dxb_ref[...] = dx_bf16
        dmix = lax.dot_general(dx_bf16, _whole(w_ref), NT, preferred_element_type=F32)
        don = dmix[:, :DN_WIDTH]
        dosc = dmix[:, DN_WIDTH:]
        silu_z = zv * sz
        dgn_full = don * oh * silu_z
        dgn_ref[0:1, :] += jnp.sum(sum(dgn_full[:, HEAD_DIM * hh:HEAD_DIM * (hh + 1)] for hh in range(HEADS)),
                                   axis=0, keepdims=True)
        dz_ref[...] = (don * oh * gn4 * (sz * (1.0 + zv * (1.0 - sz)))).astype(BF16)
        t = don * gn4 * silu_z
        for hh in range(HEADS):
            sl = slice(HEAD_DIM * hh, HEAD_DIM * (hh + 1))
            th, ohh = t[:, sl], oh[:, sl]
            do_ref[:, sl] = rs[hh] * (th - ohh * jnp.mean(th * ohh, axis=-1, keepdims=True))
        _row_acc(dgs_ref, dosc * yh)
        ty = dosc * gs_ref[...]
        gw = SC_WIDTH // SC_GROUPS
        dys = []
        for gi in range(SC_GROUPS):
            sl = slice(gw * gi, gw * (gi + 1))
            tg, yg = ty[:, sl], yh[:, sl]
            dys.append(rys[gi] * (tg - yg * jnp.mean(tg * yg, axis=-1, keepdims=True)))
        dy = jnp.concatenate(dys, axis=1)
        dgb_ref[...] = dy * cv
        dcv = dy * gate_b
        dcv_ref[...] = dcv
        for j in range(3):
            dscw_ref[j:j + 1, :] += jnp.sum(dcv * _rows_from(u, 6 + j, tb), axis=0, keepdims=True)

    tok = lambda w: pl.BlockSpec((tb, w), lambda i: (i, 0))
    full = lambda t: pl.BlockSpec(t.shape, lambda i: (0, 0))
    acc = lambda w: pl.BlockSpec((8, w), lambda i: (0, 0))
    once = lambda land, first, rows: _shard_rows(land, first, rows, single_buffer=True)
    return pl.pallas_call(
        body, name="ffn_mix_bwd", grid=(T // tb,),
        in_specs=[tok(D_MODEL), tok(D_MODEL), tok(D_FF), tok(D_FF), full(g2),
                  once(land_b, 0, FF_SHARD), once(land_b, FF_SHARD, FF_SHARD), once(land_b, 2 * FF_SHARD, FF_SHARD),
                  tok(DN_WIDTH), tok(DN_WIDTH), tok(3 * SC_WIDTH), pl.BlockSpec((8, 3 * SC_WIDTH), _before_halo(tb)),
                  once(land_a, A_OUT_AT, OUT_SHARD), full(gn), full(scw), full(gs)],
        out_specs=[tok(D_MODEL), tok(D_FF), tok(D_FF), tok(D_FF), acc(D_MODEL),
                   tok(DN_WIDTH), tok(DN_WIDTH), tok(SC_WIDTH), tok(SC_WIDTH), tok(D_MODEL),
                   acc(HEAD_DIM), acc(SC_WIDTH), acc(SC_WIDTH)],
        out_shape=[jax.ShapeDtypeStruct((T, D_MODEL), F32)]
        + [jax.ShapeDtypeStruct((T, D_FF), BF16)] * 3 + [jax.ShapeDtypeStruct((8, D_MODEL), F32)]
        + [jax.ShapeDtypeStruct((T, DN_WIDTH), F32), jax.ShapeDtypeStruct((T, DN_WIDTH), BF16),
           jax.ShapeDtypeStruct((T, SC_WIDTH), F32), jax.ShapeDtypeStruct((T, SC_WIDTH), F32),
           jax.ShapeDtypeStruct((T, D_MODEL), BF16),
           jax.ShapeDtypeStruct((8, HEAD_DIM), F32), jax.ShapeDtypeStruct((8, SC_WIDTH), F32),
           jax.ShapeDtypeStruct((8, SC_WIDTH), F32)],
        compiler_params=_params(("arbitrary",)),
    )(dx2, x1, a, b, g2, land_b, land_b, land_b, o, z, sc_in, sc_in, land_a, gn, scw, gs)


WGRAD_TOKENS = 2048


def _wgrad_share(a, b, parts, first, name):
    T = b.shape[0]
    rows = a.shape[1] // N_CHIPS
    assert first % rows == 0 and b.shape[1] == parts.shape[2]
    bk = min(T, WGRAD_TOKENS)
    n_k = T // bk
    group = 2
    assert (group * rows) % LANES == 0

    def body(a_ref, b_ref, parts_ref, o_ref, acc_ref):
        kk = pl.program_id(1)

        @pl.when(kk == 0)
        def _():
            acc_ref[...] = jnp.zeros_like(acc_ref)

        acc_ref[...] += lax.dot_general(a_ref[...], b_ref[...], TN, preferred_element_type=F32)

        @pl.when(kk == n_k - 1)
        def _():
            for s in range(group):
                o_ref[s] = acc_ref[rows * s:rows * (s + 1), :].astype(BF16)

    return pl.pallas_call(
        body, name=name, grid=(N_CHIPS // group, n_k),
        in_specs=[pl.BlockSpec((bk, group * rows), lambda i, kk: (kk, i)),
                  pl.BlockSpec((bk, b.shape[1]), lambda i, kk: (kk, 0)), _ANY],
        out_specs=pl.BlockSpec((group, rows, b.shape[1]), lambda i, kk: (i, first // rows, 0)),
        out_shape=jax.ShapeDtypeStruct(parts.shape, BF16),
        scratch_shapes=[pltpu.VMEM((group * rows, b.shape[1]), F32)],
        input_output_aliases={2: 0},
        compiler_params=_params(("parallel", "arbitrary")),
    )(a, b, parts)


def _delta_bwd(q, k, v, bg, states, xms, do, after):
    T = q.shape[0]
    tb = 512
    n_chunk = tb // CHUNK
    nb = T // tb

    def body(q_ref, k_ref, v_ref, bg_ref, st_ref, xm_ref, do_ref, after_ref, dq_ref, dk_ref, dv_ref, dbg_ref, ds_ref):
        @pl.when(pl.program_id(0) == 0)
        def _():
            ds_ref[...] = jnp.zeros_like(ds_ref)

        masks = _chunk_masks()
        causal, strict = masks
        lane = lax.broadcasted_iota(jnp.int32, (CHUNK, LANES), 1)
        last_row = lax.broadcasted_iota(jnp.int32, (CHUNK, 1), 0) == CHUNK - 1
        cat = jnp.concatenate
        heads = range(HEADS)

        def open_chunk(ci, loc):
            rows = pl.ds(pl.multiple_of(ci * CHUNK, CHUNK), CHUNK)
            dov = do_ref[rows, :]
            return dict(rows=rows, loc=loc, do=[dov[:, HEAD_DIM * h:HEAD_DIM * (h + 1)] for h in heads],
                        state=[st_ref[ci, h] for h in heads])

        def a_free(c):
            loc, do, state = c["loc"], c["do"], c["state"]
            w_s = [_mm(p["w"], s) for p, s in zip(loc, state)]
            c["dq_dec"] = [_mm(d, s, NT) for d, s in zip(do, state)]
            c["qk_do"] = [_mm(p["qk"], d, TN) for p, d in zip(loc, do)]
            c["qd_do"] = [_mm(p["q_dec"], d, TN) for p, d in zip(loc, do)]
            c["v_new"] = [p["u"] - t for p, t in zip(loc, w_s)]
            c["dqk"] = [jnp.where(causal, _mm(d, vn, NT), 0.0) for d, vn in zip(do, c["v_new"])]

        def a_state(c, ds_next):
            c["ds_next"] = ds_next
            kd_ds = [_mm(p["k_dec"], d) for p, d in zip(c["loc"], ds_next)]
            c["dk_dec"] = [_mm(vn, d, NT) for vn, d in zip(c["v_new"], ds_next)]
            c["dv_new"] = [a + b for a, b in zip(c["qk_do"], kd_ds)]

        def b_state(c):
            loc = c["loc"]
            w_dv = [_mm(p["w"], dvn, TN) for p, dvn in zip(loc, c["dv_new"])]
            c["dw"] = [-_mm(dvn, s, NT) for dvn, s in zip(c["dv_new"], c["state"])]
            return [loc[h]["gl"] * c["ds_next"][h] + c["qd_do"][h] - w_dv[h] for h in heads]

        def c_solve(c):
            loc, dv_new, dw = c["loc"], c["dv_new"], c["dw"]
            c["dtm"] = [_mm(cat([dvn, d], axis=1), cat([p["vb"], p["kbg"]], axis=1), NT) for dvn, d, p in zip(dv_new, dw, loc)]
            x_t = [_mm(p["xm"], cat([dvn, d], axis=1), TN) for p, dvn, d in zip(loc, dv_new, dw)]
            c["dvb"] = [dvn + t[:, :HEAD_DIM] for dvn, t in zip(dv_new, x_t)]
            c["dkbg"] = [d + t[:, HEAD_DIM:] for d, t in zip(dw, x_t)]

        def d_solve(c):
            c["y"] = [t + _mm(p["xm"], t, TN) for p, t in zip(c["loc"], c["dtm"])]

        def e_solve(c):
            c["dlow"] = [jnp.where(strict, -(t + _mm(t, p["xm"], NT)), 0.0) for p, t in zip(c["loc"], c["y"])]

        def f_close(c):
            loc, rows = c["loc"], c["rows"]
            dmm = [d * p["decay"] for d, p in zip(c["dlow"], loc)]
            dnn = [d * p["decay"] for d, p in zip(c["dqk"], loc)]
            by_k = [_mm(cat([a, b], axis=0), p["k"]) for a, b, p in zip(dmm, dnn, loc)]
            dk_mm = [_mm(cat([a, b], axis=0), cat([p["kb"], p["q"]], axis=0), TN) for a, b, p in zip(dmm, dnn, loc)]
            dq_out, dk_out, dv_out = [], [], []
            dbeta_all = jnp.zeros((CHUNK, LANES), F32)
            dgc_all = jnp.zeros((CHUNK, LANES), F32)
            for h in heads:
                p = loc[h]
                dkb = by_k[h][:CHUNK] + c["dkbg"][h] * p["eg"]
                dq_out.append(by_k[h][CHUNK:] + c["dq_dec"][h] * p["eg"])
                dk_out.append(dk_mm[h] + c["dk_dec"][h] * p["ek"] + dkb * p["beta"])
                dv_out.append(c["dvb"][h] * p["beta"])
                dbeta = jnp.sum(dkb * p["k"] + c["dvb"][h] * p["v"], axis=1, keepdims=True)
                e = c["dlow"][h] * p["low"] + c["dqk"][h] * p["qk"]
                kd = jnp.sum(c["dk_dec"][h] * p["k_dec"], axis=1, keepdims=True)
                dgc = (jnp.sum(e, axis=1, keepdims=True) - jnp.sum(e.T, axis=1, keepdims=True)
                       + jnp.sum(c["dq_dec"][h] * p["q_dec"], axis=1, keepdims=True) - kd
                       + jnp.sum(c["dkbg"][h] * p["kbg"], axis=1, keepdims=True))
                dgl = jnp.sum(jnp.sum(c["ds_next"][h] * c["state"][h], axis=1, keepdims=True), axis=0, keepdims=True)
                d_last = jnp.sum(kd, axis=0, keepdims=True) + dgl * p["gl"]
                dgc = dgc + jnp.where(last_row, d_last, 0.0)
                dbeta_all = jnp.where(lane == h, dbeta, dbeta_all)
                dgc_all = jnp.where(lane == h + HEADS, dgc, dgc_all)
            dq_ref[rows, :] = cat(dq_out, axis=1)
            dk_ref[rows, :] = cat(dk_out, axis=1)
            dv_ref[rows, :] = cat(dv_out, axis=1)
            dbg_ref[rows, :] = dbeta_all + dgc_all

        def group(gj, carry):
            first = n_chunk - 1 - BWD_GROUP * gj
            ids = [first - j for j in range(BWD_GROUP)]
            rows = [pl.ds(pl.multiple_of(ci * CHUNK, CHUNK), CHUNK) for ci in ids]
            loc = _units_local(sum((_chunk_units(q_ref, k_ref, v_ref, bg_ref, r) for r in rows), []), masks,
                               xms=[xm_ref[ci, h] for ci in ids for h in heads])
            chunks = [open_chunk(ci, loc[HEADS * j:HEADS * (j + 1)]) for j, ci in enumerate(ids)]
            for c in chunks:
                a_free(c)
            ds_cur = [ds_ref[h] for h in heads]
            later = (c_solve, d_solve, e_solve, f_close)
            for t in range(2 * (BWD_GROUP - 1) + 2 + len(later)):
                for j, c in enumerate(chunks):
                    stage = t - 2 * j
                    if stage == 0:
                        a_state(c, ds_cur)
                    elif stage == 1:
                        ds_cur = b_state(c)
                    elif 2 <= stage < 2 + len(later):
                        later[stage - 2](c)
            for h in heads:
                ds_ref[h] = ds_cur[h]
            return carry

        lax.fori_loop(0, n_chunk // BWD_GROUP, group, 0)

    tok = lambda w: pl.BlockSpec((tb, w), lambda i: (nb - 1 - i, 0))
    return pl.pallas_call(
        body, name="delta_bwd", grid=(nb,),
        in_specs=[tok(DN_WIDTH), tok(DN_WIDTH), tok(DN_WIDTH), tok(LANES),
                  pl.BlockSpec((n_chunk, HEADS, HEAD_DIM, HEAD_DIM), lambda i: (nb - 1 - i, 0, 0, 0)),
                  pl.BlockSpec((n_chunk, HEADS, CHUNK, CHUNK), lambda i: (nb - 1 - i, 0, 0, 0)), tok(DN_WIDTH),
                  pl.BlockSpec(memory_space=pltpu.SMEM)],
        out_specs=[tok(DN_WIDTH), tok(DN_WIDTH), tok(DN_WIDTH), tok(LANES)],
        out_shape=[jax.ShapeDtypeStruct((T, DN_WIDTH), F32)] * 3 + [jax.ShapeDtypeStruct((T, LANES), F32)],
        scratch_shapes=[pltpu.VMEM((HEADS, HEAD_DIM, HEAD_DIM), F32)],
        compiler_params=_params(("arbitrary",)),
    )(q, k, v, bg, states, xms, do, after)


def _dn_prep_back(dq, dk, dv, dbg, c, bd, al_row, dt_row, tb):
    sg = _sigmoid(c)
    a = c * sg
    dsilu = sg * (1.0 + c * (1.0 - sg))
    pieces = [None] * (2 * HEADS)
    for hd in range(HEADS):
        sl = slice(HEAD_DIM * hd, HEAD_DIM * (hd + 1))
        for which, (base, grad, scale) in enumerate(((0, dq, Q_SCALE), (DN_WIDTH, dk, 1.0))):
            sa = slice(base + HEAD_DIM * hd, base + HEAD_DIM * (hd + 1))
            raw = a[:, sa]
            r = lax.rsqrt(jnp.sum(raw * raw, axis=-1, keepdims=True) + EPS)
            nrm = raw * r
            gn_ = grad[:, sl] * scale
            pieces[which * HEADS + hd] = r * (gn_ - nrm * jnp.sum(gn_ * nrm, axis=-1, keepdims=True)) * dsilu[:, sa]
    dc = jnp.concatenate(pieces + [dv * dsilu[:, 2 * DN_WIDTH:]], axis=1)
    lane = lax.broadcasted_iota(jnp.int32, bd.shape, 1)
    is_b = lane < HEADS
    is_g = jnp.logical_and(lane >= HEADS, lane < 2 * HEADS)
    dbgv = jnp.where(is_b, dbg, _mm32(_chunk_cumsum_matrix(tb), dbg, TN))
    beta = _sigmoid(bd)
    neg_a = -jnp.exp(al_row)
    pre_sp = bd + dt_row
    g = neg_a * _softplus(pre_sp)
    da_in = dbgv * neg_a * _sigmoid(pre_sp)
    dbd = jnp.where(is_b, dbgv * beta * (1.0 - beta), jnp.where(is_g, da_in, 0.0)).astype(BF16)
    dal_row = jnp.sum(jnp.where(is_g, dbgv * g, 0.0), axis=0, keepdims=True)
    ddt_row = jnp.sum(jnp.where(is_g, da_in, 0.0), axis=0, keepdims=True)
    return dc, dbd, dal_row, ddt_row


def _dp_of_chip(dqkv, dz, dbd, dsc, s):
    lo, hi = IN_SHARD * s, IN_SHARD * (s + 1)
    pieces = []
    for w_at, w_end, block in ((0, W_Z, dqkv), (W_Z, W_BD, dz), (W_BD, W_SC, dbd), (W_SC, W_IN_COLS, dsc)):
        a, b = max(lo, w_at), min(hi, w_end)
        if a < b:
            pieces.append(block[:, a - w_at:b - w_at])
    pieces.append(jnp.zeros((dqkv.shape[0], D_MODEL - IN_SHARD), dqkv.dtype))
    return jnp.concatenate(pieces, axis=1)


def _in_proj_bwd(dq, dk, dv, dbg, qkv, c, bd, al_row, dt_row, dcv, dgb, sc_in, dz, cw, scw, dx1, x, g1, land_a):
    T = x.shape[0]
    tb = 256
    nb = T // tb

    def body(dq_ref, dk_ref, dv_ref, dbg_ref, pre_ref, c_ref, bd_ref, al_ref, dt_ref,
             dcv_ref, dcv_halo_ref, dgb_ref, sc_ref, dz_ref, cw_ref, scw_ref, dx1_ref, x_ref, g_ref, w_ref,
             dx_ref, dxb_ref, dps_ref, dg_ref, dcw_ref, dal_ref, ddt_ref, head_ref):
        @pl.when(pl.program_id(0) == 0)
        def _():
            for ref in (dg_ref, dcw_ref, dal_ref, ddt_ref, head_ref):
                ref[...] = jnp.zeros_like(ref)

        block = nb - 1 - pl.program_id(0)
        last = block == nb - 1
        dc, dbd, dal_row, ddt_row = _dn_prep_back(
            dq_ref[...], dk_ref[...], dv_ref[...], dbg_ref[...], c_ref[...], bd_ref[...], al_ref[...], dt_ref[...], tb)
        dal_ref[0:1, :] += dal_row
        ddt_ref[0:1, :] += ddt_row
        xc = jnp.concatenate([dc, head_ref[...]], axis=0)
        head_ref[...] = dc[0:8, :]
        w4 = cw_ref[...]
        pre = pre_ref[...]
        dqkv = None
        for j in range(4):
            later = xc[0:tb, :] if j == 3 else _rows_from(xc, 3 - j, tb)
            dqkv = w4[j:j + 1, :] * later if dqkv is None else dqkv + w4[j:j + 1, :] * later
            dcw_ref[j:j + 1, :] += jnp.sum(later * pre, axis=0, keepdims=True)
        yc = jnp.concatenate([dcv_ref[...], jnp.where(last, 0.0, dcv_halo_ref[...])], axis=0)
        w3 = scw_ref[...]
        du = w3[2:3, :] * yc[0:tb, :] + w3[1:2, :] * _rows_from(yc, 1, tb) + w3[0:1, :] * _rows_from(yc, 2, tb)
        sc = sc_ref[...]
        dsc = jnp.concatenate([dgb_ref[...], du * sc[:, 2 * SC_WIDTH:], du * sc[:, SC_WIDTH:2 * SC_WIDTH]], axis=1)
        blocks = (dqkv.astype(BF16), dz_ref[...], dbd, dsc.astype(BF16))
        dh = jnp.zeros((tb, D_MODEL), F32)
        for s in range(N_CHIPS):
            dps = _dp_of_chip(*blocks, s)
            dps_ref[:, D_MODEL * s:D_MODEL * (s + 1)] = dps
            dh = dh + lax.dot_general(dps, w_ref[s], NT, preferred_element_type=F32)
        xv = x_ref[...]
        r = lax.rsqrt(jnp.mean(xv * xv, axis=-1, keepdims=True) + EPS)
        xh = xv * r
        _row_acc(dg_ref, dh * xh)
        dx = dx1_ref[...] + _rms_bwd(dh, xh, r, g_ref[...])
        dx_ref[...] = dx
        dxb_ref[...] = dx.astype(BF16)

    tok = lambda w: pl.BlockSpec((tb, w), lambda i: (nb - 1 - i, 0))
    full = lambda t: pl.BlockSpec(t.shape, lambda i: (0, 0))
    acc = lambda w: pl.BlockSpec((8, w), lambda i: (0, 0))
    after = lambda w: pl.BlockSpec((8, w), lambda i: _after_halo(tb, T)(nb - 1 - i))
    return pl.pallas_call(
        body, name="in_proj_bwd", grid=(nb,),
        in_specs=[tok(DN_WIDTH), tok(DN_WIDTH), tok(DN_WIDTH), tok(LANES), tok(QKV), tok(QKV), tok(LANES),
                  full(al_row), full(dt_row), tok(SC_WIDTH), after(SC_WIDTH), tok(SC_WIDTH), tok(3 * SC_WIDTH),
                  tok(DN_WIDTH), full(cw), full(scw), tok(D_MODEL), tok(D_MODEL), full(g1), _shard_rows(land_a, 0, D_MODEL)],
        out_specs=[tok(D_MODEL), tok(D_MODEL), tok(N_CHIPS * D_MODEL), acc(D_MODEL), acc(QKV), acc(LANES), acc(LANES)],
        out_shape=[jax.ShapeDtypeStruct((T, D_MODEL), F32), jax.ShapeDtypeStruct((T, D_MODEL), BF16),
                   jax.ShapeDtypeStruct((T, N_CHIPS * D_MODEL), BF16), jax.ShapeDtypeStruct((8, D_MODEL), F32),
                   jax.ShapeDtypeStruct((8, QKV), F32), jax.ShapeDtypeStruct((8, LANES), F32),
                   jax.ShapeDtypeStruct((8, LANES), F32)],
        scratch_shapes=[pltpu.VMEM((8, QKV), F32)],
        compiler_params=_params(("arbitrary",)),
    )(dq, dk, dv, dbg, qkv, c, bd, al_row, dt_row, dcv, dcv, dgb, sc_in, dz, cw, scw, dx1, x, g1, land_a)


def _wgrad_in_share(h, dps, parts, name):
    T = h.shape[0]
    bk = min(T, WGRAD_TOKENS)
    n_k = T // bk

    def body(a_ref, b_ref, parts_ref, o_ref, acc_ref):
        kk = pl.program_id(1)

        @pl.when(kk == 0)
        def _():
            acc_ref[...] = jnp.zeros_like(acc_ref)

        acc_ref[...] += lax.dot_general(a_ref[...], b_ref[...], TN, preferred_element_type=F32)

        @pl.when(kk == n_k - 1)
        def _():
            o_ref[0] = acc_ref[...].astype(BF16)

    return pl.pallas_call(
        body, name=name, grid=(N_CHIPS, n_k),
        in_specs=[pl.BlockSpec((bk, D_MODEL), lambda j, kk: (kk, 0)), pl.BlockSpec((bk, D_MODEL), lambda j, kk: (kk, j)), _ANY],
        out_specs=pl.BlockSpec((1, D_MODEL, D_MODEL), lambda j, kk: (j, 0, 0)),
        out_shape=jax.ShapeDtypeStruct(parts.shape, BF16),
        scratch_shapes=[pltpu.VMEM((D_MODEL, D_MODEL), F32)],
        input_output_aliases={2: 0},
        compiler_params=_params(("parallel", "arbitrary")),
    )(h, dps, parts)


def _pad_rows(a, rows=8):
    return jnp.pad(a, ((0, rows - a.shape[0]), (0, 0)))


def _gate_rows(a_log, dt_bias):
    put = lambda t: jnp.pad(t.reshape(1, HEADS), ((0, 0), (HEADS, LANES - 2 * HEADS)))
    return put(a_log), put(dt_bias)


def _mixer_fwd(x, p):
    qkv, z, sc_in, bd, h, q, k, v, bg, c = _in_proj(x, p["g1"], p["land_a"], p["cw"], p["al"], p["dt"])
    o, states, xms = _delta_fwd(q, k, v, bg)
    return dict(x=x, qkv=qkv, c=c, z=z, sc_in=sc_in, bd=bd, h=h, q=q, k=k, v=v, bg=bg, o=o, states=states, xms=xms)


def _tail_fwd(s, p, land_b):
    x1, mix, x2, a, b, h2 = _mix_ffn(s["o"], s["z"], s["sc_in"], s["x"], p["land_a"], p["gn"], p["scw"], p["gs"],
                                     p["g2"], land_b)
    return x2, dict(s, mix=mix), dict(x1=x1, a=a, b=b, h2=h2)


def _ffn_back(dx2, dx2_bf16, s, sm, p, land_b):
    dx1, da, db, act, dg2, *mid = _ffn_mix_bwd(dx2, s["x1"], s["a"], s["b"], p["g2"], land_b, sm["o"], sm["z"],
                                               sm["sc_in"], p["land_a"], p["gn"], p["scw"], p["gs"])
    parts = lax.empty((N_CHIPS, B_ROWS, D_MODEL), BF16)
    parts = _wgrad_share(act, dx2_bf16, parts, 2 * FF_SHARD, "wgrad_down")
    parts = _wgrad_share(da, s["h2"], parts, 0, "wgrad_gate")
    parts = _wgrad_share(db, s["h2"], parts, FF_SHARD, "wgrad_up")
    return dx1, parts, dg2[0], mid


def _mixer_bwd(dx1, mid, s, p, after):
    do, dz, dgb, dcv, dx1_bf16, dgn, dgs, dscw = mid
    dq, dk, dv, dbg = _delta_bwd(s["q"], s["k"], s["v"], s["bg"], s["states"], s["xms"], do, after)
    dx, dx_bf16, dps, dg1, dcw, dal, ddt = _in_proj_bwd(
        dq, dk, dv, dbg, s["qkv"], s["c"], s["bd"], p["al"], p["dt"], dcv, dgb, s["sc_in"], dz, p["cw"], p["scw"], dx1,
        s["x"], p["g1"], p["land_a"])
    parts = lax.empty((N_CHIPS, A_ROWS, D_MODEL), BF16)
    parts = _wgrad_in_share(s["h"], dps, parts, "wgrad_in")
    parts = _wgrad_share(s["mix"], dx1_bf16, parts, A_OUT_AT, "wgrad_out")
    g = dict(g1=dg1[0], gn=dgn[0], gs=dgs[0], scw=dscw[:3], cw=dcw[:4], al=dal[0, HEADS:2 * HEADS], dt=ddt[0, HEADS:2 * HEADS])
    return dx, dx_bf16, parts, g


def _place():
    return lax.axis_index("x"), lax.axis_index("y"), lax.axis_index("c")


def _other_chips(x, y):
    return [(1 - x, y), (x, 1 - y), (1 - x, 1 - y)]


_HBM = pl.BlockSpec(memory_space=pltpu.HBM)


def _gather_chips(arrs, name):
    n = len(arrs)

    def body(*refs):
        ins, outs = refs[:n], refs[n:2 * n]
        send_sems, recv_sems, local_sems = refs[2 * n:]
        x, y, c = _place()
        me = 2 * x + y
        others = _other_chips(x, y)

        def remote(k, j, landing):
            px, py = others[j]
            return pltpu.make_async_remote_copy(src_ref=ins[k], dst_ref=outs[k].at[landing], send_sem=send_sems.at[k, j],
                                                recv_sem=recv_sems.at[k, j], device_id=(px, py, c), device_id_type=MESH)

        local = [pltpu.make_async_copy(ins[k], outs[k].at[me], local_sems.at[k]) for k in range(n)]
        sends = [remote(k, j, me) for k in range(n) for j in range(3)]
        for cp in local + sends:
            cp.start()
        for k in range(n):
            for j, (px, py) in enumerate(others):
                remote(k, j, 2 * px + py).wait_recv()
        for cp in sends:
            cp.wait_send()
        for cp in local:
            cp.wait()

    shapes = [jax.ShapeDtypeStruct((N_CHIPS,) + a.shape, a.dtype) for a in arrs]
    return pl.pallas_call(
        body, name=name, in_specs=[_HBM] * n, out_specs=[_HBM] * n, out_shape=shapes,
        scratch_shapes=[pltpu.SemaphoreType.DMA((n, 3)), pltpu.SemaphoreType.DMA((n, 3)), pltpu.SemaphoreType.DMA((n,))],
    )(*arrs)


_SEM = pl.BlockSpec(memory_space=pltpu.SEMAPHORE)
_ANY = pl.BlockSpec(memory_space=pl.ANY)
_EFFECT = pltpu.SideEffectType.DATAFLOW_SIDE_EFFECTING


_FLIPS = [(a, b, cc) for a in (0, 1) for b in (0, 1) for cc in (0, 1)][1:]


def _split_copies(src_ref, land_ref, send_sems, recv_sems, gather, sending):
    x, y, c = _place()
    copies = []
    if gather:
        me = 2 * x + y
        for j, (px, py) in enumerate(_other_chips(x, y)):
            copies.append(pltpu.make_async_remote_copy(
                src_ref=src_ref, dst_ref=land_ref.at[me if sending else 2 * px + py],
                send_sem=send_sems.at[j], recv_sem=recv_sems.at[j], device_id=(px, py, c), device_id_type=MESH))
        return copies
    me = 4 * x + 2 * y + c
    for j, (a, b, cc) in enumerate(_FLIPS):
        px, py, pc = (1 - x) if a else x, (1 - y) if b else y, (1 - c) if cc else c
        copies.append(pltpu.make_async_remote_copy(
            src_ref=src_ref.at[2 * px + py], dst_ref=land_ref.at[me if sending else 4 * px + 2 * py + pc],
            send_sem=send_sems.at[j], recv_sem=recv_sems.at[j], device_id=(px, py, pc), device_id_type=MESH))
    return copies


def _own_slot(share):
    chip = 2 * lax.axis_index("x") + lax.axis_index("y")
    return lax.dynamic_update_slice(lax.empty((N_CHIPS,) + share.shape, share.dtype), share[None], (chip, 0, 0))


def _own_part(parts):
    chip = 2 * lax.axis_index("x") + lax.axis_index("y")
    own = lax.dynamic_index_in_dim(parts, chip, 0, keepdims=True)
    return lax.dynamic_update_slice(lax.empty((N_DEV,) + parts.shape[1:], parts.dtype), own,
                                    (2 * chip + lax.axis_index("c"), 0, 0))


def _exchange_start(src, land, after, name, gather):
    def body(src_ref, land_ref, after_ref, send_sems, recv_sems, src_thru, land_thru, token):
        for cp in _split_copies(src_ref, land_ref, send_sems, recv_sems, gather, sending=True):
            cp.start()
        token[...] = jnp.zeros_like(token)

    hbm = lambda t: pltpu.with_memory_space_constraint(t, pltpu.HBM)
    n_copies = N_CHIPS - 1 if gather else N_DEV - 1
    return pl.pallas_call(
        body, name=name,
        out_shape=(pltpu.SemaphoreType.DMA((n_copies,)), pltpu.SemaphoreType.DMA((n_copies,)), pltpu.HBM(src.shape, src.dtype),
                   pltpu.HBM(land.shape, land.dtype), jax.ShapeDtypeStruct((8, LANES), F32)),
        in_specs=(_HBM, _HBM, _ANY), out_specs=(_SEM, _SEM, _HBM, _HBM, pl.BlockSpec(memory_space=pltpu.VMEM)),
        input_output_aliases={0: 2, 1: 3},
        compiler_params=pltpu.CompilerParams(has_side_effects=_EFFECT),
    )(hbm(src), hbm(land), after)


def _exchange_wait(started, after, name, gather):
    send_sems, recv_sems, src_thru, land_thru, _ = started

    def body(src_ref, land_ref, send_sems, recv_sems, after_ref, src_dead, got_ref):
        for cp in _split_copies(src_ref, land_ref, send_sems, recv_sems, gather, sending=False):
            cp.wait_send()
            cp.wait_recv()

    return pl.pallas_call(
        body, name=name,
        out_shape=(pltpu.HBM(src_thru.shape, src_thru.dtype), pltpu.HBM(land_thru.shape, land_thru.dtype)),
        in_specs=(_HBM, _HBM, _SEM, _SEM, _ANY), out_specs=(_HBM, _HBM), input_output_aliases={0: 0, 1: 1},
        compiler_params=pltpu.CompilerParams(has_side_effects=_EFFECT),
    )(src_thru, land_thru, send_sems, recv_sems, after)[1]


def _all_reduce_small(v):
    rows = v.shape[0]
    flips = [(a, b, cc) for a in (0, 1) for b in (0, 1) for cc in (0, 1)][1:]

    def body(v_ref, out_ref, buf_ref, send_sems, recv_sems):
        x, y, c = _place()
        me = 4 * x + 2 * y + c
        peers = [((1 - x) if a else x, (1 - y) if b else y, (1 - c) if cc else c) for a, b, cc in flips]

        def copy(j, landing):
            return pltpu.make_async_remote_copy(src_ref=v_ref, dst_ref=buf_ref.at[landing], send_sem=send_sems.at[j],
                                                recv_sem=recv_sems.at[j], device_id=peers[j], device_id_type=MESH)

        sends = [copy(j, me) for j in range(N_DEV - 1)]
        for cp in sends:
            cp.start()
        buf_ref[me] = v_ref[...]
        for j, (px, py, pc) in enumerate(peers):
            copy(j, 4 * px + 2 * py + pc).wait_recv()
        for cp in sends:
            cp.wait_send()
        acc = buf_ref[0]
        for d in range(1, N_DEV):
            acc = acc + buf_ref[d]
        out_ref[...] = acc

    vmem = pl.BlockSpec(memory_space=pltpu.VMEM)
    return pl.pallas_call(
        body, name="all_reduce_small", in_specs=[vmem], out_specs=vmem,
        out_shape=jax.ShapeDtypeStruct(v.shape, F32),
        scratch_shapes=[pltpu.VMEM((N_DEV, rows, LANES), F32), pltpu.SemaphoreType.DMA((N_DEV - 1,)),
                        pltpu.SemaphoreType.DMA((N_DEV - 1,))],
    )(v)


def _row_block(*sizes):
    return next(t for t in (256, 176, 128, 64) if all(s % t == 0 for s in sizes))


def _adam_update(w, m, v, g):
    r1 = 1.0 / (1.0 - ADAM_B1 ** ADAM_STEP)
    r2 = 1.0 / (1.0 - ADAM_B2 ** ADAM_STEP)
    m_new = ADAM_B1 * m + (1.0 - ADAM_B1) * g
    v_new = ADAM_B2 * v + (1.0 - ADAM_B2) * (g * g)
    return -ADAM_LR * ((m_new * r1) / (jnp.sqrt(v_new * r2) + ADAM_EPS) + ADAM_WD * w), m_new, v_new


def _adamw_rows(w, m, v, got, first, name):
    n_layers, rows, cols = w.shape
    tr = _row_block(rows, first)

    def body(*refs):
        w_ref, m_ref, v_ref = refs[:3]
        g_out, d_out, m_out, v_out = refs[3 + n_layers:]
        for k in range(n_layers):
            @pl.when(pl.program_id(0) == k)
            def _(p_ref=refs[3 + k]):
                g = p_ref[0].astype(F32)
                for d in range(1, N_DEV):
                    g = g + p_ref[d].astype(F32)
                g = g[:, :cols]
                d_out[0], m_out[0], v_out[0] = _adam_update(w_ref[0], m_ref[0], v_ref[0], g)
                g_out[0] = g

    blk = pl.BlockSpec((1, tr, cols), lambda l, i: (l, i, 0))
    parts = [pl.BlockSpec((N_DEV, tr, got[0].shape[2]), lambda l, i, k=k: (0, jnp.where(l == k, first // tr + i, 0), 0))
             for k in range(n_layers)]
    return pl.pallas_call(
        body, name=name, grid=(n_layers, rows // tr),
        in_specs=[blk] * 3 + parts, out_specs=[blk] * 4,
        out_shape=[jax.ShapeDtypeStruct(w.shape, F32)] * 4,
        compiler_params=_params(("arbitrary", "arbitrary")),
    )(w, m, v, *got)


def _adamw(w, m, v, g_parts, name):
    rows, cols = w.shape
    tr = min(rows, 256)
    n = len(g_parts)

    def body(*refs):
        w_ref, m_ref, v_ref = refs[:3]
        g_refs = refs[3:3 + n]
        g_out, d_out, m_out, v_out = refs[3 + n:]
        g = g_refs[0][...]
        for r in g_refs[1:]:
            g = g + r[...]
        d_out[...], m_out[...], v_out[...] = _adam_update(w_ref[...], m_ref[...], v_ref[...], g)
        g_out[...] = g

    blk = pl.BlockSpec((tr, cols), lambda i: (i, 0))
    return pl.pallas_call(
        body, name=name, grid=(rows // tr,),
        in_specs=[blk] * (3 + n), out_specs=[blk] * 4,
        out_shape=[jax.ShapeDtypeStruct((rows, cols), F32)] * 4,
        compiler_params=_params(("parallel",)),
    )(w, m, v, *g_parts)


def _pack(parts, rows, fill=0.0):
    flat = jnp.concatenate([p.reshape(-1) for p in parts])
    return jnp.pad(flat, (0, rows * LANES - flat.shape[0]), constant_values=fill).reshape(rows, LANES)


def _unpack(packed, shapes):
    flat = packed.reshape(-1)
    out, at = [], 0
    for shp in shapes:
        size = 1
        for s in shp:
            size *= s
        out.append(flat[at:at + size].reshape(shp))
        at += size
    return out


def _packed_rows(shapes):
    total = 0
    for shp in shapes:
        size = 1
        for s in shp:
            size *= s
        total += size
    return -(-total // (8 * LANES)) * 8


def _cols_full(g, l):
    t = g[:, l]
    return jnp.moveaxis(t, 0, 1).reshape(t.shape[1], N_CHIPS * t.shape[2])


def _pad_cols(t):
    return jnp.pad(t, ((0, 0),) * (t.ndim - 1) + ((0, D_MODEL - t.shape[-1]),))


def kernel(x, norm1_g, w_in, dn_conv_w, dn_a_log, dn_dt_bias, dn_norm_g, sc_conv_w, sc_norm_g, w_out, norm2_g, ffn_w_gate, ffn_w_up, ffn_w_down, final_norm_g, loss_target, m_norm1_g, m_w_in, m_dn_conv_w, m_dn_a_log, m_dn_dt_bias, m_dn_norm_g, m_sc_conv_w, m_sc_norm_g, m_w_out, m_norm2_g, m_ffn_w_gate, m_ffn_w_up, m_ffn_w_down, m_final_norm_g, v_norm1_g, v_w_in, v_dn_conv_w, v_dn_a_log, v_dn_dt_bias, v_dn_norm_g, v_sc_conv_w, v_sc_norm_g, v_w_out, v_norm2_g, v_ffn_w_gate, v_ffn_w_up, v_ffn_w_down, v_final_norm_g):
    chip = 2 * lax.axis_index("x") + lax.axis_index("y")

    g_cw, g_scw = _gather_chips([dn_conv_w, sc_conv_w], "gather_conv")

    t_last = lambda t: jnp.swapaxes(t, -1, -2)
    gate_t, up_t = t_last(ffn_w_gate), t_last(ffn_w_up)
    zero_token = jnp.zeros((8, LANES), F32)

    def shares(l, tie):
        share_a = jnp.concatenate([_pad_cols(w_in[l] + tie), w_out[l]], axis=0).astype(BF16)
        share_b = jnp.concatenate([gate_t[l] + tie, up_t[l], ffn_w_down[l]], axis=0).astype(BF16)
        return share_a, _own_slot(share_a), share_b, _own_slot(share_b)

    def gather_start(l, packed, after):
        a = _exchange_start(packed[0], packed[1], after, "gather_a_start_%d" % l, gather=True)
        b = _exchange_start(packed[2], packed[3], a[4], "gather_b_start_%d" % l, gather=True)
        return a, b

    ga, gb = gather_start(0, shares(0, 0.0), g_cw)
    packed = [None] + [shares(l, gb[4][0, 0]) for l in range(1, DEPTH)]
    packed_all = sum(t[0, 0].astype(F32) for p in packed[1:] for t in (p[0], p[2]))
    land_a = _exchange_wait(ga, zero_token + packed_all, "gather_a_wait_0", gather=True)
    act = x[0]
    layers, saved_m, saved_f, lands_b = [], [], [], []
    for l in range(DEPTH):
        hold = 0.0
        if l + 1 < DEPTH:
            ga, gb_next = gather_start(l + 1, packed[l + 1], land_a)
            hold = gb_next[4][0:1, 0:1]
        al, dt = _gate_rows(dn_a_log[l], dn_dt_bias[l])
        layers.append(dict(
            g1=norm1_g[l][None] + hold, cw=_pad_rows(_cols_full(g_cw, l)), al=al, dt=dt,
            gn=dn_norm_g[l][None], scw=_pad_rows(_cols_full(g_scw, l)), gs=sc_norm_g[l][None],
            land_a=land_a, g2=norm2_g[l][None]))
        s = _mixer_fwd(act, layers[l])
        lands_b.append(_exchange_wait(gb, s["o"], "gather_b_wait_%d" % l, gather=True))
        act, s, sf = _tail_fwd(s, layers[l], lands_b[l])
        saved_m.append(s)
        saved_f.append(sf)
        if l + 1 < DEPTH:
            land_a = _exchange_wait(ga, act, "gather_a_wait_%d" % (l + 1), gather=True)
            gb = gb_next

    dact, dact_bf16, loss_part, d_final = _loss_head(act, final_norm_g[None], loss_target[0])
    grads, reduce_a, reduce_b = [None] * DEPTH, [None] * DEPTH, [None] * DEPTH
    hold = 0.0
    for l in reversed(range(DEPTH)):
        p = layers[l]
        dx1, parts, dg2, mid = _ffn_back(dact, dact_bf16, saved_f[l], saved_m[l], dict(p, g2=p["g2"] + hold), lands_b[l])
        reduce_b[l] = _exchange_start(parts, _own_part(parts), zero_token, "reduce_b_start_%d" % l, gather=False)
        dact, dact_bf16, parts, gm = _mixer_bwd(dx1, mid, saved_m[l], p, reduce_b[l][4][0:1, 0:1])
        reduce_a[l] = _exchange_start(parts, _own_part(parts), zero_token, "reduce_a_start_%d" % l, gather=False)
        hold = reduce_a[l][4][0:1, 0:1]
        grads[l] = dict(gm, g2=dg2)
    loss = lax.psum(loss_part[0, 0], ("x", "y", "c"))
    stack = lambda key: jnp.stack([grads[l][key] for l in range(DEPTH)])

    got_b = [_exchange_wait(reduce_b[l], reduce_a[0][4], "reduce_b_wait_%d" % l, gather=False)
             for l in reversed(range(DEPTH))][::-1]
    big = dict(
        ffn_w_gate=[t_last(o) for o in _adamw_rows(gate_t, t_last(m_ffn_w_gate), t_last(v_ffn_w_gate), got_b, 0, "adamw_gate")],
        ffn_w_up=[t_last(o) for o in _adamw_rows(up_t, t_last(m_ffn_w_up), t_last(v_ffn_w_up), got_b, FF_SHARD, "adamw_up")],
        ffn_w_down=_adamw_rows(ffn_w_down, m_ffn_w_down, v_ffn_w_down, got_b, 2 * FF_SHARD, "adamw_down"))
    after_b = zero_token + sum(big[n][1][0, 0, 0] for n in ("ffn_w_gate", "ffn_w_up", "ffn_w_down"))
    got_a = [_exchange_wait(reduce_a[l], after_b, "reduce_a_wait_%d" % l, gather=False) for l in reversed(range(DEPTH))][::-1]
    big.update(
        w_in=_adamw_rows(w_in, m_w_in, v_w_in, got_a, 0, "adamw_w_in"),
        w_out=_adamw_rows(w_out, m_w_out, v_w_out, got_a, A_OUT_AT, "adamw_w_out"))

    full_shapes = [(DEPTH, D_MODEL), (DEPTH, D_MODEL), (DEPTH, HEAD_DIM), (DEPTH, SC_WIDTH), (DEPTH, HEADS),
                   (DEPTH, HEADS), (D_MODEL,), (DEPTH, 4, QKV), (DEPTH, 3, SC_WIDTH)]
    small_keys = ("g1", "g2", "gn", "gs", "al", "dt")
    packed = _pack([stack(k) for k in small_keys] + [d_final[0], stack("cw"), stack("scw")], _packed_rows(full_shapes))
    sg = _unpack(_all_reduce_small(packed), full_shapes)
    sg[7] = lax.dynamic_slice_in_dim(sg[7], chip * (QKV // N_CHIPS), QKV // N_CHIPS, axis=2)
    sg[8] = lax.dynamic_slice_in_dim(sg[8], chip * (SC_WIDTH // N_CHIPS), SC_WIDTH // N_CHIPS, axis=2)
    small_names = ("norm1_g", "norm2_g", "dn_norm_g", "sc_norm_g", "dn_a_log", "dn_dt_bias", "final_norm_g",
                   "dn_conv_w", "sc_conv_w")
    sw = (norm1_g, norm2_g, dn_norm_g, sc_norm_g, dn_a_log, dn_dt_bias, final_norm_g, dn_conv_w, sc_conv_w)
    sm = (m_norm1_g, m_norm2_g, m_dn_norm_g, m_sc_norm_g, m_dn_a_log, m_dn_dt_bias, m_final_norm_g, m_dn_conv_w, m_sc_conv_w)
    sv = (v_norm1_g, v_norm2_g, v_dn_norm_g, v_sc_norm_g, v_dn_a_log, v_dn_dt_bias, v_final_norm_g, v_dn_conv_w, v_sc_conv_w)
    shard_shapes = [t.shape for t in sw]
    rows = _packed_rows(shard_shapes)
    outs = _adamw(_pack(sw, rows), _pack(sm, rows), _pack(sv, rows, fill=1.0), [_pack(sg, rows)], "adamw_small")
    small = {name: [] for name in small_names}
    for o in outs:
        for name, t in zip(small_names, _unpack(o, shard_shapes)):
            small[name].append(t)

    order = ("norm1_g", "w_in", "dn_conv_w", "dn_a_log", "dn_dt_bias", "dn_norm_g", "sc_conv_w", "sc_norm_g", "w_out",
             "norm2_g", "ffn_w_gate", "ffn_w_up", "ffn_w_down", "final_norm_g")
    result = {**big, **small}
    return (loss, dact[None], *[result[n][0] for n in order], *[result[n][1] for n in order],
            *[result[n][2] for n in order], *[result[n][3] for n in order])
```

```python
import jax
import jax.numpy as jnp
from jax import lax
from jax.experimental import pallas as pl
from jax.experimental.pallas import tpu as pltpu

F32 = jnp.float32
BF16 = jnp.bfloat16
MESH = pl.DeviceIdType.MESH

D_MODEL = 1024
DEPTH = 4
HEADS = 4
HEAD_DIM = 128
DN_WIDTH = HEADS * HEAD_DIM
SC_WIDTH = 512
SC_GROUPS = 4
D_FF = 2816
CHUNK = 64
QKV = 3 * DN_WIDTH
W_IN_COLS = 4 * DN_WIDTH + 2 * HEADS + 3 * SC_WIDTH
LANES = 128
EPS = 1e-6
Q_SCALE = HEAD_DIM ** -0.5
N_CHIPS = 4
N_DEV = 8
IN_SHARD = W_IN_COLS // N_CHIPS
OUT_SHARD = D_MODEL // N_CHIPS
FF_SHARD = D_FF // N_CHIPS
A_OUT_AT = D_MODEL
A_ROWS = D_MODEL + OUT_SHARD
B_ROWS = 3 * FF_SHARD

ADAM_LR = 0.001
ADAM_B1 = 0.9
ADAM_B2 = 0.999
ADAM_EPS = 1e-08
ADAM_WD = 0.01
ADAM_STEP = 10

VMEM_LIMIT = 60 * 1024 * 1024

NN = (((1,), (0,)), ((), ()))
NT = (((1,), (1,)), ((), ()))
TN = (((0,), (0,)), ((), ()))


def _mm(a, b, dims=NN):
    return lax.dot_general(a.astype(BF16), b.astype(BF16), dims, preferred_element_type=F32)


def _mm32(a, b, dims=NN):
    return lax.dot_general(a, b, dims, preferred_element_type=F32, precision=lax.Precision.HIGHEST)


def _params(sem, vmem=VMEM_LIMIT):
    return pltpu.CompilerParams(dimension_semantics=sem, vmem_limit_bytes=vmem)


def _sigmoid(x):
    return 0.5 * jnp.tanh(0.5 * x) + 0.5


def _softplus(x):
    return jnp.maximum(x, 0.0) + jnp.log1p(jnp.exp(-jnp.abs(x)))


def _row_acc(acc_ref, val):
    acc_ref[0:1, :] += jnp.sum(val, axis=0, keepdims=True)


def _rms_bwd(dh, xh, r, gain):
    dxh = dh * gain
    return r * (dxh - xh * jnp.mean(dxh * xh, axis=-1, keepdims=True))


def _before_halo(tb):
    return lambda i: (jnp.maximum(i * (tb // 8) - 1, 0), 0)


def _after_halo(tb, n_rows):
    last = n_rows // 8 - 1
    return lambda i: (jnp.minimum((i + 1) * (tb // 8), last), 0)


def _rows_from(xc, offset, tb):
    part = offset % 8
    if part:
        xc = pltpu.roll(xc, xc.shape[0] - part, 0)
    return xc[offset - part:offset - part + tb, :]


def _taps(xc, w, n_taps, tb, first):
    out = w[0:1, :] * _rows_from(xc, first, tb)
    for j in range(1, n_taps):
        out = out + w[j:j + 1, :] * _rows_from(xc, first + j, tb)
    return out


W_Z = QKV
W_BD = W_Z + DN_WIDTH
W_SC = W_BD + 2 * HEADS

def _w_in_cols(shards, lo, hi):
    pieces = []
    for s in range(N_CHIPS):
        a, b = max(lo, IN_SHARD * s), min(hi, IN_SHARD * (s + 1))
        if a < b:
            pieces.append(shards[s][:, a - IN_SHARD * s:b - IN_SHARD * s])
    return pieces[0] if len(pieces) == 1 else jnp.concatenate(pieces, axis=1)


def _in_proj(x, g1, land_a, cw, al_row, dt_row):
    T = x.shape[0]
    tb = 256

    def body(x_ref, g_ref, w_ref, cw_ref, al_ref, dt_ref,
             qkv_ref, z_ref, sc_ref, bd_ref, h_ref, q_ref, k_ref, v_ref, bg_ref, c_ref, tail_ref):
        @pl.when(pl.program_id(0) == 0)
        def _():
            tail_ref[...] = jnp.zeros_like(tail_ref)

        xv = x_ref[...]
        h = (xv * lax.rsqrt(jnp.mean(xv * xv, axis=-1, keepdims=True) + EPS) * g_ref[...]).astype(BF16)
        shards = [jnp.dot(h, w_ref[s], preferred_element_type=F32) for s in range(N_CHIPS)]
        qkv = _w_in_cols(shards, 0, W_Z)
        bd = jnp.concatenate([_w_in_cols(shards, W_BD, W_SC), jnp.zeros((tb, LANES - 2 * HEADS), F32)], axis=1)
        qkv_ref[...] = qkv
        z_ref[...] = _w_in_cols(shards, W_Z, W_BD)
        bd_ref[...] = bd
        sc_ref[...] = _w_in_cols(shards, W_SC, W_IN_COLS)
        h_ref[...] = h
        halo = tail_ref[...]
        tail_ref[...] = qkv[tb - 8:, :]
        _, c, _, a = _dn_act(qkv, halo, cw_ref[...], tb)
        c_ref[...] = c
        for hd in range(HEADS):
            sl = slice(HEAD_DIM * hd, HEAD_DIM * (hd + 1))
            qs = a[:, sl]
            q_ref[:, sl] = qs * (lax.rsqrt(jnp.sum(qs * qs, axis=-1, keepdims=True) + EPS) * Q_SCALE)
            ks = a[:, DN_WIDTH + HEAD_DIM * hd:DN_WIDTH + HEAD_DIM * (hd + 1)]
            k_ref[:, sl] = ks * lax.rsqrt(jnp.sum(ks * ks, axis=-1, keepdims=True) + EPS)
        v_ref[...] = a[:, 2 * DN_WIDTH:]
        gates = _gates(bd, al_ref[...], dt_ref[...])
        lane = lax.broadcasted_iota(jnp.int32, gates.shape, 1)
        bg_ref[...] = jnp.where(lane < HEADS, gates, _mm32(_chunk_cumsum_matrix(tb), gates))

    tok = lambda w: pl.BlockSpec((tb, w), lambda i: (i, 0))
    full = lambda t: pl.BlockSpec(t.shape, lambda i: (0, 0))
    return pl.pallas_call(
        body, name="in_proj", grid=(T // tb,),
        in_specs=[tok(D_MODEL), full(g1), _shard_rows(land_a, 0, D_MODEL), full(cw), full(al_row), full(dt_row)],
        out_specs=[tok(QKV), tok(DN_WIDTH), tok(3 * SC_WIDTH), tok(LANES), tok(D_MODEL),
                   tok(DN_WIDTH), tok(DN_WIDTH), tok(DN_WIDTH), tok(LANES), tok(QKV)],
        out_shape=[jax.ShapeDtypeStruct((T, QKV), F32), jax.ShapeDtypeStruct((T, DN_WIDTH), F32),
                   jax.ShapeDtypeStruct((T, 3 * SC_WIDTH), F32), jax.ShapeDtypeStruct((T, LANES), F32),
                   jax.ShapeDtypeStruct((T, D_MODEL), BF16)]
        + [jax.ShapeDtypeStruct((T, DN_WIDTH), F32)] * 3 + [jax.ShapeDtypeStruct((T, LANES), F32),
                                                              jax.ShapeDtypeStruct((T, QKV), F32)],
        scratch_shapes=[pltpu.VMEM((8, QKV), F32)],
        compiler_params=_params(("arbitrary",)),
    )(x, g1, land_a, cw, al_row, dt_row)


def _dn_act(pre, halo, cw, tb):
    xc = jnp.concatenate([halo, pre], axis=0)
    c = _taps(xc, cw, 4, tb, 5)
    sg = _sigmoid(c)
    return xc, c, sg, c * sg


def _gates(bd, al_row, dt_row):
    lane = lax.broadcasted_iota(jnp.int32, bd.shape, 1)
    beta = _sigmoid(bd)
    g = -jnp.exp(al_row) * _softplus(bd + dt_row)
    return jnp.where(lane < HEADS, beta, jnp.where(lane < 2 * HEADS, g, 0.0))


def _chunk_masks():
    row = lax.broadcasted_iota(jnp.int32, (CHUNK, CHUNK), 0)
    col = lax.broadcasted_iota(jnp.int32, (CHUNK, CHUNK), 1)
    return row >= col, row > col


def _chunk_cumsum_matrix(n):
    row = lax.broadcasted_iota(jnp.int32, (n, n), 0)
    col = lax.broadcasted_iota(jnp.int32, (n, n), 1)
    return jnp.logical_and(row >= col, row // CHUNK == col // CHUNK).astype(F32)


def _chunk_units(q_ref, k_ref, v_ref, bg_ref, rows):
    bgc = bg_ref[rows, :]
    bg_t = bgc.T
    qv, kv, vv = q_ref[rows, :], k_ref[rows, :], v_ref[rows, :]
    units = []
    for h in range(HEADS):
        sl = slice(HEAD_DIM * h, HEAD_DIM * (h + 1))
        units.append((qv[:, sl], kv[:, sl], vv[:, sl], bgc[:, h:h + 1], bgc[:, HEADS + h:HEADS + h + 1],
                      bg_t[HEADS + h:HEADS + h + 1, :]))
    return units


def _units_local(units, masks, xms=None):
    causal, strict = masks
    pre = []
    for q, k, v, beta, gc, gr in units:
        kb = k * beta
        eg = jnp.exp(gc)
        g_last = gc[CHUNK - 1:CHUNK, :]
        ek = jnp.exp(g_last - gc)
        pre.append(dict(q=q, k=k, v=v, beta=beta, decay=jnp.exp(jnp.where(causal, gc - gr, -1e30)), kb=kb, vb=v * beta,
                        eg=eg, kbg=kb * eg, ek=ek, gl=jnp.exp(g_last), q_dec=q * eg, k_dec=k * ek))
    both = [_mm(jnp.concatenate([p["kb"], p["q"]], axis=0), p["k"], NT) for p in pre]
    for p, b in zip(pre, both):
        p["low"] = jnp.where(strict, b[:CHUNK] * p["decay"], 0.0)
        p["qk"] = jnp.where(causal, b[CHUNK:] * p["decay"], 0.0)
    xs = xms
    if xs is None:
        xs = [-p["low"] for p in pre]
        pw = [_mm(p["low"], p["low"]) for p in pre]
        for _ in range(4):
            both = [_mm(jnp.concatenate([pp, x], axis=0), pp) for pp, x in zip(pw, xs)]
            xs = [x + pp + b[CHUNK:] for x, pp, b in zip(xs, pw, both)]
            pw = [b[:CHUNK] for b in both]
        last = [_mm(x, pp) for x, pp in zip(xs, pw)]
        xs = [x + pp + b for x, pp, b in zip(xs, pw, last)]
    uw = [_mm(x, jnp.concatenate([p["vb"], p["kbg"]], axis=1)) for x, p in zip(xs, pre)]
    for p, x, b in zip(pre, xs, uw):
        p["xm"] = x
        p["u"] = p["vb"] + b[:, :HEAD_DIM]
        p["w"] = p["kbg"] + b[:, HEAD_DIM:]
    return pre


FWD_GROUP = 8
BWD_GROUP = 8


def _delta_fwd(q, k, v, bg):
    T = q.shape[0]
    tb = 512
    n_chunk = tb // CHUNK

    def body(q_ref, k_ref, v_ref, bg_ref, o_ref, st_ref, xm_ref, s_ref):
        @pl.when(pl.program_id(0) == 0)
        def _():
            s_ref[...] = jnp.zeros_like(s_ref)

        masks = _chunk_masks()

        def group(gi, carry):
            rows = [pl.ds(pl.multiple_of((FWD_GROUP * gi + j) * CHUNK, CHUNK), CHUNK) for j in range(FWD_GROUP)]
            loc = _units_local(sum((_chunk_units(q_ref, k_ref, v_ref, bg_ref, r) for r in rows), []), masks)
            states = [s_ref[h] for h in range(HEADS)]
            for j in range(FWD_GROUP):
                lj = loc[HEADS * j:HEADS * (j + 1)]
                ws = [_mm(jnp.concatenate([p["w"], p["q_dec"]], axis=0), s) for p, s in zip(lj, states)]
                v_new = [p["u"] - b[:CHUNK] for p, b in zip(lj, ws)]
                intra = [_mm(p["qk"], vn) for p, vn in zip(lj, v_new)]
                upd = [_mm(p["k_dec"], vn, TN) for p, vn in zip(lj, v_new)]
                o_ref[rows[j], :] = jnp.concatenate([b[CHUNK:] + a for b, a in zip(ws, intra)], axis=1)
                for h in range(HEADS):
                    st_ref[FWD_GROUP * gi + j, h] = states[h]
                    xm_ref[FWD_GROUP * gi + j, h] = lj[h]["xm"]
                states = [p["gl"] * s + d for p, s, d in zip(lj, states, upd)]
            for h in range(HEADS):
                s_ref[h] = states[h]
            return carry

        lax.fori_loop(0, n_chunk // FWD_GROUP, group, 0)

    tok = lambda w: pl.BlockSpec((tb, w), lambda i: (i, 0))
    return pl.pallas_call(
        body, name="delta_fwd", grid=(T // tb,),
        in_specs=[tok(DN_WIDTH), tok(DN_WIDTH), tok(DN_WIDTH), tok(LANES)],
        out_specs=[tok(DN_WIDTH), pl.BlockSpec((n_chunk, HEADS, HEAD_DIM, HEAD_DIM), lambda i: (i, 0, 0, 0)),
                   pl.BlockSpec((n_chunk, HEADS, CHUNK, CHUNK), lambda i: (i, 0, 0, 0))],
        out_shape=[jax.ShapeDtypeStruct((T, DN_WIDTH), F32),
                   jax.ShapeDtypeStruct((T // CHUNK, HEADS, HEAD_DIM, HEAD_DIM), F32),
                   jax.ShapeDtypeStruct((T // CHUNK, HEADS, CHUNK, CHUNK), F32)],
        scratch_shapes=[pltpu.VMEM((HEADS, HEAD_DIM, HEAD_DIM), F32)],
        compiler_params=_params(("arbitrary",)),
    )(q, k, v, bg)


def _dn_out(o, z, gn):
    outs, ohs, rs = [], [], []
    for hh in range(HEADS):
        oh = o[:, HEAD_DIM * hh:HEAD_DIM * (hh + 1)]
        r = lax.rsqrt(jnp.mean(oh * oh, axis=-1, keepdims=True) + EPS)
        ohs.append(oh * r)
        rs.append(r)
    sz = _sigmoid(z)
    oh = jnp.concatenate(ohs, axis=1)
    gn4 = jnp.concatenate([gn] * HEADS, axis=1)
    return oh * gn4 * (z * sz), oh, rs, sz, gn4


def _sc_fwd(sc_in, halo, cw, tb):
    xc = jnp.concatenate([halo, sc_in], axis=0)
    u = xc[:, SC_WIDTH:2 * SC_WIDTH] * xc[:, 2 * SC_WIDTH:]
    cv = _taps(u, cw, 3, tb, 6)
    gate_b = sc_in[:, :SC_WIDTH]
    y = gate_b * cv
    gw = SC_WIDTH // SC_GROUPS
    yhs, rs = [], []
    for gi in range(SC_GROUPS):
        yg = y[:, gw * gi:gw * (gi + 1)]
        r = lax.rsqrt(jnp.mean(yg * yg, axis=-1, keepdims=True) + EPS)
        yhs.append(yg * r)
        rs.append(r)
    return u, cv, gate_b, jnp.concatenate(yhs, axis=1), rs


def _shard_rows(land, first, rows, single_buffer=False):
    assert first % rows == 0 and land.shape[0] == N_CHIPS
    mode = dict(pipeline_mode=pl.Buffered(1)) if single_buffer else {}
    return pl.BlockSpec((N_CHIPS, rows, land.shape[2]), lambda i: (0, first // rows, 0), **mode)


def _whole(w_ref):
    n, rows, cols = w_ref.shape
    return w_ref[...].reshape(n * rows, cols)


def _mix_ffn(o, z, sc_in, x, land_a, gn, scw, gs, g2, land_b):
    T = x.shape[0]
    tb = 256

    def body(o_ref, z_ref, sc_ref, halo_ref, x_ref, wo_ref, gn_ref, scw_ref, gs_ref, g2_ref, wgt_ref, wut_ref, wd_ref,
             x1_ref, mix_ref, x2_ref, a_ref, b_ref, h_ref):
        o_n = _dn_out(o_ref[...], z_ref[...], gn_ref[...])[0]
        halo = jnp.where(pl.program_id(0) > 0, halo_ref[...], 0.0)
        yh = _sc_fwd(sc_ref[...], halo, scw_ref[...], tb)[3]
        mix = jnp.concatenate([o_n, yh * gs_ref[...]], axis=1).astype(BF16)
        x1 = x_ref[...] + jnp.dot(mix, _whole(wo_ref), preferred_element_type=F32)
        x1_ref[...] = x1
        mix_ref[...] = mix
        r = lax.rsqrt(jnp.mean(x1 * x1, axis=-1, keepdims=True) + EPS)
        h = (x1 * r * g2_ref[...]).astype(BF16)
        a = lax.dot_general(h, _whole(wgt_ref), NT, preferred_element_type=F32)
        b = lax.dot_general(h, _whole(wut_ref), NT, preferred_element_type=F32)
        act = (a * _sigmoid(a) * b).astype(BF16)
        x2_ref[...] = x1 + jnp.dot(act, _whole(wd_ref), preferred_element_type=F32)
        a_ref[...] = a.astype(BF16)
        b_ref[...] = b.astype(BF16)
        h_ref[...] = h

    tok = lambda w: pl.BlockSpec((tb, w), lambda i: (i, 0))
    full = lambda t: pl.BlockSpec(t.shape, lambda i: (0, 0))
    once = lambda land, first, rows: _shard_rows(land, first, rows, single_buffer=True)
    return pl.pallas_call(
        body, name="mix_ffn", grid=(T // tb,),
        in_specs=[tok(DN_WIDTH), tok(DN_WIDTH), tok(3 * SC_WIDTH), pl.BlockSpec((8, 3 * SC_WIDTH), _before_halo(tb)),
                  tok(D_MODEL), once(land_a, A_OUT_AT, OUT_SHARD), full(gn), full(scw), full(gs), full(g2),
                  once(land_b, 0, FF_SHARD), once(land_b, FF_SHARD, FF_SHARD), once(land_b, 2 * FF_SHARD, FF_SHARD)],
        out_specs=[tok(D_MODEL), tok(D_MODEL), tok(D_MODEL), tok(D_FF), tok(D_FF), tok(D_MODEL)],
        out_shape=[jax.ShapeDtypeStruct((T, D_MODEL), F32), jax.ShapeDtypeStruct((T, D_MODEL), BF16),
                   jax.ShapeDtypeStruct((T, D_MODEL), F32), jax.ShapeDtypeStruct((T, D_FF), BF16),
                   jax.ShapeDtypeStruct((T, D_FF), BF16), jax.ShapeDtypeStruct((T, D_MODEL), BF16)],
        compiler_params=_params(("parallel",)),
    )(o, z, sc_in, sc_in, x, land_a, gn, scw, gs, g2, land_b, land_b, land_b)


def _loss_head(x, gf, target):
    T = x.shape[0]
    tb = 512

    def body(x_ref, g_ref, t_ref, dx_ref, dxb_ref, loss_ref, dg_ref):
        @pl.when(pl.program_id(0) == 0)
        def _():
            loss_ref[...] = jnp.zeros_like(loss_ref)
            dg_ref[...] = jnp.zeros_like(dg_ref)

        xv = x_ref[...]
        r = lax.rsqrt(jnp.mean(xv * xv, axis=-1, keepdims=True) + EPS)
        xh = xv * r
        err = xh * g_ref[...] - t_ref[...]
        per_tok = jnp.mean(err * err, axis=-1, keepdims=True)
        loss_ref[...] += 0.5 * jnp.sum(per_tok, axis=0, keepdims=True)
        dy = err * (1.0 / D_MODEL)
        _row_acc(dg_ref, dy * xh)
        dx = _rms_bwd(dy, xh, r, g_ref[...])
        dx_ref[...] = dx
        dxb_ref[...] = dx.astype(BF16)

    tok = pl.BlockSpec((tb, D_MODEL), lambda i: (i, 0))
    return pl.pallas_call(
        body, name="loss_head", grid=(T // tb,),
        in_specs=[tok, pl.BlockSpec(gf.shape, lambda i: (0, 0)), tok],
        out_specs=[tok, tok, pl.BlockSpec((8, LANES), lambda i: (0, 0)), pl.BlockSpec((8, D_MODEL), lambda i: (0, 0))],
        out_shape=[jax.ShapeDtypeStruct((T, D_MODEL), F32), jax.ShapeDtypeStruct((T, D_MODEL), BF16),
                   jax.ShapeDtypeStruct((8, LANES), F32), jax.ShapeDtypeStruct((8, D_MODEL), F32)],
        compiler_params=_params(("arbitrary",)),
    )(x, gf, target)


def _ffn_mix_bwd(dx2, x1, a, b, g2, land_b, o, z, sc_in, land_a, gn, scw, gs):
    T = x1.shape[0]
    tb = 256

    def body(dx2_ref, x_ref, a_ref, b_ref, g_ref, wgt_ref, wut_ref, wd_ref,
             o_ref, z_ref, sc_ref, halo_ref, w_ref, gn_ref, scw_ref, gs_ref,
             dx1_ref, da_ref, db_ref, act_ref, dg_ref,
             do_ref, dz_ref, dgb_ref, dcv_ref, dxb_ref, dgn_ref, dgs_ref, dscw_ref):
        @pl.when(pl.program_id(0) == 0)
        def _():
            for ref in (dg_ref, dgn_ref, dgs_ref, dscw_ref):
                ref[...] = jnp.zeros_like(ref)

        zv = z_ref[...]
        _, oh, rs, sz, gn4 = _dn_out(o_ref[...], zv, gn_ref[...])
        halo = jnp.where(pl.program_id(0) > 0, halo_ref[...], 0.0)
        u, cv, gate_b, yh, rys = _sc_fwd(sc_ref[...], halo, scw_ref[...], tb)

        dx2v = dx2_ref[...]
        av = a_ref[...].astype(F32)
        bv = b_ref[...].astype(F32)
        dact = _mm(dx2v, _whole(wd_ref), NT)
        sa = _sigmoid(av)
        silu = av * sa
        da = (dact * bv * (sa * (1.0 + av * (1.0 - sa)))).astype(BF16)
        db = (dact * silu).astype(BF16)
        dh = _mm(da, _whole(wgt_ref)) + _mm(db, _whole(wut_ref))
        xv = x_ref[...]
        r = lax.rsqrt(jnp.mean(xv * xv, axis=-1, keepdims=True) + EPS)
        xh = xv * r
        _row_acc(dg_ref, dh * xh)
        dx1 = dx2v + _rms_bwd(dh, xh, r, g_ref[...])
        dx1_ref[...] = dx1
        da_ref[...] = da
        db_ref[...] = db
        act_ref[...] = (silu * bv).astype(BF16)

        dx_bf16 = dx1.astype(BF16)
        dxb_ref[...] = dx_bf16
        dmix = lax.dot_general(dx_bf16, _whole(w_ref), NT, preferred_element_type=F32)
        don = dmix[:, :DN_WIDTH]
        dosc = dmix[:, DN_WIDTH:]
        silu_z = zv * sz
        dgn_full = don * oh * silu_z
        dgn_ref[0:1, :] += jnp.sum(sum(dgn_full[:, HEAD_DIM * hh:HEAD_DIM * (hh + 1)] for hh in range(HEADS)),
                                   axis=0, keepdims=True)
        dz_ref[...] = (don * oh * gn4 * (sz * (1.0 + zv * (1.0 - sz)))).astype(BF16)
        t = don * gn4 * silu_z
        for hh in range(HEADS):
            sl = slice(HEAD_DIM * hh, HEAD_DIM * (hh + 1))
            th, ohh = t[:, sl], oh[:, sl]
            do_ref[:, sl] = rs[hh] * (th - ohh * jnp.mean(th * ohh, axis=-1, keepdims=True))
        _row_acc(dgs_ref, dosc * yh)
        ty = dosc * gs_ref[...]
        gw = SC_WIDTH // SC_GROUPS
        dys = []
        for gi in range(SC_GROUPS):
            sl = slice(gw * gi, gw * (gi + 1))
            tg, yg = ty[:, sl], yh[:, sl]
            dys.append(rys[gi] * (tg - yg * jnp.mean(tg * yg, axis=-1, keepdims=True)))
        dy = jnp.concatenate(dys, axis=1)
        dgb_ref[...] = dy * cv
        dcv = dy * gate_b
        dcv_ref[...] = dcv
        for j in range(3):
            dscw_ref[j:j + 1, :] += jnp.sum(dcv * _rows_from(u, 6 + j, tb), axis=0, keepdims=True)

    tok = lambda w: pl.BlockSpec((tb, w), lambda i: (i, 0))
    full = lambda t: pl.BlockSpec(t.shape, lambda i: (0, 0))
    acc = lambda w: pl.BlockSpec((8, w), lambda i: (0, 0))
    once = lambda land, first, rows: _shard_rows(land, first, rows, single_buffer=True)
    return pl.pallas_call(
        body, name="ffn_mix_bwd", grid=(T // tb,),
        in_specs=[tok(D_MODEL), tok(D_MODEL), tok(D_FF), tok(D_FF), full(g2),
                  once(land_b, 0, FF_SHARD), once(land_b, FF_SHARD, FF_SHARD), once(land_b, 2 * FF_SHARD, FF_SHARD),
                  tok(DN_WIDTH), tok(DN_WIDTH), tok(3 * SC_WIDTH), pl.BlockSpec((8, 3 * SC_WIDTH), _before_halo(tb)),
                  once(land_a, A_OUT_AT, OUT_SHARD), full(gn), full(scw), full(gs)],
        out_specs=[tok(D_MODEL), tok(D_FF), tok(D_FF), tok(D_FF), acc(D_MODEL),
                   tok(DN_WIDTH), tok(DN_WIDTH), tok(SC_WIDTH), tok(SC_WIDTH), tok(D_MODEL),
                   acc(HEAD_DIM), acc(SC_WIDTH), acc(SC_WIDTH)],
        out_shape=[jax.ShapeDtypeStruct((T, D_MODEL), F32)]
        + [jax.ShapeDtypeStruct((T, D_FF), BF16)] * 3 + [jax.ShapeDtypeStruct((8, D_MODEL), F32)]
        + [jax.ShapeDtypeStruct((T, DN_WIDTH), F32), jax.ShapeDtypeStruct((T, DN_WIDTH), BF16),
           jax.ShapeDtypeStruct((T, SC_WIDTH), F32), jax.ShapeDtypeStruct((T, SC_WIDTH), F32),
           jax.ShapeDtypeStruct((T, D_MODEL), BF16),
           jax.ShapeDtypeStruct((8, HEAD_DIM), F32), jax.ShapeDtypeStruct((8, SC_WIDTH), F32),
           jax.ShapeDtypeStruct((8, SC_WIDTH), F32)],
        compiler_params=_params(("arbitrary",)),
    )(dx2, x1, a, b, g2, land_b, land_b, land_b, o, z, sc_in, sc_in, land_a, gn, scw, gs)


WGRAD_TOKENS = 2048


def _wgrad_share(a, b, parts, first, name):
    T = b.shape[0]
    rows = a.shape[1] // N_CHIPS
    assert first % rows == 0 and b.shape[1] == parts.shape[2]
    bk = min(T, WGRAD_TOKENS)
    n_k = T // bk
    group = 2
    assert (group * rows) % LANES == 0 and n_k >= 2

    def body(a_ref, b_ref, parts_ref, o_ref, acc_ref):
        kk = pl.program_id(1)
        product = lambda: lax.dot_general(a_ref[...], b_ref[...], TN, preferred_element_type=F32)

        @pl.when(kk == 0)
        def _():
            acc_ref[...] = product()

        @pl.when(jnp.logical_and(kk > 0, kk < n_k - 1))
        def _():
            acc_ref[...] += product()

        @pl.when(kk == n_k - 1)
        def _():
            total = acc_ref[...] + product()
            for s in range(group):
                o_ref[s] = total[rows * s:rows * (s + 1), :].astype(BF16)

    return pl.pallas_call(
        body, name=name, grid=(N_CHIPS // group, n_k),
        in_specs=[pl.BlockSpec((bk, group * rows), lambda i, kk: (kk, i)),
                  pl.BlockSpec((bk, b.shape[1]), lambda i, kk: (kk, 0)), _ANY],
        out_specs=pl.BlockSpec((group, rows, b.shape[1]), lambda i, kk: (i, first // rows, 0)),
        out_shape=jax.ShapeDtypeStruct(parts.shape, BF16),
        scratch_shapes=[pltpu.VMEM((group * rows, b.shape[1]), F32)],
        input_output_aliases={2: 0},
        compiler_params=_params(("parallel", "arbitrary")),
    )(a, b, parts)


def _delta_bwd(q, k, v, bg, states, xms, do, after):
    T = q.shape[0]
    tb = 512
    n_chunk = tb // CHUNK
    nb = T // tb

    def body(q_ref, k_ref, v_ref, bg_ref, st_ref, xm_ref, do_ref, after_ref, dq_ref, dk_ref, dv_ref, dbg_ref, ds_ref):
        @pl.when(pl.program_id(0) == 0)
        def _():
            ds_ref[...] = jnp.zeros_like(ds_ref)

        masks = _chunk_masks()
        causal, strict = masks
        lane = lax.broadcasted_iota(jnp.int32, (CHUNK, LANES), 1)
        last_row = lax.broadcasted_iota(jnp.int32, (CHUNK, 1), 0) == CHUNK - 1
        cat = jnp.concatenate
        heads = range(HEADS)

        def open_chunk(ci, loc):
            rows = pl.ds(pl.multiple_of(ci * CHUNK, CHUNK), CHUNK)
            dov = do_ref[rows, :]
            return dict(rows=rows, loc=loc, do=[dov[:, HEAD_DIM * h:HEAD_DIM * (h + 1)] for h in heads],
                        state=[st_ref[ci, h] for h in heads])

        def a_free(c):
            loc, do, state = c["loc"], c["do"], c["state"]
            w_s = [_mm(p["w"], s) for p, s in zip(loc, state)]
            c["dq_dec"] = [_mm(d, s, NT) for d, s in zip(do, state)]
            c["qk_do"] = [_mm(p["qk"], d, TN) for p, d in zip(loc, do)]
            c["qd_do"] = [_mm(p["q_dec"], d, TN) for p, d in zip(loc, do)]
            c["v_new"] = [p["u"] - t for p, t in zip(loc, w_s)]
            c["dqk"] = [jnp.where(causal, _mm(d, vn, NT), 0.0) for d, vn in zip(do, c["v_new"])]

        def a_state(c, ds_next):
            c["ds_next"] = ds_next
            kd_ds = [_mm(p["k_dec"], d) for p, d in zip(c["loc"], ds_next)]
            c["dk_dec"] = [_mm(vn, d, NT) for vn, d in zip(c["v_new"], ds_next)]
            c["dv_new"] = [a + b for a, b in zip(c["qk_do"], kd_ds)]

        def b_state(c):
            loc = c["loc"]
            w_dv = [_mm(p["w"], dvn, TN) for p, dvn in zip(loc, c["dv_new"])]
            c["dw"] = [-_mm(dvn, s, NT) for dvn, s in zip(c["dv_new"], c["state"])]
            return [loc[h]["gl"] * c["ds_next"][h] + c["qd_do"][h] - w_dv[h] for h in heads]

        def c_solve(c):
            loc, dv_new, dw = c["loc"], c["dv_new"], c["dw"]
            c["dtm"] = [_mm(cat([dvn, d], axis=1), cat([p["vb"], p["kbg"]], axis=1), NT) for dvn, d, p in zip(dv_new, dw, loc)]
            x_t = [_mm(p["xm"], cat([dvn, d], axis=1), TN) for p, dvn, d in zip(loc, dv_new, dw)]
            c["dvb"] = [dvn + t[:, :HEAD_DIM] for dvn, t in zip(dv_new, x_t)]
            c["dkbg"] = [d + t[:, HEAD_DIM:] for d, t in zip(dw, x_t)]

        def d_solve(c):
            c["y"] = [t + _mm(p["xm"], t, TN) for p, t in zip(c["loc"], c["dtm"])]

        def e_solve(c):
            c["dlow"] = [jnp.where(strict, -(t + _mm(t, p["xm"], NT)), 0.0) for p, t in zip(c["loc"], c["y"])]

        def f_close(c):
            loc, rows = c["loc"], c["rows"]
            dmm = [d * p["decay"] for d, p in zip(c["dlow"], loc)]
            dnn = [d * p["decay"] for d, p in zip(c["dqk"], loc)]
            by_k = [_mm(cat([a, b], axis=0), p["k"]) for a, b, p in zip(dmm, dnn, loc)]
            dk_mm = [_mm(cat([a, b], axis=0), cat([p["kb"], p["q"]], axis=0), TN) for a, b, p in zip(dmm, dnn, loc)]
            dq_out, dk_out, dv_out = [], [], []
            dbeta_all = jnp.zeros((CHUNK, LANES), F32)
            dgc_all = jnp.zeros((CHUNK, LANES), F32)
            for h in heads:
                p = loc[h]
                dkb = by_k[h][:CHUNK] + c["dkbg"][h] * p["eg"]
                dq_out.append(by_k[h][CHUNK:] + c["dq_dec"][h] * p["eg"])
                dk_out.append(dk_mm[h] + c["dk_dec"][h] * p["ek"] + dkb * p["beta"])
                dv_out.append(c["dvb"][h] * p["beta"])
                dbeta = jnp.sum(dkb * p["k"] + c["dvb"][h] * p["v"], axis=1, keepdims=True)
                e = c["dlow"][h] * p["low"] + c["dqk"][h] * p["qk"]
                kd = jnp.sum(c["dk_dec"][h] * p["k_dec"], axis=1, keepdims=True)
                dgc = (jnp.sum(e, axis=1, keepdims=True) - jnp.sum(e.T, axis=1, keepdims=True)
                       + jnp.sum(c["dq_dec"][h] * p["q_dec"], axis=1, keepdims=True) - kd
                       + jnp.sum(c["dkbg"][h] * p["kbg"], axis=1, keepdims=True))
                dgl = jnp.sum(jnp.sum(c["ds_next"][h] * c["state"][h], axis=1, keepdims=True), axis=0, keepdims=True)
                d_last = jnp.sum(kd, axis=0, keepdims=True) + dgl * p["gl"]
                dgc = dgc + jnp.where(last_row, d_last, 0.0)
                dbeta_all = jnp.where(lane == h, dbeta, dbeta_all)
                dgc_all = jnp.where(lane == h + HEADS, dgc, dgc_all)
            dq_ref[rows, :] = cat(dq_out, axis=1)
            dk_ref[rows, :] = cat(dk_out, axis=1)
            dv_ref[rows, :] = cat(dv_out, axis=1)
            dbg_ref[rows, :] = dbeta_all + dgc_all

        def group(gj, carry):
            first = n_chunk - 1 - BWD_GROUP * gj
            ids = [first - j for j in range(BWD_GROUP)]
            rows = [pl.ds(pl.multiple_of(ci * CHUNK, CHUNK), CHUNK) for ci in ids]
            loc = _units_local(sum((_chunk_units(q_ref, k_ref, v_ref, bg_ref, r) for r in rows), []), masks,
                               xms=[xm_ref[ci, h] for ci in ids for h in heads])
            chunks = [open_chunk(ci, loc[HEADS * j:HEADS * (j + 1)]) for j, ci in enumerate(ids)]
            for c in chunks:
                a_free(c)
            ds_cur = [ds_ref[h] for h in heads]
            later = (c_solve, d_solve, e_solve, f_close)
            for t in range(2 * (BWD_GROUP - 1) + 2 + len(later)):
                for j, c in enumerate(chunks):
                    stage = t - 2 * j
                    if stage == 0:
                        a_state(c, ds_cur)
                    elif stage == 1:
                        ds_cur = b_state(c)
                    elif 2 <= stage < 2 + len(later):
                        later[stage - 2](c)
            for h in heads:
                ds_ref[h] = ds_cur[h]
            return carry

        lax.fori_loop(0, n_chunk // BWD_GROUP, group, 0)

    tok = lambda w: pl.BlockSpec((tb, w), lambda i: (nb - 1 - i, 0))
    return pl.pallas_call(
        body, name="delta_bwd", grid=(nb,),
        in_specs=[tok(DN_WIDTH), tok(DN_WIDTH), tok(DN_WIDTH), tok(LANES),
                  pl.BlockSpec((n_chunk, HEADS, HEAD_DIM, HEAD_DIM), lambda i: (nb - 1 - i, 0, 0, 0)),
                  pl.BlockSpec((n_chunk, HEADS, CHUNK, CHUNK), lambda i: (nb - 1 - i, 0, 0, 0)), tok(DN_WIDTH),
                  pl.BlockSpec(memory_space=pltpu.SMEM)],
        out_specs=[tok(DN_WIDTH), tok(DN_WIDTH), tok(DN_WIDTH), tok(LANES)],
        out_shape=[jax.ShapeDtypeStruct((T, DN_WIDTH), F32)] * 3 + [jax.ShapeDtypeStruct((T, LANES), F32)],
        scratch_shapes=[pltpu.VMEM((HEADS, HEAD_DIM, HEAD_DIM), F32)],
        compiler_params=_params(("arbitrary",)),
    )(q, k, v, bg, states, xms, do, after)


def _dn_prep_back(dq, dk, dv, dbg, c, bd, al_row, dt_row, tb):
    sg = _sigmoid(c)
    a = c * sg
    dsilu = sg * (1.0 + c * (1.0 - sg))
    pieces = [None] * (2 * HEADS)
    for hd in range(HEADS):
        sl = slice(HEAD_DIM * hd, HEAD_DIM * (hd + 1))
        for which, (base, grad, scale) in enumerate(((0, dq, Q_SCALE), (DN_WIDTH, dk, 1.0))):
            sa = slice(base + HEAD_DIM * hd, base + HEAD_DIM * (hd + 1))
            raw = a[:, sa]
            r = lax.rsqrt(jnp.sum(raw * raw, axis=-1, keepdims=True) + EPS)
            nrm = raw * r
            gn_ = grad[:, sl] * scale
            pieces[which * HEADS + hd] = r * (gn_ - nrm * jnp.sum(gn_ * nrm, axis=-1, keepdims=True)) * dsilu[:, sa]
    dc = jnp.concatenate(pieces + [dv * dsilu[:, 2 * DN_WIDTH:]], axis=1)
    lane = lax.broadcasted_iota(jnp.int32, bd.shape, 1)
    is_b = lane < HEADS
    is_g = jnp.logical_and(lane >= HEADS, lane < 2 * HEADS)
    dbgv = jnp.where(is_b, dbg, _mm32(_chunk_cumsum_matrix(tb), dbg, TN))
    beta = _sigmoid(bd)
    neg_a = -jnp.exp(al_row)
    pre_sp = bd + dt_row
    g = neg_a * _softplus(pre_sp)
    da_in = dbgv * neg_a * _sigmoid(pre_sp)
    dbd = jnp.where(is_b, dbgv * beta * (1.0 - beta), jnp.where(is_g, da_in, 0.0)).astype(BF16)
    dal_row = jnp.sum(jnp.where(is_g, dbgv * g, 0.0), axis=0, keepdims=True)
    ddt_row = jnp.sum(jnp.where(is_g, da_in, 0.0), axis=0, keepdims=True)
    return dc, dbd, dal_row, ddt_row


def _dp_of_chip(dqkv, dz, dbd, dsc, s):
    lo, hi = IN_SHARD * s, IN_SHARD * (s + 1)
    pieces = []
    for w_at, w_end, block in ((0, W_Z, dqkv), (W_Z, W_BD, dz), (W_BD, W_SC, dbd), (W_SC, W_IN_COLS, dsc)):
        a, b = max(lo, w_at), min(hi, w_end)
        if a < b:
            pieces.append(block[:, a - w_at:b - w_at])
    pieces.append(jnp.zeros((dqkv.shape[0], D_MODEL - IN_SHARD), dqkv.dtype))
    return jnp.concatenate(pieces, axis=1)


def _in_proj_bwd(dq, dk, dv, dbg, qkv, c, bd, al_row, dt_row, dcv, dgb, sc_in, dz, cw, scw, dx1, x, g1, land_a):
    T = x.shape[0]
    tb = 256
    nb = T // tb

    def body(dq_ref, dk_ref, dv_ref, dbg_ref, pre_ref, c_ref, bd_ref, al_ref, dt_ref,
             dcv_ref, dcv_halo_ref, dgb_ref, sc_ref, dz_ref, cw_ref, scw_ref, dx1_ref, x_ref, g_ref, w_ref,
             dx_ref, dxb_ref, dps_ref, dg_ref, dcw_ref, dal_ref, ddt_ref, head_ref):
        @pl.when(pl.program_id(0) == 0)
        def _():
            for ref in (dg_ref, dcw_ref, dal_ref, ddt_ref, head_ref):
                ref[...] = jnp.zeros_like(ref)

        block = nb - 1 - pl.program_id(0)
        last = block == nb - 1
        dc, dbd, dal_row, ddt_row = _dn_prep_back(
            dq_ref[...], dk_ref[...], dv_ref[...], dbg_ref[...], c_ref[...], bd_ref[...], al_ref[...], dt_ref[...], tb)
        dal_ref[0:1, :] += dal_row
        ddt_ref[0:1, :] += ddt_row
        xc = jnp.concatenate([dc, head_ref[...]], axis=0)
        head_ref[...] = dc[0:8, :]
        w4 = cw_ref[...]
        pre = pre_ref[...]
        dqkv = None
        for j in range(4):
            later = xc[0:tb, :] if j == 3 else _rows_from(xc, 3 - j, tb)
            dqkv = w4[j:j + 1, :] * later if dqkv is None else dqkv + w4[j:j + 1, :] * later
            dcw_ref[j:j + 1, :] += jnp.sum(later * pre, axis=0, keepdims=True)
        yc = jnp.concatenate([dcv_ref[...], jnp.where(last, 0.0, dcv_halo_ref[...])], axis=0)
        w3 = scw_ref[...]
        du = w3[2:3, :] * yc[0:tb, :] + w3[1:2, :] * _rows_from(yc, 1, tb) + w3[0:1, :] * _rows_from(yc, 2, tb)
        sc = sc_ref[...]
        dsc = jnp.concatenate([dgb_ref[...], du * sc[:, 2 * SC_WIDTH:], du * sc[:, SC_WIDTH:2 * SC_WIDTH]], axis=1)
        blocks = (dqkv.astype(BF16), dz_ref[...], dbd, dsc.astype(BF16))
        dh = jnp.zeros((tb, D_MODEL), F32)
        for s in range(N_CHIPS):
            dps = _dp_of_chip(*blocks, s)
            dps_ref[:, D_MODEL * s:D_MODEL * (s + 1)] = dps
            dh = dh + lax.dot_general(dps, w_ref[s], NT, preferred_element_type=F32)
        xv = x_ref[...]
        r = lax.rsqrt(jnp.mean(xv * xv, axis=-1, keepdims=True) + EPS)
        xh = xv * r
        _row_acc(dg_ref, dh * xh)
        dx = dx1_ref[...] + _rms_bwd(dh, xh, r, g_ref[...])
        dx_ref[...] = dx
        dxb_ref[...] = dx.astype(BF16)

    tok = lambda w: pl.BlockSpec((tb, w), lambda i: (nb - 1 - i, 0))
    full = lambda t: pl.BlockSpec(t.shape, lambda i: (0, 0))
    acc = lambda w: pl.BlockSpec((8, w), lambda i: (0, 0))
    after = lambda w: pl.BlockSpec((8, w), lambda i: _after_halo(tb, T)(nb - 1 - i))
    return pl.pallas_call(
        body, name="in_proj_bwd", grid=(nb,),
        in_specs=[tok(DN_WIDTH), tok(DN_WIDTH), tok(DN_WIDTH), tok(LANES), tok(QKV), tok(QKV), tok(LANES),
                  full(al_row), full(dt_row), tok(SC_WIDTH), after(SC_WIDTH), tok(SC_WIDTH), tok(3 * SC_WIDTH),
                  tok(DN_WIDTH), full(cw), full(scw), tok(D_MODEL), tok(D_MODEL), full(g1), _shard_rows(land_a, 0, D_MODEL)],
        out_specs=[tok(D_MODEL), tok(D_MODEL), tok(N_CHIPS * D_MODEL), acc(D_MODEL), acc(QKV), acc(LANES), acc(LANES)],
        out_shape=[jax.ShapeDtypeStruct((T, D_MODEL), F32), jax.ShapeDtypeStruct((T, D_MODEL), BF16),
                   jax.ShapeDtypeStruct((T, N_CHIPS * D_MODEL), BF16), jax.ShapeDtypeStruct((8, D_MODEL), F32),
                   jax.ShapeDtypeStruct((8, QKV), F32), jax.ShapeDtypeStruct((8, LANES), F32),
                   jax.ShapeDtypeStruct((8, LANES), F32)],
        scratch_shapes=[pltpu.VMEM((8, QKV), F32)],
        compiler_params=_params(("arbitrary",)),
    )(dq, dk, dv, dbg, qkv, c, bd, al_row, dt_row, dcv, dcv, dgb, sc_in, dz, cw, scw, dx1, x, g1, land_a)


def _wgrad_in_share(h, dps, parts, name):
    T = h.shape[0]
    bk = min(T, WGRAD_TOKENS)
    n_k = T // bk
    assert n_k >= 2

    def body(a_ref, b_ref, parts_ref, o_ref, acc_ref):
        kk = pl.program_id(1)
        product = lambda: lax.dot_general(a_ref[...], b_ref[...], TN, preferred_element_type=F32)

        @pl.when(kk == 0)
        def _():
            acc_ref[...] = product()

        @pl.when(jnp.logical_and(kk > 0, kk < n_k - 1))
        def _():
            acc_ref[...] += product()

        @pl.when(kk == n_k - 1)
        def _():
            o_ref[0] = (acc_ref[...] + product()).astype(BF16)

    return pl.pallas_call(
        body, name=name, grid=(N_CHIPS, n_k),
        in_specs=[pl.BlockSpec((bk, D_MODEL), lambda j, kk: (kk, 0)), pl.BlockSpec((bk, D_MODEL), lambda j, kk: (kk, j)), _ANY],
        out_specs=pl.BlockSpec((1, D_MODEL, D_MODEL), lambda j, kk: (j, 0, 0)),
        out_shape=jax.ShapeDtypeStruct(parts.shape, BF16),
        scratch_shapes=[pltpu.VMEM((D_MODEL, D_MODEL), F32)],
        input_output_aliases={2: 0},
        compiler_params=_params(("parallel", "arbitrary")),
    )(h, dps, parts)


def _pad_rows(a, rows=8):
    return jnp.pad(a, ((0, rows - a.shape[0]), (0, 0)))


def _gate_rows(a_log, dt_bias):
    put = lambda t: jnp.pad(t.reshape(1, HEADS), ((0, 0), (HEADS, LANES - 2 * HEADS)))
    return put(a_log), put(dt_bias)


def _mixer_fwd(x, p):
    qkv, z, sc_in, bd, h, q, k, v, bg, c = _in_proj(x, p["g1"], p["land_a"], p["cw"], p["al"], p["dt"])
    o, states, xms = _delta_fwd(q, k, v, bg)
    return dict(x=x, qkv=qkv, c=c, z=z, sc_in=sc_in, bd=bd, h=h, q=q, k=k, v=v, bg=bg, o=o, states=states, xms=xms)


def _tail_fwd(s, p, land_b):
    x1, mix, x2, a, b, h2 = _mix_ffn(s["o"], s["z"], s["sc_in"], s["x"], p["land_a"], p["gn"], p["scw"], p["gs"],
                                     p["g2"], land_b)
    return x2, dict(s, mix=mix), dict(x1=x1, a=a, b=b, h2=h2)


def _ffn_back(dx2, dx2_bf16, s, sm, p, land_b):
    dx1, da, db, act, dg2, *mid = _ffn_mix_bwd(dx2, s["x1"], s["a"], s["b"], p["g2"], land_b, sm["o"], sm["z"],
                                               sm["sc_in"], p["land_a"], p["gn"], p["scw"], p["gs"])
    parts = lax.empty((N_CHIPS, B_ROWS, D_MODEL), BF16)
    parts = _wgrad_share(act, dx2_bf16, parts, 2 * FF_SHARD, "wgrad_down")
    parts = _wgrad_share(da, s["h2"], parts, 0, "wgrad_gate")
    parts = _wgrad_share(db, s["h2"], parts, FF_SHARD, "wgrad_up")
    return dx1, parts, dg2[0], mid


def _mixer_bwd(dx1, mid, s, p, after):
    do, dz, dgb, dcv, dx1_bf16, dgn, dgs, dscw = mid
    dq, dk, dv, dbg = _delta_bwd(s["q"], s["k"], s["v"], s["bg"], s["states"], s["xms"], do, after)
    dx, dx_bf16, dps, dg1, dcw, dal, ddt = _in_proj_bwd(
        dq, dk, dv, dbg, s["qkv"], s["c"], s["bd"], p["al"], p["dt"], dcv, dgb, s["sc_in"], dz, p["cw"], p["scw"], dx1,
        s["x"], p["g1"], p["land_a"])
    parts = lax.empty((N_CHIPS, A_ROWS, D_MODEL), BF16)
    parts = _wgrad_in_share(s["h"], dps, parts, "wgrad_in")
    parts = _wgrad_share(s["mix"], dx1_bf16, parts, A_OUT_AT, "wgrad_out")
    g = dict(g1=dg1[0], gn=dgn[0], gs=dgs[0], scw=dscw[:3], cw=dcw[:4], al=dal[0, HEADS:2 * HEADS], dt=ddt[0, HEADS:2 * HEADS])
    return dx, dx_bf16, parts, g


def _place():
    return lax.axis_index("x"), lax.axis_index("y"), lax.axis_index("c")


def _other_chips(x, y):
    return [(1 - x, y), (x, 1 - y), (1 - x, 1 - y)]


_HBM = pl.BlockSpec(memory_space=pltpu.HBM)


def _gather_chips(arrs, name):
    n = len(arrs)

    def body(*refs):
        ins, outs = refs[:n], refs[n:2 * n]
        send_sems, recv_sems, local_sems = refs[2 * n:]
        x, y, c = _place()
        me = 2 * x + y
        others = _other_chips(x, y)

        def remote(k, j, landing):
            px, py = others[j]
            return pltpu.make_async_remote_copy(src_ref=ins[k], dst_ref=outs[k].at[landing], send_sem=send_sems.at[k, j],
                                                recv_sem=recv_sems.at[k, j], device_id=(px, py, c), device_id_type=MESH)

        local = [pltpu.make_async_copy(ins[k], outs[k].at[me], local_sems.at[k]) for k in range(n)]
        sends = [remote(k, j, me) for k in range(n) for j in range(3)]
        for cp in local + sends:
            cp.start()
        for k in range(n):
            for j, (px, py) in enumerate(others):
                remote(k, j, 2 * px + py).wait_recv()
        for cp in sends:
            cp.wait_send()
        for cp in local:
            cp.wait()

    shapes = [jax.ShapeDtypeStruct((N_CHIPS,) + a.shape, a.dtype) for a in arrs]
    return pl.pallas_call(
        body, name=name, in_specs=[_HBM] * n, out_specs=[_HBM] * n, out_shape=shapes,
        scratch_shapes=[pltpu.SemaphoreType.DMA((n, 3)), pltpu.SemaphoreType.DMA((n, 3)), pltpu.SemaphoreType.DMA((n,))],
    )(*arrs)


_SEM = pl.BlockSpec(memory_space=pltpu.SEMAPHORE)
_ANY = pl.BlockSpec(memory_space=pl.ANY)
_EFFECT = pltpu.SideEffectType.DATAFLOW_SIDE_EFFECTING


_FLIPS = [(a, b, cc) for a in (0, 1) for b in (0, 1) for cc in (0, 1)][1:]


def _split_copies(src_ref, land_ref, send_sems, recv_sems, gather, sending):
    x, y, c = _place()
    copies = []
    if gather:
        me = 2 * x + y
        for j, (px, py) in enumerate(_other_chips(x, y)):
            copies.append(pltpu.make_async_remote_copy(
                src_ref=src_ref, dst_ref=land_ref.at[me if sending else 2 * px + py],
                send_sem=send_sems.at[j], recv_sem=recv_sems.at[j], device_id=(px, py, c), device_id_type=MESH))
        return copies
    me = 4 * x + 2 * y + c
    for j, (a, b, cc) in enumerate(_FLIPS):
        px, py, pc = (1 - x) if a else x, (1 - y) if b else y, (1 - c) if cc else c
        copies.append(pltpu.make_async_remote_copy(
            src_ref=src_ref.at[2 * px + py], dst_ref=land_ref.at[me if sending else 4 * px + 2 * py + pc],
            send_sem=send_sems.at[j], recv_sem=recv_sems.at[j], device_id=(px, py, pc), device_id_type=MESH))
    return copies


def _own_slot(share):
    chip = 2 * lax.axis_index("x") + lax.axis_index("y")
    return lax.dynamic_update_slice(lax.empty((N_CHIPS,) + share.shape, share.dtype), share[None], (chip, 0, 0))


def _own_part(parts):
    chip = 2 * lax.axis_index("x") + lax.axis_index("y")
    own = lax.dynamic_index_in_dim(parts, chip, 0, keepdims=True)
    return lax.dynamic_update_slice(lax.empty((N_DEV,) + parts.shape[1:], parts.dtype), own,
                                    (2 * chip + lax.axis_index("c"), 0, 0))


def _exchange_start(src, land, after, name, gather):
    def body(src_ref, land_ref, after_ref, send_sems, recv_sems, src_thru, land_thru, token):
        for cp in _split_copies(src_ref, land_ref, send_sems, recv_sems, gather, sending=True):
            cp.start()
        token[...] = jnp.zeros_like(token)

    hbm = lambda t: pltpu.with_memory_space_constraint(t, pltpu.HBM)
    n_copies = N_CHIPS - 1 if gather else N_DEV - 1
    return pl.pallas_call(
        body, name=name,
        out_shape=(pltpu.SemaphoreType.DMA((n_copies,)), pltpu.SemaphoreType.DMA((n_copies,)), pltpu.HBM(src.shape, src.dtype),
                   pltpu.HBM(land.shape, land.dtype), jax.ShapeDtypeStruct((8, LANES), F32)),
        in_specs=(_HBM, _HBM, _ANY), out_specs=(_SEM, _SEM, _HBM, _HBM, pl.BlockSpec(memory_space=pltpu.VMEM)),
        input_output_aliases={0: 2, 1: 3},
        compiler_params=pltpu.CompilerParams(has_side_effects=_EFFECT),
    )(hbm(src), hbm(land), after)


def _exchange_wait(started, after, name, gather):
    send_sems, recv_sems, src_thru, land_thru, _ = started

    def body(src_ref, land_ref, send_sems, recv_sems, after_ref, src_dead, got_ref):
        for cp in _split_copies(src_ref, land_ref, send_sems, recv_sems, gather, sending=False):
            cp.wait_send()
            cp.wait_recv()

    return pl.pallas_call(
        body, name=name,
        out_shape=(pltpu.HBM(src_thru.shape, src_thru.dtype), pltpu.HBM(land_thru.shape, land_thru.dtype)),
        in_specs=(_HBM, _HBM, _SEM, _SEM, _ANY), out_specs=(_HBM, _HBM), input_output_aliases={0: 0, 1: 1},
        compiler_params=pltpu.CompilerParams(has_side_effects=_EFFECT),
    )(src_thru, land_thru, send_sems, recv_sems, after)[1]


def _all_reduce_small(v):
    rows = v.shape[0]
    flips = [(a, b, cc) for a in (0, 1) for b in (0, 1) for cc in (0, 1)][1:]

    def body(v_ref, out_ref, buf_ref, send_sems, recv_sems):
        x, y, c = _place()
        me = 4 * x + 2 * y + c
        peers = [((1 - x) if a else x, (1 - y) if b else y, (1 - c) if cc else c) for a, b, cc in flips]

        def copy(j, landing):
            return pltpu.make_async_remote_copy(src_ref=v_ref, dst_ref=buf_ref.at[landing], send_sem=send_sems.at[j],
                                                recv_sem=recv_sems.at[j], device_id=peers[j], device_id_type=MESH)

        sends = [copy(j, me) for j in range(N_DEV - 1)]
        for cp in sends:
            cp.start()
        buf_ref[me] = v_ref[...]
        for j, (px, py, pc) in enumerate(peers):
            copy(j, 4 * px + 2 * py + pc).wait_recv()
        for cp in sends:
            cp.wait_send()
        acc = buf_ref[0]
        for d in range(1, N_DEV):
            acc = acc + buf_ref[d]
        out_ref[...] = acc

    vmem = pl.BlockSpec(memory_space=pltpu.VMEM)
    return pl.pallas_call(
        body, name="all_reduce_small", in_specs=[vmem], out_specs=vmem,
        out_shape=jax.ShapeDtypeStruct(v.shape, F32),
        scratch_shapes=[pltpu.VMEM((N_DEV, rows, LANES), F32), pltpu.SemaphoreType.DMA((N_DEV - 1,)),
                        pltpu.SemaphoreType.DMA((N_DEV - 1,))],
    )(v)


def _row_block(*sizes):
    return next(t for t in (256, 176, 128, 64) if all(s % t == 0 for s in sizes))


def _adam_update(w, m, v, g):
    r1 = 1.0 / (1.0 - ADAM_B1 ** ADAM_STEP)
    r2 = 1.0 / (1.0 - ADAM_B2 ** ADAM_STEP)
    m_new = ADAM_B1 * m + (1.0 - ADAM_B1) * g
    v_new = ADAM_B2 * v + (1.0 - ADAM_B2) * (g * g)
    return -ADAM_LR * ((m_new * r1) / (jnp.sqrt(v_new * r2) + ADAM_EPS) + ADAM_WD * w), m_new, v_new


def _adamw_rows(w, m, v, got, first, name):
    n_layers, rows, cols = w.shape
    tr = _row_block(rows, first)

    def body(*refs):
        w_ref, m_ref, v_ref = refs[:3]
        g_out, d_out, m_out, v_out = refs[3 + n_layers:]
        for k in range(n_layers):
            @pl.when(pl.program_id(0) == k)
            def _(p_ref=refs[3 + k]):
                g = p_ref[0].astype(F32)
                for d in range(1, N_DEV):
                    g = g + p_ref[d].astype(F32)
                g = g[:, :cols]
                d_out[0], m_out[0], v_out[0] = _adam_update(w_ref[0], m_ref[0], v_ref[0], g)
                g_out[0] = g

    blk = pl.BlockSpec((1, tr, cols), lambda l, i: (l, i, 0))
    parts = [pl.BlockSpec((N_DEV, tr, got[0].shape[2]), lambda l, i, k=k: (0, jnp.where(l == k, first // tr + i, 0), 0))
             for k in range(n_layers)]
    return pl.pallas_call(
        body, name=name, grid=(n_layers, rows // tr),
        in_specs=[blk] * 3 + parts, out_specs=[blk] * 4,
        out_shape=[jax.ShapeDtypeStruct(w.shape, F32)] * 4,
        compiler_params=_params(("arbitrary", "arbitrary")),
    )(w, m, v, *got)


def _adamw(w, m, v, g_parts, name):
    rows, cols = w.shape
    tr = min(rows, 256)
    n = len(g_parts)

    def body(*refs):
        w_ref, m_ref, v_ref = refs[:3]
        g_refs = refs[3:3 + n]
        g_out, d_out, m_out, v_out = refs[3 + n:]
        g = g_refs[0][...]
        for r in g_refs[1:]:
            g = g + r[...]
        d_out[...], m_out[...], v_out[...] = _adam_update(w_ref[...], m_ref[...], v_ref[...], g)
        g_out[...] = g

    blk = pl.BlockSpec((tr, cols), lambda i: (i, 0))
    return pl.pallas_call(
        body, name=name, grid=(rows // tr,),
        in_specs=[blk] * (3 + n), out_specs=[blk] * 4,
        out_shape=[jax.ShapeDtypeStruct((rows, cols), F32)] * 4,
        compiler_params=_params(("parallel",)),
    )(w, m, v, *g_parts)


def _pack(parts, rows, fill=0.0):
    flat = jnp.concatenate([p.reshape(-1) for p in parts])
    return jnp.pad(flat, (0, rows * LANES - flat.shape[0]), constant_values=fill).reshape(rows, LANES)


def _unpack(packed, shapes):
    flat = packed.reshape(-1)
    out, at = [], 0
    for shp in shapes:
        size = 1
        for s in shp:
            size *= s
        out.append(flat[at:at + size].reshape(shp))
        at += size
    return out


def _packed_rows(shapes):
    total = 0
    for shp in shapes:
        size = 1
        for s in shp:
            size *= s
        total += size
    return -(-total // (8 * LANES)) * 8


def _cols_full(g, l):
    t = g[:, l]
    return jnp.moveaxis(t, 0, 1).reshape(t.shape[1], N_CHIPS * t.shape[2])


def _pad_cols(t):
    return jnp.pad(t, ((0, 0),) * (t.ndim - 1) + ((0, D_MODEL - t.shape[-1]),))


def kernel(x, norm1_g, w_in, dn_conv_w, dn_a_log, dn_dt_bias, dn_norm_g, sc_conv_w, sc_norm_g, w_out, norm2_g, ffn_w_gate, ffn_w_up, ffn_w_down, final_norm_g, loss_target, m_norm1_g, m_w_in, m_dn_conv_w, m_dn_a_log, m_dn_dt_bias, m_dn_norm_g, m_sc_conv_w, m_sc_norm_g, m_w_out, m_norm2_g, m_ffn_w_gate, m_ffn_w_up, m_ffn_w_down, m_final_norm_g, v_norm1_g, v_w_in, v_dn_conv_w, v_dn_a_log, v_dn_dt_bias, v_dn_norm_g, v_sc_conv_w, v_sc_norm_g, v_w_out, v_norm2_g, v_ffn_w_gate, v_ffn_w_up, v_ffn_w_down, v_final_norm_g):
    chip = 2 * lax.axis_index("x") + lax.axis_index("y")

    g_cw, g_scw = _gather_chips([dn_conv_w, sc_conv_w], "gather_conv")

    t_last = lambda t: jnp.swapaxes(t, -1, -2)
    gate_t, up_t = t_last(ffn_w_gate), t_last(ffn_w_up)
    zero_token = jnp.zeros((8, LANES), F32)

    def shares(l, tie):
        share_a = jnp.concatenate([_pad_cols(w_in[l] + tie), w_out[l]], axis=0).astype(BF16)
        share_b = jnp.concatenate([gate_t[l] + tie, up_t[l], ffn_w_down[l]], axis=0).astype(BF16)
        return share_a, _own_slot(share_a), share_b, _own_slot(share_b)

    def gather_start(l, packed, after):
        a = _exchange_start(packed[0], packed[1], after, "gather_a_start_%d" % l, gather=True)
        b = _exchange_start(packed[2], packed[3], a[4], "gather_b_start_%d" % l, gather=True)
        return a, b

    ga, gb = gather_start(0, shares(0, 0.0), g_cw)
    packed = [None] + [shares(l, gb[4][0, 0]) for l in range(1, DEPTH)]
    packed_all = sum(t[0, 0].astype(F32) for p in packed[1:] for t in (p[0], p[2]))
    land_a = _exchange_wait(ga, zero_token + packed_all, "gather_a_wait_0", gather=True)
    act = x[0]
    layers, saved_m, saved_f, lands_b = [], [], [], []
    for l in range(DEPTH):
        hold = 0.0
        if l + 1 < DEPTH:
            ga, gb_next = gather_start(l + 1, packed[l + 1], land_a)
            hold = gb_next[4][0:1, 0:1]
        al, dt = _gate_rows(dn_a_log[l], dn_dt_bias[l])
        layers.append(dict(
            g1=norm1_g[l][None] + hold, cw=_pad_rows(_cols_full(g_cw, l)), al=al, dt=dt,
            gn=dn_norm_g[l][None], scw=_pad_rows(_cols_full(g_scw, l)), gs=sc_norm_g[l][None],
            land_a=land_a, g2=norm2_g[l][None]))
        s = _mixer_fwd(act, layers[l])
        lands_b.append(_exchange_wait(gb, s["o"], "gather_b_wait_%d" % l, gather=True))
        act, s, sf = _tail_fwd(s, layers[l], lands_b[l])
        saved_m.append(s)
        saved_f.append(sf)
        if l + 1 < DEPTH:
            land_a = _exchange_wait(ga, act, "gather_a_wait_%d" % (l + 1), gather=True)
            gb = gb_next

    dact, dact_bf16, loss_part, d_final = _loss_head(act, final_norm_g[None], loss_target[0])
    grads, reduce_a, reduce_b = [None] * DEPTH, [None] * DEPTH, [None] * DEPTH
    hold = 0.0
    for l in reversed(range(DEPTH)):
        p = layers[l]
        dx1, parts, dg2, mid = _ffn_back(dact, dact_bf16, saved_f[l], saved_m[l], dict(p, g2=p["g2"] + hold), lands_b[l])
        reduce_b[l] = _exchange_start(parts, _own_part(parts), zero_token, "reduce_b_start_%d" % l, gather=False)
        dact, dact_bf16, parts, gm = _mixer_bwd(dx1, mid, saved_m[l], p, reduce_b[l][4][0:1, 0:1])
        reduce_a[l] = _exchange_start(parts, _own_part(parts), zero_token, "reduce_a_start_%d" % l, gather=False)
        hold = reduce_a[l][4][0:1, 0:1]
        grads[l] = dict(gm, g2=dg2)
    loss = lax.psum(loss_part[0, 0], ("x", "y", "c"))
    stack = lambda key: jnp.stack([grads[l][key] for l in range(DEPTH)])

    got_b = [_exchange_wait(reduce_b[l], reduce_a[0][4], "reduce_b_wait_%d" % l, gather=False)
             for l in reversed(range(DEPTH))][::-1]
    big = dict(
        ffn_w_gate=[t_last(o) for o in _adamw_rows(gate_t, t_last(m_ffn_w_gate), t_last(v_ffn_w_gate), got_b, 0, "adamw_gate")],
        ffn_w_up=[t_last(o) for o in _adamw_rows(up_t, t_last(m_ffn_w_up), t_last(v_ffn_w_up), got_b, FF_SHARD, "adamw_up")],
        ffn_w_down=_adamw_rows(ffn_w_down, m_ffn_w_down, v_ffn_w_down, got_b, 2 * FF_SHARD, "adamw_down"))
    after_b = zero_token + sum(big[n][1][0, 0, 0] for n in ("ffn_w_gate", "ffn_w_up", "ffn_w_down"))
    got_a = [_exchange_wait(reduce_a[l], after_b, "reduce_a_wait_%d" % l, gather=False) for l in reversed(range(DEPTH))][::-1]
    big.update(
        w_in=_adamw_rows(w_in, m_w_in, v_w_in, got_a, 0, "adamw_w_in"),
        w_out=_adamw_rows(w_out, m_w_out, v_w_out, got_a, A_OUT_AT, "adamw_w_out"))

    full_shapes = [(DEPTH, D_MODEL), (DEPTH, D_MODEL), (DEPTH, HEAD_DIM), (DEPTH, SC_WIDTH), (DEPTH, HEADS),
                   (DEPTH, HEADS), (D_MODEL,), (DEPTH, 4, QKV), (DEPTH, 3, SC_WIDTH)]
    small_keys = ("g1", "g2", "gn", "gs", "al", "dt")
    packed = _pack([stack(k) for k in small_keys] + [d_final[0], stack("cw"), stack("scw")], _packed_rows(full_shapes))
    sg = _unpack(_all_reduce_small(packed), full_shapes)
    sg[7] = lax.dynamic_slice_in_dim(sg[7], chip * (QKV // N_CHIPS), QKV // N_CHIPS, axis=2)
    sg[8] = lax.dynamic_slice_in_dim(sg[8], chip * (SC_WIDTH // N_CHIPS), SC_WIDTH // N_CHIPS, axis=2)
    small_names = ("norm1_g", "norm2_g", "dn_norm_g", "sc_norm_g", "dn_a_log", "dn_dt_bias", "final_norm_g",
                   "dn_conv_w", "sc_conv_w")
    sw = (norm1_g, norm2_g, dn_norm_g, sc_norm_g, dn_a_log, dn_dt_bias, final_norm_g, dn_conv_w, sc_conv_w)
    sm = (m_norm1_g, m_norm2_g, m_dn_norm_g, m_sc_norm_g, m_dn_a_log, m_dn_dt_bias, m_final_norm_g, m_dn_conv_w, m_sc_conv_w)
    sv = (v_norm1_g, v_norm2_g, v_dn_norm_g, v_sc_norm_g, v_dn_a_log, v_dn_dt_bias, v_final_norm_g, v_dn_conv_w, v_sc_conv_w)
    shard_shapes = [t.shape for t in sw]
    rows = _packed_rows(shard_shapes)
    outs = _adamw(_pack(sw, rows), _pack(sm, rows), _pack(sv, rows, fill=1.0), [_pack(sg, rows)], "adamw_small")
    small = {name: [] for name in small_names}
    for o in outs:
        for name, t in zip(small_names, _unpack(o, shard_shapes)):
            small[name].append(t)

    order = ("norm1_g", "w_in", "dn_conv_w", "dn_a_log", "dn_dt_bias", "dn_norm_g", "sc_conv_w", "sc_norm_g", "w_out",
             "norm2_g", "ffn_w_gate", "ffn_w_up", "ffn_w_down", "final_norm_g")
    result = {**big, **small}
    return (loss, dact[None], *[result[n][0] for n in order], *[result[n][1] for n in order],
            *[result[n][2] for n in order], *[result[n][3] for n in order])
```

```python
import jax
import jax.numpy as jnp
from jax import lax
from jax.experimental import pallas as pl
from jax.experimental.pallas import tpu as pltpu

F32 = jnp.float32
BF16 = jnp.bfloat16
MESH = pl.DeviceIdType.MESH

D_MODEL = 1024
DEPTH = 4
HEADS = 4
HEAD_DIM = 128
DN_WIDTH = HEADS * HEAD_DIM
SC_WIDTH = 512
SC_GROUPS = 4
D_FF = 2816
CHUNK = 64
QKV = 3 * DN_WIDTH
W_IN_COLS = 4 * DN_WIDTH + 2 * HEADS + 3 * SC_WIDTH
LANES = 128
EPS = 1e-6
Q_SCALE = HEAD_DIM ** -0.5
N_CHIPS = 4
N_DEV = 8
IN_SHARD = W_IN_COLS // N_CHIPS
OUT_SHARD = D_MODEL // N_CHIPS
FF_SHARD = D_FF // N_CHIPS
A_OUT_AT = D_MODEL
A_ROWS = D_MODEL + OUT_SHARD
B_ROWS = 3 * FF_SHARD

ADAM_LR = 0.001
ADAM_B1 = 0.9
ADAM_B2 = 0.999
ADAM_EPS = 1e-08
ADAM_WD = 0.01
ADAM_STEP = 10

VMEM_LIMIT = 60 * 1024 * 1024

NN = (((1,), (0,)), ((), ()))
NT = (((1,), (1,)), ((), ()))
TN = (((0,), (0,)), ((), ()))


def _mm(a, b, dims=NN):
    return lax.dot_general(a.astype(BF16), b.astype(BF16), dims, preferred_element_type=F32)


def _mm32(a, b, dims=NN):
    return lax.dot_general(a, b, dims, preferred_element_type=F32, precision=lax.Precision.HIGHEST)


def _params(sem, vmem=VMEM_LIMIT):
    return pltpu.CompilerParams(dimension_semantics=sem, vmem_limit_bytes=vmem)


def _sigmoid(x):
    return 0.5 * jnp.tanh(0.5 * x) + 0.5


def _softplus(x):
    return jnp.maximum(x, 0.0) + jnp.log1p(jnp.exp(-jnp.abs(x)))


def _row_acc(acc_ref, val):
    acc_ref[0:1, :] += jnp.sum(val, axis=0, keepdims=True)


def _rms_bwd(dh, xh, r, gain):
    dxh = dh * gain
    return r * (dxh - xh * jnp.mean(dxh * xh, axis=-1, keepdims=True))


def _before_halo(tb):
    return lambda i: (jnp.maximum(i * (tb // 8) - 1, 0), 0)


def _after_halo(tb, n_rows):
    last = n_rows // 8 - 1
    return lambda i: (jnp.minimum((i + 1) * (tb // 8), last), 0)


def _rows_from(xc, offset, tb):
    part = offset % 8
    if part:
        xc = pltpu.roll(xc, xc.shape[0] - part, 0)
    return xc[offset - part:offset - part + tb, :]


def _taps(xc, w, n_taps, tb, first):
    out = w[0:1, :] * _rows_from(xc, first, tb)
    for j in range(1, n_taps):
        out = out + w[j:j + 1, :] * _rows_from(xc, first + j, tb)
    return out


W_Z = QKV
W_BD = W_Z + DN_WIDTH
W_SC = W_BD + 2 * HEADS

def _w_in_cols(shards, lo, hi):
    pieces = []
    for s in range(N_CHIPS):
        a, b = max(lo, IN_SHARD * s), min(hi, IN_SHARD * (s + 1))
        if a < b:
            pieces.append(shards[s][:, a - IN_SHARD * s:b - IN_SHARD * s])
    return pieces[0] if len(pieces) == 1 else jnp.concatenate(pieces, axis=1)


def _in_proj(x, g1, land_a, cw, al_row, dt_row):
    T = x.shape[0]
    tb = 256

    def body(x_ref, g_ref, w_ref, cw_ref, al_ref, dt_ref,
             qkv_ref, z_ref, sc_ref, bd_ref, h_ref, q_ref, k_ref, v_ref, bg_ref, c_ref, tail_ref):
        @pl.when(pl.program_id(0) == 0)
        def _():
            tail_ref[...] = jnp.zeros_like(tail_ref)

        xv = x_ref[...]
        h = (xv * lax.rsqrt(jnp.mean(xv * xv, axis=-1, keepdims=True) + EPS) * g_ref[...]).astype(BF16)
        shards = [jnp.dot(h, w_ref[s], preferred_element_type=F32) for s in range(N_CHIPS)]
        qkv = _w_in_cols(shards, 0, W_Z)
        bd = jnp.concatenate([_w_in_cols(shards, W_BD, W_SC), jnp.zeros((tb, LANES - 2 * HEADS), F32)], axis=1)
        qkv_ref[...] = qkv
        z_ref[...] = _w_in_cols(shards, W_Z, W_BD)
        bd_ref[...] = bd
        sc_ref[...] = _w_in_cols(shards, W_SC, W_IN_COLS)
        h_ref[...] = h
        halo = tail_ref[...]
        tail_ref[...] = qkv[tb - 8:, :]
        _, c, _, a = _dn_act(qkv, halo, cw_ref[...], tb)
        c_ref[...] = c
        for hd in range(HEADS):
            sl = slice(HEAD_DIM * hd, HEAD_DIM * (hd + 1))
            qs = a[:, sl]
            q_ref[:, sl] = qs * (lax.rsqrt(jnp.sum(qs * qs, axis=-1, keepdims=True) + EPS) * Q_SCALE)
            ks = a[:, DN_WIDTH + HEAD_DIM * hd:DN_WIDTH + HEAD_DIM * (hd + 1)]
            k_ref[:, sl] = ks * lax.rsqrt(jnp.sum(ks * ks, axis=-1, keepdims=True) + EPS)
        v_ref[...] = a[:, 2 * DN_WIDTH:]
        gates = _gates(bd, al_ref[...], dt_ref[...])
        lane = lax.broadcasted_iota(jnp.int32, gates.shape, 1)
        bg_ref[...] = jnp.where(lane < HEADS, gates, _mm32(_chunk_cumsum_matrix(tb), gates))

    tok = lambda w: pl.BlockSpec((tb, w), lambda i: (i, 0))
    full = lambda t: pl.BlockSpec(t.shape, lambda i: (0, 0))
    return pl.pallas_call(
        body, name="in_proj", grid=(T // tb,),
        in_specs=[tok(D_MODEL), full(g1), _shard_rows(land_a, 0, D_MODEL), full(cw), full(al_row), full(dt_row)],
        out_specs=[tok(QKV), tok(DN_WIDTH), tok(3 * SC_WIDTH), tok(LANES), tok(D_MODEL),
                   tok(DN_WIDTH), tok(DN_WIDTH), tok(DN_WIDTH), tok(LANES), tok(QKV)],
        out_shape=[jax.ShapeDtypeStruct((T, QKV), F32), jax.ShapeDtypeStruct((T, DN_WIDTH), F32),
                   jax.ShapeDtypeStruct((T, 3 * SC_WIDTH), F32), jax.ShapeDtypeStruct((T, LANES), F32),
                   jax.ShapeDtypeStruct((T, D_MODEL), BF16)]
        + [jax.ShapeDtypeStruct((T, DN_WIDTH), F32)] * 3 + [jax.ShapeDtypeStruct((T, LANES), F32),
                                                              jax.ShapeDtypeStruct((T, QKV), F32)],
        scratch_shapes=[pltpu.VMEM((8, QKV), F32)],
        compiler_params=_params(("arbitrary",)),
    )(x, g1, land_a, cw, al_row, dt_row)


def _dn_act(pre, halo, cw, tb):
    xc = jnp.concatenate([halo, pre], axis=0)
    c = _taps(xc, cw, 4, tb, 5)
    sg = _sigmoid(c)
    return xc, c, sg, c * sg


def _gates(bd, al_row, dt_row):
    lane = lax.broadcasted_iota(jnp.int32, bd.shape, 1)
    beta = _sigmoid(bd)
    g = -jnp.exp(al_row) * _softplus(bd + dt_row)
    return jnp.where(lane < HEADS, beta, jnp.where(lane < 2 * HEADS, g, 0.0))


def _chunk_masks():
    row = lax.broadcasted_iota(jnp.int32, (CHUNK, CHUNK), 0)
    col = lax.broadcasted_iota(jnp.int32, (CHUNK, CHUNK), 1)
    return row >= col, row > col


def _chunk_cumsum_matrix(n):
    row = lax.broadcasted_iota(jnp.int32, (n, n), 0)
    col = lax.broadcasted_iota(jnp.int32, (n, n), 1)
    return jnp.logical_and(row >= col, row // CHUNK == col // CHUNK).astype(F32)


def _chunk_units(q_ref, k_ref, v_ref, bg_ref, rows):
    bgc = bg_ref[rows, :]
    bg_t = bgc.T
    qv, kv, vv = q_ref[rows, :], k_ref[rows, :], v_ref[rows, :]
    units = []
    for h in range(HEADS):
        sl = slice(HEAD_DIM * h, HEAD_DIM * (h + 1))
        units.append((qv[:, sl], kv[:, sl], vv[:, sl], bgc[:, h:h + 1], bgc[:, HEADS + h:HEADS + h + 1],
                      bg_t[HEADS + h:HEADS + h + 1, :]))
    return units


def _units_local(units, masks, xms=None):
    causal, strict = masks
    pre = []
    for q, k, v, beta, gc, gr in units:
        kb = k * beta
        eg = jnp.exp(gc)
        g_last = gc[CHUNK - 1:CHUNK, :]
        ek = jnp.exp(g_last - gc)
        pre.append(dict(q=q, k=k, v=v, beta=beta, decay=jnp.exp(jnp.where(causal, gc - gr, -1e30)), kb=kb, vb=v * beta,
                        eg=eg, kbg=kb * eg, ek=ek, gl=jnp.exp(g_last), q_dec=q * eg, k_dec=k * ek))
    both = [_mm(jnp.concatenate([p["kb"], p["q"]], axis=0), p["k"], NT) for p in pre]
    for p, b in zip(pre, both):
        p["low"] = jnp.where(strict, b[:CHUNK] * p["decay"], 0.0)
        p["qk"] = jnp.where(causal, b[CHUNK:] * p["decay"], 0.0)
    xs = xms
    if xs is None:
        xs = [-p["low"] for p in pre]
        pw = [_mm(p["low"], p["low"]) for p in pre]
        for _ in range(4):
            both = [_mm(jnp.concatenate([pp, x], axis=0), pp) for pp, x in zip(pw, xs)]
            xs = [x + pp + b[CHUNK:] for x, pp, b in zip(xs, pw, both)]
            pw = [b[:CHUNK] for b in both]
        last = [_mm(x, pp) for x, pp in zip(xs, pw)]
        xs = [x + pp + b for x, pp, b in zip(xs, pw, last)]
    uw = [_mm(x, jnp.concatenate([p["vb"], p["kbg"]], axis=1)) for x, p in zip(xs, pre)]
    for p, x, b in zip(pre, xs, uw):
        p["xm"] = x
        p["u"] = p["vb"] + b[:, :HEAD_DIM]
        p["w"] = p["kbg"] + b[:, HEAD_DIM:]
    return pre


FWD_GROUP = 8
BWD_GROUP = 8


def _delta_fwd(q, k, v, bg):
    T = q.shape[0]
    tb = 512
    n_chunk = tb // CHUNK

    def body(q_ref, k_ref, v_ref, bg_ref, o_ref, st_ref, xm_ref, s_ref):
        @pl.when(pl.program_id(0) == 0)
        def _():
            s_ref[...] = jnp.zeros_like(s_ref)

        masks = _chunk_masks()

        def group(gi, carry):
            rows = [pl.ds(pl.multiple_of((FWD_GROUP * gi + j) * CHUNK, CHUNK), CHUNK) for j in range(FWD_GROUP)]
            loc = _units_local(sum((_chunk_units(q_ref, k_ref, v_ref, bg_ref, r) for r in rows), []), masks)
            states = [s_ref[h] for h in range(HEADS)]
            for j in range(FWD_GROUP):
                lj = loc[HEADS * j:HEADS * (j + 1)]
                ws = [_mm(jnp.concatenate([p["w"], p["q_dec"]], axis=0), s) for p, s in zip(lj, states)]
                v_new = [p["u"] - b[:CHUNK] for p, b in zip(lj, ws)]
                intra = [_mm(p["qk"], vn) for p, vn in zip(lj, v_new)]
                upd = [_mm(p["k_dec"], vn, TN) for p, vn in zip(lj, v_new)]
                o_ref[rows[j], :] = jnp.concatenate([b[CHUNK:] + a for b, a in zip(ws, intra)], axis=1)
                for h in range(HEADS):
                    st_ref[FWD_GROUP * gi + j, h] = states[h]
                    xm_ref[FWD_GROUP * gi + j, h] = lj[h]["xm"]
                states = [p["gl"] * s + d for p, s, d in zip(lj, states, upd)]
            for h in range(HEADS):
                s_ref[h] = states[h]
            return carry

        lax.fori_loop(0, n_chunk // FWD_GROUP, group, 0)

    tok = lambda w: pl.BlockSpec((tb, w), lambda i: (i, 0))
    return pl.pallas_call(
        body, name="delta_fwd", grid=(T // tb,),
        in_specs=[tok(DN_WIDTH), tok(DN_WIDTH), tok(DN_WIDTH), tok(LANES)],
        out_specs=[tok(DN_WIDTH), pl.BlockSpec((n_chunk, HEADS, HEAD_DIM, HEAD_DIM), lambda i: (i, 0, 0, 0)),
                   pl.BlockSpec((n_chunk, HEADS, CHUNK, CHUNK), lambda i: (i, 0, 0, 0))],
        out_shape=[jax.ShapeDtypeStruct((T, DN_WIDTH), F32),
                   jax.ShapeDtypeStruct((T // CHUNK, HEADS, HEAD_DIM, HEAD_DIM), F32),
                   jax.ShapeDtypeStruct((T // CHUNK, HEADS, CHUNK, CHUNK), F32)],
        scratch_shapes=[pltpu.VMEM((HEADS, HEAD_DIM, HEAD_DIM), F32)],
        compiler_params=_params(("arbitrary",)),
    )(q, k, v, bg)


def _dn_out(o, z, gn):
    outs, ohs, rs = [], [], []
    for hh in range(HEADS):
        oh = o[:, HEAD_DIM * hh:HEAD_DIM * (hh + 1)]
        r = lax.rsqrt(jnp.mean(oh * oh, axis=-1, keepdims=True) + EPS)
        ohs.append(oh * r)
        rs.append(r)
    sz = _sigmoid(z)
    oh = jnp.concatenate(ohs, axis=1)
    gn4 = jnp.concatenate([gn] * HEADS, axis=1)
    return oh * gn4 * (z * sz), oh, rs, sz, gn4


def _sc_fwd(sc_in, halo, cw, tb):
    xc = jnp.concatenate([halo, sc_in], axis=0)
    u = xc[:, SC_WIDTH:2 * SC_WIDTH] * xc[:, 2 * SC_WIDTH:]
    cv = _taps(u, cw, 3, tb, 6)
    gate_b = sc_in[:, :SC_WIDTH]
    y = gate_b * cv
    gw = SC_WIDTH // SC_GROUPS
    yhs, rs = [], []
    for gi in range(SC_GROUPS):
        yg = y[:, gw * gi:gw * (gi + 1)]
        r = lax.rsqrt(jnp.mean(yg * yg, axis=-1, keepdims=True) + EPS)
        yhs.append(yg * r)
        rs.append(r)
    return u, cv, gate_b, jnp.concatenate(yhs, axis=1), rs


def _shard_rows(land, first, rows, single_buffer=False):
    assert first % rows == 0 and land.shape[0] == N_CHIPS
    mode = dict(pipeline_mode=pl.Buffered(1)) if single_buffer else {}
    return pl.BlockSpec((N_CHIPS, rows, land.shape[2]), lambda i: (0, first // rows, 0), **mode)


def _whole(w_ref):
    n, rows, cols = w_ref.shape
    return w_ref[...].reshape(n * rows, cols)


def _mix_ffn(o, z, sc_in, x, land_a, gn, scw, gs, g2, land_b):
    T = x.shape[0]
    tb = 256

    def body(o_ref, z_ref, sc_ref, halo_ref, x_ref, wo_ref, gn_ref, scw_ref, gs_ref, g2_ref, wgt_ref, wut_ref, wd_ref,
             x1_ref, mix_ref, x2_ref, a_ref, b_ref, h_ref):
        o_n = _dn_out(o_ref[...], z_ref[...], gn_ref[...])[0]
        halo = jnp.where(pl.program_id(0) > 0, halo_ref[...], 0.0)
        yh = _sc_fwd(sc_ref[...], halo, scw_ref[...], tb)[3]
        mix = jnp.concatenate([o_n, yh * gs_ref[...]], axis=1).astype(BF16)
        x1 = x_ref[...] + jnp.dot(mix, _whole(wo_ref), preferred_element_type=F32)
        x1_ref[...] = x1
        mix_ref[...] = mix
        r = lax.rsqrt(jnp.mean(x1 * x1, axis=-1, keepdims=True) + EPS)
        h = (x1 * r * g2_ref[...]).astype(BF16)
        a = lax.dot_general(h, _whole(wgt_ref), NT, preferred_element_type=F32)
        b = lax.dot_general(h, _whole(wut_ref), NT, preferred_element_type=F32)
        act = (a * _sigmoid(a) * b).astype(BF16)
        x2_ref[...] = x1 + jnp.dot(act, _whole(wd_ref), preferred_element_type=F32)
        a_ref[...] = a.astype(BF16)
        b_ref[...] = b.astype(BF16)
        h_ref[...] = h

    tok = lambda w: pl.BlockSpec((tb, w), lambda i: (i, 0))
    full = lambda t: pl.BlockSpec(t.shape, lambda i: (0, 0))
    once = lambda land, first, rows: _shard_rows(land, first, rows, single_buffer=True)
    return pl.pallas_call(
        body, name="mix_ffn", grid=(T // tb,),
        in_specs=[tok(DN_WIDTH), tok(DN_WIDTH), tok(3 * SC_WIDTH), pl.BlockSpec((8, 3 * SC_WIDTH), _before_halo(tb)),
                  tok(D_MODEL), once(land_a, A_OUT_AT, OUT_SHARD), full(gn), full(scw), full(gs), full(g2),
                  once(land_b, 0, FF_SHARD), once(land_b, FF_SHARD, FF_SHARD), once(land_b, 2 * FF_SHARD, FF_SHARD)],
        out_specs=[tok(D_MODEL), tok(D_MODEL), tok(D_MODEL), tok(D_FF), tok(D_FF), tok(D_MODEL)],
        out_shape=[jax.ShapeDtypeStruct((T, D_MODEL), F32), jax.ShapeDtypeStruct((T, D_MODEL), BF16),
                   jax.ShapeDtypeStruct((T, D_MODEL), F32), jax.ShapeDtypeStruct((T, D_FF), BF16),
                   jax.ShapeDtypeStruct((T, D_FF), BF16), jax.ShapeDtypeStruct((T, D_MODEL), BF16)],
        compiler_params=_params(("parallel",)),
    )(o, z, sc_in, sc_in, x, land_a, gn, scw, gs, g2, land_b, land_b, land_b)


def _loss_head(x, gf, target):
    T = x.shape[0]
    tb = 512

    def body(x_ref, g_ref, t_ref, dx_ref, dxb_ref, loss_ref, dg_ref):
        @pl.when(pl.program_id(0) == 0)
        def _():
            loss_ref[...] = jnp.zeros_like(loss_ref)
            dg_ref[...] = jnp.zeros_like(dg_ref)

        xv = x_ref[...]
        r = lax.rsqrt(jnp.mean(xv * xv, axis=-1, keepdims=True) + EPS)
        xh = xv * r
        err = xh * g_ref[...] - t_ref[...]
        per_tok = jnp.mean(err * err, axis=-1, keepdims=True)
        loss_ref[...] += 0.5 * jnp.sum(per_tok, axis=0, keepdims=True)
        dy = err * (1.0 / D_MODEL)
        _row_acc(dg_ref, dy * xh)
        dx = _rms_bwd(dy, xh, r, g_ref[...])
        dx_ref[...] = dx
        dxb_ref[...] = dx.astype(BF16)

    tok = pl.BlockSpec((tb, D_MODEL), lambda i: (i, 0))
    return pl.pallas_call(
        body, name="loss_head", grid=(T // tb,),
        in_specs=[tok, pl.BlockSpec(gf.shape, lambda i: (0, 0)), tok],
        out_specs=[tok, tok, pl.BlockSpec((8, LANES), lambda i: (0, 0)), pl.BlockSpec((8, D_MODEL), lambda i: (0, 0))],
        out_shape=[jax.ShapeDtypeStruct((T, D_MODEL), F32), jax.ShapeDtypeStruct((T, D_MODEL), BF16),
                   jax.ShapeDtypeStruct((8, LANES), F32), jax.ShapeDtypeStruct((8, D_MODEL), F32)],
        compiler_params=_params(("arbitrary",)),
    )(x, gf, target)


def _ffn_mix_bwd(dx2, x1, a, b, g2, land_b, o, z, sc_in, land_a, gn, scw, gs):
    T = x1.shape[0]
    tb = 256

    def body(dx2_ref, x_ref, a_ref, b_ref, g_ref, wgt_ref, wut_ref, wd_ref,
             o_ref, z_ref, sc_ref, halo_ref, w_ref, gn_ref, scw_ref, gs_ref,
             dx1_ref, da_ref, db_ref, act_ref, dg_ref,
             do_ref, dz_ref, dgb_ref, dcv_ref, dxb_ref, dgn_ref, dgs_ref, dscw_ref):
        @pl.when(pl.program_id(0) == 0)
        def _():
            for ref in (dg_ref, dgn_ref, dgs_ref, dscw_ref):
                ref[...] = jnp.zeros_like(ref)

        zv = z_ref[...]
        _, oh, rs, sz, gn4 = _dn_out(o_ref[...], zv, gn_ref[...])
        halo = jnp.where(pl.program_id(0) > 0, halo_ref[...], 0.0)
        u, cv, gate_b, yh, rys = _sc_fwd(sc_ref[...], halo, scw_ref[...], tb)

        dx2v = dx2_ref[...]
        av = a_ref[...].astype(F32)
        bv = b_ref[...].astype(F32)
        dact = _mm(dx2v, _whole(wd_ref), NT)
        sa = _sigmoid(av)
        silu = av * sa
        da = (dact * bv * (sa * (1.0 + av * (1.0 - sa)))).astype(BF16)
        db = (dact * silu).astype(BF16)
        dh = _mm(da, _whole(wgt_ref)) + _mm(db, _whole(wut_ref))
        xv = x_ref[...]
        r = lax.rsqrt(jnp.mean(xv * xv, axis=-1, keepdims=True) + EPS)
        xh = xv * r
        _row_acc(dg_ref, dh * xh)
        dx1 = dx2v + _rms_bwd(dh, xh, r, g_ref[...])
        dx1_ref[...] = dx1
        da_ref[...] = da
        db_ref[...] = db
        act_ref[...] = (silu * bv).astype(BF16)

        dx_bf16 = dx1.astype(BF16)
        dxb_ref[...] = dx_bf16
        dmix = lax.dot_general(dx_bf16, _whole(w_ref), NT, preferred_element_type=F32)
        don = dmix[:, :DN_WIDTH]
        dosc = dmix[:, DN_WIDTH:]
        silu_z = zv * sz
        dgn_full = don * oh * silu_z
        dgn_ref[0:1, :] += jnp.sum(sum(dgn_full[:, HEAD_DIM * hh:HEAD_DIM * (hh + 1)] for hh in range(HEADS)),
                                   axis=0, keepdims=True)
        dz_ref[...] = (don * oh * gn4 * (sz * (1.0 + zv * (1.0 - sz)))).astype(BF16)
        t = don * gn4 * silu_z
        for hh in range(HEADS):
            sl = slice(HEAD_DIM * hh, HEAD_DIM * (hh + 1))
            th, ohh = t[:, sl], oh[:, sl]
            do_ref[:, sl] = rs[hh] * (th - ohh * jnp.mean(th * ohh, axis=-1, keepdims=True))
        _row_acc(dgs_ref, dosc * yh)
        ty = dosc * gs_ref[...]
        gw = SC_WIDTH // SC_GROUPS
        dys = []
        for gi in range(SC_GROUPS):
            sl = slice(gw * gi, gw * (gi + 1))
            tg, yg = ty[:, sl], yh[:, sl]
            dys.append(rys[gi] * (tg - yg * jnp.mean(tg * yg, axis=-1, keepdims=True)))
        dy = jnp.concatenate(dys, axis=1)
        dgb_ref[...] = dy * cv
        dcv = dy * gate_b
        dcv_ref[...] = dcv
        for j in range(3):
            dscw_ref[j:j + 1, :] += jnp.sum(dcv * _rows_from(u, 6 + j, tb), axis=0, keepdims=True)

    tok = lambda w: pl.BlockSpec((tb, w), lambda i: (i, 0))
    full = lambda t: pl.BlockSpec(t.shape, lambda i: (0, 0))
    acc = lambda w: pl.BlockSpec((8, w), lambda i: (0, 0))
    once = lambda land, first, rows: _shard_rows(land, first, rows, single_buffer=True)
    return pl.pallas_call(
        body, name="ffn_mix_bwd", grid=(T // tb,),
        in_specs=[tok(D_MODEL), tok(D_MODEL), tok(D_FF), tok(D_FF), full(g2),
                  once(land_b, 0, FF_SHARD), once(land_b, FF_SHARD, FF_SHARD), once(land_b, 2 * FF_SHARD, FF_SHARD),
                  tok(DN_WIDTH), tok(DN_WIDTH), tok(3 * SC_WIDTH), pl.BlockSpec((8, 3 * SC_WIDTH), _before_halo(tb)),
                  once(land_a, A_OUT_AT, OUT_SHARD), full(gn), full(scw), full(gs)],
        out_specs=[tok(D_MODEL), tok(D_FF), tok(D_FF), tok(D_FF), acc(D_MODEL),
                   tok(DN_WIDTH), tok(DN_WIDTH), tok(SC_WIDTH), tok(SC_WIDTH), tok(D_MODEL),
                   acc(HEAD_DIM), acc(SC_WIDTH), acc(SC_WIDTH)],
        out_shape=[jax.ShapeDtypeStruct((T, D_MODEL), F32)]
        + [jax.ShapeDtypeStruct((T, D_FF), BF16)] * 3 + [jax.ShapeDtypeStruct((8, D_MODEL), F32)]
        + [jax.ShapeDtypeStruct((T, DN_WIDTH), F32), jax.ShapeDtypeStruct((T, DN_WIDTH), BF16),
           jax.ShapeDtypeStruct((T, SC_WIDTH), F32), jax.ShapeDtypeStruct((T, SC_WIDTH), F32),
           jax.ShapeDtypeStruct((T, D_MODEL), BF16),
           jax.ShapeDtypeStruct((8, HEAD_DIM), F32), jax.ShapeDtypeStruct((8, SC_WIDTH), F32),
           jax.ShapeDtypeStruct((8, SC_WIDTH), F32)],
        compiler_params=_params(("arbitrary",)),
    )(dx2, x1, a, b, g2, land_b, land_b, land_b, o, z, sc_in, sc_in, land_a, gn, scw, gs)


WGRAD_TOKENS = 2048


def _wgrad_share(a, b, parts, first, name):
    T = b.shape[0]
    rows = a.shape[1] // N_CHIPS
    assert first % rows == 0 and b.shape[1] == parts.shape[2]
    bk = min(T, WGRAD_TOKENS)
    n_k = T // bk
    group = N_CHIPS if N_CHIPS * rows <= D_MODEL else 2
    assert (group * rows) % LANES == 0 and n_k >= 2

    def body(a_ref, b_ref, parts_ref, o_ref, acc_ref):
        kk = pl.program_id(1)
        product = lambda: lax.dot_general(a_ref[...], b_ref[...], TN, preferred_element_type=F32)

        @pl.when(kk == 0)
        def _():
            acc_ref[...] = product()

        @pl.when(jnp.logical_and(kk > 0, kk < n_k - 1))
        def _():
            acc_ref[...] += product()

        @pl.when(kk == n_k - 1)
        def _():
            total = acc_ref[...] + product()
            for s in range(group):
                o_ref[s] = total[rows * s:rows * (s + 1), :].astype(BF16)

    return pl.pallas_call(
        body, name=name, grid=(N_CHIPS // group, n_k),
        in_specs=[pl.BlockSpec((bk, group * rows), lambda i, kk: (kk, i)),
                  pl.BlockSpec((bk, b.shape[1]), lambda i, kk: (kk, 0)), _ANY],
        out_specs=pl.BlockSpec((group, rows, b.shape[1]), lambda i, kk: (i, first // rows, 0)),
        out_shape=jax.ShapeDtypeStruct(parts.shape, BF16),
        scratch_shapes=[pltpu.VMEM((group * rows, b.shape[1]), F32)],
        input_output_aliases={2: 0},
        compiler_params=_params(("parallel", "arbitrary")),
    )(a, b, parts)


def _delta_bwd(q, k, v, bg, states, xms, do, after):
    T = q.shape[0]
    tb = 512
    n_chunk = tb // CHUNK
    nb = T // tb

    def body(q_ref, k_ref, v_ref, bg_ref, st_ref, xm_ref, do_ref, after_ref, dq_ref, dk_ref, dv_ref, dbg_ref, ds_ref):
        @pl.when(pl.program_id(0) == 0)
        def _():
            ds_ref[...] = jnp.zeros_like(ds_ref)

        masks = _chunk_masks()
        causal, strict = masks
        lane = lax.broadcasted_iota(jnp.int32, (CHUNK, LANES), 1)
        last_row = lax.broadcasted_iota(jnp.int32, (CHUNK, 1), 0) == CHUNK - 1
        cat = jnp.concatenate
        heads = range(HEADS)

        def open_chunk(ci, loc):
            rows = pl.ds(pl.multiple_of(ci * CHUNK, CHUNK), CHUNK)
            dov = do_ref[rows, :]
            return dict(rows=rows, loc=loc, do=[dov[:, HEAD_DIM * h:HEAD_DIM * (h + 1)] for h in heads],
                        state=[st_ref[ci, h] for h in heads])

        def a_free(c):
            loc, do, state = c["loc"], c["do"], c["state"]
            w_s = [_mm(p["w"], s) for p, s in zip(loc, state)]
            c["dq_dec"] = [_mm(d, s, NT) for d, s in zip(do, state)]
            c["qk_do"] = [_mm(p["qk"], d, TN) for p, d in zip(loc, do)]
            c["qd_do"] = [_mm(p["q_dec"], d, TN) for p, d in zip(loc, do)]
            c["v_new"] = [p["u"] - t for p, t in zip(loc, w_s)]
            c["dqk"] = [jnp.where(causal, _mm(d, vn, NT), 0.0) for d, vn in zip(do, c["v_new"])]

        def a_state(c, ds_next):
            c["ds_next"] = ds_next
            kd_ds = [_mm(p["k_dec"], d) for p, d in zip(c["loc"], ds_next)]
            c["dk_dec"] = [_mm(vn, d, NT) for vn, d in zip(c["v_new"], ds_next)]
            c["dv_new"] = [a + b for a, b in zip(c["qk_do"], kd_ds)]

        def b_state(c):
            loc = c["loc"]
            w_dv = [_mm(p["w"], dvn, TN) for p, dvn in zip(loc, c["dv_new"])]
            c["dw"] = [-_mm(dvn, s, NT) for dvn, s in zip(c["dv_new"], c["state"])]
            return [loc[h]["gl"] * c["ds_next"][h] + c["qd_do"][h] - w_dv[h] for h in heads]

        def c_solve(c):
            loc, dv_new, dw = c["loc"], c["dv_new"], c["dw"]
            c["dtm"] = [_mm(cat([dvn, d], axis=1), cat([p["vb"], p["kbg"]], axis=1), NT) for dvn, d, p in zip(dv_new, dw, loc)]
            x_t = [_mm(p["xm"], cat([dvn, d], axis=1), TN) for p, dvn, d in zip(loc, dv_new, dw)]
            c["dvb"] = [dvn + t[:, :HEAD_DIM] for dvn, t in zip(dv_new, x_t)]
            c["dkbg"] = [d + t[:, HEAD_DIM:] for d, t in zip(dw, x_t)]

        def d_solve(c):
            c["y"] = [t + _mm(p["xm"], t, TN) for p, t in zip(c["loc"], c["dtm"])]

        def e_solve(c):
            c["dlow"] = [jnp.where(strict, -(t + _mm(t, p["xm"], NT)), 0.0) for p, t in zip(c["loc"], c["y"])]

        def f_close(c):
            loc, rows = c["loc"], c["rows"]
            dmm = [d * p["decay"] for d, p in zip(c["dlow"], loc)]
            dnn = [d * p["decay"] for d, p in zip(c["dqk"], loc)]
            by_k = [_mm(cat([a, b], axis=0), p["k"]) for a, b, p in zip(dmm, dnn, loc)]
            dk_mm = [_mm(cat([a, b], axis=0), cat([p["kb"], p["q"]], axis=0), TN) for a, b, p in zip(dmm, dnn, loc)]
            dq_out, dk_out, dv_out = [], [], []
            dbeta_all = jnp.zeros((CHUNK, LANES), F32)
            dgc_all = jnp.zeros((CHUNK, LANES), F32)
            for h in heads:
                p = loc[h]
                dkb = by_k[h][:CHUNK] + c["dkbg"][h] * p["eg"]
                dq_out.append(by_k[h][CHUNK:] + c["dq_dec"][h] * p["eg"])
                dk_out.append(dk_mm[h] + c["dk_dec"][h] * p["ek"] + dkb * p["beta"])
                dv_out.append(c["dvb"][h] * p["beta"])
                dbeta = jnp.sum(dkb * p["k"] + c["dvb"][h] * p["v"], axis=1, keepdims=True)
                e = c["dlow"][h] * p["low"] + c["dqk"][h] * p["qk"]
                kd = jnp.sum(c["dk_dec"][h] * p["k_dec"], axis=1, keepdims=True)
                dgc = (jnp.sum(e, axis=1, keepdims=True) - jnp.sum(e.T, axis=1, keepdims=True)
                       + jnp.sum(c["dq_dec"][h] * p["q_dec"], axis=1, keepdims=True) - kd
                       + jnp.sum(c["dkbg"][h] * p["kbg"], axis=1, keepdims=True))
                dgl = jnp.sum(jnp.sum(c["ds_next"][h] * c["state"][h], axis=1, keepdims=True), axis=0, keepdims=True)
                d_last = jnp.sum(kd, axis=0, keepdims=True) + dgl * p["gl"]
                dgc = dgc + jnp.where(last_row, d_last, 0.0)
                dbeta_all = jnp.where(lane == h, dbeta, dbeta_all)
                dgc_all = jnp.where(lane == h + HEADS, dgc, dgc_all)
            dq_ref[rows, :] = cat(dq_out, axis=1)
            dk_ref[rows, :] = cat(dk_out, axis=1)
            dv_ref[rows, :] = cat(dv_out, axis=1)
            dbg_ref[rows, :] = dbeta_all + dgc_all

        def group(gj, carry):
            first = n_chunk - 1 - BWD_GROUP * gj
            ids = [first - j for j in range(BWD_GROUP)]
            rows = [pl.ds(pl.multiple_of(ci * CHUNK, CHUNK), CHUNK) for ci in ids]
            loc = _units_local(sum((_chunk_units(q_ref, k_ref, v_ref, bg_ref, r) for r in rows), []), masks,
                               xms=[xm_ref[ci, h] for ci in ids for h in heads])
            chunks = [open_chunk(ci, loc[HEADS * j:HEADS * (j + 1)]) for j, ci in enumerate(ids)]
            for c in chunks:
                a_free(c)
            ds_cur = [ds_ref[h] for h in heads]
            later = (c_solve, d_solve, e_solve, f_close)
            for t in range(2 * (BWD_GROUP - 1) + 2 + len(later)):
                for j, c in enumerate(chunks):
                    stage = t - 2 * j
                    if stage == 0:
                        a_state(c, ds_cur)
                    elif stage == 1:
                        ds_cur = b_state(c)
                    elif 2 <= stage < 2 + len(later):
                        later[stage - 2](c)
            for h in heads:
                ds_ref[h] = ds_cur[h]
            return carry

        lax.fori_loop(0, n_chunk // BWD_GROUP, group, 0)

    tok = lambda w: pl.BlockSpec((tb, w), lambda i: (nb - 1 - i, 0))
    return pl.pallas_call(
        body, name="delta_bwd", grid=(nb,),
        in_specs=[tok(DN_WIDTH), tok(DN_WIDTH), tok(DN_WIDTH), tok(LANES),
                  pl.BlockSpec((n_chunk, HEADS, HEAD_DIM, HEAD_DIM), lambda i: (nb - 1 - i, 0, 0, 0)),
                  pl.BlockSpec((n_chunk, HEADS, CHUNK, CHUNK), lambda i: (nb - 1 - i, 0, 0, 0)), tok(DN_WIDTH),
                  _ANY],
        out_specs=[tok(DN_WIDTH), tok(DN_WIDTH), tok(DN_WIDTH), tok(LANES)],
        out_shape=[jax.ShapeDtypeStruct((T, DN_WIDTH), F32)] * 3 + [jax.ShapeDtypeStruct((T, LANES), F32)],
        scratch_shapes=[pltpu.VMEM((HEADS, HEAD_DIM, HEAD_DIM), F32)],
        compiler_params=_params(("arbitrary",)),
    )(q, k, v, bg, states, xms, do, after)


def _dn_prep_back(dq, dk, dv, dbg, c, bd, al_row, dt_row, tb):
    sg = _sigmoid(c)
    a = c * sg
    dsilu = sg * (1.0 + c * (1.0 - sg))
    pieces = [None] * (2 * HEADS)
    for hd in range(HEADS):
        sl = slice(HEAD_DIM * hd, HEAD_DIM * (hd + 1))
        for which, (base, grad, scale) in enumerate(((0, dq, Q_SCALE), (DN_WIDTH, dk, 1.0))):
            sa = slice(base + HEAD_DIM * hd, base + HEAD_DIM * (hd + 1))
            raw = a[:, sa]
            r = lax.rsqrt(jnp.sum(raw * raw, axis=-1, keepdims=True) + EPS)
            nrm = raw * r
            gn_ = grad[:, sl] * scale
            pieces[which * HEADS + hd] = r * (gn_ - nrm * jnp.sum(gn_ * nrm, axis=-1, keepdims=True)) * dsilu[:, sa]
    dc = jnp.concatenate(pieces + [dv * dsilu[:, 2 * DN_WIDTH:]], axis=1)
    lane = lax.broadcasted_iota(jnp.int32, bd.shape, 1)
    is_b = lane < HEADS
    is_g = jnp.logical_and(lane >= HEADS, lane < 2 * HEADS)
    dbgv = jnp.where(is_b, dbg, _mm32(_chunk_cumsum_matrix(tb), dbg, TN))
    beta = _sigmoid(bd)
    neg_a = -jnp.exp(al_row)
    pre_sp = bd + dt_row
    g = neg_a * _softplus(pre_sp)
    da_in = dbgv * neg_a * _sigmoid(pre_sp)
    dbd = jnp.where(is_b, dbgv * beta * (1.0 - beta), jnp.where(is_g, da_in, 0.0)).astype(BF16)
    dal_row = jnp.sum(jnp.where(is_g, dbgv * g, 0.0), axis=0, keepdims=True)
    ddt_row = jnp.sum(jnp.where(is_g, da_in, 0.0), axis=0, keepdims=True)
    return dc, dbd, dal_row, ddt_row


def _dp_of_chip(dqkv, dz, dbd, dsc, s):
    lo, hi = IN_SHARD * s, IN_SHARD * (s + 1)
    pieces = []
    for w_at, w_end, block in ((0, W_Z, dqkv), (W_Z, W_BD, dz), (W_BD, W_SC, dbd), (W_SC, W_IN_COLS, dsc)):
        a, b = max(lo, w_at), min(hi, w_end)
        if a < b:
            pieces.append(block[:, a - w_at:b - w_at])
    pieces.append(jnp.zeros((dqkv.shape[0], D_MODEL - IN_SHARD), dqkv.dtype))
    return jnp.concatenate(pieces, axis=1)


def _in_proj_bwd(dq, dk, dv, dbg, qkv, c, bd, al_row, dt_row, dcv, dgb, sc_in, dz, cw, scw, dx1, x, g1, land_a):
    T = x.shape[0]
    tb = 256
    nb = T // tb

    def body(dq_ref, dk_ref, dv_ref, dbg_ref, pre_ref, c_ref, bd_ref, al_ref, dt_ref,
             dcv_ref, dcv_halo_ref, dgb_ref, sc_ref, dz_ref, cw_ref, scw_ref, dx1_ref, x_ref, g_ref, w_ref,
             dx_ref, dxb_ref, dps_ref, dg_ref, dcw_ref, dal_ref, ddt_ref, head_ref):
        @pl.when(pl.program_id(0) == 0)
        def _():
            for ref in (dg_ref, dcw_ref, dal_ref, ddt_ref, head_ref):
                ref[...] = jnp.zeros_like(ref)

        block = nb - 1 - pl.program_id(0)
        last = block == nb - 1
        dc, dbd, dal_row, ddt_row = _dn_prep_back(
            dq_ref[...], dk_ref[...], dv_ref[...], dbg_ref[...], c_ref[...], bd_ref[...], al_ref[...], dt_ref[...], tb)
        dal_ref[0:1, :] += dal_row
        ddt_ref[0:1, :] += ddt_row
        xc = jnp.concatenate([dc, head_ref[...]], axis=0)
        head_ref[...] = dc[0:8, :]
        w4 = cw_ref[...]
        pre = pre_ref[...]
        dqkv = None
        for j in range(4):
            later = xc[0:tb, :] if j == 3 else _rows_from(xc, 3 - j, tb)
            dqkv = w4[j:j + 1, :] * later if dqkv is None else dqkv + w4[j:j + 1, :] * later
            dcw_ref[j:j + 1, :] += jnp.sum(later * pre, axis=0, keepdims=True)
        yc = jnp.concatenate([dcv_ref[...], jnp.where(last, 0.0, dcv_halo_ref[...])], axis=0)
        w3 = scw_ref[...]
        du = w3[2:3, :] * yc[0:tb, :] + w3[1:2, :] * _rows_from(yc, 1, tb) + w3[0:1, :] * _rows_from(yc, 2, tb)
        sc = sc_ref[...]
        dsc = jnp.concatenate([dgb_ref[...], du * sc[:, 2 * SC_WIDTH:], du * sc[:, SC_WIDTH:2 * SC_WIDTH]], axis=1)
        blocks = (dqkv.astype(BF16), dz_ref[...], dbd, dsc.astype(BF16))
        dh = jnp.zeros((tb, D_MODEL), F32)
        for s in range(N_CHIPS):
            dps = _dp_of_chip(*blocks, s)
            dps_ref[:, D_MODEL * s:D_MODEL * (s + 1)] = dps
            dh = dh + lax.dot_general(dps, w_ref[s], NT, preferred_element_type=F32)
        xv = x_ref[...]
        r = lax.rsqrt(jnp.mean(xv * xv, axis=-1, keepdims=True) + EPS)
        xh = xv * r
        _row_acc(dg_ref, dh * xh)
        dx = dx1_ref[...] + _rms_bwd(dh, xh, r, g_ref[...])
        dx_ref[...] = dx
        dxb_ref[...] = dx.astype(BF16)

    tok = lambda w: pl.BlockSpec((tb, w), lambda i: (nb - 1 - i, 0))
    full = lambda t: pl.BlockSpec(t.shape, lambda i: (0, 0))
    acc = lambda w: pl.BlockSpec((8, w), lambda i: (0, 0))
    after = lambda w: pl.BlockSpec((8, w), lambda i: _after_halo(tb, T)(nb - 1 - i))
    return pl.pallas_call(
        body, name="in_proj_bwd", grid=(nb,),
        in_specs=[tok(DN_WIDTH), tok(DN_WIDTH), tok(DN_WIDTH), tok(LANES), tok(QKV), tok(QKV), tok(LANES),
                  full(al_row), full(dt_row), tok(SC_WIDTH), after(SC_WIDTH), tok(SC_WIDTH), tok(3 * SC_WIDTH),
                  tok(DN_WIDTH), full(cw), full(scw), tok(D_MODEL), tok(D_MODEL), full(g1), _shard_rows(land_a, 0, D_MODEL)],
        out_specs=[tok(D_MODEL), tok(D_MODEL), tok(N_CHIPS * D_MODEL), acc(D_MODEL), acc(QKV), acc(LANES), acc(LANES)],
        out_shape=[jax.ShapeDtypeStruct((T, D_MODEL), F32), jax.ShapeDtypeStruct((T, D_MODEL), BF16),
                   jax.ShapeDtypeStruct((T, N_CHIPS * D_MODEL), BF16), jax.ShapeDtypeStruct((8, D_MODEL), F32),
                   jax.ShapeDtypeStruct((8, QKV), F32), jax.ShapeDtypeStruct((8, LANES), F32),
                   jax.ShapeDtypeStruct((8, LANES), F32)],
        scratch_shapes=[pltpu.VMEM((8, QKV), F32)],
        compiler_params=_params(("arbitrary",)),
    )(dq, dk, dv, dbg, qkv, c, bd, al_row, dt_row, dcv, dcv, dgb, sc_in, dz, cw, scw, dx1, x, g1, land_a)


def _wgrad_in_share(h, dps, parts, name):
    T = h.shape[0]
    bk = min(T, WGRAD_TOKENS)
    n_k = T // bk
    assert n_k >= 2

    def body(a_ref, b_ref, parts_ref, o_ref, acc_ref):
        kk = pl.program_id(1)
        product = lambda: lax.dot_general(a_ref[...], b_ref[...], TN, preferred_element_type=F32)

        @pl.when(kk == 0)
        def _():
            acc_ref[...] = product()

        @pl.when(jnp.logical_and(kk > 0, kk < n_k - 1))
        def _():
            acc_ref[...] += product()

        @pl.when(kk == n_k - 1)
        def _():
            o_ref[0] = (acc_ref[...] + product()).astype(BF16)

    return pl.pallas_call(
        body, name=name, grid=(N_CHIPS, n_k),
        in_specs=[pl.BlockSpec((bk, D_MODEL), lambda j, kk: (kk, 0)), pl.BlockSpec((bk, D_MODEL), lambda j, kk: (kk, j)), _ANY],
        out_specs=pl.BlockSpec((1, D_MODEL, D_MODEL), lambda j, kk: (j, 0, 0)),
        out_shape=jax.ShapeDtypeStruct(parts.shape, BF16),
        scratch_shapes=[pltpu.VMEM((D_MODEL, D_MODEL), F32)],
        input_output_aliases={2: 0},
        compiler_params=_params(("parallel", "arbitrary")),
    )(h, dps, parts)


def _pad_rows(a, rows=8):
    return jnp.pad(a, ((0, rows - a.shape[0]), (0, 0)))


def _gate_rows(a_log, dt_bias):
    put = lambda t: jnp.pad(t.reshape(1, HEADS), ((0, 0), (HEADS, LANES - 2 * HEADS)))
    return put(a_log), put(dt_bias)


def _mixer_fwd(x, p):
    qkv, z, sc_in, bd, h, q, k, v, bg, c = _in_proj(x, p["g1"], p["land_a"], p["cw"], p["al"], p["dt"])
    o, states, xms = _delta_fwd(q, k, v, bg)
    return dict(x=x, qkv=qkv, c=c, z=z, sc_in=sc_in, bd=bd, h=h, q=q, k=k, v=v, bg=bg, o=o, states=states, xms=xms)


def _tail_fwd(s, p, land_b):
    x1, mix, x2, a, b, h2 = _mix_ffn(s["o"], s["z"], s["sc_in"], s["x"], p["land_a"], p["gn"], p["scw"], p["gs"],
                                     p["g2"], land_b)
    return x2, dict(s, mix=mix), dict(x1=x1, a=a, b=b, h2=h2)


def _ffn_back(dx2, dx2_bf16, s, sm, p, land_b):
    dx1, da, db, act, dg2, *mid = _ffn_mix_bwd(dx2, s["x1"], s["a"], s["b"], p["g2"], land_b, sm["o"], sm["z"],
                                               sm["sc_in"], p["land_a"], p["gn"], p["scw"], p["gs"])
    parts = lax.empty((N_CHIPS, B_ROWS, D_MODEL), BF16)
    parts = _wgrad_share(act, dx2_bf16, parts, 2 * FF_SHARD, "wgrad_down")
    parts = _wgrad_share(da, s["h2"], parts, 0, "wgrad_gate")
    parts = _wgrad_share(db, s["h2"], parts, FF_SHARD, "wgrad_up")
    return dx1, parts, dg2[0], mid


def _mixer_bwd(dx1, mid, s, p, after):
    do, dz, dgb, dcv, dx1_bf16, dgn, dgs, dscw = mid
    dq, dk, dv, dbg = _delta_bwd(s["q"], s["k"], s["v"], s["bg"], s["states"], s["xms"], do, after)
    dx, dx_bf16, dps, dg1, dcw, dal, ddt = _in_proj_bwd(
        dq, dk, dv, dbg, s["qkv"], s["c"], s["bd"], p["al"], p["dt"], dcv, dgb, s["sc_in"], dz, p["cw"], p["scw"], dx1,
        s["x"], p["g1"], p["land_a"])
    parts = lax.empty((N_CHIPS, A_ROWS, D_MODEL), BF16)
    parts = _wgrad_in_share(s["h"], dps, parts, "wgrad_in")
    parts = _wgrad_share(s["mix"], dx1_bf16, parts, A_OUT_AT, "wgrad_out")
    g = dict(g1=dg1[0], gn=dgn[0], gs=dgs[0], scw=dscw[:3], cw=dcw[:4], al=dal[0, HEADS:2 * HEADS], dt=ddt[0, HEADS:2 * HEADS])
    return dx, dx_bf16, parts, g


def _place():
    return lax.axis_index("x"), lax.axis_index("y"), lax.axis_index("c")


def _other_chips(x, y):
    return [(1 - x, y), (x, 1 - y), (1 - x, 1 - y)]


_HBM = pl.BlockSpec(memory_space=pltpu.HBM)


def _gather_chips(arrs, name):
    n = len(arrs)

    def body(*refs):
        ins, outs = refs[:n], refs[n:2 * n]
        send_sems, recv_sems, local_sems = refs[2 * n:]
        x, y, c = _place()
        me = 2 * x + y
        others = _other_chips(x, y)

        def remote(k, j, landing):
            px, py = others[j]
            return pltpu.make_async_remote_copy(src_ref=ins[k], dst_ref=outs[k].at[landing], send_sem=send_sems.at[k, j],
                                                recv_sem=recv_sems.at[k, j], device_id=(px, py, c), device_id_type=MESH)

        local = [pltpu.make_async_copy(ins[k], outs[k].at[me], local_sems.at[k]) for k in range(n)]
        sends = [remote(k, j, me) for k in range(n) for j in range(3)]
        for cp in local + sends:
            cp.start()
        for k in range(n):
            for j, (px, py) in enumerate(others):
                remote(k, j, 2 * px + py).wait_recv()
        for cp in sends:
            cp.wait_send()
        for cp in local:
            cp.wait()

    shapes = [jax.ShapeDtypeStruct((N_CHIPS,) + a.shape, a.dtype) for a in arrs]
    return pl.pallas_call(
        body, name=name, in_specs=[_HBM] * n, out_specs=[_HBM] * n, out_shape=shapes,
        scratch_shapes=[pltpu.SemaphoreType.DMA((n, 3)), pltpu.SemaphoreType.DMA((n, 3)), pltpu.SemaphoreType.DMA((n,))],
    )(*arrs)


_SEM = pl.BlockSpec(memory_space=pltpu.SEMAPHORE)
_ANY = pl.BlockSpec(memory_space=pl.ANY)
_EFFECT = pltpu.SideEffectType.DATAFLOW_SIDE_EFFECTING


_FLIPS = [(a, b, cc) for a in (0, 1) for b in (0, 1) for cc in (0, 1)][1:]


def _split_copies(src_ref, land_ref, send_sems, recv_sems, gather, sending):
    x, y, c = _place()
    copies = []
    if gather:
        me = 2 * x + y
        for j, (px, py) in enumerate(_other_chips(x, y)):
            copies.append(pltpu.make_async_remote_copy(
                src_ref=src_ref, dst_ref=land_ref.at[me if sending else 2 * px + py],
                send_sem=send_sems.at[j], recv_sem=recv_sems.at[j], device_id=(px, py, c), device_id_type=MESH))
        return copies
    me = 4 * x + 2 * y + c
    for j, (a, b, cc) in enumerate(_FLIPS):
        px, py, pc = (1 - x) if a else x, (1 - y) if b else y, (1 - c) if cc else c
        copies.append(pltpu.make_async_remote_copy(
            src_ref=src_ref.at[2 * px + py], dst_ref=land_ref.at[me if sending else 4 * px + 2 * py + pc],
            send_sem=send_sems.at[j], recv_sem=recv_sems.at[j], device_id=(px, py, pc), device_id_type=MESH))
    return copies


def _own_slot(share):
    chip = 2 * lax.axis_index("x") + lax.axis_index("y")
    return lax.dynamic_update_slice(lax.empty((N_CHIPS,) + share.shape, share.dtype), share[None], (chip, 0, 0))


def _own_part(parts):
    chip = 2 * lax.axis_index("x") + lax.axis_index("y")
    own = lax.dynamic_index_in_dim(parts, chip, 0, keepdims=True)
    return lax.dynamic_update_slice(lax.empty((N_DEV,) + parts.shape[1:], parts.dtype), own,
                                    (2 * chip + lax.axis_index("c"), 0, 0))


def _exchange_start(src, land, after, name, gather):
    def body(src_ref, land_ref, after_ref, send_sems, recv_sems, src_thru, land_thru, token):
        for cp in _split_copies(src_ref, land_ref, send_sems, recv_sems, gather, sending=True):
            cp.start()
        token[...] = jnp.zeros_like(token)

    hbm = lambda t: pltpu.with_memory_space_constraint(t, pltpu.HBM)
    n_copies = N_CHIPS - 1 if gather else N_DEV - 1
    return pl.pallas_call(
        body, name=name,
        out_shape=(pltpu.SemaphoreType.DMA((n_copies,)), pltpu.SemaphoreType.DMA((n_copies,)), pltpu.HBM(src.shape, src.dtype),
                   pltpu.HBM(land.shape, land.dtype), jax.ShapeDtypeStruct((8, LANES), F32)),
        in_specs=(_HBM, _HBM, _ANY), out_specs=(_SEM, _SEM, _HBM, _HBM, pl.BlockSpec(memory_space=pltpu.VMEM)),
        input_output_aliases={0: 2, 1: 3},
        compiler_params=pltpu.CompilerParams(has_side_effects=_EFFECT),
    )(hbm(src), hbm(land), after)


def _exchange_wait(started, after, name, gather):
    send_sems, recv_sems, src_thru, land_thru, _ = started

    def body(src_ref, land_ref, send_sems, recv_sems, after_ref, src_dead, got_ref):
        for cp in _split_copies(src_ref, land_ref, send_sems, recv_sems, gather, sending=False):
            cp.wait_send()
            cp.wait_recv()

    return pl.pallas_call(
        body, name=name,
        out_shape=(pltpu.HBM(src_thru.shape, src_thru.dtype), pltpu.HBM(land_thru.shape, land_thru.dtype)),
        in_specs=(_HBM, _HBM, _SEM, _SEM, _ANY), out_specs=(_HBM, _HBM), input_output_aliases={0: 0, 1: 1},
        compiler_params=pltpu.CompilerParams(has_side_effects=_EFFECT),
    )(src_thru, land_thru, send_sems, recv_sems, after)[1]


def _all_reduce_small(v):
    rows = v.shape[0]
    flips = [(a, b, cc) for a in (0, 1) for b in (0, 1) for cc in (0, 1)][1:]

    def body(v_ref, out_ref, buf_ref, send_sems, recv_sems):
        x, y, c = _place()
        me = 4 * x + 2 * y + c
        peers = [((1 - x) if a else x, (1 - y) if b else y, (1 - c) if cc else c) for a, b, cc in flips]

        def copy(j, landing):
            return pltpu.make_async_remote_copy(src_ref=v_ref, dst_ref=buf_ref.at[landing], send_sem=send_sems.at[j],
                                                recv_sem=recv_sems.at[j], device_id=peers[j], device_id_type=MESH)

        sends = [copy(j, me) for j in range(N_DEV - 1)]
        for cp in sends:
            cp.start()
        buf_ref[me] = v_ref[...]
        for j, (px, py, pc) in enumerate(peers):
            copy(j, 4 * px + 2 * py + pc).wait_recv()
        for cp in sends:
            cp.wait_send()
        acc = buf_ref[0]
        for d in range(1, N_DEV):
            acc = acc + buf_ref[d]
        out_ref[...] = acc

    vmem = pl.BlockSpec(memory_space=pltpu.VMEM)
    return pl.pallas_call(
        body, name="all_reduce_small", in_specs=[vmem], out_specs=vmem,
        out_shape=jax.ShapeDtypeStruct(v.shape, F32),
        scratch_shapes=[pltpu.VMEM((N_DEV, rows, LANES), F32), pltpu.SemaphoreType.DMA((N_DEV - 1,)),
                        pltpu.SemaphoreType.DMA((N_DEV - 1,))],
    )(v)


def _row_block(*sizes):
    return next(t for t in (256, 176, 128, 64) if all(s % t == 0 for s in sizes))


def _adam_update(w, m, v, g):
    r1 = 1.0 / (1.0 - ADAM_B1 ** ADAM_STEP)
    r2 = 1.0 / (1.0 - ADAM_B2 ** ADAM_STEP)
    m_new = ADAM_B1 * m + (1.0 - ADAM_B1) * g
    v_new = ADAM_B2 * v + (1.0 - ADAM_B2) * (g * g)
    return -ADAM_LR * ((m_new * r1) / (jnp.sqrt(v_new * r2) + ADAM_EPS) + ADAM_WD * w), m_new, v_new


def _adamw_rows(w, m, v, got, first, name):
    n_layers, rows, cols = w.shape
    tr = _row_block(rows, first)

    def body(*refs):
        w_ref, m_ref, v_ref = refs[:3]
        g_out, d_out, m_out, v_out = refs[3 + n_layers:]
        for k in range(n_layers):
            @pl.when(pl.program_id(0) == k)
            def _(p_ref=refs[3 + k]):
                g = p_ref[0].astype(F32)
                for d in range(1, N_DEV):
                    g = g + p_ref[d].astype(F32)
                g = g[:, :cols]
                d_out[0], m_out[0], v_out[0] = _adam_update(w_ref[0], m_ref[0], v_ref[0], g)
                g_out[0] = g

    blk = pl.BlockSpec((1, tr, cols), lambda l, i: (l, i, 0))
    parts = [pl.BlockSpec((N_DEV, tr, got[0].shape[2]), lambda l, i, k=k: (0, jnp.where(l == k, first // tr + i, 0), 0))
             for k in range(n_layers)]
    return pl.pallas_call(
        body, name=name, grid=(n_layers, rows // tr),
        in_specs=[blk] * 3 + parts, out_specs=[blk] * 4,
        out_shape=[jax.ShapeDtypeStruct(w.shape, F32)] * 4,
        compiler_params=_params(("arbitrary", "arbitrary")),
    )(w, m, v, *got)


def _adamw(w, m, v, g_parts, name):
    rows, cols = w.shape
    tr = min(rows, 256)
    n = len(g_parts)

    def body(*refs):
        w_ref, m_ref, v_ref = refs[:3]
        g_refs = refs[3:3 + n]
        g_out, d_out, m_out, v_out = refs[3 + n:]
        g = g_refs[0][...]
        for r in g_refs[1:]:
            g = g + r[...]
        d_out[...], m_out[...], v_out[...] = _adam_update(w_ref[...], m_ref[...], v_ref[...], g)
        g_out[...] = g

    blk = pl.BlockSpec((tr, cols), lambda i: (i, 0))
    return pl.pallas_call(
        body, name=name, grid=(rows // tr,),
        in_specs=[blk] * (3 + n), out_specs=[blk] * 4,
        out_shape=[jax.ShapeDtypeStruct((rows, cols), F32)] * 4,
        compiler_params=_params(("parallel",)),
    )(w, m, v, *g_parts)


def _pack(parts, rows, fill=0.0):
    flat = jnp.concatenate([p.reshape(-1) for p in parts])
    return jnp.pad(flat, (0, rows * LANES - flat.shape[0]), constant_values=fill).reshape(rows, LANES)


def _unpack(packed, shapes):
    flat = packed.reshape(-1)
    out, at = [], 0
    for shp in shapes:
        size = 1
        for s in shp:
            size *= s
        out.append(flat[at:at + size].reshape(shp))
        at += size
    return out


def _packed_rows(shapes):
    total = 0
    for shp in shapes:
        size = 1
        for s in shp:
            size *= s
        total += size
    return -(-total // (8 * LANES)) * 8


def _cols_full(g, l):
    t = g[:, l]
    return jnp.moveaxis(t, 0, 1).reshape(t.shape[1], N_CHIPS * t.shape[2])


def _pad_cols(t):
    return jnp.pad(t, ((0, 0),) * (t.ndim - 1) + ((0, D_MODEL - t.shape[-1]),))


def kernel(x, norm1_g, w_in, dn_conv_w, dn_a_log, dn_dt_bias, dn_norm_g, sc_conv_w, sc_norm_g, w_out, norm2_g, ffn_w_gate, ffn_w_up, ffn_w_down, final_norm_g, loss_target, m_norm1_g, m_w_in, m_dn_conv_w, m_dn_a_log, m_dn_dt_bias, m_dn_norm_g, m_sc_conv_w, m_sc_norm_g, m_w_out, m_norm2_g, m_ffn_w_gate, m_ffn_w_up, m_ffn_w_down, m_final_norm_g, v_norm1_g, v_w_in, v_dn_conv_w, v_dn_a_log, v_dn_dt_bias, v_dn_norm_g, v_sc_conv_w, v_sc_norm_g, v_w_out, v_norm2_g, v_ffn_w_gate, v_ffn_w_up, v_ffn_w_down, v_final_norm_g):
    chip = 2 * lax.axis_index("x") + lax.axis_index("y")

    g_cw, g_scw = _gather_chips([dn_conv_w, sc_conv_w], "gather_conv")

    t_last = lambda t: jnp.swapaxes(t, -1, -2)
    gate_t, up_t = t_last(ffn_w_gate), t_last(ffn_w_up)
    zero_token = jnp.zeros((8, LANES), F32)

    def shares(l, tie):
        share_a = jnp.concatenate([_pad_cols(w_in[l] + tie), w_out[l]], axis=0).astype(BF16)
        share_b = jnp.concatenate([gate_t[l] + tie, up_t[l], ffn_w_down[l]], axis=0).astype(BF16)
        return share_a, _own_slot(share_a), share_b, _own_slot(share_b)

    def gather_start(l, packed, after):
        a = _exchange_start(packed[0], packed[1], after, "gather_a_start_%d" % l, gather=True)
        b = _exchange_start(packed[2], packed[3], a[4], "gather_b_start_%d" % l, gather=True)
        return a, b

    ga, gb = gather_start(0, shares(0, 0.0), g_cw)
    packed = [None] + [shares(l, gb[4][0, 0]) for l in range(1, DEPTH)]
    packed_all = sum(t[0, 0].astype(F32) for p in packed[1:] for t in (p[0], p[2]))
    land_a = _exchange_wait(ga, zero_token + packed_all, "gather_a_wait_0", gather=True)
    act = x[0]
    layers, saved_m, saved_f, lands_b = [], [], [], []
    for l in range(DEPTH):
        hold = 0.0
        if l + 1 < DEPTH:
            ga, gb_next = gather_start(l + 1, packed[l + 1], land_a)
            hold = gb_next[4][0:1, 0:1]
        al, dt = _gate_rows(dn_a_log[l], dn_dt_bias[l])
        layers.append(dict(
            g1=norm1_g[l][None] + hold, cw=_pad_rows(_cols_full(g_cw, l)), al=al, dt=dt,
            gn=dn_norm_g[l][None], scw=_pad_rows(_cols_full(g_scw, l)), gs=sc_norm_g[l][None],
            land_a=land_a, g2=norm2_g[l][None]))
        s = _mixer_fwd(act, layers[l])
        lands_b.append(_exchange_wait(gb, s["o"], "gather_b_wait_%d" % l, gather=True))
        act, s, sf = _tail_fwd(s, layers[l], lands_b[l])
        saved_m.append(s)
        saved_f.append(sf)
        if l + 1 < DEPTH:
            land_a = _exchange_wait(ga, act, "gather_a_wait_%d" % (l + 1), gather=True)
            gb = gb_next

    dact, dact_bf16, loss_part, d_final = _loss_head(act, final_norm_g[None], loss_target[0])
    grads, reduce_a, reduce_b = [None] * DEPTH, [None] * DEPTH, [None] * DEPTH
    hold = 0.0
    for l in reversed(range(DEPTH)):
        p = layers[l]
        dx1, parts, dg2, mid = _ffn_back(dact, dact_bf16, saved_f[l], saved_m[l], dict(p, g2=p["g2"] + hold), lands_b[l])
        reduce_b[l] = _exchange_start(parts, _own_part(parts), zero_token, "reduce_b_start_%d" % l, gather=False)
        dact, dact_bf16, parts, gm = _mixer_bwd(dx1, mid, saved_m[l], p, reduce_b[l][4][0:1, 0:1])
        reduce_a[l] = _exchange_start(parts, _own_part(parts), zero_token, "reduce_a_start_%d" % l, gather=False)
        hold = reduce_a[l][4][0:1, 0:1]
        grads[l] = dict(gm, g2=dg2)
    loss = lax.psum(loss_part[0, 0], ("x", "y", "c"))
    stack = lambda key: jnp.stack([grads[l][key] for l in range(DEPTH)])

    got_b = [_exchange_wait(reduce_b[l], reduce_a[0][4], "reduce_b_wait_%d" % l, gather=False)
             for l in reversed(range(DEPTH))][::-1]
    big = dict(
        ffn_w_gate=[t_last(o) for o in _adamw_rows(gate_t, t_last(m_ffn_w_gate), t_last(v_ffn_w_gate), got_b, 0, "adamw_gate")],
        ffn_w_up=[t_last(o) for o in _adamw_rows(up_t, t_last(m_ffn_w_up), t_last(v_ffn_w_up), got_b, FF_SHARD, "adamw_up")],
        ffn_w_down=_adamw_rows(ffn_w_down, m_ffn_w_down, v_ffn_w_down, got_b, 2 * FF_SHARD, "adamw_down"))
    after_b = zero_token + sum(big[n][1][0, 0, 0] for n in ("ffn_w_gate", "ffn_w_up", "ffn_w_down"))
    got_a = [_exchange_wait(reduce_a[l], after_b, "reduce_a_wait_%d" % l, gather=False) for l in reversed(range(DEPTH))][::-1]
    big.update(
        w_in=_adamw_rows(w_in, m_w_in, v_w_in, got_a, 0, "adamw_w_in"),
        w_out=_adamw_rows(w_out, m_w_out, v_w_out, got_a, A_OUT_AT, "adamw_w_out"))

    full_shapes = [(DEPTH, D_MODEL), (DEPTH, D_MODEL), (DEPTH, HEAD_DIM), (DEPTH, SC_WIDTH), (DEPTH, HEADS),
                   (DEPTH, HEADS), (D_MODEL,), (DEPTH, 4, QKV), (DEPTH, 3, SC_WIDTH)]
    small_keys = ("g1", "g2", "gn", "gs", "al", "dt")
    packed = _pack([stack(k) for k in small_keys] + [d_final[0], stack("cw"), stack("scw")], _packed_rows(full_shapes))
    sg = _unpack(_all_reduce_small(packed), full_shapes)
    sg[7] = lax.dynamic_slice_in_dim(sg[7], chip * (QKV // N_CHIPS), QKV // N_CHIPS, axis=2)
    sg[8] = lax.dynamic_slice_in_dim(sg[8], chip * (SC_WIDTH // N_CHIPS), SC_WIDTH // N_CHIPS, axis=2)
    small_names = ("norm1_g", "norm2_g", "dn_norm_g", "sc_norm_g", "dn_a_log", "dn_dt_bias", "final_norm_g",
                   "dn_conv_w", "sc_conv_w")
    sw = (norm1_g, norm2_g, dn_norm_g, sc_norm_g, dn_a_log, dn_dt_bias, final_norm_g, dn_conv_w, sc_conv_w)
    sm = (m_norm1_g, m_norm2_g, m_dn_norm_g, m_sc_norm_g, m_dn_a_log, m_dn_dt_bias, m_final_norm_g, m_dn_conv_w, m_sc_conv_w)
    sv = (v_norm1_g, v_norm2_g, v_dn_norm_g, v_sc_norm_g, v_dn_a_log, v_dn_dt_bias, v_final_norm_g, v_dn_conv_w, v_sc_conv_w)
    shard_shapes = [t.shape for t in sw]
    rows = _packed_rows(shard_shapes)
    outs = _adamw(_pack(sw, rows), _pack(sm, rows), _pack(sv, rows, fill=1.0), [_pack(sg, rows)], "adamw_small")
    small = {name: [] for name in small_names}
    for o in outs:
        for name, t in zip(small_names, _unpack(o, shard_shapes)):
            small[name].append(t)

    order = ("norm1_g", "w_in", "dn_conv_w", "dn_a_log", "dn_dt_bias", "dn_norm_g", "sc_conv_w", "sc_norm_g", "w_out",
             "norm2_g", "ffn_w_gate", "ffn_w_up", "ffn_w_down", "final_norm_g")
    result = {**big, **small}
    return (loss, dact[None], *[result[n][0] for n in order], *[result[n][1] for n in order],
            *[result[n][2] for n in order], *[result[n][3] for n in order])
```

```python
import jax
import jax.numpy as jnp
from jax import lax
from jax.experimental import pallas as pl
from jax.experimental.pallas import tpu as pltpu

F32 = jnp.float32
BF16 = jnp.bfloat16
MESH = pl.DeviceIdType.MESH

D_MODEL = 1024
DEPTH = 4
HEADS = 4
HEAD_DIM = 128
DN_WIDTH = HEADS * HEAD_DIM
SC_WIDTH = 512
SC_GROUPS = 4
D_FF = 2816
CHUNK = 64
QKV = 3 * DN_WIDTH
W_IN_COLS = 4 * DN_WIDTH + 2 * HEADS + 3 * SC_WIDTH
LANES = 128
EPS = 1e-6
Q_SCALE = HEAD_DIM ** -0.5
N_CHIPS = 4
N_DEV = 8
IN_SHARD = W_IN_COLS // N_CHIPS
OUT_SHARD = D_MODEL // N_CHIPS
FF_SHARD = D_FF // N_CHIPS
A_OUT_AT = D_MODEL
A_ROWS = D_MODEL + OUT_SHARD
B_ROWS = 3 * FF_SHARD

ADAM_LR = 0.001
ADAM_B1 = 0.9
ADAM_B2 = 0.999
ADAM_EPS = 1e-08
ADAM_WD = 0.01
ADAM_STEP = 10

VMEM_LIMIT = 60 * 1024 * 1024

NN = (((1,), (0,)), ((), ()))
NT = (((1,), (1,)), ((), ()))
TN = (((0,), (0,)), ((), ()))


def _mm(a, b, dims=NN):
    return lax.dot_general(a.astype(BF16), b.astype(BF16), dims, preferred_element_type=F32)


def _mm32(a, b, dims=NN):
    return lax.dot_general(a, b, dims, preferred_element_type=F32, precision=lax.Precision.HIGHEST)


def _params(sem, vmem=VMEM_LIMIT):
    return pltpu.CompilerParams(dimension_semantics=sem, vmem_limit_bytes=vmem)


def _sigmoid(x):
    return 0.5 * jnp.tanh(0.5 * x) + 0.5


def _softplus(x):
    return jnp.maximum(x, 0.0) + jnp.log1p(jnp.exp(-jnp.abs(x)))


def _row_acc(acc_ref, val):
    acc_ref[0:1, :] += jnp.sum(val, axis=0, keepdims=True)


def _rms_bwd(dh, xh, r, gain):
    dxh = dh * gain
    return r * (dxh - xh * jnp.mean(dxh * xh, axis=-1, keepdims=True))


def _before_halo(tb):
    return lambda i: (jnp.maximum(i * (tb // 8) - 1, 0), 0)


def _after_halo(tb, n_rows):
    last = n_rows // 8 - 1
    return lambda i: (jnp.minimum((i + 1) * (tb // 8), last), 0)


def _rows_from(xc, offset, tb):
    part = offset % 8
    if part:
        xc = pltpu.roll(xc, xc.shape[0] - part, 0)
    return xc[offset - part:offset - part + tb, :]


def _taps(xc, w, n_taps, tb, first):
    out = w[0:1, :] * _rows_from(xc, first, tb)
    for j in range(1, n_taps):
        out = out + w[j:j + 1, :] * _rows_from(xc, first + j, tb)
    return out


W_Z = QKV
W_BD = W_Z + DN_WIDTH
W_SC = W_BD + 2 * HEADS

def _w_in_cols(shards, lo, hi):
    pieces = []
    for s in range(N_CHIPS):
        a, b = max(lo, IN_SHARD * s), min(hi, IN_SHARD * (s + 1))
        if a < b:
            pieces.append(shards[s][:, a - IN_SHARD * s:b - IN_SHARD * s])
    return pieces[0] if len(pieces) == 1 else jnp.concatenate(pieces, axis=1)


def _in_proj(x, g1, land_a, cw, al_row, dt_row):
    T = x.shape[0]
    tb = 256

    def body(x_ref, g_ref, w_ref, cw_ref, al_ref, dt_ref,
             qkv_ref, z_ref, sc_ref, bd_ref, h_ref, q_ref, k_ref, v_ref, bg_ref, c_ref, tail_ref):
        @pl.when(pl.program_id(0) == 0)
        def _():
            tail_ref[...] = jnp.zeros_like(tail_ref)

        xv = x_ref[...]
        h = (xv * lax.rsqrt(jnp.mean(xv * xv, axis=-1, keepdims=True) + EPS) * g_ref[...]).astype(BF16)
        shards = [jnp.dot(h, w_ref[s], preferred_element_type=F32) for s in range(N_CHIPS)]
        qkv = _w_in_cols(shards, 0, W_Z)
        bd = jnp.concatenate([_w_in_cols(shards, W_BD, W_SC), jnp.zeros((tb, LANES - 2 * HEADS), F32)], axis=1)
        qkv_ref[...] = qkv
        z_ref[...] = _w_in_cols(shards, W_Z, W_BD)
        bd_ref[...] = bd
        sc_ref[...] = _w_in_cols(shards, W_SC, W_IN_COLS)
        h_ref[...] = h
        halo = tail_ref[...]
        tail_ref[...] = qkv[tb - 8:, :]
        _, c, _, a = _dn_act(qkv, halo, cw_ref[...], tb)
        c_ref[...] = c
        for hd in range(HEADS):
            sl = slice(HEAD_DIM * hd, HEAD_DIM * (hd + 1))
            qs = a[:, sl]
            q_ref[:, sl] = qs * (lax.rsqrt(jnp.sum(qs * qs, axis=-1, keepdims=True) + EPS) * Q_SCALE)
            ks = a[:, DN_WIDTH + HEAD_DIM * hd:DN_WIDTH + HEAD_DIM * (hd + 1)]
            k_ref[:, sl] = ks * lax.rsqrt(jnp.sum(ks * ks, axis=-1, keepdims=True) + EPS)
        v_ref[...] = a[:, 2 * DN_WIDTH:]
        gates = _gates(bd, al_ref[...], dt_ref[...])
        lane = lax.broadcasted_iota(jnp.int32, gates.shape, 1)
        bg_ref[...] = jnp.where(lane < HEADS, gates, _mm32(_chunk_cumsum_matrix(tb), gates))

    tok = lambda w: pl.BlockSpec((tb, w), lambda i: (i, 0))
    full = lambda t: pl.BlockSpec(t.shape, lambda i: (0, 0))
    return pl.pallas_call(
        body, name="in_proj", grid=(T // tb,),
        in_specs=[tok(D_MODEL), full(g1), _shard_rows(land_a, 0, D_MODEL), full(cw), full(al_row), full(dt_row)],
        out_specs=[tok(QKV), tok(DN_WIDTH), tok(3 * SC_WIDTH), tok(LANES), tok(D_MODEL),
                   tok(DN_WIDTH), tok(DN_WIDTH), tok(DN_WIDTH), tok(LANES), tok(QKV)],
        out_shape=[jax.ShapeDtypeStruct((T, QKV), F32), jax.ShapeDtypeStruct((T, DN_WIDTH), F32),
                   jax.ShapeDtypeStruct((T, 3 * SC_WIDTH), F32), jax.ShapeDtypeStruct((T, LANES), F32),
                   jax.ShapeDtypeStruct((T, D_MODEL), BF16)]
        + [jax.ShapeDtypeStruct((T, DN_WIDTH), F32)] * 3 + [jax.ShapeDtypeStruct((T, LANES), F32),
                                                              jax.ShapeDtypeStruct((T, QKV), F32)],
        scratch_shapes=[pltpu.VMEM((8, QKV), F32)],
        compiler_params=_params(("arbitrary",)),
    )(x, g1, land_a, cw, al_row, dt_row)


def _dn_act(pre, halo, cw, tb):
    xc = jnp.concatenate([halo, pre], axis=0)
    c = _taps(xc, cw, 4, tb, 5)
    sg = _sigmoid(c)
    return xc, c, sg, c * sg


def _gates(bd, al_row, dt_row):
    lane = lax.broadcasted_iota(jnp.int32, bd.shape, 1)
    beta = _sigmoid(bd)
    g = -jnp.exp(al_row) * _softplus(bd + dt_row)
    return jnp.where(lane < HEADS, beta, jnp.where(lane < 2 * HEADS, g, 0.0))


def _chunk_masks():
    row = lax.broadcasted_iota(jnp.int32, (CHUNK, CHUNK), 0)
    col = lax.broadcasted_iota(jnp.int32, (CHUNK, CHUNK), 1)
    return row >= col, row > col


def _chunk_cumsum_matrix(n):
    row = lax.broadcasted_iota(jnp.int32, (n, n), 0)
    col = lax.broadcasted_iota(jnp.int32, (n, n), 1)
    return jnp.logical_and(row >= col, row // CHUNK == col // CHUNK).astype(F32)


def _chunk_units(q_ref, k_ref, v_ref, bg_ref, rows):
    bgc = bg_ref[rows, :]
    bg_t = bgc.T
    qv, kv, vv = q_ref[rows, :], k_ref[rows, :], v_ref[rows, :]
    units = []
    for h in range(HEADS):
        sl = slice(HEAD_DIM * h, HEAD_DIM * (h + 1))
        units.append((qv[:, sl], kv[:, sl], vv[:, sl], bgc[:, h:h + 1], bgc[:, HEADS + h:HEADS + h + 1],
                      bg_t[HEADS + h:HEADS + h + 1, :]))
    return units


def _units_local(units, masks, xms=None):
    causal, strict = masks
    pre = []
    for q, k, v, beta, gc, gr in units:
        kb = k * beta
        eg = jnp.exp(gc)
        g_last = gc[CHUNK - 1:CHUNK, :]
        ek = jnp.exp(g_last - gc)
        pre.append(dict(q=q, k=k, v=v, beta=beta, decay=jnp.exp(jnp.where(causal, gc - gr, -1e30)), kb=kb, vb=v * beta,
                        eg=eg, kbg=kb * eg, ek=ek, gl=jnp.exp(g_last), q_dec=q * eg, k_dec=k * ek))
    both = [_mm(jnp.concatenate([p["kb"], p["q"]], axis=0), p["k"], NT) for p in pre]
    for p, b in zip(pre, both):
        p["low"] = jnp.where(strict, b[:CHUNK] * p["decay"], 0.0)
        p["qk"] = jnp.where(causal, b[CHUNK:] * p["decay"], 0.0)
    xs = xms
    if xs is None:
        xs = [-p["low"] for p in pre]
        pw = [_mm(p["low"], p["low"]) for p in pre]
        for _ in range(4):
            both = [_mm(jnp.concatenate([pp, x], axis=0), pp) for pp, x in zip(pw, xs)]
            xs = [x + pp + b[CHUNK:] for x, pp, b in zip(xs, pw, both)]
            pw = [b[:CHUNK] for b in both]
        last = [_mm(x, pp) for x, pp in zip(xs, pw)]
        xs = [x + pp + b for x, pp, b in zip(xs, pw, last)]
    uw = [_mm(x, jnp.concatenate([p["vb"], p["kbg"]], axis=1)) for x, p in zip(xs, pre)]
    for p, x, b in zip(pre, xs, uw):
        p["xm"] = x
        p["u"] = p["vb"] + b[:, :HEAD_DIM]
        p["w"] = p["kbg"] + b[:, HEAD_DIM:]
    return pre


FWD_GROUP = 8
BWD_GROUP = 8


def _delta_fwd(q, k, v, bg):
    T = q.shape[0]
    tb = 512
    n_chunk = tb // CHUNK

    def body(q_ref, k_ref, v_ref, bg_ref, o_ref, st_ref, xm_ref, s_ref):
        @pl.when(pl.program_id(0) == 0)
        def _():
            s_ref[...] = jnp.zeros_like(s_ref)

        masks = _chunk_masks()

        def group(gi, carry):
            rows = [pl.ds(pl.multiple_of((FWD_GROUP * gi + j) * CHUNK, CHUNK), CHUNK) for j in range(FWD_GROUP)]
            loc = _units_local(sum((_chunk_units(q_ref, k_ref, v_ref, bg_ref, r) for r in rows), []), masks)
            states = [s_ref[h] for h in range(HEADS)]
            for j in range(FWD_GROUP):
                lj = loc[HEADS * j:HEADS * (j + 1)]
                ws = [_mm(jnp.concatenate([p["w"], p["q_dec"]], axis=0), s) for p, s in zip(lj, states)]
                v_new = [p["u"] - b[:CHUNK] for p, b in zip(lj, ws)]
                intra = [_mm(p["qk"], vn) for p, vn in zip(lj, v_new)]
                upd = [_mm(p["k_dec"], vn, TN) for p, vn in zip(lj, v_new)]
                o_ref[rows[j], :] = jnp.concatenate([b[CHUNK:] + a for b, a in zip(ws, intra)], axis=1)
                for h in range(HEADS):
                    st_ref[FWD_GROUP * gi + j, h] = states[h]
                    xm_ref[FWD_GROUP * gi + j, h] = lj[h]["xm"]
                states = [p["gl"] * s + d for p, s, d in zip(lj, states, upd)]
            for h in range(HEADS):
                s_ref[h] = states[h]
            return carry

        lax.fori_loop(0, n_chunk // FWD_GROUP, group, 0)

    tok = lambda w: pl.BlockSpec((tb, w), lambda i: (i, 0))
    return pl.pallas_call(
        body, name="delta_fwd", grid=(T // tb,),
        in_specs=[tok(DN_WIDTH), tok(DN_WIDTH), tok(DN_WIDTH), tok(LANES)],
        out_specs=[tok(DN_WIDTH), pl.BlockSpec((n_chunk, HEADS, HEAD_DIM, HEAD_DIM), lambda i: (i, 0, 0, 0)),
                   pl.BlockSpec((n_chunk, HEADS, CHUNK, CHUNK), lambda i: (i, 0, 0, 0))],
        out_shape=[jax.ShapeDtypeStruct((T, DN_WIDTH), F32),
                   jax.ShapeDtypeStruct((T // CHUNK, HEADS, HEAD_DIM, HEAD_DIM), F32),
                   jax.ShapeDtypeStruct((T // CHUNK, HEADS, CHUNK, CHUNK), F32)],
        scratch_shapes=[pltpu.VMEM((HEADS, HEAD_DIM, HEAD_DIM), F32)],
        compiler_params=_params(("arbitrary",)),
    )(q, k, v, bg)


def _dn_out(o, z, gn):
    outs, ohs, rs = [], [], []
    for hh in range(HEADS):
        oh = o[:, HEAD_DIM * hh:HEAD_DIM * (hh + 1)]
        r = lax.rsqrt(jnp.mean(oh * oh, axis=-1, keepdims=True) + EPS)
        ohs.append(oh * r)
        rs.append(r)
    sz = _sigmoid(z)
    oh = jnp.concatenate(ohs, axis=1)
    gn4 = jnp.concatenate([gn] * HEADS, axis=1)
    return oh * gn4 * (z * sz), oh, rs, sz, gn4


def _sc_fwd(sc_in, halo, cw, tb):
    xc = jnp.concatenate([halo, sc_in], axis=0)
    u = xc[:, SC_WIDTH:2 * SC_WIDTH] * xc[:, 2 * SC_WIDTH:]
    cv = _taps(u, cw, 3, tb, 6)
    gate_b = sc_in[:, :SC_WIDTH]
    y = gate_b * cv
    gw = SC_WIDTH // SC_GROUPS
    yhs, rs = [], []
    for gi in range(SC_GROUPS):
        yg = y[:, gw * gi:gw * (gi + 1)]
        r = lax.rsqrt(jnp.mean(yg * yg, axis=-1, keepdims=True) + EPS)
        yhs.append(yg * r)
        rs.append(r)
    return u, cv, gate_b, jnp.concatenate(yhs, axis=1), rs


def _shard_rows(land, first, rows, single_buffer=False):
    assert first % rows == 0 and land.shape[0] == N_CHIPS
    mode = dict(pipeline_mode=pl.Buffered(1)) if single_buffer else {}
    return pl.BlockSpec((N_CHIPS, rows, land.shape[2]), lambda i: (0, first // rows, 0), **mode)


def _whole(w_ref):
    n, rows, cols = w_ref.shape
    return w_ref[...].reshape(n * rows, cols)


def _mix_ffn(o, z, sc_in, x, land_a, gn, scw, gs, g2, land_b):
    T = x.shape[0]
    tb = 256

    def body(o_ref, z_ref, sc_ref, halo_ref, x_ref, wo_ref, gn_ref, scw_ref, gs_ref, g2_ref, wgt_ref, wut_ref, wd_ref,
             x1_ref, mix_ref, x2_ref, a_ref, b_ref, h_ref):
        o_n = _dn_out(o_ref[...], z_ref[...], gn_ref[...])[0]
        halo = jnp.where(pl.program_id(0) > 0, halo_ref[...], 0.0)
        yh = _sc_fwd(sc_ref[...], halo, scw_ref[...], tb)[3]
        mix = jnp.concatenate([o_n, yh * gs_ref[...]], axis=1).astype(BF16)
        x1 = x_ref[...] + jnp.dot(mix, _whole(wo_ref), preferred_element_type=F32)
        x1_ref[...] = x1
        mix_ref[...] = mix
        r = lax.rsqrt(jnp.mean(x1 * x1, axis=-1, keepdims=True) + EPS)
        h = (x1 * r * g2_ref[...]).astype(BF16)
        a = lax.dot_general(h, _whole(wgt_ref), NT, preferred_element_type=F32)
        b = lax.dot_general(h, _whole(wut_ref), NT, preferred_element_type=F32)
        act = (a * _sigmoid(a) * b).astype(BF16)
        x2_ref[...] = x1 + jnp.dot(act, _whole(wd_ref), preferred_element_type=F32)
        a_ref[...] = a.astype(BF16)
        b_ref[...] = b.astype(BF16)
        h_ref[...] = h

    tok = lambda w: pl.BlockSpec((tb, w), lambda i: (i, 0))
    full = lambda t: pl.BlockSpec(t.shape, lambda i: (0, 0))
    once = lambda land, first, rows: _shard_rows(land, first, rows, single_buffer=True)
    return pl.pallas_call(
        body, name="mix_ffn", grid=(T // tb,),
        in_specs=[tok(DN_WIDTH), tok(DN_WIDTH), tok(3 * SC_WIDTH), pl.BlockSpec((8, 3 * SC_WIDTH), _before_halo(tb)),
                  tok(D_MODEL), once(land_a, A_OUT_AT, OUT_SHARD), full(gn), full(scw), full(gs), full(g2),
                  once(land_b, 0, FF_SHARD), once(land_b, FF_SHARD, FF_SHARD), once(land_b, 2 * FF_SHARD, FF_SHARD)],
        out_specs=[tok(D_MODEL), tok(D_MODEL), tok(D_MODEL), tok(D_FF), tok(D_FF), tok(D_MODEL)],
        out_shape=[jax.ShapeDtypeStruct((T, D_MODEL), F32), jax.ShapeDtypeStruct((T, D_MODEL), BF16),
                   jax.ShapeDtypeStruct((T, D_MODEL), F32), jax.ShapeDtypeStruct((T, D_FF), BF16),
                   jax.ShapeDtypeStruct((T, D_FF), BF16), jax.ShapeDtypeStruct((T, D_MODEL), BF16)],
        compiler_params=_params(("parallel",)),
    )(o, z, sc_in, sc_in, x, land_a, gn, scw, gs, g2, land_b, land_b, land_b)


def _loss_head(x, gf, target):
    T = x.shape[0]
    tb = 512

    def body(x_ref, g_ref, t_ref, dx_ref, dxb_ref, loss_ref, dg_ref):
        @pl.when(pl.program_id(0) == 0)
        def _():
            loss_ref[...] = jnp.zeros_like(loss_ref)
            dg_ref[...] = jnp.zeros_like(dg_ref)

        xv = x_ref[...]
        r = lax.rsqrt(jnp.mean(xv * xv, axis=-1, keepdims=True) + EPS)
        xh = xv * r
        err = xh * g_ref[...] - t_ref[...]
        per_tok = jnp.mean(err * err, axis=-1, keepdims=True)
        loss_ref[...] += 0.5 * jnp.sum(per_tok, axis=0, keepdims=True)
        dy = err * (1.0 / D_MODEL)
        _row_acc(dg_ref, dy * xh)
        dx = _rms_bwd(dy, xh, r, g_ref[...])
        dx_ref[...] = dx
        dxb_ref[...] = dx.astype(BF16)

    tok = pl.BlockSpec((tb, D_MODEL), lambda i: (i, 0))
    return pl.pallas_call(
        body, name="loss_head", grid=(T // tb,),
        in_specs=[tok, pl.BlockSpec(gf.shape, lambda i: (0, 0)), tok],
        out_specs=[tok, tok, pl.BlockSpec((8, LANES), lambda i: (0, 0)), pl.BlockSpec((8, D_MODEL), lambda i: (0, 0))],
        out_shape=[jax.ShapeDtypeStruct((T, D_MODEL), F32), jax.ShapeDtypeStruct((T, D_MODEL), BF16),
                   jax.ShapeDtypeStruct((8, LANES), F32), jax.ShapeDtypeStruct((8, D_MODEL), F32)],
        compiler_params=_params(("arbitrary",)),
    )(x, gf, target)


def _ffn_mix_bwd(dx2, x1, a, b, g2, land_b, o, z, sc_in, land_a, gn, scw, gs):
    T = x1.shape[0]
    tb = 256

    def body(dx2_ref, x_ref, a_ref, b_ref, g_ref, wgt_ref, wut_ref, wd_ref,
             o_ref, z_ref, sc_ref, halo_ref, w_ref, gn_ref, scw_ref, gs_ref,
             dx1_ref, da_ref, db_ref, act_ref, dg_ref,
             do_ref, dz_ref, dgb_ref, dcv_ref, dxb_ref, dgn_ref, dgs_ref, dscw_ref):
        @pl.when(pl.program_id(0) == 0)
        def _():
            for ref in (dg_ref, dgn_ref, dgs_ref, dscw_ref):
                ref[...] = jnp.zeros_like(ref)

        zv = z_ref[...]
        _, oh, rs, sz, gn4 = _dn_out(o_ref[...], zv, gn_ref[...])
        halo = jnp.where(pl.program_id(0) > 0, halo_ref[...], 0.0)
        u, cv, gate_b, yh, rys = _sc_fwd(sc_ref[...], halo, scw_ref[...], tb)

        dx2v = dx2_ref[...]
        av = a_ref[...].astype(F32)
        bv = b_ref[...].astype(F32)
        dact = _mm(dx2v, _whole(wd_ref), NT)
        sa = _sigmoid(av)
        silu = av * sa
        da = (dact * bv * (sa * (1.0 + av * (1.0 - sa)))).astype(BF16)
        db = (dact * silu).astype(BF16)
        dh = _mm(da, _whole(wgt_ref)) + _mm(db, _whole(wut_ref))
        xv = x_ref[...]
        r = lax.rsqrt(jnp.mean(xv * xv, axis=-1, keepdims=True) + EPS)
        xh = xv * r
        _row_acc(dg_ref, dh * xh)
        dx1 = dx2v + _rms_bwd(dh, xh, r, g_ref[...])
        dx1_ref[...] = dx1
        da_ref[...] = da
        db_ref[...] = db
        act_ref[...] = (silu * bv).astype(BF16)

        dx_bf16 = dx1.astype(BF16)
        dxb_ref[...] = dx_bf16
        dmix = lax.dot_general(dx_bf16, _whole(w_ref), NT, preferred_element_type=F32)
        don = dmix[:, :DN_WIDTH]
        dosc = dmix[:, DN_WIDTH:]
        silu_z = zv * sz
        dgn_full = don * oh * silu_z
        dgn_ref[0:1, :] += jnp.sum(sum(dgn_full[:, HEAD_DIM * hh:HEAD_DIM * (hh + 1)] for hh in range(HEADS)),
                                   axis=0, keepdims=True)
        dz_ref[...] = (don * oh * gn4 * (sz * (1.0 + zv * (1.0 - sz)))).astype(BF16)
        t = don * gn4 * silu_z
        for hh in range(HEADS):
            sl = slice(HEAD_DIM * hh, HEAD_DIM * (hh + 1))
            th, ohh = t[:, sl], oh[:, sl]
            do_ref[:, sl] = rs[hh] * (th - ohh * jnp.mean(th * ohh, axis=-1, keepdims=True))
        _row_acc(dgs_ref, dosc * yh)
        ty = dosc * gs_ref[...]
        gw = SC_WIDTH // SC_GROUPS
        dys = []
        for gi in range(SC_GROUPS):
            sl = slice(gw * gi, gw * (gi + 1))
            tg, yg = ty[:, sl], yh[:, sl]
            dys.append(rys[gi] * (tg - yg * jnp.mean(tg * yg, axis=-1, keepdims=True)))
        dy = jnp.concatenate(dys, axis=1)
        dgb_ref[...] = dy * cv
        dcv = dy * gate_b
        dcv_ref[...] = dcv
        for j in range(3):
            dscw_ref[j:j + 1, :] += jnp.sum(dcv * _rows_from(u, 6 + j, tb), axis=0, keepdims=True)

    tok = lambda w: pl.BlockSpec((tb, w), lambda i: (i, 0))
    full = lambda t: pl.BlockSpec(t.shape, lambda i: (0, 0))
    acc = lambda w: pl.BlockSpec((8, w), lambda i: (0, 0))
    once = lambda land, first, rows: _shard_rows(land, first, rows, single_buffer=True)
    return pl.pallas_call(
        body, name="ffn_mix_bwd", grid=(T // tb,),
        in_specs=[tok(D_MODEL), tok(D_MODEL), tok(D_FF), tok(D_FF), full(g2),
                  once(land_b, 0, FF_SHARD), once(land_b, FF_SHARD, FF_SHARD), once(land_b, 2 * FF_SHARD, FF_SHARD),
                  tok(DN_WIDTH), tok(DN_WIDTH), tok(3 * SC_WIDTH), pl.BlockSpec((8, 3 * SC_WIDTH), _before_halo(tb)),
                  once(land_a, A_OUT_AT, OUT_SHARD), full(gn), full(scw), full(gs)],
        out_specs=[tok(D_MODEL), tok(D_FF), tok(D_FF), tok(D_FF), acc(D_MODEL),
                   tok(DN_WIDTH), tok(DN_WIDTH), tok(SC_WIDTH), tok(SC_WIDTH), tok(D_MODEL),
                   acc(HEAD_DIM), acc(SC_WIDTH), acc(SC_WIDTH)],
        out_shape=[jax.ShapeDtypeStruct((T, D_MODEL), F32)]
        + [jax.ShapeDtypeStruct((T, D_FF), BF16)] * 3 + [jax.ShapeDtypeStruct((8, D_MODEL), F32)]
        + [jax.ShapeDtypeStruct((T, DN_WIDTH), F32), jax.ShapeDtypeStruct((T, DN_WIDTH), BF16),
           jax.ShapeDtypeStruct((T, SC_WIDTH), F32), jax.ShapeDtypeStruct((T, SC_WIDTH), F32),
           jax.ShapeDtypeStruct((T, D_MODEL), BF16),
           jax.ShapeDtypeStruct((8, HEAD_DIM), F32), jax.ShapeDtypeStruct((8, SC_WIDTH), F32),
           jax.ShapeDtypeStruct((8, SC_WIDTH), F32)],
        compiler_params=_params(("arbitrary",)),
    )(dx2, x1, a, b, g2, land_b, land_b, land_b, o, z, sc_in, sc_in, land_a, gn, scw, gs)


WGRAD_TOKENS = 2048


def _wgrad_share(a, b, parts, first, name):
    T = b.shape[0]
    rows = a.shape[1] // N_CHIPS
    assert first % rows == 0 and b.shape[1] == parts.shape[2]
    bk = min(T, WGRAD_TOKENS)
    n_k = T // bk
    group = 2
    assert (group * rows) % LANES == 0 and n_k >= 2

    def body(a_ref, b_ref, parts_ref, o_ref, acc_ref):
        kk = pl.program_id(1)
        product = lambda: lax.dot_general(a_ref[...], b_ref[...], TN, preferred_element_type=F32)

        @pl.when(kk == 0)
        def _():
            acc_ref[...] = product()

        @pl.when(jnp.logical_and(kk > 0, kk < n_k - 1))
        def _():
            acc_ref[...] += product()

        @pl.when(kk == n_k - 1)
        def _():
            total = acc_ref[...] + product()
            for s in range(group):
                o_ref[s] = total[rows * s:rows * (s + 1), :].astype(BF16)

    return pl.pallas_call(
        body, name=name, grid=(N_CHIPS // group, n_k),
        in_specs=[pl.BlockSpec((bk, group * rows), lambda i, kk: (kk, i)),
                  pl.BlockSpec((bk, b.shape[1]), lambda i, kk: (kk, 0)), _ANY],
        out_specs=pl.BlockSpec((group, rows, b.shape[1]), lambda i, kk: (i, first // rows, 0)),
        out_shape=jax.ShapeDtypeStruct(parts.shape, BF16),
        scratch_shapes=[pltpu.VMEM((group * rows, b.shape[1]), F32)],
        input_output_aliases={2: 0},
        compiler_params=_params(("parallel", "arbitrary")),
    )(a, b, parts)


def _delta_bwd(q, k, v, bg, states, xms, do, after):
    T = q.shape[0]
    tb = 512
    n_chunk = tb // CHUNK
    nb = T // tb

    def body(q_ref, k_ref, v_ref, bg_ref, st_ref, xm_ref, do_ref, after_ref, dq_ref, dk_ref, dv_ref, dbg_ref, ds_ref):
        @pl.when(pl.program_id(0) == 0)
        def _():
            ds_ref[...] = jnp.zeros_like(ds_ref)

        masks = _chunk_masks()
        causal, strict = masks
        lane = lax.broadcasted_iota(jnp.int32, (CHUNK, LANES), 1)
        last_row = lax.broadcasted_iota(jnp.int32, (CHUNK, 1), 0) == CHUNK - 1
        cat = jnp.concatenate
        heads = range(HEADS)

        def open_chunk(ci, loc):
            rows = pl.ds(pl.multiple_of(ci * CHUNK, CHUNK), CHUNK)
            dov = do_ref[rows, :]
            return dict(rows=rows, loc=loc, do=[dov[:, HEAD_DIM * h:HEAD_DIM * (h + 1)] for h in heads],
                        state=[st_ref[ci, h] for h in heads])

        def a_free(c):
            loc, do, state = c["loc"], c["do"], c["state"]
            w_s = [_mm(p["w"], s) for p, s in zip(loc, state)]
            c["dq_dec"] = [_mm(d, s, NT) for d, s in zip(do, state)]
            c["qk_do"] = [_mm(p["qk"], d, TN) for p, d in zip(loc, do)]
            c["qd_do"] = [_mm(p["q_dec"], d, TN) for p, d in zip(loc, do)]
            c["v_new"] = [p["u"] - t for p, t in zip(loc, w_s)]
            c["dqk"] = [jnp.where(causal, _mm(d, vn, NT), 0.0) for d, vn in zip(do, c["v_new"])]

        def a_state(c, ds_next):
            c["ds_next"] = ds_next
            kd_ds = [_mm(p["k_dec"], d) for p, d in zip(c["loc"], ds_next)]
            c["dk_dec"] = [_mm(vn, d, NT) for vn, d in zip(c["v_new"], ds_next)]
            c["dv_new"] = [a + b for a, b in zip(c["qk_do"], kd_ds)]

        def b_state(c):
            loc = c["loc"]
            w_dv = [_mm(p["w"], dvn, TN) for p, dvn in zip(loc, c["dv_new"])]
            c["dw"] = [-_mm(dvn, s, NT) for dvn, s in zip(c["dv_new"], c["state"])]
            return [loc[h]["gl"] * c["ds_next"][h] + c["qd_do"][h] - w_dv[h] for h in heads]

        def c_solve(c):
            loc, dv_new, dw = c["loc"], c["dv_new"], c["dw"]
            c["dtm"] = [_mm(cat([dvn, d], axis=1), cat([p["vb"], p["kbg"]], axis=1), NT) for dvn, d, p in zip(dv_new, dw, loc)]
            x_t = [_mm(p["xm"], cat([dvn, d], axis=1), TN) for p, dvn, d in zip(loc, dv_new, dw)]
            c["dvb"] = [dvn + t[:, :HEAD_DIM] for dvn, t in zip(dv_new, x_t)]
            c["dkbg"] = [d + t[:, HEAD_DIM:] for d, t in zip(dw, x_t)]

        def d_solve(c):
            c["y"] = [t + _mm(p["xm"], t, TN) for p, t in zip(c["loc"], c["dtm"])]

        def e_solve(c):
            c["dlow"] = [jnp.where(strict, -(t + _mm(t, p["xm"], NT)), 0.0) for p, t in zip(c["loc"], c["y"])]

        def f_close(c):
            loc, rows = c["loc"], c["rows"]
            dmm = [d * p["decay"] for d, p in zip(c["dlow"], loc)]
            dnn = [d * p["decay"] for d, p in zip(c["dqk"], loc)]
            by_k = [_mm(cat([a, b], axis=0), p["k"]) for a, b, p in zip(dmm, dnn, loc)]
            dk_mm = [_mm(cat([a, b], axis=0), cat([p["kb"], p["q"]], axis=0), TN) for a, b, p in zip(dmm, dnn, loc)]
            dq_out, dk_out, dv_out = [], [], []
            dbeta_all = jnp.zeros((CHUNK, LANES), F32)
            dgc_all = jnp.zeros((CHUNK, LANES), F32)
            for h in heads:
                p = loc[h]
                dkb = by_k[h][:CHUNK] + c["dkbg"][h] * p["eg"]
                dq_out.append(by_k[h][CHUNK:] + c["dq_dec"][h] * p["eg"])
                dk_out.append(dk_mm[h] + c["dk_dec"][h] * p["ek"] + dkb * p["beta"])
                dv_out.append(c["dvb"][h] * p["beta"])
                dbeta = jnp.sum(dkb * p["k"] + c["dvb"][h] * p["v"], axis=1, keepdims=True)
                e = c["dlow"][h] * p["low"] + c["dqk"][h] * p["qk"]
                kd = jnp.sum(c["dk_dec"][h] * p["k_dec"], axis=1, keepdims=True)
                dgc = (jnp.sum(e, axis=1, keepdims=True) - jnp.sum(e.T, axis=1, keepdims=True)
                       + jnp.sum(c["dq_dec"][h] * p["q_dec"], axis=1, keepdims=True) - kd
                       + jnp.sum(c["dkbg"][h] * p["kbg"], axis=1, keepdims=True))
                dgl = jnp.sum(jnp.sum(c["ds_next"][h] * c["state"][h], axis=1, keepdims=True), axis=0, keepdims=True)
                d_last = jnp.sum(kd, axis=0, keepdims=True) + dgl * p["gl"]
                dgc = dgc + jnp.where(last_row, d_last, 0.0)
                dbeta_all = jnp.where(lane == h, dbeta, dbeta_all)
                dgc_all = jnp.where(lane == h + HEADS, dgc, dgc_all)
            dq_ref[rows, :] = cat(dq_out, axis=1)
            dk_ref[rows, :] = cat(dk_out, axis=1)
            dv_ref[rows, :] = cat(dv_out, axis=1)
            dbg_ref[rows, :] = dbeta_all + dgc_all

        def group(gj, carry):
            first = n_chunk - 1 - BWD_GROUP * gj
            ids = [first - j for j in range(BWD_GROUP)]
            rows = [pl.ds(pl.multiple_of(ci * CHUNK, CHUNK), CHUNK) for ci in ids]
            loc = _units_local(sum((_chunk_units(q_ref, k_ref, v_ref, bg_ref, r) for r in rows), []), masks,
                               xms=[xm_ref[ci, h] for ci in ids for h in heads])
            chunks = [open_chunk(ci, loc[HEADS * j:HEADS * (j + 1)]) for j, ci in enumerate(ids)]
            for c in chunks:
                a_free(c)
            ds_cur = [ds_ref[h] for h in heads]
            later = (c_solve, d_solve, e_solve, f_close)
            for t in range(2 * (BWD_GROUP - 1) + 2 + len(later)):
                for j, c in enumerate(chunks):
                    stage = t - 2 * j
                    if stage == 0:
                        a_state(c, ds_cur)
                    elif stage == 1:
                        ds_cur = b_state(c)
                    elif 2 <= stage < 2 + len(later):
                        later[stage - 2](c)
            for h in heads:
                ds_ref[h] = ds_cur[h]
            return carry

        lax.fori_loop(0, n_chunk // BWD_GROUP, group, 0)

    tok = lambda w: pl.BlockSpec((tb, w), lambda i: (nb - 1 - i, 0))
    return pl.pallas_call(
        body, name="delta_bwd", grid=(nb,),
        in_specs=[tok(DN_WIDTH), tok(DN_WIDTH), tok(DN_WIDTH), tok(LANES),
                  pl.BlockSpec((n_chunk, HEADS, HEAD_DIM, HEAD_DIM), lambda i: (nb - 1 - i, 0, 0, 0)),
                  pl.BlockSpec((n_chunk, HEADS, CHUNK, CHUNK), lambda i: (nb - 1 - i, 0, 0, 0)), tok(DN_WIDTH),
                  pl.BlockSpec(memory_space=pltpu.SMEM)],
        out_specs=[tok(DN_WIDTH), tok(DN_WIDTH), tok(DN_WIDTH), tok(LANES)],
        out_shape=[jax.ShapeDtypeStruct((T, DN_WIDTH), F32)] * 3 + [jax.ShapeDtypeStruct((T, LANES), F32)],
        scratch_shapes=[pltpu.VMEM((HEADS, HEAD_DIM, HEAD_DIM), F32)],
        compiler_params=_params(("arbitrary",)),
    )(q, k, v, bg, states, xms, do, after)


def _dn_prep_back(dq, dk, dv, dbg, c, bd, al_row, dt_row, tb):
    sg = _sigmoid(c)
    a = c * sg
    dsilu = sg * (1.0 + c * (1.0 - sg))
    pieces = [None] * (2 * HEADS)
    for hd in range(HEADS):
        sl = slice(HEAD_DIM * hd, HEAD_DIM * (hd + 1))
        for which, (base, grad, scale) in enumerate(((0, dq, Q_SCALE), (DN_WIDTH, dk, 1.0))):
            sa = slice(base + HEAD_DIM * hd, base + HEAD_DIM * (hd + 1))
            raw = a[:, sa]
            r = lax.rsqrt(jnp.sum(raw * raw, axis=-1, keepdims=True) + EPS)
            nrm = raw * r
            gn_ = grad[:, sl] * scale
            pieces[which * HEADS + hd] = r * (gn_ - nrm * jnp.sum(gn_ * nrm, axis=-1, keepdims=True)) * dsilu[:, sa]
    dc = jnp.concatenate(pieces + [dv * dsilu[:, 2 * DN_WIDTH:]], axis=1)
    lane = lax.broadcasted_iota(jnp.int32, bd.shape, 1)
    is_b = lane < HEADS
    is_g = jnp.logical_and(lane >= HEADS, lane < 2 * HEADS)
    dbgv = jnp.where(is_b, dbg, _mm32(_chunk_cumsum_matrix(tb), dbg, TN))
    beta = _sigmoid(bd)
    neg_a = -jnp.exp(al_row)
    pre_sp = bd + dt_row
    g = neg_a * _softplus(pre_sp)
    da_in = dbgv * neg_a * _sigmoid(pre_sp)
    dbd = jnp.where(is_b, dbgv * beta * (1.0 - beta), jnp.where(is_g, da_in, 0.0)).astype(BF16)
    dal_row = jnp.sum(jnp.where(is_g, dbgv * g, 0.0), axis=0, keepdims=True)
    ddt_row = jnp.sum(jnp.where(is_g, da_in, 0.0), axis=0, keepdims=True)
    return dc, dbd, dal_row, ddt_row


def _dp_of_chip(dqkv, dz, dbd, dsc, s):
    lo, hi = IN_SHARD * s, IN_SHARD * (s + 1)
    pieces = []
    for w_at, w_end, block in ((0, W_Z, dqkv), (W_Z, W_BD, dz), (W_BD, W_SC, dbd), (W_SC, W_IN_COLS, dsc)):
        a, b = max(lo, w_at), min(hi, w_end)
        if a < b:
            pieces.append(block[:, a - w_at:b - w_at])
    pieces.append(jnp.zeros((dqkv.shape[0], D_MODEL - IN_SHARD), dqkv.dtype))
    return jnp.concatenate(pieces, axis=1)


def _in_proj_bwd(dq, dk, dv, dbg, qkv, c, bd, al_row, dt_row, dcv, dgb, sc_in, dz, cw, scw, dx1, x, g1, land_a):
    T = x.shape[0]
    tb = 256
    nb = T // tb

    def body(dq_ref, dk_ref, dv_ref, dbg_ref, pre_ref, c_ref, bd_ref, al_ref, dt_ref,
             dcv_ref, dcv_halo_ref, dgb_ref, sc_ref, dz_ref, cw_ref, scw_ref, dx1_ref, x_ref, g_ref, w_ref,
             dx_ref, dxb_ref, dps_ref, dg_ref, dcw_ref, dal_ref, ddt_ref, head_ref, dbd_ref):
        @pl.when(pl.program_id(0) == 0)
        def _():
            for ref in (dg_ref, dcw_ref, dal_ref, ddt_ref, head_ref):
                ref[...] = jnp.zeros_like(ref)

        block = nb - 1 - pl.program_id(0)
        last = block == nb - 1
        def prep(ci, carry):
            rows = pl.ds(pl.multiple_of(ci * CHUNK, CHUNK), CHUNK)
            dc, dbd, dal_row, ddt_row = _dn_prep_back(
                dq_ref[rows, :], dk_ref[rows, :], dv_ref[rows, :], dbg_ref[rows, :], c_ref[rows, :], bd_ref[rows, :],
                al_ref[...], dt_ref[...], CHUNK)
            head_ref[rows, :] = dc
            dbd_ref[rows, :] = dbd
            dal_ref[0:1, :] += dal_row
            ddt_ref[0:1, :] += ddt_row
            return carry

        lax.fori_loop(0, tb // CHUNK, prep, 0)
        xc = head_ref[...]
        head_ref[tb:, :] = xc[0:8, :]
        dbd = dbd_ref[...]
        w4 = cw_ref[...]
        pre = pre_ref[...]
        dqkv = None
        for j in range(4):
            later = xc[0:tb, :] if j == 3 else _rows_from(xc, 3 - j, tb)
            dqkv = w4[j:j + 1, :] * later if dqkv is None else dqkv + w4[j:j + 1, :] * later
            dcw_ref[j:j + 1, :] += jnp.sum(later * pre, axis=0, keepdims=True)
        yc = jnp.concatenate([dcv_ref[...], jnp.where(last, 0.0, dcv_halo_ref[...])], axis=0)
        w3 = scw_ref[...]
        du = w3[2:3, :] * yc[0:tb, :] + w3[1:2, :] * _rows_from(yc, 1, tb) + w3[0:1, :] * _rows_from(yc, 2, tb)
        sc = sc_ref[...]
        dsc = jnp.concatenate([dgb_ref[...], du * sc[:, 2 * SC_WIDTH:], du * sc[:, SC_WIDTH:2 * SC_WIDTH]], axis=1)
        blocks = (dqkv.astype(BF16), dz_ref[...], dbd, dsc.astype(BF16))
        dh = jnp.zeros((tb, D_MODEL), F32)
        for s in range(N_CHIPS):
            dps = _dp_of_chip(*blocks, s)
            dps_ref[:, D_MODEL * s:D_MODEL * (s + 1)] = dps
            dh = dh + lax.dot_general(dps, w_ref[s], NT, preferred_element_type=F32)
        xv = x_ref[...]
        r = lax.rsqrt(jnp.mean(xv * xv, axis=-1, keepdims=True) + EPS)
        xh = xv * r
        _row_acc(dg_ref, dh * xh)
        dx = dx1_ref[...] + _rms_bwd(dh, xh, r, g_ref[...])
        dx_ref[...] = dx
        dxb_ref[...] = dx.astype(BF16)

    tok = lambda w: pl.BlockSpec((tb, w), lambda i: (nb - 1 - i, 0))
    full = lambda t: pl.BlockSpec(t.shape, lambda i: (0, 0))
    acc = lambda w: pl.BlockSpec((8, w), lambda i: (0, 0))
    after = lambda w: pl.BlockSpec((8, w), lambda i: _after_halo(tb, T)(nb - 1 - i))
    return pl.pallas_call(
        body, name="in_proj_bwd", grid=(nb,),
        in_specs=[tok(DN_WIDTH), tok(DN_WIDTH), tok(DN_WIDTH), tok(LANES), tok(QKV), tok(QKV), tok(LANES),
                  full(al_row), full(dt_row), tok(SC_WIDTH), after(SC_WIDTH), tok(SC_WIDTH), tok(3 * SC_WIDTH),
                  tok(DN_WIDTH), full(cw), full(scw), tok(D_MODEL), tok(D_MODEL), full(g1), _shard_rows(land_a, 0, D_MODEL)],
        out_specs=[tok(D_MODEL), tok(D_MODEL), tok(N_CHIPS * D_MODEL), acc(D_MODEL), acc(QKV), acc(LANES), acc(LANES)],
        out_shape=[jax.ShapeDtypeStruct((T, D_MODEL), F32), jax.ShapeDtypeStruct((T, D_MODEL), BF16),
                   jax.ShapeDtypeStruct((T, N_CHIPS * D_MODEL), BF16), jax.ShapeDtypeStruct((8, D_MODEL), F32),
                   jax.ShapeDtypeStruct((8, QKV), F32), jax.ShapeDtypeStruct((8, LANES), F32),
                   jax.ShapeDtypeStruct((8, LANES), F32)],
        scratch_shapes=[pltpu.VMEM((tb + 8, QKV), F32), pltpu.VMEM((tb, LANES), BF16)],
        compiler_params=_params(("arbitrary",)),
    )(dq, dk, dv, dbg, qkv, c, bd, al_row, dt_row, dcv, dcv, dgb, sc_in, dz, cw, scw, dx1, x, g1, land_a)


def _wgrad_in_share(h, dps, parts, name):
    T = h.shape[0]
    bk = min(T, WGRAD_TOKENS)
    n_k = T // bk
    assert n_k >= 2

    def body(a_ref, b_ref, parts_ref, o_ref, acc_ref):
        kk = pl.program_id(1)
        product = lambda: lax.dot_general(a_ref[...], b_ref[...], TN, preferred_element_type=F32)

        @pl.when(kk == 0)
        def _():
            acc_ref[...] = product()

        @pl.when(jnp.logical_and(kk > 0, kk < n_k - 1))
        def _():
            acc_ref[...] += product()

        @pl.when(kk == n_k - 1)
        def _():
            o_ref[0] = (acc_ref[...] + product()).astype(BF16)

    return pl.pallas_call(
        body, name=name, grid=(N_CHIPS, n_k),
        in_specs=[pl.BlockSpec((bk, D_MODEL), lambda j, kk: (kk, 0)), pl.BlockSpec((bk, D_MODEL), lambda j, kk: (kk, j)), _ANY],
        out_specs=pl.BlockSpec((1, D_MODEL, D_MODEL), lambda j, kk: (j, 0, 0)),
        out_shape=jax.ShapeDtypeStruct(parts.shape, BF16),
        scratch_shapes=[pltpu.VMEM((D_MODEL, D_MODEL), F32)],
        input_output_aliases={2: 0},
        compiler_params=_params(("parallel", "arbitrary")),
    )(h, dps, parts)


def _pad_rows(a, rows=8):
    return jnp.pad(a, ((0, rows - a.shape[0]), (0, 0)))


def _gate_rows(a_log, dt_bias):
    put = lambda t: jnp.pad(t.reshape(1, HEADS), ((0, 0), (HEADS, LANES - 2 * HEADS)))
    return put(a_log), put(dt_bias)


def _mixer_fwd(x, p):
    qkv, z, sc_in, bd, h, q, k, v, bg, c = _in_proj(x, p["g1"], p["land_a"], p["cw"], p["al"], p["dt"])
    o, states, xms = _delta_fwd(q, k, v, bg)
    return dict(x=x, qkv=qkv, c=c, z=z, sc_in=sc_in, bd=bd, h=h, q=q, k=k, v=v, bg=bg, o=o, states=states, xms=xms)


def _tail_fwd(s, p, land_b):
    x1, mix, x2, a, b, h2 = _mix_ffn(s["o"], s["z"], s["sc_in"], s["x"], p["land_a"], p["gn"], p["scw"], p["gs"],
                                     p["g2"], land_b)
    return x2, dict(s, mix=mix), dict(x1=x1, a=a, b=b, h2=h2)


def _ffn_back(dx2, dx2_bf16, s, sm, p, land_b):
    dx1, da, db, act, dg2, *mid = _ffn_mix_bwd(dx2, s["x1"], s["a"], s["b"], p["g2"], land_b, sm["o"], sm["z"],
                                               sm["sc_in"], p["land_a"], p["gn"], p["scw"], p["gs"])
    parts = lax.empty((N_CHIPS, B_ROWS, D_MODEL), BF16)
    parts = _wgrad_share(act, dx2_bf16, parts, 2 * FF_SHARD, "wgrad_down")
    parts = _wgrad_share(da, s["h2"], parts, 0, "wgrad_gate")
    parts = _wgrad_share(db, s["h2"], parts, FF_SHARD, "wgrad_up")
    return dx1, parts, dg2[0], mid


def _mixer_bwd(dx1, mid, s, p, after):
    do, dz, dgb, dcv, dx1_bf16, dgn, dgs, dscw = mid
    dq, dk, dv, dbg = _delta_bwd(s["q"], s["k"], s["v"], s["bg"], s["states"], s["xms"], do, after)
    dx, dx_bf16, dps, dg1, dcw, dal, ddt = _in_proj_bwd(
        dq, dk, dv, dbg, s["qkv"], s["c"], s["bd"], p["al"], p["dt"], dcv, dgb, s["sc_in"], dz, p["cw"], p["scw"], dx1,
        s["x"], p["g1"], p["land_a"])
    parts = lax.empty((N_CHIPS, A_ROWS, D_MODEL), BF16)
    parts = _wgrad_in_share(s["h"], dps, parts, "wgrad_in")
    parts = _wgrad_share(s["mix"], dx1_bf16, parts, A_OUT_AT, "wgrad_out")
    g = dict(g1=dg1[0], gn=dgn[0], gs=dgs[0], scw=dscw[:3], cw=dcw[:4], al=dal[0, HEADS:2 * HEADS], dt=ddt[0, HEADS:2 * HEADS])
    return dx, dx_bf16, parts, g


def _place():
    return lax.axis_index("x"), lax.axis_index("y"), lax.axis_index("c")


def _other_chips(x, y):
    return [(1 - x, y), (x, 1 - y), (1 - x, 1 - y)]


_HBM = pl.BlockSpec(memory_space=pltpu.HBM)


def _gather_chips(arrs, name):
    n = len(arrs)

    def body(*refs):
        ins, outs = refs[:n], refs[n:2 * n]
        send_sems, recv_sems, local_sems = refs[2 * n:]
        x, y, c = _place()
        me = 2 * x + y
        others = _other_chips(x, y)

        def remote(k, j, landing):
            px, py = others[j]
            return pltpu.make_async_remote_copy(src_ref=ins[k], dst_ref=outs[k].at[landing], send_sem=send_sems.at[k, j],
                                                recv_sem=recv_sems.at[k, j], device_id=(px, py, c), device_id_type=MESH)

        local = [pltpu.make_async_copy(ins[k], outs[k].at[me], local_sems.at[k]) for k in range(n)]
        sends = [remote(k, j, me) for k in range(n) for j in range(3)]
        for cp in local + sends:
            cp.start()
        for k in range(n):
            for j, (px, py) in enumerate(others):
                remote(k, j, 2 * px + py).wait_recv()
        for cp in sends:
            cp.wait_send()
        for cp in local:
            cp.wait()

    shapes = [jax.ShapeDtypeStruct((N_CHIPS,) + a.shape, a.dtype) for a in arrs]
    return pl.pallas_call(
        body, name=name, in_specs=[_HBM] * n, out_specs=[_HBM] * n, out_shape=shapes,
        scratch_shapes=[pltpu.SemaphoreType.DMA((n, 3)), pltpu.SemaphoreType.DMA((n, 3)), pltpu.SemaphoreType.DMA((n,))],
    )(*arrs)


_SEM = pl.BlockSpec(memory_space=pltpu.SEMAPHORE)
_ANY = pl.BlockSpec(memory_space=pl.ANY)
_EFFECT = pltpu.SideEffectType.DATAFLOW_SIDE_EFFECTING


_FLIPS = [(a, b, cc) for a in (0, 1) for b in (0, 1) for cc in (0, 1)][1:]


def _split_copies(src_ref, land_ref, send_sems, recv_sems, gather, sending):
    x, y, c = _place()
    copies = []
    if gather:
        me = 2 * x + y
        for j, (px, py) in enumerate(_other_chips(x, y)):
            copies.append(pltpu.make_async_remote_copy(
                src_ref=src_ref, dst_ref=land_ref.at[me if sending else 2 * px + py],
                send_sem=send_sems.at[j], recv_sem=recv_sems.at[j], device_id=(px, py, c), device_id_type=MESH))
        return copies
    me = 4 * x + 2 * y + c
    for j, (a, b, cc) in enumerate(_FLIPS):
        px, py, pc = (1 - x) if a else x, (1 - y) if b else y, (1 - c) if cc else c
        copies.append(pltpu.make_async_remote_copy(
            src_ref=src_ref.at[2 * px + py], dst_ref=land_ref.at[me if sending else 4 * px + 2 * py + pc],
            send_sem=send_sems.at[j], recv_sem=recv_sems.at[j], device_id=(px, py, pc), device_id_type=MESH))
    return copies


def _own_slot(share):
    chip = 2 * lax.axis_index("x") + lax.axis_index("y")
    return lax.dynamic_update_slice(lax.empty((N_CHIPS,) + share.shape, share.dtype), share[None], (chip, 0, 0))


def _own_part(parts):
    chip = 2 * lax.axis_index("x") + lax.axis_index("y")
    own = lax.dynamic_index_in_dim(parts, chip, 0, keepdims=True)
    return lax.dynamic_update_slice(lax.empty((N_DEV,) + parts.shape[1:], parts.dtype), own,
                                    (2 * chip + lax.axis_index("c"), 0, 0))


def _exchange_start(src, land, after, name, gather):
    def body(src_ref, land_ref, after_ref, send_sems, recv_sems, src_thru, land_thru, token):
        for cp in _split_copies(src_ref, land_ref, send_sems, recv_sems, gather, sending=True):
            cp.start()
        token[...] = jnp.zeros_like(token)

    hbm = lambda t: pltpu.with_memory_space_constraint(t, pltpu.HBM)
    n_copies = N_CHIPS - 1 if gather else N_DEV - 1
    return pl.pallas_call(
        body, name=name,
        out_shape=(pltpu.SemaphoreType.DMA((n_copies,)), pltpu.SemaphoreType.DMA((n_copies,)), pltpu.HBM(src.shape, src.dtype),
                   pltpu.HBM(land.shape, land.dtype), jax.ShapeDtypeStruct((8, LANES), F32)),
        in_specs=(_HBM, _HBM, _ANY), out_specs=(_SEM, _SEM, _HBM, _HBM, pl.BlockSpec(memory_space=pltpu.VMEM)),
        input_output_aliases={0: 2, 1: 3},
        compiler_params=pltpu.CompilerParams(has_side_effects=_EFFECT),
    )(hbm(src), hbm(land), after)


def _exchange_wait(started, after, name, gather):
    send_sems, recv_sems, src_thru, land_thru, _ = started

    def body(src_ref, land_ref, send_sems, recv_sems, after_ref, src_dead, got_ref):
        for cp in _split_copies(src_ref, land_ref, send_sems, recv_sems, gather, sending=False):
            cp.wait_send()
            cp.wait_recv()

    return pl.pallas_call(
        body, name=name,
        out_shape=(pltpu.HBM(src_thru.shape, src_thru.dtype), pltpu.HBM(land_thru.shape, land_thru.dtype)),
        in_specs=(_HBM, _HBM, _SEM, _SEM, _ANY), out_specs=(_HBM, _HBM), input_output_aliases={0: 0, 1: 1},
        compiler_params=pltpu.CompilerParams(has_side_effects=_EFFECT),
    )(src_thru, land_thru, send_sems, recv_sems, after)[1]


def _all_reduce_small(v):
    rows = v.shape[0]
    flips = [(a, b, cc) for a in (0, 1) for b in (0, 1) for cc in (0, 1)][1:]

    def body(v_ref, out_ref, buf_ref, send_sems, recv_sems):
        x, y, c = _place()
        me = 4 * x + 2 * y + c
        peers = [((1 - x) if a else x, (1 - y) if b else y, (1 - c) if cc else c) for a, b, cc in flips]

        def copy(j, landing):
            return pltpu.make_async_remote_copy(src_ref=v_ref, dst_ref=buf_ref.at[landing], send_sem=send_sems.at[j],
                                                recv_sem=recv_sems.at[j], device_id=peers[j], device_id_type=MESH)

        sends = [copy(j, me) for j in range(N_DEV - 1)]
        for cp in sends:
            cp.start()
        buf_ref[me] = v_ref[...]
        for j, (px, py, pc) in enumerate(peers):
            copy(j, 4 * px + 2 * py + pc).wait_recv()
        for cp in sends:
            cp.wait_send()
        acc = buf_ref[0]
        for d in range(1, N_DEV):
            acc = acc + buf_ref[d]
        out_ref[...] = acc

    vmem = pl.BlockSpec(memory_space=pltpu.VMEM)
    return pl.pallas_call(
        body, name="all_reduce_small", in_specs=[vmem], out_specs=vmem,
        out_shape=jax.ShapeDtypeStruct(v.shape, F32),
        scratch_shapes=[pltpu.VMEM((N_DEV, rows, LANES), F32), pltpu.SemaphoreType.DMA((N_DEV - 1,)),
                        pltpu.SemaphoreType.DMA((N_DEV - 1,))],
    )(v)


def _row_block(*sizes):
    return next(t for t in (256, 176, 128, 64) if all(s % t == 0 for s in sizes))


def _adam_update(w, m, v, g):
    r1 = 1.0 / (1.0 - ADAM_B1 ** ADAM_STEP)
    r2 = 1.0 / (1.0 - ADAM_B2 ** ADAM_STEP)
    m_new = ADAM_B1 * m + (1.0 - ADAM_B1) * g
    v_new = ADAM_B2 * v + (1.0 - ADAM_B2) * (g * g)
    return -ADAM_LR * ((m_new * r1) / (jnp.sqrt(v_new * r2) + ADAM_EPS) + ADAM_WD * w), m_new, v_new


def _adamw_rows(w, m, v, got, first, name):
    n_layers, rows, cols = w.shape
    tr = _row_block(rows, first)

    def body(*refs):
        w_ref, m_ref, v_ref = refs[:3]
        g_out, d_out, m_out, v_out = refs[3 + n_layers:]
        for k in range(n_layers):
            @pl.when(pl.program_id(0) == k)
            def _(p_ref=refs[3 + k]):
                g = p_ref[0].astype(F32)
                for d in range(1, N_DEV):
                    g = g + p_ref[d].astype(F32)
                g = g[:, :cols]
                d_out[0], m_out[0], v_out[0] = _adam_update(w_ref[0], m_ref[0], v_ref[0], g)
                g_out[0] = g

    blk = pl.BlockSpec((1, tr, cols), lambda l, i: (l, i, 0))
    parts = [pl.BlockSpec((N_DEV, tr, got[0].shape[2]), lambda l, i, k=k: (0, jnp.where(l == k, first // tr + i, 0), 0))
             for k in range(n_layers)]
    return pl.pallas_call(
        body, name=name, grid=(n_layers, rows // tr),
        in_specs=[blk] * 3 + parts, out_specs=[blk] * 4,
        out_shape=[jax.ShapeDtypeStruct(w.shape, F32)] * 4,
        compiler_params=_params(("arbitrary", "arbitrary")),
    )(w, m, v, *got)


def _adamw(w, m, v, g_parts, name):
    rows, cols = w.shape
    tr = min(rows, 256)
    n = len(g_parts)

    def body(*refs):
        w_ref, m_ref, v_ref = refs[:3]
        g_refs = refs[3:3 + n]
        g_out, d_out, m_out, v_out = refs[3 + n:]
        g = g_refs[0][...]
        for r in g_refs[1:]:
            g = g + r[...]
        d_out[...], m_out[...], v_out[...] = _adam_update(w_ref[...], m_ref[...], v_ref[...], g)
        g_out[...] = g

    blk = pl.BlockSpec((tr, cols), lambda i: (i, 0))
    return pl.pallas_call(
        body, name=name, grid=(rows // tr,),
        in_specs=[blk] * (3 + n), out_specs=[blk] * 4,
        out_shape=[jax.ShapeDtypeStruct((rows, cols), F32)] * 4,
        compiler_params=_params(("parallel",)),
    )(w, m, v, *g_parts)


def _pack(parts, rows, fill=0.0):
    flat = jnp.concatenate([p.reshape(-1) for p in parts])
    return jnp.pad(flat, (0, rows * LANES - flat.shape[0]), constant_values=fill).reshape(rows, LANES)


def _unpack(packed, shapes):
    flat = packed.reshape(-1)
    out, at = [], 0
    for shp in shapes:
        size = 1
        for s in shp:
            size *= s
        out.append(flat[at:at + size].reshape(shp))
        at += size
    return out


def _packed_rows(shapes):
    total = 0
    for shp in shapes:
        size = 1
        for s in shp:
            size *= s
        total += size
    return -(-total // (8 * LANES)) * 8


def _cols_full(g, l):
    t = g[:, l]
    return jnp.moveaxis(t, 0, 1).reshape(t.shape[1], N_CHIPS * t.shape[2])


def _pad_cols(t):
    return jnp.pad(t, ((0, 0),) * (t.ndim - 1) + ((0, D_MODEL - t.shape[-1]),))


def kernel(x, norm1_g, w_in, dn_conv_w, dn_a_log, dn_dt_bias, dn_norm_g, sc_conv_w, sc_norm_g, w_out, norm2_g, ffn_w_gate, ffn_w_up, ffn_w_down, final_norm_g, loss_target, m_norm1_g, m_w_in, m_dn_conv_w, m_dn_a_log, m_dn_dt_bias, m_dn_norm_g, m_sc_conv_w, m_sc_norm_g, m_w_out, m_norm2_g, m_ffn_w_gate, m_ffn_w_up, m_ffn_w_down, m_final_norm_g, v_norm1_g, v_w_in, v_dn_conv_w, v_dn_a_log, v_dn_dt_bias, v_dn_norm_g, v_sc_conv_w, v_sc_norm_g, v_w_out, v_norm2_g, v_ffn_w_gate, v_ffn_w_up, v_ffn_w_down, v_final_norm_g):
    chip = 2 * lax.axis_index("x") + lax.axis_index("y")

    g_cw, g_scw = _gather_chips([dn_conv_w, sc_conv_w], "gather_conv")

    t_last = lambda t: jnp.swapaxes(t, -1, -2)
    gate_t, up_t = t_last(ffn_w_gate), t_last(ffn_w_up)
    zero_token = jnp.zeros((8, LANES), F32)

    def shares(l, tie):
        share_a = jnp.concatenate([_pad_cols(w_in[l] + tie), w_out[l]], axis=0).astype(BF16)
        share_b = jnp.concatenate([gate_t[l] + tie, up_t[l], ffn_w_down[l]], axis=0).astype(BF16)
        return share_a, _own_slot(share_a), share_b, _own_slot(share_b)

    def gather_start(l, packed, after):
        a = _exchange_start(packed[0], packed[1], after, "gather_a_start_%d" % l, gather=True)
        b = _exchange_start(packed[2], packed[3], a[4], "gather_b_start_%d" % l, gather=True)
        return a, b

    ga, gb = gather_start(0, shares(0, 0.0), g_cw)
    packed = [None] + [shares(l, gb[4][0, 0]) for l in range(1, DEPTH)]
    packed_all = sum(t[0, 0].astype(F32) for p in packed[1:] for t in (p[0], p[2]))
    land_a = _exchange_wait(ga, zero_token + packed_all, "gather_a_wait_0", gather=True)
    act = x[0]
    layers, saved_m, saved_f, lands_b = [], [], [], []
    for l in range(DEPTH):
        hold = 0.0
        if l + 1 < DEPTH:
            ga, gb_next = gather_start(l + 1, packed[l + 1], land_a)
            hold = gb_next[4][0:1, 0:1]
        al, dt = _gate_rows(dn_a_log[l], dn_dt_bias[l])
        layers.append(dict(
            g1=norm1_g[l][None] + hold, cw=_pad_rows(_cols_full(g_cw, l)), al=al, dt=dt,
            gn=dn_norm_g[l][None], scw=_pad_rows(_cols_full(g_scw, l)), gs=sc_norm_g[l][None],
            land_a=land_a, g2=norm2_g[l][None]))
        s = _mixer_fwd(act, layers[l])
        lands_b.append(_exchange_wait(gb, s["o"], "gather_b_wait_%d" % l, gather=True))
        act, s, sf = _tail_fwd(s, layers[l], lands_b[l])
        saved_m.append(s)
        saved_f.append(sf)
        if l + 1 < DEPTH:
            land_a = _exchange_wait(ga, act, "gather_a_wait_%d" % (l + 1), gather=True)
            gb = gb_next

    dact, dact_bf16, loss_part, d_final = _loss_head(act, final_norm_g[None], loss_target[0])
    grads, reduce_a, reduce_b = [None] * DEPTH, [None] * DEPTH, [None] * DEPTH
    hold = 0.0
    for l in reversed(range(DEPTH)):
        p = layers[l]
        dx1, parts, dg2, mid = _ffn_back(dact, dact_bf16, saved_f[l], saved_m[l], dict(p, g2=p["g2"] + hold), lands_b[l])
        reduce_b[l] = _exchange_start(parts, _own_part(parts), zero_token, "reduce_b_start_%d" % l, gather=False)
        dact, dact_bf16, parts, gm = _mixer_bwd(dx1, mid, saved_m[l], p, reduce_b[l][4][0:1, 0:1])
        reduce_a[l] = _exchange_start(parts, _own_part(parts), zero_token, "reduce_a_start_%d" % l, gather=False)
        hold = reduce_a[l][4][0:1, 0:1]
        grads[l] = dict(gm, g2=dg2)
    loss = lax.psum(loss_part[0, 0], ("x", "y", "c"))
    stack = lambda key: jnp.stack([grads[l][key] for l in range(DEPTH)])

    got_b = [_exchange_wait(reduce_b[l], reduce_a[0][4], "reduce_b_wait_%d" % l, gather=False)
             for l in reversed(range(DEPTH))][::-1]
    big = dict(
        ffn_w_gate=[t_last(o) for o in _adamw_rows(gate_t, t_last(m_ffn_w_gate), t_last(v_ffn_w_gate), got_b, 0, "adamw_gate")],
        ffn_w_up=[t_last(o) for o in _adamw_rows(up_t, t_last(m_ffn_w_up), t_last(v_ffn_w_up), got_b, FF_SHARD, "adamw_up")],
        ffn_w_down=_adamw_rows(ffn_w_down, m_ffn_w_down, v_ffn_w_down, got_b, 2 * FF_SHARD, "adamw_down"))
    after_b = zero_token + sum(big[n][1][0, 0, 0] for n in ("ffn_w_gate", "ffn_w_up", "ffn_w_down"))
    got_a = [_exchange_wait(reduce_a[l], after_b, "reduce_a_wait_%d" % l, gather=False) for l in reversed(range(DEPTH))][::-1]
    big.update(
        w_in=_adamw_rows(w_in, m_w_in, v_w_in, got_a, 0, "adamw_w_in"),
        w_out=_adamw_rows(w_out, m_w_out, v_w_out, got_a, A_OUT_AT, "adamw_w_out"))

    full_shapes = [(DEPTH, D_MODEL), (DEPTH, D_MODEL), (DEPTH, HEAD_DIM), (DEPTH, SC_WIDTH), (DEPTH, HEADS),
                   (DEPTH, HEADS), (D_MODEL,), (DEPTH, 4, QKV), (DEPTH, 3, SC_WIDTH)]
    small_keys = ("g1", "g2", "gn", "gs", "al", "dt")
    packed = _pack([stack(k) for k in small_keys] + [d_final[0], stack("cw"), stack("scw")], _packed_rows(full_shapes))
    sg = _unpack(_all_reduce_small(packed), full_shapes)
    sg[7] = lax.dynamic_slice_in_dim(sg[7], chip * (QKV // N_CHIPS), QKV // N_CHIPS, axis=2)
    sg[8] = lax.dynamic_slice_in_dim(sg[8], chip * (SC_WIDTH // N_CHIPS), SC_WIDTH // N_CHIPS, axis=2)
    small_names = ("norm1_g", "norm2_g", "dn_norm_g", "sc_norm_g", "dn_a_log", "dn_dt_bias", "final_norm_g",
                   "dn_conv_w", "sc_conv_w")
    sw = (norm1_g, norm2_g, dn_norm_g, sc_norm_g, dn_a_log, dn_dt_bias, final_norm_g, dn_conv_w, sc_conv_w)
    sm = (m_norm1_g, m_norm2_g, m_dn_norm_g, m_sc_norm_g, m_dn_a_log, m_dn_dt_bias, m_final_norm_g, m_dn_conv_w, m_sc_conv_w)
    sv = (v_norm1_g, v_norm2_g, v_dn_norm_g, v_sc_norm_g, v_dn_a_log, v_dn_dt_bias, v_final_norm_g, v_dn_conv_w, v_sc_conv_w)
    shard_shapes = [t.shape for t in sw]
    rows = _packed_rows(shard_shapes)
    outs = _adamw(_pack(sw, rows), _pack(sm, rows), _pack(sv, rows, fill=1.0), [_pack(sg, rows)], "adamw_small")
    small = {name: [] for name in small_names}
    for o in outs:
        for name, t in zip(small_names, _unpack(o, shard_shapes)):
            small[name].append(t)

    order = ("norm1_g", "w_in", "dn_conv_w", "dn_a_log", "dn_dt_bias", "dn_norm_g", "sc_conv_w", "sc_norm_g", "w_out",
             "norm2_g", "ffn_w_gate", "ffn_w_up", "ffn_w_down", "final_norm_g")
    result = {**big, **small}
    return (loss, dact[None], *[result[n][0] for n in order], *[result[n][1] for n in order],
            *[result[n][2] for n in order], *[result[n][3] for n in order])
```
